```python
import jax, jax.numpy as jnp
from jax import lax
import numpy as np

D_MODEL = 1024
BATCH = 8
SEQ = 4096
DEPTH = 2

N_META = 16
CHUNK = 128
PAD_FRONT = (-N_META) % CHUNK
RET_HEADS = 4
RET_QK_DIM = D_MODEL // 8
RET_V_DIM = 2 * RET_QK_DIM
RET_QK_W = RET_HEADS * RET_QK_DIM
RET_V_W = RET_HEADS * RET_V_DIM
CONV_CH = D_MODEL
CONV_WIDTH = 31
MIX_IN_EVEN = 2 * RET_QK_W + 2 * RET_V_W + 2 * CONV_CH
MIX_OUT_EVEN = RET_V_W + CONV_CH
RET_DECAY_OFFSET = 5.0
ROPE_BASE = 10000.0
SB_HEADS = 16
SB_HEAD_DIM = D_MODEL // SB_HEADS
D_FF = 4 * D_MODEL
EPS = 1e-6
N_EVEN = (DEPTH + 1) // 2
N_ODD = DEPTH // 2

kernel_name = "hybrid_retention_conformer_stickbreaking_trunk"


def rmsnorm(x, g):
    xf = x.astype(jnp.float32)
    y = xf * lax.rsqrt(jnp.mean(xf * xf, axis=-1, keepdims=True) + EPS) * g.astype(jnp.float32)
    return y.astype(x.dtype)


def layernorm(x, g, b):
    xf = x.astype(jnp.float32)
    mu = jnp.mean(xf, axis=-1, keepdims=True)
    var = jnp.mean(jnp.square(xf - mu), axis=-1, keepdims=True)
    y = (xf - mu) * lax.rsqrt(var + EPS) * g.astype(jnp.float32) + b.astype(jnp.float32)
    return y.astype(x.dtype)


def rotary(x):
    P, d = x.shape[1], x.shape[-1]
    half = d // 2
    inv_freq = ROPE_BASE ** (-jnp.arange(half, dtype=jnp.float32) / half)
    ang = jnp.arange(P, dtype=jnp.float32)[:, None] * inv_freq[None, :]
    cos = jnp.cos(ang)[None, :, None, :]
    sin = jnp.sin(ang)[None, :, None, :]
    x1, x2 = x[..., :half], x[..., half:]
    return jnp.concatenate([x1 * cos - x2 * sin, x1 * sin + x2 * cos], axis=-1)


def retention_chunkwise(q, k, v):
    b, P, H, dk = q.shape
    dv = v.shape[-1]
    n = P // CHUNK
    log_g = jnp.log1p(-jnp.exp2(-RET_DECAY_OFFSET - jnp.arange(H, dtype=jnp.float32)))
    idx = jnp.arange(CHUNK, dtype=jnp.float32)
    diff = idx[:, None] - idx[None, :]
    inner_decay = jnp.where(diff[None] >= 0, jnp.exp(jnp.maximum(diff, 0.0)[None] * log_g[:, None, None]), 0.0)
    qc = q.reshape(b, n, CHUNK, H, dk)
    kc = k.reshape(b, n, CHUNK, H, dk)
    vc = v.reshape(b, n, CHUNK, H, dv)
    scores = jnp.einsum('bnihd,bnjhd->bnhij', qc, kc) * inner_decay
    o_inner = jnp.einsum('bnhij,bnjhe->bnihe', scores, vc)
    k_dec = kc * jnp.exp((CHUNK - 1 - idx)[:, None] * log_g[None, :])[:, :, None]
    kv = jnp.einsum('bnjhd,bnjhe->nbhde', k_dec, vc)
    chunk_decay = jnp.exp(CHUNK * log_g)[None, :, None, None]

    def step(state, kv_n):
        return chunk_decay * state + kv_n, state

    _, prev = lax.scan(step, jnp.zeros((b, H, dk, dv), jnp.float32), kv)
    q_dec = qc * jnp.exp((idx + 1.0)[:, None] * log_g[None, :])[:, :, None]
    o_cross = jnp.einsum('bnihd,nbhde->bnihe', q_dec, prev)
    return (o_inner + o_cross).reshape(b, P, H, dv)


def head_groupnorm(o, g):
    mu = jnp.mean(o, axis=-1, keepdims=True)
    var = jnp.mean(jnp.square(o - mu), axis=-1, keepdims=True)
    return (o - mu) * lax.rsqrt(var + EPS) * g.astype(jnp.float32)


def conformer_conv(u, conv_w, conv_b, ln_g, ln_b):
    a, gate = jnp.split(u, 2, axis=-1)
    hdn = a * jax.nn.sigmoid(gate)
    y = lax.conv_general_dilated(
        hdn, conv_w[:, None, :].astype(hdn.dtype), window_strides=(1,),
        padding=[(CONV_WIDTH - 1, 0)], dimension_numbers=('NWC', 'WIO', 'NWC'),
        feature_group_count=CONV_CH)
    y = y + conv_b.astype(y.dtype)
    return jax.nn.silu(layernorm(y, ln_g, ln_b))


def even_mixer(h, w_in, gn_g, conv_w, conv_b, ln_g, ln_b, w_out):
    b, L, _ = h.shape
    proj = h @ w_in.astype(h.dtype)
    q, k, v, g, u = jnp.split(proj, [RET_QK_W, 2 * RET_QK_W, 2 * RET_QK_W + RET_V_W,
                                     2 * RET_QK_W + 2 * RET_V_W], axis=-1)
    pad = ((0, 0), (PAD_FRONT, 0), (0, 0), (0, 0))
    q = jnp.pad(q.astype(jnp.float32).reshape(b, L, RET_HEADS, RET_QK_DIM), pad)
    k = jnp.pad(k.astype(jnp.float32).reshape(b, L, RET_HEADS, RET_QK_DIM), pad)
    v = jnp.pad(v.astype(jnp.float32).reshape(b, L, RET_HEADS, RET_V_DIM), pad)
    q = rotary(q)
    k = rotary(k) * (RET_QK_DIM ** -0.5)
    o = retention_chunkwise(q, k, v)[:, PAD_FRONT:]
    o = head_groupnorm(o, gn_g).reshape(b, L, RET_V_W).astype(h.dtype)
    o = jax.nn.silu(g) * o
    c = conformer_conv(u, conv_w, conv_b, ln_g, ln_b)
    return jnp.concatenate([o, c], axis=-1) @ w_out.astype(h.dtype)


def stick_breaking(q, k, v, n_pad):
    b, H, P, d = q.shape
    n = P // CHUNK
    scale = d ** -0.5
    key_pos = jnp.arange(P)

    def block(i):
        qb = lax.dynamic_slice_in_dim(q, i * CHUNK, CHUNK, axis=2)
        z = jnp.einsum('bhqd,bhkd->bhqk', qb, k) * scale
        q_pos = i * CHUNK + jnp.arange(CHUNK)
        valid = (key_pos[None, :] < q_pos[:, None]) & (key_pos[None, :] >= n_pad)
        log_keep = jnp.where(valid, jax.nn.log_sigmoid(-z), 0.0)
        after = lax.cumsum(log_keep, axis=3, reverse=True) - log_keep
        w = jnp.where(valid, jnp.exp(jax.nn.log_sigmoid(z) + after), 0.0)
        return jnp.einsum('bhqk,bhkd->bhqd', w, v)

    out = lax.map(block, jnp.arange(n))
    return jnp.transpose(out, (1, 0, 3, 2, 4)).reshape(b, P, H, d)


def odd_mixer(h, w_qkv, qn_g, kn_g, w_o):
    b, L, _ = h.shape
    qkv = h @ w_qkv.astype(h.dtype)
    q, k, v = jnp.split(qkv, 3, axis=-1)
    q = rmsnorm(q.reshape(b, L, SB_HEADS, SB_HEAD_DIM), qn_g)
    k = rmsnorm(k.reshape(b, L, SB_HEADS, SB_HEAD_DIM), kn_g)
    v = v.reshape(b, L, SB_HEADS, SB_HEAD_DIM)
    pad = ((0, 0), (PAD_FRONT, 0), (0, 0), (0, 0))
    to_bhpd = lambda t: jnp.transpose(jnp.pad(t.astype(jnp.float32), pad), (0, 2, 1, 3))
    o = stick_breaking(to_bhpd(q), to_bhpd(k), to_bhpd(v), PAD_FRONT)[:, PAD_FRONT:]
    o = o.reshape(b, L, D_MODEL).astype(h.dtype)
    return o @ w_o.astype(h.dtype)


def sq_relu_mlp(h, w1, w2):
    return jnp.square(jax.nn.relu(h @ w1.astype(h.dtype))) @ w2.astype(h.dtype)


def _fwd_setup_inputs(seed: int = 0) -> dict:
    key = jax.random.key(seed)
    ks = jax.random.split(key, 17)
    nrm = lambda kk, shape, s: jax.random.normal(kk, shape, jnp.float32) * s
    return {
        "x": nrm(ks[0], (BATCH, SEQ, D_MODEL), 1.0),
        "meta": nrm(ks[1], (N_META, D_MODEL), 1.0),
        "norm_mix_g": 1.0 + nrm(ks[2], (DEPTH, D_MODEL), 0.02),
        "norm_mlp_g": 1.0 + nrm(ks[3], (DEPTH, D_MODEL), 0.02),
        "even_w_in": nrm(ks[4], (N_EVEN, D_MODEL, MIX_IN_EVEN), D_MODEL ** -0.5),
        "even_ret_gn_g": 1.0 + nrm(ks[5], (N_EVEN, RET_HEADS, RET_V_DIM), 0.02),
        "even_conv_w": nrm(ks[6], (N_EVEN, CONV_WIDTH, CONV_CH), CONV_WIDTH ** -0.5),
        "even_conv_b": nrm(ks[7], (N_EVEN, CONV_CH), 0.01),
        "even_conv_ln_g": 1.0 + nrm(ks[8], (N_EVEN, CONV_CH), 0.02),
        "even_conv_ln_b": nrm(ks[9], (N_EVEN, CONV_CH), 0.01),
        "even_w_out": nrm(ks[10], (N_EVEN, MIX_OUT_EVEN, D_MODEL), MIX_OUT_EVEN ** -0.5),
        "odd_w_qkv": nrm(ks[11], (N_ODD, D_MODEL, 3 * D_MODEL), D_MODEL ** -0.5),
        "odd_q_norm_g": 1.0 + nrm(ks[12], (N_ODD, SB_HEAD_DIM), 0.02),
        "odd_k_norm_g": 1.0 + nrm(ks[13], (N_ODD, SB_HEAD_DIM), 0.02),
        "odd_w_o": nrm(ks[14], (N_ODD, D_MODEL, D_MODEL), D_MODEL ** -0.5),
        "mlp_w1": nrm(ks[15], (DEPTH, D_MODEL, D_FF), D_MODEL ** -0.5),
        "mlp_w2": nrm(ks[16], (DEPTH, D_FF, D_MODEL), D_FF ** -0.5),
    }


def _fwd_reference(x, meta, norm_mix_g, norm_mlp_g, even_w_in, even_ret_gn_g, even_conv_w,
              even_conv_b, even_conv_ln_g, even_conv_ln_b, even_w_out, odd_w_qkv,
              odd_q_norm_g, odd_k_norm_g, odd_w_o, mlp_w1, mlp_w2):
    b = x.shape[0]
    meta_b = jnp.broadcast_to(meta[None].astype(x.dtype), (b, N_META, D_MODEL))
    h = jnp.concatenate([meta_b, x], axis=1)
    for layer in range(DEPTH):
        j = layer // 2
        hn = rmsnorm(h, norm_mix_g[layer])
        if layer % 2 == 0:
            mix = even_mixer(hn, even_w_in[j], even_ret_gn_g[j], even_conv_w[j], even_conv_b[j],
                             even_conv_ln_g[j], even_conv_ln_b[j], even_w_out[j])
        else:
            mix = odd_mixer(hn, odd_w_qkv[j], odd_q_norm_g[j], odd_k_norm_g[j], odd_w_o[j])
        h = h + mix
        h = h + sq_relu_mlp(rmsnorm(h, norm_mlp_g[layer]), mlp_w1[layer], mlp_w2[layer])
    return h[:, N_META:]


import jax as _jax
import jax.numpy as _jnp

TWIN_FORMAT = 'train_step'
FWD_PARAMS = ['x', 'meta', 'norm_mix_g', 'norm_mlp_g', 'even_w_in', 'even_ret_gn_g', 'even_conv_w', 'even_conv_b', 'even_conv_ln_g', 'even_conv_ln_b', 'even_w_out', 'odd_w_qkv', 'odd_q_norm_g', 'odd_k_norm_g', 'odd_w_o', 'mlp_w1', 'mlp_w2']
TWIN_WEIGHTS = ['meta', 'norm_mix_g', 'norm_mlp_g', 'even_w_in', 'even_ret_gn_g', 'even_conv_w', 'even_conv_b', 'even_conv_ln_g', 'even_conv_ln_b', 'even_w_out', 'odd_w_qkv', 'odd_q_norm_g', 'odd_k_norm_g', 'odd_w_o', 'mlp_w1', 'mlp_w2']
TWIN_DIFF_INPUT = 'x'
TWIN_INPUTS = ['x', 'meta', 'norm_mix_g', 'norm_mlp_g', 'even_w_in', 'even_ret_gn_g', 'even_conv_w', 'even_conv_b', 'even_conv_ln_g', 'even_conv_ln_b', 'even_w_out', 'odd_w_qkv', 'odd_q_norm_g', 'odd_k_norm_g', 'odd_w_o', 'mlp_w1', 'mlp_w2', 'loss_target', 'm_meta', 'm_norm_mix_g', 'm_norm_mlp_g', 'm_even_w_in', 'm_even_ret_gn_g', 'm_even_conv_w', 'm_even_conv_b', 'm_even_conv_ln_g', 'm_even_conv_ln_b', 'm_even_w_out', 'm_odd_w_qkv', 'm_odd_q_norm_g', 'm_odd_k_norm_g', 'm_odd_w_o', 'm_mlp_w1', 'm_mlp_w2', 'v_meta', 'v_norm_mix_g', 'v_norm_mlp_g', 'v_even_w_in', 'v_even_ret_gn_g', 'v_even_conv_w', 'v_even_conv_b', 'v_even_conv_ln_g', 'v_even_conv_ln_b', 'v_even_w_out', 'v_odd_w_qkv', 'v_odd_q_norm_g', 'v_odd_k_norm_g', 'v_odd_w_o', 'v_mlp_w1', 'v_mlp_w2']
TWIN_OUTPUTS = ['loss', 'grad_x', 'grad_meta', 'grad_norm_mix_g', 'grad_norm_mlp_g', 'grad_even_w_in', 'grad_even_ret_gn_g', 'grad_even_conv_w', 'grad_even_conv_b', 'grad_even_conv_ln_g', 'grad_even_conv_ln_b', 'grad_even_w_out', 'grad_odd_w_qkv', 'grad_odd_q_norm_g', 'grad_odd_k_norm_g', 'grad_odd_w_o', 'grad_mlp_w1', 'grad_mlp_w2', 'delta_meta', 'delta_norm_mix_g', 'delta_norm_mlp_g', 'delta_even_w_in', 'delta_even_ret_gn_g', 'delta_even_conv_w', 'delta_even_conv_b', 'delta_even_conv_ln_g', 'delta_even_conv_ln_b', 'delta_even_w_out', 'delta_odd_w_qkv', 'delta_odd_q_norm_g', 'delta_odd_k_norm_g', 'delta_odd_w_o', 'delta_mlp_w1', 'delta_mlp_w2', 'new_m_meta', 'new_m_norm_mix_g', 'new_m_norm_mlp_g', 'new_m_even_w_in', 'new_m_even_ret_gn_g', 'new_m_even_conv_w', 'new_m_even_conv_b', 'new_m_even_conv_ln_g', 'new_m_even_conv_ln_b', 'new_m_even_w_out', 'new_m_odd_w_qkv', 'new_m_odd_q_norm_g', 'new_m_odd_k_norm_g', 'new_m_odd_w_o', 'new_m_mlp_w1', 'new_m_mlp_w2', 'new_v_meta', 'new_v_norm_mix_g', 'new_v_norm_mlp_g', 'new_v_even_w_in', 'new_v_even_ret_gn_g', 'new_v_even_conv_w', 'new_v_even_conv_b', 'new_v_even_conv_ln_g', 'new_v_even_conv_ln_b', 'new_v_even_w_out', 'new_v_odd_w_qkv', 'new_v_odd_q_norm_g', 'new_v_odd_k_norm_g', 'new_v_odd_w_o', 'new_v_mlp_w1', 'new_v_mlp_w2']
TWIN_LEAF_KINDS = {'loss': 'loss', 'grad_x': 'grad_x', 'grad_meta': 'grad_w', 'grad_norm_mix_g': 'grad_w', 'grad_norm_mlp_g': 'grad_w', 'grad_even_w_in': 'grad_w', 'grad_even_ret_gn_g': 'grad_w', 'grad_even_conv_w': 'grad_w', 'grad_even_conv_b': 'grad_w', 'grad_even_conv_ln_g': 'grad_w', 'grad_even_conv_ln_b': 'grad_w', 'grad_even_w_out': 'grad_w', 'grad_odd_w_qkv': 'grad_w', 'grad_odd_q_norm_g': 'grad_w', 'grad_odd_k_norm_g': 'grad_w', 'grad_odd_w_o': 'grad_w', 'grad_mlp_w1': 'grad_w', 'grad_mlp_w2': 'grad_w', 'delta_meta': 'delta_w', 'delta_norm_mix_g': 'delta_w', 'delta_norm_mlp_g': 'delta_w', 'delta_even_w_in': 'delta_w', 'delta_even_ret_gn_g': 'delta_w', 'delta_even_conv_w': 'delta_w', 'delta_even_conv_b': 'delta_w', 'delta_even_conv_ln_g': 'delta_w', 'delta_even_conv_ln_b': 'delta_w', 'delta_even_w_out': 'delta_w', 'delta_odd_w_qkv': 'delta_w', 'delta_odd_q_norm_g': 'delta_w', 'delta_odd_k_norm_g': 'delta_w', 'delta_odd_w_o': 'delta_w', 'delta_mlp_w1': 'delta_w', 'delta_mlp_w2': 'delta_w', 'new_m_meta': 'new_m', 'new_m_norm_mix_g': 'new_m', 'new_m_norm_mlp_g': 'new_m', 'new_m_even_w_in': 'new_m', 'new_m_even_ret_gn_g': 'new_m', 'new_m_even_conv_w': 'new_m', 'new_m_even_conv_b': 'new_m', 'new_m_even_conv_ln_g': 'new_m', 'new_m_even_conv_ln_b': 'new_m', 'new_m_even_w_out': 'new_m', 'new_m_odd_w_qkv': 'new_m', 'new_m_odd_q_norm_g': 'new_m', 'new_m_odd_k_norm_g': 'new_m', 'new_m_odd_w_o': 'new_m', 'new_m_mlp_w1': 'new_m', 'new_m_mlp_w2': 'new_m', 'new_v_meta': 'new_v', 'new_v_norm_mix_g': 'new_v', 'new_v_norm_mlp_g': 'new_v', 'new_v_even_w_in': 'new_v', 'new_v_even_ret_gn_g': 'new_v', 'new_v_even_conv_w': 'new_v', 'new_v_even_conv_b': 'new_v', 'new_v_even_conv_ln_g': 'new_v', 'new_v_even_conv_ln_b': 'new_v', 'new_v_even_w_out': 'new_v', 'new_v_odd_w_qkv': 'new_v', 'new_v_odd_q_norm_g': 'new_v', 'new_v_odd_k_norm_g': 'new_v', 'new_v_odd_w_o': 'new_v', 'new_v_mlp_w1': 'new_v', 'new_v_mlp_w2': 'new_v'}


def _forward(args):
    return _fwd_reference(*[args[k] for k in FWD_PARAMS])


def _output_shape():
    out = _jax.eval_shape(lambda: _forward(_fwd_setup_inputs(0)))
    return out.shape, out.dtype

N_MICROBATCH = 1
ADAM_LR = 0.001
ADAM_B1 = 0.9
ADAM_B2 = 0.999
ADAM_EPS = 1e-08
ADAM_WD = 0.01
ADAM_STEP = 10
PER_EXAMPLE_BATCH_AXIS = {'x': 0, 'loss_target': 0}
SHARED_INPUTS = []
_WEIGHT_DTYPES = {'meta': _jnp.float32, 'norm_mix_g': _jnp.float32, 'norm_mlp_g': _jnp.float32, 'even_w_in': _jnp.float32, 'even_ret_gn_g': _jnp.float32, 'even_conv_w': _jnp.float32, 'even_conv_b': _jnp.float32, 'even_conv_ln_g': _jnp.float32, 'even_conv_ln_b': _jnp.float32, 'even_w_out': _jnp.float32, 'odd_w_qkv': _jnp.float32, 'odd_q_norm_g': _jnp.float32, 'odd_k_norm_g': _jnp.float32, 'odd_w_o': _jnp.float32, 'mlp_w1': _jnp.float32, 'mlp_w2': _jnp.float32}
MOMENT_SCALE = {'meta': 6.500965e-02, 'norm_mix_g': 1.343244e+01, 'norm_mlp_g': 9.677246e+01, 'even_w_in': 5.371582e-01, 'even_ret_gn_g': 5.170500e+00, 'even_conv_w': 1.848176e+00, 'even_conv_b': 3.609424e+01, 'even_conv_ln_g': 1.540070e+01, 'even_conv_ln_b': 2.023233e+01, 'even_w_out': 7.684012e+00, 'odd_w_qkv': 6.416081e+00, 'odd_q_norm_g': 2.823623e+01, 'odd_k_norm_g': 2.820458e+01, 'odd_w_o': 1.046454e+01, 'mlp_w1': 4.751436e+00, 'mlp_w2': 1.829539e+01}


def _to_microbatches(a, axis):
    t = _jnp.moveaxis(a, axis, 0)
    t = t.reshape((N_MICROBATCH, t.shape[0] // N_MICROBATCH) + t.shape[1:])
    return _jnp.moveaxis(t, 1, axis + 1)


def setup_inputs(seed: int = 0) -> dict:
    inp = _fwd_setup_inputs(seed)
    key = _jax.random.fold_in(_jax.random.key(seed), 7919)
    shape, _ = _output_shape()
    out = dict(inp)
    out["loss_target"] = _jax.random.normal(_jax.random.fold_in(key, 0), shape, _jnp.float32)
    for i, name in enumerate(TWIN_WEIGHTS):
        w = inp[name].astype(_jnp.float32)
        if MOMENT_SCALE is None:
            s = _jnp.sqrt(_jnp.mean(_jnp.square(w)) + 1e-30)
        else:
            s = MOMENT_SCALE[name]
        km, kv = _jax.random.split(_jax.random.fold_in(key, i + 1))
        out[name] = w
        out["m_" + name] = s * _jax.random.normal(km, w.shape, _jnp.float32)
        out["v_" + name] = (s * s) * _jax.random.uniform(kv, w.shape, _jnp.float32, 0.5, 1.5)
    if N_MICROBATCH > 1:
        for name, axis in PER_EXAMPLE_BATCH_AXIS.items():
            out[name] = _to_microbatches(out[name], axis)
    return {'x': out['x'], 'meta': out['meta'], 'norm_mix_g': out['norm_mix_g'], 'norm_mlp_g': out['norm_mlp_g'], 'even_w_in': out['even_w_in'], 'even_ret_gn_g': out['even_ret_gn_g'], 'even_conv_w': out['even_conv_w'], 'even_conv_b': out['even_conv_b'], 'even_conv_ln_g': out['even_conv_ln_g'], 'even_conv_ln_b': out['even_conv_ln_b'], 'even_w_out': out['even_w_out'], 'odd_w_qkv': out['odd_w_qkv'], 'odd_q_norm_g': out['odd_q_norm_g'], 'odd_k_norm_g': out['odd_k_norm_g'], 'odd_w_o': out['odd_w_o'], 'mlp_w1': out['mlp_w1'], 'mlp_w2': out['mlp_w2'], 'loss_target': out['loss_target'], 'm_meta': out['m_meta'], 'm_norm_mix_g': out['m_norm_mix_g'], 'm_norm_mlp_g': out['m_norm_mlp_g'], 'm_even_w_in': out['m_even_w_in'], 'm_even_ret_gn_g': out['m_even_ret_gn_g'], 'm_even_conv_w': out['m_even_conv_w'], 'm_even_conv_b': out['m_even_conv_b'], 'm_even_conv_ln_g': out['m_even_conv_ln_g'], 'm_even_conv_ln_b': out['m_even_conv_ln_b'], 'm_even_w_out': out['m_even_w_out'], 'm_odd_w_qkv': out['m_odd_w_qkv'], 'm_odd_q_norm_g': out['m_odd_q_norm_g'], 'm_odd_k_norm_g': out['m_odd_k_norm_g'], 'm_odd_w_o': out['m_odd_w_o'], 'm_mlp_w1': out['m_mlp_w1'], 'm_mlp_w2': out['m_mlp_w2'], 'v_meta': out['v_meta'], 'v_norm_mix_g': out['v_norm_mix_g'], 'v_norm_mlp_g': out['v_norm_mlp_g'], 'v_even_w_in': out['v_even_w_in'], 'v_even_ret_gn_g': out['v_even_ret_gn_g'], 'v_even_conv_w': out['v_even_conv_w'], 'v_even_conv_b': out['v_even_conv_b'], 'v_even_conv_ln_g': out['v_even_conv_ln_g'], 'v_even_conv_ln_b': out['v_even_conv_ln_b'], 'v_even_w_out': out['v_even_w_out'], 'v_odd_w_qkv': out['v_odd_w_qkv'], 'v_odd_q_norm_g': out['v_odd_q_norm_g'], 'v_odd_k_norm_g': out['v_odd_k_norm_g'], 'v_odd_w_o': out['v_odd_w_o'], 'v_mlp_w1': out['v_mlp_w1'], 'v_mlp_w2': out['v_mlp_w2']}


def _loss(weights, diff, rest, loss_target):
    with _jax.named_scope("forward"):
        args = {**rest, TWIN_DIFF_INPUT: diff, **{k: w.astype(_WEIGHT_DTYPES[k]) for k, w in weights.items()}}
        y = _forward(args)
    with _jax.named_scope("loss_head"):
        err = _jnp.square(y.astype(_jnp.float32) - loss_target)
        return 0.5 * _jnp.sum(_jnp.mean(err, axis=-1)) if err.ndim else 0.5 * err


def _adamw(w, g, m, v):
    m = ADAM_B1 * m + (1.0 - ADAM_B1) * g
    v = ADAM_B2 * v + (1.0 - ADAM_B2) * _jnp.square(g)
    m_hat = m / (1.0 - ADAM_B1 ** ADAM_STEP)
    v_hat = v / (1.0 - ADAM_B2 ** ADAM_STEP)
    delta = -ADAM_LR * (m_hat / (_jnp.sqrt(v_hat) + ADAM_EPS) + ADAM_WD * w)
    return delta, m, v


def reference(x, meta, norm_mix_g, norm_mlp_g, even_w_in, even_ret_gn_g, even_conv_w, even_conv_b, even_conv_ln_g, even_conv_ln_b, even_w_out, odd_w_qkv, odd_q_norm_g, odd_k_norm_g, odd_w_o, mlp_w1, mlp_w2, loss_target, m_meta, m_norm_mix_g, m_norm_mlp_g, m_even_w_in, m_even_ret_gn_g, m_even_conv_w, m_even_conv_b, m_even_conv_ln_g, m_even_conv_ln_b, m_even_w_out, m_odd_w_qkv, m_odd_q_norm_g, m_odd_k_norm_g, m_odd_w_o, m_mlp_w1, m_mlp_w2, v_meta, v_norm_mix_g, v_norm_mlp_g, v_even_w_in, v_even_ret_gn_g, v_even_conv_w, v_even_conv_b, v_even_conv_ln_g, v_even_conv_ln_b, v_even_w_out, v_odd_w_qkv, v_odd_q_norm_g, v_odd_k_norm_g, v_odd_w_o, v_mlp_w1, v_mlp_w2):
    given = dict(x=x, meta=meta, norm_mix_g=norm_mix_g, norm_mlp_g=norm_mlp_g, even_w_in=even_w_in, even_ret_gn_g=even_ret_gn_g, even_conv_w=even_conv_w, even_conv_b=even_conv_b, even_conv_ln_g=even_conv_ln_g, even_conv_ln_b=even_conv_ln_b, even_w_out=even_w_out, odd_w_qkv=odd_w_qkv, odd_q_norm_g=odd_q_norm_g, odd_k_norm_g=odd_k_norm_g, odd_w_o=odd_w_o, mlp_w1=mlp_w1, mlp_w2=mlp_w2, loss_target=loss_target, m_meta=m_meta, m_norm_mix_g=m_norm_mix_g, m_norm_mlp_g=m_norm_mlp_g, m_even_w_in=m_even_w_in, m_even_ret_gn_g=m_even_ret_gn_g, m_even_conv_w=m_even_conv_w, m_even_conv_b=m_even_conv_b, m_even_conv_ln_g=m_even_conv_ln_g, m_even_conv_ln_b=m_even_conv_ln_b, m_even_w_out=m_even_w_out, m_odd_w_qkv=m_odd_w_qkv, m_odd_q_norm_g=m_odd_q_norm_g, m_odd_k_norm_g=m_odd_k_norm_g, m_odd_w_o=m_odd_w_o, m_mlp_w1=m_mlp_w1, m_mlp_w2=m_mlp_w2, v_meta=v_meta, v_norm_mix_g=v_norm_mix_g, v_norm_mlp_g=v_norm_mlp_g, v_even_w_in=v_even_w_in, v_even_ret_gn_g=v_even_ret_gn_g, v_even_conv_w=v_even_conv_w, v_even_conv_b=v_even_conv_b, v_even_conv_ln_g=v_even_conv_ln_g, v_even_conv_ln_b=v_even_conv_ln_b, v_even_w_out=v_even_w_out, v_odd_w_qkv=v_odd_w_qkv, v_odd_q_norm_g=v_odd_q_norm_g, v_odd_k_norm_g=v_odd_k_norm_g, v_odd_w_o=v_odd_w_o, v_mlp_w1=v_mlp_w1, v_mlp_w2=v_mlp_w2)
    weights = {n: given[n] for n in TWIN_WEIGHTS}
    shared = {n: given[n] for n in SHARED_INPUTS}
    per_example = {n: given[n] for n in ['x']}
    grad_fn = _jax.value_and_grad(_loss, argnums=(0, 1))

    def one_microbatch(ex, loss_target):
        ex = dict(ex)
        diff = ex.pop(TWIN_DIFF_INPUT)
        return grad_fn(weights, diff, {**shared, **ex}, loss_target)

    if N_MICROBATCH == 1:
        loss, (grad_w, grad_x) = one_microbatch(per_example, given["loss_target"])
    else:
        def body(carry, xs):
            loss_sum, grad_sum = carry
            l_k, (gw_k, gx_k) = one_microbatch(xs[0], xs[1])
            with _jax.named_scope("update"):
                return (loss_sum + l_k, _jax.tree.map(_jnp.add, grad_sum, gw_k)), gx_k

        init = (_jnp.zeros((), _jnp.float32), _jax.tree.map(_jnp.zeros_like, weights))
        (loss, grad_w), grad_x = _jax.lax.scan(body, init, (per_example, given["loss_target"]))
    with _jax.named_scope("update"):
        delta_w, new_m, new_v = {}, {}, {}
        for n in TWIN_WEIGHTS:
            delta_w[n], new_m[n], new_v[n] = _adamw(weights[n], grad_w[n], given["m_" + n], given["v_" + n])
    return (loss, grad_x, *[grad_w[n] for n in TWIN_WEIGHTS], *[delta_w[n] for n in TWIN_WEIGHTS],
            *[new_m[n] for n in TWIN_WEIGHTS], *[new_v[n] for n in TWIN_WEIGHTS])
```

```python
import functools

import numpy as np
import jax
import jax.numpy as jnp
from jax import lax
from jax.experimental import pallas as pl
from jax.experimental.pallas import tpu as pltpu

F32 = jnp.float32
BF16 = jnp.bfloat16

D_MODEL = 1024
N_META = 16
CHUNK = 128
PAD_FRONT = 112
TOK0 = PAD_FRONT + N_META
EPS = 1e-6
N_DEV = 8
RET_HEADS = 4
RET_DECAY_OFFSET = 5.0
ROPE_BASE = 10000.0
CONV_WIDTH = 31
HALO = 32
SB_SCALE = 64 ** -0.5
RET_SCALE = 128 ** -0.5
ADAM_LR, ADAM_B1, ADAM_B2, ADAM_EPS, ADAM_WD, ADAM_STEP = 0.001, 0.9, 0.999, 1e-08, 0.01, 10
VMEM_LIMIT = 56 * 1024 * 1024
MESH = pl.DeviceIdType.MESH


def _pcall(body, **kw):
    return pl.pallas_call(body, **kw)


def _params(**kw):
    return pltpu.CompilerParams(vmem_limit_bytes=VMEM_LIMIT, **kw)


def _tile(n, cands):
    for c in cands:
        if n % c == 0:
            return c
    raise ValueError(f"no tile for {n} in {cands}")


def _sigmoid(x):
    return 1.0 / (1.0 + jnp.exp(-x))


_DIMS = {
    "nn": (((1,), (0,)), ((), ())),
    "nt": (((1,), (1,)), ((), ())),
    "tn": (((0,), (0,)), ((), ())),
}


def _matmul(name, a, b, *, grid, a_spec, b_spec, o_spec, out_shape, contract, acc_shape,
            epi="plain", extra=None, extra_spec=None):
    nk = grid[2]
    dims = _DIMS[contract]
    n_in = 3 if extra is not None else 2
    n_out = 2 if epi == "relu2" else 1

    def body(*refs):
        a_ref, b_ref = refs[0], refs[1]
        e_ref = refs[2] if extra is not None else None
        outs = refs[n_in:n_in + n_out]
        acc = refs[-1]
        k = pl.program_id(2)

        @pl.when(k == 0)
        def _():
            acc[...] = jnp.zeros_like(acc)

        acc[...] += lax.dot_general(a_ref[...].astype(BF16), b_ref[...].astype(BF16), dims,
                                    preferred_element_type=F32)

        @pl.when(k == nk - 1)
        def _():
            r = acc[...]
            if epi == "plain":
                outs[0][...] = r.astype(outs[0].dtype)
            elif epi == "residual":
                outs[0][...] = (r + e_ref[...]).astype(outs[0].dtype)
            elif epi == "relu2":
                outs[0][...] = r
                rr = jnp.maximum(r, 0.0)
                outs[1][...] = (rr * rr).astype(BF16)
            elif epi == "drelu2":
                outs[0][...] = (r * (2.0 * jnp.maximum(e_ref[...], 0.0))).astype(outs[0].dtype)

    in_specs = [a_spec, b_spec] + ([extra_spec] if extra is not None else [])
    args = (a, b) + ((extra,) if extra is not None else ())
    if n_out == 2:
        out_specs = [o_spec, o_spec]
    else:
        out_specs = o_spec
    return _pcall(body, name=name, grid=grid, in_specs=in_specs, out_specs=out_specs,
                  out_shape=out_shape, scratch_shapes=[pltpu.VMEM(acc_shape, F32)],
                  compiler_params=_params(dimension_semantics=("parallel", "parallel", "arbitrary")))(*args)


def _tm(t):
    return _tile(t, (1408, 768, 384, 128))


def _mm_cols(name, a, wb, lead, out_dtype=F32, epi="plain"):
    t, kdim = a.shape
    n = wb.shape[-1]
    tm, tk = _tm(t), _tile(kdim, (1024, 512))
    nl = len(lead)
    b_spec = pl.BlockSpec((None,) * (1 + nl) + (tk, n), lambda i, j, k: (j,) + lead + (k, 0))
    o_spec = pl.BlockSpec((tm, n), lambda i, j, k: (i, j))
    if epi == "relu2":
        out_shape = [jax.ShapeDtypeStruct((t, N_DEV * n), F32), jax.ShapeDtypeStruct((t, N_DEV * n), BF16)]
    else:
        out_shape = jax.ShapeDtypeStruct((t, N_DEV * n), out_dtype)
    return _matmul(name, a, wb, grid=(t // tm, N_DEV, kdim // tk),
                   a_spec=pl.BlockSpec((tm, tk), lambda i, j, k: (i, k)), b_spec=b_spec, o_spec=o_spec,
                   out_shape=out_shape, contract="nn", acc_shape=(tm, n), epi=epi)


def _mm_cols_t(name, a, wb, lead, out_dtype=F32):
    t = a.shape[0]
    kdim, n = wb.shape[-2], wb.shape[-1]
    tm, tn = _tm(t), _tile(kdim, (512,))
    nl = len(lead)
    b_spec = pl.BlockSpec((None,) * (1 + nl) + (tn, n), lambda i, j, k: (k,) + lead + (j, 0))
    return _matmul(name, a, wb, grid=(t // tm, kdim // tn, N_DEV),
                   a_spec=pl.BlockSpec((tm, n), lambda i, j, k: (i, k)), b_spec=b_spec,
                   o_spec=pl.BlockSpec((tm, tn), lambda i, j, k: (i, j)),
                   out_shape=jax.ShapeDtypeStruct((t, kdim), out_dtype), contract="nt", acc_shape=(tm, tn))


def _mm_rows(name, a, wb, lead, residual):
    t = a.shape[0]
    r, n = wb.shape[-2], wb.shape[-1]
    tm, tn = _tm(t), _tile(n, (512,))
    nl = len(lead)
    b_spec = pl.BlockSpec((None,) * (1 + nl) + (r, tn), lambda i, j, k: (k,) + lead + (0, j))
    o_spec = pl.BlockSpec((tm, tn), lambda i, j, k: (i, j))
    return _matmul(name, a, wb, grid=(t // tm, n // tn, N_DEV),
                   a_spec=pl.BlockSpec((tm, r), lambda i, j, k: (i, k)), b_spec=b_spec, o_spec=o_spec,
                   out_shape=jax.ShapeDtypeStruct((t, n), F32), contract="nn", acc_shape=(tm, tn),
                   epi="residual", extra=residual, extra_spec=o_spec)


def _mm_rows_t(name, a, wb, lead, out_dtype=F32, epi="plain", extra=None):
    t, n = a.shape
    r = wb.shape[-2]
    tm, tk = _tm(t), _tile(n, (1024,))
    nl = len(lead)
    b_spec = pl.BlockSpec((None,) * (1 + nl) + (r, tk), lambda i, j, k: (j,) + lead + (0, k))
    o_spec = pl.BlockSpec((tm, r), lambda i, j, k: (i, j))
    return _matmul(name, a, wb, grid=(t // tm, N_DEV, n // tk),
                   a_spec=pl.BlockSpec((tm, tk), lambda i, j, k: (i, k)), b_spec=b_spec, o_spec=o_spec,
                   out_shape=jax.ShapeDtypeStruct((t, N_DEV * r), out_dtype), contract="nt",
                   acc_shape=(tm, r), epi=epi, extra=extra, extra_spec=o_spec if extra is not None else None)


def _wgrad_cols(name, x, dy, n):
    t, kdim = x.shape
    tk = _tm(t)
    return _matmul(name, x, dy, grid=(1, N_DEV, t // tk),
                   a_spec=pl.BlockSpec((tk, kdim), lambda i, j, k: (k, 0)),
                   b_spec=pl.BlockSpec((tk, n), lambda i, j, k: (k, j)),
                   o_spec=pl.BlockSpec((None, kdim, n), lambda i, j, k: (j, 0, 0)),
                   out_shape=jax.ShapeDtypeStruct((N_DEV, kdim, n), BF16), contract="tn", acc_shape=(kdim, n))


def _wgrad_rows(name, x, dy, r):
    t = x.shape[0]
    n = dy.shape[1]
    tk, tn = _tm(t), _tile(n, (512,))
    return _matmul(name, x, dy, grid=(N_DEV, n // tn, t // tk),
                   a_spec=pl.BlockSpec((tk, r), lambda i, j, k: (k, i)),
                   b_spec=pl.BlockSpec((tk, tn), lambda i, j, k: (k, j)),
                   o_spec=pl.BlockSpec((None, r, tn), lambda i, j, k: (i, 0, j)),
                   out_shape=jax.ShapeDtypeStruct((N_DEV, r, n), BF16), contract="tn", acc_shape=(r, tn))


def _rows(t):
    return _tile(t, (384, 128))


def _rms_fwd(name, h, g):
    t = h.shape[0]
    tr = _rows(t)

    def body(h_ref, g_ref, o_ref):
        x = h_ref[...]
        r = lax.rsqrt(jnp.mean(x * x, axis=-1, keepdims=True) + EPS)
        o_ref[...] = (x * r * g_ref[...]).astype(BF16)

    row = pl.BlockSpec((tr, D_MODEL), lambda i: (i, 0))
    vec = pl.BlockSpec((1, D_MODEL), lambda i: (0, 0))
    return _pcall(body, name=name, grid=(t // tr,), in_specs=[row, vec], out_specs=row,
                  out_shape=jax.ShapeDtypeStruct((t, D_MODEL), BF16))(h, g)


def _rms_bwd(name, dhn, h, g, dres):
    t = h.shape[0]
    tr = _rows(t)

    def body(d_ref, h_ref, g_ref, r_ref, o_ref, dg_ref):
        @pl.when(pl.program_id(0) == 0)
        def _():
            dg_ref[...] = jnp.zeros_like(dg_ref)

        x = h_ref[...]
        d = d_ref[...]
        r = lax.rsqrt(jnp.mean(x * x, axis=-1, keepdims=True) + EPS)
        u = d * g_ref[...]
        m = jnp.mean(u * x, axis=-1, keepdims=True)
        o_ref[...] = r_ref[...] + r * u - x * (r * r * r * m)
        dg_ref[...] += jnp.sum(d * x * r, axis=0, keepdims=True)

    row = pl.BlockSpec((tr, D_MODEL), lambda i: (i, 0))
    vec = pl.BlockSpec((1, D_MODEL), lambda i: (0, 0))
    return _pcall(body, name=name, grid=(t // tr,), in_specs=[row, row, vec, row], out_specs=[row, vec],
                  out_shape=[jax.ShapeDtypeStruct((t, D_MODEL), F32), jax.ShapeDtypeStruct((1, D_MODEL), F32)])(
                      dhn, h, g, dres)


def _loss_bwd(h, target):
    t = h.shape[0]
    nb = t // CHUNK

    def body(h_ref, t_ref, d_ref, l_ref):
        i = pl.program_id(0)

        @pl.when(i == 0)
        def _():
            d_ref[...] = jnp.zeros_like(d_ref)
            l_ref[...] = jnp.zeros_like(l_ref)

        @pl.when(i > 0)
        def _():
            diff = h_ref[...] - t_ref[...]
            d_ref[...] = diff * (1.0 / D_MODEL)
            l_ref[...] += jnp.sum(diff * diff) * (0.5 / D_MODEL)

    return _pcall(body, name="loss_bwd", grid=(nb,),
                  in_specs=[pl.BlockSpec((CHUNK, D_MODEL), lambda i: (i, 0)),
                            pl.BlockSpec((CHUNK, D_MODEL), lambda i: (jnp.maximum(i - 1, 0), 0))],
                  out_specs=[pl.BlockSpec((CHUNK, D_MODEL), lambda i: (i, 0)),
                             pl.BlockSpec((8, 128), lambda i: (0, 0))],
                  out_shape=[jax.ShapeDtypeStruct((t, D_MODEL), F32), jax.ShapeDtypeStruct((8, 128), F32)])(h, target)


def _ret_tables(t):
    hh = np.arange(RET_HEADS, dtype=np.float64)
    log_g = np.log1p(-np.exp2(-RET_DECAY_OFFSET - hh))
    idx = np.arange(CHUNK, dtype=np.float64)
    diff = idx[:, None] - idx[None, :]
    dmat = np.where(diff[None] >= 0, np.exp(np.maximum(diff, 0.0)[None] * log_g[:, None, None]), 0.0)
    qdec = np.exp((idx + 1.0)[None, :, None] * log_g[:, None, None]) * np.ones((1, 1, CHUNK))
    kdec = np.exp((CHUNK - 1 - idx)[None, :, None] * log_g[:, None, None]) * np.ones((1, 1, CHUNK))
    half = CHUNK // 2
    inv_freq = (ROPE_BASE ** (-np.arange(half, dtype=np.float32) / half)).astype(np.float32)
    ang = (np.arange(t, dtype=np.float32)[:, None] * inv_freq[None, :]).astype(np.float32).astype(np.float64)
    cos2 = np.concatenate([np.cos(ang), np.cos(ang)], axis=1)
    sin2 = np.concatenate([-np.sin(ang), np.sin(ang)], axis=1)
    return tuple(jnp.asarray(v, F32) for v in (dmat, qdec, kdec, cos2, sin2))


def _rot(x, c, s):
    return x * c + pltpu.roll(x, CHUNK // 2, 1) * s


def _unrot(dx, c, s):
    return dx * c + pltpu.roll(dx * s, CHUNK // 2, 1)


def _dot(a, b, contract="nn"):
    return lax.dot_general(a, b, _DIMS[contract], preferred_element_type=F32)


def _ret_fwd(proj, tables):
    t = proj.shape[0]
    nch = t // CHUNK
    dmat, qdec, kdec, cos2, sin2 = tables

    def body(q_ref, k_ref, v_ref, c_ref, s_ref, dm_ref, qd_ref, kd_ref, o_ref, st_ref, state):
        @pl.when(pl.program_id(1) == 0)
        def _():
            state[...] = jnp.zeros_like(state)

        c, s = c_ref[...], s_ref[...]
        q = _rot(q_ref[...], c, s)
        k = _rot(k_ref[...], c, s) * RET_SCALE
        vb = v_ref[...].astype(BF16)
        st = state[...]
        st_ref[...] = st
        sc = _dot(q.astype(BF16), k.astype(BF16), "nt") * dm_ref[...]
        o = _dot(sc.astype(BF16), vb)
        o += _dot((q * qd_ref[...]).astype(BF16), st.astype(BF16))
        o_ref[...] = o
        kv = _dot((k * kd_ref[...]).astype(BF16), vb, "tn")
        state[...] = qd_ref[CHUNK - 1:CHUNK, 0:1] * st + kv

    hd = lambda h, n: (h, 0, 0)
    tab = pl.BlockSpec((None, CHUNK, CHUNK), hd)
    pos = pl.BlockSpec((CHUNK, CHUNK), lambda h, n: (n, 0))
    return _pcall(
        body, name="ret_fwd", grid=(RET_HEADS, nch),
        in_specs=[pl.BlockSpec((CHUNK, 128), lambda h, n: (n, h)),
                  pl.BlockSpec((CHUNK, 128), lambda h, n: (n, RET_HEADS + h)),
                  pl.BlockSpec((CHUNK, 256), lambda h, n: (n, RET_HEADS + h)),
                  pos, pos, tab, tab, tab],
        out_specs=[pl.BlockSpec((CHUNK, 256), lambda h, n: (n, h)),
                   pl.BlockSpec((None, None, 128, 256), lambda h, n: (h, n, 0, 0))],
        out_shape=[jax.ShapeDtypeStruct((t, 1024), F32), jax.ShapeDtypeStruct((RET_HEADS, nch, 128, 256), F32)],
        scratch_shapes=[pltpu.VMEM((128, 256), F32)],
        compiler_params=_params(dimension_semantics=("parallel", "arbitrary")))(
            proj, proj, proj, cos2, sin2, dmat, qdec, kdec)


def _ret_bwd(proj, states, do, tables):
    t = proj.shape[0]
    nch = t // CHUNK
    dmat, qdec, kdec, cos2, sin2 = tables

    def body(q_ref, k_ref, v_ref, do_ref, st_ref, c_ref, s_ref, dm_ref, qd_ref, kd_ref,
             dq_ref, dk_ref, dv_ref, rst):
        @pl.when(pl.program_id(1) == 0)
        def _():
            rst[...] = jnp.zeros_like(rst)

        c, s = c_ref[...], s_ref[...]
        q = _rot(q_ref[...], c, s)
        k = _rot(k_ref[...], c, s) * RET_SCALE
        qb, kb = q.astype(BF16), k.astype(BF16)
        vb = v_ref[...].astype(BF16)
        dob = do_ref[...].astype(BF16)
        pb = st_ref[...].astype(BF16)
        r = rst[...]
        rb = r.astype(BF16)
        dm, qd, kd = dm_ref[...], qd_ref[...], kd_ref[...]
        sb = (_dot(qb, kb, "nt") * dm).astype(BF16)
        dsb = (_dot(dob, vb, "nt") * dm).astype(BF16)
        dq = _dot(dsb, kb) + _dot(dob, pb, "nt") * qd
        dk = _dot(dsb, qb, "tn") + _dot(vb, rb, "nt") * kd
        dv = _dot(sb, dob, "tn") + _dot((k * kd).astype(BF16), rb)
        rst[...] = _dot((q * qd).astype(BF16), dob, "tn") + qd[CHUNK - 1:CHUNK, 0:1] * r
        dq_ref[...] = _unrot(dq, c, s).astype(BF16)
        dk_ref[...] = (_unrot(dk, c, s) * RET_SCALE).astype(BF16)
        dv_ref[...] = dv.astype(BF16)

    rev = lambda n: nch - 1 - n
    tab = pl.BlockSpec((None, CHUNK, CHUNK), lambda h, n: (h, 0, 0))
    pos = pl.BlockSpec((CHUNK, CHUNK), lambda h, n: (rev(n), 0))
    return _pcall(
        body, name="ret_bwd", grid=(RET_HEADS, nch),
        in_specs=[pl.BlockSpec((CHUNK, 128), lambda h, n: (rev(n), h)),
                  pl.BlockSpec((CHUNK, 128), lambda h, n: (rev(n), RET_HEADS + h)),
                  pl.BlockSpec((CHUNK, 256), lambda h, n: (rev(n), RET_HEADS + h)),
                  pl.BlockSpec((CHUNK, 256), lambda h, n: (rev(n), h)),
                  pl.BlockSpec((None, None, 128, 256), lambda h, n: (h, rev(n), 0, 0)),
                  pos, pos, tab, tab, tab],
        out_specs=[pl.BlockSpec((CHUNK, 128), lambda h, n: (rev(n), h)),
                   pl.BlockSpec((CHUNK, 128), lambda h, n: (rev(n), h)),
                   pl.BlockSpec((CHUNK, 256), lambda h, n: (rev(n), h))],
        out_shape=[jax.ShapeDtypeStruct((t, 512), BF16), jax.ShapeDtypeStruct((t, 512), BF16),
                   jax.ShapeDtypeStruct((t, 1024), BF16)],
        scratch_shapes=[pltpu.VMEM((128, 256), F32)],
        compiler_params=_params(dimension_semantics=("parallel", "arbitrary")))(
            proj, proj, proj, do, states, cos2, sin2, dmat, qdec, kdec)


def _gn_gate_fwd(o, proj, gn_g):
    t = o.shape[0]
    tr = _rows(t)

    def body(o_ref, g_ref, w_ref, c_ref):
        for h in range(RET_HEADS):
            sl = slice(256 * h, 256 * (h + 1))
            x = o_ref[:, sl]
            mu = jnp.mean(x, axis=-1, keepdims=True)
            xc = x - mu
            rstd = lax.rsqrt(jnp.mean(xc * xc, axis=-1, keepdims=True) + EPS)
            g = g_ref[:, sl]
            c_ref[:, sl] = (g * _sigmoid(g) * (xc * rstd * w_ref[:, sl])).astype(BF16)

    return _pcall(body, name="gn_gate_fwd", grid=(t // tr,),
                  in_specs=[pl.BlockSpec((tr, 1024), lambda i: (i, 0)),
                            pl.BlockSpec((tr, 1024), lambda i: (i, 2)),
                            pl.BlockSpec((1, 1024), lambda i: (0, 0))],
                  out_specs=pl.BlockSpec((tr, 1024), lambda i: (i, 0)),
                  out_shape=jax.ShapeDtypeStruct((t, 2048), BF16))(o, proj, gn_g)


def _gn_gate_bwd(dcat, o, proj, gn_g):
    t = o.shape[0]
    tr = _rows(t)

    def body(d_ref, o_ref, g_ref, w_ref, do_ref, dg_ref, dw_ref):
        @pl.when(pl.program_id(0) == 0)
        def _():
            dw_ref[...] = jnp.zeros_like(dw_ref)

        for h in range(RET_HEADS):
            sl = slice(256 * h, 256 * (h + 1))
            x = o_ref[:, sl]
            mu = jnp.mean(x, axis=-1, keepdims=True)
            xc = x - mu
            rstd = lax.rsqrt(jnp.mean(xc * xc, axis=-1, keepdims=True) + EPS)
            xh = xc * rstd
            w = w_ref[:, sl]
            g = g_ref[:, sl]
            sg = _sigmoid(g)
            d = d_ref[:, sl]
            don = d * (g * sg)
            dg_ref[:, sl] = (d * (xh * w) * (sg * (1.0 + g * (1.0 - sg)))).astype(BF16)
            dw_ref[:, sl] += jnp.sum(don * xh, axis=0, keepdims=True)
            dxh = don * w
            m1 = jnp.mean(dxh, axis=-1, keepdims=True)
            m2 = jnp.mean(dxh * xh, axis=-1, keepdims=True)
            do_ref[:, sl] = rstd * (dxh - m1 - xh * m2)

    row = pl.BlockSpec((tr, 1024), lambda i: (i, 0))
    vec = pl.BlockSpec((1, 1024), lambda i: (0, 0))
    return _pcall(body, name="gn_gate_bwd", grid=(t // tr,),
                  in_specs=[row, row, pl.BlockSpec((tr, 1024), lambda i: (i, 2)), vec],
                  out_specs=[row, row, vec],
                  out_shape=[jax.ShapeDtypeStruct((t, 1024), F32), jax.ShapeDtypeStruct((t, 1024), BF16),
                             jax.ShapeDtypeStruct((1, 1024), F32)])(dcat, o, proj, gn_g)


def _row_ids(i, tr):
    return i * tr + lax.broadcasted_iota(jnp.int32, (tr, 1), 0)


def _conv_fwd(cat, proj, conv_w, conv_b, ln_g, ln_b):
    t = proj.shape[0]
    tr = _rows(t)
    hb = tr // HALO

    def body(cat_in, ua_ref, ug_ref, pa_ref, pg_ref, w_ref, b_ref, lg_ref, lb_ref, c_ref, hd_ref, y_ref, xs):
        del cat_in
        i = pl.program_id(0)
        hdn = ua_ref[...] * _sigmoid(ug_ref[...])
        hd_ref[...] = hdn
        prev = pa_ref[...] * _sigmoid(pg_ref[...])
        xs[0:HALO, :] = jnp.where(i > 0, prev, 0.0)
        xs[HALO:HALO + tr, :] = hdn
        acc = jnp.zeros((tr, 1024), F32) + b_ref[...]
        for w in range(CONV_WIDTH):
            acc += w_ref[w:w + 1, :] * xs[pl.ds(HALO - (CONV_WIDTH - 1) + w, tr), :]
        y_ref[...] = acc
        mu = jnp.mean(acc, axis=-1, keepdims=True)
        yc = acc - mu
        rstd = lax.rsqrt(jnp.mean(yc * yc, axis=-1, keepdims=True) + EPS)
        yn = yc * rstd * lg_ref[...] + lb_ref[...]
        c = yn * _sigmoid(yn)
        c_ref[...] = jnp.where(_row_ids(i, tr) >= PAD_FRONT, c, 0.0).astype(BF16)

    row = pl.BlockSpec((tr, 1024), lambda i: (i, 0))
    vec = pl.BlockSpec((1, 1024), lambda i: (0, 0))
    halo = lambda col: pl.BlockSpec((HALO, 1024), lambda i: (jnp.maximum(i * hb - 1, 0), col))
    return _pcall(body, name="conv_fwd", grid=(t // tr,),
                  in_specs=[pl.BlockSpec(memory_space=pl.ANY),
                            pl.BlockSpec((tr, 1024), lambda i: (i, 3)), pl.BlockSpec((tr, 1024), lambda i: (i, 4)),
                            halo(3), halo(4), pl.BlockSpec((32, 1024), lambda i: (0, 0)), vec, vec, vec],
                  out_specs=[pl.BlockSpec((tr, 1024), lambda i: (i, 1)), row, row],
                  out_shape=[jax.ShapeDtypeStruct((t, 2048), BF16), jax.ShapeDtypeStruct((t, 1024), F32),
                             jax.ShapeDtypeStruct((t, 1024), F32)],
                  scratch_shapes=[pltpu.VMEM((tr + HALO, 1024), F32)],
                  input_output_aliases={0: 0})(cat, proj, proj, proj, proj, conv_w, conv_b, ln_g, ln_b)


def _conv_bwd_ln(dcat, y, ln_g, ln_b):
    t = y.shape[0]
    tr = _rows(t)

    def body(d_ref, y_ref, lg_ref, lb_ref, dy_ref, dlg_ref, dlb_ref, dcb_ref):
        i = pl.program_id(0)

        @pl.when(i == 0)
        def _():
            dlg_ref[...] = jnp.zeros_like(dlg_ref)
            dlb_ref[...] = jnp.zeros_like(dlb_ref)
            dcb_ref[...] = jnp.zeros_like(dcb_ref)

        y = y_ref[...]
        mu = jnp.mean(y, axis=-1, keepdims=True)
        yc = y - mu
        rstd = lax.rsqrt(jnp.mean(yc * yc, axis=-1, keepdims=True) + EPS)
        xh = yc * rstd
        lg = lg_ref[...]
        yn = xh * lg + lb_ref[...]
        sg = _sigmoid(yn)
        dyn = jnp.where(_row_ids(i, tr) >= PAD_FRONT, d_ref[...] * (sg * (1.0 + yn * (1.0 - sg))), 0.0)
        dlg_ref[...] += jnp.sum(dyn * xh, axis=0, keepdims=True)
        dlb_ref[...] += jnp.sum(dyn, axis=0, keepdims=True)
        dxh = dyn * lg
        m1 = jnp.mean(dxh, axis=-1, keepdims=True)
        m2 = jnp.mean(dxh * xh, axis=-1, keepdims=True)
        dy = rstd * (dxh - m1 - xh * m2)
        dy_ref[...] = dy
        dcb_ref[...] += jnp.sum(dy, axis=0, keepdims=True)

    row = pl.BlockSpec((tr, 1024), lambda i: (i, 0))
    vec = pl.BlockSpec((1, 1024), lambda i: (0, 0))
    vshape = jax.ShapeDtypeStruct((1, 1024), F32)
    return _pcall(body, name="conv_bwd_ln", grid=(t // tr,),
                  in_specs=[pl.BlockSpec((tr, 1024), lambda i: (i, 1)), row, vec, vec],
                  out_specs=[row, vec, vec, vec],
                  out_shape=[jax.ShapeDtypeStruct((t, 1024), F32), vshape, vshape, vshape])(dcat, y, ln_g, ln_b)


def _conv_bwd_taps(dy, hdn, proj, conv_w):
    t = dy.shape[0]
    tr = _rows(t)
    hb = tr // HALO
    nt = t // tr

    def body(dy_ref, nx_ref, hd_ref, ph_ref, ua_ref, ug_ref, w_ref, da_ref, dg_ref, dw_ref, ys, xs):
        i = pl.program_id(0)

        @pl.when(i == 0)
        def _():
            dw_ref[...] = jnp.zeros_like(dw_ref)

        dy = dy_ref[...]
        ys[0:tr, :] = dy
        ys[tr:tr + HALO, :] = jnp.where(i < nt - 1, nx_ref[...], 0.0)
        xs[0:HALO, :] = jnp.where(i > 0, ph_ref[...], 0.0)
        xs[HALO:HALO + tr, :] = hd_ref[...]
        dh = jnp.zeros((tr, 1024), F32)
        for w in range(CONV_WIDTH):
            dh += w_ref[w:w + 1, :] * ys[pl.ds(CONV_WIDTH - 1 - w, tr), :]
            dw_ref[w:w + 1, :] += jnp.sum(dy * xs[pl.ds(HALO - (CONV_WIDTH - 1) + w, tr), :], axis=0, keepdims=True)
        dh = jnp.where(_row_ids(i, tr) >= PAD_FRONT, dh, 0.0)
        sg = _sigmoid(ug_ref[...])
        da_ref[...] = (dh * sg).astype(BF16)
        dg_ref[...] = (dh * ua_ref[...] * sg * (1.0 - sg)).astype(BF16)

    row = pl.BlockSpec((tr, 1024), lambda i: (i, 0))
    return _pcall(body, name="conv_bwd_taps", grid=(nt,),
                  in_specs=[row, pl.BlockSpec((HALO, 1024), lambda i: (jnp.minimum((i + 1) * hb, nt * hb - 1), 0)),
                            row, pl.BlockSpec((HALO, 1024), lambda i: (jnp.maximum(i * hb - 1, 0), 0)),
                            pl.BlockSpec((tr, 1024), lambda i: (i, 3)), pl.BlockSpec((tr, 1024), lambda i: (i, 4)),
                            pl.BlockSpec((32, 1024), lambda i: (0, 0))],
                  out_specs=[row, row, pl.BlockSpec((32, 1024), lambda i: (0, 0))],
                  out_shape=[jax.ShapeDtypeStruct((t, 1024), BF16), jax.ShapeDtypeStruct((t, 1024), BF16),
                             jax.ShapeDtypeStruct((32, 1024), F32)],
                  scratch_shapes=[pltpu.VMEM((tr + HALO, 1024), F32), pltpu.VMEM((tr + HALO, 1024), F32)])(
                      dy, dy, hdn, hdn, proj, proj, conv_w)


def _seg_tables():
    j = np.arange(128)
    bd = (j[:, None] // 64 == j[None, :] // 64).astype(np.float32)
    ones = np.ones((128, 128), np.float32)
    after = np.concatenate([(j[:, None] > j[None, :]).astype(np.float32), ones], axis=1)
    before = np.concatenate([(j[:, None] < j[None, :]).astype(np.float32), ones], axis=1)
    return jnp.asarray(bd, BF16), jnp.asarray(after, BF16), jnp.asarray(before, BF16), jnp.asarray(ones, BF16)


def _split_dot(x, m):
    hi = x.astype(BF16)
    lo = (x - hi.astype(F32)).astype(BF16)
    return _dot(hi, m) + _dot(lo, m)


def _qk_norm_fwd(qkv, qg, kg, bd):
    t = qkv.shape[0]
    tr = _rows(t)

    def body(q_ref, k_ref, v_ref, qg_ref, kg_ref, bd_ref, qo, ko, vo):
        bdm = bd_ref[...]
        for src, g_ref, dst in ((q_ref, qg_ref, qo), (k_ref, kg_ref, ko)):
            for cix in range(8):
                sl = slice(128 * cix, 128 * (cix + 1))
                x = src[:, sl]
                ss = _split_dot(x * x, bdm)
                dst[:, sl] = (x * lax.rsqrt(ss * (1.0 / 64) + EPS) * g_ref[:, sl]).astype(BF16)
        vo[...] = v_ref[...].astype(BF16)

    col = lambda c: pl.BlockSpec((tr, 1024), lambda i: (i, c))
    vec = pl.BlockSpec((1, 1024), lambda i: (0, 0))
    osh = jax.ShapeDtypeStruct((t, 1024), BF16)
    return _pcall(body, name="qk_norm_fwd", grid=(t // tr,),
                  in_specs=[col(0), col(1), col(2), vec, vec, pl.BlockSpec((128, 128), lambda i: (0, 0))],
                  out_specs=[col(0), col(0), col(0)], out_shape=[osh, osh, osh])(qkv, qkv, qkv, qg, kg, bd)


def _qk_norm_bwd(qkv, dq, dk, dv, qg, kg, bd):
    t = qkv.shape[0]
    tr = _rows(t)

    def body(q_ref, k_ref, dq_ref, dk_ref, dv_ref, qg_ref, kg_ref, bd_ref, o_ref, dqg_ref, dkg_ref):
        @pl.when(pl.program_id(0) == 0)
        def _():
            dqg_ref[...] = jnp.zeros_like(dqg_ref)
            dkg_ref[...] = jnp.zeros_like(dkg_ref)

        bdm = bd_ref[...]
        for part, (src, d_ref, g_ref, dg_ref) in enumerate(((q_ref, dq_ref, qg_ref, dqg_ref),
                                                           (k_ref, dk_ref, kg_ref, dkg_ref))):
            for cix in range(8):
                sl = slice(128 * cix, 128 * (cix + 1))
                x = src[:, sl]
                d = d_ref[:, sl]
                r = lax.rsqrt(_split_dot(x * x, bdm) * (1.0 / 64) + EPS)
                u = d * g_ref[:, sl]
                m = _split_dot(u * x, bdm) * (1.0 / 64)
                o_ref[:, 1024 * part + 128 * cix:1024 * part + 128 * (cix + 1)] = (r * u - x * (r * r * r * m)).astype(BF16)
                dg_ref[:, sl] += jnp.sum(d * x * r, axis=0, keepdims=True)
        o_ref[:, 2048:3072] = dv_ref[...].astype(BF16)

    col = lambda c: pl.BlockSpec((tr, 1024), lambda i: (i, c))
    vec = pl.BlockSpec((1, 1024), lambda i: (0, 0))
    vsh = jax.ShapeDtypeStruct((1, 1024), F32)
    return _pcall(body, name="qk_norm_bwd", grid=(t // tr,),
                  in_specs=[col(0), col(1), col(0), col(0), col(0), vec, vec, pl.BlockSpec((128, 128), lambda i: (0, 0))],
                  out_specs=[pl.BlockSpec((tr, 3072), lambda i: (i, 0)), vec, vec],
                  out_shape=[jax.ShapeDtypeStruct((t, 3072), BF16), vsh, vsh])(qkv, qkv, dq, dk, dv, qg, kg, bd)


def _sb_block(qm, kblk, valid, after_tab, carry):
    z = _dot(qm, kblk, "nt") * SB_SCALE
    e = jnp.exp(-jnp.abs(z))
    sp = jnp.maximum(z, 0.0) + jnp.log(1.0 + e)
    lk = jnp.where(valid, -sp, 0.0)
    cu = _split_dot(lk, after_tab)
    w = jnp.where(valid, jnp.exp(z + lk + cu[:, :128] + carry), 0.0)
    return z, e, w, cu[:, 128:]


def _sb_qb(t):
    return _tile(t, (384, 128))


def _sb_fwd(qh, kh, vh, after_tab):
    t = qh.shape[0]
    qb = _sb_qb(t)
    per = qb // CHUNK

    def body(q_ref, k_ref, v_ref, tab_ref, o_ref, c_ref, acc, carry):
        i = pl.program_id(1)
        q = q_ref[...]
        lane = lax.broadcasted_iota(jnp.int32, (1, 128), 1)
        masks = (lane < 64, lane >= 64)
        qms = tuple(jnp.where(m, q, jnp.zeros_like(q)) for m in masks)
        qpos = i * qb + lax.broadcasted_iota(jnp.int32, (qb, 1), 0)
        acc[...] = jnp.zeros_like(acc)
        carry[...] = jnp.zeros_like(carry)
        c_ref[...] = jnp.zeros_like(c_ref)
        nkb = (i + 1) * per

        def step(s, _):
            kb = nkb - 1 - s
            off = pl.multiple_of(kb * CHUNK, CHUNK)
            kblk = k_ref[pl.ds(off, CHUNK), :]
            vblk = v_ref[pl.ds(off, CHUNK), :]
            valid = (off + lane < qpos) & (off + lane >= PAD_FRONT)
            for hh in range(2):
                cin = carry[hh]
                c_ref[:, 128 * hh:128 * (hh + 1)] = jnp.where(lane == kb, cin, c_ref[:, 128 * hh:128 * (hh + 1)])
                _, _, w, tot = _sb_block(qms[hh], kblk, valid, tab_ref[...], cin)
                carry[hh] = cin + tot
                acc[...] += _dot(w.astype(BF16), jnp.where(masks[hh], vblk, jnp.zeros_like(vblk)))
            return 0

        lax.fori_loop(0, nkb, step, 0)
        o_ref[...] = acc[...]

    full = pl.BlockSpec((t, 128), lambda h, i: (0, h))
    blk = pl.BlockSpec((qb, 128), lambda h, i: (i, h))
    return _pcall(body, name="sb_fwd", grid=(8, t // qb),
                  in_specs=[blk, full, full, pl.BlockSpec((128, 256), lambda h, i: (0, 0))],
                  out_specs=[blk, pl.BlockSpec((qb, 256), lambda h, i: (i, h))],
                  out_shape=[jax.ShapeDtypeStruct((t, 1024), F32), jax.ShapeDtypeStruct((t, 2048), F32)],
                  scratch_shapes=[pltpu.VMEM((qb, 128), F32), pltpu.VMEM((2, qb, 128), F32)],
                  compiler_params=_params(dimension_semantics=("parallel", "arbitrary")))(qh, kh, vh, after_tab)


def _sb_bwd(qh, kh, vh, carries, do, after_tab, before_tab, ones_tab):
    t = qh.shape[0]
    qb = _sb_qb(t)
    per = qb // CHUNK

    def body(q_ref, k_ref, v_ref, c_ref, do_ref, tab_ref, btab_ref, ones_ref, dq_ref, dk_ref, dv_ref, acc, gcarry):
        i = pl.program_id(1)

        @pl.when(i == 0)
        def _():
            dk_ref[...] = jnp.zeros_like(dk_ref)
            dv_ref[...] = jnp.zeros_like(dv_ref)

        q = q_ref[...]
        do = do_ref[...]
        lane = lax.broadcasted_iota(jnp.int32, (1, 128), 1)
        masks = (lane < 64, lane >= 64)
        qms = tuple(jnp.where(m, q, jnp.zeros_like(q)) for m in masks)
        dos = tuple(jnp.where(m, do, 0.0).astype(BF16) for m in masks)
        qpos = i * qb + lax.broadcasted_iota(jnp.int32, (qb, 1), 0)
        acc[...] = jnp.zeros_like(acc)
        gcarry[...] = jnp.zeros_like(gcarry)

        def step(kb, _):
            off = pl.multiple_of(kb * CHUNK, CHUNK)
            kblk = k_ref[pl.ds(off, CHUNK), :]
            vblk = v_ref[pl.ds(off, CHUNK), :]
            valid = (off + lane < qpos) & (off + lane >= PAD_FRONT)
            for hh in range(2):
                cin = _split_dot(jnp.where(lane == kb, c_ref[:, 128 * hh:128 * (hh + 1)], 0.0), ones_ref[...])
                z, e, w, _ = _sb_block(qms[hh], kblk, valid, tab_ref[...], cin)
                gw = w * _dot(dos[hh], vblk, "nt")
                cu = _split_dot(gw, btab_ref[...])
                before = cu[:, :128] + gcarry[hh]
                gcarry[hh] = gcarry[hh] + cu[:, 128:]
                sig = jnp.where(z >= 0, 1.0, e) / (1.0 + e)
                dz = (jnp.where(valid, gw * (1.0 - sig) - before * sig, 0.0) * SB_SCALE).astype(BF16)
                acc[...] += _dot(dz, jnp.where(masks[hh], kblk, jnp.zeros_like(kblk)))
                dk_ref[pl.ds(off, CHUNK), :] += _dot(dz, qms[hh], "tn")
                dv_ref[pl.ds(off, CHUNK), :] += _dot(w.astype(BF16), dos[hh], "tn")
            return 0

        lax.fori_loop(0, (i + 1) * per, step, 0)
        dq_ref[...] = acc[...]

    full = pl.BlockSpec((t, 128), lambda h, i: (0, h))
    blk = pl.BlockSpec((qb, 128), lambda h, i: (i, h))
    tab = pl.BlockSpec((128, 256), lambda h, i: (0, 0))
    osh = jax.ShapeDtypeStruct((t, 1024), F32)
    return _pcall(body, name="sb_bwd", grid=(8, t // qb),
                  in_specs=[blk, full, full, pl.BlockSpec((qb, 256), lambda h, i: (i, h)), blk, tab, tab,
                            pl.BlockSpec((128, 128), lambda h, i: (0, 0))],
                  out_specs=[blk, full, full], out_shape=[osh, osh, osh],
                  scratch_shapes=[pltpu.VMEM((qb, 128), F32), pltpu.VMEM((2, qb, 128), F32)],
                  compiler_params=_params(dimension_semantics=("parallel", "arbitrary")))(
                      qh, kh, vh, carries, do, after_tab, before_tab, ones_tab)


def _adamw_math(w, g, m, v):
    m = ADAM_B1 * m + (1.0 - ADAM_B1) * g
    v = ADAM_B2 * v + (1.0 - ADAM_B2) * (g * g)
    m_hat = m / (1.0 - ADAM_B1 ** ADAM_STEP)
    v_hat = v / (1.0 - ADAM_B2 ** ADAM_STEP)
    delta = -ADAM_LR * (m_hat / (jnp.sqrt(v_hat) + ADAM_EPS) + ADAM_WD * w)
    return delta, m, v


def _adamw(name, w, parts, m, v):
    shape = w.shape
    c = shape[-1]
    w2, m2, v2 = (a.reshape(-1, c) for a in (w, m, v))
    p3 = parts.reshape(N_DEV, -1, c)
    r = w2.shape[0]
    tr = _tile(r, (256, 128, 112, 8))

    def body(w_ref, p_ref, m_ref, v_ref, g_out, d_out, m_out, v_out):
        g = p_ref[0].astype(F32)
        for s in range(1, N_DEV):
            g = g + p_ref[s].astype(F32)
        d, mn, vn = _adamw_math(w_ref[...], g, m_ref[...], v_ref[...])
        g_out[...] = g
        d_out[...] = d
        m_out[...] = mn
        v_out[...] = vn

    row = pl.BlockSpec((tr, c), lambda i: (i, 0))
    osh = jax.ShapeDtypeStruct((r, c), F32)
    outs = _pcall(body, name=name, grid=(r // tr,),
                  in_specs=[row, pl.BlockSpec((N_DEV, tr, c), lambda i: (0, i, 0)), row, row],
                  out_specs=[row, row, row, row], out_shape=[osh, osh, osh, osh])(w2, p3, m2, v2)
    return tuple(o.reshape(shape) for o in outs)


def _place():
    x, y, c = lax.axis_index("x"), lax.axis_index("y"), lax.axis_index("c")
    return x, y, c, 4 * x + 2 * y + c


def _peer(x, y, c, rel):
    return (x ^ ((rel >> 2) & 1), y ^ ((rel >> 1) & 1), c ^ (rel & 1))


def _gather_weights(shards, small):
    n = len(shards)

    def body(*refs):
        ins, small_in = refs[:n], refs[n]
        outs, small_out = refs[n + 1:2 * n + 1], refs[2 * n + 1]
        stage = refs[2 * n + 2:3 * n + 2]
        send, recv, lsem = refs[3 * n + 2], refs[3 * n + 3], refs[3 * n + 4]
        x, y, c, me = _place()
        copies = []
        for w in range(n + 1):
            if w < n:
                stage[w][...] = ins[w][...].astype(BF16)
                src, dst = stage[w], outs[w]
            else:
                src, dst = small_in, small_out
            local = pltpu.make_async_copy(src, dst.at[me], lsem.at[w])
            local.start()
            copies.append(local)
            for rel in range(1, N_DEV):
                cp = pltpu.make_async_remote_copy(src_ref=src, dst_ref=dst.at[me], send_sem=send.at[w, rel - 1],
                                                  recv_sem=recv.at[w, rel - 1], device_id=_peer(x, y, c, rel),
                                                  device_id_type=MESH)
                cp.start()
        for w in range(n + 1):
            src, dst = (stage[w], outs[w]) if w < n else (small_in, small_out)
            for rel in range(1, N_DEV):
                cp = pltpu.make_async_remote_copy(src_ref=src, dst_ref=dst.at[me ^ rel], send_sem=send.at[w, rel - 1],
                                                  recv_sem=recv.at[w, rel - 1], device_id=_peer(x, y, c, rel),
                                                  device_id_type=MESH)
                cp.wait_send()
                cp.wait_recv()
        for local in copies:
            local.wait()

    vm = pl.BlockSpec(memory_space=pltpu.VMEM)
    hbm = pl.BlockSpec(memory_space=pl.ANY)
    out_shape = [jax.ShapeDtypeStruct((N_DEV,) + s.shape, BF16) for s in shards]
    out_shape.append(jax.ShapeDtypeStruct((N_DEV,) + small.shape, F32))
    return _pcall(body, name="gather_weights", in_specs=[vm] * (n + 1), out_specs=[hbm] * n + [vm],
                  out_shape=out_shape,
                  scratch_shapes=[pltpu.VMEM(s.shape, BF16) for s in shards]
                  + [pltpu.SemaphoreType.DMA((n + 1, N_DEV - 1)), pltpu.SemaphoreType.DMA((n + 1, N_DEV - 1)),
                     pltpu.SemaphoreType.DMA((n + 1,))],
                  compiler_params=_params(has_side_effects=True))(*shards, small)


def _scatter_grads(grads, layouts):
    n = len(grads)
    n_out = 1 + max(o for o, _ in layouts)
    out_shape = [None] * n_out
    for g, (o, lead) in zip(grads, layouts):
        if lead is None:
            out_shape[o] = jax.ShapeDtypeStruct((N_DEV,) + g.shape[1:], g.dtype)
        else:
            out_shape[o] = jax.ShapeDtypeStruct((N_DEV, 2) + g.shape[1:], g.dtype)

    def body(*refs):
        ins, outs = refs[:n], refs[n:n + n_out]
        send, recv, lsem = refs[n + n_out], refs[n + n_out + 1], refs[n + n_out + 2]
        x, y, c, me = _place()

        def dst_of(i, slot):
            o, lead = layouts[i]
            return outs[o].at[slot] if lead is None else outs[o].at[slot, lead]

        locals_ = []
        for i in range(n):
            local = pltpu.make_async_copy(ins[i].at[me], dst_of(i, me), lsem.at[i])
            local.start()
            locals_.append(local)
            for rel in range(1, N_DEV):
                pltpu.make_async_remote_copy(src_ref=ins[i].at[me ^ rel], dst_ref=dst_of(i, me),
                                             send_sem=send.at[i, rel - 1], recv_sem=recv.at[i, rel - 1],
                                             device_id=_peer(x, y, c, rel), device_id_type=MESH).start()
        for i in range(n):
            for rel in range(1, N_DEV):
                cp = pltpu.make_async_remote_copy(src_ref=ins[i].at[me ^ rel], dst_ref=dst_of(i, me ^ rel),
                                                  send_sem=send.at[i, rel - 1], recv_sem=recv.at[i, rel - 1],
                                                  device_id=_peer(x, y, c, rel), device_id_type=MESH)
                cp.wait_send()
                cp.wait_recv()
        for local in locals_:
            local.wait()

    hbm = pl.BlockSpec(memory_space=pl.ANY)
    return _pcall(body, name="scatter_grads", in_specs=[hbm] * n, out_specs=[hbm] * n_out, out_shape=out_shape,
                  scratch_shapes=[pltpu.SemaphoreType.DMA((n, N_DEV - 1)), pltpu.SemaphoreType.DMA((n, N_DEV - 1)),
                                  pltpu.SemaphoreType.DMA((n,))],
                  compiler_params=_params(has_side_effects=True))(*grads)


ROW_MIX, ROW_MLP, ROW_CB, ROW_LG, ROW_LB, ROW_QN, ROW_KN = 0, 2, 4, 5, 6, 7, 8
ROW_META, ROW_CW, ROW_GN, SMALL_ROWS = 16, 32, 64, 72


def _allreduce_small(part):
    def body(p_ref, o_ref, slots, send, recv):
        x, y, c, me = _place()
        slots[me] = p_ref[...]
        for rel in range(1, N_DEV):
            pltpu.make_async_remote_copy(src_ref=p_ref, dst_ref=slots.at[me], send_sem=send.at[rel - 1],
                                         recv_sem=recv.at[rel - 1], device_id=_peer(x, y, c, rel),
                                         device_id_type=MESH).start()
        for rel in range(1, N_DEV):
            cp = pltpu.make_async_remote_copy(src_ref=p_ref, dst_ref=slots.at[me ^ rel], send_sem=send.at[rel - 1],
                                              recv_sem=recv.at[rel - 1], device_id=_peer(x, y, c, rel),
                                              device_id_type=MESH)
            cp.wait_send()
            cp.wait_recv()
        tot = slots[0]
        for s in range(1, N_DEV):
            tot = tot + slots[s]
        o_ref[...] = tot
        for row in (ROW_QN, ROW_KN):
            v = tot[row:row + 1, :]
            f = v[:, 0:128]
            for k in range(1, 8):
                f = f + v[:, 128 * k:128 * (k + 1)]
            o_ref[row:row + 1, 0:64] = f[:, 0:64] + f[:, 64:128]

    vm = pl.BlockSpec(memory_space=pltpu.VMEM)
    return _pcall(body, name="allreduce_small", in_specs=[vm], out_specs=vm,
                  out_shape=jax.ShapeDtypeStruct(part.shape, F32),
                  scratch_shapes=[pltpu.VMEM((N_DEV,) + part.shape, F32), pltpu.SemaphoreType.DMA((N_DEV - 1,)),
                                  pltpu.SemaphoreType.DMA((N_DEV - 1,))],
                  compiler_params=_params(has_side_effects=True))(part)


def _adamw_small(w, g, m, v):
    def body(w_ref, g_ref, m_ref, v_ref, d_out, m_out, v_out):
        d, mn, vn = _adamw_math(w_ref[...], g_ref[...], m_ref[...], v_ref[...])
        d_out[...] = d
        m_out[...] = mn
        v_out[...] = vn

    osh = jax.ShapeDtypeStruct(w.shape, F32)
    return _pcall(body, name="adamw_small", out_shape=[osh, osh, osh])(w, g, m, v)


def _local_step(h0, target, p):
    t = h0.shape[0]
    tables = _ret_tables(t)
    bd, after_tab, before_tab, ones_tab = _seg_tables()
    row = lambda a, i: a[i:i + 1]

    hn_a = _rms_fwd("rms_mix0", h0, row(p["norm_mix_g"], 0))
    proj = _mm_cols("proj_in", hn_a, p["w_in"], ())
    o_ret, states = _ret_fwd(proj, tables)
    gn_flat = p["gn_g"].reshape(1, 1024)
    cat = _gn_gate_fwd(o_ret, proj, gn_flat)
    cat, hdn, ycv = _conv_fwd(cat, proj, p["conv_w"], p["conv_b"], p["ln_g"], p["ln_b"])
    h1 = _mm_rows("mix_out", cat, p["w_out"], (), h0)
    hn_b = _rms_fwd("rms_mlp0", h1, row(p["norm_mlp_g"], 0))
    a0, s0 = _mm_cols("mlp0_up", hn_b, p["w1"], (0,), epi="relu2")
    h2 = _mm_rows("mlp0_down", s0, p["w2"], (0,), h1)

    hn_c = _rms_fwd("rms_mix1", h2, row(p["norm_mix_g"], 1))
    qkv = _mm_cols("qkv", hn_c, p["w_qkv"], ())
    qg = jnp.tile(p["qn_g"], (1, 16))
    kg = jnp.tile(p["kn_g"], (1, 16))
    qh, kh, vh = _qk_norm_fwd(qkv, qg, kg, bd)
    o_sb, carries = _sb_fwd(qh, kh, vh, after_tab)
    h3 = _mm_rows("attn_out", o_sb, p["w_o"], (), h2)
    hn_d = _rms_fwd("rms_mlp1", h3, row(p["norm_mlp_g"], 1))
    a1, s1 = _mm_cols("mlp1_up", hn_d, p["w1"], (1,), epi="relu2")
    h4 = _mm_rows("mlp1_down", s1, p["w2"], (1,), h3)

    dh, loss = _loss_bwd(h4, target)

    def mlp_bwd(tag, layer, dh, h_in, hn, a, s):
        da = _mm_rows_t(f"{tag}_dact", dh, p["w2"], (layer,), out_dtype=BF16, epi="drelu2", extra=a)
        dw2 = _wgrad_rows(f"{tag}_dw2", s, dh, 512)
        dw1 = _wgrad_cols(f"{tag}_dw1", hn, da, 512)
        dhn = _mm_cols_t(f"{tag}_dhn", da, p["w1"], (layer,))
        dh, dg = _rms_bwd(f"{tag}_rms_bwd", dhn, h_in, row(p["norm_mlp_g"], layer), dh)
        return dh, dg, dw1, dw2

    dh, dg_mlp1, dw1_1, dw2_1 = mlp_bwd("mlp1", 1, dh, h3, hn_d, a1, s1)

    do_sb = _mm_rows_t("attn_dout", dh, p["w_o"], ())
    dw_o = _wgrad_rows("attn_dwo", o_sb, dh, 128)
    dq, dk, dv = _sb_bwd(qh, kh, vh, carries, do_sb, after_tab, before_tab, ones_tab)
    dqkv, dqg, dkg = _qk_norm_bwd(qkv, dq, dk, dv, qg, kg, bd)
    dw_qkv = _wgrad_cols("qkv_dw", hn_c, dqkv, 384)
    dhn = _mm_cols_t("qkv_dhn", dqkv, p["w_qkv"], ())
    dh, dg_mix1 = _rms_bwd("mix1_rms_bwd", dhn, h2, row(p["norm_mix_g"], 1), dh)

    dh, dg_mlp0, dw1_0, dw2_0 = mlp_bwd("mlp0", 0, dh, h1, hn_b, a0, s0)

    dcat = _mm_rows_t("mix_dcat", dh, p["w_out"], ())
    dw_out = _wgrad_rows("mix_dwout", cat, dh, 256)
    do_ret, dgate, dgn = _gn_gate_bwd(dcat, o_ret, proj, gn_flat)
    dq_r, dk_r, dv_r = _ret_bwd(proj, states, do_ret, tables)
    dy, dlg, dlb, dcb = _conv_bwd_ln(dcat, ycv, p["ln_g"], p["ln_b"])
    dua, dug, dcw = _conv_bwd_taps(dy, hdn, proj, p["conv_w"])
    dproj = jnp.concatenate([dq_r, dk_r, dv_r, dgate, dua, dug], axis=1)
    dw_in = _wgrad_cols("proj_dw", hn_a, dproj, 640)
    dhn = _mm_cols_t("proj_dhn", dproj, p["w_in"], ())
    dh, dg_mix0 = _rms_bwd("mix0_rms_bwd", dhn, h0, row(p["norm_mix_g"], 0), dh)

    rid = lax.broadcasted_iota(jnp.int32, (16, 1), 0)
    vecs = sum(jnp.where(rid == k, v, 0.0)
               for k, v in enumerate((dg_mix0, dg_mix1, dg_mlp0, dg_mlp1, dcb, dlg, dlb, dqg, dkg)))
    small = jnp.concatenate([vecs, dh[PAD_FRONT:TOK0], dcw, jnp.where(rid[:8] == 0, dgn, 0.0)], axis=0)
    big = dict(w_in=dw_in, w_out=dw_out, w_qkv=dw_qkv, w_o=dw_o, w1=(dw1_0, dw1_1), w2=(dw2_0, dw2_1))
    return loss[0, 0], dh[TOK0:], small, big


_SMALL_NAMES = ("meta", "norm_mix_g", "norm_mlp_g", "even_ret_gn_g", "even_conv_w", "even_conv_b",
                "even_conv_ln_g", "even_conv_ln_b", "odd_q_norm_g", "odd_k_norm_g")
_BIG_NAMES = ("even_w_in", "even_w_out", "odd_w_qkv", "odd_w_o", "mlp_w1", "mlp_w2")
_ORDER = ("meta", "norm_mix_g", "norm_mlp_g", "even_w_in", "even_ret_gn_g", "even_conv_w", "even_conv_b",
          "even_conv_ln_g", "even_conv_ln_b", "even_w_out", "odd_w_qkv", "odd_q_norm_g", "odd_k_norm_g",
          "odd_w_o", "mlp_w1", "mlp_w2")


def _pack128(a):
    flat = a.reshape(-1)
    n = flat.shape[0]
    rows = -(-n // 128)
    rows8 = -(-rows // 8) * 8
    return jnp.pad(flat, (0, rows8 * 128 - n)).reshape(rows8, 128)


def kernel(x, meta, norm_mix_g, norm_mlp_g, even_w_in, even_ret_gn_g, even_conv_w, even_conv_b, even_conv_ln_g, even_conv_ln_b, even_w_out, odd_w_qkv, odd_q_norm_g, odd_k_norm_g, odd_w_o, mlp_w1, mlp_w2, loss_target, m_meta, m_norm_mix_g, m_norm_mlp_g, m_even_w_in, m_even_ret_gn_g, m_even_conv_w, m_even_conv_b, m_even_conv_ln_g, m_even_conv_ln_b, m_even_w_out, m_odd_w_qkv, m_odd_q_norm_g, m_odd_k_norm_g, m_odd_w_o, m_mlp_w1, m_mlp_w2, v_meta, v_norm_mix_g, v_norm_mlp_g, v_even_w_in, v_even_ret_gn_g, v_even_conv_w, v_even_conv_b, v_even_conv_ln_g, v_even_conv_ln_b, v_even_w_out, v_odd_w_qkv, v_odd_q_norm_g, v_odd_k_norm_g, v_odd_w_o, v_mlp_w1, v_mlp_w2):
    w = dict(meta=meta, norm_mix_g=norm_mix_g, norm_mlp_g=norm_mlp_g, even_w_in=even_w_in,
             even_ret_gn_g=even_ret_gn_g, even_conv_w=even_conv_w, even_conv_b=even_conv_b,
             even_conv_ln_g=even_conv_ln_g, even_conv_ln_b=even_conv_ln_b, even_w_out=even_w_out,
             odd_w_qkv=odd_w_qkv, odd_q_norm_g=odd_q_norm_g, odd_k_norm_g=odd_k_norm_g, odd_w_o=odd_w_o,
             mlp_w1=mlp_w1, mlp_w2=mlp_w2)
    mom = dict(meta=m_meta, norm_mix_g=m_norm_mix_g, norm_mlp_g=m_norm_mlp_g, even_w_in=m_even_w_in,
               even_ret_gn_g=m_even_ret_gn_g, even_conv_w=m_even_conv_w, even_conv_b=m_even_conv_b,
               even_conv_ln_g=m_even_conv_ln_g, even_conv_ln_b=m_even_conv_ln_b, even_w_out=m_even_w_out,
               odd_w_qkv=m_odd_w_qkv, odd_q_norm_g=m_odd_q_norm_g, odd_k_norm_g=m_odd_k_norm_g, odd_w_o=m_odd_w_o,
               mlp_w1=m_mlp_w1, mlp_w2=m_mlp_w2)
    var = dict(meta=v_meta, norm_mix_g=v_norm_mix_g, norm_mlp_g=v_norm_mlp_g, even_w_in=v_even_w_in,
               even_ret_gn_g=v_even_ret_gn_g, even_conv_w=v_even_conv_w, even_conv_b=v_even_conv_b,
               even_conv_ln_g=v_even_conv_ln_g, even_conv_ln_b=v_even_conv_ln_b, even_w_out=v_even_w_out,
               odd_w_qkv=v_odd_w_qkv, odd_q_norm_g=v_odd_q_norm_g, odd_k_norm_g=v_odd_k_norm_g, odd_w_o=v_odd_w_o,
               mlp_w1=v_mlp_w1, mlp_w2=v_mlp_w2)
    me = 4 * lax.axis_index("x") + 2 * lax.axis_index("y") + lax.axis_index("c")

    small_in = jnp.concatenate([meta, jnp.pad(even_conv_w[0], ((0, 1), (0, 0))),
                                jnp.pad(even_ret_gn_g[0], ((0, 4), (0, 96)))], axis=0)
    shards = [even_w_in[0], even_w_out[0], odd_w_qkv[0], odd_w_o[0], mlp_w1, mlp_w2]
    g_in, g_out, g_qkv, g_o, g_w1, g_w2, g_small = _gather_weights(shards, small_in)
    cols = lambda a: jnp.transpose(a, (1, 0, 2)).reshape(a.shape[1], -1)
    p = dict(w_in=g_in, w_out=g_out, w_qkv=g_qkv, w_o=g_o, w1=g_w1, w2=g_w2,
             norm_mix_g=norm_mix_g, norm_mlp_g=norm_mlp_g, conv_b=even_conv_b, ln_g=even_conv_ln_g,
             ln_b=even_conv_ln_b, qn_g=odd_q_norm_g, kn_g=odd_k_norm_g,
             gn_g=cols(g_small[:, 48:52, :32]),
             conv_w=jnp.pad(cols(g_small[:, 16:47]), ((0, 1), (0, 0))))
    meta_full = cols(g_small[:, 0:16])

    h0 = jnp.concatenate([jnp.zeros((PAD_FRONT, D_MODEL), F32), meta_full, x[0]], axis=0)
    loss_part, grad_x, small_part, big = _local_step(h0, loss_target[0], p)
    loss = lax.psum(loss_part, ("x", "y", "c"))

    grads = [big["w_in"], big["w_out"], big["w_qkv"], big["w_o"], big["w1"][0], big["w1"][1],
             big["w2"][0], big["w2"][1]]
    layouts = [(0, None), (1, None), (2, None), (3, None), (4, 0), (4, 1), (5, 0), (5, 1)]
    r_in, r_out, r_qkv, r_o, r_w1, r_w2 = _scatter_grads(grads, layouts)
    tot = _allreduce_small(small_part)

    out = {}
    for name, parts in (("even_w_in", r_in), ("even_w_out", r_out), ("odd_w_qkv", r_qkv), ("odd_w_o", r_o),
                        ("mlp_w1", r_w1), ("mlp_w2", r_w2)):
        out[name] = _adamw("adamw_" + name, w[name], parts, mom[name], var[name])

    shard_cols = lambda a, width: lax.dynamic_slice_in_dim(a, me * width, width, axis=1)
    one = lambda r: tot[r:r + 1]
    small_g = dict(
        norm_mix_g=tot[ROW_MIX:ROW_MIX + 2], norm_mlp_g=tot[ROW_MLP:ROW_MLP + 2],
        even_conv_b=one(ROW_CB), even_conv_ln_g=one(ROW_LG), even_conv_ln_b=one(ROW_LB),
        odd_q_norm_g=one(ROW_QN)[:, :64], odd_k_norm_g=one(ROW_KN)[:, :64],
        meta=shard_cols(tot[ROW_META:ROW_META + N_META], 128),
        even_conv_w=shard_cols(tot[ROW_CW:ROW_CW + CONV_WIDTH], 128)[None],
        even_ret_gn_g=shard_cols(tot[ROW_GN].reshape(4, 256), 32)[None])
    packs = {n: (_pack128(w[n]), _pack128(small_g[n]), _pack128(mom[n]), _pack128(var[n])) for n in _SMALL_NAMES}
    cat4 = [jnp.concatenate([packs[n][i] for n in _SMALL_NAMES], axis=0) for i in range(4)]
    d_s, m_s, v_s = _adamw_small(*cat4)
    r0 = 0
    for n in _SMALL_NAMES:
        rows = packs[n][0].shape[0]
        size = w[n].size
        take = lambda a: a[r0:r0 + rows].reshape(-1)[:size].reshape(w[n].shape)
        out[n] = (small_g[n].reshape(w[n].shape), take(d_s), take(m_s), take(v_s))
        r0 += rows

    res = [loss, grad_x[None]]
    for i in range(4):
        res.extend(out[n][i] for n in _ORDER)
    return tuple(res)
```

```python
import functools

import numpy as np
import jax
import jax.numpy as jnp
from jax import lax
from jax.experimental import pallas as pl
from jax.experimental.pallas import tpu as pltpu

F32 = jnp.float32
BF16 = jnp.bfloat16

D_MODEL = 1024
N_META = 16
CHUNK = 128
PAD_FRONT = 112
TOK0 = PAD_FRONT + N_META
EPS = 1e-6
N_DEV = 8
RET_HEADS = 4
RET_DECAY_OFFSET = 5.0
ROPE_BASE = 10000.0
CONV_WIDTH = 31
HALO = 32
SB_SCALE = 64 ** -0.5
RET_SCALE = 128 ** -0.5
ADAM_LR, ADAM_B1, ADAM_B2, ADAM_EPS, ADAM_WD, ADAM_STEP = 0.001, 0.9, 0.999, 1e-08, 0.01, 10
VMEM_LIMIT = 56 * 1024 * 1024
MESH = pl.DeviceIdType.MESH


def _pcall(body, **kw):
    return pl.pallas_call(body, **kw)


def _params(**kw):
    return pltpu.CompilerParams(vmem_limit_bytes=VMEM_LIMIT, **kw)


def _tile(n, cands):
    for c in cands:
        if n % c == 0:
            return c
    raise ValueError(f"no tile for {n} in {cands}")


def _sigmoid(x):
    return 1.0 / (1.0 + jnp.exp(-x))


_DIMS = {
    "nn": (((1,), (0,)), ((), ())),
    "nt": (((1,), (1,)), ((), ())),
    "tn": (((0,), (0,)), ((), ())),
}


def _matmul(name, a, b, *, grid, a_spec, b_spec, o_spec, out_shape, contract, acc_shape,
            epi="plain", extra=None, extra_spec=None):
    nk = grid[2]
    dims = _DIMS[contract]
    n_in = 3 if extra is not None else 2
    n_out = 2 if epi == "relu2" else 1

    def body(*refs):
        a_ref, b_ref = refs[0], refs[1]
        e_ref = refs[2] if extra is not None else None
        outs = refs[n_in:n_in + n_out]
        acc = refs[-1]
        k = pl.program_id(2)

        @pl.when(k == 0)
        def _():
            acc[...] = jnp.zeros_like(acc)

        acc[...] += lax.dot_general(a_ref[...].astype(BF16), b_ref[...].astype(BF16), dims,
                                    preferred_element_type=F32)

        @pl.when(k == nk - 1)
        def _():
            r = acc[...]
            if epi == "plain":
                outs[0][...] = r.astype(outs[0].dtype)
            elif epi == "residual":
                outs[0][...] = (r + e_ref[...]).astype(outs[0].dtype)
            elif epi == "relu2":
                outs[0][...] = r
                rr = jnp.maximum(r, 0.0)
                outs[1][...] = (rr * rr).astype(BF16)
            elif epi == "drelu2":
                outs[0][...] = (r * (2.0 * jnp.maximum(e_ref[...], 0.0))).astype(outs[0].dtype)

    in_specs = [a_spec, b_spec] + ([extra_spec] if extra is not None else [])
    args = (a, b) + ((extra,) if extra is not None else ())
    if n_out == 2:
        out_specs = [o_spec, o_spec]
    else:
        out_specs = o_spec
    return _pcall(body, name=name, grid=grid, in_specs=in_specs, out_specs=out_specs,
                  out_shape=out_shape, scratch_shapes=[pltpu.VMEM(acc_shape, F32)],
                  compiler_params=_params(dimension_semantics=("parallel", "parallel", "arbitrary")))(*args)


def _tm(t):
    return _tile(t, (1408, 768, 384, 128))


def _mm_cols(name, a, wb, lead, out_dtype=F32, epi="plain"):
    t, kdim = a.shape
    n = wb.shape[-1]
    tm, tk = _tm(t), _tile(kdim, (1024, 512))
    nl = len(lead)
    b_spec = pl.BlockSpec((None,) * (1 + nl) + (tk, n), lambda i, j, k: (j,) + lead + (k, 0))
    o_spec = pl.BlockSpec((tm, n), lambda i, j, k: (i, j))
    if epi == "relu2":
        out_shape = [jax.ShapeDtypeStruct((t, N_DEV * n), F32), jax.ShapeDtypeStruct((t, N_DEV * n), BF16)]
    else:
        out_shape = jax.ShapeDtypeStruct((t, N_DEV * n), out_dtype)
    return _matmul(name, a, wb, grid=(t // tm, N_DEV, kdim // tk),
                   a_spec=pl.BlockSpec((tm, tk), lambda i, j, k: (i, k)), b_spec=b_spec, o_spec=o_spec,
                   out_shape=out_shape, contract="nn", acc_shape=(tm, n), epi=epi)


def _mm_cols_t(name, a, wb, lead, out_dtype=F32):
    t = a.shape[0]
    kdim, n = wb.shape[-2], wb.shape[-1]
    tm, tn = _tm(t), _tile(kdim, (512,))
    nl = len(lead)
    b_spec = pl.BlockSpec((None,) * (1 + nl) + (tn, n), lambda i, j, k: (k,) + lead + (j, 0))
    return _matmul(name, a, wb, grid=(t // tm, kdim // tn, N_DEV),
                   a_spec=pl.BlockSpec((tm, n), lambda i, j, k: (i, k)), b_spec=b_spec,
                   o_spec=pl.BlockSpec((tm, tn), lambda i, j, k: (i, j)),
                   out_shape=jax.ShapeDtypeStruct((t, kdim), out_dtype), contract="nt", acc_shape=(tm, tn))


def _mm_rows(name, a, wb, lead, residual):
    t = a.shape[0]
    r, n = wb.shape[-2], wb.shape[-1]
    tm, tn = _tm(t), _tile(n, (512,))
    nl = len(lead)
    b_spec = pl.BlockSpec((None,) * (1 + nl) + (r, tn), lambda i, j, k: (k,) + lead + (0, j))
    o_spec = pl.BlockSpec((tm, tn), lambda i, j, k: (i, j))
    return _matmul(name, a, wb, grid=(t // tm, n // tn, N_DEV),
                   a_spec=pl.BlockSpec((tm, r), lambda i, j, k: (i, k)), b_spec=b_spec, o_spec=o_spec,
                   out_shape=jax.ShapeDtypeStruct((t, n), F32), contract="nn", acc_shape=(tm, tn),
                   epi="residual", extra=residual, extra_spec=o_spec)


def _mm_rows_t(name, a, wb, lead, out_dtype=F32, epi="plain", extra=None):
    t, n = a.shape
    r = wb.shape[-2]
    tm, tk = _tm(t), _tile(n, (1024,))
    nl = len(lead)
    b_spec = pl.BlockSpec((None,) * (1 + nl) + (r, tk), lambda i, j, k: (j,) + lead + (0, k))
    o_spec = pl.BlockSpec((tm, r), lambda i, j, k: (i, j))
    return _matmul(name, a, wb, grid=(t // tm, N_DEV, n // tk),
                   a_spec=pl.BlockSpec((tm, tk), lambda i, j, k: (i, k)), b_spec=b_spec, o_spec=o_spec,
                   out_shape=jax.ShapeDtypeStruct((t, N_DEV * r), out_dtype), contract="nt",
                   acc_shape=(tm, r), epi=epi, extra=extra, extra_spec=o_spec if extra is not None else None)


def _wgrad_cols(name, x, dy, n):
    t, kdim = x.shape
    tk = _tm(t)
    return _matmul(name, x, dy, grid=(1, N_DEV, t // tk),
                   a_spec=pl.BlockSpec((tk, kdim), lambda i, j, k: (k, 0)),
                   b_spec=pl.BlockSpec((tk, n), lambda i, j, k: (k, j)),
                   o_spec=pl.BlockSpec((None, kdim, n), lambda i, j, k: (j, 0, 0)),
                   out_shape=jax.ShapeDtypeStruct((N_DEV, kdim, n), BF16), contract="tn", acc_shape=(kdim, n))


def _wgrad_rows(name, x, dy, r):
    t = x.shape[0]
    n = dy.shape[1]
    tk, tn = _tm(t), _tile(n, (512,))
    return _matmul(name, x, dy, grid=(N_DEV, n // tn, t // tk),
                   a_spec=pl.BlockSpec((tk, r), lambda i, j, k: (k, i)),
                   b_spec=pl.BlockSpec((tk, tn), lambda i, j, k: (k, j)),
                   o_spec=pl.BlockSpec((None, r, tn), lambda i, j, k: (i, 0, j)),
                   out_shape=jax.ShapeDtypeStruct((N_DEV, r, n), BF16), contract="tn", acc_shape=(r, tn))


def _rows(t):
    return _tile(t, (384, 128))


def _rms_fwd(name, h, g):
    t = h.shape[0]
    tr = _rows(t)

    def body(h_ref, g_ref, o_ref):
        x = h_ref[...]
        r = lax.rsqrt(jnp.mean(x * x, axis=-1, keepdims=True) + EPS)
        o_ref[...] = (x * r * g_ref[...]).astype(BF16)

    row = pl.BlockSpec((tr, D_MODEL), lambda i: (i, 0))
    vec = pl.BlockSpec((1, D_MODEL), lambda i: (0, 0))
    return _pcall(body, name=name, grid=(t // tr,), in_specs=[row, vec], out_specs=row,
                  out_shape=jax.ShapeDtypeStruct((t, D_MODEL), BF16))(h, g)


def _rms_bwd(name, dhn, h, g, dres):
    t = h.shape[0]
    tr = _rows(t)

    def body(d_ref, h_ref, g_ref, r_ref, o_ref, dg_ref):
        @pl.when(pl.program_id(0) == 0)
        def _():
            dg_ref[...] = jnp.zeros_like(dg_ref)

        x = h_ref[...]
        d = d_ref[...]
        r = lax.rsqrt(jnp.mean(x * x, axis=-1, keepdims=True) + EPS)
        u = d * g_ref[...]
        m = jnp.mean(u * x, axis=-1, keepdims=True)
        o_ref[...] = r_ref[...] + r * u - x * (r * r * r * m)
        dg_ref[...] += jnp.sum(d * x * r, axis=0, keepdims=True)

    row = pl.BlockSpec((tr, D_MODEL), lambda i: (i, 0))
    vec = pl.BlockSpec((1, D_MODEL), lambda i: (0, 0))
    return _pcall(body, name=name, grid=(t // tr,), in_specs=[row, row, vec, row], out_specs=[row, vec],
                  out_shape=[jax.ShapeDtypeStruct((t, D_MODEL), F32), jax.ShapeDtypeStruct((1, D_MODEL), F32)])(
                      dhn, h, g, dres)


def _loss_bwd(h, target):
    t = h.shape[0]
    nb = t // CHUNK

    def body(h_ref, t_ref, d_ref, l_ref):
        i = pl.program_id(0)

        @pl.when(i == 0)
        def _():
            d_ref[...] = jnp.zeros_like(d_ref)
            l_ref[...] = jnp.zeros_like(l_ref)

        @pl.when(i > 0)
        def _():
            diff = h_ref[...] - t_ref[...]
            d_ref[...] = diff * (1.0 / D_MODEL)
            l_ref[...] += jnp.sum(diff * diff) * (0.5 / D_MODEL)

    return _pcall(body, name="loss_bwd", grid=(nb,),
                  in_specs=[pl.BlockSpec((CHUNK, D_MODEL), lambda i: (i, 0)),
                            pl.BlockSpec((CHUNK, D_MODEL), lambda i: (jnp.maximum(i - 1, 0), 0))],
                  out_specs=[pl.BlockSpec((CHUNK, D_MODEL), lambda i: (i, 0)),
                             pl.BlockSpec((8, 128), lambda i: (0, 0))],
                  out_shape=[jax.ShapeDtypeStruct((t, D_MODEL), F32), jax.ShapeDtypeStruct((8, 128), F32)])(h, target)


def _ret_tables(t):
    hh = np.arange(RET_HEADS, dtype=np.float64)
    log_g = np.log1p(-np.exp2(-RET_DECAY_OFFSET - hh))
    idx = np.arange(CHUNK, dtype=np.float64)
    diff = idx[:, None] - idx[None, :]
    dmat = np.where(diff[None] >= 0, np.exp(np.maximum(diff, 0.0)[None] * log_g[:, None, None]), 0.0)
    qdec = np.exp((idx + 1.0)[None, :, None] * log_g[:, None, None]) * np.ones((1, 1, CHUNK))
    kdec = np.exp((CHUNK - 1 - idx)[None, :, None] * log_g[:, None, None]) * np.ones((1, 1, CHUNK))
    half = CHUNK // 2
    inv_freq = (ROPE_BASE ** (-np.arange(half, dtype=np.float32) / half)).astype(np.float32)
    ang = (np.arange(t, dtype=np.float32)[:, None] * inv_freq[None, :]).astype(np.float32).astype(np.float64)
    cos2 = np.concatenate([np.cos(ang), np.cos(ang)], axis=1)
    sin2 = np.concatenate([-np.sin(ang), np.sin(ang)], axis=1)
    return tuple(jnp.asarray(v, F32) for v in (dmat, qdec, kdec, cos2, sin2))


def _rot(x, c, s):
    return x * c + pltpu.roll(x, CHUNK // 2, 1) * s


def _unrot(dx, c, s):
    return dx * c + pltpu.roll(dx * s, CHUNK // 2, 1)


def _dot(a, b, contract="nn"):
    return lax.dot_general(a, b, _DIMS[contract], preferred_element_type=F32)


def _ret_fwd(proj, tables):
    t = proj.shape[0]
    nch = t // CHUNK
    dmat, qdec, kdec, cos2, sin2 = tables

    def body(q_ref, k_ref, v_ref, c_ref, s_ref, dm_ref, qd_ref, kd_ref, o_ref, st_ref, state):
        @pl.when(pl.program_id(1) == 0)
        def _():
            state[...] = jnp.zeros_like(state)

        c, s = c_ref[...], s_ref[...]
        q = _rot(q_ref[...], c, s)
        k = _rot(k_ref[...], c, s) * RET_SCALE
        vb = v_ref[...].astype(BF16)
        st = state[...]
        st_ref[...] = st
        sc = _dot(q.astype(BF16), k.astype(BF16), "nt") * dm_ref[...]
        o = _dot(sc.astype(BF16), vb)
        o += _dot((q * qd_ref[...]).astype(BF16), st.astype(BF16))
        o_ref[...] = o
        kv = _dot((k * kd_ref[...]).astype(BF16), vb, "tn")
        state[...] = qd_ref[CHUNK - 1:CHUNK, 0:1] * st + kv

    hd = lambda h, n: (h, 0, 0)
    tab = pl.BlockSpec((None, CHUNK, CHUNK), hd)
    pos = pl.BlockSpec((CHUNK, CHUNK), lambda h, n: (n, 0))
    return _pcall(
        body, name="ret_fwd", grid=(RET_HEADS, nch),
        in_specs=[pl.BlockSpec((CHUNK, 128), lambda h, n: (n, h)),
                  pl.BlockSpec((CHUNK, 128), lambda h, n: (n, RET_HEADS + h)),
                  pl.BlockSpec((CHUNK, 256), lambda h, n: (n, RET_HEADS + h)),
                  pos, pos, tab, tab, tab],
        out_specs=[pl.BlockSpec((CHUNK, 256), lambda h, n: (n, h)),
                   pl.BlockSpec((None, None, 128, 256), lambda h, n: (h, n, 0, 0))],
        out_shape=[jax.ShapeDtypeStruct((t, 1024), F32), jax.ShapeDtypeStruct((RET_HEADS, nch, 128, 256), F32)],
        scratch_shapes=[pltpu.VMEM((128, 256), F32)],
        compiler_params=_params(dimension_semantics=("parallel", "arbitrary")))(
            proj, proj, proj, cos2, sin2, dmat, qdec, kdec)


def _ret_bwd(proj, states, do, tables):
    t = proj.shape[0]
    nch = t // CHUNK
    dmat, qdec, kdec, cos2, sin2 = tables

    def body(q_ref, k_ref, v_ref, do_ref, st_ref, c_ref, s_ref, dm_ref, qd_ref, kd_ref,
             dq_ref, dk_ref, dv_ref, rst):
        @pl.when(pl.program_id(1) == 0)
        def _():
            rst[...] = jnp.zeros_like(rst)

        c, s = c_ref[...], s_ref[...]
        q = _rot(q_ref[...], c, s)
        k = _rot(k_ref[...], c, s) * RET_SCALE
        qb, kb = q.astype(BF16), k.astype(BF16)
        vb = v_ref[...].astype(BF16)
        dob = do_ref[...].astype(BF16)
        pb = st_ref[...].astype(BF16)
        r = rst[...]
        rb = r.astype(BF16)
        dm, qd, kd = dm_ref[...], qd_ref[...], kd_ref[...]
        sb = (_dot(qb, kb, "nt") * dm).astype(BF16)
        dsb = (_dot(dob, vb, "nt") * dm).astype(BF16)
        dq = _dot(dsb, kb) + _dot(dob, pb, "nt") * qd
        dk = _dot(dsb, qb, "tn") + _dot(vb, rb, "nt") * kd
        dv = _dot(sb, dob, "tn") + _dot((k * kd).astype(BF16), rb)
        rst[...] = _dot((q * qd).astype(BF16), dob, "tn") + qd[CHUNK - 1:CHUNK, 0:1] * r
        dq_ref[...] = _unrot(dq, c, s).astype(BF16)
        dk_ref[...] = (_unrot(dk, c, s) * RET_SCALE).astype(BF16)
        dv_ref[...] = dv.astype(BF16)

    rev = lambda n: nch - 1 - n
    tab = pl.BlockSpec((None, CHUNK, CHUNK), lambda h, n: (h, 0, 0))
    pos = pl.BlockSpec((CHUNK, CHUNK), lambda h, n: (rev(n), 0))
    return _pcall(
        body, name="ret_bwd", grid=(RET_HEADS, nch),
        in_specs=[pl.BlockSpec((CHUNK, 128), lambda h, n: (rev(n), h)),
                  pl.BlockSpec((CHUNK, 128), lambda h, n: (rev(n), RET_HEADS + h)),
                  pl.BlockSpec((CHUNK, 256), lambda h, n: (rev(n), RET_HEADS + h)),
                  pl.BlockSpec((CHUNK, 256), lambda h, n: (rev(n), h)),
                  pl.BlockSpec((None, None, 128, 256), lambda h, n: (h, rev(n), 0, 0)),
                  pos, pos, tab, tab, tab],
        out_specs=[pl.BlockSpec((CHUNK, 128), lambda h, n: (rev(n), h)),
                   pl.BlockSpec((CHUNK, 128), lambda h, n: (rev(n), h)),
                   pl.BlockSpec((CHUNK, 256), lambda h, n: (rev(n), h))],
        out_shape=[jax.ShapeDtypeStruct((t, 512), BF16), jax.ShapeDtypeStruct((t, 512), BF16),
                   jax.ShapeDtypeStruct((t, 1024), BF16)],
        scratch_shapes=[pltpu.VMEM((128, 256), F32)],
        compiler_params=_params(dimension_semantics=("parallel", "arbitrary")))(
            proj, proj, proj, do, states, cos2, sin2, dmat, qdec, kdec)


def _gn_gate_fwd(o, proj, gn_g):
    t = o.shape[0]
    tr = _rows(t)

    def body(o_ref, g_ref, w_ref, c_ref):
        for h in range(RET_HEADS):
            sl = slice(256 * h, 256 * (h + 1))
            x = o_ref[:, sl]
            mu = jnp.mean(x, axis=-1, keepdims=True)
            xc = x - mu
            rstd = lax.rsqrt(jnp.mean(xc * xc, axis=-1, keepdims=True) + EPS)
            g = g_ref[:, sl]
            c_ref[:, sl] = (g * _sigmoid(g) * (xc * rstd * w_ref[:, sl])).astype(BF16)

    return _pcall(body, name="gn_gate_fwd", grid=(t // tr,),
                  in_specs=[pl.BlockSpec((tr, 1024), lambda i: (i, 0)),
                            pl.BlockSpec((tr, 1024), lambda i: (i, 2)),
                            pl.BlockSpec((1, 1024), lambda i: (0, 0))],
                  out_specs=pl.BlockSpec((tr, 1024), lambda i: (i, 0)),
                  out_shape=jax.ShapeDtypeStruct((t, 2048), BF16))(o, proj, gn_g)


def _gn_gate_bwd(dcat, o, proj, gn_g):
    t = o.shape[0]
    tr = _rows(t)

    def body(d_ref, o_ref, g_ref, w_ref, do_ref, dg_ref, dw_ref):
        @pl.when(pl.program_id(0) == 0)
        def _():
            dw_ref[...] = jnp.zeros_like(dw_ref)

        for h in range(RET_HEADS):
            sl = slice(256 * h, 256 * (h + 1))
            x = o_ref[:, sl]
            mu = jnp.mean(x, axis=-1, keepdims=True)
            xc = x - mu
            rstd = lax.rsqrt(jnp.mean(xc * xc, axis=-1, keepdims=True) + EPS)
            xh = xc * rstd
            w = w_ref[:, sl]
            g = g_ref[:, sl]
            sg = _sigmoid(g)
            d = d_ref[:, sl]
            don = d * (g * sg)
            dg_ref[:, sl] = (d * (xh * w) * (sg * (1.0 + g * (1.0 - sg)))).astype(BF16)
            dw_ref[:, sl] += jnp.sum(don * xh, axis=0, keepdims=True)
            dxh = don * w
            m1 = jnp.mean(dxh, axis=-1, keepdims=True)
            m2 = jnp.mean(dxh * xh, axis=-1, keepdims=True)
            do_ref[:, sl] = rstd * (dxh - m1 - xh * m2)

    row = pl.BlockSpec((tr, 1024), lambda i: (i, 0))
    vec = pl.BlockSpec((1, 1024), lambda i: (0, 0))
    return _pcall(body, name="gn_gate_bwd", grid=(t // tr,),
                  in_specs=[row, row, pl.BlockSpec((tr, 1024), lambda i: (i, 2)), vec],
                  out_specs=[row, row, vec],
                  out_shape=[jax.ShapeDtypeStruct((t, 1024), F32), jax.ShapeDtypeStruct((t, 1024), BF16),
                             jax.ShapeDtypeStruct((1, 1024), F32)])(dcat, o, proj, gn_g)


def _row_ids(i, tr):
    return i * tr + lax.broadcasted_iota(jnp.int32, (tr, 1), 0)


def _conv_fwd(cat, proj, conv_w, conv_b, ln_g, ln_b):
    t = proj.shape[0]
    tr = _rows(t)
    hb = tr // HALO

    def body(cat_in, ua_ref, ug_ref, pa_ref, pg_ref, w_ref, b_ref, lg_ref, lb_ref, c_ref, hd_ref, y_ref, xs):
        del cat_in
        i = pl.program_id(0)
        hdn = ua_ref[...] * _sigmoid(ug_ref[...])
        hd_ref[...] = hdn
        prev = pa_ref[...] * _sigmoid(pg_ref[...])
        xs[0:HALO, :] = jnp.where(i > 0, prev, 0.0)
        xs[HALO:HALO + tr, :] = hdn
        acc = jnp.zeros((tr, 1024), F32) + b_ref[...]
        for w in range(CONV_WIDTH):
            acc += w_ref[w:w + 1, :] * xs[pl.ds(HALO - (CONV_WIDTH - 1) + w, tr), :]
        y_ref[...] = acc
        mu = jnp.mean(acc, axis=-1, keepdims=True)
        yc = acc - mu
        rstd = lax.rsqrt(jnp.mean(yc * yc, axis=-1, keepdims=True) + EPS)
        yn = yc * rstd * lg_ref[...] + lb_ref[...]
        c = yn * _sigmoid(yn)
        c_ref[...] = jnp.where(_row_ids(i, tr) >= PAD_FRONT, c, 0.0).astype(BF16)

    row = pl.BlockSpec((tr, 1024), lambda i: (i, 0))
    vec = pl.BlockSpec((1, 1024), lambda i: (0, 0))
    halo = lambda col: pl.BlockSpec((HALO, 1024), lambda i: (jnp.maximum(i * hb - 1, 0), col))
    return _pcall(body, name="conv_fwd", grid=(t // tr,),
                  in_specs=[pl.BlockSpec(memory_space=pl.ANY),
                            pl.BlockSpec((tr, 1024), lambda i: (i, 3)), pl.BlockSpec((tr, 1024), lambda i: (i, 4)),
                            halo(3), halo(4), pl.BlockSpec((32, 1024), lambda i: (0, 0)), vec, vec, vec],
                  out_specs=[pl.BlockSpec((tr, 1024), lambda i: (i, 1)), row, row],
                  out_shape=[jax.ShapeDtypeStruct((t, 2048), BF16), jax.ShapeDtypeStruct((t, 1024), F32),
                             jax.ShapeDtypeStruct((t, 1024), F32)],
                  scratch_shapes=[pltpu.VMEM((tr + HALO, 1024), F32)],
                  input_output_aliases={0: 0})(cat, proj, proj, proj, proj, conv_w, conv_b, ln_g, ln_b)


def _conv_bwd_ln(dcat, y, ln_g, ln_b):
    t = y.shape[0]
    tr = _rows(t)

    def body(d_ref, y_ref, lg_ref, lb_ref, dy_ref, dlg_ref, dlb_ref, dcb_ref):
        i = pl.program_id(0)

        @pl.when(i == 0)
        def _():
            dlg_ref[...] = jnp.zeros_like(dlg_ref)
            dlb_ref[...] = jnp.zeros_like(dlb_ref)
            dcb_ref[...] = jnp.zeros_like(dcb_ref)

        y = y_ref[...]
        mu = jnp.mean(y, axis=-1, keepdims=True)
        yc = y - mu
        rstd = lax.rsqrt(jnp.mean(yc * yc, axis=-1, keepdims=True) + EPS)
        xh = yc * rstd
        lg = lg_ref[...]
        yn = xh * lg + lb_ref[...]
        sg = _sigmoid(yn)
        dyn = jnp.where(_row_ids(i, tr) >= PAD_FRONT, d_ref[...] * (sg * (1.0 + yn * (1.0 - sg))), 0.0)
        dlg_ref[...] += jnp.sum(dyn * xh, axis=0, keepdims=True)
        dlb_ref[...] += jnp.sum(dyn, axis=0, keepdims=True)
        dxh = dyn * lg
        m1 = jnp.mean(dxh, axis=-1, keepdims=True)
        m2 = jnp.mean(dxh * xh, axis=-1, keepdims=True)
        dy = rstd * (dxh - m1 - xh * m2)
        dy_ref[...] = dy
        dcb_ref[...] += jnp.sum(dy, axis=0, keepdims=True)

    row = pl.BlockSpec((tr, 1024), lambda i: (i, 0))
    vec = pl.BlockSpec((1, 1024), lambda i: (0, 0))
    vshape = jax.ShapeDtypeStruct((1, 1024), F32)
    return _pcall(body, name="conv_bwd_ln", grid=(t // tr,),
                  in_specs=[pl.BlockSpec((tr, 1024), lambda i: (i, 1)), row, vec, vec],
                  out_specs=[row, vec, vec, vec],
                  out_shape=[jax.ShapeDtypeStruct((t, 1024), F32), vshape, vshape, vshape])(dcat, y, ln_g, ln_b)


def _conv_bwd_taps(dy, hdn, proj, conv_w):
    t = dy.shape[0]
    tr = _rows(t)
    hb = tr // HALO
    nt = t // tr

    def body(dy_ref, nx_ref, hd_ref, ph_ref, ua_ref, ug_ref, w_ref, da_ref, dg_ref, dw_ref, ys, xs):
        i = pl.program_id(0)

        @pl.when(i == 0)
        def _():
            dw_ref[...] = jnp.zeros_like(dw_ref)

        dy = dy_ref[...]
        ys[0:tr, :] = dy
        ys[tr:tr + HALO, :] = jnp.where(i < nt - 1, nx_ref[...], 0.0)
        xs[0:HALO, :] = jnp.where(i > 0, ph_ref[...], 0.0)
        xs[HALO:HALO + tr, :] = hd_ref[...]
        dh = jnp.zeros((tr, 1024), F32)
        for w in range(CONV_WIDTH):
            dh += w_ref[w:w + 1, :] * ys[pl.ds(CONV_WIDTH - 1 - w, tr), :]
            dw_ref[w:w + 1, :] += jnp.sum(dy * xs[pl.ds(HALO - (CONV_WIDTH - 1) + w, tr), :], axis=0, keepdims=True)
        dh = jnp.where(_row_ids(i, tr) >= PAD_FRONT, dh, 0.0)
        sg = _sigmoid(ug_ref[...])
        da_ref[...] = (dh * sg).astype(BF16)
        dg_ref[...] = (dh * ua_ref[...] * sg * (1.0 - sg)).astype(BF16)

    row = pl.BlockSpec((tr, 1024), lambda i: (i, 0))
    return _pcall(body, name="conv_bwd_taps", grid=(nt,),
                  in_specs=[row, pl.BlockSpec((HALO, 1024), lambda i: (jnp.minimum((i + 1) * hb, nt * hb - 1), 0)),
                            row, pl.BlockSpec((HALO, 1024), lambda i: (jnp.maximum(i * hb - 1, 0), 0)),
                            pl.BlockSpec((tr, 1024), lambda i: (i, 3)), pl.BlockSpec((tr, 1024), lambda i: (i, 4)),
                            pl.BlockSpec((32, 1024), lambda i: (0, 0))],
                  out_specs=[row, row, pl.BlockSpec((32, 1024), lambda i: (0, 0))],
                  out_shape=[jax.ShapeDtypeStruct((t, 1024), BF16), jax.ShapeDtypeStruct((t, 1024), BF16),
                             jax.ShapeDtypeStruct((32, 1024), F32)],
                  scratch_shapes=[pltpu.VMEM((tr + HALO, 1024), F32), pltpu.VMEM((tr + HALO, 1024), F32)])(
                      dy, dy, hdn, hdn, proj, proj, conv_w)


NEG_BIG = -1e30


def _seg_tables(qb):
    j = np.arange(128)
    bd = (j[:, None] // 64 == j[None, :] // 64).astype(np.float32)
    ones = np.ones((128, 128), np.float32)
    later = np.concatenate([(j[:, None] >= j[None, :]).astype(np.float32), ones], axis=1)
    earlier = np.concatenate([(j[:, None] < j[None, :]).astype(np.float32), ones], axis=1)
    per = qb // CHUNK
    row = np.arange(qb)[:, None]
    pad = np.broadcast_to(j[None, :] < PAD_FRONT, (qb, 128))
    diag = [(g * CHUNK + j[None, :]) >= row for g in range(per)]
    masks = diag + [np.zeros((qb, 128), bool), pad, diag[0] | pad]
    bias = np.stack([np.where(m, NEG_BIG, 0.0) for m in masks]).astype(np.float32)
    dup = lambda m: np.concatenate([m, m], axis=0)
    return (jnp.asarray(bd, BF16), jnp.asarray(dup(later), BF16), jnp.asarray(dup(earlier), BF16),
            jnp.asarray(bias, F32))


def _split_dot(x, m):
    hi = x.astype(BF16)
    lo = (x - hi.astype(F32)).astype(BF16)
    return _dot(hi, m) + _dot(lo, m)


def _qk_norm_fwd(qkv, qg, kg, bd):
    t = qkv.shape[0]
    tr = _rows(t)
    nb = tr // CHUNK

    def body(q_ref, k_ref, v_ref, qg_ref, kg_ref, bd_ref, qo, kt, k2, vt, v2):
        bdm = bd_ref[...]
        lane = lax.broadcasted_iota(jnp.int32, (1, 128), 1)
        sub = lax.broadcasted_iota(jnp.int32, (128, 1), 0)

        def pair_layouts(x, t_ref, s_ref, hp, b):
            xt = x.T
            t_ref[hp, b] = jnp.concatenate([jnp.where(sub < 64, xt, 0.0), jnp.where(sub >= 64, xt, 0.0)],
                                           axis=1).astype(BF16)
            s_ref[hp, b] = jnp.concatenate([jnp.where(lane < 64, x, 0.0), jnp.where(lane >= 64, x, 0.0)],
                                           axis=0).astype(BF16)

        for hp in range(8):
            sl = slice(128 * hp, 128 * (hp + 1))
            x = q_ref[:, sl]
            r = lax.rsqrt(_split_dot(x * x, bdm) * (1.0 / 64) + EPS)
            qo[:, sl] = (x * r * (qg_ref[:, sl] * SB_SCALE)).astype(BF16)
            x = k_ref[:, sl]
            r = lax.rsqrt(_split_dot(x * x, bdm) * (1.0 / 64) + EPS)
            kn = x * r * kg_ref[:, sl]
            v = v_ref[:, sl]
            for b in range(nb):
                rows = slice(CHUNK * b, CHUNK * (b + 1))
                pair_layouts(kn[rows], kt, k2, hp, b)
                pair_layouts(v[rows], vt, v2, hp, b)

    col = lambda c: pl.BlockSpec((tr, 1024), lambda i: (i, c))
    vec = pl.BlockSpec((1, 1024), lambda i: (0, 0))
    wide = pl.BlockSpec((8, nb, 128, 256), lambda i: (0, i, 0, 0))
    tall = pl.BlockSpec((8, nb, 256, 128), lambda i: (0, i, 0, 0))
    wsh = jax.ShapeDtypeStruct((8, t // CHUNK, 128, 256), BF16)
    tsh = jax.ShapeDtypeStruct((8, t // CHUNK, 256, 128), BF16)
    return _pcall(body, name="qk_norm_fwd", grid=(t // tr,),
                  in_specs=[col(0), col(1), col(2), vec, vec, pl.BlockSpec((128, 128), lambda i: (0, 0))],
                  out_specs=[col(0), wide, tall, wide, tall],
                  out_shape=[jax.ShapeDtypeStruct((t, 1024), BF16), wsh, tsh, wsh, tsh])(qkv, qkv, qkv, qg, kg, bd)


def _qk_norm_bwd(qkv, dq, dk, dv, qg, kg, bd):
    t = qkv.shape[0]
    tr = _rows(t)

    def body(q_ref, k_ref, dq_ref, dk_ref, dv_ref, qg_ref, kg_ref, bd_ref, o_ref, dqg_ref, dkg_ref):
        @pl.when(pl.program_id(0) == 0)
        def _():
            dqg_ref[...] = jnp.zeros_like(dqg_ref)
            dkg_ref[...] = jnp.zeros_like(dkg_ref)

        bdm = bd_ref[...]
        for part, (src, d_ref, g_ref, dg_ref) in enumerate(((q_ref, dq_ref, qg_ref, dqg_ref),
                                                           (k_ref, dk_ref, kg_ref, dkg_ref))):
            for cix in range(8):
                sl = slice(128 * cix, 128 * (cix + 1))
                x = src[:, sl]
                d = d_ref[:, sl]
                r = lax.rsqrt(_split_dot(x * x, bdm) * (1.0 / 64) + EPS)
                u = d * g_ref[:, sl]
                m = _split_dot(u * x, bdm) * (1.0 / 64)
                o_ref[:, 1024 * part + 128 * cix:1024 * part + 128 * (cix + 1)] = (r * u - x * (r * r * r * m)).astype(BF16)
                dg_ref[:, sl] += jnp.sum(d * x * r, axis=0, keepdims=True)
        o_ref[:, 2048:3072] = dv_ref[...].astype(BF16)

    col = lambda c: pl.BlockSpec((tr, 1024), lambda i: (i, c))
    vec = pl.BlockSpec((1, 1024), lambda i: (0, 0))
    vsh = jax.ShapeDtypeStruct((1, 1024), F32)
    return _pcall(body, name="qk_norm_bwd", grid=(t // tr,),
                  in_specs=[col(0), col(1), col(0), col(0), col(0), vec, vec, pl.BlockSpec((128, 128), lambda i: (0, 0))],
                  out_specs=[pl.BlockSpec((tr, 3072), lambda i: (i, 0)), vec, vec],
                  out_shape=[jax.ShapeDtypeStruct((t, 3072), BF16), vsh, vsh])(qkv, qkv, dq, dk, dv, qg, kg, bd)


def _split2(x):
    hi = x.astype(BF16)
    lo = (x - hi.astype(F32)).astype(BF16)
    return jnp.concatenate([hi, lo], axis=1)


def _sb_scores(z, later_tab):
    e = jnp.exp(-jnp.abs(z))
    ope = 1.0 + e
    sp = jnp.maximum(z, 0.0) + jnp.log(ope)
    return e, ope, _dot(_split2(sp), later_tab)


def _sb_bias_index(i, kb, per):
    g = kb - i * per
    return jnp.where(kb == 0, jnp.where(i == 0, per + 2, per + 1), jnp.where(g >= 0, g, per))


def _sb_qb(t):
    return _tile(t, (384, 128))


def _sb_fwd(qh, kt, v2, later_tab, bias_tab):
    t = qh.shape[0]
    qb = _sb_qb(t)
    per = qb // CHUNK
    nkb_all = t // CHUNK

    def body(q_ref, kt_ref, v2_ref, tab_ref, bias_ref, o_ref, c_ref, acc, carry, zbuf):
        i = pl.program_id(1)
        q = q_ref[...]
        acc[...] = jnp.zeros_like(acc)
        carry[...] = jnp.zeros_like(carry)
        nkb = (i + 1) * per
        zbuf[...] = _dot(q, kt_ref[nkb - 1])

        def step(s, _):
            kb = nkb - 1 - s
            bias = bias_ref[_sb_bias_index(i, kb, per)]
            z2 = zbuf[...]
            zbuf[...] = _dot(q, kt_ref[jnp.maximum(kb - 1, 0)])
            ws = []
            for hh in range(2):
                sl = slice(128 * hh, 128 * (hh + 1))
                z = z2[:, sl] + bias
                _, _, cu = _sb_scores(z, tab_ref[...])
                cin = carry[hh]
                ws.append(jnp.exp(z - cu[:, :128] - cin).astype(BF16))
                carry[hh] = cin + cu[:, 128:]
            acc[...] += _dot(jnp.concatenate(ws, axis=1), v2_ref[kb])
            return 0

        lax.fori_loop(0, nkb, step, 0)
        o_ref[...] = acc[...]
        for hh in range(2):
            c_ref[:, 128 * hh:128 * (hh + 1)] = carry[hh]

    blk = pl.BlockSpec((qb, 128), lambda h, i: (i, h))
    wide = pl.BlockSpec((None, nkb_all, 128, 256), lambda h, i: (h, 0, 0, 0))
    tall = pl.BlockSpec((None, nkb_all, 256, 128), lambda h, i: (h, 0, 0, 0))
    return _pcall(body, name="sb_fwd", grid=(8, t // qb),
                  in_specs=[blk, wide, tall, pl.BlockSpec((256, 256), lambda h, i: (0, 0)),
                            pl.BlockSpec((per + 3, qb, 128), lambda h, i: (0, 0, 0))],
                  out_specs=[blk, pl.BlockSpec((qb, 256), lambda h, i: (i, h))],
                  out_shape=[jax.ShapeDtypeStruct((t, 1024), F32), jax.ShapeDtypeStruct((t, 2048), F32)],
                  scratch_shapes=[pltpu.VMEM((qb, 128), F32), pltpu.VMEM((2, qb, 128), F32),
                                  pltpu.VMEM((qb, 256), F32)],
                  compiler_params=_params(dimension_semantics=("parallel", "arbitrary")))(
                      qh, kt, v2, later_tab, bias_tab)


def _sb_bwd(qh, kt, k2, vt, carries, do, later_tab, earlier_tab, bias_tab):
    t = qh.shape[0]
    qb = _sb_qb(t)
    per = qb // CHUNK
    nkb_all = t // CHUNK

    def body(q_ref, kt_ref, k2_ref, vt_ref, c_ref, do_ref, tab_ref, etab_ref, bias_ref,
             dq_ref, dk_ref, dv_ref, acc, gcarry, later, zbuf, dwbuf):
        i = pl.program_id(1)

        @pl.when(i == 0)
        def _():
            dk_ref[...] = jnp.zeros_like(dk_ref)
            dv_ref[...] = jnp.zeros_like(dv_ref)

        q = q_ref[...]
        dob = do_ref[...].astype(BF16)
        lane = lax.broadcasted_iota(jnp.int32, (1, 128), 1)
        acc[...] = jnp.zeros_like(acc)
        gcarry[...] = jnp.zeros_like(gcarry)
        for hh in range(2):
            later[hh] = c_ref[:, 128 * hh:128 * (hh + 1)]
        nkb = (i + 1) * per
        zbuf[...] = _dot(q, kt_ref[0])
        dwbuf[...] = _dot(dob, vt_ref[0])

        def step(kb, _):
            bias = bias_ref[_sb_bias_index(i, kb, per)]
            z2 = zbuf[...]
            dw2 = dwbuf[...]
            nxt = jnp.minimum(kb + 1, nkb - 1)
            zbuf[...] = _dot(q, kt_ref[nxt])
            dwbuf[...] = _dot(dob, vt_ref[nxt])
            dzs, ws = [], []
            for hh in range(2):
                sl = slice(128 * hh, 128 * (hh + 1))
                z = z2[:, sl] + bias
                e, ope, cu = _sb_scores(z, tab_ref[...])
                cin = later[hh] - cu[:, 128:]
                later[hh] = cin
                w = jnp.exp(z - cu[:, :128] - cin)
                gw = w * dw2[:, sl]
                cu2 = _dot(_split2(gw), etab_ref[...])
                gin = gcarry[hh]
                gcarry[hh] = gin + cu2[:, 128:]
                r = 1.0 / ope
                sig = jnp.where(z >= 0, r, e * r)
                dzs.append((gw - sig * (gw + cu2[:, :128] + gin)).astype(BF16))
                ws.append(w.astype(BF16))
            dz2 = jnp.concatenate(dzs, axis=1)
            w2 = jnp.concatenate(ws, axis=1)
            acc[...] += _dot(dz2, k2_ref[kb])
            dk2 = _dot(dz2, q, "tn")
            dv2 = _dot(w2, dob, "tn")
            dk_ref[kb] += jnp.where(lane < 64, dk2[:128], dk2[128:])
            dv_ref[kb] += jnp.where(lane < 64, dv2[:128], dv2[128:])
            return 0

        lax.fori_loop(0, nkb, step, 0)
        dq_ref[...] = acc[...] * SB_SCALE

    blk = pl.BlockSpec((qb, 128), lambda h, i: (i, h))
    wide = pl.BlockSpec((None, nkb_all, 128, 256), lambda h, i: (h, 0, 0, 0))
    tall = pl.BlockSpec((None, nkb_all, 256, 128), lambda h, i: (h, 0, 0, 0))
    tab = pl.BlockSpec((256, 256), lambda h, i: (0, 0))
    kv_out = pl.BlockSpec((nkb_all, 128, 128), lambda h, i: (0, 0, h))
    ksh = jax.ShapeDtypeStruct((nkb_all, 128, 1024), F32)
    dq, dk, dv = _pcall(
        body, name="sb_bwd", grid=(8, t // qb),
        in_specs=[blk, wide, tall, wide, pl.BlockSpec((qb, 256), lambda h, i: (i, h)), blk, tab, tab,
                  pl.BlockSpec((per + 3, qb, 128), lambda h, i: (0, 0, 0))],
        out_specs=[blk, kv_out, kv_out], out_shape=[jax.ShapeDtypeStruct((t, 1024), F32), ksh, ksh],
        scratch_shapes=[pltpu.VMEM((qb, 128), F32), pltpu.VMEM((2, qb, 128), F32), pltpu.VMEM((2, qb, 128), F32),
                        pltpu.VMEM((qb, 256), F32), pltpu.VMEM((qb, 256), F32)],
        compiler_params=_params(dimension_semantics=("parallel", "arbitrary")))(
            qh, kt, k2, vt, carries, do, later_tab, earlier_tab, bias_tab)
    return dq, dk.reshape(t, 1024), dv.reshape(t, 1024)


def _adamw_math(w, g, m, v):
    m = ADAM_B1 * m + (1.0 - ADAM_B1) * g
    v = ADAM_B2 * v + (1.0 - ADAM_B2) * (g * g)
    m_hat = m / (1.0 - ADAM_B1 ** ADAM_STEP)
    v_hat = v / (1.0 - ADAM_B2 ** ADAM_STEP)
    delta = -ADAM_LR * (m_hat / (jnp.sqrt(v_hat) + ADAM_EPS) + ADAM_WD * w)
    return delta, m, v


def _adamw(name, w, parts, m, v):
    shape = w.shape
    c = shape[-1]
    w2, m2, v2 = (a.reshape(-1, c) for a in (w, m, v))
    p3 = parts.reshape(N_DEV, -1, c)
    r = w2.shape[0]
    tr = _tile(r, (256, 128, 112, 8))

    def body(w_ref, p_ref, m_ref, v_ref, g_out, d_out, m_out, v_out):
        g = p_ref[0].astype(F32)
        for s in range(1, N_DEV):
            g = g + p_ref[s].astype(F32)
        d, mn, vn = _adamw_math(w_ref[...], g, m_ref[...], v_ref[...])
        g_out[...] = g
        d_out[...] = d
        m_out[...] = mn
        v_out[...] = vn

    row = pl.BlockSpec((tr, c), lambda i: (i, 0))
    osh = jax.ShapeDtypeStruct((r, c), F32)
    outs = _pcall(body, name=name, grid=(r // tr,),
                  in_specs=[row, pl.BlockSpec((N_DEV, tr, c), lambda i: (0, i, 0)), row, row],
                  out_specs=[row, row, row, row], out_shape=[osh, osh, osh, osh])(w2, p3, m2, v2)
    return tuple(o.reshape(shape) for o in outs)


def _place():
    x, y, c = lax.axis_index("x"), lax.axis_index("y"), lax.axis_index("c")
    return x, y, c, 4 * x + 2 * y + c


def _peer(x, y, c, rel):
    return (x ^ ((rel >> 2) & 1), y ^ ((rel >> 1) & 1), c ^ (rel & 1))


def _gather_weights(shards, small):
    n = len(shards)

    def body(*refs):
        ins, small_in = refs[:n], refs[n]
        outs, small_out = refs[n + 1:2 * n + 1], refs[2 * n + 1]
        stage = refs[2 * n + 2:3 * n + 2]
        send, recv, lsem = refs[3 * n + 2], refs[3 * n + 3], refs[3 * n + 4]
        x, y, c, me = _place()
        copies = []
        for w in range(n + 1):
            if w < n:
                stage[w][...] = ins[w][...].astype(BF16)
                src, dst = stage[w], outs[w]
            else:
                src, dst = small_in, small_out
            local = pltpu.make_async_copy(src, dst.at[me], lsem.at[w])
            local.start()
            copies.append(local)
            for rel in range(1, N_DEV):
                cp = pltpu.make_async_remote_copy(src_ref=src, dst_ref=dst.at[me], send_sem=send.at[w, rel - 1],
                                                  recv_sem=recv.at[w, rel - 1], device_id=_peer(x, y, c, rel),
                                                  device_id_type=MESH)
                cp.start()
        for w in range(n + 1):
            src, dst = (stage[w], outs[w]) if w < n else (small_in, small_out)
            for rel in range(1, N_DEV):
                cp = pltpu.make_async_remote_copy(src_ref=src, dst_ref=dst.at[me ^ rel], send_sem=send.at[w, rel - 1],
                                                  recv_sem=recv.at[w, rel - 1], device_id=_peer(x, y, c, rel),
                                                  device_id_type=MESH)
                cp.wait_send()
                cp.wait_recv()
        for local in copies:
            local.wait()

    vm = pl.BlockSpec(memory_space=pltpu.VMEM)
    hbm = pl.BlockSpec(memory_space=pl.ANY)
    out_shape = [jax.ShapeDtypeStruct((N_DEV,) + s.shape, BF16) for s in shards]
    out_shape.append(jax.ShapeDtypeStruct((N_DEV,) + small.shape, F32))
    return _pcall(body, name="gather_weights", in_specs=[vm] * (n + 1), out_specs=[hbm] * n + [vm],
                  out_shape=out_shape,
                  scratch_shapes=[pltpu.VMEM(s.shape, BF16) for s in shards]
                  + [pltpu.SemaphoreType.DMA((n + 1, N_DEV - 1)), pltpu.SemaphoreType.DMA((n + 1, N_DEV - 1)),
                     pltpu.SemaphoreType.DMA((n + 1,))],
                  compiler_params=_params(has_side_effects=True))(*shards, small)


def _scatter_grads(grads, layouts):
    n = len(grads)
    n_out = 1 + max(o for o, _ in layouts)
    out_shape = [None] * n_out
    for g, (o, lead) in zip(grads, layouts):
        if lead is None:
            out_shape[o] = jax.ShapeDtypeStruct((N_DEV,) + g.shape[1:], g.dtype)
        else:
            out_shape[o] = jax.ShapeDtypeStruct((N_DEV, 2) + g.shape[1:], g.dtype)

    def body(*refs):
        ins, outs = refs[:n], refs[n:n + n_out]
        send, recv, lsem = refs[n + n_out], refs[n + n_out + 1], refs[n + n_out + 2]
        x, y, c, me = _place()

        def dst_of(i, slot):
            o, lead = layouts[i]
            return outs[o].at[slot] if lead is None else outs[o].at[slot, lead]

        locals_ = []
        for i in range(n):
            local = pltpu.make_async_copy(ins[i].at[me], dst_of(i, me), lsem.at[i])
            local.start()
            locals_.append(local)
            for rel in range(1, N_DEV):
                pltpu.make_async_remote_copy(src_ref=ins[i].at[me ^ rel], dst_ref=dst_of(i, me),
                                             send_sem=send.at[i, rel - 1], recv_sem=recv.at[i, rel - 1],
                                             device_id=_peer(x, y, c, rel), device_id_type=MESH).start()
        for i in range(n):
            for rel in range(1, N_DEV):
                cp = pltpu.make_async_remote_copy(src_ref=ins[i].at[me ^ rel], dst_ref=dst_of(i, me ^ rel),
                                                  send_sem=send.at[i, rel - 1], recv_sem=recv.at[i, rel - 1],
                                                  device_id=_peer(x, y, c, rel), device_id_type=MESH)
                cp.wait_send()
                cp.wait_recv()
        for local in locals_:
            local.wait()

    hbm = pl.BlockSpec(memory_space=pl.ANY)
    return _pcall(body, name="scatter_grads", in_specs=[hbm] * n, out_specs=[hbm] * n_out, out_shape=out_shape,
                  scratch_shapes=[pltpu.SemaphoreType.DMA((n, N_DEV - 1)), pltpu.SemaphoreType.DMA((n, N_DEV - 1)),
                                  pltpu.SemaphoreType.DMA((n,))],
                  compiler_params=_params(has_side_effects=True))(*grads)


ROW_MIX, ROW_MLP, ROW_CB, ROW_LG, ROW_LB, ROW_QN, ROW_KN = 0, 2, 4, 5, 6, 7, 8
ROW_META, ROW_CW, ROW_GN, SMALL_ROWS = 16, 32, 64, 72


def _allreduce_small(part):
    def body(p_ref, o_ref, slots, send, recv):
        x, y, c, me = _place()
        slots[me] = p_ref[...]
        for rel in range(1, N_DEV):
            pltpu.make_async_remote_copy(src_ref=p_ref, dst_ref=slots.at[me], send_sem=send.at[rel - 1],
                                         recv_sem=recv.at[rel - 1], device_id=_peer(x, y, c, rel),
                                         device_id_type=MESH).start()
        for rel in range(1, N_DEV):
            cp = pltpu.make_async_remote_copy(src_ref=p_ref, dst_ref=slots.at[me ^ rel], send_sem=send.at[rel - 1],
                                              recv_sem=recv.at[rel - 1], device_id=_peer(x, y, c, rel),
                                              device_id_type=MESH)
            cp.wait_send()
            cp.wait_recv()
        tot = slots[0]
        for s in range(1, N_DEV):
            tot = tot + slots[s]
        o_ref[...] = tot
        for row in (ROW_QN, ROW_KN):
            v = tot[row:row + 1, :]
            f = v[:, 0:128]
            for k in range(1, 8):
                f = f + v[:, 128 * k:128 * (k + 1)]
            o_ref[row:row + 1, 0:64] = f[:, 0:64] + f[:, 64:128]

    vm = pl.BlockSpec(memory_space=pltpu.VMEM)
    return _pcall(body, name="allreduce_small", in_specs=[vm], out_specs=vm,
                  out_shape=jax.ShapeDtypeStruct(part.shape, F32),
                  scratch_shapes=[pltpu.VMEM((N_DEV,) + part.shape, F32), pltpu.SemaphoreType.DMA((N_DEV - 1,)),
                                  pltpu.SemaphoreType.DMA((N_DEV - 1,))],
                  compiler_params=_params(has_side_effects=True))(part)


def _adamw_small(w, g, m, v):
    def body(w_ref, g_ref, m_ref, v_ref, d_out, m_out, v_out):
        d, mn, vn = _adamw_math(w_ref[...], g_ref[...], m_ref[...], v_ref[...])
        d_out[...] = d
        m_out[...] = mn
        v_out[...] = vn

    osh = jax.ShapeDtypeStruct(w.shape, F32)
    return _pcall(body, name="adamw_small", out_shape=[osh, osh, osh])(w, g, m, v)


def _local_step(h0, target, p):
    t = h0.shape[0]
    tables = _ret_tables(t)
    bd, later_tab, earlier_tab, bias_tab = _seg_tables(_sb_qb(t))
    row = lambda a, i: a[i:i + 1]

    hn_a = _rms_fwd("rms_mix0", h0, row(p["norm_mix_g"], 0))
    proj = _mm_cols("proj_in", hn_a, p["w_in"], ())
    o_ret, states = _ret_fwd(proj, tables)
    gn_flat = p["gn_g"].reshape(1, 1024)
    cat = _gn_gate_fwd(o_ret, proj, gn_flat)
    cat, hdn, ycv = _conv_fwd(cat, proj, p["conv_w"], p["conv_b"], p["ln_g"], p["ln_b"])
    h1 = _mm_rows("mix_out", cat, p["w_out"], (), h0)
    hn_b = _rms_fwd("rms_mlp0", h1, row(p["norm_mlp_g"], 0))
    a0, s0 = _mm_cols("mlp0_up", hn_b, p["w1"], (0,), epi="relu2")
    h2 = _mm_rows("mlp0_down", s0, p["w2"], (0,), h1)

    hn_c = _rms_fwd("rms_mix1", h2, row(p["norm_mix_g"], 1))
    qkv = _mm_cols("qkv", hn_c, p["w_qkv"], ())
    qg = jnp.tile(p["qn_g"], (1, 16))
    kg = jnp.tile(p["kn_g"], (1, 16))
    qh, kt, k2, vt, v2 = _qk_norm_fwd(qkv, qg, kg, bd)
    o_sb, carries = _sb_fwd(qh, kt, v2, later_tab, bias_tab)
    h3 = _mm_rows("attn_out", o_sb, p["w_o"], (), h2)
    hn_d = _rms_fwd("rms_mlp1", h3, row(p["norm_mlp_g"], 1))
    a1, s1 = _mm_cols("mlp1_up", hn_d, p["w1"], (1,), epi="relu2")
    h4 = _mm_rows("mlp1_down", s1, p["w2"], (1,), h3)

    dh, loss = _loss_bwd(h4, target)

    def mlp_bwd(tag, layer, dh, h_in, hn, a, s):
        da = _mm_rows_t(f"{tag}_dact", dh, p["w2"], (layer,), out_dtype=BF16, epi="drelu2", extra=a)
        dw2 = _wgrad_rows(f"{tag}_dw2", s, dh, 512)
        dw1 = _wgrad_cols(f"{tag}_dw1", hn, da, 512)
        dhn = _mm_cols_t(f"{tag}_dhn", da, p["w1"], (layer,))
        dh, dg = _rms_bwd(f"{tag}_rms_bwd", dhn, h_in, row(p["norm_mlp_g"], layer), dh)
        return dh, dg, dw1, dw2

    dh, dg_mlp1, dw1_1, dw2_1 = mlp_bwd("mlp1", 1, dh, h3, hn_d, a1, s1)

    do_sb = _mm_rows_t("attn_dout", dh, p["w_o"], ())
    dw_o = _wgrad_rows("attn_dwo", o_sb, dh, 128)
    dq, dk, dv = _sb_bwd(qh, kt, k2, vt, carries, do_sb, later_tab, earlier_tab, bias_tab)
    dqkv, dqg, dkg = _qk_norm_bwd(qkv, dq, dk, dv, qg, kg, bd)
    dw_qkv = _wgrad_cols("qkv_dw", hn_c, dqkv, 384)
    dhn = _mm_cols_t("qkv_dhn", dqkv, p["w_qkv"], ())
    dh, dg_mix1 = _rms_bwd("mix1_rms_bwd", dhn, h2, row(p["norm_mix_g"], 1), dh)

    dh, dg_mlp0, dw1_0, dw2_0 = mlp_bwd("mlp0", 0, dh, h1, hn_b, a0, s0)

    dcat = _mm_rows_t("mix_dcat", dh, p["w_out"], ())
    dw_out = _wgrad_rows("mix_dwout", cat, dh, 256)
    do_ret, dgate, dgn = _gn_gate_bwd(dcat, o_ret, proj, gn_flat)
    dq_r, dk_r, dv_r = _ret_bwd(proj, states, do_ret, tables)
    dy, dlg, dlb, dcb = _conv_bwd_ln(dcat, ycv, p["ln_g"], p["ln_b"])
    dua, dug, dcw = _conv_bwd_taps(dy, hdn, proj, p["conv_w"])
    dproj = jnp.concatenate([dq_r, dk_r, dv_r, dgate, dua, dug], axis=1)
    dw_in = _wgrad_cols("proj_dw", hn_a, dproj, 640)
    dhn = _mm_cols_t("proj_dhn", dproj, p["w_in"], ())
    dh, dg_mix0 = _rms_bwd("mix0_rms_bwd", dhn, h0, row(p["norm_mix_g"], 0), dh)

    rid = lax.broadcasted_iota(jnp.int32, (16, 1), 0)
    vecs = sum(jnp.where(rid == k, v, 0.0)
               for k, v in enumerate((dg_mix0, dg_mix1, dg_mlp0, dg_mlp1, dcb, dlg, dlb, dqg, dkg)))
    small = jnp.concatenate([vecs, dh[PAD_FRONT:TOK0], dcw, jnp.where(rid[:8] == 0, dgn, 0.0)], axis=0)
    big = dict(w_in=dw_in, w_out=dw_out, w_qkv=dw_qkv, w_o=dw_o, w1=(dw1_0, dw1_1), w2=(dw2_0, dw2_1))
    return loss[0, 0], dh[TOK0:], small, big


_SMALL_NAMES = ("meta", "norm_mix_g", "norm_mlp_g", "even_ret_gn_g", "even_conv_w", "even_conv_b",
                "even_conv_ln_g", "even_conv_ln_b", "odd_q_norm_g", "odd_k_norm_g")
_BIG_NAMES = ("even_w_in", "even_w_out", "odd_w_qkv", "odd_w_o", "mlp_w1", "mlp_w2")
_ORDER = ("meta", "norm_mix_g", "norm_mlp_g", "even_w_in", "even_ret_gn_g", "even_conv_w", "even_conv_b",
          "even_conv_ln_g", "even_conv_ln_b", "even_w_out", "odd_w_qkv", "odd_q_norm_g", "odd_k_norm_g",
          "odd_w_o", "mlp_w1", "mlp_w2")


def _pack128(a):
    flat = a.reshape(-1)
    n = flat.shape[0]
    rows = -(-n // 128)
    rows8 = -(-rows // 8) * 8
    return jnp.pad(flat, (0, rows8 * 128 - n)).reshape(rows8, 128)


def kernel(x, meta, norm_mix_g, norm_mlp_g, even_w_in, even_ret_gn_g, even_conv_w, even_conv_b, even_conv_ln_g, even_conv_ln_b, even_w_out, odd_w_qkv, odd_q_norm_g, odd_k_norm_g, odd_w_o, mlp_w1, mlp_w2, loss_target, m_meta, m_norm_mix_g, m_norm_mlp_g, m_even_w_in, m_even_ret_gn_g, m_even_conv_w, m_even_conv_b, m_even_conv_ln_g, m_even_conv_ln_b, m_even_w_out, m_odd_w_qkv, m_odd_q_norm_g, m_odd_k_norm_g, m_odd_w_o, m_mlp_w1, m_mlp_w2, v_meta, v_norm_mix_g, v_norm_mlp_g, v_even_w_in, v_even_ret_gn_g, v_even_conv_w, v_even_conv_b, v_even_conv_ln_g, v_even_conv_ln_b, v_even_w_out, v_odd_w_qkv, v_odd_q_norm_g, v_odd_k_norm_g, v_odd_w_o, v_mlp_w1, v_mlp_w2):
    w = dict(meta=meta, norm_mix_g=norm_mix_g, norm_mlp_g=norm_mlp_g, even_w_in=even_w_in,
             even_ret_gn_g=even_ret_gn_g, even_conv_w=even_conv_w, even_conv_b=even_conv_b,
             even_conv_ln_g=even_conv_ln_g, even_conv_ln_b=even_conv_ln_b, even_w_out=even_w_out,
             odd_w_qkv=odd_w_qkv, odd_q_norm_g=odd_q_norm_g, odd_k_norm_g=odd_k_norm_g, odd_w_o=odd_w_o,
             mlp_w1=mlp_w1, mlp_w2=mlp_w2)
    mom = dict(meta=m_meta, norm_mix_g=m_norm_mix_g, norm_mlp_g=m_norm_mlp_g, even_w_in=m_even_w_in,
               even_ret_gn_g=m_even_ret_gn_g, even_conv_w=m_even_conv_w, even_conv_b=m_even_conv_b,
               even_conv_ln_g=m_even_conv_ln_g, even_conv_ln_b=m_even_conv_ln_b, even_w_out=m_even_w_out,
               odd_w_qkv=m_odd_w_qkv, odd_q_norm_g=m_odd_q_norm_g, odd_k_norm_g=m_odd_k_norm_g, odd_w_o=m_odd_w_o,
               mlp_w1=m_mlp_w1, mlp_w2=m_mlp_w2)
    var = dict(meta=v_meta, norm_mix_g=v_norm_mix_g, norm_mlp_g=v_norm_mlp_g, even_w_in=v_even_w_in,
               even_ret_gn_g=v_even_ret_gn_g, even_conv_w=v_even_conv_w, even_conv_b=v_even_conv_b,
               even_conv_ln_g=v_even_conv_ln_g, even_conv_ln_b=v_even_conv_ln_b, even_w_out=v_even_w_out,
               odd_w_qkv=v_odd_w_qkv, odd_q_norm_g=v_odd_q_norm_g, odd_k_norm_g=v_odd_k_norm_g, odd_w_o=v_odd_w_o,
               mlp_w1=v_mlp_w1, mlp_w2=v_mlp_w2)
    me = 4 * lax.axis_index("x") + 2 * lax.axis_index("y") + lax.axis_index("c")

    small_in = jnp.concatenate([meta, jnp.pad(even_conv_w[0], ((0, 1), (0, 0))),
                                jnp.pad(even_ret_gn_g[0], ((0, 4), (0, 96)))], axis=0)
    shards = [even_w_in[0], even_w_out[0], odd_w_qkv[0], odd_w_o[0], mlp_w1, mlp_w2]
    g_in, g_out, g_qkv, g_o, g_w1, g_w2, g_small = _gather_weights(shards, small_in)
    cols = lambda a: jnp.transpose(a, (1, 0, 2)).reshape(a.shape[1], -1)
    p = dict(w_in=g_in, w_out=g_out, w_qkv=g_qkv, w_o=g_o, w1=g_w1, w2=g_w2,
             norm_mix_g=norm_mix_g, norm_mlp_g=norm_mlp_g, conv_b=even_conv_b, ln_g=even_conv_ln_g,
             ln_b=even_conv_ln_b, qn_g=odd_q_norm_g, kn_g=odd_k_norm_g,
             gn_g=cols(g_small[:, 48:52, :32]),
             conv_w=jnp.pad(cols(g_small[:, 16:47]), ((0, 1), (0, 0))))
    meta_full = cols(g_small[:, 0:16])

    h0 = jnp.concatenate([jnp.zeros((PAD_FRONT, D_MODEL), F32), meta_full, x[0]], axis=0)
    loss_part, grad_x, small_part, big = _local_step(h0, loss_target[0], p)
    loss = lax.psum(loss_part, ("x", "y", "c"))

    grads = [big["w_in"], big["w_out"], big["w_qkv"], big["w_o"], big["w1"][0], big["w1"][1],
             big["w2"][0], big["w2"][1]]
    layouts = [(0, None), (1, None), (2, None), (3, None), (4, 0), (4, 1), (5, 0), (5, 1)]
    r_in, r_out, r_qkv, r_o, r_w1, r_w2 = _scatter_grads(grads, layouts)
    tot = _allreduce_small(small_part)

    out = {}
    for name, parts in (("even_w_in", r_in), ("even_w_out", r_out), ("odd_w_qkv", r_qkv), ("odd_w_o", r_o),
                        ("mlp_w1", r_w1), ("mlp_w2", r_w2)):
        out[name] = _adamw("adamw_" + name, w[name], parts, mom[name], var[name])

    shard_cols = lambda a, width: lax.dynamic_slice_in_dim(a, me * width, width, axis=1)
    one = lambda r: tot[r:r + 1]
    small_g = dict(
        norm_mix_g=tot[ROW_MIX:ROW_MIX + 2], norm_mlp_g=tot[ROW_MLP:ROW_MLP + 2],
        even_conv_b=one(ROW_CB), even_conv_ln_g=one(ROW_LG), even_conv_ln_b=one(ROW_LB),
        odd_q_norm_g=one(ROW_QN)[:, :64], odd_k_norm_g=one(ROW_KN)[:, :64],
        meta=shard_cols(tot[ROW_META:ROW_META + N_META], 128),
        even_conv_w=shard_cols(tot[ROW_CW:ROW_CW + CONV_WIDTH], 128)[None],
        even_ret_gn_g=shard_cols(tot[ROW_GN].reshape(4, 256), 32)[None])
    packs = {n: (_pack128(w[n]), _pack128(small_g[n]), _pack128(mom[n]), _pack128(var[n])) for n in _SMALL_NAMES}
    cat4 = [jnp.concatenate([packs[n][i] for n in _SMALL_NAMES], axis=0) for i in range(4)]
    d_s, m_s, v_s = _adamw_small(*cat4)
    r0 = 0
    for n in _SMALL_NAMES:
        rows = packs[n][0].shape[0]
        size = w[n].size
        take = lambda a: a[r0:r0 + rows].reshape(-1)[:size].reshape(w[n].shape)
        out[n] = (small_g[n].reshape(w[n].shape), take(d_s), take(m_s), take(v_s))
        r0 += rows

    res = [loss, grad_x[None]]
    for i in range(4):
        res.extend(out[n][i] for n in _ORDER)
    return tuple(res)
```

```python
import functools

import numpy as np
import jax
import jax.numpy as jnp
from jax import lax
from jax.experimental import pallas as pl
from jax.experimental.pallas import tpu as pltpu

F32 = jnp.float32
BF16 = jnp.bfloat16

D_MODEL = 1024
N_META = 16
CHUNK = 128
PAD_FRONT = 112
TOK0 = PAD_FRONT + N_META
EPS = 1e-6
N_DEV = 8
RET_HEADS = 4
RET_DECAY_OFFSET = 5.0
ROPE_BASE = 10000.0
CONV_WIDTH = 31
HALO = 32
SB_SCALE = 64 ** -0.5
RET_SCALE = 128 ** -0.5
ADAM_LR, ADAM_B1, ADAM_B2, ADAM_EPS, ADAM_WD, ADAM_STEP = 0.001, 0.9, 0.999, 1e-08, 0.01, 10
VMEM_LIMIT = 56 * 1024 * 1024
MESH = pl.DeviceIdType.MESH


def _pcall(body, **kw):
    return pl.pallas_call(body, **kw)


def _params(**kw):
    return pltpu.CompilerParams(vmem_limit_bytes=VMEM_LIMIT, **kw)


def _tile(n, cands):
    for c in cands:
        if n % c == 0:
            return c
    raise ValueError(f"no tile for {n} in {cands}")


def _sigmoid(x):
    return 1.0 / (1.0 + jnp.exp(-x))


_DIMS = {
    "nn": (((1,), (0,)), ((), ())),
    "nt": (((1,), (1,)), ((), ())),
    "tn": (((0,), (0,)), ((), ())),
}


def _matmul(name, a, b, *, grid, a_spec, b_spec, o_spec, out_shape, contract, acc_shape,
            epi="plain", extra=None, extra_spec=None):
    nk = grid[2]
    dims = _DIMS[contract]
    n_in = 3 if extra is not None else 2
    n_out = 2 if epi == "relu2" else 1

    def body(*refs):
        a_ref, b_ref = refs[0], refs[1]
        e_ref = refs[2] if extra is not None else None
        outs = refs[n_in:n_in + n_out]
        acc = refs[-1]
        k = pl.program_id(2)

        @pl.when(k == 0)
        def _():
            acc[...] = jnp.zeros_like(acc)

        acc[...] += lax.dot_general(a_ref[...].astype(BF16), b_ref[...].astype(BF16), dims,
                                    preferred_element_type=F32)

        @pl.when(k == nk - 1)
        def _():
            r = acc[...]
            if epi == "plain":
                outs[0][...] = r.astype(outs[0].dtype)
            elif epi == "residual":
                outs[0][...] = (r + e_ref[...]).astype(outs[0].dtype)
            elif epi == "relu2":
                outs[0][...] = r
                rr = jnp.maximum(r, 0.0)
                outs[1][...] = (rr * rr).astype(BF16)
            elif epi == "drelu2":
                outs[0][...] = (r * (2.0 * jnp.maximum(e_ref[...], 0.0))).astype(outs[0].dtype)

    in_specs = [a_spec, b_spec] + ([extra_spec] if extra is not None else [])
    args = (a, b) + ((extra,) if extra is not None else ())
    if n_out == 2:
        out_specs = [o_spec, o_spec]
    else:
        out_specs = o_spec
    return _pcall(body, name=name, grid=grid, in_specs=in_specs, out_specs=out_specs,
                  out_shape=out_shape, scratch_shapes=[pltpu.VMEM(acc_shape, F32)],
                  compiler_params=_params(dimension_semantics=("parallel", "parallel", "arbitrary")))(*args)


def _tm(t):
    return _tile(t, (1408, 768, 384, 128))


def _mm_cols(name, a, wb, lead, out_dtype=F32, epi="plain"):
    t, kdim = a.shape
    n = wb.shape[-1]
    tm, tk = _tm(t), _tile(kdim, (1024, 512))
    nl = len(lead)
    b_spec = pl.BlockSpec((None,) * (1 + nl) + (tk, n), lambda i, j, k: (j,) + lead + (k, 0))
    o_spec = pl.BlockSpec((tm, n), lambda i, j, k: (i, j))
    if epi == "relu2":
        out_shape = [jax.ShapeDtypeStruct((t, N_DEV * n), F32), jax.ShapeDtypeStruct((t, N_DEV * n), BF16)]
    else:
        out_shape = jax.ShapeDtypeStruct((t, N_DEV * n), out_dtype)
    return _matmul(name, a, wb, grid=(t // tm, N_DEV, kdim // tk),
                   a_spec=pl.BlockSpec((tm, tk), lambda i, j, k: (i, k)), b_spec=b_spec, o_spec=o_spec,
                   out_shape=out_shape, contract="nn", acc_shape=(tm, n), epi=epi)


def _mm_cols_t(name, a, wb, lead, out_dtype=F32):
    t = a.shape[0]
    kdim, n = wb.shape[-2], wb.shape[-1]
    tm, tn = _tm(t), _tile(kdim, (512,))
    nl = len(lead)
    b_spec = pl.BlockSpec((None,) * (1 + nl) + (tn, n), lambda i, j, k: (k,) + lead + (j, 0))
    return _matmul(name, a, wb, grid=(t // tm, kdim // tn, N_DEV),
                   a_spec=pl.BlockSpec((tm, n), lambda i, j, k: (i, k)), b_spec=b_spec,
                   o_spec=pl.BlockSpec((tm, tn), lambda i, j, k: (i, j)),
                   out_shape=jax.ShapeDtypeStruct((t, kdim), out_dtype), contract="nt", acc_shape=(tm, tn))


def _mm_rows(name, a, wb, lead, residual):
    t = a.shape[0]
    r, n = wb.shape[-2], wb.shape[-1]
    tm, tn = _tm(t), _tile(n, (512,))
    nl = len(lead)
    b_spec = pl.BlockSpec((None,) * (1 + nl) + (r, tn), lambda i, j, k: (k,) + lead + (0, j))
    o_spec = pl.BlockSpec((tm, tn), lambda i, j, k: (i, j))
    return _matmul(name, a, wb, grid=(t // tm, n // tn, N_DEV),
                   a_spec=pl.BlockSpec((tm, r), lambda i, j, k: (i, k)), b_spec=b_spec, o_spec=o_spec,
                   out_shape=jax.ShapeDtypeStruct((t, n), F32), contract="nn", acc_shape=(tm, tn),
                   epi="residual", extra=residual, extra_spec=o_spec)


def _mm_rows_t(name, a, wb, lead, out_dtype=F32, epi="plain", extra=None):
    t, n = a.shape
    r = wb.shape[-2]
    tm, tk = _tm(t), _tile(n, (1024,))
    nl = len(lead)
    b_spec = pl.BlockSpec((None,) * (1 + nl) + (r, tk), lambda i, j, k: (j,) + lead + (0, k))
    o_spec = pl.BlockSpec((tm, r), lambda i, j, k: (i, j))
    return _matmul(name, a, wb, grid=(t // tm, N_DEV, n // tk),
                   a_spec=pl.BlockSpec((tm, tk), lambda i, j, k: (i, k)), b_spec=b_spec, o_spec=o_spec,
                   out_shape=jax.ShapeDtypeStruct((t, N_DEV * r), out_dtype), contract="nt",
                   acc_shape=(tm, r), epi=epi, extra=extra, extra_spec=o_spec if extra is not None else None)


def _wgrad_cols(name, x, dy, n):
    t, kdim = x.shape
    tk = _tm(t)
    return _matmul(name, x, dy, grid=(1, N_DEV, t // tk),
                   a_spec=pl.BlockSpec((tk, kdim), lambda i, j, k: (k, 0)),
                   b_spec=pl.BlockSpec((tk, n), lambda i, j, k: (k, j)),
                   o_spec=pl.BlockSpec((None, kdim, n), lambda i, j, k: (j, 0, 0)),
                   out_shape=jax.ShapeDtypeStruct((N_DEV, kdim, n), BF16), contract="tn", acc_shape=(kdim, n))


def _wgrad_rows(name, x, dy, r):
    t = x.shape[0]
    n = dy.shape[1]
    tk, tn = _tm(t), _tile(n, (512,))
    return _matmul(name, x, dy, grid=(N_DEV, n // tn, t // tk),
                   a_spec=pl.BlockSpec((tk, r), lambda i, j, k: (k, i)),
                   b_spec=pl.BlockSpec((tk, tn), lambda i, j, k: (k, j)),
                   o_spec=pl.BlockSpec((None, r, tn), lambda i, j, k: (i, 0, j)),
                   out_shape=jax.ShapeDtypeStruct((N_DEV, r, n), BF16), contract="tn", acc_shape=(r, tn))


def _rows(t):
    return _tile(t, (384, 128))


def _rms_fwd(name, h, g):
    t = h.shape[0]
    tr = _rows(t)

    def body(h_ref, g_ref, o_ref):
        x = h_ref[...]
        r = lax.rsqrt(jnp.mean(x * x, axis=-1, keepdims=True) + EPS)
        o_ref[...] = (x * r * g_ref[...]).astype(BF16)

    row = pl.BlockSpec((tr, D_MODEL), lambda i: (i, 0))
    vec = pl.BlockSpec((1, D_MODEL), lambda i: (0, 0))
    return _pcall(body, name=name, grid=(t // tr,), in_specs=[row, vec], out_specs=row,
                  out_shape=jax.ShapeDtypeStruct((t, D_MODEL), BF16))(h, g)


def _rms_bwd(name, dhn, h, g, dres):
    t = h.shape[0]
    tr = _rows(t)

    def body(d_ref, h_ref, g_ref, r_ref, o_ref, dg_ref):
        @pl.when(pl.program_id(0) == 0)
        def _():
            dg_ref[...] = jnp.zeros_like(dg_ref)

        x = h_ref[...]
        d = d_ref[...]
        r = lax.rsqrt(jnp.mean(x * x, axis=-1, keepdims=True) + EPS)
        u = d * g_ref[...]
        m = jnp.mean(u * x, axis=-1, keepdims=True)
        o_ref[...] = r_ref[...] + r * u - x * (r * r * r * m)
        dg_ref[...] += jnp.sum(d * x * r, axis=0, keepdims=True)

    row = pl.BlockSpec((tr, D_MODEL), lambda i: (i, 0))
    vec = pl.BlockSpec((1, D_MODEL), lambda i: (0, 0))
    return _pcall(body, name=name, grid=(t // tr,), in_specs=[row, row, vec, row], out_specs=[row, vec],
                  out_shape=[jax.ShapeDtypeStruct((t, D_MODEL), F32), jax.ShapeDtypeStruct((1, D_MODEL), F32)])(
                      dhn, h, g, dres)


def _loss_bwd(h, target):
    t = h.shape[0]
    nb = t // CHUNK

    def body(h_ref, t_ref, d_ref, l_ref):
        i = pl.program_id(0)

        @pl.when(i == 0)
        def _():
            d_ref[...] = jnp.zeros_like(d_ref)
            l_ref[...] = jnp.zeros_like(l_ref)

        @pl.when(i > 0)
        def _():
            diff = h_ref[...] - t_ref[...]
            d_ref[...] = diff * (1.0 / D_MODEL)
            l_ref[...] += jnp.sum(diff * diff) * (0.5 / D_MODEL)

    return _pcall(body, name="loss_bwd", grid=(nb,),
                  in_specs=[pl.BlockSpec((CHUNK, D_MODEL), lambda i: (i, 0)),
                            pl.BlockSpec((CHUNK, D_MODEL), lambda i: (jnp.maximum(i - 1, 0), 0))],
                  out_specs=[pl.BlockSpec((CHUNK, D_MODEL), lambda i: (i, 0)),
                             pl.BlockSpec((8, 128), lambda i: (0, 0))],
                  out_shape=[jax.ShapeDtypeStruct((t, D_MODEL), F32), jax.ShapeDtypeStruct((8, 128), F32)])(h, target)


def _ret_tables(t):
    hh = np.arange(RET_HEADS, dtype=np.float64)
    log_g = np.log1p(-np.exp2(-RET_DECAY_OFFSET - hh))
    idx = np.arange(CHUNK, dtype=np.float64)
    diff = idx[:, None] - idx[None, :]
    dmat = np.where(diff[None] >= 0, np.exp(np.maximum(diff, 0.0)[None] * log_g[:, None, None]), 0.0)
    qdec = np.exp((idx + 1.0)[None, :, None] * log_g[:, None, None]) * np.ones((1, 1, CHUNK))
    kdec = np.exp((CHUNK - 1 - idx)[None, :, None] * log_g[:, None, None]) * np.ones((1, 1, CHUNK))
    half = CHUNK // 2
    inv_freq = (ROPE_BASE ** (-np.arange(half, dtype=np.float32) / half)).astype(np.float32)
    ang = (np.arange(t, dtype=np.float32)[:, None] * inv_freq[None, :]).astype(np.float32).astype(np.float64)
    cos2 = np.concatenate([np.cos(ang), np.cos(ang)], axis=1)
    sin2 = np.concatenate([-np.sin(ang), np.sin(ang)], axis=1)
    return tuple(jnp.asarray(v, F32) for v in (dmat, qdec, kdec, cos2, sin2))


def _rot(x, c, s):
    return x * c + pltpu.roll(x, CHUNK // 2, 1) * s


def _unrot(dx, c, s):
    return dx * c + pltpu.roll(dx * s, CHUNK // 2, 1)


def _dot(a, b, contract="nn"):
    return lax.dot_general(a, b, _DIMS[contract], preferred_element_type=F32)


def _ret_fwd(proj, tables):
    t = proj.shape[0]
    nch = t // CHUNK
    dmat, qdec, kdec, cos2, sin2 = tables

    def body(q_ref, k_ref, v_ref, c_ref, s_ref, dm_ref, qd_ref, kd_ref, o_ref, st_ref, state):
        @pl.when(pl.program_id(1) == 0)
        def _():
            state[...] = jnp.zeros_like(state)

        c, s = c_ref[...], s_ref[...]
        q = _rot(q_ref[...], c, s)
        k = _rot(k_ref[...], c, s) * RET_SCALE
        vb = v_ref[...].astype(BF16)
        st = state[...]
        st_ref[...] = st
        sc = _dot(q.astype(BF16), k.astype(BF16), "nt") * dm_ref[...]
        o = _dot(sc.astype(BF16), vb)
        o += _dot((q * qd_ref[...]).astype(BF16), st.astype(BF16))
        o_ref[...] = o
        kv = _dot((k * kd_ref[...]).astype(BF16), vb, "tn")
        state[...] = qd_ref[CHUNK - 1:CHUNK, 0:1] * st + kv

    hd = lambda h, n: (h, 0, 0)
    tab = pl.BlockSpec((None, CHUNK, CHUNK), hd)
    pos = pl.BlockSpec((CHUNK, CHUNK), lambda h, n: (n, 0))
    return _pcall(
        body, name="ret_fwd", grid=(RET_HEADS, nch),
        in_specs=[pl.BlockSpec((CHUNK, 128), lambda h, n: (n, h)),
                  pl.BlockSpec((CHUNK, 128), lambda h, n: (n, RET_HEADS + h)),
                  pl.BlockSpec((CHUNK, 256), lambda h, n: (n, RET_HEADS + h)),
                  pos, pos, tab, tab, tab],
        out_specs=[pl.BlockSpec((CHUNK, 256), lambda h, n: (n, h)),
                   pl.BlockSpec((None, None, 128, 256), lambda h, n: (h, n, 0, 0))],
        out_shape=[jax.ShapeDtypeStruct((t, 1024), F32), jax.ShapeDtypeStruct((RET_HEADS, nch, 128, 256), F32)],
        scratch_shapes=[pltpu.VMEM((128, 256), F32)],
        compiler_params=_params(dimension_semantics=("parallel", "arbitrary")))(
            proj, proj, proj, cos2, sin2, dmat, qdec, kdec)


def _ret_bwd(proj, states, do, tables):
    t = proj.shape[0]
    nch = t // CHUNK
    dmat, qdec, kdec, cos2, sin2 = tables

    def body(q_ref, k_ref, v_ref, do_ref, st_ref, c_ref, s_ref, dm_ref, qd_ref, kd_ref,
             dq_ref, dk_ref, dv_ref, rst):
        @pl.when(pl.program_id(1) == 0)
        def _():
            rst[...] = jnp.zeros_like(rst)

        c, s = c_ref[...], s_ref[...]
        q = _rot(q_ref[...], c, s)
        k = _rot(k_ref[...], c, s) * RET_SCALE
        qb, kb = q.astype(BF16), k.astype(BF16)
        vb = v_ref[...].astype(BF16)
        dob = do_ref[...].astype(BF16)
        pb = st_ref[...].astype(BF16)
        r = rst[...]
        rb = r.astype(BF16)
        dm, qd, kd = dm_ref[...], qd_ref[...], kd_ref[...]
        sb = (_dot(qb, kb, "nt") * dm).astype(BF16)
        dsb = (_dot(dob, vb, "nt") * dm).astype(BF16)
        dq = _dot(dsb, kb) + _dot(dob, pb, "nt") * qd
        dk = _dot(dsb, qb, "tn") + _dot(vb, rb, "nt") * kd
        dv = _dot(sb, dob, "tn") + _dot((k * kd).astype(BF16), rb)
        rst[...] = _dot((q * qd).astype(BF16), dob, "tn") + qd[CHUNK - 1:CHUNK, 0:1] * r
        dq_ref[...] = _unrot(dq, c, s).astype(BF16)
        dk_ref[...] = (_unrot(dk, c, s) * RET_SCALE).astype(BF16)
        dv_ref[...] = dv.astype(BF16)

    rev = lambda n: nch - 1 - n
    tab = pl.BlockSpec((None, CHUNK, CHUNK), lambda h, n: (h, 0, 0))
    pos = pl.BlockSpec((CHUNK, CHUNK), lambda h, n: (rev(n), 0))
    return _pcall(
        body, name="ret_bwd", grid=(RET_HEADS, nch),
        in_specs=[pl.BlockSpec((CHUNK, 128), lambda h, n: (rev(n), h)),
                  pl.BlockSpec((CHUNK, 128), lambda h, n: (rev(n), RET_HEADS + h)),
                  pl.BlockSpec((CHUNK, 256), lambda h, n: (rev(n), RET_HEADS + h)),
                  pl.BlockSpec((CHUNK, 256), lambda h, n: (rev(n), h)),
                  pl.BlockSpec((None, None, 128, 256), lambda h, n: (h, rev(n), 0, 0)),
                  pos, pos, tab, tab, tab],
        out_specs=[pl.BlockSpec((CHUNK, 128), lambda h, n: (rev(n), h)),
                   pl.BlockSpec((CHUNK, 128), lambda h, n: (rev(n), h)),
                   pl.BlockSpec((CHUNK, 256), lambda h, n: (rev(n), h))],
        out_shape=[jax.ShapeDtypeStruct((t, 512), BF16), jax.ShapeDtypeStruct((t, 512), BF16),
                   jax.ShapeDtypeStruct((t, 1024), BF16)],
        scratch_shapes=[pltpu.VMEM((128, 256), F32)],
        compiler_params=_params(dimension_semantics=("parallel", "arbitrary")))(
            proj, proj, proj, do, states, cos2, sin2, dmat, qdec, kdec)


def _gn_gate_fwd(o, proj, gn_g):
    t = o.shape[0]
    tr = _rows(t)

    def body(o_ref, g_ref, w_ref, c_ref):
        for h in range(RET_HEADS):
            sl = slice(256 * h, 256 * (h + 1))
            x = o_ref[:, sl]
            mu = jnp.mean(x, axis=-1, keepdims=True)
            xc = x - mu
            rstd = lax.rsqrt(jnp.mean(xc * xc, axis=-1, keepdims=True) + EPS)
            g = g_ref[:, sl]
            c_ref[:, sl] = (g * _sigmoid(g) * (xc * rstd * w_ref[:, sl])).astype(BF16)

    return _pcall(body, name="gn_gate_fwd", grid=(t // tr,),
                  in_specs=[pl.BlockSpec((tr, 1024), lambda i: (i, 0)),
                            pl.BlockSpec((tr, 1024), lambda i: (i, 2)),
                            pl.BlockSpec((1, 1024), lambda i: (0, 0))],
                  out_specs=pl.BlockSpec((tr, 1024), lambda i: (i, 0)),
                  out_shape=jax.ShapeDtypeStruct((t, 2048), BF16))(o, proj, gn_g)


def _gn_gate_bwd(dcat, o, proj, gn_g):
    t = o.shape[0]
    tr = _rows(t)

    def body(d_ref, o_ref, g_ref, w_ref, do_ref, dg_ref, dw_ref):
        @pl.when(pl.program_id(0) == 0)
        def _():
            dw_ref[...] = jnp.zeros_like(dw_ref)

        for h in range(RET_HEADS):
            sl = slice(256 * h, 256 * (h + 1))
            x = o_ref[:, sl]
            mu = jnp.mean(x, axis=-1, keepdims=True)
            xc = x - mu
            rstd = lax.rsqrt(jnp.mean(xc * xc, axis=-1, keepdims=True) + EPS)
            xh = xc * rstd
            w = w_ref[:, sl]
            g = g_ref[:, sl]
            sg = _sigmoid(g)
            d = d_ref[:, sl]
            don = d * (g * sg)
            dg_ref[:, sl] = (d * (xh * w) * (sg * (1.0 + g * (1.0 - sg)))).astype(BF16)
            dw_ref[:, sl] += jnp.sum(don * xh, axis=0, keepdims=True)
            dxh = don * w
            m1 = jnp.mean(dxh, axis=-1, keepdims=True)
            m2 = jnp.mean(dxh * xh, axis=-1, keepdims=True)
            do_ref[:, sl] = rstd * (dxh - m1 - xh * m2)

    row = pl.BlockSpec((tr, 1024), lambda i: (i, 0))
    vec = pl.BlockSpec((1, 1024), lambda i: (0, 0))
    return _pcall(body, name="gn_gate_bwd", grid=(t // tr,),
                  in_specs=[row, row, pl.BlockSpec((tr, 1024), lambda i: (i, 2)), vec],
                  out_specs=[row, row, vec],
                  out_shape=[jax.ShapeDtypeStruct((t, 1024), F32), jax.ShapeDtypeStruct((t, 1024), BF16),
                             jax.ShapeDtypeStruct((1, 1024), F32)])(dcat, o, proj, gn_g)


def _row_ids(i, tr):
    return i * tr + lax.broadcasted_iota(jnp.int32, (tr, 1), 0)


def _conv_fwd(cat, proj, conv_w, conv_b, ln_g, ln_b):
    t = proj.shape[0]
    tr = _rows(t)
    hb = tr // HALO

    def body(cat_in, ua_ref, ug_ref, pa_ref, pg_ref, w_ref, b_ref, lg_ref, lb_ref, c_ref, hd_ref, y_ref, xs):
        del cat_in
        i = pl.program_id(0)
        hdn = ua_ref[...] * _sigmoid(ug_ref[...])
        hd_ref[...] = hdn
        prev = pa_ref[...] * _sigmoid(pg_ref[...])
        xs[0:HALO, :] = jnp.where(i > 0, prev, 0.0)
        xs[HALO:HALO + tr, :] = hdn
        acc = jnp.zeros((tr, 1024), F32) + b_ref[...]
        for w in range(CONV_WIDTH):
            acc += w_ref[w:w + 1, :] * xs[pl.ds(HALO - (CONV_WIDTH - 1) + w, tr), :]
        y_ref[...] = acc
        mu = jnp.mean(acc, axis=-1, keepdims=True)
        yc = acc - mu
        rstd = lax.rsqrt(jnp.mean(yc * yc, axis=-1, keepdims=True) + EPS)
        yn = yc * rstd * lg_ref[...] + lb_ref[...]
        c = yn * _sigmoid(yn)
        c_ref[...] = jnp.where(_row_ids(i, tr) >= PAD_FRONT, c, 0.0).astype(BF16)

    row = pl.BlockSpec((tr, 1024), lambda i: (i, 0))
    vec = pl.BlockSpec((1, 1024), lambda i: (0, 0))
    halo = lambda col: pl.BlockSpec((HALO, 1024), lambda i: (jnp.maximum(i * hb - 1, 0), col))
    return _pcall(body, name="conv_fwd", grid=(t // tr,),
                  in_specs=[pl.BlockSpec(memory_space=pl.ANY),
                            pl.BlockSpec((tr, 1024), lambda i: (i, 3)), pl.BlockSpec((tr, 1024), lambda i: (i, 4)),
                            halo(3), halo(4), pl.BlockSpec((32, 1024), lambda i: (0, 0)), vec, vec, vec],
                  out_specs=[pl.BlockSpec((tr, 1024), lambda i: (i, 1)), row, row],
                  out_shape=[jax.ShapeDtypeStruct((t, 2048), BF16), jax.ShapeDtypeStruct((t, 1024), F32),
                             jax.ShapeDtypeStruct((t, 1024), F32)],
                  scratch_shapes=[pltpu.VMEM((tr + HALO, 1024), F32)],
                  input_output_aliases={0: 0})(cat, proj, proj, proj, proj, conv_w, conv_b, ln_g, ln_b)


def _conv_bwd_ln(dcat, y, ln_g, ln_b):
    t = y.shape[0]
    tr = _rows(t)

    def body(d_ref, y_ref, lg_ref, lb_ref, dy_ref, dlg_ref, dlb_ref, dcb_ref):
        i = pl.program_id(0)

        @pl.when(i == 0)
        def _():
            dlg_ref[...] = jnp.zeros_like(dlg_ref)
            dlb_ref[...] = jnp.zeros_like(dlb_ref)
            dcb_ref[...] = jnp.zeros_like(dcb_ref)

        y = y_ref[...]
        mu = jnp.mean(y, axis=-1, keepdims=True)
        yc = y - mu
        rstd = lax.rsqrt(jnp.mean(yc * yc, axis=-1, keepdims=True) + EPS)
        xh = yc * rstd
        lg = lg_ref[...]
        yn = xh * lg + lb_ref[...]
        sg = _sigmoid(yn)
        dyn = jnp.where(_row_ids(i, tr) >= PAD_FRONT, d_ref[...] * (sg * (1.0 + yn * (1.0 - sg))), 0.0)
        dlg_ref[...] += jnp.sum(dyn * xh, axis=0, keepdims=True)
        dlb_ref[...] += jnp.sum(dyn, axis=0, keepdims=True)
        dxh = dyn * lg
        m1 = jnp.mean(dxh, axis=-1, keepdims=True)
        m2 = jnp.mean(dxh * xh, axis=-1, keepdims=True)
        dy = rstd * (dxh - m1 - xh * m2)
        dy_ref[...] = dy
        dcb_ref[...] += jnp.sum(dy, axis=0, keepdims=True)

    row = pl.BlockSpec((tr, 1024), lambda i: (i, 0))
    vec = pl.BlockSpec((1, 1024), lambda i: (0, 0))
    vshape = jax.ShapeDtypeStruct((1, 1024), F32)
    return _pcall(body, name="conv_bwd_ln", grid=(t // tr,),
                  in_specs=[pl.BlockSpec((tr, 1024), lambda i: (i, 1)), row, vec, vec],
                  out_specs=[row, vec, vec, vec],
                  out_shape=[jax.ShapeDtypeStruct((t, 1024), F32), vshape, vshape, vshape])(dcat, y, ln_g, ln_b)


def _conv_bwd_taps(dy, hdn, proj, conv_w):
    t = dy.shape[0]
    tr = _rows(t)
    hb = tr // HALO
    nt = t // tr

    def body(dy_ref, nx_ref, hd_ref, ph_ref, ua_ref, ug_ref, w_ref, da_ref, dg_ref, dw_ref, ys, xs):
        i = pl.program_id(0)

        @pl.when(i == 0)
        def _():
            dw_ref[...] = jnp.zeros_like(dw_ref)

        dy = dy_ref[...]
        ys[0:tr, :] = dy
        ys[tr:tr + HALO, :] = jnp.where(i < nt - 1, nx_ref[...], 0.0)
        xs[0:HALO, :] = jnp.where(i > 0, ph_ref[...], 0.0)
        xs[HALO:HALO + tr, :] = hd_ref[...]
        dh = jnp.zeros((tr, 1024), F32)
        for w in range(CONV_WIDTH):
            dh += w_ref[w:w + 1, :] * ys[pl.ds(CONV_WIDTH - 1 - w, tr), :]
            dw_ref[w:w + 1, :] += jnp.sum(dy * xs[pl.ds(HALO - (CONV_WIDTH - 1) + w, tr), :], axis=0, keepdims=True)
        dh = jnp.where(_row_ids(i, tr) >= PAD_FRONT, dh, 0.0)
        sg = _sigmoid(ug_ref[...])
        da_ref[...] = (dh * sg).astype(BF16)
        dg_ref[...] = (dh * ua_ref[...] * sg * (1.0 - sg)).astype(BF16)

    row = pl.BlockSpec((tr, 1024), lambda i: (i, 0))
    return _pcall(body, name="conv_bwd_taps", grid=(nt,),
                  in_specs=[row, pl.BlockSpec((HALO, 1024), lambda i: (jnp.minimum((i + 1) * hb, nt * hb - 1), 0)),
                            row, pl.BlockSpec((HALO, 1024), lambda i: (jnp.maximum(i * hb - 1, 0), 0)),
                            pl.BlockSpec((tr, 1024), lambda i: (i, 3)), pl.BlockSpec((tr, 1024), lambda i: (i, 4)),
                            pl.BlockSpec((32, 1024), lambda i: (0, 0))],
                  out_specs=[row, row, pl.BlockSpec((32, 1024), lambda i: (0, 0))],
                  out_shape=[jax.ShapeDtypeStruct((t, 1024), BF16), jax.ShapeDtypeStruct((t, 1024), BF16),
                             jax.ShapeDtypeStruct((32, 1024), F32)],
                  scratch_shapes=[pltpu.VMEM((tr + HALO, 1024), F32), pltpu.VMEM((tr + HALO, 1024), F32)])(
                      dy, dy, hdn, hdn, proj, proj, conv_w)


NEG_BIG = -1e30


def _seg_tables(qb):
    j = np.arange(128)
    bd = (j[:, None] // 64 == j[None, :] // 64).astype(np.float32)
    ones = np.ones((128, 128), np.float32)
    later = np.concatenate([(j[:, None] >= j[None, :]).astype(np.float32), ones], axis=1)
    earlier = np.concatenate([(j[:, None] < j[None, :]).astype(np.float32), ones], axis=1)
    per = qb // CHUNK
    row = np.arange(qb)[:, None]
    pad = np.broadcast_to(j[None, :] < PAD_FRONT, (qb, 128))
    diag = [(g * CHUNK + j[None, :]) >= row for g in range(per)]
    masks = diag + [np.zeros((qb, 128), bool), pad, diag[0] | pad]
    bias = np.stack([np.where(m, NEG_BIG, 0.0) for m in masks]).astype(np.float32)
    dup = lambda m: np.concatenate([m, m], axis=0)
    return (jnp.asarray(bd, BF16), jnp.asarray(dup(later), BF16), jnp.asarray(dup(earlier), BF16),
            jnp.asarray(bias, F32))


def _split_dot(x, m):
    hi = x.astype(BF16)
    lo = (x - hi.astype(F32)).astype(BF16)
    return _dot(hi, m) + _dot(lo, m)


def _qk_norm_fwd(qkv, qg, kg, bd):
    t = qkv.shape[0]
    tr = _rows(t)
    nb = tr // CHUNK

    def body(q_ref, k_ref, v_ref, qg_ref, kg_ref, bd_ref, qo, kt, k2, vt, v2):
        bdm = bd_ref[...]
        lane = lax.broadcasted_iota(jnp.int32, (1, 128), 1)
        sub = lax.broadcasted_iota(jnp.int32, (128, 1), 0)

        def pair_layouts(x, t_ref, s_ref, hp, b):
            xt = x.T
            t_ref[hp, b] = jnp.concatenate([jnp.where(sub < 64, xt, 0.0), jnp.where(sub >= 64, xt, 0.0)],
                                           axis=1).astype(BF16)
            s_ref[hp, b] = jnp.concatenate([jnp.where(lane < 64, x, 0.0), jnp.where(lane >= 64, x, 0.0)],
                                           axis=0).astype(BF16)

        for hp in range(8):
            sl = slice(128 * hp, 128 * (hp + 1))
            x = q_ref[:, sl]
            r = lax.rsqrt(_split_dot(x * x, bdm) * (1.0 / 64) + EPS)
            qo[:, sl] = (x * r * (qg_ref[:, sl] * SB_SCALE)).astype(BF16)
            x = k_ref[:, sl]
            r = lax.rsqrt(_split_dot(x * x, bdm) * (1.0 / 64) + EPS)
            kn = x * r * kg_ref[:, sl]
            v = v_ref[:, sl]
            for b in range(nb):
                rows = slice(CHUNK * b, CHUNK * (b + 1))
                pair_layouts(kn[rows], kt, k2, hp, b)
                pair_layouts(v[rows], vt, v2, hp, b)

    col = lambda c: pl.BlockSpec((tr, 1024), lambda i: (i, c))
    vec = pl.BlockSpec((1, 1024), lambda i: (0, 0))
    wide = pl.BlockSpec((8, nb, 128, 256), lambda i: (0, i, 0, 0))
    tall = pl.BlockSpec((8, nb, 256, 128), lambda i: (0, i, 0, 0))
    wsh = jax.ShapeDtypeStruct((8, t // CHUNK, 128, 256), BF16)
    tsh = jax.ShapeDtypeStruct((8, t // CHUNK, 256, 128), BF16)
    return _pcall(body, name="qk_norm_fwd", grid=(t // tr,),
                  in_specs=[col(0), col(1), col(2), vec, vec, pl.BlockSpec((128, 128), lambda i: (0, 0))],
                  out_specs=[col(0), wide, tall, wide, tall],
                  out_shape=[jax.ShapeDtypeStruct((t, 1024), BF16), wsh, tsh, wsh, tsh])(qkv, qkv, qkv, qg, kg, bd)


def _qk_norm_bwd(qkv, dq, dk, dv, qg, kg, bd):
    t = qkv.shape[0]
    tr = _rows(t)

    def body(q_ref, k_ref, dq_ref, dk_ref, dv_ref, qg_ref, kg_ref, bd_ref, o_ref, dqg_ref, dkg_ref):
        @pl.when(pl.program_id(0) == 0)
        def _():
            dqg_ref[...] = jnp.zeros_like(dqg_ref)
            dkg_ref[...] = jnp.zeros_like(dkg_ref)

        bdm = bd_ref[...]
        for part, (src, d_ref, g_ref, dg_ref) in enumerate(((q_ref, dq_ref, qg_ref, dqg_ref),
                                                           (k_ref, dk_ref, kg_ref, dkg_ref))):
            for cix in range(8):
                sl = slice(128 * cix, 128 * (cix + 1))
                x = src[:, sl]
                d = d_ref[:, sl]
                r = lax.rsqrt(_split_dot(x * x, bdm) * (1.0 / 64) + EPS)
                u = d * g_ref[:, sl]
                m = _split_dot(u * x, bdm) * (1.0 / 64)
                o_ref[:, 1024 * part + 128 * cix:1024 * part + 128 * (cix + 1)] = (r * u - x * (r * r * r * m)).astype(BF16)
                dg_ref[:, sl] += jnp.sum(d * x * r, axis=0, keepdims=True)
        o_ref[:, 2048:3072] = dv_ref[...].astype(BF16)

    col = lambda c: pl.BlockSpec((tr, 1024), lambda i: (i, c))
    vec = pl.BlockSpec((1, 1024), lambda i: (0, 0))
    vsh = jax.ShapeDtypeStruct((1, 1024), F32)
    return _pcall(body, name="qk_norm_bwd", grid=(t // tr,),
                  in_specs=[col(0), col(1), col(0), col(0), col(0), vec, vec, pl.BlockSpec((128, 128), lambda i: (0, 0))],
                  out_specs=[pl.BlockSpec((tr, 3072), lambda i: (i, 0)), vec, vec],
                  out_shape=[jax.ShapeDtypeStruct((t, 3072), BF16), vsh, vsh])(qkv, qkv, dq, dk, dv, qg, kg, bd)


def _split2(x):
    hi = x.astype(BF16)
    lo = (x - hi.astype(F32)).astype(BF16)
    return jnp.concatenate([hi, lo], axis=1)


def _sb_scores(z, later_tab):
    e = jnp.exp(-jnp.abs(z))
    ope = 1.0 + e
    sp = jnp.maximum(z, 0.0) + jnp.log(ope)
    return e, ope, _dot(_split2(sp), later_tab)


def _sb_bias_index(i, kb, per):
    g = kb - i * per
    return jnp.where(kb == 0, jnp.where(i == 0, per + 2, per + 1), jnp.where(g >= 0, g, per))


def _sb_qb(t):
    return _tile(t, (384, 128))


def _sb_fwd(qh, kt, v2, later_tab, bias_tab):
    t = qh.shape[0]
    qb = _sb_qb(t)
    per = qb // CHUNK
    nkb_all = t // CHUNK

    def body(q_ref, kt_ref, v2_ref, tab_ref, bias_ref, o_ref, c_ref, acc, carry, zbuf):
        i = pl.program_id(1)
        q = q_ref[...]
        acc[...] = jnp.zeros_like(acc)
        carry[...] = jnp.zeros_like(carry)
        nkb = (i + 1) * per
        zbuf[...] = _dot(q, kt_ref[nkb - 1])

        def step(s, _):
            kb = nkb - 1 - s
            bias = bias_ref[_sb_bias_index(i, kb, per)]
            z2 = zbuf[...]
            zbuf[...] = _dot(q, kt_ref[jnp.maximum(kb - 1, 0)])
            ws = []
            for hh in range(2):
                sl = slice(128 * hh, 128 * (hh + 1))
                z = z2[:, sl] + bias
                _, _, cu = _sb_scores(z, tab_ref[...])
                cin = carry[hh]
                ws.append(jnp.exp(z - cu[:, :128] - cin).astype(BF16))
                carry[hh] = cin + cu[:, 128:]
            acc[...] += _dot(jnp.concatenate(ws, axis=1), v2_ref[kb])
            return 0

        lax.fori_loop(0, nkb, step, 0)
        o_ref[...] = acc[...]
        for hh in range(2):
            c_ref[:, 128 * hh:128 * (hh + 1)] = carry[hh]

    blk = pl.BlockSpec((qb, 128), lambda h, i: (i, h))
    wide = pl.BlockSpec((None, nkb_all, 128, 256), lambda h, i: (h, 0, 0, 0))
    tall = pl.BlockSpec((None, nkb_all, 256, 128), lambda h, i: (h, 0, 0, 0))
    return _pcall(body, name="sb_fwd", grid=(8, t // qb),
                  in_specs=[blk, wide, tall, pl.BlockSpec((256, 256), lambda h, i: (0, 0)),
                            pl.BlockSpec((per + 3, qb, 128), lambda h, i: (0, 0, 0))],
                  out_specs=[blk, pl.BlockSpec((qb, 256), lambda h, i: (i, h))],
                  out_shape=[jax.ShapeDtypeStruct((t, 1024), F32), jax.ShapeDtypeStruct((t, 2048), F32)],
                  scratch_shapes=[pltpu.VMEM((qb, 128), F32), pltpu.VMEM((2, qb, 128), F32),
                                  pltpu.VMEM((qb, 256), F32)],
                  compiler_params=_params(dimension_semantics=("parallel", "arbitrary")))(
                      qh, kt, v2, later_tab, bias_tab)


def _sb_bwd(qh, kt, k2, vt, carries, do, later_tab, earlier_tab, bias_tab):
    t = qh.shape[0]
    qb = _sb_qb(t)
    per = qb // CHUNK
    nkb_all = t // CHUNK

    def body(q_ref, kt_ref, k2_ref, vt_ref, c_ref, do_ref, tab_ref, etab_ref, bias_ref,
             dq_ref, dk_ref, dv_ref, acc, gcarry, later, zbuf, dwbuf):
        i = pl.program_id(1)

        @pl.when(i == 0)
        def _():
            dk_ref[...] = jnp.zeros_like(dk_ref)
            dv_ref[...] = jnp.zeros_like(dv_ref)

        q = q_ref[...]
        dob = do_ref[...].astype(BF16)
        lane = lax.broadcasted_iota(jnp.int32, (1, 128), 1)
        acc[...] = jnp.zeros_like(acc)
        gcarry[...] = jnp.zeros_like(gcarry)
        for hh in range(2):
            later[hh] = c_ref[:, 128 * hh:128 * (hh + 1)]
        nkb = (i + 1) * per
        zbuf[...] = _dot(q, kt_ref[0])
        dwbuf[...] = _dot(dob, vt_ref[0])

        def step(kb, _):
            bias = bias_ref[_sb_bias_index(i, kb, per)]
            z2 = zbuf[...]
            dw2 = dwbuf[...]
            nxt = jnp.minimum(kb + 1, nkb - 1)
            zbuf[...] = _dot(q, kt_ref[nxt])
            dwbuf[...] = _dot(dob, vt_ref[nxt])
            dzs, ws = [], []
            for hh in range(2):
                sl = slice(128 * hh, 128 * (hh + 1))
                z = z2[:, sl] + bias
                e, ope, cu = _sb_scores(z, tab_ref[...])
                cin = later[hh] - cu[:, 128:]
                later[hh] = cin
                w = jnp.exp(z - cu[:, :128] - cin)
                gw = w * dw2[:, sl]
                cu2 = _dot(_split2(gw), etab_ref[...])
                gin = gcarry[hh]
                gcarry[hh] = gin + cu2[:, 128:]
                r = 1.0 / ope
                sig = jnp.where(z >= 0, r, e * r)
                dzs.append((gw - sig * (gw + cu2[:, :128] + gin)).astype(BF16))
                ws.append(w.astype(BF16))
            dz2 = jnp.concatenate(dzs, axis=1)
            w2 = jnp.concatenate(ws, axis=1)
            acc[...] += _dot(dz2, k2_ref[kb])
            dk2 = _dot(dz2, q, "tn")
            dv2 = _dot(w2, dob, "tn")
            dk_ref[kb] += jnp.where(lane < 64, dk2[:128], dk2[128:])
            dv_ref[kb] += jnp.where(lane < 64, dv2[:128], dv2[128:])
            return 0

        lax.fori_loop(0, nkb, step, 0)
        dq_ref[...] = acc[...] * SB_SCALE

    blk = pl.BlockSpec((qb, 128), lambda h, i: (i, h))
    wide = pl.BlockSpec((None, nkb_all, 128, 256), lambda h, i: (h, 0, 0, 0))
    tall = pl.BlockSpec((None, nkb_all, 256, 128), lambda h, i: (h, 0, 0, 0))
    tab = pl.BlockSpec((256, 256), lambda h, i: (0, 0))
    kv_out = pl.BlockSpec((nkb_all, 128, 128), lambda h, i: (0, 0, h))
    ksh = jax.ShapeDtypeStruct((nkb_all, 128, 1024), F32)
    dq, dk, dv = _pcall(
        body, name="sb_bwd", grid=(8, t // qb),
        in_specs=[blk, wide, tall, wide, pl.BlockSpec((qb, 256), lambda h, i: (i, h)), blk, tab, tab,
                  pl.BlockSpec((per + 3, qb, 128), lambda h, i: (0, 0, 0))],
        out_specs=[blk, kv_out, kv_out], out_shape=[jax.ShapeDtypeStruct((t, 1024), F32), ksh, ksh],
        scratch_shapes=[pltpu.VMEM((qb, 128), F32), pltpu.VMEM((2, qb, 128), F32), pltpu.VMEM((2, qb, 128), F32),
                        pltpu.VMEM((qb, 256), F32), pltpu.VMEM((qb, 256), F32)],
        compiler_params=_params(dimension_semantics=("parallel", "arbitrary")))(
            qh, kt, k2, vt, carries, do, later_tab, earlier_tab, bias_tab)
    return dq, dk.reshape(t, 1024), dv.reshape(t, 1024)


def _adamw_math(w, g, m, v):
    m = ADAM_B1 * m + (1.0 - ADAM_B1) * g
    v = ADAM_B2 * v + (1.0 - ADAM_B2) * (g * g)
    m_hat = m / (1.0 - ADAM_B1 ** ADAM_STEP)
    v_hat = v / (1.0 - ADAM_B2 ** ADAM_STEP)
    delta = -ADAM_LR * (m_hat / (jnp.sqrt(v_hat) + ADAM_EPS) + ADAM_WD * w)
    return delta, m, v


def _adamw(name, w, owns, recvs, m, v, me):
    shape = w.shape
    c = shape[-1]
    nl = len(owns)
    w3, m3, v3 = (a.reshape(nl, -1, c) for a in (w, m, v))
    r = w3.shape[1]
    tr = _tile(r, (256, 128))
    owns = [o.reshape(N_DEV, r, c) for o in owns]
    recvs = [p.reshape(N_DEV - 1, r, c) for p in recvs]

    def body(me_ref, w_ref, *rest):
        own_refs, recv_refs = rest[:nl], rest[nl:2 * nl]
        m_ref, v_ref = rest[2 * nl:2 * nl + 2]
        g_out, d_out, m_out, v_out = rest[2 * nl + 2:]
        layer = pl.program_id(0)

        def grad(k):
            g = own_refs[k][...].astype(F32)
            for s in range(N_DEV - 1):
                g = g + recv_refs[k][s].astype(F32)
            return g

        g = grad(0)
        for k in range(1, nl):
            g = jnp.where(layer == k, grad(k), g)
        d, mn, vn = _adamw_math(w_ref[...], g, m_ref[...], v_ref[...])
        g_out[...] = g
        d_out[...] = d
        m_out[...] = mn
        v_out[...] = vn

    row = pl.BlockSpec((None, tr, c), lambda l, i, me_ref: (l, i, 0))
    own = lambda k: pl.BlockSpec((None, tr, c), lambda l, i, me_ref: (me_ref[0], jnp.where(l == k, i, 0), 0))
    rcv = lambda k: pl.BlockSpec((N_DEV - 1, tr, c), lambda l, i, me_ref: (0, jnp.where(l == k, i, 0), 0))
    osh = jax.ShapeDtypeStruct((nl, r, c), F32)
    grid_spec = pltpu.PrefetchScalarGridSpec(
        num_scalar_prefetch=1, grid=(nl, r // tr),
        in_specs=[row] + [own(k) for k in range(nl)] + [rcv(k) for k in range(nl)] + [row, row],
        out_specs=[row, row, row, row])
    outs = _pcall(body, name=name, grid_spec=grid_spec, out_shape=[osh, osh, osh, osh])(
        me.reshape(1), w3, *owns, *recvs, m3, v3)
    return tuple(o.reshape(shape) for o in outs)


def _place():
    x, y, c = lax.axis_index("x"), lax.axis_index("y"), lax.axis_index("c")
    return x, y, c, 4 * x + 2 * y + c


def _peer(x, y, c, rel):
    return (x ^ ((rel >> 2) & 1), y ^ ((rel >> 1) & 1), c ^ (rel & 1))


def _gather_first(now, later):
    n, k = len(now), len(later)

    def body(*refs):
        ins, outs = refs[:n + k], refs[n + k:2 * (n + k)]
        send, recv, lsem = refs[2 * (n + k):]
        x, y, c, me = _place()
        locals_ = []
        for w in range(n + k):
            local = pltpu.make_async_copy(ins[w], outs[w].at[me], lsem.at[w])
            local.start()
            locals_.append(local)
        for w in range(n):
            for rel in range(1, N_DEV):
                pltpu.make_async_remote_copy(src_ref=ins[w], dst_ref=outs[w].at[me], send_sem=send.at[w, rel - 1],
                                             recv_sem=recv.at[w, rel - 1], device_id=_peer(x, y, c, rel),
                                             device_id_type=MESH).start()
        for w in range(n):
            for rel in range(1, N_DEV):
                cp = pltpu.make_async_remote_copy(src_ref=ins[w], dst_ref=outs[w].at[me ^ rel],
                                                  send_sem=send.at[w, rel - 1], recv_sem=recv.at[w, rel - 1],
                                                  device_id=_peer(x, y, c, rel), device_id_type=MESH)
                cp.wait_send()
                cp.wait_recv()
        for local in locals_:
            local.wait()

    hbm = pl.BlockSpec(memory_space=pl.ANY)
    arrays = list(now) + list(later)
    return _pcall(body, name="gather_first", in_specs=[hbm] * (n + k), out_specs=[hbm] * (n + k),
                  out_shape=[jax.ShapeDtypeStruct((N_DEV,) + a.shape, a.dtype) for a in arrays],
                  scratch_shapes=[pltpu.SemaphoreType.DMA((n, N_DEV - 1)), pltpu.SemaphoreType.DMA((n, N_DEV - 1)),
                                  pltpu.SemaphoreType.DMA((n + k,))],
                  compiler_params=_params(has_side_effects=True))(*arrays)


_HBM = pl.BlockSpec(memory_space=pltpu.HBM)
_SEM = pl.BlockSpec(memory_space=pltpu.SEMAPHORE)
_DATAFLOW = pltpu.SideEffectType.DATAFLOW_SIDE_EFFECTING


def _exchange_refs(srcs, lands, mode, me, rel, j):
    if mode == "gather":
        return srcs[j], lands[j].at[me], lands[j].at[me ^ rel]
    return srcs[j].at[me ^ rel], lands[j].at[rel - 1], lands[j].at[rel - 1]


def _exchange_start(name, srcs, lands, mode):
    n = len(srcs)

    def body(*refs):
        ins, lnd = refs[:n], refs[n:2 * n]
        send, recv = refs[2 * n], refs[2 * n + 1]
        token = refs[-1]
        x, y, c, me = _place()
        for j in range(n):
            for rel in range(1, N_DEV):
                src, dst, _ = _exchange_refs(ins, lnd, mode, me, rel, j)
                pltpu.make_async_remote_copy(src_ref=src, dst_ref=dst, send_sem=send.at[j * (N_DEV - 1) + rel - 1],
                                             recv_sem=recv.at[j * (N_DEV - 1) + rel - 1],
                                             device_id=_peer(x, y, c, rel), device_id_type=MESH).start()
        token[...] = jnp.zeros_like(token)

    sems = pltpu.SemaphoreType.DMA((n * (N_DEV - 1),))
    hbm_like = lambda a: pltpu.HBM(a.shape, a.dtype)
    outs = _pcall(body, name=name + "_start",
                  in_specs=[_HBM] * (2 * n), out_specs=[_SEM, _SEM] + [_HBM] * (2 * n) + [pl.BlockSpec(memory_space=pltpu.VMEM)],
                  out_shape=[sems, sems] + [hbm_like(a) for a in srcs] + [hbm_like(a) for a in lands]
                  + [jax.ShapeDtypeStruct((8, 128), F32)],
                  input_output_aliases={i: 2 + i for i in range(2 * n)},
                  compiler_params=pltpu.CompilerParams(has_side_effects=_DATAFLOW))(
                      *[pltpu.with_memory_space_constraint(a, pltpu.HBM) for a in list(srcs) + list(lands)])
    return dict(name=name, mode=mode, n=n, send=outs[0], recv=outs[1], srcs=outs[2:2 + n], lands=outs[2 + n:2 + 2 * n],
                token=outs[-1][0, 0])


def _exchange_wait(ex, after):
    n, mode = ex["n"], ex["mode"]

    def body(*refs):
        ins, lnd = refs[:n], refs[n:2 * n]
        send, recv = refs[2 * n], refs[2 * n + 1]
        x, y, c, me = _place()
        for j in range(n):
            for rel in range(1, N_DEV):
                src, dst, landed = _exchange_refs(ins, lnd, mode, me, rel, j)
                pltpu.make_async_remote_copy(src_ref=src, dst_ref=dst, send_sem=send.at[j * (N_DEV - 1) + rel - 1],
                                             recv_sem=recv.at[j * (N_DEV - 1) + rel - 1],
                                             device_id=_peer(x, y, c, rel), device_id_type=MESH).wait_send()
                pltpu.make_async_remote_copy(src_ref=src, dst_ref=landed, send_sem=send.at[j * (N_DEV - 1) + rel - 1],
                                             recv_sem=recv.at[j * (N_DEV - 1) + rel - 1],
                                             device_id=_peer(x, y, c, rel), device_id_type=MESH).wait_recv()

    hbm_like = lambda a: pltpu.HBM(a.shape, a.dtype)
    arrays = list(ex["srcs"]) + list(ex["lands"])
    outs = _pcall(body, name=ex["name"] + "_wait",
                  in_specs=[_HBM] * (2 * n) + [_SEM, _SEM, pl.BlockSpec(memory_space=pl.ANY)],
                  out_specs=[_HBM] * (2 * n), out_shape=[hbm_like(a) for a in arrays],
                  input_output_aliases={i: i for i in range(2 * n)},
                  compiler_params=pltpu.CompilerParams(has_side_effects=_DATAFLOW))(
                      *arrays, ex["send"], ex["recv"], after)
    return outs[:n], outs[n:]


def _scatter_start(name, grads):
    lands = [lax.empty((N_DEV - 1,) + g.shape[1:], g.dtype) for g in grads]
    return _exchange_start(name, grads, lands, "scatter")


ROW_MIX, ROW_MLP, ROW_CB, ROW_LG, ROW_LB, ROW_QN, ROW_KN, ROW_LOSS = 0, 2, 4, 5, 6, 7, 8, 9
ROW_META, ROW_CW, ROW_GN, SMALL_ROWS = 16, 32, 64, 72


def _allreduce_small(part):
    def body(p_ref, o_ref, slots, send, recv):
        x, y, c, me = _place()
        slots[me] = p_ref[...]
        for rel in range(1, N_DEV):
            pltpu.make_async_remote_copy(src_ref=p_ref, dst_ref=slots.at[me], send_sem=send.at[rel - 1],
                                         recv_sem=recv.at[rel - 1], device_id=_peer(x, y, c, rel),
                                         device_id_type=MESH).start()
        for rel in range(1, N_DEV):
            cp = pltpu.make_async_remote_copy(src_ref=p_ref, dst_ref=slots.at[me ^ rel], send_sem=send.at[rel - 1],
                                              recv_sem=recv.at[rel - 1], device_id=_peer(x, y, c, rel),
                                              device_id_type=MESH)
            cp.wait_send()
            cp.wait_recv()
        tot = slots[0]
        for s in range(1, N_DEV):
            tot = tot + slots[s]
        o_ref[...] = tot
        for row in (ROW_QN, ROW_KN):
            v = tot[row:row + 1, :]
            f = v[:, 0:128]
            for k in range(1, 8):
                f = f + v[:, 128 * k:128 * (k + 1)]
            o_ref[row:row + 1, 0:64] = f[:, 0:64] + f[:, 64:128]

    vm = pl.BlockSpec(memory_space=pltpu.VMEM)
    return _pcall(body, name="allreduce_small", in_specs=[vm], out_specs=vm,
                  out_shape=jax.ShapeDtypeStruct(part.shape, F32),
                  scratch_shapes=[pltpu.VMEM((N_DEV,) + part.shape, F32), pltpu.SemaphoreType.DMA((N_DEV - 1,)),
                                  pltpu.SemaphoreType.DMA((N_DEV - 1,))],
                  compiler_params=_params(has_side_effects=True))(part)


def _adamw_small(w, g, m, v):
    def body(w_ref, g_ref, m_ref, v_ref, d_out, m_out, v_out):
        d, mn, vn = _adamw_math(w_ref[...], g_ref[...], m_ref[...], v_ref[...])
        d_out[...] = d
        m_out[...] = mn
        v_out[...] = vn

    osh = jax.ShapeDtypeStruct(w.shape, F32)
    return _pcall(body, name="adamw_small", out_shape=[osh, osh, osh])(w, g, m, v)


def _local_step(h0, target, p, weight, emit):
    t = h0.shape[0]
    tables = _ret_tables(t)
    bd, later_tab, earlier_tab, bias_tab = _seg_tables(_sb_qb(t))
    row = lambda a, i: a[i:i + 1]

    hn_a = _rms_fwd("rms_mix0", h0, row(p["norm_mix_g"], 0))
    w_in = weight("w_in", hn_a)
    proj = _mm_cols("proj_in", hn_a, w_in, ())
    o_ret, states = _ret_fwd(proj, tables)
    gn_flat = p["gn_g"].reshape(1, 1024)
    cat = _gn_gate_fwd(o_ret, proj, gn_flat)
    cat, hdn, ycv = _conv_fwd(cat, proj, p["conv_w"], p["conv_b"], p["ln_g"], p["ln_b"])
    w_out = weight("w_out", cat)
    h1 = _mm_rows("mix_out", cat, w_out, (), h0)
    hn_b = _rms_fwd("rms_mlp0", h1, row(p["norm_mlp_g"], 0))
    w1_0, w2_0 = weight("w1_0", hn_b), weight("w2_0", hn_b)
    a0, s0 = _mm_cols("mlp0_up", hn_b, w1_0, (), epi="relu2")
    h2 = _mm_rows("mlp0_down", s0, w2_0, (), h1)

    hn_c = _rms_fwd("rms_mix1", h2, row(p["norm_mix_g"], 1))
    w_qkv = weight("w_qkv", hn_c)
    qkv = _mm_cols("qkv", hn_c, w_qkv, ())
    qg = jnp.tile(p["qn_g"], (1, 16))
    kg = jnp.tile(p["kn_g"], (1, 16))
    qh, kt, k2, vt, v2 = _qk_norm_fwd(qkv, qg, kg, bd)
    o_sb, carries = _sb_fwd(qh, kt, v2, later_tab, bias_tab)
    w_o = weight("w_o", o_sb)
    h3 = _mm_rows("attn_out", o_sb, w_o, (), h2)
    hn_d = _rms_fwd("rms_mlp1", h3, row(p["norm_mlp_g"], 1))
    w1_1, w2_1 = weight("w1_1", hn_d), weight("w2_1", hn_d)
    a1, s1 = _mm_cols("mlp1_up", hn_d, w1_1, (), epi="relu2")
    h4 = _mm_rows("mlp1_down", s1, w2_1, (), h3)

    dh, loss = _loss_bwd(h4, target)

    def mlp_bwd(tag, layer, w1, w2, dh, h_in, hn, a, s):
        da = _mm_rows_t(f"{tag}_dact", dh, w2, (), out_dtype=BF16, epi="drelu2", extra=a)
        dw2 = _wgrad_rows(f"{tag}_dw2", s, dh, 512)
        dw1 = _wgrad_cols(f"{tag}_dw1", hn, da, 512)
        tok = emit(tag, [dw1, dw2])
        dhn = _mm_cols_t(f"{tag}_dhn", da, w1, ())
        return _rms_bwd(f"{tag}_rms_bwd", dhn, h_in, row(p["norm_mlp_g"], layer) + tok, dh)

    dh, dg_mlp1 = mlp_bwd("mlp1", 1, w1_1, w2_1, dh, h3, hn_d, a1, s1)

    do_sb = _mm_rows_t("attn_dout", dh, w_o, ())
    dw_o = _wgrad_rows("attn_dwo", o_sb, dh, 128)
    dq, dk, dv = _sb_bwd(qh, kt, k2, vt, carries, do_sb, later_tab, earlier_tab, bias_tab)
    dqkv, dqg, dkg = _qk_norm_bwd(qkv, dq, dk, dv, qg, kg, bd)
    dw_qkv = _wgrad_cols("qkv_dw", hn_c, dqkv, 384)
    tok = emit("attn", [dw_qkv, dw_o])
    dhn = _mm_cols_t("qkv_dhn", dqkv, w_qkv, ())
    dh, dg_mix1 = _rms_bwd("mix1_rms_bwd", dhn, h2, row(p["norm_mix_g"], 1) + tok, dh)

    dh, dg_mlp0 = mlp_bwd("mlp0", 0, w1_0, w2_0, dh, h1, hn_b, a0, s0)

    dcat = _mm_rows_t("mix_dcat", dh, w_out, ())
    dw_out = _wgrad_rows("mix_dwout", cat, dh, 256)
    do_ret, dgate, dgn = _gn_gate_bwd(dcat, o_ret, proj, gn_flat)
    dq_r, dk_r, dv_r = _ret_bwd(proj, states, do_ret, tables)
    dy, dlg, dlb, dcb = _conv_bwd_ln(dcat, ycv, p["ln_g"], p["ln_b"])
    dua, dug, dcw = _conv_bwd_taps(dy, hdn, proj, p["conv_w"])
    dproj = jnp.concatenate([dq_r, dk_r, dv_r, dgate, dua, dug], axis=1)
    dw_in = _wgrad_cols("proj_dw", hn_a, dproj, 640)
    tok = emit("mix0", [dw_in, dw_out])
    dhn = _mm_cols_t("proj_dhn", dproj, w_in, ())
    dh, dg_mix0 = _rms_bwd("mix0_rms_bwd", dhn, h0, row(p["norm_mix_g"], 0) + tok, dh)

    rid = lax.broadcasted_iota(jnp.int32, (16, 1), 0)
    loss_row = jnp.broadcast_to(loss[0:1, 0:1], (1, D_MODEL))
    vecs = sum(jnp.where(rid == k, v, 0.0)
               for k, v in enumerate((dg_mix0, dg_mix1, dg_mlp0, dg_mlp1, dcb, dlg, dlb, dqg, dkg, loss_row)))
    small = jnp.concatenate([vecs, dh[PAD_FRONT:TOK0], dcw, jnp.where(rid[:8] == 0, dgn, 0.0)], axis=0)
    return dh[TOK0:], small


_SMALL_NAMES = ("meta", "norm_mix_g", "norm_mlp_g", "even_ret_gn_g", "even_conv_w", "even_conv_b",
                "even_conv_ln_g", "even_conv_ln_b", "odd_q_norm_g", "odd_k_norm_g")
_BIG_NAMES = ("even_w_in", "even_w_out", "odd_w_qkv", "odd_w_o", "mlp_w1", "mlp_w2")
_ORDER = ("meta", "norm_mix_g", "norm_mlp_g", "even_w_in", "even_ret_gn_g", "even_conv_w", "even_conv_b",
          "even_conv_ln_g", "even_conv_ln_b", "even_w_out", "odd_w_qkv", "odd_q_norm_g", "odd_k_norm_g",
          "odd_w_o", "mlp_w1", "mlp_w2")


def _pack128(a):
    flat = a.reshape(-1)
    n = flat.shape[0]
    rows = -(-n // 128)
    rows8 = -(-rows // 8) * 8
    return jnp.pad(flat, (0, rows8 * 128 - n)).reshape(rows8, 128)


def kernel(x, meta, norm_mix_g, norm_mlp_g, even_w_in, even_ret_gn_g, even_conv_w, even_conv_b, even_conv_ln_g, even_conv_ln_b, even_w_out, odd_w_qkv, odd_q_norm_g, odd_k_norm_g, odd_w_o, mlp_w1, mlp_w2, loss_target, m_meta, m_norm_mix_g, m_norm_mlp_g, m_even_w_in, m_even_ret_gn_g, m_even_conv_w, m_even_conv_b, m_even_conv_ln_g, m_even_conv_ln_b, m_even_w_out, m_odd_w_qkv, m_odd_q_norm_g, m_odd_k_norm_g, m_odd_w_o, m_mlp_w1, m_mlp_w2, v_meta, v_norm_mix_g, v_norm_mlp_g, v_even_w_in, v_even_ret_gn_g, v_even_conv_w, v_even_conv_b, v_even_conv_ln_g, v_even_conv_ln_b, v_even_w_out, v_odd_w_qkv, v_odd_q_norm_g, v_odd_k_norm_g, v_odd_w_o, v_mlp_w1, v_mlp_w2):
    w = dict(meta=meta, norm_mix_g=norm_mix_g, norm_mlp_g=norm_mlp_g, even_w_in=even_w_in,
             even_ret_gn_g=even_ret_gn_g, even_conv_w=even_conv_w, even_conv_b=even_conv_b,
             even_conv_ln_g=even_conv_ln_g, even_conv_ln_b=even_conv_ln_b, even_w_out=even_w_out,
             odd_w_qkv=odd_w_qkv, odd_q_norm_g=odd_q_norm_g, odd_k_norm_g=odd_k_norm_g, odd_w_o=odd_w_o,
             mlp_w1=mlp_w1, mlp_w2=mlp_w2)
    mom = dict(meta=m_meta, norm_mix_g=m_norm_mix_g, norm_mlp_g=m_norm_mlp_g, even_w_in=m_even_w_in,
               even_ret_gn_g=m_even_ret_gn_g, even_conv_w=m_even_conv_w, even_conv_b=m_even_conv_b,
               even_conv_ln_g=m_even_conv_ln_g, even_conv_ln_b=m_even_conv_ln_b, even_w_out=m_even_w_out,
               odd_w_qkv=m_odd_w_qkv, odd_q_norm_g=m_odd_q_norm_g, odd_k_norm_g=m_odd_k_norm_g, odd_w_o=m_odd_w_o,
               mlp_w1=m_mlp_w1, mlp_w2=m_mlp_w2)
    var = dict(meta=v_meta, norm_mix_g=v_norm_mix_g, norm_mlp_g=v_norm_mlp_g, even_w_in=v_even_w_in,
               even_ret_gn_g=v_even_ret_gn_g, even_conv_w=v_even_conv_w, even_conv_b=v_even_conv_b,
               even_conv_ln_g=v_even_conv_ln_g, even_conv_ln_b=v_even_conv_ln_b, even_w_out=v_even_w_out,
               odd_w_qkv=v_odd_w_qkv, odd_q_norm_g=v_odd_q_norm_g, odd_k_norm_g=v_odd_k_norm_g, odd_w_o=v_odd_w_o,
               mlp_w1=v_mlp_w1, mlp_w2=v_mlp_w2)
    me = 4 * lax.axis_index("x") + 2 * lax.axis_index("y") + lax.axis_index("c")

    small_in = jnp.concatenate([meta, jnp.pad(even_conv_w[0], ((0, 1), (0, 0))),
                                jnp.pad(even_ret_gn_g[0], ((0, 4), (0, 96)))], axis=0)
    b16 = lambda a: a.astype(BF16)
    later_src = dict(w_out=b16(even_w_out[0]), w1_0=b16(mlp_w1[0]), w2_0=b16(mlp_w2[0]),
                     w_qkv=b16(odd_w_qkv[0]), w_o=b16(odd_w_o[0]), w1_1=b16(mlp_w1[1]), w2_1=b16(mlp_w2[1]))
    landed = _gather_first([b16(even_w_in[0]), small_in], list(later_src.values()))
    g_in, g_small = landed[0], landed[1]
    own_slot = dict(zip(later_src, landed[2:]))
    groups = (("gather_l0", ("w_out", "w1_0", "w2_0")), ("gather_attn", ("w_qkv", "w_o")),
              ("gather_l1", ("w1_1", "w2_1")))
    pending = {}
    gather_tok = jnp.zeros((), F32)
    for gname, names in groups:
        ex = _exchange_start(gname, [later_src[n] for n in names], [own_slot[n] for n in names], "gather")
        gather_tok = gather_tok + ex["token"]
        for n in names:
            pending[n] = (ex, names)
    arrived = dict(w_in=g_in)

    def weight(name, after):
        if name not in arrived:
            ex, names = pending[name]
            arrived.update(zip(names, _exchange_wait(ex, after)[1]))
        return arrived[name]

    cols = lambda a: jnp.transpose(a, (1, 0, 2)).reshape(a.shape[1], -1)
    p = dict(norm_mix_g=norm_mix_g + gather_tok, norm_mlp_g=norm_mlp_g, conv_b=even_conv_b, ln_g=even_conv_ln_g,
             ln_b=even_conv_ln_b, qn_g=odd_q_norm_g, kn_g=odd_k_norm_g,
             gn_g=cols(g_small[:, 48:52, :32]),
             conv_w=jnp.pad(cols(g_small[:, 16:47]), ((0, 1), (0, 0))))
    meta_full = cols(g_small[:, 0:16])

    scatters = {}

    def emit(tag, grads):
        scatters[tag] = _scatter_start("scatter_" + tag, grads)
        return scatters[tag]["token"]

    h0 = jnp.concatenate([jnp.zeros((PAD_FRONT, D_MODEL), F32), meta_full, x[0]], axis=0)
    grad_x, small_part = _local_step(h0, loss_target[0], p, weight, emit)
    tot = _allreduce_small(small_part)
    loss = tot[ROW_LOSS, 0]

    got = {tag: _exchange_wait(ex, tot) for tag, ex in scatters.items()}
    pick = lambda tag, j: (got[tag][0][j], got[tag][1][j])
    terms = dict(even_w_in=[pick("mix0", 0)], even_w_out=[pick("mix0", 1)], odd_w_qkv=[pick("attn", 0)],
                 odd_w_o=[pick("attn", 1)], mlp_w1=[pick("mlp0", 0), pick("mlp1", 0)],
                 mlp_w2=[pick("mlp0", 1), pick("mlp1", 1)])
    out = {}
    for name in _BIG_NAMES:
        owns, recvs = zip(*terms[name])
        out[name] = _adamw("adamw_" + name, w[name], list(owns), list(recvs), mom[name], var[name], me)

    shard_cols = lambda a, width: lax.dynamic_slice_in_dim(a, me * width, width, axis=1)
    one = lambda r: tot[r:r + 1]
    small_g = dict(
        norm_mix_g=tot[ROW_MIX:ROW_MIX + 2], norm_mlp_g=tot[ROW_MLP:ROW_MLP + 2],
        even_conv_b=one(ROW_CB), even_conv_ln_g=one(ROW_LG), even_conv_ln_b=one(ROW_LB),
        odd_q_norm_g=one(ROW_QN)[:, :64], odd_k_norm_g=one(ROW_KN)[:, :64],
        meta=shard_cols(tot[ROW_META:ROW_META + N_META], 128),
        even_conv_w=shard_cols(tot[ROW_CW:ROW_CW + CONV_WIDTH], 128)[None],
        even_ret_gn_g=shard_cols(tot[ROW_GN].reshape(4, 256), 32)[None])
    packs = {n: (_pack128(w[n]), _pack128(small_g[n]), _pack128(mom[n]), _pack128(var[n])) for n in _SMALL_NAMES}
    cat4 = [jnp.concatenate([packs[n][i] for n in _SMALL_NAMES], axis=0) for i in range(4)]
    d_s, m_s, v_s = _adamw_small(*cat4)
    r0 = 0
    for n in _SMALL_NAMES:
        rows = packs[n][0].shape[0]
        size = w[n].size
        take = lambda a: a[r0:r0 + rows].reshape(-1)[:size].reshape(w[n].shape)
        out[n] = (small_g[n].reshape(w[n].shape), take(d_s), take(m_s), take(v_s))
        r0 += rows

    res = [loss, grad_x[None]]
    for i in range(4):
        res.extend(out[n][i] for n in _ORDER)
    return tuple(res)
```

```python
import functools

import numpy as np
import jax
import jax.numpy as jnp
from jax import lax
from jax.experimental import pallas as pl
from jax.experimental.pallas import tpu as pltpu

F32 = jnp.float32
BF16 = jnp.bfloat16

D_MODEL = 1024
N_META = 16
CHUNK = 128
PAD_FRONT = 112
TOK0 = PAD_FRONT + N_META
EPS = 1e-6
N_DEV = 8
RET_HEADS = 4
RET_DECAY_OFFSET = 5.0
ROPE_BASE = 10000.0
CONV_WIDTH = 31
HALO = 32
SB_SCALE = 64 ** -0.5
RET_SCALE = 128 ** -0.5
ADAM_LR, ADAM_B1, ADAM_B2, ADAM_EPS, ADAM_WD, ADAM_STEP = 0.001, 0.9, 0.999, 1e-08, 0.01, 10
VMEM_LIMIT = 56 * 1024 * 1024
MESH = pl.DeviceIdType.MESH


def _pcall(body, **kw):
    return pl.pallas_call(body, **kw)


def _params(**kw):
    return pltpu.CompilerParams(vmem_limit_bytes=VMEM_LIMIT, **kw)


def _tile(n, cands):
    for c in cands:
        if n % c == 0:
            return c
    raise ValueError(f"no tile for {n} in {cands}")


def _sigmoid(x):
    return 1.0 / (1.0 + jnp.exp(-x))


_DIMS = {
    "nn": (((1,), (0,)), ((), ())),
    "nt": (((1,), (1,)), ((), ())),
    "tn": (((0,), (0,)), ((), ())),
}


def _matmul(name, a, b, *, grid, a_spec, b_spec, o_spec, out_shape, contract, acc_shape,
            epi="plain", extra=None, extra_spec=None):
    nk = grid[2]
    dims = _DIMS[contract]
    n_in = 3 if extra is not None else 2
    n_out = 2 if epi == "relu2" else 1

    def body(*refs):
        a_ref, b_ref = refs[0], refs[1]
        e_ref = refs[2] if extra is not None else None
        outs = refs[n_in:n_in + n_out]
        acc = refs[-1]
        k = pl.program_id(2)
        part = lax.dot_general(a_ref[...].astype(BF16), b_ref[...].astype(BF16), dims, preferred_element_type=F32)
        if nk > 1:
            @pl.when(k == 0)
            def _():
                acc[...] = jnp.zeros_like(acc)

            acc[...] += part

        @pl.when(k == nk - 1)
        def _():
            r = acc[...] if nk > 1 else part
            if epi == "plain":
                outs[0][...] = r.astype(outs[0].dtype)
            elif epi == "residual":
                outs[0][...] = (r + e_ref[...]).astype(outs[0].dtype)
            elif epi == "relu2":
                outs[0][...] = r
                rr = jnp.maximum(r, 0.0)
                outs[1][...] = (rr * rr).astype(BF16)
            elif epi == "drelu2":
                outs[0][...] = (r * (2.0 * jnp.maximum(e_ref[...], 0.0))).astype(outs[0].dtype)

    in_specs = [a_spec, b_spec] + ([extra_spec] if extra is not None else [])
    args = (a, b) + ((extra,) if extra is not None else ())
    if n_out == 2:
        out_specs = [o_spec, o_spec]
    else:
        out_specs = o_spec
    return _pcall(body, name=name, grid=grid, in_specs=in_specs, out_specs=out_specs,
                  out_shape=out_shape, scratch_shapes=[pltpu.VMEM(acc_shape, F32)],
                  compiler_params=_params(dimension_semantics=("parallel", "parallel", "arbitrary")))(*args)


def _tm(t):
    return _tile(t, (1408, 768, 384, 128))


def _mm_cols(name, a, wb, lead, out_dtype=F32, epi="plain"):
    t, kdim = a.shape
    n = wb.shape[-1]
    tm, tk = _tm(t), _tile(kdim, (1024, 512))
    nl = len(lead)
    b_spec = pl.BlockSpec((None,) * (1 + nl) + (tk, n), lambda i, j, k: (j,) + lead + (k, 0))
    o_spec = pl.BlockSpec((tm, n), lambda i, j, k: (i, j))
    if epi == "relu2":
        out_shape = [jax.ShapeDtypeStruct((t, N_DEV * n), F32), jax.ShapeDtypeStruct((t, N_DEV * n), BF16)]
    else:
        out_shape = jax.ShapeDtypeStruct((t, N_DEV * n), out_dtype)
    return _matmul(name, a, wb, grid=(t // tm, N_DEV, kdim // tk),
                   a_spec=pl.BlockSpec((tm, tk), lambda i, j, k: (i, k)), b_spec=b_spec, o_spec=o_spec,
                   out_shape=out_shape, contract="nn", acc_shape=(tm, n), epi=epi)


def _tm_deep(t, kdim):
    return _tm(t) if kdim <= 2048 else _tile(t, (704, 384, 128))


def _mm_cols_t(name, a, wb):
    t = a.shape[0]
    nb, kdim, n = wb.shape
    tm, tn = _tm_deep(t, nb * n), _tile(kdim, (512,))

    def body(a_ref, b_ref, o_ref):
        acc = _dot(a_ref[:, 0:n].astype(BF16), b_ref[0], "nt")
        for j in range(1, nb):
            acc = acc + _dot(a_ref[:, j * n:(j + 1) * n].astype(BF16), b_ref[j], "nt")
        o_ref[...] = acc

    return _pcall(body, name=name, grid=(t // tm, kdim // tn),
                  in_specs=[pl.BlockSpec((tm, nb * n), lambda i, j: (i, 0)),
                            pl.BlockSpec((nb, tn, n), lambda i, j: (0, j, 0))],
                  out_specs=pl.BlockSpec((tm, tn), lambda i, j: (i, j)),
                  out_shape=jax.ShapeDtypeStruct((t, kdim), F32),
                  compiler_params=_params(dimension_semantics=("parallel", "parallel")))(a, wb)


def _mm_rows(name, a, wb, residual):
    t = a.shape[0]
    nb, r, n = wb.shape
    tm, tn = _tm_deep(t, nb * r), _tile(n, (512,))

    def body(a_ref, b_ref, r_ref, o_ref):
        o_ref[...] = r_ref[...] + _dot(a_ref[...].astype(BF16), b_ref[...].reshape(nb * r, tn))

    o_spec = pl.BlockSpec((tm, tn), lambda i, j: (i, j))
    return _pcall(body, name=name, grid=(t // tm, n // tn),
                  in_specs=[pl.BlockSpec((tm, nb * r), lambda i, j: (i, 0)),
                            pl.BlockSpec((nb, r, tn), lambda i, j: (0, 0, j)), o_spec],
                  out_specs=o_spec, out_shape=jax.ShapeDtypeStruct((t, n), F32),
                  compiler_params=_params(dimension_semantics=("parallel", "parallel")))(a, wb, residual)


def _mm_rows_t(name, a, wb, lead, out_dtype=F32, epi="plain", extra=None):
    t, n = a.shape
    r = wb.shape[-2]
    tm, tk = _tm(t), _tile(n, (1024,))
    nl = len(lead)
    b_spec = pl.BlockSpec((None,) * (1 + nl) + (r, tk), lambda i, j, k: (j,) + lead + (0, k))
    o_spec = pl.BlockSpec((tm, r), lambda i, j, k: (i, j))
    return _matmul(name, a, wb, grid=(t // tm, N_DEV, n // tk),
                   a_spec=pl.BlockSpec((tm, tk), lambda i, j, k: (i, k)), b_spec=b_spec, o_spec=o_spec,
                   out_shape=jax.ShapeDtypeStruct((t, N_DEV * r), out_dtype), contract="nt",
                   acc_shape=(tm, r), epi=epi, extra=extra, extra_spec=o_spec if extra is not None else None)


def _wgrad_cols(name, x, dy, n):
    t, kdim = x.shape
    tk = _tm(t)
    return _matmul(name, x, dy, grid=(1, N_DEV, t // tk),
                   a_spec=pl.BlockSpec((tk, kdim), lambda i, j, k: (k, 0)),
                   b_spec=pl.BlockSpec((tk, n), lambda i, j, k: (k, j)),
                   o_spec=pl.BlockSpec((None, kdim, n), lambda i, j, k: (j, 0, 0)),
                   out_shape=jax.ShapeDtypeStruct((N_DEV, kdim, n), BF16), contract="tn", acc_shape=(kdim, n))


def _wgrad_rows(name, x, dy, r):
    t = x.shape[0]
    n = dy.shape[1]
    tk, tn = _tm(t), _tile(n, (512,))
    return _matmul(name, x, dy, grid=(N_DEV, n // tn, t // tk),
                   a_spec=pl.BlockSpec((tk, r), lambda i, j, k: (k, i)),
                   b_spec=pl.BlockSpec((tk, tn), lambda i, j, k: (k, j)),
                   o_spec=pl.BlockSpec((None, r, tn), lambda i, j, k: (i, 0, j)),
                   out_shape=jax.ShapeDtypeStruct((N_DEV, r, n), BF16), contract="tn", acc_shape=(r, tn))


def _rows(t):
    return _tile(t, (384, 128))


def _rms_fwd(name, h, g):
    t = h.shape[0]
    tr = _rows(t)

    def body(h_ref, g_ref, o_ref):
        x = h_ref[...]
        r = lax.rsqrt(jnp.mean(x * x, axis=-1, keepdims=True) + EPS)
        o_ref[...] = (x * r * g_ref[...]).astype(BF16)

    row = pl.BlockSpec((tr, D_MODEL), lambda i: (i, 0))
    vec = pl.BlockSpec((1, D_MODEL), lambda i: (0, 0))
    return _pcall(body, name=name, grid=(t // tr,), in_specs=[row, vec], out_specs=row,
                  out_shape=jax.ShapeDtypeStruct((t, D_MODEL), BF16))(h, g)


def _rms_bwd(name, dhn, h, g, dres):
    t = h.shape[0]
    tr = _rows(t)

    def body(d_ref, h_ref, g_ref, r_ref, o_ref, dg_ref):
        @pl.when(pl.program_id(0) == 0)
        def _():
            dg_ref[...] = jnp.zeros_like(dg_ref)

        x = h_ref[...]
        d = d_ref[...]
        r = lax.rsqrt(jnp.mean(x * x, axis=-1, keepdims=True) + EPS)
        u = d * g_ref[...]
        m = jnp.mean(u * x, axis=-1, keepdims=True)
        o_ref[...] = r_ref[...] + r * u - x * (r * r * r * m)
        dg_ref[...] += jnp.sum(d * x * r, axis=0, keepdims=True)

    row = pl.BlockSpec((tr, D_MODEL), lambda i: (i, 0))
    vec = pl.BlockSpec((1, D_MODEL), lambda i: (0, 0))
    return _pcall(body, name=name, grid=(t // tr,), in_specs=[row, row, vec, row], out_specs=[row, vec],
                  out_shape=[jax.ShapeDtypeStruct((t, D_MODEL), F32), jax.ShapeDtypeStruct((1, D_MODEL), F32)])(
                      dhn, h, g, dres)


def _loss_bwd(h, target):
    t = h.shape[0]
    nb = t // CHUNK

    def body(h_ref, t_ref, d_ref, l_ref):
        i = pl.program_id(0)

        @pl.when(i == 0)
        def _():
            d_ref[...] = jnp.zeros_like(d_ref)
            l_ref[...] = jnp.zeros_like(l_ref)

        @pl.when(i > 0)
        def _():
            diff = h_ref[...] - t_ref[...]
            d_ref[...] = diff * (1.0 / D_MODEL)
            l_ref[...] += jnp.sum(diff * diff) * (0.5 / D_MODEL)

    return _pcall(body, name="loss_bwd", grid=(nb,),
                  in_specs=[pl.BlockSpec((CHUNK, D_MODEL), lambda i: (i, 0)),
                            pl.BlockSpec((CHUNK, D_MODEL), lambda i: (jnp.maximum(i - 1, 0), 0))],
                  out_specs=[pl.BlockSpec((CHUNK, D_MODEL), lambda i: (i, 0)),
                             pl.BlockSpec((8, 128), lambda i: (0, 0))],
                  out_shape=[jax.ShapeDtypeStruct((t, D_MODEL), F32), jax.ShapeDtypeStruct((8, 128), F32)])(h, target)


def _ret_tables(t):
    hh = np.arange(RET_HEADS, dtype=np.float64)
    log_g = np.log1p(-np.exp2(-RET_DECAY_OFFSET - hh))
    idx = np.arange(CHUNK, dtype=np.float64)
    diff = idx[:, None] - idx[None, :]
    dmat = np.where(diff[None] >= 0, np.exp(np.maximum(diff, 0.0)[None] * log_g[:, None, None]), 0.0)
    qdec = np.exp((idx + 1.0)[None, :, None] * log_g[:, None, None]) * np.ones((1, 1, CHUNK))
    kdec = np.exp((CHUNK - 1 - idx)[None, :, None] * log_g[:, None, None]) * np.ones((1, 1, CHUNK))
    half = CHUNK // 2
    inv_freq = (ROPE_BASE ** (-np.arange(half, dtype=np.float32) / half)).astype(np.float32)
    ang = (np.arange(t, dtype=np.float32)[:, None] * inv_freq[None, :]).astype(np.float32).astype(np.float64)
    cos2 = np.concatenate([np.cos(ang), np.cos(ang)], axis=1)
    sin2 = np.concatenate([-np.sin(ang), np.sin(ang)], axis=1)
    return tuple(jnp.asarray(v, F32) for v in (dmat, qdec, kdec, cos2, sin2))


def _rot(x, c, s):
    return x * c + pltpu.roll(x, CHUNK // 2, 1) * s


def _unrot(dx, c, s):
    return dx * c + pltpu.roll(dx * s, CHUNK // 2, 1)


def _dot(a, b, contract="nn"):
    return lax.dot_general(a, b, _DIMS[contract], preferred_element_type=F32)


def _ret_fwd(proj, tables):
    t = proj.shape[0]
    nch = t // CHUNK
    dmat, qdec, kdec, cos2, sin2 = tables

    def body(q_ref, k_ref, v_ref, c_ref, s_ref, dm_ref, qd_ref, kd_ref, o_ref, st_ref, state):
        @pl.when(pl.program_id(1) == 0)
        def _():
            state[...] = jnp.zeros_like(state)

        c, s = c_ref[...], s_ref[...]
        q = _rot(q_ref[...], c, s)
        k = _rot(k_ref[...], c, s) * RET_SCALE
        vb = v_ref[...].astype(BF16)
        st = state[...]
        st_ref[...] = st
        sc = _dot(q.astype(BF16), k.astype(BF16), "nt") * dm_ref[...]
        o = _dot(sc.astype(BF16), vb)
        o += _dot((q * qd_ref[...]).astype(BF16), st.astype(BF16))
        o_ref[...] = o
        kv = _dot((k * kd_ref[...]).astype(BF16), vb, "tn")
        state[...] = qd_ref[CHUNK - 1:CHUNK, 0:1] * st + kv

    hd = lambda h, n: (h, 0, 0)
    tab = pl.BlockSpec((None, CHUNK, CHUNK), hd)
    pos = pl.BlockSpec((CHUNK, CHUNK), lambda h, n: (n, 0))
    return _pcall(
        body, name="ret_fwd", grid=(RET_HEADS, nch),
        in_specs=[pl.BlockSpec((CHUNK, 128), lambda h, n: (n, h)),
                  pl.BlockSpec((CHUNK, 128), lambda h, n: (n, RET_HEADS + h)),
                  pl.BlockSpec((CHUNK, 256), lambda h, n: (n, RET_HEADS + h)),
                  pos, pos, tab, tab, tab],
        out_specs=[pl.BlockSpec((CHUNK, 256), lambda h, n: (n, h)),
                   pl.BlockSpec((None, None, 128, 256), lambda h, n: (h, n, 0, 0))],
        out_shape=[jax.ShapeDtypeStruct((t, 1024), F32), jax.ShapeDtypeStruct((RET_HEADS, nch, 128, 256), F32)],
        scratch_shapes=[pltpu.VMEM((128, 256), F32)],
        compiler_params=_params(dimension_semantics=("parallel", "arbitrary")))(
            proj, proj, proj, cos2, sin2, dmat, qdec, kdec)


def _ret_bwd(proj, states, do, tables):
    t = proj.shape[0]
    nch = t // CHUNK
    dmat, qdec, kdec, cos2, sin2 = tables

    def body(q_ref, k_ref, v_ref, do_ref, st_ref, c_ref, s_ref, dm_ref, qd_ref, kd_ref,
             dq_ref, dk_ref, dv_ref, rst):
        @pl.when(pl.program_id(1) == 0)
        def _():
            rst[...] = jnp.zeros_like(rst)

        c, s = c_ref[...], s_ref[...]
        q = _rot(q_ref[...], c, s)
        k = _rot(k_ref[...], c, s) * RET_SCALE
        qb, kb = q.astype(BF16), k.astype(BF16)
        vb = v_ref[...].astype(BF16)
        dob = do_ref[...].astype(BF16)
        pb = st_ref[...].astype(BF16)
        r = rst[...]
        rb = r.astype(BF16)
        dm, qd, kd = dm_ref[...], qd_ref[...], kd_ref[...]
        sb = (_dot(qb, kb, "nt") * dm).astype(BF16)
        dsb = (_dot(dob, vb, "nt") * dm).astype(BF16)
        dq = _dot(dsb, kb) + _dot(dob, pb, "nt") * qd
        dk = _dot(dsb, qb, "tn") + _dot(vb, rb, "nt") * kd
        dv = _dot(sb, dob, "tn") + _dot((k * kd).astype(BF16), rb)
        rst[...] = _dot((q * qd).astype(BF16), dob, "tn") + qd[CHUNK - 1:CHUNK, 0:1] * r
        dq_ref[...] = _unrot(dq, c, s).astype(BF16)
        dk_ref[...] = (_unrot(dk, c, s) * RET_SCALE).astype(BF16)
        dv_ref[...] = dv.astype(BF16)

    rev = lambda n: nch - 1 - n
    tab = pl.BlockSpec((None, CHUNK, CHUNK), lambda h, n: (h, 0, 0))
    pos = pl.BlockSpec((CHUNK, CHUNK), lambda h, n: (rev(n), 0))
    return _pcall(
        body, name="ret_bwd", grid=(RET_HEADS, nch),
        in_specs=[pl.BlockSpec((CHUNK, 128), lambda h, n: (rev(n), h)),
                  pl.BlockSpec((CHUNK, 128), lambda h, n: (rev(n), RET_HEADS + h)),
                  pl.BlockSpec((CHUNK, 256), lambda h, n: (rev(n), RET_HEADS + h)),
                  pl.BlockSpec((CHUNK, 256), lambda h, n: (rev(n), h)),
                  pl.BlockSpec((None, None, 128, 256), lambda h, n: (h, rev(n), 0, 0)),
                  pos, pos, tab, tab, tab],
        out_specs=[pl.BlockSpec((CHUNK, 128), lambda h, n: (rev(n), h)),
                   pl.BlockSpec((CHUNK, 128), lambda h, n: (rev(n), h)),
                   pl.BlockSpec((CHUNK, 256), lambda h, n: (rev(n), h))],
        out_shape=[jax.ShapeDtypeStruct((t, 512), BF16), jax.ShapeDtypeStruct((t, 512), BF16),
                   jax.ShapeDtypeStruct((t, 1024), BF16)],
        scratch_shapes=[pltpu.VMEM((128, 256), F32)],
        compiler_params=_params(dimension_semantics=("parallel", "arbitrary")))(
            proj, proj, proj, do, states, cos2, sin2, dmat, qdec, kdec)


def _gn_gate_fwd(o, proj, gn_g):
    t = o.shape[0]
    tr = _rows(t)

    def body(o_ref, g_ref, w_ref, c_ref):
        for h in range(RET_HEADS):
            sl = slice(256 * h, 256 * (h + 1))
            x = o_ref[:, sl]
            mu = jnp.mean(x, axis=-1, keepdims=True)
            xc = x - mu
            rstd = lax.rsqrt(jnp.mean(xc * xc, axis=-1, keepdims=True) + EPS)
            g = g_ref[:, sl]
            c_ref[:, sl] = (g * _sigmoid(g) * (xc * rstd * w_ref[:, sl])).astype(BF16)

    return _pcall(body, name="gn_gate_fwd", grid=(t // tr,),
                  in_specs=[pl.BlockSpec((tr, 1024), lambda i: (i, 0)),
                            pl.BlockSpec((tr, 1024), lambda i: (i, 2)),
                            pl.BlockSpec((1, 1024), lambda i: (0, 0))],
                  out_specs=pl.BlockSpec((tr, 1024), lambda i: (i, 0)),
                  out_shape=jax.ShapeDtypeStruct((t, 2048), BF16))(o, proj, gn_g)


def _gn_gate_bwd(dcat, o, proj, gn_g):
    t = o.shape[0]
    tr = _rows(t)

    def body(d_ref, o_ref, g_ref, w_ref, do_ref, dg_ref, dw_ref):
        @pl.when(pl.program_id(0) == 0)
        def _():
            dw_ref[...] = jnp.zeros_like(dw_ref)

        for h in range(RET_HEADS):
            sl = slice(256 * h, 256 * (h + 1))
            x = o_ref[:, sl]
            mu = jnp.mean(x, axis=-1, keepdims=True)
            xc = x - mu
            rstd = lax.rsqrt(jnp.mean(xc * xc, axis=-1, keepdims=True) + EPS)
            xh = xc * rstd
            w = w_ref[:, sl]
            g = g_ref[:, sl]
            sg = _sigmoid(g)
            d = d_ref[:, sl]
            don = d * (g * sg)
            dg_ref[:, sl] = (d * (xh * w) * (sg * (1.0 + g * (1.0 - sg)))).astype(BF16)
            dw_ref[:, sl] += jnp.sum(don * xh, axis=0, keepdims=True)
            dxh = don * w
            m1 = jnp.mean(dxh, axis=-1, keepdims=True)
            m2 = jnp.mean(dxh * xh, axis=-1, keepdims=True)
            do_ref[:, sl] = rstd * (dxh - m1 - xh * m2)

    row = pl.BlockSpec((tr, 1024), lambda i: (i, 0))
    vec = pl.BlockSpec((1, 1024), lambda i: (0, 0))
    return _pcall(body, name="gn_gate_bwd", grid=(t // tr,),
                  in_specs=[row, row, pl.BlockSpec((tr, 1024), lambda i: (i, 2)), vec],
                  out_specs=[row, row, vec],
                  out_shape=[jax.ShapeDtypeStruct((t, 1024), F32), jax.ShapeDtypeStruct((t, 1024), BF16),
                             jax.ShapeDtypeStruct((1, 1024), F32)])(dcat, o, proj, gn_g)


def _row_ids(i, tr):
    return i * tr + lax.broadcasted_iota(jnp.int32, (tr, 1), 0)


def _conv_fwd(cat, proj, conv_w, conv_b, ln_g, ln_b):
    t = proj.shape[0]
    tr = _rows(t)
    hb = tr // HALO

    def body(cat_in, ua_ref, ug_ref, pa_ref, pg_ref, w_ref, b_ref, lg_ref, lb_ref, c_ref, hd_ref, y_ref, xs):
        del cat_in
        i = pl.program_id(0)
        hdn = ua_ref[...] * _sigmoid(ug_ref[...])
        hd_ref[...] = hdn
        prev = pa_ref[...] * _sigmoid(pg_ref[...])
        xs[0:HALO, :] = jnp.where(i > 0, prev, 0.0)
        xs[HALO:HALO + tr, :] = hdn
        acc = jnp.zeros((tr, 1024), F32) + b_ref[...]
        for w in range(CONV_WIDTH):
            acc += w_ref[w:w + 1, :] * xs[pl.ds(HALO - (CONV_WIDTH - 1) + w, tr), :]
        y_ref[...] = acc
        mu = jnp.mean(acc, axis=-1, keepdims=True)
        yc = acc - mu
        rstd = lax.rsqrt(jnp.mean(yc * yc, axis=-1, keepdims=True) + EPS)
        yn = yc * rstd * lg_ref[...] + lb_ref[...]
        c = yn * _sigmoid(yn)
        c_ref[...] = jnp.where(_row_ids(i, tr) >= PAD_FRONT, c, 0.0).astype(BF16)

    row = pl.BlockSpec((tr, 1024), lambda i: (i, 0))
    vec = pl.BlockSpec((1, 1024), lambda i: (0, 0))
    halo = lambda col: pl.BlockSpec((HALO, 1024), lambda i: (jnp.maximum(i * hb - 1, 0), col))
    return _pcall(body, name="conv_fwd", grid=(t // tr,),
                  in_specs=[pl.BlockSpec(memory_space=pl.ANY),
                            pl.BlockSpec((tr, 1024), lambda i: (i, 3)), pl.BlockSpec((tr, 1024), lambda i: (i, 4)),
                            halo(3), halo(4), pl.BlockSpec((32, 1024), lambda i: (0, 0)), vec, vec, vec],
                  out_specs=[pl.BlockSpec((tr, 1024), lambda i: (i, 1)), row, row],
                  out_shape=[jax.ShapeDtypeStruct((t, 2048), BF16), jax.ShapeDtypeStruct((t, 1024), F32),
                             jax.ShapeDtypeStruct((t, 1024), F32)],
                  scratch_shapes=[pltpu.VMEM((tr + HALO, 1024), F32)],
                  input_output_aliases={0: 0})(cat, proj, proj, proj, proj, conv_w, conv_b, ln_g, ln_b)


def _conv_bwd_ln(dcat, y, ln_g, ln_b):
    t = y.shape[0]
    tr = _rows(t)

    def body(d_ref, y_ref, lg_ref, lb_ref, dy_ref, dlg_ref, dlb_ref, dcb_ref):
        i = pl.program_id(0)

        @pl.when(i == 0)
        def _():
            dlg_ref[...] = jnp.zeros_like(dlg_ref)
            dlb_ref[...] = jnp.zeros_like(dlb_ref)
            dcb_ref[...] = jnp.zeros_like(dcb_ref)

        y = y_ref[...]
        mu = jnp.mean(y, axis=-1, keepdims=True)
        yc = y - mu
        rstd = lax.rsqrt(jnp.mean(yc * yc, axis=-1, keepdims=True) + EPS)
        xh = yc * rstd
        lg = lg_ref[...]
        yn = xh * lg + lb_ref[...]
        sg = _sigmoid(yn)
        dyn = jnp.where(_row_ids(i, tr) >= PAD_FRONT, d_ref[...] * (sg * (1.0 + yn * (1.0 - sg))), 0.0)
        dlg_ref[...] += jnp.sum(dyn * xh, axis=0, keepdims=True)
        dlb_ref[...] += jnp.sum(dyn, axis=0, keepdims=True)
        dxh = dyn * lg
        m1 = jnp.mean(dxh, axis=-1, keepdims=True)
        m2 = jnp.mean(dxh * xh, axis=-1, keepdims=True)
        dy = rstd * (dxh - m1 - xh * m2)
        dy_ref[...] = dy
        dcb_ref[...] += jnp.sum(dy, axis=0, keepdims=True)

    row = pl.BlockSpec((tr, 1024), lambda i: (i, 0))
    vec = pl.BlockSpec((1, 1024), lambda i: (0, 0))
    vshape = jax.ShapeDtypeStruct((1, 1024), F32)
    return _pcall(body, name="conv_bwd_ln", grid=(t // tr,),
                  in_specs=[pl.BlockSpec((tr, 1024), lambda i: (i, 1)), row, vec, vec],
                  out_specs=[row, vec, vec, vec],
                  out_shape=[jax.ShapeDtypeStruct((t, 1024), F32), vshape, vshape, vshape])(dcat, y, ln_g, ln_b)


def _conv_bwd_taps(dy, hdn, proj, conv_w):
    t = dy.shape[0]
    tr = _rows(t)
    hb = tr // HALO
    nt = t // tr

    def body(dy_ref, nx_ref, hd_ref, ph_ref, ua_ref, ug_ref, w_ref, da_ref, dg_ref, dw_ref, ys, xs):
        i = pl.program_id(0)

        @pl.when(i == 0)
        def _():
            dw_ref[...] = jnp.zeros_like(dw_ref)

        dy = dy_ref[...]
        ys[0:tr, :] = dy
        ys[tr:tr + HALO, :] = jnp.where(i < nt - 1, nx_ref[...], 0.0)
        xs[0:HALO, :] = jnp.where(i > 0, ph_ref[...], 0.0)
        xs[HALO:HALO + tr, :] = hd_ref[...]
        dh = jnp.zeros((tr, 1024), F32)
        for w in range(CONV_WIDTH):
            dh += w_ref[w:w + 1, :] * ys[pl.ds(CONV_WIDTH - 1 - w, tr), :]
            dw_ref[w:w + 1, :] += jnp.sum(dy * xs[pl.ds(HALO - (CONV_WIDTH - 1) + w, tr), :], axis=0, keepdims=True)
        dh = jnp.where(_row_ids(i, tr) >= PAD_FRONT, dh, 0.0)
        sg = _sigmoid(ug_ref[...])
        da_ref[...] = (dh * sg).astype(BF16)
        dg_ref[...] = (dh * ua_ref[...] * sg * (1.0 - sg)).astype(BF16)

    row = pl.BlockSpec((tr, 1024), lambda i: (i, 0))
    return _pcall(body, name="conv_bwd_taps", grid=(nt,),
                  in_specs=[row, pl.BlockSpec((HALO, 1024), lambda i: (jnp.minimum((i + 1) * hb, nt * hb - 1), 0)),
                            row, pl.BlockSpec((HALO, 1024), lambda i: (jnp.maximum(i * hb - 1, 0), 0)),
                            pl.BlockSpec((tr, 1024), lambda i: (i, 3)), pl.BlockSpec((tr, 1024), lambda i: (i, 4)),
                            pl.BlockSpec((32, 1024), lambda i: (0, 0))],
                  out_specs=[row, row, pl.BlockSpec((32, 1024), lambda i: (0, 0))],
                  out_shape=[jax.ShapeDtypeStruct((t, 1024), BF16), jax.ShapeDtypeStruct((t, 1024), BF16),
                             jax.ShapeDtypeStruct((32, 1024), F32)],
                  scratch_shapes=[pltpu.VMEM((tr + HALO, 1024), F32), pltpu.VMEM((tr + HALO, 1024), F32)])(
                      dy, dy, hdn, hdn, proj, proj, conv_w)


NEG_BIG = -1e30


def _seg_tables(qb):
    j = np.arange(128)
    bd = (j[:, None] // 64 == j[None, :] // 64).astype(np.float32)
    ones = np.ones((128, 128), np.float32)
    later = np.concatenate([(j[:, None] >= j[None, :]).astype(np.float32), ones], axis=1)
    earlier = np.concatenate([(j[:, None] < j[None, :]).astype(np.float32), ones], axis=1)
    per = qb // CHUNK
    row = np.arange(qb)[:, None]
    pad = np.broadcast_to(j[None, :] < PAD_FRONT, (qb, 128))
    diag = [(g * CHUNK + j[None, :]) >= row for g in range(per)]
    masks = diag + [np.zeros((qb, 128), bool), pad, diag[0] | pad]
    bias = np.stack([np.where(m, NEG_BIG, 0.0) for m in masks]).astype(np.float32)
    dup = lambda m: np.concatenate([m, m], axis=0)
    return (jnp.asarray(bd, BF16), jnp.asarray(dup(later), BF16), jnp.asarray(dup(earlier), BF16),
            jnp.asarray(bias, F32))


def _split_dot(x, m):
    hi = x.astype(BF16)
    lo = (x - hi.astype(F32)).astype(BF16)
    return _dot(hi, m) + _dot(lo, m)


def _qk_norm_fwd(qkv, qg, kg, bd):
    t = qkv.shape[0]
    tr = _rows(t)
    nb = tr // CHUNK

    def body(q_ref, k_ref, v_ref, qg_ref, kg_ref, bd_ref, qo, kt, k2, vt, v2):
        bdm = bd_ref[...]
        lane = lax.broadcasted_iota(jnp.int32, (1, 128), 1)
        sub = lax.broadcasted_iota(jnp.int32, (128, 1), 0)

        def pair_layouts(x, t_ref, s_ref, hp, b):
            xt = x.T
            t_ref[hp, b] = jnp.concatenate([jnp.where(sub < 64, xt, 0.0), jnp.where(sub >= 64, xt, 0.0)],
                                           axis=1).astype(BF16)
            s_ref[hp, b] = jnp.concatenate([jnp.where(lane < 64, x, 0.0), jnp.where(lane >= 64, x, 0.0)],
                                           axis=0).astype(BF16)

        for hp in range(8):
            sl = slice(128 * hp, 128 * (hp + 1))
            x = q_ref[:, sl]
            r = lax.rsqrt(_split_dot(x * x, bdm) * (1.0 / 64) + EPS)
            qo[:, sl] = (x * r * (qg_ref[:, sl] * SB_SCALE)).astype(BF16)
            x = k_ref[:, sl]
            r = lax.rsqrt(_split_dot(x * x, bdm) * (1.0 / 64) + EPS)
            kn = x * r * kg_ref[:, sl]
            v = v_ref[:, sl]
            for b in range(nb):
                rows = slice(CHUNK * b, CHUNK * (b + 1))
                pair_layouts(kn[rows], kt, k2, hp, b)
                pair_layouts(v[rows], vt, v2, hp, b)

    col = lambda c: pl.BlockSpec((tr, 1024), lambda i: (i, c))
    vec = pl.BlockSpec((1, 1024), lambda i: (0, 0))
    wide = pl.BlockSpec((8, nb, 128, 256), lambda i: (0, i, 0, 0))
    tall = pl.BlockSpec((8, nb, 256, 128), lambda i: (0, i, 0, 0))
    wsh = jax.ShapeDtypeStruct((8, t // CHUNK, 128, 256), BF16)
    tsh = jax.ShapeDtypeStruct((8, t // CHUNK, 256, 128), BF16)
    return _pcall(body, name="qk_norm_fwd", grid=(t // tr,),
                  in_specs=[col(0), col(1), col(2), vec, vec, pl.BlockSpec((128, 128), lambda i: (0, 0))],
                  out_specs=[col(0), wide, tall, wide, tall],
                  out_shape=[jax.ShapeDtypeStruct((t, 1024), BF16), wsh, tsh, wsh, tsh])(qkv, qkv, qkv, qg, kg, bd)


def _qk_norm_bwd(qkv, dq, dk, dv, qg, kg, bd):
    t = qkv.shape[0]
    tr = _rows(t)

    def body(q_ref, k_ref, dq_ref, dk_ref, dv_ref, qg_ref, kg_ref, bd_ref, o_ref, dqg_ref, dkg_ref):
        @pl.when(pl.program_id(0) == 0)
        def _():
            dqg_ref[...] = jnp.zeros_like(dqg_ref)
            dkg_ref[...] = jnp.zeros_like(dkg_ref)

        bdm = bd_ref[...]
        for part, (src, d_ref, g_ref, dg_ref) in enumerate(((q_ref, dq_ref, qg_ref, dqg_ref),
                                                           (k_ref, dk_ref, kg_ref, dkg_ref))):
            for cix in range(8):
                sl = slice(128 * cix, 128 * (cix + 1))
                x = src[:, sl]
                d = d_ref[:, sl]
                r = lax.rsqrt(_split_dot(x * x, bdm) * (1.0 / 64) + EPS)
                u = d * g_ref[:, sl]
                m = _split_dot(u * x, bdm) * (1.0 / 64)
                o_ref[:, 1024 * part + 128 * cix:1024 * part + 128 * (cix + 1)] = (r * u - x * (r * r * r * m)).astype(BF16)
                dg_ref[:, sl] += jnp.sum(d * x * r, axis=0, keepdims=True)
        o_ref[:, 2048:3072] = dv_ref[...].astype(BF16)

    col = lambda c: pl.BlockSpec((tr, 1024), lambda i: (i, c))
    vec = pl.BlockSpec((1, 1024), lambda i: (0, 0))
    vsh = jax.ShapeDtypeStruct((1, 1024), F32)
    return _pcall(body, name="qk_norm_bwd", grid=(t // tr,),
                  in_specs=[col(0), col(1), col(0), col(0), col(0), vec, vec, pl.BlockSpec((128, 128), lambda i: (0, 0))],
                  out_specs=[pl.BlockSpec((tr, 3072), lambda i: (i, 0)), vec, vec],
                  out_shape=[jax.ShapeDtypeStruct((t, 3072), BF16), vsh, vsh])(qkv, qkv, dq, dk, dv, qg, kg, bd)


def _split2(x):
    hi = x.astype(BF16)
    lo = (x - hi.astype(F32)).astype(BF16)
    return jnp.concatenate([hi, lo], axis=1)


def _sb_scores(z, later_tab):
    e = jnp.exp(-jnp.abs(z))
    ope = 1.0 + e
    sp = jnp.maximum(z, 0.0) + jnp.log(ope)
    return e, ope, _dot(_split2(sp), later_tab)


def _sb_bias_index(i, kb, per):
    g = kb - i * per
    return jnp.where(kb == 0, jnp.where(i == 0, per + 2, per + 1), jnp.where(g >= 0, g, per))


def _sb_qb(t):
    return _tile(t, (384, 128))


def _sb_fwd(qh, kt, v2, later_tab, bias_tab):
    t = qh.shape[0]
    qb = _sb_qb(t)
    per = qb // CHUNK
    nkb_all = t // CHUNK

    def body(q_ref, kt_ref, v2_ref, tab_ref, bias_ref, o_ref, c_ref, acc, carry, zbuf, zk, cub, wbuf):
        i = pl.program_id(1)
        q = q_ref[...]
        acc[...] = jnp.zeros_like(acc)
        carry[...] = jnp.zeros_like(carry)
        nkb = (i + 1) * per

        def sums(z2, kb):
            bias = bias_ref[_sb_bias_index(i, kb, per)]
            for hh in range(2):
                sl = slice(128 * hh, 128 * (hh + 1))
                z = z2[:, sl] + bias
                zk[:, sl] = z
                cub[hh] = _sb_scores(z, tab_ref[...])[2]

        def weights():
            for hh in range(2):
                sl = slice(128 * hh, 128 * (hh + 1))
                cu = cub[hh]
                cin = carry[hh]
                wbuf[:, sl] = jnp.exp(zk[:, sl] - cu[:, :128] - cin).astype(BF16)
                carry[hh] = cin + cu[:, 128:]

        def output(kb):
            acc[...] += _dot(wbuf[...], v2_ref[kb])

        sums(_dot(q, kt_ref[nkb - 1]), nkb - 1)
        zbuf[...] = _dot(q, kt_ref[jnp.maximum(nkb - 2, 0)])
        wbuf[...] = jnp.zeros_like(wbuf)

        def step(s, _):
            kb = nkb - 1 - s
            z2 = zbuf[...]
            zbuf[...] = _dot(q, kt_ref[jnp.maximum(kb - 2, 0)])
            output(jnp.minimum(kb + 1, nkb - 1))
            weights()
            sums(z2, kb - 1)
            return 0

        lax.fori_loop(0, nkb - 1, step, 0)
        output(jnp.minimum(1, nkb - 1))
        weights()
        output(0)
        o_ref[...] = acc[...]
        for hh in range(2):
            c_ref[:, 128 * hh:128 * (hh + 1)] = carry[hh]

    blk = pl.BlockSpec((qb, 128), lambda h, i: (i, h))
    wide = pl.BlockSpec((None, nkb_all, 128, 256), lambda h, i: (h, 0, 0, 0))
    tall = pl.BlockSpec((None, nkb_all, 256, 128), lambda h, i: (h, 0, 0, 0))
    return _pcall(body, name="sb_fwd", grid=(8, t // qb),
                  in_specs=[blk, wide, tall, pl.BlockSpec((256, 256), lambda h, i: (0, 0)),
                            pl.BlockSpec((per + 3, qb, 128), lambda h, i: (0, 0, 0))],
                  out_specs=[blk, pl.BlockSpec((qb, 256), lambda h, i: (i, h))],
                  out_shape=[jax.ShapeDtypeStruct((t, 1024), F32), jax.ShapeDtypeStruct((t, 2048), F32)],
                  scratch_shapes=[pltpu.VMEM((qb, 128), F32), pltpu.VMEM((2, qb, 128), F32),
                                  pltpu.VMEM((qb, 256), F32), pltpu.VMEM((qb, 256), F32),
                                  pltpu.VMEM((2, qb, 256), F32), pltpu.VMEM((qb, 256), BF16)],
                  compiler_params=_params(dimension_semantics=("parallel", "arbitrary")))(
                      qh, kt, v2, later_tab, bias_tab)


def _sb_bwd(qh, kt, k2, vt, carries, do, later_tab, earlier_tab, bias_tab):
    t = qh.shape[0]
    qb = _sb_qb(t)
    per = qb // CHUNK
    nkb_all = t // CHUNK

    def body(q_ref, kt_ref, k2_ref, vt_ref, c_ref, do_ref, tab_ref, etab_ref, bias_ref,
             dq_ref, dk_ref, dv_ref, acc, gcarry, later, zbuf, dwbuf):
        i = pl.program_id(1)

        @pl.when(i == 0)
        def _():
            dk_ref[...] = jnp.zeros_like(dk_ref)
            dv_ref[...] = jnp.zeros_like(dv_ref)

        q = q_ref[...]
        dob = do_ref[...].astype(BF16)
        lane = lax.broadcasted_iota(jnp.int32, (1, 128), 1)
        acc[...] = jnp.zeros_like(acc)
        gcarry[...] = jnp.zeros_like(gcarry)
        for hh in range(2):
            later[hh] = c_ref[:, 128 * hh:128 * (hh + 1)]
        nkb = (i + 1) * per
        zbuf[...] = _dot(q, kt_ref[0])
        dwbuf[...] = _dot(dob, vt_ref[0])

        def step(kb, _):
            bias = bias_ref[_sb_bias_index(i, kb, per)]
            z2 = zbuf[...]
            dw2 = dwbuf[...]
            nxt = jnp.minimum(kb + 1, nkb - 1)
            zbuf[...] = _dot(q, kt_ref[nxt])
            dwbuf[...] = _dot(dob, vt_ref[nxt])
            dzs, ws = [], []
            for hh in range(2):
                sl = slice(128 * hh, 128 * (hh + 1))
                z = z2[:, sl] + bias
                e, ope, cu = _sb_scores(z, tab_ref[...])
                cin = later[hh] - cu[:, 128:]
                later[hh] = cin
                w = jnp.exp(z - cu[:, :128] - cin)
                gw = w * dw2[:, sl]
                cu2 = _dot(_split2(gw), etab_ref[...])
                gin = gcarry[hh]
                gcarry[hh] = gin + cu2[:, 128:]
                r = 1.0 / ope
                sig = jnp.where(z >= 0, r, e * r)
                dzs.append((gw - sig * (gw + cu2[:, :128] + gin)).astype(BF16))
                ws.append(w.astype(BF16))
            dz2 = jnp.concatenate(dzs, axis=1)
            w2 = jnp.concatenate(ws, axis=1)
            acc[...] += _dot(dz2, k2_ref[kb])
            dk2 = _dot(dz2, q, "tn")
            dv2 = _dot(w2, dob, "tn")
            dk_ref[kb] += jnp.where(lane < 64, dk2[:128], dk2[128:])
            dv_ref[kb] += jnp.where(lane < 64, dv2[:128], dv2[128:])
            return 0

        lax.fori_loop(0, nkb, step, 0)
        dq_ref[...] = acc[...] * SB_SCALE

    blk = pl.BlockSpec((qb, 128), lambda h, i: (i, h))
    wide = pl.BlockSpec((None, nkb_all, 128, 256), lambda h, i: (h, 0, 0, 0))
    tall = pl.BlockSpec((None, nkb_all, 256, 128), lambda h, i: (h, 0, 0, 0))
    tab = pl.BlockSpec((256, 256), lambda h, i: (0, 0))
    kv_out = pl.BlockSpec((nkb_all, 128, 128), lambda h, i: (0, 0, h))
    ksh = jax.ShapeDtypeStruct((nkb_all, 128, 1024), F32)
    dq, dk, dv = _pcall(
        body, name="sb_bwd", grid=(8, t // qb),
        in_specs=[blk, wide, tall, wide, pl.BlockSpec((qb, 256), lambda h, i: (i, h)), blk, tab, tab,
                  pl.BlockSpec((per + 3, qb, 128), lambda h, i: (0, 0, 0))],
        out_specs=[blk, kv_out, kv_out], out_shape=[jax.ShapeDtypeStruct((t, 1024), F32), ksh, ksh],
        scratch_shapes=[pltpu.VMEM((qb, 128), F32), pltpu.VMEM((2, qb, 128), F32), pltpu.VMEM((2, qb, 128), F32),
                        pltpu.VMEM((qb, 256), F32), pltpu.VMEM((qb, 256), F32)],
        compiler_params=_params(dimension_semantics=("parallel", "arbitrary")))(
            qh, kt, k2, vt, carries, do, later_tab, earlier_tab, bias_tab)
    return dq, dk.reshape(t, 1024), dv.reshape(t, 1024)


def _adamw_math(w, g, m, v):
    m = ADAM_B1 * m + (1.0 - ADAM_B1) * g
    v = ADAM_B2 * v + (1.0 - ADAM_B2) * (g * g)
    m_hat = m / (1.0 - ADAM_B1 ** ADAM_STEP)
    v_hat = v / (1.0 - ADAM_B2 ** ADAM_STEP)
    delta = -ADAM_LR * (m_hat / (jnp.sqrt(v_hat) + ADAM_EPS) + ADAM_WD * w)
    return delta, m, v


def _adamw(name, w, owns, recvs, m, v, me):
    shape = w.shape
    c = shape[-1]
    nl = len(owns)
    w3, m3, v3 = (a.reshape(nl, -1, c) for a in (w, m, v))
    r = w3.shape[1]
    tr = _tile(r, (256, 128))
    owns = [o.reshape(N_DEV, r, c) for o in owns]
    recvs = [p.reshape(N_DEV - 1, r, c) for p in recvs]

    def body(me_ref, w_ref, *rest):
        own_refs, recv_refs = rest[:nl], rest[nl:2 * nl]
        m_ref, v_ref = rest[2 * nl:2 * nl + 2]
        g_out, d_out, m_out, v_out = rest[2 * nl + 2:]
        layer = pl.program_id(0)

        def grad(k):
            g = own_refs[k][...].astype(F32)
            for s in range(N_DEV - 1):
                g = g + recv_refs[k][s].astype(F32)
            return g

        g = grad(0)
        for k in range(1, nl):
            g = jnp.where(layer == k, grad(k), g)
        d, mn, vn = _adamw_math(w_ref[...], g, m_ref[...], v_ref[...])
        g_out[...] = g
        d_out[...] = d
        m_out[...] = mn
        v_out[...] = vn

    row = pl.BlockSpec((None, tr, c), lambda l, i, me_ref: (l, i, 0))
    own = lambda k: pl.BlockSpec((None, tr, c), lambda l, i, me_ref: (me_ref[0], jnp.where(l == k, i, 0), 0))
    rcv = lambda k: pl.BlockSpec((N_DEV - 1, tr, c), lambda l, i, me_ref: (0, jnp.where(l == k, i, 0), 0))
    osh = jax.ShapeDtypeStruct((nl, r, c), F32)
    grid_spec = pltpu.PrefetchScalarGridSpec(
        num_scalar_prefetch=1, grid=(nl, r // tr),
        in_specs=[row] + [own(k) for k in range(nl)] + [rcv(k) for k in range(nl)] + [row, row],
        out_specs=[row, row, row, row])
    outs = _pcall(body, name=name, grid_spec=grid_spec, out_shape=[osh, osh, osh, osh])(
        me.reshape(1), w3, *owns, *recvs, m3, v3)
    return tuple(o.reshape(shape) for o in outs)


def _place():
    x, y, c = lax.axis_index("x"), lax.axis_index("y"), lax.axis_index("c")
    return x, y, c, 4 * x + 2 * y + c


def _peer(x, y, c, rel):
    return (x ^ ((rel >> 2) & 1), y ^ ((rel >> 1) & 1), c ^ (rel & 1))


def _gather_first(now, later):
    n, k = len(now), len(later)

    def body(*refs):
        ins, outs = refs[:n + k], refs[n + k:2 * (n + k)]
        send, recv, lsem = refs[2 * (n + k):]
        x, y, c, me = _place()
        locals_ = []
        for w in range(n + k):
            local = pltpu.make_async_copy(ins[w], outs[w].at[me], lsem.at[w])
            local.start()
            locals_.append(local)
        for w in range(n):
            for rel in range(1, N_DEV):
                pltpu.make_async_remote_copy(src_ref=ins[w], dst_ref=outs[w].at[me], send_sem=send.at[w, rel - 1],
                                             recv_sem=recv.at[w, rel - 1], device_id=_peer(x, y, c, rel),
                                             device_id_type=MESH).start()
        for w in range(n):
            for rel in range(1, N_DEV):
                cp = pltpu.make_async_remote_copy(src_ref=ins[w], dst_ref=outs[w].at[me ^ rel],
                                                  send_sem=send.at[w, rel - 1], recv_sem=recv.at[w, rel - 1],
                                                  device_id=_peer(x, y, c, rel), device_id_type=MESH)
                cp.wait_send()
                cp.wait_recv()
        for local in locals_:
            local.wait()

    hbm = pl.BlockSpec(memory_space=pl.ANY)
    arrays = list(now) + list(later)
    return _pcall(body, name="gather_first", in_specs=[hbm] * (n + k), out_specs=[hbm] * (n + k),
                  out_shape=[jax.ShapeDtypeStruct((N_DEV,) + a.shape, a.dtype) for a in arrays],
                  scratch_shapes=[pltpu.SemaphoreType.DMA((n, N_DEV - 1)), pltpu.SemaphoreType.DMA((n, N_DEV - 1)),
                                  pltpu.SemaphoreType.DMA((n + k,))],
                  compiler_params=_params(has_side_effects=True))(*arrays)


_HBM = pl.BlockSpec(memory_space=pltpu.HBM)
_SEM = pl.BlockSpec(memory_space=pltpu.SEMAPHORE)
_DATAFLOW = pltpu.SideEffectType.DATAFLOW_SIDE_EFFECTING


def _exchange_refs(srcs, lands, mode, me, rel, j):
    if mode == "gather":
        return srcs[j], lands[j].at[me], lands[j].at[me ^ rel]
    return srcs[j].at[me ^ rel], lands[j].at[rel - 1], lands[j].at[rel - 1]


def _exchange_start(name, srcs, lands, mode):
    n = len(srcs)

    def body(*refs):
        ins, lnd = refs[:n], refs[n:2 * n]
        send, recv = refs[2 * n], refs[2 * n + 1]
        token = refs[-1]
        x, y, c, me = _place()
        for j in range(n):
            for rel in range(1, N_DEV):
                src, dst, _ = _exchange_refs(ins, lnd, mode, me, rel, j)
                pltpu.make_async_remote_copy(src_ref=src, dst_ref=dst, send_sem=send.at[j * (N_DEV - 1) + rel - 1],
                                             recv_sem=recv.at[j * (N_DEV - 1) + rel - 1],
                                             device_id=_peer(x, y, c, rel), device_id_type=MESH).start()
        token[...] = jnp.zeros_like(token)

    sems = pltpu.SemaphoreType.DMA((n * (N_DEV - 1),))
    hbm_like = lambda a: pltpu.HBM(a.shape, a.dtype)
    outs = _pcall(body, name=name + "_start",
                  in_specs=[_HBM] * (2 * n), out_specs=[_SEM, _SEM] + [_HBM] * (2 * n) + [pl.BlockSpec(memory_space=pltpu.VMEM)],
                  out_shape=[sems, sems] + [hbm_like(a) for a in srcs] + [hbm_like(a) for a in lands]
                  + [jax.ShapeDtypeStruct((8, 128), F32)],
                  input_output_aliases={i: 2 + i for i in range(2 * n)},
                  compiler_params=pltpu.CompilerParams(has_side_effects=_DATAFLOW))(
                      *[pltpu.with_memory_space_constraint(a, pltpu.HBM) for a in list(srcs) + list(lands)])
    return dict(name=name, mode=mode, n=n, send=outs[0], recv=outs[1], srcs=outs[2:2 + n], lands=outs[2 + n:2 + 2 * n],
                token=outs[-1][0, 0])


def _exchange_wait(ex, after):
    n, mode = ex["n"], ex["mode"]

    def body(*refs):
        ins, lnd = refs[:n], refs[n:2 * n]
        send, recv = refs[2 * n], refs[2 * n + 1]
        x, y, c, me = _place()
        for j in range(n):
            for rel in range(1, N_DEV):
                src, dst, landed = _exchange_refs(ins, lnd, mode, me, rel, j)
                pltpu.make_async_remote_copy(src_ref=src, dst_ref=dst, send_sem=send.at[j * (N_DEV - 1) + rel - 1],
                                             recv_sem=recv.at[j * (N_DEV - 1) + rel - 1],
                                             device_id=_peer(x, y, c, rel), device_id_type=MESH).wait_send()
                pltpu.make_async_remote_copy(src_ref=src, dst_ref=landed, send_sem=send.at[j * (N_DEV - 1) + rel - 1],
                                             recv_sem=recv.at[j * (N_DEV - 1) + rel - 1],
                                             device_id=_peer(x, y, c, rel), device_id_type=MESH).wait_recv()

    hbm_like = lambda a: pltpu.HBM(a.shape, a.dtype)
    arrays = list(ex["srcs"]) + list(ex["lands"])
    outs = _pcall(body, name=ex["name"] + "_wait",
                  in_specs=[_HBM] * (2 * n) + [_SEM, _SEM, pl.BlockSpec(memory_space=pl.ANY)],
                  out_specs=[_HBM] * (2 * n), out_shape=[hbm_like(a) for a in arrays],
                  input_output_aliases={i: i for i in range(2 * n)},
                  compiler_params=pltpu.CompilerParams(has_side_effects=_DATAFLOW))(
                      *arrays, ex["send"], ex["recv"], after)
    return outs[:n], outs[n:]


def _scatter_start(name, grads):
    lands = [lax.empty((N_DEV - 1,) + g.shape[1:], g.dtype) for g in grads]
    return _exchange_start(name, grads, lands, "scatter")


ROW_MIX, ROW_MLP, ROW_CB, ROW_LG, ROW_LB, ROW_QN, ROW_KN, ROW_LOSS = 0, 2, 4, 5, 6, 7, 8, 9
ROW_META, ROW_CW, ROW_GN, SMALL_ROWS = 16, 32, 64, 72


def _allreduce_small(part):
    def body(p_ref, o_ref, slots, send, recv):
        x, y, c, me = _place()
        slots[me] = p_ref[...]
        for rel in range(1, N_DEV):
            pltpu.make_async_remote_copy(src_ref=p_ref, dst_ref=slots.at[me], send_sem=send.at[rel - 1],
                                         recv_sem=recv.at[rel - 1], device_id=_peer(x, y, c, rel),
                                         device_id_type=MESH).start()
        for rel in range(1, N_DEV):
            cp = pltpu.make_async_remote_copy(src_ref=p_ref, dst_ref=slots.at[me ^ rel], send_sem=send.at[rel - 1],
                                              recv_sem=recv.at[rel - 1], device_id=_peer(x, y, c, rel),
                                              device_id_type=MESH)
            cp.wait_send()
            cp.wait_recv()
        tot = slots[0]
        for s in range(1, N_DEV):
            tot = tot + slots[s]
        o_ref[...] = tot
        for row in (ROW_QN, ROW_KN):
            v = tot[row:row + 1, :]
            f = v[:, 0:128]
            for k in range(1, 8):
                f = f + v[:, 128 * k:128 * (k + 1)]
            o_ref[row:row + 1, 0:64] = f[:, 0:64] + f[:, 64:128]

    vm = pl.BlockSpec(memory_space=pltpu.VMEM)
    return _pcall(body, name="allreduce_small", in_specs=[vm], out_specs=vm,
                  out_shape=jax.ShapeDtypeStruct(part.shape, F32),
                  scratch_shapes=[pltpu.VMEM((N_DEV,) + part.shape, F32), pltpu.SemaphoreType.DMA((N_DEV - 1,)),
                                  pltpu.SemaphoreType.DMA((N_DEV - 1,))],
                  compiler_params=_params(has_side_effects=True))(part)


def _adamw_small(w, g, m, v):
    def body(w_ref, g_ref, m_ref, v_ref, d_out, m_out, v_out):
        d, mn, vn = _adamw_math(w_ref[...], g_ref[...], m_ref[...], v_ref[...])
        d_out[...] = d
        m_out[...] = mn
        v_out[...] = vn

    osh = jax.ShapeDtypeStruct(w.shape, F32)
    return _pcall(body, name="adamw_small", out_shape=[osh, osh, osh])(w, g, m, v)


def _local_step(h0, target, p, weight, emit):
    t = h0.shape[0]
    tables = _ret_tables(t)
    bd, later_tab, earlier_tab, bias_tab = _seg_tables(_sb_qb(t))
    row = lambda a, i: a[i:i + 1]

    hn_a = _rms_fwd("rms_mix0", h0, row(p["norm_mix_g"], 0))
    w_in = weight("w_in", hn_a)
    proj = _mm_cols("proj_in", hn_a, w_in, ())
    o_ret, states = _ret_fwd(proj, tables)
    gn_flat = p["gn_g"].reshape(1, 1024)
    cat = _gn_gate_fwd(o_ret, proj, gn_flat)
    cat, hdn, ycv = _conv_fwd(cat, proj, p["conv_w"], p["conv_b"], p["ln_g"], p["ln_b"])
    w_out = weight("w_out", cat)
    h1 = _mm_rows("mix_out", cat, w_out, h0)
    hn_b = _rms_fwd("rms_mlp0", h1, row(p["norm_mlp_g"], 0))
    w1_0, w2_0 = weight("w1_0", hn_b), weight("w2_0", hn_b)
    a0, s0 = _mm_cols("mlp0_up", hn_b, w1_0, (), epi="relu2")
    h2 = _mm_rows("mlp0_down", s0, w2_0, h1)

    hn_c = _rms_fwd("rms_mix1", h2, row(p["norm_mix_g"], 1))
    w_qkv = weight("w_qkv", hn_c)
    qkv = _mm_cols("qkv", hn_c, w_qkv, ())
    qg = jnp.tile(p["qn_g"], (1, 16))
    kg = jnp.tile(p["kn_g"], (1, 16))
    qh, kt, k2, vt, v2 = _qk_norm_fwd(qkv, qg, kg, bd)
    o_sb, carries = _sb_fwd(qh, kt, v2, later_tab, bias_tab)
    w_o = weight("w_o", o_sb)
    h3 = _mm_rows("attn_out", o_sb, w_o, h2)
    hn_d = _rms_fwd("rms_mlp1", h3, row(p["norm_mlp_g"], 1))
    w1_1, w2_1 = weight("w1_1", hn_d), weight("w2_1", hn_d)
    a1, s1 = _mm_cols("mlp1_up", hn_d, w1_1, (), epi="relu2")
    h4 = _mm_rows("mlp1_down", s1, w2_1, h3)

    dh, loss = _loss_bwd(h4, target)

    def mlp_bwd(tag, layer, w1, w2, dh, h_in, hn, a, s):
        da = _mm_rows_t(f"{tag}_dact", dh, w2, (), out_dtype=BF16, epi="drelu2", extra=a)
        dw2 = _wgrad_rows(f"{tag}_dw2", s, dh, 512)
        dw1 = _wgrad_cols(f"{tag}_dw1", hn, da, 512)
        tok = emit(tag, [dw1, dw2])
        dhn = _mm_cols_t(f"{tag}_dhn", da, w1)
        return _rms_bwd(f"{tag}_rms_bwd", dhn, h_in, row(p["norm_mlp_g"], layer) + tok, dh)

    dh, dg_mlp1 = mlp_bwd("mlp1", 1, w1_1, w2_1, dh, h3, hn_d, a1, s1)

    do_sb = _mm_rows_t("attn_dout", dh, w_o, ())
    dw_o = _wgrad_rows("attn_dwo", o_sb, dh, 128)
    dq, dk, dv = _sb_bwd(qh, kt, k2, vt, carries, do_sb, later_tab, earlier_tab, bias_tab)
    dqkv, dqg, dkg = _qk_norm_bwd(qkv, dq, dk, dv, qg, kg, bd)
    dw_qkv = _wgrad_cols("qkv_dw", hn_c, dqkv, 384)
    tok = emit("attn", [dw_qkv, dw_o])
    dhn = _mm_cols_t("qkv_dhn", dqkv, w_qkv)
    dh, dg_mix1 = _rms_bwd("mix1_rms_bwd", dhn, h2, row(p["norm_mix_g"], 1) + tok, dh)

    dh, dg_mlp0 = mlp_bwd("mlp0", 0, w1_0, w2_0, dh, h1, hn_b, a0, s0)

    dcat = _mm_rows_t("mix_dcat", dh, w_out, ())
    dw_out = _wgrad_rows("mix_dwout", cat, dh, 256)
    do_ret, dgate, dgn = _gn_gate_bwd(dcat, o_ret, proj, gn_flat)
    dq_r, dk_r, dv_r = _ret_bwd(proj, states, do_ret, tables)
    dy, dlg, dlb, dcb = _conv_bwd_ln(dcat, ycv, p["ln_g"], p["ln_b"])
    dua, dug, dcw = _conv_bwd_taps(dy, hdn, proj, p["conv_w"])
    dproj = jnp.concatenate([dq_r, dk_r, dv_r, dgate, dua, dug], axis=1)
    dw_in = _wgrad_cols("proj_dw", hn_a, dproj, 640)
    tok = emit("mix0", [dw_in, dw_out])
    dhn = _mm_cols_t("proj_dhn", dproj, w_in)
    dh, dg_mix0 = _rms_bwd("mix0_rms_bwd", dhn, h0, row(p["norm_mix_g"], 0) + tok, dh)

    rid = lax.broadcasted_iota(jnp.int32, (16, 1), 0)
    loss_row = jnp.broadcast_to(loss[0:1, 0:1], (1, D_MODEL))
    vecs = sum(jnp.where(rid == k, v, 0.0)
               for k, v in enumerate((dg_mix0, dg_mix1, dg_mlp0, dg_mlp1, dcb, dlg, dlb, dqg, dkg, loss_row)))
    small = jnp.concatenate([vecs, dh[PAD_FRONT:TOK0], dcw, jnp.where(rid[:8] == 0, dgn, 0.0)], axis=0)
    return dh[TOK0:], small


_SMALL_NAMES = ("meta", "norm_mix_g", "norm_mlp_g", "even_ret_gn_g", "even_conv_w", "even_conv_b",
                "even_conv_ln_g", "even_conv_ln_b", "odd_q_norm_g", "odd_k_norm_g")
_BIG_NAMES = ("even_w_in", "even_w_out", "odd_w_qkv", "odd_w_o", "mlp_w1", "mlp_w2")
_ORDER = ("meta", "norm_mix_g", "norm_mlp_g", "even_w_in", "even_ret_gn_g", "even_conv_w", "even_conv_b",
          "even_conv_ln_g", "even_conv_ln_b", "even_w_out", "odd_w_qkv", "odd_q_norm_g", "odd_k_norm_g",
          "odd_w_o", "mlp_w1", "mlp_w2")


def _pack128(a):
    flat = a.reshape(-1)
    n = flat.shape[0]
    rows = -(-n // 128)
    rows8 = -(-rows // 8) * 8
    return jnp.pad(flat, (0, rows8 * 128 - n)).reshape(rows8, 128)


def kernel(x, meta, norm_mix_g, norm_mlp_g, even_w_in, even_ret_gn_g, even_conv_w, even_conv_b, even_conv_ln_g, even_conv_ln_b, even_w_out, odd_w_qkv, odd_q_norm_g, odd_k_norm_g, odd_w_o, mlp_w1, mlp_w2, loss_target, m_meta, m_norm_mix_g, m_norm_mlp_g, m_even_w_in, m_even_ret_gn_g, m_even_conv_w, m_even_conv_b, m_even_conv_ln_g, m_even_conv_ln_b, m_even_w_out, m_odd_w_qkv, m_odd_q_norm_g, m_odd_k_norm_g, m_odd_w_o, m_mlp_w1, m_mlp_w2, v_meta, v_norm_mix_g, v_norm_mlp_g, v_even_w_in, v_even_ret_gn_g, v_even_conv_w, v_even_conv_b, v_even_conv_ln_g, v_even_conv_ln_b, v_even_w_out, v_odd_w_qkv, v_odd_q_norm_g, v_odd_k_norm_g, v_odd_w_o, v_mlp_w1, v_mlp_w2):
    w = dict(meta=meta, norm_mix_g=norm_mix_g, norm_mlp_g=norm_mlp_g, even_w_in=even_w_in,
             even_ret_gn_g=even_ret_gn_g, even_conv_w=even_conv_w, even_conv_b=even_conv_b,
             even_conv_ln_g=even_conv_ln_g, even_conv_ln_b=even_conv_ln_b, even_w_out=even_w_out,
             odd_w_qkv=odd_w_qkv, odd_q_norm_g=odd_q_norm_g, odd_k_norm_g=odd_k_norm_g, odd_w_o=odd_w_o,
             mlp_w1=mlp_w1, mlp_w2=mlp_w2)
    mom = dict(meta=m_meta, norm_mix_g=m_norm_mix_g, norm_mlp_g=m_norm_mlp_g, even_w_in=m_even_w_in,
               even_ret_gn_g=m_even_ret_gn_g, even_conv_w=m_even_conv_w, even_conv_b=m_even_conv_b,
               even_conv_ln_g=m_even_conv_ln_g, even_conv_ln_b=m_even_conv_ln_b, even_w_out=m_even_w_out,
               odd_w_qkv=m_odd_w_qkv, odd_q_norm_g=m_odd_q_norm_g, odd_k_norm_g=m_odd_k_norm_g, odd_w_o=m_odd_w_o,
               mlp_w1=m_mlp_w1, mlp_w2=m_mlp_w2)
    var = dict(meta=v_meta, norm_mix_g=v_norm_mix_g, norm_mlp_g=v_norm_mlp_g, even_w_in=v_even_w_in,
               even_ret_gn_g=v_even_ret_gn_g, even_conv_w=v_even_conv_w, even_conv_b=v_even_conv_b,
               even_conv_ln_g=v_even_conv_ln_g, even_conv_ln_b=v_even_conv_ln_b, even_w_out=v_even_w_out,
               odd_w_qkv=v_odd_w_qkv, odd_q_norm_g=v_odd_q_norm_g, odd_k_norm_g=v_odd_k_norm_g, odd_w_o=v_odd_w_o,
               mlp_w1=v_mlp_w1, mlp_w2=v_mlp_w2)
    me = 4 * lax.axis_index("x") + 2 * lax.axis_index("y") + lax.axis_index("c")

    small_in = jnp.concatenate([meta, jnp.pad(even_conv_w[0], ((0, 1), (0, 0))),
                                jnp.pad(even_ret_gn_g[0], ((0, 4), (0, 96)))], axis=0)
    b16 = lambda a: a.astype(BF16)
    later_src = dict(w_out=b16(even_w_out[0]), w1_0=b16(mlp_w1[0]), w2_0=b16(mlp_w2[0]),
                     w_qkv=b16(odd_w_qkv[0]), w_o=b16(odd_w_o[0]), w1_1=b16(mlp_w1[1]), w2_1=b16(mlp_w2[1]))
    landed = _gather_first([b16(even_w_in[0]), small_in], list(later_src.values()))
    g_in, g_small = landed[0], landed[1]
    own_slot = dict(zip(later_src, landed[2:]))
    groups = (("gather_l0", ("w_out", "w1_0", "w2_0")), ("gather_attn", ("w_qkv", "w_o")),
              ("gather_l1", ("w1_1", "w2_1")))
    pending = {}
    gather_tok = jnp.zeros((), F32)
    for gname, names in groups:
        ex = _exchange_start(gname, [later_src[n] for n in names], [own_slot[n] for n in names], "gather")
        gather_tok = gather_tok + ex["token"]
        for n in names:
            pending[n] = (ex, names)
    arrived = dict(w_in=g_in)

    def weight(name, after):
        if name not in arrived:
            ex, names = pending[name]
            arrived.update(zip(names, _exchange_wait(ex, after)[1]))
        return arrived[name]

    cols = lambda a: jnp.transpose(a, (1, 0, 2)).reshape(a.shape[1], -1)
    p = dict(norm_mix_g=norm_mix_g + gather_tok, norm_mlp_g=norm_mlp_g, conv_b=even_conv_b, ln_g=even_conv_ln_g,
             ln_b=even_conv_ln_b, qn_g=odd_q_norm_g, kn_g=odd_k_norm_g,
             gn_g=cols(g_small[:, 48:52, :32]),
             conv_w=jnp.pad(cols(g_small[:, 16:47]), ((0, 1), (0, 0))))
    meta_full = cols(g_small[:, 0:16])

    scatters = {}

    def emit(tag, grads):
        scatters[tag] = _scatter_start("scatter_" + tag, grads)
        return scatters[tag]["token"]

    h0 = jnp.concatenate([jnp.zeros((PAD_FRONT, D_MODEL), F32), meta_full, x[0]], axis=0)
    grad_x, small_part = _local_step(h0, loss_target[0], p, weight, emit)
    tot = _allreduce_small(small_part)
    loss = tot[ROW_LOSS, 0]

    got = {tag: _exchange_wait(ex, tot) for tag, ex in scatters.items()}
    pick = lambda tag, j: (got[tag][0][j], got[tag][1][j])
    terms = dict(even_w_in=[pick("mix0", 0)], even_w_out=[pick("mix0", 1)], odd_w_qkv=[pick("attn", 0)],
                 odd_w_o=[pick("attn", 1)], mlp_w1=[pick("mlp0", 0), pick("mlp1", 0)],
                 mlp_w2=[pick("mlp0", 1), pick("mlp1", 1)])
    out = {}
    for name in _BIG_NAMES:
        owns, recvs = zip(*terms[name])
        out[name] = _adamw("adamw_" + name, w[name], list(owns), list(recvs), mom[name], var[name], me)

    shard_cols = lambda a, width: lax.dynamic_slice_in_dim(a, me * width, width, axis=1)
    one = lambda r: tot[r:r + 1]
    small_g = dict(
        norm_mix_g=tot[ROW_MIX:ROW_MIX + 2], norm_mlp_g=tot[ROW_MLP:ROW_MLP + 2],
        even_conv_b=one(ROW_CB), even_conv_ln_g=one(ROW_LG), even_conv_ln_b=one(ROW_LB),
        odd_q_norm_g=one(ROW_QN)[:, :64], odd_k_norm_g=one(ROW_KN)[:, :64],
        meta=shard_cols(tot[ROW_META:ROW_META + N_META], 128),
        even_conv_w=shard_cols(tot[ROW_CW:ROW_CW + CONV_WIDTH], 128)[None],
        even_ret_gn_g=shard_cols(tot[ROW_GN].reshape(4, 256), 32)[None])
    packs = {n: (_pack128(w[n]), _pack128(small_g[n]), _pack128(mom[n]), _pack128(var[n])) for n in _SMALL_NAMES}
    cat4 = [jnp.concatenate([packs[n][i] for n in _SMALL_NAMES], axis=0) for i in range(4)]
    d_s, m_s, v_s = _adamw_small(*cat4)
    r0 = 0
    for n in _SMALL_NAMES:
        rows = packs[n][0].shape[0]
        size = w[n].size
        take = lambda a: a[r0:r0 + rows].reshape(-1)[:size].reshape(w[n].shape)
        out[n] = (small_g[n].reshape(w[n].shape), take(d_s), take(m_s), take(v_s))
        r0 += rows

    res = [loss, grad_x[None]]
    for i in range(4):
        res.extend(out[n][i] for n in _ORDER)
    return tuple(res)
```

```python
import functools

import numpy as np
import jax
import jax.numpy as jnp
from jax import lax
from jax.experimental import pallas as pl
from jax.experimental.pallas import tpu as pltpu

F32 = jnp.float32
BF16 = jnp.bfloat16

D_MODEL = 1024
N_META = 16
CHUNK = 128
PAD_FRONT = 112
TOK0 = PAD_FRONT + N_META
EPS = 1e-6
N_DEV = 8
RET_HEADS = 4
RET_DECAY_OFFSET = 5.0
ROPE_BASE = 10000.0
CONV_WIDTH = 31
HALO = 32
SB_SCALE = 64 ** -0.5
RET_SCALE = 128 ** -0.5
ADAM_LR, ADAM_B1, ADAM_B2, ADAM_EPS, ADAM_WD, ADAM_STEP = 0.001, 0.9, 0.999, 1e-08, 0.01, 10
VMEM_LIMIT = 56 * 1024 * 1024
MESH = pl.DeviceIdType.MESH


def _pcall(body, **kw):
    return pl.pallas_call(body, **kw)


def _params(**kw):
    return pltpu.CompilerParams(vmem_limit_bytes=VMEM_LIMIT, **kw)


def _tile(n, cands):
    for c in cands:
        if n % c == 0:
            return c
    raise ValueError(f"no tile for {n} in {cands}")


def _sigmoid(x):
    return 1.0 / (1.0 + jnp.exp(-x))


_DIMS = {
    "nn": (((1,), (0,)), ((), ())),
    "nt": (((1,), (1,)), ((), ())),
    "tn": (((0,), (0,)), ((), ())),
}


def _matmul(name, a, b, *, grid, a_spec, b_spec, o_spec, out_shape, contract, acc_shape,
            epi="plain", extra=None, extra_spec=None):
    nk = grid[2]
    dims = _DIMS[contract]
    n_in = 3 if extra is not None else 2
    n_out = 2 if epi == "relu2" else 1

    def body(*refs):
        a_ref, b_ref = refs[0], refs[1]
        e_ref = refs[2] if extra is not None else None
        outs = refs[n_in:n_in + n_out]
        acc = refs[-1]
        k = pl.program_id(2)
        part = lax.dot_general(a_ref[...].astype(BF16), b_ref[...].astype(BF16), dims, preferred_element_type=F32)
        if nk > 1:
            @pl.when(k == 0)
            def _():
                acc[...] = jnp.zeros_like(acc)

            acc[...] += part

        @pl.when(k == nk - 1)
        def _():
            r = acc[...] if nk > 1 else part
            if epi == "plain":
                outs[0][...] = r.astype(outs[0].dtype)
            elif epi == "residual":
                outs[0][...] = (r + e_ref[...]).astype(outs[0].dtype)
            elif epi == "relu2":
                outs[0][...] = r
                rr = jnp.maximum(r, 0.0)
                outs[1][...] = (rr * rr).astype(BF16)
            elif epi == "drelu2":
                outs[0][...] = (r * (2.0 * jnp.maximum(e_ref[...], 0.0))).astype(outs[0].dtype)

    in_specs = [a_spec, b_spec] + ([extra_spec] if extra is not None else [])
    args = (a, b) + ((extra,) if extra is not None else ())
    if n_out == 2:
        out_specs = [o_spec, o_spec]
    else:
        out_specs = o_spec
    return _pcall(body, name=name, grid=grid, in_specs=in_specs, out_specs=out_specs,
                  out_shape=out_shape, scratch_shapes=[pltpu.VMEM(acc_shape, F32)],
                  compiler_params=_params(dimension_semantics=("parallel", "parallel", "arbitrary")))(*args)


def _tm(t):
    return _tile(t, (1408, 768, 384, 128))


def _mm_cols(name, a, wb, lead, out_dtype=F32, epi="plain"):
    t, kdim = a.shape
    n = wb.shape[-1]
    tm, tk = _tm(t), _tile(kdim, (1024, 512))
    nl = len(lead)
    b_spec = pl.BlockSpec((None,) * (1 + nl) + (tk, n), lambda i, j, k: (j,) + lead + (k, 0))
    o_spec = pl.BlockSpec((tm, n), lambda i, j, k: (i, j))
    if epi == "relu2":
        out_shape = [jax.ShapeDtypeStruct((t, N_DEV * n), F32), jax.ShapeDtypeStruct((t, N_DEV * n), BF16)]
    else:
        out_shape = jax.ShapeDtypeStruct((t, N_DEV * n), out_dtype)
    return _matmul(name, a, wb, grid=(t // tm, N_DEV, kdim // tk),
                   a_spec=pl.BlockSpec((tm, tk), lambda i, j, k: (i, k)), b_spec=b_spec, o_spec=o_spec,
                   out_shape=out_shape, contract="nn", acc_shape=(tm, n), epi=epi)


def _tm_deep(t, kdim):
    return _tm(t) if kdim <= 2048 else _tile(t, (704, 384, 128))


def _mm_cols_t(name, a, wb):
    t = a.shape[0]
    nb, kdim, n = wb.shape
    tm, tn = _tm_deep(t, nb * n), _tile(kdim, (512,))

    def body(a_ref, b_ref, o_ref):
        acc = _dot(a_ref[:, 0:n].astype(BF16), b_ref[0], "nt")
        for j in range(1, nb):
            acc = acc + _dot(a_ref[:, j * n:(j + 1) * n].astype(BF16), b_ref[j], "nt")
        o_ref[...] = acc

    return _pcall(body, name=name, grid=(t // tm, kdim // tn),
                  in_specs=[pl.BlockSpec((tm, nb * n), lambda i, j: (i, 0)),
                            pl.BlockSpec((nb, tn, n), lambda i, j: (0, j, 0))],
                  out_specs=pl.BlockSpec((tm, tn), lambda i, j: (i, j)),
                  out_shape=jax.ShapeDtypeStruct((t, kdim), F32),
                  compiler_params=_params(dimension_semantics=("parallel", "parallel")))(a, wb)


def _mm_rows(name, a, wb, residual):
    t = a.shape[0]
    nb, r, n = wb.shape
    tm, tn = _tm_deep(t, nb * r), _tile(n, (512,))

    def body(a_ref, b_ref, r_ref, o_ref):
        o_ref[...] = r_ref[...] + _dot(a_ref[...].astype(BF16), b_ref[...].reshape(nb * r, tn))

    o_spec = pl.BlockSpec((tm, tn), lambda i, j: (i, j))
    return _pcall(body, name=name, grid=(t // tm, n // tn),
                  in_specs=[pl.BlockSpec((tm, nb * r), lambda i, j: (i, 0)),
                            pl.BlockSpec((nb, r, tn), lambda i, j: (0, 0, j)), o_spec],
                  out_specs=o_spec, out_shape=jax.ShapeDtypeStruct((t, n), F32),
                  compiler_params=_params(dimension_semantics=("parallel", "parallel")))(a, wb, residual)


def _mm_rows_t(name, a, wb, lead, out_dtype=F32, epi="plain", extra=None):
    t, n = a.shape
    r = wb.shape[-2]
    tm, tk = _tm(t), _tile(n, (1024,))
    nl = len(lead)
    b_spec = pl.BlockSpec((None,) * (1 + nl) + (r, tk), lambda i, j, k: (j,) + lead + (0, k))
    o_spec = pl.BlockSpec((tm, r), lambda i, j, k: (i, j))
    return _matmul(name, a, wb, grid=(t // tm, N_DEV, n // tk),
                   a_spec=pl.BlockSpec((tm, tk), lambda i, j, k: (i, k)), b_spec=b_spec, o_spec=o_spec,
                   out_shape=jax.ShapeDtypeStruct((t, N_DEV * r), out_dtype), contract="nt",
                   acc_shape=(tm, r), epi=epi, extra=extra, extra_spec=o_spec if extra is not None else None)


def _wgrad_cols(name, x, dy, n):
    t, kdim = x.shape
    tk = _tm(t)
    return _matmul(name, x, dy, grid=(1, N_DEV, t // tk),
                   a_spec=pl.BlockSpec((tk, kdim), lambda i, j, k: (k, 0)),
                   b_spec=pl.BlockSpec((tk, n), lambda i, j, k: (k, j)),
                   o_spec=pl.BlockSpec((None, kdim, n), lambda i, j, k: (j, 0, 0)),
                   out_shape=jax.ShapeDtypeStruct((N_DEV, kdim, n), BF16), contract="tn", acc_shape=(kdim, n))


def _wgrad_rows(name, x, dy, r):
    t = x.shape[0]
    n = dy.shape[1]
    tk, tn = _tm(t), _tile(n, (512,))
    return _matmul(name, x, dy, grid=(N_DEV, n // tn, t // tk),
                   a_spec=pl.BlockSpec((tk, r), lambda i, j, k: (k, i)),
                   b_spec=pl.BlockSpec((tk, tn), lambda i, j, k: (k, j)),
                   o_spec=pl.BlockSpec((None, r, tn), lambda i, j, k: (i, 0, j)),
                   out_shape=jax.ShapeDtypeStruct((N_DEV, r, n), BF16), contract="tn", acc_shape=(r, tn))


def _rows(t):
    return _tile(t, (384, 128))


def _rms_fwd(name, h, g):
    t = h.shape[0]
    tr = _rows(t)

    def body(h_ref, g_ref, o_ref):
        x = h_ref[...]
        r = lax.rsqrt(jnp.mean(x * x, axis=-1, keepdims=True) + EPS)
        o_ref[...] = (x * r * g_ref[...]).astype(BF16)

    row = pl.BlockSpec((tr, D_MODEL), lambda i: (i, 0))
    vec = pl.BlockSpec((1, D_MODEL), lambda i: (0, 0))
    return _pcall(body, name=name, grid=(t // tr,), in_specs=[row, vec], out_specs=row,
                  out_shape=jax.ShapeDtypeStruct((t, D_MODEL), BF16))(h, g)


def _rms_bwd(name, dhn, h, g, dres):
    t = h.shape[0]
    tr = _rows(t)

    def body(d_ref, h_ref, g_ref, r_ref, o_ref, dg_ref):
        @pl.when(pl.program_id(0) == 0)
        def _():
            dg_ref[...] = jnp.zeros_like(dg_ref)

        x = h_ref[...]
        d = d_ref[...]
        r = lax.rsqrt(jnp.mean(x * x, axis=-1, keepdims=True) + EPS)
        u = d * g_ref[...]
        m = jnp.mean(u * x, axis=-1, keepdims=True)
        o_ref[...] = r_ref[...] + r * u - x * (r * r * r * m)
        dg_ref[...] += jnp.sum(d * x * r, axis=0, keepdims=True)

    row = pl.BlockSpec((tr, D_MODEL), lambda i: (i, 0))
    vec = pl.BlockSpec((1, D_MODEL), lambda i: (0, 0))
    return _pcall(body, name=name, grid=(t // tr,), in_specs=[row, row, vec, row], out_specs=[row, vec],
                  out_shape=[jax.ShapeDtypeStruct((t, D_MODEL), F32), jax.ShapeDtypeStruct((1, D_MODEL), F32)])(
                      dhn, h, g, dres)


def _loss_bwd(h, target):
    t = h.shape[0]
    nb = t // CHUNK

    def body(h_ref, t_ref, d_ref, l_ref):
        i = pl.program_id(0)

        @pl.when(i == 0)
        def _():
            d_ref[...] = jnp.zeros_like(d_ref)
            l_ref[...] = jnp.zeros_like(l_ref)

        @pl.when(i > 0)
        def _():
            diff = h_ref[...] - t_ref[...]
            d_ref[...] = diff * (1.0 / D_MODEL)
            l_ref[...] += jnp.sum(diff * diff) * (0.5 / D_MODEL)

    return _pcall(body, name="loss_bwd", grid=(nb,),
                  in_specs=[pl.BlockSpec((CHUNK, D_MODEL), lambda i: (i, 0)),
                            pl.BlockSpec((CHUNK, D_MODEL), lambda i: (jnp.maximum(i - 1, 0), 0))],
                  out_specs=[pl.BlockSpec((CHUNK, D_MODEL), lambda i: (i, 0)),
                             pl.BlockSpec((8, 128), lambda i: (0, 0))],
                  out_shape=[jax.ShapeDtypeStruct((t, D_MODEL), F32), jax.ShapeDtypeStruct((8, 128), F32)])(h, target)


def _ret_tables(t):
    hh = np.arange(RET_HEADS, dtype=np.float64)
    log_g = np.log1p(-np.exp2(-RET_DECAY_OFFSET - hh))
    idx = np.arange(CHUNK, dtype=np.float64)
    diff = idx[:, None] - idx[None, :]
    dmat = np.where(diff[None] >= 0, np.exp(np.maximum(diff, 0.0)[None] * log_g[:, None, None]), 0.0)
    qdec = np.exp((idx + 1.0)[None, :, None] * log_g[:, None, None]) * np.ones((1, 1, CHUNK))
    kdec = np.exp((CHUNK - 1 - idx)[None, :, None] * log_g[:, None, None]) * np.ones((1, 1, CHUNK))
    half = CHUNK // 2
    inv_freq = (ROPE_BASE ** (-np.arange(half, dtype=np.float32) / half)).astype(np.float32)
    ang = (np.arange(t, dtype=np.float32)[:, None] * inv_freq[None, :]).astype(np.float32).astype(np.float64)
    cos2 = np.concatenate([np.cos(ang), np.cos(ang)], axis=1)
    sin2 = np.concatenate([-np.sin(ang), np.sin(ang)], axis=1)
    return tuple(jnp.asarray(v, F32) for v in (dmat, qdec, kdec, cos2, sin2))


def _rot(x, c, s):
    return x * c + pltpu.roll(x, CHUNK // 2, 1) * s


def _unrot(dx, c, s):
    return dx * c + pltpu.roll(dx * s, CHUNK // 2, 1)


def _dot(a, b, contract="nn"):
    return lax.dot_general(a, b, _DIMS[contract], preferred_element_type=F32)


def _ret_fwd(proj, tables):
    t = proj.shape[0]
    nch = t // CHUNK
    dmat, qdec, kdec, cos2, sin2 = tables

    def body(q_ref, k_ref, v_ref, c_ref, s_ref, dm_ref, qd_ref, kd_ref, o_ref, st_ref, state):
        @pl.when(pl.program_id(1) == 0)
        def _():
            state[...] = jnp.zeros_like(state)

        c, s = c_ref[...], s_ref[...]
        q = _rot(q_ref[...], c, s)
        k = _rot(k_ref[...], c, s) * RET_SCALE
        vb = v_ref[...].astype(BF16)
        st = state[...]
        st_ref[...] = st
        sc = _dot(q.astype(BF16), k.astype(BF16), "nt") * dm_ref[...]
        o = _dot(sc.astype(BF16), vb)
        o += _dot((q * qd_ref[...]).astype(BF16), st.astype(BF16))
        o_ref[...] = o
        kv = _dot((k * kd_ref[...]).astype(BF16), vb, "tn")
        state[...] = qd_ref[CHUNK - 1:CHUNK, 0:1] * st + kv

    hd = lambda h, n: (h, 0, 0)
    tab = pl.BlockSpec((None, CHUNK, CHUNK), hd)
    pos = pl.BlockSpec((CHUNK, CHUNK), lambda h, n: (n, 0))
    return _pcall(
        body, name="ret_fwd", grid=(RET_HEADS, nch),
        in_specs=[pl.BlockSpec((CHUNK, 128), lambda h, n: (n, h)),
                  pl.BlockSpec((CHUNK, 128), lambda h, n: (n, RET_HEADS + h)),
                  pl.BlockSpec((CHUNK, 256), lambda h, n: (n, RET_HEADS + h)),
                  pos, pos, tab, tab, tab],
        out_specs=[pl.BlockSpec((CHUNK, 256), lambda h, n: (n, h)),
                   pl.BlockSpec((None, None, 128, 256), lambda h, n: (h, n, 0, 0))],
        out_shape=[jax.ShapeDtypeStruct((t, 1024), F32), jax.ShapeDtypeStruct((RET_HEADS, nch, 128, 256), F32)],
        scratch_shapes=[pltpu.VMEM((128, 256), F32)],
        compiler_params=_params(dimension_semantics=("parallel", "arbitrary")))(
            proj, proj, proj, cos2, sin2, dmat, qdec, kdec)


def _ret_bwd(proj, states, do, tables):
    t = proj.shape[0]
    nch = t // CHUNK
    dmat, qdec, kdec, cos2, sin2 = tables

    def body(q_ref, k_ref, v_ref, do_ref, st_ref, c_ref, s_ref, dm_ref, qd_ref, kd_ref,
             dq_ref, dk_ref, dv_ref, rst):
        @pl.when(pl.program_id(1) == 0)
        def _():
            rst[...] = jnp.zeros_like(rst)

        c, s = c_ref[...], s_ref[...]
        q = _rot(q_ref[...], c, s)
        k = _rot(k_ref[...], c, s) * RET_SCALE
        qb, kb = q.astype(BF16), k.astype(BF16)
        vb = v_ref[...].astype(BF16)
        dob = do_ref[...].astype(BF16)
        pb = st_ref[...].astype(BF16)
        r = rst[...]
        rb = r.astype(BF16)
        dm, qd, kd = dm_ref[...], qd_ref[...], kd_ref[...]
        sb = (_dot(qb, kb, "nt") * dm).astype(BF16)
        dsb = (_dot(dob, vb, "nt") * dm).astype(BF16)
        dq = _dot(dsb, kb) + _dot(dob, pb, "nt") * qd
        dk = _dot(dsb, qb, "tn") + _dot(vb, rb, "nt") * kd
        dv = _dot(sb, dob, "tn") + _dot((k * kd).astype(BF16), rb)
        rst[...] = _dot((q * qd).astype(BF16), dob, "tn") + qd[CHUNK - 1:CHUNK, 0:1] * r
        dq_ref[...] = _unrot(dq, c, s).astype(BF16)
        dk_ref[...] = (_unrot(dk, c, s) * RET_SCALE).astype(BF16)
        dv_ref[...] = dv.astype(BF16)

    rev = lambda n: nch - 1 - n
    tab = pl.BlockSpec((None, CHUNK, CHUNK), lambda h, n: (h, 0, 0))
    pos = pl.BlockSpec((CHUNK, CHUNK), lambda h, n: (rev(n), 0))
    return _pcall(
        body, name="ret_bwd", grid=(RET_HEADS, nch),
        in_specs=[pl.BlockSpec((CHUNK, 128), lambda h, n: (rev(n), h)),
                  pl.BlockSpec((CHUNK, 128), lambda h, n: (rev(n), RET_HEADS + h)),
                  pl.BlockSpec((CHUNK, 256), lambda h, n: (rev(n), RET_HEADS + h)),
                  pl.BlockSpec((CHUNK, 256), lambda h, n: (rev(n), h)),
                  pl.BlockSpec((None, None, 128, 256), lambda h, n: (h, rev(n), 0, 0)),
                  pos, pos, tab, tab, tab],
        out_specs=[pl.BlockSpec((CHUNK, 128), lambda h, n: (rev(n), h)),
                   pl.BlockSpec((CHUNK, 128), lambda h, n: (rev(n), h)),
                   pl.BlockSpec((CHUNK, 256), lambda h, n: (rev(n), h))],
        out_shape=[jax.ShapeDtypeStruct((t, 512), BF16), jax.ShapeDtypeStruct((t, 512), BF16),
                   jax.ShapeDtypeStruct((t, 1024), BF16)],
        scratch_shapes=[pltpu.VMEM((128, 256), F32)],
        compiler_params=_params(dimension_semantics=("parallel", "arbitrary")))(
            proj, proj, proj, do, states, cos2, sin2, dmat, qdec, kdec)


def _gn_gate_fwd(o, proj, gn_g):
    t = o.shape[0]
    tr = _rows(t)

    def body(o_ref, g_ref, w_ref, c_ref):
        for h in range(RET_HEADS):
            sl = slice(256 * h, 256 * (h + 1))
            x = o_ref[:, sl]
            mu = jnp.mean(x, axis=-1, keepdims=True)
            xc = x - mu
            rstd = lax.rsqrt(jnp.mean(xc * xc, axis=-1, keepdims=True) + EPS)
            g = g_ref[:, sl]
            c_ref[:, sl] = (g * _sigmoid(g) * (xc * rstd * w_ref[:, sl])).astype(BF16)

    return _pcall(body, name="gn_gate_fwd", grid=(t // tr,),
                  in_specs=[pl.BlockSpec((tr, 1024), lambda i: (i, 0)),
                            pl.BlockSpec((tr, 1024), lambda i: (i, 2)),
                            pl.BlockSpec((1, 1024), lambda i: (0, 0))],
                  out_specs=pl.BlockSpec((tr, 1024), lambda i: (i, 0)),
                  out_shape=jax.ShapeDtypeStruct((t, 2048), BF16))(o, proj, gn_g)


def _gn_gate_bwd(dcat, o, proj, gn_g):
    t = o.shape[0]
    tr = _rows(t)

    def body(d_ref, o_ref, g_ref, w_ref, do_ref, dg_ref, dw_ref):
        @pl.when(pl.program_id(0) == 0)
        def _():
            dw_ref[...] = jnp.zeros_like(dw_ref)

        for h in range(RET_HEADS):
            sl = slice(256 * h, 256 * (h + 1))
            x = o_ref[:, sl]
            mu = jnp.mean(x, axis=-1, keepdims=True)
            xc = x - mu
            rstd = lax.rsqrt(jnp.mean(xc * xc, axis=-1, keepdims=True) + EPS)
            xh = xc * rstd
            w = w_ref[:, sl]
            g = g_ref[:, sl]
            sg = _sigmoid(g)
            d = d_ref[:, sl]
            don = d * (g * sg)
            dg_ref[:, sl] = (d * (xh * w) * (sg * (1.0 + g * (1.0 - sg)))).astype(BF16)
            dw_ref[:, sl] += jnp.sum(don * xh, axis=0, keepdims=True)
            dxh = don * w
            m1 = jnp.mean(dxh, axis=-1, keepdims=True)
            m2 = jnp.mean(dxh * xh, axis=-1, keepdims=True)
            do_ref[:, sl] = rstd * (dxh - m1 - xh * m2)

    row = pl.BlockSpec((tr, 1024), lambda i: (i, 0))
    vec = pl.BlockSpec((1, 1024), lambda i: (0, 0))
    return _pcall(body, name="gn_gate_bwd", grid=(t // tr,),
                  in_specs=[row, row, pl.BlockSpec((tr, 1024), lambda i: (i, 2)), vec],
                  out_specs=[row, row, vec],
                  out_shape=[jax.ShapeDtypeStruct((t, 1024), F32), jax.ShapeDtypeStruct((t, 1024), BF16),
                             jax.ShapeDtypeStruct((1, 1024), F32)])(dcat, o, proj, gn_g)


def _row_ids(i, tr):
    return i * tr + lax.broadcasted_iota(jnp.int32, (tr, 1), 0)


def _conv_fwd(cat, proj, conv_w, conv_b, ln_g, ln_b):
    t = proj.shape[0]
    tr = _rows(t)
    hb = tr // HALO

    def body(cat_in, ua_ref, ug_ref, pa_ref, pg_ref, w_ref, b_ref, lg_ref, lb_ref, c_ref, hd_ref, y_ref, xs):
        del cat_in
        i = pl.program_id(0)
        hdn = ua_ref[...] * _sigmoid(ug_ref[...])
        hd_ref[...] = hdn
        prev = pa_ref[...] * _sigmoid(pg_ref[...])
        xs[0:HALO, :] = jnp.where(i > 0, prev, 0.0)
        xs[HALO:HALO + tr, :] = hdn
        acc = jnp.zeros((tr, 1024), F32) + b_ref[...]
        for w in range(CONV_WIDTH):
            acc += w_ref[w:w + 1, :] * xs[pl.ds(HALO - (CONV_WIDTH - 1) + w, tr), :]
        y_ref[...] = acc
        mu = jnp.mean(acc, axis=-1, keepdims=True)
        yc = acc - mu
        rstd = lax.rsqrt(jnp.mean(yc * yc, axis=-1, keepdims=True) + EPS)
        yn = yc * rstd * lg_ref[...] + lb_ref[...]
        c = yn * _sigmoid(yn)
        c_ref[...] = jnp.where(_row_ids(i, tr) >= PAD_FRONT, c, 0.0).astype(BF16)

    row = pl.BlockSpec((tr, 1024), lambda i: (i, 0))
    vec = pl.BlockSpec((1, 1024), lambda i: (0, 0))
    halo = lambda col: pl.BlockSpec((HALO, 1024), lambda i: (jnp.maximum(i * hb - 1, 0), col))
    return _pcall(body, name="conv_fwd", grid=(t // tr,),
                  in_specs=[pl.BlockSpec(memory_space=pl.ANY),
                            pl.BlockSpec((tr, 1024), lambda i: (i, 3)), pl.BlockSpec((tr, 1024), lambda i: (i, 4)),
                            halo(3), halo(4), pl.BlockSpec((32, 1024), lambda i: (0, 0)), vec, vec, vec],
                  out_specs=[pl.BlockSpec((tr, 1024), lambda i: (i, 1)), row, row],
                  out_shape=[jax.ShapeDtypeStruct((t, 2048), BF16), jax.ShapeDtypeStruct((t, 1024), F32),
                             jax.ShapeDtypeStruct((t, 1024), F32)],
                  scratch_shapes=[pltpu.VMEM((tr + HALO, 1024), F32)],
                  input_output_aliases={0: 0})(cat, proj, proj, proj, proj, conv_w, conv_b, ln_g, ln_b)


def _conv_bwd_ln(dcat, y, ln_g, ln_b):
    t = y.shape[0]
    tr = _rows(t)

    def body(d_ref, y_ref, lg_ref, lb_ref, dy_ref, dlg_ref, dlb_ref, dcb_ref):
        i = pl.program_id(0)

        @pl.when(i == 0)
        def _():
            dlg_ref[...] = jnp.zeros_like(dlg_ref)
            dlb_ref[...] = jnp.zeros_like(dlb_ref)
            dcb_ref[...] = jnp.zeros_like(dcb_ref)

        y = y_ref[...]
        mu = jnp.mean(y, axis=-1, keepdims=True)
        yc = y - mu
        rstd = lax.rsqrt(jnp.mean(yc * yc, axis=-1, keepdims=True) + EPS)
        xh = yc * rstd
        lg = lg_ref[...]
        yn = xh * lg + lb_ref[...]
        sg = _sigmoid(yn)
        dyn = jnp.where(_row_ids(i, tr) >= PAD_FRONT, d_ref[...] * (sg * (1.0 + yn * (1.0 - sg))), 0.0)
        dlg_ref[...] += jnp.sum(dyn * xh, axis=0, keepdims=True)
        dlb_ref[...] += jnp.sum(dyn, axis=0, keepdims=True)
        dxh = dyn * lg
        m1 = jnp.mean(dxh, axis=-1, keepdims=True)
        m2 = jnp.mean(dxh * xh, axis=-1, keepdims=True)
        dy = rstd * (dxh - m1 - xh * m2)
        dy_ref[...] = dy
        dcb_ref[...] += jnp.sum(dy, axis=0, keepdims=True)

    row = pl.BlockSpec((tr, 1024), lambda i: (i, 0))
    vec = pl.BlockSpec((1, 1024), lambda i: (0, 0))
    vshape = jax.ShapeDtypeStruct((1, 1024), F32)
    return _pcall(body, name="conv_bwd_ln", grid=(t // tr,),
                  in_specs=[pl.BlockSpec((tr, 1024), lambda i: (i, 1)), row, vec, vec],
                  out_specs=[row, vec, vec, vec],
                  out_shape=[jax.ShapeDtypeStruct((t, 1024), F32), vshape, vshape, vshape])(dcat, y, ln_g, ln_b)


def _conv_bwd_taps(dy, hdn, proj, conv_w):
    t = dy.shape[0]
    tr = _rows(t)
    hb = tr // HALO
    nt = t // tr

    def body(dy_ref, nx_ref, hd_ref, ph_ref, ua_ref, ug_ref, w_ref, da_ref, dg_ref, dw_ref, ys, xs):
        i = pl.program_id(0)

        @pl.when(i == 0)
        def _():
            dw_ref[...] = jnp.zeros_like(dw_ref)

        dy = dy_ref[...]
        ys[0:tr, :] = dy
        ys[tr:tr + HALO, :] = jnp.where(i < nt - 1, nx_ref[...], 0.0)
        xs[0:HALO, :] = jnp.where(i > 0, ph_ref[...], 0.0)
        xs[HALO:HALO + tr, :] = hd_ref[...]
        dh = jnp.zeros((tr, 1024), F32)
        for w in range(CONV_WIDTH):
            dh += w_ref[w:w + 1, :] * ys[pl.ds(CONV_WIDTH - 1 - w, tr), :]
            dw_ref[w:w + 1, :] += jnp.sum(dy * xs[pl.ds(HALO - (CONV_WIDTH - 1) + w, tr), :], axis=0, keepdims=True)
        dh = jnp.where(_row_ids(i, tr) >= PAD_FRONT, dh, 0.0)
        sg = _sigmoid(ug_ref[...])
        da_ref[...] = (dh * sg).astype(BF16)
        dg_ref[...] = (dh * ua_ref[...] * sg * (1.0 - sg)).astype(BF16)

    row = pl.BlockSpec((tr, 1024), lambda i: (i, 0))
    return _pcall(body, name="conv_bwd_taps", grid=(nt,),
                  in_specs=[row, pl.BlockSpec((HALO, 1024), lambda i: (jnp.minimum((i + 1) * hb, nt * hb - 1), 0)),
                            row, pl.BlockSpec((HALO, 1024), lambda i: (jnp.maximum(i * hb - 1, 0), 0)),
                            pl.BlockSpec((tr, 1024), lambda i: (i, 3)), pl.BlockSpec((tr, 1024), lambda i: (i, 4)),
                            pl.BlockSpec((32, 1024), lambda i: (0, 0))],
                  out_specs=[row, row, pl.BlockSpec((32, 1024), lambda i: (0, 0))],
                  out_shape=[jax.ShapeDtypeStruct((t, 1024), BF16), jax.ShapeDtypeStruct((t, 1024), BF16),
                             jax.ShapeDtypeStruct((32, 1024), F32)],
                  scratch_shapes=[pltpu.VMEM((tr + HALO, 1024), F32), pltpu.VMEM((tr + HALO, 1024), F32)])(
                      dy, dy, hdn, hdn, proj, proj, conv_w)


NEG_BIG = -1e30


def _seg_tables(qb):
    j = np.arange(128)
    bd = (j[:, None] // 64 == j[None, :] // 64).astype(np.float32)
    ones = np.ones((128, 128), np.float32)
    later = np.concatenate([(j[:, None] >= j[None, :]).astype(np.float32), ones], axis=1)
    earlier = np.concatenate([(j[:, None] < j[None, :]).astype(np.float32), ones], axis=1)
    per = qb // CHUNK
    row = np.arange(qb)[:, None]
    pad = np.broadcast_to(j[None, :] < PAD_FRONT, (qb, 128))
    diag = [(g * CHUNK + j[None, :]) >= row for g in range(per)]
    masks = diag + [np.zeros((qb, 128), bool), pad, diag[0] | pad]
    bias = np.stack([np.where(m, NEG_BIG, 0.0) for m in masks]).astype(np.float32)
    dup = lambda m: np.concatenate([m, m], axis=0)
    return (jnp.asarray(bd, BF16), jnp.asarray(dup(later), BF16), jnp.asarray(dup(earlier), BF16),
            jnp.asarray(bias, F32))


def _split_dot(x, m):
    hi = x.astype(BF16)
    lo = (x - hi.astype(F32)).astype(BF16)
    return _dot(hi, m) + _dot(lo, m)


def _qk_norm_fwd(qkv, qg, kg, bd):
    t = qkv.shape[0]
    tr = _rows(t)
    nb = tr // CHUNK

    def body(q_ref, k_ref, v_ref, qg_ref, kg_ref, bd_ref, qo, kt, k2, vt, v2):
        bdm = bd_ref[...]
        lane = lax.broadcasted_iota(jnp.int32, (1, 128), 1)
        sub = lax.broadcasted_iota(jnp.int32, (128, 1), 0)

        def pair_layouts(x, t_ref, s_ref, hp, b):
            xt = x.T
            t_ref[hp, b] = jnp.concatenate([jnp.where(sub < 64, xt, 0.0), jnp.where(sub >= 64, xt, 0.0)],
                                           axis=1).astype(BF16)
            s_ref[hp, b] = jnp.concatenate([jnp.where(lane < 64, x, 0.0), jnp.where(lane >= 64, x, 0.0)],
                                           axis=0).astype(BF16)

        for hp in range(8):
            sl = slice(128 * hp, 128 * (hp + 1))
            x = q_ref[:, sl]
            r = lax.rsqrt(_split_dot(x * x, bdm) * (1.0 / 64) + EPS)
            qo[:, sl] = (x * r * (qg_ref[:, sl] * SB_SCALE)).astype(BF16)
            x = k_ref[:, sl]
            r = lax.rsqrt(_split_dot(x * x, bdm) * (1.0 / 64) + EPS)
            kn = x * r * kg_ref[:, sl]
            v = v_ref[:, sl]
            for b in range(nb):
                rows = slice(CHUNK * b, CHUNK * (b + 1))
                pair_layouts(kn[rows], kt, k2, hp, b)
                pair_layouts(v[rows], vt, v2, hp, b)

    col = lambda c: pl.BlockSpec((tr, 1024), lambda i: (i, c))
    vec = pl.BlockSpec((1, 1024), lambda i: (0, 0))
    wide = pl.BlockSpec((8, nb, 128, 256), lambda i: (0, i, 0, 0))
    tall = pl.BlockSpec((8, nb, 256, 128), lambda i: (0, i, 0, 0))
    wsh = jax.ShapeDtypeStruct((8, t // CHUNK, 128, 256), BF16)
    tsh = jax.ShapeDtypeStruct((8, t // CHUNK, 256, 128), BF16)
    return _pcall(body, name="qk_norm_fwd", grid=(t // tr,),
                  in_specs=[col(0), col(1), col(2), vec, vec, pl.BlockSpec((128, 128), lambda i: (0, 0))],
                  out_specs=[col(0), wide, tall, wide, tall],
                  out_shape=[jax.ShapeDtypeStruct((t, 1024), BF16), wsh, tsh, wsh, tsh])(qkv, qkv, qkv, qg, kg, bd)


def _qk_norm_bwd(qkv, dq, dk, dv, qg, kg, bd):
    t = qkv.shape[0]
    tr = _rows(t)

    def body(q_ref, k_ref, dq_ref, dk_ref, dv_ref, qg_ref, kg_ref, bd_ref, o_ref, dqg_ref, dkg_ref):
        @pl.when(pl.program_id(0) == 0)
        def _():
            dqg_ref[...] = jnp.zeros_like(dqg_ref)
            dkg_ref[...] = jnp.zeros_like(dkg_ref)

        bdm = bd_ref[...]
        for part, (src, d_ref, g_ref, dg_ref) in enumerate(((q_ref, dq_ref, qg_ref, dqg_ref),
                                                           (k_ref, dk_ref, kg_ref, dkg_ref))):
            for cix in range(8):
                sl = slice(128 * cix, 128 * (cix + 1))
                x = src[:, sl]
                d = d_ref[:, sl]
                r = lax.rsqrt(_split_dot(x * x, bdm) * (1.0 / 64) + EPS)
                u = d * g_ref[:, sl]
                m = _split_dot(u * x, bdm) * (1.0 / 64)
                o_ref[:, 1024 * part + 128 * cix:1024 * part + 128 * (cix + 1)] = (r * u - x * (r * r * r * m)).astype(BF16)
                dg_ref[:, sl] += jnp.sum(d * x * r, axis=0, keepdims=True)
        o_ref[:, 2048:3072] = dv_ref[...].astype(BF16)

    col = lambda c: pl.BlockSpec((tr, 1024), lambda i: (i, c))
    vec = pl.BlockSpec((1, 1024), lambda i: (0, 0))
    vsh = jax.ShapeDtypeStruct((1, 1024), F32)
    return _pcall(body, name="qk_norm_bwd", grid=(t // tr,),
                  in_specs=[col(0), col(1), col(0), col(0), col(0), vec, vec, pl.BlockSpec((128, 128), lambda i: (0, 0))],
                  out_specs=[pl.BlockSpec((tr, 3072), lambda i: (i, 0)), vec, vec],
                  out_shape=[jax.ShapeDtypeStruct((t, 3072), BF16), vsh, vsh])(qkv, qkv, dq, dk, dv, qg, kg, bd)


def _split2(x):
    hi = x.astype(BF16)
    lo = (x - hi.astype(F32)).astype(BF16)
    return jnp.concatenate([hi, lo], axis=1)


def _sb_scores(z, later_tab):
    e = jnp.exp(-jnp.abs(z))
    ope = 1.0 + e
    sp = jnp.maximum(z, 0.0) + jnp.log(ope)
    return e, ope, _dot(_split2(sp), later_tab)


def _sb_bias_index(i, kb, per):
    g = kb - i * per
    return jnp.where(kb == 0, jnp.where(i == 0, per + 2, per + 1), jnp.where(g >= 0, g, per))


def _sb_qb(t):
    return _tile(t, (384, 128))


def _sb_fwd(qh, kt, v2, later_tab, bias_tab):
    t = qh.shape[0]
    qb = _sb_qb(t)
    per = qb // CHUNK
    nkb_all = t // CHUNK
    zero_slot = nkb_all

    def body(q_ref, kt_ref, v2_ref, tab_ref, bias_ref, o_ref, ws_ref, acc, carry, zbuf, zk, cub, wbuf, wsem):
        h, i = pl.program_id(0), pl.program_id(1)
        q = q_ref[...]
        acc[...] = jnp.zeros_like(acc)
        carry[...] = jnp.zeros_like(carry)
        nkb = (i + 1) * per
        save = lambda kb: pltpu.make_async_copy(wbuf.at[kb], ws_ref.at[h, i, kb], wsem.at[kb])

        def sums(z2, kb):
            bias = bias_ref[_sb_bias_index(i, kb, per)]
            for hh in range(2):
                sl = slice(128 * hh, 128 * (hh + 1))
                z = z2[:, sl] + bias
                zk[:, sl] = z
                cub[hh] = _sb_scores(z, tab_ref[...])[2]

        def weights(kb):
            for hh in range(2):
                sl = slice(128 * hh, 128 * (hh + 1))
                cu = cub[hh]
                cin = carry[hh]
                wbuf[kb, :, sl] = jnp.exp(zk[:, sl] - cu[:, :128] - cin).astype(BF16)
                carry[hh] = cin + cu[:, 128:]

        def output(slot, kb):
            acc[...] += _dot(wbuf[slot], v2_ref[kb])

        sums(_dot(q, kt_ref[nkb - 1]), nkb - 1)
        zbuf[...] = _dot(q, kt_ref[jnp.maximum(nkb - 2, 0)])
        wbuf[zero_slot] = jnp.zeros((qb, 256), BF16)

        def step(s, _):
            kb = nkb - 1 - s

            @pl.when(s > 0)
            def _():
                save(kb + 1).start()

            z2 = zbuf[...]
            zbuf[...] = _dot(q, kt_ref[jnp.maximum(kb - 2, 0)])
            output(jnp.where(s == 0, zero_slot, kb + 1), jnp.minimum(kb + 1, nkb - 1))
            weights(kb)
            sums(z2, kb - 1)
            return 0

        lax.fori_loop(0, nkb - 1, step, 0)

        @pl.when(nkb > 1)
        def _():
            save(1).start()

        output(jnp.where(nkb == 1, zero_slot, 1), jnp.minimum(1, nkb - 1))
        weights(0)
        output(0, 0)
        save(0).start()
        o_ref[...] = acc[...]

        def drain(kb, _):
            save(kb).wait()
            return 0

        lax.fori_loop(0, nkb, drain, 0)

    blk = pl.BlockSpec((qb, 128), lambda h, i: (i, h))
    wide = pl.BlockSpec((None, nkb_all, 128, 256), lambda h, i: (h, 0, 0, 0))
    tall = pl.BlockSpec((None, nkb_all, 256, 128), lambda h, i: (h, 0, 0, 0))
    return _pcall(body, name="sb_fwd", grid=(8, t // qb),
                  in_specs=[blk, wide, tall, pl.BlockSpec((256, 256), lambda h, i: (0, 0)),
                            pl.BlockSpec((per + 3, qb, 128), lambda h, i: (0, 0, 0))],
                  out_specs=[blk, pl.BlockSpec(memory_space=pl.ANY)],
                  out_shape=[jax.ShapeDtypeStruct((t, 1024), F32),
                             jax.ShapeDtypeStruct((8, t // qb, nkb_all, qb, 256), BF16)],
                  scratch_shapes=[pltpu.VMEM((qb, 128), F32), pltpu.VMEM((2, qb, 128), F32),
                                  pltpu.VMEM((qb, 256), F32), pltpu.VMEM((qb, 256), F32),
                                  pltpu.VMEM((2, qb, 256), F32), pltpu.VMEM((nkb_all + 1, qb, 256), BF16),
                                  pltpu.SemaphoreType.DMA((nkb_all,))],
                  compiler_params=_params(dimension_semantics=("parallel", "arbitrary")))(
                      qh, kt, v2, later_tab, bias_tab)


def _sb_bwd(qh, kt, k2, vt, wsave, do, earlier_tab, bias_tab):
    t = qh.shape[0]
    qb = _sb_qb(t)
    per = qb // CHUNK
    nkb_all = t // CHUNK

    zero_slot = nkb_all

    def body(q_ref, kt_ref, k2_ref, vt_ref, ws_ref, do_ref, etab_ref, bias_ref,
             dq_ref, dk_ref, dv_ref, acc, gcarry, zbuf, dwbuf, wbuf, wsem, dzbuf):
        h, i = pl.program_id(0), pl.program_id(1)

        @pl.when(i == 0)
        def _():
            dk_ref[...] = jnp.zeros_like(dk_ref)
            dv_ref[...] = jnp.zeros_like(dv_ref)

        nkb = (i + 1) * per
        fetch = lambda kb: pltpu.make_async_copy(ws_ref.at[h, i, kb], wbuf.at[kb], wsem.at[kb])

        def prefetch(kb, _):
            fetch(kb).start()
            return 0

        lax.fori_loop(0, nkb, prefetch, 0)
        q = q_ref[...]
        dob = do_ref[...].astype(BF16)
        lane = lax.broadcasted_iota(jnp.int32, (1, 128), 1)
        acc[...] = jnp.zeros_like(acc)
        gcarry[...] = jnp.zeros_like(gcarry)
        zbuf[...] = _dot(q, kt_ref[0])
        dwbuf[...] = _dot(dob, vt_ref[0])
        dzbuf[...] = jnp.zeros_like(dzbuf)
        wbuf[zero_slot] = jnp.zeros((qb, 256), BF16)

        def gradients(slot, kb):
            dz2 = dzbuf[...]
            acc[...] += _dot(dz2, k2_ref[kb])
            dk2 = _dot(dz2, q, "tn")
            dv2 = _dot(wbuf[slot], dob, "tn")
            dk_ref[kb] += jnp.where(lane < 64, dk2[:128], dk2[128:])
            dv_ref[kb] += jnp.where(lane < 64, dv2[:128], dv2[128:])

        def step(kb, _):
            fetch(kb).wait()
            bias = bias_ref[_sb_bias_index(i, kb, per)]
            z2 = zbuf[...]
            dw2 = dwbuf[...]
            nxt = jnp.minimum(kb + 1, nkb - 1)
            zbuf[...] = _dot(q, kt_ref[nxt])
            dwbuf[...] = _dot(dob, vt_ref[nxt])
            gradients(jnp.where(kb == 0, zero_slot, kb - 1), jnp.maximum(kb - 1, 0))
            w2 = wbuf[kb]
            for hh in range(2):
                sl = slice(128 * hh, 128 * (hh + 1))
                z = z2[:, sl] + bias
                e = jnp.exp(-jnp.abs(z))
                r = 1.0 / (1.0 + e)
                sig = jnp.where(z >= 0, r, e * r)
                gw = w2[:, sl].astype(F32) * dw2[:, sl]
                cu2 = _dot(_split2(gw), etab_ref[...])
                gin = gcarry[hh]
                gcarry[hh] = gin + cu2[:, 128:]
                dzbuf[:, sl] = (gw - sig * (gw + cu2[:, :128] + gin)).astype(BF16)
            return 0

        lax.fori_loop(0, nkb, step, 0)
        gradients(nkb - 1, nkb - 1)
        dq_ref[...] = acc[...] * SB_SCALE

    blk = pl.BlockSpec((qb, 128), lambda h, i: (i, h))
    wide = pl.BlockSpec((None, nkb_all, 128, 256), lambda h, i: (h, 0, 0, 0))
    tall = pl.BlockSpec((None, nkb_all, 256, 128), lambda h, i: (h, 0, 0, 0))
    tab = pl.BlockSpec((256, 256), lambda h, i: (0, 0))
    kv_out = pl.BlockSpec((nkb_all, 128, 128), lambda h, i: (0, 0, h))
    ksh = jax.ShapeDtypeStruct((nkb_all, 128, 1024), F32)
    dq, dk, dv = _pcall(
        body, name="sb_bwd", grid=(8, t // qb),
        in_specs=[blk, wide, tall, wide, pl.BlockSpec(memory_space=pl.ANY), blk, tab,
                  pl.BlockSpec((per + 3, qb, 128), lambda h, i: (0, 0, 0))],
        out_specs=[blk, kv_out, kv_out], out_shape=[jax.ShapeDtypeStruct((t, 1024), F32), ksh, ksh],
        scratch_shapes=[pltpu.VMEM((qb, 128), F32), pltpu.VMEM((2, qb, 128), F32),
                        pltpu.VMEM((qb, 256), F32), pltpu.VMEM((qb, 256), F32),
                        pltpu.VMEM((nkb_all + 1, qb, 256), BF16), pltpu.SemaphoreType.DMA((nkb_all,)),
                        pltpu.VMEM((qb, 256), BF16)],
        compiler_params=_params(dimension_semantics=("parallel", "arbitrary")))(
            qh, kt, k2, vt, wsave, do, earlier_tab, bias_tab)
    return dq, dk.reshape(t, 1024), dv.reshape(t, 1024)


def _adamw_math(w, g, m, v):
    m = ADAM_B1 * m + (1.0 - ADAM_B1) * g
    v = ADAM_B2 * v + (1.0 - ADAM_B2) * (g * g)
    m_hat = m / (1.0 - ADAM_B1 ** ADAM_STEP)
    v_hat = v / (1.0 - ADAM_B2 ** ADAM_STEP)
    delta = -ADAM_LR * (m_hat / (jnp.sqrt(v_hat) + ADAM_EPS) + ADAM_WD * w)
    return delta, m, v


def _adamw(name, w, owns, recvs, m, v, me):
    shape = w.shape
    c = shape[-1]
    nl = len(owns)
    w3, m3, v3 = (a.reshape(nl, -1, c) for a in (w, m, v))
    r = w3.shape[1]
    tr = _tile(r, (256, 128))
    owns = [o.reshape(N_DEV, r, c) for o in owns]
    recvs = [p.reshape(N_DEV - 1, r, c) for p in recvs]

    def body(me_ref, w_ref, *rest):
        own_refs, recv_refs = rest[:nl], rest[nl:2 * nl]
        m_ref, v_ref = rest[2 * nl:2 * nl + 2]
        g_out, d_out, m_out, v_out = rest[2 * nl + 2:]
        layer = pl.program_id(0)

        def grad(k):
            g = own_refs[k][...].astype(F32)
            for s in range(N_DEV - 1):
                g = g + recv_refs[k][s].astype(F32)
            return g

        g = grad(0)
        for k in range(1, nl):
            g = jnp.where(layer == k, grad(k), g)
        d, mn, vn = _adamw_math(w_ref[...], g, m_ref[...], v_ref[...])
        g_out[...] = g
        d_out[...] = d
        m_out[...] = mn
        v_out[...] = vn

    row = pl.BlockSpec((None, tr, c), lambda l, i, me_ref: (l, i, 0))
    own = lambda k: pl.BlockSpec((None, tr, c), lambda l, i, me_ref: (me_ref[0], jnp.where(l == k, i, 0), 0))
    rcv = lambda k: pl.BlockSpec((N_DEV - 1, tr, c), lambda l, i, me_ref: (0, jnp.where(l == k, i, 0), 0))
    osh = jax.ShapeDtypeStruct((nl, r, c), F32)
    grid_spec = pltpu.PrefetchScalarGridSpec(
        num_scalar_prefetch=1, grid=(nl, r // tr),
        in_specs=[row] + [own(k) for k in range(nl)] + [rcv(k) for k in range(nl)] + [row, row],
        out_specs=[row, row, row, row])
    outs = _pcall(body, name=name, grid_spec=grid_spec, out_shape=[osh, osh, osh, osh])(
        me.reshape(1), w3, *owns, *recvs, m3, v3)
    return tuple(o.reshape(shape) for o in outs)


def _place():
    x, y, c = lax.axis_index("x"), lax.axis_index("y"), lax.axis_index("c")
    return x, y, c, 4 * x + 2 * y + c


def _peer(x, y, c, rel):
    return (x ^ ((rel >> 2) & 1), y ^ ((rel >> 1) & 1), c ^ (rel & 1))


def _gather_first(now, later):
    n, k = len(now), len(later)

    def body(*refs):
        ins, outs = refs[:n + k], refs[n + k:2 * (n + k)]
        send, recv, lsem = refs[2 * (n + k):]
        x, y, c, me = _place()
        locals_ = []
        for w in range(n + k):
            local = pltpu.make_async_copy(ins[w], outs[w].at[me], lsem.at[w])
            local.start()
            locals_.append(local)
        for w in range(n):
            for rel in range(1, N_DEV):
                pltpu.make_async_remote_copy(src_ref=ins[w], dst_ref=outs[w].at[me], send_sem=send.at[w, rel - 1],
                                             recv_sem=recv.at[w, rel - 1], device_id=_peer(x, y, c, rel),
                                             device_id_type=MESH).start()
        for w in range(n):
            for rel in range(1, N_DEV):
                cp = pltpu.make_async_remote_copy(src_ref=ins[w], dst_ref=outs[w].at[me ^ rel],
                                                  send_sem=send.at[w, rel - 1], recv_sem=recv.at[w, rel - 1],
                                                  device_id=_peer(x, y, c, rel), device_id_type=MESH)
                cp.wait_send()
                cp.wait_recv()
        for local in locals_:
            local.wait()

    hbm = pl.BlockSpec(memory_space=pl.ANY)
    arrays = list(now) + list(later)
    return _pcall(body, name="gather_first", in_specs=[hbm] * (n + k), out_specs=[hbm] * (n + k),
                  out_shape=[jax.ShapeDtypeStruct((N_DEV,) + a.shape, a.dtype) for a in arrays],
                  scratch_shapes=[pltpu.SemaphoreType.DMA((n, N_DEV - 1)), pltpu.SemaphoreType.DMA((n, N_DEV - 1)),
                                  pltpu.SemaphoreType.DMA((n + k,))],
                  compiler_params=_params(has_side_effects=True))(*arrays)


_HBM = pl.BlockSpec(memory_space=pltpu.HBM)
_SEM = pl.BlockSpec(memory_space=pltpu.SEMAPHORE)
_DATAFLOW = pltpu.SideEffectType.DATAFLOW_SIDE_EFFECTING


def _exchange_refs(srcs, lands, mode, me, rel, j):
    if mode == "gather":
        return srcs[j], lands[j].at[me], lands[j].at[me ^ rel]
    return srcs[j].at[me ^ rel], lands[j].at[rel - 1], lands[j].at[rel - 1]


def _exchange_start(name, srcs, lands, mode):
    n = len(srcs)

    def body(*refs):
        ins, lnd = refs[:n], refs[n:2 * n]
        send, recv = refs[2 * n], refs[2 * n + 1]
        token = refs[-1]
        x, y, c, me = _place()
        for j in range(n):
            for rel in range(1, N_DEV):
                src, dst, _ = _exchange_refs(ins, lnd, mode, me, rel, j)
                pltpu.make_async_remote_copy(src_ref=src, dst_ref=dst, send_sem=send.at[j * (N_DEV - 1) + rel - 1],
                                             recv_sem=recv.at[j * (N_DEV - 1) + rel - 1],
                                             device_id=_peer(x, y, c, rel), device_id_type=MESH).start()
        token[...] = jnp.zeros_like(token)

    sems = pltpu.SemaphoreType.DMA((n * (N_DEV - 1),))
    hbm_like = lambda a: pltpu.HBM(a.shape, a.dtype)
    outs = _pcall(body, name=name + "_start",
                  in_specs=[_HBM] * (2 * n), out_specs=[_SEM, _SEM] + [_HBM] * (2 * n) + [pl.BlockSpec(memory_space=pltpu.VMEM)],
                  out_shape=[sems, sems] + [hbm_like(a) for a in srcs] + [hbm_like(a) for a in lands]
                  + [jax.ShapeDtypeStruct((8, 128), F32)],
                  input_output_aliases={i: 2 + i for i in range(2 * n)},
                  compiler_params=pltpu.CompilerParams(has_side_effects=_DATAFLOW))(
                      *[pltpu.with_memory_space_constraint(a, pltpu.HBM) for a in list(srcs) + list(lands)])
    return dict(name=name, mode=mode, n=n, send=outs[0], recv=outs[1], srcs=outs[2:2 + n], lands=outs[2 + n:2 + 2 * n],
                token=outs[-1][0, 0])


def _exchange_wait(ex, after):
    n, mode = ex["n"], ex["mode"]

    def body(*refs):
        ins, lnd = refs[:n], refs[n:2 * n]
        send, recv = refs[2 * n], refs[2 * n + 1]
        x, y, c, me = _place()
        for j in range(n):
            for rel in range(1, N_DEV):
                src, dst, landed = _exchange_refs(ins, lnd, mode, me, rel, j)
                pltpu.make_async_remote_copy(src_ref=src, dst_ref=dst, send_sem=send.at[j * (N_DEV - 1) + rel - 1],
                                             recv_sem=recv.at[j * (N_DEV - 1) + rel - 1],
                                             device_id=_peer(x, y, c, rel), device_id_type=MESH).wait_send()
                pltpu.make_async_remote_copy(src_ref=src, dst_ref=landed, send_sem=send.at[j * (N_DEV - 1) + rel - 1],
                                             recv_sem=recv.at[j * (N_DEV - 1) + rel - 1],
                                             device_id=_peer(x, y, c, rel), device_id_type=MESH).wait_recv()

    hbm_like = lambda a: pltpu.HBM(a.shape, a.dtype)
    arrays = list(ex["srcs"]) + list(ex["lands"])
    outs = _pcall(body, name=ex["name"] + "_wait",
                  in_specs=[_HBM] * (2 * n) + [_SEM, _SEM, pl.BlockSpec(memory_space=pl.ANY)],
                  out_specs=[_HBM] * (2 * n), out_shape=[hbm_like(a) for a in arrays],
                  input_output_aliases={i: i for i in range(2 * n)},
                  compiler_params=pltpu.CompilerParams(has_side_effects=_DATAFLOW))(
                      *arrays, ex["send"], ex["recv"], after)
    return outs[:n], outs[n:]


def _scatter_start(name, grads):
    lands = [lax.empty((N_DEV - 1,) + g.shape[1:], g.dtype) for g in grads]
    return _exchange_start(name, grads, lands, "scatter")


ROW_MIX, ROW_MLP, ROW_CB, ROW_LG, ROW_LB, ROW_QN, ROW_KN, ROW_LOSS = 0, 2, 4, 5, 6, 7, 8, 9
ROW_META, ROW_CW, ROW_GN, SMALL_ROWS = 16, 32, 64, 72


def _allreduce_small(part):
    def body(p_ref, o_ref, slots, send, recv):
        x, y, c, me = _place()
        slots[me] = p_ref[...]
        for rel in range(1, N_DEV):
            pltpu.make_async_remote_copy(src_ref=p_ref, dst_ref=slots.at[me], send_sem=send.at[rel - 1],
                                         recv_sem=recv.at[rel - 1], device_id=_peer(x, y, c, rel),
                                         device_id_type=MESH).start()
        for rel in range(1, N_DEV):
            cp = pltpu.make_async_remote_copy(src_ref=p_ref, dst_ref=slots.at[me ^ rel], send_sem=send.at[rel - 1],
                                              recv_sem=recv.at[rel - 1], device_id=_peer(x, y, c, rel),
                                              device_id_type=MESH)
            cp.wait_send()
            cp.wait_recv()
        tot = slots[0]
        for s in range(1, N_DEV):
            tot = tot + slots[s]
        o_ref[...] = tot
        for row in (ROW_QN, ROW_KN):
            v = tot[row:row + 1, :]
            f = v[:, 0:128]
            for k in range(1, 8):
                f = f + v[:, 128 * k:128 * (k + 1)]
            o_ref[row:row + 1, 0:64] = f[:, 0:64] + f[:, 64:128]

    vm = pl.BlockSpec(memory_space=pltpu.VMEM)
    return _pcall(body, name="allreduce_small", in_specs=[vm], out_specs=vm,
                  out_shape=jax.ShapeDtypeStruct(part.shape, F32),
                  scratch_shapes=[pltpu.VMEM((N_DEV,) + part.shape, F32), pltpu.SemaphoreType.DMA((N_DEV - 1,)),
                                  pltpu.SemaphoreType.DMA((N_DEV - 1,))],
                  compiler_params=_params(has_side_effects=True))(part)


def _adamw_small(w, g, m, v):
    def body(w_ref, g_ref, m_ref, v_ref, d_out, m_out, v_out):
        d, mn, vn = _adamw_math(w_ref[...], g_ref[...], m_ref[...], v_ref[...])
        d_out[...] = d
        m_out[...] = mn
        v_out[...] = vn

    osh = jax.ShapeDtypeStruct(w.shape, F32)
    return _pcall(body, name="adamw_small", out_shape=[osh, osh, osh])(w, g, m, v)


def _local_step(h0, target, p, weight, emit):
    t = h0.shape[0]
    tables = _ret_tables(t)
    bd, later_tab, earlier_tab, bias_tab = _seg_tables(_sb_qb(t))
    row = lambda a, i: a[i:i + 1]

    hn_a = _rms_fwd("rms_mix0", h0, row(p["norm_mix_g"], 0))
    w_in = weight("w_in", hn_a)
    proj = _mm_cols("proj_in", hn_a, w_in, ())
    o_ret, states = _ret_fwd(proj, tables)
    gn_flat = p["gn_g"].reshape(1, 1024)
    cat = _gn_gate_fwd(o_ret, proj, gn_flat)
    cat, hdn, ycv = _conv_fwd(cat, proj, p["conv_w"], p["conv_b"], p["ln_g"], p["ln_b"])
    w_out = weight("w_out", cat)
    h1 = _mm_rows("mix_out", cat, w_out, h0)
    hn_b = _rms_fwd("rms_mlp0", h1, row(p["norm_mlp_g"], 0))
    w1_0, w2_0 = weight("w1_0", hn_b), weight("w2_0", hn_b)
    a0, s0 = _mm_cols("mlp0_up", hn_b, w1_0, (), epi="relu2")
    h2 = _mm_rows("mlp0_down", s0, w2_0, h1)

    hn_c = _rms_fwd("rms_mix1", h2, row(p["norm_mix_g"], 1))
    w_qkv = weight("w_qkv", hn_c)
    qkv = _mm_cols("qkv", hn_c, w_qkv, ())
    qg = jnp.tile(p["qn_g"], (1, 16))
    kg = jnp.tile(p["kn_g"], (1, 16))
    qh, kt, k2, vt, v2 = _qk_norm_fwd(qkv, qg, kg, bd)
    o_sb, w_sb = _sb_fwd(qh, kt, v2, later_tab, bias_tab)
    w_o = weight("w_o", o_sb)
    h3 = _mm_rows("attn_out", o_sb, w_o, h2)
    hn_d = _rms_fwd("rms_mlp1", h3, row(p["norm_mlp_g"], 1))
    w1_1, w2_1 = weight("w1_1", hn_d), weight("w2_1", hn_d)
    a1, s1 = _mm_cols("mlp1_up", hn_d, w1_1, (), epi="relu2")
    h4 = _mm_rows("mlp1_down", s1, w2_1, h3)

    dh, loss = _loss_bwd(h4, target)

    def mlp_bwd(tag, layer, w1, w2, dh, h_in, hn, a, s):
        da = _mm_rows_t(f"{tag}_dact", dh, w2, (), out_dtype=BF16, epi="drelu2", extra=a)
        dw2 = _wgrad_rows(f"{tag}_dw2", s, dh, 512)
        dw1 = _wgrad_cols(f"{tag}_dw1", hn, da, 512)
        tok = emit(tag, [dw1, dw2])
        dhn = _mm_cols_t(f"{tag}_dhn", da, w1)
        return _rms_bwd(f"{tag}_rms_bwd", dhn, h_in, row(p["norm_mlp_g"], layer) + tok, dh)

    dh, dg_mlp1 = mlp_bwd("mlp1", 1, w1_1, w2_1, dh, h3, hn_d, a1, s1)

    do_sb = _mm_rows_t("attn_dout", dh, w_o, ())
    dw_o = _wgrad_rows("attn_dwo", o_sb, dh, 128)
    dq, dk, dv = _sb_bwd(qh, kt, k2, vt, w_sb, do_sb, earlier_tab, bias_tab)
    dqkv, dqg, dkg = _qk_norm_bwd(qkv, dq, dk, dv, qg, kg, bd)
    dw_qkv = _wgrad_cols("qkv_dw", hn_c, dqkv, 384)
    tok = emit("attn", [dw_qkv, dw_o])
    dhn = _mm_cols_t("qkv_dhn", dqkv, w_qkv)
    dh, dg_mix1 = _rms_bwd("mix1_rms_bwd", dhn, h2, row(p["norm_mix_g"], 1) + tok, dh)

    dh, dg_mlp0 = mlp_bwd("mlp0", 0, w1_0, w2_0, dh, h1, hn_b, a0, s0)

    dcat = _mm_rows_t("mix_dcat", dh, w_out, ())
    dw_out = _wgrad_rows("mix_dwout", cat, dh, 256)
    do_ret, dgate, dgn = _gn_gate_bwd(dcat, o_ret, proj, gn_flat)
    dq_r, dk_r, dv_r = _ret_bwd(proj, states, do_ret, tables)
    dy, dlg, dlb, dcb = _conv_bwd_ln(dcat, ycv, p["ln_g"], p["ln_b"])
    dua, dug, dcw = _conv_bwd_taps(dy, hdn, proj, p["conv_w"])
    dproj = jnp.concatenate([dq_r, dk_r, dv_r, dgate, dua, dug], axis=1)
    dw_in = _wgrad_cols("proj_dw", hn_a, dproj, 640)
    tok = emit("mix0", [dw_in, dw_out])
    dhn = _mm_cols_t("proj_dhn", dproj, w_in)
    dh, dg_mix0 = _rms_bwd("mix0_rms_bwd", dhn, h0, row(p["norm_mix_g"], 0) + tok, dh)

    rid = lax.broadcasted_iota(jnp.int32, (16, 1), 0)
    loss_row = jnp.broadcast_to(loss[0:1, 0:1], (1, D_MODEL))
    vecs = sum(jnp.where(rid == k, v, 0.0)
               for k, v in enumerate((dg_mix0, dg_mix1, dg_mlp0, dg_mlp1, dcb, dlg, dlb, dqg, dkg, loss_row)))
    small = jnp.concatenate([vecs, dh[PAD_FRONT:TOK0], dcw, jnp.where(rid[:8] == 0, dgn, 0.0)], axis=0)
    return dh[TOK0:], small


_SMALL_NAMES = ("meta", "norm_mix_g", "norm_mlp_g", "even_ret_gn_g", "even_conv_w", "even_conv_b",
                "even_conv_ln_g", "even_conv_ln_b", "odd_q_norm_g", "odd_k_norm_g")
_BIG_NAMES = ("even_w_in", "even_w_out", "odd_w_qkv", "odd_w_o", "mlp_w1", "mlp_w2")
_ORDER = ("meta", "norm_mix_g", "norm_mlp_g", "even_w_in", "even_ret_gn_g", "even_conv_w", "even_conv_b",
          "even_conv_ln_g", "even_conv_ln_b", "even_w_out", "odd_w_qkv", "odd_q_norm_g", "odd_k_norm_g",
          "odd_w_o", "mlp_w1", "mlp_w2")


def _pack128(a):
    flat = a.reshape(-1)
    n = flat.shape[0]
    rows = -(-n // 128)
    rows8 = -(-rows // 8) * 8
    return jnp.pad(flat, (0, rows8 * 128 - n)).reshape(rows8, 128)


def kernel(x, meta, norm_mix_g, norm_mlp_g, even_w_in, even_ret_gn_g, even_conv_w, even_conv_b, even_conv_ln_g, even_conv_ln_b, even_w_out, odd_w_qkv, odd_q_norm_g, odd_k_norm_g, odd_w_o, mlp_w1, mlp_w2, loss_target, m_meta, m_norm_mix_g, m_norm_mlp_g, m_even_w_in, m_even_ret_gn_g, m_even_conv_w, m_even_conv_b, m_even_conv_ln_g, m_even_conv_ln_b, m_even_w_out, m_odd_w_qkv, m_odd_q_norm_g, m_odd_k_norm_g, m_odd_w_o, m_mlp_w1, m_mlp_w2, v_meta, v_norm_mix_g, v_norm_mlp_g, v_even_w_in, v_even_ret_gn_g, v_even_conv_w, v_even_conv_b, v_even_conv_ln_g, v_even_conv_ln_b, v_even_w_out, v_odd_w_qkv, v_odd_q_norm_g, v_odd_k_norm_g, v_odd_w_o, v_mlp_w1, v_mlp_w2):
    w = dict(meta=meta, norm_mix_g=norm_mix_g, norm_mlp_g=norm_mlp_g, even_w_in=even_w_in,
             even_ret_gn_g=even_ret_gn_g, even_conv_w=even_conv_w, even_conv_b=even_conv_b,
             even_conv_ln_g=even_conv_ln_g, even_conv_ln_b=even_conv_ln_b, even_w_out=even_w_out,
             odd_w_qkv=odd_w_qkv, odd_q_norm_g=odd_q_norm_g, odd_k_norm_g=odd_k_norm_g, odd_w_o=odd_w_o,
             mlp_w1=mlp_w1, mlp_w2=mlp_w2)
    mom = dict(meta=m_meta, norm_mix_g=m_norm_mix_g, norm_mlp_g=m_norm_mlp_g, even_w_in=m_even_w_in,
               even_ret_gn_g=m_even_ret_gn_g, even_conv_w=m_even_conv_w, even_conv_b=m_even_conv_b,
               even_conv_ln_g=m_even_conv_ln_g, even_conv_ln_b=m_even_conv_ln_b, even_w_out=m_even_w_out,
               odd_w_qkv=m_odd_w_qkv, odd_q_norm_g=m_odd_q_norm_g, odd_k_norm_g=m_odd_k_norm_g, odd_w_o=m_odd_w_o,
               mlp_w1=m_mlp_w1, mlp_w2=m_mlp_w2)
    var = dict(meta=v_meta, norm_mix_g=v_norm_mix_g, norm_mlp_g=v_norm_mlp_g, even_w_in=v_even_w_in,
               even_ret_gn_g=v_even_ret_gn_g, even_conv_w=v_even_conv_w, even_conv_b=v_even_conv_b,
               even_conv_ln_g=v_even_conv_ln_g, even_conv_ln_b=v_even_conv_ln_b, even_w_out=v_even_w_out,
               odd_w_qkv=v_odd_w_qkv, odd_q_norm_g=v_odd_q_norm_g, odd_k_norm_g=v_odd_k_norm_g, odd_w_o=v_odd_w_o,
               mlp_w1=v_mlp_w1, mlp_w2=v_mlp_w2)
    me = 4 * lax.axis_index("x") + 2 * lax.axis_index("y") + lax.axis_index("c")

    small_in = jnp.concatenate([meta, jnp.pad(even_conv_w[0], ((0, 1), (0, 0))),
                                jnp.pad(even_ret_gn_g[0], ((0, 4), (0, 96)))], axis=0)
    b16 = lambda a: a.astype(BF16)
    later_src = dict(w_out=b16(even_w_out[0]), w1_0=b16(mlp_w1[0]), w2_0=b16(mlp_w2[0]),
                     w_qkv=b16(odd_w_qkv[0]), w_o=b16(odd_w_o[0]), w1_1=b16(mlp_w1[1]), w2_1=b16(mlp_w2[1]))
    landed = _gather_first([b16(even_w_in[0]), small_in], list(later_src.values()))
    g_in, g_small = landed[0], landed[1]
    own_slot = dict(zip(later_src, landed[2:]))
    groups = (("gather_l0", ("w_out", "w1_0", "w2_0")), ("gather_attn", ("w_qkv", "w_o")),
              ("gather_l1", ("w1_1", "w2_1")))
    pending = {}
    gather_tok = jnp.zeros((), F32)
    for gname, names in groups:
        ex = _exchange_start(gname, [later_src[n] for n in names], [own_slot[n] for n in names], "gather")
        gather_tok = gather_tok + ex["token"]
        for n in names:
            pending[n] = (ex, names)
    arrived = dict(w_in=g_in)

    def weight(name, after):
        if name not in arrived:
            ex, names = pending[name]
            arrived.update(zip(names, _exchange_wait(ex, after)[1]))
        return arrived[name]

    cols = lambda a: jnp.transpose(a, (1, 0, 2)).reshape(a.shape[1], -1)
    p = dict(norm_mix_g=norm_mix_g + gather_tok, norm_mlp_g=norm_mlp_g, conv_b=even_conv_b, ln_g=even_conv_ln_g,
             ln_b=even_conv_ln_b, qn_g=odd_q_norm_g, kn_g=odd_k_norm_g,
             gn_g=cols(g_small[:, 48:52, :32]),
             conv_w=jnp.pad(cols(g_small[:, 16:47]), ((0, 1), (0, 0))))
    meta_full = cols(g_small[:, 0:16])

    scatters = {}

    def emit(tag, grads):
        scatters[tag] = _scatter_start("scatter_" + tag, grads)
        return scatters[tag]["token"]

    h0 = jnp.concatenate([jnp.zeros((PAD_FRONT, D_MODEL), F32), meta_full, x[0]], axis=0)
    grad_x, small_part = _local_step(h0, loss_target[0], p, weight, emit)
    tot = _allreduce_small(small_part)
    loss = tot[ROW_LOSS, 0]

    got = {tag: _exchange_wait(ex, tot) for tag, ex in scatters.items()}
    pick = lambda tag, j: (got[tag][0][j], got[tag][1][j])
    terms = dict(even_w_in=[pick("mix0", 0)], even_w_out=[pick("mix0", 1)], odd_w_qkv=[pick("attn", 0)],
                 odd_w_o=[pick("attn", 1)], mlp_w1=[pick("mlp0", 0), pick("mlp1", 0)],
                 mlp_w2=[pick("mlp0", 1), pick("mlp1", 1)])
    out = {}
    for name in _BIG_NAMES:
        owns, recvs = zip(*terms[name])
        out[name] = _adamw("adamw_" + name, w[name], list(owns), list(recvs), mom[name], var[name], me)

    shard_cols = lambda a, width: lax.dynamic_slice_in_dim(a, me * width, width, axis=1)
    one = lambda r: tot[r:r + 1]
    small_g = dict(
        norm_mix_g=tot[ROW_MIX:ROW_MIX + 2], norm_mlp_g=tot[ROW_MLP:ROW_MLP + 2],
        even_conv_b=one(ROW_CB), even_conv_ln_g=one(ROW_LG), even_conv_ln_b=one(ROW_LB),
        odd_q_norm_g=one(ROW_QN)[:, :64], odd_k_norm_g=one(ROW_KN)[:, :64],
        meta=shard_cols(tot[ROW_META:ROW_META + N_META], 128),
        even_conv_w=shard_cols(tot[ROW_CW:ROW_CW + CONV_WIDTH], 128)[None],
        even_ret_gn_g=shard_cols(tot[ROW_GN].reshape(4, 256), 32)[None])
    packs = {n: (_pack128(w[n]), _pack128(small_g[n]), _pack128(mom[n]), _pack128(var[n])) for n in _SMALL_NAMES}
    cat4 = [jnp.concatenate([packs[n][i] for n in _SMALL_NAMES], axis=0) for i in range(4)]
    d_s, m_s, v_s = _adamw_small(*cat4)
    r0 = 0
    for n in _SMALL_NAMES:
        rows = packs[n][0].shape[0]
        size = w[n].size
        take = lambda a: a[r0:r0 + rows].reshape(-1)[:size].reshape(w[n].shape)
        out[n] = (small_g[n].reshape(w[n].shape), take(d_s), take(m_s), take(v_s))
        r0 += rows

    res = [loss, grad_x[None]]
    for i in range(4):
        res.extend(out[n][i] for n in _ORDER)
    return tuple(res)
```

```python
import functools

import numpy as np
import jax
import jax.numpy as jnp
from jax import lax
from jax.experimental import pallas as pl
from jax.experimental.pallas import tpu as pltpu

F32 = jnp.float32
BF16 = jnp.bfloat16

D_MODEL = 1024
N_META = 16
CHUNK = 128
PAD_FRONT = 112
TOK0 = PAD_FRONT + N_META
EPS = 1e-6
N_DEV = 8
RET_HEADS = 4
RET_DECAY_OFFSET = 5.0
ROPE_BASE = 10000.0
CONV_WIDTH = 31
HALO = 32
SB_SCALE = 64 ** -0.5
RET_SCALE = 128 ** -0.5
ADAM_LR, ADAM_B1, ADAM_B2, ADAM_EPS, ADAM_WD, ADAM_STEP = 0.001, 0.9, 0.999, 1e-08, 0.01, 10
VMEM_LIMIT = 56 * 1024 * 1024
MESH = pl.DeviceIdType.MESH


def _pcall(body, **kw):
    return pl.pallas_call(body, **kw)


def _params(**kw):
    return pltpu.CompilerParams(vmem_limit_bytes=VMEM_LIMIT, **kw)


def _tile(n, cands):
    for c in cands:
        if n % c == 0:
            return c
    raise ValueError(f"no tile for {n} in {cands}")


def _sigmoid(x):
    return 1.0 / (1.0 + jnp.exp(-x))


_DIMS = {
    "nn": (((1,), (0,)), ((), ())),
    "nt": (((1,), (1,)), ((), ())),
    "tn": (((0,), (0,)), ((), ())),
}


def _matmul(name, a, b, *, grid, a_spec, b_spec, o_spec, out_shape, contract, acc_shape,
            epi="plain", extra=None, extra_spec=None):
    nk = grid[2]
    dims = _DIMS[contract]
    n_in = 3 if extra is not None else 2
    n_out = 2 if epi == "relu2" else 1

    def body(*refs):
        a_ref, b_ref = refs[0], refs[1]
        e_ref = refs[2] if extra is not None else None
        outs = refs[n_in:n_in + n_out]
        acc = refs[-1]
        k = pl.program_id(2)
        part = lax.dot_general(a_ref[...].astype(BF16), b_ref[...].astype(BF16), dims, preferred_element_type=F32)
        if nk > 1:
            @pl.when(k == 0)
            def _():
                acc[...] = jnp.zeros_like(acc)

            acc[...] += part

        @pl.when(k == nk - 1)
        def _():
            r = acc[...] if nk > 1 else part
            if epi == "plain":
                outs[0][...] = r.astype(outs[0].dtype)
            elif epi == "residual":
                outs[0][...] = (r + e_ref[...]).astype(outs[0].dtype)
            elif epi == "relu2":
                outs[0][...] = r
                rr = jnp.maximum(r, 0.0)
                outs[1][...] = (rr * rr).astype(BF16)
            elif epi == "drelu2":
                outs[0][...] = (r * (2.0 * jnp.maximum(e_ref[...], 0.0))).astype(outs[0].dtype)

    in_specs = [a_spec, b_spec] + ([extra_spec] if extra is not None else [])
    args = (a, b) + ((extra,) if extra is not None else ())
    if n_out == 2:
        out_specs = [o_spec, o_spec]
    else:
        out_specs = o_spec
    return _pcall(body, name=name, grid=grid, in_specs=in_specs, out_specs=out_specs,
                  out_shape=out_shape, scratch_shapes=[pltpu.VMEM(acc_shape, F32)],
                  compiler_params=_params(dimension_semantics=("parallel", "parallel", "arbitrary")))(*args)


def _tm(t):
    return _tile(t, (1408, 768, 384, 128))


def _mm_cols(name, a, wb, lead, out_dtype=F32, epi="plain"):
    t, kdim = a.shape
    n = wb.shape[-1]
    tm, tk = _tm(t), _tile(kdim, (1024, 512))
    nl = len(lead)
    b_spec = pl.BlockSpec((None,) * (1 + nl) + (tk, n), lambda i, j, k: (j,) + lead + (k, 0))
    o_spec = pl.BlockSpec((tm, n), lambda i, j, k: (i, j))
    if epi == "relu2":
        out_shape = [jax.ShapeDtypeStruct((t, N_DEV * n), F32), jax.ShapeDtypeStruct((t, N_DEV * n), BF16)]
    else:
        out_shape = jax.ShapeDtypeStruct((t, N_DEV * n), out_dtype)
    return _matmul(name, a, wb, grid=(t // tm, N_DEV, kdim // tk),
                   a_spec=pl.BlockSpec((tm, tk), lambda i, j, k: (i, k)), b_spec=b_spec, o_spec=o_spec,
                   out_shape=out_shape, contract="nn", acc_shape=(tm, n), epi=epi)


def _tm_deep(t, kdim):
    return _tm(t) if kdim <= 2048 else _tile(t, (704, 384, 128))


def _mm_cols_t(name, a, wb):
    t = a.shape[0]
    nb, kdim, n = wb.shape
    tm, tn = _tm_deep(t, nb * n), _tile(kdim, (512,))

    def body(a_ref, b_ref, o_ref):
        acc = _dot(a_ref[:, 0:n].astype(BF16), b_ref[0], "nt")
        for j in range(1, nb):
            acc = acc + _dot(a_ref[:, j * n:(j + 1) * n].astype(BF16), b_ref[j], "nt")
        o_ref[...] = acc

    return _pcall(body, name=name, grid=(t // tm, kdim // tn),
                  in_specs=[pl.BlockSpec((tm, nb * n), lambda i, j: (i, 0)),
                            pl.BlockSpec((nb, tn, n), lambda i, j: (0, j, 0))],
                  out_specs=pl.BlockSpec((tm, tn), lambda i, j: (i, j)),
                  out_shape=jax.ShapeDtypeStruct((t, kdim), F32),
                  compiler_params=_params(dimension_semantics=("parallel", "parallel")))(a, wb)


def _mm_rows(name, a, wb, residual):
    t = a.shape[0]
    nb, r, n = wb.shape
    tm, tn = _tm_deep(t, nb * r), _tile(n, (512,))

    def body(a_ref, b_ref, r_ref, o_ref):
        o_ref[...] = r_ref[...] + _dot(a_ref[...].astype(BF16), b_ref[...].reshape(nb * r, tn))

    o_spec = pl.BlockSpec((tm, tn), lambda i, j: (i, j))
    return _pcall(body, name=name, grid=(t // tm, n // tn),
                  in_specs=[pl.BlockSpec((tm, nb * r), lambda i, j: (i, 0)),
                            pl.BlockSpec((nb, r, tn), lambda i, j: (0, 0, j)), o_spec],
                  out_specs=o_spec, out_shape=jax.ShapeDtypeStruct((t, n), F32),
                  compiler_params=_params(dimension_semantics=("parallel", "parallel")))(a, wb, residual)


def _mm_rows_t(name, a, wb, lead, out_dtype=F32, epi="plain", extra=None):
    t, n = a.shape
    r = wb.shape[-2]
    tm, tk = _tm(t), _tile(n, (1024,))
    nl = len(lead)
    b_spec = pl.BlockSpec((None,) * (1 + nl) + (r, tk), lambda i, j, k: (j,) + lead + (0, k))
    o_spec = pl.BlockSpec((tm, r), lambda i, j, k: (i, j))
    return _matmul(name, a, wb, grid=(t // tm, N_DEV, n // tk),
                   a_spec=pl.BlockSpec((tm, tk), lambda i, j, k: (i, k)), b_spec=b_spec, o_spec=o_spec,
                   out_shape=jax.ShapeDtypeStruct((t, N_DEV * r), out_dtype), contract="nt",
                   acc_shape=(tm, r), epi=epi, extra=extra, extra_spec=o_spec if extra is not None else None)


def _wgrad_cols(name, x, dy, n):
    t, kdim = x.shape
    tk = _tm(t)
    return _matmul(name, x, dy, grid=(1, N_DEV, t // tk),
                   a_spec=pl.BlockSpec((tk, kdim), lambda i, j, k: (k, 0)),
                   b_spec=pl.BlockSpec((tk, n), lambda i, j, k: (k, j)),
                   o_spec=pl.BlockSpec((None, kdim, n), lambda i, j, k: (j, 0, 0)),
                   out_shape=jax.ShapeDtypeStruct((N_DEV, kdim, n), BF16), contract="tn", acc_shape=(kdim, n))


def _wgrad_rows(name, x, dy, r):
    t = x.shape[0]
    n = dy.shape[1]
    tk, tn = _tm(t), _tile(n, (512,))
    return _matmul(name, x, dy, grid=(N_DEV, n // tn, t // tk),
                   a_spec=pl.BlockSpec((tk, r), lambda i, j, k: (k, i)),
                   b_spec=pl.BlockSpec((tk, tn), lambda i, j, k: (k, j)),
                   o_spec=pl.BlockSpec((None, r, tn), lambda i, j, k: (i, 0, j)),
                   out_shape=jax.ShapeDtypeStruct((N_DEV, r, n), BF16), contract="tn", acc_shape=(r, tn))


def _rows(t):
    return _tile(t, (384, 128))


def _rms_fwd(name, h, g):
    t = h.shape[0]
    tr = _rows(t)

    def body(h_ref, g_ref, o_ref):
        x = h_ref[...]
        r = lax.rsqrt(jnp.mean(x * x, axis=-1, keepdims=True) + EPS)
        o_ref[...] = (x * r * g_ref[...]).astype(BF16)

    row = pl.BlockSpec((tr, D_MODEL), lambda i: (i, 0))
    vec = pl.BlockSpec((1, D_MODEL), lambda i: (0, 0))
    return _pcall(body, name=name, grid=(t // tr,), in_specs=[row, vec], out_specs=row,
                  out_shape=jax.ShapeDtypeStruct((t, D_MODEL), BF16))(h, g)


def _rms_bwd(name, dhn, h, g, dres):
    t = h.shape[0]
    tr = _rows(t)

    def body(d_ref, h_ref, g_ref, r_ref, o_ref, dg_ref):
        @pl.when(pl.program_id(0) == 0)
        def _():
            dg_ref[...] = jnp.zeros_like(dg_ref)

        x = h_ref[...]
        d = d_ref[...]
        r = lax.rsqrt(jnp.mean(x * x, axis=-1, keepdims=True) + EPS)
        u = d * g_ref[...]
        m = jnp.mean(u * x, axis=-1, keepdims=True)
        o_ref[...] = r_ref[...] + r * u - x * (r * r * r * m)
        dg_ref[...] += jnp.sum(d * x * r, axis=0, keepdims=True)

    row = pl.BlockSpec((tr, D_MODEL), lambda i: (i, 0))
    vec = pl.BlockSpec((1, D_MODEL), lambda i: (0, 0))
    return _pcall(body, name=name, grid=(t // tr,), in_specs=[row, row, vec, row], out_specs=[row, vec],
                  out_shape=[jax.ShapeDtypeStruct((t, D_MODEL), F32), jax.ShapeDtypeStruct((1, D_MODEL), F32)])(
                      dhn, h, g, dres)


def _loss_bwd(h, target):
    t = h.shape[0]
    nb = t // CHUNK

    def body(h_ref, t_ref, d_ref, l_ref):
        i = pl.program_id(0)

        @pl.when(i == 0)
        def _():
            d_ref[...] = jnp.zeros_like(d_ref)
            l_ref[...] = jnp.zeros_like(l_ref)

        @pl.when(i > 0)
        def _():
            diff = h_ref[...] - t_ref[...]
            d_ref[...] = diff * (1.0 / D_MODEL)
            l_ref[...] += jnp.sum(diff * diff) * (0.5 / D_MODEL)

    return _pcall(body, name="loss_bwd", grid=(nb,),
                  in_specs=[pl.BlockSpec((CHUNK, D_MODEL), lambda i: (i, 0)),
                            pl.BlockSpec((CHUNK, D_MODEL), lambda i: (jnp.maximum(i - 1, 0), 0))],
                  out_specs=[pl.BlockSpec((CHUNK, D_MODEL), lambda i: (i, 0)),
                             pl.BlockSpec((8, 128), lambda i: (0, 0))],
                  out_shape=[jax.ShapeDtypeStruct((t, D_MODEL), F32), jax.ShapeDtypeStruct((8, 128), F32)])(h, target)


def _ret_tables(t):
    hh = np.arange(RET_HEADS, dtype=np.float64)
    log_g = np.log1p(-np.exp2(-RET_DECAY_OFFSET - hh))
    idx = np.arange(CHUNK, dtype=np.float64)
    diff = idx[:, None] - idx[None, :]
    dmat = np.where(diff[None] >= 0, np.exp(np.maximum(diff, 0.0)[None] * log_g[:, None, None]), 0.0)
    qdec = np.exp((idx + 1.0)[None, :, None] * log_g[:, None, None]) * np.ones((1, 1, CHUNK))
    kdec = np.exp((CHUNK - 1 - idx)[None, :, None] * log_g[:, None, None]) * np.ones((1, 1, CHUNK))
    half = CHUNK // 2
    inv_freq = (ROPE_BASE ** (-np.arange(half, dtype=np.float32) / half)).astype(np.float32)
    ang = (np.arange(t, dtype=np.float32)[:, None] * inv_freq[None, :]).astype(np.float32).astype(np.float64)
    cos2 = np.concatenate([np.cos(ang), np.cos(ang)], axis=1)
    sin2 = np.concatenate([-np.sin(ang), np.sin(ang)], axis=1)
    return tuple(jnp.asarray(v, F32) for v in (dmat, qdec, kdec, cos2, sin2))


def _rot(x, c, s):
    return x * c + pltpu.roll(x, CHUNK // 2, 1) * s


def _unrot(dx, c, s):
    return dx * c + pltpu.roll(dx * s, CHUNK // 2, 1)


def _dot(a, b, contract="nn"):
    return lax.dot_general(a, b, _DIMS[contract], preferred_element_type=F32)


def _ret_fwd(proj, tables):
    t = proj.shape[0]
    nch = t // CHUNK
    dmat, qdec, kdec, cos2, sin2 = tables

    def body(qk_ref, v_ref, c_ref, s_ref, dm_ref, qd_ref, kd_ref, o_ref, st_ref, state):
        @pl.when(pl.program_id(0) == 0)
        def _():
            state[...] = jnp.zeros_like(state)

        c, s = c_ref[...], s_ref[...]
        for h in range(RET_HEADS):
            q = _rot(qk_ref[:, 128 * h:128 * (h + 1)], c, s)
            k = _rot(qk_ref[:, 512 + 128 * h:512 + 128 * (h + 1)], c, s) * RET_SCALE
            vb = v_ref[:, 256 * h:256 * (h + 1)].astype(BF16)
            st = state[h]
            st_ref[h] = st
            sc = _dot(q.astype(BF16), k.astype(BF16), "nt") * dm_ref[h]
            o = _dot(sc.astype(BF16), vb)
            o += _dot((q * qd_ref[h]).astype(BF16), st.astype(BF16))
            o_ref[:, 256 * h:256 * (h + 1)] = o
            kv = _dot((k * kd_ref[h]).astype(BF16), vb, "tn")
            state[h] = qd_ref[h, CHUNK - 1:CHUNK, 0:1] * st + kv

    tab = pl.BlockSpec((RET_HEADS, CHUNK, CHUNK), lambda n: (0, 0, 0))
    pos = pl.BlockSpec((CHUNK, CHUNK), lambda n: (n, 0))
    return _pcall(
        body, name="ret_fwd", grid=(nch,),
        in_specs=[pl.BlockSpec((CHUNK, 1024), lambda n: (n, 0)), pl.BlockSpec((CHUNK, 1024), lambda n: (n, 1)),
                  pos, pos, tab, tab, tab],
        out_specs=[pl.BlockSpec((CHUNK, 1024), lambda n: (n, 0)),
                   pl.BlockSpec((RET_HEADS, None, 128, 256), lambda n: (0, n, 0, 0))],
        out_shape=[jax.ShapeDtypeStruct((t, 1024), F32), jax.ShapeDtypeStruct((RET_HEADS, nch, 128, 256), F32)],
        scratch_shapes=[pltpu.VMEM((RET_HEADS, 128, 256), F32)],
        compiler_params=_params(dimension_semantics=("arbitrary",)))(
            proj, proj, cos2, sin2, dmat, qdec, kdec)


def _ret_bwd(proj, states, do, tables):
    t = proj.shape[0]
    nch = t // CHUNK
    dmat, qdec, kdec, cos2, sin2 = tables

    def body(qk_ref, v_ref, do_ref, st_ref, c_ref, s_ref, dm_ref, qd_ref, kd_ref, dqk_ref, dv_ref, rst):
        @pl.when(pl.program_id(0) == 0)
        def _():
            rst[...] = jnp.zeros_like(rst)

        c, s = c_ref[...], s_ref[...]
        for h in range(RET_HEADS):
            q = _rot(qk_ref[:, 128 * h:128 * (h + 1)], c, s)
            k = _rot(qk_ref[:, 512 + 128 * h:512 + 128 * (h + 1)], c, s) * RET_SCALE
            qb, kb = q.astype(BF16), k.astype(BF16)
            vb = v_ref[:, 256 * h:256 * (h + 1)].astype(BF16)
            dob = do_ref[:, 256 * h:256 * (h + 1)].astype(BF16)
            pb = st_ref[h].astype(BF16)
            r = rst[h]
            rb = r.astype(BF16)
            dm, qd, kd = dm_ref[h], qd_ref[h], kd_ref[h]
            sb = (_dot(qb, kb, "nt") * dm).astype(BF16)
            dsb = (_dot(dob, vb, "nt") * dm).astype(BF16)
            dq = _dot(dsb, kb) + _dot(dob, pb, "nt") * qd
            dk = _dot(dsb, qb, "tn") + _dot(vb, rb, "nt") * kd
            dv = _dot(sb, dob, "tn") + _dot((k * kd).astype(BF16), rb)
            rst[h] = _dot((q * qd).astype(BF16), dob, "tn") + qd[CHUNK - 1:CHUNK, 0:1] * r
            dqk_ref[:, 128 * h:128 * (h + 1)] = _unrot(dq, c, s).astype(BF16)
            dqk_ref[:, 512 + 128 * h:512 + 128 * (h + 1)] = (_unrot(dk, c, s) * RET_SCALE).astype(BF16)
            dv_ref[:, 256 * h:256 * (h + 1)] = dv.astype(BF16)

    rev = lambda n: nch - 1 - n
    tab = pl.BlockSpec((RET_HEADS, CHUNK, CHUNK), lambda n: (0, 0, 0))
    pos = pl.BlockSpec((CHUNK, CHUNK), lambda n: (rev(n), 0))
    row = pl.BlockSpec((CHUNK, 1024), lambda n: (rev(n), 0))
    osh = jax.ShapeDtypeStruct((t, 1024), BF16)
    return _pcall(
        body, name="ret_bwd", grid=(nch,),
        in_specs=[row, pl.BlockSpec((CHUNK, 1024), lambda n: (rev(n), 1)), row,
                  pl.BlockSpec((RET_HEADS, None, 128, 256), lambda n: (0, rev(n), 0, 0)),
                  pos, pos, tab, tab, tab],
        out_specs=[row, row], out_shape=[osh, osh],
        scratch_shapes=[pltpu.VMEM((RET_HEADS, 128, 256), F32)],
        compiler_params=_params(dimension_semantics=("arbitrary",)))(
            proj, proj, do, states, cos2, sin2, dmat, qdec, kdec)


def _gn_gate_fwd(o, proj, gn_g):
    t = o.shape[0]
    tr = _rows(t)

    def body(o_ref, g_ref, w_ref, c_ref):
        for h in range(RET_HEADS):
            sl = slice(256 * h, 256 * (h + 1))
            x = o_ref[:, sl]
            mu = jnp.mean(x, axis=-1, keepdims=True)
            xc = x - mu
            rstd = lax.rsqrt(jnp.mean(xc * xc, axis=-1, keepdims=True) + EPS)
            g = g_ref[:, sl]
            c_ref[:, sl] = (g * _sigmoid(g) * (xc * rstd * w_ref[:, sl])).astype(BF16)

    return _pcall(body, name="gn_gate_fwd", grid=(t // tr,),
                  in_specs=[pl.BlockSpec((tr, 1024), lambda i: (i, 0)),
                            pl.BlockSpec((tr, 1024), lambda i: (i, 2)),
                            pl.BlockSpec((1, 1024), lambda i: (0, 0))],
                  out_specs=pl.BlockSpec((tr, 1024), lambda i: (i, 0)),
                  out_shape=jax.ShapeDtypeStruct((t, 2048), BF16))(o, proj, gn_g)


def _gn_gate_bwd(dcat, o, proj, gn_g):
    t = o.shape[0]
    tr = _rows(t)

    def body(d_ref, o_ref, g_ref, w_ref, do_ref, dg_ref, dw_ref):
        @pl.when(pl.program_id(0) == 0)
        def _():
            dw_ref[...] = jnp.zeros_like(dw_ref)

        for h in range(RET_HEADS):
            sl = slice(256 * h, 256 * (h + 1))
            x = o_ref[:, sl]
            mu = jnp.mean(x, axis=-1, keepdims=True)
            xc = x - mu
            rstd = lax.rsqrt(jnp.mean(xc * xc, axis=-1, keepdims=True) + EPS)
            xh = xc * rstd
            w = w_ref[:, sl]
            g = g_ref[:, sl]
            sg = _sigmoid(g)
            d = d_ref[:, sl]
            don = d * (g * sg)
            dg_ref[:, sl] = (d * (xh * w) * (sg * (1.0 + g * (1.0 - sg)))).astype(BF16)
            dw_ref[:, sl] += jnp.sum(don * xh, axis=0, keepdims=True)
            dxh = don * w
            m1 = jnp.mean(dxh, axis=-1, keepdims=True)
            m2 = jnp.mean(dxh * xh, axis=-1, keepdims=True)
            do_ref[:, sl] = rstd * (dxh - m1 - xh * m2)

    row = pl.BlockSpec((tr, 1024), lambda i: (i, 0))
    vec = pl.BlockSpec((1, 1024), lambda i: (0, 0))
    return _pcall(body, name="gn_gate_bwd", grid=(t // tr,),
                  in_specs=[row, row, pl.BlockSpec((tr, 1024), lambda i: (i, 2)), vec],
                  out_specs=[row, row, vec],
                  out_shape=[jax.ShapeDtypeStruct((t, 1024), F32), jax.ShapeDtypeStruct((t, 1024), BF16),
                             jax.ShapeDtypeStruct((1, 1024), F32)])(dcat, o, proj, gn_g)


def _row_ids(i, tr):
    return i * tr + lax.broadcasted_iota(jnp.int32, (tr, 1), 0)


SH_ROWS = HALO - 8


def _shifted_copies(xs, sh, tr):
    for b in range(1, 8):
        sh[b - 1] = xs[pl.ds(b, tr + SH_ROWS), :]


def _shifted(xs, sh, off, tr):
    a, b = divmod(off, 8)
    return xs[pl.ds(8 * a, tr), :] if b == 0 else sh[b - 1, pl.ds(8 * a, tr), :]


def _conv_fwd(cat, proj, conv_w, conv_b, ln_g, ln_b):
    t = proj.shape[0]
    tr = _rows(t)
    hb = tr // HALO

    def body(cat_in, ua_ref, ug_ref, pa_ref, pg_ref, w_ref, b_ref, lg_ref, lb_ref, c_ref, hd_ref, y_ref, xs, sh):
        del cat_in
        i = pl.program_id(0)
        hdn = ua_ref[...] * _sigmoid(ug_ref[...])
        hd_ref[...] = hdn
        prev = pa_ref[...] * _sigmoid(pg_ref[...])
        xs[0:HALO, :] = jnp.where(i > 0, prev, 0.0)
        xs[HALO:HALO + tr, :] = hdn
        _shifted_copies(xs, sh, tr)
        acc = jnp.zeros((tr, 1024), F32) + b_ref[...]
        for w in range(CONV_WIDTH):
            acc += w_ref[w:w + 1, :] * _shifted(xs, sh, HALO - (CONV_WIDTH - 1) + w, tr)
        y_ref[...] = acc
        mu = jnp.mean(acc, axis=-1, keepdims=True)
        yc = acc - mu
        rstd = lax.rsqrt(jnp.mean(yc * yc, axis=-1, keepdims=True) + EPS)
        yn = yc * rstd * lg_ref[...] + lb_ref[...]
        c = yn * _sigmoid(yn)
        c_ref[...] = jnp.where(_row_ids(i, tr) >= PAD_FRONT, c, 0.0).astype(BF16)

    row = pl.BlockSpec((tr, 1024), lambda i: (i, 0))
    vec = pl.BlockSpec((1, 1024), lambda i: (0, 0))
    halo = lambda col: pl.BlockSpec((HALO, 1024), lambda i: (jnp.maximum(i * hb - 1, 0), col))
    return _pcall(body, name="conv_fwd", grid=(t // tr,),
                  in_specs=[pl.BlockSpec(memory_space=pl.ANY),
                            pl.BlockSpec((tr, 1024), lambda i: (i, 3)), pl.BlockSpec((tr, 1024), lambda i: (i, 4)),
                            halo(3), halo(4), pl.BlockSpec((32, 1024), lambda i: (0, 0)), vec, vec, vec],
                  out_specs=[pl.BlockSpec((tr, 1024), lambda i: (i, 1)), row, row],
                  out_shape=[jax.ShapeDtypeStruct((t, 2048), BF16), jax.ShapeDtypeStruct((t, 1024), F32),
                             jax.ShapeDtypeStruct((t, 1024), F32)],
                  scratch_shapes=[pltpu.VMEM((tr + HALO, 1024), F32), pltpu.VMEM((7, tr + SH_ROWS, 1024), F32)],
                  input_output_aliases={0: 0}, compiler_params=_params())(
                      cat, proj, proj, proj, proj, conv_w, conv_b, ln_g, ln_b)


def _conv_bwd_ln(dcat, y, ln_g, ln_b):
    t = y.shape[0]
    tr = _rows(t)

    def body(d_ref, y_ref, lg_ref, lb_ref, dy_ref, dlg_ref, dlb_ref, dcb_ref):
        i = pl.program_id(0)

        @pl.when(i == 0)
        def _():
            dlg_ref[...] = jnp.zeros_like(dlg_ref)
            dlb_ref[...] = jnp.zeros_like(dlb_ref)
            dcb_ref[...] = jnp.zeros_like(dcb_ref)

        y = y_ref[...]
        mu = jnp.mean(y, axis=-1, keepdims=True)
        yc = y - mu
        rstd = lax.rsqrt(jnp.mean(yc * yc, axis=-1, keepdims=True) + EPS)
        xh = yc * rstd
        lg = lg_ref[...]
        yn = xh * lg + lb_ref[...]
        sg = _sigmoid(yn)
        dyn = jnp.where(_row_ids(i, tr) >= PAD_FRONT, d_ref[...] * (sg * (1.0 + yn * (1.0 - sg))), 0.0)
        dlg_ref[...] += jnp.sum(dyn * xh, axis=0, keepdims=True)
        dlb_ref[...] += jnp.sum(dyn, axis=0, keepdims=True)
        dxh = dyn * lg
        m1 = jnp.mean(dxh, axis=-1, keepdims=True)
        m2 = jnp.mean(dxh * xh, axis=-1, keepdims=True)
        dy = rstd * (dxh - m1 - xh * m2)
        dy_ref[...] = dy
        dcb_ref[...] += jnp.sum(dy, axis=0, keepdims=True)

    row = pl.BlockSpec((tr, 1024), lambda i: (i, 0))
    vec = pl.BlockSpec((1, 1024), lambda i: (0, 0))
    vshape = jax.ShapeDtypeStruct((1, 1024), F32)
    return _pcall(body, name="conv_bwd_ln", grid=(t // tr,),
                  in_specs=[pl.BlockSpec((tr, 1024), lambda i: (i, 1)), row, vec, vec],
                  out_specs=[row, vec, vec, vec],
                  out_shape=[jax.ShapeDtypeStruct((t, 1024), F32), vshape, vshape, vshape])(dcat, y, ln_g, ln_b)


def _conv_bwd_taps(dy, hdn, proj, conv_w):
    t = dy.shape[0]
    tr = _rows(t)
    hb = tr // HALO
    nt = t // tr

    def body(dy_ref, nx_ref, hd_ref, ph_ref, ua_ref, ug_ref, w_ref, da_ref, dg_ref, dw_ref, xs, sh):
        i = pl.program_id(0)

        @pl.when(i == 0)
        def _():
            dw_ref[...] = jnp.zeros_like(dw_ref)

        dy = dy_ref[...]
        xs[0:tr, :] = dy
        xs[tr:tr + HALO, :] = jnp.where(i < nt - 1, nx_ref[...], 0.0)
        _shifted_copies(xs, sh, tr)
        dh = jnp.zeros((tr, 1024), F32)
        for w in range(CONV_WIDTH):
            dh += w_ref[w:w + 1, :] * _shifted(xs, sh, CONV_WIDTH - 1 - w, tr)
        xs[0:HALO, :] = jnp.where(i > 0, ph_ref[...], 0.0)
        xs[HALO:HALO + tr, :] = hd_ref[...]
        _shifted_copies(xs, sh, tr)
        for w in range(CONV_WIDTH):
            dw_ref[w:w + 1, :] += jnp.sum(dy * _shifted(xs, sh, HALO - (CONV_WIDTH - 1) + w, tr), axis=0, keepdims=True)
        dh = jnp.where(_row_ids(i, tr) >= PAD_FRONT, dh, 0.0)
        sg = _sigmoid(ug_ref[...])
        da_ref[...] = (dh * sg).astype(BF16)
        dg_ref[...] = (dh * ua_ref[...] * sg * (1.0 - sg)).astype(BF16)

    row = pl.BlockSpec((tr, 1024), lambda i: (i, 0))
    return _pcall(body, name="conv_bwd_taps", grid=(nt,),
                  in_specs=[row, pl.BlockSpec((HALO, 1024), lambda i: (jnp.minimum((i + 1) * hb, nt * hb - 1), 0)),
                            row, pl.BlockSpec((HALO, 1024), lambda i: (jnp.maximum(i * hb - 1, 0), 0)),
                            pl.BlockSpec((tr, 1024), lambda i: (i, 3)), pl.BlockSpec((tr, 1024), lambda i: (i, 4)),
                            pl.BlockSpec((32, 1024), lambda i: (0, 0))],
                  out_specs=[row, row, pl.BlockSpec((32, 1024), lambda i: (0, 0))],
                  out_shape=[jax.ShapeDtypeStruct((t, 1024), BF16), jax.ShapeDtypeStruct((t, 1024), BF16),
                             jax.ShapeDtypeStruct((32, 1024), F32)],
                  scratch_shapes=[pltpu.VMEM((tr + HALO, 1024), F32), pltpu.VMEM((7, tr + SH_ROWS, 1024), F32)],
                  compiler_params=_params())(dy, dy, hdn, hdn, proj, proj, conv_w)


NEG_BIG = -1e30


def _seg_tables(qb):
    j = np.arange(128)
    bd = (j[:, None] // 64 == j[None, :] // 64).astype(np.float32)
    ones = np.ones((128, 128), np.float32)
    later = np.concatenate([(j[:, None] >= j[None, :]).astype(np.float32), ones], axis=1)
    earlier = np.concatenate([(j[:, None] < j[None, :]).astype(np.float32), ones], axis=1)
    per = qb // CHUNK
    row = np.arange(qb)[:, None]
    pad = np.broadcast_to(j[None, :] < PAD_FRONT, (qb, 128))
    diag = [(g * CHUNK + j[None, :]) >= row for g in range(per)]
    masks = diag + [np.zeros((qb, 128), bool), pad, diag[0] | pad]
    bias = np.stack([np.where(m, NEG_BIG, 0.0) for m in masks]).astype(np.float32)
    dup = lambda m: np.concatenate([m, m], axis=0)
    return (jnp.asarray(bd, BF16), jnp.asarray(dup(later), BF16), jnp.asarray(dup(earlier), BF16),
            jnp.asarray(bias, F32))


def _split_dot(x, m):
    hi = x.astype(BF16)
    lo = (x - hi.astype(F32)).astype(BF16)
    return _dot(hi, m) + _dot(lo, m)


def _qk_norm_fwd(qkv, qg, kg, bd):
    t = qkv.shape[0]
    tr = _rows(t)
    nb = tr // CHUNK

    def body(q_ref, k_ref, v_ref, qg_ref, kg_ref, bd_ref, qo, kt, k2, vt, v2):
        bdm = bd_ref[...]
        lane = lax.broadcasted_iota(jnp.int32, (1, 128), 1)
        sub = lax.broadcasted_iota(jnp.int32, (128, 1), 0)

        def pair_layouts(x, t_ref, s_ref, hp, b):
            xt = x.T
            t_ref[hp, b] = jnp.concatenate([jnp.where(sub < 64, xt, 0.0), jnp.where(sub >= 64, xt, 0.0)],
                                           axis=1).astype(BF16)
            s_ref[hp, b] = jnp.concatenate([jnp.where(lane < 64, x, 0.0), jnp.where(lane >= 64, x, 0.0)],
                                           axis=0).astype(BF16)

        for hp in range(8):
            sl = slice(128 * hp, 128 * (hp + 1))
            x = q_ref[:, sl]
            r = lax.rsqrt(_split_dot(x * x, bdm) * (1.0 / 64) + EPS)
            qo[:, sl] = (x * r * (qg_ref[:, sl] * SB_SCALE)).astype(BF16)
            x = k_ref[:, sl]
            r = lax.rsqrt(_split_dot(x * x, bdm) * (1.0 / 64) + EPS)
            kn = x * r * kg_ref[:, sl]
            v = v_ref[:, sl]
            for b in range(nb):
                rows = slice(CHUNK * b, CHUNK * (b + 1))
                pair_layouts(kn[rows], kt, k2, hp, b)
                pair_layouts(v[rows], vt, v2, hp, b)

    col = lambda c: pl.BlockSpec((tr, 1024), lambda i: (i, c))
    vec = pl.BlockSpec((1, 1024), lambda i: (0, 0))
    wide = pl.BlockSpec((8, nb, 128, 256), lambda i: (0, i, 0, 0))
    tall = pl.BlockSpec((8, nb, 256, 128), lambda i: (0, i, 0, 0))
    wsh = jax.ShapeDtypeStruct((8, t // CHUNK, 128, 256), BF16)
    tsh = jax.ShapeDtypeStruct((8, t // CHUNK, 256, 128), BF16)
    return _pcall(body, name="qk_norm_fwd", grid=(t // tr,),
                  in_specs=[col(0), col(1), col(2), vec, vec, pl.BlockSpec((128, 128), lambda i: (0, 0))],
                  out_specs=[col(0), wide, tall, wide, tall],
                  out_shape=[jax.ShapeDtypeStruct((t, 1024), BF16), wsh, tsh, wsh, tsh])(qkv, qkv, qkv, qg, kg, bd)


def _qk_norm_bwd(qkv, dq, dk, dv, qg, kg, bd):
    t = qkv.shape[0]
    tr = _rows(t)

    def body(q_ref, k_ref, dq_ref, dk_ref, dv_ref, qg_ref, kg_ref, bd_ref, o_ref, dqg_ref, dkg_ref):
        @pl.when(pl.program_id(0) == 0)
        def _():
            dqg_ref[...] = jnp.zeros_like(dqg_ref)
            dkg_ref[...] = jnp.zeros_like(dkg_ref)

        bdm = bd_ref[...]
        for part, (src, d_ref, g_ref, dg_ref) in enumerate(((q_ref, dq_ref, qg_ref, dqg_ref),
                                                           (k_ref, dk_ref, kg_ref, dkg_ref))):
            for cix in range(8):
                sl = slice(128 * cix, 128 * (cix + 1))
                x = src[:, sl]
                d = d_ref[:, sl]
                r = lax.rsqrt(_split_dot(x * x, bdm) * (1.0 / 64) + EPS)
                u = d * g_ref[:, sl]
                m = _split_dot(u * x, bdm) * (1.0 / 64)
                o_ref[:, 1024 * part + 128 * cix:1024 * part + 128 * (cix + 1)] = (r * u - x * (r * r * r * m)).astype(BF16)
                dg_ref[:, sl] += jnp.sum(d * x * r, axis=0, keepdims=True)
        o_ref[:, 2048:3072] = dv_ref[...].astype(BF16)

    col = lambda c: pl.BlockSpec((tr, 1024), lambda i: (i, c))
    vec = pl.BlockSpec((1, 1024), lambda i: (0, 0))
    vsh = jax.ShapeDtypeStruct((1, 1024), F32)
    return _pcall(body, name="qk_norm_bwd", grid=(t // tr,),
                  in_specs=[col(0), col(1), col(0), col(0), col(0), vec, vec, pl.BlockSpec((128, 128), lambda i: (0, 0))],
                  out_specs=[pl.BlockSpec((tr, 3072), lambda i: (i, 0)), vec, vec],
                  out_shape=[jax.ShapeDtypeStruct((t, 3072), BF16), vsh, vsh])(qkv, qkv, dq, dk, dv, qg, kg, bd)


def _split2(x):
    hi = x.astype(BF16)
    lo = (x - hi.astype(F32)).astype(BF16)
    return jnp.concatenate([hi, lo], axis=1)


def _sb_scores(z, later_tab):
    e = jnp.exp(-jnp.abs(z))
    ope = 1.0 + e
    sp = jnp.maximum(z, 0.0) + jnp.log(ope)
    return e, ope, _dot(_split2(sp), later_tab)


def _sb_bias_index(i, kb, per):
    g = kb - i * per
    return jnp.where(kb == 0, jnp.where(i == 0, per + 2, per + 1), jnp.where(g >= 0, g, per))


def _sb_qb(t):
    return _tile(t, (384, 128))


def _sb_fwd(qh, kt, v2, later_tab, bias_tab):
    t = qh.shape[0]
    qb = _sb_qb(t)
    per = qb // CHUNK
    nkb_all = t // CHUNK
    zero_slot = nkb_all

    def body(q_ref, kt_ref, v2_ref, tab_ref, bias_ref, o_ref, ws_ref, acc, carry, zbuf, zk, cub, wbuf, wsem):
        h, i = pl.program_id(0), pl.program_id(1)
        q = q_ref[...]
        acc[...] = jnp.zeros_like(acc)
        carry[...] = jnp.zeros_like(carry)
        nkb = (i + 1) * per
        save = lambda kb: pltpu.make_async_copy(wbuf.at[kb], ws_ref.at[h, i, kb], wsem.at[kb])

        def sums(z2, kb):
            bias = bias_ref[_sb_bias_index(i, kb, per)]
            for hh in range(2):
                sl = slice(128 * hh, 128 * (hh + 1))
                z = z2[:, sl] + bias
                zk[:, sl] = z
                cub[hh] = _sb_scores(z, tab_ref[...])[2]

        def weights(kb):
            for hh in range(2):
                sl = slice(128 * hh, 128 * (hh + 1))
                cu = cub[hh]
                cin = carry[hh]
                wbuf[kb, :, sl] = jnp.exp(zk[:, sl] - cu[:, :128] - cin).astype(BF16)
                carry[hh] = cin + cu[:, 128:]

        def output(slot, kb):
            acc[...] += _dot(wbuf[slot], v2_ref[kb])

        sums(_dot(q, kt_ref[nkb - 1]), nkb - 1)
        zbuf[...] = _dot(q, kt_ref[jnp.maximum(nkb - 2, 0)])
        wbuf[zero_slot] = jnp.zeros((qb, 256), BF16)

        def step(s, _):
            kb = nkb - 1 - s

            @pl.when(s > 0)
            def _():
                save(kb + 1).start()

            z2 = zbuf[...]
            zbuf[...] = _dot(q, kt_ref[jnp.maximum(kb - 2, 0)])
            output(jnp.where(s == 0, zero_slot, kb + 1), jnp.minimum(kb + 1, nkb - 1))
            weights(kb)
            sums(z2, kb - 1)
            return 0

        lax.fori_loop(0, nkb - 1, step, 0)

        @pl.when(nkb > 1)
        def _():
            save(1).start()

        output(jnp.where(nkb == 1, zero_slot, 1), jnp.minimum(1, nkb - 1))
        weights(0)
        output(0, 0)
        save(0).start()
        o_ref[...] = acc[...]

        def drain(kb, _):
            save(kb).wait()
            return 0

        lax.fori_loop(0, nkb, drain, 0)

    blk = pl.BlockSpec((qb, 128), lambda h, i: (i, h))
    wide = pl.BlockSpec((None, nkb_all, 128, 256), lambda h, i: (h, 0, 0, 0))
    tall = pl.BlockSpec((None, nkb_all, 256, 128), lambda h, i: (h, 0, 0, 0))
    return _pcall(body, name="sb_fwd", grid=(8, t // qb),
                  in_specs=[blk, wide, tall, pl.BlockSpec((256, 256), lambda h, i: (0, 0)),
                            pl.BlockSpec((per + 3, qb, 128), lambda h, i: (0, 0, 0))],
                  out_specs=[blk, pl.BlockSpec(memory_space=pl.ANY)],
                  out_shape=[jax.ShapeDtypeStruct((t, 1024), F32),
                             jax.ShapeDtypeStruct((8, t // qb, nkb_all, qb, 256), BF16)],
                  scratch_shapes=[pltpu.VMEM((qb, 128), F32), pltpu.VMEM((2, qb, 128), F32),
                                  pltpu.VMEM((qb, 256), F32), pltpu.VMEM((qb, 256), F32),
                                  pltpu.VMEM((2, qb, 256), F32), pltpu.VMEM((nkb_all + 1, qb, 256), BF16),
                                  pltpu.SemaphoreType.DMA((nkb_all,))],
                  compiler_params=_params(dimension_semantics=("parallel", "arbitrary")))(
                      qh, kt, v2, later_tab, bias_tab)


def _sb_bwd(qh, kt, k2, vt, wsave, do, earlier_tab, bias_tab):
    t = qh.shape[0]
    qb = _sb_qb(t)
    per = qb // CHUNK
    nkb_all = t // CHUNK

    zero_slot = nkb_all

    def body(q_ref, kt_ref, k2_ref, vt_ref, ws_ref, do_ref, etab_ref, bias_ref,
             dq_ref, dk_ref, dv_ref, acc, gcarry, zbuf, dwbuf, wbuf, wsem, dzbuf):
        h, i = pl.program_id(0), pl.program_id(1)

        @pl.when(i == 0)
        def _():
            dk_ref[...] = jnp.zeros_like(dk_ref)
            dv_ref[...] = jnp.zeros_like(dv_ref)

        nkb = (i + 1) * per
        fetch = lambda kb: pltpu.make_async_copy(ws_ref.at[h, i, kb], wbuf.at[kb], wsem.at[kb])

        def prefetch(kb, _):
            fetch(kb).start()
            return 0

        lax.fori_loop(0, nkb, prefetch, 0)
        q = q_ref[...]
        dob = do_ref[...].astype(BF16)
        lane = lax.broadcasted_iota(jnp.int32, (1, 128), 1)
        acc[...] = jnp.zeros_like(acc)
        gcarry[...] = jnp.zeros_like(gcarry)
        zbuf[...] = _dot(q, kt_ref[0])
        dwbuf[...] = _dot(dob, vt_ref[0])
        dzbuf[...] = jnp.zeros_like(dzbuf)
        wbuf[zero_slot] = jnp.zeros((qb, 256), BF16)

        def gradients(slot, kb):
            dz2 = dzbuf[...]
            acc[...] += _dot(dz2, k2_ref[kb])
            dk2 = _dot(dz2, q, "tn")
            dv2 = _dot(wbuf[slot], dob, "tn")
            dk_ref[kb] += jnp.where(lane < 64, dk2[:128], dk2[128:])
            dv_ref[kb] += jnp.where(lane < 64, dv2[:128], dv2[128:])

        def step(kb, _):
            fetch(kb).wait()
            bias = bias_ref[_sb_bias_index(i, kb, per)]
            z2 = zbuf[...]
            dw2 = dwbuf[...]
            nxt = jnp.minimum(kb + 1, nkb - 1)
            zbuf[...] = _dot(q, kt_ref[nxt])
            dwbuf[...] = _dot(dob, vt_ref[nxt])
            gradients(jnp.where(kb == 0, zero_slot, kb - 1), jnp.maximum(kb - 1, 0))
            w2 = wbuf[kb]
            for hh in range(2):
                sl = slice(128 * hh, 128 * (hh + 1))
                z = z2[:, sl] + bias
                e = jnp.exp(-jnp.abs(z))
                r = 1.0 / (1.0 + e)
                sig = jnp.where(z >= 0, r, e * r)
                gw = w2[:, sl].astype(F32) * dw2[:, sl]
                cu2 = _dot(_split2(gw), etab_ref[...])
                gin = gcarry[hh]
                gcarry[hh] = gin + cu2[:, 128:]
                dzbuf[:, sl] = (gw - sig * (gw + cu2[:, :128] + gin)).astype(BF16)
            return 0

        lax.fori_loop(0, nkb, step, 0)
        gradients(nkb - 1, nkb - 1)
        dq_ref[...] = acc[...] * SB_SCALE

    blk = pl.BlockSpec((qb, 128), lambda h, i: (i, h))
    wide = pl.BlockSpec((None, nkb_all, 128, 256), lambda h, i: (h, 0, 0, 0))
    tall = pl.BlockSpec((None, nkb_all, 256, 128), lambda h, i: (h, 0, 0, 0))
    tab = pl.BlockSpec((256, 256), lambda h, i: (0, 0))
    kv_out = pl.BlockSpec((nkb_all, 128, 128), lambda h, i: (0, 0, h))
    ksh = jax.ShapeDtypeStruct((nkb_all, 128, 1024), F32)
    dq, dk, dv = _pcall(
        body, name="sb_bwd", grid=(8, t // qb),
        in_specs=[blk, wide, tall, wide, pl.BlockSpec(memory_space=pl.ANY), blk, tab,
                  pl.BlockSpec((per + 3, qb, 128), lambda h, i: (0, 0, 0))],
        out_specs=[blk, kv_out, kv_out], out_shape=[jax.ShapeDtypeStruct((t, 1024), F32), ksh, ksh],
        scratch_shapes=[pltpu.VMEM((qb, 128), F32), pltpu.VMEM((2, qb, 128), F32),
                        pltpu.VMEM((qb, 256), F32), pltpu.VMEM((qb, 256), F32),
                        pltpu.VMEM((nkb_all + 1, qb, 256), BF16), pltpu.SemaphoreType.DMA((nkb_all,)),
                        pltpu.VMEM((qb, 256), BF16)],
        compiler_params=_params(dimension_semantics=("parallel", "arbitrary")))(
            qh, kt, k2, vt, wsave, do, earlier_tab, bias_tab)
    return dq, dk.reshape(t, 1024), dv.reshape(t, 1024)


def _adamw_math(w, g, m, v):
    m = ADAM_B1 * m + (1.0 - ADAM_B1) * g
    v = ADAM_B2 * v + (1.0 - ADAM_B2) * (g * g)
    m_hat = m / (1.0 - ADAM_B1 ** ADAM_STEP)
    v_hat = v / (1.0 - ADAM_B2 ** ADAM_STEP)
    delta = -ADAM_LR * (m_hat / (jnp.sqrt(v_hat) + ADAM_EPS) + ADAM_WD * w)
    return delta, m, v


def _adamw(name, w, owns, recvs, m, v, me):
    shape = w.shape
    c = shape[-1]
    nl = len(owns)
    w3, m3, v3 = (a.reshape(nl, -1, c) for a in (w, m, v))
    r = w3.shape[1]
    tr = _tile(r, (256, 128))
    owns = [o.reshape(N_DEV, r, c) for o in owns]
    recvs = [p.reshape(N_DEV - 1, r, c) for p in recvs]

    def body(me_ref, w_ref, *rest):
        own_refs, recv_refs = rest[:nl], rest[nl:2 * nl]
        m_ref, v_ref = rest[2 * nl:2 * nl + 2]
        g_out, d_out, m_out, v_out = rest[2 * nl + 2:]
        layer = pl.program_id(0)

        def grad(k):
            g = own_refs[k][...].astype(F32)
            for s in range(N_DEV - 1):
                g = g + recv_refs[k][s].astype(F32)
            return g

        g = grad(0)
        for k in range(1, nl):
            g = jnp.where(layer == k, grad(k), g)
        d, mn, vn = _adamw_math(w_ref[...], g, m_ref[...], v_ref[...])
        g_out[...] = g
        d_out[...] = d
        m_out[...] = mn
        v_out[...] = vn

    row = pl.BlockSpec((None, tr, c), lambda l, i, me_ref: (l, i, 0))
    own = lambda k: pl.BlockSpec((None, tr, c), lambda l, i, me_ref: (me_ref[0], jnp.where(l == k, i, 0), 0))
    rcv = lambda k: pl.BlockSpec((N_DEV - 1, tr, c), lambda l, i, me_ref: (0, jnp.where(l == k, i, 0), 0))
    osh = jax.ShapeDtypeStruct((nl, r, c), F32)
    grid_spec = pltpu.PrefetchScalarGridSpec(
        num_scalar_prefetch=1, grid=(nl, r // tr),
        in_specs=[row] + [own(k) for k in range(nl)] + [rcv(k) for k in range(nl)] + [row, row],
        out_specs=[row, row, row, row])
    outs = _pcall(body, name=name, grid_spec=grid_spec, out_shape=[osh, osh, osh, osh])(
        me.reshape(1), w3, *owns, *recvs, m3, v3)
    return tuple(o.reshape(shape) for o in outs)


def _place():
    x, y, c = lax.axis_index("x"), lax.axis_index("y"), lax.axis_index("c")
    return x, y, c, 4 * x + 2 * y + c


def _peer(x, y, c, rel):
    return (x ^ ((rel >> 2) & 1), y ^ ((rel >> 1) & 1), c ^ (rel & 1))


def _gather_first(now, later):
    n, k = len(now), len(later)

    def body(*refs):
        ins, outs = refs[:n + k], refs[n + k:2 * (n + k)]
        send, recv, lsem = refs[2 * (n + k):]
        x, y, c, me = _place()
        locals_ = []
        for w in range(n + k):
            local = pltpu.make_async_copy(ins[w], outs[w].at[me], lsem.at[w])
            local.start()
            locals_.append(local)
        for w in range(n):
            for rel in range(1, N_DEV):
                pltpu.make_async_remote_copy(src_ref=ins[w], dst_ref=outs[w].at[me], send_sem=send.at[w, rel - 1],
                                             recv_sem=recv.at[w, rel - 1], device_id=_peer(x, y, c, rel),
                                             device_id_type=MESH).start()
        for w in range(n):
            for rel in range(1, N_DEV):
                cp = pltpu.make_async_remote_copy(src_ref=ins[w], dst_ref=outs[w].at[me ^ rel],
                                                  send_sem=send.at[w, rel - 1], recv_sem=recv.at[w, rel - 1],
                                                  device_id=_peer(x, y, c, rel), device_id_type=MESH)
                cp.wait_send()
                cp.wait_recv()
        for local in locals_:
            local.wait()

    hbm = pl.BlockSpec(memory_space=pl.ANY)
    arrays = list(now) + list(later)
    return _pcall(body, name="gather_first", in_specs=[hbm] * (n + k), out_specs=[hbm] * (n + k),
                  out_shape=[jax.ShapeDtypeStruct((N_DEV,) + a.shape, a.dtype) for a in arrays],
                  scratch_shapes=[pltpu.SemaphoreType.DMA((n, N_DEV - 1)), pltpu.SemaphoreType.DMA((n, N_DEV - 1)),
                                  pltpu.SemaphoreType.DMA((n + k,))],
                  compiler_params=_params(has_side_effects=True))(*arrays)


_HBM = pl.BlockSpec(memory_space=pltpu.HBM)
_SEM = pl.BlockSpec(memory_space=pltpu.SEMAPHORE)
_DATAFLOW = pltpu.SideEffectType.DATAFLOW_SIDE_EFFECTING


def _exchange_refs(srcs, lands, mode, me, rel, j):
    if mode == "gather":
        return srcs[j], lands[j].at[me], lands[j].at[me ^ rel]
    return srcs[j].at[me ^ rel], lands[j].at[rel - 1], lands[j].at[rel - 1]


def _exchange_start(name, srcs, lands, mode):
    n = len(srcs)

    def body(*refs):
        ins, lnd = refs[:n], refs[n:2 * n]
        send, recv = refs[2 * n], refs[2 * n + 1]
        token = refs[-1]
        x, y, c, me = _place()
        for j in range(n):
            for rel in range(1, N_DEV):
                src, dst, _ = _exchange_refs(ins, lnd, mode, me, rel, j)
                pltpu.make_async_remote_copy(src_ref=src, dst_ref=dst, send_sem=send.at[j * (N_DEV - 1) + rel - 1],
                                             recv_sem=recv.at[j * (N_DEV - 1) + rel - 1],
                                             device_id=_peer(x, y, c, rel), device_id_type=MESH).start()
        token[...] = jnp.zeros_like(token)

    sems = pltpu.SemaphoreType.DMA((n * (N_DEV - 1),))
    hbm_like = lambda a: pltpu.HBM(a.shape, a.dtype)
    outs = _pcall(body, name=name + "_start",
                  in_specs=[_HBM] * (2 * n), out_specs=[_SEM, _SEM] + [_HBM] * (2 * n) + [pl.BlockSpec(memory_space=pltpu.VMEM)],
                  out_shape=[sems, sems] + [hbm_like(a) for a in srcs] + [hbm_like(a) for a in lands]
                  + [jax.ShapeDtypeStruct((8, 128), F32)],
                  input_output_aliases={i: 2 + i for i in range(2 * n)},
                  compiler_params=pltpu.CompilerParams(has_side_effects=_DATAFLOW))(
                      *[pltpu.with_memory_space_constraint(a, pltpu.HBM) for a in list(srcs) + list(lands)])
    return dict(name=name, mode=mode, n=n, send=outs[0], recv=outs[1], srcs=outs[2:2 + n], lands=outs[2 + n:2 + 2 * n],
                token=outs[-1][0, 0])


def _exchange_wait(ex, after):
    n, mode = ex["n"], ex["mode"]

    def body(*refs):
        ins, lnd = refs[:n], refs[n:2 * n]
        send, recv = refs[2 * n], refs[2 * n + 1]
        x, y, c, me = _place()
        for j in range(n):
            for rel in range(1, N_DEV):
                src, dst, landed = _exchange_refs(ins, lnd, mode, me, rel, j)
                pltpu.make_async_remote_copy(src_ref=src, dst_ref=dst, send_sem=send.at[j * (N_DEV - 1) + rel - 1],
                                             recv_sem=recv.at[j * (N_DEV - 1) + rel - 1],
                                             device_id=_peer(x, y, c, rel), device_id_type=MESH).wait_send()
                pltpu.make_async_remote_copy(src_ref=src, dst_ref=landed, send_sem=send.at[j * (N_DEV - 1) + rel - 1],
                                             recv_sem=recv.at[j * (N_DEV - 1) + rel - 1],
                                             device_id=_peer(x, y, c, rel), device_id_type=MESH).wait_recv()

    hbm_like = lambda a: pltpu.HBM(a.shape, a.dtype)
    arrays = list(ex["srcs"]) + list(ex["lands"])
    outs = _pcall(body, name=ex["name"] + "_wait",
                  in_specs=[_HBM] * (2 * n) + [_SEM, _SEM, pl.BlockSpec(memory_space=pl.ANY)],
                  out_specs=[_HBM] * (2 * n), out_shape=[hbm_like(a) for a in arrays],
                  input_output_aliases={i: i for i in range(2 * n)},
                  compiler_params=pltpu.CompilerParams(has_side_effects=_DATAFLOW))(
                      *arrays, ex["send"], ex["recv"], after)
    return outs[:n], outs[n:]


def _scatter_start(name, grads):
    lands = [lax.empty((N_DEV - 1,) + g.shape[1:], g.dtype) for g in grads]
    return _exchange_start(name, grads, lands, "scatter")


ROW_MIX, ROW_MLP, ROW_CB, ROW_LG, ROW_LB, ROW_QN, ROW_KN, ROW_LOSS = 0, 2, 4, 5, 6, 7, 8, 9
ROW_META, ROW_CW, ROW_GN, SMALL_ROWS = 16, 32, 64, 72


def _allreduce_small(part):
    def body(p_ref, o_ref, slots, send, recv):
        x, y, c, me = _place()
        slots[me] = p_ref[...]
        for rel in range(1, N_DEV):
            pltpu.make_async_remote_copy(src_ref=p_ref, dst_ref=slots.at[me], send_sem=send.at[rel - 1],
                                         recv_sem=recv.at[rel - 1], device_id=_peer(x, y, c, rel),
                                         device_id_type=MESH).start()
        for rel in range(1, N_DEV):
            cp = pltpu.make_async_remote_copy(src_ref=p_ref, dst_ref=slots.at[me ^ rel], send_sem=send.at[rel - 1],
                                              recv_sem=recv.at[rel - 1], device_id=_peer(x, y, c, rel),
                                              device_id_type=MESH)
            cp.wait_send()
            cp.wait_recv()
        tot = slots[0]
        for s in range(1, N_DEV):
            tot = tot + slots[s]
        o_ref[...] = tot
        for row in (ROW_QN, ROW_KN):
            v = tot[row:row + 1, :]
            f = v[:, 0:128]
            for k in range(1, 8):
                f = f + v[:, 128 * k:128 * (k + 1)]
            o_ref[row:row + 1, 0:64] = f[:, 0:64] + f[:, 64:128]

    vm = pl.BlockSpec(memory_space=pltpu.VMEM)
    return _pcall(body, name="allreduce_small", in_specs=[vm], out_specs=vm,
                  out_shape=jax.ShapeDtypeStruct(part.shape, F32),
                  scratch_shapes=[pltpu.VMEM((N_DEV,) + part.shape, F32), pltpu.SemaphoreType.DMA((N_DEV - 1,)),
                                  pltpu.SemaphoreType.DMA((N_DEV - 1,))],
                  compiler_params=_params(has_side_effects=True))(part)


def _adamw_small(w, g, m, v):
    def body(w_ref, g_ref, m_ref, v_ref, d_out, m_out, v_out):
        d, mn, vn = _adamw_math(w_ref[...], g_ref[...], m_ref[...], v_ref[...])
        d_out[...] = d
        m_out[...] = mn
        v_out[...] = vn

    osh = jax.ShapeDtypeStruct(w.shape, F32)
    return _pcall(body, name="adamw_small", out_shape=[osh, osh, osh])(w, g, m, v)


def _local_step(h0, target, p, weight, emit):
    t = h0.shape[0]
    tables = _ret_tables(t)
    bd, later_tab, earlier_tab, bias_tab = _seg_tables(_sb_qb(t))
    row = lambda a, i: a[i:i + 1]

    hn_a = _rms_fwd("rms_mix0", h0, row(p["norm_mix_g"], 0))
    w_in = weight("w_in", hn_a)
    proj = _mm_cols("proj_in", hn_a, w_in, ())
    o_ret, states = _ret_fwd(proj, tables)
    gn_flat = p["gn_g"].reshape(1, 1024)
    cat = _gn_gate_fwd(o_ret, proj, gn_flat)
    cat, hdn, ycv = _conv_fwd(cat, proj, p["conv_w"], p["conv_b"], p["ln_g"], p["ln_b"])
    w_out = weight("w_out", cat)
    h1 = _mm_rows("mix_out", cat, w_out, h0)
    hn_b = _rms_fwd("rms_mlp0", h1, row(p["norm_mlp_g"], 0))
    w1_0, w2_0 = weight("w1_0", hn_b), weight("w2_0", hn_b)
    a0, s0 = _mm_cols("mlp0_up", hn_b, w1_0, (), epi="relu2")
    h2 = _mm_rows("mlp0_down", s0, w2_0, h1)

    hn_c = _rms_fwd("rms_mix1", h2, row(p["norm_mix_g"], 1))
    w_qkv = weight("w_qkv", hn_c)
    qkv = _mm_cols("qkv", hn_c, w_qkv, ())
    qg = jnp.tile(p["qn_g"], (1, 16))
    kg = jnp.tile(p["kn_g"], (1, 16))
    qh, kt, k2, vt, v2 = _qk_norm_fwd(qkv, qg, kg, bd)
    o_sb, w_sb = _sb_fwd(qh, kt, v2, later_tab, bias_tab)
    w_o = weight("w_o", o_sb)
    h3 = _mm_rows("attn_out", o_sb, w_o, h2)
    hn_d = _rms_fwd("rms_mlp1", h3, row(p["norm_mlp_g"], 1))
    w1_1, w2_1 = weight("w1_1", hn_d), weight("w2_1", hn_d)
    a1, s1 = _mm_cols("mlp1_up", hn_d, w1_1, (), epi="relu2")
    h4 = _mm_rows("mlp1_down", s1, w2_1, h3)

    dh, loss = _loss_bwd(h4, target)

    def mlp_bwd(tag, layer, w1, w2, dh, h_in, hn, a, s):
        da = _mm_rows_t(f"{tag}_dact", dh, w2, (), out_dtype=BF16, epi="drelu2", extra=a)
        dw2 = _wgrad_rows(f"{tag}_dw2", s, dh, 512)
        dw1 = _wgrad_cols(f"{tag}_dw1", hn, da, 512)
        tok = emit(tag, [dw1, dw2])
        dhn = _mm_cols_t(f"{tag}_dhn", da, w1)
        return _rms_bwd(f"{tag}_rms_bwd", dhn, h_in, row(p["norm_mlp_g"], layer) + tok, dh)

    dh, dg_mlp1 = mlp_bwd("mlp1", 1, w1_1, w2_1, dh, h3, hn_d, a1, s1)

    do_sb = _mm_rows_t("attn_dout", dh, w_o, ())
    dw_o = _wgrad_rows("attn_dwo", o_sb, dh, 128)
    dq, dk, dv = _sb_bwd(qh, kt, k2, vt, w_sb, do_sb, earlier_tab, bias_tab)
    dqkv, dqg, dkg = _qk_norm_bwd(qkv, dq, dk, dv, qg, kg, bd)
    dw_qkv = _wgrad_cols("qkv_dw", hn_c, dqkv, 384)
    tok = emit("attn", [dw_qkv, dw_o])
    dhn = _mm_cols_t("qkv_dhn", dqkv, w_qkv)
    dh, dg_mix1 = _rms_bwd("mix1_rms_bwd", dhn, h2, row(p["norm_mix_g"], 1) + tok, dh)

    dh, dg_mlp0 = mlp_bwd("mlp0", 0, w1_0, w2_0, dh, h1, hn_b, a0, s0)

    dcat = _mm_rows_t("mix_dcat", dh, w_out, ())
    dw_out = _wgrad_rows("mix_dwout", cat, dh, 256)
    do_ret, dgate, dgn = _gn_gate_bwd(dcat, o_ret, proj, gn_flat)
    dqk_r, dv_r = _ret_bwd(proj, states, do_ret, tables)
    dy, dlg, dlb, dcb = _conv_bwd_ln(dcat, ycv, p["ln_g"], p["ln_b"])
    dua, dug, dcw = _conv_bwd_taps(dy, hdn, proj, p["conv_w"])
    dproj = jnp.concatenate([dqk_r, dv_r, dgate, dua, dug], axis=1)
    dw_in = _wgrad_cols("proj_dw", hn_a, dproj, 640)
    tok = emit("mix0", [dw_in, dw_out])
    dhn = _mm_cols_t("proj_dhn", dproj, w_in)
    dh, dg_mix0 = _rms_bwd("mix0_rms_bwd", dhn, h0, row(p["norm_mix_g"], 0) + tok, dh)

    rid = lax.broadcasted_iota(jnp.int32, (16, 1), 0)
    loss_row = jnp.broadcast_to(loss[0:1, 0:1], (1, D_MODEL))
    vecs = sum(jnp.where(rid == k, v, 0.0)
               for k, v in enumerate((dg_mix0, dg_mix1, dg_mlp0, dg_mlp1, dcb, dlg, dlb, dqg, dkg, loss_row)))
    small = jnp.concatenate([vecs, dh[PAD_FRONT:TOK0], dcw, jnp.where(rid[:8] == 0, dgn, 0.0)], axis=0)
    return dh[TOK0:], small


_SMALL_NAMES = ("meta", "norm_mix_g", "norm_mlp_g", "even_ret_gn_g", "even_conv_w", "even_conv_b",
                "even_conv_ln_g", "even_conv_ln_b", "odd_q_norm_g", "odd_k_norm_g")
_BIG_NAMES = ("even_w_in", "even_w_out", "odd_w_qkv", "odd_w_o", "mlp_w1", "mlp_w2")
_ORDER = ("meta", "norm_mix_g", "norm_mlp_g", "even_w_in", "even_ret_gn_g", "even_conv_w", "even_conv_b",
          "even_conv_ln_g", "even_conv_ln_b", "even_w_out", "odd_w_qkv", "odd_q_norm_g", "odd_k_norm_g",
          "odd_w_o", "mlp_w1", "mlp_w2")


def _pack128(a):
    flat = a.reshape(-1)
    n = flat.shape[0]
    rows = -(-n // 128)
    rows8 = -(-rows // 8) * 8
    return jnp.pad(flat, (0, rows8 * 128 - n)).reshape(rows8, 128)


def kernel(x, meta, norm_mix_g, norm_mlp_g, even_w_in, even_ret_gn_g, even_conv_w, even_conv_b, even_conv_ln_g, even_conv_ln_b, even_w_out, odd_w_qkv, odd_q_norm_g, odd_k_norm_g, odd_w_o, mlp_w1, mlp_w2, loss_target, m_meta, m_norm_mix_g, m_norm_mlp_g, m_even_w_in, m_even_ret_gn_g, m_even_conv_w, m_even_conv_b, m_even_conv_ln_g, m_even_conv_ln_b, m_even_w_out, m_odd_w_qkv, m_odd_q_norm_g, m_odd_k_norm_g, m_odd_w_o, m_mlp_w1, m_mlp_w2, v_meta, v_norm_mix_g, v_norm_mlp_g, v_even_w_in, v_even_ret_gn_g, v_even_conv_w, v_even_conv_b, v_even_conv_ln_g, v_even_conv_ln_b, v_even_w_out, v_odd_w_qkv, v_odd_q_norm_g, v_odd_k_norm_g, v_odd_w_o, v_mlp_w1, v_mlp_w2):
    w = dict(meta=meta, norm_mix_g=norm_mix_g, norm_mlp_g=norm_mlp_g, even_w_in=even_w_in,
             even_ret_gn_g=even_ret_gn_g, even_conv_w=even_conv_w, even_conv_b=even_conv_b,
             even_conv_ln_g=even_conv_ln_g, even_conv_ln_b=even_conv_ln_b, even_w_out=even_w_out,
             odd_w_qkv=odd_w_qkv, odd_q_norm_g=odd_q_norm_g, odd_k_norm_g=odd_k_norm_g, odd_w_o=odd_w_o,
             mlp_w1=mlp_w1, mlp_w2=mlp_w2)
    mom = dict(meta=m_meta, norm_mix_g=m_norm_mix_g, norm_mlp_g=m_norm_mlp_g, even_w_in=m_even_w_in,
               even_ret_gn_g=m_even_ret_gn_g, even_conv_w=m_even_conv_w, even_conv_b=m_even_conv_b,
               even_conv_ln_g=m_even_conv_ln_g, even_conv_ln_b=m_even_conv_ln_b, even_w_out=m_even_w_out,
               odd_w_qkv=m_odd_w_qkv, odd_q_norm_g=m_odd_q_norm_g, odd_k_norm_g=m_odd_k_norm_g, odd_w_o=m_odd_w_o,
               mlp_w1=m_mlp_w1, mlp_w2=m_mlp_w2)
    var = dict(meta=v_meta, norm_mix_g=v_norm_mix_g, norm_mlp_g=v_norm_mlp_g, even_w_in=v_even_w_in,
               even_ret_gn_g=v_even_ret_gn_g, even_conv_w=v_even_conv_w, even_conv_b=v_even_conv_b,
               even_conv_ln_g=v_even_conv_ln_g, even_conv_ln_b=v_even_conv_ln_b, even_w_out=v_even_w_out,
               odd_w_qkv=v_odd_w_qkv, odd_q_norm_g=v_odd_q_norm_g, odd_k_norm_g=v_odd_k_norm_g, odd_w_o=v_odd_w_o,
               mlp_w1=v_mlp_w1, mlp_w2=v_mlp_w2)
    me = 4 * lax.axis_index("x") + 2 * lax.axis_index("y") + lax.axis_index("c")

    small_in = jnp.concatenate([meta, jnp.pad(even_conv_w[0], ((0, 1), (0, 0))),
                                jnp.pad(even_ret_gn_g[0], ((0, 4), (0, 96)))], axis=0)
    b16 = lambda a: a.astype(BF16)
    later_src = dict(w_out=b16(even_w_out[0]), w1_0=b16(mlp_w1[0]), w2_0=b16(mlp_w2[0]),
                     w_qkv=b16(odd_w_qkv[0]), w_o=b16(odd_w_o[0]), w1_1=b16(mlp_w1[1]), w2_1=b16(mlp_w2[1]))
    landed = _gather_first([b16(even_w_in[0]), small_in], list(later_src.values()))
    g_in, g_small = landed[0], landed[1]
    own_slot = dict(zip(later_src, landed[2:]))
    groups = (("gather_l0", ("w_out", "w1_0", "w2_0")), ("gather_attn", ("w_qkv", "w_o")),
              ("gather_l1", ("w1_1", "w2_1")))
    pending = {}
    gather_tok = jnp.zeros((), F32)
    for gname, names in groups:
        ex = _exchange_start(gname, [later_src[n] for n in names], [own_slot[n] for n in names], "gather")
        gather_tok = gather_tok + ex["token"]
        for n in names:
            pending[n] = (ex, names)
    arrived = dict(w_in=g_in)

    def weight(name, after):
        if name not in arrived:
            ex, names = pending[name]
            arrived.update(zip(names, _exchange_wait(ex, after)[1]))
        return arrived[name]

    cols = lambda a: jnp.transpose(a, (1, 0, 2)).reshape(a.shape[1], -1)
    p = dict(norm_mix_g=norm_mix_g + gather_tok, norm_mlp_g=norm_mlp_g, conv_b=even_conv_b, ln_g=even_conv_ln_g,
             ln_b=even_conv_ln_b, qn_g=odd_q_norm_g, kn_g=odd_k_norm_g,
             gn_g=cols(g_small[:, 48:52, :32]),
             conv_w=jnp.pad(cols(g_small[:, 16:47]), ((0, 1), (0, 0))))
    meta_full = cols(g_small[:, 0:16])

    scatters = {}

    def emit(tag, grads):
        scatters[tag] = _scatter_start("scatter_" + tag, grads)
        return scatters[tag]["token"]

    h0 = jnp.concatenate([jnp.zeros((PAD_FRONT, D_MODEL), F32), meta_full, x[0]], axis=0)
    grad_x, small_part = _local_step(h0, loss_target[0], p, weight, emit)
    tot = _allreduce_small(small_part)
    loss = tot[ROW_LOSS, 0]

    got = {tag: _exchange_wait(ex, tot) for tag, ex in scatters.items()}
    pick = lambda tag, j: (got[tag][0][j], got[tag][1][j])
    terms = dict(even_w_in=[pick("mix0", 0)], even_w_out=[pick("mix0", 1)], odd_w_qkv=[pick("attn", 0)],
                 odd_w_o=[pick("attn", 1)], mlp_w1=[pick("mlp0", 0), pick("mlp1", 0)],
                 mlp_w2=[pick("mlp0", 1), pick("mlp1", 1)])
    out = {}
    for name in _BIG_NAMES:
        owns, recvs = zip(*terms[name])
        out[name] = _adamw("adamw_" + name, w[name], list(owns), list(recvs), mom[name], var[name], me)

    shard_cols = lambda a, width: lax.dynamic_slice_in_dim(a, me * width, width, axis=1)
    one = lambda r: tot[r:r + 1]
    small_g = dict(
        norm_mix_g=tot[ROW_MIX:ROW_MIX + 2], norm_mlp_g=tot[ROW_MLP:ROW_MLP + 2],
        even_conv_b=one(ROW_CB), even_conv_ln_g=one(ROW_LG), even_conv_ln_b=one(ROW_LB),
        odd_q_norm_g=one(ROW_QN)[:, :64], odd_k_norm_g=one(ROW_KN)[:, :64],
        meta=shard_cols(tot[ROW_META:ROW_META + N_META], 128),
        even_conv_w=shard_cols(tot[ROW_CW:ROW_CW + CONV_WIDTH], 128)[None],
        even_ret_gn_g=shard_cols(tot[ROW_GN].reshape(4, 256), 32)[None])
    packs = {n: (_pack128(w[n]), _pack128(small_g[n]), _pack128(mom[n]), _pack128(var[n])) for n in _SMALL_NAMES}
    cat4 = [jnp.concatenate([packs[n][i] for n in _SMALL_NAMES], axis=0) for i in range(4)]
    d_s, m_s, v_s = _adamw_small(*cat4)
    r0 = 0
    for n in _SMALL_NAMES:
        rows = packs[n][0].shape[0]
        size = w[n].size
        take = lambda a: a[r0:r0 + rows].reshape(-1)[:size].reshape(w[n].shape)
        out[n] = (small_g[n].reshape(w[n].shape), take(d_s), take(m_s), take(v_s))
        r0 += rows

    res = [loss, grad_x[None]]
    for i in range(4):
        res.extend(out[n][i] for n in _ORDER)
    return tuple(res)
```

```python
import functools

import numpy as np
import jax
import jax.numpy as jnp
from jax import lax
from jax.experimental import pallas as pl
from jax.experimental.pallas import tpu as pltpu

F32 = jnp.float32
BF16 = jnp.bfloat16

D_MODEL = 1024
N_META = 16
CHUNK = 128
PAD_FRONT = 112
TOK0 = PAD_FRONT + N_META
EPS = 1e-6
N_DEV = 8
RET_HEADS = 4
RET_DECAY_OFFSET = 5.0
ROPE_BASE = 10000.0
CONV_WIDTH = 31
HALO = 32
SB_SCALE = 64 ** -0.5
RET_SCALE = 128 ** -0.5
ADAM_LR, ADAM_B1, ADAM_B2, ADAM_EPS, ADAM_WD, ADAM_STEP = 0.001, 0.9, 0.999, 1e-08, 0.01, 10
VMEM_LIMIT = 56 * 1024 * 1024
MESH = pl.DeviceIdType.MESH


def _pcall(body, **kw):
    return pl.pallas_call(body, **kw)


def _params(**kw):
    return pltpu.CompilerParams(vmem_limit_bytes=VMEM_LIMIT, **kw)


def _tile(n, cands):
    for c in cands:
        if n % c == 0:
            return c
    raise ValueError(f"no tile for {n} in {cands}")


def _sigmoid(x):
    return 1.0 / (1.0 + jnp.exp(-x))


_DIMS = {
    "nn": (((1,), (0,)), ((), ())),
    "nt": (((1,), (1,)), ((), ())),
    "tn": (((0,), (0,)), ((), ())),
}


def _matmul(name, a, b, *, grid, a_spec, b_spec, o_spec, out_shape, contract, acc_shape,
            epi="plain", extra=None, extra_spec=None):
    nk = grid[2]
    dims = _DIMS[contract]
    n_in = 3 if extra is not None else 2
    n_out = 2 if epi == "relu2" else 1

    def body(*refs):
        a_ref, b_ref = refs[0], refs[1]
        e_ref = refs[2] if extra is not None else None
        outs = refs[n_in:n_in + n_out]
        acc = refs[-1]
        k = pl.program_id(2)
        part = lax.dot_general(a_ref[...].astype(BF16), b_ref[...].astype(BF16), dims, preferred_element_type=F32)
        if nk > 1:
            @pl.when(k == 0)
            def _():
                acc[...] = jnp.zeros_like(acc)

            acc[...] += part

        @pl.when(k == nk - 1)
        def _():
            r = acc[...] if nk > 1 else part
            if epi == "plain":
                outs[0][...] = r.astype(outs[0].dtype)
            elif epi == "residual":
                outs[0][...] = (r + e_ref[...]).astype(outs[0].dtype)
            elif epi == "relu2":
                outs[0][...] = r
                rr = jnp.maximum(r, 0.0)
                outs[1][...] = (rr * rr).astype(BF16)
            elif epi == "drelu2":
                outs[0][...] = (r * (2.0 * jnp.maximum(e_ref[...], 0.0))).astype(outs[0].dtype)

    in_specs = [a_spec, b_spec] + ([extra_spec] if extra is not None else [])
    args = (a, b) + ((extra,) if extra is not None else ())
    if n_out == 2:
        out_specs = [o_spec, o_spec]
    else:
        out_specs = o_spec
    return _pcall(body, name=name, grid=grid, in_specs=in_specs, out_specs=out_specs,
                  out_shape=out_shape, scratch_shapes=[pltpu.VMEM(acc_shape, F32)],
                  compiler_params=_params(dimension_semantics=("parallel", "parallel", "arbitrary")))(*args)


def _tm(t):
    return _tile(t, (1408, 768, 384, 128))


def _mm_cols(name, a, wb, lead, out_dtype=F32, epi="plain"):
    t, kdim = a.shape
    n = wb.shape[-1]
    tm, tk = _tm(t), _tile(kdim, (1024, 512))
    nl = len(lead)
    b_spec = pl.BlockSpec((None,) * (1 + nl) + (tk, n), lambda i, j, k: (j,) + lead + (k, 0))
    o_spec = pl.BlockSpec((tm, n), lambda i, j, k: (i, j))
    if epi == "relu2":
        out_shape = [jax.ShapeDtypeStruct((t, N_DEV * n), F32), jax.ShapeDtypeStruct((t, N_DEV * n), BF16)]
    else:
        out_shape = jax.ShapeDtypeStruct((t, N_DEV * n), out_dtype)
    return _matmul(name, a, wb, grid=(t // tm, N_DEV, kdim // tk),
                   a_spec=pl.BlockSpec((tm, tk), lambda i, j, k: (i, k)), b_spec=b_spec, o_spec=o_spec,
                   out_shape=out_shape, contract="nn", acc_shape=(tm, n), epi=epi)


def _tm_deep(t, kdim):
    return _tm(t) if kdim <= 2048 else _tile(t, (704, 384, 128))


def _mm_cols_t(name, a, wb):
    t = a.shape[0]
    nb, kdim, n = wb.shape
    tm, tn = _tm_deep(t, nb * n), _tile(kdim, (512,))

    def body(a_ref, b_ref, o_ref):
        acc = _dot(a_ref[:, 0:n].astype(BF16), b_ref[0], "nt")
        for j in range(1, nb):
            acc = acc + _dot(a_ref[:, j * n:(j + 1) * n].astype(BF16), b_ref[j], "nt")
        o_ref[...] = acc

    return _pcall(body, name=name, grid=(t // tm, kdim // tn),
                  in_specs=[pl.BlockSpec((tm, nb * n), lambda i, j: (i, 0)),
                            pl.BlockSpec((nb, tn, n), lambda i, j: (0, j, 0))],
                  out_specs=pl.BlockSpec((tm, tn), lambda i, j: (i, j)),
                  out_shape=jax.ShapeDtypeStruct((t, kdim), F32),
                  compiler_params=_params(dimension_semantics=("parallel", "parallel")))(a, wb)


def _mm_rows(name, a, wb, residual):
    t = a.shape[0]
    nb, r, n = wb.shape
    tm, tn = _tm_deep(t, nb * r), _tile(n, (512,))

    def body(a_ref, b_ref, r_ref, o_ref):
        o_ref[...] = r_ref[...] + _dot(a_ref[...].astype(BF16), b_ref[...].reshape(nb * r, tn))

    o_spec = pl.BlockSpec((tm, tn), lambda i, j: (i, j))
    return _pcall(body, name=name, grid=(t // tm, n // tn),
                  in_specs=[pl.BlockSpec((tm, nb * r), lambda i, j: (i, 0)),
                            pl.BlockSpec((nb, r, tn), lambda i, j: (0, 0, j)), o_spec],
                  out_specs=o_spec, out_shape=jax.ShapeDtypeStruct((t, n), F32),
                  compiler_params=_params(dimension_semantics=("parallel", "parallel")))(a, wb, residual)


def _mm_rows_t(name, a, wb, lead, out_dtype=F32, epi="plain", extra=None):
    t, n = a.shape
    r = wb.shape[-2]
    tm, tk = _tm(t), _tile(n, (1024,))
    nl = len(lead)
    b_spec = pl.BlockSpec((None,) * (1 + nl) + (r, tk), lambda i, j, k: (j,) + lead + (0, k))
    o_spec = pl.BlockSpec((tm, r), lambda i, j, k: (i, j))
    return _matmul(name, a, wb, grid=(t // tm, N_DEV, n // tk),
                   a_spec=pl.BlockSpec((tm, tk), lambda i, j, k: (i, k)), b_spec=b_spec, o_spec=o_spec,
                   out_shape=jax.ShapeDtypeStruct((t, N_DEV * r), out_dtype), contract="nt",
                   acc_shape=(tm, r), epi=epi, extra=extra, extra_spec=o_spec if extra is not None else None)


def _wgrad_cols(name, x, dy, n):
    t, kdim = x.shape
    tk = _tm(t)
    return _matmul(name, x, dy, grid=(1, N_DEV, t // tk),
                   a_spec=pl.BlockSpec((tk, kdim), lambda i, j, k: (k, 0)),
                   b_spec=pl.BlockSpec((tk, n), lambda i, j, k: (k, j)),
                   o_spec=pl.BlockSpec((None, kdim, n), lambda i, j, k: (j, 0, 0)),
                   out_shape=jax.ShapeDtypeStruct((N_DEV, kdim, n), BF16), contract="tn", acc_shape=(kdim, n))


def _wgrad_rows(name, x, dy, r):
    t = x.shape[0]
    n = dy.shape[1]
    tk, tn = _tm(t), _tile(n, (512,))
    tm = min(N_DEV * r, 1024)
    out = _matmul(name, x, dy, grid=(N_DEV * r // tm, n // tn, t // tk),
                  a_spec=pl.BlockSpec((tk, tm), lambda i, j, k: (k, i)),
                  b_spec=pl.BlockSpec((tk, tn), lambda i, j, k: (k, j)),
                  o_spec=pl.BlockSpec((tm, tn), lambda i, j, k: (i, j)),
                  out_shape=jax.ShapeDtypeStruct((N_DEV * r, n), BF16), contract="tn", acc_shape=(tm, tn))
    return out.reshape(N_DEV, r, n)


def _rows(t):
    return _tile(t, (384, 128))


def _rms_fwd(name, h, g):
    t = h.shape[0]
    tr = _rows(t)

    def body(h_ref, g_ref, o_ref):
        x = h_ref[...]
        r = lax.rsqrt(jnp.mean(x * x, axis=-1, keepdims=True) + EPS)
        o_ref[...] = (x * r * g_ref[...]).astype(BF16)

    row = pl.BlockSpec((tr, D_MODEL), lambda i: (i, 0))
    vec = pl.BlockSpec((1, D_MODEL), lambda i: (0, 0))
    return _pcall(body, name=name, grid=(t // tr,), in_specs=[row, vec], out_specs=row,
                  out_shape=jax.ShapeDtypeStruct((t, D_MODEL), BF16))(h, g)


def _rms_bwd(name, dhn, h, g, dres):
    t = h.shape[0]
    tr = _rows(t)

    def body(d_ref, h_ref, g_ref, r_ref, o_ref, dg_ref):
        @pl.when(pl.program_id(0) == 0)
        def _():
            dg_ref[...] = jnp.zeros_like(dg_ref)

        x = h_ref[...]
        d = d_ref[...]
        r = lax.rsqrt(jnp.mean(x * x, axis=-1, keepdims=True) + EPS)
        u = d * g_ref[...]
        m = jnp.mean(u * x, axis=-1, keepdims=True)
        o_ref[...] = r_ref[...] + r * u - x * (r * r * r * m)
        dg_ref[...] += jnp.sum(d * x * r, axis=0, keepdims=True)

    row = pl.BlockSpec((tr, D_MODEL), lambda i: (i, 0))
    vec = pl.BlockSpec((1, D_MODEL), lambda i: (0, 0))
    return _pcall(body, name=name, grid=(t // tr,), in_specs=[row, row, vec, row], out_specs=[row, vec],
                  out_shape=[jax.ShapeDtypeStruct((t, D_MODEL), F32), jax.ShapeDtypeStruct((1, D_MODEL), F32)])(
                      dhn, h, g, dres)


def _loss_bwd(h, target):
    t = h.shape[0]
    nb = t // CHUNK

    def body(h_ref, t_ref, d_ref, l_ref):
        i = pl.program_id(0)

        @pl.when(i == 0)
        def _():
            d_ref[...] = jnp.zeros_like(d_ref)
            l_ref[...] = jnp.zeros_like(l_ref)

        @pl.when(i > 0)
        def _():
            diff = h_ref[...] - t_ref[...]
            d_ref[...] = diff * (1.0 / D_MODEL)
            l_ref[...] += jnp.sum(diff * diff) * (0.5 / D_MODEL)

    return _pcall(body, name="loss_bwd", grid=(nb,),
                  in_specs=[pl.BlockSpec((CHUNK, D_MODEL), lambda i: (i, 0)),
                            pl.BlockSpec((CHUNK, D_MODEL), lambda i: (jnp.maximum(i - 1, 0), 0))],
                  out_specs=[pl.BlockSpec((CHUNK, D_MODEL), lambda i: (i, 0)),
                             pl.BlockSpec((8, 128), lambda i: (0, 0))],
                  out_shape=[jax.ShapeDtypeStruct((t, D_MODEL), F32), jax.ShapeDtypeStruct((8, 128), F32)])(h, target)


def _ret_tables(t):
    hh = np.arange(RET_HEADS, dtype=np.float64)
    log_g = np.log1p(-np.exp2(-RET_DECAY_OFFSET - hh))
    idx = np.arange(CHUNK, dtype=np.float64)
    diff = idx[:, None] - idx[None, :]
    dmat = np.where(diff[None] >= 0, np.exp(np.maximum(diff, 0.0)[None] * log_g[:, None, None]), 0.0)
    qdec = np.exp((idx + 1.0)[None, :, None] * log_g[:, None, None]) * np.ones((1, 1, CHUNK))
    kdec = np.exp((CHUNK - 1 - idx)[None, :, None] * log_g[:, None, None]) * np.ones((1, 1, CHUNK))
    half = CHUNK // 2
    inv_freq = (ROPE_BASE ** (-np.arange(half, dtype=np.float32) / half)).astype(np.float32)
    ang = (np.arange(t, dtype=np.float32)[:, None] * inv_freq[None, :]).astype(np.float32).astype(np.float64)
    cos2 = np.concatenate([np.cos(ang), np.cos(ang)], axis=1)
    sin2 = np.concatenate([-np.sin(ang), np.sin(ang)], axis=1)
    return tuple(jnp.asarray(v, F32) for v in (dmat, qdec, kdec, cos2, sin2))


def _rot(x, c, s):
    return x * c + pltpu.roll(x, CHUNK // 2, 1) * s


def _unrot(dx, c, s):
    return dx * c + pltpu.roll(dx * s, CHUNK // 2, 1)


def _dot(a, b, contract="nn"):
    return lax.dot_general(a, b, _DIMS[contract], preferred_element_type=F32)


def _ret_fwd(proj, tables):
    t = proj.shape[0]
    nch = t // CHUNK
    dmat, qdec, kdec, cos2, sin2 = tables

    def body(qk_ref, v_ref, c_ref, s_ref, dm_ref, qd_ref, kd_ref, o_ref, st_ref, state):
        @pl.when(pl.program_id(0) == 0)
        def _():
            state[...] = jnp.zeros_like(state)

        c, s = c_ref[...], s_ref[...]
        for h in range(RET_HEADS):
            q = _rot(qk_ref[:, 128 * h:128 * (h + 1)], c, s)
            k = _rot(qk_ref[:, 512 + 128 * h:512 + 128 * (h + 1)], c, s) * RET_SCALE
            vb = v_ref[:, 256 * h:256 * (h + 1)].astype(BF16)
            st = state[h]
            st_ref[h] = st
            sc = _dot(q.astype(BF16), k.astype(BF16), "nt") * dm_ref[h]
            o = _dot(sc.astype(BF16), vb)
            o += _dot((q * qd_ref[h]).astype(BF16), st.astype(BF16))
            o_ref[:, 256 * h:256 * (h + 1)] = o
            kv = _dot((k * kd_ref[h]).astype(BF16), vb, "tn")
            state[h] = qd_ref[h, CHUNK - 1:CHUNK, 0:1] * st + kv

    tab = pl.BlockSpec((RET_HEADS, CHUNK, CHUNK), lambda n: (0, 0, 0))
    pos = pl.BlockSpec((CHUNK, CHUNK), lambda n: (n, 0))
    return _pcall(
        body, name="ret_fwd", grid=(nch,),
        in_specs=[pl.BlockSpec((CHUNK, 1024), lambda n: (n, 0)), pl.BlockSpec((CHUNK, 1024), lambda n: (n, 1)),
                  pos, pos, tab, tab, tab],
        out_specs=[pl.BlockSpec((CHUNK, 1024), lambda n: (n, 0)),
                   pl.BlockSpec((RET_HEADS, None, 128, 256), lambda n: (0, n, 0, 0))],
        out_shape=[jax.ShapeDtypeStruct((t, 1024), F32), jax.ShapeDtypeStruct((RET_HEADS, nch, 128, 256), F32)],
        scratch_shapes=[pltpu.VMEM((RET_HEADS, 128, 256), F32)],
        compiler_params=_params(dimension_semantics=("arbitrary",)))(
            proj, proj, cos2, sin2, dmat, qdec, kdec)


def _ret_bwd(proj, states, do, tables):
    t = proj.shape[0]
    nch = t // CHUNK
    dmat, qdec, kdec, cos2, sin2 = tables

    def body(qk_ref, v_ref, do_ref, st_ref, c_ref, s_ref, dm_ref, qd_ref, kd_ref, dqk_ref, dv_ref, rst):
        @pl.when(pl.program_id(0) == 0)
        def _():
            rst[...] = jnp.zeros_like(rst)

        c, s = c_ref[...], s_ref[...]
        for h in range(RET_HEADS):
            q = _rot(qk_ref[:, 128 * h:128 * (h + 1)], c, s)
            k = _rot(qk_ref[:, 512 + 128 * h:512 + 128 * (h + 1)], c, s) * RET_SCALE
            qb, kb = q.astype(BF16), k.astype(BF16)
            vb = v_ref[:, 256 * h:256 * (h + 1)].astype(BF16)
            dob = do_ref[:, 256 * h:256 * (h + 1)].astype(BF16)
            pb = st_ref[h].astype(BF16)
            r = rst[h]
            rb = r.astype(BF16)
            dm, qd, kd = dm_ref[h], qd_ref[h], kd_ref[h]
            sb = (_dot(qb, kb, "nt") * dm).astype(BF16)
            dsb = (_dot(dob, vb, "nt") * dm).astype(BF16)
            dq = _dot(dsb, kb) + _dot(dob, pb, "nt") * qd
            dk = _dot(dsb, qb, "tn") + _dot(vb, rb, "nt") * kd
            dv = _dot(sb, dob, "tn") + _dot((k * kd).astype(BF16), rb)
            rst[h] = _dot((q * qd).astype(BF16), dob, "tn") + qd[CHUNK - 1:CHUNK, 0:1] * r
            dqk_ref[:, 128 * h:128 * (h + 1)] = _unrot(dq, c, s).astype(BF16)
            dqk_ref[:, 512 + 128 * h:512 + 128 * (h + 1)] = (_unrot(dk, c, s) * RET_SCALE).astype(BF16)
            dv_ref[:, 256 * h:256 * (h + 1)] = dv.astype(BF16)

    rev = lambda n: nch - 1 - n
    tab = pl.BlockSpec((RET_HEADS, CHUNK, CHUNK), lambda n: (0, 0, 0))
    pos = pl.BlockSpec((CHUNK, CHUNK), lambda n: (rev(n), 0))
    row = pl.BlockSpec((CHUNK, 1024), lambda n: (rev(n), 0))
    osh = jax.ShapeDtypeStruct((t, 1024), BF16)
    return _pcall(
        body, name="ret_bwd", grid=(nch,),
        in_specs=[row, pl.BlockSpec((CHUNK, 1024), lambda n: (rev(n), 1)), row,
                  pl.BlockSpec((RET_HEADS, None, 128, 256), lambda n: (0, rev(n), 0, 0)),
                  pos, pos, tab, tab, tab],
        out_specs=[row, row], out_shape=[osh, osh],
        scratch_shapes=[pltpu.VMEM((RET_HEADS, 128, 256), F32)],
        compiler_params=_params(dimension_semantics=("arbitrary",)))(
            proj, proj, do, states, cos2, sin2, dmat, qdec, kdec)


def _gn_gate_fwd(o, proj, gn_g):
    t = o.shape[0]
    tr = _rows(t)

    def body(o_ref, g_ref, w_ref, c_ref):
        for h in range(RET_HEADS):
            sl = slice(256 * h, 256 * (h + 1))
            x = o_ref[:, sl]
            mu = jnp.mean(x, axis=-1, keepdims=True)
            xc = x - mu
            rstd = lax.rsqrt(jnp.mean(xc * xc, axis=-1, keepdims=True) + EPS)
            g = g_ref[:, sl]
            c_ref[:, sl] = (g * _sigmoid(g) * (xc * rstd * w_ref[:, sl])).astype(BF16)

    return _pcall(body, name="gn_gate_fwd", grid=(t // tr,),
                  in_specs=[pl.BlockSpec((tr, 1024), lambda i: (i, 0)),
                            pl.BlockSpec((tr, 1024), lambda i: (i, 2)),
                            pl.BlockSpec((1, 1024), lambda i: (0, 0))],
                  out_specs=pl.BlockSpec((tr, 1024), lambda i: (i, 0)),
                  out_shape=jax.ShapeDtypeStruct((t, 2048), BF16))(o, proj, gn_g)


def _gn_gate_bwd(dcat, o, proj, gn_g):
    t = o.shape[0]
    tr = _rows(t)

    def body(d_ref, o_ref, g_ref, w_ref, do_ref, dg_ref, dw_ref):
        @pl.when(pl.program_id(0) == 0)
        def _():
            dw_ref[...] = jnp.zeros_like(dw_ref)

        for h in range(RET_HEADS):
            sl = slice(256 * h, 256 * (h + 1))
            x = o_ref[:, sl]
            mu = jnp.mean(x, axis=-1, keepdims=True)
            xc = x - mu
            rstd = lax.rsqrt(jnp.mean(xc * xc, axis=-1, keepdims=True) + EPS)
            xh = xc * rstd
            w = w_ref[:, sl]
            g = g_ref[:, sl]
            sg = _sigmoid(g)
            d = d_ref[:, sl]
            don = d * (g * sg)
            dg_ref[:, sl] = (d * (xh * w) * (sg * (1.0 + g * (1.0 - sg)))).astype(BF16)
            dw_ref[:, sl] += jnp.sum(don * xh, axis=0, keepdims=True)
            dxh = don * w
            m1 = jnp.mean(dxh, axis=-1, keepdims=True)
            m2 = jnp.mean(dxh * xh, axis=-1, keepdims=True)
            do_ref[:, sl] = rstd * (dxh - m1 - xh * m2)

    row = pl.BlockSpec((tr, 1024), lambda i: (i, 0))
    vec = pl.BlockSpec((1, 1024), lambda i: (0, 0))
    return _pcall(body, name="gn_gate_bwd", grid=(t // tr,),
                  in_specs=[row, row, pl.BlockSpec((tr, 1024), lambda i: (i, 2)), vec],
                  out_specs=[row, row, vec],
                  out_shape=[jax.ShapeDtypeStruct((t, 1024), F32), jax.ShapeDtypeStruct((t, 1024), BF16),
                             jax.ShapeDtypeStruct((1, 1024), F32)])(dcat, o, proj, gn_g)


def _row_ids(i, tr):
    return i * tr + lax.broadcasted_iota(jnp.int32, (tr, 1), 0)


SH_ROWS = HALO - 8


def _shifted_copies(xs, sh, tr):
    for b in range(1, 8):
        sh[b - 1] = xs[pl.ds(b, tr + SH_ROWS), :]


def _shifted(xs, sh, off, tr):
    a, b = divmod(off, 8)
    return xs[pl.ds(8 * a, tr), :] if b == 0 else sh[b - 1, pl.ds(8 * a, tr), :]


def _conv_fwd(cat, proj, conv_w, conv_b, ln_g, ln_b):
    t = proj.shape[0]
    tr = _rows(t)
    hb = tr // HALO

    def body(cat_in, ua_ref, ug_ref, pa_ref, pg_ref, w_ref, b_ref, lg_ref, lb_ref, c_ref, hd_ref, y_ref, xs, sh):
        del cat_in
        i = pl.program_id(0)
        hdn = ua_ref[...] * _sigmoid(ug_ref[...])
        hd_ref[...] = hdn
        prev = pa_ref[...] * _sigmoid(pg_ref[...])
        xs[0:HALO, :] = jnp.where(i > 0, prev, 0.0)
        xs[HALO:HALO + tr, :] = hdn
        _shifted_copies(xs, sh, tr)
        acc = jnp.zeros((tr, 1024), F32) + b_ref[...]
        for w in range(CONV_WIDTH):
            acc += w_ref[w:w + 1, :] * _shifted(xs, sh, HALO - (CONV_WIDTH - 1) + w, tr)
        y_ref[...] = acc
        mu = jnp.mean(acc, axis=-1, keepdims=True)
        yc = acc - mu
        rstd = lax.rsqrt(jnp.mean(yc * yc, axis=-1, keepdims=True) + EPS)
        yn = yc * rstd * lg_ref[...] + lb_ref[...]
        c = yn * _sigmoid(yn)
        c_ref[...] = jnp.where(_row_ids(i, tr) >= PAD_FRONT, c, 0.0).astype(BF16)

    row = pl.BlockSpec((tr, 1024), lambda i: (i, 0))
    vec = pl.BlockSpec((1, 1024), lambda i: (0, 0))
    halo = lambda col: pl.BlockSpec((HALO, 1024), lambda i: (jnp.maximum(i * hb - 1, 0), col))
    return _pcall(body, name="conv_fwd", grid=(t // tr,),
                  in_specs=[pl.BlockSpec(memory_space=pl.ANY),
                            pl.BlockSpec((tr, 1024), lambda i: (i, 3)), pl.BlockSpec((tr, 1024), lambda i: (i, 4)),
                            halo(3), halo(4), pl.BlockSpec((32, 1024), lambda i: (0, 0)), vec, vec, vec],
                  out_specs=[pl.BlockSpec((tr, 1024), lambda i: (i, 1)), row, row],
                  out_shape=[jax.ShapeDtypeStruct((t, 2048), BF16), jax.ShapeDtypeStruct((t, 1024), F32),
                             jax.ShapeDtypeStruct((t, 1024), F32)],
                  scratch_shapes=[pltpu.VMEM((tr + HALO, 1024), F32), pltpu.VMEM((7, tr + SH_ROWS, 1024), F32)],
                  input_output_aliases={0: 0}, compiler_params=_params())(
                      cat, proj, proj, proj, proj, conv_w, conv_b, ln_g, ln_b)


def _conv_bwd_ln(dcat, y, ln_g, ln_b):
    t = y.shape[0]
    tr = _rows(t)

    def body(d_ref, y_ref, lg_ref, lb_ref, dy_ref, dlg_ref, dlb_ref, dcb_ref):
        i = pl.program_id(0)

        @pl.when(i == 0)
        def _():
            dlg_ref[...] = jnp.zeros_like(dlg_ref)
            dlb_ref[...] = jnp.zeros_like(dlb_ref)
            dcb_ref[...] = jnp.zeros_like(dcb_ref)

        y = y_ref[...]
        mu = jnp.mean(y, axis=-1, keepdims=True)
        yc = y - mu
        rstd = lax.rsqrt(jnp.mean(yc * yc, axis=-1, keepdims=True) + EPS)
        xh = yc * rstd
        lg = lg_ref[...]
        yn = xh * lg + lb_ref[...]
        sg = _sigmoid(yn)
        dyn = jnp.where(_row_ids(i, tr) >= PAD_FRONT, d_ref[...] * (sg * (1.0 + yn * (1.0 - sg))), 0.0)
        dlg_ref[...] += jnp.sum(dyn * xh, axis=0, keepdims=True)
        dlb_ref[...] += jnp.sum(dyn, axis=0, keepdims=True)
        dxh = dyn * lg
        m1 = jnp.mean(dxh, axis=-1, keepdims=True)
        m2 = jnp.mean(dxh * xh, axis=-1, keepdims=True)
        dy = rstd * (dxh - m1 - xh * m2)
        dy_ref[...] = dy
        dcb_ref[...] += jnp.sum(dy, axis=0, keepdims=True)

    row = pl.BlockSpec((tr, 1024), lambda i: (i, 0))
    vec = pl.BlockSpec((1, 1024), lambda i: (0, 0))
    vshape = jax.ShapeDtypeStruct((1, 1024), F32)
    return _pcall(body, name="conv_bwd_ln", grid=(t // tr,),
                  in_specs=[pl.BlockSpec((tr, 1024), lambda i: (i, 1)), row, vec, vec],
                  out_specs=[row, vec, vec, vec],
                  out_shape=[jax.ShapeDtypeStruct((t, 1024), F32), vshape, vshape, vshape])(dcat, y, ln_g, ln_b)


def _conv_bwd_taps(dy, hdn, proj, conv_w):
    t = dy.shape[0]
    tr = _rows(t)
    hb = tr // HALO
    nt = t // tr

    def body(dy_ref, nx_ref, hd_ref, ph_ref, ua_ref, ug_ref, w_ref, da_ref, dg_ref, dw_ref, xs, sh):
        i = pl.program_id(0)

        @pl.when(i == 0)
        def _():
            dw_ref[...] = jnp.zeros_like(dw_ref)

        dy = dy_ref[...]
        xs[0:tr, :] = dy
        xs[tr:tr + HALO, :] = jnp.where(i < nt - 1, nx_ref[...], 0.0)
        _shifted_copies(xs, sh, tr)
        dh = jnp.zeros((tr, 1024), F32)
        for w in range(CONV_WIDTH):
            dh += w_ref[w:w + 1, :] * _shifted(xs, sh, CONV_WIDTH - 1 - w, tr)
        xs[0:HALO, :] = jnp.where(i > 0, ph_ref[...], 0.0)
        xs[HALO:HALO + tr, :] = hd_ref[...]
        _shifted_copies(xs, sh, tr)
        for w in range(CONV_WIDTH):
            dw_ref[w:w + 1, :] += jnp.sum(dy * _shifted(xs, sh, HALO - (CONV_WIDTH - 1) + w, tr), axis=0, keepdims=True)
        dh = jnp.where(_row_ids(i, tr) >= PAD_FRONT, dh, 0.0)
        sg = _sigmoid(ug_ref[...])
        da_ref[...] = (dh * sg).astype(BF16)
        dg_ref[...] = (dh * ua_ref[...] * sg * (1.0 - sg)).astype(BF16)

    row = pl.BlockSpec((tr, 1024), lambda i: (i, 0))
    return _pcall(body, name="conv_bwd_taps", grid=(nt,),
                  in_specs=[row, pl.BlockSpec((HALO, 1024), lambda i: (jnp.minimum((i + 1) * hb, nt * hb - 1), 0)),
                            row, pl.BlockSpec((HALO, 1024), lambda i: (jnp.maximum(i * hb - 1, 0), 0)),
                            pl.BlockSpec((tr, 1024), lambda i: (i, 3)), pl.BlockSpec((tr, 1024), lambda i: (i, 4)),
                            pl.BlockSpec((32, 1024), lambda i: (0, 0))],
                  out_specs=[row, row, pl.BlockSpec((32, 1024), lambda i: (0, 0))],
                  out_shape=[jax.ShapeDtypeStruct((t, 1024), BF16), jax.ShapeDtypeStruct((t, 1024), BF16),
                             jax.ShapeDtypeStruct((32, 1024), F32)],
                  scratch_shapes=[pltpu.VMEM((tr + HALO, 1024), F32), pltpu.VMEM((7, tr + SH_ROWS, 1024), F32)],
                  compiler_params=_params())(dy, dy, hdn, hdn, proj, proj, conv_w)


NEG_BIG = -1e30


def _seg_tables(qb):
    j = np.arange(128)
    bd = (j[:, None] // 64 == j[None, :] // 64).astype(np.float32)
    ones = np.ones((128, 128), np.float32)
    later = np.concatenate([(j[:, None] >= j[None, :]).astype(np.float32), ones], axis=1)
    earlier = np.concatenate([(j[:, None] < j[None, :]).astype(np.float32), ones], axis=1)
    per = qb // CHUNK
    row = np.arange(qb)[:, None]
    pad = np.broadcast_to(j[None, :] < PAD_FRONT, (qb, 128))
    diag = [(g * CHUNK + j[None, :]) >= row for g in range(per)]
    masks = diag + [np.zeros((qb, 128), bool), pad, diag[0] | pad]
    bias = np.stack([np.where(m, NEG_BIG, 0.0) for m in masks]).astype(np.float32)
    dup = lambda m: np.concatenate([m, m], axis=0)
    return (jnp.asarray(bd, BF16), jnp.asarray(dup(later), BF16), jnp.asarray(dup(earlier), BF16),
            jnp.asarray(bias, F32))


def _split_dot(x, m):
    hi = x.astype(BF16)
    lo = (x - hi.astype(F32)).astype(BF16)
    return _dot(hi, m) + _dot(lo, m)


def _qk_norm_fwd(qkv, qg, kg, bd):
    t = qkv.shape[0]
    tr = _rows(t)
    nb = tr // CHUNK

    def body(q_ref, k_ref, v_ref, qg_ref, kg_ref, bd_ref, qo, kt, k2, vt, v2):
        bdm = bd_ref[...]
        lane = lax.broadcasted_iota(jnp.int32, (1, 128), 1)
        sub = lax.broadcasted_iota(jnp.int32, (128, 1), 0)

        def pair_layouts(x, t_ref, s_ref, hp, b):
            xt = x.T
            t_ref[hp, b] = jnp.concatenate([jnp.where(sub < 64, xt, 0.0), jnp.where(sub >= 64, xt, 0.0)],
                                           axis=1).astype(BF16)
            s_ref[hp, b] = jnp.concatenate([jnp.where(lane < 64, x, 0.0), jnp.where(lane >= 64, x, 0.0)],
                                           axis=0).astype(BF16)

        for hp in range(8):
            sl = slice(128 * hp, 128 * (hp + 1))
            x = q_ref[:, sl]
            r = lax.rsqrt(_split_dot(x * x, bdm) * (1.0 / 64) + EPS)
            qo[:, sl] = (x * r * (qg_ref[:, sl] * SB_SCALE)).astype(BF16)
            x = k_ref[:, sl]
            r = lax.rsqrt(_split_dot(x * x, bdm) * (1.0 / 64) + EPS)
            kn = x * r * kg_ref[:, sl]
            v = v_ref[:, sl]
            for b in range(nb):
                rows = slice(CHUNK * b, CHUNK * (b + 1))
                pair_layouts(kn[rows], kt, k2, hp, b)
                pair_layouts(v[rows], vt, v2, hp, b)

    col = lambda c: pl.BlockSpec((tr, 1024), lambda i: (i, c))
    vec = pl.BlockSpec((1, 1024), lambda i: (0, 0))
    wide = pl.BlockSpec((8, nb, 128, 256), lambda i: (0, i, 0, 0))
    tall = pl.BlockSpec((8, nb, 256, 128), lambda i: (0, i, 0, 0))
    wsh = jax.ShapeDtypeStruct((8, t // CHUNK, 128, 256), BF16)
    tsh = jax.ShapeDtypeStruct((8, t // CHUNK, 256, 128), BF16)
    return _pcall(body, name="qk_norm_fwd", grid=(t // tr,),
                  in_specs=[col(0), col(1), col(2), vec, vec, pl.BlockSpec((128, 128), lambda i: (0, 0))],
                  out_specs=[col(0), wide, tall, wide, tall],
                  out_shape=[jax.ShapeDtypeStruct((t, 1024), BF16), wsh, tsh, wsh, tsh])(qkv, qkv, qkv, qg, kg, bd)


def _qk_norm_bwd(qkv, dq, dk, dv, qg, kg, bd):
    t = qkv.shape[0]
    tr = _rows(t)

    def body(q_ref, k_ref, dq_ref, dk_ref, dv_ref, qg_ref, kg_ref, bd_ref, o_ref, dqg_ref, dkg_ref):
        @pl.when(pl.program_id(0) == 0)
        def _():
            dqg_ref[...] = jnp.zeros_like(dqg_ref)
            dkg_ref[...] = jnp.zeros_like(dkg_ref)

        bdm = bd_ref[...]
        for part, (src, d_ref, g_ref, dg_ref) in enumerate(((q_ref, dq_ref, qg_ref, dqg_ref),
                                                           (k_ref, dk_ref, kg_ref, dkg_ref))):
            for cix in range(8):
                sl = slice(128 * cix, 128 * (cix + 1))
                x = src[:, sl]
                d = d_ref[:, sl]
                r = lax.rsqrt(_split_dot(x * x, bdm) * (1.0 / 64) + EPS)
                u = d * g_ref[:, sl]
                m = _split_dot(u * x, bdm) * (1.0 / 64)
                o_ref[:, 1024 * part + 128 * cix:1024 * part + 128 * (cix + 1)] = (r * u - x * (r * r * r * m)).astype(BF16)
                dg_ref[:, sl] += jnp.sum(d * x * r, axis=0, keepdims=True)
        o_ref[:, 2048:3072] = dv_ref[...].astype(BF16)

    col = lambda c: pl.BlockSpec((tr, 1024), lambda i: (i, c))
    vec = pl.BlockSpec((1, 1024), lambda i: (0, 0))
    vsh = jax.ShapeDtypeStruct((1, 1024), F32)
    return _pcall(body, name="qk_norm_bwd", grid=(t // tr,),
                  in_specs=[col(0), col(1), col(0), col(0), col(0), vec, vec, pl.BlockSpec((128, 128), lambda i: (0, 0))],
                  out_specs=[pl.BlockSpec((tr, 3072), lambda i: (i, 0)), vec, vec],
                  out_shape=[jax.ShapeDtypeStruct((t, 3072), BF16), vsh, vsh])(qkv, qkv, dq, dk, dv, qg, kg, bd)


def _split2(x):
    hi = x.astype(BF16)
    lo = (x - hi.astype(F32)).astype(BF16)
    return jnp.concatenate([hi, lo], axis=1)


def _sb_scores(z, later_tab):
    e = jnp.exp(-jnp.abs(z))
    ope = 1.0 + e
    sp = jnp.maximum(z, 0.0) + jnp.log(ope)
    return e, ope, _dot(_split2(sp), later_tab)


def _sb_bias_index(i, kb, per):
    g = kb - i * per
    return jnp.where(kb == 0, jnp.where(i == 0, per + 2, per + 1), jnp.where(g >= 0, g, per))


def _sb_qb(t):
    return _tile(t, (384, 128))


def _sb_fwd(qh, kt, v2, later_tab, bias_tab):
    t = qh.shape[0]
    qb = _sb_qb(t)
    per = qb // CHUNK
    nkb_all = t // CHUNK
    zero_slot = nkb_all

    def body(q_ref, kt_ref, v2_ref, tab_ref, bias_ref, o_ref, ws_ref, acc, carry, zbuf, zk, cub, wbuf, wsem):
        h, i = pl.program_id(0), pl.program_id(1)
        q = q_ref[...]
        acc[...] = jnp.zeros_like(acc)
        carry[...] = jnp.zeros_like(carry)
        nkb = (i + 1) * per
        save = lambda kb: pltpu.make_async_copy(wbuf.at[kb], ws_ref.at[h, i, kb], wsem.at[kb])

        def sums(z2, kb):
            bias = bias_ref[_sb_bias_index(i, kb, per)]
            for hh in range(2):
                sl = slice(128 * hh, 128 * (hh + 1))
                z = z2[:, sl] + bias
                zk[:, sl] = z
                cub[hh] = _sb_scores(z, tab_ref[...])[2]

        def weights(kb):
            for hh in range(2):
                sl = slice(128 * hh, 128 * (hh + 1))
                cu = cub[hh]
                cin = carry[hh]
                wbuf[kb, :, sl] = jnp.exp(zk[:, sl] - cu[:, :128] - cin).astype(BF16)
                carry[hh] = cin + cu[:, 128:]

        def output(slot, kb):
            acc[...] += _dot(wbuf[slot], v2_ref[kb])

        sums(_dot(q, kt_ref[nkb - 1]), nkb - 1)
        zbuf[...] = _dot(q, kt_ref[jnp.maximum(nkb - 2, 0)])
        wbuf[zero_slot] = jnp.zeros((qb, 256), BF16)

        def step(s, _):
            kb = nkb - 1 - s

            @pl.when(s > 0)
            def _():
                save(kb + 1).start()

            z2 = zbuf[...]
            zbuf[...] = _dot(q, kt_ref[jnp.maximum(kb - 2, 0)])
            output(jnp.where(s == 0, zero_slot, kb + 1), jnp.minimum(kb + 1, nkb - 1))
            weights(kb)
            sums(z2, kb - 1)
            return 0

        lax.fori_loop(0, nkb - 1, step, 0)

        @pl.when(nkb > 1)
        def _():
            save(1).start()

        output(jnp.where(nkb == 1, zero_slot, 1), jnp.minimum(1, nkb - 1))
        weights(0)
        output(0, 0)
        save(0).start()
        o_ref[...] = acc[...]

        def drain(kb, _):
            save(kb).wait()
            return 0

        lax.fori_loop(0, nkb, drain, 0)

    blk = pl.BlockSpec((qb, 128), lambda h, i: (i, h))
    wide = pl.BlockSpec((None, nkb_all, 128, 256), lambda h, i: (h, 0, 0, 0))
    tall = pl.BlockSpec((None, nkb_all, 256, 128), lambda h, i: (h, 0, 0, 0))
    return _pcall(body, name="sb_fwd", grid=(8, t // qb),
                  in_specs=[blk, wide, tall, pl.BlockSpec((256, 256), lambda h, i: (0, 0)),
                            pl.BlockSpec((per + 3, qb, 128), lambda h, i: (0, 0, 0))],
                  out_specs=[blk, pl.BlockSpec(memory_space=pl.ANY)],
                  out_shape=[jax.ShapeDtypeStruct((t, 1024), F32),
                             jax.ShapeDtypeStruct((8, t // qb, nkb_all, qb, 256), BF16)],
                  scratch_shapes=[pltpu.VMEM((qb, 128), F32), pltpu.VMEM((2, qb, 128), F32),
                                  pltpu.VMEM((qb, 256), F32), pltpu.VMEM((qb, 256), F32),
                                  pltpu.VMEM((2, qb, 256), F32), pltpu.VMEM((nkb_all + 1, qb, 256), BF16),
                                  pltpu.SemaphoreType.DMA((nkb_all,))],
                  compiler_params=_params(dimension_semantics=("parallel", "arbitrary")))(
                      qh, kt, v2, later_tab, bias_tab)


def _sb_bwd(qh, kt, k2, vt, wsave, do, earlier_tab, bias_tab):
    t = qh.shape[0]
    qb = _sb_qb(t)
    per = qb // CHUNK
    nkb_all = t // CHUNK

    zero_slot = nkb_all

    def body(q_ref, kt_ref, k2_ref, vt_ref, ws_ref, do_ref, etab_ref, bias_ref,
             dq_ref, dk_ref, dv_ref, acc, gcarry, zbuf, dwbuf, wbuf, wsem, dzbuf):
        h, i = pl.program_id(0), pl.program_id(1)

        @pl.when(i == 0)
        def _():
            dk_ref[...] = jnp.zeros_like(dk_ref)
            dv_ref[...] = jnp.zeros_like(dv_ref)

        nkb = (i + 1) * per
        fetch = lambda kb: pltpu.make_async_copy(ws_ref.at[h, i, kb], wbuf.at[kb], wsem.at[kb])

        def prefetch(kb, _):
            fetch(kb).start()
            return 0

        lax.fori_loop(0, nkb, prefetch, 0)
        q = q_ref[...]
        dob = do_ref[...].astype(BF16)
        lane = lax.broadcasted_iota(jnp.int32, (1, 128), 1)
        acc[...] = jnp.zeros_like(acc)
        gcarry[...] = jnp.zeros_like(gcarry)
        zbuf[...] = _dot(q, kt_ref[0])
        dwbuf[...] = _dot(dob, vt_ref[0])
        dzbuf[...] = jnp.zeros_like(dzbuf)
        wbuf[zero_slot] = jnp.zeros((qb, 256), BF16)

        def gradients(slot, kb):
            dz2 = dzbuf[...]
            acc[...] += _dot(dz2, k2_ref[kb])
            dk2 = _dot(dz2, q, "tn")
            dv2 = _dot(wbuf[slot], dob, "tn")
            dk_ref[kb] += jnp.where(lane < 64, dk2[:128], dk2[128:])
            dv_ref[kb] += jnp.where(lane < 64, dv2[:128], dv2[128:])

        def step(kb, _):
            fetch(kb).wait()
            bias = bias_ref[_sb_bias_index(i, kb, per)]
            z2 = zbuf[...]
            dw2 = dwbuf[...]
            nxt = jnp.minimum(kb + 1, nkb - 1)
            zbuf[...] = _dot(q, kt_ref[nxt])
            dwbuf[...] = _dot(dob, vt_ref[nxt])
            gradients(jnp.where(kb == 0, zero_slot, kb - 1), jnp.maximum(kb - 1, 0))
            w2 = wbuf[kb]
            for hh in range(2):
                sl = slice(128 * hh, 128 * (hh + 1))
                z = z2[:, sl] + bias
                e = jnp.exp(-jnp.abs(z))
                r = 1.0 / (1.0 + e)
                sig = jnp.where(z >= 0, r, e * r)
                gw = w2[:, sl].astype(F32) * dw2[:, sl]
                cu2 = _dot(_split2(gw), etab_ref[...])
                gin = gcarry[hh]
                gcarry[hh] = gin + cu2[:, 128:]
                dzbuf[:, sl] = (gw - sig * (gw + cu2[:, :128] + gin)).astype(BF16)
            return 0

        lax.fori_loop(0, nkb, step, 0)
        gradients(nkb - 1, nkb - 1)
        dq_ref[...] = acc[...] * SB_SCALE

    blk = pl.BlockSpec((qb, 128), lambda h, i: (i, h))
    wide = pl.BlockSpec((None, nkb_all, 128, 256), lambda h, i: (h, 0, 0, 0))
    tall = pl.BlockSpec((None, nkb_all, 256, 128), lambda h, i: (h, 0, 0, 0))
    tab = pl.BlockSpec((256, 256), lambda h, i: (0, 0))
    kv_out = pl.BlockSpec((nkb_all, 128, 128), lambda h, i: (0, 0, h))
    ksh = jax.ShapeDtypeStruct((nkb_all, 128, 1024), F32)
    dq, dk, dv = _pcall(
        body, name="sb_bwd", grid=(8, t // qb),
        in_specs=[blk, wide, tall, wide, pl.BlockSpec(memory_space=pl.ANY), blk, tab,
                  pl.BlockSpec((per + 3, qb, 128), lambda h, i: (0, 0, 0))],
        out_specs=[blk, kv_out, kv_out], out_shape=[jax.ShapeDtypeStruct((t, 1024), F32), ksh, ksh],
        scratch_shapes=[pltpu.VMEM((qb, 128), F32), pltpu.VMEM((2, qb, 128), F32),
                        pltpu.VMEM((qb, 256), F32), pltpu.VMEM((qb, 256), F32),
                        pltpu.VMEM((nkb_all + 1, qb, 256), BF16), pltpu.SemaphoreType.DMA((nkb_all,)),
                        pltpu.VMEM((qb, 256), BF16)],
        compiler_params=_params(dimension_semantics=("parallel", "arbitrary")))(
            qh, kt, k2, vt, wsave, do, earlier_tab, bias_tab)
    return dq, dk.reshape(t, 1024), dv.reshape(t, 1024)


def _adamw_math(w, g, m, v):
    m = ADAM_B1 * m + (1.0 - ADAM_B1) * g
    v = ADAM_B2 * v + (1.0 - ADAM_B2) * (g * g)
    m_hat = m / (1.0 - ADAM_B1 ** ADAM_STEP)
    v_hat = v / (1.0 - ADAM_B2 ** ADAM_STEP)
    delta = -ADAM_LR * (m_hat / (jnp.sqrt(v_hat) + ADAM_EPS) + ADAM_WD * w)
    return delta, m, v


def _adamw(name, w, owns, recvs, m, v, me):
    shape = w.shape
    c = shape[-1]
    nl = len(owns)
    w3, m3, v3 = (a.reshape(nl, -1, c) for a in (w, m, v))
    r = w3.shape[1]
    tr = _tile(r, (256, 128))
    owns = [o.reshape(N_DEV, r, c) for o in owns]
    recvs = [p.reshape(N_DEV - 1, r, c) for p in recvs]

    def body(me_ref, w_ref, *rest):
        own_refs, recv_refs = rest[:nl], rest[nl:2 * nl]
        m_ref, v_ref = rest[2 * nl:2 * nl + 2]
        g_out, d_out, m_out, v_out = rest[2 * nl + 2:]
        layer = pl.program_id(0)

        def grad(k):
            g = own_refs[k][...].astype(F32)
            for s in range(N_DEV - 1):
                g = g + recv_refs[k][s].astype(F32)
            return g

        g = grad(0)
        for k in range(1, nl):
            g = jnp.where(layer == k, grad(k), g)
        d, mn, vn = _adamw_math(w_ref[...], g, m_ref[...], v_ref[...])
        g_out[...] = g
        d_out[...] = d
        m_out[...] = mn
        v_out[...] = vn

    row = pl.BlockSpec((None, tr, c), lambda l, i, me_ref: (l, i, 0))
    own = lambda k: pl.BlockSpec((None, tr, c), lambda l, i, me_ref: (me_ref[0], jnp.where(l == k, i, 0), 0))
    rcv = lambda k: pl.BlockSpec((N_DEV - 1, tr, c), lambda l, i, me_ref: (0, jnp.where(l == k, i, 0), 0))
    osh = jax.ShapeDtypeStruct((nl, r, c), F32)
    grid_spec = pltpu.PrefetchScalarGridSpec(
        num_scalar_prefetch=1, grid=(nl, r // tr),
        in_specs=[row] + [own(k) for k in range(nl)] + [rcv(k) for k in range(nl)] + [row, row],
        out_specs=[row, row, row, row])
    outs = _pcall(body, name=name, grid_spec=grid_spec, out_shape=[osh, osh, osh, osh])(
        me.reshape(1), w3, *owns, *recvs, m3, v3)
    return tuple(o.reshape(shape) for o in outs)


def _place():
    x, y, c = lax.axis_index("x"), lax.axis_index("y"), lax.axis_index("c")
    return x, y, c, 4 * x + 2 * y + c


def _peer(x, y, c, rel):
    return (x ^ ((rel >> 2) & 1), y ^ ((rel >> 1) & 1), c ^ (rel & 1))


def _gather_first(now, later):
    n, k = len(now), len(later)

    def body(*refs):
        ins, outs = refs[:n + k], refs[n + k:2 * (n + k)]
        send, recv, lsem = refs[2 * (n + k):]
        x, y, c, me = _place()
        locals_ = []
        for w in range(n + k):
            local = pltpu.make_async_copy(ins[w], outs[w].at[me], lsem.at[w])
            local.start()
            locals_.append(local)
        def copy(w, src, slot, rel, to_rel):
            return pltpu.make_async_remote_copy(src_ref=src, dst_ref=outs[w].at[slot], send_sem=send.at[w, rel - 1],
                                                recv_sem=recv.at[w, rel - 1], device_id=_peer(x, y, c, to_rel),
                                                device_id_type=MESH)

        for w in range(n):
            for rel in (1, 2, 4, 6):
                copy(w, ins[w], me, rel, rel).start()
        for w in range(n):
            for rel in (2, 4, 6):
                copy(w, ins[w], me ^ rel, rel, rel).wait_recv()
                copy(w, outs[w].at[me ^ rel], me ^ rel, rel | 1, 1).start()
        for w in range(n):
            for rel in (1, 3, 5, 7):
                copy(w, ins[w], me ^ rel, rel, 1).wait_recv()
            for rel in range(1, N_DEV):
                copy(w, ins[w], me, rel, rel).wait_send()
        for local in locals_:
            local.wait()

    hbm = pl.BlockSpec(memory_space=pl.ANY)
    arrays = list(now) + list(later)
    return _pcall(body, name="gather_first", in_specs=[hbm] * (n + k), out_specs=[hbm] * (n + k),
                  out_shape=[jax.ShapeDtypeStruct((N_DEV,) + a.shape, a.dtype) for a in arrays],
                  scratch_shapes=[pltpu.SemaphoreType.DMA((n, N_DEV - 1)), pltpu.SemaphoreType.DMA((n, N_DEV - 1)),
                                  pltpu.SemaphoreType.DMA((n + k,))],
                  compiler_params=_params(has_side_effects=True))(*arrays)


_HBM = pl.BlockSpec(memory_space=pltpu.HBM)
_SEM = pl.BlockSpec(memory_space=pltpu.SEMAPHORE)
_DATAFLOW = pltpu.SideEffectType.DATAFLOW_SIDE_EFFECTING


def _exchange_refs(srcs, lands, mode, me, rel, j):
    if mode == "gather":
        return srcs[j], lands[j].at[me], lands[j].at[me ^ rel]
    return srcs[j].at[me ^ rel], lands[j].at[rel - 1], lands[j].at[rel - 1]


def _exchange_start(name, srcs, lands, mode):
    n = len(srcs)

    def body(*refs):
        ins, lnd = refs[:n], refs[n:2 * n]
        send, recv = refs[2 * n], refs[2 * n + 1]
        token = refs[-1]
        x, y, c, me = _place()
        for j in range(n):
            for rel in range(1, N_DEV):
                src, dst, _ = _exchange_refs(ins, lnd, mode, me, rel, j)
                pltpu.make_async_remote_copy(src_ref=src, dst_ref=dst, send_sem=send.at[j * (N_DEV - 1) + rel - 1],
                                             recv_sem=recv.at[j * (N_DEV - 1) + rel - 1],
                                             device_id=_peer(x, y, c, rel), device_id_type=MESH).start()
        token[...] = jnp.zeros_like(token)

    sems = pltpu.SemaphoreType.DMA((n * (N_DEV - 1),))
    hbm_like = lambda a: pltpu.HBM(a.shape, a.dtype)
    outs = _pcall(body, name=name + "_start",
                  in_specs=[_HBM] * (2 * n), out_specs=[_SEM, _SEM] + [_HBM] * (2 * n) + [pl.BlockSpec(memory_space=pltpu.VMEM)],
                  out_shape=[sems, sems] + [hbm_like(a) for a in srcs] + [hbm_like(a) for a in lands]
                  + [jax.ShapeDtypeStruct((8, 128), F32)],
                  input_output_aliases={i: 2 + i for i in range(2 * n)},
                  compiler_params=pltpu.CompilerParams(has_side_effects=_DATAFLOW))(
                      *[pltpu.with_memory_space_constraint(a, pltpu.HBM) for a in list(srcs) + list(lands)])
    return dict(name=name, mode=mode, n=n, send=outs[0], recv=outs[1], srcs=outs[2:2 + n], lands=outs[2 + n:2 + 2 * n],
                token=outs[-1][0, 0])


def _exchange_wait(ex, after):
    n, mode = ex["n"], ex["mode"]

    def body(*refs):
        ins, lnd = refs[:n], refs[n:2 * n]
        send, recv = refs[2 * n], refs[2 * n + 1]
        x, y, c, me = _place()
        for j in range(n):
            for rel in range(1, N_DEV):
                src, dst, landed = _exchange_refs(ins, lnd, mode, me, rel, j)
                pltpu.make_async_remote_copy(src_ref=src, dst_ref=dst, send_sem=send.at[j * (N_DEV - 1) + rel - 1],
                                             recv_sem=recv.at[j * (N_DEV - 1) + rel - 1],
                                             device_id=_peer(x, y, c, rel), device_id_type=MESH).wait_send()
                pltpu.make_async_remote_copy(src_ref=src, dst_ref=landed, send_sem=send.at[j * (N_DEV - 1) + rel - 1],
                                             recv_sem=recv.at[j * (N_DEV - 1) + rel - 1],
                                             device_id=_peer(x, y, c, rel), device_id_type=MESH).wait_recv()

    hbm_like = lambda a: pltpu.HBM(a.shape, a.dtype)
    arrays = list(ex["srcs"]) + list(ex["lands"])
    outs = _pcall(body, name=ex["name"] + "_wait",
                  in_specs=[_HBM] * (2 * n) + [_SEM, _SEM, pl.BlockSpec(memory_space=pl.ANY)],
                  out_specs=[_HBM] * (2 * n), out_shape=[hbm_like(a) for a in arrays],
                  input_output_aliases={i: i for i in range(2 * n)},
                  compiler_params=pltpu.CompilerParams(has_side_effects=_DATAFLOW))(
                      *arrays, ex["send"], ex["recv"], after)
    return outs[:n], outs[n:]


def _scatter_start(name, grads):
    lands = [lax.empty((N_DEV - 1,) + g.shape[1:], g.dtype) for g in grads]
    return _exchange_start(name, grads, lands, "scatter")


ROW_MIX, ROW_MLP, ROW_CB, ROW_LG, ROW_LB, ROW_QN, ROW_KN, ROW_LOSS = 0, 2, 4, 5, 6, 7, 8, 9
ROW_META, ROW_CW, ROW_GN, SMALL_ROWS = 16, 32, 64, 72


def _allreduce_small(part):
    def body(p_ref, o_ref, slots, send, recv):
        x, y, c, me = _place()
        slots[me] = p_ref[...]
        for rel in range(1, N_DEV):
            pltpu.make_async_remote_copy(src_ref=p_ref, dst_ref=slots.at[me], send_sem=send.at[rel - 1],
                                         recv_sem=recv.at[rel - 1], device_id=_peer(x, y, c, rel),
                                         device_id_type=MESH).start()
        for rel in range(1, N_DEV):
            cp = pltpu.make_async_remote_copy(src_ref=p_ref, dst_ref=slots.at[me ^ rel], send_sem=send.at[rel - 1],
                                              recv_sem=recv.at[rel - 1], device_id=_peer(x, y, c, rel),
                                              device_id_type=MESH)
            cp.wait_send()
            cp.wait_recv()
        tot = slots[0]
        for s in range(1, N_DEV):
            tot = tot + slots[s]
        o_ref[...] = tot
        for row in (ROW_QN, ROW_KN):
            v = tot[row:row + 1, :]
            f = v[:, 0:128]
            for k in range(1, 8):
                f = f + v[:, 128 * k:128 * (k + 1)]
            o_ref[row:row + 1, 0:64] = f[:, 0:64] + f[:, 64:128]

    vm = pl.BlockSpec(memory_space=pltpu.VMEM)
    return _pcall(body, name="allreduce_small", in_specs=[vm], out_specs=vm,
                  out_shape=jax.ShapeDtypeStruct(part.shape, F32),
                  scratch_shapes=[pltpu.VMEM((N_DEV,) + part.shape, F32), pltpu.SemaphoreType.DMA((N_DEV - 1,)),
                                  pltpu.SemaphoreType.DMA((N_DEV - 1,))],
                  compiler_params=_params(has_side_effects=True))(part)


def _adamw_small(w, g, m, v):
    def body(w_ref, g_ref, m_ref, v_ref, d_out, m_out, v_out):
        d, mn, vn = _adamw_math(w_ref[...], g_ref[...], m_ref[...], v_ref[...])
        d_out[...] = d
        m_out[...] = mn
        v_out[...] = vn

    osh = jax.ShapeDtypeStruct(w.shape, F32)
    return _pcall(body, name="adamw_small", out_shape=[osh, osh, osh])(w, g, m, v)


def _local_step(h0, target, p, weight, emit):
    t = h0.shape[0]
    tables = _ret_tables(t)
    bd, later_tab, earlier_tab, bias_tab = _seg_tables(_sb_qb(t))
    row = lambda a, i: a[i:i + 1]

    hn_a = _rms_fwd("rms_mix0", h0, row(p["norm_mix_g"], 0))
    w_in = weight("w_in", hn_a)
    proj = _mm_cols("proj_in", hn_a, w_in, ())
    o_ret, states = _ret_fwd(proj, tables)
    gn_flat = p["gn_g"].reshape(1, 1024)
    cat = _gn_gate_fwd(o_ret, proj, gn_flat)
    cat, hdn, ycv = _conv_fwd(cat, proj, p["conv_w"], p["conv_b"], p["ln_g"], p["ln_b"])
    w_out = weight("w_out", cat)
    h1 = _mm_rows("mix_out", cat, w_out, h0)
    hn_b = _rms_fwd("rms_mlp0", h1, row(p["norm_mlp_g"], 0))
    w1_0, w2_0 = weight("w1_0", hn_b), weight("w2_0", hn_b)
    a0, s0 = _mm_cols("mlp0_up", hn_b, w1_0, (), epi="relu2")
    h2 = _mm_rows("mlp0_down", s0, w2_0, h1)

    hn_c = _rms_fwd("rms_mix1", h2, row(p["norm_mix_g"], 1))
    w_qkv = weight("w_qkv", hn_c)
    qkv = _mm_cols("qkv", hn_c, w_qkv, ())
    qg = jnp.tile(p["qn_g"], (1, 16))
    kg = jnp.tile(p["kn_g"], (1, 16))
    qh, kt, k2, vt, v2 = _qk_norm_fwd(qkv, qg, kg, bd)
    o_sb, w_sb = _sb_fwd(qh, kt, v2, later_tab, bias_tab)
    w_o = weight("w_o", o_sb)
    h3 = _mm_rows("attn_out", o_sb, w_o, h2)
    hn_d = _rms_fwd("rms_mlp1", h3, row(p["norm_mlp_g"], 1))
    w1_1, w2_1 = weight("w1_1", hn_d), weight("w2_1", hn_d)
    a1, s1 = _mm_cols("mlp1_up", hn_d, w1_1, (), epi="relu2")
    h4 = _mm_rows("mlp1_down", s1, w2_1, h3)

    dh, loss = _loss_bwd(h4, target)

    def mlp_bwd(tag, layer, w1, w2, dh, h_in, hn, a, s):
        da = _mm_rows_t(f"{tag}_dact", dh, w2, (), out_dtype=BF16, epi="drelu2", extra=a)
        dw2 = _wgrad_rows(f"{tag}_dw2", s, dh, 512)
        dw1 = _wgrad_cols(f"{tag}_dw1", hn, da, 512)
        tok = emit(tag, [dw1, dw2])
        dhn = _mm_cols_t(f"{tag}_dhn", da, w1)
        return _rms_bwd(f"{tag}_rms_bwd", dhn, h_in, row(p["norm_mlp_g"], layer) + tok, dh)

    dh, dg_mlp1 = mlp_bwd("mlp1", 1, w1_1, w2_1, dh, h3, hn_d, a1, s1)

    do_sb = _mm_rows_t("attn_dout", dh, w_o, ())
    dw_o = _wgrad_rows("attn_dwo", o_sb, dh, 128)
    dq, dk, dv = _sb_bwd(qh, kt, k2, vt, w_sb, do_sb, earlier_tab, bias_tab)
    dqkv, dqg, dkg = _qk_norm_bwd(qkv, dq, dk, dv, qg, kg, bd)
    dw_qkv = _wgrad_cols("qkv_dw", hn_c, dqkv, 384)
    tok = emit("attn", [dw_qkv, dw_o])
    dhn = _mm_cols_t("qkv_dhn", dqkv, w_qkv)
    dh, dg_mix1 = _rms_bwd("mix1_rms_bwd", dhn, h2, row(p["norm_mix_g"], 1) + tok, dh)

    dh, dg_mlp0 = mlp_bwd("mlp0", 0, w1_0, w2_0, dh, h1, hn_b, a0, s0)

    dcat = _mm_rows_t("mix_dcat", dh, w_out, ())
    dw_out = _wgrad_rows("mix_dwout", cat, dh, 256)
    do_ret, dgate, dgn = _gn_gate_bwd(dcat, o_ret, proj, gn_flat)
    dqk_r, dv_r = _ret_bwd(proj, states, do_ret, tables)
    dy, dlg, dlb, dcb = _conv_bwd_ln(dcat, ycv, p["ln_g"], p["ln_b"])
    dua, dug, dcw = _conv_bwd_taps(dy, hdn, proj, p["conv_w"])
    dproj = jnp.concatenate([dqk_r, dv_r, dgate, dua, dug], axis=1)
    dw_in = _wgrad_cols("proj_dw", hn_a, dproj, 640)
    tok = emit("mix0", [dw_in, dw_out])
    dhn = _mm_cols_t("proj_dhn", dproj, w_in)
    dh, dg_mix0 = _rms_bwd("mix0_rms_bwd", dhn, h0, row(p["norm_mix_g"], 0) + tok, dh)

    rid = lax.broadcasted_iota(jnp.int32, (16, 1), 0)
    loss_row = jnp.broadcast_to(loss[0:1, 0:1], (1, D_MODEL))
    vecs = sum(jnp.where(rid == k, v, 0.0)
               for k, v in enumerate((dg_mix0, dg_mix1, dg_mlp0, dg_mlp1, dcb, dlg, dlb, dqg, dkg, loss_row)))
    small = jnp.concatenate([vecs, dh[PAD_FRONT:TOK0], dcw, jnp.where(rid[:8] == 0, dgn, 0.0)], axis=0)
    return dh[TOK0:], small


_SMALL_NAMES = ("meta", "norm_mix_g", "norm_mlp_g", "even_ret_gn_g", "even_conv_w", "even_conv_b",
                "even_conv_ln_g", "even_conv_ln_b", "odd_q_norm_g", "odd_k_norm_g")
_BIG_NAMES = ("even_w_in", "even_w_out", "odd_w_qkv", "odd_w_o", "mlp_w1", "mlp_w2")
_ORDER = ("meta", "norm_mix_g", "norm_mlp_g", "even_w_in", "even_ret_gn_g", "even_conv_w", "even_conv_b",
          "even_conv_ln_g", "even_conv_ln_b", "even_w_out", "odd_w_qkv", "odd_q_norm_g", "odd_k_norm_g",
          "odd_w_o", "mlp_w1", "mlp_w2")


def _pack128(a):
    flat = a.reshape(-1)
    n = flat.shape[0]
    rows = -(-n // 128)
    rows8 = -(-rows // 8) * 8
    return jnp.pad(flat, (0, rows8 * 128 - n)).reshape(rows8, 128)


def kernel(x, meta, norm_mix_g, norm_mlp_g, even_w_in, even_ret_gn_g, even_conv_w, even_conv_b, even_conv_ln_g, even_conv_ln_b, even_w_out, odd_w_qkv, odd_q_norm_g, odd_k_norm_g, odd_w_o, mlp_w1, mlp_w2, loss_target, m_meta, m_norm_mix_g, m_norm_mlp_g, m_even_w_in, m_even_ret_gn_g, m_even_conv_w, m_even_conv_b, m_even_conv_ln_g, m_even_conv_ln_b, m_even_w_out, m_odd_w_qkv, m_odd_q_norm_g, m_odd_k_norm_g, m_odd_w_o, m_mlp_w1, m_mlp_w2, v_meta, v_norm_mix_g, v_norm_mlp_g, v_even_w_in, v_even_ret_gn_g, v_even_conv_w, v_even_conv_b, v_even_conv_ln_g, v_even_conv_ln_b, v_even_w_out, v_odd_w_qkv, v_odd_q_norm_g, v_odd_k_norm_g, v_odd_w_o, v_mlp_w1, v_mlp_w2):
    w = dict(meta=meta, norm_mix_g=norm_mix_g, norm_mlp_g=norm_mlp_g, even_w_in=even_w_in,
             even_ret_gn_g=even_ret_gn_g, even_conv_w=even_conv_w, even_conv_b=even_conv_b,
             even_conv_ln_g=even_conv_ln_g, even_conv_ln_b=even_conv_ln_b, even_w_out=even_w_out,
             odd_w_qkv=odd_w_qkv, odd_q_norm_g=odd_q_norm_g, odd_k_norm_g=odd_k_norm_g, odd_w_o=odd_w_o,
             mlp_w1=mlp_w1, mlp_w2=mlp_w2)
    mom = dict(meta=m_meta, norm_mix_g=m_norm_mix_g, norm_mlp_g=m_norm_mlp_g, even_w_in=m_even_w_in,
               even_ret_gn_g=m_even_ret_gn_g, even_conv_w=m_even_conv_w, even_conv_b=m_even_conv_b,
               even_conv_ln_g=m_even_conv_ln_g, even_conv_ln_b=m_even_conv_ln_b, even_w_out=m_even_w_out,
               odd_w_qkv=m_odd_w_qkv, odd_q_norm_g=m_odd_q_norm_g, odd_k_norm_g=m_odd_k_norm_g, odd_w_o=m_odd_w_o,
               mlp_w1=m_mlp_w1, mlp_w2=m_mlp_w2)
    var = dict(meta=v_meta, norm_mix_g=v_norm_mix_g, norm_mlp_g=v_norm_mlp_g, even_w_in=v_even_w_in,
               even_ret_gn_g=v_even_ret_gn_g, even_conv_w=v_even_conv_w, even_conv_b=v_even_conv_b,
               even_conv_ln_g=v_even_conv_ln_g, even_conv_ln_b=v_even_conv_ln_b, even_w_out=v_even_w_out,
               odd_w_qkv=v_odd_w_qkv, odd_q_norm_g=v_odd_q_norm_g, odd_k_norm_g=v_odd_k_norm_g, odd_w_o=v_odd_w_o,
               mlp_w1=v_mlp_w1, mlp_w2=v_mlp_w2)
    me = 4 * lax.axis_index("x") + 2 * lax.axis_index("y") + lax.axis_index("c")

    small_in = jnp.concatenate([meta, jnp.pad(even_conv_w[0], ((0, 1), (0, 0))),
                                jnp.pad(even_ret_gn_g[0], ((0, 4), (0, 96)))], axis=0)
    b16 = lambda a: a.astype(BF16)
    later_src = dict(w_out=b16(even_w_out[0]), w1_0=b16(mlp_w1[0]), w2_0=b16(mlp_w2[0]),
                     w_qkv=b16(odd_w_qkv[0]), w_o=b16(odd_w_o[0]), w1_1=b16(mlp_w1[1]), w2_1=b16(mlp_w2[1]))
    landed = _gather_first([b16(even_w_in[0]), small_in], list(later_src.values()))
    g_in, g_small = landed[0], landed[1]
    own_slot = dict(zip(later_src, landed[2:]))
    groups = (("gather_l0", ("w_out", "w1_0", "w2_0")), ("gather_attn", ("w_qkv", "w_o")),
              ("gather_l1", ("w1_1", "w2_1")))
    pending = {}
    gather_tok = jnp.zeros((), F32)
    for gname, names in groups:
        ex = _exchange_start(gname, [later_src[n] for n in names], [own_slot[n] for n in names], "gather")
        gather_tok = gather_tok + ex["token"]
        for n in names:
            pending[n] = (ex, names)
    arrived = dict(w_in=g_in)

    def weight(name, after):
        if name not in arrived:
            ex, names = pending[name]
            arrived.update(zip(names, _exchange_wait(ex, after)[1]))
        return arrived[name]

    cols = lambda a: jnp.transpose(a, (1, 0, 2)).reshape(a.shape[1], -1)
    p = dict(norm_mix_g=norm_mix_g + gather_tok, norm_mlp_g=norm_mlp_g, conv_b=even_conv_b, ln_g=even_conv_ln_g,
             ln_b=even_conv_ln_b, qn_g=odd_q_norm_g, kn_g=odd_k_norm_g,
             gn_g=cols(g_small[:, 48:52, :32]),
             conv_w=jnp.pad(cols(g_small[:, 16:47]), ((0, 1), (0, 0))))
    meta_full = cols(g_small[:, 0:16])

    scatters = {}

    def emit(tag, grads):
        scatters[tag] = _scatter_start("scatter_" + tag, grads)
        return scatters[tag]["token"]

    h0 = jnp.concatenate([jnp.zeros((PAD_FRONT, D_MODEL), F32), meta_full, x[0]], axis=0)
    grad_x, small_part = _local_step(h0, loss_target[0], p, weight, emit)

    out = {}
    got = {}

    def update(names, terms, after):
        for tag in {t for name in names for t, _ in terms[name]} - set(got):
            got[tag] = _exchange_wait(scatters[tag], after)
        for name in names:
            owns, recvs = zip(*[(got[t][0][j], got[t][1][j]) for t, j in terms[name]])
            out[name] = _adamw("adamw_" + name, w[name], list(owns), list(recvs), mom[name], var[name], me)

    terms = dict(even_w_in=[("mix0", 0)], even_w_out=[("mix0", 1)], odd_w_qkv=[("attn", 0)], odd_w_o=[("attn", 1)],
                 mlp_w1=[("mlp0", 0), ("mlp1", 0)], mlp_w2=[("mlp0", 1), ("mlp1", 1)])
    update(("mlp_w1", "mlp_w2", "odd_w_qkv", "odd_w_o"), terms, grad_x)
    tot = _allreduce_small(small_part)
    loss = tot[ROW_LOSS, 0]
    update(("even_w_in", "even_w_out"), terms, tot)

    shard_cols = lambda a, width: lax.dynamic_slice_in_dim(a, me * width, width, axis=1)
    one = lambda r: tot[r:r + 1]
    small_g = dict(
        norm_mix_g=tot[ROW_MIX:ROW_MIX + 2], norm_mlp_g=tot[ROW_MLP:ROW_MLP + 2],
        even_conv_b=one(ROW_CB), even_conv_ln_g=one(ROW_LG), even_conv_ln_b=one(ROW_LB),
        odd_q_norm_g=one(ROW_QN)[:, :64], odd_k_norm_g=one(ROW_KN)[:, :64],
        meta=shard_cols(tot[ROW_META:ROW_META + N_META], 128),
        even_conv_w=shard_cols(tot[ROW_CW:ROW_CW + CONV_WIDTH], 128)[None],
        even_ret_gn_g=shard_cols(tot[ROW_GN].reshape(4, 256), 32)[None])
    packs = {n: (_pack128(w[n]), _pack128(small_g[n]), _pack128(mom[n]), _pack128(var[n])) for n in _SMALL_NAMES}
    cat4 = [jnp.concatenate([packs[n][i] for n in _SMALL_NAMES], axis=0) for i in range(4)]
    d_s, m_s, v_s = _adamw_small(*cat4)
    r0 = 0
    for n in _SMALL_NAMES:
        rows = packs[n][0].shape[0]
        size = w[n].size
        take = lambda a: a[r0:r0 + rows].reshape(-1)[:size].reshape(w[n].shape)
        out[n] = (small_g[n].reshape(w[n].shape), take(d_s), take(m_s), take(v_s))
        r0 += rows

    res = [loss, grad_x[None]]
    for i in range(4):
        res.extend(out[n][i] for n in _ORDER)
    return tuple(res)
```

```python
import functools

import numpy as np
import jax
import jax.numpy as jnp
from jax import lax
from jax.experimental import pallas as pl
from jax.experimental.pallas import tpu as pltpu

F32 = jnp.float32
BF16 = jnp.bfloat16

D_MODEL = 1024
N_META = 16
CHUNK = 128
PAD_FRONT = 112
TOK0 = PAD_FRONT + N_META
EPS = 1e-6
N_DEV = 8
RET_HEADS = 4
RET_DECAY_OFFSET = 5.0
ROPE_BASE = 10000.0
CONV_WIDTH = 31
HALO = 32
SB_SCALE = 64 ** -0.5
RET_SCALE = 128 ** -0.5
ADAM_LR, ADAM_B1, ADAM_B2, ADAM_EPS, ADAM_WD, ADAM_STEP = 0.001, 0.9, 0.999, 1e-08, 0.01, 10
VMEM_LIMIT = 56 * 1024 * 1024
MESH = pl.DeviceIdType.MESH


def _pcall(body, **kw):
    return pl.pallas_call(body, **kw)


def _params(**kw):
    return pltpu.CompilerParams(vmem_limit_bytes=VMEM_LIMIT, **kw)


def _tile(n, cands):
    for c in cands:
        if n % c == 0:
            return c
    raise ValueError(f"no tile for {n} in {cands}")


def _sigmoid(x):
    return 1.0 / (1.0 + jnp.exp(-x))


_DIMS = {
    "nn": (((1,), (0,)), ((), ())),
    "nt": (((1,), (1,)), ((), ())),
    "tn": (((0,), (0,)), ((), ())),
}


def _matmul(name, a, b, *, grid, a_spec, b_spec, o_spec, out_shape, contract, acc_shape,
            epi="plain", extra=None, extra_spec=None):
    nk = grid[2]
    dims = _DIMS[contract]
    n_in = 3 if extra is not None else 2
    n_out = 2 if epi == "relu2" else 1

    def body(*refs):
        a_ref, b_ref = refs[0], refs[1]
        e_ref = refs[2] if extra is not None else None
        outs = refs[n_in:n_in + n_out]
        acc = refs[-1]
        k = pl.program_id(2)
        part = lax.dot_general(a_ref[...].astype(BF16), b_ref[...].astype(BF16), dims, preferred_element_type=F32)
        if nk > 1:
            @pl.when(k == 0)
            def _():
                acc[...] = jnp.zeros_like(acc)

            acc[...] += part

        @pl.when(k == nk - 1)
        def _():
            r = acc[...] if nk > 1 else part
            if epi == "plain":
                outs[0][...] = r.astype(outs[0].dtype)
            elif epi == "residual":
                outs[0][...] = (r + e_ref[...]).astype(outs[0].dtype)
            elif epi == "relu2":
                outs[0][...] = r
                rr = jnp.maximum(r, 0.0)
                outs[1][...] = (rr * rr).astype(BF16)
            elif epi == "drelu2":
                outs[0][...] = (r * (2.0 * jnp.maximum(e_ref[...], 0.0))).astype(outs[0].dtype)

    in_specs = [a_spec, b_spec] + ([extra_spec] if extra is not None else [])
    args = (a, b) + ((extra,) if extra is not None else ())
    if n_out == 2:
        out_specs = [o_spec, o_spec]
    else:
        out_specs = o_spec
    return _pcall(body, name=name, grid=grid, in_specs=in_specs, out_specs=out_specs,
                  out_shape=out_shape, scratch_shapes=[pltpu.VMEM(acc_shape, F32)],
                  compiler_params=_params(dimension_semantics=("parallel", "parallel", "arbitrary")))(*args)


def _tm(t):
    return _tile(t, (1408, 768, 384, 128))


def _mm_cols(name, a, wb, lead, out_dtype=F32, epi="plain"):
    t, kdim = a.shape
    n = wb.shape[-1]
    tm, tk = _tm(t), _tile(kdim, (1024, 512))
    nl = len(lead)
    b_spec = pl.BlockSpec((None,) * (1 + nl) + (tk, n), lambda i, j, k: (j,) + lead + (k, 0))
    o_spec = pl.BlockSpec((tm, n), lambda i, j, k: (i, j))
    if epi == "relu2":
        out_shape = [jax.ShapeDtypeStruct((t, N_DEV * n), F32), jax.ShapeDtypeStruct((t, N_DEV * n), BF16)]
    else:
        out_shape = jax.ShapeDtypeStruct((t, N_DEV * n), out_dtype)
    return _matmul(name, a, wb, grid=(t // tm, N_DEV, kdim // tk),
                   a_spec=pl.BlockSpec((tm, tk), lambda i, j, k: (i, k)), b_spec=b_spec, o_spec=o_spec,
                   out_shape=out_shape, contract="nn", acc_shape=(tm, n), epi=epi)


def _tm_deep(t, kdim):
    return _tm(t) if kdim <= 2048 else _tile(t, (704, 384, 128))


def _mm_cols_t(name, a, wb):
    t = a.shape[0]
    nb, kdim, n = wb.shape
    tm, tn = _tm_deep(t, nb * n), _tile(kdim, (512,))

    def body(a_ref, b_ref, o_ref):
        acc = _dot(a_ref[:, 0:n].astype(BF16), b_ref[0], "nt")
        for j in range(1, nb):
            acc = acc + _dot(a_ref[:, j * n:(j + 1) * n].astype(BF16), b_ref[j], "nt")
        o_ref[...] = acc

    return _pcall(body, name=name, grid=(t // tm, kdim // tn),
                  in_specs=[pl.BlockSpec((tm, nb * n), lambda i, j: (i, 0)),
                            pl.BlockSpec((nb, tn, n), lambda i, j: (0, j, 0))],
                  out_specs=pl.BlockSpec((tm, tn), lambda i, j: (i, j)),
                  out_shape=jax.ShapeDtypeStruct((t, kdim), F32),
                  compiler_params=_params(dimension_semantics=("parallel", "parallel")))(a, wb)


def _mm_rows(name, a, wb, residual):
    t = a.shape[0]
    nb, r, n = wb.shape
    tm, tn = _tm_deep(t, nb * r), _tile(n, (512,))

    def body(a_ref, b_ref, r_ref, o_ref):
        o_ref[...] = r_ref[...] + _dot(a_ref[...].astype(BF16), b_ref[...].reshape(nb * r, tn))

    o_spec = pl.BlockSpec((tm, tn), lambda i, j: (i, j))
    return _pcall(body, name=name, grid=(t // tm, n // tn),
                  in_specs=[pl.BlockSpec((tm, nb * r), lambda i, j: (i, 0)),
                            pl.BlockSpec((nb, r, tn), lambda i, j: (0, 0, j)), o_spec],
                  out_specs=o_spec, out_shape=jax.ShapeDtypeStruct((t, n), F32),
                  compiler_params=_params(dimension_semantics=("parallel", "parallel")))(a, wb, residual)


def _mm_rows_t(name, a, wb, lead, out_dtype=F32, epi="plain", extra=None):
    t, n = a.shape
    r = wb.shape[-2]
    tm, tk = _tm(t), _tile(n, (1024,))
    nl = len(lead)
    b_spec = pl.BlockSpec((None,) * (1 + nl) + (r, tk), lambda i, j, k: (j,) + lead + (0, k))
    o_spec = pl.BlockSpec((tm, r), lambda i, j, k: (i, j))
    return _matmul(name, a, wb, grid=(t // tm, N_DEV, n // tk),
                   a_spec=pl.BlockSpec((tm, tk), lambda i, j, k: (i, k)), b_spec=b_spec, o_spec=o_spec,
                   out_shape=jax.ShapeDtypeStruct((t, N_DEV * r), out_dtype), contract="nt",
                   acc_shape=(tm, r), epi=epi, extra=extra, extra_spec=o_spec if extra is not None else None)


def _wgrad_cols(name, x, dy, n):
    t, kdim = x.shape
    tk = _tm(t)
    return _matmul(name, x, dy, grid=(1, N_DEV, t // tk),
                   a_spec=pl.BlockSpec((tk, kdim), lambda i, j, k: (k, 0)),
                   b_spec=pl.BlockSpec((tk, n), lambda i, j, k: (k, j)),
                   o_spec=pl.BlockSpec((None, kdim, n), lambda i, j, k: (j, 0, 0)),
                   out_shape=jax.ShapeDtypeStruct((N_DEV, kdim, n), BF16), contract="tn", acc_shape=(kdim, n))


def _wgrad_rows(name, x, dy, r):
    t = x.shape[0]
    n = dy.shape[1]
    tk, tn = _tm(t), _tile(n, (512,))
    tm = min(N_DEV * r, 1024)
    out = _matmul(name, x, dy, grid=(N_DEV * r // tm, n // tn, t // tk),
                  a_spec=pl.BlockSpec((tk, tm), lambda i, j, k: (k, i)),
                  b_spec=pl.BlockSpec((tk, tn), lambda i, j, k: (k, j)),
                  o_spec=pl.BlockSpec((tm, tn), lambda i, j, k: (i, j)),
                  out_shape=jax.ShapeDtypeStruct((N_DEV * r, n), BF16), contract="tn", acc_shape=(tm, tn))
    return out.reshape(N_DEV, r, n)


def _rows(t):
    return _tile(t, (384, 128))


def _rms_fwd(name, h, g):
    t = h.shape[0]
    tr = _rows(t)

    def body(h_ref, g_ref, o_ref):
        x = h_ref[...]
        r = lax.rsqrt(jnp.mean(x * x, axis=-1, keepdims=True) + EPS)
        o_ref[...] = (x * r * g_ref[...]).astype(BF16)

    row = pl.BlockSpec((tr, D_MODEL), lambda i: (i, 0))
    vec = pl.BlockSpec((1, D_MODEL), lambda i: (0, 0))
    return _pcall(body, name=name, grid=(t // tr,), in_specs=[row, vec], out_specs=row,
                  out_shape=jax.ShapeDtypeStruct((t, D_MODEL), BF16))(h, g)


def _rms_bwd(name, dhn, h, g, dres):
    t = h.shape[0]
    tr = _rows(t)

    def body(d_ref, h_ref, g_ref, r_ref, o_ref, dg_ref):
        @pl.when(pl.program_id(0) == 0)
        def _():
            dg_ref[...] = jnp.zeros_like(dg_ref)

        x = h_ref[...]
        d = d_ref[...]
        r = lax.rsqrt(jnp.mean(x * x, axis=-1, keepdims=True) + EPS)
        u = d * g_ref[...]
        m = jnp.mean(u * x, axis=-1, keepdims=True)
        o_ref[...] = r_ref[...] + r * u - x * (r * r * r * m)
        dg_ref[...] += jnp.sum(d * x * r, axis=0, keepdims=True)

    row = pl.BlockSpec((tr, D_MODEL), lambda i: (i, 0))
    vec = pl.BlockSpec((1, D_MODEL), lambda i: (0, 0))
    return _pcall(body, name=name, grid=(t // tr,), in_specs=[row, row, vec, row], out_specs=[row, vec],
                  out_shape=[jax.ShapeDtypeStruct((t, D_MODEL), F32), jax.ShapeDtypeStruct((1, D_MODEL), F32)])(
                      dhn, h, g, dres)


def _loss_bwd(h, target):
    t = h.shape[0]
    nb = t // CHUNK

    def body(h_ref, t_ref, d_ref, l_ref):
        i = pl.program_id(0)

        @pl.when(i == 0)
        def _():
            d_ref[...] = jnp.zeros_like(d_ref)
            l_ref[...] = jnp.zeros_like(l_ref)

        @pl.when(i > 0)
        def _():
            diff = h_ref[...] - t_ref[...]
            d_ref[...] = diff * (1.0 / D_MODEL)
            l_ref[...] += jnp.sum(diff * diff) * (0.5 / D_MODEL)

    return _pcall(body, name="loss_bwd", grid=(nb,),
                  in_specs=[pl.BlockSpec((CHUNK, D_MODEL), lambda i: (i, 0)),
                            pl.BlockSpec((CHUNK, D_MODEL), lambda i: (jnp.maximum(i - 1, 0), 0))],
                  out_specs=[pl.BlockSpec((CHUNK, D_MODEL), lambda i: (i, 0)),
                             pl.BlockSpec((8, 128), lambda i: (0, 0))],
                  out_shape=[jax.ShapeDtypeStruct((t, D_MODEL), F32), jax.ShapeDtypeStruct((8, 128), F32)])(h, target)


def _ret_tables(t):
    hh = np.arange(RET_HEADS, dtype=np.float64)
    log_g = np.log1p(-np.exp2(-RET_DECAY_OFFSET - hh))
    idx = np.arange(CHUNK, dtype=np.float64)
    diff = idx[:, None] - idx[None, :]
    dmat = np.where(diff[None] >= 0, np.exp(np.maximum(diff, 0.0)[None] * log_g[:, None, None]), 0.0)
    qdec = np.exp((idx + 1.0)[None, :, None] * log_g[:, None, None]) * np.ones((1, 1, CHUNK))
    kdec = np.exp((CHUNK - 1 - idx)[None, :, None] * log_g[:, None, None]) * np.ones((1, 1, CHUNK))
    half = CHUNK // 2
    inv_freq = (ROPE_BASE ** (-np.arange(half, dtype=np.float32) / half)).astype(np.float32)
    ang = (np.arange(t, dtype=np.float32)[:, None] * inv_freq[None, :]).astype(np.float32).astype(np.float64)
    cos2 = np.concatenate([np.cos(ang), np.cos(ang)], axis=1)
    sin2 = np.concatenate([-np.sin(ang), np.sin(ang)], axis=1)
    return tuple(jnp.asarray(v, F32) for v in (dmat, qdec, kdec, cos2, sin2))


def _rot(x, c, s):
    return x * c + pltpu.roll(x, CHUNK // 2, 1) * s


def _unrot(dx, c, s):
    return dx * c + pltpu.roll(dx * s, CHUNK // 2, 1)


def _dot(a, b, contract="nn"):
    return lax.dot_general(a, b, _DIMS[contract], preferred_element_type=F32)


def _ret_fwd(proj, tables):
    t = proj.shape[0]
    nch = t // CHUNK
    dmat, qdec, kdec, cos2, sin2 = tables

    def body(qk_ref, v_ref, c_ref, s_ref, dm_ref, qd_ref, kd_ref, o_ref, st_ref, state):
        @pl.when(pl.program_id(0) == 0)
        def _():
            state[...] = jnp.zeros_like(state)

        c, s = c_ref[...], s_ref[...]
        for h in range(RET_HEADS):
            q = _rot(qk_ref[:, 128 * h:128 * (h + 1)], c, s)
            k = _rot(qk_ref[:, 512 + 128 * h:512 + 128 * (h + 1)], c, s) * RET_SCALE
            vb = v_ref[:, 256 * h:256 * (h + 1)].astype(BF16)
            st = state[h]
            st_ref[h] = st
            sc = _dot(q.astype(BF16), k.astype(BF16), "nt") * dm_ref[h]
            o = _dot(sc.astype(BF16), vb)
            o += _dot((q * qd_ref[h]).astype(BF16), st.astype(BF16))
            o_ref[:, 256 * h:256 * (h + 1)] = o
            kv = _dot((k * kd_ref[h]).astype(BF16), vb, "tn")
            state[h] = qd_ref[h, CHUNK - 1:CHUNK, 0:1] * st + kv

    tab = pl.BlockSpec((RET_HEADS, CHUNK, CHUNK), lambda n: (0, 0, 0))
    pos = pl.BlockSpec((CHUNK, CHUNK), lambda n: (n, 0))
    return _pcall(
        body, name="ret_fwd", grid=(nch,),
        in_specs=[pl.BlockSpec((CHUNK, 1024), lambda n: (n, 0)), pl.BlockSpec((CHUNK, 1024), lambda n: (n, 1)),
                  pos, pos, tab, tab, tab],
        out_specs=[pl.BlockSpec((CHUNK, 1024), lambda n: (n, 0)),
                   pl.BlockSpec((RET_HEADS, None, 128, 256), lambda n: (0, n, 0, 0))],
        out_shape=[jax.ShapeDtypeStruct((t, 1024), F32), jax.ShapeDtypeStruct((RET_HEADS, nch, 128, 256), F32)],
        scratch_shapes=[pltpu.VMEM((RET_HEADS, 128, 256), F32)],
        compiler_params=_params(dimension_semantics=("arbitrary",)))(
            proj, proj, cos2, sin2, dmat, qdec, kdec)


def _ret_bwd(proj, states, do, tables):
    t = proj.shape[0]
    nch = t // CHUNK
    dmat, qdec, kdec, cos2, sin2 = tables

    def body(qk_ref, v_ref, do_ref, st_ref, c_ref, s_ref, dm_ref, qd_ref, kd_ref, dqk_ref, dv_ref, rst):
        @pl.when(pl.program_id(0) == 0)
        def _():
            rst[...] = jnp.zeros_like(rst)

        c, s = c_ref[...], s_ref[...]
        for h in range(RET_HEADS):
            q = _rot(qk_ref[:, 128 * h:128 * (h + 1)], c, s)
            k = _rot(qk_ref[:, 512 + 128 * h:512 + 128 * (h + 1)], c, s) * RET_SCALE
            qb, kb = q.astype(BF16), k.astype(BF16)
            vb = v_ref[:, 256 * h:256 * (h + 1)].astype(BF16)
            dob = do_ref[:, 256 * h:256 * (h + 1)].astype(BF16)
            pb = st_ref[h].astype(BF16)
            r = rst[h]
            rb = r.astype(BF16)
            dm, qd, kd = dm_ref[h], qd_ref[h], kd_ref[h]
            sb = (_dot(qb, kb, "nt") * dm).astype(BF16)
            dsb = (_dot(dob, vb, "nt") * dm).astype(BF16)
            dq = _dot(dsb, kb) + _dot(dob, pb, "nt") * qd
            dk = _dot(dsb, qb, "tn") + _dot(vb, rb, "nt") * kd
            dv = _dot(sb, dob, "tn") + _dot((k * kd).astype(BF16), rb)
            rst[h] = _dot((q * qd).astype(BF16), dob, "tn") + qd[CHUNK - 1:CHUNK, 0:1] * r
            dqk_ref[:, 128 * h:128 * (h + 1)] = _unrot(dq, c, s).astype(BF16)
            dqk_ref[:, 512 + 128 * h:512 + 128 * (h + 1)] = (_unrot(dk, c, s) * RET_SCALE).astype(BF16)
            dv_ref[:, 256 * h:256 * (h + 1)] = dv.astype(BF16)

    rev = lambda n: nch - 1 - n
    tab = pl.BlockSpec((RET_HEADS, CHUNK, CHUNK), lambda n: (0, 0, 0))
    pos = pl.BlockSpec((CHUNK, CHUNK), lambda n: (rev(n), 0))
    row = pl.BlockSpec((CHUNK, 1024), lambda n: (rev(n), 0))
    osh = jax.ShapeDtypeStruct((t, 1024), BF16)
    return _pcall(
        body, name="ret_bwd", grid=(nch,),
        in_specs=[row, pl.BlockSpec((CHUNK, 1024), lambda n: (rev(n), 1)), row,
                  pl.BlockSpec((RET_HEADS, None, 128, 256), lambda n: (0, rev(n), 0, 0)),
                  pos, pos, tab, tab, tab],
        out_specs=[row, row], out_shape=[osh, osh],
        scratch_shapes=[pltpu.VMEM((RET_HEADS, 128, 256), F32)],
        compiler_params=_params(dimension_semantics=("arbitrary",)))(
            proj, proj, do, states, cos2, sin2, dmat, qdec, kdec)


def _gn_gate_fwd(o, proj, gn_g):
    t = o.shape[0]
    tr = _rows(t)

    def body(o_ref, g_ref, w_ref, c_ref):
        for h in range(RET_HEADS):
            sl = slice(256 * h, 256 * (h + 1))
            x = o_ref[:, sl]
            mu = jnp.mean(x, axis=-1, keepdims=True)
            xc = x - mu
            rstd = lax.rsqrt(jnp.mean(xc * xc, axis=-1, keepdims=True) + EPS)
            g = g_ref[:, sl]
            c_ref[:, sl] = (g * _sigmoid(g) * (xc * rstd * w_ref[:, sl])).astype(BF16)

    return _pcall(body, name="gn_gate_fwd", grid=(t // tr,),
                  in_specs=[pl.BlockSpec((tr, 1024), lambda i: (i, 0)),
                            pl.BlockSpec((tr, 1024), lambda i: (i, 2)),
                            pl.BlockSpec((1, 1024), lambda i: (0, 0))],
                  out_specs=pl.BlockSpec((tr, 1024), lambda i: (i, 0)),
                  out_shape=jax.ShapeDtypeStruct((t, 2048), BF16))(o, proj, gn_g)


def _gn_gate_bwd(dcat, o, proj, gn_g):
    t = o.shape[0]
    tr = _rows(t)

    def body(d_ref, o_ref, g_ref, w_ref, do_ref, dg_ref, dw_ref):
        @pl.when(pl.program_id(0) == 0)
        def _():
            dw_ref[...] = jnp.zeros_like(dw_ref)

        for h in range(RET_HEADS):
            sl = slice(256 * h, 256 * (h + 1))
            x = o_ref[:, sl]
            mu = jnp.mean(x, axis=-1, keepdims=True)
            xc = x - mu
            rstd = lax.rsqrt(jnp.mean(xc * xc, axis=-1, keepdims=True) + EPS)
            xh = xc * rstd
            w = w_ref[:, sl]
            g = g_ref[:, sl]
            sg = _sigmoid(g)
            d = d_ref[:, sl]
            don = d * (g * sg)
            dg_ref[:, sl] = (d * (xh * w) * (sg * (1.0 + g * (1.0 - sg)))).astype(BF16)
            dw_ref[:, sl] += jnp.sum(don * xh, axis=0, keepdims=True)
            dxh = don * w
            m1 = jnp.mean(dxh, axis=-1, keepdims=True)
            m2 = jnp.mean(dxh * xh, axis=-1, keepdims=True)
            do_ref[:, sl] = rstd * (dxh - m1 - xh * m2)

    row = pl.BlockSpec((tr, 1024), lambda i: (i, 0))
    vec = pl.BlockSpec((1, 1024), lambda i: (0, 0))
    return _pcall(body, name="gn_gate_bwd", grid=(t // tr,),
                  in_specs=[row, row, pl.BlockSpec((tr, 1024), lambda i: (i, 2)), vec],
                  out_specs=[row, row, vec],
                  out_shape=[jax.ShapeDtypeStruct((t, 1024), F32), jax.ShapeDtypeStruct((t, 1024), BF16),
                             jax.ShapeDtypeStruct((1, 1024), F32)])(dcat, o, proj, gn_g)


def _row_ids(i, tr):
    return i * tr + lax.broadcasted_iota(jnp.int32, (tr, 1), 0)


SH_ROWS = HALO - 8


def _shifted_copies(xs, sh, tr):
    for b in range(1, 8):
        sh[b - 1] = xs[pl.ds(b, tr + SH_ROWS), :]


def _shifted(xs, sh, off, tr):
    a, b = divmod(off, 8)
    return xs[pl.ds(8 * a, tr), :] if b == 0 else sh[b - 1, pl.ds(8 * a, tr), :]


def _conv_fwd(cat, proj, conv_w, conv_b, ln_g, ln_b):
    t = proj.shape[0]
    tr = _rows(t)
    hb = tr // HALO

    def body(cat_in, ua_ref, ug_ref, pa_ref, pg_ref, w_ref, b_ref, lg_ref, lb_ref, c_ref, hd_ref, y_ref, xs, sh):
        del cat_in
        i = pl.program_id(0)
        hdn = ua_ref[...] * _sigmoid(ug_ref[...])
        hd_ref[...] = hdn
        prev = pa_ref[...] * _sigmoid(pg_ref[...])
        xs[0:HALO, :] = jnp.where(i > 0, prev, 0.0)
        xs[HALO:HALO + tr, :] = hdn
        _shifted_copies(xs, sh, tr)
        acc = jnp.zeros((tr, 1024), F32) + b_ref[...]
        for w in range(CONV_WIDTH):
            acc += w_ref[w:w + 1, :] * _shifted(xs, sh, HALO - (CONV_WIDTH - 1) + w, tr)
        y_ref[...] = acc
        mu = jnp.mean(acc, axis=-1, keepdims=True)
        yc = acc - mu
        rstd = lax.rsqrt(jnp.mean(yc * yc, axis=-1, keepdims=True) + EPS)
        yn = yc * rstd * lg_ref[...] + lb_ref[...]
        c = yn * _sigmoid(yn)
        c_ref[...] = jnp.where(_row_ids(i, tr) >= PAD_FRONT, c, 0.0).astype(BF16)

    row = pl.BlockSpec((tr, 1024), lambda i: (i, 0))
    vec = pl.BlockSpec((1, 1024), lambda i: (0, 0))
    halo = lambda col: pl.BlockSpec((HALO, 1024), lambda i: (jnp.maximum(i * hb - 1, 0), col))
    return _pcall(body, name="conv_fwd", grid=(t // tr,),
                  in_specs=[pl.BlockSpec(memory_space=pl.ANY),
                            pl.BlockSpec((tr, 1024), lambda i: (i, 3)), pl.BlockSpec((tr, 1024), lambda i: (i, 4)),
                            halo(3), halo(4), pl.BlockSpec((32, 1024), lambda i: (0, 0)), vec, vec, vec],
                  out_specs=[pl.BlockSpec((tr, 1024), lambda i: (i, 1)), row, row],
                  out_shape=[jax.ShapeDtypeStruct((t, 2048), BF16), jax.ShapeDtypeStruct((t, 1024), F32),
                             jax.ShapeDtypeStruct((t, 1024), F32)],
                  scratch_shapes=[pltpu.VMEM((tr + HALO, 1024), F32), pltpu.VMEM((7, tr + SH_ROWS, 1024), F32)],
                  input_output_aliases={0: 0}, compiler_params=_params())(
                      cat, proj, proj, proj, proj, conv_w, conv_b, ln_g, ln_b)


def _conv_bwd_ln(dcat, y, ln_g, ln_b):
    t = y.shape[0]
    tr = _rows(t)

    def body(d_ref, y_ref, lg_ref, lb_ref, dy_ref, dlg_ref, dlb_ref, dcb_ref):
        i = pl.program_id(0)

        @pl.when(i == 0)
        def _():
            dlg_ref[...] = jnp.zeros_like(dlg_ref)
            dlb_ref[...] = jnp.zeros_like(dlb_ref)
            dcb_ref[...] = jnp.zeros_like(dcb_ref)

        y = y_ref[...]
        mu = jnp.mean(y, axis=-1, keepdims=True)
        yc = y - mu
        rstd = lax.rsqrt(jnp.mean(yc * yc, axis=-1, keepdims=True) + EPS)
        xh = yc * rstd
        lg = lg_ref[...]
        yn = xh * lg + lb_ref[...]
        sg = _sigmoid(yn)
        dyn = jnp.where(_row_ids(i, tr) >= PAD_FRONT, d_ref[...] * (sg * (1.0 + yn * (1.0 - sg))), 0.0)
        dlg_ref[...] += jnp.sum(dyn * xh, axis=0, keepdims=True)
        dlb_ref[...] += jnp.sum(dyn, axis=0, keepdims=True)
        dxh = dyn * lg
        m1 = jnp.mean(dxh, axis=-1, keepdims=True)
        m2 = jnp.mean(dxh * xh, axis=-1, keepdims=True)
        dy = rstd * (dxh - m1 - xh * m2)
        dy_ref[...] = dy
        dcb_ref[...] += jnp.sum(dy, axis=0, keepdims=True)

    row = pl.BlockSpec((tr, 1024), lambda i: (i, 0))
    vec = pl.BlockSpec((1, 1024), lambda i: (0, 0))
    vshape = jax.ShapeDtypeStruct((1, 1024), F32)
    return _pcall(body, name="conv_bwd_ln", grid=(t // tr,),
                  in_specs=[pl.BlockSpec((tr, 1024), lambda i: (i, 1)), row, vec, vec],
                  out_specs=[row, vec, vec, vec],
                  out_shape=[jax.ShapeDtypeStruct((t, 1024), F32), vshape, vshape, vshape])(dcat, y, ln_g, ln_b)


def _conv_bwd_taps(dy, hdn, proj, conv_w):
    t = dy.shape[0]
    tr = _rows(t)
    hb = tr // HALO
    nt = t // tr

    def body(dy_ref, nx_ref, hd_ref, ph_ref, ua_ref, ug_ref, w_ref, da_ref, dg_ref, dw_ref, xs, sh):
        i = pl.program_id(0)

        @pl.when(i == 0)
        def _():
            dw_ref[...] = jnp.zeros_like(dw_ref)

        dy = dy_ref[...]
        xs[0:tr, :] = dy
        xs[tr:tr + HALO, :] = jnp.where(i < nt - 1, nx_ref[...], 0.0)
        _shifted_copies(xs, sh, tr)
        dh = jnp.zeros((tr, 1024), F32)
        for w in range(CONV_WIDTH):
            dh += w_ref[w:w + 1, :] * _shifted(xs, sh, CONV_WIDTH - 1 - w, tr)
        xs[0:HALO, :] = jnp.where(i > 0, ph_ref[...], 0.0)
        xs[HALO:HALO + tr, :] = hd_ref[...]
        _shifted_copies(xs, sh, tr)
        for w in range(CONV_WIDTH):
            dw_ref[w:w + 1, :] += jnp.sum(dy * _shifted(xs, sh, HALO - (CONV_WIDTH - 1) + w, tr), axis=0, keepdims=True)
        dh = jnp.where(_row_ids(i, tr) >= PAD_FRONT, dh, 0.0)
        sg = _sigmoid(ug_ref[...])
        da_ref[...] = (dh * sg).astype(BF16)
        dg_ref[...] = (dh * ua_ref[...] * sg * (1.0 - sg)).astype(BF16)

    row = pl.BlockSpec((tr, 1024), lambda i: (i, 0))
    return _pcall(body, name="conv_bwd_taps", grid=(nt,),
                  in_specs=[row, pl.BlockSpec((HALO, 1024), lambda i: (jnp.minimum((i + 1) * hb, nt * hb - 1), 0)),
                            row, pl.BlockSpec((HALO, 1024), lambda i: (jnp.maximum(i * hb - 1, 0), 0)),
                            pl.BlockSpec((tr, 1024), lambda i: (i, 3)), pl.BlockSpec((tr, 1024), lambda i: (i, 4)),
                            pl.BlockSpec((32, 1024), lambda i: (0, 0))],
                  out_specs=[row, row, pl.BlockSpec((32, 1024), lambda i: (0, 0))],
                  out_shape=[jax.ShapeDtypeStruct((t, 1024), BF16), jax.ShapeDtypeStruct((t, 1024), BF16),
                             jax.ShapeDtypeStruct((32, 1024), F32)],
                  scratch_shapes=[pltpu.VMEM((tr + HALO, 1024), F32), pltpu.VMEM((7, tr + SH_ROWS, 1024), F32)],
                  compiler_params=_params())(dy, dy, hdn, hdn, proj, proj, conv_w)


NEG_BIG = -1e30


def _seg_tables(qb):
    j = np.arange(128)
    bd = (j[:, None] // 64 == j[None, :] // 64).astype(np.float32)
    ones = np.ones((128, 128), np.float32)
    later = np.concatenate([(j[:, None] >= j[None, :]).astype(np.float32), ones], axis=1)
    earlier = np.concatenate([(j[:, None] < j[None, :]).astype(np.float32), ones], axis=1)
    per = qb // CHUNK
    row = np.arange(qb)[:, None]
    pad = np.broadcast_to(j[None, :] < PAD_FRONT, (qb, 128))
    diag = [(g * CHUNK + j[None, :]) >= row for g in range(per)]
    masks = diag + [np.zeros((qb, 128), bool), pad, diag[0] | pad]
    bias = np.stack([np.where(m, NEG_BIG, 0.0) for m in masks]).astype(np.float32)
    dup = lambda m: np.concatenate([m, m], axis=0)
    return (jnp.asarray(bd, BF16), jnp.asarray(dup(later), BF16), jnp.asarray(dup(earlier), BF16),
            jnp.asarray(bias, F32))


def _split_dot(x, m):
    hi = x.astype(BF16)
    lo = (x - hi.astype(F32)).astype(BF16)
    return _dot(hi, m) + _dot(lo, m)


def _qk_norm_fwd(qkv, qg, kg, bd):
    t = qkv.shape[0]
    tr = _rows(t)
    nb = tr // CHUNK

    def body(q_ref, k_ref, v_ref, qg_ref, kg_ref, bd_ref, qo, kt, k2, vt, v2):
        bdm = bd_ref[...]
        lane = lax.broadcasted_iota(jnp.int32, (1, 128), 1)
        sub = lax.broadcasted_iota(jnp.int32, (128, 1), 0)

        def pair_layouts(x, t_ref, s_ref, hp, b):
            xt = x.T
            t_ref[hp, b] = jnp.concatenate([jnp.where(sub < 64, xt, 0.0), jnp.where(sub >= 64, xt, 0.0)],
                                           axis=1).astype(BF16)
            s_ref[hp, b] = jnp.concatenate([jnp.where(lane < 64, x, 0.0), jnp.where(lane >= 64, x, 0.0)],
                                           axis=0).astype(BF16)

        for hp in range(8):
            sl = slice(128 * hp, 128 * (hp + 1))
            x = q_ref[:, sl]
            r = lax.rsqrt(_split_dot(x * x, bdm) * (1.0 / 64) + EPS)
            qo[:, sl] = (x * r * (qg_ref[:, sl] * SB_SCALE)).astype(BF16)
            x = k_ref[:, sl]
            r = lax.rsqrt(_split_dot(x * x, bdm) * (1.0 / 64) + EPS)
            kn = x * r * kg_ref[:, sl]
            v = v_ref[:, sl]
            for b in range(nb):
                rows = slice(CHUNK * b, CHUNK * (b + 1))
                pair_layouts(kn[rows], kt, k2, hp, b)
                pair_layouts(v[rows], vt, v2, hp, b)

    col = lambda c: pl.BlockSpec((tr, 1024), lambda i: (i, c))
    vec = pl.BlockSpec((1, 1024), lambda i: (0, 0))
    wide = pl.BlockSpec((8, nb, 128, 256), lambda i: (0, i, 0, 0))
    tall = pl.BlockSpec((8, nb, 256, 128), lambda i: (0, i, 0, 0))
    wsh = jax.ShapeDtypeStruct((8, t // CHUNK, 128, 256), BF16)
    tsh = jax.ShapeDtypeStruct((8, t // CHUNK, 256, 128), BF16)
    return _pcall(body, name="qk_norm_fwd", grid=(t // tr,),
                  in_specs=[col(0), col(1), col(2), vec, vec, pl.BlockSpec((128, 128), lambda i: (0, 0))],
                  out_specs=[col(0), wide, tall, wide, tall],
                  out_shape=[jax.ShapeDtypeStruct((t, 1024), BF16), wsh, tsh, wsh, tsh])(qkv, qkv, qkv, qg, kg, bd)


def _qk_norm_bwd(qkv, dq, dk, dv, qg, kg, bd):
    t = qkv.shape[0]
    tr = _rows(t)

    def body(q_ref, k_ref, dq_ref, dk_ref, dv_ref, qg_ref, kg_ref, bd_ref, o_ref, dqg_ref, dkg_ref):
        @pl.when(pl.program_id(0) == 0)
        def _():
            dqg_ref[...] = jnp.zeros_like(dqg_ref)
            dkg_ref[...] = jnp.zeros_like(dkg_ref)

        bdm = bd_ref[...]
        for part, (src, d_ref, g_ref, dg_ref) in enumerate(((q_ref, dq_ref, qg_ref, dqg_ref),
                                                           (k_ref, dk_ref, kg_ref, dkg_ref))):
            for cix in range(8):
                sl = slice(128 * cix, 128 * (cix + 1))
                x = src[:, sl]
                d = d_ref[:, sl]
                r = lax.rsqrt(_split_dot(x * x, bdm) * (1.0 / 64) + EPS)
                u = d * g_ref[:, sl]
                m = _split_dot(u * x, bdm) * (1.0 / 64)
                o_ref[:, 1024 * part + 128 * cix:1024 * part + 128 * (cix + 1)] = (r * u - x * (r * r * r * m)).astype(BF16)
                dg_ref[:, sl] += jnp.sum(d * x * r, axis=0, keepdims=True)
        o_ref[:, 2048:3072] = dv_ref[...].astype(BF16)

    col = lambda c: pl.BlockSpec((tr, 1024), lambda i: (i, c))
    vec = pl.BlockSpec((1, 1024), lambda i: (0, 0))
    vsh = jax.ShapeDtypeStruct((1, 1024), F32)
    return _pcall(body, name="qk_norm_bwd", grid=(t // tr,),
                  in_specs=[col(0), col(1), col(0), col(0), col(0), vec, vec, pl.BlockSpec((128, 128), lambda i: (0, 0))],
                  out_specs=[pl.BlockSpec((tr, 3072), lambda i: (i, 0)), vec, vec],
                  out_shape=[jax.ShapeDtypeStruct((t, 3072), BF16), vsh, vsh])(qkv, qkv, dq, dk, dv, qg, kg, bd)


def _split2(x):
    hi = x.astype(BF16)
    lo = (x - hi.astype(F32)).astype(BF16)
    return jnp.concatenate([hi, lo], axis=1)


def _sb_scores(z, later_tab):
    e = jnp.exp(-jnp.abs(z))
    ope = 1.0 + e
    sp = jnp.maximum(z, 0.0) + jnp.log(ope)
    return e, ope, _dot(_split2(sp), later_tab)


def _sb_bias_index(i, kb, per):
    g = kb - i * per
    return jnp.where(kb == 0, jnp.where(i == 0, per + 2, per + 1), jnp.where(g >= 0, g, per))


def _sb_qb(t):
    return _tile(t, (384, 128))


def _sb_fwd(qh, kt, v2, later_tab, bias_tab):
    t = qh.shape[0]
    qb = _sb_qb(t)
    per = qb // CHUNK
    nkb_all = t // CHUNK

    def body(q_ref, kt_ref, v2_ref, tab_ref, bias_ref, o_ref, ws_ref, acc, carry, zbuf, wbuf, wsem):
        h, i = pl.program_id(0), pl.program_id(1)
        q = q_ref[...]
        acc[...] = jnp.zeros_like(acc)
        carry[...] = jnp.zeros_like(carry)
        nkb = (i + 1) * per
        save = lambda kb: pltpu.make_async_copy(wbuf.at[kb], ws_ref.at[h, i, kb], wsem.at[kb])

        for u in range(per):
            zbuf[u] = _dot(q, kt_ref[nkb - 1 - u])

        def step(s, _):
            top = nkb - 1 - per * s

            @pl.when(s > 0)
            def _():
                for u in range(per):
                    save(top + per - u).start()

            z2s = [zbuf[u] for u in range(per)]
            for u in range(per):
                zbuf[u] = _dot(q, kt_ref[jnp.maximum(top - per - u, 0)])
            cins = [carry[0], carry[1]]
            part = None
            for u in range(per):
                kb = top - u
                bias = bias_ref[_sb_bias_index(i, kb, per)]
                for hh in range(2):
                    sl = slice(128 * hh, 128 * (hh + 1))
                    z = z2s[u][:, sl] + bias
                    cu = _sb_scores(z, tab_ref[...])[2]
                    wbuf[kb, :, sl] = jnp.exp(z - cu[:, :128] - cins[hh]).astype(BF16)
                    cins[hh] = cins[hh] + cu[:, 128:]
                d = _dot(wbuf[kb], v2_ref[kb])
                part = d if part is None else part + d
            carry[0], carry[1] = cins[0], cins[1]
            acc[...] += part
            return 0

        lax.fori_loop(0, nkb // per, step, 0)
        for u in range(per):
            save(per - 1 - u).start()
        o_ref[...] = acc[...]

        def drain(kb, _):
            save(kb).wait()
            return 0

        lax.fori_loop(0, nkb, drain, 0)

    blk = pl.BlockSpec((qb, 128), lambda h, i: (i, h))
    wide = pl.BlockSpec((None, nkb_all, 128, 256), lambda h, i: (h, 0, 0, 0))
    tall = pl.BlockSpec((None, nkb_all, 256, 128), lambda h, i: (h, 0, 0, 0))
    return _pcall(body, name="sb_fwd", grid=(8, t // qb),
                  in_specs=[blk, wide, tall, pl.BlockSpec((256, 256), lambda h, i: (0, 0)),
                            pl.BlockSpec((per + 3, qb, 128), lambda h, i: (0, 0, 0))],
                  out_specs=[blk, pl.BlockSpec(memory_space=pl.ANY)],
                  out_shape=[jax.ShapeDtypeStruct((t, 1024), F32),
                             jax.ShapeDtypeStruct((8, t // qb, nkb_all, qb, 256), BF16)],
                  scratch_shapes=[pltpu.VMEM((qb, 128), F32), pltpu.VMEM((2, qb, 128), F32),
                                  pltpu.VMEM((per, qb, 256), F32), pltpu.VMEM((nkb_all, qb, 256), BF16),
                                  pltpu.SemaphoreType.DMA((nkb_all,))],
                  compiler_params=_params(dimension_semantics=("parallel", "arbitrary")))(
                      qh, kt, v2, later_tab, bias_tab)


def _sb_bwd(qh, kt, k2, vt, wsave, do, earlier_tab, bias_tab):
    t = qh.shape[0]
    qb = _sb_qb(t)
    per = qb // CHUNK
    nkb_all = t // CHUNK

    zero_slot = nkb_all

    def body(q_ref, kt_ref, k2_ref, vt_ref, ws_ref, do_ref, etab_ref, bias_ref,
             dq_ref, dk_ref, dv_ref, acc, gcarry, zbuf, dwbuf, wbuf, wsem, dzbuf):
        h, i = pl.program_id(0), pl.program_id(1)

        @pl.when(i == 0)
        def _():
            dk_ref[...] = jnp.zeros_like(dk_ref)
            dv_ref[...] = jnp.zeros_like(dv_ref)

        nkb = (i + 1) * per
        fetch = lambda kb: pltpu.make_async_copy(ws_ref.at[h, i, kb], wbuf.at[kb], wsem.at[kb])

        def prefetch(kb, _):
            fetch(kb).start()
            return 0

        lax.fori_loop(0, nkb, prefetch, 0)
        q = q_ref[...]
        dob = do_ref[...].astype(BF16)
        lane = lax.broadcasted_iota(jnp.int32, (1, 128), 1)
        acc[...] = jnp.zeros_like(acc)
        gcarry[...] = jnp.zeros_like(gcarry)
        zbuf[...] = _dot(q, kt_ref[0])
        dwbuf[...] = _dot(dob, vt_ref[0])
        dzbuf[...] = jnp.zeros_like(dzbuf)
        wbuf[zero_slot] = jnp.zeros((qb, 256), BF16)

        def gradients(slot, kb):
            dz2 = dzbuf[...]
            acc[...] += _dot(dz2, k2_ref[kb])
            dk2 = _dot(dz2, q, "tn")
            dv2 = _dot(wbuf[slot], dob, "tn")
            dk_ref[kb] += jnp.where(lane < 64, dk2[:128], dk2[128:])
            dv_ref[kb] += jnp.where(lane < 64, dv2[:128], dv2[128:])

        def step(kb, _):
            fetch(kb).wait()
            bias = bias_ref[_sb_bias_index(i, kb, per)]
            z2 = zbuf[...]
            dw2 = dwbuf[...]
            nxt = jnp.minimum(kb + 1, nkb - 1)
            zbuf[...] = _dot(q, kt_ref[nxt])
            dwbuf[...] = _dot(dob, vt_ref[nxt])
            gradients(jnp.where(kb == 0, zero_slot, kb - 1), jnp.maximum(kb - 1, 0))
            w2 = wbuf[kb]
            for hh in range(2):
                sl = slice(128 * hh, 128 * (hh + 1))
                z = z2[:, sl] + bias
                e = jnp.exp(-jnp.abs(z))
                r = 1.0 / (1.0 + e)
                sig = jnp.where(z >= 0, r, e * r)
                gw = w2[:, sl].astype(F32) * dw2[:, sl]
                cu2 = _dot(_split2(gw), etab_ref[...])
                gin = gcarry[hh]
                gcarry[hh] = gin + cu2[:, 128:]
                dzbuf[:, sl] = (gw - sig * (gw + cu2[:, :128] + gin)).astype(BF16)
            return 0

        lax.fori_loop(0, nkb, step, 0)
        gradients(nkb - 1, nkb - 1)
        dq_ref[...] = acc[...] * SB_SCALE

    blk = pl.BlockSpec((qb, 128), lambda h, i: (i, h))
    wide = pl.BlockSpec((None, nkb_all, 128, 256), lambda h, i: (h, 0, 0, 0))
    tall = pl.BlockSpec((None, nkb_all, 256, 128), lambda h, i: (h, 0, 0, 0))
    tab = pl.BlockSpec((256, 256), lambda h, i: (0, 0))
    kv_out = pl.BlockSpec((nkb_all, 128, 128), lambda h, i: (0, 0, h))
    ksh = jax.ShapeDtypeStruct((nkb_all, 128, 1024), F32)
    dq, dk, dv = _pcall(
        body, name="sb_bwd", grid=(8, t // qb),
        in_specs=[blk, wide, tall, wide, pl.BlockSpec(memory_space=pl.ANY), blk, tab,
                  pl.BlockSpec((per + 3, qb, 128), lambda h, i: (0, 0, 0))],
        out_specs=[blk, kv_out, kv_out], out_shape=[jax.ShapeDtypeStruct((t, 1024), F32), ksh, ksh],
        scratch_shapes=[pltpu.VMEM((qb, 128), F32), pltpu.VMEM((2, qb, 128), F32),
                        pltpu.VMEM((qb, 256), F32), pltpu.VMEM((qb, 256), F32),
                        pltpu.VMEM((nkb_all + 1, qb, 256), BF16), pltpu.SemaphoreType.DMA((nkb_all,)),
                        pltpu.VMEM((qb, 256), BF16)],
        compiler_params=_params(dimension_semantics=("parallel", "arbitrary")))(
            qh, kt, k2, vt, wsave, do, earlier_tab, bias_tab)
    return dq, dk.reshape(t, 1024), dv.reshape(t, 1024)


def _adamw_math(w, g, m, v):
    m = ADAM_B1 * m + (1.0 - ADAM_B1) * g
    v = ADAM_B2 * v + (1.0 - ADAM_B2) * (g * g)
    m_hat = m / (1.0 - ADAM_B1 ** ADAM_STEP)
    v_hat = v / (1.0 - ADAM_B2 ** ADAM_STEP)
    delta = -ADAM_LR * (m_hat / (jnp.sqrt(v_hat) + ADAM_EPS) + ADAM_WD * w)
    return delta, m, v


def _adamw(name, w, owns, recvs, m, v, me):
    shape = w.shape
    c = shape[-1]
    nl = len(owns)
    w3, m3, v3 = (a.reshape(nl, -1, c) for a in (w, m, v))
    r = w3.shape[1]
    tr = _tile(r, (256, 128))
    owns = [o.reshape(N_DEV, r, c) for o in owns]
    recvs = [p.reshape(N_DEV - 1, r, c) for p in recvs]

    def body(me_ref, w_ref, *rest):
        own_refs, recv_refs = rest[:nl], rest[nl:2 * nl]
        m_ref, v_ref = rest[2 * nl:2 * nl + 2]
        g_out, d_out, m_out, v_out = rest[2 * nl + 2:]
        layer = pl.program_id(0)

        def grad(k):
            g = own_refs[k][...].astype(F32)
            for s in range(N_DEV - 1):
                g = g + recv_refs[k][s].astype(F32)
            return g

        g = grad(0)
        for k in range(1, nl):
            g = jnp.where(layer == k, grad(k), g)
        d, mn, vn = _adamw_math(w_ref[...], g, m_ref[...], v_ref[...])
        g_out[...] = g
        d_out[...] = d
        m_out[...] = mn
        v_out[...] = vn

    row = pl.BlockSpec((None, tr, c), lambda l, i, me_ref: (l, i, 0))
    own = lambda k: pl.BlockSpec((None, tr, c), lambda l, i, me_ref: (me_ref[0], jnp.where(l == k, i, 0), 0))
    rcv = lambda k: pl.BlockSpec((N_DEV - 1, tr, c), lambda l, i, me_ref: (0, jnp.where(l == k, i, 0), 0))
    osh = jax.ShapeDtypeStruct((nl, r, c), F32)
    grid_spec = pltpu.PrefetchScalarGridSpec(
        num_scalar_prefetch=1, grid=(nl, r // tr),
        in_specs=[row] + [own(k) for k in range(nl)] + [rcv(k) for k in range(nl)] + [row, row],
        out_specs=[row, row, row, row])
    outs = _pcall(body, name=name, grid_spec=grid_spec, out_shape=[osh, osh, osh, osh])(
        me.reshape(1), w3, *owns, *recvs, m3, v3)
    return tuple(o.reshape(shape) for o in outs)


def _place():
    x, y, c = lax.axis_index("x"), lax.axis_index("y"), lax.axis_index("c")
    return x, y, c, 4 * x + 2 * y + c


def _peer(x, y, c, rel):
    return (x ^ ((rel >> 2) & 1), y ^ ((rel >> 1) & 1), c ^ (rel & 1))


def _gather_first(now, later):
    n, k = len(now), len(later)

    def body(*refs):
        ins, outs = refs[:n + k], refs[n + k:2 * (n + k)]
        send, recv, lsem = refs[2 * (n + k):]
        x, y, c, me = _place()
        locals_ = []
        for w in range(n + k):
            local = pltpu.make_async_copy(ins[w], outs[w].at[me], lsem.at[w])
            local.start()
            locals_.append(local)
        def copy(w, src, slot, rel, to_rel):
            return pltpu.make_async_remote_copy(src_ref=src, dst_ref=outs[w].at[slot], send_sem=send.at[w, rel - 1],
                                                recv_sem=recv.at[w, rel - 1], device_id=_peer(x, y, c, to_rel),
                                                device_id_type=MESH)

        for w in range(n):
            for rel in (1, 2, 4, 6):
                copy(w, ins[w], me, rel, rel).start()
        for w in range(n):
            for rel in (2, 4, 6):
                copy(w, ins[w], me ^ rel, rel, rel).wait_recv()
                copy(w, outs[w].at[me ^ rel], me ^ rel, rel | 1, 1).start()
        for w in range(n):
            for rel in (1, 3, 5, 7):
                copy(w, ins[w], me ^ rel, rel, 1).wait_recv()
            for rel in range(1, N_DEV):
                copy(w, ins[w], me, rel, rel).wait_send()
        for local in locals_:
            local.wait()

    hbm = pl.BlockSpec(memory_space=pl.ANY)
    vmem = pl.BlockSpec(memory_space=pltpu.VMEM)
    arrays = list(now) + list(later)
    return _pcall(body, name="gather_first", in_specs=[vmem] * (n + k), out_specs=[hbm] * (n + k),
                  out_shape=[jax.ShapeDtypeStruct((N_DEV,) + a.shape, a.dtype) for a in arrays],
                  scratch_shapes=[pltpu.SemaphoreType.DMA((n, N_DEV - 1)), pltpu.SemaphoreType.DMA((n, N_DEV - 1)),
                                  pltpu.SemaphoreType.DMA((n + k,))],
                  compiler_params=_params(has_side_effects=True))(*arrays)


_HBM = pl.BlockSpec(memory_space=pltpu.HBM)
_SEM = pl.BlockSpec(memory_space=pltpu.SEMAPHORE)
_DATAFLOW = pltpu.SideEffectType.DATAFLOW_SIDE_EFFECTING


def _exchange_refs(srcs, lands, mode, me, rel, j):
    if mode == "gather":
        return srcs[j], lands[j].at[me], lands[j].at[me ^ rel]
    return srcs[j].at[me ^ rel], lands[j].at[rel - 1], lands[j].at[rel - 1]


def _exchange_start(name, srcs, lands, mode):
    n = len(srcs)

    def body(*refs):
        ins, lnd = refs[:n], refs[n:2 * n]
        send, recv = refs[2 * n], refs[2 * n + 1]
        token = refs[-1]
        x, y, c, me = _place()
        for j in range(n):
            for rel in range(1, N_DEV):
                src, dst, _ = _exchange_refs(ins, lnd, mode, me, rel, j)
                pltpu.make_async_remote_copy(src_ref=src, dst_ref=dst, send_sem=send.at[j * (N_DEV - 1) + rel - 1],
                                             recv_sem=recv.at[j * (N_DEV - 1) + rel - 1],
                                             device_id=_peer(x, y, c, rel), device_id_type=MESH).start()
        token[...] = jnp.zeros_like(token)

    sems = pltpu.SemaphoreType.DMA((n * (N_DEV - 1),))
    hbm_like = lambda a: pltpu.HBM(a.shape, a.dtype)
    outs = _pcall(body, name=name + "_start",
                  in_specs=[_HBM] * (2 * n), out_specs=[_SEM, _SEM] + [_HBM] * (2 * n) + [pl.BlockSpec(memory_space=pltpu.VMEM)],
                  out_shape=[sems, sems] + [hbm_like(a) for a in srcs] + [hbm_like(a) for a in lands]
                  + [jax.ShapeDtypeStruct((8, 128), F32)],
                  input_output_aliases={i: 2 + i for i in range(2 * n)},
                  compiler_params=pltpu.CompilerParams(has_side_effects=_DATAFLOW))(
                      *[pltpu.with_memory_space_constraint(a, pltpu.HBM) for a in list(srcs) + list(lands)])
    return dict(name=name, mode=mode, n=n, send=outs[0], recv=outs[1], srcs=outs[2:2 + n], lands=outs[2 + n:2 + 2 * n],
                token=outs[-1][0, 0])


def _exchange_wait(ex, after):
    n, mode = ex["n"], ex["mode"]

    def body(*refs):
        ins, lnd = refs[:n], refs[n:2 * n]
        send, recv = refs[2 * n], refs[2 * n + 1]
        x, y, c, me = _place()
        for j in range(n):
            for rel in range(1, N_DEV):
                src, dst, landed = _exchange_refs(ins, lnd, mode, me, rel, j)
                pltpu.make_async_remote_copy(src_ref=src, dst_ref=dst, send_sem=send.at[j * (N_DEV - 1) + rel - 1],
                                             recv_sem=recv.at[j * (N_DEV - 1) + rel - 1],
                                             device_id=_peer(x, y, c, rel), device_id_type=MESH).wait_send()
                pltpu.make_async_remote_copy(src_ref=src, dst_ref=landed, send_sem=send.at[j * (N_DEV - 1) + rel - 1],
                                             recv_sem=recv.at[j * (N_DEV - 1) + rel - 1],
                                             device_id=_peer(x, y, c, rel), device_id_type=MESH).wait_recv()

    hbm_like = lambda a: pltpu.HBM(a.shape, a.dtype)
    arrays = list(ex["srcs"]) + list(ex["lands"])
    outs = _pcall(body, name=ex["name"] + "_wait",
                  in_specs=[_HBM] * (2 * n) + [_SEM, _SEM, pl.BlockSpec(memory_space=pl.ANY)],
                  out_specs=[_HBM] * (2 * n), out_shape=[hbm_like(a) for a in arrays],
                  input_output_aliases={i: i for i in range(2 * n)},
                  compiler_params=pltpu.CompilerParams(has_side_effects=_DATAFLOW))(
                      *arrays, ex["send"], ex["recv"], after)
    return outs[:n], outs[n:]


def _scatter_start(name, grads):
    lands = [lax.empty((N_DEV - 1,) + g.shape[1:], g.dtype) for g in grads]
    return _exchange_start(name, grads, lands, "scatter")


ROW_MIX, ROW_MLP, ROW_CB, ROW_LG, ROW_LB, ROW_QN, ROW_KN, ROW_LOSS = 0, 2, 4, 5, 6, 7, 8, 9
ROW_META, ROW_CW, ROW_GN, SMALL_ROWS = 16, 32, 64, 72


def _sum_small(slots):
    def body(s_ref, o_ref):
        tot = s_ref[0]
        for s in range(1, N_DEV):
            tot = tot + s_ref[s]
        o_ref[...] = tot
        for row in (ROW_QN, ROW_KN):
            v = tot[row:row + 1, :]
            f = v[:, 0:128]
            for k in range(1, 8):
                f = f + v[:, 128 * k:128 * (k + 1)]
            o_ref[row:row + 1, 0:64] = f[:, 0:64] + f[:, 64:128]

    return _pcall(body, name="sum_small", out_shape=jax.ShapeDtypeStruct(slots.shape[1:], F32))(slots)


def _adamw_small(w, g, m, v):
    def body(w_ref, g_ref, m_ref, v_ref, d_out, m_out, v_out):
        d, mn, vn = _adamw_math(w_ref[...], g_ref[...], m_ref[...], v_ref[...])
        d_out[...] = d
        m_out[...] = mn
        v_out[...] = vn

    osh = jax.ShapeDtypeStruct(w.shape, F32)
    return _pcall(body, name="adamw_small", out_shape=[osh, osh, osh])(w, g, m, v)


def _local_step(h0, target, p, weight, emit):
    t = h0.shape[0]
    tables = _ret_tables(t)
    bd, later_tab, earlier_tab, bias_tab = _seg_tables(_sb_qb(t))
    row = lambda a, i: a[i:i + 1]

    hn_a = _rms_fwd("rms_mix0", h0, row(p["norm_mix_g"], 0))
    w_in = weight("w_in", hn_a)
    proj = _mm_cols("proj_in", hn_a, w_in, ())
    o_ret, states = _ret_fwd(proj, tables)
    gn_flat = p["gn_g"].reshape(1, 1024)
    cat = _gn_gate_fwd(o_ret, proj, gn_flat)
    cat, hdn, ycv = _conv_fwd(cat, proj, p["conv_w"], p["conv_b"], p["ln_g"], p["ln_b"])
    w_out = weight("w_out", cat)
    h1 = _mm_rows("mix_out", cat, w_out, h0)
    hn_b = _rms_fwd("rms_mlp0", h1, row(p["norm_mlp_g"], 0))
    w1_0, w2_0 = weight("w1_0", hn_b), weight("w2_0", hn_b)
    a0, s0 = _mm_cols("mlp0_up", hn_b, w1_0, (), epi="relu2")
    h2 = _mm_rows("mlp0_down", s0, w2_0, h1)

    hn_c = _rms_fwd("rms_mix1", h2, row(p["norm_mix_g"], 1))
    w_qkv = weight("w_qkv", hn_c)
    qkv = _mm_cols("qkv", hn_c, w_qkv, ())
    qg = jnp.tile(p["qn_g"], (1, 16))
    kg = jnp.tile(p["kn_g"], (1, 16))
    qh, kt, k2, vt, v2 = _qk_norm_fwd(qkv, qg, kg, bd)
    o_sb, w_sb = _sb_fwd(qh, kt, v2, later_tab, bias_tab)
    w_o = weight("w_o", o_sb)
    h3 = _mm_rows("attn_out", o_sb, w_o, h2)
    hn_d = _rms_fwd("rms_mlp1", h3, row(p["norm_mlp_g"], 1))
    w1_1, w2_1 = weight("w1_1", hn_d), weight("w2_1", hn_d)
    a1, s1 = _mm_cols("mlp1_up", hn_d, w1_1, (), epi="relu2")
    h4 = _mm_rows("mlp1_down", s1, w2_1, h3)

    dh, loss = _loss_bwd(h4, target)

    def mlp_bwd(tag, layer, w1, w2, dh, h_in, hn, a, s):
        da = _mm_rows_t(f"{tag}_dact", dh, w2, (), out_dtype=BF16, epi="drelu2", extra=a)
        dw2 = _wgrad_rows(f"{tag}_dw2", s, dh, 512)
        dw1 = _wgrad_cols(f"{tag}_dw1", hn, da, 512)
        tok = emit(tag, [dw1, dw2])
        dhn = _mm_cols_t(f"{tag}_dhn", da, w1)
        return _rms_bwd(f"{tag}_rms_bwd", dhn, h_in, row(p["norm_mlp_g"], layer) + tok, dh)

    dh, dg_mlp1 = mlp_bwd("mlp1", 1, w1_1, w2_1, dh, h3, hn_d, a1, s1)

    do_sb = _mm_rows_t("attn_dout", dh, w_o, ())
    dw_o = _wgrad_rows("attn_dwo", o_sb, dh, 128)
    dq, dk, dv = _sb_bwd(qh, kt, k2, vt, w_sb, do_sb, earlier_tab, bias_tab)
    dqkv, dqg, dkg = _qk_norm_bwd(qkv, dq, dk, dv, qg, kg, bd)
    dw_qkv = _wgrad_cols("qkv_dw", hn_c, dqkv, 384)
    tok = emit("attn", [dw_qkv, dw_o])
    dhn = _mm_cols_t("qkv_dhn", dqkv, w_qkv)
    dh, dg_mix1 = _rms_bwd("mix1_rms_bwd", dhn, h2, row(p["norm_mix_g"], 1) + tok, dh)

    dh, dg_mlp0 = mlp_bwd("mlp0", 0, w1_0, w2_0, dh, h1, hn_b, a0, s0)

    dcat = _mm_rows_t("mix_dcat", dh, w_out, ())
    dw_out = _wgrad_rows("mix_dwout", cat, dh, 256)
    do_ret, dgate, dgn = _gn_gate_bwd(dcat, o_ret, proj, gn_flat)
    dqk_r, dv_r = _ret_bwd(proj, states, do_ret, tables)
    dy, dlg, dlb, dcb = _conv_bwd_ln(dcat, ycv, p["ln_g"], p["ln_b"])
    dua, dug, dcw = _conv_bwd_taps(dy, hdn, proj, p["conv_w"])
    dproj = jnp.concatenate([dqk_r, dv_r, dgate, dua, dug], axis=1)
    dw_in = _wgrad_cols("proj_dw", hn_a, dproj, 640)
    tok = emit("mix0", [dw_in, dw_out])
    dhn = _mm_cols_t("proj_dhn", dproj, w_in)
    dh, dg_mix0 = _rms_bwd("mix0_rms_bwd", dhn, h0, row(p["norm_mix_g"], 0) + tok, dh)

    rid = lax.broadcasted_iota(jnp.int32, (16, 1), 0)
    loss_row = jnp.broadcast_to(loss[0:1, 0:1], (1, D_MODEL))
    vecs = sum(jnp.where(rid == k, v, 0.0)
               for k, v in enumerate((dg_mix0, dg_mix1, dg_mlp0, dg_mlp1, dcb, dlg, dlb, dqg, dkg, loss_row)))
    small = jnp.concatenate([vecs, dh[PAD_FRONT:TOK0], dcw, jnp.where(rid[:8] == 0, dgn, 0.0)], axis=0)
    return dh[TOK0:], small


_SMALL_NAMES = ("meta", "norm_mix_g", "norm_mlp_g", "even_ret_gn_g", "even_conv_w", "even_conv_b",
                "even_conv_ln_g", "even_conv_ln_b", "odd_q_norm_g", "odd_k_norm_g")
_BIG_NAMES = ("even_w_in", "even_w_out", "odd_w_qkv", "odd_w_o", "mlp_w1", "mlp_w2")
_ORDER = ("meta", "norm_mix_g", "norm_mlp_g", "even_w_in", "even_ret_gn_g", "even_conv_w", "even_conv_b",
          "even_conv_ln_g", "even_conv_ln_b", "even_w_out", "odd_w_qkv", "odd_q_norm_g", "odd_k_norm_g",
          "odd_w_o", "mlp_w1", "mlp_w2")


def _pack128(a):
    flat = a.reshape(-1)
    n = flat.shape[0]
    rows = -(-n // 128)
    rows8 = -(-rows // 8) * 8
    return jnp.pad(flat, (0, rows8 * 128 - n)).reshape(rows8, 128)


def kernel(x, meta, norm_mix_g, norm_mlp_g, even_w_in, even_ret_gn_g, even_conv_w, even_conv_b, even_conv_ln_g, even_conv_ln_b, even_w_out, odd_w_qkv, odd_q_norm_g, odd_k_norm_g, odd_w_o, mlp_w1, mlp_w2, loss_target, m_meta, m_norm_mix_g, m_norm_mlp_g, m_even_w_in, m_even_ret_gn_g, m_even_conv_w, m_even_conv_b, m_even_conv_ln_g, m_even_conv_ln_b, m_even_w_out, m_odd_w_qkv, m_odd_q_norm_g, m_odd_k_norm_g, m_odd_w_o, m_mlp_w1, m_mlp_w2, v_meta, v_norm_mix_g, v_norm_mlp_g, v_even_w_in, v_even_ret_gn_g, v_even_conv_w, v_even_conv_b, v_even_conv_ln_g, v_even_conv_ln_b, v_even_w_out, v_odd_w_qkv, v_odd_q_norm_g, v_odd_k_norm_g, v_odd_w_o, v_mlp_w1, v_mlp_w2):
    w = dict(meta=meta, norm_mix_g=norm_mix_g, norm_mlp_g=norm_mlp_g, even_w_in=even_w_in,
             even_ret_gn_g=even_ret_gn_g, even_conv_w=even_conv_w, even_conv_b=even_conv_b,
             even_conv_ln_g=even_conv_ln_g, even_conv_ln_b=even_conv_ln_b, even_w_out=even_w_out,
             odd_w_qkv=odd_w_qkv, odd_q_norm_g=odd_q_norm_g, odd_k_norm_g=odd_k_norm_g, odd_w_o=odd_w_o,
             mlp_w1=mlp_w1, mlp_w2=mlp_w2)
    mom = dict(meta=m_meta, norm_mix_g=m_norm_mix_g, norm_mlp_g=m_norm_mlp_g, even_w_in=m_even_w_in,
               even_ret_gn_g=m_even_ret_gn_g, even_conv_w=m_even_conv_w, even_conv_b=m_even_conv_b,
               even_conv_ln_g=m_even_conv_ln_g, even_conv_ln_b=m_even_conv_ln_b, even_w_out=m_even_w_out,
               odd_w_qkv=m_odd_w_qkv, odd_q_norm_g=m_odd_q_norm_g, odd_k_norm_g=m_odd_k_norm_g, odd_w_o=m_odd_w_o,
               mlp_w1=m_mlp_w1, mlp_w2=m_mlp_w2)
    var = dict(meta=v_meta, norm_mix_g=v_norm_mix_g, norm_mlp_g=v_norm_mlp_g, even_w_in=v_even_w_in,
               even_ret_gn_g=v_even_ret_gn_g, even_conv_w=v_even_conv_w, even_conv_b=v_even_conv_b,
               even_conv_ln_g=v_even_conv_ln_g, even_conv_ln_b=v_even_conv_ln_b, even_w_out=v_even_w_out,
               odd_w_qkv=v_odd_w_qkv, odd_q_norm_g=v_odd_q_norm_g, odd_k_norm_g=v_odd_k_norm_g, odd_w_o=v_odd_w_o,
               mlp_w1=v_mlp_w1, mlp_w2=v_mlp_w2)
    me = 4 * lax.axis_index("x") + 2 * lax.axis_index("y") + lax.axis_index("c")

    small_in = jnp.concatenate([meta, jnp.pad(even_conv_w[0], ((0, 1), (0, 0))),
                                jnp.pad(even_ret_gn_g[0], ((0, 4), (0, 96)))], axis=0)
    b16 = lambda a: a.astype(BF16)
    later_src = dict(w_out=b16(even_w_out[0]), w1_0=b16(mlp_w1[0]), w2_0=b16(mlp_w2[0]),
                     w_qkv=b16(odd_w_qkv[0]), w_o=b16(odd_w_o[0]), w1_1=b16(mlp_w1[1]), w2_1=b16(mlp_w2[1]))
    landed = _gather_first([b16(even_w_in[0]), small_in], list(later_src.values()))
    g_in, g_small = landed[0], landed[1]
    own_slot = dict(zip(later_src, landed[2:]))
    groups = (("gather_l0", ("w_out", "w1_0", "w2_0")), ("gather_attn", ("w_qkv", "w_o")),
              ("gather_l1", ("w1_1", "w2_1")))
    pending = {}
    gather_tok = jnp.zeros((), F32)
    for gname, names in groups:
        ex = _exchange_start(gname, [later_src[n] for n in names], [own_slot[n] for n in names], "gather")
        gather_tok = gather_tok + ex["token"]
        for n in names:
            pending[n] = (ex, names)
    arrived = dict(w_in=g_in)

    def weight(name, after):
        if name not in arrived:
            ex, names = pending[name]
            arrived.update(zip(names, _exchange_wait(ex, after)[1]))
        return arrived[name]

    cols = lambda a: jnp.transpose(a, (1, 0, 2)).reshape(a.shape[1], -1)
    p = dict(norm_mix_g=norm_mix_g + gather_tok, norm_mlp_g=norm_mlp_g, conv_b=even_conv_b, ln_g=even_conv_ln_g,
             ln_b=even_conv_ln_b, qn_g=odd_q_norm_g, kn_g=odd_k_norm_g,
             gn_g=cols(g_small[:, 48:52, :32]),
             conv_w=jnp.pad(cols(g_small[:, 16:47]), ((0, 1), (0, 0))))
    meta_full = cols(g_small[:, 0:16])

    scatters = {}

    def emit(tag, grads):
        scatters[tag] = _scatter_start("scatter_" + tag, grads)
        return scatters[tag]["token"]

    h0 = jnp.concatenate([jnp.zeros((PAD_FRONT, D_MODEL), F32), meta_full, x[0]], axis=0)
    grad_x, small_part = _local_step(h0, loss_target[0], p, weight, emit)

    out = {}
    got = {}

    def update(names, terms, after):
        for tag in {t for name in names for t, _ in terms[name]} - set(got):
            got[tag] = _exchange_wait(scatters[tag], after)
        for name in names:
            owns, recvs = zip(*[(got[t][0][j], got[t][1][j]) for t, j in terms[name]])
            out[name] = _adamw("adamw_" + name, w[name], list(owns), list(recvs), mom[name], var[name], me)

    terms = dict(even_w_in=[("mix0", 0)], even_w_out=[("mix0", 1)], odd_w_qkv=[("attn", 0)], odd_w_o=[("attn", 1)],
                 mlp_w1=[("mlp0", 0), ("mlp1", 0)], mlp_w2=[("mlp0", 1), ("mlp1", 1)])
    small_ex = _exchange_start("small", [small_part], [lax.empty((N_DEV,) + small_part.shape, F32)], "gather")
    update(("mlp_w1", "mlp_w2", "odd_w_qkv", "odd_w_o"), terms, grad_x)
    update(("even_w_in", "even_w_out"), terms, out["odd_w_o"][1])
    (own_part,), (slots,) = _exchange_wait(small_ex, out["even_w_in"][1])
    tot = _sum_small(lax.dynamic_update_slice(slots, own_part[None], (me, 0, 0)))
    loss = tot[ROW_LOSS, 0]

    shard_cols = lambda a, width: lax.dynamic_slice_in_dim(a, me * width, width, axis=1)
    one = lambda r: tot[r:r + 1]
    small_g = dict(
        norm_mix_g=tot[ROW_MIX:ROW_MIX + 2], norm_mlp_g=tot[ROW_MLP:ROW_MLP + 2],
        even_conv_b=one(ROW_CB), even_conv_ln_g=one(ROW_LG), even_conv_ln_b=one(ROW_LB),
        odd_q_norm_g=one(ROW_QN)[:, :64], odd_k_norm_g=one(ROW_KN)[:, :64],
        meta=shard_cols(tot[ROW_META:ROW_META + N_META], 128),
        even_conv_w=shard_cols(tot[ROW_CW:ROW_CW + CONV_WIDTH], 128)[None],
        even_ret_gn_g=shard_cols(tot[ROW_GN].reshape(4, 256), 32)[None])
    packs = {n: (_pack128(w[n]), _pack128(small_g[n]), _pack128(mom[n]), _pack128(var[n])) for n in _SMALL_NAMES}
    cat4 = [jnp.concatenate([packs[n][i] for n in _SMALL_NAMES], axis=0) for i in range(4)]
    d_s, m_s, v_s = _adamw_small(*cat4)
    r0 = 0
    for n in _SMALL_NAMES:
        rows = packs[n][0].shape[0]
        size = w[n].size
        take = lambda a: a[r0:r0 + rows].reshape(-1)[:size].reshape(w[n].shape)
        out[n] = (small_g[n].reshape(w[n].shape), take(d_s), take(m_s), take(v_s))
        r0 += rows

    res = [loss, grad_x[None]]
    for i in range(4):
        res.extend(out[n][i] for n in _ORDER)
    return tuple(res)
```

```python
import functools

import numpy as np
import jax
import jax.numpy as jnp
from jax import lax
from jax.experimental import pallas as pl
from jax.experimental.pallas import tpu as pltpu

F32 = jnp.float32
BF16 = jnp.bfloat16

D_MODEL = 1024
N_META = 16
CHUNK = 128
PAD_FRONT = 112
TOK0 = PAD_FRONT + N_META
EPS = 1e-6
N_DEV = 8
RET_HEADS = 4
RET_DECAY_OFFSET = 5.0
ROPE_BASE = 10000.0
CONV_WIDTH = 31
HALO = 32
SB_SCALE = 64 ** -0.5
RET_SCALE = 128 ** -0.5
ADAM_LR, ADAM_B1, ADAM_B2, ADAM_EPS, ADAM_WD, ADAM_STEP = 0.001, 0.9, 0.999, 1e-08, 0.01, 10
VMEM_LIMIT = 56 * 1024 * 1024
MESH = pl.DeviceIdType.MESH


def _pcall(body, **kw):
    return pl.pallas_call(body, **kw)


def _params(**kw):
    return pltpu.CompilerParams(vmem_limit_bytes=VMEM_LIMIT, **kw)


def _tile(n, cands):
    for c in cands:
        if n % c == 0:
            return c
    raise ValueError(f"no tile for {n} in {cands}")


def _sigmoid(x):
    return 1.0 / (1.0 + jnp.exp(-x))


_DIMS = {
    "nn": (((1,), (0,)), ((), ())),
    "nt": (((1,), (1,)), ((), ())),
    "tn": (((0,), (0,)), ((), ())),
}


def _matmul(name, a, b, *, grid, a_spec, b_spec, o_spec, out_shape, contract, acc_shape,
            epi="plain", extra=None, extra_spec=None):
    nk = grid[2]
    dims = _DIMS[contract]
    n_in = 3 if extra is not None else 2
    n_out = 2 if epi == "relu2" else 1

    def body(*refs):
        a_ref, b_ref = refs[0], refs[1]
        e_ref = refs[2] if extra is not None else None
        outs = refs[n_in:n_in + n_out]
        acc = refs[-1]
        k = pl.program_id(2)
        part = lax.dot_general(a_ref[...].astype(BF16), b_ref[...].astype(BF16), dims, preferred_element_type=F32)
        if nk > 1:
            @pl.when(k == 0)
            def _():
                acc[...] = jnp.zeros_like(acc)

            acc[...] += part

        @pl.when(k == nk - 1)
        def _():
            r = acc[...] if nk > 1 else part
            if epi == "plain":
                outs[0][...] = r.astype(outs[0].dtype)
            elif epi == "residual":
                outs[0][...] = (r + e_ref[...]).astype(outs[0].dtype)
            elif epi == "relu2":
                outs[0][...] = r
                rr = jnp.maximum(r, 0.0)
                outs[1][...] = (rr * rr).astype(BF16)
            elif epi == "drelu2":
                outs[0][...] = (r * (2.0 * jnp.maximum(e_ref[...], 0.0))).astype(outs[0].dtype)

    in_specs = [a_spec, b_spec] + ([extra_spec] if extra is not None else [])
    args = (a, b) + ((extra,) if extra is not None else ())
    if n_out == 2:
        out_specs = [o_spec, o_spec]
    else:
        out_specs = o_spec
    return _pcall(body, name=name, grid=grid, in_specs=in_specs, out_specs=out_specs,
                  out_shape=out_shape, scratch_shapes=[pltpu.VMEM(acc_shape, F32)],
                  compiler_params=_params(dimension_semantics=("parallel", "parallel", "arbitrary")))(*args)


def _tm(t):
    return _tile(t, (1408, 768, 384, 128))


def _mm_cols(name, a, wb, lead, out_dtype=F32, epi="plain"):
    t, kdim = a.shape
    n = wb.shape[-1]
    tm, tk = _tm(t), _tile(kdim, (1024, 512))
    nl = len(lead)
    b_spec = pl.BlockSpec((None,) * (1 + nl) + (tk, n), lambda i, j, k: (j,) + lead + (k, 0))
    o_spec = pl.BlockSpec((tm, n), lambda i, j, k: (i, j))
    if epi == "relu2":
        out_shape = [jax.ShapeDtypeStruct((t, N_DEV * n), F32), jax.ShapeDtypeStruct((t, N_DEV * n), BF16)]
    else:
        out_shape = jax.ShapeDtypeStruct((t, N_DEV * n), out_dtype)
    return _matmul(name, a, wb, grid=(t // tm, N_DEV, kdim // tk),
                   a_spec=pl.BlockSpec((tm, tk), lambda i, j, k: (i, k)), b_spec=b_spec, o_spec=o_spec,
                   out_shape=out_shape, contract="nn", acc_shape=(tm, n), epi=epi)


def _tm_deep(t, kdim):
    return _tm(t) if kdim <= 2048 else _tile(t, (704, 384, 128))


def _mm_cols_t(name, a, wb):
    t = a.shape[0]
    nb, kdim, n = wb.shape
    tm, tn = _tm_deep(t, nb * n), _tile(kdim, (512,))

    def body(a_ref, b_ref, o_ref):
        acc = _dot(a_ref[:, 0:n].astype(BF16), b_ref[0], "nt")
        for j in range(1, nb):
            acc = acc + _dot(a_ref[:, j * n:(j + 1) * n].astype(BF16), b_ref[j], "nt")
        o_ref[...] = acc

    return _pcall(body, name=name, grid=(t // tm, kdim // tn),
                  in_specs=[pl.BlockSpec((tm, nb * n), lambda i, j: (i, 0)),
                            pl.BlockSpec((nb, tn, n), lambda i, j: (0, j, 0))],
                  out_specs=pl.BlockSpec((tm, tn), lambda i, j: (i, j)),
                  out_shape=jax.ShapeDtypeStruct((t, kdim), F32),
                  compiler_params=_params(dimension_semantics=("parallel", "parallel")))(a, wb)


def _mm_rows(name, a, wb, residual):
    t = a.shape[0]
    nb, r, n = wb.shape
    tm, tn = _tm_deep(t, nb * r), _tile(n, (512,))

    def body(a_ref, b_ref, r_ref, o_ref):
        o_ref[...] = r_ref[...] + _dot(a_ref[...].astype(BF16), b_ref[...].reshape(nb * r, tn))

    o_spec = pl.BlockSpec((tm, tn), lambda i, j: (i, j))
    return _pcall(body, name=name, grid=(t // tm, n // tn),
                  in_specs=[pl.BlockSpec((tm, nb * r), lambda i, j: (i, 0)),
                            pl.BlockSpec((nb, r, tn), lambda i, j: (0, 0, j)), o_spec],
                  out_specs=o_spec, out_shape=jax.ShapeDtypeStruct((t, n), F32),
                  compiler_params=_params(dimension_semantics=("parallel", "parallel")))(a, wb, residual)


def _mm_rows_t(name, a, wb, lead, out_dtype=F32, epi="plain", extra=None):
    t, n = a.shape
    r = wb.shape[-2]
    tm, tk = _tm(t), _tile(n, (1024,))
    nl = len(lead)
    b_spec = pl.BlockSpec((None,) * (1 + nl) + (r, tk), lambda i, j, k: (j,) + lead + (0, k))
    o_spec = pl.BlockSpec((tm, r), lambda i, j, k: (i, j))
    return _matmul(name, a, wb, grid=(t // tm, N_DEV, n // tk),
                   a_spec=pl.BlockSpec((tm, tk), lambda i, j, k: (i, k)), b_spec=b_spec, o_spec=o_spec,
                   out_shape=jax.ShapeDtypeStruct((t, N_DEV * r), out_dtype), contract="nt",
                   acc_shape=(tm, r), epi=epi, extra=extra, extra_spec=o_spec if extra is not None else None)


def _wgrad_cols(name, x, dy, n):
    t, kdim = x.shape
    tk = _tm(t)
    return _matmul(name, x, dy, grid=(1, N_DEV, t // tk),
                   a_spec=pl.BlockSpec((tk, kdim), lambda i, j, k: (k, 0)),
                   b_spec=pl.BlockSpec((tk, n), lambda i, j, k: (k, j)),
                   o_spec=pl.BlockSpec((None, kdim, n), lambda i, j, k: (j, 0, 0)),
                   out_shape=jax.ShapeDtypeStruct((N_DEV, kdim, n), BF16), contract="tn", acc_shape=(kdim, n))


def _wgrad_rows(name, x, dy, r):
    t = x.shape[0]
    n = dy.shape[1]
    tk, tn = _tm(t), _tile(n, (512,))
    tm = min(N_DEV * r, 1024)
    out = _matmul(name, x, dy, grid=(N_DEV * r // tm, n // tn, t // tk),
                  a_spec=pl.BlockSpec((tk, tm), lambda i, j, k: (k, i)),
                  b_spec=pl.BlockSpec((tk, tn), lambda i, j, k: (k, j)),
                  o_spec=pl.BlockSpec((tm, tn), lambda i, j, k: (i, j)),
                  out_shape=jax.ShapeDtypeStruct((N_DEV * r, n), BF16), contract="tn", acc_shape=(tm, tn))
    return out.reshape(N_DEV, r, n)


def _rows(t):
    return _tile(t, (384, 128))


def _rms_fwd(name, h, g):
    t = h.shape[0]
    tr = _rows(t)

    def body(h_ref, g_ref, o_ref):
        x = h_ref[...]
        r = lax.rsqrt(jnp.mean(x * x, axis=-1, keepdims=True) + EPS)
        o_ref[...] = (x * r * g_ref[...]).astype(BF16)

    row = pl.BlockSpec((tr, D_MODEL), lambda i: (i, 0))
    vec = pl.BlockSpec((1, D_MODEL), lambda i: (0, 0))
    return _pcall(body, name=name, grid=(t // tr,), in_specs=[row, vec], out_specs=row,
                  out_shape=jax.ShapeDtypeStruct((t, D_MODEL), BF16))(h, g)


def _rms_bwd(name, dhn, h, g, dres):
    t = h.shape[0]
    tr = _rows(t)

    def body(d_ref, h_ref, g_ref, r_ref, o_ref, dg_ref):
        @pl.when(pl.program_id(0) == 0)
        def _():
            dg_ref[...] = jnp.zeros_like(dg_ref)

        x = h_ref[...]
        d = d_ref[...]
        r = lax.rsqrt(jnp.mean(x * x, axis=-1, keepdims=True) + EPS)
        u = d * g_ref[...]
        m = jnp.mean(u * x, axis=-1, keepdims=True)
        o_ref[...] = r_ref[...] + r * u - x * (r * r * r * m)
        dg_ref[...] += jnp.sum(d * x * r, axis=0, keepdims=True)

    row = pl.BlockSpec((tr, D_MODEL), lambda i: (i, 0))
    vec = pl.BlockSpec((1, D_MODEL), lambda i: (0, 0))
    return _pcall(body, name=name, grid=(t // tr,), in_specs=[row, row, vec, row], out_specs=[row, vec],
                  out_shape=[jax.ShapeDtypeStruct((t, D_MODEL), F32), jax.ShapeDtypeStruct((1, D_MODEL), F32)])(
                      dhn, h, g, dres)


def _loss_bwd(h, target):
    t = h.shape[0]
    nb = t // CHUNK

    def body(h_ref, t_ref, d_ref, l_ref):
        i = pl.program_id(0)

        @pl.when(i == 0)
        def _():
            d_ref[...] = jnp.zeros_like(d_ref)
            l_ref[...] = jnp.zeros_like(l_ref)

        @pl.when(i > 0)
        def _():
            diff = h_ref[...] - t_ref[...]
            d_ref[...] = diff * (1.0 / D_MODEL)
            l_ref[...] += jnp.sum(diff * diff) * (0.5 / D_MODEL)

    return _pcall(body, name="loss_bwd", grid=(nb,),
                  in_specs=[pl.BlockSpec((CHUNK, D_MODEL), lambda i: (i, 0)),
                            pl.BlockSpec((CHUNK, D_MODEL), lambda i: (jnp.maximum(i - 1, 0), 0))],
                  out_specs=[pl.BlockSpec((CHUNK, D_MODEL), lambda i: (i, 0)),
                             pl.BlockSpec((8, 128), lambda i: (0, 0))],
                  out_shape=[jax.ShapeDtypeStruct((t, D_MODEL), F32), jax.ShapeDtypeStruct((8, 128), F32)])(h, target)


def _ret_tables(t):
    hh = np.arange(RET_HEADS, dtype=np.float64)
    log_g = np.log1p(-np.exp2(-RET_DECAY_OFFSET - hh))
    idx = np.arange(CHUNK, dtype=np.float64)
    diff = idx[:, None] - idx[None, :]
    dmat = np.where(diff[None] >= 0, np.exp(np.maximum(diff, 0.0)[None] * log_g[:, None, None]), 0.0)
    qdec = np.exp((idx + 1.0)[None, :, None] * log_g[:, None, None]) * np.ones((1, 1, CHUNK))
    kdec = np.exp((CHUNK - 1 - idx)[None, :, None] * log_g[:, None, None]) * np.ones((1, 1, CHUNK))
    half = CHUNK // 2
    inv_freq = (ROPE_BASE ** (-np.arange(half, dtype=np.float32) / half)).astype(np.float32)
    ang = (np.arange(t, dtype=np.float32)[:, None] * inv_freq[None, :]).astype(np.float32).astype(np.float64)
    cos2 = np.concatenate([np.cos(ang), np.cos(ang)], axis=1)
    sin2 = np.concatenate([-np.sin(ang), np.sin(ang)], axis=1)
    return tuple(jnp.asarray(v, F32) for v in (dmat, qdec, kdec, cos2, sin2))


def _rot(x, c, s):
    return x * c + pltpu.roll(x, CHUNK // 2, 1) * s


def _unrot(dx, c, s):
    return dx * c + pltpu.roll(dx * s, CHUNK // 2, 1)


def _dot(a, b, contract="nn"):
    return lax.dot_general(a, b, _DIMS[contract], preferred_element_type=F32)


def _ret_fwd(proj, tables):
    t = proj.shape[0]
    nch = t // CHUNK
    dmat, qdec, kdec, cos2, sin2 = tables

    def body(qk_ref, v_ref, c_ref, s_ref, dm_ref, qd_ref, kd_ref, o_ref, st_ref, state):
        @pl.when(pl.program_id(0) == 0)
        def _():
            state[...] = jnp.zeros_like(state)

        c, s = c_ref[...], s_ref[...]
        for h in range(RET_HEADS):
            q = _rot(qk_ref[:, 128 * h:128 * (h + 1)], c, s)
            k = _rot(qk_ref[:, 512 + 128 * h:512 + 128 * (h + 1)], c, s) * RET_SCALE
            vb = v_ref[:, 256 * h:256 * (h + 1)].astype(BF16)
            st = state[h]
            st_ref[h] = st
            sc = _dot(q.astype(BF16), k.astype(BF16), "nt") * dm_ref[h]
            o = _dot(sc.astype(BF16), vb)
            o += _dot((q * qd_ref[h]).astype(BF16), st.astype(BF16))
            o_ref[:, 256 * h:256 * (h + 1)] = o
            kv = _dot((k * kd_ref[h]).astype(BF16), vb, "tn")
            state[h] = qd_ref[h, CHUNK - 1:CHUNK, 0:1] * st + kv

    tab = pl.BlockSpec((RET_HEADS, CHUNK, CHUNK), lambda n: (0, 0, 0))
    pos = pl.BlockSpec((CHUNK, CHUNK), lambda n: (n, 0))
    return _pcall(
        body, name="ret_fwd", grid=(nch,),
        in_specs=[pl.BlockSpec((CHUNK, 1024), lambda n: (n, 0)), pl.BlockSpec((CHUNK, 1024), lambda n: (n, 1)),
                  pos, pos, tab, tab, tab],
        out_specs=[pl.BlockSpec((CHUNK, 1024), lambda n: (n, 0)),
                   pl.BlockSpec((RET_HEADS, None, 128, 256), lambda n: (0, n, 0, 0))],
        out_shape=[jax.ShapeDtypeStruct((t, 1024), F32), jax.ShapeDtypeStruct((RET_HEADS, nch, 128, 256), F32)],
        scratch_shapes=[pltpu.VMEM((RET_HEADS, 128, 256), F32)],
        compiler_params=_params(dimension_semantics=("arbitrary",)))(
            proj, proj, cos2, sin2, dmat, qdec, kdec)


def _ret_bwd(dproj, proj, states, do, tables):
    t = proj.shape[0]
    nch = t // CHUNK
    dmat, qdec, kdec, cos2, sin2 = tables

    def body(dp_in, qk_ref, v_ref, do_ref, st_ref, c_ref, s_ref, dm_ref, qd_ref, kd_ref, dp_ref, rst):
        del dp_in
        @pl.when(pl.program_id(0) == 0)
        def _():
            rst[...] = jnp.zeros_like(rst)

        c, s = c_ref[...], s_ref[...]
        for h in range(RET_HEADS):
            q = _rot(qk_ref[:, 128 * h:128 * (h + 1)], c, s)
            k = _rot(qk_ref[:, 512 + 128 * h:512 + 128 * (h + 1)], c, s) * RET_SCALE
            qb, kb = q.astype(BF16), k.astype(BF16)
            vb = v_ref[:, 256 * h:256 * (h + 1)].astype(BF16)
            dob = do_ref[:, 256 * h:256 * (h + 1)].astype(BF16)
            pb = st_ref[h].astype(BF16)
            r = rst[h]
            rb = r.astype(BF16)
            dm, qd, kd = dm_ref[h], qd_ref[h], kd_ref[h]
            sb = (_dot(qb, kb, "nt") * dm).astype(BF16)
            dsb = (_dot(dob, vb, "nt") * dm).astype(BF16)
            dq = _dot(dsb, kb) + _dot(dob, pb, "nt") * qd
            dk = _dot(dsb, qb, "tn") + _dot(vb, rb, "nt") * kd
            dv = _dot(sb, dob, "tn") + _dot((k * kd).astype(BF16), rb)
            rst[h] = _dot((q * qd).astype(BF16), dob, "tn") + qd[CHUNK - 1:CHUNK, 0:1] * r
            dp_ref[:, 128 * h:128 * (h + 1)] = _unrot(dq, c, s).astype(BF16)
            dp_ref[:, 512 + 128 * h:512 + 128 * (h + 1)] = (_unrot(dk, c, s) * RET_SCALE).astype(BF16)
            dp_ref[:, 1024 + 256 * h:1024 + 256 * (h + 1)] = dv.astype(BF16)

    rev = lambda n: nch - 1 - n
    tab = pl.BlockSpec((RET_HEADS, CHUNK, CHUNK), lambda n: (0, 0, 0))
    pos = pl.BlockSpec((CHUNK, CHUNK), lambda n: (rev(n), 0))
    row = pl.BlockSpec((CHUNK, 1024), lambda n: (rev(n), 0))
    return _pcall(
        body, name="ret_bwd", grid=(nch,),
        in_specs=[pl.BlockSpec(memory_space=pl.ANY), row, pl.BlockSpec((CHUNK, 1024), lambda n: (rev(n), 1)), row,
                  pl.BlockSpec((RET_HEADS, None, 128, 256), lambda n: (0, rev(n), 0, 0)),
                  pos, pos, tab, tab, tab],
        out_specs=pl.BlockSpec((CHUNK, 2048), lambda n: (rev(n), 0)),
        out_shape=jax.ShapeDtypeStruct((t, 5120), BF16),
        scratch_shapes=[pltpu.VMEM((RET_HEADS, 128, 256), F32)], input_output_aliases={0: 0},
        compiler_params=_params(dimension_semantics=("arbitrary",)))(
            dproj, proj, proj, do, states, cos2, sin2, dmat, qdec, kdec)


def _gn_gate_fwd(o, proj, gn_g):
    t = o.shape[0]
    tr = _rows(t)

    def body(o_ref, g_ref, w_ref, c_ref):
        for h in range(RET_HEADS):
            sl = slice(256 * h, 256 * (h + 1))
            x = o_ref[:, sl]
            mu = jnp.mean(x, axis=-1, keepdims=True)
            xc = x - mu
            rstd = lax.rsqrt(jnp.mean(xc * xc, axis=-1, keepdims=True) + EPS)
            g = g_ref[:, sl]
            c_ref[:, sl] = (g * _sigmoid(g) * (xc * rstd * w_ref[:, sl])).astype(BF16)

    return _pcall(body, name="gn_gate_fwd", grid=(t // tr,),
                  in_specs=[pl.BlockSpec((tr, 1024), lambda i: (i, 0)),
                            pl.BlockSpec((tr, 1024), lambda i: (i, 2)),
                            pl.BlockSpec((1, 1024), lambda i: (0, 0))],
                  out_specs=pl.BlockSpec((tr, 1024), lambda i: (i, 0)),
                  out_shape=jax.ShapeDtypeStruct((t, 2048), BF16))(o, proj, gn_g)


def _gn_gate_bwd(dcat, o, proj, gn_g):
    t = o.shape[0]
    tr = _rows(t)

    def body(d_ref, o_ref, g_ref, w_ref, do_ref, dg_ref, dw_ref):
        @pl.when(pl.program_id(0) == 0)
        def _():
            dw_ref[...] = jnp.zeros_like(dw_ref)

        for h in range(RET_HEADS):
            sl = slice(256 * h, 256 * (h + 1))
            x = o_ref[:, sl]
            mu = jnp.mean(x, axis=-1, keepdims=True)
            xc = x - mu
            rstd = lax.rsqrt(jnp.mean(xc * xc, axis=-1, keepdims=True) + EPS)
            xh = xc * rstd
            w = w_ref[:, sl]
            g = g_ref[:, sl]
            sg = _sigmoid(g)
            d = d_ref[:, sl]
            don = d * (g * sg)
            dg_ref[:, sl] = (d * (xh * w) * (sg * (1.0 + g * (1.0 - sg)))).astype(BF16)
            dw_ref[:, sl] += jnp.sum(don * xh, axis=0, keepdims=True)
            dxh = don * w
            m1 = jnp.mean(dxh, axis=-1, keepdims=True)
            m2 = jnp.mean(dxh * xh, axis=-1, keepdims=True)
            do_ref[:, sl] = rstd * (dxh - m1 - xh * m2)

    row = pl.BlockSpec((tr, 1024), lambda i: (i, 0))
    vec = pl.BlockSpec((1, 1024), lambda i: (0, 0))
    return _pcall(body, name="gn_gate_bwd", grid=(t // tr,),
                  in_specs=[row, row, pl.BlockSpec((tr, 1024), lambda i: (i, 2)), vec],
                  out_specs=[row, pl.BlockSpec((tr, 1024), lambda i: (i, 2)), vec],
                  out_shape=[jax.ShapeDtypeStruct((t, 1024), F32), jax.ShapeDtypeStruct((t, 5120), BF16),
                             jax.ShapeDtypeStruct((1, 1024), F32)])(dcat, o, proj, gn_g)


def _row_ids(i, tr):
    return i * tr + lax.broadcasted_iota(jnp.int32, (tr, 1), 0)


SH_ROWS = HALO - 8


def _shifted_copies(xs, sh, tr):
    for b in range(1, 8):
        sh[b - 1] = xs[pl.ds(b, tr + SH_ROWS), :]


def _shifted(xs, sh, off, tr):
    a, b = divmod(off, 8)
    return xs[pl.ds(8 * a, tr), :] if b == 0 else sh[b - 1, pl.ds(8 * a, tr), :]


def _conv_fwd(cat, proj, conv_w, conv_b, ln_g, ln_b):
    t = proj.shape[0]
    tr = _rows(t)
    hb = tr // HALO

    def body(cat_in, ua_ref, ug_ref, pa_ref, pg_ref, w_ref, b_ref, lg_ref, lb_ref, c_ref, hd_ref, y_ref, xs, sh):
        del cat_in
        i = pl.program_id(0)
        hdn = ua_ref[...] * _sigmoid(ug_ref[...])
        hd_ref[...] = hdn
        prev = pa_ref[...] * _sigmoid(pg_ref[...])
        xs[0:HALO, :] = jnp.where(i > 0, prev, 0.0)
        xs[HALO:HALO + tr, :] = hdn
        _shifted_copies(xs, sh, tr)
        acc = jnp.zeros((tr, 1024), F32) + b_ref[...]
        for w in range(CONV_WIDTH):
            acc += w_ref[w:w + 1, :] * _shifted(xs, sh, HALO - (CONV_WIDTH - 1) + w, tr)
        y_ref[...] = acc
        mu = jnp.mean(acc, axis=-1, keepdims=True)
        yc = acc - mu
        rstd = lax.rsqrt(jnp.mean(yc * yc, axis=-1, keepdims=True) + EPS)
        yn = yc * rstd * lg_ref[...] + lb_ref[...]
        c = yn * _sigmoid(yn)
        c_ref[...] = jnp.where(_row_ids(i, tr) >= PAD_FRONT, c, 0.0).astype(BF16)

    row = pl.BlockSpec((tr, 1024), lambda i: (i, 0))
    vec = pl.BlockSpec((1, 1024), lambda i: (0, 0))
    halo = lambda col: pl.BlockSpec((HALO, 1024), lambda i: (jnp.maximum(i * hb - 1, 0), col))
    return _pcall(body, name="conv_fwd", grid=(t // tr,),
                  in_specs=[pl.BlockSpec(memory_space=pl.ANY),
                            pl.BlockSpec((tr, 1024), lambda i: (i, 3)), pl.BlockSpec((tr, 1024), lambda i: (i, 4)),
                            halo(3), halo(4), pl.BlockSpec((32, 1024), lambda i: (0, 0)), vec, vec, vec],
                  out_specs=[pl.BlockSpec((tr, 1024), lambda i: (i, 1)), row, row],
                  out_shape=[jax.ShapeDtypeStruct((t, 2048), BF16), jax.ShapeDtypeStruct((t, 1024), F32),
                             jax.ShapeDtypeStruct((t, 1024), F32)],
                  scratch_shapes=[pltpu.VMEM((tr + HALO, 1024), F32), pltpu.VMEM((7, tr + SH_ROWS, 1024), F32)],
                  input_output_aliases={0: 0}, compiler_params=_params())(
                      cat, proj, proj, proj, proj, conv_w, conv_b, ln_g, ln_b)


def _conv_bwd_ln(dcat, y, ln_g, ln_b):
    t = y.shape[0]
    tr = _rows(t)

    def body(d_ref, y_ref, lg_ref, lb_ref, dy_ref, dlg_ref, dlb_ref, dcb_ref):
        i = pl.program_id(0)

        @pl.when(i == 0)
        def _():
            dlg_ref[...] = jnp.zeros_like(dlg_ref)
            dlb_ref[...] = jnp.zeros_like(dlb_ref)
            dcb_ref[...] = jnp.zeros_like(dcb_ref)

        y = y_ref[...]
        mu = jnp.mean(y, axis=-1, keepdims=True)
        yc = y - mu
        rstd = lax.rsqrt(jnp.mean(yc * yc, axis=-1, keepdims=True) + EPS)
        xh = yc * rstd
        lg = lg_ref[...]
        yn = xh * lg + lb_ref[...]
        sg = _sigmoid(yn)
        dyn = jnp.where(_row_ids(i, tr) >= PAD_FRONT, d_ref[...] * (sg * (1.0 + yn * (1.0 - sg))), 0.0)
        dlg_ref[...] += jnp.sum(dyn * xh, axis=0, keepdims=True)
        dlb_ref[...] += jnp.sum(dyn, axis=0, keepdims=True)
        dxh = dyn * lg
        m1 = jnp.mean(dxh, axis=-1, keepdims=True)
        m2 = jnp.mean(dxh * xh, axis=-1, keepdims=True)
        dy = rstd * (dxh - m1 - xh * m2)
        dy_ref[...] = dy
        dcb_ref[...] += jnp.sum(dy, axis=0, keepdims=True)

    row = pl.BlockSpec((tr, 1024), lambda i: (i, 0))
    vec = pl.BlockSpec((1, 1024), lambda i: (0, 0))
    vshape = jax.ShapeDtypeStruct((1, 1024), F32)
    return _pcall(body, name="conv_bwd_ln", grid=(t // tr,),
                  in_specs=[pl.BlockSpec((tr, 1024), lambda i: (i, 1)), row, vec, vec],
                  out_specs=[row, vec, vec, vec],
                  out_shape=[jax.ShapeDtypeStruct((t, 1024), F32), vshape, vshape, vshape])(dcat, y, ln_g, ln_b)


def _conv_bwd_taps(dproj, dy, hdn, proj, conv_w):
    t = dy.shape[0]
    tr = _rows(t)
    hb = tr // HALO
    nt = t // tr

    def body(dp_in, dy_ref, nx_ref, hd_ref, ph_ref, ua_ref, ug_ref, w_ref, da_ref, dg_ref, dw_ref, xs, sh):
        del dp_in
        i = pl.program_id(0)

        @pl.when(i == 0)
        def _():
            dw_ref[...] = jnp.zeros_like(dw_ref)

        dy = dy_ref[...]
        xs[0:tr, :] = dy
        xs[tr:tr + HALO, :] = jnp.where(i < nt - 1, nx_ref[...], 0.0)
        _shifted_copies(xs, sh, tr)
        dh = jnp.zeros((tr, 1024), F32)
        for w in range(CONV_WIDTH):
            dh += w_ref[w:w + 1, :] * _shifted(xs, sh, CONV_WIDTH - 1 - w, tr)
        xs[0:HALO, :] = jnp.where(i > 0, ph_ref[...], 0.0)
        xs[HALO:HALO + tr, :] = hd_ref[...]
        _shifted_copies(xs, sh, tr)
        for w in range(CONV_WIDTH):
            dw_ref[w:w + 1, :] += jnp.sum(dy * _shifted(xs, sh, HALO - (CONV_WIDTH - 1) + w, tr), axis=0, keepdims=True)
        dh = jnp.where(_row_ids(i, tr) >= PAD_FRONT, dh, 0.0)
        sg = _sigmoid(ug_ref[...])
        da_ref[...] = (dh * sg).astype(BF16)
        dg_ref[...] = (dh * ua_ref[...] * sg * (1.0 - sg)).astype(BF16)

    row = pl.BlockSpec((tr, 1024), lambda i: (i, 0))
    return _pcall(body, name="conv_bwd_taps", grid=(nt,),
                  in_specs=[pl.BlockSpec(memory_space=pl.ANY),
                            row, pl.BlockSpec((HALO, 1024), lambda i: (jnp.minimum((i + 1) * hb, nt * hb - 1), 0)),
                            row, pl.BlockSpec((HALO, 1024), lambda i: (jnp.maximum(i * hb - 1, 0), 0)),
                            pl.BlockSpec((tr, 1024), lambda i: (i, 3)), pl.BlockSpec((tr, 1024), lambda i: (i, 4)),
                            pl.BlockSpec((32, 1024), lambda i: (0, 0))],
                  out_specs=[pl.BlockSpec((tr, 1024), lambda i: (i, 3)), row, pl.BlockSpec((32, 1024), lambda i: (0, 0))],
                  out_shape=[jax.ShapeDtypeStruct((t, 5120), BF16), jax.ShapeDtypeStruct((t, 1024), BF16),
                             jax.ShapeDtypeStruct((32, 1024), F32)],
                  scratch_shapes=[pltpu.VMEM((tr + HALO, 1024), F32), pltpu.VMEM((7, tr + SH_ROWS, 1024), F32)],
                  input_output_aliases={0: 0}, compiler_params=_params())(
                      dproj, dy, dy, hdn, hdn, proj, proj, conv_w)


NEG_BIG = -1e30


def _seg_tables(qb):
    j = np.arange(128)
    bd = (j[:, None] // 64 == j[None, :] // 64).astype(np.float32)
    ones = np.ones((128, 128), np.float32)
    later = np.concatenate([(j[:, None] >= j[None, :]).astype(np.float32), ones], axis=1)
    earlier = np.concatenate([(j[:, None] < j[None, :]).astype(np.float32), ones], axis=1)
    per = qb // CHUNK
    row = np.arange(qb)[:, None]
    pad = np.broadcast_to(j[None, :] < PAD_FRONT, (qb, 128))
    diag = [(g * CHUNK + j[None, :]) >= row for g in range(per)]
    masks = diag + [np.zeros((qb, 128), bool), pad, diag[0] | pad]
    bias = np.stack([np.where(m, NEG_BIG, 0.0) for m in masks]).astype(np.float32)
    dup = lambda m: np.concatenate([m, m], axis=0)
    return (jnp.asarray(bd, BF16), jnp.asarray(dup(later), BF16), jnp.asarray(dup(earlier), BF16),
            jnp.asarray(bias, F32))


def _split_dot(x, m):
    hi = x.astype(BF16)
    lo = (x - hi.astype(F32)).astype(BF16)
    return _dot(hi, m) + _dot(lo, m)


def _qk_norm_fwd(qkv, qg, kg, bd):
    t = qkv.shape[0]
    tr = _rows(t)
    nb = tr // CHUNK

    def body(q_ref, k_ref, v_ref, qg_ref, kg_ref, bd_ref, qo, kt, k2, vt, v2):
        bdm = bd_ref[...]
        lane = lax.broadcasted_iota(jnp.int32, (1, 128), 1)
        sub = lax.broadcasted_iota(jnp.int32, (128, 1), 0)

        def pair_layouts(x, t_ref, s_ref, hp, b):
            xt = x.T
            t_ref[hp, b] = jnp.concatenate([jnp.where(sub < 64, xt, 0.0), jnp.where(sub >= 64, xt, 0.0)],
                                           axis=1).astype(BF16)
            s_ref[hp, b] = jnp.concatenate([jnp.where(lane < 64, x, 0.0), jnp.where(lane >= 64, x, 0.0)],
                                           axis=0).astype(BF16)

        for hp in range(8):
            sl = slice(128 * hp, 128 * (hp + 1))
            x = q_ref[:, sl]
            r = lax.rsqrt(_split_dot(x * x, bdm) * (1.0 / 64) + EPS)
            qo[:, sl] = (x * r * (qg_ref[:, sl] * SB_SCALE)).astype(BF16)
            x = k_ref[:, sl]
            r = lax.rsqrt(_split_dot(x * x, bdm) * (1.0 / 64) + EPS)
            kn = x * r * kg_ref[:, sl]
            v = v_ref[:, sl]
            for b in range(nb):
                rows = slice(CHUNK * b, CHUNK * (b + 1))
                pair_layouts(kn[rows], kt, k2, hp, b)
                pair_layouts(v[rows], vt, v2, hp, b)

    col = lambda c: pl.BlockSpec((tr, 1024), lambda i: (i, c))
    vec = pl.BlockSpec((1, 1024), lambda i: (0, 0))
    wide = pl.BlockSpec((8, nb, 128, 256), lambda i: (0, i, 0, 0))
    tall = pl.BlockSpec((8, nb, 256, 128), lambda i: (0, i, 0, 0))
    wsh = jax.ShapeDtypeStruct((8, t // CHUNK, 128, 256), BF16)
    tsh = jax.ShapeDtypeStruct((8, t // CHUNK, 256, 128), BF16)
    return _pcall(body, name="qk_norm_fwd", grid=(t // tr,),
                  in_specs=[col(0), col(1), col(2), vec, vec, pl.BlockSpec((128, 128), lambda i: (0, 0))],
                  out_specs=[col(0), wide, tall, wide, tall],
                  out_shape=[jax.ShapeDtypeStruct((t, 1024), BF16), wsh, tsh, wsh, tsh])(qkv, qkv, qkv, qg, kg, bd)


def _qk_norm_bwd(qkv, dq, dk, dv, qg, kg, bd):
    t = qkv.shape[0]
    tr = _rows(t)

    def body(q_ref, k_ref, dq_ref, dk_ref, dv_ref, qg_ref, kg_ref, bd_ref, o_ref, dqg_ref, dkg_ref):
        @pl.when(pl.program_id(0) == 0)
        def _():
            dqg_ref[...] = jnp.zeros_like(dqg_ref)
            dkg_ref[...] = jnp.zeros_like(dkg_ref)

        bdm = bd_ref[...]
        for part, (src, d_ref, g_ref, dg_ref) in enumerate(((q_ref, dq_ref, qg_ref, dqg_ref),
                                                           (k_ref, dk_ref, kg_ref, dkg_ref))):
            for cix in range(8):
                sl = slice(128 * cix, 128 * (cix + 1))
                x = src[:, sl]
                d = d_ref[:, sl]
                r = lax.rsqrt(_split_dot(x * x, bdm) * (1.0 / 64) + EPS)
                u = d * g_ref[:, sl]
                m = _split_dot(u * x, bdm) * (1.0 / 64)
                o_ref[:, 1024 * part + 128 * cix:1024 * part + 128 * (cix + 1)] = (r * u - x * (r * r * r * m)).astype(BF16)
                dg_ref[:, sl] += jnp.sum(d * x * r, axis=0, keepdims=True)
        o_ref[:, 2048:3072] = dv_ref[...].astype(BF16)

    col = lambda c: pl.BlockSpec((tr, 1024), lambda i: (i, c))
    vec = pl.BlockSpec((1, 1024), lambda i: (0, 0))
    vsh = jax.ShapeDtypeStruct((1, 1024), F32)
    return _pcall(body, name="qk_norm_bwd", grid=(t // tr,),
                  in_specs=[col(0), col(1), col(0), col(0), col(0), vec, vec, pl.BlockSpec((128, 128), lambda i: (0, 0))],
                  out_specs=[pl.BlockSpec((tr, 3072), lambda i: (i, 0)), vec, vec],
                  out_shape=[jax.ShapeDtypeStruct((t, 3072), BF16), vsh, vsh])(qkv, qkv, dq, dk, dv, qg, kg, bd)


def _split2(x):
    hi = x.astype(BF16)
    lo = (x - hi.astype(F32)).astype(BF16)
    return jnp.concatenate([hi, lo], axis=1)


def _sb_scores(z, later_tab):
    e = jnp.exp(-jnp.abs(z))
    ope = 1.0 + e
    sp = jnp.maximum(z, 0.0) + jnp.log(ope)
    return e, ope, _dot(_split2(sp), later_tab)


def _sb_bias_index(i, kb, per):
    g = kb - i * per
    return jnp.where(kb == 0, jnp.where(i == 0, per + 2, per + 1), jnp.where(g >= 0, g, per))


def _sb_qb(t):
    return _tile(t, (384, 128))


def _sb_fwd(qh, kt, v2, later_tab, bias_tab):
    t = qh.shape[0]
    qb = _sb_qb(t)
    per = qb // CHUNK
    nkb_all = t // CHUNK

    def body(q_ref, kt_ref, v2_ref, tab_ref, bias_ref, o_ref, ws_ref, acc, carry, zbuf, wbuf, wsem):
        h, i = pl.program_id(0), pl.program_id(1)
        q = q_ref[...]
        acc[...] = jnp.zeros_like(acc)
        carry[...] = jnp.zeros_like(carry)
        nkb = (i + 1) * per
        save = lambda kb: pltpu.make_async_copy(wbuf.at[kb], ws_ref.at[h, i, kb], wsem.at[kb])

        for u in range(per):
            zbuf[u] = _dot(q, kt_ref[nkb - 1 - u])

        def step(s, _):
            top = nkb - 1 - per * s

            @pl.when(s > 0)
            def _():
                for u in range(per):
                    save(top + per - u).start()

            z2s = [zbuf[u] for u in range(per)]
            for u in range(per):
                zbuf[u] = _dot(q, kt_ref[jnp.maximum(top - per - u, 0)])
            cins = [carry[0], carry[1]]
            part = None
            for u in range(per):
                kb = top - u
                bias = bias_ref[_sb_bias_index(i, kb, per)]
                for hh in range(2):
                    sl = slice(128 * hh, 128 * (hh + 1))
                    z = z2s[u][:, sl] + bias
                    cu = _sb_scores(z, tab_ref[...])[2]
                    wbuf[kb, :, sl] = jnp.exp(z - cu[:, :128] - cins[hh]).astype(BF16)
                    cins[hh] = cins[hh] + cu[:, 128:]
                d = _dot(wbuf[kb], v2_ref[kb])
                part = d if part is None else part + d
            carry[0], carry[1] = cins[0], cins[1]
            acc[...] += part
            return 0

        lax.fori_loop(0, nkb // per, step, 0)
        for u in range(per):
            save(per - 1 - u).start()
        o_ref[...] = acc[...]

        def drain(kb, _):
            save(kb).wait()
            return 0

        lax.fori_loop(0, nkb, drain, 0)

    blk = pl.BlockSpec((qb, 128), lambda h, i: (i, h))
    wide = pl.BlockSpec((None, nkb_all, 128, 256), lambda h, i: (h, 0, 0, 0))
    tall = pl.BlockSpec((None, nkb_all, 256, 128), lambda h, i: (h, 0, 0, 0))
    return _pcall(body, name="sb_fwd", grid=(8, t // qb),
                  in_specs=[blk, wide, tall, pl.BlockSpec((256, 256), lambda h, i: (0, 0)),
                            pl.BlockSpec((per + 3, qb, 128), lambda h, i: (0, 0, 0))],
                  out_specs=[blk, pl.BlockSpec(memory_space=pl.ANY)],
                  out_shape=[jax.ShapeDtypeStruct((t, 1024), F32),
                             jax.ShapeDtypeStruct((8, t // qb, nkb_all, qb, 256), BF16)],
                  scratch_shapes=[pltpu.VMEM((qb, 128), F32), pltpu.VMEM((2, qb, 128), F32),
                                  pltpu.VMEM((per, qb, 256), F32), pltpu.VMEM((nkb_all, qb, 256), BF16),
                                  pltpu.SemaphoreType.DMA((nkb_all,))],
                  compiler_params=_params(dimension_semantics=("parallel", "arbitrary")))(
                      qh, kt, v2, later_tab, bias_tab)


def _sb_bwd(qh, kt, k2, vt, wsave, do, earlier_tab, bias_tab):
    t = qh.shape[0]
    qb = _sb_qb(t)
    per = qb // CHUNK
    nkb_all = t // CHUNK

    zero_slot = nkb_all

    def body(q_ref, kt_ref, k2_ref, vt_ref, ws_ref, do_ref, etab_ref, bias_ref,
             dq_ref, dk_ref, dv_ref, acc, gcarry, zbuf, dwbuf, wbuf, wsem, dzbuf):
        h, i = pl.program_id(0), pl.program_id(1)

        @pl.when(i == 0)
        def _():
            dk_ref[...] = jnp.zeros_like(dk_ref)
            dv_ref[...] = jnp.zeros_like(dv_ref)

        nkb = (i + 1) * per
        fetch = lambda kb: pltpu.make_async_copy(ws_ref.at[h, i, kb], wbuf.at[kb], wsem.at[kb])

        def prefetch(kb, _):
            fetch(kb).start()
            return 0

        lax.fori_loop(0, nkb, prefetch, 0)
        q = q_ref[...]
        dob = do_ref[...].astype(BF16)
        lane = lax.broadcasted_iota(jnp.int32, (1, 128), 1)
        acc[...] = jnp.zeros_like(acc)
        gcarry[...] = jnp.zeros_like(gcarry)
        zbuf[...] = _dot(q, kt_ref[0])
        dwbuf[...] = _dot(dob, vt_ref[0])
        dzbuf[...] = jnp.zeros_like(dzbuf)
        wbuf[zero_slot] = jnp.zeros((qb, 256), BF16)

        def gradients(slot, kb):
            dz2 = dzbuf[...]
            acc[...] += _dot(dz2, k2_ref[kb])
            dk2 = _dot(dz2, q, "tn")
            dv2 = _dot(wbuf[slot], dob, "tn")
            dk_ref[kb] += jnp.where(lane < 64, dk2[:128], dk2[128:])
            dv_ref[kb] += jnp.where(lane < 64, dv2[:128], dv2[128:])

        def step(kb, _):
            fetch(kb).wait()
            bias = bias_ref[_sb_bias_index(i, kb, per)]
            z2 = zbuf[...]
            dw2 = dwbuf[...]
            nxt = jnp.minimum(kb + 1, nkb - 1)
            zbuf[...] = _dot(q, kt_ref[nxt])
            dwbuf[...] = _dot(dob, vt_ref[nxt])
            gradients(jnp.where(kb == 0, zero_slot, kb - 1), jnp.maximum(kb - 1, 0))
            w2 = wbuf[kb]
            for hh in range(2):
                sl = slice(128 * hh, 128 * (hh + 1))
                z = z2[:, sl] + bias
                e = jnp.exp(-jnp.abs(z))
                r = 1.0 / (1.0 + e)
                sig = jnp.where(z >= 0, r, e * r)
                gw = w2[:, sl].astype(F32) * dw2[:, sl]
                cu2 = _dot(_split2(gw), etab_ref[...])
                gin = gcarry[hh]
                gcarry[hh] = gin + cu2[:, 128:]
                dzbuf[:, sl] = (gw - sig * (gw + cu2[:, :128] + gin)).astype(BF16)
            return 0

        lax.fori_loop(0, nkb, step, 0)
        gradients(nkb - 1, nkb - 1)
        dq_ref[...] = acc[...] * SB_SCALE

    blk = pl.BlockSpec((qb, 128), lambda h, i: (i, h))
    wide = pl.BlockSpec((None, nkb_all, 128, 256), lambda h, i: (h, 0, 0, 0))
    tall = pl.BlockSpec((None, nkb_all, 256, 128), lambda h, i: (h, 0, 0, 0))
    tab = pl.BlockSpec((256, 256), lambda h, i: (0, 0))
    kv_out = pl.BlockSpec((nkb_all, 128, 128), lambda h, i: (0, 0, h))
    ksh = jax.ShapeDtypeStruct((nkb_all, 128, 1024), F32)
    dq, dk, dv = _pcall(
        body, name="sb_bwd", grid=(8, t // qb),
        in_specs=[blk, wide, tall, wide, pl.BlockSpec(memory_space=pl.ANY), blk, tab,
                  pl.BlockSpec((per + 3, qb, 128), lambda h, i: (0, 0, 0))],
        out_specs=[blk, kv_out, kv_out], out_shape=[jax.ShapeDtypeStruct((t, 1024), F32), ksh, ksh],
        scratch_shapes=[pltpu.VMEM((qb, 128), F32), pltpu.VMEM((2, qb, 128), F32),
                        pltpu.VMEM((qb, 256), F32), pltpu.VMEM((qb, 256), F32),
                        pltpu.VMEM((nkb_all + 1, qb, 256), BF16), pltpu.SemaphoreType.DMA((nkb_all,)),
                        pltpu.VMEM((qb, 256), BF16)],
        compiler_params=_params(dimension_semantics=("parallel", "arbitrary")))(
            qh, kt, k2, vt, wsave, do, earlier_tab, bias_tab)
    return dq, dk.reshape(t, 1024), dv.reshape(t, 1024)


def _adamw_math(w, g, m, v):
    m = ADAM_B1 * m + (1.0 - ADAM_B1) * g
    v = ADAM_B2 * v + (1.0 - ADAM_B2) * (g * g)
    m_hat = m / (1.0 - ADAM_B1 ** ADAM_STEP)
    v_hat = v / (1.0 - ADAM_B2 ** ADAM_STEP)
    delta = -ADAM_LR * (m_hat / (jnp.sqrt(v_hat) + ADAM_EPS) + ADAM_WD * w)
    return delta, m, v


def _adamw(name, w, owns, recvs, m, v, me):
    shape = w.shape
    c = shape[-1]
    nl = len(owns)
    w3, m3, v3 = (a.reshape(nl, -1, c) for a in (w, m, v))
    r = w3.shape[1]
    tr = _tile(r, (256, 128))
    owns = [o.reshape(N_DEV, r, c) for o in owns]
    recvs = [p.reshape(N_DEV - 1, r, c) for p in recvs]

    def body(me_ref, w_ref, *rest):
        own_refs, recv_refs = rest[:nl], rest[nl:2 * nl]
        m_ref, v_ref = rest[2 * nl:2 * nl + 2]
        g_out, d_out, m_out, v_out = rest[2 * nl + 2:]
        layer = pl.program_id(0)

        def grad(k):
            g = own_refs[k][...].astype(F32)
            for s in range(N_DEV - 1):
                g = g + recv_refs[k][s].astype(F32)
            return g

        g = grad(0)
        for k in range(1, nl):
            g = jnp.where(layer == k, grad(k), g)
        d, mn, vn = _adamw_math(w_ref[...], g, m_ref[...], v_ref[...])
        g_out[...] = g
        d_out[...] = d
        m_out[...] = mn
        v_out[...] = vn

    row = pl.BlockSpec((None, tr, c), lambda l, i, me_ref: (l, i, 0))
    own = lambda k: pl.BlockSpec((None, tr, c), lambda l, i, me_ref: (me_ref[0], jnp.where(l == k, i, 0), 0))
    rcv = lambda k: pl.BlockSpec((N_DEV - 1, tr, c), lambda l, i, me_ref: (0, jnp.where(l == k, i, 0), 0))
    osh = jax.ShapeDtypeStruct((nl, r, c), F32)
    grid_spec = pltpu.PrefetchScalarGridSpec(
        num_scalar_prefetch=1, grid=(nl, r // tr),
        in_specs=[row] + [own(k) for k in range(nl)] + [rcv(k) for k in range(nl)] + [row, row],
        out_specs=[row, row, row, row])
    outs = _pcall(body, name=name, grid_spec=grid_spec, out_shape=[osh, osh, osh, osh])(
        me.reshape(1), w3, *owns, *recvs, m3, v3)
    return tuple(o.reshape(shape) for o in outs)


def _place():
    x, y, c = lax.axis_index("x"), lax.axis_index("y"), lax.axis_index("c")
    return x, y, c, 4 * x + 2 * y + c


def _peer(x, y, c, rel):
    return (x ^ ((rel >> 2) & 1), y ^ ((rel >> 1) & 1), c ^ (rel & 1))


def _gather_first(now, later):
    n, k = len(now), len(later)

    def body(*refs):
        ins, outs = refs[:n + k], refs[n + k:2 * (n + k)]
        send, recv, lsem = refs[2 * (n + k):]
        x, y, c, me = _place()
        locals_ = []
        for w in range(n + k):
            local = pltpu.make_async_copy(ins[w], outs[w].at[me], lsem.at[w])
            local.start()
            locals_.append(local)
        def copy(w, src, slot, rel, to_rel):
            return pltpu.make_async_remote_copy(src_ref=src, dst_ref=outs[w].at[slot], send_sem=send.at[w, rel - 1],
                                                recv_sem=recv.at[w, rel - 1], device_id=_peer(x, y, c, to_rel),
                                                device_id_type=MESH)

        for w in range(n):
            for rel in (1, 2, 4, 6):
                copy(w, ins[w], me, rel, rel).start()
        for w in range(n):
            for rel in (2, 4, 6):
                copy(w, ins[w], me ^ rel, rel, rel).wait_recv()
                copy(w, outs[w].at[me ^ rel], me ^ rel, rel | 1, 1).start()
        for w in range(n):
            for rel in (1, 3, 5, 7):
                copy(w, ins[w], me ^ rel, rel, 1).wait_recv()
            for rel in range(1, N_DEV):
                copy(w, ins[w], me, rel, rel).wait_send()
        for local in locals_:
            local.wait()

    hbm = pl.BlockSpec(memory_space=pl.ANY)
    vmem = pl.BlockSpec(memory_space=pltpu.VMEM)
    arrays = list(now) + list(later)
    return _pcall(body, name="gather_first", in_specs=[vmem] * (n + k), out_specs=[hbm] * (n + k),
                  out_shape=[jax.ShapeDtypeStruct((N_DEV,) + a.shape, a.dtype) for a in arrays],
                  scratch_shapes=[pltpu.SemaphoreType.DMA((n, N_DEV - 1)), pltpu.SemaphoreType.DMA((n, N_DEV - 1)),
                                  pltpu.SemaphoreType.DMA((n + k,))],
                  compiler_params=_params(has_side_effects=True))(*arrays)


_HBM = pl.BlockSpec(memory_space=pltpu.HBM)
_SEM = pl.BlockSpec(memory_space=pltpu.SEMAPHORE)
_DATAFLOW = pltpu.SideEffectType.DATAFLOW_SIDE_EFFECTING


def _exchange_refs(srcs, lands, mode, me, rel, j):
    if mode == "gather":
        return srcs[j], lands[j].at[me], lands[j].at[me ^ rel]
    return srcs[j].at[me ^ rel], lands[j].at[rel - 1], lands[j].at[rel - 1]


def _exchange_start(name, srcs, lands, mode):
    n = len(srcs)

    def body(*refs):
        ins, lnd = refs[:n], refs[n:2 * n]
        send, recv = refs[2 * n], refs[2 * n + 1]
        token = refs[-1]
        x, y, c, me = _place()
        for j in range(n):
            for rel in range(1, N_DEV):
                src, dst, _ = _exchange_refs(ins, lnd, mode, me, rel, j)
                pltpu.make_async_remote_copy(src_ref=src, dst_ref=dst, send_sem=send.at[j * (N_DEV - 1) + rel - 1],
                                             recv_sem=recv.at[j * (N_DEV - 1) + rel - 1],
                                             device_id=_peer(x, y, c, rel), device_id_type=MESH).start()
        token[...] = jnp.zeros_like(token)

    sems = pltpu.SemaphoreType.DMA((n * (N_DEV - 1),))
    hbm_like = lambda a: pltpu.HBM(a.shape, a.dtype)
    outs = _pcall(body, name=name + "_start",
                  in_specs=[_HBM] * (2 * n), out_specs=[_SEM, _SEM] + [_HBM] * (2 * n) + [pl.BlockSpec(memory_space=pltpu.VMEM)],
                  out_shape=[sems, sems] + [hbm_like(a) for a in srcs] + [hbm_like(a) for a in lands]
                  + [jax.ShapeDtypeStruct((8, 128), F32)],
                  input_output_aliases={i: 2 + i for i in range(2 * n)},
                  compiler_params=pltpu.CompilerParams(has_side_effects=_DATAFLOW))(
                      *[pltpu.with_memory_space_constraint(a, pltpu.HBM) for a in list(srcs) + list(lands)])
    return dict(name=name, mode=mode, n=n, send=outs[0], recv=outs[1], srcs=outs[2:2 + n], lands=outs[2 + n:2 + 2 * n],
                token=outs[-1][0, 0])


def _exchange_wait(ex, after):
    n, mode = ex["n"], ex["mode"]

    def body(*refs):
        ins, lnd = refs[:n], refs[n:2 * n]
        send, recv = refs[2 * n], refs[2 * n + 1]
        x, y, c, me = _place()
        for j in range(n):
            for rel in range(1, N_DEV):
                src, dst, landed = _exchange_refs(ins, lnd, mode, me, rel, j)
                pltpu.make_async_remote_copy(src_ref=src, dst_ref=dst, send_sem=send.at[j * (N_DEV - 1) + rel - 1],
                                             recv_sem=recv.at[j * (N_DEV - 1) + rel - 1],
                                             device_id=_peer(x, y, c, rel), device_id_type=MESH).wait_send()
                pltpu.make_async_remote_copy(src_ref=src, dst_ref=landed, send_sem=send.at[j * (N_DEV - 1) + rel - 1],
                                             recv_sem=recv.at[j * (N_DEV - 1) + rel - 1],
                                             device_id=_peer(x, y, c, rel), device_id_type=MESH).wait_recv()

    hbm_like = lambda a: pltpu.HBM(a.shape, a.dtype)
    arrays = list(ex["srcs"]) + list(ex["lands"])
    outs = _pcall(body, name=ex["name"] + "_wait",
                  in_specs=[_HBM] * (2 * n) + [_SEM, _SEM, pl.BlockSpec(memory_space=pl.ANY)],
                  out_specs=[_HBM] * (2 * n), out_shape=[hbm_like(a) for a in arrays],
                  input_output_aliases={i: i for i in range(2 * n)},
                  compiler_params=pltpu.CompilerParams(has_side_effects=_DATAFLOW))(
                      *arrays, ex["send"], ex["recv"], after)
    return outs[:n], outs[n:]


def _scatter_start(name, grads):
    lands = [lax.empty((N_DEV - 1,) + g.shape[1:], g.dtype) for g in grads]
    return _exchange_start(name, grads, lands, "scatter")


ROW_MIX, ROW_MLP, ROW_CB, ROW_LG, ROW_LB, ROW_QN, ROW_KN, ROW_LOSS = 0, 2, 4, 5, 6, 7, 8, 9
ROW_META, ROW_CW, ROW_GN, SMALL_ROWS = 16, 32, 64, 72


def _sum_small(slots):
    def body(s_ref, o_ref):
        tot = s_ref[0]
        for s in range(1, N_DEV):
            tot = tot + s_ref[s]
        o_ref[...] = tot
        for row in (ROW_QN, ROW_KN):
            v = tot[row:row + 1, :]
            f = v[:, 0:128]
            for k in range(1, 8):
                f = f + v[:, 128 * k:128 * (k + 1)]
            o_ref[row:row + 1, 0:64] = f[:, 0:64] + f[:, 64:128]

    return _pcall(body, name="sum_small", out_shape=jax.ShapeDtypeStruct(slots.shape[1:], F32))(slots)


def _adamw_small(w, g, m, v):
    def body(w_ref, g_ref, m_ref, v_ref, d_out, m_out, v_out):
        d, mn, vn = _adamw_math(w_ref[...], g_ref[...], m_ref[...], v_ref[...])
        d_out[...] = d
        m_out[...] = mn
        v_out[...] = vn

    osh = jax.ShapeDtypeStruct(w.shape, F32)
    return _pcall(body, name="adamw_small", out_shape=[osh, osh, osh])(w, g, m, v)


def _local_step(h0, target, p, weight, emit):
    t = h0.shape[0]
    tables = _ret_tables(t)
    bd, later_tab, earlier_tab, bias_tab = _seg_tables(_sb_qb(t))
    row = lambda a, i: a[i:i + 1]

    hn_a = _rms_fwd("rms_mix0", h0, row(p["norm_mix_g"], 0))
    w_in = weight("w_in", hn_a)
    proj = _mm_cols("proj_in", hn_a, w_in, ())
    o_ret, states = _ret_fwd(proj, tables)
    gn_flat = p["gn_g"].reshape(1, 1024)
    cat = _gn_gate_fwd(o_ret, proj, gn_flat)
    cat, hdn, ycv = _conv_fwd(cat, proj, p["conv_w"], p["conv_b"], p["ln_g"], p["ln_b"])
    w_out = weight("w_out", cat)
    h1 = _mm_rows("mix_out", cat, w_out, h0)
    hn_b = _rms_fwd("rms_mlp0", h1, row(p["norm_mlp_g"], 0))
    w1_0, w2_0 = weight("w1_0", hn_b), weight("w2_0", hn_b)
    a0, s0 = _mm_cols("mlp0_up", hn_b, w1_0, (), epi="relu2")
    h2 = _mm_rows("mlp0_down", s0, w2_0, h1)

    hn_c = _rms_fwd("rms_mix1", h2, row(p["norm_mix_g"], 1))
    w_qkv = weight("w_qkv", hn_c)
    qkv = _mm_cols("qkv", hn_c, w_qkv, ())
    qg = jnp.tile(p["qn_g"], (1, 16))
    kg = jnp.tile(p["kn_g"], (1, 16))
    qh, kt, k2, vt, v2 = _qk_norm_fwd(qkv, qg, kg, bd)
    o_sb, w_sb = _sb_fwd(qh, kt, v2, later_tab, bias_tab)
    w_o = weight("w_o", o_sb)
    h3 = _mm_rows("attn_out", o_sb, w_o, h2)
    hn_d = _rms_fwd("rms_mlp1", h3, row(p["norm_mlp_g"], 1))
    w1_1, w2_1 = weight("w1_1", hn_d), weight("w2_1", hn_d)
    a1, s1 = _mm_cols("mlp1_up", hn_d, w1_1, (), epi="relu2")
    h4 = _mm_rows("mlp1_down", s1, w2_1, h3)

    dh, loss = _loss_bwd(h4, target)

    def mlp_bwd(tag, layer, w1, w2, dh, h_in, hn, a, s):
        da = _mm_rows_t(f"{tag}_dact", dh, w2, (), out_dtype=BF16, epi="drelu2", extra=a)
        dw2 = _wgrad_rows(f"{tag}_dw2", s, dh, 512)
        dw1 = _wgrad_cols(f"{tag}_dw1", hn, da, 512)
        tok = emit(tag, [dw1, dw2])
        dhn = _mm_cols_t(f"{tag}_dhn", da, w1)
        return _rms_bwd(f"{tag}_rms_bwd", dhn, h_in, row(p["norm_mlp_g"], layer) + tok, dh)

    dh, dg_mlp1 = mlp_bwd("mlp1", 1, w1_1, w2_1, dh, h3, hn_d, a1, s1)

    do_sb = _mm_rows_t("attn_dout", dh, w_o, ())
    dw_o = _wgrad_rows("attn_dwo", o_sb, dh, 128)
    dq, dk, dv = _sb_bwd(qh, kt, k2, vt, w_sb, do_sb, earlier_tab, bias_tab)
    dqkv, dqg, dkg = _qk_norm_bwd(qkv, dq, dk, dv, qg, kg, bd)
    dw_qkv = _wgrad_cols("qkv_dw", hn_c, dqkv, 384)
    tok = emit("attn", [dw_qkv, dw_o])
    dhn = _mm_cols_t("qkv_dhn", dqkv, w_qkv)
    dh, dg_mix1 = _rms_bwd("mix1_rms_bwd", dhn, h2, row(p["norm_mix_g"], 1) + tok, dh)

    dh, dg_mlp0 = mlp_bwd("mlp0", 0, w1_0, w2_0, dh, h1, hn_b, a0, s0)

    dcat = _mm_rows_t("mix_dcat", dh, w_out, ())
    dw_out = _wgrad_rows("mix_dwout", cat, dh, 256)
    tok = emit("mix0_out", [dw_out])
    do_ret, dproj, dgn = _gn_gate_bwd(dcat, o_ret, proj, gn_flat + tok)
    dproj = _ret_bwd(dproj, proj, states, do_ret, tables)
    dy, dlg, dlb, dcb = _conv_bwd_ln(dcat, ycv, p["ln_g"], p["ln_b"])
    dproj, dug, dcw = _conv_bwd_taps(dproj, dy, hdn, proj, p["conv_w"])
    dproj = lax.dynamic_update_slice(dproj, dug, (0, 4096))
    dw_in = _wgrad_cols("proj_dw", hn_a, dproj, 640)
    tok = emit("mix0", [dw_in])
    dhn = _mm_cols_t("proj_dhn", dproj, w_in)
    dh, dg_mix0 = _rms_bwd("mix0_rms_bwd", dhn, h0, row(p["norm_mix_g"], 0) + tok, dh)

    rid = lax.broadcasted_iota(jnp.int32, (16, 1), 0)
    loss_row = jnp.broadcast_to(loss[0:1, 0:1], (1, D_MODEL))
    vecs = sum(jnp.where(rid == k, v, 0.0)
               for k, v in enumerate((dg_mix0, dg_mix1, dg_mlp0, dg_mlp1, dcb, dlg, dlb, dqg, dkg, loss_row)))
    small = jnp.concatenate([vecs, dh[PAD_FRONT:TOK0], dcw, jnp.where(rid[:8] == 0, dgn, 0.0)], axis=0)
    return dh[TOK0:], small


_SMALL_NAMES = ("meta", "norm_mix_g", "norm_mlp_g", "even_ret_gn_g", "even_conv_w", "even_conv_b",
                "even_conv_ln_g", "even_conv_ln_b", "odd_q_norm_g", "odd_k_norm_g")
_BIG_NAMES = ("even_w_in", "even_w_out", "odd_w_qkv", "odd_w_o", "mlp_w1", "mlp_w2")
_ORDER = ("meta", "norm_mix_g", "norm_mlp_g", "even_w_in", "even_ret_gn_g", "even_conv_w", "even_conv_b",
          "even_conv_ln_g", "even_conv_ln_b", "even_w_out", "odd_w_qkv", "odd_q_norm_g", "odd_k_norm_g",
          "odd_w_o", "mlp_w1", "mlp_w2")


def _pack128(a):
    flat = a.reshape(-1)
    n = flat.shape[0]
    rows = -(-n // 128)
    rows8 = -(-rows // 8) * 8
    return jnp.pad(flat, (0, rows8 * 128 - n)).reshape(rows8, 128)


def kernel(x, meta, norm_mix_g, norm_mlp_g, even_w_in, even_ret_gn_g, even_conv_w, even_conv_b, even_conv_ln_g, even_conv_ln_b, even_w_out, odd_w_qkv, odd_q_norm_g, odd_k_norm_g, odd_w_o, mlp_w1, mlp_w2, loss_target, m_meta, m_norm_mix_g, m_norm_mlp_g, m_even_w_in, m_even_ret_gn_g, m_even_conv_w, m_even_conv_b, m_even_conv_ln_g, m_even_conv_ln_b, m_even_w_out, m_odd_w_qkv, m_odd_q_norm_g, m_odd_k_norm_g, m_odd_w_o, m_mlp_w1, m_mlp_w2, v_meta, v_norm_mix_g, v_norm_mlp_g, v_even_w_in, v_even_ret_gn_g, v_even_conv_w, v_even_conv_b, v_even_conv_ln_g, v_even_conv_ln_b, v_even_w_out, v_odd_w_qkv, v_odd_q_norm_g, v_odd_k_norm_g, v_odd_w_o, v_mlp_w1, v_mlp_w2):
    w = dict(meta=meta, norm_mix_g=norm_mix_g, norm_mlp_g=norm_mlp_g, even_w_in=even_w_in,
             even_ret_gn_g=even_ret_gn_g, even_conv_w=even_conv_w, even_conv_b=even_conv_b,
             even_conv_ln_g=even_conv_ln_g, even_conv_ln_b=even_conv_ln_b, even_w_out=even_w_out,
             odd_w_qkv=odd_w_qkv, odd_q_norm_g=odd_q_norm_g, odd_k_norm_g=odd_k_norm_g, odd_w_o=odd_w_o,
             mlp_w1=mlp_w1, mlp_w2=mlp_w2)
    mom = dict(meta=m_meta, norm_mix_g=m_norm_mix_g, norm_mlp_g=m_norm_mlp_g, even_w_in=m_even_w_in,
               even_ret_gn_g=m_even_ret_gn_g, even_conv_w=m_even_conv_w, even_conv_b=m_even_conv_b,
               even_conv_ln_g=m_even_conv_ln_g, even_conv_ln_b=m_even_conv_ln_b, even_w_out=m_even_w_out,
               odd_w_qkv=m_odd_w_qkv, odd_q_norm_g=m_odd_q_norm_g, odd_k_norm_g=m_odd_k_norm_g, odd_w_o=m_odd_w_o,
               mlp_w1=m_mlp_w1, mlp_w2=m_mlp_w2)
    var = dict(meta=v_meta, norm_mix_g=v_norm_mix_g, norm_mlp_g=v_norm_mlp_g, even_w_in=v_even_w_in,
               even_ret_gn_g=v_even_ret_gn_g, even_conv_w=v_even_conv_w, even_conv_b=v_even_conv_b,
               even_conv_ln_g=v_even_conv_ln_g, even_conv_ln_b=v_even_conv_ln_b, even_w_out=v_even_w_out,
               odd_w_qkv=v_odd_w_qkv, odd_q_norm_g=v_odd_q_norm_g, odd_k_norm_g=v_odd_k_norm_g, odd_w_o=v_odd_w_o,
               mlp_w1=v_mlp_w1, mlp_w2=v_mlp_w2)
    me = 4 * lax.axis_index("x") + 2 * lax.axis_index("y") + lax.axis_index("c")

    small_in = jnp.concatenate([meta, jnp.pad(even_conv_w[0], ((0, 1), (0, 0))),
                                jnp.pad(even_ret_gn_g[0], ((0, 4), (0, 96)))], axis=0)
    b16 = lambda a: a.astype(BF16)
    later_src = dict(w_out=b16(even_w_out[0]), w1_0=b16(mlp_w1[0]), w2_0=b16(mlp_w2[0]),
                     w_qkv=b16(odd_w_qkv[0]), w_o=b16(odd_w_o[0]), w1_1=b16(mlp_w1[1]), w2_1=b16(mlp_w2[1]))
    landed = _gather_first([b16(even_w_in[0]), small_in], list(later_src.values()))
    g_in, g_small = landed[0], landed[1]
    own_slot = dict(zip(later_src, landed[2:]))
    groups = (("gather_l0", ("w_out", "w1_0", "w2_0")), ("gather_attn", ("w_qkv", "w_o")),
              ("gather_l1", ("w1_1", "w2_1")))
    pending = {}
    gather_tok = jnp.zeros((), F32)
    for gname, names in groups:
        ex = _exchange_start(gname, [later_src[n] for n in names], [own_slot[n] for n in names], "gather")
        gather_tok = gather_tok + ex["token"]
        for n in names:
            pending[n] = (ex, names)
    arrived = dict(w_in=g_in)

    def weight(name, after):
        if name not in arrived:
            ex, names = pending[name]
            arrived.update(zip(names, _exchange_wait(ex, after)[1]))
        return arrived[name]

    cols = lambda a: jnp.transpose(a, (1, 0, 2)).reshape(a.shape[1], -1)
    p = dict(norm_mix_g=norm_mix_g + gather_tok, norm_mlp_g=norm_mlp_g, conv_b=even_conv_b, ln_g=even_conv_ln_g,
             ln_b=even_conv_ln_b, qn_g=odd_q_norm_g, kn_g=odd_k_norm_g,
             gn_g=cols(g_small[:, 48:52, :32]),
             conv_w=jnp.pad(cols(g_small[:, 16:47]), ((0, 1), (0, 0))))
    meta_full = cols(g_small[:, 0:16])

    scatters = {}

    def emit(tag, grads):
        scatters[tag] = _scatter_start("scatter_" + tag, grads)
        return scatters[tag]["token"]

    h0 = jnp.concatenate([jnp.zeros((PAD_FRONT, D_MODEL), F32), meta_full, x[0]], axis=0)
    grad_x, small_part = _local_step(h0, loss_target[0], p, weight, emit)

    out = {}
    got = {}

    def update(names, terms, after):
        for tag in {t for name in names for t, _ in terms[name]} - set(got):
            got[tag] = _exchange_wait(scatters[tag], after)
        for name in names:
            owns, recvs = zip(*[(got[t][0][j], got[t][1][j]) for t, j in terms[name]])
            out[name] = _adamw("adamw_" + name, w[name], list(owns), list(recvs), mom[name], var[name], me)

    terms = dict(even_w_in=[("mix0", 0)], even_w_out=[("mix0_out", 0)], odd_w_qkv=[("attn", 0)], odd_w_o=[("attn", 1)],
                 mlp_w1=[("mlp0", 0), ("mlp1", 0)], mlp_w2=[("mlp0", 1), ("mlp1", 1)])
    small_ex = _exchange_start("small", [small_part], [lax.empty((N_DEV,) + small_part.shape, F32)], "gather")
    update(("mlp_w1", "mlp_w2", "odd_w_qkv", "odd_w_o", "even_w_out"), terms, grad_x)
    update(("even_w_in",), terms, out["even_w_out"][1])
    (own_part,), (slots,) = _exchange_wait(small_ex, out["even_w_in"][1])
    tot = _sum_small(lax.dynamic_update_slice(slots, own_part[None], (me, 0, 0)))
    loss = tot[ROW_LOSS, 0]

    shard_cols = lambda a, width: lax.dynamic_slice_in_dim(a, me * width, width, axis=1)
    one = lambda r: tot[r:r + 1]
    small_g = dict(
        norm_mix_g=tot[ROW_MIX:ROW_MIX + 2], norm_mlp_g=tot[ROW_MLP:ROW_MLP + 2],
        even_conv_b=one(ROW_CB), even_conv_ln_g=one(ROW_LG), even_conv_ln_b=one(ROW_LB),
        odd_q_norm_g=one(ROW_QN)[:, :64], odd_k_norm_g=one(ROW_KN)[:, :64],
        meta=shard_cols(tot[ROW_META:ROW_META + N_META], 128),
        even_conv_w=shard_cols(tot[ROW_CW:ROW_CW + CONV_WIDTH], 128)[None],
        even_ret_gn_g=shard_cols(tot[ROW_GN].reshape(4, 256), 32)[None])
    packs = {n: (_pack128(w[n]), _pack128(small_g[n]), _pack128(mom[n]), _pack128(var[n])) for n in _SMALL_NAMES}
    cat4 = [jnp.concatenate([packs[n][i] for n in _SMALL_NAMES], axis=0) for i in range(4)]
    d_s, m_s, v_s = _adamw_small(*cat4)
    r0 = 0
    for n in _SMALL_NAMES:
        rows = packs[n][0].shape[0]
        size = w[n].size
        take = lambda a: a[r0:r0 + rows].reshape(-1)[:size].reshape(w[n].shape)
        out[n] = (small_g[n].reshape(w[n].shape), take(d_s), take(m_s), take(v_s))
        r0 += rows

    res = [loss, grad_x[None]]
    for i in range(4):
        res.extend(out[n][i] for n in _ORDER)
    return tuple(res)
```

```python
import functools

import numpy as np
import jax
import jax.numpy as jnp
from jax import lax
from jax.experimental import pallas as pl
from jax.experimental.pallas import tpu as pltpu

F32 = jnp.float32
BF16 = jnp.bfloat16

D_MODEL = 1024
N_META = 16
CHUNK = 128
PAD_FRONT = 112
TOK0 = PAD_FRONT + N_META
EPS = 1e-6
N_DEV = 8
RET_HEADS = 4
RET_DECAY_OFFSET = 5.0
ROPE_BASE = 10000.0
CONV_WIDTH = 31
HALO = 32
SB_SCALE = 64 ** -0.5
RET_SCALE = 128 ** -0.5
ADAM_LR, ADAM_B1, ADAM_B2, ADAM_EPS, ADAM_WD, ADAM_STEP = 0.001, 0.9, 0.999, 1e-08, 0.01, 10
VMEM_LIMIT = 56 * 1024 * 1024
MESH = pl.DeviceIdType.MESH


def _pcall(body, **kw):
    return pl.pallas_call(body, **kw)


def _params(**kw):
    return pltpu.CompilerParams(vmem_limit_bytes=VMEM_LIMIT, **kw)


def _tile(n, cands):
    for c in cands:
        if n % c == 0:
            return c
    raise ValueError(f"no tile for {n} in {cands}")


def _sigmoid(x):
    return 1.0 / (1.0 + jnp.exp(-x))


_DIMS = {
    "nn": (((1,), (0,)), ((), ())),
    "nt": (((1,), (1,)), ((), ())),
    "tn": (((0,), (0,)), ((), ())),
}


def _matmul(name, a, b, *, grid, a_spec, b_spec, o_spec, out_shape, contract, acc_shape,
            epi="plain", extra=None, extra_spec=None):
    nk = grid[2]
    dims = _DIMS[contract]
    n_in = 3 if extra is not None else 2
    n_out = 2 if epi == "relu2" else 1

    def body(*refs):
        a_ref, b_ref = refs[0], refs[1]
        e_ref = refs[2] if extra is not None else None
        outs = refs[n_in:n_in + n_out]
        acc = refs[-1]
        k = pl.program_id(2)
        part = lax.dot_general(a_ref[...].astype(BF16), b_ref[...].astype(BF16), dims, preferred_element_type=F32)
        if nk > 1:
            @pl.when(k == 0)
            def _():
                acc[...] = jnp.zeros_like(acc)

            acc[...] += part

        @pl.when(k == nk - 1)
        def _():
            r = acc[...] if nk > 1 else part
            if epi == "plain":
                outs[0][...] = r.astype(outs[0].dtype)
            elif epi == "residual":
                outs[0][...] = (r + e_ref[...]).astype(outs[0].dtype)
            elif epi == "relu2":
                outs[0][...] = r
                rr = jnp.maximum(r, 0.0)
                outs[1][...] = (rr * rr).astype(BF16)
            elif epi == "drelu2":
                outs[0][...] = (r * (2.0 * jnp.maximum(e_ref[...], 0.0))).astype(outs[0].dtype)

    in_specs = [a_spec, b_spec] + ([extra_spec] if extra is not None else [])
    args = (a, b) + ((extra,) if extra is not None else ())
    if n_out == 2:
        out_specs = [o_spec, o_spec]
    else:
        out_specs = o_spec
    return _pcall(body, name=name, grid=grid, in_specs=in_specs, out_specs=out_specs,
                  out_shape=out_shape, scratch_shapes=[pltpu.VMEM(acc_shape, F32)],
                  compiler_params=_params(dimension_semantics=("parallel", "parallel", "arbitrary")))(*args)


def _tm(t):
    return _tile(t, (1408, 768, 384, 128))


def _mm_cols(name, a, wb, lead, out_dtype=F32, epi="plain"):
    t, kdim = a.shape
    n = wb.shape[-1]
    tm, tk = _tm(t), _tile(kdim, (1024, 512))
    nl = len(lead)
    b_spec = pl.BlockSpec((None,) * (1 + nl) + (tk, n), lambda i, j, k: (j,) + lead + (k, 0))
    o_spec = pl.BlockSpec((tm, n), lambda i, j, k: (i, j))
    if epi == "relu2":
        out_shape = [jax.ShapeDtypeStruct((t, N_DEV * n), F32), jax.ShapeDtypeStruct((t, N_DEV * n), BF16)]
    else:
        out_shape = jax.ShapeDtypeStruct((t, N_DEV * n), out_dtype)
    return _matmul(name, a, wb, grid=(t // tm, N_DEV, kdim // tk),
                   a_spec=pl.BlockSpec((tm, tk), lambda i, j, k: (i, k)), b_spec=b_spec, o_spec=o_spec,
                   out_shape=out_shape, contract="nn", acc_shape=(tm, n), epi=epi)


def _tm_deep(t, kdim):
    return _tm(t) if kdim <= 2048 else _tile(t, (704, 384, 128))


def _mm_cols_t(name, a, wb):
    t = a.shape[0]
    nb, kdim, n = wb.shape
    tm, tn = _tm_deep(t, nb * n), _tile(kdim, (512,))

    def body(a_ref, b_ref, o_ref):
        acc = _dot(a_ref[:, 0:n].astype(BF16), b_ref[0], "nt")
        for j in range(1, nb):
            acc = acc + _dot(a_ref[:, j * n:(j + 1) * n].astype(BF16), b_ref[j], "nt")
        o_ref[...] = acc

    return _pcall(body, name=name, grid=(t // tm, kdim // tn),
                  in_specs=[pl.BlockSpec((tm, nb * n), lambda i, j: (i, 0)),
                            pl.BlockSpec((nb, tn, n), lambda i, j: (0, j, 0))],
                  out_specs=pl.BlockSpec((tm, tn), lambda i, j: (i, j)),
                  out_shape=jax.ShapeDtypeStruct((t, kdim), F32),
                  compiler_params=_params(dimension_semantics=("parallel", "parallel")))(a, wb)


def _mm_rows(name, a, wb, residual):
    t = a.shape[0]
    nb, r, n = wb.shape
    tm, tn = _tm_deep(t, nb * r), _tile(n, (512,))

    def body(a_ref, b_ref, r_ref, o_ref):
        o_ref[...] = r_ref[...] + _dot(a_ref[...].astype(BF16), b_ref[...].reshape(nb * r, tn))

    o_spec = pl.BlockSpec((tm, tn), lambda i, j: (i, j))
    return _pcall(body, name=name, grid=(t // tm, n // tn),
                  in_specs=[pl.BlockSpec((tm, nb * r), lambda i, j: (i, 0)),
                            pl.BlockSpec((nb, r, tn), lambda i, j: (0, 0, j)), o_spec],
                  out_specs=o_spec, out_shape=jax.ShapeDtypeStruct((t, n), F32),
                  compiler_params=_params(dimension_semantics=("parallel", "parallel")))(a, wb, residual)


def _mm_rows_t(name, a, wb, lead, out_dtype=F32, epi="plain", extra=None):
    t, n = a.shape
    r = wb.shape[-2]
    tm, tk = _tm(t), _tile(n, (1024,))
    nl = len(lead)
    b_spec = pl.BlockSpec((None,) * (1 + nl) + (r, tk), lambda i, j, k: (j,) + lead + (0, k))
    o_spec = pl.BlockSpec((tm, r), lambda i, j, k: (i, j))
    return _matmul(name, a, wb, grid=(t // tm, N_DEV, n // tk),
                   a_spec=pl.BlockSpec((tm, tk), lambda i, j, k: (i, k)), b_spec=b_spec, o_spec=o_spec,
                   out_shape=jax.ShapeDtypeStruct((t, N_DEV * r), out_dtype), contract="nt",
                   acc_shape=(tm, r), epi=epi, extra=extra, extra_spec=o_spec if extra is not None else None)


def _mm_rows_t_rms(name, dhn, h, g, dres, wb, out_dtype=F32, relu_of=None):
    t, n = h.shape
    nb, r, _ = wb.shape
    tm = _tile(t, (704, 384, 128))

    def body(*refs):
        d_ref, h_ref, g_ref, r_ref, b_ref = refs[:5]
        e_ref = refs[5] if relu_of is not None else None
        dh_ref, dg_ref, o_ref, dhb = refs[-4:]
        i, j = pl.program_id(0), pl.program_id(1)

        @pl.when((i == 0) & (j == 0))
        def _():
            dg_ref[...] = jnp.zeros_like(dg_ref)

        @pl.when(j == 0)
        def _():
            x = h_ref[...]
            d = d_ref[...]
            rs = lax.rsqrt(jnp.mean(x * x, axis=-1, keepdims=True) + EPS)
            u = d * g_ref[...]
            m = jnp.mean(u * x, axis=-1, keepdims=True)
            dh = r_ref[...] + rs * u - x * (rs * rs * rs * m)
            dh_ref[...] = dh
            dhb[...] = dh.astype(BF16)
            dg_ref[...] += jnp.sum(d * x * rs, axis=0, keepdims=True)

        acc = _dot(dhb[...], b_ref[...], "nt")
        if relu_of is not None:
            acc = acc * (2.0 * jnp.maximum(e_ref[...], 0.0))
        o_ref[...] = acc.astype(out_dtype)

    row = pl.BlockSpec((tm, n), lambda i, j: (i, 0))
    vec = pl.BlockSpec((1, n), lambda i, j: (0, 0))
    o_spec = pl.BlockSpec((tm, r), lambda i, j: (i, j))
    extra = [relu_of] if relu_of is not None else []
    return _pcall(body, name=name, grid=(t // tm, nb),
                  in_specs=[row, row, vec, row, pl.BlockSpec((None, r, n), lambda i, j: (j, 0, 0))]
                  + [o_spec] * len(extra),
                  out_specs=[row, vec, o_spec],
                  out_shape=[jax.ShapeDtypeStruct((t, n), F32), jax.ShapeDtypeStruct((1, n), F32),
                             jax.ShapeDtypeStruct((t, nb * r), out_dtype)],
                  scratch_shapes=[pltpu.VMEM((tm, n), BF16)],
                  compiler_params=_params(dimension_semantics=("arbitrary", "arbitrary")))(
                      dhn, h, g, dres, wb, *extra)


def _mm_rows_norm(name, a, wb, residual, g):
    t = a.shape[0]
    nb, r, n = wb.shape
    tm = _tile(t, (704, 384, 128))

    def body(a_ref, b_ref, r_ref, g_ref, h_ref, hn_ref):
        h = r_ref[...] + _dot(a_ref[...].astype(BF16), b_ref[...].reshape(nb * r, n))
        h_ref[...] = h
        hn_ref[...] = (h * lax.rsqrt(jnp.mean(h * h, axis=-1, keepdims=True) + EPS) * g_ref[...]).astype(BF16)

    row = pl.BlockSpec((tm, n), lambda i: (i, 0))
    return _pcall(body, name=name, grid=(t // tm,),
                  in_specs=[pl.BlockSpec((tm, nb * r), lambda i: (i, 0)), pl.BlockSpec((nb, r, n), lambda i: (0, 0, 0)),
                            row, pl.BlockSpec((1, n), lambda i: (0, 0))],
                  out_specs=[row, row],
                  out_shape=[jax.ShapeDtypeStruct((t, n), F32), jax.ShapeDtypeStruct((t, n), BF16)],
                  compiler_params=_params(dimension_semantics=("parallel",)))(a, wb, residual, g)


def _wgrad_cols(name, x, dy, n):
    t, kdim = x.shape
    tk = _tm(t)
    return _matmul(name, x, dy, grid=(1, N_DEV, t // tk),
                   a_spec=pl.BlockSpec((tk, kdim), lambda i, j, k: (k, 0)),
                   b_spec=pl.BlockSpec((tk, n), lambda i, j, k: (k, j)),
                   o_spec=pl.BlockSpec((None, kdim, n), lambda i, j, k: (j, 0, 0)),
                   out_shape=jax.ShapeDtypeStruct((N_DEV, kdim, n), BF16), contract="tn", acc_shape=(kdim, n))


def _wgrad_rows(name, x, dy, r):
    t = x.shape[0]
    n = dy.shape[1]
    tk, tn = _tm(t), _tile(n, (512,))
    tm = min(N_DEV * r, 1024)
    out = _matmul(name, x, dy, grid=(N_DEV * r // tm, n // tn, t // tk),
                  a_spec=pl.BlockSpec((tk, tm), lambda i, j, k: (k, i)),
                  b_spec=pl.BlockSpec((tk, tn), lambda i, j, k: (k, j)),
                  o_spec=pl.BlockSpec((tm, tn), lambda i, j, k: (i, j)),
                  out_shape=jax.ShapeDtypeStruct((N_DEV * r, n), BF16), contract="tn", acc_shape=(tm, tn))
    return out.reshape(N_DEV, r, n)


def _rows(t):
    return _tile(t, (384, 128))


def _rms_fwd(name, h, g):
    t = h.shape[0]
    tr = _rows(t)

    def body(h_ref, g_ref, o_ref):
        x = h_ref[...]
        r = lax.rsqrt(jnp.mean(x * x, axis=-1, keepdims=True) + EPS)
        o_ref[...] = (x * r * g_ref[...]).astype(BF16)

    row = pl.BlockSpec((tr, D_MODEL), lambda i: (i, 0))
    vec = pl.BlockSpec((1, D_MODEL), lambda i: (0, 0))
    return _pcall(body, name=name, grid=(t // tr,), in_specs=[row, vec], out_specs=row,
                  out_shape=jax.ShapeDtypeStruct((t, D_MODEL), BF16))(h, g)


def _rms_bwd(name, dhn, h, g, dres):
    t = h.shape[0]
    tr = _rows(t)

    def body(d_ref, h_ref, g_ref, r_ref, o_ref, dg_ref):
        @pl.when(pl.program_id(0) == 0)
        def _():
            dg_ref[...] = jnp.zeros_like(dg_ref)

        x = h_ref[...]
        d = d_ref[...]
        r = lax.rsqrt(jnp.mean(x * x, axis=-1, keepdims=True) + EPS)
        u = d * g_ref[...]
        m = jnp.mean(u * x, axis=-1, keepdims=True)
        o_ref[...] = r_ref[...] + r * u - x * (r * r * r * m)
        dg_ref[...] += jnp.sum(d * x * r, axis=0, keepdims=True)

    row = pl.BlockSpec((tr, D_MODEL), lambda i: (i, 0))
    vec = pl.BlockSpec((1, D_MODEL), lambda i: (0, 0))
    return _pcall(body, name=name, grid=(t // tr,), in_specs=[row, row, vec, row], out_specs=[row, vec],
                  out_shape=[jax.ShapeDtypeStruct((t, D_MODEL), F32), jax.ShapeDtypeStruct((1, D_MODEL), F32)])(
                      dhn, h, g, dres)


def _loss_bwd(h, target):
    t = h.shape[0]
    nb = t // CHUNK

    def body(h_ref, t_ref, d_ref, l_ref):
        i = pl.program_id(0)

        @pl.when(i == 0)
        def _():
            d_ref[...] = jnp.zeros_like(d_ref)
            l_ref[...] = jnp.zeros_like(l_ref)

        @pl.when(i > 0)
        def _():
            diff = h_ref[...] - t_ref[...]
            d_ref[...] = diff * (1.0 / D_MODEL)
            l_ref[...] += jnp.sum(diff * diff) * (0.5 / D_MODEL)

    return _pcall(body, name="loss_bwd", grid=(nb,),
                  in_specs=[pl.BlockSpec((CHUNK, D_MODEL), lambda i: (i, 0)),
                            pl.BlockSpec((CHUNK, D_MODEL), lambda i: (jnp.maximum(i - 1, 0), 0))],
                  out_specs=[pl.BlockSpec((CHUNK, D_MODEL), lambda i: (i, 0)),
                             pl.BlockSpec((8, 128), lambda i: (0, 0))],
                  out_shape=[jax.ShapeDtypeStruct((t, D_MODEL), F32), jax.ShapeDtypeStruct((8, 128), F32)])(h, target)


def _ret_tables(t):
    hh = np.arange(RET_HEADS, dtype=np.float64)
    log_g = np.log1p(-np.exp2(-RET_DECAY_OFFSET - hh))
    idx = np.arange(CHUNK, dtype=np.float64)
    diff = idx[:, None] - idx[None, :]
    dmat = np.where(diff[None] >= 0, np.exp(np.maximum(diff, 0.0)[None] * log_g[:, None, None]), 0.0)
    qdec = np.exp((idx + 1.0)[None, :, None] * log_g[:, None, None]) * np.ones((1, 1, CHUNK))
    kdec = np.exp((CHUNK - 1 - idx)[None, :, None] * log_g[:, None, None]) * np.ones((1, 1, CHUNK))
    half = CHUNK // 2
    inv_freq = (ROPE_BASE ** (-np.arange(half, dtype=np.float32) / half)).astype(np.float32)
    ang = (np.arange(t, dtype=np.float32)[:, None] * inv_freq[None, :]).astype(np.float32).astype(np.float64)
    cos2 = np.concatenate([np.cos(ang), np.cos(ang)], axis=1)
    sin2 = np.concatenate([-np.sin(ang), np.sin(ang)], axis=1)
    return tuple(jnp.asarray(v, F32) for v in (dmat, qdec, kdec, cos2, sin2))


def _rot(x, c, s):
    return x * c + pltpu.roll(x, CHUNK // 2, 1) * s


def _unrot(dx, c, s):
    return dx * c + pltpu.roll(dx * s, CHUNK // 2, 1)


def _dot(a, b, contract="nn"):
    return lax.dot_general(a, b, _DIMS[contract], preferred_element_type=F32)


def _ret_fwd(proj, tables):
    t = proj.shape[0]
    nch = t // CHUNK
    dmat, qdec, kdec, cos2, sin2 = tables

    def body(qk_ref, v_ref, c_ref, s_ref, dm_ref, qd_ref, kd_ref, o_ref, st_ref, state):
        @pl.when(pl.program_id(0) == 0)
        def _():
            state[...] = jnp.zeros_like(state)

        c, s = c_ref[...], s_ref[...]
        for h in range(RET_HEADS):
            q = _rot(qk_ref[:, 128 * h:128 * (h + 1)], c, s)
            k = _rot(qk_ref[:, 512 + 128 * h:512 + 128 * (h + 1)], c, s) * RET_SCALE
            vb = v_ref[:, 256 * h:256 * (h + 1)].astype(BF16)
            st = state[h]
            st_ref[h] = st
            sc = _dot(q.astype(BF16), k.astype(BF16), "nt") * dm_ref[h]
            o = _dot(sc.astype(BF16), vb)
            o += _dot((q * qd_ref[h]).astype(BF16), st.astype(BF16))
            o_ref[:, 256 * h:256 * (h + 1)] = o
            kv = _dot((k * kd_ref[h]).astype(BF16), vb, "tn")
            state[h] = qd_ref[h, CHUNK - 1:CHUNK, 0:1] * st + kv

    tab = pl.BlockSpec((RET_HEADS, CHUNK, CHUNK), lambda n: (0, 0, 0))
    pos = pl.BlockSpec((CHUNK, CHUNK), lambda n: (n, 0))
    return _pcall(
        body, name="ret_fwd", grid=(nch,),
        in_specs=[pl.BlockSpec((CHUNK, 1024), lambda n: (n, 0)), pl.BlockSpec((CHUNK, 1024), lambda n: (n, 1)),
                  pos, pos, tab, tab, tab],
        out_specs=[pl.BlockSpec((CHUNK, 1024), lambda n: (n, 0)),
                   pl.BlockSpec((RET_HEADS, None, 128, 256), lambda n: (0, n, 0, 0))],
        out_shape=[jax.ShapeDtypeStruct((t, 1024), F32), jax.ShapeDtypeStruct((RET_HEADS, nch, 128, 256), F32)],
        scratch_shapes=[pltpu.VMEM((RET_HEADS, 128, 256), F32)],
        compiler_params=_params(dimension_semantics=("arbitrary",)))(
            proj, proj, cos2, sin2, dmat, qdec, kdec)


def _ret_bwd(dproj, proj, states, do, tables):
    t = proj.shape[0]
    nch = t // CHUNK
    dmat, qdec, kdec, cos2, sin2 = tables

    def body(dp_in, qk_ref, v_ref, do_ref, st_ref, c_ref, s_ref, dm_ref, qd_ref, kd_ref, dp_ref, rst):
        del dp_in
        @pl.when(pl.program_id(0) == 0)
        def _():
            rst[...] = jnp.zeros_like(rst)

        c, s = c_ref[...], s_ref[...]
        for h in range(RET_HEADS):
            q = _rot(qk_ref[:, 128 * h:128 * (h + 1)], c, s)
            k = _rot(qk_ref[:, 512 + 128 * h:512 + 128 * (h + 1)], c, s) * RET_SCALE
            qb, kb = q.astype(BF16), k.astype(BF16)
            vb = v_ref[:, 256 * h:256 * (h + 1)].astype(BF16)
            dob = do_ref[:, 256 * h:256 * (h + 1)].astype(BF16)
            pb = st_ref[h].astype(BF16)
            r = rst[h]
            rb = r.astype(BF16)
            dm, qd, kd = dm_ref[h], qd_ref[h], kd_ref[h]
            sb = (_dot(qb, kb, "nt") * dm).astype(BF16)
            dsb = (_dot(dob, vb, "nt") * dm).astype(BF16)
            dq = _dot(dsb, kb) + _dot(dob, pb, "nt") * qd
            dk = _dot(dsb, qb, "tn") + _dot(vb, rb, "nt") * kd
            dv = _dot(sb, dob, "tn") + _dot((k * kd).astype(BF16), rb)
            rst[h] = _dot((q * qd).astype(BF16), dob, "tn") + qd[CHUNK - 1:CHUNK, 0:1] * r
            dp_ref[:, 128 * h:128 * (h + 1)] = _unrot(dq, c, s).astype(BF16)
            dp_ref[:, 512 + 128 * h:512 + 128 * (h + 1)] = (_unrot(dk, c, s) * RET_SCALE).astype(BF16)
            dp_ref[:, 1024 + 256 * h:1024 + 256 * (h + 1)] = dv.astype(BF16)

    rev = lambda n: nch - 1 - n
    tab = pl.BlockSpec((RET_HEADS, CHUNK, CHUNK), lambda n: (0, 0, 0))
    pos = pl.BlockSpec((CHUNK, CHUNK), lambda n: (rev(n), 0))
    row = pl.BlockSpec((CHUNK, 1024), lambda n: (rev(n), 0))
    return _pcall(
        body, name="ret_bwd", grid=(nch,),
        in_specs=[pl.BlockSpec(memory_space=pl.ANY), row, pl.BlockSpec((CHUNK, 1024), lambda n: (rev(n), 1)), row,
                  pl.BlockSpec((RET_HEADS, None, 128, 256), lambda n: (0, rev(n), 0, 0)),
                  pos, pos, tab, tab, tab],
        out_specs=pl.BlockSpec((CHUNK, 2048), lambda n: (rev(n), 0)),
        out_shape=jax.ShapeDtypeStruct((t, 5120), BF16),
        scratch_shapes=[pltpu.VMEM((RET_HEADS, 128, 256), F32)], input_output_aliases={0: 0},
        compiler_params=_params(dimension_semantics=("arbitrary",)))(
            dproj, proj, proj, do, states, cos2, sin2, dmat, qdec, kdec)


def _gn_gate_fwd(o, proj, gn_g):
    t = o.shape[0]
    tr = _rows(t)

    def body(o_ref, g_ref, w_ref, c_ref):
        for h in range(RET_HEADS):
            sl = slice(256 * h, 256 * (h + 1))
            x = o_ref[:, sl]
            mu = jnp.mean(x, axis=-1, keepdims=True)
            xc = x - mu
            rstd = lax.rsqrt(jnp.mean(xc * xc, axis=-1, keepdims=True) + EPS)
            g = g_ref[:, sl]
            c_ref[:, sl] = (g * _sigmoid(g) * (xc * rstd * w_ref[:, sl])).astype(BF16)

    return _pcall(body, name="gn_gate_fwd", grid=(t // tr,),
                  in_specs=[pl.BlockSpec((tr, 1024), lambda i: (i, 0)),
                            pl.BlockSpec((tr, 1024), lambda i: (i, 2)),
                            pl.BlockSpec((1, 1024), lambda i: (0, 0))],
                  out_specs=pl.BlockSpec((tr, 1024), lambda i: (i, 0)),
                  out_shape=jax.ShapeDtypeStruct((t, 2048), BF16))(o, proj, gn_g)


def _gn_gate_bwd(dcat, o, proj, gn_g):
    t = o.shape[0]
    tr = _rows(t)

    def body(d_ref, o_ref, g_ref, w_ref, do_ref, dg_ref, dw_ref):
        @pl.when(pl.program_id(0) == 0)
        def _():
            dw_ref[...] = jnp.zeros_like(dw_ref)

        for h in range(RET_HEADS):
            sl = slice(256 * h, 256 * (h + 1))
            x = o_ref[:, sl]
            mu = jnp.mean(x, axis=-1, keepdims=True)
            xc = x - mu
            rstd = lax.rsqrt(jnp.mean(xc * xc, axis=-1, keepdims=True) + EPS)
            xh = xc * rstd
            w = w_ref[:, sl]
            g = g_ref[:, sl]
            sg = _sigmoid(g)
            d = d_ref[:, sl]
            don = d * (g * sg)
            dg_ref[:, sl] = (d * (xh * w) * (sg * (1.0 + g * (1.0 - sg)))).astype(BF16)
            dw_ref[:, sl] += jnp.sum(don * xh, axis=0, keepdims=True)
            dxh = don * w
            m1 = jnp.mean(dxh, axis=-1, keepdims=True)
            m2 = jnp.mean(dxh * xh, axis=-1, keepdims=True)
            do_ref[:, sl] = rstd * (dxh - m1 - xh * m2)

    row = pl.BlockSpec((tr, 1024), lambda i: (i, 0))
    vec = pl.BlockSpec((1, 1024), lambda i: (0, 0))
    return _pcall(body, name="gn_gate_bwd", grid=(t // tr,),
                  in_specs=[row, row, pl.BlockSpec((tr, 1024), lambda i: (i, 2)), vec],
                  out_specs=[row, pl.BlockSpec((tr, 1024), lambda i: (i, 2)), vec],
                  out_shape=[jax.ShapeDtypeStruct((t, 1024), F32), jax.ShapeDtypeStruct((t, 5120), BF16),
                             jax.ShapeDtypeStruct((1, 1024), F32)])(dcat, o, proj, gn_g)


def _row_ids(i, tr):
    return i * tr + lax.broadcasted_iota(jnp.int32, (tr, 1), 0)


SH_ROWS = HALO - 8


def _shifted_copies(xs, sh, tr):
    for b in range(1, 8):
        sh[b - 1] = xs[pl.ds(b, tr + SH_ROWS), :]


def _shifted(xs, sh, off, tr):
    a, b = divmod(off, 8)
    return xs[pl.ds(8 * a, tr), :] if b == 0 else sh[b - 1, pl.ds(8 * a, tr), :]


def _conv_fwd(cat, proj, conv_w, conv_b, ln_g, ln_b):
    t = proj.shape[0]
    tr = _rows(t)
    hb = tr // HALO

    def body(cat_in, ua_ref, ug_ref, pa_ref, pg_ref, w_ref, b_ref, lg_ref, lb_ref, c_ref, hd_ref, y_ref, xs, sh):
        del cat_in
        i = pl.program_id(0)
        hdn = ua_ref[...] * _sigmoid(ug_ref[...])
        hd_ref[...] = hdn
        prev = pa_ref[...] * _sigmoid(pg_ref[...])
        xs[0:HALO, :] = jnp.where(i > 0, prev, 0.0)
        xs[HALO:HALO + tr, :] = hdn
        _shifted_copies(xs, sh, tr)
        acc = jnp.zeros((tr, 1024), F32) + b_ref[...]
        for w in range(CONV_WIDTH):
            acc += w_ref[w:w + 1, :] * _shifted(xs, sh, HALO - (CONV_WIDTH - 1) + w, tr)
        y_ref[...] = acc
        mu = jnp.mean(acc, axis=-1, keepdims=True)
        yc = acc - mu
        rstd = lax.rsqrt(jnp.mean(yc * yc, axis=-1, keepdims=True) + EPS)
        yn = yc * rstd * lg_ref[...] + lb_ref[...]
        c = yn * _sigmoid(yn)
        c_ref[...] = jnp.where(_row_ids(i, tr) >= PAD_FRONT, c, 0.0).astype(BF16)

    row = pl.BlockSpec((tr, 1024), lambda i: (i, 0))
    vec = pl.BlockSpec((1, 1024), lambda i: (0, 0))
    halo = lambda col: pl.BlockSpec((HALO, 1024), lambda i: (jnp.maximum(i * hb - 1, 0), col))
    return _pcall(body, name="conv_fwd", grid=(t // tr,),
                  in_specs=[pl.BlockSpec(memory_space=pl.ANY),
                            pl.BlockSpec((tr, 1024), lambda i: (i, 3)), pl.BlockSpec((tr, 1024), lambda i: (i, 4)),
                            halo(3), halo(4), pl.BlockSpec((32, 1024), lambda i: (0, 0)), vec, vec, vec],
                  out_specs=[pl.BlockSpec((tr, 1024), lambda i: (i, 1)), row, row],
                  out_shape=[jax.ShapeDtypeStruct((t, 2048), BF16), jax.ShapeDtypeStruct((t, 1024), F32),
                             jax.ShapeDtypeStruct((t, 1024), F32)],
                  scratch_shapes=[pltpu.VMEM((tr + HALO, 1024), F32), pltpu.VMEM((7, tr + SH_ROWS, 1024), F32)],
                  input_output_aliases={0: 0}, compiler_params=_params())(
                      cat, proj, proj, proj, proj, conv_w, conv_b, ln_g, ln_b)


def _conv_bwd_ln(dcat, y, ln_g, ln_b):
    t = y.shape[0]
    tr = _rows(t)

    def body(d_ref, y_ref, lg_ref, lb_ref, dy_ref, dlg_ref, dlb_ref, dcb_ref):
        i = pl.program_id(0)

        @pl.when(i == 0)
        def _():
            dlg_ref[...] = jnp.zeros_like(dlg_ref)
            dlb_ref[...] = jnp.zeros_like(dlb_ref)
            dcb_ref[...] = jnp.zeros_like(dcb_ref)

        y = y_ref[...]
        mu = jnp.mean(y, axis=-1, keepdims=True)
        yc = y - mu
        rstd = lax.rsqrt(jnp.mean(yc * yc, axis=-1, keepdims=True) + EPS)
        xh = yc * rstd
        lg = lg_ref[...]
        yn = xh * lg + lb_ref[...]
        sg = _sigmoid(yn)
        dyn = jnp.where(_row_ids(i, tr) >= PAD_FRONT, d_ref[...] * (sg * (1.0 + yn * (1.0 - sg))), 0.0)
        dlg_ref[...] += jnp.sum(dyn * xh, axis=0, keepdims=True)
        dlb_ref[...] += jnp.sum(dyn, axis=0, keepdims=True)
        dxh = dyn * lg
        m1 = jnp.mean(dxh, axis=-1, keepdims=True)
        m2 = jnp.mean(dxh * xh, axis=-1, keepdims=True)
        dy = rstd * (dxh - m1 - xh * m2)
        dy_ref[...] = dy
        dcb_ref[...] += jnp.sum(dy, axis=0, keepdims=True)

    row = pl.BlockSpec((tr, 1024), lambda i: (i, 0))
    vec = pl.BlockSpec((1, 1024), lambda i: (0, 0))
    vshape = jax.ShapeDtypeStruct((1, 1024), F32)
    return _pcall(body, name="conv_bwd_ln", grid=(t // tr,),
                  in_specs=[pl.BlockSpec((tr, 1024), lambda i: (i, 1)), row, vec, vec],
                  out_specs=[row, vec, vec, vec],
                  out_shape=[jax.ShapeDtypeStruct((t, 1024), F32), vshape, vshape, vshape])(dcat, y, ln_g, ln_b)


def _conv_bwd_taps(dproj, dy, hdn, proj, conv_w):
    t = dy.shape[0]
    tr = _rows(t)
    hb = tr // HALO
    nt = t // tr

    def body(dp_in, dy_ref, nx_ref, hd_ref, ph_ref, ua_ref, ug_ref, w_ref, da_ref, dg_ref, dw_ref, xs, sh):
        del dp_in
        i = pl.program_id(0)

        @pl.when(i == 0)
        def _():
            dw_ref[...] = jnp.zeros_like(dw_ref)

        dy = dy_ref[...]
        xs[0:tr, :] = dy
        xs[tr:tr + HALO, :] = jnp.where(i < nt - 1, nx_ref[...], 0.0)
        _shifted_copies(xs, sh, tr)
        dh = jnp.zeros((tr, 1024), F32)
        for w in range(CONV_WIDTH):
            dh += w_ref[w:w + 1, :] * _shifted(xs, sh, CONV_WIDTH - 1 - w, tr)
        xs[0:HALO, :] = jnp.where(i > 0, ph_ref[...], 0.0)
        xs[HALO:HALO + tr, :] = hd_ref[...]
        _shifted_copies(xs, sh, tr)
        for w in range(CONV_WIDTH):
            dw_ref[w:w + 1, :] += jnp.sum(dy * _shifted(xs, sh, HALO - (CONV_WIDTH - 1) + w, tr), axis=0, keepdims=True)
        dh = jnp.where(_row_ids(i, tr) >= PAD_FRONT, dh, 0.0)
        sg = _sigmoid(ug_ref[...])
        da_ref[...] = (dh * sg).astype(BF16)
        dg_ref[...] = (dh * ua_ref[...] * sg * (1.0 - sg)).astype(BF16)

    row = pl.BlockSpec((tr, 1024), lambda i: (i, 0))
    return _pcall(body, name="conv_bwd_taps", grid=(nt,),
                  in_specs=[pl.BlockSpec(memory_space=pl.ANY),
                            row, pl.BlockSpec((HALO, 1024), lambda i: (jnp.minimum((i + 1) * hb, nt * hb - 1), 0)),
                            row, pl.BlockSpec((HALO, 1024), lambda i: (jnp.maximum(i * hb - 1, 0), 0)),
                            pl.BlockSpec((tr, 1024), lambda i: (i, 3)), pl.BlockSpec((tr, 1024), lambda i: (i, 4)),
                            pl.BlockSpec((32, 1024), lambda i: (0, 0))],
                  out_specs=[pl.BlockSpec((tr, 1024), lambda i: (i, 3)), row, pl.BlockSpec((32, 1024), lambda i: (0, 0))],
                  out_shape=[jax.ShapeDtypeStruct((t, 5120), BF16), jax.ShapeDtypeStruct((t, 1024), BF16),
                             jax.ShapeDtypeStruct((32, 1024), F32)],
                  scratch_shapes=[pltpu.VMEM((tr + HALO, 1024), F32), pltpu.VMEM((7, tr + SH_ROWS, 1024), F32)],
                  input_output_aliases={0: 0}, compiler_params=_params())(
                      dproj, dy, dy, hdn, hdn, proj, proj, conv_w)


NEG_BIG = -1e30


def _seg_tables(qb):
    j = np.arange(128)
    bd = (j[:, None] // 64 == j[None, :] // 64).astype(np.float32)
    ones = np.ones((128, 128), np.float32)
    later = np.concatenate([(j[:, None] >= j[None, :]).astype(np.float32), ones], axis=1)
    earlier = np.concatenate([(j[:, None] < j[None, :]).astype(np.float32), ones], axis=1)
    per = qb // CHUNK
    row = np.arange(qb)[:, None]
    pad = np.broadcast_to(j[None, :] < PAD_FRONT, (qb, 128))
    diag = [(g * CHUNK + j[None, :]) >= row for g in range(per)]
    masks = diag + [np.zeros((qb, 128), bool), pad, diag[0] | pad]
    bias = np.stack([np.where(m, NEG_BIG, 0.0) for m in masks]).astype(np.float32)
    dup = lambda m: np.concatenate([m, m], axis=0)
    return (jnp.asarray(bd, BF16), jnp.asarray(dup(later), BF16), jnp.asarray(dup(earlier), BF16),
            jnp.asarray(bias, F32))


def _split_dot(x, m):
    hi = x.astype(BF16)
    lo = (x - hi.astype(F32)).astype(BF16)
    return _dot(hi, m) + _dot(lo, m)


def _qk_norm_fwd(qkv, qg, kg, bd):
    t = qkv.shape[0]
    tr = _rows(t)
    nb = tr // CHUNK

    def body(q_ref, k_ref, v_ref, qg_ref, kg_ref, bd_ref, qo, kt, k2, vt, v2):
        bdm = bd_ref[...]
        lane = lax.broadcasted_iota(jnp.int32, (1, 128), 1)
        sub = lax.broadcasted_iota(jnp.int32, (128, 1), 0)

        def pair_layouts(x, t_ref, s_ref, hp, b):
            xt = x.T
            t_ref[hp, b] = jnp.concatenate([jnp.where(sub < 64, xt, 0.0), jnp.where(sub >= 64, xt, 0.0)],
                                           axis=1).astype(BF16)
            s_ref[hp, b] = jnp.concatenate([jnp.where(lane < 64, x, 0.0), jnp.where(lane >= 64, x, 0.0)],
                                           axis=0).astype(BF16)

        for hp in range(8):
            sl = slice(128 * hp, 128 * (hp + 1))
            x = q_ref[:, sl]
            r = lax.rsqrt(_split_dot(x * x, bdm) * (1.0 / 64) + EPS)
            qo[:, sl] = (x * r * (qg_ref[:, sl] * SB_SCALE)).astype(BF16)
            x = k_ref[:, sl]
            r = lax.rsqrt(_split_dot(x * x, bdm) * (1.0 / 64) + EPS)
            kn = x * r * kg_ref[:, sl]
            v = v_ref[:, sl]
            for b in range(nb):
                rows = slice(CHUNK * b, CHUNK * (b + 1))
                pair_layouts(kn[rows], kt, k2, hp, b)
                pair_layouts(v[rows], vt, v2, hp, b)

    col = lambda c: pl.BlockSpec((tr, 1024), lambda i: (i, c))
    vec = pl.BlockSpec((1, 1024), lambda i: (0, 0))
    wide = pl.BlockSpec((8, nb, 128, 256), lambda i: (0, i, 0, 0))
    tall = pl.BlockSpec((8, nb, 256, 128), lambda i: (0, i, 0, 0))
    wsh = jax.ShapeDtypeStruct((8, t // CHUNK, 128, 256), BF16)
    tsh = jax.ShapeDtypeStruct((8, t // CHUNK, 256, 128), BF16)
    return _pcall(body, name="qk_norm_fwd", grid=(t // tr,),
                  in_specs=[col(0), col(1), col(2), vec, vec, pl.BlockSpec((128, 128), lambda i: (0, 0))],
                  out_specs=[col(0), wide, tall, wide, tall],
                  out_shape=[jax.ShapeDtypeStruct((t, 1024), BF16), wsh, tsh, wsh, tsh])(qkv, qkv, qkv, qg, kg, bd)


def _qk_norm_bwd(qkv, dq, dk, dv, qg, kg, bd):
    t = qkv.shape[0]
    tr = _rows(t)

    def body(q_ref, k_ref, dq_ref, dk_ref, dv_ref, qg_ref, kg_ref, bd_ref, o_ref, dqg_ref, dkg_ref):
        @pl.when(pl.program_id(0) == 0)
        def _():
            dqg_ref[...] = jnp.zeros_like(dqg_ref)
            dkg_ref[...] = jnp.zeros_like(dkg_ref)

        bdm = bd_ref[...]
        for part, (src, d_ref, g_ref, dg_ref) in enumerate(((q_ref, dq_ref, qg_ref, dqg_ref),
                                                           (k_ref, dk_ref, kg_ref, dkg_ref))):
            for cix in range(8):
                sl = slice(128 * cix, 128 * (cix + 1))
                x = src[:, sl]
                d = d_ref[:, sl]
                r = lax.rsqrt(_split_dot(x * x, bdm) * (1.0 / 64) + EPS)
                u = d * g_ref[:, sl]
                m = _split_dot(u * x, bdm) * (1.0 / 64)
                o_ref[:, 1024 * part + 128 * cix:1024 * part + 128 * (cix + 1)] = (r * u - x * (r * r * r * m)).astype(BF16)
                dg_ref[:, sl] += jnp.sum(d * x * r, axis=0, keepdims=True)
        o_ref[:, 2048:3072] = dv_ref[...].astype(BF16)

    col = lambda c: pl.BlockSpec((tr, 1024), lambda i: (i, c))
    vec = pl.BlockSpec((1, 1024), lambda i: (0, 0))
    vsh = jax.ShapeDtypeStruct((1, 1024), F32)
    return _pcall(body, name="qk_norm_bwd", grid=(t // tr,),
                  in_specs=[col(0), col(1), col(0), col(0), col(0), vec, vec, pl.BlockSpec((128, 128), lambda i: (0, 0))],
                  out_specs=[pl.BlockSpec((tr, 3072), lambda i: (i, 0)), vec, vec],
                  out_shape=[jax.ShapeDtypeStruct((t, 3072), BF16), vsh, vsh])(qkv, qkv, dq, dk, dv, qg, kg, bd)


def _split2(x):
    hi = x.astype(BF16)
    lo = (x - hi.astype(F32)).astype(BF16)
    return jnp.concatenate([hi, lo], axis=1)


def _sb_scores(z, later_tab):
    e = jnp.exp(-jnp.abs(z))
    ope = 1.0 + e
    sp = jnp.maximum(z, 0.0) + jnp.log(ope)
    return e, ope, _dot(_split2(sp), later_tab)


def _sb_bias_index(i, kb, per):
    g = kb - i * per
    return jnp.where(kb == 0, jnp.where(i == 0, per + 2, per + 1), jnp.where(g >= 0, g, per))


def _sb_qb(t):
    return _tile(t, (384, 128))


def _sb_fwd(qh, kt, v2, later_tab, bias_tab):
    t = qh.shape[0]
    qb = _sb_qb(t)
    per = qb // CHUNK
    nkb_all = t // CHUNK

    def body(q_ref, kt_ref, v2_ref, tab_ref, bias_ref, o_ref, ws_ref, acc, carry, zbuf, wbuf, wsem):
        h, i = pl.program_id(0), pl.program_id(1)
        q = q_ref[...]
        acc[...] = jnp.zeros_like(acc)
        carry[...] = jnp.zeros_like(carry)
        nkb = (i + 1) * per
        save = lambda kb: pltpu.make_async_copy(wbuf.at[kb], ws_ref.at[h, i, kb], wsem.at[kb])

        for u in range(per):
            zbuf[u] = _dot(q, kt_ref[nkb - 1 - u])

        def step(s, _):
            top = nkb - 1 - per * s

            @pl.when(s > 0)
            def _():
                for u in range(per):
                    save(top + per - u).start()

            z2s = [zbuf[u] for u in range(per)]
            for u in range(per):
                zbuf[u] = _dot(q, kt_ref[jnp.maximum(top - per - u, 0)])
            cins = [carry[0], carry[1]]
            zs, cus = [], []
            for u in range(per):
                bias = bias_ref[_sb_bias_index(i, top - u, per)]
                zs.append([z2s[u][:, 128 * hh:128 * (hh + 1)] + bias for hh in range(2)])
                cus.append([_sb_scores(z, tab_ref[...])[2] for z in zs[u]])
            part = None
            for u in range(per):
                kb = top - u
                for hh in range(2):
                    cu = cus[u][hh]
                    wbuf[kb, :, 128 * hh:128 * (hh + 1)] = jnp.exp(zs[u][hh] - cu[:, :128] - cins[hh]).astype(BF16)
                    cins[hh] = cins[hh] + cu[:, 128:]
                d = _dot(wbuf[kb], v2_ref[kb])
                part = d if part is None else part + d
            carry[0], carry[1] = cins[0], cins[1]
            acc[...] += part
            return 0

        lax.fori_loop(0, nkb // per, step, 0)
        for u in range(per):
            save(per - 1 - u).start()
        o_ref[...] = acc[...]

        def drain(kb, _):
            save(kb).wait()
            return 0

        lax.fori_loop(0, nkb, drain, 0)

    blk = pl.BlockSpec((qb, 128), lambda h, i: (i, h))
    wide = pl.BlockSpec((None, nkb_all, 128, 256), lambda h, i: (h, 0, 0, 0))
    tall = pl.BlockSpec((None, nkb_all, 256, 128), lambda h, i: (h, 0, 0, 0))
    return _pcall(body, name="sb_fwd", grid=(8, t // qb),
                  in_specs=[blk, wide, tall, pl.BlockSpec((256, 256), lambda h, i: (0, 0)),
                            pl.BlockSpec((per + 3, qb, 128), lambda h, i: (0, 0, 0))],
                  out_specs=[blk, pl.BlockSpec(memory_space=pl.ANY)],
                  out_shape=[jax.ShapeDtypeStruct((t, 1024), F32),
                             jax.ShapeDtypeStruct((8, t // qb, nkb_all, qb, 256), BF16)],
                  scratch_shapes=[pltpu.VMEM((qb, 128), F32), pltpu.VMEM((2, qb, 128), F32),
                                  pltpu.VMEM((per, qb, 256), F32), pltpu.VMEM((nkb_all, qb, 256), BF16),
                                  pltpu.SemaphoreType.DMA((nkb_all,))],
                  compiler_params=_params(dimension_semantics=("parallel", "arbitrary")))(
                      qh, kt, v2, later_tab, bias_tab)


def _sb_bwd(qh, kt, k2, vt, wsave, do, earlier_tab, bias_tab):
    t = qh.shape[0]
    qb = _sb_qb(t)
    per = qb // CHUNK
    nkb_all = t // CHUNK

    zero_slot = nkb_all

    def body(q_ref, kt_ref, k2_ref, vt_ref, ws_ref, do_ref, etab_ref, bias_ref,
             dq_ref, dk_ref, dv_ref, acc, gcarry, zbuf, dwbuf, wbuf, wsem, dzbuf):
        h, i = pl.program_id(0), pl.program_id(1)

        @pl.when(i == 0)
        def _():
            dk_ref[...] = jnp.zeros_like(dk_ref)
            dv_ref[...] = jnp.zeros_like(dv_ref)

        nkb = (i + 1) * per
        fetch = lambda kb: pltpu.make_async_copy(ws_ref.at[h, i, kb], wbuf.at[kb], wsem.at[kb])

        def prefetch(kb, _):
            fetch(kb).start()
            return 0

        lax.fori_loop(0, nkb, prefetch, 0)
        q = q_ref[...]
        dob = do_ref[...].astype(BF16)
        lane = lax.broadcasted_iota(jnp.int32, (1, 128), 1)
        acc[...] = jnp.zeros_like(acc)
        gcarry[...] = jnp.zeros_like(gcarry)
        zbuf[...] = _dot(q, kt_ref[0])
        dwbuf[...] = _dot(dob, vt_ref[0])
        dzbuf[...] = jnp.zeros_like(dzbuf)
        wbuf[zero_slot] = jnp.zeros((qb, 256), BF16)

        def gradients(slot, kb):
            dz2 = dzbuf[...]
            acc[...] += _dot(dz2, k2_ref[kb])
            dk2 = _dot(dz2, q, "tn")
            dv2 = _dot(wbuf[slot], dob, "tn")
            dk_ref[kb] += jnp.where(lane < 64, dk2[:128], dk2[128:])
            dv_ref[kb] += jnp.where(lane < 64, dv2[:128], dv2[128:])

        def step(kb, _):
            fetch(kb).wait()
            bias = bias_ref[_sb_bias_index(i, kb, per)]
            z2 = zbuf[...]
            dw2 = dwbuf[...]
            nxt = jnp.minimum(kb + 1, nkb - 1)
            zbuf[...] = _dot(q, kt_ref[nxt])
            dwbuf[...] = _dot(dob, vt_ref[nxt])
            gradients(jnp.where(kb == 0, zero_slot, kb - 1), jnp.maximum(kb - 1, 0))
            w2 = wbuf[kb]
            for hh in range(2):
                sl = slice(128 * hh, 128 * (hh + 1))
                z = z2[:, sl] + bias
                e = jnp.exp(-jnp.abs(z))
                r = 1.0 / (1.0 + e)
                sig = jnp.where(z >= 0, r, e * r)
                gw = w2[:, sl].astype(F32) * dw2[:, sl]
                cu2 = _dot(_split2(gw), etab_ref[...])
                gin = gcarry[hh]
                gcarry[hh] = gin + cu2[:, 128:]
                dzbuf[:, sl] = (gw - sig * (gw + cu2[:, :128] + gin)).astype(BF16)
            return 0

        lax.fori_loop(0, nkb, step, 0)
        gradients(nkb - 1, nkb - 1)
        dq_ref[...] = acc[...] * SB_SCALE

    blk = pl.BlockSpec((qb, 128), lambda h, i: (i, h))
    wide = pl.BlockSpec((None, nkb_all, 128, 256), lambda h, i: (h, 0, 0, 0))
    tall = pl.BlockSpec((None, nkb_all, 256, 128), lambda h, i: (h, 0, 0, 0))
    tab = pl.BlockSpec((256, 256), lambda h, i: (0, 0))
    kv_out = pl.BlockSpec((nkb_all, 128, 128), lambda h, i: (0, 0, h))
    ksh = jax.ShapeDtypeStruct((nkb_all, 128, 1024), F32)
    dq, dk, dv = _pcall(
        body, name="sb_bwd", grid=(8, t // qb),
        in_specs=[blk, wide, tall, wide, pl.BlockSpec(memory_space=pl.ANY), blk, tab,
                  pl.BlockSpec((per + 3, qb, 128), lambda h, i: (0, 0, 0))],
        out_specs=[blk, kv_out, kv_out], out_shape=[jax.ShapeDtypeStruct((t, 1024), F32), ksh, ksh],
        scratch_shapes=[pltpu.VMEM((qb, 128), F32), pltpu.VMEM((2, qb, 128), F32),
                        pltpu.VMEM((qb, 256), F32), pltpu.VMEM((qb, 256), F32),
                        pltpu.VMEM((nkb_all + 1, qb, 256), BF16), pltpu.SemaphoreType.DMA((nkb_all,)),
                        pltpu.VMEM((qb, 256), BF16)],
        compiler_params=_params(dimension_semantics=("parallel", "arbitrary")))(
            qh, kt, k2, vt, wsave, do, earlier_tab, bias_tab)
    return dq, dk.reshape(t, 1024), dv.reshape(t, 1024)


def _adamw_math(w, g, m, v):
    m = ADAM_B1 * m + (1.0 - ADAM_B1) * g
    v = ADAM_B2 * v + (1.0 - ADAM_B2) * (g * g)
    m_hat = m / (1.0 - ADAM_B1 ** ADAM_STEP)
    v_hat = v / (1.0 - ADAM_B2 ** ADAM_STEP)
    delta = -ADAM_LR * (m_hat / (jnp.sqrt(v_hat) + ADAM_EPS) + ADAM_WD * w)
    return delta, m, v


def _adamw(name, w, owns, recvs, m, v, me):
    shape = w.shape
    c = shape[-1]
    nl = len(owns)
    w3, m3, v3 = (a.reshape(nl, -1, c) for a in (w, m, v))
    r = w3.shape[1]
    tr = _tile(r, (256, 128))
    owns = [o.reshape(N_DEV, r, c) for o in owns]
    recvs = [p.reshape(N_DEV - 1, r, c) for p in recvs]

    def body(me_ref, w_ref, *rest):
        own_refs, recv_refs = rest[:nl], rest[nl:2 * nl]
        m_ref, v_ref = rest[2 * nl:2 * nl + 2]
        g_out, d_out, m_out, v_out = rest[2 * nl + 2:]
        layer = pl.program_id(0)

        def grad(k):
            g = own_refs[k][...].astype(F32)
            for s in range(N_DEV - 1):
                g = g + recv_refs[k][s].astype(F32)
            return g

        g = grad(0)
        for k in range(1, nl):
            g = jnp.where(layer == k, grad(k), g)
        d, mn, vn = _adamw_math(w_ref[...], g, m_ref[...], v_ref[...])
        g_out[...] = g
        d_out[...] = d
        m_out[...] = mn
        v_out[...] = vn

    row = pl.BlockSpec((None, tr, c), lambda l, i, me_ref: (l, i, 0))
    own = lambda k: pl.BlockSpec((None, tr, c), lambda l, i, me_ref: (me_ref[0], jnp.where(l == k, i, 0), 0))
    rcv = lambda k: pl.BlockSpec((N_DEV - 1, tr, c), lambda l, i, me_ref: (0, jnp.where(l == k, i, 0), 0))
    osh = jax.ShapeDtypeStruct((nl, r, c), F32)
    grid_spec = pltpu.PrefetchScalarGridSpec(
        num_scalar_prefetch=1, grid=(nl, r // tr),
        in_specs=[row] + [own(k) for k in range(nl)] + [rcv(k) for k in range(nl)] + [row, row],
        out_specs=[row, row, row, row])
    outs = _pcall(body, name=name, grid_spec=grid_spec, out_shape=[osh, osh, osh, osh])(
        me.reshape(1), w3, *owns, *recvs, m3, v3)
    return tuple(o.reshape(shape) for o in outs)


def _place():
    x, y, c = lax.axis_index("x"), lax.axis_index("y"), lax.axis_index("c")
    return x, y, c, 4 * x + 2 * y + c


def _peer(x, y, c, rel):
    return (x ^ ((rel >> 2) & 1), y ^ ((rel >> 1) & 1), c ^ (rel & 1))


def _gather_first(now, later):
    n, k = len(now), len(later)

    def body(*refs):
        ins, outs = refs[:n + k], refs[n + k:2 * (n + k)]
        send, recv, lsem = refs[2 * (n + k):]
        x, y, c, me = _place()
        locals_ = []
        for w in range(n + k):
            local = pltpu.make_async_copy(ins[w], outs[w].at[me], lsem.at[w])
            local.start()
            locals_.append(local)
        def copy(w, src, slot, rel, to_rel):
            return pltpu.make_async_remote_copy(src_ref=src, dst_ref=outs[w].at[slot], send_sem=send.at[w, rel - 1],
                                                recv_sem=recv.at[w, rel - 1], device_id=_peer(x, y, c, to_rel),
                                                device_id_type=MESH)

        for w in range(n):
            for rel in (1, 2, 4, 6):
                copy(w, ins[w], me, rel, rel).start()
        for w in range(n):
            for rel in (2, 4, 6):
                copy(w, ins[w], me ^ rel, rel, rel).wait_recv()
                copy(w, outs[w].at[me ^ rel], me ^ rel, rel | 1, 1).start()
        for w in range(n):
            for rel in (1, 3, 5, 7):
                copy(w, ins[w], me ^ rel, rel, 1).wait_recv()
            for rel in range(1, N_DEV):
                copy(w, ins[w], me, rel, rel).wait_send()
        for local in locals_:
            local.wait()

    hbm = pl.BlockSpec(memory_space=pl.ANY)
    vmem = pl.BlockSpec(memory_space=pltpu.VMEM)
    arrays = list(now) + list(later)
    return _pcall(body, name="gather_first", in_specs=[vmem] * (n + k), out_specs=[hbm] * (n + k),
                  out_shape=[jax.ShapeDtypeStruct((N_DEV,) + a.shape, a.dtype) for a in arrays],
                  scratch_shapes=[pltpu.SemaphoreType.DMA((n, N_DEV - 1)), pltpu.SemaphoreType.DMA((n, N_DEV - 1)),
                                  pltpu.SemaphoreType.DMA((n + k,))],
                  compiler_params=_params(has_side_effects=True))(*arrays)


_HBM = pl.BlockSpec(memory_space=pltpu.HBM)
_SEM = pl.BlockSpec(memory_space=pltpu.SEMAPHORE)
_DATAFLOW = pltpu.SideEffectType.DATAFLOW_SIDE_EFFECTING


def _exchange_refs(srcs, lands, mode, me, rel, j):
    if mode == "gather":
        return srcs[j], lands[j].at[me], lands[j].at[me ^ rel]
    return srcs[j].at[me ^ rel], lands[j].at[rel - 1], lands[j].at[rel - 1]


def _exchange_start(name, srcs, lands, mode):
    n = len(srcs)

    def body(*refs):
        ins, lnd = refs[:n], refs[n:2 * n]
        send, recv = refs[2 * n], refs[2 * n + 1]
        token = refs[-1]
        x, y, c, me = _place()
        for j in range(n):
            for rel in range(1, N_DEV):
                src, dst, _ = _exchange_refs(ins, lnd, mode, me, rel, j)
                pltpu.make_async_remote_copy(src_ref=src, dst_ref=dst, send_sem=send.at[j * (N_DEV - 1) + rel - 1],
                                             recv_sem=recv.at[j * (N_DEV - 1) + rel - 1],
                                             device_id=_peer(x, y, c, rel), device_id_type=MESH).start()
        token[...] = jnp.zeros_like(token)

    sems = pltpu.SemaphoreType.DMA((n * (N_DEV - 1),))
    hbm_like = lambda a: pltpu.HBM(a.shape, a.dtype)
    outs = _pcall(body, name=name + "_start",
                  in_specs=[_HBM] * (2 * n), out_specs=[_SEM, _SEM] + [_HBM] * (2 * n) + [pl.BlockSpec(memory_space=pltpu.VMEM)],
                  out_shape=[sems, sems] + [hbm_like(a) for a in srcs] + [hbm_like(a) for a in lands]
                  + [jax.ShapeDtypeStruct((8, 128), F32)],
                  input_output_aliases={i: 2 + i for i in range(2 * n)},
                  compiler_params=pltpu.CompilerParams(has_side_effects=_DATAFLOW))(
                      *[pltpu.with_memory_space_constraint(a, pltpu.HBM) for a in list(srcs) + list(lands)])
    return dict(name=name, mode=mode, n=n, send=outs[0], recv=outs[1], srcs=outs[2:2 + n], lands=outs[2 + n:2 + 2 * n],
                token=outs[-1][0, 0])


def _exchange_wait(ex, after):
    n, mode = ex["n"], ex["mode"]

    def body(*refs):
        ins, lnd = refs[:n], refs[n:2 * n]
        send, recv = refs[2 * n], refs[2 * n + 1]
        x, y, c, me = _place()
        for j in range(n):
            for rel in range(1, N_DEV):
                src, dst, landed = _exchange_refs(ins, lnd, mode, me, rel, j)
                pltpu.make_async_remote_copy(src_ref=src, dst_ref=dst, send_sem=send.at[j * (N_DEV - 1) + rel - 1],
                                             recv_sem=recv.at[j * (N_DEV - 1) + rel - 1],
                                             device_id=_peer(x, y, c, rel), device_id_type=MESH).wait_send()
                pltpu.make_async_remote_copy(src_ref=src, dst_ref=landed, send_sem=send.at[j * (N_DEV - 1) + rel - 1],
                                             recv_sem=recv.at[j * (N_DEV - 1) + rel - 1],
                                             device_id=_peer(x, y, c, rel), device_id_type=MESH).wait_recv()

    hbm_like = lambda a: pltpu.HBM(a.shape, a.dtype)
    arrays = list(ex["srcs"]) + list(ex["lands"])
    outs = _pcall(body, name=ex["name"] + "_wait",
                  in_specs=[_HBM] * (2 * n) + [_SEM, _SEM, pl.BlockSpec(memory_space=pl.ANY)],
                  out_specs=[_HBM] * (2 * n), out_shape=[hbm_like(a) for a in arrays],
                  input_output_aliases={i: i for i in range(2 * n)},
                  compiler_params=pltpu.CompilerParams(has_side_effects=_DATAFLOW))(
                      *arrays, ex["send"], ex["recv"], after)
    return outs[:n], outs[n:]


def _scatter_start(name, grads):
    lands = [lax.empty((N_DEV - 1,) + g.shape[1:], g.dtype) for g in grads]
    return _exchange_start(name, grads, lands, "scatter")


ROW_MIX, ROW_MLP, ROW_CB, ROW_LG, ROW_LB, ROW_QN, ROW_KN, ROW_LOSS = 0, 2, 4, 5, 6, 7, 8, 9
ROW_META, ROW_CW, ROW_GN, SMALL_ROWS = 16, 32, 64, 72


def _sum_small(slots):
    def body(s_ref, o_ref):
        tot = s_ref[0]
        for s in range(1, N_DEV):
            tot = tot + s_ref[s]
        o_ref[...] = tot
        for row in (ROW_QN, ROW_KN):
            v = tot[row:row + 1, :]
            f = v[:, 0:128]
            for k in range(1, 8):
                f = f + v[:, 128 * k:128 * (k + 1)]
            o_ref[row:row + 1, 0:64] = f[:, 0:64] + f[:, 64:128]

    return _pcall(body, name="sum_small", out_shape=jax.ShapeDtypeStruct(slots.shape[1:], F32))(slots)


def _adamw_small(w, g, m, v):
    def body(w_ref, g_ref, m_ref, v_ref, d_out, m_out, v_out):
        d, mn, vn = _adamw_math(w_ref[...], g_ref[...], m_ref[...], v_ref[...])
        d_out[...] = d
        m_out[...] = mn
        v_out[...] = vn

    osh = jax.ShapeDtypeStruct(w.shape, F32)
    return _pcall(body, name="adamw_small", out_shape=[osh, osh, osh])(w, g, m, v)


def _local_step(h0, target, p, weight, emit):
    t = h0.shape[0]
    tables = _ret_tables(t)
    bd, later_tab, earlier_tab, bias_tab = _seg_tables(_sb_qb(t))
    row = lambda a, i: a[i:i + 1]

    hn_a = _rms_fwd("rms_mix0", h0, row(p["norm_mix_g"], 0))
    w_in = weight("w_in", hn_a)
    proj = _mm_cols("proj_in", hn_a, w_in, ())
    o_ret, states = _ret_fwd(proj, tables)
    gn_flat = p["gn_g"].reshape(1, 1024)
    cat = _gn_gate_fwd(o_ret, proj, gn_flat)
    cat, hdn, ycv = _conv_fwd(cat, proj, p["conv_w"], p["conv_b"], p["ln_g"], p["ln_b"])
    w_out = weight("w_out", cat)
    h1, hn_b = _mm_rows_norm("mix_out", cat, w_out, h0, row(p["norm_mlp_g"], 0))
    w1_0, w2_0 = weight("w1_0", hn_b), weight("w2_0", hn_b)
    a0, s0 = _mm_cols("mlp0_up", hn_b, w1_0, (), epi="relu2")
    h2, hn_c = _mm_rows_norm("mlp0_down", s0, w2_0, h1, row(p["norm_mix_g"], 1))

    w_qkv = weight("w_qkv", hn_c)
    qkv = _mm_cols("qkv", hn_c, w_qkv, ())
    qg = jnp.tile(p["qn_g"], (1, 16))
    kg = jnp.tile(p["kn_g"], (1, 16))
    qh, kt, k2, vt, v2 = _qk_norm_fwd(qkv, qg, kg, bd)
    o_sb, w_sb = _sb_fwd(qh, kt, v2, later_tab, bias_tab)
    w_o = weight("w_o", o_sb)
    h3, hn_d = _mm_rows_norm("attn_out", o_sb, w_o, h2, row(p["norm_mlp_g"], 1))
    w1_1, w2_1 = weight("w1_1", hn_d), weight("w2_1", hn_d)
    a1, s1 = _mm_cols("mlp1_up", hn_d, w1_1, (), epi="relu2")
    h4 = _mm_rows("mlp1_down", s1, w2_1, h3)

    dh, loss = _loss_bwd(h4, target)

    def mlp_bwd(tag, w1, dh, hn, s, da):
        dw2 = _wgrad_rows(f"{tag}_dw2", s, dh, 512)
        dw1 = _wgrad_cols(f"{tag}_dw1", hn, da, 512)
        return _mm_cols_t(f"{tag}_dhn", da, w1), emit(tag, [dw1, dw2])

    da1 = _mm_rows_t("mlp1_dact", dh, w2_1, (), out_dtype=BF16, epi="drelu2", extra=a1)
    dhn, tok = mlp_bwd("mlp1", w1_1, dh, hn_d, s1, da1)

    dh, dg_mlp1, do_sb = _mm_rows_t_rms("attn_dout", dhn, h3, row(p["norm_mlp_g"], 1) + tok, dh, w_o)
    dw_o = _wgrad_rows("attn_dwo", o_sb, dh, 128)
    dq, dk, dv = _sb_bwd(qh, kt, k2, vt, w_sb, do_sb, earlier_tab, bias_tab)
    dqkv, dqg, dkg = _qk_norm_bwd(qkv, dq, dk, dv, qg, kg, bd)
    dw_qkv = _wgrad_cols("qkv_dw", hn_c, dqkv, 384)
    tok = emit("attn", [dw_qkv, dw_o])
    dhn = _mm_cols_t("qkv_dhn", dqkv, w_qkv)
    dh, dg_mix1, da0 = _mm_rows_t_rms("mlp0_dact", dhn, h2, row(p["norm_mix_g"], 1) + tok, dh, w2_0,
                                      out_dtype=BF16, relu_of=a0)
    dhn, tok = mlp_bwd("mlp0", w1_0, dh, hn_b, s0, da0)

    dh, dg_mlp0, dcat = _mm_rows_t_rms("mix_dcat", dhn, h1, row(p["norm_mlp_g"], 0) + tok, dh, w_out)
    dw_out = _wgrad_rows("mix_dwout", cat, dh, 256)
    tok = emit("mix0_out", [dw_out])
    do_ret, dproj, dgn = _gn_gate_bwd(dcat, o_ret, proj, gn_flat + tok)
    dproj = _ret_bwd(dproj, proj, states, do_ret, tables)
    dy, dlg, dlb, dcb = _conv_bwd_ln(dcat, ycv, p["ln_g"], p["ln_b"])
    dproj, dug, dcw = _conv_bwd_taps(dproj, dy, hdn, proj, p["conv_w"])
    dproj = lax.dynamic_update_slice(dproj, dug, (0, 4096))
    dw_in = _wgrad_cols("proj_dw", hn_a, dproj, 640)
    tok = emit("mix0", [dw_in])
    dhn = _mm_cols_t("proj_dhn", dproj, w_in)
    dh, dg_mix0 = _rms_bwd("mix0_rms_bwd", dhn, h0, row(p["norm_mix_g"], 0) + tok, dh)

    rid = lax.broadcasted_iota(jnp.int32, (16, 1), 0)
    loss_row = jnp.broadcast_to(loss[0:1, 0:1], (1, D_MODEL))
    vecs = sum(jnp.where(rid == k, v, 0.0)
               for k, v in enumerate((dg_mix0, dg_mix1, dg_mlp0, dg_mlp1, dcb, dlg, dlb, dqg, dkg, loss_row)))
    small = jnp.concatenate([vecs, dh[PAD_FRONT:TOK0], dcw, jnp.where(rid[:8] == 0, dgn, 0.0)], axis=0)
    return dh[TOK0:], small


_SMALL_NAMES = ("meta", "norm_mix_g", "norm_mlp_g", "even_ret_gn_g", "even_conv_w", "even_conv_b",
                "even_conv_ln_g", "even_conv_ln_b", "odd_q_norm_g", "odd_k_norm_g")
_BIG_NAMES = ("even_w_in", "even_w_out", "odd_w_qkv", "odd_w_o", "mlp_w1", "mlp_w2")
_ORDER = ("meta", "norm_mix_g", "norm_mlp_g", "even_w_in", "even_ret_gn_g", "even_conv_w", "even_conv_b",
          "even_conv_ln_g", "even_conv_ln_b", "even_w_out", "odd_w_qkv", "odd_q_norm_g", "odd_k_norm_g",
          "odd_w_o", "mlp_w1", "mlp_w2")


def _pack128(a):
    flat = a.reshape(-1)
    n = flat.shape[0]
    rows = -(-n // 128)
    rows8 = -(-rows // 8) * 8
    return jnp.pad(flat, (0, rows8 * 128 - n)).reshape(rows8, 128)


def kernel(x, meta, norm_mix_g, norm_mlp_g, even_w_in, even_ret_gn_g, even_conv_w, even_conv_b, even_conv_ln_g, even_conv_ln_b, even_w_out, odd_w_qkv, odd_q_norm_g, odd_k_norm_g, odd_w_o, mlp_w1, mlp_w2, loss_target, m_meta, m_norm_mix_g, m_norm_mlp_g, m_even_w_in, m_even_ret_gn_g, m_even_conv_w, m_even_conv_b, m_even_conv_ln_g, m_even_conv_ln_b, m_even_w_out, m_odd_w_qkv, m_odd_q_norm_g, m_odd_k_norm_g, m_odd_w_o, m_mlp_w1, m_mlp_w2, v_meta, v_norm_mix_g, v_norm_mlp_g, v_even_w_in, v_even_ret_gn_g, v_even_conv_w, v_even_conv_b, v_even_conv_ln_g, v_even_conv_ln_b, v_even_w_out, v_odd_w_qkv, v_odd_q_norm_g, v_odd_k_norm_g, v_odd_w_o, v_mlp_w1, v_mlp_w2):
    w = dict(meta=meta, norm_mix_g=norm_mix_g, norm_mlp_g=norm_mlp_g, even_w_in=even_w_in,
             even_ret_gn_g=even_ret_gn_g, even_conv_w=even_conv_w, even_conv_b=even_conv_b,
             even_conv_ln_g=even_conv_ln_g, even_conv_ln_b=even_conv_ln_b, even_w_out=even_w_out,
             odd_w_qkv=odd_w_qkv, odd_q_norm_g=odd_q_norm_g, odd_k_norm_g=odd_k_norm_g, odd_w_o=odd_w_o,
             mlp_w1=mlp_w1, mlp_w2=mlp_w2)
    mom = dict(meta=m_meta, norm_mix_g=m_norm_mix_g, norm_mlp_g=m_norm_mlp_g, even_w_in=m_even_w_in,
               even_ret_gn_g=m_even_ret_gn_g, even_conv_w=m_even_conv_w, even_conv_b=m_even_conv_b,
               even_conv_ln_g=m_even_conv_ln_g, even_conv_ln_b=m_even_conv_ln_b, even_w_out=m_even_w_out,
               odd_w_qkv=m_odd_w_qkv, odd_q_norm_g=m_odd_q_norm_g, odd_k_norm_g=m_odd_k_norm_g, odd_w_o=m_odd_w_o,
               mlp_w1=m_mlp_w1, mlp_w2=m_mlp_w2)
    var = dict(meta=v_meta, norm_mix_g=v_norm_mix_g, norm_mlp_g=v_norm_mlp_g, even_w_in=v_even_w_in,
               even_ret_gn_g=v_even_ret_gn_g, even_conv_w=v_even_conv_w, even_conv_b=v_even_conv_b,
               even_conv_ln_g=v_even_conv_ln_g, even_conv_ln_b=v_even_conv_ln_b, even_w_out=v_even_w_out,
               odd_w_qkv=v_odd_w_qkv, odd_q_norm_g=v_odd_q_norm_g, odd_k_norm_g=v_odd_k_norm_g, odd_w_o=v_odd_w_o,
               mlp_w1=v_mlp_w1, mlp_w2=v_mlp_w2)
    me = 4 * lax.axis_index("x") + 2 * lax.axis_index("y") + lax.axis_index("c")

    small_in = jnp.concatenate([meta, jnp.pad(even_conv_w[0], ((0, 1), (0, 0))),
                                jnp.pad(even_ret_gn_g[0], ((0, 4), (0, 96)))], axis=0)
    b16 = lambda a: a.astype(BF16)
    later_src = dict(w_out=b16(even_w_out[0]), w1_0=b16(mlp_w1[0]), w2_0=b16(mlp_w2[0]),
                     w_qkv=b16(odd_w_qkv[0]), w_o=b16(odd_w_o[0]), w1_1=b16(mlp_w1[1]), w2_1=b16(mlp_w2[1]))
    landed = _gather_first([b16(even_w_in[0]), small_in], list(later_src.values()))
    g_in, g_small = landed[0], landed[1]
    own_slot = dict(zip(later_src, landed[2:]))
    groups = (("gather_l0", ("w_out", "w1_0", "w2_0")), ("gather_attn", ("w_qkv", "w_o")),
              ("gather_l1", ("w1_1", "w2_1")))
    pending = {}
    gather_tok = jnp.zeros((), F32)
    for gname, names in groups:
        ex = _exchange_start(gname, [later_src[n] for n in names], [own_slot[n] for n in names], "gather")
        gather_tok = gather_tok + ex["token"]
        for n in names:
            pending[n] = (ex, names)
    arrived = dict(w_in=g_in)

    def weight(name, after):
        if name not in arrived:
            ex, names = pending[name]
            arrived.update(zip(names, _exchange_wait(ex, after)[1]))
        return arrived[name]

    cols = lambda a: jnp.transpose(a, (1, 0, 2)).reshape(a.shape[1], -1)
    p = dict(norm_mix_g=norm_mix_g + gather_tok, norm_mlp_g=norm_mlp_g, conv_b=even_conv_b, ln_g=even_conv_ln_g,
             ln_b=even_conv_ln_b, qn_g=odd_q_norm_g, kn_g=odd_k_norm_g,
             gn_g=cols(g_small[:, 48:52, :32]),
             conv_w=jnp.pad(cols(g_small[:, 16:47]), ((0, 1), (0, 0))))
    meta_full = cols(g_small[:, 0:16])

    scatters = {}

    def emit(tag, grads):
        scatters[tag] = _scatter_start("scatter_" + tag, grads)
        return scatters[tag]["token"]

    h0 = jnp.concatenate([jnp.zeros((PAD_FRONT, D_MODEL), F32), meta_full, x[0]], axis=0)
    grad_x, small_part = _local_step(h0, loss_target[0], p, weight, emit)

    out = {}
    got = {}

    def update(names, terms, after):
        for tag in {t for name in names for t, _ in terms[name]} - set(got):
            got[tag] = _exchange_wait(scatters[tag], after)
        for name in names:
            owns, recvs = zip(*[(got[t][0][j], got[t][1][j]) for t, j in terms[name]])
            out[name] = _adamw("adamw_" + name, w[name], list(owns), list(recvs), mom[name], var[name], me)

    terms = dict(even_w_in=[("mix0", 0)], even_w_out=[("mix0_out", 0)], odd_w_qkv=[("attn", 0)], odd_w_o=[("attn", 1)],
                 mlp_w1=[("mlp0", 0), ("mlp1", 0)], mlp_w2=[("mlp0", 1), ("mlp1", 1)])
    small_ex = _exchange_start("small", [small_part], [lax.empty((N_DEV,) + small_part.shape, F32)], "gather")
    update(("mlp_w1", "mlp_w2", "odd_w_qkv", "odd_w_o", "even_w_out"), terms, grad_x)
    update(("even_w_in",), terms, out["even_w_out"][1])
    (own_part,), (slots,) = _exchange_wait(small_ex, out["even_w_in"][1])
    tot = _sum_small(lax.dynamic_update_slice(slots, own_part[None], (me, 0, 0)))
    loss = tot[ROW_LOSS, 0]

    shard_cols = lambda a, width: lax.dynamic_slice_in_dim(a, me * width, width, axis=1)
    one = lambda r: tot[r:r + 1]
    small_g = dict(
        norm_mix_g=tot[ROW_MIX:ROW_MIX + 2], norm_mlp_g=tot[ROW_MLP:ROW_MLP + 2],
        even_conv_b=one(ROW_CB), even_conv_ln_g=one(ROW_LG), even_conv_ln_b=one(ROW_LB),
        odd_q_norm_g=one(ROW_QN)[:, :64], odd_k_norm_g=one(ROW_KN)[:, :64],
        meta=shard_cols(tot[ROW_META:ROW_META + N_META], 128),
        even_conv_w=shard_cols(tot[ROW_CW:ROW_CW + CONV_WIDTH], 128)[None],
        even_ret_gn_g=shard_cols(tot[ROW_GN].reshape(4, 256), 32)[None])
    packs = {n: (_pack128(w[n]), _pack128(small_g[n]), _pack128(mom[n]), _pack128(var[n])) for n in _SMALL_NAMES}
    cat4 = [jnp.concatenate([packs[n][i] for n in _SMALL_NAMES], axis=0) for i in range(4)]
    d_s, m_s, v_s = _adamw_small(*cat4)
    r0 = 0
    for n in _SMALL_NAMES:
        rows = packs[n][0].shape[0]
        size = w[n].size
        take = lambda a: a[r0:r0 + rows].reshape(-1)[:size].reshape(w[n].shape)
        out[n] = (small_g[n].reshape(w[n].shape), take(d_s), take(m_s), take(v_s))
        r0 += rows

    res = [loss, grad_x[None]]
    for i in range(4):
        res.extend(out[n][i] for n in _ORDER)
    return tuple(res)
```

```python
import functools

import numpy as np
import jax
import jax.numpy as jnp
from jax import lax
from jax.experimental import pallas as pl
from jax.experimental.pallas import tpu as pltpu

F32 = jnp.float32
BF16 = jnp.bfloat16

D_MODEL = 1024
N_META = 16
CHUNK = 128
PAD_FRONT = 112
TOK0 = PAD_FRONT + N_META
EPS = 1e-6
N_DEV = 8
RET_HEADS = 4
RET_DECAY_OFFSET = 5.0
ROPE_BASE = 10000.0
CONV_WIDTH = 31
HALO = 32
SB_SCALE = 64 ** -0.5
RET_SCALE = 128 ** -0.5
ADAM_LR, ADAM_B1, ADAM_B2, ADAM_EPS, ADAM_WD, ADAM_STEP = 0.001, 0.9, 0.999, 1e-08, 0.01, 10
VMEM_LIMIT = 56 * 1024 * 1024
MESH = pl.DeviceIdType.MESH


def _pcall(body, **kw):
    return pl.pallas_call(body, **kw)


def _params(**kw):
    return pltpu.CompilerParams(vmem_limit_bytes=VMEM_LIMIT, **kw)


def _tile(n, cands):
    for c in cands:
        if n % c == 0:
            return c
    raise ValueError(f"no tile for {n} in {cands}")


def _sigmoid(x):
    return 1.0 / (1.0 + jnp.exp(-x))


_DIMS = {
    "nn": (((1,), (0,)), ((), ())),
    "nt": (((1,), (1,)), ((), ())),
    "tn": (((0,), (0,)), ((), ())),
}


def _matmul(name, a, b, *, grid, a_spec, b_spec, o_spec, out_shape, contract, acc_shape,
            epi="plain", extra=None, extra_spec=None):
    nk = grid[2]
    dims = _DIMS[contract]
    n_in = 3 if extra is not None else 2
    n_out = 2 if epi == "relu2" else 1

    def body(*refs):
        a_ref, b_ref = refs[0], refs[1]
        e_ref = refs[2] if extra is not None else None
        outs = refs[n_in:n_in + n_out]
        acc = refs[-1]
        k = pl.program_id(2)
        part = lax.dot_general(a_ref[...].astype(BF16), b_ref[...].astype(BF16), dims, preferred_element_type=F32)
        if nk > 1:
            @pl.when(k == 0)
            def _():
                acc[...] = jnp.zeros_like(acc)

            acc[...] += part

        @pl.when(k == nk - 1)
        def _():
            r = acc[...] if nk > 1 else part
            if epi == "plain":
                outs[0][...] = r.astype(outs[0].dtype)
            elif epi == "residual":
                outs[0][...] = (r + e_ref[...]).astype(outs[0].dtype)
            elif epi == "relu2":
                outs[0][...] = r
                rr = jnp.maximum(r, 0.0)
                outs[1][...] = (rr * rr).astype(BF16)
            elif epi == "drelu2":
                outs[0][...] = (r * (2.0 * jnp.maximum(e_ref[...], 0.0))).astype(outs[0].dtype)

    in_specs = [a_spec, b_spec] + ([extra_spec] if extra is not None else [])
    args = (a, b) + ((extra,) if extra is not None else ())
    if n_out == 2:
        out_specs = [o_spec, o_spec]
    else:
        out_specs = o_spec
    return _pcall(body, name=name, grid=grid, in_specs=in_specs, out_specs=out_specs,
                  out_shape=out_shape, scratch_shapes=[pltpu.VMEM(acc_shape, F32)],
                  compiler_params=_params(dimension_semantics=("parallel", "parallel", "arbitrary")))(*args)


def _tm(t):
    return _tile(t, (1408, 768, 384, 128))


def _mm_cols(name, a, wb, lead, out_dtype=F32, epi="plain"):
    t, kdim = a.shape
    n = wb.shape[-1]
    tm, tk = _tm(t), _tile(kdim, (1024, 512))
    nl = len(lead)
    b_spec = pl.BlockSpec((None,) * (1 + nl) + (tk, n), lambda i, j, k: (j,) + lead + (k, 0))
    o_spec = pl.BlockSpec((tm, n), lambda i, j, k: (i, j))
    if epi == "relu2":
        out_shape = [jax.ShapeDtypeStruct((t, N_DEV * n), F32), jax.ShapeDtypeStruct((t, N_DEV * n), BF16)]
    else:
        out_shape = jax.ShapeDtypeStruct((t, N_DEV * n), out_dtype)
    return _matmul(name, a, wb, grid=(t // tm, N_DEV, kdim // tk),
                   a_spec=pl.BlockSpec((tm, tk), lambda i, j, k: (i, k)), b_spec=b_spec, o_spec=o_spec,
                   out_shape=out_shape, contract="nn", acc_shape=(tm, n), epi=epi)


def _tm_deep(t, kdim):
    return _tm(t) if kdim <= 2048 else _tile(t, (704, 384, 128))


def _mm_cols_t(name, a, wb):
    t = a.shape[0]
    nb, kdim, n = wb.shape
    tm, tn = _tm_deep(t, nb * n), _tile(kdim, (512,))

    def body(a_ref, b_ref, o_ref):
        acc = _dot(a_ref[:, 0:n].astype(BF16), b_ref[0], "nt")
        for j in range(1, nb):
            acc = acc + _dot(a_ref[:, j * n:(j + 1) * n].astype(BF16), b_ref[j], "nt")
        o_ref[...] = acc

    return _pcall(body, name=name, grid=(t // tm, kdim // tn),
                  in_specs=[pl.BlockSpec((tm, nb * n), lambda i, j: (i, 0)),
                            pl.BlockSpec((nb, tn, n), lambda i, j: (0, j, 0))],
                  out_specs=pl.BlockSpec((tm, tn), lambda i, j: (i, j)),
                  out_shape=jax.ShapeDtypeStruct((t, kdim), F32),
                  compiler_params=_params(dimension_semantics=("parallel", "parallel")))(a, wb)


def _mm_rows(name, a, wb, residual):
    t = a.shape[0]
    nb, r, n = wb.shape
    tm, tn = _tm_deep(t, nb * r), _tile(n, (512,))

    def body(a_ref, b_ref, r_ref, o_ref):
        o_ref[...] = r_ref[...] + _dot(a_ref[...].astype(BF16), b_ref[...].reshape(nb * r, tn))

    o_spec = pl.BlockSpec((tm, tn), lambda i, j: (i, j))
    return _pcall(body, name=name, grid=(t // tm, n // tn),
                  in_specs=[pl.BlockSpec((tm, nb * r), lambda i, j: (i, 0)),
                            pl.BlockSpec((nb, r, tn), lambda i, j: (0, 0, j)), o_spec],
                  out_specs=o_spec, out_shape=jax.ShapeDtypeStruct((t, n), F32),
                  compiler_params=_params(dimension_semantics=("parallel", "parallel")))(a, wb, residual)


def _mm_rows_t(name, a, wb, lead, out_dtype=F32, epi="plain", extra=None):
    t, n = a.shape
    r = wb.shape[-2]
    tm, tk = _tm(t), _tile(n, (1024,))
    nl = len(lead)
    b_spec = pl.BlockSpec((None,) * (1 + nl) + (r, tk), lambda i, j, k: (j,) + lead + (0, k))
    o_spec = pl.BlockSpec((tm, r), lambda i, j, k: (i, j))
    return _matmul(name, a, wb, grid=(t // tm, N_DEV, n // tk),
                   a_spec=pl.BlockSpec((tm, tk), lambda i, j, k: (i, k)), b_spec=b_spec, o_spec=o_spec,
                   out_shape=jax.ShapeDtypeStruct((t, N_DEV * r), out_dtype), contract="nt",
                   acc_shape=(tm, r), epi=epi, extra=extra, extra_spec=o_spec if extra is not None else None)


def _mm_rows_norm(name, a, wb, residual, g):
    t = a.shape[0]
    nb, r, n = wb.shape
    tm = _tile(t, (704, 384, 128))

    def body(a_ref, b_ref, r_ref, g_ref, h_ref, hn_ref):
        h = r_ref[...] + _dot(a_ref[...].astype(BF16), b_ref[...].reshape(nb * r, n))
        h_ref[...] = h
        hn_ref[...] = (h * lax.rsqrt(jnp.mean(h * h, axis=-1, keepdims=True) + EPS) * g_ref[...]).astype(BF16)

    row = pl.BlockSpec((tm, n), lambda i: (i, 0))
    return _pcall(body, name=name, grid=(t // tm,),
                  in_specs=[pl.BlockSpec((tm, nb * r), lambda i: (i, 0)), pl.BlockSpec((nb, r, n), lambda i: (0, 0, 0)),
                            row, pl.BlockSpec((1, n), lambda i: (0, 0))],
                  out_specs=[row, row],
                  out_shape=[jax.ShapeDtypeStruct((t, n), F32), jax.ShapeDtypeStruct((t, n), BF16)],
                  compiler_params=_params(dimension_semantics=("parallel",)))(a, wb, residual, g)


def _wgrad_cols(name, x, dy, n):
    t, kdim = x.shape
    tk = _tm(t)
    return _matmul(name, x, dy, grid=(1, N_DEV, t // tk),
                   a_spec=pl.BlockSpec((tk, kdim), lambda i, j, k: (k, 0)),
                   b_spec=pl.BlockSpec((tk, n), lambda i, j, k: (k, j)),
                   o_spec=pl.BlockSpec((None, kdim, n), lambda i, j, k: (j, 0, 0)),
                   out_shape=jax.ShapeDtypeStruct((N_DEV, kdim, n), BF16), contract="tn", acc_shape=(kdim, n))


def _wgrad_rows(name, x, dy, r):
    t = x.shape[0]
    n = dy.shape[1]
    tk, tn = _tm(t), _tile(n, (512,))
    tm = min(N_DEV * r, 1024)
    out = _matmul(name, x, dy, grid=(N_DEV * r // tm, n // tn, t // tk),
                  a_spec=pl.BlockSpec((tk, tm), lambda i, j, k: (k, i)),
                  b_spec=pl.BlockSpec((tk, tn), lambda i, j, k: (k, j)),
                  o_spec=pl.BlockSpec((tm, tn), lambda i, j, k: (i, j)),
                  out_shape=jax.ShapeDtypeStruct((N_DEV * r, n), BF16), contract="tn", acc_shape=(tm, tn))
    return out.reshape(N_DEV, r, n)


def _rows(t):
    return _tile(t, (384, 128))


def _rms_fwd(name, h, g):
    t = h.shape[0]
    tr = _rows(t)

    def body(h_ref, g_ref, o_ref):
        x = h_ref[...]
        r = lax.rsqrt(jnp.mean(x * x, axis=-1, keepdims=True) + EPS)
        o_ref[...] = (x * r * g_ref[...]).astype(BF16)

    row = pl.BlockSpec((tr, D_MODEL), lambda i: (i, 0))
    vec = pl.BlockSpec((1, D_MODEL), lambda i: (0, 0))
    return _pcall(body, name=name, grid=(t // tr,), in_specs=[row, vec], out_specs=row,
                  out_shape=jax.ShapeDtypeStruct((t, D_MODEL), BF16))(h, g)


def _rms_bwd(name, dhn, h, g, dres):
    t = h.shape[0]
    tr = _rows(t)

    def body(d_ref, h_ref, g_ref, r_ref, o_ref, dg_ref):
        @pl.when(pl.program_id(0) == 0)
        def _():
            dg_ref[...] = jnp.zeros_like(dg_ref)

        x = h_ref[...]
        d = d_ref[...]
        r = lax.rsqrt(jnp.mean(x * x, axis=-1, keepdims=True) + EPS)
        u = d * g_ref[...]
        m = jnp.mean(u * x, axis=-1, keepdims=True)
        o_ref[...] = r_ref[...] + r * u - x * (r * r * r * m)
        dg_ref[...] += jnp.sum(d * x * r, axis=0, keepdims=True)

    row = pl.BlockSpec((tr, D_MODEL), lambda i: (i, 0))
    vec = pl.BlockSpec((1, D_MODEL), lambda i: (0, 0))
    return _pcall(body, name=name, grid=(t // tr,), in_specs=[row, row, vec, row], out_specs=[row, vec],
                  out_shape=[jax.ShapeDtypeStruct((t, D_MODEL), F32), jax.ShapeDtypeStruct((1, D_MODEL), F32)])(
                      dhn, h, g, dres)


def _loss_bwd(h, target):
    t = h.shape[0]
    nb = t // CHUNK

    def body(h_ref, t_ref, d_ref, l_ref):
        i = pl.program_id(0)

        @pl.when(i == 0)
        def _():
            d_ref[...] = jnp.zeros_like(d_ref)
            l_ref[...] = jnp.zeros_like(l_ref)

        @pl.when(i > 0)
        def _():
            diff = h_ref[...] - t_ref[...]
            d_ref[...] = diff * (1.0 / D_MODEL)
            l_ref[...] += jnp.sum(diff * diff) * (0.5 / D_MODEL)

    return _pcall(body, name="loss_bwd", grid=(nb,),
                  in_specs=[pl.BlockSpec((CHUNK, D_MODEL), lambda i: (i, 0)),
                            pl.BlockSpec((CHUNK, D_MODEL), lambda i: (jnp.maximum(i - 1, 0), 0))],
                  out_specs=[pl.BlockSpec((CHUNK, D_MODEL), lambda i: (i, 0)),
                             pl.BlockSpec((8, 128), lambda i: (0, 0))],
                  out_shape=[jax.ShapeDtypeStruct((t, D_MODEL), F32), jax.ShapeDtypeStruct((8, 128), F32)])(h, target)


def _ret_tables(t):
    hh = np.arange(RET_HEADS, dtype=np.float64)
    log_g = np.log1p(-np.exp2(-RET_DECAY_OFFSET - hh))
    idx = np.arange(CHUNK, dtype=np.float64)
    diff = idx[:, None] - idx[None, :]
    dmat = np.where(diff[None] >= 0, np.exp(np.maximum(diff, 0.0)[None] * log_g[:, None, None]), 0.0)
    qdec = np.exp((idx + 1.0)[None, :, None] * log_g[:, None, None]) * np.ones((1, 1, CHUNK))
    kdec = np.exp((CHUNK - 1 - idx)[None, :, None] * log_g[:, None, None]) * np.ones((1, 1, CHUNK))
    half = CHUNK // 2
    inv_freq = (ROPE_BASE ** (-np.arange(half, dtype=np.float32) / half)).astype(np.float32)
    ang = (np.arange(t, dtype=np.float32)[:, None] * inv_freq[None, :]).astype(np.float32).astype(np.float64)
    cos2 = np.concatenate([np.cos(ang), np.cos(ang)], axis=1)
    sin2 = np.concatenate([-np.sin(ang), np.sin(ang)], axis=1)
    return tuple(jnp.asarray(v, F32) for v in (dmat, qdec, kdec, cos2, sin2))


def _rot(x, c, s):
    return x * c + pltpu.roll(x, CHUNK // 2, 1) * s


def _unrot(dx, c, s):
    return dx * c + pltpu.roll(dx * s, CHUNK // 2, 1)


def _dot(a, b, contract="nn"):
    return lax.dot_general(a, b, _DIMS[contract], preferred_element_type=F32)


def _ret_fwd(proj, tables):
    t = proj.shape[0]
    nch = t // CHUNK
    dmat, qdec, kdec, cos2, sin2 = tables

    def body(qk_ref, v_ref, c_ref, s_ref, dm_ref, qd_ref, kd_ref, o_ref, st_ref, state):
        @pl.when(pl.program_id(0) == 0)
        def _():
            state[...] = jnp.zeros_like(state)

        c, s = c_ref[...], s_ref[...]
        for h in range(RET_HEADS):
            q = _rot(qk_ref[:, 128 * h:128 * (h + 1)], c, s)
            k = _rot(qk_ref[:, 512 + 128 * h:512 + 128 * (h + 1)], c, s) * RET_SCALE
            vb = v_ref[:, 256 * h:256 * (h + 1)].astype(BF16)
            st = state[h]
            st_ref[h] = st
            sc = _dot(q.astype(BF16), k.astype(BF16), "nt") * dm_ref[h]
            o = _dot(sc.astype(BF16), vb)
            o += _dot((q * qd_ref[h]).astype(BF16), st.astype(BF16))
            o_ref[:, 256 * h:256 * (h + 1)] = o
            kv = _dot((k * kd_ref[h]).astype(BF16), vb, "tn")
            state[h] = qd_ref[h, CHUNK - 1:CHUNK, 0:1] * st + kv

    tab = pl.BlockSpec((RET_HEADS, CHUNK, CHUNK), lambda n: (0, 0, 0))
    pos = pl.BlockSpec((CHUNK, CHUNK), lambda n: (n, 0))
    return _pcall(
        body, name="ret_fwd", grid=(nch,),
        in_specs=[pl.BlockSpec((CHUNK, 1024), lambda n: (n, 0)), pl.BlockSpec((CHUNK, 1024), lambda n: (n, 1)),
                  pos, pos, tab, tab, tab],
        out_specs=[pl.BlockSpec((CHUNK, 1024), lambda n: (n, 0)),
                   pl.BlockSpec((RET_HEADS, None, 128, 256), lambda n: (0, n, 0, 0))],
        out_shape=[jax.ShapeDtypeStruct((t, 1024), F32), jax.ShapeDtypeStruct((RET_HEADS, nch, 128, 256), F32)],
        scratch_shapes=[pltpu.VMEM((RET_HEADS, 128, 256), F32)],
        compiler_params=_params(dimension_semantics=("arbitrary",)))(
            proj, proj, cos2, sin2, dmat, qdec, kdec)


def _ret_bwd(dproj, proj, states, do, tables):
    t = proj.shape[0]
    nch = t // CHUNK
    dmat, qdec, kdec, cos2, sin2 = tables

    def body(dp_in, qk_ref, v_ref, do_ref, st_ref, c_ref, s_ref, dm_ref, qd_ref, kd_ref, dp_ref, rst):
        del dp_in
        @pl.when(pl.program_id(0) == 0)
        def _():
            rst[...] = jnp.zeros_like(rst)

        c, s = c_ref[...], s_ref[...]
        for h in range(RET_HEADS):
            q = _rot(qk_ref[:, 128 * h:128 * (h + 1)], c, s)
            k = _rot(qk_ref[:, 512 + 128 * h:512 + 128 * (h + 1)], c, s) * RET_SCALE
            qb, kb = q.astype(BF16), k.astype(BF16)
            vb = v_ref[:, 256 * h:256 * (h + 1)].astype(BF16)
            dob = do_ref[:, 256 * h:256 * (h + 1)].astype(BF16)
            pb = st_ref[h].astype(BF16)
            r = rst[h]
            rb = r.astype(BF16)
            dm, qd, kd = dm_ref[h], qd_ref[h], kd_ref[h]
            sb = (_dot(qb, kb, "nt") * dm).astype(BF16)
            dsb = (_dot(dob, vb, "nt") * dm).astype(BF16)
            dq = _dot(dsb, kb) + _dot(dob, pb, "nt") * qd
            dk = _dot(dsb, qb, "tn") + _dot(vb, rb, "nt") * kd
            dv = _dot(sb, dob, "tn") + _dot((k * kd).astype(BF16), rb)
            rst[h] = _dot((q * qd).astype(BF16), dob, "tn") + qd[CHUNK - 1:CHUNK, 0:1] * r
            dp_ref[:, 128 * h:128 * (h + 1)] = _unrot(dq, c, s).astype(BF16)
            dp_ref[:, 512 + 128 * h:512 + 128 * (h + 1)] = (_unrot(dk, c, s) * RET_SCALE).astype(BF16)
            dp_ref[:, 1024 + 256 * h:1024 + 256 * (h + 1)] = dv.astype(BF16)

    rev = lambda n: nch - 1 - n
    tab = pl.BlockSpec((RET_HEADS, CHUNK, CHUNK), lambda n: (0, 0, 0))
    pos = pl.BlockSpec((CHUNK, CHUNK), lambda n: (rev(n), 0))
    row = pl.BlockSpec((CHUNK, 1024), lambda n: (rev(n), 0))
    return _pcall(
        body, name="ret_bwd", grid=(nch,),
        in_specs=[pl.BlockSpec(memory_space=pl.ANY), row, pl.BlockSpec((CHUNK, 1024), lambda n: (rev(n), 1)), row,
                  pl.BlockSpec((RET_HEADS, None, 128, 256), lambda n: (0, rev(n), 0, 0)),
                  pos, pos, tab, tab, tab],
        out_specs=pl.BlockSpec((CHUNK, 2048), lambda n: (rev(n), 0)),
        out_shape=jax.ShapeDtypeStruct((t, 5120), BF16),
        scratch_shapes=[pltpu.VMEM((RET_HEADS, 128, 256), F32)], input_output_aliases={0: 0},
        compiler_params=_params(dimension_semantics=("arbitrary",)))(
            dproj, proj, proj, do, states, cos2, sin2, dmat, qdec, kdec)


def _gn_gate_fwd(o, proj, gn_g):
    t = o.shape[0]
    tr = _rows(t)

    def body(o_ref, g_ref, w_ref, c_ref):
        for h in range(RET_HEADS):
            sl = slice(256 * h, 256 * (h + 1))
            x = o_ref[:, sl]
            mu = jnp.mean(x, axis=-1, keepdims=True)
            xc = x - mu
            rstd = lax.rsqrt(jnp.mean(xc * xc, axis=-1, keepdims=True) + EPS)
            g = g_ref[:, sl]
            c_ref[:, sl] = (g * _sigmoid(g) * (xc * rstd * w_ref[:, sl])).astype(BF16)

    return _pcall(body, name="gn_gate_fwd", grid=(t // tr,),
                  in_specs=[pl.BlockSpec((tr, 1024), lambda i: (i, 0)),
                            pl.BlockSpec((tr, 1024), lambda i: (i, 2)),
                            pl.BlockSpec((1, 1024), lambda i: (0, 0))],
                  out_specs=pl.BlockSpec((tr, 1024), lambda i: (i, 0)),
                  out_shape=jax.ShapeDtypeStruct((t, 2048), BF16))(o, proj, gn_g)


def _gn_gate_bwd(dcat, o, proj, gn_g):
    t = o.shape[0]
    tr = _rows(t)

    def body(d_ref, o_ref, g_ref, w_ref, do_ref, dg_ref, dw_ref):
        @pl.when(pl.program_id(0) == 0)
        def _():
            dw_ref[...] = jnp.zeros_like(dw_ref)

        for h in range(RET_HEADS):
            sl = slice(256 * h, 256 * (h + 1))
            x = o_ref[:, sl]
            mu = jnp.mean(x, axis=-1, keepdims=True)
            xc = x - mu
            rstd = lax.rsqrt(jnp.mean(xc * xc, axis=-1, keepdims=True) + EPS)
            xh = xc * rstd
            w = w_ref[:, sl]
            g = g_ref[:, sl]
            sg = _sigmoid(g)
            d = d_ref[:, sl]
            don = d * (g * sg)
            dg_ref[:, sl] = (d * (xh * w) * (sg * (1.0 + g * (1.0 - sg)))).astype(BF16)
            dw_ref[:, sl] += jnp.sum(don * xh, axis=0, keepdims=True)
            dxh = don * w
            m1 = jnp.mean(dxh, axis=-1, keepdims=True)
            m2 = jnp.mean(dxh * xh, axis=-1, keepdims=True)
            do_ref[:, sl] = rstd * (dxh - m1 - xh * m2)

    row = pl.BlockSpec((tr, 1024), lambda i: (i, 0))
    vec = pl.BlockSpec((1, 1024), lambda i: (0, 0))
    return _pcall(body, name="gn_gate_bwd", grid=(t // tr,),
                  in_specs=[row, row, pl.BlockSpec((tr, 1024), lambda i: (i, 2)), vec],
                  out_specs=[row, pl.BlockSpec((tr, 1024), lambda i: (i, 2)), vec],
                  out_shape=[jax.ShapeDtypeStruct((t, 1024), F32), jax.ShapeDtypeStruct((t, 5120), BF16),
                             jax.ShapeDtypeStruct((1, 1024), F32)])(dcat, o, proj, gn_g)


def _row_ids(i, tr):
    return i * tr + lax.broadcasted_iota(jnp.int32, (tr, 1), 0)


SH_ROWS = HALO - 8


def _shifted_copies(xs, sh, tr):
    for b in range(1, 8):
        sh[b - 1] = xs[pl.ds(b, tr + SH_ROWS), :]


def _shifted(xs, sh, off, tr):
    a, b = divmod(off, 8)
    return xs[pl.ds(8 * a, tr), :] if b == 0 else sh[b - 1, pl.ds(8 * a, tr), :]


def _conv_fwd(cat, proj, conv_w, conv_b, ln_g, ln_b):
    t = proj.shape[0]
    tr = _rows(t)
    hb = tr // HALO

    def body(cat_in, ua_ref, ug_ref, pa_ref, pg_ref, w_ref, b_ref, lg_ref, lb_ref, c_ref, hd_ref, y_ref, xs, sh):
        del cat_in
        i = pl.program_id(0)
        hdn = ua_ref[...] * _sigmoid(ug_ref[...])
        hd_ref[...] = hdn
        prev = pa_ref[...] * _sigmoid(pg_ref[...])
        xs[0:HALO, :] = jnp.where(i > 0, prev, 0.0)
        xs[HALO:HALO + tr, :] = hdn
        _shifted_copies(xs, sh, tr)
        acc = jnp.zeros((tr, 1024), F32) + b_ref[...]
        for w in range(CONV_WIDTH):
            acc += w_ref[w:w + 1, :] * _shifted(xs, sh, HALO - (CONV_WIDTH - 1) + w, tr)
        y_ref[...] = acc
        mu = jnp.mean(acc, axis=-1, keepdims=True)
        yc = acc - mu
        rstd = lax.rsqrt(jnp.mean(yc * yc, axis=-1, keepdims=True) + EPS)
        yn = yc * rstd * lg_ref[...] + lb_ref[...]
        c = yn * _sigmoid(yn)
        c_ref[...] = jnp.where(_row_ids(i, tr) >= PAD_FRONT, c, 0.0).astype(BF16)

    row = pl.BlockSpec((tr, 1024), lambda i: (i, 0))
    vec = pl.BlockSpec((1, 1024), lambda i: (0, 0))
    halo = lambda col: pl.BlockSpec((HALO, 1024), lambda i: (jnp.maximum(i * hb - 1, 0), col))
    return _pcall(body, name="conv_fwd", grid=(t // tr,),
                  in_specs=[pl.BlockSpec(memory_space=pl.ANY),
                            pl.BlockSpec((tr, 1024), lambda i: (i, 3)), pl.BlockSpec((tr, 1024), lambda i: (i, 4)),
                            halo(3), halo(4), pl.BlockSpec((32, 1024), lambda i: (0, 0)), vec, vec, vec],
                  out_specs=[pl.BlockSpec((tr, 1024), lambda i: (i, 1)), row, row],
                  out_shape=[jax.ShapeDtypeStruct((t, 2048), BF16), jax.ShapeDtypeStruct((t, 1024), F32),
                             jax.ShapeDtypeStruct((t, 1024), F32)],
                  scratch_shapes=[pltpu.VMEM((tr + HALO, 1024), F32), pltpu.VMEM((7, tr + SH_ROWS, 1024), F32)],
                  input_output_aliases={0: 0}, compiler_params=_params())(
                      cat, proj, proj, proj, proj, conv_w, conv_b, ln_g, ln_b)


def _conv_bwd_ln(dcat, y, ln_g, ln_b):
    t = y.shape[0]
    tr = _rows(t)

    def body(d_ref, y_ref, lg_ref, lb_ref, dy_ref, dlg_ref, dlb_ref, dcb_ref):
        i = pl.program_id(0)

        @pl.when(i == 0)
        def _():
            dlg_ref[...] = jnp.zeros_like(dlg_ref)
            dlb_ref[...] = jnp.zeros_like(dlb_ref)
            dcb_ref[...] = jnp.zeros_like(dcb_ref)

        y = y_ref[...]
        mu = jnp.mean(y, axis=-1, keepdims=True)
        yc = y - mu
        rstd = lax.rsqrt(jnp.mean(yc * yc, axis=-1, keepdims=True) + EPS)
        xh = yc * rstd
        lg = lg_ref[...]
        yn = xh * lg + lb_ref[...]
        sg = _sigmoid(yn)
        dyn = jnp.where(_row_ids(i, tr) >= PAD_FRONT, d_ref[...] * (sg * (1.0 + yn * (1.0 - sg))), 0.0)
        dlg_ref[...] += jnp.sum(dyn * xh, axis=0, keepdims=True)
        dlb_ref[...] += jnp.sum(dyn, axis=0, keepdims=True)
        dxh = dyn * lg
        m1 = jnp.mean(dxh, axis=-1, keepdims=True)
        m2 = jnp.mean(dxh * xh, axis=-1, keepdims=True)
        dy = rstd * (dxh - m1 - xh * m2)
        dy_ref[...] = dy
        dcb_ref[...] += jnp.sum(dy, axis=0, keepdims=True)

    row = pl.BlockSpec((tr, 1024), lambda i: (i, 0))
    vec = pl.BlockSpec((1, 1024), lambda i: (0, 0))
    vshape = jax.ShapeDtypeStruct((1, 1024), F32)
    return _pcall(body, name="conv_bwd_ln", grid=(t // tr,),
                  in_specs=[pl.BlockSpec((tr, 1024), lambda i: (i, 1)), row, vec, vec],
                  out_specs=[row, vec, vec, vec],
                  out_shape=[jax.ShapeDtypeStruct((t, 1024), F32), vshape, vshape, vshape])(dcat, y, ln_g, ln_b)


def _conv_bwd_taps(dproj, dy, hdn, proj, conv_w):
    t = dy.shape[0]
    tr = _rows(t)
    hb = tr // HALO
    nt = t // tr

    def body(dp_in, dy_ref, nx_ref, hd_ref, ph_ref, ua_ref, ug_ref, w_ref, da_ref, dg_ref, dw_ref, xs, sh):
        del dp_in
        i = pl.program_id(0)

        @pl.when(i == 0)
        def _():
            dw_ref[...] = jnp.zeros_like(dw_ref)

        dy = dy_ref[...]
        xs[0:tr, :] = dy
        xs[tr:tr + HALO, :] = jnp.where(i < nt - 1, nx_ref[...], 0.0)
        _shifted_copies(xs, sh, tr)
        dh = jnp.zeros((tr, 1024), F32)
        for w in range(CONV_WIDTH):
            dh += w_ref[w:w + 1, :] * _shifted(xs, sh, CONV_WIDTH - 1 - w, tr)
        xs[0:HALO, :] = jnp.where(i > 0, ph_ref[...], 0.0)
        xs[HALO:HALO + tr, :] = hd_ref[...]
        _shifted_copies(xs, sh, tr)
        for w in range(CONV_WIDTH):
            dw_ref[w:w + 1, :] += jnp.sum(dy * _shifted(xs, sh, HALO - (CONV_WIDTH - 1) + w, tr), axis=0, keepdims=True)
        dh = jnp.where(_row_ids(i, tr) >= PAD_FRONT, dh, 0.0)
        sg = _sigmoid(ug_ref[...])
        da_ref[...] = (dh * sg).astype(BF16)
        dg_ref[...] = (dh * ua_ref[...] * sg * (1.0 - sg)).astype(BF16)

    row = pl.BlockSpec((tr, 1024), lambda i: (i, 0))
    return _pcall(body, name="conv_bwd_taps", grid=(nt,),
                  in_specs=[pl.BlockSpec(memory_space=pl.ANY),
                            row, pl.BlockSpec((HALO, 1024), lambda i: (jnp.minimum((i + 1) * hb, nt * hb - 1), 0)),
                            row, pl.BlockSpec((HALO, 1024), lambda i: (jnp.maximum(i * hb - 1, 0), 0)),
                            pl.BlockSpec((tr, 1024), lambda i: (i, 3)), pl.BlockSpec((tr, 1024), lambda i: (i, 4)),
                            pl.BlockSpec((32, 1024), lambda i: (0, 0))],
                  out_specs=[pl.BlockSpec((tr, 1024), lambda i: (i, 3)), row, pl.BlockSpec((32, 1024), lambda i: (0, 0))],
                  out_shape=[jax.ShapeDtypeStruct((t, 5120), BF16), jax.ShapeDtypeStruct((t, 1024), BF16),
                             jax.ShapeDtypeStruct((32, 1024), F32)],
                  scratch_shapes=[pltpu.VMEM((tr + HALO, 1024), F32), pltpu.VMEM((7, tr + SH_ROWS, 1024), F32)],
                  input_output_aliases={0: 0}, compiler_params=_params())(
                      dproj, dy, dy, hdn, hdn, proj, proj, conv_w)


NEG_BIG = -1e30


def _seg_tables(qb):
    j = np.arange(128)
    bd = (j[:, None] // 64 == j[None, :] // 64).astype(np.float32)
    ones = np.ones((128, 128), np.float32)
    later = np.concatenate([(j[:, None] >= j[None, :]).astype(np.float32), ones], axis=1)
    earlier = np.concatenate([(j[:, None] < j[None, :]).astype(np.float32), ones], axis=1)
    per = qb // CHUNK
    row = np.arange(qb)[:, None]
    pad = np.broadcast_to(j[None, :] < PAD_FRONT, (qb, 128))
    diag = [(g * CHUNK + j[None, :]) >= row for g in range(per)]
    masks = diag + [np.zeros((qb, 128), bool), pad, diag[0] | pad]
    bias = np.stack([np.where(m, NEG_BIG, 0.0) for m in masks]).astype(np.float32)
    dup = lambda m: np.concatenate([m, m], axis=0)
    return (jnp.asarray(bd, BF16), jnp.asarray(dup(later), BF16), jnp.asarray(dup(earlier), BF16),
            jnp.asarray(bias, F32))


def _split_dot(x, m):
    hi = x.astype(BF16)
    lo = (x - hi.astype(F32)).astype(BF16)
    return _dot(hi, m) + _dot(lo, m)


def _qk_norm_fwd(qkv, qg, kg, bd):
    t = qkv.shape[0]
    tr = _rows(t)
    nb = tr // CHUNK

    def body(q_ref, k_ref, v_ref, qg_ref, kg_ref, bd_ref, qo, kt, k2, vt, v2):
        bdm = bd_ref[...]
        lane = lax.broadcasted_iota(jnp.int32, (1, 128), 1)
        sub = lax.broadcasted_iota(jnp.int32, (128, 1), 0)

        def pair_layouts(x, t_ref, s_ref, hp, b):
            xt = x.T
            t_ref[hp, b] = jnp.concatenate([jnp.where(sub < 64, xt, 0.0), jnp.where(sub >= 64, xt, 0.0)],
                                           axis=1).astype(BF16)
            s_ref[hp, b] = jnp.concatenate([jnp.where(lane < 64, x, 0.0), jnp.where(lane >= 64, x, 0.0)],
                                           axis=0).astype(BF16)

        for hp in range(8):
            sl = slice(128 * hp, 128 * (hp + 1))
            x = q_ref[:, sl]
            r = lax.rsqrt(_split_dot(x * x, bdm) * (1.0 / 64) + EPS)
            qo[:, sl] = (x * r * (qg_ref[:, sl] * SB_SCALE)).astype(BF16)
            x = k_ref[:, sl]
            r = lax.rsqrt(_split_dot(x * x, bdm) * (1.0 / 64) + EPS)
            kn = x * r * kg_ref[:, sl]
            v = v_ref[:, sl]
            for b in range(nb):
                rows = slice(CHUNK * b, CHUNK * (b + 1))
                pair_layouts(kn[rows], kt, k2, hp, b)
                pair_layouts(v[rows], vt, v2, hp, b)

    col = lambda c: pl.BlockSpec((tr, 1024), lambda i: (i, c))
    vec = pl.BlockSpec((1, 1024), lambda i: (0, 0))
    wide = pl.BlockSpec((8, nb, 128, 256), lambda i: (0, i, 0, 0))
    tall = pl.BlockSpec((8, nb, 256, 128), lambda i: (0, i, 0, 0))
    wsh = jax.ShapeDtypeStruct((8, t // CHUNK, 128, 256), BF16)
    tsh = jax.ShapeDtypeStruct((8, t // CHUNK, 256, 128), BF16)
    return _pcall(body, name="qk_norm_fwd", grid=(t // tr,),
                  in_specs=[col(0), col(1), col(2), vec, vec, pl.BlockSpec((128, 128), lambda i: (0, 0))],
                  out_specs=[col(0), wide, tall, wide, tall],
                  out_shape=[jax.ShapeDtypeStruct((t, 1024), BF16), wsh, tsh, wsh, tsh])(qkv, qkv, qkv, qg, kg, bd)


def _qk_norm_bwd(qkv, dq, dk, dv, qg, kg, bd):
    t = qkv.shape[0]
    tr = _rows(t)

    def body(q_ref, k_ref, dq_ref, dk_ref, dv_ref, qg_ref, kg_ref, bd_ref, o_ref, dqg_ref, dkg_ref):
        @pl.when(pl.program_id(0) == 0)
        def _():
            dqg_ref[...] = jnp.zeros_like(dqg_ref)
            dkg_ref[...] = jnp.zeros_like(dkg_ref)

        bdm = bd_ref[...]
        for part, (src, d_ref, g_ref, dg_ref) in enumerate(((q_ref, dq_ref, qg_ref, dqg_ref),
                                                           (k_ref, dk_ref, kg_ref, dkg_ref))):
            for cix in range(8):
                sl = slice(128 * cix, 128 * (cix + 1))
                x = src[:, sl]
                d = d_ref[:, sl]
                r = lax.rsqrt(_split_dot(x * x, bdm) * (1.0 / 64) + EPS)
                u = d * g_ref[:, sl]
                m = _split_dot(u * x, bdm) * (1.0 / 64)
                o_ref[:, 1024 * part + 128 * cix:1024 * part + 128 * (cix + 1)] = (r * u - x * (r * r * r * m)).astype(BF16)
                dg_ref[:, sl] += jnp.sum(d * x * r, axis=0, keepdims=True)
        o_ref[:, 2048:3072] = dv_ref[...].astype(BF16)

    col = lambda c: pl.BlockSpec((tr, 1024), lambda i: (i, c))
    vec = pl.BlockSpec((1, 1024), lambda i: (0, 0))
    vsh = jax.ShapeDtypeStruct((1, 1024), F32)
    return _pcall(body, name="qk_norm_bwd", grid=(t // tr,),
                  in_specs=[col(0), col(1), col(0), col(0), col(0), vec, vec, pl.BlockSpec((128, 128), lambda i: (0, 0))],
                  out_specs=[pl.BlockSpec((tr, 3072), lambda i: (i, 0)), vec, vec],
                  out_shape=[jax.ShapeDtypeStruct((t, 3072), BF16), vsh, vsh])(qkv, qkv, dq, dk, dv, qg, kg, bd)


def _split2(x):
    hi = x.astype(BF16)
    lo = (x - hi.astype(F32)).astype(BF16)
    return jnp.concatenate([hi, lo], axis=1)


def _sb_scores(z, later_tab):
    e = jnp.exp(-jnp.abs(z))
    ope = 1.0 + e
    sp = jnp.maximum(z, 0.0) + jnp.log(ope)
    return e, ope, _dot(_split2(sp), later_tab)


def _sb_bias_index(i, kb, per):
    g = kb - i * per
    return jnp.where(kb == 0, jnp.where(i == 0, per + 2, per + 1), jnp.where(g >= 0, g, per))


def _sb_qb(t):
    return _tile(t, (384, 128))


def _sb_fwd(qh, kt, v2, later_tab, bias_tab):
    t = qh.shape[0]
    qb = _sb_qb(t)
    per = qb // CHUNK
    nkb_all = t // CHUNK

    def body(q_ref, kt_ref, v2_ref, tab_ref, bias_ref, o_ref, ws_ref, acc, carry, zbuf, wbuf, wsem):
        h, i = pl.program_id(0), pl.program_id(1)
        q = q_ref[...]
        acc[...] = jnp.zeros_like(acc)
        carry[...] = jnp.zeros_like(carry)
        nkb = (i + 1) * per
        save = lambda kb: pltpu.make_async_copy(wbuf.at[kb], ws_ref.at[h, i, kb], wsem.at[kb])

        for u in range(per):
            zbuf[u] = _dot(q, kt_ref[nkb - 1 - u])

        def step(s, _):
            top = nkb - 1 - per * s

            @pl.when(s > 0)
            def _():
                for u in range(per):
                    save(top + per - u).start()

            z2s = [zbuf[u] for u in range(per)]
            for u in range(per):
                zbuf[u] = _dot(q, kt_ref[jnp.maximum(top - per - u, 0)])
            cins = [carry[0], carry[1]]
            zs, cus = [], []
            for u in range(per):
                bias = bias_ref[_sb_bias_index(i, top - u, per)]
                zs.append([z2s[u][:, 128 * hh:128 * (hh + 1)] + bias for hh in range(2)])
                cus.append([_sb_scores(z, tab_ref[...])[2] for z in zs[u]])
            part = None
            for u in range(per):
                kb = top - u
                for hh in range(2):
                    cu = cus[u][hh]
                    wbuf[kb, :, 128 * hh:128 * (hh + 1)] = jnp.exp(zs[u][hh] - cu[:, :128] - cins[hh]).astype(BF16)
                    cins[hh] = cins[hh] + cu[:, 128:]
                d = _dot(wbuf[kb], v2_ref[kb])
                part = d if part is None else part + d
            carry[0], carry[1] = cins[0], cins[1]
            acc[...] += part
            return 0

        lax.fori_loop(0, nkb // per, step, 0)
        for u in range(per):
            save(per - 1 - u).start()
        o_ref[...] = acc[...]

        def drain(kb, _):
            save(kb).wait()
            return 0

        lax.fori_loop(0, nkb, drain, 0)

    blk = pl.BlockSpec((qb, 128), lambda h, i: (i, h))
    wide = pl.BlockSpec((None, nkb_all, 128, 256), lambda h, i: (h, 0, 0, 0))
    tall = pl.BlockSpec((None, nkb_all, 256, 128), lambda h, i: (h, 0, 0, 0))
    return _pcall(body, name="sb_fwd", grid=(8, t // qb),
                  in_specs=[blk, wide, tall, pl.BlockSpec((256, 256), lambda h, i: (0, 0)),
                            pl.BlockSpec((per + 3, qb, 128), lambda h, i: (0, 0, 0))],
                  out_specs=[blk, pl.BlockSpec(memory_space=pl.ANY)],
                  out_shape=[jax.ShapeDtypeStruct((t, 1024), F32),
                             jax.ShapeDtypeStruct((8, t // qb, nkb_all, qb, 256), BF16)],
                  scratch_shapes=[pltpu.VMEM((qb, 128), F32), pltpu.VMEM((2, qb, 128), F32),
                                  pltpu.VMEM((per, qb, 256), F32), pltpu.VMEM((nkb_all, qb, 256), BF16),
                                  pltpu.SemaphoreType.DMA((nkb_all,))],
                  compiler_params=_params(dimension_semantics=("parallel", "arbitrary")))(
                      qh, kt, v2, later_tab, bias_tab)


def _sb_bwd(qh, kt, k2, vt, wsave, do, earlier_tab, bias_tab):
    t = qh.shape[0]
    qb = _sb_qb(t)
    per = qb // CHUNK
    nkb_all = t // CHUNK

    zero_slot = nkb_all

    def body(q_ref, kt_ref, k2_ref, vt_ref, ws_ref, do_ref, etab_ref, bias_ref,
             dq_ref, dk_ref, dv_ref, acc, gcarry, zbuf, dwbuf, wbuf, wsem, dzbuf):
        h, i = pl.program_id(0), pl.program_id(1)

        @pl.when(i == 0)
        def _():
            dk_ref[...] = jnp.zeros_like(dk_ref)
            dv_ref[...] = jnp.zeros_like(dv_ref)

        nkb = (i + 1) * per
        fetch = lambda kb: pltpu.make_async_copy(ws_ref.at[h, i, kb], wbuf.at[kb], wsem.at[kb])

        def prefetch(kb, _):
            fetch(kb).start()
            return 0

        lax.fori_loop(0, nkb, prefetch, 0)
        q = q_ref[...]
        dob = do_ref[...].astype(BF16)
        lane = lax.broadcasted_iota(jnp.int32, (1, 128), 1)
        acc[...] = jnp.zeros_like(acc)
        gcarry[...] = jnp.zeros_like(gcarry)
        zbuf[...] = _dot(q, kt_ref[0])
        dwbuf[...] = _dot(dob, vt_ref[0])
        dzbuf[...] = jnp.zeros_like(dzbuf)
        wbuf[zero_slot] = jnp.zeros((qb, 256), BF16)

        def gradients(slot, kb):
            dz2 = dzbuf[...]
            acc[...] += _dot(dz2, k2_ref[kb])
            dk2 = _dot(dz2, q, "tn")
            dv2 = _dot(wbuf[slot], dob, "tn")
            dk_ref[kb] += jnp.where(lane < 64, dk2[:128], dk2[128:])
            dv_ref[kb] += jnp.where(lane < 64, dv2[:128], dv2[128:])

        def step(kb, _):
            fetch(kb).wait()
            bias = bias_ref[_sb_bias_index(i, kb, per)]
            z2 = zbuf[...]
            dw2 = dwbuf[...]
            nxt = jnp.minimum(kb + 1, nkb - 1)
            zbuf[...] = _dot(q, kt_ref[nxt])
            dwbuf[...] = _dot(dob, vt_ref[nxt])
            gradients(jnp.where(kb == 0, zero_slot, kb - 1), jnp.maximum(kb - 1, 0))
            w2 = wbuf[kb]
            for hh in range(2):
                sl = slice(128 * hh, 128 * (hh + 1))
                z = z2[:, sl] + bias
                e = jnp.exp(-jnp.abs(z))
                r = 1.0 / (1.0 + e)
                sig = jnp.where(z >= 0, r, e * r)
                gw = w2[:, sl].astype(F32) * dw2[:, sl]
                cu2 = _dot(_split2(gw), etab_ref[...])
                gin = gcarry[hh]
                gcarry[hh] = gin + cu2[:, 128:]
                dzbuf[:, sl] = (gw - sig * (gw + cu2[:, :128] + gin)).astype(BF16)
            return 0

        lax.fori_loop(0, nkb, step, 0)
        gradients(nkb - 1, nkb - 1)
        dq_ref[...] = acc[...] * SB_SCALE

    blk = pl.BlockSpec((qb, 128), lambda h, i: (i, h))
    wide = pl.BlockSpec((None, nkb_all, 128, 256), lambda h, i: (h, 0, 0, 0))
    tall = pl.BlockSpec((None, nkb_all, 256, 128), lambda h, i: (h, 0, 0, 0))
    tab = pl.BlockSpec((256, 256), lambda h, i: (0, 0))
    kv_out = pl.BlockSpec((nkb_all, 128, 128), lambda h, i: (0, 0, h))
    ksh = jax.ShapeDtypeStruct((nkb_all, 128, 1024), F32)
    dq, dk, dv = _pcall(
        body, name="sb_bwd", grid=(8, t // qb),
        in_specs=[blk, wide, tall, wide, pl.BlockSpec(memory_space=pl.ANY), blk, tab,
                  pl.BlockSpec((per + 3, qb, 128), lambda h, i: (0, 0, 0))],
        out_specs=[blk, kv_out, kv_out], out_shape=[jax.ShapeDtypeStruct((t, 1024), F32), ksh, ksh],
        scratch_shapes=[pltpu.VMEM((qb, 128), F32), pltpu.VMEM((2, qb, 128), F32),
                        pltpu.VMEM((qb, 256), F32), pltpu.VMEM((qb, 256), F32),
                        pltpu.VMEM((nkb_all + 1, qb, 256), BF16), pltpu.SemaphoreType.DMA((nkb_all,)),
                        pltpu.VMEM((qb, 256), BF16)],
        compiler_params=_params(dimension_semantics=("parallel", "arbitrary")))(
            qh, kt, k2, vt, wsave, do, earlier_tab, bias_tab)
    return dq, dk.reshape(t, 1024), dv.reshape(t, 1024)


def _adamw_math(w, g, m, v):
    m = ADAM_B1 * m + (1.0 - ADAM_B1) * g
    v = ADAM_B2 * v + (1.0 - ADAM_B2) * (g * g)
    m_hat = m / (1.0 - ADAM_B1 ** ADAM_STEP)
    v_hat = v / (1.0 - ADAM_B2 ** ADAM_STEP)
    delta = -ADAM_LR * (m_hat / (jnp.sqrt(v_hat) + ADAM_EPS) + ADAM_WD * w)
    return delta, m, v


def _adamw(name, w, owns, recvs, m, v, me):
    shape = w.shape
    c = shape[-1]
    nl = len(owns)
    w3, m3, v3 = (a.reshape(nl, -1, c) for a in (w, m, v))
    r = w3.shape[1]
    tr = _tile(r, (256, 128))
    owns = [o.reshape(N_DEV, r, c) for o in owns]
    recvs = [p.reshape(N_DEV - 1, r, c) for p in recvs]

    def body(me_ref, w_ref, *rest):
        own_refs, recv_refs = rest[:nl], rest[nl:2 * nl]
        m_ref, v_ref = rest[2 * nl:2 * nl + 2]
        g_out, d_out, m_out, v_out = rest[2 * nl + 2:]
        layer = pl.program_id(0)

        def grad(k):
            g = own_refs[k][...].astype(F32)
            for s in range(N_DEV - 1):
                g = g + recv_refs[k][s].astype(F32)
            return g

        g = grad(0)
        for k in range(1, nl):
            g = jnp.where(layer == k, grad(k), g)
        d, mn, vn = _adamw_math(w_ref[...], g, m_ref[...], v_ref[...])
        g_out[...] = g
        d_out[...] = d
        m_out[...] = mn
        v_out[...] = vn

    row = pl.BlockSpec((None, tr, c), lambda l, i, me_ref: (l, i, 0))
    own = lambda k: pl.BlockSpec((None, tr, c), lambda l, i, me_ref: (me_ref[0], jnp.where(l == k, i, 0), 0))
    rcv = lambda k: pl.BlockSpec((N_DEV - 1, tr, c), lambda l, i, me_ref: (0, jnp.where(l == k, i, 0), 0))
    osh = jax.ShapeDtypeStruct((nl, r, c), F32)
    grid_spec = pltpu.PrefetchScalarGridSpec(
        num_scalar_prefetch=1, grid=(nl, r // tr),
        in_specs=[row] + [own(k) for k in range(nl)] + [rcv(k) for k in range(nl)] + [row, row],
        out_specs=[row, row, row, row])
    outs = _pcall(body, name=name, grid_spec=grid_spec, out_shape=[osh, osh, osh, osh])(
        me.reshape(1), w3, *owns, *recvs, m3, v3)
    return tuple(o.reshape(shape) for o in outs)


def _place():
    x, y, c = lax.axis_index("x"), lax.axis_index("y"), lax.axis_index("c")
    return x, y, c, 4 * x + 2 * y + c


def _peer(x, y, c, rel):
    return (x ^ ((rel >> 2) & 1), y ^ ((rel >> 1) & 1), c ^ (rel & 1))


def _gather_first(now, later):
    n, k = len(now), len(later)

    def body(*refs):
        ins, outs = refs[:n + k], refs[n + k:2 * (n + k)]
        send, recv, lsem = refs[2 * (n + k):]
        x, y, c, me = _place()
        locals_ = []
        for w in range(n + k):
            local = pltpu.make_async_copy(ins[w], outs[w].at[me], lsem.at[w])
            local.start()
            locals_.append(local)
        def copy(w, src, slot, rel, to_rel):
            return pltpu.make_async_remote_copy(src_ref=src, dst_ref=outs[w].at[slot], send_sem=send.at[w, rel - 1],
                                                recv_sem=recv.at[w, rel - 1], device_id=_peer(x, y, c, to_rel),
                                                device_id_type=MESH)

        for w in range(n):
            for rel in (1, 2, 4, 6):
                copy(w, ins[w], me, rel, rel).start()
        for w in range(n):
            for rel in (2, 4, 6):
                copy(w, ins[w], me ^ rel, rel, rel).wait_recv()
                copy(w, outs[w].at[me ^ rel], me ^ rel, rel | 1, 1).start()
        for w in range(n):
            for rel in (1, 3, 5, 7):
                copy(w, ins[w], me ^ rel, rel, 1).wait_recv()
            for rel in range(1, N_DEV):
                copy(w, ins[w], me, rel, rel).wait_send()
        for local in locals_:
            local.wait()

    hbm = pl.BlockSpec(memory_space=pl.ANY)
    vmem = pl.BlockSpec(memory_space=pltpu.VMEM)
    arrays = list(now) + list(later)
    return _pcall(body, name="gather_first", in_specs=[vmem] * (n + k), out_specs=[hbm] * (n + k),
                  out_shape=[jax.ShapeDtypeStruct((N_DEV,) + a.shape, a.dtype) for a in arrays],
                  scratch_shapes=[pltpu.SemaphoreType.DMA((n, N_DEV - 1)), pltpu.SemaphoreType.DMA((n, N_DEV - 1)),
                                  pltpu.SemaphoreType.DMA((n + k,))],
                  compiler_params=_params(has_side_effects=True))(*arrays)


_HBM = pl.BlockSpec(memory_space=pltpu.HBM)
_SEM = pl.BlockSpec(memory_space=pltpu.SEMAPHORE)
_DATAFLOW = pltpu.SideEffectType.DATAFLOW_SIDE_EFFECTING


def _exchange_refs(srcs, lands, mode, me, rel, j):
    if mode == "gather":
        return srcs[j], lands[j].at[me], lands[j].at[me ^ rel]
    return srcs[j].at[me ^ rel], lands[j].at[rel - 1], lands[j].at[rel - 1]


def _exchange_start(name, srcs, lands, mode):
    n = len(srcs)

    def body(*refs):
        ins, lnd = refs[:n], refs[n:2 * n]
        send, recv = refs[2 * n], refs[2 * n + 1]
        token = refs[-1]
        x, y, c, me = _place()
        for j in range(n):
            for rel in range(1, N_DEV):
                src, dst, _ = _exchange_refs(ins, lnd, mode, me, rel, j)
                pltpu.make_async_remote_copy(src_ref=src, dst_ref=dst, send_sem=send.at[j * (N_DEV - 1) + rel - 1],
                                             recv_sem=recv.at[j * (N_DEV - 1) + rel - 1],
                                             device_id=_peer(x, y, c, rel), device_id_type=MESH).start()
        token[...] = jnp.zeros_like(token)

    sems = pltpu.SemaphoreType.DMA((n * (N_DEV - 1),))
    hbm_like = lambda a: pltpu.HBM(a.shape, a.dtype)
    outs = _pcall(body, name=name + "_start",
                  in_specs=[_HBM] * (2 * n), out_specs=[_SEM, _SEM] + [_HBM] * (2 * n) + [pl.BlockSpec(memory_space=pltpu.VMEM)],
                  out_shape=[sems, sems] + [hbm_like(a) for a in srcs] + [hbm_like(a) for a in lands]
                  + [jax.ShapeDtypeStruct((8, 128), F32)],
                  input_output_aliases={i: 2 + i for i in range(2 * n)},
                  compiler_params=pltpu.CompilerParams(has_side_effects=_DATAFLOW))(
                      *[pltpu.with_memory_space_constraint(a, pltpu.HBM) for a in list(srcs) + list(lands)])
    return dict(name=name, mode=mode, n=n, send=outs[0], recv=outs[1], srcs=outs[2:2 + n], lands=outs[2 + n:2 + 2 * n],
                token=outs[-1][0, 0])


def _exchange_wait(ex, after):
    n, mode = ex["n"], ex["mode"]

    def body(*refs):
        ins, lnd = refs[:n], refs[n:2 * n]
        send, recv = refs[2 * n], refs[2 * n + 1]
        x, y, c, me = _place()
        for j in range(n):
            for rel in range(1, N_DEV):
                src, dst, landed = _exchange_refs(ins, lnd, mode, me, rel, j)
                pltpu.make_async_remote_copy(src_ref=src, dst_ref=dst, send_sem=send.at[j * (N_DEV - 1) + rel - 1],
                                             recv_sem=recv.at[j * (N_DEV - 1) + rel - 1],
                                             device_id=_peer(x, y, c, rel), device_id_type=MESH).wait_send()
                pltpu.make_async_remote_copy(src_ref=src, dst_ref=landed, send_sem=send.at[j * (N_DEV - 1) + rel - 1],
                                             recv_sem=recv.at[j * (N_DEV - 1) + rel - 1],
                                             device_id=_peer(x, y, c, rel), device_id_type=MESH).wait_recv()

    hbm_like = lambda a: pltpu.HBM(a.shape, a.dtype)
    arrays = list(ex["srcs"]) + list(ex["lands"])
    outs = _pcall(body, name=ex["name"] + "_wait",
                  in_specs=[_HBM] * (2 * n) + [_SEM, _SEM, pl.BlockSpec(memory_space=pl.ANY)],
                  out_specs=[_HBM] * (2 * n), out_shape=[hbm_like(a) for a in arrays],
                  input_output_aliases={i: i for i in range(2 * n)},
                  compiler_params=pltpu.CompilerParams(has_side_effects=_DATAFLOW))(
                      *arrays, ex["send"], ex["recv"], after)
    return outs[:n], outs[n:]


def _scatter_start(name, grads):
    lands = [lax.empty((N_DEV - 1,) + g.shape[1:], g.dtype) for g in grads]
    return _exchange_start(name, grads, lands, "scatter")


ROW_MIX, ROW_MLP, ROW_CB, ROW_LG, ROW_LB, ROW_QN, ROW_KN, ROW_LOSS = 0, 2, 4, 5, 6, 7, 8, 9
ROW_META, ROW_CW, ROW_GN, SMALL_ROWS = 16, 32, 64, 72


def _sum_small(slots):
    def body(s_ref, o_ref):
        tot = s_ref[0]
        for s in range(1, N_DEV):
            tot = tot + s_ref[s]
        o_ref[...] = tot
        for row in (ROW_QN, ROW_KN):
            v = tot[row:row + 1, :]
            f = v[:, 0:128]
            for k in range(1, 8):
                f = f + v[:, 128 * k:128 * (k + 1)]
            o_ref[row:row + 1, 0:64] = f[:, 0:64] + f[:, 64:128]

    return _pcall(body, name="sum_small", out_shape=jax.ShapeDtypeStruct(slots.shape[1:], F32))(slots)


def _adamw_small(w, g, m, v):
    def body(w_ref, g_ref, m_ref, v_ref, d_out, m_out, v_out):
        d, mn, vn = _adamw_math(w_ref[...], g_ref[...], m_ref[...], v_ref[...])
        d_out[...] = d
        m_out[...] = mn
        v_out[...] = vn

    osh = jax.ShapeDtypeStruct(w.shape, F32)
    return _pcall(body, name="adamw_small", out_shape=[osh, osh, osh])(w, g, m, v)


def _local_step(h0, target, p, weight, emit):
    t = h0.shape[0]
    tables = _ret_tables(t)
    bd, later_tab, earlier_tab, bias_tab = _seg_tables(_sb_qb(t))
    row = lambda a, i: a[i:i + 1]

    hn_a = _rms_fwd("rms_mix0", h0, row(p["norm_mix_g"], 0))
    w_in = weight("w_in", hn_a)
    proj = _mm_cols("proj_in", hn_a, w_in, ())
    o_ret, states = _ret_fwd(proj, tables)
    gn_flat = p["gn_g"].reshape(1, 1024)
    cat = _gn_gate_fwd(o_ret, proj, gn_flat)
    cat, hdn, ycv = _conv_fwd(cat, proj, p["conv_w"], p["conv_b"], p["ln_g"], p["ln_b"])
    w_out = weight("w_out", cat)
    h1, hn_b = _mm_rows_norm("mix_out", cat, w_out, h0, row(p["norm_mlp_g"], 0))
    w1_0, w2_0 = weight("w1_0", hn_b), weight("w2_0", hn_b)
    a0, s0 = _mm_cols("mlp0_up", hn_b, w1_0, (), epi="relu2")
    h2, hn_c = _mm_rows_norm("mlp0_down", s0, w2_0, h1, row(p["norm_mix_g"], 1))

    w_qkv = weight("w_qkv", hn_c)
    qkv = _mm_cols("qkv", hn_c, w_qkv, ())
    qg = jnp.tile(p["qn_g"], (1, 16))
    kg = jnp.tile(p["kn_g"], (1, 16))
    qh, kt, k2, vt, v2 = _qk_norm_fwd(qkv, qg, kg, bd)
    o_sb, w_sb = _sb_fwd(qh, kt, v2, later_tab, bias_tab)
    w_o = weight("w_o", o_sb)
    h3, hn_d = _mm_rows_norm("attn_out", o_sb, w_o, h2, row(p["norm_mlp_g"], 1))
    w1_1, w2_1 = weight("w1_1", hn_d), weight("w2_1", hn_d)
    a1, s1 = _mm_cols("mlp1_up", hn_d, w1_1, (), epi="relu2")
    h4 = _mm_rows("mlp1_down", s1, w2_1, h3)

    dh, loss = _loss_bwd(h4, target)

    def mlp_bwd(tag, layer, w1, w2, dh, h_in, hn, a, s):
        da = _mm_rows_t(f"{tag}_dact", dh, w2, (), out_dtype=BF16, epi="drelu2", extra=a)
        dw2 = _wgrad_rows(f"{tag}_dw2", s, dh, 512)
        dw1 = _wgrad_cols(f"{tag}_dw1", hn, da, 512)
        tok = emit(tag, [dw1, dw2])
        dhn = _mm_cols_t(f"{tag}_dhn", da, w1)
        return _rms_bwd(f"{tag}_rms_bwd", dhn, h_in, row(p["norm_mlp_g"], layer) + tok, dh)

    dh, dg_mlp1 = mlp_bwd("mlp1", 1, w1_1, w2_1, dh, h3, hn_d, a1, s1)

    do_sb = _mm_rows_t("attn_dout", dh, w_o, ())
    dw_o = _wgrad_rows("attn_dwo", o_sb, dh, 128)
    dq, dk, dv = _sb_bwd(qh, kt, k2, vt, w_sb, do_sb, earlier_tab, bias_tab)
    dqkv, dqg, dkg = _qk_norm_bwd(qkv, dq, dk, dv, qg, kg, bd)
    dw_qkv = _wgrad_cols("qkv_dw", hn_c, dqkv, 384)
    tok = emit("attn", [dw_qkv, dw_o])
    dhn = _mm_cols_t("qkv_dhn", dqkv, w_qkv)
    dh, dg_mix1 = _rms_bwd("mix1_rms_bwd", dhn, h2, row(p["norm_mix_g"], 1) + tok, dh)

    dh, dg_mlp0 = mlp_bwd("mlp0", 0, w1_0, w2_0, dh, h1, hn_b, a0, s0)

    dcat = _mm_rows_t("mix_dcat", dh, w_out, ())
    dw_out = _wgrad_rows("mix_dwout", cat, dh, 256)
    tok = emit("mix0_out", [dw_out])
    do_ret, dproj, dgn = _gn_gate_bwd(dcat, o_ret, proj, gn_flat + tok)
    dproj = _ret_bwd(dproj, proj, states, do_ret, tables)
    dy, dlg, dlb, dcb = _conv_bwd_ln(dcat, ycv, p["ln_g"], p["ln_b"])
    dproj, dug, dcw = _conv_bwd_taps(dproj, dy, hdn, proj, p["conv_w"])
    dproj = lax.dynamic_update_slice(dproj, dug, (0, 4096))
    dw_in = _wgrad_cols("proj_dw", hn_a, dproj, 640)
    tok = emit("mix0", [dw_in])
    dhn = _mm_cols_t("proj_dhn", dproj, w_in)
    dh, dg_mix0 = _rms_bwd("mix0_rms_bwd", dhn, h0, row(p["norm_mix_g"], 0) + tok, dh)

    rid = lax.broadcasted_iota(jnp.int32, (16, 1), 0)
    loss_row = jnp.broadcast_to(loss[0:1, 0:1], (1, D_MODEL))
    vecs = sum(jnp.where(rid == k, v, 0.0)
               for k, v in enumerate((dg_mix0, dg_mix1, dg_mlp0, dg_mlp1, dcb, dlg, dlb, dqg, dkg, loss_row)))
    small = jnp.concatenate([vecs, dh[PAD_FRONT:TOK0], dcw, jnp.where(rid[:8] == 0, dgn, 0.0)], axis=0)
    return dh[TOK0:], small


_SMALL_NAMES = ("meta", "norm_mix_g", "norm_mlp_g", "even_ret_gn_g", "even_conv_w", "even_conv_b",
                "even_conv_ln_g", "even_conv_ln_b", "odd_q_norm_g", "odd_k_norm_g")
_BIG_NAMES = ("even_w_in", "even_w_out", "odd_w_qkv", "odd_w_o", "mlp_w1", "mlp_w2")
_ORDER = ("meta", "norm_mix_g", "norm_mlp_g", "even_w_in", "even_ret_gn_g", "even_conv_w", "even_conv_b",
          "even_conv_ln_g", "even_conv_ln_b", "even_w_out", "odd_w_qkv", "odd_q_norm_g", "odd_k_norm_g",
          "odd_w_o", "mlp_w1", "mlp_w2")


def _pack128(a):
    flat = a.reshape(-1)
    n = flat.shape[0]
    rows = -(-n // 128)
    rows8 = -(-rows // 8) * 8
    return jnp.pad(flat, (0, rows8 * 128 - n)).reshape(rows8, 128)


def kernel(x, meta, norm_mix_g, norm_mlp_g, even_w_in, even_ret_gn_g, even_conv_w, even_conv_b, even_conv_ln_g, even_conv_ln_b, even_w_out, odd_w_qkv, odd_q_norm_g, odd_k_norm_g, odd_w_o, mlp_w1, mlp_w2, loss_target, m_meta, m_norm_mix_g, m_norm_mlp_g, m_even_w_in, m_even_ret_gn_g, m_even_conv_w, m_even_conv_b, m_even_conv_ln_g, m_even_conv_ln_b, m_even_w_out, m_odd_w_qkv, m_odd_q_norm_g, m_odd_k_norm_g, m_odd_w_o, m_mlp_w1, m_mlp_w2, v_meta, v_norm_mix_g, v_norm_mlp_g, v_even_w_in, v_even_ret_gn_g, v_even_conv_w, v_even_conv_b, v_even_conv_ln_g, v_even_conv_ln_b, v_even_w_out, v_odd_w_qkv, v_odd_q_norm_g, v_odd_k_norm_g, v_odd_w_o, v_mlp_w1, v_mlp_w2):
    w = dict(meta=meta, norm_mix_g=norm_mix_g, norm_mlp_g=norm_mlp_g, even_w_in=even_w_in,
             even_ret_gn_g=even_ret_gn_g, even_conv_w=even_conv_w, even_conv_b=even_conv_b,
             even_conv_ln_g=even_conv_ln_g, even_conv_ln_b=even_conv_ln_b, even_w_out=even_w_out,
             odd_w_qkv=odd_w_qkv, odd_q_norm_g=odd_q_norm_g, odd_k_norm_g=odd_k_norm_g, odd_w_o=odd_w_o,
             mlp_w1=mlp_w1, mlp_w2=mlp_w2)
    mom = dict(meta=m_meta, norm_mix_g=m_norm_mix_g, norm_mlp_g=m_norm_mlp_g, even_w_in=m_even_w_in,
               even_ret_gn_g=m_even_ret_gn_g, even_conv_w=m_even_conv_w, even_conv_b=m_even_conv_b,
               even_conv_ln_g=m_even_conv_ln_g, even_conv_ln_b=m_even_conv_ln_b, even_w_out=m_even_w_out,
               odd_w_qkv=m_odd_w_qkv, odd_q_norm_g=m_odd_q_norm_g, odd_k_norm_g=m_odd_k_norm_g, odd_w_o=m_odd_w_o,
               mlp_w1=m_mlp_w1, mlp_w2=m_mlp_w2)
    var = dict(meta=v_meta, norm_mix_g=v_norm_mix_g, norm_mlp_g=v_norm_mlp_g, even_w_in=v_even_w_in,
               even_ret_gn_g=v_even_ret_gn_g, even_conv_w=v_even_conv_w, even_conv_b=v_even_conv_b,
               even_conv_ln_g=v_even_conv_ln_g, even_conv_ln_b=v_even_conv_ln_b, even_w_out=v_even_w_out,
               odd_w_qkv=v_odd_w_qkv, odd_q_norm_g=v_odd_q_norm_g, odd_k_norm_g=v_odd_k_norm_g, odd_w_o=v_odd_w_o,
               mlp_w1=v_mlp_w1, mlp_w2=v_mlp_w2)
    me = 4 * lax.axis_index("x") + 2 * lax.axis_index("y") + lax.axis_index("c")

    small_in = jnp.concatenate([meta, jnp.pad(even_conv_w[0], ((0, 1), (0, 0))),
                                jnp.pad(even_ret_gn_g[0], ((0, 4), (0, 96)))], axis=0)
    b16 = lambda a: a.astype(BF16)
    later_src = dict(w_out=b16(even_w_out[0]), w1_0=b16(mlp_w1[0]), w2_0=b16(mlp_w2[0]),
                     w_qkv=b16(odd_w_qkv[0]), w_o=b16(odd_w_o[0]), w1_1=b16(mlp_w1[1]), w2_1=b16(mlp_w2[1]))
    landed = _gather_first([b16(even_w_in[0]), small_in], list(later_src.values()))
    g_in, g_small = landed[0], landed[1]
    own_slot = dict(zip(later_src, landed[2:]))
    groups = (("gather_l0", ("w_out", "w1_0", "w2_0")), ("gather_attn", ("w_qkv", "w_o")),
              ("gather_l1", ("w1_1", "w2_1")))
    pending = {}
    gather_tok = jnp.zeros((), F32)
    for gname, names in groups:
        ex = _exchange_start(gname, [later_src[n] for n in names], [own_slot[n] for n in names], "gather")
        gather_tok = gather_tok + ex["token"]
        for n in names:
            pending[n] = (ex, names)
    arrived = dict(w_in=g_in)

    def weight(name, after):
        if name not in arrived:
            ex, names = pending[name]
            arrived.update(zip(names, _exchange_wait(ex, after)[1]))
        return arrived[name]

    cols = lambda a: jnp.transpose(a, (1, 0, 2)).reshape(a.shape[1], -1)
    p = dict(norm_mix_g=norm_mix_g + gather_tok, norm_mlp_g=norm_mlp_g, conv_b=even_conv_b, ln_g=even_conv_ln_g,
             ln_b=even_conv_ln_b, qn_g=odd_q_norm_g, kn_g=odd_k_norm_g,
             gn_g=cols(g_small[:, 48:52, :32]),
             conv_w=jnp.pad(cols(g_small[:, 16:47]), ((0, 1), (0, 0))))
    meta_full = cols(g_small[:, 0:16])

    scatters = {}

    def emit(tag, grads):
        scatters[tag] = _scatter_start("scatter_" + tag, grads)
        return scatters[tag]["token"]

    h0 = jnp.concatenate([jnp.zeros((PAD_FRONT, D_MODEL), F32), meta_full, x[0]], axis=0)
    grad_x, small_part = _local_step(h0, loss_target[0], p, weight, emit)

    out = {}
    got = {}

    def update(names, terms, after):
        for tag in {t for name in names for t, _ in terms[name]} - set(got):
            got[tag] = _exchange_wait(scatters[tag], after)
        for name in names:
            owns, recvs = zip(*[(got[t][0][j], got[t][1][j]) for t, j in terms[name]])
            out[name] = _adamw("adamw_" + name, w[name], list(owns), list(recvs), mom[name], var[name], me)

    terms = dict(even_w_in=[("mix0", 0)], even_w_out=[("mix0_out", 0)], odd_w_qkv=[("attn", 0)], odd_w_o=[("attn", 1)],
                 mlp_w1=[("mlp0", 0), ("mlp1", 0)], mlp_w2=[("mlp0", 1), ("mlp1", 1)])
    small_ex = _exchange_start("small", [small_part], [lax.empty((N_DEV,) + small_part.shape, F32)], "gather")
    update(("mlp_w1", "mlp_w2", "odd_w_qkv", "odd_w_o", "even_w_out"), terms, grad_x)
    update(("even_w_in",), terms, out["even_w_out"][1])
    (own_part,), (slots,) = _exchange_wait(small_ex, out["even_w_in"][1])
    tot = _sum_small(lax.dynamic_update_slice(slots, own_part[None], (me, 0, 0)))
    loss = tot[ROW_LOSS, 0]

    shard_cols = lambda a, width: lax.dynamic_slice_in_dim(a, me * width, width, axis=1)
    one = lambda r: tot[r:r + 1]
    small_g = dict(
        norm_mix_g=tot[ROW_MIX:ROW_MIX + 2], norm_mlp_g=tot[ROW_MLP:ROW_MLP + 2],
        even_conv_b=one(ROW_CB), even_conv_ln_g=one(ROW_LG), even_conv_ln_b=one(ROW_LB),
        odd_q_norm_g=one(ROW_QN)[:, :64], odd_k_norm_g=one(ROW_KN)[:, :64],
        meta=shard_cols(tot[ROW_META:ROW_META + N_META], 128),
        even_conv_w=shard_cols(tot[ROW_CW:ROW_CW + CONV_WIDTH], 128)[None],
        even_ret_gn_g=shard_cols(tot[ROW_GN].reshape(4, 256), 32)[None])
    packs = {n: (_pack128(w[n]), _pack128(small_g[n]), _pack128(mom[n]), _pack128(var[n])) for n in _SMALL_NAMES}
    cat4 = [jnp.concatenate([packs[n][i] for n in _SMALL_NAMES], axis=0) for i in range(4)]
    d_s, m_s, v_s = _adamw_small(*cat4)
    r0 = 0
    for n in _SMALL_NAMES:
        rows = packs[n][0].shape[0]
        size = w[n].size
        take = lambda a: a[r0:r0 + rows].reshape(-1)[:size].reshape(w[n].shape)
        out[n] = (small_g[n].reshape(w[n].shape), take(d_s), take(m_s), take(v_s))
        r0 += rows

    res = [loss, grad_x[None]]
    for i in range(4):
        res.extend(out[n][i] for n in _ORDER)
    return tuple(res)
```

```python
import functools

import numpy as np
import jax
import jax.numpy as jnp
from jax import lax
from jax.experimental import pallas as pl
from jax.experimental.pallas import tpu as pltpu

F32 = jnp.float32
BF16 = jnp.bfloat16

D_MODEL = 1024
N_META = 16
CHUNK = 128
PAD_FRONT = 112
TOK0 = PAD_FRONT + N_META
EPS = 1e-6
N_DEV = 8
RET_HEADS = 4
RET_DECAY_OFFSET = 5.0
ROPE_BASE = 10000.0
CONV_WIDTH = 31
HALO = 32
SB_SCALE = 64 ** -0.5
RET_SCALE = 128 ** -0.5
ADAM_LR, ADAM_B1, ADAM_B2, ADAM_EPS, ADAM_WD, ADAM_STEP = 0.001, 0.9, 0.999, 1e-08, 0.01, 10
VMEM_LIMIT = 56 * 1024 * 1024
MESH = pl.DeviceIdType.MESH


def _pcall(body, **kw):
    return pl.pallas_call(body, **kw)


def _params(**kw):
    return pltpu.CompilerParams(vmem_limit_bytes=VMEM_LIMIT, **kw)


def _tile(n, cands):
    for c in cands:
        if n % c == 0:
            return c
    raise ValueError(f"no tile for {n} in {cands}")


def _sigmoid(x):
    return 1.0 / (1.0 + jnp.exp(-x))


_DIMS = {
    "nn": (((1,), (0,)), ((), ())),
    "nt": (((1,), (1,)), ((), ())),
    "tn": (((0,), (0,)), ((), ())),
}


def _matmul(name, a, b, *, grid, a_spec, b_spec, o_spec, out_shape, contract, acc_shape,
            epi="plain", extra=None, extra_spec=None):
    nk = grid[2]
    dims = _DIMS[contract]
    n_in = 3 if extra is not None else 2
    n_out = 2 if epi == "relu2" else 1

    def body(*refs):
        a_ref, b_ref = refs[0], refs[1]
        e_ref = refs[2] if extra is not None else None
        outs = refs[n_in:n_in + n_out]
        acc = refs[-1]
        k = pl.program_id(2)
        part = lax.dot_general(a_ref[...].astype(BF16), b_ref[...].astype(BF16), dims, preferred_element_type=F32)
        if nk > 1:
            @pl.when(k == 0)
            def _():
                acc[...] = jnp.zeros_like(acc)

            acc[...] += part

        @pl.when(k == nk - 1)
        def _():
            r = acc[...] if nk > 1 else part
            if epi == "plain":
                outs[0][...] = r.astype(outs[0].dtype)
            elif epi == "residual":
                outs[0][...] = (r + e_ref[...]).astype(outs[0].dtype)
            elif epi == "relu2":
                outs[0][...] = r
                rr = jnp.maximum(r, 0.0)
                outs[1][...] = (rr * rr).astype(BF16)
            elif epi == "drelu2":
                outs[0][...] = (r * (2.0 * jnp.maximum(e_ref[...], 0.0))).astype(outs[0].dtype)

    in_specs = [a_spec, b_spec] + ([extra_spec] if extra is not None else [])
    args = (a, b) + ((extra,) if extra is not None else ())
    if n_out == 2:
        out_specs = [o_spec, o_spec]
    else:
        out_specs = o_spec
    return _pcall(body, name=name, grid=grid, in_specs=in_specs, out_specs=out_specs,
                  out_shape=out_shape, scratch_shapes=[pltpu.VMEM(acc_shape, F32)],
                  compiler_params=_params(dimension_semantics=("parallel", "parallel", "arbitrary")))(*args)


def _tm(t):
    return _tile(t, (1408, 768, 384, 128))


def _mm_cols(name, a, wb, lead, out_dtype=F32, epi="plain"):
    t, kdim = a.shape
    n = wb.shape[-1]
    tm, tk = _tm(t), _tile(kdim, (1024, 512))
    nl = len(lead)
    b_spec = pl.BlockSpec((None,) * (1 + nl) + (tk, n), lambda i, j, k: (j,) + lead + (k, 0))
    o_spec = pl.BlockSpec((tm, n), lambda i, j, k: (i, j))
    if epi == "relu2":
        out_shape = [jax.ShapeDtypeStruct((t, N_DEV * n), F32), jax.ShapeDtypeStruct((t, N_DEV * n), BF16)]
    else:
        out_shape = jax.ShapeDtypeStruct((t, N_DEV * n), out_dtype)
    return _matmul(name, a, wb, grid=(t // tm, N_DEV, kdim // tk),
                   a_spec=pl.BlockSpec((tm, tk), lambda i, j, k: (i, k)), b_spec=b_spec, o_spec=o_spec,
                   out_shape=out_shape, contract="nn", acc_shape=(tm, n), epi=epi)


def _tm_deep(t, kdim):
    return _tm(t) if kdim <= 2048 else _tile(t, (704, 384, 128))


def _mm_cols_t(name, a, wb):
    t = a.shape[0]
    nb, kdim, n = wb.shape
    tm, tn = _tm_deep(t, nb * n), _tile(kdim, (512,))

    def body(a_ref, b_ref, o_ref):
        acc = _dot(a_ref[:, 0:n].astype(BF16), b_ref[0], "nt")
        for j in range(1, nb):
            acc = acc + _dot(a_ref[:, j * n:(j + 1) * n].astype(BF16), b_ref[j], "nt")
        o_ref[...] = acc

    return _pcall(body, name=name, grid=(t // tm, kdim // tn),
                  in_specs=[pl.BlockSpec((tm, nb * n), lambda i, j: (i, 0)),
                            pl.BlockSpec((nb, tn, n), lambda i, j: (0, j, 0))],
                  out_specs=pl.BlockSpec((tm, tn), lambda i, j: (i, j)),
                  out_shape=jax.ShapeDtypeStruct((t, kdim), F32),
                  compiler_params=_params(dimension_semantics=("parallel", "parallel")))(a, wb)


def _mm_rows(name, a, wb, residual):
    t = a.shape[0]
    nb, r, n = wb.shape
    tm, tn = _tm_deep(t, nb * r), _tile(n, (512,))

    def body(a_ref, b_ref, r_ref, o_ref):
        o_ref[...] = r_ref[...] + _dot(a_ref[...].astype(BF16), b_ref[...].reshape(nb * r, tn))

    o_spec = pl.BlockSpec((tm, tn), lambda i, j: (i, j))
    return _pcall(body, name=name, grid=(t // tm, n // tn),
                  in_specs=[pl.BlockSpec((tm, nb * r), lambda i, j: (i, 0)),
                            pl.BlockSpec((nb, r, tn), lambda i, j: (0, 0, j)), o_spec],
                  out_specs=o_spec, out_shape=jax.ShapeDtypeStruct((t, n), F32),
                  compiler_params=_params(dimension_semantics=("parallel", "parallel")))(a, wb, residual)


def _mm_rows_t(name, a, wb, lead, out_dtype=F32, epi="plain", extra=None):
    t, n = a.shape
    r = wb.shape[-2]
    tm, tk = _tm(t), _tile(n, (1024,))
    nl = len(lead)
    b_spec = pl.BlockSpec((None,) * (1 + nl) + (r, tk), lambda i, j, k: (j,) + lead + (0, k))
    o_spec = pl.BlockSpec((tm, r), lambda i, j, k: (i, j))
    return _matmul(name, a, wb, grid=(t // tm, N_DEV, n // tk),
                   a_spec=pl.BlockSpec((tm, tk), lambda i, j, k: (i, k)), b_spec=b_spec, o_spec=o_spec,
                   out_shape=jax.ShapeDtypeStruct((t, N_DEV * r), out_dtype), contract="nt",
                   acc_shape=(tm, r), epi=epi, extra=extra, extra_spec=o_spec if extra is not None else None)


def _mm_rows_norm(name, a, wb, residual, g):
    t = a.shape[0]
    nb, r, n = wb.shape
    tm = _tile(t, (704, 384, 128))

    def body(a_ref, b_ref, r_ref, g_ref, h_ref, hn_ref):
        h = r_ref[...] + _dot(a_ref[...].astype(BF16), b_ref[...].reshape(nb * r, n))
        h_ref[...] = h
        hn_ref[...] = (h * lax.rsqrt(jnp.mean(h * h, axis=-1, keepdims=True) + EPS) * g_ref[...]).astype(BF16)

    row = pl.BlockSpec((tm, n), lambda i: (i, 0))
    return _pcall(body, name=name, grid=(t // tm,),
                  in_specs=[pl.BlockSpec((tm, nb * r), lambda i: (i, 0)), pl.BlockSpec((nb, r, n), lambda i: (0, 0, 0)),
                            row, pl.BlockSpec((1, n), lambda i: (0, 0))],
                  out_specs=[row, row],
                  out_shape=[jax.ShapeDtypeStruct((t, n), F32), jax.ShapeDtypeStruct((t, n), BF16)],
                  compiler_params=_params(dimension_semantics=("parallel",)))(a, wb, residual, g)


def _wgrad_cols(name, x, dy, n):
    t, kdim = x.shape
    tk = _tm(t)
    return _matmul(name, x, dy, grid=(1, N_DEV, t // tk),
                   a_spec=pl.BlockSpec((tk, kdim), lambda i, j, k: (k, 0)),
                   b_spec=pl.BlockSpec((tk, n), lambda i, j, k: (k, j)),
                   o_spec=pl.BlockSpec((None, kdim, n), lambda i, j, k: (j, 0, 0)),
                   out_shape=jax.ShapeDtypeStruct((N_DEV, kdim, n), BF16), contract="tn", acc_shape=(kdim, n))


def _wgrad_rows(name, x, dy, r):
    t = x.shape[0]
    n = dy.shape[1]
    tk, tn = _tm(t), _tile(n, (512,))
    tm = min(N_DEV * r, 1024)
    out = _matmul(name, x, dy, grid=(N_DEV * r // tm, n // tn, t // tk),
                  a_spec=pl.BlockSpec((tk, tm), lambda i, j, k: (k, i)),
                  b_spec=pl.BlockSpec((tk, tn), lambda i, j, k: (k, j)),
                  o_spec=pl.BlockSpec((tm, tn), lambda i, j, k: (i, j)),
                  out_shape=jax.ShapeDtypeStruct((N_DEV * r, n), BF16), contract="tn", acc_shape=(tm, tn))
    return out.reshape(N_DEV, r, n)


def _rows(t):
    return _tile(t, (384, 128))


def _rms_fwd(name, h, g):
    t = h.shape[0]
    tr = _rows(t)

    def body(h_ref, g_ref, o_ref):
        x = h_ref[...]
        r = lax.rsqrt(jnp.mean(x * x, axis=-1, keepdims=True) + EPS)
        o_ref[...] = (x * r * g_ref[...]).astype(BF16)

    row = pl.BlockSpec((tr, D_MODEL), lambda i: (i, 0))
    vec = pl.BlockSpec((1, D_MODEL), lambda i: (0, 0))
    return _pcall(body, name=name, grid=(t // tr,), in_specs=[row, vec], out_specs=row,
                  out_shape=jax.ShapeDtypeStruct((t, D_MODEL), BF16))(h, g)


def _rms_bwd(name, dhn, h, g, dres):
    t = h.shape[0]
    tr = _rows(t)

    def body(d_ref, h_ref, g_ref, r_ref, o_ref, dg_ref):
        @pl.when(pl.program_id(0) == 0)
        def _():
            dg_ref[...] = jnp.zeros_like(dg_ref)

        x = h_ref[...]
        d = d_ref[...]
        r = lax.rsqrt(jnp.mean(x * x, axis=-1, keepdims=True) + EPS)
        u = d * g_ref[...]
        m = jnp.mean(u * x, axis=-1, keepdims=True)
        o_ref[...] = r_ref[...] + r * u - x * (r * r * r * m)
        dg_ref[...] += jnp.sum(d * x * r, axis=0, keepdims=True)

    row = pl.BlockSpec((tr, D_MODEL), lambda i: (i, 0))
    vec = pl.BlockSpec((1, D_MODEL), lambda i: (0, 0))
    return _pcall(body, name=name, grid=(t // tr,), in_specs=[row, row, vec, row], out_specs=[row, vec],
                  out_shape=[jax.ShapeDtypeStruct((t, D_MODEL), F32), jax.ShapeDtypeStruct((1, D_MODEL), F32)])(
                      dhn, h, g, dres)


def _loss_bwd(h, target):
    t = h.shape[0]
    nb = t // CHUNK

    def body(h_ref, t_ref, d_ref, l_ref):
        i = pl.program_id(0)

        @pl.when(i == 0)
        def _():
            d_ref[...] = jnp.zeros_like(d_ref)
            l_ref[...] = jnp.zeros_like(l_ref)

        @pl.when(i > 0)
        def _():
            diff = h_ref[...] - t_ref[...]
            d_ref[...] = diff * (1.0 / D_MODEL)
            l_ref[...] += jnp.sum(diff * diff) * (0.5 / D_MODEL)

    return _pcall(body, name="loss_bwd", grid=(nb,),
                  in_specs=[pl.BlockSpec((CHUNK, D_MODEL), lambda i: (i, 0)),
                            pl.BlockSpec((CHUNK, D_MODEL), lambda i: (jnp.maximum(i - 1, 0), 0))],
                  out_specs=[pl.BlockSpec((CHUNK, D_MODEL), lambda i: (i, 0)),
                             pl.BlockSpec((8, 128), lambda i: (0, 0))],
                  out_shape=[jax.ShapeDtypeStruct((t, D_MODEL), F32), jax.ShapeDtypeStruct((8, 128), F32)])(h, target)


def _ret_tables(t):
    hh = np.arange(RET_HEADS, dtype=np.float64)
    log_g = np.log1p(-np.exp2(-RET_DECAY_OFFSET - hh))
    idx = np.arange(CHUNK, dtype=np.float64)
    diff = idx[:, None] - idx[None, :]
    dmat = np.where(diff[None] >= 0, np.exp(np.maximum(diff, 0.0)[None] * log_g[:, None, None]), 0.0)
    qdec = np.exp((idx + 1.0)[None, :, None] * log_g[:, None, None]) * np.ones((1, 1, CHUNK))
    kdec = np.exp((CHUNK - 1 - idx)[None, :, None] * log_g[:, None, None]) * np.ones((1, 1, CHUNK))
    half = CHUNK // 2
    inv_freq = (ROPE_BASE ** (-np.arange(half, dtype=np.float32) / half)).astype(np.float32)
    ang = (np.arange(t, dtype=np.float32)[:, None] * inv_freq[None, :]).astype(np.float32).astype(np.float64)
    cos2 = np.concatenate([np.cos(ang), np.cos(ang)], axis=1)
    sin2 = np.concatenate([-np.sin(ang), np.sin(ang)], axis=1)
    return tuple(jnp.asarray(v, F32) for v in (dmat, qdec, kdec, cos2, sin2))


def _rot(x, c, s):
    return x * c + pltpu.roll(x, CHUNK // 2, 1) * s


def _unrot(dx, c, s):
    return dx * c + pltpu.roll(dx * s, CHUNK // 2, 1)


def _dot(a, b, contract="nn"):
    return lax.dot_general(a, b, _DIMS[contract], preferred_element_type=F32)


def _ret_fwd(proj, tables):
    t = proj.shape[0]
    nch = t // CHUNK
    dmat, qdec, kdec, cos2, sin2 = tables

    def body(qk_ref, v_ref, c_ref, s_ref, dm_ref, qd_ref, kd_ref, o_ref, st_ref, state):
        @pl.when(pl.program_id(0) == 0)
        def _():
            state[...] = jnp.zeros_like(state)

        c, s = c_ref[...], s_ref[...]
        for h in range(RET_HEADS):
            q = _rot(qk_ref[:, 128 * h:128 * (h + 1)], c, s)
            k = _rot(qk_ref[:, 512 + 128 * h:512 + 128 * (h + 1)], c, s) * RET_SCALE
            vb = v_ref[:, 256 * h:256 * (h + 1)].astype(BF16)
            st = state[h]
            st_ref[h] = st
            sc = _dot(q.astype(BF16), k.astype(BF16), "nt") * dm_ref[h]
            o = _dot(sc.astype(BF16), vb)
            o += _dot((q * qd_ref[h]).astype(BF16), st.astype(BF16))
            o_ref[:, 256 * h:256 * (h + 1)] = o
            kv = _dot((k * kd_ref[h]).astype(BF16), vb, "tn")
            state[h] = qd_ref[h, CHUNK - 1:CHUNK, 0:1] * st + kv

    tab = pl.BlockSpec((RET_HEADS, CHUNK, CHUNK), lambda n: (0, 0, 0))
    pos = pl.BlockSpec((CHUNK, CHUNK), lambda n: (n, 0))
    return _pcall(
        body, name="ret_fwd", grid=(nch,),
        in_specs=[pl.BlockSpec((CHUNK, 1024), lambda n: (n, 0)), pl.BlockSpec((CHUNK, 1024), lambda n: (n, 1)),
                  pos, pos, tab, tab, tab],
        out_specs=[pl.BlockSpec((CHUNK, 1024), lambda n: (n, 0)),
                   pl.BlockSpec((RET_HEADS, None, 128, 256), lambda n: (0, n, 0, 0))],
        out_shape=[jax.ShapeDtypeStruct((t, 1024), F32), jax.ShapeDtypeStruct((RET_HEADS, nch, 128, 256), F32)],
        scratch_shapes=[pltpu.VMEM((RET_HEADS, 128, 256), F32)],
        compiler_params=_params(dimension_semantics=("arbitrary",)))(
            proj, proj, cos2, sin2, dmat, qdec, kdec)


def _ret_bwd(dproj, proj, states, do, tables):
    t = proj.shape[0]
    nch = t // CHUNK
    dmat, qdec, kdec, cos2, sin2 = tables

    def body(dp_in, qk_ref, v_ref, do_ref, st_ref, c_ref, s_ref, dm_ref, qd_ref, kd_ref, dp_ref, rst):
        del dp_in
        @pl.when(pl.program_id(0) == 0)
        def _():
            rst[...] = jnp.zeros_like(rst)

        c, s = c_ref[...], s_ref[...]
        for h in range(RET_HEADS):
            q = _rot(qk_ref[:, 128 * h:128 * (h + 1)], c, s)
            k = _rot(qk_ref[:, 512 + 128 * h:512 + 128 * (h + 1)], c, s) * RET_SCALE
            qb, kb = q.astype(BF16), k.astype(BF16)
            vb = v_ref[:, 256 * h:256 * (h + 1)].astype(BF16)
            dob = do_ref[:, 256 * h:256 * (h + 1)].astype(BF16)
            pb = st_ref[h].astype(BF16)
            r = rst[h]
            rb = r.astype(BF16)
            dm, qd, kd = dm_ref[h], qd_ref[h], kd_ref[h]
            sb = (_dot(qb, kb, "nt") * dm).astype(BF16)
            dsb = (_dot(dob, vb, "nt") * dm).astype(BF16)
            dq = _dot(dsb, kb) + _dot(dob, pb, "nt") * qd
            dk = _dot(dsb, qb, "tn") + _dot(vb, rb, "nt") * kd
            dv = _dot(sb, dob, "tn") + _dot((k * kd).astype(BF16), rb)
            rst[h] = _dot((q * qd).astype(BF16), dob, "tn") + qd[CHUNK - 1:CHUNK, 0:1] * r
            dp_ref[:, 128 * h:128 * (h + 1)] = _unrot(dq, c, s).astype(BF16)
            dp_ref[:, 512 + 128 * h:512 + 128 * (h + 1)] = (_unrot(dk, c, s) * RET_SCALE).astype(BF16)
            dp_ref[:, 1024 + 256 * h:1024 + 256 * (h + 1)] = dv.astype(BF16)

    rev = lambda n: nch - 1 - n
    tab = pl.BlockSpec((RET_HEADS, CHUNK, CHUNK), lambda n: (0, 0, 0))
    pos = pl.BlockSpec((CHUNK, CHUNK), lambda n: (rev(n), 0))
    row = pl.BlockSpec((CHUNK, 1024), lambda n: (rev(n), 0))
    return _pcall(
        body, name="ret_bwd", grid=(nch,),
        in_specs=[pl.BlockSpec(memory_space=pl.ANY), row, pl.BlockSpec((CHUNK, 1024), lambda n: (rev(n), 1)), row,
                  pl.BlockSpec((RET_HEADS, None, 128, 256), lambda n: (0, rev(n), 0, 0)),
                  pos, pos, tab, tab, tab],
        out_specs=pl.BlockSpec((CHUNK, 2048), lambda n: (rev(n), 0)),
        out_shape=jax.ShapeDtypeStruct((t, 5120), BF16),
        scratch_shapes=[pltpu.VMEM((RET_HEADS, 128, 256), F32)], input_output_aliases={0: 0},
        compiler_params=_params(dimension_semantics=("arbitrary",)))(
            dproj, proj, proj, do, states, cos2, sin2, dmat, qdec, kdec)


def _gn_gate_fwd(o, proj, gn_g):
    t = o.shape[0]
    tr = _rows(t)

    def body(o_ref, g_ref, w_ref, c_ref):
        for h in range(RET_HEADS):
            sl = slice(256 * h, 256 * (h + 1))
            x = o_ref[:, sl]
            mu = jnp.mean(x, axis=-1, keepdims=True)
            xc = x - mu
            rstd = lax.rsqrt(jnp.mean(xc * xc, axis=-1, keepdims=True) + EPS)
            g = g_ref[:, sl]
            c_ref[:, sl] = (g * _sigmoid(g) * (xc * rstd * w_ref[:, sl])).astype(BF16)

    return _pcall(body, name="gn_gate_fwd", grid=(t // tr,),
                  in_specs=[pl.BlockSpec((tr, 1024), lambda i: (i, 0)),
                            pl.BlockSpec((tr, 1024), lambda i: (i, 2)),
                            pl.BlockSpec((1, 1024), lambda i: (0, 0))],
                  out_specs=pl.BlockSpec((tr, 1024), lambda i: (i, 0)),
                  out_shape=jax.ShapeDtypeStruct((t, 2048), BF16))(o, proj, gn_g)


def _gn_gate_bwd(dcat, o, proj, gn_g):
    t = o.shape[0]
    tr = _rows(t)

    def body(d_ref, o_ref, g_ref, w_ref, do_ref, dg_ref, dw_ref):
        @pl.when(pl.program_id(0) == 0)
        def _():
            dw_ref[...] = jnp.zeros_like(dw_ref)

        for h in range(RET_HEADS):
            sl = slice(256 * h, 256 * (h + 1))
            x = o_ref[:, sl]
            mu = jnp.mean(x, axis=-1, keepdims=True)
            xc = x - mu
            rstd = lax.rsqrt(jnp.mean(xc * xc, axis=-1, keepdims=True) + EPS)
            xh = xc * rstd
            w = w_ref[:, sl]
            g = g_ref[:, sl]
            sg = _sigmoid(g)
            d = d_ref[:, sl]
            don = d * (g * sg)
            dg_ref[:, sl] = (d * (xh * w) * (sg * (1.0 + g * (1.0 - sg)))).astype(BF16)
            dw_ref[:, sl] += jnp.sum(don * xh, axis=0, keepdims=True)
            dxh = don * w
            m1 = jnp.mean(dxh, axis=-1, keepdims=True)
            m2 = jnp.mean(dxh * xh, axis=-1, keepdims=True)
            do_ref[:, sl] = rstd * (dxh - m1 - xh * m2)

    row = pl.BlockSpec((tr, 1024), lambda i: (i, 0))
    vec = pl.BlockSpec((1, 1024), lambda i: (0, 0))
    return _pcall(body, name="gn_gate_bwd", grid=(t // tr,),
                  in_specs=[row, row, pl.BlockSpec((tr, 1024), lambda i: (i, 2)), vec],
                  out_specs=[row, pl.BlockSpec((tr, 1024), lambda i: (i, 2)), vec],
                  out_shape=[jax.ShapeDtypeStruct((t, 1024), F32), jax.ShapeDtypeStruct((t, 5120), BF16),
                             jax.ShapeDtypeStruct((1, 1024), F32)])(dcat, o, proj, gn_g)


def _row_ids(i, tr):
    return i * tr + lax.broadcasted_iota(jnp.int32, (tr, 1), 0)


SH_ROWS = HALO - 8


def _shifted_copies(xs, sh, tr):
    for b in range(1, 8):
        sh[b - 1] = xs[pl.ds(b, tr + SH_ROWS), :]


def _shifted(xs, sh, off, tr):
    a, b = divmod(off, 8)
    return xs[pl.ds(8 * a, tr), :] if b == 0 else sh[b - 1, pl.ds(8 * a, tr), :]


def _conv_fwd(cat, proj, conv_w, conv_b, ln_g, ln_b):
    t = proj.shape[0]
    tr = _rows(t)
    hb = tr // HALO

    def body(cat_in, ua_ref, ug_ref, pa_ref, pg_ref, w_ref, b_ref, lg_ref, lb_ref, c_ref, hd_ref, y_ref, xs, sh):
        del cat_in
        i = pl.program_id(0)
        hdn = ua_ref[...] * _sigmoid(ug_ref[...])
        hd_ref[...] = hdn
        prev = pa_ref[...] * _sigmoid(pg_ref[...])
        xs[0:HALO, :] = jnp.where(i > 0, prev, 0.0)
        xs[HALO:HALO + tr, :] = hdn
        _shifted_copies(xs, sh, tr)
        acc = jnp.zeros((tr, 1024), F32) + b_ref[...]
        for w in range(CONV_WIDTH):
            acc += w_ref[w:w + 1, :] * _shifted(xs, sh, HALO - (CONV_WIDTH - 1) + w, tr)
        y_ref[...] = acc
        mu = jnp.mean(acc, axis=-1, keepdims=True)
        yc = acc - mu
        rstd = lax.rsqrt(jnp.mean(yc * yc, axis=-1, keepdims=True) + EPS)
        yn = yc * rstd * lg_ref[...] + lb_ref[...]
        c = yn * _sigmoid(yn)
        c_ref[...] = jnp.where(_row_ids(i, tr) >= PAD_FRONT, c, 0.0).astype(BF16)

    row = pl.BlockSpec((tr, 1024), lambda i: (i, 0))
    vec = pl.BlockSpec((1, 1024), lambda i: (0, 0))
    halo = lambda col: pl.BlockSpec((HALO, 1024), lambda i: (jnp.maximum(i * hb - 1, 0), col))
    return _pcall(body, name="conv_fwd", grid=(t // tr,),
                  in_specs=[pl.BlockSpec(memory_space=pl.ANY),
                            pl.BlockSpec((tr, 1024), lambda i: (i, 3)), pl.BlockSpec((tr, 1024), lambda i: (i, 4)),
                            halo(3), halo(4), pl.BlockSpec((32, 1024), lambda i: (0, 0)), vec, vec, vec],
                  out_specs=[pl.BlockSpec((tr, 1024), lambda i: (i, 1)), row, row],
                  out_shape=[jax.ShapeDtypeStruct((t, 2048), BF16), jax.ShapeDtypeStruct((t, 1024), F32),
                             jax.ShapeDtypeStruct((t, 1024), F32)],
                  scratch_shapes=[pltpu.VMEM((tr + HALO, 1024), F32), pltpu.VMEM((7, tr + SH_ROWS, 1024), F32)],
                  input_output_aliases={0: 0}, compiler_params=_params())(
                      cat, proj, proj, proj, proj, conv_w, conv_b, ln_g, ln_b)


def _conv_bwd_ln(dcat, y, ln_g, ln_b):
    t = y.shape[0]
    tr = _rows(t)

    def body(d_ref, y_ref, lg_ref, lb_ref, dy_ref, dlg_ref, dlb_ref, dcb_ref):
        i = pl.program_id(0)

        @pl.when(i == 0)
        def _():
            dlg_ref[...] = jnp.zeros_like(dlg_ref)
            dlb_ref[...] = jnp.zeros_like(dlb_ref)
            dcb_ref[...] = jnp.zeros_like(dcb_ref)

        y = y_ref[...]
        mu = jnp.mean(y, axis=-1, keepdims=True)
        yc = y - mu
        rstd = lax.rsqrt(jnp.mean(yc * yc, axis=-1, keepdims=True) + EPS)
        xh = yc * rstd
        lg = lg_ref[...]
        yn = xh * lg + lb_ref[...]
        sg = _sigmoid(yn)
        dyn = jnp.where(_row_ids(i, tr) >= PAD_FRONT, d_ref[...] * (sg * (1.0 + yn * (1.0 - sg))), 0.0)
        dlg_ref[...] += jnp.sum(dyn * xh, axis=0, keepdims=True)
        dlb_ref[...] += jnp.sum(dyn, axis=0, keepdims=True)
        dxh = dyn * lg
        m1 = jnp.mean(dxh, axis=-1, keepdims=True)
        m2 = jnp.mean(dxh * xh, axis=-1, keepdims=True)
        dy = rstd * (dxh - m1 - xh * m2)
        dy_ref[...] = dy
        dcb_ref[...] += jnp.sum(dy, axis=0, keepdims=True)

    row = pl.BlockSpec((tr, 1024), lambda i: (i, 0))
    vec = pl.BlockSpec((1, 1024), lambda i: (0, 0))
    vshape = jax.ShapeDtypeStruct((1, 1024), F32)
    return _pcall(body, name="conv_bwd_ln", grid=(t // tr,),
                  in_specs=[pl.BlockSpec((tr, 1024), lambda i: (i, 1)), row, vec, vec],
                  out_specs=[row, vec, vec, vec],
                  out_shape=[jax.ShapeDtypeStruct((t, 1024), F32), vshape, vshape, vshape])(dcat, y, ln_g, ln_b)


def _conv_bwd_taps(dproj, dy, hdn, proj, conv_w):
    t = dy.shape[0]
    tr = _rows(t)
    hb = tr // HALO
    nt = t // tr

    def body(dp_in, dy_ref, nx_ref, hd_ref, ph_ref, ua_ref, ug_ref, w_ref, da_ref, dg_ref, dw_ref, xs, sh):
        del dp_in
        i = pl.program_id(0)

        @pl.when(i == 0)
        def _():
            dw_ref[...] = jnp.zeros_like(dw_ref)

        dy = dy_ref[...]
        xs[0:tr, :] = dy
        xs[tr:tr + HALO, :] = jnp.where(i < nt - 1, nx_ref[...], 0.0)
        _shifted_copies(xs, sh, tr)
        dh = jnp.zeros((tr, 1024), F32)
        for w in range(CONV_WIDTH):
            dh += w_ref[w:w + 1, :] * _shifted(xs, sh, CONV_WIDTH - 1 - w, tr)
        xs[0:HALO, :] = jnp.where(i > 0, ph_ref[...], 0.0)
        xs[HALO:HALO + tr, :] = hd_ref[...]
        _shifted_copies(xs, sh, tr)
        for w in range(CONV_WIDTH):
            dw_ref[w:w + 1, :] += jnp.sum(dy * _shifted(xs, sh, HALO - (CONV_WIDTH - 1) + w, tr), axis=0, keepdims=True)
        dh = jnp.where(_row_ids(i, tr) >= PAD_FRONT, dh, 0.0)
        sg = _sigmoid(ug_ref[...])
        da_ref[...] = (dh * sg).astype(BF16)
        dg_ref[...] = (dh * ua_ref[...] * sg * (1.0 - sg)).astype(BF16)

    row = pl.BlockSpec((tr, 1024), lambda i: (i, 0))
    return _pcall(body, name="conv_bwd_taps", grid=(nt,),
                  in_specs=[pl.BlockSpec(memory_space=pl.ANY),
                            row, pl.BlockSpec((HALO, 1024), lambda i: (jnp.minimum((i + 1) * hb, nt * hb - 1), 0)),
                            row, pl.BlockSpec((HALO, 1024), lambda i: (jnp.maximum(i * hb - 1, 0), 0)),
                            pl.BlockSpec((tr, 1024), lambda i: (i, 3)), pl.BlockSpec((tr, 1024), lambda i: (i, 4)),
                            pl.BlockSpec((32, 1024), lambda i: (0, 0))],
                  out_specs=[pl.BlockSpec((tr, 1024), lambda i: (i, 3)), row, pl.BlockSpec((32, 1024), lambda i: (0, 0))],
                  out_shape=[jax.ShapeDtypeStruct((t, 5120), BF16), jax.ShapeDtypeStruct((t, 1024), BF16),
                             jax.ShapeDtypeStruct((32, 1024), F32)],
                  scratch_shapes=[pltpu.VMEM((tr + HALO, 1024), F32), pltpu.VMEM((7, tr + SH_ROWS, 1024), F32)],
                  input_output_aliases={0: 0}, compiler_params=_params())(
                      dproj, dy, dy, hdn, hdn, proj, proj, conv_w)


NEG_BIG = -1e30


def _seg_tables(qb):
    j = np.arange(128)
    bd = (j[:, None] // 64 == j[None, :] // 64).astype(np.float32)
    ones = np.ones((128, 128), np.float32)
    later = np.concatenate([(j[:, None] >= j[None, :]).astype(np.float32), ones], axis=1)
    earlier = np.concatenate([(j[:, None] < j[None, :]).astype(np.float32), ones], axis=1)
    per = qb // CHUNK
    row = np.arange(qb)[:, None]
    pad = np.broadcast_to(j[None, :] < PAD_FRONT, (qb, 128))
    diag = [(g * CHUNK + j[None, :]) >= row for g in range(per)]
    masks = diag + [np.zeros((qb, 128), bool), pad, diag[0] | pad]
    bias = np.stack([np.where(m, NEG_BIG, 0.0) for m in masks]).astype(np.float32)
    dup = lambda m: np.concatenate([m, m], axis=0)
    return (jnp.asarray(bd, BF16), jnp.asarray(dup(later), BF16), jnp.asarray(dup(earlier), BF16),
            jnp.asarray(bias, F32))


def _split_dot(x, m):
    hi = x.astype(BF16)
    lo = (x - hi.astype(F32)).astype(BF16)
    return _dot(hi, m) + _dot(lo, m)


def _qk_norm_fwd(qkv, qg, kg, bd):
    t = qkv.shape[0]
    tr = _rows(t)
    nb = tr // CHUNK

    def body(q_ref, k_ref, v_ref, qg_ref, kg_ref, bd_ref, qo, kt, k2, vt, v2):
        bdm = bd_ref[...]
        lane = lax.broadcasted_iota(jnp.int32, (1, 128), 1)
        sub = lax.broadcasted_iota(jnp.int32, (128, 1), 0)

        def pair_layouts(x, t_ref, s_ref, hp, b):
            xt = x.T
            t_ref[hp, b] = jnp.concatenate([jnp.where(sub < 64, xt, 0.0), jnp.where(sub >= 64, xt, 0.0)],
                                           axis=1).astype(BF16)
            s_ref[hp, b] = jnp.concatenate([jnp.where(lane < 64, x, 0.0), jnp.where(lane >= 64, x, 0.0)],
                                           axis=0).astype(BF16)

        for hp in range(8):
            sl = slice(128 * hp, 128 * (hp + 1))
            x = q_ref[:, sl]
            r = lax.rsqrt(_split_dot(x * x, bdm) * (1.0 / 64) + EPS)
            qo[:, sl] = (x * r * (qg_ref[:, sl] * SB_SCALE)).astype(BF16)
            x = k_ref[:, sl]
            r = lax.rsqrt(_split_dot(x * x, bdm) * (1.0 / 64) + EPS)
            kn = x * r * kg_ref[:, sl]
            v = v_ref[:, sl]
            for b in range(nb):
                rows = slice(CHUNK * b, CHUNK * (b + 1))
                pair_layouts(kn[rows], kt, k2, hp, b)
                pair_layouts(v[rows], vt, v2, hp, b)

    col = lambda c: pl.BlockSpec((tr, 1024), lambda i: (i, c))
    vec = pl.BlockSpec((1, 1024), lambda i: (0, 0))
    wide = pl.BlockSpec((8, nb, 128, 256), lambda i: (0, i, 0, 0))
    tall = pl.BlockSpec((8, nb, 256, 128), lambda i: (0, i, 0, 0))
    wsh = jax.ShapeDtypeStruct((8, t // CHUNK, 128, 256), BF16)
    tsh = jax.ShapeDtypeStruct((8, t // CHUNK, 256, 128), BF16)
    return _pcall(body, name="qk_norm_fwd", grid=(t // tr,),
                  in_specs=[col(0), col(1), col(2), vec, vec, pl.BlockSpec((128, 128), lambda i: (0, 0))],
                  out_specs=[col(0), wide, tall, wide, tall],
                  out_shape=[jax.ShapeDtypeStruct((t, 1024), BF16), wsh, tsh, wsh, tsh])(qkv, qkv, qkv, qg, kg, bd)


def _qk_norm_bwd(qkv, dq, dk, dv, qg, kg, bd):
    t = qkv.shape[0]
    tr = _rows(t)

    def body(q_ref, k_ref, dq_ref, dk_ref, dv_ref, qg_ref, kg_ref, bd_ref, o_ref, dqg_ref, dkg_ref):
        @pl.when(pl.program_id(0) == 0)
        def _():
            dqg_ref[...] = jnp.zeros_like(dqg_ref)
            dkg_ref[...] = jnp.zeros_like(dkg_ref)

        bdm = bd_ref[...]
        for part, (src, d_ref, g_ref, dg_ref) in enumerate(((q_ref, dq_ref, qg_ref, dqg_ref),
                                                           (k_ref, dk_ref, kg_ref, dkg_ref))):
            for cix in range(8):
                sl = slice(128 * cix, 128 * (cix + 1))
                x = src[:, sl]
                d = d_ref[:, sl]
                r = lax.rsqrt(_split_dot(x * x, bdm) * (1.0 / 64) + EPS)
                u = d * g_ref[:, sl]
                m = _split_dot(u * x, bdm) * (1.0 / 64)
                o_ref[:, 1024 * part + 128 * cix:1024 * part + 128 * (cix + 1)] = (r * u - x * (r * r * r * m)).astype(BF16)
                dg_ref[:, sl] += jnp.sum(d * x * r, axis=0, keepdims=True)
        o_ref[:, 2048:3072] = dv_ref[...].astype(BF16)

    col = lambda c: pl.BlockSpec((tr, 1024), lambda i: (i, c))
    vec = pl.BlockSpec((1, 1024), lambda i: (0, 0))
    vsh = jax.ShapeDtypeStruct((1, 1024), F32)
    return _pcall(body, name="qk_norm_bwd", grid=(t // tr,),
                  in_specs=[col(0), col(1), col(0), col(0), col(0), vec, vec, pl.BlockSpec((128, 128), lambda i: (0, 0))],
                  out_specs=[pl.BlockSpec((tr, 3072), lambda i: (i, 0)), vec, vec],
                  out_shape=[jax.ShapeDtypeStruct((t, 3072), BF16), vsh, vsh])(qkv, qkv, dq, dk, dv, qg, kg, bd)


def _split2(x):
    hi = x.astype(BF16)
    lo = (x - hi.astype(F32)).astype(BF16)
    return jnp.concatenate([hi, lo], axis=1)


def _sb_scores(z, later_tab):
    e = jnp.exp(-jnp.abs(z))
    ope = 1.0 + e
    sp = jnp.maximum(z, 0.0) + jnp.log(ope)
    return e, ope, _dot(_split2(sp), later_tab)


def _sb_bias_index(i, kb, per):
    g = kb - i * per
    return jnp.where(kb == 0, jnp.where(i == 0, per + 2, per + 1), jnp.where(g >= 0, g, per))


def _sb_qb(t):
    return _tile(t, (384, 128))


def _sb_fwd(qh, kt, v2, later_tab, bias_tab):
    t = qh.shape[0]
    qb = _sb_qb(t)
    per = qb // CHUNK
    nkb_all = t // CHUNK

    nq = t // qb

    def body(q_ref, kt_ref, v2_ref, tab_ref, bias_ref, o_ref, ws_ref, acc, carry, zbuf, wbuf, wsem):
        h, i = pl.program_id(0), pl.program_id(1)
        n = h * nq + i
        p = n & 1
        q = q_ref[...]
        acc[...] = jnp.zeros_like(acc)
        carry[...] = jnp.zeros_like(carry)
        nkb = (i + 1) * per
        save = lambda kb: pltpu.make_async_copy(wbuf.at[p, kb], ws_ref.at[h, i, kb], wsem.at[p, kb])

        def drain(step, par):
            hs, is_ = step // nq, step % nq

            def one(kb, _):
                pltpu.make_async_copy(wbuf.at[par, kb], ws_ref.at[hs, is_, kb], wsem.at[par, kb]).wait()
                return 0

            lax.fori_loop(0, (is_ + 1) * per, one, 0)

        @pl.when(n >= 2)
        def _():
            drain(n - 2, p)

        for u in range(per):
            zbuf[u] = _dot(q, kt_ref[nkb - 1 - u])

        def step(s, _):
            top = nkb - 1 - per * s

            @pl.when(s > 0)
            def _():
                for u in range(per):
                    save(top + per - u).start()

            z2s = [zbuf[u] for u in range(per)]
            for u in range(per):
                zbuf[u] = _dot(q, kt_ref[jnp.maximum(top - per - u, 0)])
            cins = [carry[0], carry[1]]
            zs, cus = [], []
            for u in range(per):
                bias = bias_ref[_sb_bias_index(i, top - u, per)]
                zs.append([z2s[u][:, 128 * hh:128 * (hh + 1)] + bias for hh in range(2)])
                cus.append([_sb_scores(z, tab_ref[...])[2] for z in zs[u]])
            part = None
            for u in range(per):
                kb = top - u
                for hh in range(2):
                    cu = cus[u][hh]
                    wbuf[p, kb, :, 128 * hh:128 * (hh + 1)] = jnp.exp(zs[u][hh] - cu[:, :128] - cins[hh]).astype(BF16)
                    cins[hh] = cins[hh] + cu[:, 128:]
                d = _dot(wbuf[p, kb], v2_ref[kb])
                part = d if part is None else part + d
            carry[0], carry[1] = cins[0], cins[1]
            acc[...] += part
            return 0

        lax.fori_loop(0, nkb // per, step, 0)
        for u in range(per):
            save(per - 1 - u).start()
        o_ref[...] = acc[...]

        @pl.when(n == 8 * nq - 1)
        def _():
            drain(n - 1, 1 - p)
            drain(n, p)

    blk = pl.BlockSpec((qb, 128), lambda h, i: (i, h))
    wide = pl.BlockSpec((None, nkb_all, 128, 256), lambda h, i: (h, 0, 0, 0))
    tall = pl.BlockSpec((None, nkb_all, 256, 128), lambda h, i: (h, 0, 0, 0))
    return _pcall(body, name="sb_fwd", grid=(8, t // qb),
                  in_specs=[blk, wide, tall, pl.BlockSpec((256, 256), lambda h, i: (0, 0)),
                            pl.BlockSpec((per + 3, qb, 128), lambda h, i: (0, 0, 0))],
                  out_specs=[blk, pl.BlockSpec(memory_space=pl.ANY)],
                  out_shape=[jax.ShapeDtypeStruct((t, 1024), F32),
                             jax.ShapeDtypeStruct((8, t // qb, nkb_all, qb, 256), BF16)],
                  scratch_shapes=[pltpu.VMEM((qb, 128), F32), pltpu.VMEM((2, qb, 128), F32),
                                  pltpu.VMEM((per, qb, 256), F32), pltpu.VMEM((2, nkb_all, qb, 256), BF16),
                                  pltpu.SemaphoreType.DMA((2, nkb_all))],
                  compiler_params=_params(dimension_semantics=("arbitrary", "arbitrary")))(
                      qh, kt, v2, later_tab, bias_tab)


def _sb_bwd(qh, kt, k2, vt, wsave, do, earlier_tab, bias_tab):
    t = qh.shape[0]
    qb = _sb_qb(t)
    per = qb // CHUNK
    nkb_all = t // CHUNK

    zero_slot = nkb_all
    nq = t // qb

    def body(q_ref, kt_ref, k2_ref, vt_ref, ws_ref, do_ref, etab_ref, bias_ref,
             dq_ref, dk_ref, dv_ref, acc, gcarry, zbuf, dwbuf, wbuf, wsem, dzbuf):
        h, i = pl.program_id(0), pl.program_id(1)
        n = h * nq + i
        p = n & 1

        @pl.when(i == 0)
        def _():
            dk_ref[...] = jnp.zeros_like(dk_ref)
            dv_ref[...] = jnp.zeros_like(dv_ref)

        nkb = (i + 1) * per
        fetch = lambda kb: pltpu.make_async_copy(ws_ref.at[h, i, kb], wbuf.at[p, kb], wsem.at[p, kb])

        def prefetch(step, par):
            hs, is_ = step // nq, step % nq

            def one(kb, _):
                pltpu.make_async_copy(ws_ref.at[hs, is_, kb], wbuf.at[par, kb], wsem.at[par, kb]).start()
                return 0

            lax.fori_loop(0, (is_ + 1) * per, one, 0)

        @pl.when(n == 0)
        def _():
            prefetch(n, p)

        @pl.when(n + 1 < 8 * nq)
        def _():
            prefetch(n + 1, 1 - p)

        q = q_ref[...]
        dob = do_ref[...].astype(BF16)
        lane = lax.broadcasted_iota(jnp.int32, (1, 128), 1)
        acc[...] = jnp.zeros_like(acc)
        gcarry[...] = jnp.zeros_like(gcarry)
        zbuf[...] = _dot(q, kt_ref[0])
        dwbuf[...] = _dot(dob, vt_ref[0])
        dzbuf[...] = jnp.zeros_like(dzbuf)
        wbuf[p, zero_slot] = jnp.zeros((qb, 256), BF16)

        def gradients(slot, kb):
            dz2 = dzbuf[...]
            acc[...] += _dot(dz2, k2_ref[kb])
            dk2 = _dot(dz2, q, "tn")
            dv2 = _dot(wbuf[p, slot], dob, "tn")
            dk_ref[kb] += jnp.where(lane < 64, dk2[:128], dk2[128:])
            dv_ref[kb] += jnp.where(lane < 64, dv2[:128], dv2[128:])

        def step(kb, _):
            fetch(kb).wait()
            bias = bias_ref[_sb_bias_index(i, kb, per)]
            z2 = zbuf[...]
            dw2 = dwbuf[...]
            nxt = jnp.minimum(kb + 1, nkb - 1)
            zbuf[...] = _dot(q, kt_ref[nxt])
            dwbuf[...] = _dot(dob, vt_ref[nxt])
            gradients(jnp.where(kb == 0, zero_slot, kb - 1), jnp.maximum(kb - 1, 0))
            w2 = wbuf[p, kb]
            for hh in range(2):
                sl = slice(128 * hh, 128 * (hh + 1))
                z = z2[:, sl] + bias
                e = jnp.exp(-jnp.abs(z))
                r = 1.0 / (1.0 + e)
                sig = jnp.where(z >= 0, r, e * r)
                gw = w2[:, sl].astype(F32) * dw2[:, sl]
                cu2 = _dot(_split2(gw), etab_ref[...])
                gin = gcarry[hh]
                gcarry[hh] = gin + cu2[:, 128:]
                dzbuf[:, sl] = (gw - sig * (gw + cu2[:, :128] + gin)).astype(BF16)
            return 0

        lax.fori_loop(0, nkb, step, 0)
        gradients(nkb - 1, nkb - 1)
        dq_ref[...] = acc[...] * SB_SCALE

    blk = pl.BlockSpec((qb, 128), lambda h, i: (i, h))
    wide = pl.BlockSpec((None, nkb_all, 128, 256), lambda h, i: (h, 0, 0, 0))
    tall = pl.BlockSpec((None, nkb_all, 256, 128), lambda h, i: (h, 0, 0, 0))
    tab = pl.BlockSpec((256, 256), lambda h, i: (0, 0))
    kv_out = pl.BlockSpec((nkb_all, 128, 128), lambda h, i: (0, 0, h))
    ksh = jax.ShapeDtypeStruct((nkb_all, 128, 1024), F32)
    dq, dk, dv = _pcall(
        body, name="sb_bwd", grid=(8, t // qb),
        in_specs=[blk, wide, tall, wide, pl.BlockSpec(memory_space=pl.ANY), blk, tab,
                  pl.BlockSpec((per + 3, qb, 128), lambda h, i: (0, 0, 0))],
        out_specs=[blk, kv_out, kv_out], out_shape=[jax.ShapeDtypeStruct((t, 1024), F32), ksh, ksh],
        scratch_shapes=[pltpu.VMEM((qb, 128), F32), pltpu.VMEM((2, qb, 128), F32),
                        pltpu.VMEM((qb, 256), F32), pltpu.VMEM((qb, 256), F32),
                        pltpu.VMEM((2, nkb_all + 1, qb, 256), BF16), pltpu.SemaphoreType.DMA((2, nkb_all)),
                        pltpu.VMEM((qb, 256), BF16)],
        compiler_params=_params(dimension_semantics=("arbitrary", "arbitrary")))(
            qh, kt, k2, vt, wsave, do, earlier_tab, bias_tab)
    return dq, dk.reshape(t, 1024), dv.reshape(t, 1024)


def _adamw_math(w, g, m, v):
    m = ADAM_B1 * m + (1.0 - ADAM_B1) * g
    v = ADAM_B2 * v + (1.0 - ADAM_B2) * (g * g)
    m_hat = m / (1.0 - ADAM_B1 ** ADAM_STEP)
    v_hat = v / (1.0 - ADAM_B2 ** ADAM_STEP)
    delta = -ADAM_LR * (m_hat / (jnp.sqrt(v_hat) + ADAM_EPS) + ADAM_WD * w)
    return delta, m, v


def _adamw(name, w, owns, recvs, m, v, me):
    shape = w.shape
    c = shape[-1]
    nl = len(owns)
    w3, m3, v3 = (a.reshape(nl, -1, c) for a in (w, m, v))
    r = w3.shape[1]
    tr = _tile(r, (256, 128))
    owns = [o.reshape(N_DEV, r, c) for o in owns]
    recvs = [p.reshape(N_DEV - 1, r, c) for p in recvs]

    def body(me_ref, w_ref, *rest):
        own_refs, recv_refs = rest[:nl], rest[nl:2 * nl]
        m_ref, v_ref = rest[2 * nl:2 * nl + 2]
        g_out, d_out, m_out, v_out = rest[2 * nl + 2:]
        layer = pl.program_id(0)

        def grad(k):
            g = own_refs[k][...].astype(F32)
            for s in range(N_DEV - 1):
                g = g + recv_refs[k][s].astype(F32)
            return g

        g = grad(0)
        for k in range(1, nl):
            g = jnp.where(layer == k, grad(k), g)
        d, mn, vn = _adamw_math(w_ref[...], g, m_ref[...], v_ref[...])
        g_out[...] = g
        d_out[...] = d
        m_out[...] = mn
        v_out[...] = vn

    row = pl.BlockSpec((None, tr, c), lambda l, i, me_ref: (l, i, 0))
    own = lambda k: pl.BlockSpec((None, tr, c), lambda l, i, me_ref: (me_ref[0], jnp.where(l == k, i, 0), 0))
    rcv = lambda k: pl.BlockSpec((N_DEV - 1, tr, c), lambda l, i, me_ref: (0, jnp.where(l == k, i, 0), 0))
    osh = jax.ShapeDtypeStruct((nl, r, c), F32)
    grid_spec = pltpu.PrefetchScalarGridSpec(
        num_scalar_prefetch=1, grid=(nl, r // tr),
        in_specs=[row] + [own(k) for k in range(nl)] + [rcv(k) for k in range(nl)] + [row, row],
        out_specs=[row, row, row, row])
    outs = _pcall(body, name=name, grid_spec=grid_spec, out_shape=[osh, osh, osh, osh])(
        me.reshape(1), w3, *owns, *recvs, m3, v3)
    return tuple(o.reshape(shape) for o in outs)


def _place():
    x, y, c = lax.axis_index("x"), lax.axis_index("y"), lax.axis_index("c")
    return x, y, c, 4 * x + 2 * y + c


def _peer(x, y, c, rel):
    return (x ^ ((rel >> 2) & 1), y ^ ((rel >> 1) & 1), c ^ (rel & 1))


def _gather_first(now, later):
    n, k = len(now), len(later)

    def body(*refs):
        ins, outs = refs[:n + k], refs[n + k:2 * (n + k)]
        send, recv, lsem = refs[2 * (n + k):]
        x, y, c, me = _place()
        locals_ = []
        for w in range(n + k):
            local = pltpu.make_async_copy(ins[w], outs[w].at[me], lsem.at[w])
            local.start()
            locals_.append(local)
        def copy(w, src, slot, rel, to_rel):
            return pltpu.make_async_remote_copy(src_ref=src, dst_ref=outs[w].at[slot], send_sem=send.at[w, rel - 1],
                                                recv_sem=recv.at[w, rel - 1], device_id=_peer(x, y, c, to_rel),
                                                device_id_type=MESH)

        for w in range(n):
            for rel in (1, 2, 4, 6):
                copy(w, ins[w], me, rel, rel).start()
        for w in range(n):
            for rel in (2, 4, 6):
                copy(w, ins[w], me ^ rel, rel, rel).wait_recv()
                copy(w, outs[w].at[me ^ rel], me ^ rel, rel | 1, 1).start()
        for w in range(n):
            for rel in (1, 3, 5, 7):
                copy(w, ins[w], me ^ rel, rel, 1).wait_recv()
            for rel in range(1, N_DEV):
                copy(w, ins[w], me, rel, rel).wait_send()
        for local in locals_:
            local.wait()

    hbm = pl.BlockSpec(memory_space=pl.ANY)
    vmem = pl.BlockSpec(memory_space=pltpu.VMEM)
    arrays = list(now) + list(later)
    return _pcall(body, name="gather_first", in_specs=[vmem] * (n + k), out_specs=[hbm] * (n + k),
                  out_shape=[jax.ShapeDtypeStruct((N_DEV,) + a.shape, a.dtype) for a in arrays],
                  scratch_shapes=[pltpu.SemaphoreType.DMA((n, N_DEV - 1)), pltpu.SemaphoreType.DMA((n, N_DEV - 1)),
                                  pltpu.SemaphoreType.DMA((n + k,))],
                  compiler_params=_params(has_side_effects=True))(*arrays)


_HBM = pl.BlockSpec(memory_space=pltpu.HBM)
_SEM = pl.BlockSpec(memory_space=pltpu.SEMAPHORE)
_DATAFLOW = pltpu.SideEffectType.DATAFLOW_SIDE_EFFECTING


def _exchange_refs(srcs, lands, mode, me, rel, j):
    if mode == "gather":
        return srcs[j], lands[j].at[me], lands[j].at[me ^ rel]
    return srcs[j].at[me ^ rel], lands[j].at[rel - 1], lands[j].at[rel - 1]


def _exchange_start(name, srcs, lands, mode):
    n = len(srcs)

    def body(*refs):
        ins, lnd = refs[:n], refs[n:2 * n]
        send, recv = refs[2 * n], refs[2 * n + 1]
        token = refs[-1]
        x, y, c, me = _place()
        for j in range(n):
            for rel in range(1, N_DEV):
                src, dst, _ = _exchange_refs(ins, lnd, mode, me, rel, j)
                pltpu.make_async_remote_copy(src_ref=src, dst_ref=dst, send_sem=send.at[j * (N_DEV - 1) + rel - 1],
                                             recv_sem=recv.at[j * (N_DEV - 1) + rel - 1],
                                             device_id=_peer(x, y, c, rel), device_id_type=MESH).start()
        token[...] = jnp.zeros_like(token)

    sems = pltpu.SemaphoreType.DMA((n * (N_DEV - 1),))
    hbm_like = lambda a: pltpu.HBM(a.shape, a.dtype)
    outs = _pcall(body, name=name + "_start",
                  in_specs=[_HBM] * (2 * n), out_specs=[_SEM, _SEM] + [_HBM] * (2 * n) + [pl.BlockSpec(memory_space=pltpu.VMEM)],
                  out_shape=[sems, sems] + [hbm_like(a) for a in srcs] + [hbm_like(a) for a in lands]
                  + [jax.ShapeDtypeStruct((8, 128), F32)],
                  input_output_aliases={i: 2 + i for i in range(2 * n)},
                  compiler_params=pltpu.CompilerParams(has_side_effects=_DATAFLOW))(
                      *[pltpu.with_memory_space_constraint(a, pltpu.HBM) for a in list(srcs) + list(lands)])
    return dict(name=name, mode=mode, n=n, send=outs[0], recv=outs[1], srcs=outs[2:2 + n], lands=outs[2 + n:2 + 2 * n],
                token=outs[-1][0, 0])


def _exchange_wait(ex, after):
    n, mode = ex["n"], ex["mode"]

    def body(*refs):
        ins, lnd = refs[:n], refs[n:2 * n]
        send, recv = refs[2 * n], refs[2 * n + 1]
        x, y, c, me = _place()
        for j in range(n):
            for rel in range(1, N_DEV):
                src, dst, landed = _exchange_refs(ins, lnd, mode, me, rel, j)
                pltpu.make_async_remote_copy(src_ref=src, dst_ref=dst, send_sem=send.at[j * (N_DEV - 1) + rel - 1],
                                             recv_sem=recv.at[j * (N_DEV - 1) + rel - 1],
                                             device_id=_peer(x, y, c, rel), device_id_type=MESH).wait_send()
                pltpu.make_async_remote_copy(src_ref=src, dst_ref=landed, send_sem=send.at[j * (N_DEV - 1) + rel - 1],
                                             recv_sem=recv.at[j * (N_DEV - 1) + rel - 1],
                                             device_id=_peer(x, y, c, rel), device_id_type=MESH).wait_recv()

    hbm_like = lambda a: pltpu.HBM(a.shape, a.dtype)
    arrays = list(ex["srcs"]) + list(ex["lands"])
    outs = _pcall(body, name=ex["name"] + "_wait",
                  in_specs=[_HBM] * (2 * n) + [_SEM, _SEM, pl.BlockSpec(memory_space=pl.ANY)],
                  out_specs=[_HBM] * (2 * n), out_shape=[hbm_like(a) for a in arrays],
                  input_output_aliases={i: i for i in range(2 * n)},
                  compiler_params=pltpu.CompilerParams(has_side_effects=_DATAFLOW))(
                      *arrays, ex["send"], ex["recv"], after)
    return outs[:n], outs[n:]


def _scatter_start(name, grads):
    lands = [lax.empty((N_DEV - 1,) + g.shape[1:], g.dtype) for g in grads]
    return _exchange_start(name, grads, lands, "scatter")


ROW_MIX, ROW_MLP, ROW_CB, ROW_LG, ROW_LB, ROW_QN, ROW_KN, ROW_LOSS = 0, 2, 4, 5, 6, 7, 8, 9
ROW_META, ROW_CW, ROW_GN, SMALL_ROWS = 16, 32, 64, 72


def _sum_small(slots):
    def body(s_ref, o_ref):
        tot = s_ref[0]
        for s in range(1, N_DEV):
            tot = tot + s_ref[s]
        o_ref[...] = tot
        for row in (ROW_QN, ROW_KN):
            v = tot[row:row + 1, :]
            f = v[:, 0:128]
            for k in range(1, 8):
                f = f + v[:, 128 * k:128 * (k + 1)]
            o_ref[row:row + 1, 0:64] = f[:, 0:64] + f[:, 64:128]

    return _pcall(body, name="sum_small", out_shape=jax.ShapeDtypeStruct(slots.shape[1:], F32))(slots)


def _adamw_small(w, g, m, v):
    def body(w_ref, g_ref, m_ref, v_ref, d_out, m_out, v_out):
        d, mn, vn = _adamw_math(w_ref[...], g_ref[...], m_ref[...], v_ref[...])
        d_out[...] = d
        m_out[...] = mn
        v_out[...] = vn

    osh = jax.ShapeDtypeStruct(w.shape, F32)
    return _pcall(body, name="adamw_small", out_shape=[osh, osh, osh])(w, g, m, v)


def _local_step(h0, target, p, weight, emit):
    t = h0.shape[0]
    tables = _ret_tables(t)
    bd, later_tab, earlier_tab, bias_tab = _seg_tables(_sb_qb(t))
    row = lambda a, i: a[i:i + 1]

    hn_a = _rms_fwd("rms_mix0", h0, row(p["norm_mix_g"], 0))
    w_in = weight("w_in", hn_a)
    proj = _mm_cols("proj_in", hn_a, w_in, ())
    o_ret, states = _ret_fwd(proj, tables)
    gn_flat = p["gn_g"].reshape(1, 1024)
    cat = _gn_gate_fwd(o_ret, proj, gn_flat)
    cat, hdn, ycv = _conv_fwd(cat, proj, p["conv_w"], p["conv_b"], p["ln_g"], p["ln_b"])
    w_out = weight("w_out", cat)
    h1, hn_b = _mm_rows_norm("mix_out", cat, w_out, h0, row(p["norm_mlp_g"], 0))
    w1_0, w2_0 = weight("w1_0", hn_b), weight("w2_0", hn_b)
    a0, s0 = _mm_cols("mlp0_up", hn_b, w1_0, (), epi="relu2")
    h2, hn_c = _mm_rows_norm("mlp0_down", s0, w2_0, h1, row(p["norm_mix_g"], 1))

    w_qkv = weight("w_qkv", hn_c)
    qkv = _mm_cols("qkv", hn_c, w_qkv, ())
    qg = jnp.tile(p["qn_g"], (1, 16))
    kg = jnp.tile(p["kn_g"], (1, 16))
    qh, kt, k2, vt, v2 = _qk_norm_fwd(qkv, qg, kg, bd)
    o_sb, w_sb = _sb_fwd(qh, kt, v2, later_tab, bias_tab)
    w_o = weight("w_o", o_sb)
    h3, hn_d = _mm_rows_norm("attn_out", o_sb, w_o, h2, row(p["norm_mlp_g"], 1))
    w1_1, w2_1 = weight("w1_1", hn_d), weight("w2_1", hn_d)
    a1, s1 = _mm_cols("mlp1_up", hn_d, w1_1, (), epi="relu2")
    h4 = _mm_rows("mlp1_down", s1, w2_1, h3)

    dh, loss = _loss_bwd(h4, target)

    def mlp_bwd(tag, layer, w1, w2, dh, h_in, hn, a, s):
        da = _mm_rows_t(f"{tag}_dact", dh, w2, (), out_dtype=BF16, epi="drelu2", extra=a)
        dw2 = _wgrad_rows(f"{tag}_dw2", s, dh, 512)
        dw1 = _wgrad_cols(f"{tag}_dw1", hn, da, 512)
        tok = emit(tag, [dw1, dw2])
        dhn = _mm_cols_t(f"{tag}_dhn", da, w1)
        return _rms_bwd(f"{tag}_rms_bwd", dhn, h_in, row(p["norm_mlp_g"], layer) + tok, dh)

    dh, dg_mlp1 = mlp_bwd("mlp1", 1, w1_1, w2_1, dh, h3, hn_d, a1, s1)

    do_sb = _mm_rows_t("attn_dout", dh, w_o, ())
    dw_o = _wgrad_rows("attn_dwo", o_sb, dh, 128)
    dq, dk, dv = _sb_bwd(qh, kt, k2, vt, w_sb, do_sb, earlier_tab, bias_tab)
    dqkv, dqg, dkg = _qk_norm_bwd(qkv, dq, dk, dv, qg, kg, bd)
    dw_qkv = _wgrad_cols("qkv_dw", hn_c, dqkv, 384)
    tok = emit("attn", [dw_qkv, dw_o])
    dhn = _mm_cols_t("qkv_dhn", dqkv, w_qkv)
    dh, dg_mix1 = _rms_bwd("mix1_rms_bwd", dhn, h2, row(p["norm_mix_g"], 1) + tok, dh)

    dh, dg_mlp0 = mlp_bwd("mlp0", 0, w1_0, w2_0, dh, h1, hn_b, a0, s0)

    dcat = _mm_rows_t("mix_dcat", dh, w_out, ())
    dw_out = _wgrad_rows("mix_dwout", cat, dh, 256)
    tok = emit("mix0_out", [dw_out])
    do_ret, dproj, dgn = _gn_gate_bwd(dcat, o_ret, proj, gn_flat + tok)
    dproj = _ret_bwd(dproj, proj, states, do_ret, tables)
    dy, dlg, dlb, dcb = _conv_bwd_ln(dcat, ycv, p["ln_g"], p["ln_b"])
    dproj, dug, dcw = _conv_bwd_taps(dproj, dy, hdn, proj, p["conv_w"])
    dproj = lax.dynamic_update_slice(dproj, dug, (0, 4096))
    dw_in = _wgrad_cols("proj_dw", hn_a, dproj, 640)
    tok = emit("mix0", [dw_in])
    dhn = _mm_cols_t("proj_dhn", dproj, w_in)
    dh, dg_mix0 = _rms_bwd("mix0_rms_bwd", dhn, h0, row(p["norm_mix_g"], 0) + tok, dh)

    rid = lax.broadcasted_iota(jnp.int32, (16, 1), 0)
    loss_row = jnp.broadcast_to(loss[0:1, 0:1], (1, D_MODEL))
    vecs = sum(jnp.where(rid == k, v, 0.0)
               for k, v in enumerate((dg_mix0, dg_mix1, dg_mlp0, dg_mlp1, dcb, dlg, dlb, dqg, dkg, loss_row)))
    small = jnp.concatenate([vecs, dh[PAD_FRONT:TOK0], dcw, jnp.where(rid[:8] == 0, dgn, 0.0)], axis=0)
    return dh[TOK0:], small


_SMALL_NAMES = ("meta", "norm_mix_g", "norm_mlp_g", "even_ret_gn_g", "even_conv_w", "even_conv_b",
                "even_conv_ln_g", "even_conv_ln_b", "odd_q_norm_g", "odd_k_norm_g")
_BIG_NAMES = ("even_w_in", "even_w_out", "odd_w_qkv", "odd_w_o", "mlp_w1", "mlp_w2")
_ORDER = ("meta", "norm_mix_g", "norm_mlp_g", "even_w_in", "even_ret_gn_g", "even_conv_w", "even_conv_b",
          "even_conv_ln_g", "even_conv_ln_b", "even_w_out", "odd_w_qkv", "odd_q_norm_g", "odd_k_norm_g",
          "odd_w_o", "mlp_w1", "mlp_w2")


def _pack128(a):
    flat = a.reshape(-1)
    n = flat.shape[0]
    rows = -(-n // 128)
    rows8 = -(-rows // 8) * 8
    return jnp.pad(flat, (0, rows8 * 128 - n)).reshape(rows8, 128)


def kernel(x, meta, norm_mix_g, norm_mlp_g, even_w_in, even_ret_gn_g, even_conv_w, even_conv_b, even_conv_ln_g, even_conv_ln_b, even_w_out, odd_w_qkv, odd_q_norm_g, odd_k_norm_g, odd_w_o, mlp_w1, mlp_w2, loss_target, m_meta, m_norm_mix_g, m_norm_mlp_g, m_even_w_in, m_even_ret_gn_g, m_even_conv_w, m_even_conv_b, m_even_conv_ln_g, m_even_conv_ln_b, m_even_w_out, m_odd_w_qkv, m_odd_q_norm_g, m_odd_k_norm_g, m_odd_w_o, m_mlp_w1, m_mlp_w2, v_meta, v_norm_mix_g, v_norm_mlp_g, v_even_w_in, v_even_ret_gn_g, v_even_conv_w, v_even_conv_b, v_even_conv_ln_g, v_even_conv_ln_b, v_even_w_out, v_odd_w_qkv, v_odd_q_norm_g, v_odd_k_norm_g, v_odd_w_o, v_mlp_w1, v_mlp_w2):
    w = dict(meta=meta, norm_mix_g=norm_mix_g, norm_mlp_g=norm_mlp_g, even_w_in=even_w_in,
             even_ret_gn_g=even_ret_gn_g, even_conv_w=even_conv_w, even_conv_b=even_conv_b,
             even_conv_ln_g=even_conv_ln_g, even_conv_ln_b=even_conv_ln_b, even_w_out=even_w_out,
             odd_w_qkv=odd_w_qkv, odd_q_norm_g=odd_q_norm_g, odd_k_norm_g=odd_k_norm_g, odd_w_o=odd_w_o,
             mlp_w1=mlp_w1, mlp_w2=mlp_w2)
    mom = dict(meta=m_meta, norm_mix_g=m_norm_mix_g, norm_mlp_g=m_norm_mlp_g, even_w_in=m_even_w_in,
               even_ret_gn_g=m_even_ret_gn_g, even_conv_w=m_even_conv_w, even_conv_b=m_even_conv_b,
               even_conv_ln_g=m_even_conv_ln_g, even_conv_ln_b=m_even_conv_ln_b, even_w_out=m_even_w_out,
               odd_w_qkv=m_odd_w_qkv, odd_q_norm_g=m_odd_q_norm_g, odd_k_norm_g=m_odd_k_norm_g, odd_w_o=m_odd_w_o,
               mlp_w1=m_mlp_w1, mlp_w2=m_mlp_w2)
    var = dict(meta=v_meta, norm_mix_g=v_norm_mix_g, norm_mlp_g=v_norm_mlp_g, even_w_in=v_even_w_in,
               even_ret_gn_g=v_even_ret_gn_g, even_conv_w=v_even_conv_w, even_conv_b=v_even_conv_b,
               even_conv_ln_g=v_even_conv_ln_g, even_conv_ln_b=v_even_conv_ln_b, even_w_out=v_even_w_out,
               odd_w_qkv=v_odd_w_qkv, odd_q_norm_g=v_odd_q_norm_g, odd_k_norm_g=v_odd_k_norm_g, odd_w_o=v_odd_w_o,
               mlp_w1=v_mlp_w1, mlp_w2=v_mlp_w2)
    me = 4 * lax.axis_index("x") + 2 * lax.axis_index("y") + lax.axis_index("c")

    small_in = jnp.concatenate([meta, jnp.pad(even_conv_w[0], ((0, 1), (0, 0))),
                                jnp.pad(even_ret_gn_g[0], ((0, 4), (0, 96)))], axis=0)
    b16 = lambda a: a.astype(BF16)
    later_src = dict(w_out=b16(even_w_out[0]), w1_0=b16(mlp_w1[0]), w2_0=b16(mlp_w2[0]),
                     w_qkv=b16(odd_w_qkv[0]), w_o=b16(odd_w_o[0]), w1_1=b16(mlp_w1[1]), w2_1=b16(mlp_w2[1]))
    landed = _gather_first([b16(even_w_in[0]), small_in], list(later_src.values()))
    g_in, g_small = landed[0], landed[1]
    own_slot = dict(zip(later_src, landed[2:]))
    groups = (("gather_l0", ("w_out", "w1_0", "w2_0")), ("gather_attn", ("w_qkv", "w_o")),
              ("gather_l1", ("w1_1", "w2_1")))
    pending = {}
    gather_tok = jnp.zeros((), F32)
    for gname, names in groups:
        ex = _exchange_start(gname, [later_src[n] for n in names], [own_slot[n] for n in names], "gather")
        gather_tok = gather_tok + ex["token"]
        for n in names:
            pending[n] = (ex, names)
    arrived = dict(w_in=g_in)

    def weight(name, after):
        if name not in arrived:
            ex, names = pending[name]
            arrived.update(zip(names, _exchange_wait(ex, after)[1]))
        return arrived[name]

    cols = lambda a: jnp.transpose(a, (1, 0, 2)).reshape(a.shape[1], -1)
    p = dict(norm_mix_g=norm_mix_g + gather_tok, norm_mlp_g=norm_mlp_g, conv_b=even_conv_b, ln_g=even_conv_ln_g,
             ln_b=even_conv_ln_b, qn_g=odd_q_norm_g, kn_g=odd_k_norm_g,
             gn_g=cols(g_small[:, 48:52, :32]),
             conv_w=jnp.pad(cols(g_small[:, 16:47]), ((0, 1), (0, 0))))
    meta_full = cols(g_small[:, 0:16])

    scatters = {}

    def emit(tag, grads):
        scatters[tag] = _scatter_start("scatter_" + tag, grads)
        return scatters[tag]["token"]

    h0 = jnp.concatenate([jnp.zeros((PAD_FRONT, D_MODEL), F32), meta_full, x[0]], axis=0)
    grad_x, small_part = _local_step(h0, loss_target[0], p, weight, emit)

    out = {}
    got = {}

    def update(names, terms, after):
        for tag in {t for name in names for t, _ in terms[name]} - set(got):
            got[tag] = _exchange_wait(scatters[tag], after)
        for name in names:
            owns, recvs = zip(*[(got[t][0][j], got[t][1][j]) for t, j in terms[name]])
            out[name] = _adamw("adamw_" + name, w[name], list(owns), list(recvs), mom[name], var[name], me)

    terms = dict(even_w_in=[("mix0", 0)], even_w_out=[("mix0_out", 0)], odd_w_qkv=[("attn", 0)], odd_w_o=[("attn", 1)],
                 mlp_w1=[("mlp0", 0), ("mlp1", 0)], mlp_w2=[("mlp0", 1), ("mlp1", 1)])
    small_ex = _exchange_start("small", [small_part], [lax.empty((N_DEV,) + small_part.shape, F32)], "gather")
    update(("mlp_w1", "mlp_w2", "odd_w_qkv", "odd_w_o", "even_w_out"), terms, grad_x)
    update(("even_w_in",), terms, out["even_w_out"][1])
    (own_part,), (slots,) = _exchange_wait(small_ex, out["even_w_in"][1])
    tot = _sum_small(lax.dynamic_update_slice(slots, own_part[None], (me, 0, 0)))
    loss = tot[ROW_LOSS, 0]

    shard_cols = lambda a, width: lax.dynamic_slice_in_dim(a, me * width, width, axis=1)
    one = lambda r: tot[r:r + 1]
    small_g = dict(
        norm_mix_g=tot[ROW_MIX:ROW_MIX + 2], norm_mlp_g=tot[ROW_MLP:ROW_MLP + 2],
        even_conv_b=one(ROW_CB), even_conv_ln_g=one(ROW_LG), even_conv_ln_b=one(ROW_LB),
        odd_q_norm_g=one(ROW_QN)[:, :64], odd_k_norm_g=one(ROW_KN)[:, :64],
        meta=shard_cols(tot[ROW_META:ROW_META + N_META], 128),
        even_conv_w=shard_cols(tot[ROW_CW:ROW_CW + CONV_WIDTH], 128)[None],
        even_ret_gn_g=shard_cols(tot[ROW_GN].reshape(4, 256), 32)[None])
    packs = {n: (_pack128(w[n]), _pack128(small_g[n]), _pack128(mom[n]), _pack128(var[n])) for n in _SMALL_NAMES}
    cat4 = [jnp.concatenate([packs[n][i] for n in _SMALL_NAMES], axis=0) for i in range(4)]
    d_s, m_s, v_s = _adamw_small(*cat4)
    r0 = 0
    for n in _SMALL_NAMES:
        rows = packs[n][0].shape[0]
        size = w[n].size
        take = lambda a: a[r0:r0 + rows].reshape(-1)[:size].reshape(w[n].shape)
        out[n] = (small_g[n].reshape(w[n].shape), take(d_s), take(m_s), take(v_s))
        r0 += rows

    res = [loss, grad_x[None]]
    for i in range(4):
        res.extend(out[n][i] for n in _ORDER)
    return tuple(res)
```

```python
import functools

import numpy as np
import jax
import jax.numpy as jnp
from jax import lax
from jax.experimental import pallas as pl
from jax.experimental.pallas import tpu as pltpu

F32 = jnp.float32
BF16 = jnp.bfloat16

D_MODEL = 1024
N_META = 16
CHUNK = 128
PAD_FRONT = 112
TOK0 = PAD_FRONT + N_META
EPS = 1e-6
N_DEV = 8
RET_HEADS = 4
RET_DECAY_OFFSET = 5.0
ROPE_BASE = 10000.0
CONV_WIDTH = 31
HALO = 32
SB_SCALE = 64 ** -0.5
RET_SCALE = 128 ** -0.5
ADAM_LR, ADAM_B1, ADAM_B2, ADAM_EPS, ADAM_WD, ADAM_STEP = 0.001, 0.9, 0.999, 1e-08, 0.01, 10
VMEM_LIMIT = 56 * 1024 * 1024
MESH = pl.DeviceIdType.MESH


def _pcall(body, **kw):
    return pl.pallas_call(body, **kw)


def _params(**kw):
    return pltpu.CompilerParams(vmem_limit_bytes=VMEM_LIMIT, **kw)


def _tile(n, cands):
    for c in cands:
        if n % c == 0:
            return c
    raise ValueError(f"no tile for {n} in {cands}")


def _sigmoid(x):
    return 1.0 / (1.0 + jnp.exp(-x))


_DIMS = {
    "nn": (((1,), (0,)), ((), ())),
    "nt": (((1,), (1,)), ((), ())),
    "tn": (((0,), (0,)), ((), ())),
}


def _matmul(name, a, b, *, grid, a_spec, b_spec, o_spec, out_shape, contract, acc_shape,
            epi="plain", extra=None, extra_spec=None):
    nk = grid[2]
    dims = _DIMS[contract]
    n_in = 3 if extra is not None else 2
    n_out = 2 if epi == "relu2" else 1

    def body(*refs):
        a_ref, b_ref = refs[0], refs[1]
        e_ref = refs[2] if extra is not None else None
        outs = refs[n_in:n_in + n_out]
        acc = refs[-1]
        k = pl.program_id(2)
        part = lax.dot_general(a_ref[...].astype(BF16), b_ref[...].astype(BF16), dims, preferred_element_type=F32)
        if nk > 1:
            @pl.when(k == 0)
            def _():
                acc[...] = jnp.zeros_like(acc)

            acc[...] += part

        @pl.when(k == nk - 1)
        def _():
            r = acc[...] if nk > 1 else part
            if epi == "plain":
                outs[0][...] = r.astype(outs[0].dtype)
            elif epi == "residual":
                outs[0][...] = (r + e_ref[...]).astype(outs[0].dtype)
            elif epi == "relu2":
                outs[0][...] = r
                rr = jnp.maximum(r, 0.0)
                outs[1][...] = (rr * rr).astype(BF16)
            elif epi == "drelu2":
                outs[0][...] = (r * (2.0 * jnp.maximum(e_ref[...], 0.0))).astype(outs[0].dtype)

    in_specs = [a_spec, b_spec] + ([extra_spec] if extra is not None else [])
    args = (a, b) + ((extra,) if extra is not None else ())
    if n_out == 2:
        out_specs = [o_spec, o_spec]
    else:
        out_specs = o_spec
    return _pcall(body, name=name, grid=grid, in_specs=in_specs, out_specs=out_specs,
                  out_shape=out_shape, scratch_shapes=[pltpu.VMEM(acc_shape, F32)],
                  compiler_params=_params(dimension_semantics=("parallel", "parallel", "arbitrary")))(*args)


def _tm(t):
    return _tile(t, (1408, 768, 384, 128))


def _mm_cols(name, a, wb, lead, out_dtype=F32, epi="plain"):
    t, kdim = a.shape
    n = wb.shape[-1]
    tm, tk = _tm(t), _tile(kdim, (1024, 512))
    nl = len(lead)
    b_spec = pl.BlockSpec((None,) * (1 + nl) + (tk, n), lambda i, j, k: (j,) + lead + (k, 0))
    o_spec = pl.BlockSpec((tm, n), lambda i, j, k: (i, j))
    if epi == "relu2":
        out_shape = [jax.ShapeDtypeStruct((t, N_DEV * n), F32), jax.ShapeDtypeStruct((t, N_DEV * n), BF16)]
    else:
        out_shape = jax.ShapeDtypeStruct((t, N_DEV * n), out_dtype)
    return _matmul(name, a, wb, grid=(t // tm, N_DEV, kdim // tk),
                   a_spec=pl.BlockSpec((tm, tk), lambda i, j, k: (i, k)), b_spec=b_spec, o_spec=o_spec,
                   out_shape=out_shape, contract="nn", acc_shape=(tm, n), epi=epi)


def _tm_deep(t, kdim):
    return _tm(t) if kdim <= 2048 else _tile(t, (704, 384, 128))


def _mm_cols_t_rms(name, a, wb, h, g, dres):
    t = a.shape[0]
    nb, kdim, n = wb.shape
    tm = _tile(t, (704, 384, 128))

    def body(a_ref, b_ref, h_ref, g_ref, r_ref, o_ref, dg_ref):
        @pl.when(pl.program_id(0) == 0)
        def _():
            dg_ref[...] = jnp.zeros_like(dg_ref)

        d = _dot(a_ref[:, 0:n].astype(BF16), b_ref[0], "nt")
        for j in range(1, nb):
            d = d + _dot(a_ref[:, j * n:(j + 1) * n].astype(BF16), b_ref[j], "nt")
        x = h_ref[...]
        rs = lax.rsqrt(jnp.mean(x * x, axis=-1, keepdims=True) + EPS)
        u = d * g_ref[...]
        m = jnp.mean(u * x, axis=-1, keepdims=True)
        o_ref[...] = r_ref[...] + rs * u - x * (rs * rs * rs * m)
        dg_ref[...] += jnp.sum(d * x * rs, axis=0, keepdims=True)

    row = pl.BlockSpec((tm, kdim), lambda i: (i, 0))
    vec = pl.BlockSpec((1, kdim), lambda i: (0, 0))
    return _pcall(body, name=name, grid=(t // tm,),
                  in_specs=[pl.BlockSpec((tm, nb * n), lambda i: (i, 0)),
                            pl.BlockSpec((nb, kdim, n), lambda i: (0, 0, 0)), row, vec, row],
                  out_specs=[row, vec],
                  out_shape=[jax.ShapeDtypeStruct((t, kdim), F32), jax.ShapeDtypeStruct((1, kdim), F32)],
                  compiler_params=_params(dimension_semantics=("arbitrary",)))(a, wb, h, g, dres)


def _mm_rows_t(name, a, wb, lead, out_dtype=F32, epi="plain", extra=None):
    t, n = a.shape
    r = wb.shape[-2]
    tm, tk = _tm(t), _tile(n, (1024,))
    nl = len(lead)
    b_spec = pl.BlockSpec((None,) * (1 + nl) + (r, tk), lambda i, j, k: (j,) + lead + (0, k))
    o_spec = pl.BlockSpec((tm, r), lambda i, j, k: (i, j))
    return _matmul(name, a, wb, grid=(t // tm, N_DEV, n // tk),
                   a_spec=pl.BlockSpec((tm, tk), lambda i, j, k: (i, k)), b_spec=b_spec, o_spec=o_spec,
                   out_shape=jax.ShapeDtypeStruct((t, N_DEV * r), out_dtype), contract="nt",
                   acc_shape=(tm, r), epi=epi, extra=extra, extra_spec=o_spec if extra is not None else None)


def _mm_rows_loss(name, a, wb, residual, target):
    t = a.shape[0]
    nb, r, n = wb.shape
    tm, tn = _tm_deep(t, nb * r), _tile(n, (512,))

    def body(a_ref, b_ref, r_ref, t_ref, d_ref, l_ref):
        i = pl.program_id(0)

        @pl.when((i == 0) & (pl.program_id(1) == 0))
        def _():
            l_ref[...] = jnp.zeros_like(l_ref)

        y = r_ref[...] + _dot(a_ref[...].astype(BF16), b_ref[...].reshape(nb * r, tn))
        diff = jnp.where(_row_ids(i, tm) >= TOK0, y - t_ref[...], 0.0)
        d_ref[...] = diff * (1.0 / D_MODEL)
        l_ref[...] += jnp.sum(diff * diff) * (0.5 / D_MODEL)

    o_spec = pl.BlockSpec((tm, tn), lambda i, j: (i, j))
    return _pcall(body, name=name, grid=(t // tm, n // tn),
                  in_specs=[pl.BlockSpec((tm, nb * r), lambda i, j: (i, 0)),
                            pl.BlockSpec((nb, r, tn), lambda i, j: (0, 0, j)), o_spec, o_spec],
                  out_specs=[o_spec, pl.BlockSpec((8, 128), lambda i, j: (0, 0))],
                  out_shape=[jax.ShapeDtypeStruct((t, n), F32), jax.ShapeDtypeStruct((8, 128), F32)],
                  compiler_params=_params(dimension_semantics=("arbitrary", "arbitrary")))(a, wb, residual, target)


def _mm_rows_norm(name, a, wb, residual, g):
    t = a.shape[0]
    nb, r, n = wb.shape
    tm = _tile(t, (704, 384, 128))

    def body(a_ref, b_ref, r_ref, g_ref, h_ref, hn_ref):
        h = r_ref[...] + _dot(a_ref[...].astype(BF16), b_ref[...].reshape(nb * r, n))
        h_ref[...] = h
        hn_ref[...] = (h * lax.rsqrt(jnp.mean(h * h, axis=-1, keepdims=True) + EPS) * g_ref[...]).astype(BF16)

    row = pl.BlockSpec((tm, n), lambda i: (i, 0))
    return _pcall(body, name=name, grid=(t // tm,),
                  in_specs=[pl.BlockSpec((tm, nb * r), lambda i: (i, 0)), pl.BlockSpec((nb, r, n), lambda i: (0, 0, 0)),
                            row, pl.BlockSpec((1, n), lambda i: (0, 0))],
                  out_specs=[row, row],
                  out_shape=[jax.ShapeDtypeStruct((t, n), F32), jax.ShapeDtypeStruct((t, n), BF16)],
                  compiler_params=_params(dimension_semantics=("parallel",)))(a, wb, residual, g)


def _wgrad_cols(name, x, dy, n):
    t, kdim = x.shape
    tk = _tm(t)
    return _matmul(name, x, dy, grid=(1, N_DEV, t // tk),
                   a_spec=pl.BlockSpec((tk, kdim), lambda i, j, k: (k, 0)),
                   b_spec=pl.BlockSpec((tk, n), lambda i, j, k: (k, j)),
                   o_spec=pl.BlockSpec((None, kdim, n), lambda i, j, k: (j, 0, 0)),
                   out_shape=jax.ShapeDtypeStruct((N_DEV, kdim, n), BF16), contract="tn", acc_shape=(kdim, n))


def _wgrad_rows(name, x, dy, r):
    t = x.shape[0]
    n = dy.shape[1]
    tk, tn = _tm(t), _tile(n, (512,))
    tm = min(N_DEV * r, 1024)
    out = _matmul(name, x, dy, grid=(N_DEV * r // tm, n // tn, t // tk),
                  a_spec=pl.BlockSpec((tk, tm), lambda i, j, k: (k, i)),
                  b_spec=pl.BlockSpec((tk, tn), lambda i, j, k: (k, j)),
                  o_spec=pl.BlockSpec((tm, tn), lambda i, j, k: (i, j)),
                  out_shape=jax.ShapeDtypeStruct((N_DEV * r, n), BF16), contract="tn", acc_shape=(tm, tn))
    return out.reshape(N_DEV, r, n)


def _rows(t):
    return _tile(t, (384, 128))


def _rms_fwd(name, h, g):
    t = h.shape[0]
    tr = _rows(t)

    def body(h_ref, g_ref, o_ref):
        x = h_ref[...]
        r = lax.rsqrt(jnp.mean(x * x, axis=-1, keepdims=True) + EPS)
        o_ref[...] = (x * r * g_ref[...]).astype(BF16)

    row = pl.BlockSpec((tr, D_MODEL), lambda i: (i, 0))
    vec = pl.BlockSpec((1, D_MODEL), lambda i: (0, 0))
    return _pcall(body, name=name, grid=(t // tr,), in_specs=[row, vec], out_specs=row,
                  out_shape=jax.ShapeDtypeStruct((t, D_MODEL), BF16))(h, g)


def _ret_tables(t):
    hh = np.arange(RET_HEADS, dtype=np.float64)
    log_g = np.log1p(-np.exp2(-RET_DECAY_OFFSET - hh))
    idx = np.arange(CHUNK, dtype=np.float64)
    diff = idx[:, None] - idx[None, :]
    dmat = np.where(diff[None] >= 0, np.exp(np.maximum(diff, 0.0)[None] * log_g[:, None, None]), 0.0)
    qdec = np.exp((idx + 1.0)[None, :, None] * log_g[:, None, None]) * np.ones((1, 1, CHUNK))
    kdec = np.exp((CHUNK - 1 - idx)[None, :, None] * log_g[:, None, None]) * np.ones((1, 1, CHUNK))
    half = CHUNK // 2
    inv_freq = (ROPE_BASE ** (-np.arange(half, dtype=np.float32) / half)).astype(np.float32)
    ang = (np.arange(t, dtype=np.float32)[:, None] * inv_freq[None, :]).astype(np.float32).astype(np.float64)
    cos2 = np.concatenate([np.cos(ang), np.cos(ang)], axis=1)
    sin2 = np.concatenate([-np.sin(ang), np.sin(ang)], axis=1)
    return tuple(jnp.asarray(v, F32) for v in (dmat, qdec, kdec, cos2, sin2))


def _rot(x, c, s):
    return x * c + pltpu.roll(x, CHUNK // 2, 1) * s


def _unrot(dx, c, s):
    return dx * c + pltpu.roll(dx * s, CHUNK // 2, 1)


def _dot(a, b, contract="nn"):
    return lax.dot_general(a, b, _DIMS[contract], preferred_element_type=F32)


def _ret_fwd(proj, tables):
    t = proj.shape[0]
    nch = t // CHUNK
    dmat, qdec, kdec, cos2, sin2 = tables

    def body(qk_ref, v_ref, c_ref, s_ref, dm_ref, qd_ref, kd_ref, o_ref, st_ref, state):
        @pl.when(pl.program_id(0) == 0)
        def _():
            state[...] = jnp.zeros_like(state)

        c, s = c_ref[...], s_ref[...]
        for h in range(RET_HEADS):
            q = _rot(qk_ref[:, 128 * h:128 * (h + 1)], c, s)
            k = _rot(qk_ref[:, 512 + 128 * h:512 + 128 * (h + 1)], c, s) * RET_SCALE
            vb = v_ref[:, 256 * h:256 * (h + 1)].astype(BF16)
            st = state[h]
            st_ref[h] = st
            sc = _dot(q.astype(BF16), k.astype(BF16), "nt") * dm_ref[h]
            o = _dot(sc.astype(BF16), vb)
            o += _dot((q * qd_ref[h]).astype(BF16), st.astype(BF16))
            o_ref[:, 256 * h:256 * (h + 1)] = o
            kv = _dot((k * kd_ref[h]).astype(BF16), vb, "tn")
            state[h] = qd_ref[h, CHUNK - 1:CHUNK, 0:1] * st + kv

    tab = pl.BlockSpec((RET_HEADS, CHUNK, CHUNK), lambda n: (0, 0, 0))
    pos = pl.BlockSpec((CHUNK, CHUNK), lambda n: (n, 0))
    return _pcall(
        body, name="ret_fwd", grid=(nch,),
        in_specs=[pl.BlockSpec((CHUNK, 1024), lambda n: (n, 0)), pl.BlockSpec((CHUNK, 1024), lambda n: (n, 1)),
                  pos, pos, tab, tab, tab],
        out_specs=[pl.BlockSpec((CHUNK, 1024), lambda n: (n, 0)),
                   pl.BlockSpec((RET_HEADS, None, 128, 256), lambda n: (0, n, 0, 0))],
        out_shape=[jax.ShapeDtypeStruct((t, 1024), F32), jax.ShapeDtypeStruct((RET_HEADS, nch, 128, 256), F32)],
        scratch_shapes=[pltpu.VMEM((RET_HEADS, 128, 256), F32)],
        compiler_params=_params(dimension_semantics=("arbitrary",)))(
            proj, proj, cos2, sin2, dmat, qdec, kdec)


def _ret_bwd(dproj, proj, states, do, tables):
    t = proj.shape[0]
    nch = t // CHUNK
    dmat, qdec, kdec, cos2, sin2 = tables

    def body(dp_in, qk_ref, v_ref, do_ref, st_ref, c_ref, s_ref, dm_ref, qd_ref, kd_ref, dp_ref, rst):
        del dp_in
        @pl.when(pl.program_id(0) == 0)
        def _():
            rst[...] = jnp.zeros_like(rst)

        c, s = c_ref[...], s_ref[...]
        for h in range(RET_HEADS):
            q = _rot(qk_ref[:, 128 * h:128 * (h + 1)], c, s)
            k = _rot(qk_ref[:, 512 + 128 * h:512 + 128 * (h + 1)], c, s) * RET_SCALE
            qb, kb = q.astype(BF16), k.astype(BF16)
            vb = v_ref[:, 256 * h:256 * (h + 1)].astype(BF16)
            dob = do_ref[:, 256 * h:256 * (h + 1)].astype(BF16)
            pb = st_ref[h].astype(BF16)
            r = rst[h]
            rb = r.astype(BF16)
            dm, qd, kd = dm_ref[h], qd_ref[h], kd_ref[h]
            sb = (_dot(qb, kb, "nt") * dm).astype(BF16)
            dsb = (_dot(dob, vb, "nt") * dm).astype(BF16)
            dq = _dot(dsb, kb) + _dot(dob, pb, "nt") * qd
            dk = _dot(dsb, qb, "tn") + _dot(vb, rb, "nt") * kd
            dv = _dot(sb, dob, "tn") + _dot((k * kd).astype(BF16), rb)
            rst[h] = _dot((q * qd).astype(BF16), dob, "tn") + qd[CHUNK - 1:CHUNK, 0:1] * r
            dp_ref[:, 128 * h:128 * (h + 1)] = _unrot(dq, c, s).astype(BF16)
            dp_ref[:, 512 + 128 * h:512 + 128 * (h + 1)] = (_unrot(dk, c, s) * RET_SCALE).astype(BF16)
            dp_ref[:, 1024 + 256 * h:1024 + 256 * (h + 1)] = dv.astype(BF16)

    rev = lambda n: nch - 1 - n
    tab = pl.BlockSpec((RET_HEADS, CHUNK, CHUNK), lambda n: (0, 0, 0))
    pos = pl.BlockSpec((CHUNK, CHUNK), lambda n: (rev(n), 0))
    row = pl.BlockSpec((CHUNK, 1024), lambda n: (rev(n), 0))
    return _pcall(
        body, name="ret_bwd", grid=(nch,),
        in_specs=[pl.BlockSpec(memory_space=pl.ANY), row, pl.BlockSpec((CHUNK, 1024), lambda n: (rev(n), 1)), row,
                  pl.BlockSpec((RET_HEADS, None, 128, 256), lambda n: (0, rev(n), 0, 0)),
                  pos, pos, tab, tab, tab],
        out_specs=pl.BlockSpec((CHUNK, 2048), lambda n: (rev(n), 0)),
        out_shape=jax.ShapeDtypeStruct((t, 5120), BF16),
        scratch_shapes=[pltpu.VMEM((RET_HEADS, 128, 256), F32)], input_output_aliases={0: 0},
        compiler_params=_params(dimension_semantics=("arbitrary",)))(
            dproj, proj, proj, do, states, cos2, sin2, dmat, qdec, kdec)


def _gn_gate_fwd(o, proj, gn_g):
    t = o.shape[0]
    tr = _rows(t)

    def body(o_ref, g_ref, w_ref, c_ref):
        for h in range(RET_HEADS):
            sl = slice(256 * h, 256 * (h + 1))
            x = o_ref[:, sl]
            mu = jnp.mean(x, axis=-1, keepdims=True)
            xc = x - mu
            rstd = lax.rsqrt(jnp.mean(xc * xc, axis=-1, keepdims=True) + EPS)
            g = g_ref[:, sl]
            c_ref[:, sl] = (g * _sigmoid(g) * (xc * rstd * w_ref[:, sl])).astype(BF16)

    return _pcall(body, name="gn_gate_fwd", grid=(t // tr,),
                  in_specs=[pl.BlockSpec((tr, 1024), lambda i: (i, 0)),
                            pl.BlockSpec((tr, 1024), lambda i: (i, 2)),
                            pl.BlockSpec((1, 1024), lambda i: (0, 0))],
                  out_specs=pl.BlockSpec((tr, 1024), lambda i: (i, 0)),
                  out_shape=jax.ShapeDtypeStruct((t, 2048), BF16))(o, proj, gn_g)


def _gn_gate_bwd(dcat, o, proj, gn_g):
    t = o.shape[0]
    tr = _rows(t)

    def body(d_ref, o_ref, g_ref, w_ref, do_ref, dg_ref, dw_ref):
        @pl.when(pl.program_id(0) == 0)
        def _():
            dw_ref[...] = jnp.zeros_like(dw_ref)

        for h in range(RET_HEADS):
            sl = slice(256 * h, 256 * (h + 1))
            x = o_ref[:, sl]
            mu = jnp.mean(x, axis=-1, keepdims=True)
            xc = x - mu
            rstd = lax.rsqrt(jnp.mean(xc * xc, axis=-1, keepdims=True) + EPS)
            xh = xc * rstd
            w = w_ref[:, sl]
            g = g_ref[:, sl]
            sg = _sigmoid(g)
            d = d_ref[:, sl]
            don = d * (g * sg)
            dg_ref[:, sl] = (d * (xh * w) * (sg * (1.0 + g * (1.0 - sg)))).astype(BF16)
            dw_ref[:, sl] += jnp.sum(don * xh, axis=0, keepdims=True)
            dxh = don * w
            m1 = jnp.mean(dxh, axis=-1, keepdims=True)
            m2 = jnp.mean(dxh * xh, axis=-1, keepdims=True)
            do_ref[:, sl] = rstd * (dxh - m1 - xh * m2)

    row = pl.BlockSpec((tr, 1024), lambda i: (i, 0))
    vec = pl.BlockSpec((1, 1024), lambda i: (0, 0))
    return _pcall(body, name="gn_gate_bwd", grid=(t // tr,),
                  in_specs=[row, row, pl.BlockSpec((tr, 1024), lambda i: (i, 2)), vec],
                  out_specs=[row, pl.BlockSpec((tr, 1024), lambda i: (i, 2)), vec],
                  out_shape=[jax.ShapeDtypeStruct((t, 1024), F32), jax.ShapeDtypeStruct((t, 5120), BF16),
                             jax.ShapeDtypeStruct((1, 1024), F32)])(dcat, o, proj, gn_g)


def _row_ids(i, tr):
    return i * tr + lax.broadcasted_iota(jnp.int32, (tr, 1), 0)


SH_ROWS = HALO - 8


def _shifted_copies(xs, sh, tr):
    for b in range(1, 8):
        sh[b - 1] = xs[pl.ds(b, tr + SH_ROWS), :]


def _shifted(xs, sh, off, tr):
    a, b = divmod(off, 8)
    return xs[pl.ds(8 * a, tr), :] if b == 0 else sh[b - 1, pl.ds(8 * a, tr), :]


def _conv_fwd(cat, proj, conv_w, conv_b, ln_g, ln_b):
    t = proj.shape[0]
    tr = _rows(t)
    hb = tr // HALO

    def body(cat_in, ua_ref, ug_ref, pa_ref, pg_ref, w_ref, b_ref, lg_ref, lb_ref, c_ref, hd_ref, y_ref, xs, sh):
        del cat_in
        i = pl.program_id(0)
        hdn = ua_ref[...] * _sigmoid(ug_ref[...])
        hd_ref[...] = hdn
        prev = pa_ref[...] * _sigmoid(pg_ref[...])
        xs[0:HALO, :] = jnp.where(i > 0, prev, 0.0)
        xs[HALO:HALO + tr, :] = hdn
        _shifted_copies(xs, sh, tr)
        acc = jnp.zeros((tr, 1024), F32) + b_ref[...]
        for w in range(CONV_WIDTH):
            acc += w_ref[w:w + 1, :] * _shifted(xs, sh, HALO - (CONV_WIDTH - 1) + w, tr)
        y_ref[...] = acc
        mu = jnp.mean(acc, axis=-1, keepdims=True)
        yc = acc - mu
        rstd = lax.rsqrt(jnp.mean(yc * yc, axis=-1, keepdims=True) + EPS)
        yn = yc * rstd * lg_ref[...] + lb_ref[...]
        c = yn * _sigmoid(yn)
        c_ref[...] = jnp.where(_row_ids(i, tr) >= PAD_FRONT, c, 0.0).astype(BF16)

    row = pl.BlockSpec((tr, 1024), lambda i: (i, 0))
    vec = pl.BlockSpec((1, 1024), lambda i: (0, 0))
    halo = lambda col: pl.BlockSpec((HALO, 1024), lambda i: (jnp.maximum(i * hb - 1, 0), col))
    return _pcall(body, name="conv_fwd", grid=(t // tr,),
                  in_specs=[pl.BlockSpec(memory_space=pl.ANY),
                            pl.BlockSpec((tr, 1024), lambda i: (i, 3)), pl.BlockSpec((tr, 1024), lambda i: (i, 4)),
                            halo(3), halo(4), pl.BlockSpec((32, 1024), lambda i: (0, 0)), vec, vec, vec],
                  out_specs=[pl.BlockSpec((tr, 1024), lambda i: (i, 1)), row, row],
                  out_shape=[jax.ShapeDtypeStruct((t, 2048), BF16), jax.ShapeDtypeStruct((t, 1024), F32),
                             jax.ShapeDtypeStruct((t, 1024), F32)],
                  scratch_shapes=[pltpu.VMEM((tr + HALO, 1024), F32), pltpu.VMEM((7, tr + SH_ROWS, 1024), F32)],
                  input_output_aliases={0: 0}, compiler_params=_params())(
                      cat, proj, proj, proj, proj, conv_w, conv_b, ln_g, ln_b)


def _conv_bwd_ln(dcat, y, ln_g, ln_b):
    t = y.shape[0]
    tr = _rows(t)

    def body(d_ref, y_ref, lg_ref, lb_ref, dy_ref, dlg_ref, dlb_ref, dcb_ref):
        i = pl.program_id(0)

        @pl.when(i == 0)
        def _():
            dlg_ref[...] = jnp.zeros_like(dlg_ref)
            dlb_ref[...] = jnp.zeros_like(dlb_ref)
            dcb_ref[...] = jnp.zeros_like(dcb_ref)

        y = y_ref[...]
        mu = jnp.mean(y, axis=-1, keepdims=True)
        yc = y - mu
        rstd = lax.rsqrt(jnp.mean(yc * yc, axis=-1, keepdims=True) + EPS)
        xh = yc * rstd
        lg = lg_ref[...]
        yn = xh * lg + lb_ref[...]
        sg = _sigmoid(yn)
        dyn = jnp.where(_row_ids(i, tr) >= PAD_FRONT, d_ref[...] * (sg * (1.0 + yn * (1.0 - sg))), 0.0)
        dlg_ref[...] += jnp.sum(dyn * xh, axis=0, keepdims=True)
        dlb_ref[...] += jnp.sum(dyn, axis=0, keepdims=True)
        dxh = dyn * lg
        m1 = jnp.mean(dxh, axis=-1, keepdims=True)
        m2 = jnp.mean(dxh * xh, axis=-1, keepdims=True)
        dy = rstd * (dxh - m1 - xh * m2)
        dy_ref[...] = dy
        dcb_ref[...] += jnp.sum(dy, axis=0, keepdims=True)

    row = pl.BlockSpec((tr, 1024), lambda i: (i, 0))
    vec = pl.BlockSpec((1, 1024), lambda i: (0, 0))
    vshape = jax.ShapeDtypeStruct((1, 1024), F32)
    return _pcall(body, name="conv_bwd_ln", grid=(t // tr,),
                  in_specs=[pl.BlockSpec((tr, 1024), lambda i: (i, 1)), row, vec, vec],
                  out_specs=[row, vec, vec, vec],
                  out_shape=[jax.ShapeDtypeStruct((t, 1024), F32), vshape, vshape, vshape])(dcat, y, ln_g, ln_b)


def _conv_bwd_taps(dproj, dy, hdn, proj, conv_w):
    t = dy.shape[0]
    tr = _rows(t)
    hb = tr // HALO
    nt = t // tr

    def body(dp_in, dy_ref, nx_ref, hd_ref, ph_ref, ua_ref, ug_ref, w_ref, da_ref, dg_ref, dw_ref, xs, sh):
        del dp_in
        i = pl.program_id(0)

        @pl.when(i == 0)
        def _():
            dw_ref[...] = jnp.zeros_like(dw_ref)

        dy = dy_ref[...]
        xs[0:tr, :] = dy
        xs[tr:tr + HALO, :] = jnp.where(i < nt - 1, nx_ref[...], 0.0)
        _shifted_copies(xs, sh, tr)
        dh = jnp.zeros((tr, 1024), F32)
        for w in range(CONV_WIDTH):
            dh += w_ref[w:w + 1, :] * _shifted(xs, sh, CONV_WIDTH - 1 - w, tr)
        xs[0:HALO, :] = jnp.where(i > 0, ph_ref[...], 0.0)
        xs[HALO:HALO + tr, :] = hd_ref[...]
        _shifted_copies(xs, sh, tr)
        for w in range(CONV_WIDTH):
            dw_ref[w:w + 1, :] += jnp.sum(dy * _shifted(xs, sh, HALO - (CONV_WIDTH - 1) + w, tr), axis=0, keepdims=True)
        dh = jnp.where(_row_ids(i, tr) >= PAD_FRONT, dh, 0.0)
        sg = _sigmoid(ug_ref[...])
        da_ref[...] = (dh * sg).astype(BF16)
        dg_ref[...] = (dh * ua_ref[...] * sg * (1.0 - sg)).astype(BF16)

    row = pl.BlockSpec((tr, 1024), lambda i: (i, 0))
    return _pcall(body, name="conv_bwd_taps", grid=(nt,),
                  in_specs=[pl.BlockSpec(memory_space=pl.ANY),
                            row, pl.BlockSpec((HALO, 1024), lambda i: (jnp.minimum((i + 1) * hb, nt * hb - 1), 0)),
                            row, pl.BlockSpec((HALO, 1024), lambda i: (jnp.maximum(i * hb - 1, 0), 0)),
                            pl.BlockSpec((tr, 1024), lambda i: (i, 3)), pl.BlockSpec((tr, 1024), lambda i: (i, 4)),
                            pl.BlockSpec((32, 1024), lambda i: (0, 0))],
                  out_specs=[pl.BlockSpec((tr, 1024), lambda i: (i, 3)), row, pl.BlockSpec((32, 1024), lambda i: (0, 0))],
                  out_shape=[jax.ShapeDtypeStruct((t, 5120), BF16), jax.ShapeDtypeStruct((t, 1024), BF16),
                             jax.ShapeDtypeStruct((32, 1024), F32)],
                  scratch_shapes=[pltpu.VMEM((tr + HALO, 1024), F32), pltpu.VMEM((7, tr + SH_ROWS, 1024), F32)],
                  input_output_aliases={0: 0}, compiler_params=_params())(
                      dproj, dy, dy, hdn, hdn, proj, proj, conv_w)


NEG_BIG = -1e30


def _seg_tables(qb):
    j = np.arange(128)
    bd = (j[:, None] // 64 == j[None, :] // 64).astype(np.float32)
    ones = np.ones((128, 128), np.float32)
    later = np.concatenate([(j[:, None] >= j[None, :]).astype(np.float32), ones], axis=1)
    earlier = np.concatenate([(j[:, None] < j[None, :]).astype(np.float32), ones], axis=1)
    per = qb // CHUNK
    row = np.arange(qb)[:, None]
    pad = np.broadcast_to(j[None, :] < PAD_FRONT, (qb, 128))
    diag = [(g * CHUNK + j[None, :]) >= row for g in range(per)]
    masks = diag + [np.zeros((qb, 128), bool), pad, diag[0] | pad]
    bias = np.stack([np.where(m, NEG_BIG, 0.0) for m in masks]).astype(np.float32)
    dup = lambda m: np.concatenate([m, m], axis=0)
    return (jnp.asarray(bd, BF16), jnp.asarray(dup(later), BF16), jnp.asarray(dup(earlier), BF16),
            jnp.asarray(bias, F32))


def _split_dot(x, m):
    hi = x.astype(BF16)
    lo = (x - hi.astype(F32)).astype(BF16)
    return _dot(hi, m) + _dot(lo, m)


def _qk_norm_fwd(qkv, qg, kg, bd):
    t = qkv.shape[0]
    tr = _rows(t)
    nb = tr // CHUNK

    def body(q_ref, k_ref, v_ref, qg_ref, kg_ref, bd_ref, qo, kt, k2, vt, v2):
        bdm = bd_ref[...]
        lane = lax.broadcasted_iota(jnp.int32, (1, 128), 1)
        sub = lax.broadcasted_iota(jnp.int32, (128, 1), 0)

        def pair_layouts(x, t_ref, s_ref, hp, b):
            xt = x.T
            t_ref[hp, b] = jnp.concatenate([jnp.where(sub < 64, xt, 0.0), jnp.where(sub >= 64, xt, 0.0)],
                                           axis=1).astype(BF16)
            s_ref[hp, b] = jnp.concatenate([jnp.where(lane < 64, x, 0.0), jnp.where(lane >= 64, x, 0.0)],
                                           axis=0).astype(BF16)

        for hp in range(8):
            sl = slice(128 * hp, 128 * (hp + 1))
            x = q_ref[:, sl]
            r = lax.rsqrt(_split_dot(x * x, bdm) * (1.0 / 64) + EPS)
            qo[:, sl] = (x * r * (qg_ref[:, sl] * SB_SCALE)).astype(BF16)
            x = k_ref[:, sl]
            r = lax.rsqrt(_split_dot(x * x, bdm) * (1.0 / 64) + EPS)
            kn = x * r * kg_ref[:, sl]
            v = v_ref[:, sl]
            for b in range(nb):
                rows = slice(CHUNK * b, CHUNK * (b + 1))
                pair_layouts(kn[rows], kt, k2, hp, b)
                pair_layouts(v[rows], vt, v2, hp, b)

    col = lambda c: pl.BlockSpec((tr, 1024), lambda i: (i, c))
    vec = pl.BlockSpec((1, 1024), lambda i: (0, 0))
    wide = pl.BlockSpec((8, nb, 128, 256), lambda i: (0, i, 0, 0))
    tall = pl.BlockSpec((8, nb, 256, 128), lambda i: (0, i, 0, 0))
    wsh = jax.ShapeDtypeStruct((8, t // CHUNK, 128, 256), BF16)
    tsh = jax.ShapeDtypeStruct((8, t // CHUNK, 256, 128), BF16)
    return _pcall(body, name="qk_norm_fwd", grid=(t // tr,),
                  in_specs=[col(0), col(1), col(2), vec, vec, pl.BlockSpec((128, 128), lambda i: (0, 0))],
                  out_specs=[col(0), wide, tall, wide, tall],
                  out_shape=[jax.ShapeDtypeStruct((t, 1024), BF16), wsh, tsh, wsh, tsh])(qkv, qkv, qkv, qg, kg, bd)


def _qk_norm_bwd(qkv, dq, dk, dv, qg, kg, bd):
    t = qkv.shape[0]
    tr = _rows(t)

    def body(q_ref, k_ref, dq_ref, dk_ref, dv_ref, qg_ref, kg_ref, bd_ref, o_ref, dqg_ref, dkg_ref):
        @pl.when(pl.program_id(0) == 0)
        def _():
            dqg_ref[...] = jnp.zeros_like(dqg_ref)
            dkg_ref[...] = jnp.zeros_like(dkg_ref)

        bdm = bd_ref[...]
        for part, (src, d_ref, g_ref, dg_ref) in enumerate(((q_ref, dq_ref, qg_ref, dqg_ref),
                                                           (k_ref, dk_ref, kg_ref, dkg_ref))):
            for cix in range(8):
                sl = slice(128 * cix, 128 * (cix + 1))
                x = src[:, sl]
                d = d_ref[:, sl]
                r = lax.rsqrt(_split_dot(x * x, bdm) * (1.0 / 64) + EPS)
                u = d * g_ref[:, sl]
                m = _split_dot(u * x, bdm) * (1.0 / 64)
                o_ref[:, 1024 * part + 128 * cix:1024 * part + 128 * (cix + 1)] = (r * u - x * (r * r * r * m)).astype(BF16)
                dg_ref[:, sl] += jnp.sum(d * x * r, axis=0, keepdims=True)
        o_ref[:, 2048:3072] = dv_ref[...].astype(BF16)

    col = lambda c: pl.BlockSpec((tr, 1024), lambda i: (i, c))
    vec = pl.BlockSpec((1, 1024), lambda i: (0, 0))
    vsh = jax.ShapeDtypeStruct((1, 1024), F32)
    return _pcall(body, name="qk_norm_bwd", grid=(t // tr,),
                  in_specs=[col(0), col(1), col(0), col(0), col(0), vec, vec, pl.BlockSpec((128, 128), lambda i: (0, 0))],
                  out_specs=[pl.BlockSpec((tr, 3072), lambda i: (i, 0)), vec, vec],
                  out_shape=[jax.ShapeDtypeStruct((t, 3072), BF16), vsh, vsh])(qkv, qkv, dq, dk, dv, qg, kg, bd)


def _split2(x):
    hi = x.astype(BF16)
    lo = (x - hi.astype(F32)).astype(BF16)
    return jnp.concatenate([hi, lo], axis=1)


def _sb_scores(z, later_tab):
    e = jnp.exp(-jnp.abs(z))
    ope = 1.0 + e
    sp = jnp.maximum(z, 0.0) + jnp.log(ope)
    return e, ope, _dot(_split2(sp), later_tab)


def _sb_bias_index(i, kb, per):
    g = kb - i * per
    return jnp.where(kb == 0, jnp.where(i == 0, per + 2, per + 1), jnp.where(g >= 0, g, per))


def _sb_qb(t):
    return _tile(t, (384, 128))


def _sb_fwd(qh, kt, v2, later_tab, bias_tab):
    t = qh.shape[0]
    qb = _sb_qb(t)
    per = qb // CHUNK
    nkb_all = t // CHUNK

    nq = t // qb

    def body(q_ref, kt_ref, v2_ref, tab_ref, bias_ref, o_ref, ws_ref, acc, carry, zbuf, wbuf, wsem):
        h, i = pl.program_id(0), pl.program_id(1)
        n = h * nq + i
        p = n & 1
        q = q_ref[...]
        acc[...] = jnp.zeros_like(acc)
        carry[...] = jnp.zeros_like(carry)
        nkb = (i + 1) * per
        save = lambda kb: pltpu.make_async_copy(wbuf.at[p, kb], ws_ref.at[h, i, kb], wsem.at[p, kb])

        def drain(step, par):
            hs, is_ = step // nq, step % nq

            def one(kb, _):
                pltpu.make_async_copy(wbuf.at[par, kb], ws_ref.at[hs, is_, kb], wsem.at[par, kb]).wait()
                return 0

            lax.fori_loop(0, (is_ + 1) * per, one, 0)

        @pl.when(n >= 2)
        def _():
            drain(n - 2, p)

        for u in range(per):
            zbuf[u] = _dot(q, kt_ref[nkb - 1 - u])

        def step(s, _):
            top = nkb - 1 - per * s

            @pl.when(s > 0)
            def _():
                for u in range(per):
                    save(top + per - u).start()

            z2s = [zbuf[u] for u in range(per)]
            for u in range(per):
                zbuf[u] = _dot(q, kt_ref[jnp.maximum(top - per - u, 0)])
            cins = [carry[0], carry[1]]
            zs, cus = [], []
            for u in range(per):
                bias = bias_ref[_sb_bias_index(i, top - u, per)]
                zs.append([z2s[u][:, 128 * hh:128 * (hh + 1)] + bias for hh in range(2)])
                cus.append([_sb_scores(z, tab_ref[...])[2] for z in zs[u]])
            part = None
            for u in range(per):
                kb = top - u
                for hh in range(2):
                    cu = cus[u][hh]
                    wbuf[p, kb, :, 128 * hh:128 * (hh + 1)] = jnp.exp(zs[u][hh] - cu[:, :128] - cins[hh]).astype(BF16)
                    cins[hh] = cins[hh] + cu[:, 128:]
                d = _dot(wbuf[p, kb], v2_ref[kb])
                part = d if part is None else part + d
            carry[0], carry[1] = cins[0], cins[1]
            acc[...] += part
            return 0

        lax.fori_loop(0, nkb // per, step, 0)
        for u in range(per):
            save(per - 1 - u).start()
        o_ref[...] = acc[...]

        @pl.when(n == 8 * nq - 1)
        def _():
            drain(n - 1, 1 - p)
            drain(n, p)

    blk = pl.BlockSpec((qb, 128), lambda h, i: (i, h))
    wide = pl.BlockSpec((None, nkb_all, 128, 256), lambda h, i: (h, 0, 0, 0))
    tall = pl.BlockSpec((None, nkb_all, 256, 128), lambda h, i: (h, 0, 0, 0))
    return _pcall(body, name="sb_fwd", grid=(8, t // qb),
                  in_specs=[blk, wide, tall, pl.BlockSpec((256, 256), lambda h, i: (0, 0)),
                            pl.BlockSpec((per + 3, qb, 128), lambda h, i: (0, 0, 0))],
                  out_specs=[blk, pl.BlockSpec(memory_space=pl.ANY)],
                  out_shape=[jax.ShapeDtypeStruct((t, 1024), F32),
                             jax.ShapeDtypeStruct((8, t // qb, nkb_all, qb, 256), BF16)],
                  scratch_shapes=[pltpu.VMEM((qb, 128), F32), pltpu.VMEM((2, qb, 128), F32),
                                  pltpu.VMEM((per, qb, 256), F32), pltpu.VMEM((2, nkb_all, qb, 256), BF16),
                                  pltpu.SemaphoreType.DMA((2, nkb_all))],
                  compiler_params=_params(dimension_semantics=("arbitrary", "arbitrary")))(
                      qh, kt, v2, later_tab, bias_tab)


def _sb_bwd(qh, kt, k2, vt, wsave, do, earlier_tab, bias_tab):
    t = qh.shape[0]
    qb = _sb_qb(t)
    per = qb // CHUNK
    nkb_all = t // CHUNK

    zero_slot = nkb_all
    nq = t // qb

    def body(q_ref, kt_ref, k2_ref, vt_ref, ws_ref, do_ref, etab_ref, bias_ref,
             dq_ref, dk_ref, dv_ref, acc, gcarry, zbuf, dwbuf, wbuf, wsem, dzbuf):
        h, i = pl.program_id(0), pl.program_id(1)
        n = h * nq + i
        p = n & 1

        @pl.when(i == 0)
        def _():
            dk_ref[...] = jnp.zeros_like(dk_ref)
            dv_ref[...] = jnp.zeros_like(dv_ref)

        nkb = (i + 1) * per
        fetch = lambda kb: pltpu.make_async_copy(ws_ref.at[h, i, kb], wbuf.at[p, kb], wsem.at[p, kb])

        def prefetch(step, par):
            hs, is_ = step // nq, step % nq

            def one(kb, _):
                pltpu.make_async_copy(ws_ref.at[hs, is_, kb], wbuf.at[par, kb], wsem.at[par, kb]).start()
                return 0

            lax.fori_loop(0, (is_ + 1) * per, one, 0)

        @pl.when(n == 0)
        def _():
            prefetch(n, p)

        @pl.when(n + 1 < 8 * nq)
        def _():
            prefetch(n + 1, 1 - p)

        q = q_ref[...]
        dob = do_ref[...].astype(BF16)
        lane = lax.broadcasted_iota(jnp.int32, (1, 128), 1)
        acc[...] = jnp.zeros_like(acc)
        gcarry[...] = jnp.zeros_like(gcarry)
        zbuf[...] = _dot(q, kt_ref[0])
        dwbuf[...] = _dot(dob, vt_ref[0])
        dzbuf[...] = jnp.zeros_like(dzbuf)
        wbuf[p, zero_slot] = jnp.zeros((qb, 256), BF16)

        def gradients(slot, kb):
            dz2 = dzbuf[...]
            acc[...] += _dot(dz2, k2_ref[kb])
            dk2 = _dot(dz2, q, "tn")
            dv2 = _dot(wbuf[p, slot], dob, "tn")
            dk_ref[kb] += jnp.where(lane < 64, dk2[:128], dk2[128:])
            dv_ref[kb] += jnp.where(lane < 64, dv2[:128], dv2[128:])

        def step(kb, _):
            fetch(kb).wait()
            bias = bias_ref[_sb_bias_index(i, kb, per)]
            z2 = zbuf[...]
            dw2 = dwbuf[...]
            nxt = jnp.minimum(kb + 1, nkb - 1)
            zbuf[...] = _dot(q, kt_ref[nxt])
            dwbuf[...] = _dot(dob, vt_ref[nxt])
            gradients(jnp.where(kb == 0, zero_slot, kb - 1), jnp.maximum(kb - 1, 0))
            w2 = wbuf[p, kb]
            for hh in range(2):
                sl = slice(128 * hh, 128 * (hh + 1))
                z = z2[:, sl] + bias
                e = jnp.exp(-jnp.abs(z))
                r = 1.0 / (1.0 + e)
                sig = jnp.where(z >= 0, r, e * r)
                gw = w2[:, sl].astype(F32) * dw2[:, sl]
                cu2 = _dot(_split2(gw), etab_ref[...])
                gin = gcarry[hh]
                gcarry[hh] = gin + cu2[:, 128:]
                dzbuf[:, sl] = (gw - sig * (gw + cu2[:, :128] + gin)).astype(BF16)
            return 0

        lax.fori_loop(0, nkb, step, 0)
        gradients(nkb - 1, nkb - 1)
        dq_ref[...] = acc[...] * SB_SCALE

    blk = pl.BlockSpec((qb, 128), lambda h, i: (i, h))
    wide = pl.BlockSpec((None, nkb_all, 128, 256), lambda h, i: (h, 0, 0, 0))
    tall = pl.BlockSpec((None, nkb_all, 256, 128), lambda h, i: (h, 0, 0, 0))
    tab = pl.BlockSpec((256, 256), lambda h, i: (0, 0))
    kv_out = pl.BlockSpec((nkb_all, 128, 128), lambda h, i: (0, 0, h))
    ksh = jax.ShapeDtypeStruct((nkb_all, 128, 1024), F32)
    dq, dk, dv = _pcall(
        body, name="sb_bwd", grid=(8, t // qb),
        in_specs=[blk, wide, tall, wide, pl.BlockSpec(memory_space=pl.ANY), blk, tab,
                  pl.BlockSpec((per + 3, qb, 128), lambda h, i: (0, 0, 0))],
        out_specs=[blk, kv_out, kv_out], out_shape=[jax.ShapeDtypeStruct((t, 1024), F32), ksh, ksh],
        scratch_shapes=[pltpu.VMEM((qb, 128), F32), pltpu.VMEM((2, qb, 128), F32),
                        pltpu.VMEM((qb, 256), F32), pltpu.VMEM((qb, 256), F32),
                        pltpu.VMEM((2, nkb_all + 1, qb, 256), BF16), pltpu.SemaphoreType.DMA((2, nkb_all)),
                        pltpu.VMEM((qb, 256), BF16)],
        compiler_params=_params(dimension_semantics=("arbitrary", "arbitrary")))(
            qh, kt, k2, vt, wsave, do, earlier_tab, bias_tab)
    return dq, dk.reshape(t, 1024), dv.reshape(t, 1024)


def _adamw_math(w, g, m, v):
    m = ADAM_B1 * m + (1.0 - ADAM_B1) * g
    v = ADAM_B2 * v + (1.0 - ADAM_B2) * (g * g)
    m_hat = m / (1.0 - ADAM_B1 ** ADAM_STEP)
    v_hat = v / (1.0 - ADAM_B2 ** ADAM_STEP)
    delta = -ADAM_LR * (m_hat / (jnp.sqrt(v_hat) + ADAM_EPS) + ADAM_WD * w)
    return delta, m, v


def _adamw(name, w, owns, recvs, m, v, me):
    shape = w.shape
    c = shape[-1]
    nl = len(owns)
    w3, m3, v3 = (a.reshape(nl, -1, c) for a in (w, m, v))
    r = w3.shape[1]
    tr = _tile(r, (256, 128))
    owns = [o.reshape(N_DEV, r, c) for o in owns]
    recvs = [p.reshape(N_DEV - 1, r, c) for p in recvs]

    def body(me_ref, w_ref, *rest):
        own_refs, recv_refs = rest[:nl], rest[nl:2 * nl]
        m_ref, v_ref = rest[2 * nl:2 * nl + 2]
        g_out, d_out, m_out, v_out = rest[2 * nl + 2:]
        layer = pl.program_id(0)

        def grad(k):
            g = own_refs[k][...].astype(F32)
            for s in range(N_DEV - 1):
                g = g + recv_refs[k][s].astype(F32)
            return g

        g = grad(0)
        for k in range(1, nl):
            g = jnp.where(layer == k, grad(k), g)
        d, mn, vn = _adamw_math(w_ref[...], g, m_ref[...], v_ref[...])
        g_out[...] = g
        d_out[...] = d
        m_out[...] = mn
        v_out[...] = vn

    row = pl.BlockSpec((None, tr, c), lambda l, i, me_ref: (l, i, 0))
    own = lambda k: pl.BlockSpec((None, tr, c), lambda l, i, me_ref: (me_ref[0], jnp.where(l == k, i, 0), 0))
    rcv = lambda k: pl.BlockSpec((N_DEV - 1, tr, c), lambda l, i, me_ref: (0, jnp.where(l == k, i, 0), 0))
    osh = jax.ShapeDtypeStruct((nl, r, c), F32)
    grid_spec = pltpu.PrefetchScalarGridSpec(
        num_scalar_prefetch=1, grid=(nl, r // tr),
        in_specs=[row] + [own(k) for k in range(nl)] + [rcv(k) for k in range(nl)] + [row, row],
        out_specs=[row, row, row, row])
    outs = _pcall(body, name=name, grid_spec=grid_spec, out_shape=[osh, osh, osh, osh])(
        me.reshape(1), w3, *owns, *recvs, m3, v3)
    return tuple(o.reshape(shape) for o in outs)


def _place():
    x, y, c = lax.axis_index("x"), lax.axis_index("y"), lax.axis_index("c")
    return x, y, c, 4 * x + 2 * y + c


def _peer(x, y, c, rel):
    return (x ^ ((rel >> 2) & 1), y ^ ((rel >> 1) & 1), c ^ (rel & 1))


def _gather_first(now, later):
    n, k = len(now), len(later)

    def body(*refs):
        ins, outs = refs[:n + k], refs[n + k:2 * (n + k)]
        send, recv, lsem = refs[2 * (n + k):]
        x, y, c, me = _place()
        locals_ = []
        for w in range(n + k):
            local = pltpu.make_async_copy(ins[w], outs[w].at[me], lsem.at[w])
            local.start()
            locals_.append(local)
        def copy(w, src, slot, rel, to_rel):
            return pltpu.make_async_remote_copy(src_ref=src, dst_ref=outs[w].at[slot], send_sem=send.at[w, rel - 1],
                                                recv_sem=recv.at[w, rel - 1], device_id=_peer(x, y, c, to_rel),
                                                device_id_type=MESH)

        for w in range(n):
            for rel in (1, 2, 4, 6):
                copy(w, ins[w], me, rel, rel).start()
        for w in range(n):
            for rel in (2, 4, 6):
                copy(w, ins[w], me ^ rel, rel, rel).wait_recv()
                copy(w, outs[w].at[me ^ rel], me ^ rel, rel | 1, 1).start()
        for w in range(n):
            for rel in (1, 3, 5, 7):
                copy(w, ins[w], me ^ rel, rel, 1).wait_recv()
            for rel in range(1, N_DEV):
                copy(w, ins[w], me, rel, rel).wait_send()
        for local in locals_:
            local.wait()

    hbm = pl.BlockSpec(memory_space=pl.ANY)
    vmem = pl.BlockSpec(memory_space=pltpu.VMEM)
    arrays = list(now) + list(later)
    return _pcall(body, name="gather_first", in_specs=[vmem] * (n + k), out_specs=[hbm] * (n + k),
                  out_shape=[jax.ShapeDtypeStruct((N_DEV,) + a.shape, a.dtype) for a in arrays],
                  scratch_shapes=[pltpu.SemaphoreType.DMA((n, N_DEV - 1)), pltpu.SemaphoreType.DMA((n, N_DEV - 1)),
                                  pltpu.SemaphoreType.DMA((n + k,))],
                  compiler_params=_params(has_side_effects=True))(*arrays)


_HBM = pl.BlockSpec(memory_space=pltpu.HBM)
_SEM = pl.BlockSpec(memory_space=pltpu.SEMAPHORE)
_DATAFLOW = pltpu.SideEffectType.DATAFLOW_SIDE_EFFECTING


def _exchange_refs(srcs, lands, mode, me, rel, j):
    if mode == "gather":
        return srcs[j], lands[j].at[me], lands[j].at[me ^ rel]
    return srcs[j].at[me ^ rel], lands[j].at[rel - 1], lands[j].at[rel - 1]


def _exchange_start(name, srcs, lands, mode):
    n = len(srcs)

    def body(*refs):
        ins, lnd = refs[:n], refs[n:2 * n]
        send, recv = refs[2 * n], refs[2 * n + 1]
        token = refs[-1]
        x, y, c, me = _place()
        for j in range(n):
            for rel in range(1, N_DEV):
                src, dst, _ = _exchange_refs(ins, lnd, mode, me, rel, j)
                pltpu.make_async_remote_copy(src_ref=src, dst_ref=dst, send_sem=send.at[j * (N_DEV - 1) + rel - 1],
                                             recv_sem=recv.at[j * (N_DEV - 1) + rel - 1],
                                             device_id=_peer(x, y, c, rel), device_id_type=MESH).start()
        token[...] = jnp.zeros_like(token)

    sems = pltpu.SemaphoreType.DMA((n * (N_DEV - 1),))
    hbm_like = lambda a: pltpu.HBM(a.shape, a.dtype)
    outs = _pcall(body, name=name + "_start",
                  in_specs=[_HBM] * (2 * n), out_specs=[_SEM, _SEM] + [_HBM] * (2 * n) + [pl.BlockSpec(memory_space=pltpu.VMEM)],
                  out_shape=[sems, sems] + [hbm_like(a) for a in srcs] + [hbm_like(a) for a in lands]
                  + [jax.ShapeDtypeStruct((8, 128), F32)],
                  input_output_aliases={i: 2 + i for i in range(2 * n)},
                  compiler_params=pltpu.CompilerParams(has_side_effects=_DATAFLOW))(
                      *[pltpu.with_memory_space_constraint(a, pltpu.HBM) for a in list(srcs) + list(lands)])
    return dict(name=name, mode=mode, n=n, send=outs[0], recv=outs[1], srcs=outs[2:2 + n], lands=outs[2 + n:2 + 2 * n],
                token=outs[-1][0, 0])


def _exchange_wait(ex, after):
    n, mode = ex["n"], ex["mode"]

    def body(*refs):
        ins, lnd = refs[:n], refs[n:2 * n]
        send, recv = refs[2 * n], refs[2 * n + 1]
        x, y, c, me = _place()
        for j in range(n):
            for rel in range(1, N_DEV):
                src, dst, landed = _exchange_refs(ins, lnd, mode, me, rel, j)
                pltpu.make_async_remote_copy(src_ref=src, dst_ref=dst, send_sem=send.at[j * (N_DEV - 1) + rel - 1],
                                             recv_sem=recv.at[j * (N_DEV - 1) + rel - 1],
                                             device_id=_peer(x, y, c, rel), device_id_type=MESH).wait_send()
                pltpu.make_async_remote_copy(src_ref=src, dst_ref=landed, send_sem=send.at[j * (N_DEV - 1) + rel - 1],
                                             recv_sem=recv.at[j * (N_DEV - 1) + rel - 1],
                                             device_id=_peer(x, y, c, rel), device_id_type=MESH).wait_recv()

    hbm_like = lambda a: pltpu.HBM(a.shape, a.dtype)
    arrays = list(ex["srcs"]) + list(ex["lands"])
    outs = _pcall(body, name=ex["name"] + "_wait",
                  in_specs=[_HBM] * (2 * n) + [_SEM, _SEM, pl.BlockSpec(memory_space=pl.ANY)],
                  out_specs=[_HBM] * (2 * n), out_shape=[hbm_like(a) for a in arrays],
                  input_output_aliases={i: i for i in range(2 * n)},
                  compiler_params=pltpu.CompilerParams(has_side_effects=_DATAFLOW))(
                      *arrays, ex["send"], ex["recv"], after)
    return outs[:n], outs[n:]


def _scatter_start(name, grads):
    lands = [lax.empty((N_DEV - 1,) + g.shape[1:], g.dtype) for g in grads]
    return _exchange_start(name, grads, lands, "scatter")


ROW_MIX, ROW_MLP, ROW_CB, ROW_LG, ROW_LB, ROW_QN, ROW_KN, ROW_LOSS = 0, 2, 4, 5, 6, 7, 8, 9
ROW_META, ROW_CW, ROW_GN, SMALL_ROWS = 16, 32, 64, 72


def _sum_small(slots):
    def body(s_ref, o_ref):
        tot = s_ref[0]
        for s in range(1, N_DEV):
            tot = tot + s_ref[s]
        o_ref[...] = tot
        for row in (ROW_QN, ROW_KN):
            v = tot[row:row + 1, :]
            f = v[:, 0:128]
            for k in range(1, 8):
                f = f + v[:, 128 * k:128 * (k + 1)]
            o_ref[row:row + 1, 0:64] = f[:, 0:64] + f[:, 64:128]

    return _pcall(body, name="sum_small", out_shape=jax.ShapeDtypeStruct(slots.shape[1:], F32))(slots)


def _adamw_small(w, g, m, v):
    def body(w_ref, g_ref, m_ref, v_ref, d_out, m_out, v_out):
        d, mn, vn = _adamw_math(w_ref[...], g_ref[...], m_ref[...], v_ref[...])
        d_out[...] = d
        m_out[...] = mn
        v_out[...] = vn

    osh = jax.ShapeDtypeStruct(w.shape, F32)
    return _pcall(body, name="adamw_small", out_shape=[osh, osh, osh])(w, g, m, v)


def _local_step(h0, target, p, weight, emit):
    t = h0.shape[0]
    tables = _ret_tables(t)
    bd, later_tab, earlier_tab, bias_tab = _seg_tables(_sb_qb(t))
    row = lambda a, i: a[i:i + 1]

    hn_a = _rms_fwd("rms_mix0", h0, row(p["norm_mix_g"], 0))
    w_in = weight("w_in", hn_a)
    proj = _mm_cols("proj_in", hn_a, w_in, ())
    o_ret, states = _ret_fwd(proj, tables)
    gn_flat = p["gn_g"].reshape(1, 1024)
    cat = _gn_gate_fwd(o_ret, proj, gn_flat)
    cat, hdn, ycv = _conv_fwd(cat, proj, p["conv_w"], p["conv_b"], p["ln_g"], p["ln_b"])
    w_out = weight("w_out", cat)
    h1, hn_b = _mm_rows_norm("mix_out", cat, w_out, h0, row(p["norm_mlp_g"], 0))
    w1_0, w2_0 = weight("w1_0", hn_b), weight("w2_0", hn_b)
    a0, s0 = _mm_cols("mlp0_up", hn_b, w1_0, (), epi="relu2")
    h2, hn_c = _mm_rows_norm("mlp0_down", s0, w2_0, h1, row(p["norm_mix_g"], 1))

    w_qkv = weight("w_qkv", hn_c)
    qkv = _mm_cols("qkv", hn_c, w_qkv, ())
    qg = jnp.tile(p["qn_g"], (1, 16))
    kg = jnp.tile(p["kn_g"], (1, 16))
    qh, kt, k2, vt, v2 = _qk_norm_fwd(qkv, qg, kg, bd)
    o_sb, w_sb = _sb_fwd(qh, kt, v2, later_tab, bias_tab)
    w_o = weight("w_o", o_sb)
    h3, hn_d = _mm_rows_norm("attn_out", o_sb, w_o, h2, row(p["norm_mlp_g"], 1))
    w1_1, w2_1 = weight("w1_1", hn_d), weight("w2_1", hn_d)
    a1, s1 = _mm_cols("mlp1_up", hn_d, w1_1, (), epi="relu2")
    dh, loss = _mm_rows_loss("mlp1_down", s1, w2_1, h3, target)

    def mlp_bwd(tag, layer, w1, w2, dh, h_in, hn, a, s):
        da = _mm_rows_t(f"{tag}_dact", dh, w2, (), out_dtype=BF16, epi="drelu2", extra=a)
        dw2 = _wgrad_rows(f"{tag}_dw2", s, dh, 512)
        dw1 = _wgrad_cols(f"{tag}_dw1", hn, da, 512)
        tok = emit(tag, [dw1, dw2])
        return _mm_cols_t_rms(f"{tag}_dhn", da, w1, h_in, row(p["norm_mlp_g"], layer) + tok, dh)

    dh, dg_mlp1 = mlp_bwd("mlp1", 1, w1_1, w2_1, dh, h3, hn_d, a1, s1)

    do_sb = _mm_rows_t("attn_dout", dh, w_o, ())
    dw_o = _wgrad_rows("attn_dwo", o_sb, dh, 128)
    dq, dk, dv = _sb_bwd(qh, kt, k2, vt, w_sb, do_sb, earlier_tab, bias_tab)
    dqkv, dqg, dkg = _qk_norm_bwd(qkv, dq, dk, dv, qg, kg, bd)
    dw_qkv = _wgrad_cols("qkv_dw", hn_c, dqkv, 384)
    tok = emit("attn", [dw_qkv, dw_o])
    dh, dg_mix1 = _mm_cols_t_rms("qkv_dhn", dqkv, w_qkv, h2, row(p["norm_mix_g"], 1) + tok, dh)

    dh, dg_mlp0 = mlp_bwd("mlp0", 0, w1_0, w2_0, dh, h1, hn_b, a0, s0)

    dcat = _mm_rows_t("mix_dcat", dh, w_out, ())
    dw_out = _wgrad_rows("mix_dwout", cat, dh, 256)
    tok = emit("mix0_out", [dw_out])
    do_ret, dproj, dgn = _gn_gate_bwd(dcat, o_ret, proj, gn_flat + tok)
    dproj = _ret_bwd(dproj, proj, states, do_ret, tables)
    dy, dlg, dlb, dcb = _conv_bwd_ln(dcat, ycv, p["ln_g"], p["ln_b"])
    dproj, dug, dcw = _conv_bwd_taps(dproj, dy, hdn, proj, p["conv_w"])
    dproj = lax.dynamic_update_slice(dproj, dug, (0, 4096))
    dw_in = _wgrad_cols("proj_dw", hn_a, dproj, 640)
    tok = emit("mix0", [dw_in])
    dh, dg_mix0 = _mm_cols_t_rms("proj_dhn", dproj, w_in, h0, row(p["norm_mix_g"], 0) + tok, dh)

    rid = lax.broadcasted_iota(jnp.int32, (16, 1), 0)
    loss_row = jnp.broadcast_to(loss[0:1, 0:1], (1, D_MODEL))
    vecs = sum(jnp.where(rid == k, v, 0.0)
               for k, v in enumerate((dg_mix0, dg_mix1, dg_mlp0, dg_mlp1, dcb, dlg, dlb, dqg, dkg, loss_row)))
    small = jnp.concatenate([vecs, dh[PAD_FRONT:TOK0], dcw, jnp.where(rid[:8] == 0, dgn, 0.0)], axis=0)
    return dh[TOK0:], small


_SMALL_NAMES = ("meta", "norm_mix_g", "norm_mlp_g", "even_ret_gn_g", "even_conv_w", "even_conv_b",
                "even_conv_ln_g", "even_conv_ln_b", "odd_q_norm_g", "odd_k_norm_g")
_BIG_NAMES = ("even_w_in", "even_w_out", "odd_w_qkv", "odd_w_o", "mlp_w1", "mlp_w2")
_ORDER = ("meta", "norm_mix_g", "norm_mlp_g", "even_w_in", "even_ret_gn_g", "even_conv_w", "even_conv_b",
          "even_conv_ln_g", "even_conv_ln_b", "even_w_out", "odd_w_qkv", "odd_q_norm_g", "odd_k_norm_g",
          "odd_w_o", "mlp_w1", "mlp_w2")


def _pack128(a):
    flat = a.reshape(-1)
    n = flat.shape[0]
    rows = -(-n // 128)
    rows8 = -(-rows // 8) * 8
    return jnp.pad(flat, (0, rows8 * 128 - n)).reshape(rows8, 128)


def kernel(x, meta, norm_mix_g, norm_mlp_g, even_w_in, even_ret_gn_g, even_conv_w, even_conv_b, even_conv_ln_g, even_conv_ln_b, even_w_out, odd_w_qkv, odd_q_norm_g, odd_k_norm_g, odd_w_o, mlp_w1, mlp_w2, loss_target, m_meta, m_norm_mix_g, m_norm_mlp_g, m_even_w_in, m_even_ret_gn_g, m_even_conv_w, m_even_conv_b, m_even_conv_ln_g, m_even_conv_ln_b, m_even_w_out, m_odd_w_qkv, m_odd_q_norm_g, m_odd_k_norm_g, m_odd_w_o, m_mlp_w1, m_mlp_w2, v_meta, v_norm_mix_g, v_norm_mlp_g, v_even_w_in, v_even_ret_gn_g, v_even_conv_w, v_even_conv_b, v_even_conv_ln_g, v_even_conv_ln_b, v_even_w_out, v_odd_w_qkv, v_odd_q_norm_g, v_odd_k_norm_g, v_odd_w_o, v_mlp_w1, v_mlp_w2):
    w = dict(meta=meta, norm_mix_g=norm_mix_g, norm_mlp_g=norm_mlp_g, even_w_in=even_w_in,
             even_ret_gn_g=even_ret_gn_g, even_conv_w=even_conv_w, even_conv_b=even_conv_b,
             even_conv_ln_g=even_conv_ln_g, even_conv_ln_b=even_conv_ln_b, even_w_out=even_w_out,
             odd_w_qkv=odd_w_qkv, odd_q_norm_g=odd_q_norm_g, odd_k_norm_g=odd_k_norm_g, odd_w_o=odd_w_o,
             mlp_w1=mlp_w1, mlp_w2=mlp_w2)
    mom = dict(meta=m_meta, norm_mix_g=m_norm_mix_g, norm_mlp_g=m_norm_mlp_g, even_w_in=m_even_w_in,
               even_ret_gn_g=m_even_ret_gn_g, even_conv_w=m_even_conv_w, even_conv_b=m_even_conv_b,
               even_conv_ln_g=m_even_conv_ln_g, even_conv_ln_b=m_even_conv_ln_b, even_w_out=m_even_w_out,
               odd_w_qkv=m_odd_w_qkv, odd_q_norm_g=m_odd_q_norm_g, odd_k_norm_g=m_odd_k_norm_g, odd_w_o=m_odd_w_o,
               mlp_w1=m_mlp_w1, mlp_w2=m_mlp_w2)
    var = dict(meta=v_meta, norm_mix_g=v_norm_mix_g, norm_mlp_g=v_norm_mlp_g, even_w_in=v_even_w_in,
               even_ret_gn_g=v_even_ret_gn_g, even_conv_w=v_even_conv_w, even_conv_b=v_even_conv_b,
               even_conv_ln_g=v_even_conv_ln_g, even_conv_ln_b=v_even_conv_ln_b, even_w_out=v_even_w_out,
               odd_w_qkv=v_odd_w_qkv, odd_q_norm_g=v_odd_q_norm_g, odd_k_norm_g=v_odd_k_norm_g, odd_w_o=v_odd_w_o,
               mlp_w1=v_mlp_w1, mlp_w2=v_mlp_w2)
    me = 4 * lax.axis_index("x") + 2 * lax.axis_index("y") + lax.axis_index("c")

    small_in = jnp.concatenate([meta, jnp.pad(even_conv_w[0], ((0, 1), (0, 0))),
                                jnp.pad(even_ret_gn_g[0], ((0, 4), (0, 96)))], axis=0)
    b16 = lambda a: a.astype(BF16)
    later_src = dict(w_out=b16(even_w_out[0]), w1_0=b16(mlp_w1[0]), w2_0=b16(mlp_w2[0]),
                     w_qkv=b16(odd_w_qkv[0]), w_o=b16(odd_w_o[0]), w1_1=b16(mlp_w1[1]), w2_1=b16(mlp_w2[1]))
    landed = _gather_first([b16(even_w_in[0]), small_in], list(later_src.values()))
    g_in, g_small = landed[0], landed[1]
    own_slot = dict(zip(later_src, landed[2:]))
    groups = (("gather_l0", ("w_out", "w1_0", "w2_0")), ("gather_attn", ("w_qkv", "w_o")),
              ("gather_l1", ("w1_1", "w2_1")))
    pending = {}
    gather_tok = jnp.zeros((), F32)
    for gname, names in groups:
        ex = _exchange_start(gname, [later_src[n] for n in names], [own_slot[n] for n in names], "gather")
        gather_tok = gather_tok + ex["token"]
        for n in names:
            pending[n] = (ex, names)
    arrived = dict(w_in=g_in)

    def weight(name, after):
        if name not in arrived:
            ex, names = pending[name]
            arrived.update(zip(names, _exchange_wait(ex, after)[1]))
        return arrived[name]

    cols = lambda a: jnp.transpose(a, (1, 0, 2)).reshape(a.shape[1], -1)
    p = dict(norm_mix_g=norm_mix_g + gather_tok, norm_mlp_g=norm_mlp_g, conv_b=even_conv_b, ln_g=even_conv_ln_g,
             ln_b=even_conv_ln_b, qn_g=odd_q_norm_g, kn_g=odd_k_norm_g,
             gn_g=cols(g_small[:, 48:52, :32]),
             conv_w=jnp.pad(cols(g_small[:, 16:47]), ((0, 1), (0, 0))))
    meta_full = cols(g_small[:, 0:16])

    scatters = {}

    def emit(tag, grads):
        scatters[tag] = _scatter_start("scatter_" + tag, grads)
        return scatters[tag]["token"]

    h0 = jnp.concatenate([jnp.zeros((PAD_FRONT, D_MODEL), F32), meta_full, x[0]], axis=0)
    target = jnp.concatenate([jnp.zeros((TOK0, D_MODEL), F32), loss_target[0]], axis=0)
    grad_x, small_part = _local_step(h0, target, p, weight, emit)

    out = {}
    got = {}

    def update(names, terms, after):
        for tag in {t for name in names for t, _ in terms[name]} - set(got):
            got[tag] = _exchange_wait(scatters[tag], after)
        for name in names:
            owns, recvs = zip(*[(got[t][0][j], got[t][1][j]) for t, j in terms[name]])
            out[name] = _adamw("adamw_" + name, w[name], list(owns), list(recvs), mom[name], var[name], me)

    terms = dict(even_w_in=[("mix0", 0)], even_w_out=[("mix0_out", 0)], odd_w_qkv=[("attn", 0)], odd_w_o=[("attn", 1)],
                 mlp_w1=[("mlp0", 0), ("mlp1", 0)], mlp_w2=[("mlp0", 1), ("mlp1", 1)])
    small_ex = _exchange_start("small", [small_part], [lax.empty((N_DEV,) + small_part.shape, F32)], "gather")
    update(("mlp_w1", "mlp_w2", "odd_w_qkv", "odd_w_o", "even_w_out"), terms, grad_x)
    update(("even_w_in",), terms, out["even_w_out"][1])
    (own_part,), (slots,) = _exchange_wait(small_ex, out["even_w_in"][1])
    tot = _sum_small(lax.dynamic_update_slice(slots, own_part[None], (me, 0, 0)))
    loss = tot[ROW_LOSS, 0]

    shard_cols = lambda a, width: lax.dynamic_slice_in_dim(a, me * width, width, axis=1)
    one = lambda r: tot[r:r + 1]
    small_g = dict(
        norm_mix_g=tot[ROW_MIX:ROW_MIX + 2], norm_mlp_g=tot[ROW_MLP:ROW_MLP + 2],
        even_conv_b=one(ROW_CB), even_conv_ln_g=one(ROW_LG), even_conv_ln_b=one(ROW_LB),
        odd_q_norm_g=one(ROW_QN)[:, :64], odd_k_norm_g=one(ROW_KN)[:, :64],
        meta=shard_cols(tot[ROW_META:ROW_META + N_META], 128),
        even_conv_w=shard_cols(tot[ROW_CW:ROW_CW + CONV_WIDTH], 128)[None],
        even_ret_gn_g=shard_cols(tot[ROW_GN].reshape(4, 256), 32)[None])
    packs = {n: (_pack128(w[n]), _pack128(small_g[n]), _pack128(mom[n]), _pack128(var[n])) for n in _SMALL_NAMES}
    cat4 = [jnp.concatenate([packs[n][i] for n in _SMALL_NAMES], axis=0) for i in range(4)]
    d_s, m_s, v_s = _adamw_small(*cat4)
    r0 = 0
    for n in _SMALL_NAMES:
        rows = packs[n][0].shape[0]
        size = w[n].size
        take = lambda a: a[r0:r0 + rows].reshape(-1)[:size].reshape(w[n].shape)
        out[n] = (small_g[n].reshape(w[n].shape), take(d_s), take(m_s), take(v_s))
        r0 += rows

    res = [loss, grad_x[None]]
    for i in range(4):
        res.extend(out[n][i] for n in _ORDER)
    return tuple(res)
```

```python
import functools

import numpy as np
import jax
import jax.numpy as jnp
from jax import lax
from jax.experimental import pallas as pl
from jax.experimental.pallas import tpu as pltpu

F32 = jnp.float32
BF16 = jnp.bfloat16

D_MODEL = 1024
N_META = 16
CHUNK = 128
PAD_FRONT = 112
TOK0 = PAD_FRONT + N_META
EPS = 1e-6
N_DEV = 8
RET_HEADS = 4
RET_DECAY_OFFSET = 5.0
ROPE_BASE = 10000.0
CONV_WIDTH = 31
HALO = 32
SB_SCALE = 64 ** -0.5
RET_SCALE = 128 ** -0.5
ADAM_LR, ADAM_B1, ADAM_B2, ADAM_EPS, ADAM_WD, ADAM_STEP = 0.001, 0.9, 0.999, 1e-08, 0.01, 10
VMEM_LIMIT = 56 * 1024 * 1024
MESH = pl.DeviceIdType.MESH


def _pcall(body, **kw):
    return pl.pallas_call(body, **kw)


def _params(**kw):
    return pltpu.CompilerParams(vmem_limit_bytes=VMEM_LIMIT, **kw)


def _tile(n, cands):
    for c in cands:
        if n % c == 0:
            return c
    raise ValueError(f"no tile for {n} in {cands}")


def _sigmoid(x):
    return 1.0 / (1.0 + jnp.exp(-x))


_DIMS = {
    "nn": (((1,), (0,)), ((), ())),
    "nt": (((1,), (1,)), ((), ())),
    "tn": (((0,), (0,)), ((), ())),
}


def _matmul(name, a, b, *, grid, a_spec, b_spec, o_spec, out_shape, contract, acc_shape,
            epi="plain", extra=None, extra_spec=None):
    nk = grid[2]
    dims = _DIMS[contract]
    n_in = 3 if extra is not None else 2
    n_out = 2 if epi == "relu2" else 1

    def body(*refs):
        a_ref, b_ref = refs[0], refs[1]
        e_ref = refs[2] if extra is not None else None
        outs = refs[n_in:n_in + n_out]
        acc = refs[-1]
        k = pl.program_id(2)
        part = lax.dot_general(a_ref[...].astype(BF16), b_ref[...].astype(BF16), dims, preferred_element_type=F32)
        if nk > 1:
            @pl.when(k == 0)
            def _():
                acc[...] = jnp.zeros_like(acc)

            acc[...] += part

        @pl.when(k == nk - 1)
        def _():
            r = acc[...] if nk > 1 else part
            if epi == "plain":
                outs[0][...] = r.astype(outs[0].dtype)
            elif epi == "residual":
                outs[0][...] = (r + e_ref[...]).astype(outs[0].dtype)
            elif epi == "relu2":
                outs[0][...] = r
                rr = jnp.maximum(r, 0.0)
                outs[1][...] = (rr * rr).astype(BF16)
            elif epi == "drelu2":
                outs[0][...] = (r * (2.0 * jnp.maximum(e_ref[...], 0.0))).astype(outs[0].dtype)

    in_specs = [a_spec, b_spec] + ([extra_spec] if extra is not None else [])
    args = (a, b) + ((extra,) if extra is not None else ())
    if n_out == 2:
        out_specs = [o_spec, o_spec]
    else:
        out_specs = o_spec
    return _pcall(body, name=name, grid=grid, in_specs=in_specs, out_specs=out_specs,
                  out_shape=out_shape, scratch_shapes=[pltpu.VMEM(acc_shape, F32)],
                  compiler_params=_params(dimension_semantics=("parallel", "parallel", "arbitrary")))(*args)


def _tm(t):
    return _tile(t, (1408, 768, 384, 128))


def _mm_cols(name, a, wb, lead, out_dtype=F32, epi="plain"):
    t, kdim = a.shape
    n = wb.shape[-1]
    tm, tk = _tm(t), _tile(kdim, (1024, 512))
    nl = len(lead)
    b_spec = pl.BlockSpec((None,) * (1 + nl) + (tk, n), lambda i, j, k: (j,) + lead + (k, 0))
    o_spec = pl.BlockSpec((tm, n), lambda i, j, k: (i, j))
    if epi == "relu2":
        out_shape = [jax.ShapeDtypeStruct((t, N_DEV * n), F32), jax.ShapeDtypeStruct((t, N_DEV * n), BF16)]
    else:
        out_shape = jax.ShapeDtypeStruct((t, N_DEV * n), out_dtype)
    return _matmul(name, a, wb, grid=(t // tm, N_DEV, kdim // tk),
                   a_spec=pl.BlockSpec((tm, tk), lambda i, j, k: (i, k)), b_spec=b_spec, o_spec=o_spec,
                   out_shape=out_shape, contract="nn", acc_shape=(tm, n), epi=epi)


def _tm_deep(t, kdim):
    return _tm(t) if kdim <= 2048 else _tile(t, (704, 384, 128))


def _mm_cols_t_rms(name, a, wb, h, g, dres):
    t = a.shape[0]
    nb, kdim, n = wb.shape
    tm = _tile(t, (704, 384, 128))

    def body(a_ref, b_ref, h_ref, g_ref, r_ref, o_ref, dg_ref):
        @pl.when(pl.program_id(0) == 0)
        def _():
            dg_ref[...] = jnp.zeros_like(dg_ref)

        d = _dot(a_ref[:, 0:n].astype(BF16), b_ref[0], "nt")
        for j in range(1, nb):
            d = d + _dot(a_ref[:, j * n:(j + 1) * n].astype(BF16), b_ref[j], "nt")
        x = h_ref[...]
        rs = lax.rsqrt(jnp.mean(x * x, axis=-1, keepdims=True) + EPS)
        u = d * g_ref[...]
        m = jnp.mean(u * x, axis=-1, keepdims=True)
        o_ref[...] = r_ref[...] + rs * u - x * (rs * rs * rs * m)
        dg_ref[...] += jnp.sum(d * x * rs, axis=0, keepdims=True)

    row = pl.BlockSpec((tm, kdim), lambda i: (i, 0))
    vec = pl.BlockSpec((1, kdim), lambda i: (0, 0))
    return _pcall(body, name=name, grid=(t // tm,),
                  in_specs=[pl.BlockSpec((tm, nb * n), lambda i: (i, 0)),
                            pl.BlockSpec((nb, kdim, n), lambda i: (0, 0, 0)), row, vec, row],
                  out_specs=[row, vec],
                  out_shape=[jax.ShapeDtypeStruct((t, kdim), F32), jax.ShapeDtypeStruct((1, kdim), F32)],
                  compiler_params=_params(dimension_semantics=("arbitrary",)))(a, wb, h, g, dres)


def _mm_rows_t(name, a, wb, lead, out_dtype=F32, epi="plain", extra=None):
    t, n = a.shape
    r = wb.shape[-2]
    tm, tk = _tm(t), _tile(n, (1024,))
    nl = len(lead)
    b_spec = pl.BlockSpec((None,) * (1 + nl) + (r, tk), lambda i, j, k: (j,) + lead + (0, k))
    o_spec = pl.BlockSpec((tm, r), lambda i, j, k: (i, j))
    return _matmul(name, a, wb, grid=(t // tm, N_DEV, n // tk),
                   a_spec=pl.BlockSpec((tm, tk), lambda i, j, k: (i, k)), b_spec=b_spec, o_spec=o_spec,
                   out_shape=jax.ShapeDtypeStruct((t, N_DEV * r), out_dtype), contract="nt",
                   acc_shape=(tm, r), epi=epi, extra=extra, extra_spec=o_spec if extra is not None else None)


def _mm_rows_loss(name, a, wb, residual, target):
    t = a.shape[0]
    nb, r, n = wb.shape
    tm, tn = _tm_deep(t, nb * r), _tile(n, (512,))

    def body(a_ref, b_ref, r_ref, t_ref, d_ref, l_ref):
        i = pl.program_id(0)

        @pl.when((i == 0) & (pl.program_id(1) == 0))
        def _():
            l_ref[...] = jnp.zeros_like(l_ref)

        y = r_ref[...] + _dot(a_ref[...].astype(BF16), b_ref[...].reshape(nb * r, tn))
        diff = jnp.where(_row_ids(i, tm) >= TOK0, y - t_ref[...], 0.0)
        d_ref[...] = diff * (1.0 / D_MODEL)
        l_ref[...] += jnp.sum(diff * diff) * (0.5 / D_MODEL)

    o_spec = pl.BlockSpec((tm, tn), lambda i, j: (i, j))
    return _pcall(body, name=name, grid=(t // tm, n // tn),
                  in_specs=[pl.BlockSpec((tm, nb * r), lambda i, j: (i, 0)),
                            pl.BlockSpec((nb, r, tn), lambda i, j: (0, 0, j)), o_spec, o_spec],
                  out_specs=[o_spec, pl.BlockSpec((8, 128), lambda i, j: (0, 0))],
                  out_shape=[jax.ShapeDtypeStruct((t, n), F32), jax.ShapeDtypeStruct((8, 128), F32)],
                  compiler_params=_params(dimension_semantics=("arbitrary", "arbitrary")))(a, wb, residual, target)


def _mm_rows_norm(name, a, wb, residual, g):
    t = a.shape[0]
    nb, r, n = wb.shape
    tm = _tile(t, (704, 384, 128))

    def body(a_ref, b_ref, r_ref, g_ref, h_ref, hn_ref):
        h = r_ref[...] + _dot(a_ref[...].astype(BF16), b_ref[...].reshape(nb * r, n))
        h_ref[...] = h
        hn_ref[...] = (h * lax.rsqrt(jnp.mean(h * h, axis=-1, keepdims=True) + EPS) * g_ref[...]).astype(BF16)

    row = pl.BlockSpec((tm, n), lambda i: (i, 0))
    return _pcall(body, name=name, grid=(t // tm,),
                  in_specs=[pl.BlockSpec((tm, nb * r), lambda i: (i, 0)), pl.BlockSpec((nb, r, n), lambda i: (0, 0, 0)),
                            row, pl.BlockSpec((1, n), lambda i: (0, 0))],
                  out_specs=[row, row],
                  out_shape=[jax.ShapeDtypeStruct((t, n), F32), jax.ShapeDtypeStruct((t, n), BF16)],
                  compiler_params=_params(dimension_semantics=("parallel",)))(a, wb, residual, g)


def _wgrad_cols(name, x, dy, n):
    t, kdim = x.shape
    tk = _tm(t)
    return _matmul(name, x, dy, grid=(1, N_DEV, t // tk),
                   a_spec=pl.BlockSpec((tk, kdim), lambda i, j, k: (k, 0)),
                   b_spec=pl.BlockSpec((tk, n), lambda i, j, k: (k, j)),
                   o_spec=pl.BlockSpec((None, kdim, n), lambda i, j, k: (j, 0, 0)),
                   out_shape=jax.ShapeDtypeStruct((N_DEV, kdim, n), BF16), contract="tn", acc_shape=(kdim, n))


def _wgrad_rows(name, x, dy, r):
    t = x.shape[0]
    n = dy.shape[1]
    tk, tn = _tm(t), _tile(n, (512,))
    tm = min(N_DEV * r, 1024)
    out = _matmul(name, x, dy, grid=(N_DEV * r // tm, n // tn, t // tk),
                  a_spec=pl.BlockSpec((tk, tm), lambda i, j, k: (k, i)),
                  b_spec=pl.BlockSpec((tk, tn), lambda i, j, k: (k, j)),
                  o_spec=pl.BlockSpec((tm, tn), lambda i, j, k: (i, j)),
                  out_shape=jax.ShapeDtypeStruct((N_DEV * r, n), BF16), contract="tn", acc_shape=(tm, tn))
    return out.reshape(N_DEV, r, n)


def _rows(t):
    return _tile(t, (384, 128))


def _rms_fwd(name, h, g):
    t = h.shape[0]
    tr = _rows(t)

    def body(h_ref, g_ref, o_ref):
        x = h_ref[...]
        r = lax.rsqrt(jnp.mean(x * x, axis=-1, keepdims=True) + EPS)
        o_ref[...] = (x * r * g_ref[...]).astype(BF16)

    row = pl.BlockSpec((tr, D_MODEL), lambda i: (i, 0))
    vec = pl.BlockSpec((1, D_MODEL), lambda i: (0, 0))
    return _pcall(body, name=name, grid=(t // tr,), in_specs=[row, vec], out_specs=row,
                  out_shape=jax.ShapeDtypeStruct((t, D_MODEL), BF16))(h, g)


def _ret_tables(t):
    hh = np.arange(RET_HEADS, dtype=np.float64)
    log_g = np.log1p(-np.exp2(-RET_DECAY_OFFSET - hh))
    idx = np.arange(CHUNK, dtype=np.float64)
    diff = idx[:, None] - idx[None, :]
    dmat = np.where(diff[None] >= 0, np.exp(np.maximum(diff, 0.0)[None] * log_g[:, None, None]), 0.0)
    qdec = np.exp((idx + 1.0)[None, :, None] * log_g[:, None, None]) * np.ones((1, 1, CHUNK))
    kdec = np.exp((CHUNK - 1 - idx)[None, :, None] * log_g[:, None, None]) * np.ones((1, 1, CHUNK))
    half = CHUNK // 2
    inv_freq = (ROPE_BASE ** (-np.arange(half, dtype=np.float32) / half)).astype(np.float32)
    ang = (np.arange(t, dtype=np.float32)[:, None] * inv_freq[None, :]).astype(np.float32).astype(np.float64)
    cos2 = np.concatenate([np.cos(ang), np.cos(ang)], axis=1)
    sin2 = np.concatenate([-np.sin(ang), np.sin(ang)], axis=1)
    return tuple(jnp.asarray(v, F32) for v in (dmat, qdec, kdec, cos2, sin2))


def _rot(x, c, s):
    return x * c + pltpu.roll(x, CHUNK // 2, 1) * s


def _unrot(dx, c, s):
    return dx * c + pltpu.roll(dx * s, CHUNK // 2, 1)


def _dot(a, b, contract="nn"):
    return lax.dot_general(a, b, _DIMS[contract], preferred_element_type=F32)


def _ret_fwd(proj, gn_g, tables):
    t = proj.shape[0]
    nch = t // CHUNK
    dmat, qdec, kdec, cos2, sin2 = tables

    def body(qk_ref, v_ref, g_ref, w_ref, c_ref, s_ref, dm_ref, qd_ref, kd_ref, o_ref, st_ref, cat_ref, state):
        @pl.when(pl.program_id(0) == 0)
        def _():
            state[...] = jnp.zeros_like(state)

        c, s = c_ref[...], s_ref[...]
        for h in range(RET_HEADS):
            q = _rot(qk_ref[:, 128 * h:128 * (h + 1)], c, s)
            k = _rot(qk_ref[:, 512 + 128 * h:512 + 128 * (h + 1)], c, s) * RET_SCALE
            vb = v_ref[:, 256 * h:256 * (h + 1)].astype(BF16)
            st = state[h]
            st_ref[h] = st
            sc = _dot(q.astype(BF16), k.astype(BF16), "nt") * dm_ref[h]
            o = _dot(sc.astype(BF16), vb)
            o += _dot((q * qd_ref[h]).astype(BF16), st.astype(BF16))
            sl = slice(256 * h, 256 * (h + 1))
            o_ref[:, sl] = o
            kv = _dot((k * kd_ref[h]).astype(BF16), vb, "tn")
            state[h] = qd_ref[h, CHUNK - 1:CHUNK, 0:1] * st + kv
            mu = jnp.mean(o, axis=-1, keepdims=True)
            oc = o - mu
            rstd = lax.rsqrt(jnp.mean(oc * oc, axis=-1, keepdims=True) + EPS)
            g = g_ref[:, sl]
            cat_ref[:, sl] = (g * _sigmoid(g) * (oc * rstd * w_ref[:, sl])).astype(BF16)

    tab = pl.BlockSpec((RET_HEADS, CHUNK, CHUNK), lambda n: (0, 0, 0))
    pos = pl.BlockSpec((CHUNK, CHUNK), lambda n: (n, 0))
    row = pl.BlockSpec((CHUNK, 1024), lambda n: (n, 0))
    return _pcall(
        body, name="ret_fwd", grid=(nch,),
        in_specs=[row, pl.BlockSpec((CHUNK, 1024), lambda n: (n, 1)), pl.BlockSpec((CHUNK, 1024), lambda n: (n, 2)),
                  pl.BlockSpec((1, 1024), lambda n: (0, 0)), pos, pos, tab, tab, tab],
        out_specs=[row, pl.BlockSpec((RET_HEADS, None, 128, 256), lambda n: (0, n, 0, 0)), row],
        out_shape=[jax.ShapeDtypeStruct((t, 1024), F32), jax.ShapeDtypeStruct((RET_HEADS, nch, 128, 256), F32),
                   jax.ShapeDtypeStruct((t, 2048), BF16)],
        scratch_shapes=[pltpu.VMEM((RET_HEADS, 128, 256), F32)],
        compiler_params=_params(dimension_semantics=("arbitrary",)))(
            proj, proj, proj, gn_g, cos2, sin2, dmat, qdec, kdec)


def _ret_bwd(dproj, proj, states, do, tables):
    t = proj.shape[0]
    nch = t // CHUNK
    dmat, qdec, kdec, cos2, sin2 = tables

    def body(dp_in, qk_ref, v_ref, do_ref, st_ref, c_ref, s_ref, dm_ref, qd_ref, kd_ref, dp_ref, rst):
        del dp_in
        @pl.when(pl.program_id(0) == 0)
        def _():
            rst[...] = jnp.zeros_like(rst)

        c, s = c_ref[...], s_ref[...]
        for h in range(RET_HEADS):
            q = _rot(qk_ref[:, 128 * h:128 * (h + 1)], c, s)
            k = _rot(qk_ref[:, 512 + 128 * h:512 + 128 * (h + 1)], c, s) * RET_SCALE
            qb, kb = q.astype(BF16), k.astype(BF16)
            vb = v_ref[:, 256 * h:256 * (h + 1)].astype(BF16)
            dob = do_ref[:, 256 * h:256 * (h + 1)].astype(BF16)
            pb = st_ref[h].astype(BF16)
            r = rst[h]
            rb = r.astype(BF16)
            dm, qd, kd = dm_ref[h], qd_ref[h], kd_ref[h]
            sb = (_dot(qb, kb, "nt") * dm).astype(BF16)
            dsb = (_dot(dob, vb, "nt") * dm).astype(BF16)
            dq = _dot(dsb, kb) + _dot(dob, pb, "nt") * qd
            dk = _dot(dsb, qb, "tn") + _dot(vb, rb, "nt") * kd
            dv = _dot(sb, dob, "tn") + _dot((k * kd).astype(BF16), rb)
            rst[h] = _dot((q * qd).astype(BF16), dob, "tn") + qd[CHUNK - 1:CHUNK, 0:1] * r
            dp_ref[:, 128 * h:128 * (h + 1)] = _unrot(dq, c, s).astype(BF16)
            dp_ref[:, 512 + 128 * h:512 + 128 * (h + 1)] = (_unrot(dk, c, s) * RET_SCALE).astype(BF16)
            dp_ref[:, 1024 + 256 * h:1024 + 256 * (h + 1)] = dv.astype(BF16)

    rev = lambda n: nch - 1 - n
    tab = pl.BlockSpec((RET_HEADS, CHUNK, CHUNK), lambda n: (0, 0, 0))
    pos = pl.BlockSpec((CHUNK, CHUNK), lambda n: (rev(n), 0))
    row = pl.BlockSpec((CHUNK, 1024), lambda n: (rev(n), 0))
    return _pcall(
        body, name="ret_bwd", grid=(nch,),
        in_specs=[pl.BlockSpec(memory_space=pl.ANY), row, pl.BlockSpec((CHUNK, 1024), lambda n: (rev(n), 1)), row,
                  pl.BlockSpec((RET_HEADS, None, 128, 256), lambda n: (0, rev(n), 0, 0)),
                  pos, pos, tab, tab, tab],
        out_specs=pl.BlockSpec((CHUNK, 2048), lambda n: (rev(n), 0)),
        out_shape=jax.ShapeDtypeStruct((t, 5120), BF16),
        scratch_shapes=[pltpu.VMEM((RET_HEADS, 128, 256), F32)], input_output_aliases={0: 0},
        compiler_params=_params(dimension_semantics=("arbitrary",)))(
            dproj, proj, proj, do, states, cos2, sin2, dmat, qdec, kdec)


def _row_ids(i, tr):
    return i * tr + lax.broadcasted_iota(jnp.int32, (tr, 1), 0)


SH_ROWS = HALO - 8


def _shifted_copies(xs, sh, tr):
    for b in range(1, 8):
        sh[b - 1] = xs[pl.ds(b, tr + SH_ROWS), :]


def _shifted(xs, sh, off, tr):
    a, b = divmod(off, 8)
    return xs[pl.ds(8 * a, tr), :] if b == 0 else sh[b - 1, pl.ds(8 * a, tr), :]


def _conv_fwd(cat, proj, conv_w, conv_b, ln_g, ln_b):
    t = proj.shape[0]
    tr = _rows(t)
    hb = tr // HALO

    def body(cat_in, ua_ref, ug_ref, pa_ref, pg_ref, w_ref, b_ref, lg_ref, lb_ref, c_ref, hd_ref, y_ref, xs, sh):
        del cat_in
        i = pl.program_id(0)
        hdn = ua_ref[...] * _sigmoid(ug_ref[...])
        hd_ref[...] = hdn
        prev = pa_ref[...] * _sigmoid(pg_ref[...])
        xs[0:HALO, :] = jnp.where(i > 0, prev, 0.0)
        xs[HALO:HALO + tr, :] = hdn
        _shifted_copies(xs, sh, tr)
        acc = jnp.zeros((tr, 1024), F32) + b_ref[...]
        for w in range(CONV_WIDTH):
            acc += w_ref[w:w + 1, :] * _shifted(xs, sh, HALO - (CONV_WIDTH - 1) + w, tr)
        y_ref[...] = acc
        mu = jnp.mean(acc, axis=-1, keepdims=True)
        yc = acc - mu
        rstd = lax.rsqrt(jnp.mean(yc * yc, axis=-1, keepdims=True) + EPS)
        yn = yc * rstd * lg_ref[...] + lb_ref[...]
        c = yn * _sigmoid(yn)
        c_ref[...] = jnp.where(_row_ids(i, tr) >= PAD_FRONT, c, 0.0).astype(BF16)

    row = pl.BlockSpec((tr, 1024), lambda i: (i, 0))
    vec = pl.BlockSpec((1, 1024), lambda i: (0, 0))
    halo = lambda col: pl.BlockSpec((HALO, 1024), lambda i: (jnp.maximum(i * hb - 1, 0), col))
    return _pcall(body, name="conv_fwd", grid=(t // tr,),
                  in_specs=[pl.BlockSpec(memory_space=pl.ANY),
                            pl.BlockSpec((tr, 1024), lambda i: (i, 3)), pl.BlockSpec((tr, 1024), lambda i: (i, 4)),
                            halo(3), halo(4), pl.BlockSpec((32, 1024), lambda i: (0, 0)), vec, vec, vec],
                  out_specs=[pl.BlockSpec((tr, 1024), lambda i: (i, 1)), row, row],
                  out_shape=[jax.ShapeDtypeStruct((t, 2048), BF16), jax.ShapeDtypeStruct((t, 1024), F32),
                             jax.ShapeDtypeStruct((t, 1024), F32)],
                  scratch_shapes=[pltpu.VMEM((tr + HALO, 1024), F32), pltpu.VMEM((7, tr + SH_ROWS, 1024), F32)],
                  input_output_aliases={0: 0}, compiler_params=_params())(
                      cat, proj, proj, proj, proj, conv_w, conv_b, ln_g, ln_b)


def _mix_bwd_head(dh, w_out, o, proj, gn_g, y, ln_g, ln_b):
    t = dh.shape[0]
    tr = _rows(t)
    nb, r, n = w_out.shape

    def body(dh_ref, b_ref, o_ref, g_ref, w_ref, y_ref, lg_ref, lb_ref,
             do_ref, dp_ref, dw_ref, dy_ref, dlg_ref, dlb_ref, dcb_ref):
        i = pl.program_id(0)

        @pl.when(i == 0)
        def _():
            for ref in (dw_ref, dlg_ref, dlb_ref, dcb_ref):
                ref[...] = jnp.zeros_like(ref)

        dcat = _dot(dh_ref[...].astype(BF16), b_ref[...].reshape(nb * r, n), "nt")
        for h in range(RET_HEADS):
            sl = slice(256 * h, 256 * (h + 1))
            x = o_ref[:, sl]
            mu = jnp.mean(x, axis=-1, keepdims=True)
            xc = x - mu
            rstd = lax.rsqrt(jnp.mean(xc * xc, axis=-1, keepdims=True) + EPS)
            xh = xc * rstd
            w = w_ref[:, sl]
            g = g_ref[:, sl]
            sg = _sigmoid(g)
            d = dcat[:, sl]
            don = d * (g * sg)
            dp_ref[:, sl] = (d * (xh * w) * (sg * (1.0 + g * (1.0 - sg)))).astype(BF16)
            dw_ref[:, sl] += jnp.sum(don * xh, axis=0, keepdims=True)
            dxh = don * w
            m1 = jnp.mean(dxh, axis=-1, keepdims=True)
            m2 = jnp.mean(dxh * xh, axis=-1, keepdims=True)
            do_ref[:, sl] = rstd * (dxh - m1 - xh * m2)
        yv = y_ref[...]
        mu = jnp.mean(yv, axis=-1, keepdims=True)
        yc = yv - mu
        rstd = lax.rsqrt(jnp.mean(yc * yc, axis=-1, keepdims=True) + EPS)
        xh = yc * rstd
        lg = lg_ref[...]
        yn = xh * lg + lb_ref[...]
        sg = _sigmoid(yn)
        dyn = jnp.where(_row_ids(i, tr) >= PAD_FRONT, dcat[:, 1024:] * (sg * (1.0 + yn * (1.0 - sg))), 0.0)
        dlg_ref[...] += jnp.sum(dyn * xh, axis=0, keepdims=True)
        dlb_ref[...] += jnp.sum(dyn, axis=0, keepdims=True)
        dxh = dyn * lg
        m1 = jnp.mean(dxh, axis=-1, keepdims=True)
        m2 = jnp.mean(dxh * xh, axis=-1, keepdims=True)
        dy = rstd * (dxh - m1 - xh * m2)
        dy_ref[...] = dy
        dcb_ref[...] += jnp.sum(dy, axis=0, keepdims=True)

    row = pl.BlockSpec((tr, 1024), lambda i: (i, 0))
    vec = pl.BlockSpec((1, 1024), lambda i: (0, 0))
    gate = pl.BlockSpec((tr, 1024), lambda i: (i, 2))
    vsh = jax.ShapeDtypeStruct((1, 1024), F32)
    fsh = jax.ShapeDtypeStruct((t, 1024), F32)
    return _pcall(body, name="mix_bwd_head", grid=(t // tr,),
                  in_specs=[row, pl.BlockSpec((nb, r, n), lambda i: (0, 0, 0)), row, gate, vec, row, vec, vec],
                  out_specs=[row, gate, vec, row, vec, vec, vec],
                  out_shape=[fsh, jax.ShapeDtypeStruct((t, 5120), BF16), vsh, fsh, vsh, vsh, vsh],
                  compiler_params=_params(dimension_semantics=("arbitrary",)))(
                      dh, w_out, o, proj, gn_g, y, ln_g, ln_b)


def _conv_bwd_taps(dproj, dy, hdn, proj, conv_w):
    t = dy.shape[0]
    tr = _rows(t)
    hb = tr // HALO
    nt = t // tr

    def body(dp_in, dy_ref, nx_ref, hd_ref, ph_ref, ua_ref, ug_ref, w_ref, da_ref, dg_ref, dw_ref, xs, sh):
        del dp_in
        i = pl.program_id(0)

        @pl.when(i == 0)
        def _():
            dw_ref[...] = jnp.zeros_like(dw_ref)

        dy = dy_ref[...]
        xs[0:tr, :] = dy
        xs[tr:tr + HALO, :] = jnp.where(i < nt - 1, nx_ref[...], 0.0)
        _shifted_copies(xs, sh, tr)
        dh = jnp.zeros((tr, 1024), F32)
        for w in range(CONV_WIDTH):
            dh += w_ref[w:w + 1, :] * _shifted(xs, sh, CONV_WIDTH - 1 - w, tr)
        xs[0:HALO, :] = jnp.where(i > 0, ph_ref[...], 0.0)
        xs[HALO:HALO + tr, :] = hd_ref[...]
        _shifted_copies(xs, sh, tr)
        for w in range(CONV_WIDTH):
            dw_ref[w:w + 1, :] += jnp.sum(dy * _shifted(xs, sh, HALO - (CONV_WIDTH - 1) + w, tr), axis=0, keepdims=True)
        dh = jnp.where(_row_ids(i, tr) >= PAD_FRONT, dh, 0.0)
        sg = _sigmoid(ug_ref[...])
        da_ref[...] = (dh * sg).astype(BF16)
        dg_ref[...] = (dh * ua_ref[...] * sg * (1.0 - sg)).astype(BF16)

    row = pl.BlockSpec((tr, 1024), lambda i: (i, 0))
    return _pcall(body, name="conv_bwd_taps", grid=(nt,),
                  in_specs=[pl.BlockSpec(memory_space=pl.ANY),
                            row, pl.BlockSpec((HALO, 1024), lambda i: (jnp.minimum((i + 1) * hb, nt * hb - 1), 0)),
                            row, pl.BlockSpec((HALO, 1024), lambda i: (jnp.maximum(i * hb - 1, 0), 0)),
                            pl.BlockSpec((tr, 1024), lambda i: (i, 3)), pl.BlockSpec((tr, 1024), lambda i: (i, 4)),
                            pl.BlockSpec((32, 1024), lambda i: (0, 0))],
                  out_specs=[pl.BlockSpec((tr, 1024), lambda i: (i, 3)), row, pl.BlockSpec((32, 1024), lambda i: (0, 0))],
                  out_shape=[jax.ShapeDtypeStruct((t, 5120), BF16), jax.ShapeDtypeStruct((t, 1024), BF16),
                             jax.ShapeDtypeStruct((32, 1024), F32)],
                  scratch_shapes=[pltpu.VMEM((tr + HALO, 1024), F32), pltpu.VMEM((7, tr + SH_ROWS, 1024), F32)],
                  input_output_aliases={0: 0}, compiler_params=_params())(
                      dproj, dy, dy, hdn, hdn, proj, proj, conv_w)


NEG_BIG = -1e30


def _seg_tables(qb):
    j = np.arange(128)
    bd = (j[:, None] // 64 == j[None, :] // 64).astype(np.float32)
    ones = np.ones((128, 128), np.float32)
    later = np.concatenate([(j[:, None] >= j[None, :]).astype(np.float32), ones], axis=1)
    earlier = np.concatenate([(j[:, None] < j[None, :]).astype(np.float32), ones], axis=1)
    per = qb // CHUNK
    row = np.arange(qb)[:, None]
    pad = np.broadcast_to(j[None, :] < PAD_FRONT, (qb, 128))
    diag = [(g * CHUNK + j[None, :]) >= row for g in range(per)]
    masks = diag + [np.zeros((qb, 128), bool), pad, diag[0] | pad]
    bias = np.stack([np.where(m, NEG_BIG, 0.0) for m in masks]).astype(np.float32)
    dup = lambda m: np.concatenate([m, m], axis=0)
    return (jnp.asarray(bd, BF16), jnp.asarray(dup(later), BF16), jnp.asarray(dup(earlier), BF16),
            jnp.asarray(bias, F32))


def _split_dot(x, m):
    hi = x.astype(BF16)
    lo = (x - hi.astype(F32)).astype(BF16)
    return _dot(hi, m) + _dot(lo, m)


def _qk_norm_fwd(qkv, qg, kg, bd):
    t = qkv.shape[0]
    tr = _rows(t)
    nb = tr // CHUNK

    def body(q_ref, k_ref, v_ref, qg_ref, kg_ref, bd_ref, qo, kt, k2, vt, v2):
        bdm = bd_ref[...]
        lane = lax.broadcasted_iota(jnp.int32, (1, 128), 1)
        sub = lax.broadcasted_iota(jnp.int32, (128, 1), 0)

        def pair_layouts(x, t_ref, s_ref, hp, b):
            xt = x.T
            t_ref[hp, b] = jnp.concatenate([jnp.where(sub < 64, xt, 0.0), jnp.where(sub >= 64, xt, 0.0)],
                                           axis=1).astype(BF16)
            s_ref[hp, b] = jnp.concatenate([jnp.where(lane < 64, x, 0.0), jnp.where(lane >= 64, x, 0.0)],
                                           axis=0).astype(BF16)

        for hp in range(8):
            sl = slice(128 * hp, 128 * (hp + 1))
            x = q_ref[:, sl]
            r = lax.rsqrt(_split_dot(x * x, bdm) * (1.0 / 64) + EPS)
            qo[:, sl] = (x * r * (qg_ref[:, sl] * SB_SCALE)).astype(BF16)
            x = k_ref[:, sl]
            r = lax.rsqrt(_split_dot(x * x, bdm) * (1.0 / 64) + EPS)
            kn = x * r * kg_ref[:, sl]
            v = v_ref[:, sl]
            for b in range(nb):
                rows = slice(CHUNK * b, CHUNK * (b + 1))
                pair_layouts(kn[rows], kt, k2, hp, b)
                pair_layouts(v[rows], vt, v2, hp, b)

    col = lambda c: pl.BlockSpec((tr, 1024), lambda i: (i, c))
    vec = pl.BlockSpec((1, 1024), lambda i: (0, 0))
    wide = pl.BlockSpec((8, nb, 128, 256), lambda i: (0, i, 0, 0))
    tall = pl.BlockSpec((8, nb, 256, 128), lambda i: (0, i, 0, 0))
    wsh = jax.ShapeDtypeStruct((8, t // CHUNK, 128, 256), BF16)
    tsh = jax.ShapeDtypeStruct((8, t // CHUNK, 256, 128), BF16)
    return _pcall(body, name="qk_norm_fwd", grid=(t // tr,),
                  in_specs=[col(0), col(1), col(2), vec, vec, pl.BlockSpec((128, 128), lambda i: (0, 0))],
                  out_specs=[col(0), wide, tall, wide, tall],
                  out_shape=[jax.ShapeDtypeStruct((t, 1024), BF16), wsh, tsh, wsh, tsh])(qkv, qkv, qkv, qg, kg, bd)


def _qk_norm_bwd(qkv, dq, dk, dv, qg, kg, bd):
    t = qkv.shape[0]
    tr = _rows(t)

    def body(q_ref, k_ref, dq_ref, dk_ref, dv_ref, qg_ref, kg_ref, bd_ref, o_ref, dqg_ref, dkg_ref):
        @pl.when(pl.program_id(0) == 0)
        def _():
            dqg_ref[...] = jnp.zeros_like(dqg_ref)
            dkg_ref[...] = jnp.zeros_like(dkg_ref)

        bdm = bd_ref[...]
        for part, (src, d_ref, g_ref, dg_ref) in enumerate(((q_ref, dq_ref, qg_ref, dqg_ref),
                                                           (k_ref, dk_ref, kg_ref, dkg_ref))):
            for cix in range(8):
                sl = slice(128 * cix, 128 * (cix + 1))
                x = src[:, sl]
                d = d_ref[:, sl]
                r = lax.rsqrt(_split_dot(x * x, bdm) * (1.0 / 64) + EPS)
                u = d * g_ref[:, sl]
                m = _split_dot(u * x, bdm) * (1.0 / 64)
                o_ref[:, 1024 * part + 128 * cix:1024 * part + 128 * (cix + 1)] = (r * u - x * (r * r * r * m)).astype(BF16)
                dg_ref[:, sl] += jnp.sum(d * x * r, axis=0, keepdims=True)
        o_ref[:, 2048:3072] = dv_ref[...].astype(BF16)

    col = lambda c: pl.BlockSpec((tr, 1024), lambda i: (i, c))
    vec = pl.BlockSpec((1, 1024), lambda i: (0, 0))
    vsh = jax.ShapeDtypeStruct((1, 1024), F32)
    return _pcall(body, name="qk_norm_bwd", grid=(t // tr,),
                  in_specs=[col(0), col(1), col(0), col(0), col(0), vec, vec, pl.BlockSpec((128, 128), lambda i: (0, 0))],
                  out_specs=[pl.BlockSpec((tr, 3072), lambda i: (i, 0)), vec, vec],
                  out_shape=[jax.ShapeDtypeStruct((t, 3072), BF16), vsh, vsh])(qkv, qkv, dq, dk, dv, qg, kg, bd)


def _split2(x):
    hi = x.astype(BF16)
    lo = (x - hi.astype(F32)).astype(BF16)
    return jnp.concatenate([hi, lo], axis=1)


def _sb_scores(z, later_tab):
    e = jnp.exp(-jnp.abs(z))
    ope = 1.0 + e
    sp = jnp.maximum(z, 0.0) + jnp.log(ope)
    return e, ope, _dot(_split2(sp), later_tab)


def _sb_bias_index(i, kb, per):
    g = kb - i * per
    return jnp.where(kb == 0, jnp.where(i == 0, per + 2, per + 1), jnp.where(g >= 0, g, per))


def _sb_qb(t):
    return _tile(t, (384, 128))


def _sb_fwd(qh, kt, v2, later_tab, bias_tab):
    t = qh.shape[0]
    qb = _sb_qb(t)
    per = qb // CHUNK
    nkb_all = t // CHUNK

    nq = t // qb

    def body(q_ref, kt_ref, v2_ref, tab_ref, bias_ref, o_ref, ws_ref, acc, carry, zbuf, wbuf, wsem):
        h, i = pl.program_id(0), pl.program_id(1)
        n = h * nq + i
        p = n & 1
        q = q_ref[...]
        acc[...] = jnp.zeros_like(acc)
        carry[...] = jnp.zeros_like(carry)
        nkb = (i + 1) * per
        save = lambda kb: pltpu.make_async_copy(wbuf.at[p, kb], ws_ref.at[h, i, kb], wsem.at[p, kb])

        def drain(step, par):
            hs, is_ = step // nq, step % nq

            def one(kb, _):
                pltpu.make_async_copy(wbuf.at[par, kb], ws_ref.at[hs, is_, kb], wsem.at[par, kb]).wait()
                return 0

            lax.fori_loop(0, (is_ + 1) * per, one, 0)

        @pl.when(n >= 2)
        def _():
            drain(n - 2, p)

        for u in range(per):
            zbuf[u] = _dot(q, kt_ref[nkb - 1 - u])

        def step(s, _):
            top = nkb - 1 - per * s

            @pl.when(s > 0)
            def _():
                for u in range(per):
                    save(top + per - u).start()

            z2s = [zbuf[u] for u in range(per)]
            for u in range(per):
                zbuf[u] = _dot(q, kt_ref[jnp.maximum(top - per - u, 0)])
            cins = [carry[0], carry[1]]
            zs, cus = [], []
            for u in range(per):
                bias = bias_ref[_sb_bias_index(i, top - u, per)]
                zs.append([z2s[u][:, 128 * hh:128 * (hh + 1)] + bias for hh in range(2)])
                cus.append([_sb_scores(z, tab_ref[...])[2] for z in zs[u]])
            part = None
            for u in range(per):
                kb = top - u
                for hh in range(2):
                    cu = cus[u][hh]
                    wbuf[p, kb, :, 128 * hh:128 * (hh + 1)] = jnp.exp(zs[u][hh] - cu[:, :128] - cins[hh]).astype(BF16)
                    cins[hh] = cins[hh] + cu[:, 128:]
                d = _dot(wbuf[p, kb], v2_ref[kb])
                part = d if part is None else part + d
            carry[0], carry[1] = cins[0], cins[1]
            acc[...] += part
            return 0

        lax.fori_loop(0, nkb // per, step, 0)
        for u in range(per):
            save(per - 1 - u).start()
        o_ref[...] = acc[...]

        @pl.when(n == 8 * nq - 1)
        def _():
            drain(n - 1, 1 - p)
            drain(n, p)

    blk = pl.BlockSpec((qb, 128), lambda h, i: (i, h))
    wide = pl.BlockSpec((None, nkb_all, 128, 256), lambda h, i: (h, 0, 0, 0))
    tall = pl.BlockSpec((None, nkb_all, 256, 128), lambda h, i: (h, 0, 0, 0))
    return _pcall(body, name="sb_fwd", grid=(8, t // qb),
                  in_specs=[blk, wide, tall, pl.BlockSpec((256, 256), lambda h, i: (0, 0)),
                            pl.BlockSpec((per + 3, qb, 128), lambda h, i: (0, 0, 0))],
                  out_specs=[blk, pl.BlockSpec(memory_space=pl.ANY)],
                  out_shape=[jax.ShapeDtypeStruct((t, 1024), F32),
                             jax.ShapeDtypeStruct((8, t // qb, nkb_all, qb, 256), BF16)],
                  scratch_shapes=[pltpu.VMEM((qb, 128), F32), pltpu.VMEM((2, qb, 128), F32),
                                  pltpu.VMEM((per, qb, 256), F32), pltpu.VMEM((2, nkb_all, qb, 256), BF16),
                                  pltpu.SemaphoreType.DMA((2, nkb_all))],
                  compiler_params=_params(dimension_semantics=("arbitrary", "arbitrary")))(
                      qh, kt, v2, later_tab, bias_tab)


def _sb_bwd(qh, kt, k2, vt, wsave, do, earlier_tab, bias_tab):
    t = qh.shape[0]
    qb = _sb_qb(t)
    per = qb // CHUNK
    nkb_all = t // CHUNK

    zero_slot = nkb_all
    nq = t // qb

    def body(q_ref, kt_ref, k2_ref, vt_ref, ws_ref, do_ref, etab_ref, bias_ref,
             dq_ref, dk_ref, dv_ref, acc, gcarry, zbuf, dwbuf, wbuf, wsem, dzbuf):
        h, i = pl.program_id(0), pl.program_id(1)
        n = h * nq + i
        p = n & 1

        @pl.when(i == 0)
        def _():
            dk_ref[...] = jnp.zeros_like(dk_ref)
            dv_ref[...] = jnp.zeros_like(dv_ref)

        nkb = (i + 1) * per
        fetch = lambda kb: pltpu.make_async_copy(ws_ref.at[h, i, kb], wbuf.at[p, kb], wsem.at[p, kb])

        def prefetch(step, par):
            hs, is_ = step // nq, step % nq

            def one(kb, _):
                pltpu.make_async_copy(ws_ref.at[hs, is_, kb], wbuf.at[par, kb], wsem.at[par, kb]).start()
                return 0

            lax.fori_loop(0, (is_ + 1) * per, one, 0)

        @pl.when(n == 0)
        def _():
            prefetch(n, p)

        @pl.when(n + 1 < 8 * nq)
        def _():
            prefetch(n + 1, 1 - p)

        q = q_ref[...]
        dob = do_ref[...].astype(BF16)
        lane = lax.broadcasted_iota(jnp.int32, (1, 128), 1)
        acc[...] = jnp.zeros_like(acc)
        gcarry[...] = jnp.zeros_like(gcarry)
        zbuf[...] = _dot(q, kt_ref[0])
        dwbuf[...] = _dot(dob, vt_ref[0])
        dzbuf[...] = jnp.zeros_like(dzbuf)
        wbuf[p, zero_slot] = jnp.zeros((qb, 256), BF16)

        def gradients(slot, kb):
            dz2 = dzbuf[...]
            acc[...] += _dot(dz2, k2_ref[kb])
            dk2 = _dot(dz2, q, "tn")
            dv2 = _dot(wbuf[p, slot], dob, "tn")
            dk_ref[kb] += jnp.where(lane < 64, dk2[:128], dk2[128:])
            dv_ref[kb] += jnp.where(lane < 64, dv2[:128], dv2[128:])

        def step(kb, _):
            fetch(kb).wait()
            bias = bias_ref[_sb_bias_index(i, kb, per)]
            z2 = zbuf[...]
            dw2 = dwbuf[...]
            nxt = jnp.minimum(kb + 1, nkb - 1)
            zbuf[...] = _dot(q, kt_ref[nxt])
            dwbuf[...] = _dot(dob, vt_ref[nxt])
            gradients(jnp.where(kb == 0, zero_slot, kb - 1), jnp.maximum(kb - 1, 0))
            w2 = wbuf[p, kb]
            for hh in range(2):
                sl = slice(128 * hh, 128 * (hh + 1))
                z = z2[:, sl] + bias
                e = jnp.exp(-jnp.abs(z))
                r = 1.0 / (1.0 + e)
                sig = jnp.where(z >= 0, r, e * r)
                gw = w2[:, sl].astype(F32) * dw2[:, sl]
                cu2 = _dot(_split2(gw), etab_ref[...])
                gin = gcarry[hh]
                gcarry[hh] = gin + cu2[:, 128:]
                dzbuf[:, sl] = (gw - sig * (gw + cu2[:, :128] + gin)).astype(BF16)
            return 0

        lax.fori_loop(0, nkb, step, 0)
        gradients(nkb - 1, nkb - 1)
        dq_ref[...] = acc[...] * SB_SCALE

    blk = pl.BlockSpec((qb, 128), lambda h, i: (i, h))
    wide = pl.BlockSpec((None, nkb_all, 128, 256), lambda h, i: (h, 0, 0, 0))
    tall = pl.BlockSpec((None, nkb_all, 256, 128), lambda h, i: (h, 0, 0, 0))
    tab = pl.BlockSpec((256, 256), lambda h, i: (0, 0))
    kv_out = pl.BlockSpec((nkb_all, 128, 128), lambda h, i: (0, 0, h))
    ksh = jax.ShapeDtypeStruct((nkb_all, 128, 1024), F32)
    dq, dk, dv = _pcall(
        body, name="sb_bwd", grid=(8, t // qb),
        in_specs=[blk, wide, tall, wide, pl.BlockSpec(memory_space=pl.ANY), blk, tab,
                  pl.BlockSpec((per + 3, qb, 128), lambda h, i: (0, 0, 0))],
        out_specs=[blk, kv_out, kv_out], out_shape=[jax.ShapeDtypeStruct((t, 1024), F32), ksh, ksh],
        scratch_shapes=[pltpu.VMEM((qb, 128), F32), pltpu.VMEM((2, qb, 128), F32),
                        pltpu.VMEM((qb, 256), F32), pltpu.VMEM((qb, 256), F32),
                        pltpu.VMEM((2, nkb_all + 1, qb, 256), BF16), pltpu.SemaphoreType.DMA((2, nkb_all)),
                        pltpu.VMEM((qb, 256), BF16)],
        compiler_params=_params(dimension_semantics=("arbitrary", "arbitrary")))(
            qh, kt, k2, vt, wsave, do, earlier_tab, bias_tab)
    return dq, dk.reshape(t, 1024), dv.reshape(t, 1024)


def _adamw_math(w, g, m, v):
    m = ADAM_B1 * m + (1.0 - ADAM_B1) * g
    v = ADAM_B2 * v + (1.0 - ADAM_B2) * (g * g)
    m_hat = m / (1.0 - ADAM_B1 ** ADAM_STEP)
    v_hat = v / (1.0 - ADAM_B2 ** ADAM_STEP)
    delta = -ADAM_LR * (m_hat / (jnp.sqrt(v_hat) + ADAM_EPS) + ADAM_WD * w)
    return delta, m, v


def _adamw(name, w, owns, recvs, m, v, me):
    shape = w.shape
    c = shape[-1]
    nl = len(owns)
    w3, m3, v3 = (a.reshape(nl, -1, c) for a in (w, m, v))
    r = w3.shape[1]
    tr = _tile(r, (256, 128))
    owns = [o.reshape(N_DEV, r, c) for o in owns]
    recvs = [p.reshape(N_DEV - 1, r, c) for p in recvs]

    def body(me_ref, w_ref, *rest):
        own_refs, recv_refs = rest[:nl], rest[nl:2 * nl]
        m_ref, v_ref = rest[2 * nl:2 * nl + 2]
        g_out, d_out, m_out, v_out = rest[2 * nl + 2:]
        layer = pl.program_id(0)

        def grad(k):
            g = own_refs[k][...].astype(F32)
            for s in range(N_DEV - 1):
                g = g + recv_refs[k][s].astype(F32)
            return g

        g = grad(0)
        for k in range(1, nl):
            g = jnp.where(layer == k, grad(k), g)
        d, mn, vn = _adamw_math(w_ref[...], g, m_ref[...], v_ref[...])
        g_out[...] = g
        d_out[...] = d
        m_out[...] = mn
        v_out[...] = vn

    row = pl.BlockSpec((None, tr, c), lambda l, i, me_ref: (l, i, 0))
    own = lambda k: pl.BlockSpec((None, tr, c), lambda l, i, me_ref: (me_ref[0], jnp.where(l == k, i, 0), 0))
    rcv = lambda k: pl.BlockSpec((N_DEV - 1, tr, c), lambda l, i, me_ref: (0, jnp.where(l == k, i, 0), 0))
    osh = jax.ShapeDtypeStruct((nl, r, c), F32)
    grid_spec = pltpu.PrefetchScalarGridSpec(
        num_scalar_prefetch=1, grid=(nl, r // tr),
        in_specs=[row] + [own(k) for k in range(nl)] + [rcv(k) for k in range(nl)] + [row, row],
        out_specs=[row, row, row, row])
    outs = _pcall(body, name=name, grid_spec=grid_spec, out_shape=[osh, osh, osh, osh])(
        me.reshape(1), w3, *owns, *recvs, m3, v3)
    return tuple(o.reshape(shape) for o in outs)


def _place():
    x, y, c = lax.axis_index("x"), lax.axis_index("y"), lax.axis_index("c")
    return x, y, c, 4 * x + 2 * y + c


def _peer(x, y, c, rel):
    return (x ^ ((rel >> 2) & 1), y ^ ((rel >> 1) & 1), c ^ (rel & 1))


def _gather_first(now, later):
    n, k = len(now), len(later)

    def body(*refs):
        ins, outs = refs[:n + k], refs[n + k:2 * (n + k)]
        send, recv, lsem = refs[2 * (n + k):]
        x, y, c, me = _place()
        locals_ = []
        for w in range(n + k):
            local = pltpu.make_async_copy(ins[w], outs[w].at[me], lsem.at[w])
            local.start()
            locals_.append(local)
        def copy(w, src, slot, rel, to_rel):
            return pltpu.make_async_remote_copy(src_ref=src, dst_ref=outs[w].at[slot], send_sem=send.at[w, rel - 1],
                                                recv_sem=recv.at[w, rel - 1], device_id=_peer(x, y, c, to_rel),
                                                device_id_type=MESH)

        for w in range(n):
            for rel in (1, 2, 4, 6):
                copy(w, ins[w], me, rel, rel).start()
        for w in range(n):
            for rel in (2, 4, 6):
                copy(w, ins[w], me ^ rel, rel, rel).wait_recv()
                copy(w, outs[w].at[me ^ rel], me ^ rel, rel | 1, 1).start()
        for w in range(n):
            for rel in (1, 3, 5, 7):
                copy(w, ins[w], me ^ rel, rel, 1).wait_recv()
            for rel in range(1, N_DEV):
                copy(w, ins[w], me, rel, rel).wait_send()
        for local in locals_:
            local.wait()

    hbm = pl.BlockSpec(memory_space=pl.ANY)
    vmem = pl.BlockSpec(memory_space=pltpu.VMEM)
    arrays = list(now) + list(later)
    return _pcall(body, name="gather_first", in_specs=[vmem] * (n + k), out_specs=[hbm] * (n + k),
                  out_shape=[jax.ShapeDtypeStruct((N_DEV,) + a.shape, a.dtype) for a in arrays],
                  scratch_shapes=[pltpu.SemaphoreType.DMA((n, N_DEV - 1)), pltpu.SemaphoreType.DMA((n, N_DEV - 1)),
                                  pltpu.SemaphoreType.DMA((n + k,))],
                  compiler_params=_params(has_side_effects=True))(*arrays)


_HBM = pl.BlockSpec(memory_space=pltpu.HBM)
_SEM = pl.BlockSpec(memory_space=pltpu.SEMAPHORE)
_DATAFLOW = pltpu.SideEffectType.DATAFLOW_SIDE_EFFECTING


def _exchange_refs(srcs, lands, mode, me, rel, j):
    if mode == "gather":
        return srcs[j], lands[j].at[me], lands[j].at[me ^ rel]
    return srcs[j].at[me ^ rel], lands[j].at[rel - 1], lands[j].at[rel - 1]


def _exchange_start(name, srcs, lands, mode):
    n = len(srcs)

    def body(*refs):
        ins, lnd = refs[:n], refs[n:2 * n]
        send, recv = refs[2 * n], refs[2 * n + 1]
        token = refs[-1]
        x, y, c, me = _place()
        for j in range(n):
            for rel in range(1, N_DEV):
                src, dst, _ = _exchange_refs(ins, lnd, mode, me, rel, j)
                pltpu.make_async_remote_copy(src_ref=src, dst_ref=dst, send_sem=send.at[j * (N_DEV - 1) + rel - 1],
                                             recv_sem=recv.at[j * (N_DEV - 1) + rel - 1],
                                             device_id=_peer(x, y, c, rel), device_id_type=MESH).start()
        token[...] = jnp.zeros_like(token)

    sems = pltpu.SemaphoreType.DMA((n * (N_DEV - 1),))
    hbm_like = lambda a: pltpu.HBM(a.shape, a.dtype)
    outs = _pcall(body, name=name + "_start",
                  in_specs=[_HBM] * (2 * n), out_specs=[_SEM, _SEM] + [_HBM] * (2 * n) + [pl.BlockSpec(memory_space=pltpu.VMEM)],
                  out_shape=[sems, sems] + [hbm_like(a) for a in srcs] + [hbm_like(a) for a in lands]
                  + [jax.ShapeDtypeStruct((8, 128), F32)],
                  input_output_aliases={i: 2 + i for i in range(2 * n)},
                  compiler_params=pltpu.CompilerParams(has_side_effects=_DATAFLOW))(
                      *[pltpu.with_memory_space_constraint(a, pltpu.HBM) for a in list(srcs) + list(lands)])
    return dict(name=name, mode=mode, n=n, send=outs[0], recv=outs[1], srcs=outs[2:2 + n], lands=outs[2 + n:2 + 2 * n],
                token=outs[-1][0, 0])


def _exchange_wait(ex, after):
    n, mode = ex["n"], ex["mode"]

    def body(*refs):
        ins, lnd = refs[:n], refs[n:2 * n]
        send, recv = refs[2 * n], refs[2 * n + 1]
        x, y, c, me = _place()
        for j in range(n):
            for rel in range(1, N_DEV):
                src, dst, landed = _exchange_refs(ins, lnd, mode, me, rel, j)
                pltpu.make_async_remote_copy(src_ref=src, dst_ref=dst, send_sem=send.at[j * (N_DEV - 1) + rel - 1],
                                             recv_sem=recv.at[j * (N_DEV - 1) + rel - 1],
                                             device_id=_peer(x, y, c, rel), device_id_type=MESH).wait_send()
                pltpu.make_async_remote_copy(src_ref=src, dst_ref=landed, send_sem=send.at[j * (N_DEV - 1) + rel - 1],
                                             recv_sem=recv.at[j * (N_DEV - 1) + rel - 1],
                                             device_id=_peer(x, y, c, rel), device_id_type=MESH).wait_recv()

    hbm_like = lambda a: pltpu.HBM(a.shape, a.dtype)
    arrays = list(ex["srcs"]) + list(ex["lands"])
    outs = _pcall(body, name=ex["name"] + "_wait",
                  in_specs=[_HBM] * (2 * n) + [_SEM, _SEM, pl.BlockSpec(memory_space=pl.ANY)],
                  out_specs=[_HBM] * (2 * n), out_shape=[hbm_like(a) for a in arrays],
                  input_output_aliases={i: i for i in range(2 * n)},
                  compiler_params=pltpu.CompilerParams(has_side_effects=_DATAFLOW))(
                      *arrays, ex["send"], ex["recv"], after)
    return outs[:n], outs[n:]


def _scatter_start(name, grads):
    lands = [lax.empty((N_DEV - 1,) + g.shape[1:], g.dtype) for g in grads]
    return _exchange_start(name, grads, lands, "scatter")


ROW_MIX, ROW_MLP, ROW_CB, ROW_LG, ROW_LB, ROW_QN, ROW_KN, ROW_LOSS = 0, 2, 4, 5, 6, 7, 8, 9
ROW_META, ROW_CW, ROW_GN, SMALL_ROWS = 16, 32, 64, 72


def _sum_small(slots):
    def body(s_ref, o_ref):
        tot = s_ref[0]
        for s in range(1, N_DEV):
            tot = tot + s_ref[s]
        o_ref[...] = tot
        for row in (ROW_QN, ROW_KN):
            v = tot[row:row + 1, :]
            f = v[:, 0:128]
            for k in range(1, 8):
                f = f + v[:, 128 * k:128 * (k + 1)]
            o_ref[row:row + 1, 0:64] = f[:, 0:64] + f[:, 64:128]

    return _pcall(body, name="sum_small", out_shape=jax.ShapeDtypeStruct(slots.shape[1:], F32))(slots)


def _adamw_small(w, g, m, v):
    def body(w_ref, g_ref, m_ref, v_ref, d_out, m_out, v_out):
        d, mn, vn = _adamw_math(w_ref[...], g_ref[...], m_ref[...], v_ref[...])
        d_out[...] = d
        m_out[...] = mn
        v_out[...] = vn

    osh = jax.ShapeDtypeStruct(w.shape, F32)
    return _pcall(body, name="adamw_small", out_shape=[osh, osh, osh])(w, g, m, v)


def _local_step(h0, target, p, weight, emit):
    t = h0.shape[0]
    tables = _ret_tables(t)
    bd, later_tab, earlier_tab, bias_tab = _seg_tables(_sb_qb(t))
    row = lambda a, i: a[i:i + 1]

    hn_a = _rms_fwd("rms_mix0", h0, row(p["norm_mix_g"], 0))
    w_in = weight("w_in", hn_a)
    proj = _mm_cols("proj_in", hn_a, w_in, ())
    gn_flat = p["gn_g"].reshape(1, 1024)
    o_ret, states, cat = _ret_fwd(proj, gn_flat, tables)
    cat, hdn, ycv = _conv_fwd(cat, proj, p["conv_w"], p["conv_b"], p["ln_g"], p["ln_b"])
    w_out = weight("w_out", cat)
    h1, hn_b = _mm_rows_norm("mix_out", cat, w_out, h0, row(p["norm_mlp_g"], 0))
    w1_0, w2_0 = weight("w1_0", hn_b), weight("w2_0", hn_b)
    a0, s0 = _mm_cols("mlp0_up", hn_b, w1_0, (), epi="relu2")
    h2, hn_c = _mm_rows_norm("mlp0_down", s0, w2_0, h1, row(p["norm_mix_g"], 1))

    w_qkv = weight("w_qkv", hn_c)
    qkv = _mm_cols("qkv", hn_c, w_qkv, ())
    qg = jnp.tile(p["qn_g"], (1, 16))
    kg = jnp.tile(p["kn_g"], (1, 16))
    qh, kt, k2, vt, v2 = _qk_norm_fwd(qkv, qg, kg, bd)
    o_sb, w_sb = _sb_fwd(qh, kt, v2, later_tab, bias_tab)
    w_o = weight("w_o", o_sb)
    h3, hn_d = _mm_rows_norm("attn_out", o_sb, w_o, h2, row(p["norm_mlp_g"], 1))
    w1_1, w2_1 = weight("w1_1", hn_d), weight("w2_1", hn_d)
    a1, s1 = _mm_cols("mlp1_up", hn_d, w1_1, (), epi="relu2")
    dh, loss = _mm_rows_loss("mlp1_down", s1, w2_1, h3, target)

    def mlp_bwd(tag, layer, w1, w2, dh, h_in, hn, a, s):
        da = _mm_rows_t(f"{tag}_dact", dh, w2, (), out_dtype=BF16, epi="drelu2", extra=a)
        dw2 = _wgrad_rows(f"{tag}_dw2", s, dh, 512)
        dw1 = _wgrad_cols(f"{tag}_dw1", hn, da, 512)
        tok = emit(tag, [dw1, dw2])
        return _mm_cols_t_rms(f"{tag}_dhn", da, w1, h_in, row(p["norm_mlp_g"], layer) + tok, dh)

    dh, dg_mlp1 = mlp_bwd("mlp1", 1, w1_1, w2_1, dh, h3, hn_d, a1, s1)

    do_sb = _mm_rows_t("attn_dout", dh, w_o, ())
    dw_o = _wgrad_rows("attn_dwo", o_sb, dh, 128)
    dq, dk, dv = _sb_bwd(qh, kt, k2, vt, w_sb, do_sb, earlier_tab, bias_tab)
    dqkv, dqg, dkg = _qk_norm_bwd(qkv, dq, dk, dv, qg, kg, bd)
    dw_qkv = _wgrad_cols("qkv_dw", hn_c, dqkv, 384)
    tok = emit("attn", [dw_qkv, dw_o])
    dh, dg_mix1 = _mm_cols_t_rms("qkv_dhn", dqkv, w_qkv, h2, row(p["norm_mix_g"], 1) + tok, dh)

    dh, dg_mlp0 = mlp_bwd("mlp0", 0, w1_0, w2_0, dh, h1, hn_b, a0, s0)

    dw_out = _wgrad_rows("mix_dwout", cat, dh, 256)
    tok = emit("mix0_out", [dw_out])
    do_ret, dproj, dgn, dy, dlg, dlb, dcb = _mix_bwd_head(dh, w_out, o_ret, proj, gn_flat + tok, ycv,
                                                          p["ln_g"], p["ln_b"])
    dproj = _ret_bwd(dproj, proj, states, do_ret, tables)
    dproj, dug, dcw = _conv_bwd_taps(dproj, dy, hdn, proj, p["conv_w"])
    dproj = lax.dynamic_update_slice(dproj, dug, (0, 4096))
    dw_in = _wgrad_cols("proj_dw", hn_a, dproj, 640)
    tok = emit("mix0", [dw_in])
    dh, dg_mix0 = _mm_cols_t_rms("proj_dhn", dproj, w_in, h0, row(p["norm_mix_g"], 0) + tok, dh)

    rid = lax.broadcasted_iota(jnp.int32, (16, 1), 0)
    loss_row = jnp.broadcast_to(loss[0:1, 0:1], (1, D_MODEL))
    vecs = sum(jnp.where(rid == k, v, 0.0)
               for k, v in enumerate((dg_mix0, dg_mix1, dg_mlp0, dg_mlp1, dcb, dlg, dlb, dqg, dkg, loss_row)))
    small = jnp.concatenate([vecs, dh[PAD_FRONT:TOK0], dcw, jnp.where(rid[:8] == 0, dgn, 0.0)], axis=0)
    return dh[TOK0:], small


_SMALL_NAMES = ("meta", "norm_mix_g", "norm_mlp_g", "even_ret_gn_g", "even_conv_w", "even_conv_b",
                "even_conv_ln_g", "even_conv_ln_b", "odd_q_norm_g", "odd_k_norm_g")
_BIG_NAMES = ("even_w_in", "even_w_out", "odd_w_qkv", "odd_w_o", "mlp_w1", "mlp_w2")
_ORDER = ("meta", "norm_mix_g", "norm_mlp_g", "even_w_in", "even_ret_gn_g", "even_conv_w", "even_conv_b",
          "even_conv_ln_g", "even_conv_ln_b", "even_w_out", "odd_w_qkv", "odd_q_norm_g", "odd_k_norm_g",
          "odd_w_o", "mlp_w1", "mlp_w2")


def _pack128(a):
    flat = a.reshape(-1)
    n = flat.shape[0]
    rows = -(-n // 128)
    rows8 = -(-rows // 8) * 8
    return jnp.pad(flat, (0, rows8 * 128 - n)).reshape(rows8, 128)


def kernel(x, meta, norm_mix_g, norm_mlp_g, even_w_in, even_ret_gn_g, even_conv_w, even_conv_b, even_conv_ln_g, even_conv_ln_b, even_w_out, odd_w_qkv, odd_q_norm_g, odd_k_norm_g, odd_w_o, mlp_w1, mlp_w2, loss_target, m_meta, m_norm_mix_g, m_norm_mlp_g, m_even_w_in, m_even_ret_gn_g, m_even_conv_w, m_even_conv_b, m_even_conv_ln_g, m_even_conv_ln_b, m_even_w_out, m_odd_w_qkv, m_odd_q_norm_g, m_odd_k_norm_g, m_odd_w_o, m_mlp_w1, m_mlp_w2, v_meta, v_norm_mix_g, v_norm_mlp_g, v_even_w_in, v_even_ret_gn_g, v_even_conv_w, v_even_conv_b, v_even_conv_ln_g, v_even_conv_ln_b, v_even_w_out, v_odd_w_qkv, v_odd_q_norm_g, v_odd_k_norm_g, v_odd_w_o, v_mlp_w1, v_mlp_w2):
    w = dict(meta=meta, norm_mix_g=norm_mix_g, norm_mlp_g=norm_mlp_g, even_w_in=even_w_in,
             even_ret_gn_g=even_ret_gn_g, even_conv_w=even_conv_w, even_conv_b=even_conv_b,
             even_conv_ln_g=even_conv_ln_g, even_conv_ln_b=even_conv_ln_b, even_w_out=even_w_out,
             odd_w_qkv=odd_w_qkv, odd_q_norm_g=odd_q_norm_g, odd_k_norm_g=odd_k_norm_g, odd_w_o=odd_w_o,
             mlp_w1=mlp_w1, mlp_w2=mlp_w2)
    mom = dict(meta=m_meta, norm_mix_g=m_norm_mix_g, norm_mlp_g=m_norm_mlp_g, even_w_in=m_even_w_in,
               even_ret_gn_g=m_even_ret_gn_g, even_conv_w=m_even_conv_w, even_conv_b=m_even_conv_b,
               even_conv_ln_g=m_even_conv_ln_g, even_conv_ln_b=m_even_conv_ln_b, even_w_out=m_even_w_out,
               odd_w_qkv=m_odd_w_qkv, odd_q_norm_g=m_odd_q_norm_g, odd_k_norm_g=m_odd_k_norm_g, odd_w_o=m_odd_w_o,
               mlp_w1=m_mlp_w1, mlp_w2=m_mlp_w2)
    var = dict(meta=v_meta, norm_mix_g=v_norm_mix_g, norm_mlp_g=v_norm_mlp_g, even_w_in=v_even_w_in,
               even_ret_gn_g=v_even_ret_gn_g, even_conv_w=v_even_conv_w, even_conv_b=v_even_conv_b,
               even_conv_ln_g=v_even_conv_ln_g, even_conv_ln_b=v_even_conv_ln_b, even_w_out=v_even_w_out,
               odd_w_qkv=v_odd_w_qkv, odd_q_norm_g=v_odd_q_norm_g, odd_k_norm_g=v_odd_k_norm_g, odd_w_o=v_odd_w_o,
               mlp_w1=v_mlp_w1, mlp_w2=v_mlp_w2)
    me = 4 * lax.axis_index("x") + 2 * lax.axis_index("y") + lax.axis_index("c")

    small_in = jnp.concatenate([meta, jnp.pad(even_conv_w[0], ((0, 1), (0, 0))),
                                jnp.pad(even_ret_gn_g[0], ((0, 4), (0, 96)))], axis=0)
    b16 = lambda a: a.astype(BF16)
    later_src = dict(w_out=b16(even_w_out[0]), w1_0=b16(mlp_w1[0]), w2_0=b16(mlp_w2[0]),
                     w_qkv=b16(odd_w_qkv[0]), w_o=b16(odd_w_o[0]), w1_1=b16(mlp_w1[1]), w2_1=b16(mlp_w2[1]))
    landed = _gather_first([b16(even_w_in[0]), small_in], list(later_src.values()))
    g_in, g_small = landed[0], landed[1]
    own_slot = dict(zip(later_src, landed[2:]))
    groups = (("gather_l0", ("w_out", "w1_0", "w2_0")), ("gather_attn", ("w_qkv", "w_o")),
              ("gather_l1", ("w1_1", "w2_1")))
    pending = {}
    gather_tok = jnp.zeros((), F32)
    for gname, names in groups:
        ex = _exchange_start(gname, [later_src[n] for n in names], [own_slot[n] for n in names], "gather")
        gather_tok = gather_tok + ex["token"]
        for n in names:
            pending[n] = (ex, names)
    arrived = dict(w_in=g_in)

    def weight(name, after):
        if name not in arrived:
            ex, names = pending[name]
            arrived.update(zip(names, _exchange_wait(ex, after)[1]))
        return arrived[name]

    cols = lambda a: jnp.transpose(a, (1, 0, 2)).reshape(a.shape[1], -1)
    p = dict(norm_mix_g=norm_mix_g + gather_tok, norm_mlp_g=norm_mlp_g, conv_b=even_conv_b, ln_g=even_conv_ln_g,
             ln_b=even_conv_ln_b, qn_g=odd_q_norm_g, kn_g=odd_k_norm_g,
             gn_g=cols(g_small[:, 48:52, :32]),
             conv_w=jnp.pad(cols(g_small[:, 16:47]), ((0, 1), (0, 0))))
    meta_full = cols(g_small[:, 0:16])

    scatters = {}

    def emit(tag, grads):
        scatters[tag] = _scatter_start("scatter_" + tag, grads)
        return scatters[tag]["token"]

    h0 = jnp.concatenate([jnp.zeros((PAD_FRONT, D_MODEL), F32), meta_full, x[0]], axis=0)
    target = jnp.concatenate([jnp.zeros((TOK0, D_MODEL), F32), loss_target[0]], axis=0)
    grad_x, small_part = _local_step(h0, target, p, weight, emit)

    out = {}
    got = {}

    def update(names, terms, after):
        for tag in {t for name in names for t, _ in terms[name]} - set(got):
            got[tag] = _exchange_wait(scatters[tag], after)
        for name in names:
            owns, recvs = zip(*[(got[t][0][j], got[t][1][j]) for t, j in terms[name]])
            out[name] = _adamw("adamw_" + name, w[name], list(owns), list(recvs), mom[name], var[name], me)

    terms = dict(even_w_in=[("mix0", 0)], even_w_out=[("mix0_out", 0)], odd_w_qkv=[("attn", 0)], odd_w_o=[("attn", 1)],
                 mlp_w1=[("mlp0", 0), ("mlp1", 0)], mlp_w2=[("mlp0", 1), ("mlp1", 1)])
    small_ex = _exchange_start("small", [small_part], [lax.empty((N_DEV,) + small_part.shape, F32)], "gather")
    update(("mlp_w1", "mlp_w2", "odd_w_qkv", "odd_w_o", "even_w_out"), terms, grad_x)
    update(("even_w_in",), terms, out["even_w_out"][1])
    (own_part,), (slots,) = _exchange_wait(small_ex, out["even_w_in"][1])
    tot = _sum_small(lax.dynamic_update_slice(slots, own_part[None], (me, 0, 0)))
    loss = tot[ROW_LOSS, 0]

    shard_cols = lambda a, width: lax.dynamic_slice_in_dim(a, me * width, width, axis=1)
    one = lambda r: tot[r:r + 1]
    small_g = dict(
        norm_mix_g=tot[ROW_MIX:ROW_MIX + 2], norm_mlp_g=tot[ROW_MLP:ROW_MLP + 2],
        even_conv_b=one(ROW_CB), even_conv_ln_g=one(ROW_LG), even_conv_ln_b=one(ROW_LB),
        odd_q_norm_g=one(ROW_QN)[:, :64], odd_k_norm_g=one(ROW_KN)[:, :64],
        meta=shard_cols(tot[ROW_META:ROW_META + N_META], 128),
        even_conv_w=shard_cols(tot[ROW_CW:ROW_CW + CONV_WIDTH], 128)[None],
        even_ret_gn_g=shard_cols(tot[ROW_GN].reshape(4, 256), 32)[None])
    packs = {n: (_pack128(w[n]), _pack128(small_g[n]), _pack128(mom[n]), _pack128(var[n])) for n in _SMALL_NAMES}
    cat4 = [jnp.concatenate([packs[n][i] for n in _SMALL_NAMES], axis=0) for i in range(4)]
    d_s, m_s, v_s = _adamw_small(*cat4)
    r0 = 0
    for n in _SMALL_NAMES:
        rows = packs[n][0].shape[0]
        size = w[n].size
        take = lambda a: a[r0:r0 + rows].reshape(-1)[:size].reshape(w[n].shape)
        out[n] = (small_g[n].reshape(w[n].shape), take(d_s), take(m_s), take(v_s))
        r0 += rows

    res = [loss, grad_x[None]]
    for i in range(4):
        res.extend(out[n][i] for n in _ORDER)
    return tuple(res)
```

```python
import functools

import numpy as np
import jax
import jax.numpy as jnp
from jax import lax
from jax.experimental import pallas as pl
from jax.experimental.pallas import tpu as pltpu

F32 = jnp.float32
BF16 = jnp.bfloat16

D_MODEL = 1024
N_META = 16
CHUNK = 128
PAD_FRONT = 112
TOK0 = PAD_FRONT + N_META
EPS = 1e-6
N_DEV = 8
RET_HEADS = 4
RET_DECAY_OFFSET = 5.0
ROPE_BASE = 10000.0
CONV_WIDTH = 31
HALO = 32
SB_SCALE = 64 ** -0.5
RET_SCALE = 128 ** -0.5
ADAM_LR, ADAM_B1, ADAM_B2, ADAM_EPS, ADAM_WD, ADAM_STEP = 0.001, 0.9, 0.999, 1e-08, 0.01, 10
VMEM_LIMIT = 56 * 1024 * 1024
MESH = pl.DeviceIdType.MESH


def _pcall(body, **kw):
    return pl.pallas_call(body, **kw)


def _params(**kw):
    return pltpu.CompilerParams(vmem_limit_bytes=VMEM_LIMIT, **kw)


def _tile(n, cands):
    for c in cands:
        if n % c == 0:
            return c
    raise ValueError(f"no tile for {n} in {cands}")


def _sigmoid(x):
    return 1.0 / (1.0 + jnp.exp(-x))


_DIMS = {
    "nn": (((1,), (0,)), ((), ())),
    "nt": (((1,), (1,)), ((), ())),
    "tn": (((0,), (0,)), ((), ())),
}


def _matmul(name, a, b, *, grid, a_spec, b_spec, o_spec, out_shape, contract, acc_shape,
            epi="plain", extra=None, extra_spec=None):
    nk = grid[2]
    dims = _DIMS[contract]
    n_in = 3 if extra is not None else 2
    n_out = 2 if epi == "relu2" else 1

    def body(*refs):
        a_ref, b_ref = refs[0], refs[1]
        e_ref = refs[2] if extra is not None else None
        outs = refs[n_in:n_in + n_out]
        acc = refs[-1]
        k = pl.program_id(2)
        part = lax.dot_general(a_ref[...].astype(BF16), b_ref[...].astype(BF16), dims, preferred_element_type=F32)
        if nk > 1:
            @pl.when(k == 0)
            def _():
                acc[...] = jnp.zeros_like(acc)

            acc[...] += part

        @pl.when(k == nk - 1)
        def _():
            r = acc[...] if nk > 1 else part
            if epi == "plain":
                outs[0][...] = r.astype(outs[0].dtype)
            elif epi == "residual":
                outs[0][...] = (r + e_ref[...]).astype(outs[0].dtype)
            elif epi == "relu2":
                outs[0][...] = r
                rr = jnp.maximum(r, 0.0)
                outs[1][...] = (rr * rr).astype(BF16)
            elif epi == "drelu2":
                outs[0][...] = (r * (2.0 * jnp.maximum(e_ref[...], 0.0))).astype(outs[0].dtype)

    in_specs = [a_spec, b_spec] + ([extra_spec] if extra is not None else [])
    args = (a, b) + ((extra,) if extra is not None else ())
    if n_out == 2:
        out_specs = [o_spec, o_spec]
    else:
        out_specs = o_spec
    return _pcall(body, name=name, grid=grid, in_specs=in_specs, out_specs=out_specs,
                  out_shape=out_shape, scratch_shapes=[pltpu.VMEM(acc_shape, F32)],
                  compiler_params=_params(dimension_semantics=("parallel", "parallel", "arbitrary")))(*args)


def _tm(t):
    return _tile(t, (1408, 768, 384, 128))


def _mm_cols(name, a, wb, lead, out_dtype=F32, epi="plain"):
    t, kdim = a.shape
    n = wb.shape[-1]
    tm, tk = _tm(t), _tile(kdim, (1024, 512))
    nl = len(lead)
    b_spec = pl.BlockSpec((None,) * (1 + nl) + (tk, n), lambda i, j, k: (j,) + lead + (k, 0))
    o_spec = pl.BlockSpec((tm, n), lambda i, j, k: (i, j))
    if epi == "relu2":
        out_shape = [jax.ShapeDtypeStruct((t, N_DEV * n), F32), jax.ShapeDtypeStruct((t, N_DEV * n), BF16)]
    else:
        out_shape = jax.ShapeDtypeStruct((t, N_DEV * n), out_dtype)
    return _matmul(name, a, wb, grid=(t // tm, N_DEV, kdim // tk),
                   a_spec=pl.BlockSpec((tm, tk), lambda i, j, k: (i, k)), b_spec=b_spec, o_spec=o_spec,
                   out_shape=out_shape, contract="nn", acc_shape=(tm, n), epi=epi)


def _tm_deep(t, kdim):
    return _tm(t) if kdim <= 2048 else _tile(t, (704, 384, 128))


def _mm_cols_t_rms(name, a, wb, h, g, dres):
    t = a.shape[0]
    nb, kdim, n = wb.shape
    tm = _tile(t, (704, 384, 128))

    def body(a_ref, b_ref, h_ref, g_ref, r_ref, o_ref, dg_ref):
        @pl.when(pl.program_id(0) == 0)
        def _():
            dg_ref[...] = jnp.zeros_like(dg_ref)

        d = _dot(a_ref[:, 0:n].astype(BF16), b_ref[0], "nt")
        for j in range(1, nb):
            d = d + _dot(a_ref[:, j * n:(j + 1) * n].astype(BF16), b_ref[j], "nt")
        x = h_ref[...]
        rs = lax.rsqrt(jnp.mean(x * x, axis=-1, keepdims=True) + EPS)
        u = d * g_ref[...]
        m = jnp.mean(u * x, axis=-1, keepdims=True)
        o_ref[...] = r_ref[...] + rs * u - x * (rs * rs * rs * m)
        dg_ref[...] += jnp.sum(d * x * rs, axis=0, keepdims=True)

    row = pl.BlockSpec((tm, kdim), lambda i: (i, 0))
    vec = pl.BlockSpec((1, kdim), lambda i: (0, 0))
    return _pcall(body, name=name, grid=(t // tm,),
                  in_specs=[pl.BlockSpec((tm, nb * n), lambda i: (i, 0)),
                            pl.BlockSpec((nb, kdim, n), lambda i: (0, 0, 0)), row, vec, row],
                  out_specs=[row, vec],
                  out_shape=[jax.ShapeDtypeStruct((t, kdim), F32), jax.ShapeDtypeStruct((1, kdim), F32)],
                  compiler_params=_params(dimension_semantics=("arbitrary",)))(a, wb, h, g, dres)


def _mm_rows_t(name, a, wb, lead, out_dtype=F32, epi="plain", extra=None):
    t, n = a.shape
    r = wb.shape[-2]
    tm, tk = _tm(t), _tile(n, (1024,))
    nl = len(lead)
    b_spec = pl.BlockSpec((None,) * (1 + nl) + (r, tk), lambda i, j, k: (j,) + lead + (0, k))
    o_spec = pl.BlockSpec((tm, r), lambda i, j, k: (i, j))
    return _matmul(name, a, wb, grid=(t // tm, N_DEV, n // tk),
                   a_spec=pl.BlockSpec((tm, tk), lambda i, j, k: (i, k)), b_spec=b_spec, o_spec=o_spec,
                   out_shape=jax.ShapeDtypeStruct((t, N_DEV * r), out_dtype), contract="nt",
                   acc_shape=(tm, r), epi=epi, extra=extra, extra_spec=o_spec if extra is not None else None)


def _mm_rows_loss(name, a, wb, residual, target):
    t = a.shape[0]
    nb, r, n = wb.shape
    tm, tn = _tm_deep(t, nb * r), _tile(n, (512,))

    def body(a_ref, b_ref, r_ref, t_ref, d_ref, l_ref):
        i = pl.program_id(0)

        @pl.when((i == 0) & (pl.program_id(1) == 0))
        def _():
            l_ref[...] = jnp.zeros_like(l_ref)

        y = r_ref[...] + _dot(a_ref[...].astype(BF16), b_ref[...].reshape(nb * r, tn))
        diff = jnp.where(_row_ids(i, tm) >= TOK0, y - t_ref[...], 0.0)
        d_ref[...] = diff * (1.0 / D_MODEL)
        l_ref[...] += jnp.sum(diff * diff) * (0.5 / D_MODEL)

    o_spec = pl.BlockSpec((tm, tn), lambda i, j: (i, j))
    return _pcall(body, name=name, grid=(t // tm, n // tn),
                  in_specs=[pl.BlockSpec((tm, nb * r), lambda i, j: (i, 0)),
                            pl.BlockSpec((nb, r, tn), lambda i, j: (0, 0, j)), o_spec, o_spec],
                  out_specs=[o_spec, pl.BlockSpec((8, 128), lambda i, j: (0, 0))],
                  out_shape=[jax.ShapeDtypeStruct((t, n), F32), jax.ShapeDtypeStruct((8, 128), F32)],
                  compiler_params=_params(dimension_semantics=("arbitrary", "arbitrary")))(a, wb, residual, target)


def _mm_rows_norm(name, a, wb, residual, g):
    t = a.shape[0]
    nb, r, n = wb.shape
    tm = _tile(t, (704, 384, 128))

    def body(a_ref, b_ref, r_ref, g_ref, h_ref, hn_ref):
        h = r_ref[...] + _dot(a_ref[...].astype(BF16), b_ref[...].reshape(nb * r, n))
        h_ref[...] = h
        hn_ref[...] = (h * lax.rsqrt(jnp.mean(h * h, axis=-1, keepdims=True) + EPS) * g_ref[...]).astype(BF16)

    row = pl.BlockSpec((tm, n), lambda i: (i, 0))
    return _pcall(body, name=name, grid=(t // tm,),
                  in_specs=[pl.BlockSpec((tm, nb * r), lambda i: (i, 0)), pl.BlockSpec((nb, r, n), lambda i: (0, 0, 0)),
                            row, pl.BlockSpec((1, n), lambda i: (0, 0))],
                  out_specs=[row, row],
                  out_shape=[jax.ShapeDtypeStruct((t, n), F32), jax.ShapeDtypeStruct((t, n), BF16)],
                  compiler_params=_params(dimension_semantics=("parallel",)))(a, wb, residual, g)


def _wgrad_cols(name, x, dy, n):
    t, kdim = x.shape
    tk = _tm(t)
    return _matmul(name, x, dy, grid=(1, N_DEV, t // tk),
                   a_spec=pl.BlockSpec((tk, kdim), lambda i, j, k: (k, 0)),
                   b_spec=pl.BlockSpec((tk, n), lambda i, j, k: (k, j)),
                   o_spec=pl.BlockSpec((None, kdim, n), lambda i, j, k: (j, 0, 0)),
                   out_shape=jax.ShapeDtypeStruct((N_DEV, kdim, n), BF16), contract="tn", acc_shape=(kdim, n))


def _wgrad_rows(name, x, dy, r):
    t = x.shape[0]
    n = dy.shape[1]
    tk, tn = _tm(t), _tile(n, (512,))
    tm = min(N_DEV * r, 1024)
    out = _matmul(name, x, dy, grid=(N_DEV * r // tm, n // tn, t // tk),
                  a_spec=pl.BlockSpec((tk, tm), lambda i, j, k: (k, i)),
                  b_spec=pl.BlockSpec((tk, tn), lambda i, j, k: (k, j)),
                  o_spec=pl.BlockSpec((tm, tn), lambda i, j, k: (i, j)),
                  out_shape=jax.ShapeDtypeStruct((N_DEV * r, n), BF16), contract="tn", acc_shape=(tm, tn))
    return out.reshape(N_DEV, r, n)


def _rows(t):
    return _tile(t, (384, 128))


def _rms_fwd(name, h, g):
    t = h.shape[0]
    tr = _rows(t)

    def body(h_ref, g_ref, o_ref):
        x = h_ref[...]
        r = lax.rsqrt(jnp.mean(x * x, axis=-1, keepdims=True) + EPS)
        o_ref[...] = (x * r * g_ref[...]).astype(BF16)

    row = pl.BlockSpec((tr, D_MODEL), lambda i: (i, 0))
    vec = pl.BlockSpec((1, D_MODEL), lambda i: (0, 0))
    return _pcall(body, name=name, grid=(t // tr,), in_specs=[row, vec], out_specs=row,
                  out_shape=jax.ShapeDtypeStruct((t, D_MODEL), BF16))(h, g)


def _ret_tables(t):
    hh = np.arange(RET_HEADS, dtype=np.float64)
    log_g = np.log1p(-np.exp2(-RET_DECAY_OFFSET - hh))
    idx = np.arange(CHUNK, dtype=np.float64)
    diff = idx[:, None] - idx[None, :]
    dmat = np.where(diff[None] >= 0, np.exp(np.maximum(diff, 0.0)[None] * log_g[:, None, None]), 0.0)
    qdec = np.exp((idx + 1.0)[None, :, None] * log_g[:, None, None]) * np.ones((1, 1, CHUNK))
    kdec = np.exp((CHUNK - 1 - idx)[None, :, None] * log_g[:, None, None]) * np.ones((1, 1, CHUNK))
    half = CHUNK // 2
    inv_freq = (ROPE_BASE ** (-np.arange(half, dtype=np.float32) / half)).astype(np.float32)
    ang = (np.arange(t, dtype=np.float32)[:, None] * inv_freq[None, :]).astype(np.float32).astype(np.float64)
    cos2 = np.concatenate([np.cos(ang), np.cos(ang)], axis=1)
    sin2 = np.concatenate([-np.sin(ang), np.sin(ang)], axis=1)
    return tuple(jnp.asarray(v, F32) for v in (dmat, qdec, kdec, cos2, sin2))


def _rot(x, c, s):
    return x * c + pltpu.roll(x, CHUNK // 2, 1) * s


def _unrot(dx, c, s):
    return dx * c + pltpu.roll(dx * s, CHUNK // 2, 1)


def _dot(a, b, contract="nn"):
    return lax.dot_general(a, b, _DIMS[contract], preferred_element_type=F32)


def _ret_fwd(proj, gn_g, tables):
    t = proj.shape[0]
    nch = t // CHUNK
    dmat, qdec, kdec, cos2, sin2 = tables

    def body(qk_ref, v_ref, g_ref, w_ref, c_ref, s_ref, dm_ref, qd_ref, kd_ref, o_ref, st_ref, cat_ref, state):
        @pl.when(pl.program_id(0) == 0)
        def _():
            state[...] = jnp.zeros_like(state)

        c, s = c_ref[...], s_ref[...]
        for h in range(RET_HEADS):
            q = _rot(qk_ref[:, 128 * h:128 * (h + 1)], c, s)
            k = _rot(qk_ref[:, 512 + 128 * h:512 + 128 * (h + 1)], c, s) * RET_SCALE
            vb = v_ref[:, 256 * h:256 * (h + 1)].astype(BF16)
            st = state[h]
            st_ref[h] = st
            sc = _dot(q.astype(BF16), k.astype(BF16), "nt") * dm_ref[h]
            o = _dot(sc.astype(BF16), vb)
            o += _dot((q * qd_ref[h]).astype(BF16), st.astype(BF16))
            sl = slice(256 * h, 256 * (h + 1))
            o_ref[:, sl] = o
            kv = _dot((k * kd_ref[h]).astype(BF16), vb, "tn")
            state[h] = qd_ref[h, CHUNK - 1:CHUNK, 0:1] * st + kv
            mu = jnp.mean(o, axis=-1, keepdims=True)
            oc = o - mu
            rstd = lax.rsqrt(jnp.mean(oc * oc, axis=-1, keepdims=True) + EPS)
            g = g_ref[:, sl]
            cat_ref[:, sl] = (g * _sigmoid(g) * (oc * rstd * w_ref[:, sl])).astype(BF16)

    tab = pl.BlockSpec((RET_HEADS, CHUNK, CHUNK), lambda n: (0, 0, 0))
    pos = pl.BlockSpec((CHUNK, CHUNK), lambda n: (n, 0))
    row = pl.BlockSpec((CHUNK, 1024), lambda n: (n, 0))
    return _pcall(
        body, name="ret_fwd", grid=(nch,),
        in_specs=[row, pl.BlockSpec((CHUNK, 1024), lambda n: (n, 1)), pl.BlockSpec((CHUNK, 1024), lambda n: (n, 2)),
                  pl.BlockSpec((1, 1024), lambda n: (0, 0)), pos, pos, tab, tab, tab],
        out_specs=[row, pl.BlockSpec((RET_HEADS, None, 128, 256), lambda n: (0, n, 0, 0)), row],
        out_shape=[jax.ShapeDtypeStruct((t, 1024), F32), jax.ShapeDtypeStruct((RET_HEADS, nch, 128, 256), F32),
                   jax.ShapeDtypeStruct((t, 2048), BF16)],
        scratch_shapes=[pltpu.VMEM((RET_HEADS, 128, 256), F32)],
        compiler_params=_params(dimension_semantics=("arbitrary",)))(
            proj, proj, proj, gn_g, cos2, sin2, dmat, qdec, kdec)


def _ret_bwd(dproj, proj, states, do, tables):
    t = proj.shape[0]
    nch = t // CHUNK
    dmat, qdec, kdec, cos2, sin2 = tables

    def body(dp_in, qk_ref, v_ref, do_ref, st_ref, c_ref, s_ref, dm_ref, qd_ref, kd_ref, dp_ref, rst):
        del dp_in
        @pl.when(pl.program_id(0) == 0)
        def _():
            rst[...] = jnp.zeros_like(rst)

        c, s = c_ref[...], s_ref[...]
        for h in range(RET_HEADS):
            q = _rot(qk_ref[:, 128 * h:128 * (h + 1)], c, s)
            k = _rot(qk_ref[:, 512 + 128 * h:512 + 128 * (h + 1)], c, s) * RET_SCALE
            qb, kb = q.astype(BF16), k.astype(BF16)
            vb = v_ref[:, 256 * h:256 * (h + 1)].astype(BF16)
            dob = do_ref[:, 256 * h:256 * (h + 1)].astype(BF16)
            pb = st_ref[h].astype(BF16)
            r = rst[h]
            rb = r.astype(BF16)
            dm, qd, kd = dm_ref[h], qd_ref[h], kd_ref[h]
            sb = (_dot(qb, kb, "nt") * dm).astype(BF16)
            dsb = (_dot(dob, vb, "nt") * dm).astype(BF16)
            dq = _dot(dsb, kb) + _dot(dob, pb, "nt") * qd
            dk = _dot(dsb, qb, "tn") + _dot(vb, rb, "nt") * kd
            dv = _dot(sb, dob, "tn") + _dot((k * kd).astype(BF16), rb)
            rst[h] = _dot((q * qd).astype(BF16), dob, "tn") + qd[CHUNK - 1:CHUNK, 0:1] * r
            dp_ref[:, 128 * h:128 * (h + 1)] = _unrot(dq, c, s).astype(BF16)
            dp_ref[:, 512 + 128 * h:512 + 128 * (h + 1)] = (_unrot(dk, c, s) * RET_SCALE).astype(BF16)
            dp_ref[:, 1024 + 256 * h:1024 + 256 * (h + 1)] = dv.astype(BF16)

    rev = lambda n: nch - 1 - n
    tab = pl.BlockSpec((RET_HEADS, CHUNK, CHUNK), lambda n: (0, 0, 0))
    pos = pl.BlockSpec((CHUNK, CHUNK), lambda n: (rev(n), 0))
    row = pl.BlockSpec((CHUNK, 1024), lambda n: (rev(n), 0))
    return _pcall(
        body, name="ret_bwd", grid=(nch,),
        in_specs=[pl.BlockSpec(memory_space=pl.ANY), row, pl.BlockSpec((CHUNK, 1024), lambda n: (rev(n), 1)), row,
                  pl.BlockSpec((RET_HEADS, None, 128, 256), lambda n: (0, rev(n), 0, 0)),
                  pos, pos, tab, tab, tab],
        out_specs=pl.BlockSpec((CHUNK, 2048), lambda n: (rev(n), 0)),
        out_shape=jax.ShapeDtypeStruct((t, 5120), BF16),
        scratch_shapes=[pltpu.VMEM((RET_HEADS, 128, 256), F32)], input_output_aliases={0: 0},
        compiler_params=_params(dimension_semantics=("arbitrary",)))(
            dproj, proj, proj, do, states, cos2, sin2, dmat, qdec, kdec)


def _row_ids(i, tr):
    return i * tr + lax.broadcasted_iota(jnp.int32, (tr, 1), 0)


SH_ROWS = HALO - 8


def _shifted_copies(xs, sh, tr):
    for b in range(1, 8):
        sh[b - 1] = xs[pl.ds(b, tr + SH_ROWS), :]


def _shifted(xs, sh, off, tr):
    a, b = divmod(off, 8)
    return xs[pl.ds(8 * a, tr), :] if b == 0 else sh[b - 1, pl.ds(8 * a, tr), :]


def _conv_fwd(cat, proj, conv_w, conv_b, ln_g, ln_b):
    t = proj.shape[0]
    tr = _rows(t)
    hb = tr // HALO

    def body(cat_in, ua_ref, ug_ref, pa_ref, pg_ref, w_ref, b_ref, lg_ref, lb_ref, c_ref, hd_ref, y_ref, xs, sh):
        del cat_in
        i = pl.program_id(0)
        hdn = ua_ref[...] * _sigmoid(ug_ref[...])
        hd_ref[...] = hdn
        prev = pa_ref[...] * _sigmoid(pg_ref[...])
        xs[0:HALO, :] = jnp.where(i > 0, prev, 0.0)
        xs[HALO:HALO + tr, :] = hdn
        _shifted_copies(xs, sh, tr)
        acc = jnp.zeros((tr, 1024), F32) + b_ref[...]
        for w in range(CONV_WIDTH):
            acc += w_ref[w:w + 1, :] * _shifted(xs, sh, HALO - (CONV_WIDTH - 1) + w, tr)
        y_ref[...] = acc
        mu = jnp.mean(acc, axis=-1, keepdims=True)
        yc = acc - mu
        rstd = lax.rsqrt(jnp.mean(yc * yc, axis=-1, keepdims=True) + EPS)
        yn = yc * rstd * lg_ref[...] + lb_ref[...]
        c = yn * _sigmoid(yn)
        c_ref[...] = jnp.where(_row_ids(i, tr) >= PAD_FRONT, c, 0.0).astype(BF16)

    row = pl.BlockSpec((tr, 1024), lambda i: (i, 0))
    vec = pl.BlockSpec((1, 1024), lambda i: (0, 0))
    halo = lambda col: pl.BlockSpec((HALO, 1024), lambda i: (jnp.maximum(i * hb - 1, 0), col))
    return _pcall(body, name="conv_fwd", grid=(t // tr,),
                  in_specs=[pl.BlockSpec(memory_space=pl.ANY),
                            pl.BlockSpec((tr, 1024), lambda i: (i, 3)), pl.BlockSpec((tr, 1024), lambda i: (i, 4)),
                            halo(3), halo(4), pl.BlockSpec((32, 1024), lambda i: (0, 0)), vec, vec, vec],
                  out_specs=[pl.BlockSpec((tr, 1024), lambda i: (i, 1)), row, row],
                  out_shape=[jax.ShapeDtypeStruct((t, 2048), BF16), jax.ShapeDtypeStruct((t, 1024), F32),
                             jax.ShapeDtypeStruct((t, 1024), F32)],
                  scratch_shapes=[pltpu.VMEM((tr + HALO, 1024), F32), pltpu.VMEM((7, tr + SH_ROWS, 1024), F32)],
                  input_output_aliases={0: 0}, compiler_params=_params())(
                      cat, proj, proj, proj, proj, conv_w, conv_b, ln_g, ln_b)


def _mix_bwd_head(dh, w_out, o, proj, gn_g, y, ln_g, ln_b):
    t = dh.shape[0]
    tr = _rows(t)
    nb, r, n = w_out.shape

    def body(dh_ref, b_ref, o_ref, g_ref, w_ref, y_ref, lg_ref, lb_ref,
             do_ref, dp_ref, dw_ref, dy_ref, dlg_ref, dlb_ref, dcb_ref):
        i = pl.program_id(0)

        @pl.when(i == 0)
        def _():
            for ref in (dw_ref, dlg_ref, dlb_ref, dcb_ref):
                ref[...] = jnp.zeros_like(ref)

        dcat = _dot(dh_ref[...].astype(BF16), b_ref[...].reshape(nb * r, n), "nt")
        for h in range(RET_HEADS):
            sl = slice(256 * h, 256 * (h + 1))
            x = o_ref[:, sl]
            mu = jnp.mean(x, axis=-1, keepdims=True)
            xc = x - mu
            rstd = lax.rsqrt(jnp.mean(xc * xc, axis=-1, keepdims=True) + EPS)
            xh = xc * rstd
            w = w_ref[:, sl]
            g = g_ref[:, sl]
            sg = _sigmoid(g)
            d = dcat[:, sl]
            don = d * (g * sg)
            dp_ref[:, sl] = (d * (xh * w) * (sg * (1.0 + g * (1.0 - sg)))).astype(BF16)
            dw_ref[:, sl] += jnp.sum(don * xh, axis=0, keepdims=True)
            dxh = don * w
            m1 = jnp.mean(dxh, axis=-1, keepdims=True)
            m2 = jnp.mean(dxh * xh, axis=-1, keepdims=True)
            do_ref[:, sl] = rstd * (dxh - m1 - xh * m2)
        yv = y_ref[...]
        mu = jnp.mean(yv, axis=-1, keepdims=True)
        yc = yv - mu
        rstd = lax.rsqrt(jnp.mean(yc * yc, axis=-1, keepdims=True) + EPS)
        xh = yc * rstd
        lg = lg_ref[...]
        yn = xh * lg + lb_ref[...]
        sg = _sigmoid(yn)
        dyn = jnp.where(_row_ids(i, tr) >= PAD_FRONT, dcat[:, 1024:] * (sg * (1.0 + yn * (1.0 - sg))), 0.0)
        dlg_ref[...] += jnp.sum(dyn * xh, axis=0, keepdims=True)
        dlb_ref[...] += jnp.sum(dyn, axis=0, keepdims=True)
        dxh = dyn * lg
        m1 = jnp.mean(dxh, axis=-1, keepdims=True)
        m2 = jnp.mean(dxh * xh, axis=-1, keepdims=True)
        dy = rstd * (dxh - m1 - xh * m2)
        dy_ref[...] = dy
        dcb_ref[...] += jnp.sum(dy, axis=0, keepdims=True)

    row = pl.BlockSpec((tr, 1024), lambda i: (i, 0))
    vec = pl.BlockSpec((1, 1024), lambda i: (0, 0))
    gate = pl.BlockSpec((tr, 1024), lambda i: (i, 2))
    vsh = jax.ShapeDtypeStruct((1, 1024), F32)
    fsh = jax.ShapeDtypeStruct((t, 1024), F32)
    return _pcall(body, name="mix_bwd_head", grid=(t // tr,),
                  in_specs=[row, pl.BlockSpec((nb, r, n), lambda i: (0, 0, 0)), row, gate, vec, row, vec, vec],
                  out_specs=[row, gate, vec, row, vec, vec, vec],
                  out_shape=[fsh, jax.ShapeDtypeStruct((t, 5120), BF16), vsh, fsh, vsh, vsh, vsh],
                  compiler_params=_params(dimension_semantics=("arbitrary",)))(
                      dh, w_out, o, proj, gn_g, y, ln_g, ln_b)


def _conv_bwd_taps(dproj, dy, hdn, proj, conv_w):
    t = dy.shape[0]
    tr = _rows(t)
    hb = tr // HALO
    nt = t // tr

    def body(dp_in, dy_ref, nx_ref, hd_ref, ph_ref, ua_ref, ug_ref, w_ref, da_ref, dg_ref, dw_ref, xs, sh):
        del dp_in
        i = pl.program_id(0)

        @pl.when(i == 0)
        def _():
            dw_ref[...] = jnp.zeros_like(dw_ref)

        dy = dy_ref[...]
        xs[0:tr, :] = dy
        xs[tr:tr + HALO, :] = jnp.where(i < nt - 1, nx_ref[...], 0.0)
        _shifted_copies(xs, sh, tr)
        dh = jnp.zeros((tr, 1024), F32)
        for w in range(CONV_WIDTH):
            dh += w_ref[w:w + 1, :] * _shifted(xs, sh, CONV_WIDTH - 1 - w, tr)
        xs[0:HALO, :] = jnp.where(i > 0, ph_ref[...], 0.0)
        xs[HALO:HALO + tr, :] = hd_ref[...]
        _shifted_copies(xs, sh, tr)
        for w in range(CONV_WIDTH):
            dw_ref[w:w + 1, :] += jnp.sum(dy * _shifted(xs, sh, HALO - (CONV_WIDTH - 1) + w, tr), axis=0, keepdims=True)
        dh = jnp.where(_row_ids(i, tr) >= PAD_FRONT, dh, 0.0)
        sg = _sigmoid(ug_ref[...])
        da_ref[...] = (dh * sg).astype(BF16)
        dg_ref[...] = (dh * ua_ref[...] * sg * (1.0 - sg)).astype(BF16)

    row = pl.BlockSpec((tr, 1024), lambda i: (i, 0))
    return _pcall(body, name="conv_bwd_taps", grid=(nt,),
                  in_specs=[pl.BlockSpec(memory_space=pl.ANY),
                            row, pl.BlockSpec((HALO, 1024), lambda i: (jnp.minimum((i + 1) * hb, nt * hb - 1), 0)),
                            row, pl.BlockSpec((HALO, 1024), lambda i: (jnp.maximum(i * hb - 1, 0), 0)),
                            pl.BlockSpec((tr, 1024), lambda i: (i, 3)), pl.BlockSpec((tr, 1024), lambda i: (i, 4)),
                            pl.BlockSpec((32, 1024), lambda i: (0, 0))],
                  out_specs=[pl.BlockSpec((tr, 1024), lambda i: (i, 3)), row, pl.BlockSpec((32, 1024), lambda i: (0, 0))],
                  out_shape=[jax.ShapeDtypeStruct((t, 5120), BF16), jax.ShapeDtypeStruct((t, 1024), BF16),
                             jax.ShapeDtypeStruct((32, 1024), F32)],
                  scratch_shapes=[pltpu.VMEM((tr + HALO, 1024), F32), pltpu.VMEM((7, tr + SH_ROWS, 1024), F32)],
                  input_output_aliases={0: 0}, compiler_params=_params())(
                      dproj, dy, dy, hdn, hdn, proj, proj, conv_w)


NEG_BIG = -1e30


def _seg_tables(qb):
    j = np.arange(128)
    bd = (j[:, None] // 64 == j[None, :] // 64).astype(np.float32)
    ones = np.ones((128, 128), np.float32)
    later = np.concatenate([(j[:, None] >= j[None, :]).astype(np.float32), ones], axis=1)
    earlier = np.concatenate([(j[:, None] < j[None, :]).astype(np.float32), ones], axis=1)
    per = qb // CHUNK
    row = np.arange(qb)[:, None]
    pad = np.broadcast_to(j[None, :] < PAD_FRONT, (qb, 128))
    diag = [(g * CHUNK + j[None, :]) >= row for g in range(per)]
    masks = diag + [np.zeros((qb, 128), bool), pad, diag[0] | pad]
    bias = np.stack([np.where(m, NEG_BIG, 0.0) for m in masks]).astype(np.float32)
    dup = lambda m: np.concatenate([m, m], axis=0)
    return (jnp.asarray(bd, BF16), jnp.asarray(dup(later), BF16), jnp.asarray(dup(earlier), BF16),
            jnp.asarray(bias, F32))


def _split_dot(x, m):
    hi = x.astype(BF16)
    lo = (x - hi.astype(F32)).astype(BF16)
    return _dot(hi, m) + _dot(lo, m)


def _qk_norm_fwd(qkv, qg, kg, bd):
    t = qkv.shape[0]
    tr = _rows(t)
    nb = tr // CHUNK

    def body(q_ref, k_ref, v_ref, qg_ref, kg_ref, bd_ref, qo, kt, k2, vt, v2):
        bdm = bd_ref[...]
        lane = lax.broadcasted_iota(jnp.int32, (1, 128), 1)
        sub = lax.broadcasted_iota(jnp.int32, (128, 1), 0)

        def pair_layouts(x, t_ref, s_ref, hp, b):
            xt = x.T
            t_ref[hp, b] = jnp.concatenate([jnp.where(sub < 64, xt, 0.0), jnp.where(sub >= 64, xt, 0.0)],
                                           axis=1).astype(BF16)
            s_ref[hp, b] = jnp.concatenate([jnp.where(lane < 64, x, 0.0), jnp.where(lane >= 64, x, 0.0)],
                                           axis=0).astype(BF16)

        for hp in range(8):
            sl = slice(128 * hp, 128 * (hp + 1))
            x = q_ref[:, sl]
            r = lax.rsqrt(_split_dot(x * x, bdm) * (1.0 / 64) + EPS)
            qo[:, sl] = (x * r * (qg_ref[:, sl] * SB_SCALE)).astype(BF16)
            x = k_ref[:, sl]
            r = lax.rsqrt(_split_dot(x * x, bdm) * (1.0 / 64) + EPS)
            kn = x * r * kg_ref[:, sl]
            v = v_ref[:, sl]
            for b in range(nb):
                rows = slice(CHUNK * b, CHUNK * (b + 1))
                pair_layouts(kn[rows], kt, k2, hp, b)
                pair_layouts(v[rows], vt, v2, hp, b)

    col = lambda c: pl.BlockSpec((tr, 1024), lambda i: (i, c))
    vec = pl.BlockSpec((1, 1024), lambda i: (0, 0))
    wide = pl.BlockSpec((8, nb, 128, 256), lambda i: (0, i, 0, 0))
    tall = pl.BlockSpec((8, nb, 256, 128), lambda i: (0, i, 0, 0))
    wsh = jax.ShapeDtypeStruct((8, t // CHUNK, 128, 256), BF16)
    tsh = jax.ShapeDtypeStruct((8, t // CHUNK, 256, 128), BF16)
    return _pcall(body, name="qk_norm_fwd", grid=(t // tr,),
                  in_specs=[col(0), col(1), col(2), vec, vec, pl.BlockSpec((128, 128), lambda i: (0, 0))],
                  out_specs=[col(0), wide, tall, wide, tall],
                  out_shape=[jax.ShapeDtypeStruct((t, 1024), BF16), wsh, tsh, wsh, tsh])(qkv, qkv, qkv, qg, kg, bd)


def _qk_norm_bwd(qkv, dq, dk, dv, qg, kg, bd):
    t = qkv.shape[0]
    tr = _rows(t)

    def body(q_ref, k_ref, dq_ref, dk_ref, dv_ref, qg_ref, kg_ref, bd_ref, o_ref, dqg_ref, dkg_ref):
        @pl.when(pl.program_id(0) == 0)
        def _():
            dqg_ref[...] = jnp.zeros_like(dqg_ref)
            dkg_ref[...] = jnp.zeros_like(dkg_ref)

        bdm = bd_ref[...]
        for part, (src, d_ref, g_ref, dg_ref) in enumerate(((q_ref, dq_ref, qg_ref, dqg_ref),
                                                           (k_ref, dk_ref, kg_ref, dkg_ref))):
            for cix in range(8):
                sl = slice(128 * cix, 128 * (cix + 1))
                x = src[:, sl]
                d = d_ref[:, sl]
                r = lax.rsqrt(_split_dot(x * x, bdm) * (1.0 / 64) + EPS)
                u = d * g_ref[:, sl]
                m = _split_dot(u * x, bdm) * (1.0 / 64)
                o_ref[:, 1024 * part + 128 * cix:1024 * part + 128 * (cix + 1)] = (r * u - x * (r * r * r * m)).astype(BF16)
                dg_ref[:, sl] += jnp.sum(d * x * r, axis=0, keepdims=True)
        o_ref[:, 2048:3072] = dv_ref[...].astype(BF16)

    col = lambda c: pl.BlockSpec((tr, 1024), lambda i: (i, c))
    vec = pl.BlockSpec((1, 1024), lambda i: (0, 0))
    vsh = jax.ShapeDtypeStruct((1, 1024), F32)
    return _pcall(body, name="qk_norm_bwd", grid=(t // tr,),
                  in_specs=[col(0), col(1), col(0), col(0), col(0), vec, vec, pl.BlockSpec((128, 128), lambda i: (0, 0))],
                  out_specs=[pl.BlockSpec((tr, 3072), lambda i: (i, 0)), vec, vec],
                  out_shape=[jax.ShapeDtypeStruct((t, 3072), BF16), vsh, vsh])(qkv, qkv, dq, dk, dv, qg, kg, bd)


def _split2(x):
    hi = x.astype(BF16)
    lo = (x - hi.astype(F32)).astype(BF16)
    return jnp.concatenate([hi, lo], axis=1)


def _sb_scores(z, later_tab):
    e = jnp.exp(-jnp.abs(z))
    ope = 1.0 + e
    sp = jnp.maximum(z, 0.0) + jnp.log(ope)
    return e, ope, _dot(_split2(sp), later_tab)


def _sb_bias_index(i, kb, per):
    g = kb - i * per
    return jnp.where(kb == 0, jnp.where(i == 0, per + 2, per + 1), jnp.where(g >= 0, g, per))


def _sb_qb(t):
    return _tile(t, (384, 128))


def _sb_fwd(qh, kt, v2, later_tab, bias_tab):
    t = qh.shape[0]
    qb = _sb_qb(t)
    per = qb // CHUNK
    nkb_all = t // CHUNK

    nq = t // qb

    def body(q_ref, kt_ref, v2_ref, tab_ref, bias_ref, o_ref, ws_ref, acc, carry, zbuf, wbuf, wsem):
        h, i = pl.program_id(0), pl.program_id(1)
        n = h * nq + i
        p = n & 1
        q = q_ref[...]
        acc[...] = jnp.zeros_like(acc)
        carry[...] = jnp.zeros_like(carry)
        nkb = (i + 1) * per
        save = lambda kb: pltpu.make_async_copy(wbuf.at[p, kb], ws_ref.at[h, i, kb], wsem.at[p, kb])

        def drain(step, par):
            hs, is_ = step // nq, step % nq

            def one(kb, _):
                pltpu.make_async_copy(wbuf.at[par, kb], ws_ref.at[hs, is_, kb], wsem.at[par, kb]).wait()
                return 0

            lax.fori_loop(0, (is_ + 1) * per, one, 0)

        @pl.when(n >= 2)
        def _():
            drain(n - 2, p)

        for u in range(per):
            zbuf[u] = _dot(q, kt_ref[nkb - 1 - u])

        def trip(s, diagonal):
            top = nkb - 1 - per * s
            if not diagonal:
                for u in range(per):
                    save(top + per - u).start()
            z2s = [zbuf[u] for u in range(per)]
            for u in range(per):
                zbuf[u] = _dot(q, kt_ref[jnp.maximum(top - per - u, 0)])
            first = [CHUNK * (per - 1 - u) if diagonal else 0 for u in range(per)]
            cins = [carry[0], carry[1]]
            zs, cus = [], []
            for u in range(per):
                bias = bias_ref[_sb_bias_index(i, top - u, per)][first[u]:]
                zs.append([z2s[u][first[u]:, 128 * hh:128 * (hh + 1)] + bias for hh in range(2)])
                cus.append([_sb_scores(z, tab_ref[...])[2] for z in zs[u]])
            part = None
            for u in range(per):
                kb, lo = top - u, first[u]
                for hh in range(2):
                    sl = slice(128 * hh, 128 * (hh + 1))
                    cu = cus[u][hh]
                    wbuf[p, kb, lo:, sl] = jnp.exp(zs[u][hh] - cu[:, :128] - cins[hh][lo:]).astype(BF16)
                    if lo:
                        wbuf[p, kb, :lo, sl] = jnp.zeros((lo, 128), BF16)
                        cins[hh] = jnp.concatenate([cins[hh][:lo], cins[hh][lo:] + cu[:, 128:]], axis=0)
                    else:
                        cins[hh] = cins[hh] + cu[:, 128:]
                d = _dot(wbuf[p, kb], v2_ref[kb])
                part = d if part is None else part + d
            carry[0], carry[1] = cins[0], cins[1]
            acc[...] += part

        trip(0, True)

        def step(s, _):
            trip(s, False)
            return 0

        lax.fori_loop(1, nkb // per, step, 0)
        for u in range(per):
            save(per - 1 - u).start()
        o_ref[...] = acc[...]

        @pl.when(n == 8 * nq - 1)
        def _():
            drain(n - 1, 1 - p)
            drain(n, p)

    blk = pl.BlockSpec((qb, 128), lambda h, i: (i, h))
    wide = pl.BlockSpec((None, nkb_all, 128, 256), lambda h, i: (h, 0, 0, 0))
    tall = pl.BlockSpec((None, nkb_all, 256, 128), lambda h, i: (h, 0, 0, 0))
    return _pcall(body, name="sb_fwd", grid=(8, t // qb),
                  in_specs=[blk, wide, tall, pl.BlockSpec((256, 256), lambda h, i: (0, 0)),
                            pl.BlockSpec((per + 3, qb, 128), lambda h, i: (0, 0, 0))],
                  out_specs=[blk, pl.BlockSpec(memory_space=pl.ANY)],
                  out_shape=[jax.ShapeDtypeStruct((t, 1024), F32),
                             jax.ShapeDtypeStruct((8, t // qb, nkb_all, qb, 256), BF16)],
                  scratch_shapes=[pltpu.VMEM((qb, 128), F32), pltpu.VMEM((2, qb, 128), F32),
                                  pltpu.VMEM((per, qb, 256), F32), pltpu.VMEM((2, nkb_all, qb, 256), BF16),
                                  pltpu.SemaphoreType.DMA((2, nkb_all))],
                  compiler_params=_params(dimension_semantics=("arbitrary", "arbitrary")))(
                      qh, kt, v2, later_tab, bias_tab)


def _sb_bwd(qh, kt, k2, vt, wsave, do, earlier_tab, bias_tab):
    t = qh.shape[0]
    qb = _sb_qb(t)
    per = qb // CHUNK
    nkb_all = t // CHUNK

    zero_slot = nkb_all
    nq = t // qb

    def body(q_ref, kt_ref, k2_ref, vt_ref, ws_ref, do_ref, etab_ref, bias_ref,
             dq_ref, dk_ref, dv_ref, acc, gcarry, zbuf, dwbuf, wbuf, wsem, dzbuf):
        h, i = pl.program_id(0), pl.program_id(1)
        n = h * nq + i
        p = n & 1

        @pl.when(i == 0)
        def _():
            dk_ref[...] = jnp.zeros_like(dk_ref)
            dv_ref[...] = jnp.zeros_like(dv_ref)

        nkb = (i + 1) * per
        fetch = lambda kb: pltpu.make_async_copy(ws_ref.at[h, i, kb], wbuf.at[p, kb], wsem.at[p, kb])

        def prefetch(step, par):
            hs, is_ = step // nq, step % nq

            def one(kb, _):
                pltpu.make_async_copy(ws_ref.at[hs, is_, kb], wbuf.at[par, kb], wsem.at[par, kb]).start()
                return 0

            lax.fori_loop(0, (is_ + 1) * per, one, 0)

        @pl.when(n == 0)
        def _():
            prefetch(n, p)

        @pl.when(n + 1 < 8 * nq)
        def _():
            prefetch(n + 1, 1 - p)

        q = q_ref[...]
        dob = do_ref[...].astype(BF16)
        lane = lax.broadcasted_iota(jnp.int32, (1, 128), 1)
        acc[...] = jnp.zeros_like(acc)
        gcarry[...] = jnp.zeros_like(gcarry)
        zbuf[...] = _dot(q, kt_ref[0])
        dwbuf[...] = _dot(dob, vt_ref[0])
        dzbuf[...] = jnp.zeros_like(dzbuf)
        wbuf[p, zero_slot] = jnp.zeros((qb, 256), BF16)

        def gradients(slot, kb):
            dz2 = dzbuf[...]
            acc[...] += _dot(dz2, k2_ref[kb])
            dk2 = _dot(dz2, q, "tn")
            dv2 = _dot(wbuf[p, slot], dob, "tn")
            dk_ref[kb] += jnp.where(lane < 64, dk2[:128], dk2[128:])
            dv_ref[kb] += jnp.where(lane < 64, dv2[:128], dv2[128:])

        def trip(kb, lo):
            fetch(kb).wait()
            bias = bias_ref[_sb_bias_index(i, kb, per)][lo:]
            z2 = zbuf[...]
            dw2 = dwbuf[...]
            nxt = jnp.minimum(kb + 1, nkb - 1)
            zbuf[...] = _dot(q, kt_ref[nxt])
            dwbuf[...] = _dot(dob, vt_ref[nxt])
            gradients(jnp.where(kb == 0, zero_slot, kb - 1), jnp.maximum(kb - 1, 0))
            w2 = wbuf[p, kb]
            for hh in range(2):
                sl = slice(128 * hh, 128 * (hh + 1))
                z = z2[lo:, sl] + bias
                e = jnp.exp(-jnp.abs(z))
                r = 1.0 / (1.0 + e)
                sig = jnp.where(z >= 0, r, e * r)
                gw = w2[lo:, sl].astype(F32) * dw2[lo:, sl]
                cu2 = _dot(_split2(gw), etab_ref[...])
                gin = gcarry[hh, lo:, :]
                gcarry[hh, lo:, :] = gin + cu2[:, 128:]
                dzbuf[lo:, sl] = (gw - sig * (gw + cu2[:, :128] + gin)).astype(BF16)
                if lo:
                    dzbuf[:lo, sl] = jnp.zeros((lo, 128), BF16)

        def step(kb, _):
            trip(kb, 0)
            return 0

        lax.fori_loop(0, nkb - per, step, 0)
        for g in range(per):
            trip(nkb - per + g, CHUNK * g)
        gradients(nkb - 1, nkb - 1)
        dq_ref[...] = acc[...] * SB_SCALE

    blk = pl.BlockSpec((qb, 128), lambda h, i: (i, h))
    wide = pl.BlockSpec((None, nkb_all, 128, 256), lambda h, i: (h, 0, 0, 0))
    tall = pl.BlockSpec((None, nkb_all, 256, 128), lambda h, i: (h, 0, 0, 0))
    tab = pl.BlockSpec((256, 256), lambda h, i: (0, 0))
    kv_out = pl.BlockSpec((nkb_all, 128, 128), lambda h, i: (0, 0, h))
    ksh = jax.ShapeDtypeStruct((nkb_all, 128, 1024), F32)
    dq, dk, dv = _pcall(
        body, name="sb_bwd", grid=(8, t // qb),
        in_specs=[blk, wide, tall, wide, pl.BlockSpec(memory_space=pl.ANY), blk, tab,
                  pl.BlockSpec((per + 3, qb, 128), lambda h, i: (0, 0, 0))],
        out_specs=[blk, kv_out, kv_out], out_shape=[jax.ShapeDtypeStruct((t, 1024), F32), ksh, ksh],
        scratch_shapes=[pltpu.VMEM((qb, 128), F32), pltpu.VMEM((2, qb, 128), F32),
                        pltpu.VMEM((qb, 256), F32), pltpu.VMEM((qb, 256), F32),
                        pltpu.VMEM((2, nkb_all + 1, qb, 256), BF16), pltpu.SemaphoreType.DMA((2, nkb_all)),
                        pltpu.VMEM((qb, 256), BF16)],
        compiler_params=_params(dimension_semantics=("arbitrary", "arbitrary")))(
            qh, kt, k2, vt, wsave, do, earlier_tab, bias_tab)
    return dq, dk.reshape(t, 1024), dv.reshape(t, 1024)


def _adamw_math(w, g, m, v):
    m = ADAM_B1 * m + (1.0 - ADAM_B1) * g
    v = ADAM_B2 * v + (1.0 - ADAM_B2) * (g * g)
    m_hat = m / (1.0 - ADAM_B1 ** ADAM_STEP)
    v_hat = v / (1.0 - ADAM_B2 ** ADAM_STEP)
    delta = -ADAM_LR * (m_hat / (jnp.sqrt(v_hat) + ADAM_EPS) + ADAM_WD * w)
    return delta, m, v


def _adamw(name, w, owns, recvs, m, v, me):
    shape = w.shape
    c = shape[-1]
    nl = len(owns)
    w3, m3, v3 = (a.reshape(nl, -1, c) for a in (w, m, v))
    r = w3.shape[1]
    tr = _tile(r, (256, 128))
    owns = [o.reshape(N_DEV, r, c) for o in owns]
    recvs = [p.reshape(N_DEV - 1, r, c) for p in recvs]

    def body(me_ref, w_ref, *rest):
        own_refs, recv_refs = rest[:nl], rest[nl:2 * nl]
        m_ref, v_ref = rest[2 * nl:2 * nl + 2]
        g_out, d_out, m_out, v_out = rest[2 * nl + 2:]
        layer = pl.program_id(0)

        def grad(k):
            g = own_refs[k][...].astype(F32)
            for s in range(N_DEV - 1):
                g = g + recv_refs[k][s].astype(F32)
            return g

        g = grad(0)
        for k in range(1, nl):
            g = jnp.where(layer == k, grad(k), g)
        d, mn, vn = _adamw_math(w_ref[...], g, m_ref[...], v_ref[...])
        g_out[...] = g
        d_out[...] = d
        m_out[...] = mn
        v_out[...] = vn

    row = pl.BlockSpec((None, tr, c), lambda l, i, me_ref: (l, i, 0))
    own = lambda k: pl.BlockSpec((None, tr, c), lambda l, i, me_ref: (me_ref[0], jnp.where(l == k, i, 0), 0))
    rcv = lambda k: pl.BlockSpec((N_DEV - 1, tr, c), lambda l, i, me_ref: (0, jnp.where(l == k, i, 0), 0))
    osh = jax.ShapeDtypeStruct((nl, r, c), F32)
    grid_spec = pltpu.PrefetchScalarGridSpec(
        num_scalar_prefetch=1, grid=(nl, r // tr),
        in_specs=[row] + [own(k) for k in range(nl)] + [rcv(k) for k in range(nl)] + [row, row],
        out_specs=[row, row, row, row])
    outs = _pcall(body, name=name, grid_spec=grid_spec, out_shape=[osh, osh, osh, osh])(
        me.reshape(1), w3, *owns, *recvs, m3, v3)
    return tuple(o.reshape(shape) for o in outs)


def _place():
    x, y, c = lax.axis_index("x"), lax.axis_index("y"), lax.axis_index("c")
    return x, y, c, 4 * x + 2 * y + c


def _peer(x, y, c, rel):
    return (x ^ ((rel >> 2) & 1), y ^ ((rel >> 1) & 1), c ^ (rel & 1))


def _gather_first(now, later):
    n, k = len(now), len(later)

    def body(*refs):
        ins, outs = refs[:n + k], refs[n + k:2 * (n + k)]
        send, recv, lsem = refs[2 * (n + k):]
        x, y, c, me = _place()
        locals_ = []
        for w in range(n + k):
            local = pltpu.make_async_copy(ins[w], outs[w].at[me], lsem.at[w])
            local.start()
            locals_.append(local)
        def copy(w, src, slot, rel, to_rel):
            return pltpu.make_async_remote_copy(src_ref=src, dst_ref=outs[w].at[slot], send_sem=send.at[w, rel - 1],
                                                recv_sem=recv.at[w, rel - 1], device_id=_peer(x, y, c, to_rel),
                                                device_id_type=MESH)

        for w in range(n):
            for rel in (1, 2, 4, 6):
                copy(w, ins[w], me, rel, rel).start()
        for w in range(n):
            for rel in (2, 4, 6):
                copy(w, ins[w], me ^ rel, rel, rel).wait_recv()
                copy(w, outs[w].at[me ^ rel], me ^ rel, rel | 1, 1).start()
        for w in range(n):
            for rel in (1, 3, 5, 7):
                copy(w, ins[w], me ^ rel, rel, 1).wait_recv()
            for rel in range(1, N_DEV):
                copy(w, ins[w], me, rel, rel).wait_send()
        for local in locals_:
            local.wait()

    hbm = pl.BlockSpec(memory_space=pl.ANY)
    vmem = pl.BlockSpec(memory_space=pltpu.VMEM)
    arrays = list(now) + list(later)
    return _pcall(body, name="gather_first", in_specs=[vmem] * (n + k), out_specs=[hbm] * (n + k),
                  out_shape=[jax.ShapeDtypeStruct((N_DEV,) + a.shape, a.dtype) for a in arrays],
                  scratch_shapes=[pltpu.SemaphoreType.DMA((n, N_DEV - 1)), pltpu.SemaphoreType.DMA((n, N_DEV - 1)),
                                  pltpu.SemaphoreType.DMA((n + k,))],
                  compiler_params=_params(has_side_effects=True))(*arrays)


_HBM = pl.BlockSpec(memory_space=pltpu.HBM)
_SEM = pl.BlockSpec(memory_space=pltpu.SEMAPHORE)
_DATAFLOW = pltpu.SideEffectType.DATAFLOW_SIDE_EFFECTING


def _exchange_refs(srcs, lands, mode, me, rel, j):
    if mode == "gather":
        return srcs[j], lands[j].at[me], lands[j].at[me ^ rel]
    return srcs[j].at[me ^ rel], lands[j].at[rel - 1], lands[j].at[rel - 1]


def _exchange_start(name, srcs, lands, mode):
    n = len(srcs)

    def body(*refs):
        ins, lnd = refs[:n], refs[n:2 * n]
        send, recv = refs[2 * n], refs[2 * n + 1]
        token = refs[-1]
        x, y, c, me = _place()
        for j in range(n):
            for rel in range(1, N_DEV):
                src, dst, _ = _exchange_refs(ins, lnd, mode, me, rel, j)
                pltpu.make_async_remote_copy(src_ref=src, dst_ref=dst, send_sem=send.at[j * (N_DEV - 1) + rel - 1],
                                             recv_sem=recv.at[j * (N_DEV - 1) + rel - 1],
                                             device_id=_peer(x, y, c, rel), device_id_type=MESH).start()
        token[...] = jnp.zeros_like(token)

    sems = pltpu.SemaphoreType.DMA((n * (N_DEV - 1),))
    hbm_like = lambda a: pltpu.HBM(a.shape, a.dtype)
    outs = _pcall(body, name=name + "_start",
                  in_specs=[_HBM] * (2 * n), out_specs=[_SEM, _SEM] + [_HBM] * (2 * n) + [pl.BlockSpec(memory_space=pltpu.VMEM)],
                  out_shape=[sems, sems] + [hbm_like(a) for a in srcs] + [hbm_like(a) for a in lands]
                  + [jax.ShapeDtypeStruct((8, 128), F32)],
                  input_output_aliases={i: 2 + i for i in range(2 * n)},
                  compiler_params=pltpu.CompilerParams(has_side_effects=_DATAFLOW))(
                      *[pltpu.with_memory_space_constraint(a, pltpu.HBM) for a in list(srcs) + list(lands)])
    return dict(name=name, mode=mode, n=n, send=outs[0], recv=outs[1], srcs=outs[2:2 + n], lands=outs[2 + n:2 + 2 * n],
                token=outs[-1][0, 0])


def _exchange_wait(ex, after):
    n, mode = ex["n"], ex["mode"]

    def body(*refs):
        ins, lnd = refs[:n], refs[n:2 * n]
        send, recv = refs[2 * n], refs[2 * n + 1]
        x, y, c, me = _place()
        for j in range(n):
            for rel in range(1, N_DEV):
                src, dst, landed = _exchange_refs(ins, lnd, mode, me, rel, j)
                pltpu.make_async_remote_copy(src_ref=src, dst_ref=dst, send_sem=send.at[j * (N_DEV - 1) + rel - 1],
                                             recv_sem=recv.at[j * (N_DEV - 1) + rel - 1],
                                             device_id=_peer(x, y, c, rel), device_id_type=MESH).wait_send()
                pltpu.make_async_remote_copy(src_ref=src, dst_ref=landed, send_sem=send.at[j * (N_DEV - 1) + rel - 1],
                                             recv_sem=recv.at[j * (N_DEV - 1) + rel - 1],
                                             device_id=_peer(x, y, c, rel), device_id_type=MESH).wait_recv()

    hbm_like = lambda a: pltpu.HBM(a.shape, a.dtype)
    arrays = list(ex["srcs"]) + list(ex["lands"])
    outs = _pcall(body, name=ex["name"] + "_wait",
                  in_specs=[_HBM] * (2 * n) + [_SEM, _SEM, pl.BlockSpec(memory_space=pl.ANY)],
                  out_specs=[_HBM] * (2 * n), out_shape=[hbm_like(a) for a in arrays],
                  input_output_aliases={i: i for i in range(2 * n)},
                  compiler_params=pltpu.CompilerParams(has_side_effects=_DATAFLOW))(
                      *arrays, ex["send"], ex["recv"], after)
    return outs[:n], outs[n:]


def _scatter_start(name, grads):
    lands = [lax.empty((N_DEV - 1,) + g.shape[1:], g.dtype) for g in grads]
    return _exchange_start(name, grads, lands, "scatter")


ROW_MIX, ROW_MLP, ROW_CB, ROW_LG, ROW_LB, ROW_QN, ROW_KN, ROW_LOSS = 0, 2, 4, 5, 6, 7, 8, 9
ROW_META, ROW_CW, ROW_GN, SMALL_ROWS = 16, 32, 64, 72


def _sum_small(slots):
    def body(s_ref, o_ref):
        tot = s_ref[0]
        for s in range(1, N_DEV):
            tot = tot + s_ref[s]
        o_ref[...] = tot
        for row in (ROW_QN, ROW_KN):
            v = tot[row:row + 1, :]
            f = v[:, 0:128]
            for k in range(1, 8):
                f = f + v[:, 128 * k:128 * (k + 1)]
            o_ref[row:row + 1, 0:64] = f[:, 0:64] + f[:, 64:128]

    return _pcall(body, name="sum_small", out_shape=jax.ShapeDtypeStruct(slots.shape[1:], F32))(slots)


def _adamw_small(w, g, m, v):
    def body(w_ref, g_ref, m_ref, v_ref, d_out, m_out, v_out):
        d, mn, vn = _adamw_math(w_ref[...], g_ref[...], m_ref[...], v_ref[...])
        d_out[...] = d
        m_out[...] = mn
        v_out[...] = vn

    osh = jax.ShapeDtypeStruct(w.shape, F32)
    return _pcall(body, name="adamw_small", out_shape=[osh, osh, osh])(w, g, m, v)


def _local_step(h0, target, p, weight, emit):
    t = h0.shape[0]
    tables = _ret_tables(t)
    bd, later_tab, earlier_tab, bias_tab = _seg_tables(_sb_qb(t))
    row = lambda a, i: a[i:i + 1]

    hn_a = _rms_fwd("rms_mix0", h0, row(p["norm_mix_g"], 0))
    w_in = weight("w_in", hn_a)
    proj = _mm_cols("proj_in", hn_a, w_in, ())
    gn_flat = p["gn_g"].reshape(1, 1024)
    o_ret, states, cat = _ret_fwd(proj, gn_flat, tables)
    cat, hdn, ycv = _conv_fwd(cat, proj, p["conv_w"], p["conv_b"], p["ln_g"], p["ln_b"])
    w_out = weight("w_out", cat)
    h1, hn_b = _mm_rows_norm("mix_out", cat, w_out, h0, row(p["norm_mlp_g"], 0))
    w1_0, w2_0 = weight("w1_0", hn_b), weight("w2_0", hn_b)
    a0, s0 = _mm_cols("mlp0_up", hn_b, w1_0, (), epi="relu2")
    h2, hn_c = _mm_rows_norm("mlp0_down", s0, w2_0, h1, row(p["norm_mix_g"], 1))

    w_qkv = weight("w_qkv", hn_c)
    qkv = _mm_cols("qkv", hn_c, w_qkv, ())
    qg = jnp.tile(p["qn_g"], (1, 16))
    kg = jnp.tile(p["kn_g"], (1, 16))
    qh, kt, k2, vt, v2 = _qk_norm_fwd(qkv, qg, kg, bd)
    o_sb, w_sb = _sb_fwd(qh, kt, v2, later_tab, bias_tab)
    w_o = weight("w_o", o_sb)
    h3, hn_d = _mm_rows_norm("attn_out", o_sb, w_o, h2, row(p["norm_mlp_g"], 1))
    w1_1, w2_1 = weight("w1_1", hn_d), weight("w2_1", hn_d)
    a1, s1 = _mm_cols("mlp1_up", hn_d, w1_1, (), epi="relu2")
    dh, loss = _mm_rows_loss("mlp1_down", s1, w2_1, h3, target)

    def mlp_bwd(tag, layer, w1, w2, dh, h_in, hn, a, s):
        da = _mm_rows_t(f"{tag}_dact", dh, w2, (), out_dtype=BF16, epi="drelu2", extra=a)
        dw2 = _wgrad_rows(f"{tag}_dw2", s, dh, 512)
        dw1 = _wgrad_cols(f"{tag}_dw1", hn, da, 512)
        tok = emit(tag, [dw1, dw2])
        return _mm_cols_t_rms(f"{tag}_dhn", da, w1, h_in, row(p["norm_mlp_g"], layer) + tok, dh)

    dh, dg_mlp1 = mlp_bwd("mlp1", 1, w1_1, w2_1, dh, h3, hn_d, a1, s1)

    do_sb = _mm_rows_t("attn_dout", dh, w_o, ())
    dw_o = _wgrad_rows("attn_dwo", o_sb, dh, 128)
    dq, dk, dv = _sb_bwd(qh, kt, k2, vt, w_sb, do_sb, earlier_tab, bias_tab)
    dqkv, dqg, dkg = _qk_norm_bwd(qkv, dq, dk, dv, qg, kg, bd)
    dw_qkv = _wgrad_cols("qkv_dw", hn_c, dqkv, 384)
    tok = emit("attn", [dw_qkv, dw_o])
    dh, dg_mix1 = _mm_cols_t_rms("qkv_dhn", dqkv, w_qkv, h2, row(p["norm_mix_g"], 1) + tok, dh)

    dh, dg_mlp0 = mlp_bwd("mlp0", 0, w1_0, w2_0, dh, h1, hn_b, a0, s0)

    dw_out = _wgrad_rows("mix_dwout", cat, dh, 256)
    tok = emit("mix0_out", [dw_out])
    do_ret, dproj, dgn, dy, dlg, dlb, dcb = _mix_bwd_head(dh, w_out, o_ret, proj, gn_flat + tok, ycv,
                                                          p["ln_g"], p["ln_b"])
    dproj = _ret_bwd(dproj, proj, states, do_ret, tables)
    dproj, dug, dcw = _conv_bwd_taps(dproj, dy, hdn, proj, p["conv_w"])
    dproj = lax.dynamic_update_slice(dproj, dug, (0, 4096))
    dw_in = _wgrad_cols("proj_dw", hn_a, dproj, 640)
    tok = emit("mix0", [dw_in])
    dh, dg_mix0 = _mm_cols_t_rms("proj_dhn", dproj, w_in, h0, row(p["norm_mix_g"], 0) + tok, dh)

    rid = lax.broadcasted_iota(jnp.int32, (16, 1), 0)
    loss_row = jnp.broadcast_to(loss[0:1, 0:1], (1, D_MODEL))
    vecs = sum(jnp.where(rid == k, v, 0.0)
               for k, v in enumerate((dg_mix0, dg_mix1, dg_mlp0, dg_mlp1, dcb, dlg, dlb, dqg, dkg, loss_row)))
    small = jnp.concatenate([vecs, dh[PAD_FRONT:TOK0], dcw, jnp.where(rid[:8] == 0, dgn, 0.0)], axis=0)
    return dh[TOK0:], small


_SMALL_NAMES = ("meta", "norm_mix_g", "norm_mlp_g", "even_ret_gn_g", "even_conv_w", "even_conv_b",
                "even_conv_ln_g", "even_conv_ln_b", "odd_q_norm_g", "odd_k_norm_g")
_BIG_NAMES = ("even_w_in", "even_w_out", "odd_w_qkv", "odd_w_o", "mlp_w1", "mlp_w2")
_ORDER = ("meta", "norm_mix_g", "norm_mlp_g", "even_w_in", "even_ret_gn_g", "even_conv_w", "even_conv_b",
          "even_conv_ln_g", "even_conv_ln_b", "even_w_out", "odd_w_qkv", "odd_q_norm_g", "odd_k_norm_g",
          "odd_w_o", "mlp_w1", "mlp_w2")


def _pack128(a):
    flat = a.reshape(-1)
    n = flat.shape[0]
    rows = -(-n // 128)
    rows8 = -(-rows // 8) * 8
    return jnp.pad(flat, (0, rows8 * 128 - n)).reshape(rows8, 128)


def kernel(x, meta, norm_mix_g, norm_mlp_g, even_w_in, even_ret_gn_g, even_conv_w, even_conv_b, even_conv_ln_g, even_conv_ln_b, even_w_out, odd_w_qkv, odd_q_norm_g, odd_k_norm_g, odd_w_o, mlp_w1, mlp_w2, loss_target, m_meta, m_norm_mix_g, m_norm_mlp_g, m_even_w_in, m_even_ret_gn_g, m_even_conv_w, m_even_conv_b, m_even_conv_ln_g, m_even_conv_ln_b, m_even_w_out, m_odd_w_qkv, m_odd_q_norm_g, m_odd_k_norm_g, m_odd_w_o, m_mlp_w1, m_mlp_w2, v_meta, v_norm_mix_g, v_norm_mlp_g, v_even_w_in, v_even_ret_gn_g, v_even_conv_w, v_even_conv_b, v_even_conv_ln_g, v_even_conv_ln_b, v_even_w_out, v_odd_w_qkv, v_odd_q_norm_g, v_odd_k_norm_g, v_odd_w_o, v_mlp_w1, v_mlp_w2):
    w = dict(meta=meta, norm_mix_g=norm_mix_g, norm_mlp_g=norm_mlp_g, even_w_in=even_w_in,
             even_ret_gn_g=even_ret_gn_g, even_conv_w=even_conv_w, even_conv_b=even_conv_b,
             even_conv_ln_g=even_conv_ln_g, even_conv_ln_b=even_conv_ln_b, even_w_out=even_w_out,
             odd_w_qkv=odd_w_qkv, odd_q_norm_g=odd_q_norm_g, odd_k_norm_g=odd_k_norm_g, odd_w_o=odd_w_o,
             mlp_w1=mlp_w1, mlp_w2=mlp_w2)
    mom = dict(meta=m_meta, norm_mix_g=m_norm_mix_g, norm_mlp_g=m_norm_mlp_g, even_w_in=m_even_w_in,
               even_ret_gn_g=m_even_ret_gn_g, even_conv_w=m_even_conv_w, even_conv_b=m_even_conv_b,
               even_conv_ln_g=m_even_conv_ln_g, even_conv_ln_b=m_even_conv_ln_b, even_w_out=m_even_w_out,
               odd_w_qkv=m_odd_w_qkv, odd_q_norm_g=m_odd_q_norm_g, odd_k_norm_g=m_odd_k_norm_g, odd_w_o=m_odd_w_o,
               mlp_w1=m_mlp_w1, mlp_w2=m_mlp_w2)
    var = dict(meta=v_meta, norm_mix_g=v_norm_mix_g, norm_mlp_g=v_norm_mlp_g, even_w_in=v_even_w_in,
               even_ret_gn_g=v_even_ret_gn_g, even_conv_w=v_even_conv_w, even_conv_b=v_even_conv_b,
               even_conv_ln_g=v_even_conv_ln_g, even_conv_ln_b=v_even_conv_ln_b, even_w_out=v_even_w_out,
               odd_w_qkv=v_odd_w_qkv, odd_q_norm_g=v_odd_q_norm_g, odd_k_norm_g=v_odd_k_norm_g, odd_w_o=v_odd_w_o,
               mlp_w1=v_mlp_w1, mlp_w2=v_mlp_w2)
    me = 4 * lax.axis_index("x") + 2 * lax.axis_index("y") + lax.axis_index("c")

    small_in = jnp.concatenate([meta, jnp.pad(even_conv_w[0], ((0, 1), (0, 0))),
                                jnp.pad(even_ret_gn_g[0], ((0, 4), (0, 96)))], axis=0)
    b16 = lambda a: a.astype(BF16)
    later_src = dict(w_out=b16(even_w_out[0]), w1_0=b16(mlp_w1[0]), w2_0=b16(mlp_w2[0]),
                     w_qkv=b16(odd_w_qkv[0]), w_o=b16(odd_w_o[0]), w1_1=b16(mlp_w1[1]), w2_1=b16(mlp_w2[1]))
    landed = _gather_first([b16(even_w_in[0]), small_in], list(later_src.values()))
    g_in, g_small = landed[0], landed[1]
    own_slot = dict(zip(later_src, landed[2:]))
    groups = (("gather_l0", ("w_out", "w1_0", "w2_0")), ("gather_attn", ("w_qkv", "w_o")),
              ("gather_l1", ("w1_1", "w2_1")))
    pending = {}
    gather_tok = jnp.zeros((), F32)
    for gname, names in groups:
        ex = _exchange_start(gname, [later_src[n] for n in names], [own_slot[n] for n in names], "gather")
        gather_tok = gather_tok + ex["token"]
        for n in names:
            pending[n] = (ex, names)
    arrived = dict(w_in=g_in)

    def weight(name, after):
        if name not in arrived:
            ex, names = pending[name]
            arrived.update(zip(names, _exchange_wait(ex, after)[1]))
        return arrived[name]

    cols = lambda a: jnp.transpose(a, (1, 0, 2)).reshape(a.shape[1], -1)
    p = dict(norm_mix_g=norm_mix_g + gather_tok, norm_mlp_g=norm_mlp_g, conv_b=even_conv_b, ln_g=even_conv_ln_g,
             ln_b=even_conv_ln_b, qn_g=odd_q_norm_g, kn_g=odd_k_norm_g,
             gn_g=cols(g_small[:, 48:52, :32]),
             conv_w=jnp.pad(cols(g_small[:, 16:47]), ((0, 1), (0, 0))))
    meta_full = cols(g_small[:, 0:16])

    scatters = {}

    def emit(tag, grads):
        scatters[tag] = _scatter_start("scatter_" + tag, grads)
        return scatters[tag]["token"]

    h0 = jnp.concatenate([jnp.zeros((PAD_FRONT, D_MODEL), F32), meta_full, x[0]], axis=0)
    target = jnp.concatenate([jnp.zeros((TOK0, D_MODEL), F32), loss_target[0]], axis=0)
    grad_x, small_part = _local_step(h0, target, p, weight, emit)

    out = {}
    got = {}

    def update(names, terms, after):
        for tag in {t for name in names for t, _ in terms[name]} - set(got):
            got[tag] = _exchange_wait(scatters[tag], after)
        for name in names:
            owns, recvs = zip(*[(got[t][0][j], got[t][1][j]) for t, j in terms[name]])
            out[name] = _adamw("adamw_" + name, w[name], list(owns), list(recvs), mom[name], var[name], me)

    terms = dict(even_w_in=[("mix0", 0)], even_w_out=[("mix0_out", 0)], odd_w_qkv=[("attn", 0)], odd_w_o=[("attn", 1)],
                 mlp_w1=[("mlp0", 0), ("mlp1", 0)], mlp_w2=[("mlp0", 1), ("mlp1", 1)])
    small_ex = _exchange_start("small", [small_part], [lax.empty((N_DEV,) + small_part.shape, F32)], "gather")
    update(("mlp_w1", "mlp_w2", "odd_w_qkv", "odd_w_o", "even_w_out"), terms, grad_x)
    update(("even_w_in",), terms, out["even_w_out"][1])
    (own_part,), (slots,) = _exchange_wait(small_ex, out["even_w_in"][1])
    tot = _sum_small(lax.dynamic_update_slice(slots, own_part[None], (me, 0, 0)))
    loss = tot[ROW_LOSS, 0]

    shard_cols = lambda a, width: lax.dynamic_slice_in_dim(a, me * width, width, axis=1)
    one = lambda r: tot[r:r + 1]
    small_g = dict(
        norm_mix_g=tot[ROW_MIX:ROW_MIX + 2], norm_mlp_g=tot[ROW_MLP:ROW_MLP + 2],
        even_conv_b=one(ROW_CB), even_conv_ln_g=one(ROW_LG), even_conv_ln_b=one(ROW_LB),
        odd_q_norm_g=one(ROW_QN)[:, :64], odd_k_norm_g=one(ROW_KN)[:, :64],
        meta=shard_cols(tot[ROW_META:ROW_META + N_META], 128),
        even_conv_w=shard_cols(tot[ROW_CW:ROW_CW + CONV_WIDTH], 128)[None],
        even_ret_gn_g=shard_cols(tot[ROW_GN].reshape(4, 256), 32)[None])
    packs = {n: (_pack128(w[n]), _pack128(small_g[n]), _pack128(mom[n]), _pack128(var[n])) for n in _SMALL_NAMES}
    cat4 = [jnp.concatenate([packs[n][i] for n in _SMALL_NAMES], axis=0) for i in range(4)]
    d_s, m_s, v_s = _adamw_small(*cat4)
    r0 = 0
    for n in _SMALL_NAMES:
        rows = packs[n][0].shape[0]
        size = w[n].size
        take = lambda a: a[r0:r0 + rows].reshape(-1)[:size].reshape(w[n].shape)
        out[n] = (small_g[n].reshape(w[n].shape), take(d_s), take(m_s), take(v_s))
        r0 += rows

    res = [loss, grad_x[None]]
    for i in range(4):
        res.extend(out[n][i] for n in _ORDER)
    return tuple(res)
```

```python
import functools

import numpy as np
import jax
import jax.numpy as jnp
from jax import lax
from jax.experimental import pallas as pl
from jax.experimental.pallas import tpu as pltpu

F32 = jnp.float32
BF16 = jnp.bfloat16

D_MODEL = 1024
N_META = 16
CHUNK = 128
PAD_FRONT = 112
TOK0 = PAD_FRONT + N_META
EPS = 1e-6
N_DEV = 8
RET_HEADS = 4
RET_DECAY_OFFSET = 5.0
ROPE_BASE = 10000.0
CONV_WIDTH = 31
HALO = 32
SB_SCALE = 64 ** -0.5
RET_SCALE = 128 ** -0.5
ADAM_LR, ADAM_B1, ADAM_B2, ADAM_EPS, ADAM_WD, ADAM_STEP = 0.001, 0.9, 0.999, 1e-08, 0.01, 10
VMEM_LIMIT = 56 * 1024 * 1024
MESH = pl.DeviceIdType.MESH


def _pcall(body, **kw):
    return pl.pallas_call(body, **kw)


def _params(**kw):
    return pltpu.CompilerParams(vmem_limit_bytes=VMEM_LIMIT, **kw)


def _tile(n, cands):
    for c in cands:
        if n % c == 0:
            return c
    raise ValueError(f"no tile for {n} in {cands}")


def _sigmoid(x):
    return 1.0 / (1.0 + jnp.exp(-x))


_DIMS = {
    "nn": (((1,), (0,)), ((), ())),
    "nt": (((1,), (1,)), ((), ())),
    "tn": (((0,), (0,)), ((), ())),
}


def _matmul(name, a, b, *, grid, a_spec, b_spec, o_spec, out_shape, contract, acc_shape,
            epi="plain", extra=None, extra_spec=None):
    nk = grid[2]
    dims = _DIMS[contract]
    n_in = 3 if extra is not None else 2
    n_out = 2 if epi == "relu2" else 1

    def body(*refs):
        a_ref, b_ref = refs[0], refs[1]
        e_ref = refs[2] if extra is not None else None
        outs = refs[n_in:n_in + n_out]
        acc = refs[-1]
        k = pl.program_id(2)
        part = lax.dot_general(a_ref[...].astype(BF16), b_ref[...].astype(BF16), dims, preferred_element_type=F32)
        if nk > 1:
            @pl.when(k == 0)
            def _():
                acc[...] = jnp.zeros_like(acc)

            acc[...] += part

        @pl.when(k == nk - 1)
        def _():
            r = acc[...] if nk > 1 else part
            if epi == "plain":
                outs[0][...] = r.astype(outs[0].dtype)
            elif epi == "residual":
                outs[0][...] = (r + e_ref[...]).astype(outs[0].dtype)
            elif epi == "relu2":
                outs[0][...] = r
                rr = jnp.maximum(r, 0.0)
                outs[1][...] = (rr * rr).astype(BF16)
            elif epi == "drelu2":
                outs[0][...] = (r * (2.0 * jnp.maximum(e_ref[...], 0.0))).astype(outs[0].dtype)

    in_specs = [a_spec, b_spec] + ([extra_spec] if extra is not None else [])
    args = (a, b) + ((extra,) if extra is not None else ())
    if n_out == 2:
        out_specs = [o_spec, o_spec]
    else:
        out_specs = o_spec
    return _pcall(body, name=name, grid=grid, in_specs=in_specs, out_specs=out_specs,
                  out_shape=out_shape, scratch_shapes=[pltpu.VMEM(acc_shape, F32)],
                  compiler_params=_params(dimension_semantics=("parallel", "parallel", "arbitrary")))(*args)


def _tm(t):
    return _tile(t, (1408, 768, 384, 128))


def _mm_cols(name, a, wb, lead, out_dtype=F32, epi="plain"):
    t, kdim = a.shape
    n = wb.shape[-1]
    tm, tk = _tm(t), _tile(kdim, (1024, 512))
    nl = len(lead)
    b_spec = pl.BlockSpec((None,) * (1 + nl) + (tk, n), lambda i, j, k: (j,) + lead + (k, 0))
    o_spec = pl.BlockSpec((tm, n), lambda i, j, k: (i, j))
    if epi == "relu2":
        out_shape = [jax.ShapeDtypeStruct((t, N_DEV * n), F32), jax.ShapeDtypeStruct((t, N_DEV * n), BF16)]
    else:
        out_shape = jax.ShapeDtypeStruct((t, N_DEV * n), out_dtype)
    return _matmul(name, a, wb, grid=(t // tm, N_DEV, kdim // tk),
                   a_spec=pl.BlockSpec((tm, tk), lambda i, j, k: (i, k)), b_spec=b_spec, o_spec=o_spec,
                   out_shape=out_shape, contract="nn", acc_shape=(tm, n), epi=epi)


def _tm_deep(t, kdim):
    return _tm(t) if kdim <= 2048 else _tile(t, (704, 384, 128))


def _mm_cols_t_rms(name, a, wb, h, g, dres):
    t = a.shape[0]
    nb, kdim, n = wb.shape
    tm = _tile(t, (704, 384, 128))

    def body(a_ref, b_ref, h_ref, g_ref, r_ref, o_ref, dg_ref):
        @pl.when(pl.program_id(0) == 0)
        def _():
            dg_ref[...] = jnp.zeros_like(dg_ref)

        d = _dot(a_ref[:, 0:n].astype(BF16), b_ref[0], "nt")
        for j in range(1, nb):
            d = d + _dot(a_ref[:, j * n:(j + 1) * n].astype(BF16), b_ref[j], "nt")
        x = h_ref[...]
        rs = lax.rsqrt(jnp.mean(x * x, axis=-1, keepdims=True) + EPS)
        u = d * g_ref[...]
        m = jnp.mean(u * x, axis=-1, keepdims=True)
        o_ref[...] = r_ref[...] + rs * u - x * (rs * rs * rs * m)
        dg_ref[...] += jnp.sum(d * x * rs, axis=0, keepdims=True)

    row = pl.BlockSpec((tm, kdim), lambda i: (i, 0))
    vec = pl.BlockSpec((1, kdim), lambda i: (0, 0))
    return _pcall(body, name=name, grid=(t // tm,),
                  in_specs=[pl.BlockSpec((tm, nb * n), lambda i: (i, 0)),
                            pl.BlockSpec((nb, kdim, n), lambda i: (0, 0, 0)), row, vec, row],
                  out_specs=[row, vec],
                  out_shape=[jax.ShapeDtypeStruct((t, kdim), F32), jax.ShapeDtypeStruct((1, kdim), F32)],
                  compiler_params=_params(dimension_semantics=("arbitrary",)))(a, wb, h, g, dres)


def _mm_rows_t(name, a, wb, lead, out_dtype=F32, epi="plain", extra=None):
    t, n = a.shape
    r = wb.shape[-2]
    tm, tk = _tm(t), _tile(n, (1024,))
    nl = len(lead)
    b_spec = pl.BlockSpec((None,) * (1 + nl) + (r, tk), lambda i, j, k: (j,) + lead + (0, k))
    o_spec = pl.BlockSpec((tm, r), lambda i, j, k: (i, j))
    return _matmul(name, a, wb, grid=(t // tm, N_DEV, n // tk),
                   a_spec=pl.BlockSpec((tm, tk), lambda i, j, k: (i, k)), b_spec=b_spec, o_spec=o_spec,
                   out_shape=jax.ShapeDtypeStruct((t, N_DEV * r), out_dtype), contract="nt",
                   acc_shape=(tm, r), epi=epi, extra=extra, extra_spec=o_spec if extra is not None else None)


def _mm_rows_loss(name, a, wb, residual, target):
    t = a.shape[0]
    nb, r, n = wb.shape
    tm, tn = _tm_deep(t, nb * r), _tile(n, (512,))

    def body(a_ref, b_ref, r_ref, t_ref, d_ref, l_ref):
        i = pl.program_id(0)

        @pl.when((i == 0) & (pl.program_id(1) == 0))
        def _():
            l_ref[...] = jnp.zeros_like(l_ref)

        y = r_ref[...] + _dot(a_ref[...].astype(BF16), b_ref[...].reshape(nb * r, tn))
        diff = jnp.where(_row_ids(i, tm) >= TOK0, y - t_ref[...], 0.0)
        d_ref[...] = diff * (1.0 / D_MODEL)
        l_ref[...] += jnp.sum(diff * diff) * (0.5 / D_MODEL)

    o_spec = pl.BlockSpec((tm, tn), lambda i, j: (i, j))
    return _pcall(body, name=name, grid=(t // tm, n // tn),
                  in_specs=[pl.BlockSpec((tm, nb * r), lambda i, j: (i, 0)),
                            pl.BlockSpec((nb, r, tn), lambda i, j: (0, 0, j)), o_spec, o_spec],
                  out_specs=[o_spec, pl.BlockSpec((8, 128), lambda i, j: (0, 0))],
                  out_shape=[jax.ShapeDtypeStruct((t, n), F32), jax.ShapeDtypeStruct((8, 128), F32)],
                  compiler_params=_params(dimension_semantics=("arbitrary", "arbitrary")))(a, wb, residual, target)


def _mm_rows_norm(name, a, wb, residual, g):
    t = a.shape[0]
    nb, r, n = wb.shape
    tm = _tile(t, (704, 384, 128))

    def body(a_ref, b_ref, r_ref, g_ref, h_ref, hn_ref):
        h = r_ref[...] + _dot(a_ref[...].astype(BF16), b_ref[...].reshape(nb * r, n))
        h_ref[...] = h
        hn_ref[...] = (h * lax.rsqrt(jnp.mean(h * h, axis=-1, keepdims=True) + EPS) * g_ref[...]).astype(BF16)

    row = pl.BlockSpec((tm, n), lambda i: (i, 0))
    return _pcall(body, name=name, grid=(t // tm,),
                  in_specs=[pl.BlockSpec((tm, nb * r), lambda i: (i, 0)), pl.BlockSpec((nb, r, n), lambda i: (0, 0, 0)),
                            row, pl.BlockSpec((1, n), lambda i: (0, 0))],
                  out_specs=[row, row],
                  out_shape=[jax.ShapeDtypeStruct((t, n), F32), jax.ShapeDtypeStruct((t, n), BF16)],
                  compiler_params=_params(dimension_semantics=("parallel",)))(a, wb, residual, g)


def _wgrad_cols(name, x, dy, n):
    t, kdim = x.shape
    tk = _tm(t)
    return _matmul(name, x, dy, grid=(1, N_DEV, t // tk),
                   a_spec=pl.BlockSpec((tk, kdim), lambda i, j, k: (k, 0)),
                   b_spec=pl.BlockSpec((tk, n), lambda i, j, k: (k, j)),
                   o_spec=pl.BlockSpec((None, kdim, n), lambda i, j, k: (j, 0, 0)),
                   out_shape=jax.ShapeDtypeStruct((N_DEV, kdim, n), BF16), contract="tn", acc_shape=(kdim, n))


def _wgrad_rows(name, x, dy, r):
    t = x.shape[0]
    n = dy.shape[1]
    tk, tn = _tm(t), _tile(n, (512,))
    tm = min(N_DEV * r, 1024)
    out = _matmul(name, x, dy, grid=(N_DEV * r // tm, n // tn, t // tk),
                  a_spec=pl.BlockSpec((tk, tm), lambda i, j, k: (k, i)),
                  b_spec=pl.BlockSpec((tk, tn), lambda i, j, k: (k, j)),
                  o_spec=pl.BlockSpec((tm, tn), lambda i, j, k: (i, j)),
                  out_shape=jax.ShapeDtypeStruct((N_DEV * r, n), BF16), contract="tn", acc_shape=(tm, tn))
    return out.reshape(N_DEV, r, n)


def _rows(t):
    return _tile(t, (384, 128))


def _rms_fwd(name, h, g):
    t = h.shape[0]
    tr = _rows(t)

    def body(h_ref, g_ref, o_ref):
        x = h_ref[...]
        r = lax.rsqrt(jnp.mean(x * x, axis=-1, keepdims=True) + EPS)
        o_ref[...] = (x * r * g_ref[...]).astype(BF16)

    row = pl.BlockSpec((tr, D_MODEL), lambda i: (i, 0))
    vec = pl.BlockSpec((1, D_MODEL), lambda i: (0, 0))
    return _pcall(body, name=name, grid=(t // tr,), in_specs=[row, vec], out_specs=row,
                  out_shape=jax.ShapeDtypeStruct((t, D_MODEL), BF16))(h, g)


def _ret_tables(t):
    hh = np.arange(RET_HEADS, dtype=np.float64)
    log_g = np.log1p(-np.exp2(-RET_DECAY_OFFSET - hh))
    idx = np.arange(CHUNK, dtype=np.float64)
    diff = idx[:, None] - idx[None, :]
    dmat = np.where(diff[None] >= 0, np.exp(np.maximum(diff, 0.0)[None] * log_g[:, None, None]), 0.0)
    qdec = np.exp((idx + 1.0)[None, :, None] * log_g[:, None, None]) * np.ones((1, 1, CHUNK))
    kdec = np.exp((CHUNK - 1 - idx)[None, :, None] * log_g[:, None, None]) * np.ones((1, 1, CHUNK))
    half = CHUNK // 2
    inv_freq = (ROPE_BASE ** (-np.arange(half, dtype=np.float32) / half)).astype(np.float32)
    ang = (np.arange(t, dtype=np.float32)[:, None] * inv_freq[None, :]).astype(np.float32).astype(np.float64)
    cos2 = np.concatenate([np.cos(ang), np.cos(ang)], axis=1)
    sin2 = np.concatenate([-np.sin(ang), np.sin(ang)], axis=1)
    return tuple(jnp.asarray(v, F32) for v in (dmat, qdec, kdec, cos2, sin2))


def _rot(x, c, s):
    return x * c + pltpu.roll(x, CHUNK // 2, 1) * s


def _unrot(dx, c, s):
    return dx * c + pltpu.roll(dx * s, CHUNK // 2, 1)


def _dot(a, b, contract="nn"):
    return lax.dot_general(a, b, _DIMS[contract], preferred_element_type=F32)


def _ret_fwd(proj, gn_g, tables):
    t = proj.shape[0]
    nch = t // CHUNK
    dmat, qdec, kdec, cos2, sin2 = tables

    def body(qk_ref, v_ref, g_ref, w_ref, c_ref, s_ref, dm_ref, qd_ref, kd_ref, o_ref, st_ref, cat_ref, state):
        @pl.when(pl.program_id(0) == 0)
        def _():
            state[...] = jnp.zeros_like(state)

        c, s = c_ref[...], s_ref[...]
        for h in range(RET_HEADS):
            q = _rot(qk_ref[:, 128 * h:128 * (h + 1)], c, s)
            k = _rot(qk_ref[:, 512 + 128 * h:512 + 128 * (h + 1)], c, s) * RET_SCALE
            vb = v_ref[:, 256 * h:256 * (h + 1)].astype(BF16)
            st = state[h]
            st_ref[h] = st
            sc = _dot(q.astype(BF16), k.astype(BF16), "nt") * dm_ref[h]
            o = _dot(sc.astype(BF16), vb)
            o += _dot((q * qd_ref[h]).astype(BF16), st.astype(BF16))
            sl = slice(256 * h, 256 * (h + 1))
            o_ref[:, sl] = o
            kv = _dot((k * kd_ref[h]).astype(BF16), vb, "tn")
            state[h] = qd_ref[h, CHUNK - 1:CHUNK, 0:1] * st + kv
            mu = jnp.mean(o, axis=-1, keepdims=True)
            oc = o - mu
            rstd = lax.rsqrt(jnp.mean(oc * oc, axis=-1, keepdims=True) + EPS)
            g = g_ref[:, sl]
            cat_ref[:, sl] = (g * _sigmoid(g) * (oc * rstd * w_ref[:, sl])).astype(BF16)

    tab = pl.BlockSpec((RET_HEADS, CHUNK, CHUNK), lambda n: (0, 0, 0))
    pos = pl.BlockSpec((CHUNK, CHUNK), lambda n: (n, 0))
    row = pl.BlockSpec((CHUNK, 1024), lambda n: (n, 0))
    return _pcall(
        body, name="ret_fwd", grid=(nch,),
        in_specs=[row, pl.BlockSpec((CHUNK, 1024), lambda n: (n, 1)), pl.BlockSpec((CHUNK, 1024), lambda n: (n, 2)),
                  pl.BlockSpec((1, 1024), lambda n: (0, 0)), pos, pos, tab, tab, tab],
        out_specs=[row, pl.BlockSpec((RET_HEADS, None, 128, 256), lambda n: (0, n, 0, 0)), row],
        out_shape=[jax.ShapeDtypeStruct((t, 1024), F32), jax.ShapeDtypeStruct((RET_HEADS, nch, 128, 256), F32),
                   jax.ShapeDtypeStruct((t, 2048), BF16)],
        scratch_shapes=[pltpu.VMEM((RET_HEADS, 128, 256), F32)],
        compiler_params=_params(dimension_semantics=("arbitrary",)))(
            proj, proj, proj, gn_g, cos2, sin2, dmat, qdec, kdec)


def _ret_bwd(dproj, proj, states, do, tables):
    t = proj.shape[0]
    nch = t // CHUNK
    dmat, qdec, kdec, cos2, sin2 = tables

    def body(dp_in, qk_ref, v_ref, do_ref, st_ref, c_ref, s_ref, dm_ref, qd_ref, kd_ref, dp_ref, rst):
        del dp_in
        @pl.when(pl.program_id(0) == 0)
        def _():
            rst[...] = jnp.zeros_like(rst)

        c, s = c_ref[...], s_ref[...]
        for h in range(RET_HEADS):
            q = _rot(qk_ref[:, 128 * h:128 * (h + 1)], c, s)
            k = _rot(qk_ref[:, 512 + 128 * h:512 + 128 * (h + 1)], c, s) * RET_SCALE
            qb, kb = q.astype(BF16), k.astype(BF16)
            vb = v_ref[:, 256 * h:256 * (h + 1)].astype(BF16)
            dob = do_ref[:, 256 * h:256 * (h + 1)].astype(BF16)
            pb = st_ref[h].astype(BF16)
            r = rst[h]
            rb = r.astype(BF16)
            dm, qd, kd = dm_ref[h], qd_ref[h], kd_ref[h]
            sb = (_dot(qb, kb, "nt") * dm).astype(BF16)
            dsb = (_dot(dob, vb, "nt") * dm).astype(BF16)
            dq = _dot(dsb, kb) + _dot(dob, pb, "nt") * qd
            dk = _dot(dsb, qb, "tn") + _dot(vb, rb, "nt") * kd
            dv = _dot(sb, dob, "tn") + _dot((k * kd).astype(BF16), rb)
            rst[h] = _dot((q * qd).astype(BF16), dob, "tn") + qd[CHUNK - 1:CHUNK, 0:1] * r
            dp_ref[:, 128 * h:128 * (h + 1)] = _unrot(dq, c, s).astype(BF16)
            dp_ref[:, 512 + 128 * h:512 + 128 * (h + 1)] = (_unrot(dk, c, s) * RET_SCALE).astype(BF16)
            dp_ref[:, 1024 + 256 * h:1024 + 256 * (h + 1)] = dv.astype(BF16)

    rev = lambda n: nch - 1 - n
    tab = pl.BlockSpec((RET_HEADS, CHUNK, CHUNK), lambda n: (0, 0, 0))
    pos = pl.BlockSpec((CHUNK, CHUNK), lambda n: (rev(n), 0))
    row = pl.BlockSpec((CHUNK, 1024), lambda n: (rev(n), 0))
    return _pcall(
        body, name="ret_bwd", grid=(nch,),
        in_specs=[pl.BlockSpec(memory_space=pl.ANY), row, pl.BlockSpec((CHUNK, 1024), lambda n: (rev(n), 1)), row,
                  pl.BlockSpec((RET_HEADS, None, 128, 256), lambda n: (0, rev(n), 0, 0)),
                  pos, pos, tab, tab, tab],
        out_specs=pl.BlockSpec((CHUNK, 2048), lambda n: (rev(n), 0)),
        out_shape=jax.ShapeDtypeStruct((t, 5120), BF16),
        scratch_shapes=[pltpu.VMEM((RET_HEADS, 128, 256), F32)], input_output_aliases={0: 0},
        compiler_params=_params(dimension_semantics=("arbitrary",)))(
            dproj, proj, proj, do, states, cos2, sin2, dmat, qdec, kdec)


def _row_ids(i, tr):
    return i * tr + lax.broadcasted_iota(jnp.int32, (tr, 1), 0)


SH_ROWS = HALO - 8
CONV_VPU_TAPS = 21


def _shifted_copies(xs, sh, tr):
    for b in range(1, 8):
        sh[b - 1] = xs[pl.ds(b, tr + SH_ROWS), :]


def _shifted(xs, sh, off, tr, lanes=slice(None)):
    a, b = divmod(off, 8)
    return xs[pl.ds(8 * a, tr), lanes] if b == 0 else sh[b - 1, pl.ds(8 * a, tr), lanes]


def _taps_mxu(xs, sh, w_ref, offs, tr, first=0):
    sub = lax.broadcasted_iota(jnp.int32, (256, 128), 0)
    eye = (sub & 127) == lax.broadcasted_iota(jnp.int32, (256, 128), 1)
    outs = []
    for c in range(8):
        lanes = slice(128 * c, 128 * (c + 1))
        acc = None
        for w in range(first, len(offs), 2):
            wb = min(w + 1, len(offs) - 1)
            w_hi = w_ref[w:w + 1, lanes]
            w_lo = w_ref[wb:wb + 1, lanes] if wb > w else jnp.zeros((1, 128), F32)
            dmat = jnp.where(eye, jnp.where(sub < 128, w_hi, w_lo), 0.0).astype(BF16)
            lhs = jnp.concatenate([_shifted(xs, sh, offs[w], tr, lanes).astype(BF16),
                                   _shifted(xs, sh, offs[wb], tr, lanes).astype(BF16)], axis=1)
            d = _dot(lhs, dmat)
            acc = d if acc is None else acc + d
        outs.append(acc)
    return jnp.concatenate(outs, axis=1)


def _conv_fwd(cat, proj, conv_w, conv_b, ln_g, ln_b):
    t = proj.shape[0]
    tr = _rows(t)
    hb = tr // HALO

    def body(cat_in, ua_ref, ug_ref, pa_ref, pg_ref, w_ref, b_ref, lg_ref, lb_ref, c_ref, hd_ref, y_ref, xs, sh):
        del cat_in
        i = pl.program_id(0)
        hdn = ua_ref[...] * _sigmoid(ug_ref[...])
        hd_ref[...] = hdn
        prev = pa_ref[...] * _sigmoid(pg_ref[...])
        xs[0:HALO, :] = jnp.where(i > 0, prev, 0.0)
        xs[HALO:HALO + tr, :] = hdn
        _shifted_copies(xs, sh, tr)
        offs = [HALO - (CONV_WIDTH - 1) + w for w in range(CONV_WIDTH)]
        acc = _taps_mxu(xs, sh, w_ref, offs, tr, first=CONV_VPU_TAPS) + b_ref[...]
        for w in range(CONV_VPU_TAPS):
            acc += w_ref[w:w + 1, :] * _shifted(xs, sh, offs[w], tr)
        y_ref[...] = acc
        mu = jnp.mean(acc, axis=-1, keepdims=True)
        yc = acc - mu
        rstd = lax.rsqrt(jnp.mean(yc * yc, axis=-1, keepdims=True) + EPS)
        yn = yc * rstd * lg_ref[...] + lb_ref[...]
        c = yn * _sigmoid(yn)
        c_ref[...] = jnp.where(_row_ids(i, tr) >= PAD_FRONT, c, 0.0).astype(BF16)

    row = pl.BlockSpec((tr, 1024), lambda i: (i, 0))
    vec = pl.BlockSpec((1, 1024), lambda i: (0, 0))
    halo = lambda col: pl.BlockSpec((HALO, 1024), lambda i: (jnp.maximum(i * hb - 1, 0), col))
    return _pcall(body, name="conv_fwd", grid=(t // tr,),
                  in_specs=[pl.BlockSpec(memory_space=pl.ANY),
                            pl.BlockSpec((tr, 1024), lambda i: (i, 3)), pl.BlockSpec((tr, 1024), lambda i: (i, 4)),
                            halo(3), halo(4), pl.BlockSpec((32, 1024), lambda i: (0, 0)), vec, vec, vec],
                  out_specs=[pl.BlockSpec((tr, 1024), lambda i: (i, 1)), row, row],
                  out_shape=[jax.ShapeDtypeStruct((t, 2048), BF16), jax.ShapeDtypeStruct((t, 1024), F32),
                             jax.ShapeDtypeStruct((t, 1024), F32)],
                  scratch_shapes=[pltpu.VMEM((tr + HALO, 1024), F32), pltpu.VMEM((7, tr + SH_ROWS, 1024), F32)],
                  input_output_aliases={0: 0}, compiler_params=_params())(
                      cat, proj, proj, proj, proj, conv_w, conv_b, ln_g, ln_b)


def _mix_bwd_head(dh, w_out, o, proj, gn_g, y, ln_g, ln_b):
    t = dh.shape[0]
    tr = _rows(t)
    nb, r, n = w_out.shape

    def body(dh_ref, b_ref, o_ref, g_ref, w_ref, y_ref, lg_ref, lb_ref,
             do_ref, dp_ref, dw_ref, dy_ref, dlg_ref, dlb_ref, dcb_ref):
        i = pl.program_id(0)

        @pl.when(i == 0)
        def _():
            for ref in (dw_ref, dlg_ref, dlb_ref, dcb_ref):
                ref[...] = jnp.zeros_like(ref)

        dcat = _dot(dh_ref[...].astype(BF16), b_ref[...].reshape(nb * r, n), "nt")
        for h in range(RET_HEADS):
            sl = slice(256 * h, 256 * (h + 1))
            x = o_ref[:, sl]
            mu = jnp.mean(x, axis=-1, keepdims=True)
            xc = x - mu
            rstd = lax.rsqrt(jnp.mean(xc * xc, axis=-1, keepdims=True) + EPS)
            xh = xc * rstd
            w = w_ref[:, sl]
            g = g_ref[:, sl]
            sg = _sigmoid(g)
            d = dcat[:, sl]
            don = d * (g * sg)
            dp_ref[:, sl] = (d * (xh * w) * (sg * (1.0 + g * (1.0 - sg)))).astype(BF16)
            dw_ref[:, sl] += jnp.sum(don * xh, axis=0, keepdims=True)
            dxh = don * w
            m1 = jnp.mean(dxh, axis=-1, keepdims=True)
            m2 = jnp.mean(dxh * xh, axis=-1, keepdims=True)
            do_ref[:, sl] = rstd * (dxh - m1 - xh * m2)
        yv = y_ref[...]
        mu = jnp.mean(yv, axis=-1, keepdims=True)
        yc = yv - mu
        rstd = lax.rsqrt(jnp.mean(yc * yc, axis=-1, keepdims=True) + EPS)
        xh = yc * rstd
        lg = lg_ref[...]
        yn = xh * lg + lb_ref[...]
        sg = _sigmoid(yn)
        dyn = jnp.where(_row_ids(i, tr) >= PAD_FRONT, dcat[:, 1024:] * (sg * (1.0 + yn * (1.0 - sg))), 0.0)
        dlg_ref[...] += jnp.sum(dyn * xh, axis=0, keepdims=True)
        dlb_ref[...] += jnp.sum(dyn, axis=0, keepdims=True)
        dxh = dyn * lg
        m1 = jnp.mean(dxh, axis=-1, keepdims=True)
        m2 = jnp.mean(dxh * xh, axis=-1, keepdims=True)
        dy = rstd * (dxh - m1 - xh * m2)
        dy_ref[...] = dy
        dcb_ref[...] += jnp.sum(dy, axis=0, keepdims=True)

    row = pl.BlockSpec((tr, 1024), lambda i: (i, 0))
    vec = pl.BlockSpec((1, 1024), lambda i: (0, 0))
    gate = pl.BlockSpec((tr, 1024), lambda i: (i, 2))
    vsh = jax.ShapeDtypeStruct((1, 1024), F32)
    fsh = jax.ShapeDtypeStruct((t, 1024), F32)
    return _pcall(body, name="mix_bwd_head", grid=(t // tr,),
                  in_specs=[row, pl.BlockSpec((nb, r, n), lambda i: (0, 0, 0)), row, gate, vec, row, vec, vec],
                  out_specs=[row, gate, vec, row, vec, vec, vec],
                  out_shape=[fsh, jax.ShapeDtypeStruct((t, 5120), BF16), vsh, fsh, vsh, vsh, vsh],
                  compiler_params=_params(dimension_semantics=("arbitrary",)))(
                      dh, w_out, o, proj, gn_g, y, ln_g, ln_b)


def _conv_bwd_taps(dproj, dy, hdn, proj, conv_w):
    t = dy.shape[0]
    tr = _rows(t)
    hb = tr // HALO
    nt = t // tr

    def body(dp_in, dy_ref, nx_ref, hd_ref, ph_ref, ua_ref, ug_ref, w_ref, da_ref, dg_ref, dw_ref, xs, sh):
        del dp_in
        i = pl.program_id(0)

        @pl.when(i == 0)
        def _():
            dw_ref[...] = jnp.zeros_like(dw_ref)

        dy = dy_ref[...]
        xs[0:tr, :] = dy
        xs[tr:tr + HALO, :] = jnp.where(i < nt - 1, nx_ref[...], 0.0)
        _shifted_copies(xs, sh, tr)
        dh = _taps_mxu(xs, sh, w_ref, [CONV_WIDTH - 1 - w for w in range(CONV_WIDTH)], tr)
        xs[0:HALO, :] = jnp.where(i > 0, ph_ref[...], 0.0)
        xs[HALO:HALO + tr, :] = hd_ref[...]
        _shifted_copies(xs, sh, tr)
        for w in range(CONV_WIDTH):
            dw_ref[w:w + 1, :] += jnp.sum(dy * _shifted(xs, sh, HALO - (CONV_WIDTH - 1) + w, tr), axis=0, keepdims=True)
        dh = jnp.where(_row_ids(i, tr) >= PAD_FRONT, dh, 0.0)
        sg = _sigmoid(ug_ref[...])
        da_ref[...] = (dh * sg).astype(BF16)
        dg_ref[...] = (dh * ua_ref[...] * sg * (1.0 - sg)).astype(BF16)

    row = pl.BlockSpec((tr, 1024), lambda i: (i, 0))
    return _pcall(body, name="conv_bwd_taps", grid=(nt,),
                  in_specs=[pl.BlockSpec(memory_space=pl.ANY),
                            row, pl.BlockSpec((HALO, 1024), lambda i: (jnp.minimum((i + 1) * hb, nt * hb - 1), 0)),
                            row, pl.BlockSpec((HALO, 1024), lambda i: (jnp.maximum(i * hb - 1, 0), 0)),
                            pl.BlockSpec((tr, 1024), lambda i: (i, 3)), pl.BlockSpec((tr, 1024), lambda i: (i, 4)),
                            pl.BlockSpec((32, 1024), lambda i: (0, 0))],
                  out_specs=[pl.BlockSpec((tr, 1024), lambda i: (i, 3)), row, pl.BlockSpec((32, 1024), lambda i: (0, 0))],
                  out_shape=[jax.ShapeDtypeStruct((t, 5120), BF16), jax.ShapeDtypeStruct((t, 1024), BF16),
                             jax.ShapeDtypeStruct((32, 1024), F32)],
                  scratch_shapes=[pltpu.VMEM((tr + HALO, 1024), F32), pltpu.VMEM((7, tr + SH_ROWS, 1024), F32)],
                  input_output_aliases={0: 0}, compiler_params=_params())(
                      dproj, dy, dy, hdn, hdn, proj, proj, conv_w)


NEG_BIG = -1e30


def _seg_tables(qb):
    j = np.arange(128)
    bd = (j[:, None] // 64 == j[None, :] // 64).astype(np.float32)
    ones = np.ones((128, 128), np.float32)
    later = np.concatenate([(j[:, None] >= j[None, :]).astype(np.float32), ones], axis=1)
    earlier = np.concatenate([(j[:, None] < j[None, :]).astype(np.float32), ones], axis=1)
    per = qb // CHUNK
    row = np.arange(qb)[:, None]
    pad = np.broadcast_to(j[None, :] < PAD_FRONT, (qb, 128))
    diag = [(g * CHUNK + j[None, :]) >= row for g in range(per)]
    masks = diag + [np.zeros((qb, 128), bool), pad, diag[0] | pad]
    bias = np.stack([np.where(m, NEG_BIG, 0.0) for m in masks]).astype(np.float32)
    dup = lambda m: np.concatenate([m, m], axis=0)
    return (jnp.asarray(bd, BF16), jnp.asarray(dup(later), BF16), jnp.asarray(dup(earlier), BF16),
            jnp.asarray(bias, F32))


def _split_dot(x, m):
    hi = x.astype(BF16)
    lo = (x - hi.astype(F32)).astype(BF16)
    return _dot(hi, m) + _dot(lo, m)


def _qk_norm_fwd(qkv, qg, kg, bd):
    t = qkv.shape[0]
    tr = _rows(t)
    nb = tr // CHUNK

    def body(q_ref, k_ref, v_ref, qg_ref, kg_ref, bd_ref, qo, kt, k2, vt, v2):
        bdm = bd_ref[...]
        lane = lax.broadcasted_iota(jnp.int32, (1, 128), 1)
        sub = lax.broadcasted_iota(jnp.int32, (128, 1), 0)

        def pair_layouts(x, t_ref, s_ref, hp, b):
            xt = x.T
            t_ref[hp, b] = jnp.concatenate([jnp.where(sub < 64, xt, 0.0), jnp.where(sub >= 64, xt, 0.0)],
                                           axis=1).astype(BF16)
            s_ref[hp, b] = jnp.concatenate([jnp.where(lane < 64, x, 0.0), jnp.where(lane >= 64, x, 0.0)],
                                           axis=0).astype(BF16)

        for hp in range(8):
            sl = slice(128 * hp, 128 * (hp + 1))
            x = q_ref[:, sl]
            r = lax.rsqrt(_split_dot(x * x, bdm) * (1.0 / 64) + EPS)
            qo[:, sl] = (x * r * (qg_ref[:, sl] * SB_SCALE)).astype(BF16)
            x = k_ref[:, sl]
            r = lax.rsqrt(_split_dot(x * x, bdm) * (1.0 / 64) + EPS)
            kn = x * r * kg_ref[:, sl]
            v = v_ref[:, sl]
            for b in range(nb):
                rows = slice(CHUNK * b, CHUNK * (b + 1))
                pair_layouts(kn[rows], kt, k2, hp, b)
                pair_layouts(v[rows], vt, v2, hp, b)

    col = lambda c: pl.BlockSpec((tr, 1024), lambda i: (i, c))
    vec = pl.BlockSpec((1, 1024), lambda i: (0, 0))
    wide = pl.BlockSpec((8, nb, 128, 256), lambda i: (0, i, 0, 0))
    tall = pl.BlockSpec((8, nb, 256, 128), lambda i: (0, i, 0, 0))
    wsh = jax.ShapeDtypeStruct((8, t // CHUNK, 128, 256), BF16)
    tsh = jax.ShapeDtypeStruct((8, t // CHUNK, 256, 128), BF16)
    return _pcall(body, name="qk_norm_fwd", grid=(t // tr,),
                  in_specs=[col(0), col(1), col(2), vec, vec, pl.BlockSpec((128, 128), lambda i: (0, 0))],
                  out_specs=[col(0), wide, tall, wide, tall],
                  out_shape=[jax.ShapeDtypeStruct((t, 1024), BF16), wsh, tsh, wsh, tsh])(qkv, qkv, qkv, qg, kg, bd)


def _qk_norm_bwd(qkv, dq, dk, dv, qg, kg, bd):
    t = qkv.shape[0]
    tr = _rows(t)

    def body(q_ref, k_ref, dq_ref, dk_ref, dv_ref, qg_ref, kg_ref, bd_ref, o_ref, dqg_ref, dkg_ref):
        @pl.when(pl.program_id(0) == 0)
        def _():
            dqg_ref[...] = jnp.zeros_like(dqg_ref)
            dkg_ref[...] = jnp.zeros_like(dkg_ref)

        bdm = bd_ref[...]
        for part, (src, d_ref, g_ref, dg_ref) in enumerate(((q_ref, dq_ref, qg_ref, dqg_ref),
                                                           (k_ref, dk_ref, kg_ref, dkg_ref))):
            for cix in range(8):
                sl = slice(128 * cix, 128 * (cix + 1))
                x = src[:, sl]
                d = d_ref[:, sl]
                r = lax.rsqrt(_split_dot(x * x, bdm) * (1.0 / 64) + EPS)
                u = d * g_ref[:, sl]
                m = _split_dot(u * x, bdm) * (1.0 / 64)
                o_ref[:, 1024 * part + 128 * cix:1024 * part + 128 * (cix + 1)] = (r * u - x * (r * r * r * m)).astype(BF16)
                dg_ref[:, sl] += jnp.sum(d * x * r, axis=0, keepdims=True)
        o_ref[:, 2048:3072] = dv_ref[...].astype(BF16)

    col = lambda c: pl.BlockSpec((tr, 1024), lambda i: (i, c))
    vec = pl.BlockSpec((1, 1024), lambda i: (0, 0))
    vsh = jax.ShapeDtypeStruct((1, 1024), F32)
    return _pcall(body, name="qk_norm_bwd", grid=(t // tr,),
                  in_specs=[col(0), col(1), col(0), col(0), col(0), vec, vec, pl.BlockSpec((128, 128), lambda i: (0, 0))],
                  out_specs=[pl.BlockSpec((tr, 3072), lambda i: (i, 0)), vec, vec],
                  out_shape=[jax.ShapeDtypeStruct((t, 3072), BF16), vsh, vsh])(qkv, qkv, dq, dk, dv, qg, kg, bd)


def _split2(x):
    hi = x.astype(BF16)
    lo = (x - hi.astype(F32)).astype(BF16)
    return jnp.concatenate([hi, lo], axis=1)


def _sb_scores(z, later_tab):
    e = jnp.exp(-jnp.abs(z))
    ope = 1.0 + e
    sp = jnp.maximum(z, 0.0) + jnp.log(ope)
    return e, ope, _dot(_split2(sp), later_tab)


def _sb_bias_index(i, kb, per):
    g = kb - i * per
    return jnp.where(kb == 0, jnp.where(i == 0, per + 2, per + 1), jnp.where(g >= 0, g, per))


def _sb_qb(t):
    return _tile(t, (384, 128))


def _sb_fwd(qh, kt, v2, later_tab, bias_tab):
    t = qh.shape[0]
    qb = _sb_qb(t)
    per = qb // CHUNK
    nkb_all = t // CHUNK

    nq = t // qb

    def body(q_ref, kt_ref, v2_ref, tab_ref, bias_ref, o_ref, ws_ref, acc, carry, zbuf, wbuf, wsem):
        h, i = pl.program_id(0), pl.program_id(1)
        n = h * nq + i
        p = n & 1
        q = q_ref[...]
        acc[...] = jnp.zeros_like(acc)
        carry[...] = jnp.zeros_like(carry)
        nkb = (i + 1) * per
        save = lambda kb: pltpu.make_async_copy(wbuf.at[p, kb], ws_ref.at[h, i, kb], wsem.at[p, kb])

        def drain(step, par):
            hs, is_ = step // nq, step % nq

            def one(kb, _):
                pltpu.make_async_copy(wbuf.at[par, kb], ws_ref.at[hs, is_, kb], wsem.at[par, kb]).wait()
                return 0

            lax.fori_loop(0, (is_ + 1) * per, one, 0)

        @pl.when(n >= 2)
        def _():
            drain(n - 2, p)

        for u in range(per):
            zbuf[u] = _dot(q, kt_ref[nkb - 1 - u])

        def trip(s, diagonal):
            top = nkb - 1 - per * s
            if not diagonal:
                for u in range(per):
                    save(top + per - u).start()
            z2s = [zbuf[u] for u in range(per)]
            for u in range(per):
                zbuf[u] = _dot(q, kt_ref[jnp.maximum(top - per - u, 0)])
            first = [CHUNK * (per - 1 - u) if diagonal else 0 for u in range(per)]
            cins = [carry[0], carry[1]]
            zs, cus = [], []
            for u in range(per):
                bias = bias_ref[_sb_bias_index(i, top - u, per)][first[u]:]
                zs.append([z2s[u][first[u]:, 128 * hh:128 * (hh + 1)] + bias for hh in range(2)])
                cus.append([_sb_scores(z, tab_ref[...])[2] for z in zs[u]])
            part = None
            for u in range(per):
                kb, lo = top - u, first[u]
                for hh in range(2):
                    sl = slice(128 * hh, 128 * (hh + 1))
                    cu = cus[u][hh]
                    wbuf[p, kb, lo:, sl] = jnp.exp(zs[u][hh] - cu[:, :128] - cins[hh][lo:]).astype(BF16)
                    if lo:
                        wbuf[p, kb, :lo, sl] = jnp.zeros((lo, 128), BF16)
                        cins[hh] = jnp.concatenate([cins[hh][:lo], cins[hh][lo:] + cu[:, 128:]], axis=0)
                    else:
                        cins[hh] = cins[hh] + cu[:, 128:]
                d = _dot(wbuf[p, kb], v2_ref[kb])
                part = d if part is None else part + d
            carry[0], carry[1] = cins[0], cins[1]
            acc[...] += part

        trip(0, True)

        def step(s, _):
            trip(s, False)
            return 0

        lax.fori_loop(1, nkb // per, step, 0)
        for u in range(per):
            save(per - 1 - u).start()
        o_ref[...] = acc[...]

        @pl.when(n == 8 * nq - 1)
        def _():
            drain(n - 1, 1 - p)
            drain(n, p)

    blk = pl.BlockSpec((qb, 128), lambda h, i: (i, h))
    wide = pl.BlockSpec((None, nkb_all, 128, 256), lambda h, i: (h, 0, 0, 0))
    tall = pl.BlockSpec((None, nkb_all, 256, 128), lambda h, i: (h, 0, 0, 0))
    return _pcall(body, name="sb_fwd", grid=(8, t // qb),
                  in_specs=[blk, wide, tall, pl.BlockSpec((256, 256), lambda h, i: (0, 0)),
                            pl.BlockSpec((per + 3, qb, 128), lambda h, i: (0, 0, 0))],
                  out_specs=[blk, pl.BlockSpec(memory_space=pl.ANY)],
                  out_shape=[jax.ShapeDtypeStruct((t, 1024), F32),
                             jax.ShapeDtypeStruct((8, t // qb, nkb_all, qb, 256), BF16)],
                  scratch_shapes=[pltpu.VMEM((qb, 128), F32), pltpu.VMEM((2, qb, 128), F32),
                                  pltpu.VMEM((per, qb, 256), F32), pltpu.VMEM((2, nkb_all, qb, 256), BF16),
                                  pltpu.SemaphoreType.DMA((2, nkb_all))],
                  compiler_params=_params(dimension_semantics=("arbitrary", "arbitrary")))(
                      qh, kt, v2, later_tab, bias_tab)


def _sb_bwd(qh, kt, k2, vt, wsave, do, earlier_tab, bias_tab):
    t = qh.shape[0]
    qb = _sb_qb(t)
    per = qb // CHUNK
    nkb_all = t // CHUNK

    zero_slot = nkb_all
    nq = t // qb

    def body(q_ref, kt_ref, k2_ref, vt_ref, ws_ref, do_ref, etab_ref, bias_ref,
             dq_ref, dk_ref, dv_ref, acc, gcarry, zbuf, dwbuf, wbuf, wsem, dzbuf):
        h, i = pl.program_id(0), pl.program_id(1)
        n = h * nq + i
        p = n & 1

        @pl.when(i == 0)
        def _():
            dk_ref[...] = jnp.zeros_like(dk_ref)
            dv_ref[...] = jnp.zeros_like(dv_ref)

        nkb = (i + 1) * per
        fetch = lambda kb: pltpu.make_async_copy(ws_ref.at[h, i, kb], wbuf.at[p, kb], wsem.at[p, kb])

        def prefetch(step, par):
            hs, is_ = step // nq, step % nq

            def one(kb, _):
                pltpu.make_async_copy(ws_ref.at[hs, is_, kb], wbuf.at[par, kb], wsem.at[par, kb]).start()
                return 0

            lax.fori_loop(0, (is_ + 1) * per, one, 0)

        @pl.when(n == 0)
        def _():
            prefetch(n, p)

        @pl.when(n + 1 < 8 * nq)
        def _():
            prefetch(n + 1, 1 - p)

        q = q_ref[...]
        dob = do_ref[...].astype(BF16)
        lane = lax.broadcasted_iota(jnp.int32, (1, 128), 1)
        acc[...] = jnp.zeros_like(acc)
        gcarry[...] = jnp.zeros_like(gcarry)
        zbuf[...] = _dot(q, kt_ref[0])
        dwbuf[...] = _dot(dob, vt_ref[0])
        dzbuf[...] = jnp.zeros_like(dzbuf)
        wbuf[p, zero_slot] = jnp.zeros((qb, 256), BF16)

        def gradients(slot, kb):
            dz2 = dzbuf[...]
            acc[...] += _dot(dz2, k2_ref[kb])
            dk2 = _dot(dz2, q, "tn")
            dv2 = _dot(wbuf[p, slot], dob, "tn")
            dk_ref[kb] += jnp.where(lane < 64, dk2[:128], dk2[128:])
            dv_ref[kb] += jnp.where(lane < 64, dv2[:128], dv2[128:])

        def trip(kb, lo):
            fetch(kb).wait()
            bias = bias_ref[_sb_bias_index(i, kb, per)][lo:]
            z2 = zbuf[...]
            dw2 = dwbuf[...]
            nxt = jnp.minimum(kb + 1, nkb - 1)
            zbuf[...] = _dot(q, kt_ref[nxt])
            dwbuf[...] = _dot(dob, vt_ref[nxt])
            gradients(jnp.where(kb == 0, zero_slot, kb - 1), jnp.maximum(kb - 1, 0))
            w2 = wbuf[p, kb]
            for hh in range(2):
                sl = slice(128 * hh, 128 * (hh + 1))
                z = z2[lo:, sl] + bias
                e = jnp.exp(-jnp.abs(z))
                r = 1.0 / (1.0 + e)
                sig = jnp.where(z >= 0, r, e * r)
                gw = w2[lo:, sl].astype(F32) * dw2[lo:, sl]
                cu2 = _dot(_split2(gw), etab_ref[...])
                gin = gcarry[hh, lo:, :]
                gcarry[hh, lo:, :] = gin + cu2[:, 128:]
                dzbuf[lo:, sl] = (gw - sig * (gw + cu2[:, :128] + gin)).astype(BF16)
                if lo:
                    dzbuf[:lo, sl] = jnp.zeros((lo, 128), BF16)

        def step(kb, _):
            trip(kb, 0)
            return 0

        lax.fori_loop(0, nkb - per, step, 0)
        for g in range(per):
            trip(nkb - per + g, CHUNK * g)
        gradients(nkb - 1, nkb - 1)
        dq_ref[...] = acc[...] * SB_SCALE

    blk = pl.BlockSpec((qb, 128), lambda h, i: (i, h))
    wide = pl.BlockSpec((None, nkb_all, 128, 256), lambda h, i: (h, 0, 0, 0))
    tall = pl.BlockSpec((None, nkb_all, 256, 128), lambda h, i: (h, 0, 0, 0))
    tab = pl.BlockSpec((256, 256), lambda h, i: (0, 0))
    kv_out = pl.BlockSpec((nkb_all, 128, 128), lambda h, i: (0, 0, h))
    ksh = jax.ShapeDtypeStruct((nkb_all, 128, 1024), F32)
    dq, dk, dv = _pcall(
        body, name="sb_bwd", grid=(8, t // qb),
        in_specs=[blk, wide, tall, wide, pl.BlockSpec(memory_space=pl.ANY), blk, tab,
                  pl.BlockSpec((per + 3, qb, 128), lambda h, i: (0, 0, 0))],
        out_specs=[blk, kv_out, kv_out], out_shape=[jax.ShapeDtypeStruct((t, 1024), F32), ksh, ksh],
        scratch_shapes=[pltpu.VMEM((qb, 128), F32), pltpu.VMEM((2, qb, 128), F32),
                        pltpu.VMEM((qb, 256), F32), pltpu.VMEM((qb, 256), F32),
                        pltpu.VMEM((2, nkb_all + 1, qb, 256), BF16), pltpu.SemaphoreType.DMA((2, nkb_all)),
                        pltpu.VMEM((qb, 256), BF16)],
        compiler_params=_params(dimension_semantics=("arbitrary", "arbitrary")))(
            qh, kt, k2, vt, wsave, do, earlier_tab, bias_tab)
    return dq, dk.reshape(t, 1024), dv.reshape(t, 1024)


def _adamw_math(w, g, m, v):
    m = ADAM_B1 * m + (1.0 - ADAM_B1) * g
    v = ADAM_B2 * v + (1.0 - ADAM_B2) * (g * g)
    m_hat = m / (1.0 - ADAM_B1 ** ADAM_STEP)
    v_hat = v / (1.0 - ADAM_B2 ** ADAM_STEP)
    delta = -ADAM_LR * (m_hat / (jnp.sqrt(v_hat) + ADAM_EPS) + ADAM_WD * w)
    return delta, m, v


def _adamw(name, w, owns, recvs, m, v, me):
    shape = w.shape
    c = shape[-1]
    nl = len(owns)
    w3, m3, v3 = (a.reshape(nl, -1, c) for a in (w, m, v))
    r = w3.shape[1]
    tr = _tile(r, (256, 128))
    owns = [o.reshape(N_DEV, r, c) for o in owns]
    recvs = [p.reshape(N_DEV - 1, r, c) for p in recvs]

    def body(me_ref, w_ref, *rest):
        own_refs, recv_refs = rest[:nl], rest[nl:2 * nl]
        m_ref, v_ref = rest[2 * nl:2 * nl + 2]
        g_out, d_out, m_out, v_out = rest[2 * nl + 2:]
        layer = pl.program_id(0)

        def grad(k):
            g = own_refs[k][...].astype(F32)
            for s in range(N_DEV - 1):
                g = g + recv_refs[k][s].astype(F32)
            return g

        g = grad(0)
        for k in range(1, nl):
            g = jnp.where(layer == k, grad(k), g)
        d, mn, vn = _adamw_math(w_ref[...], g, m_ref[...], v_ref[...])
        g_out[...] = g
        d_out[...] = d
        m_out[...] = mn
        v_out[...] = vn

    row = pl.BlockSpec((None, tr, c), lambda l, i, me_ref: (l, i, 0))
    own = lambda k: pl.BlockSpec((None, tr, c), lambda l, i, me_ref: (me_ref[0], jnp.where(l == k, i, 0), 0))
    rcv = lambda k: pl.BlockSpec((N_DEV - 1, tr, c), lambda l, i, me_ref: (0, jnp.where(l == k, i, 0), 0))
    osh = jax.ShapeDtypeStruct((nl, r, c), F32)
    grid_spec = pltpu.PrefetchScalarGridSpec(
        num_scalar_prefetch=1, grid=(nl, r // tr),
        in_specs=[row] + [own(k) for k in range(nl)] + [rcv(k) for k in range(nl)] + [row, row],
        out_specs=[row, row, row, row])
    outs = _pcall(body, name=name, grid_spec=grid_spec, out_shape=[osh, osh, osh, osh])(
        me.reshape(1), w3, *owns, *recvs, m3, v3)
    return tuple(o.reshape(shape) for o in outs)


def _place():
    x, y, c = lax.axis_index("x"), lax.axis_index("y"), lax.axis_index("c")
    return x, y, c, 4 * x + 2 * y + c


def _peer(x, y, c, rel):
    return (x ^ ((rel >> 2) & 1), y ^ ((rel >> 1) & 1), c ^ (rel & 1))


def _gather_first(now, later):
    n, k = len(now), len(later)

    def body(*refs):
        ins, outs = refs[:n + k], refs[n + k:2 * (n + k)]
        send, recv, lsem = refs[2 * (n + k):]
        x, y, c, me = _place()
        locals_ = []
        for w in range(n + k):
            local = pltpu.make_async_copy(ins[w], outs[w].at[me], lsem.at[w])
            local.start()
            locals_.append(local)
        def copy(w, src, slot, rel, to_rel):
            return pltpu.make_async_remote_copy(src_ref=src, dst_ref=outs[w].at[slot], send_sem=send.at[w, rel - 1],
                                                recv_sem=recv.at[w, rel - 1], device_id=_peer(x, y, c, to_rel),
                                                device_id_type=MESH)

        for w in range(n):
            for rel in (1, 2, 4, 6):
                copy(w, ins[w], me, rel, rel).start()
        for w in range(n):
            for rel in (2, 4, 6):
                copy(w, ins[w], me ^ rel, rel, rel).wait_recv()
                copy(w, outs[w].at[me ^ rel], me ^ rel, rel | 1, 1).start()
        for w in range(n):
            for rel in (1, 3, 5, 7):
                copy(w, ins[w], me ^ rel, rel, 1).wait_recv()
            for rel in range(1, N_DEV):
                copy(w, ins[w], me, rel, rel).wait_send()
        for local in locals_:
            local.wait()

    hbm = pl.BlockSpec(memory_space=pl.ANY)
    vmem = pl.BlockSpec(memory_space=pltpu.VMEM)
    arrays = list(now) + list(later)
    return _pcall(body, name="gather_first", in_specs=[vmem] * (n + k), out_specs=[hbm] * (n + k),
                  out_shape=[jax.ShapeDtypeStruct((N_DEV,) + a.shape, a.dtype) for a in arrays],
                  scratch_shapes=[pltpu.SemaphoreType.DMA((n, N_DEV - 1)), pltpu.SemaphoreType.DMA((n, N_DEV - 1)),
                                  pltpu.SemaphoreType.DMA((n + k,))],
                  compiler_params=_params(has_side_effects=True))(*arrays)


_HBM = pl.BlockSpec(memory_space=pltpu.HBM)
_SEM = pl.BlockSpec(memory_space=pltpu.SEMAPHORE)
_DATAFLOW = pltpu.SideEffectType.DATAFLOW_SIDE_EFFECTING


def _exchange_refs(srcs, lands, mode, me, rel, j):
    if mode == "gather":
        return srcs[j], lands[j].at[me], lands[j].at[me ^ rel]
    return srcs[j].at[me ^ rel], lands[j].at[rel - 1], lands[j].at[rel - 1]


def _exchange_start(name, srcs, lands, mode):
    n = len(srcs)

    def body(*refs):
        ins, lnd = refs[:n], refs[n:2 * n]
        send, recv = refs[2 * n], refs[2 * n + 1]
        token = refs[-1]
        x, y, c, me = _place()
        for j in range(n):
            for rel in range(1, N_DEV):
                src, dst, _ = _exchange_refs(ins, lnd, mode, me, rel, j)
                pltpu.make_async_remote_copy(src_ref=src, dst_ref=dst, send_sem=send.at[j * (N_DEV - 1) + rel - 1],
                                             recv_sem=recv.at[j * (N_DEV - 1) + rel - 1],
                                             device_id=_peer(x, y, c, rel), device_id_type=MESH).start()
        token[...] = jnp.zeros_like(token)

    sems = pltpu.SemaphoreType.DMA((n * (N_DEV - 1),))
    hbm_like = lambda a: pltpu.HBM(a.shape, a.dtype)
    outs = _pcall(body, name=name + "_start",
                  in_specs=[_HBM] * (2 * n), out_specs=[_SEM, _SEM] + [_HBM] * (2 * n) + [pl.BlockSpec(memory_space=pltpu.VMEM)],
                  out_shape=[sems, sems] + [hbm_like(a) for a in srcs] + [hbm_like(a) for a in lands]
                  + [jax.ShapeDtypeStruct((8, 128), F32)],
                  input_output_aliases={i: 2 + i for i in range(2 * n)},
                  compiler_params=pltpu.CompilerParams(has_side_effects=_DATAFLOW))(
                      *[pltpu.with_memory_space_constraint(a, pltpu.HBM) for a in list(srcs) + list(lands)])
    return dict(name=name, mode=mode, n=n, send=outs[0], recv=outs[1], srcs=outs[2:2 + n], lands=outs[2 + n:2 + 2 * n],
                token=outs[-1][0, 0])


def _exchange_wait(ex, after):
    n, mode = ex["n"], ex["mode"]

    def body(*refs):
        ins, lnd = refs[:n], refs[n:2 * n]
        send, recv = refs[2 * n], refs[2 * n + 1]
        x, y, c, me = _place()
        for j in range(n):
            for rel in range(1, N_DEV):
                src, dst, landed = _exchange_refs(ins, lnd, mode, me, rel, j)
                pltpu.make_async_remote_copy(src_ref=src, dst_ref=dst, send_sem=send.at[j * (N_DEV - 1) + rel - 1],
                                             recv_sem=recv.at[j * (N_DEV - 1) + rel - 1],
                                             device_id=_peer(x, y, c, rel), device_id_type=MESH).wait_send()
                pltpu.make_async_remote_copy(src_ref=src, dst_ref=landed, send_sem=send.at[j * (N_DEV - 1) + rel - 1],
                                             recv_sem=recv.at[j * (N_DEV - 1) + rel - 1],
                                             device_id=_peer(x, y, c, rel), device_id_type=MESH).wait_recv()

    hbm_like = lambda a: pltpu.HBM(a.shape, a.dtype)
    arrays = list(ex["srcs"]) + list(ex["lands"])
    outs = _pcall(body, name=ex["name"] + "_wait",
                  in_specs=[_HBM] * (2 * n) + [_SEM, _SEM, pl.BlockSpec(memory_space=pl.ANY)],
                  out_specs=[_HBM] * (2 * n), out_shape=[hbm_like(a) for a in arrays],
                  input_output_aliases={i: i for i in range(2 * n)},
                  compiler_params=pltpu.CompilerParams(has_side_effects=_DATAFLOW))(
                      *arrays, ex["send"], ex["recv"], after)
    return outs[:n], outs[n:]


def _scatter_start(name, grads):
    lands = [lax.empty((N_DEV - 1,) + g.shape[1:], g.dtype) for g in grads]
    return _exchange_start(name, grads, lands, "scatter")


ROW_MIX, ROW_MLP, ROW_CB, ROW_LG, ROW_LB, ROW_QN, ROW_KN, ROW_LOSS = 0, 2, 4, 5, 6, 7, 8, 9
ROW_META, ROW_CW, ROW_GN, SMALL_ROWS = 16, 32, 64, 72


def _sum_small(slots):
    def body(s_ref, o_ref):
        tot = s_ref[0]
        for s in range(1, N_DEV):
            tot = tot + s_ref[s]
        o_ref[...] = tot
        for row in (ROW_QN, ROW_KN):
            v = tot[row:row + 1, :]
            f = v[:, 0:128]
            for k in range(1, 8):
                f = f + v[:, 128 * k:128 * (k + 1)]
            o_ref[row:row + 1, 0:64] = f[:, 0:64] + f[:, 64:128]

    return _pcall(body, name="sum_small", out_shape=jax.ShapeDtypeStruct(slots.shape[1:], F32))(slots)


def _adamw_small(w, g, m, v):
    def body(w_ref, g_ref, m_ref, v_ref, d_out, m_out, v_out):
        d, mn, vn = _adamw_math(w_ref[...], g_ref[...], m_ref[...], v_ref[...])
        d_out[...] = d
        m_out[...] = mn
        v_out[...] = vn

    osh = jax.ShapeDtypeStruct(w.shape, F32)
    return _pcall(body, name="adamw_small", out_shape=[osh, osh, osh])(w, g, m, v)


def _local_step(h0, target, p, weight, emit):
    t = h0.shape[0]
    tables = _ret_tables(t)
    bd, later_tab, earlier_tab, bias_tab = _seg_tables(_sb_qb(t))
    row = lambda a, i: a[i:i + 1]

    hn_a = _rms_fwd("rms_mix0", h0, row(p["norm_mix_g"], 0))
    w_in = weight("w_in", hn_a)
    proj = _mm_cols("proj_in", hn_a, w_in, ())
    gn_flat = p["gn_g"].reshape(1, 1024)
    o_ret, states, cat = _ret_fwd(proj, gn_flat, tables)
    cat, hdn, ycv = _conv_fwd(cat, proj, p["conv_w"], p["conv_b"], p["ln_g"], p["ln_b"])
    w_out = weight("w_out", cat)
    h1, hn_b = _mm_rows_norm("mix_out", cat, w_out, h0, row(p["norm_mlp_g"], 0))
    w1_0, w2_0 = weight("w1_0", hn_b), weight("w2_0", hn_b)
    a0, s0 = _mm_cols("mlp0_up", hn_b, w1_0, (), epi="relu2")
    h2, hn_c = _mm_rows_norm("mlp0_down", s0, w2_0, h1, row(p["norm_mix_g"], 1))

    w_qkv = weight("w_qkv", hn_c)
    qkv = _mm_cols("qkv", hn_c, w_qkv, ())
    qg = jnp.tile(p["qn_g"], (1, 16))
    kg = jnp.tile(p["kn_g"], (1, 16))
    qh, kt, k2, vt, v2 = _qk_norm_fwd(qkv, qg, kg, bd)
    o_sb, w_sb = _sb_fwd(qh, kt, v2, later_tab, bias_tab)
    w_o = weight("w_o", o_sb)
    h3, hn_d = _mm_rows_norm("attn_out", o_sb, w_o, h2, row(p["norm_mlp_g"], 1))
    w1_1, w2_1 = weight("w1_1", hn_d), weight("w2_1", hn_d)
    a1, s1 = _mm_cols("mlp1_up", hn_d, w1_1, (), epi="relu2")
    dh, loss = _mm_rows_loss("mlp1_down", s1, w2_1, h3, target)

    def mlp_bwd(tag, layer, w1, w2, dh, h_in, hn, a, s):
        da = _mm_rows_t(f"{tag}_dact", dh, w2, (), out_dtype=BF16, epi="drelu2", extra=a)
        dw2 = _wgrad_rows(f"{tag}_dw2", s, dh, 512)
        dw1 = _wgrad_cols(f"{tag}_dw1", hn, da, 512)
        tok = emit(tag, [dw1, dw2])
        return _mm_cols_t_rms(f"{tag}_dhn", da, w1, h_in, row(p["norm_mlp_g"], layer) + tok, dh)

    dh, dg_mlp1 = mlp_bwd("mlp1", 1, w1_1, w2_1, dh, h3, hn_d, a1, s1)

    do_sb = _mm_rows_t("attn_dout", dh, w_o, ())
    dw_o = _wgrad_rows("attn_dwo", o_sb, dh, 128)
    dq, dk, dv = _sb_bwd(qh, kt, k2, vt, w_sb, do_sb, earlier_tab, bias_tab)
    dqkv, dqg, dkg = _qk_norm_bwd(qkv, dq, dk, dv, qg, kg, bd)
    dw_qkv = _wgrad_cols("qkv_dw", hn_c, dqkv, 384)
    tok = emit("attn", [dw_qkv, dw_o])
    dh, dg_mix1 = _mm_cols_t_rms("qkv_dhn", dqkv, w_qkv, h2, row(p["norm_mix_g"], 1) + tok, dh)

    dh, dg_mlp0 = mlp_bwd("mlp0", 0, w1_0, w2_0, dh, h1, hn_b, a0, s0)

    dw_out = _wgrad_rows("mix_dwout", cat, dh, 256)
    tok = emit("mix0_out", [dw_out])
    do_ret, dproj, dgn, dy, dlg, dlb, dcb = _mix_bwd_head(dh, w_out, o_ret, proj, gn_flat + tok, ycv,
                                                          p["ln_g"], p["ln_b"])
    dproj = _ret_bwd(dproj, proj, states, do_ret, tables)
    dproj, dug, dcw = _conv_bwd_taps(dproj, dy, hdn, proj, p["conv_w"])
    dproj = lax.dynamic_update_slice(dproj, dug, (0, 4096))
    dw_in = _wgrad_cols("proj_dw", hn_a, dproj, 640)
    tok = emit("mix0", [dw_in])
    dh, dg_mix0 = _mm_cols_t_rms("proj_dhn", dproj, w_in, h0, row(p["norm_mix_g"], 0) + tok, dh)

    rid = lax.broadcasted_iota(jnp.int32, (16, 1), 0)
    loss_row = jnp.broadcast_to(loss[0:1, 0:1], (1, D_MODEL))
    vecs = sum(jnp.where(rid == k, v, 0.0)
               for k, v in enumerate((dg_mix0, dg_mix1, dg_mlp0, dg_mlp1, dcb, dlg, dlb, dqg, dkg, loss_row)))
    small = jnp.concatenate([vecs, dh[PAD_FRONT:TOK0], dcw, jnp.where(rid[:8] == 0, dgn, 0.0)], axis=0)
    return dh[TOK0:], small


_SMALL_NAMES = ("meta", "norm_mix_g", "norm_mlp_g", "even_ret_gn_g", "even_conv_w", "even_conv_b",
                "even_conv_ln_g", "even_conv_ln_b", "odd_q_norm_g", "odd_k_norm_g")
_BIG_NAMES = ("even_w_in", "even_w_out", "odd_w_qkv", "odd_w_o", "mlp_w1", "mlp_w2")
_ORDER = ("meta", "norm_mix_g", "norm_mlp_g", "even_w_in", "even_ret_gn_g", "even_conv_w", "even_conv_b",
          "even_conv_ln_g", "even_conv_ln_b", "even_w_out", "odd_w_qkv", "odd_q_norm_g", "odd_k_norm_g",
          "odd_w_o", "mlp_w1", "mlp_w2")


def _pack128(a):
    flat = a.reshape(-1)
    n = flat.shape[0]
    rows = -(-n // 128)
    rows8 = -(-rows // 8) * 8
    return jnp.pad(flat, (0, rows8 * 128 - n)).reshape(rows8, 128)


def kernel(x, meta, norm_mix_g, norm_mlp_g, even_w_in, even_ret_gn_g, even_conv_w, even_conv_b, even_conv_ln_g, even_conv_ln_b, even_w_out, odd_w_qkv, odd_q_norm_g, odd_k_norm_g, odd_w_o, mlp_w1, mlp_w2, loss_target, m_meta, m_norm_mix_g, m_norm_mlp_g, m_even_w_in, m_even_ret_gn_g, m_even_conv_w, m_even_conv_b, m_even_conv_ln_g, m_even_conv_ln_b, m_even_w_out, m_odd_w_qkv, m_odd_q_norm_g, m_odd_k_norm_g, m_odd_w_o, m_mlp_w1, m_mlp_w2, v_meta, v_norm_mix_g, v_norm_mlp_g, v_even_w_in, v_even_ret_gn_g, v_even_conv_w, v_even_conv_b, v_even_conv_ln_g, v_even_conv_ln_b, v_even_w_out, v_odd_w_qkv, v_odd_q_norm_g, v_odd_k_norm_g, v_odd_w_o, v_mlp_w1, v_mlp_w2):
    w = dict(meta=meta, norm_mix_g=norm_mix_g, norm_mlp_g=norm_mlp_g, even_w_in=even_w_in,
             even_ret_gn_g=even_ret_gn_g, even_conv_w=even_conv_w, even_conv_b=even_conv_b,
             even_conv_ln_g=even_conv_ln_g, even_conv_ln_b=even_conv_ln_b, even_w_out=even_w_out,
             odd_w_qkv=odd_w_qkv, odd_q_norm_g=odd_q_norm_g, odd_k_norm_g=odd_k_norm_g, odd_w_o=odd_w_o,
             mlp_w1=mlp_w1, mlp_w2=mlp_w2)
    mom = dict(meta=m_meta, norm_mix_g=m_norm_mix_g, norm_mlp_g=m_norm_mlp_g, even_w_in=m_even_w_in,
               even_ret_gn_g=m_even_ret_gn_g, even_conv_w=m_even_conv_w, even_conv_b=m_even_conv_b,
               even_conv_ln_g=m_even_conv_ln_g, even_conv_ln_b=m_even_conv_ln_b, even_w_out=m_even_w_out,
               odd_w_qkv=m_odd_w_qkv, odd_q_norm_g=m_odd_q_norm_g, odd_k_norm_g=m_odd_k_norm_g, odd_w_o=m_odd_w_o,
               mlp_w1=m_mlp_w1, mlp_w2=m_mlp_w2)
    var = dict(meta=v_meta, norm_mix_g=v_norm_mix_g, norm_mlp_g=v_norm_mlp_g, even_w_in=v_even_w_in,
               even_ret_gn_g=v_even_ret_gn_g, even_conv_w=v_even_conv_w, even_conv_b=v_even_conv_b,
               even_conv_ln_g=v_even_conv_ln_g, even_conv_ln_b=v_even_conv_ln_b, even_w_out=v_even_w_out,
               odd_w_qkv=v_odd_w_qkv, odd_q_norm_g=v_odd_q_norm_g, odd_k_norm_g=v_odd_k_norm_g, odd_w_o=v_odd_w_o,
               mlp_w1=v_mlp_w1, mlp_w2=v_mlp_w2)
    me = 4 * lax.axis_index("x") + 2 * lax.axis_index("y") + lax.axis_index("c")

    small_in = jnp.concatenate([meta, jnp.pad(even_conv_w[0], ((0, 1), (0, 0))),
                                jnp.pad(even_ret_gn_g[0], ((0, 4), (0, 96)))], axis=0)
    b16 = lambda a: a.astype(BF16)
    later_src = dict(w_out=b16(even_w_out[0]), w1_0=b16(mlp_w1[0]), w2_0=b16(mlp_w2[0]),
                     w_qkv=b16(odd_w_qkv[0]), w_o=b16(odd_w_o[0]), w1_1=b16(mlp_w1[1]), w2_1=b16(mlp_w2[1]))
    landed = _gather_first([b16(even_w_in[0]), small_in], list(later_src.values()))
    g_in, g_small = landed[0], landed[1]
    own_slot = dict(zip(later_src, landed[2:]))
    groups = (("gather_l0", ("w_out", "w1_0", "w2_0")), ("gather_attn", ("w_qkv", "w_o")),
              ("gather_l1", ("w1_1", "w2_1")))
    pending = {}
    gather_tok = jnp.zeros((), F32)
    for gname, names in groups:
        ex = _exchange_start(gname, [later_src[n] for n in names], [own_slot[n] for n in names], "gather")
        gather_tok = gather_tok + ex["token"]
        for n in names:
            pending[n] = (ex, names)
    arrived = dict(w_in=g_in)

    def weight(name, after):
        if name not in arrived:
            ex, names = pending[name]
            arrived.update(zip(names, _exchange_wait(ex, after)[1]))
        return arrived[name]

    cols = lambda a: jnp.transpose(a, (1, 0, 2)).reshape(a.shape[1], -1)
    p = dict(norm_mix_g=norm_mix_g + gather_tok, norm_mlp_g=norm_mlp_g, conv_b=even_conv_b, ln_g=even_conv_ln_g,
             ln_b=even_conv_ln_b, qn_g=odd_q_norm_g, kn_g=odd_k_norm_g,
             gn_g=cols(g_small[:, 48:52, :32]),
             conv_w=jnp.pad(cols(g_small[:, 16:47]), ((0, 1), (0, 0))))
    meta_full = cols(g_small[:, 0:16])

    scatters = {}

    def emit(tag, grads):
        scatters[tag] = _scatter_start("scatter_" + tag, grads)
        return scatters[tag]["token"]

    h0 = jnp.concatenate([jnp.zeros((PAD_FRONT, D_MODEL), F32), meta_full, x[0]], axis=0)
    target = jnp.concatenate([jnp.zeros((TOK0, D_MODEL), F32), loss_target[0]], axis=0)
    grad_x, small_part = _local_step(h0, target, p, weight, emit)

    out = {}
    got = {}

    def update(names, terms, after):
        for tag in {t for name in names for t, _ in terms[name]} - set(got):
            got[tag] = _exchange_wait(scatters[tag], after)
        for name in names:
            owns, recvs = zip(*[(got[t][0][j], got[t][1][j]) for t, j in terms[name]])
            out[name] = _adamw("adamw_" + name, w[name], list(owns), list(recvs), mom[name], var[name], me)

    terms = dict(even_w_in=[("mix0", 0)], even_w_out=[("mix0_out", 0)], odd_w_qkv=[("attn", 0)], odd_w_o=[("attn", 1)],
                 mlp_w1=[("mlp0", 0), ("mlp1", 0)], mlp_w2=[("mlp0", 1), ("mlp1", 1)])
    small_ex = _exchange_start("small", [small_part], [lax.empty((N_DEV,) + small_part.shape, F32)], "gather")
    update(("mlp_w1", "mlp_w2", "odd_w_qkv", "odd_w_o", "even_w_out"), terms, grad_x)
    update(("even_w_in",), terms, out["even_w_out"][1])
    (own_part,), (slots,) = _exchange_wait(small_ex, out["even_w_in"][1])
    tot = _sum_small(lax.dynamic_update_slice(slots, own_part[None], (me, 0, 0)))
    loss = tot[ROW_LOSS, 0]

    shard_cols = lambda a, width: lax.dynamic_slice_in_dim(a, me * width, width, axis=1)
    one = lambda r: tot[r:r + 1]
    small_g = dict(
        norm_mix_g=tot[ROW_MIX:ROW_MIX + 2], norm_mlp_g=tot[ROW_MLP:ROW_MLP + 2],
        even_conv_b=one(ROW_CB), even_conv_ln_g=one(ROW_LG), even_conv_ln_b=one(ROW_LB),
        odd_q_norm_g=one(ROW_QN)[:, :64], odd_k_norm_g=one(ROW_KN)[:, :64],
        meta=shard_cols(tot[ROW_META:ROW_META + N_META], 128),
        even_conv_w=shard_cols(tot[ROW_CW:ROW_CW + CONV_WIDTH], 128)[None],
        even_ret_gn_g=shard_cols(tot[ROW_GN].reshape(4, 256), 32)[None])
    packs = {n: (_pack128(w[n]), _pack128(small_g[n]), _pack128(mom[n]), _pack128(var[n])) for n in _SMALL_NAMES}
    cat4 = [jnp.concatenate([packs[n][i] for n in _SMALL_NAMES], axis=0) for i in range(4)]
    d_s, m_s, v_s = _adamw_small(*cat4)
    r0 = 0
    for n in _SMALL_NAMES:
        rows = packs[n][0].shape[0]
        size = w[n].size
        take = lambda a: a[r0:r0 + rows].reshape(-1)[:size].reshape(w[n].shape)
        out[n] = (small_g[n].reshape(w[n].shape), take(d_s), take(m_s), take(v_s))
        r0 += rows

    res = [loss, grad_x[None]]
    for i in range(4):
        res.extend(out[n][i] for n in _ORDER)
    return tuple(res)
```

```python
import functools

import numpy as np
import jax
import jax.numpy as jnp
from jax import lax
from jax.experimental import pallas as pl
from jax.experimental.pallas import tpu as pltpu

F32 = jnp.float32
BF16 = jnp.bfloat16

D_MODEL = 1024
N_META = 16
CHUNK = 128
PAD_FRONT = 112
TOK0 = PAD_FRONT + N_META
EPS = 1e-6
N_DEV = 8
RET_HEADS = 4
RET_DECAY_OFFSET = 5.0
ROPE_BASE = 10000.0
CONV_WIDTH = 31
HALO = 32
SB_SCALE = 64 ** -0.5
RET_SCALE = 128 ** -0.5
ADAM_LR, ADAM_B1, ADAM_B2, ADAM_EPS, ADAM_WD, ADAM_STEP = 0.001, 0.9, 0.999, 1e-08, 0.01, 10
VMEM_LIMIT = 56 * 1024 * 1024
MESH = pl.DeviceIdType.MESH


def _pcall(body, **kw):
    return pl.pallas_call(body, **kw)


def _params(**kw):
    return pltpu.CompilerParams(vmem_limit_bytes=VMEM_LIMIT, **kw)


def _tile(n, cands):
    for c in cands:
        if n % c == 0:
            return c
    raise ValueError(f"no tile for {n} in {cands}")


def _sigmoid(x):
    return 1.0 / (1.0 + jnp.exp(-x))


_DIMS = {
    "nn": (((1,), (0,)), ((), ())),
    "nt": (((1,), (1,)), ((), ())),
    "tn": (((0,), (0,)), ((), ())),
}


def _matmul(name, a, b, *, grid, a_spec, b_spec, o_spec, out_shape, contract, acc_shape,
            epi="plain", extra=None, extra_spec=None):
    nk = grid[2]
    dims = _DIMS[contract]
    n_in = 3 if extra is not None else 2
    n_out = 2 if epi == "relu2" else 1

    def body(*refs):
        a_ref, b_ref = refs[0], refs[1]
        e_ref = refs[2] if extra is not None else None
        outs = refs[n_in:n_in + n_out]
        acc = refs[-1]
        k = pl.program_id(2)
        part = lax.dot_general(a_ref[...].astype(BF16), b_ref[...].astype(BF16), dims, preferred_element_type=F32)
        if nk > 1:
            @pl.when(k == 0)
            def _():
                acc[...] = jnp.zeros_like(acc)

            acc[...] += part

        @pl.when(k == nk - 1)
        def _():
            r = acc[...] if nk > 1 else part
            if epi == "plain":
                outs[0][...] = r.astype(outs[0].dtype)
            elif epi == "residual":
                outs[0][...] = (r + e_ref[...]).astype(outs[0].dtype)
            elif epi == "relu2":
                outs[0][...] = r
                rr = jnp.maximum(r, 0.0)
                outs[1][...] = (rr * rr).astype(BF16)
            elif epi == "drelu2":
                outs[0][...] = (r * (2.0 * jnp.maximum(e_ref[...], 0.0))).astype(outs[0].dtype)

    in_specs = [a_spec, b_spec] + ([extra_spec] if extra is not None else [])
    args = (a, b) + ((extra,) if extra is not None else ())
    if n_out == 2:
        out_specs = [o_spec, o_spec]
    else:
        out_specs = o_spec
    return _pcall(body, name=name, grid=grid, in_specs=in_specs, out_specs=out_specs,
                  out_shape=out_shape, scratch_shapes=[pltpu.VMEM(acc_shape, F32)],
                  compiler_params=_params(dimension_semantics=("parallel", "parallel", "arbitrary")))(*args)


def _tm(t):
    return _tile(t, (1408, 768, 384, 128))


def _mm_cols(name, a, wb, lead, out_dtype=F32, epi="plain"):
    t, kdim = a.shape
    n = wb.shape[-1]
    tm, tk = _tm(t), _tile(kdim, (1024, 512))
    nl = len(lead)
    b_spec = pl.BlockSpec((None,) * (1 + nl) + (tk, n), lambda i, j, k: (j,) + lead + (k, 0))
    o_spec = pl.BlockSpec((tm, n), lambda i, j, k: (i, j))
    if epi == "relu2":
        out_shape = [jax.ShapeDtypeStruct((t, N_DEV * n), F32), jax.ShapeDtypeStruct((t, N_DEV * n), BF16)]
    else:
        out_shape = jax.ShapeDtypeStruct((t, N_DEV * n), out_dtype)
    return _matmul(name, a, wb, grid=(t // tm, N_DEV, kdim // tk),
                   a_spec=pl.BlockSpec((tm, tk), lambda i, j, k: (i, k)), b_spec=b_spec, o_spec=o_spec,
                   out_shape=out_shape, contract="nn", acc_shape=(tm, n), epi=epi)


def _tm_deep(t, kdim):
    return _tm(t) if kdim <= 2048 else _tile(t, (704, 384, 128))


def _mm_cols_t_rms(name, a, wb, h, g, dres):
    t = a.shape[0]
    nb, kdim, n = wb.shape
    tm = _tile(t, (704, 384, 128))

    def body(a_ref, b_ref, h_ref, g_ref, r_ref, o_ref, dg_ref):
        @pl.when(pl.program_id(0) == 0)
        def _():
            dg_ref[...] = jnp.zeros_like(dg_ref)

        d = _dot(a_ref[:, 0:n].astype(BF16), b_ref[0], "nt")
        for j in range(1, nb):
            d = d + _dot(a_ref[:, j * n:(j + 1) * n].astype(BF16), b_ref[j], "nt")
        x = h_ref[...]
        rs = lax.rsqrt(jnp.mean(x * x, axis=-1, keepdims=True) + EPS)
        u = d * g_ref[...]
        m = jnp.mean(u * x, axis=-1, keepdims=True)
        o_ref[...] = r_ref[...] + rs * u - x * (rs * rs * rs * m)
        dg_ref[...] += jnp.sum(d * x * rs, axis=0, keepdims=True)

    row = pl.BlockSpec((tm, kdim), lambda i: (i, 0))
    vec = pl.BlockSpec((1, kdim), lambda i: (0, 0))
    return _pcall(body, name=name, grid=(t // tm,),
                  in_specs=[pl.BlockSpec((tm, nb * n), lambda i: (i, 0)),
                            pl.BlockSpec((nb, kdim, n), lambda i: (0, 0, 0)), row, vec, row],
                  out_specs=[row, vec],
                  out_shape=[jax.ShapeDtypeStruct((t, kdim), F32), jax.ShapeDtypeStruct((1, kdim), F32)],
                  compiler_params=_params(dimension_semantics=("arbitrary",)))(a, wb, h, g, dres)


def _mm_rows_t(name, a, wb, lead, out_dtype=F32, epi="plain", extra=None):
    t, n = a.shape
    r = wb.shape[-2]
    tm, tk = _tm(t), _tile(n, (1024,))
    nl = len(lead)
    b_spec = pl.BlockSpec((None,) * (1 + nl) + (r, tk), lambda i, j, k: (j,) + lead + (0, k))
    o_spec = pl.BlockSpec((tm, r), lambda i, j, k: (i, j))
    return _matmul(name, a, wb, grid=(t // tm, N_DEV, n // tk),
                   a_spec=pl.BlockSpec((tm, tk), lambda i, j, k: (i, k)), b_spec=b_spec, o_spec=o_spec,
                   out_shape=jax.ShapeDtypeStruct((t, N_DEV * r), out_dtype), contract="nt",
                   acc_shape=(tm, r), epi=epi, extra=extra, extra_spec=o_spec if extra is not None else None)


def _mm_rows_loss(name, a, wb, residual, target):
    t = a.shape[0]
    nb, r, n = wb.shape
    tm, tn = _tm_deep(t, nb * r), _tile(n, (512,))

    def body(a_ref, b_ref, r_ref, t_ref, d_ref, l_ref):
        i = pl.program_id(0)

        @pl.when((i == 0) & (pl.program_id(1) == 0))
        def _():
            l_ref[...] = jnp.zeros_like(l_ref)

        y = r_ref[...] + _dot(a_ref[...].astype(BF16), b_ref[...].reshape(nb * r, tn))
        diff = jnp.where(_row_ids(i, tm) >= TOK0, y - t_ref[...], 0.0)
        d_ref[...] = diff * (1.0 / D_MODEL)
        l_ref[...] += jnp.sum(diff * diff) * (0.5 / D_MODEL)

    o_spec = pl.BlockSpec((tm, tn), lambda i, j: (i, j))
    return _pcall(body, name=name, grid=(t // tm, n // tn),
                  in_specs=[pl.BlockSpec((tm, nb * r), lambda i, j: (i, 0)),
                            pl.BlockSpec((nb, r, tn), lambda i, j: (0, 0, j)), o_spec, o_spec],
                  out_specs=[o_spec, pl.BlockSpec((8, 128), lambda i, j: (0, 0))],
                  out_shape=[jax.ShapeDtypeStruct((t, n), F32), jax.ShapeDtypeStruct((8, 128), F32)],
                  compiler_params=_params(dimension_semantics=("arbitrary", "arbitrary")))(a, wb, residual, target)


def _mm_rows_norm(name, a, wb, residual, g):
    t = a.shape[0]
    nb, r, n = wb.shape
    tm = _tile(t, (704, 384, 128))

    def body(a_ref, b_ref, r_ref, g_ref, h_ref, hn_ref):
        h = r_ref[...] + _dot(a_ref[...].astype(BF16), b_ref[...].reshape(nb * r, n))
        h_ref[...] = h
        hn_ref[...] = (h * lax.rsqrt(jnp.mean(h * h, axis=-1, keepdims=True) + EPS) * g_ref[...]).astype(BF16)

    row = pl.BlockSpec((tm, n), lambda i: (i, 0))
    return _pcall(body, name=name, grid=(t // tm,),
                  in_specs=[pl.BlockSpec((tm, nb * r), lambda i: (i, 0)), pl.BlockSpec((nb, r, n), lambda i: (0, 0, 0)),
                            row, pl.BlockSpec((1, n), lambda i: (0, 0))],
                  out_specs=[row, row],
                  out_shape=[jax.ShapeDtypeStruct((t, n), F32), jax.ShapeDtypeStruct((t, n), BF16)],
                  compiler_params=_params(dimension_semantics=("parallel",)))(a, wb, residual, g)


def _wgrad_cols(name, x, dy, n):
    t, kdim = x.shape
    tk = _tm(t)
    return _matmul(name, x, dy, grid=(1, N_DEV, t // tk),
                   a_spec=pl.BlockSpec((tk, kdim), lambda i, j, k: (k, 0)),
                   b_spec=pl.BlockSpec((tk, n), lambda i, j, k: (k, j)),
                   o_spec=pl.BlockSpec((None, kdim, n), lambda i, j, k: (j, 0, 0)),
                   out_shape=jax.ShapeDtypeStruct((N_DEV, kdim, n), BF16), contract="tn", acc_shape=(kdim, n))


def _wgrad_rows(name, x, dy, r):
    t = x.shape[0]
    n = dy.shape[1]
    tk, tn = _tm(t), _tile(n, (512,))
    tm = min(N_DEV * r, 1024)
    out = _matmul(name, x, dy, grid=(N_DEV * r // tm, n // tn, t // tk),
                  a_spec=pl.BlockSpec((tk, tm), lambda i, j, k: (k, i)),
                  b_spec=pl.BlockSpec((tk, tn), lambda i, j, k: (k, j)),
                  o_spec=pl.BlockSpec((tm, tn), lambda i, j, k: (i, j)),
                  out_shape=jax.ShapeDtypeStruct((N_DEV * r, n), BF16), contract="tn", acc_shape=(tm, tn))
    return out.reshape(N_DEV, r, n)


def _rows(t):
    return _tile(t, (384, 128))


def _rms_fwd(name, h, g):
    t = h.shape[0]
    tr = _rows(t)

    def body(h_ref, g_ref, o_ref):
        x = h_ref[...]
        r = lax.rsqrt(jnp.mean(x * x, axis=-1, keepdims=True) + EPS)
        o_ref[...] = (x * r * g_ref[...]).astype(BF16)

    row = pl.BlockSpec((tr, D_MODEL), lambda i: (i, 0))
    vec = pl.BlockSpec((1, D_MODEL), lambda i: (0, 0))
    return _pcall(body, name=name, grid=(t // tr,), in_specs=[row, vec], out_specs=row,
                  out_shape=jax.ShapeDtypeStruct((t, D_MODEL), BF16))(h, g)


def _ret_tables(t):
    hh = np.arange(RET_HEADS, dtype=np.float64)
    log_g = np.log1p(-np.exp2(-RET_DECAY_OFFSET - hh))
    idx = np.arange(CHUNK, dtype=np.float64)
    diff = idx[:, None] - idx[None, :]
    dmat = np.where(diff[None] >= 0, np.exp(np.maximum(diff, 0.0)[None] * log_g[:, None, None]), 0.0)
    qdec = np.exp((idx + 1.0)[None, :, None] * log_g[:, None, None]) * np.ones((1, 1, CHUNK))
    kdec = np.exp((CHUNK - 1 - idx)[None, :, None] * log_g[:, None, None]) * np.ones((1, 1, CHUNK))
    half = CHUNK // 2
    inv_freq = (ROPE_BASE ** (-np.arange(half, dtype=np.float32) / half)).astype(np.float32)
    ang = (np.arange(t, dtype=np.float32)[:, None] * inv_freq[None, :]).astype(np.float32).astype(np.float64)
    cos2 = np.concatenate([np.cos(ang), np.cos(ang)], axis=1)
    sin2 = np.concatenate([-np.sin(ang), np.sin(ang)], axis=1)
    return tuple(jnp.asarray(v, F32) for v in (dmat, qdec, kdec, cos2, sin2))


def _rot(x, c, s):
    return x * c + pltpu.roll(x, CHUNK // 2, 1) * s


def _unrot(dx, c, s):
    return dx * c + pltpu.roll(dx * s, CHUNK // 2, 1)


def _dot(a, b, contract="nn"):
    return lax.dot_general(a, b, _DIMS[contract], preferred_element_type=F32)


def _ret_fwd(proj, gn_g, tables):
    t = proj.shape[0]
    nch = t // CHUNK
    dmat, qdec, kdec, cos2, sin2 = tables

    def body(qk_ref, v_ref, g_ref, w_ref, c_ref, s_ref, dm_ref, qd_ref, kd_ref, o_ref, st_ref, cat_ref, state):
        @pl.when(pl.program_id(0) == 0)
        def _():
            state[...] = jnp.zeros_like(state)

        c, s = c_ref[...], s_ref[...]
        for h in range(RET_HEADS):
            q = _rot(qk_ref[:, 128 * h:128 * (h + 1)], c, s)
            k = _rot(qk_ref[:, 512 + 128 * h:512 + 128 * (h + 1)], c, s) * RET_SCALE
            vb = v_ref[:, 256 * h:256 * (h + 1)].astype(BF16)
            st = state[h]
            st_ref[h] = st
            sc = _dot(q.astype(BF16), k.astype(BF16), "nt") * dm_ref[h]
            o = _dot(sc.astype(BF16), vb)
            o += _dot((q * qd_ref[h]).astype(BF16), st.astype(BF16))
            sl = slice(256 * h, 256 * (h + 1))
            o_ref[:, sl] = o
            kv = _dot((k * kd_ref[h]).astype(BF16), vb, "tn")
            state[h] = qd_ref[h, CHUNK - 1:CHUNK, 0:1] * st + kv
            mu = jnp.mean(o, axis=-1, keepdims=True)
            oc = o - mu
            rstd = lax.rsqrt(jnp.mean(oc * oc, axis=-1, keepdims=True) + EPS)
            g = g_ref[:, sl]
            cat_ref[:, sl] = (g * _sigmoid(g) * (oc * rstd * w_ref[:, sl])).astype(BF16)

    tab = pl.BlockSpec((RET_HEADS, CHUNK, CHUNK), lambda n: (0, 0, 0))
    pos = pl.BlockSpec((CHUNK, CHUNK), lambda n: (n, 0))
    row = pl.BlockSpec((CHUNK, 1024), lambda n: (n, 0))
    return _pcall(
        body, name="ret_fwd", grid=(nch,),
        in_specs=[row, pl.BlockSpec((CHUNK, 1024), lambda n: (n, 1)), pl.BlockSpec((CHUNK, 1024), lambda n: (n, 2)),
                  pl.BlockSpec((1, 1024), lambda n: (0, 0)), pos, pos, tab, tab, tab],
        out_specs=[row, pl.BlockSpec((RET_HEADS, None, 128, 256), lambda n: (0, n, 0, 0)), row],
        out_shape=[jax.ShapeDtypeStruct((t, 1024), F32), jax.ShapeDtypeStruct((RET_HEADS, nch, 128, 256), F32),
                   jax.ShapeDtypeStruct((t, 2048), BF16)],
        scratch_shapes=[pltpu.VMEM((RET_HEADS, 128, 256), F32)],
        compiler_params=_params(dimension_semantics=("arbitrary",)))(
            proj, proj, proj, gn_g, cos2, sin2, dmat, qdec, kdec)


def _ret_bwd(dproj, proj, states, do, tables):
    t = proj.shape[0]
    nch = t // CHUNK
    dmat, qdec, kdec, cos2, sin2 = tables

    def body(dp_in, qk_ref, v_ref, do_ref, st_ref, c_ref, s_ref, dm_ref, qd_ref, kd_ref, dp_ref, rst):
        del dp_in
        @pl.when(pl.program_id(0) == 0)
        def _():
            rst[...] = jnp.zeros_like(rst)

        c, s = c_ref[...], s_ref[...]
        for h in range(RET_HEADS):
            q = _rot(qk_ref[:, 128 * h:128 * (h + 1)], c, s)
            k = _rot(qk_ref[:, 512 + 128 * h:512 + 128 * (h + 1)], c, s) * RET_SCALE
            qb, kb = q.astype(BF16), k.astype(BF16)
            vb = v_ref[:, 256 * h:256 * (h + 1)].astype(BF16)
            dob = do_ref[:, 256 * h:256 * (h + 1)].astype(BF16)
            pb = st_ref[h].astype(BF16)
            r = rst[h]
            rb = r.astype(BF16)
            dm, qd, kd = dm_ref[h], qd_ref[h], kd_ref[h]
            sb = (_dot(qb, kb, "nt") * dm).astype(BF16)
            dsb = (_dot(dob, vb, "nt") * dm).astype(BF16)
            dq = _dot(dsb, kb) + _dot(dob, pb, "nt") * qd
            dk = _dot(dsb, qb, "tn") + _dot(vb, rb, "nt") * kd
            dv = _dot(sb, dob, "tn") + _dot((k * kd).astype(BF16), rb)
            rst[h] = _dot((q * qd).astype(BF16), dob, "tn") + qd[CHUNK - 1:CHUNK, 0:1] * r
            dp_ref[:, 128 * h:128 * (h + 1)] = _unrot(dq, c, s).astype(BF16)
            dp_ref[:, 512 + 128 * h:512 + 128 * (h + 1)] = (_unrot(dk, c, s) * RET_SCALE).astype(BF16)
            dp_ref[:, 1024 + 256 * h:1024 + 256 * (h + 1)] = dv.astype(BF16)

    rev = lambda n: nch - 1 - n
    tab = pl.BlockSpec((RET_HEADS, CHUNK, CHUNK), lambda n: (0, 0, 0))
    pos = pl.BlockSpec((CHUNK, CHUNK), lambda n: (rev(n), 0))
    row = pl.BlockSpec((CHUNK, 1024), lambda n: (rev(n), 0))
    return _pcall(
        body, name="ret_bwd", grid=(nch,),
        in_specs=[pl.BlockSpec(memory_space=pl.ANY), row, pl.BlockSpec((CHUNK, 1024), lambda n: (rev(n), 1)), row,
                  pl.BlockSpec((RET_HEADS, None, 128, 256), lambda n: (0, rev(n), 0, 0)),
                  pos, pos, tab, tab, tab],
        out_specs=pl.BlockSpec((CHUNK, 2048), lambda n: (rev(n), 0)),
        out_shape=jax.ShapeDtypeStruct((t, 5120), BF16),
        scratch_shapes=[pltpu.VMEM((RET_HEADS, 128, 256), F32)], input_output_aliases={0: 0},
        compiler_params=_params(dimension_semantics=("arbitrary",)))(
            dproj, proj, proj, do, states, cos2, sin2, dmat, qdec, kdec)


def _row_ids(i, tr):
    return i * tr + lax.broadcasted_iota(jnp.int32, (tr, 1), 0)


SH_ROWS = HALO - 8
CONV_VPU_TAPS = 21


def _shifted_copies(xs, sh, tr):
    for b in range(1, 8):
        sh[b - 1] = xs[pl.ds(b, tr + SH_ROWS), :]


def _shifted(xs, sh, off, tr, lanes=slice(None)):
    a, b = divmod(off, 8)
    return xs[pl.ds(8 * a, tr), lanes] if b == 0 else sh[b - 1, pl.ds(8 * a, tr), lanes]


def _taps_mxu(xs, sh, w_ref, offs, tr, first=0):
    sub = lax.broadcasted_iota(jnp.int32, (256, 128), 0)
    eye = (sub & 127) == lax.broadcasted_iota(jnp.int32, (256, 128), 1)
    outs = []
    for c in range(8):
        lanes = slice(128 * c, 128 * (c + 1))
        acc = None
        for w in range(first, len(offs), 2):
            wb = min(w + 1, len(offs) - 1)
            w_hi = w_ref[w:w + 1, lanes]
            w_lo = w_ref[wb:wb + 1, lanes] if wb > w else jnp.zeros((1, 128), F32)
            dmat = jnp.where(eye, jnp.where(sub < 128, w_hi, w_lo), 0.0).astype(BF16)
            lhs = jnp.concatenate([_shifted(xs, sh, offs[w], tr, lanes).astype(BF16),
                                   _shifted(xs, sh, offs[wb], tr, lanes).astype(BF16)], axis=1)
            d = _dot(lhs, dmat)
            acc = d if acc is None else acc + d
        outs.append(acc)
    return jnp.concatenate(outs, axis=1)


def _conv_fwd(cat, proj, conv_w, conv_b, ln_g, ln_b):
    t = proj.shape[0]
    tr = _rows(t)
    hb = tr // HALO

    def body(cat_in, ua_ref, ug_ref, pa_ref, pg_ref, w_ref, b_ref, lg_ref, lb_ref, c_ref, hd_ref, y_ref, xs, sh):
        del cat_in
        i = pl.program_id(0)
        hdn = ua_ref[...] * _sigmoid(ug_ref[...])
        hd_ref[...] = hdn
        prev = pa_ref[...] * _sigmoid(pg_ref[...])
        xs[0:HALO, :] = jnp.where(i > 0, prev, 0.0)
        xs[HALO:HALO + tr, :] = hdn
        _shifted_copies(xs, sh, tr)
        offs = [HALO - (CONV_WIDTH - 1) + w for w in range(CONV_WIDTH)]
        acc = _taps_mxu(xs, sh, w_ref, offs, tr, first=CONV_VPU_TAPS) + b_ref[...]
        for w in range(CONV_VPU_TAPS):
            acc += w_ref[w:w + 1, :] * _shifted(xs, sh, offs[w], tr)
        y_ref[...] = acc
        mu = jnp.mean(acc, axis=-1, keepdims=True)
        yc = acc - mu
        rstd = lax.rsqrt(jnp.mean(yc * yc, axis=-1, keepdims=True) + EPS)
        yn = yc * rstd * lg_ref[...] + lb_ref[...]
        c = yn * _sigmoid(yn)
        c_ref[...] = jnp.where(_row_ids(i, tr) >= PAD_FRONT, c, 0.0).astype(BF16)

    row = pl.BlockSpec((tr, 1024), lambda i: (i, 0))
    vec = pl.BlockSpec((1, 1024), lambda i: (0, 0))
    halo = lambda col: pl.BlockSpec((HALO, 1024), lambda i: (jnp.maximum(i * hb - 1, 0), col))
    return _pcall(body, name="conv_fwd", grid=(t // tr,),
                  in_specs=[pl.BlockSpec(memory_space=pl.ANY),
                            pl.BlockSpec((tr, 1024), lambda i: (i, 3)), pl.BlockSpec((tr, 1024), lambda i: (i, 4)),
                            halo(3), halo(4), pl.BlockSpec((32, 1024), lambda i: (0, 0)), vec, vec, vec],
                  out_specs=[pl.BlockSpec((tr, 1024), lambda i: (i, 1)), row, row],
                  out_shape=[jax.ShapeDtypeStruct((t, 2048), BF16), jax.ShapeDtypeStruct((t, 1024), F32),
                             jax.ShapeDtypeStruct((t, 1024), F32)],
                  scratch_shapes=[pltpu.VMEM((tr + HALO, 1024), F32), pltpu.VMEM((7, tr + SH_ROWS, 1024), F32)],
                  input_output_aliases={0: 0}, compiler_params=_params())(
                      cat, proj, proj, proj, proj, conv_w, conv_b, ln_g, ln_b)


def _mix_bwd_head(dh, w_out, o, proj, gn_g, y, ln_g, ln_b):
    t = dh.shape[0]
    tr = _rows(t)
    nb, r, n = w_out.shape

    def body(dh_ref, b_ref, o_ref, g_ref, w_ref, y_ref, lg_ref, lb_ref,
             do_ref, dp_ref, dw_ref, dy_ref, dlg_ref, dlb_ref, dcb_ref):
        i = pl.program_id(0)

        @pl.when(i == 0)
        def _():
            for ref in (dw_ref, dlg_ref, dlb_ref, dcb_ref):
                ref[...] = jnp.zeros_like(ref)

        dcat = _dot(dh_ref[...].astype(BF16), b_ref[...].reshape(nb * r, n), "nt")
        for h in range(RET_HEADS):
            sl = slice(256 * h, 256 * (h + 1))
            x = o_ref[:, sl]
            mu = jnp.mean(x, axis=-1, keepdims=True)
            xc = x - mu
            rstd = lax.rsqrt(jnp.mean(xc * xc, axis=-1, keepdims=True) + EPS)
            xh = xc * rstd
            w = w_ref[:, sl]
            g = g_ref[:, sl]
            sg = _sigmoid(g)
            d = dcat[:, sl]
            don = d * (g * sg)
            dp_ref[:, sl] = (d * (xh * w) * (sg * (1.0 + g * (1.0 - sg)))).astype(BF16)
            dw_ref[:, sl] += jnp.sum(don * xh, axis=0, keepdims=True)
            dxh = don * w
            m1 = jnp.mean(dxh, axis=-1, keepdims=True)
            m2 = jnp.mean(dxh * xh, axis=-1, keepdims=True)
            do_ref[:, sl] = rstd * (dxh - m1 - xh * m2)
        yv = y_ref[...]
        mu = jnp.mean(yv, axis=-1, keepdims=True)
        yc = yv - mu
        rstd = lax.rsqrt(jnp.mean(yc * yc, axis=-1, keepdims=True) + EPS)
        xh = yc * rstd
        lg = lg_ref[...]
        yn = xh * lg + lb_ref[...]
        sg = _sigmoid(yn)
        dyn = jnp.where(_row_ids(i, tr) >= PAD_FRONT, dcat[:, 1024:] * (sg * (1.0 + yn * (1.0 - sg))), 0.0)
        dlg_ref[...] += jnp.sum(dyn * xh, axis=0, keepdims=True)
        dlb_ref[...] += jnp.sum(dyn, axis=0, keepdims=True)
        dxh = dyn * lg
        m1 = jnp.mean(dxh, axis=-1, keepdims=True)
        m2 = jnp.mean(dxh * xh, axis=-1, keepdims=True)
        dy = rstd * (dxh - m1 - xh * m2)
        dy_ref[...] = dy
        dcb_ref[...] += jnp.sum(dy, axis=0, keepdims=True)

    row = pl.BlockSpec((tr, 1024), lambda i: (i, 0))
    vec = pl.BlockSpec((1, 1024), lambda i: (0, 0))
    gate = pl.BlockSpec((tr, 1024), lambda i: (i, 2))
    vsh = jax.ShapeDtypeStruct((1, 1024), F32)
    fsh = jax.ShapeDtypeStruct((t, 1024), F32)
    return _pcall(body, name="mix_bwd_head", grid=(t // tr,),
                  in_specs=[row, pl.BlockSpec((nb, r, n), lambda i: (0, 0, 0)), row, gate, vec, row, vec, vec],
                  out_specs=[row, gate, vec, row, vec, vec, vec],
                  out_shape=[fsh, jax.ShapeDtypeStruct((t, 5120), BF16), vsh, fsh, vsh, vsh, vsh],
                  compiler_params=_params(dimension_semantics=("arbitrary",)))(
                      dh, w_out, o, proj, gn_g, y, ln_g, ln_b)


def _conv_bwd_taps(dproj, dy, hdn, proj, conv_w):
    t = dy.shape[0]
    tr = _rows(t)
    hb = tr // HALO
    nt = t // tr

    def body(dp_in, dy_ref, nx_ref, hd_ref, ph_ref, ua_ref, ug_ref, w_ref, da_ref, dg_ref, dw_ref, xs, sh):
        del dp_in
        i = pl.program_id(0)

        @pl.when(i == 0)
        def _():
            dw_ref[...] = jnp.zeros_like(dw_ref)

        dy = dy_ref[...]
        xs[0:tr, :] = dy
        xs[tr:tr + HALO, :] = jnp.where(i < nt - 1, nx_ref[...], 0.0)
        _shifted_copies(xs, sh, tr)
        dh = _taps_mxu(xs, sh, w_ref, [CONV_WIDTH - 1 - w for w in range(CONV_WIDTH)], tr)
        xs[0:HALO, :] = jnp.where(i > 0, ph_ref[...], 0.0)
        xs[HALO:HALO + tr, :] = hd_ref[...]
        _shifted_copies(xs, sh, tr)
        for w in range(CONV_WIDTH):
            dw_ref[w:w + 1, :] += jnp.sum(dy * _shifted(xs, sh, HALO - (CONV_WIDTH - 1) + w, tr), axis=0, keepdims=True)
        dh = jnp.where(_row_ids(i, tr) >= PAD_FRONT, dh, 0.0)
        sg = _sigmoid(ug_ref[...])
        da_ref[...] = (dh * sg).astype(BF16)
        dg_ref[...] = (dh * ua_ref[...] * sg * (1.0 - sg)).astype(BF16)

    row = pl.BlockSpec((tr, 1024), lambda i: (i, 0))
    return _pcall(body, name="conv_bwd_taps", grid=(nt,),
                  in_specs=[pl.BlockSpec(memory_space=pl.ANY),
                            row, pl.BlockSpec((HALO, 1024), lambda i: (jnp.minimum((i + 1) * hb, nt * hb - 1), 0)),
                            row, pl.BlockSpec((HALO, 1024), lambda i: (jnp.maximum(i * hb - 1, 0), 0)),
                            pl.BlockSpec((tr, 1024), lambda i: (i, 3)), pl.BlockSpec((tr, 1024), lambda i: (i, 4)),
                            pl.BlockSpec((32, 1024), lambda i: (0, 0))],
                  out_specs=[pl.BlockSpec((tr, 1024), lambda i: (i, 3)), row, pl.BlockSpec((32, 1024), lambda i: (0, 0))],
                  out_shape=[jax.ShapeDtypeStruct((t, 5120), BF16), jax.ShapeDtypeStruct((t, 1024), BF16),
                             jax.ShapeDtypeStruct((32, 1024), F32)],
                  scratch_shapes=[pltpu.VMEM((tr + HALO, 1024), F32), pltpu.VMEM((7, tr + SH_ROWS, 1024), F32)],
                  input_output_aliases={0: 0}, compiler_params=_params())(
                      dproj, dy, dy, hdn, hdn, proj, proj, conv_w)


NEG_BIG = -1e30


def _seg_tables(qb):
    j = np.arange(128)
    bd = (j[:, None] // 64 == j[None, :] // 64).astype(np.float32)
    ones = np.ones((128, 128), np.float32)
    later = np.concatenate([(j[:, None] >= j[None, :]).astype(np.float32), ones], axis=1)
    earlier = np.concatenate([(j[:, None] < j[None, :]).astype(np.float32), ones], axis=1)
    per = qb // CHUNK
    row = np.arange(qb)[:, None]
    pad = np.broadcast_to(j[None, :] < PAD_FRONT, (qb, 128))
    diag = [(g * CHUNK + j[None, :]) >= row for g in range(per)]
    masks = diag + [np.zeros((qb, 128), bool), pad, diag[0] | pad]
    bias = np.stack([np.where(m, NEG_BIG, 0.0) for m in masks]).astype(np.float32)
    dup = lambda m: np.concatenate([m, m], axis=0)
    return (jnp.asarray(bd, BF16), jnp.asarray(dup(later), BF16), jnp.asarray(dup(earlier), BF16),
            jnp.asarray(bias, F32))


def _split_dot(x, m):
    hi = x.astype(BF16)
    lo = (x - hi.astype(F32)).astype(BF16)
    return _dot(hi, m) + _dot(lo, m)


def _qk_norm_fwd(qkv, qg, kg, bd):
    t = qkv.shape[0]
    tr = _rows(t)
    nb = tr // CHUNK

    def body(q_ref, k_ref, v_ref, qg_ref, kg_ref, bd_ref, qo, kt, k2, vt, v2):
        bdm = bd_ref[...]
        lane = lax.broadcasted_iota(jnp.int32, (1, 128), 1)
        sub = lax.broadcasted_iota(jnp.int32, (128, 1), 0)

        def pair_layouts(x, t_ref, s_ref, hp, b):
            xt = x.T
            t_ref[hp, b] = jnp.concatenate([jnp.where(sub < 64, xt, 0.0), jnp.where(sub >= 64, xt, 0.0)],
                                           axis=1).astype(BF16)
            s_ref[hp, b] = jnp.concatenate([jnp.where(lane < 64, x, 0.0), jnp.where(lane >= 64, x, 0.0)],
                                           axis=0).astype(BF16)

        for hp in range(8):
            sl = slice(128 * hp, 128 * (hp + 1))
            x = q_ref[:, sl]
            r = lax.rsqrt(_split_dot(x * x, bdm) * (1.0 / 64) + EPS)
            qo[:, sl] = (x * r * (qg_ref[:, sl] * SB_SCALE)).astype(BF16)
            x = k_ref[:, sl]
            r = lax.rsqrt(_split_dot(x * x, bdm) * (1.0 / 64) + EPS)
            kn = x * r * kg_ref[:, sl]
            v = v_ref[:, sl]
            for b in range(nb):
                rows = slice(CHUNK * b, CHUNK * (b + 1))
                pair_layouts(kn[rows], kt, k2, hp, b)
                pair_layouts(v[rows], vt, v2, hp, b)

    col = lambda c: pl.BlockSpec((tr, 1024), lambda i: (i, c))
    vec = pl.BlockSpec((1, 1024), lambda i: (0, 0))
    wide = pl.BlockSpec((8, nb, 128, 256), lambda i: (0, i, 0, 0))
    tall = pl.BlockSpec((8, nb, 256, 128), lambda i: (0, i, 0, 0))
    wsh = jax.ShapeDtypeStruct((8, t // CHUNK, 128, 256), BF16)
    tsh = jax.ShapeDtypeStruct((8, t // CHUNK, 256, 128), BF16)
    return _pcall(body, name="qk_norm_fwd", grid=(t // tr,),
                  in_specs=[col(0), col(1), col(2), vec, vec, pl.BlockSpec((128, 128), lambda i: (0, 0))],
                  out_specs=[col(0), wide, tall, wide, tall],
                  out_shape=[jax.ShapeDtypeStruct((t, 1024), BF16), wsh, tsh, wsh, tsh])(qkv, qkv, qkv, qg, kg, bd)


def _qk_norm_bwd(qkv, dq, dk, dv, qg, kg, bd):
    t = qkv.shape[0]
    tr = _rows(t)

    def body(q_ref, k_ref, dq_ref, dk_ref, dv_ref, qg_ref, kg_ref, bd_ref, o_ref, dqg_ref, dkg_ref):
        @pl.when(pl.program_id(0) == 0)
        def _():
            dqg_ref[...] = jnp.zeros_like(dqg_ref)
            dkg_ref[...] = jnp.zeros_like(dkg_ref)

        bdm = bd_ref[...]
        for part, (src, d_ref, g_ref, dg_ref) in enumerate(((q_ref, dq_ref, qg_ref, dqg_ref),
                                                           (k_ref, dk_ref, kg_ref, dkg_ref))):
            for cix in range(8):
                sl = slice(128 * cix, 128 * (cix + 1))
                x = src[:, sl]
                d = d_ref[:, sl]
                r = lax.rsqrt(_split_dot(x * x, bdm) * (1.0 / 64) + EPS)
                u = d * g_ref[:, sl]
                m = _split_dot(u * x, bdm) * (1.0 / 64)
                o_ref[:, 1024 * part + 128 * cix:1024 * part + 128 * (cix + 1)] = (r * u - x * (r * r * r * m)).astype(BF16)
                dg_ref[:, sl] += jnp.sum(d * x * r, axis=0, keepdims=True)
        o_ref[:, 2048:3072] = dv_ref[...].astype(BF16)

    col = lambda c: pl.BlockSpec((tr, 1024), lambda i: (i, c))
    vec = pl.BlockSpec((1, 1024), lambda i: (0, 0))
    vsh = jax.ShapeDtypeStruct((1, 1024), F32)
    return _pcall(body, name="qk_norm_bwd", grid=(t // tr,),
                  in_specs=[col(0), col(1), col(0), col(0), col(0), vec, vec, pl.BlockSpec((128, 128), lambda i: (0, 0))],
                  out_specs=[pl.BlockSpec((tr, 3072), lambda i: (i, 0)), vec, vec],
                  out_shape=[jax.ShapeDtypeStruct((t, 3072), BF16), vsh, vsh])(qkv, qkv, dq, dk, dv, qg, kg, bd)


def _split2(x):
    hi = x.astype(BF16)
    lo = (x - hi.astype(F32)).astype(BF16)
    return jnp.concatenate([hi, lo], axis=1)


def _sb_scores(z, later_tab):
    e = jnp.exp(-jnp.abs(z))
    ope = 1.0 + e
    sp = jnp.maximum(z, 0.0) + jnp.log(ope)
    return e, ope, _dot(_split2(sp), later_tab)


def _sb_bias_index(i, kb, per):
    g = kb - i * per
    return jnp.where(kb == 0, jnp.where(i == 0, per + 2, per + 1), jnp.where(g >= 0, g, per))


def _sb_qb(t):
    return _tile(t, (384, 128))


def _sb_fwd(qh, kt, v2, later_tab, bias_tab):
    t = qh.shape[0]
    qb = _sb_qb(t)
    per = qb // CHUNK
    nkb_all = t // CHUNK

    nq = t // qb

    def body(q_ref, kt_ref, v2_ref, tab_ref, bias_ref, o_ref, ws_ref, acc, carry, zbuf, wbuf, wsem):
        h, i = pl.program_id(0), pl.program_id(1)
        n = h * nq + i
        p = n & 1
        q = q_ref[...]
        acc[...] = jnp.zeros_like(acc)
        carry[...] = jnp.zeros_like(carry)
        nkb = (i + 1) * per
        save = lambda kb: pltpu.make_async_copy(wbuf.at[p, kb], ws_ref.at[h, i, kb], wsem.at[p, kb])

        def drain(step, par):
            hs, is_ = step // nq, step % nq

            def one(kb, _):
                pltpu.make_async_copy(wbuf.at[par, kb], ws_ref.at[hs, is_, kb], wsem.at[par, kb]).wait()
                return 0

            lax.fori_loop(0, (is_ + 1) * per, one, 0)

        @pl.when(n >= 2)
        def _():
            drain(n - 2, p)

        for u in range(per):
            zbuf[u] = _dot(q, kt_ref[nkb - 1 - u])

        def trip(s, diagonal):
            top = nkb - 1 - per * s
            if not diagonal:
                for u in range(per):
                    save(top + per - u).start()
            z2s = [zbuf[u] for u in range(per)]
            for u in range(per):
                zbuf[u] = _dot(q, kt_ref[jnp.maximum(top - per - u, 0)])
            first = [CHUNK * (per - 1 - u) if diagonal else 0 for u in range(per)]
            cins = [carry[0], carry[1]]
            zs, cus = [], []
            for u in range(per):
                bias = bias_ref[_sb_bias_index(i, top - u, per)][first[u]:]
                zs.append([z2s[u][first[u]:, 128 * hh:128 * (hh + 1)] + bias for hh in range(2)])
                cus.append([_sb_scores(z, tab_ref[...])[2] for z in zs[u]])
            part = None
            for u in range(per):
                kb, lo = top - u, first[u]
                for hh in range(2):
                    sl = slice(128 * hh, 128 * (hh + 1))
                    cu = cus[u][hh]
                    wbuf[p, kb, lo:, sl] = jnp.exp(zs[u][hh] - cu[:, :128] - cins[hh][lo:]).astype(BF16)
                    if lo:
                        wbuf[p, kb, :lo, sl] = jnp.zeros((lo, 128), BF16)
                        cins[hh] = jnp.concatenate([cins[hh][:lo], cins[hh][lo:] + cu[:, 128:]], axis=0)
                    else:
                        cins[hh] = cins[hh] + cu[:, 128:]
                d = _dot(wbuf[p, kb], v2_ref[kb])
                part = d if part is None else part + d
            carry[0], carry[1] = cins[0], cins[1]
            acc[...] += part

        trip(0, True)

        def step(s, _):
            trip(s, False)
            return 0

        lax.fori_loop(1, nkb // per, step, 0)
        for u in range(per):
            save(per - 1 - u).start()
        o_ref[...] = acc[...]

        @pl.when(n == 8 * nq - 1)
        def _():
            drain(n - 1, 1 - p)
            drain(n, p)

    blk = pl.BlockSpec((qb, 128), lambda h, i: (i, h))
    wide = pl.BlockSpec((None, nkb_all, 128, 256), lambda h, i: (h, 0, 0, 0))
    tall = pl.BlockSpec((None, nkb_all, 256, 128), lambda h, i: (h, 0, 0, 0))
    return _pcall(body, name="sb_fwd", grid=(8, t // qb),
                  in_specs=[blk, wide, tall, pl.BlockSpec((256, 256), lambda h, i: (0, 0)),
                            pl.BlockSpec((per + 3, qb, 128), lambda h, i: (0, 0, 0))],
                  out_specs=[blk, pl.BlockSpec(memory_space=pl.ANY)],
                  out_shape=[jax.ShapeDtypeStruct((t, 1024), F32),
                             jax.ShapeDtypeStruct((8, t // qb, nkb_all, qb, 256), BF16)],
                  scratch_shapes=[pltpu.VMEM((qb, 128), F32), pltpu.VMEM((2, qb, 128), F32),
                                  pltpu.VMEM((per, qb, 256), F32), pltpu.VMEM((2, nkb_all, qb, 256), BF16),
                                  pltpu.SemaphoreType.DMA((2, nkb_all))],
                  compiler_params=_params(dimension_semantics=("arbitrary", "arbitrary")))(
                      qh, kt, v2, later_tab, bias_tab)


def _sb_bwd(qh, kt, k2, vt, wsave, do, earlier_tab, bias_tab):
    t = qh.shape[0]
    qb = _sb_qb(t)
    per = qb // CHUNK
    nkb_all = t // CHUNK

    zero_slot = nkb_all
    nq = t // qb

    def body(q_ref, kt_ref, k2_ref, vt_ref, ws_ref, do_ref, etab_ref, bias_ref,
             dq_ref, dk_ref, dv_ref, acc, gcarry, zbuf, dwbuf, wbuf, wsem, dzbuf):
        h, i = pl.program_id(0), pl.program_id(1)
        n = h * nq + i
        p = n & 1

        @pl.when(i == 0)
        def _():
            dk_ref[...] = jnp.zeros_like(dk_ref)
            dv_ref[...] = jnp.zeros_like(dv_ref)

        nkb = (i + 1) * per
        fetch = lambda kb: pltpu.make_async_copy(ws_ref.at[h, i, kb], wbuf.at[p, kb], wsem.at[p, kb])

        def prefetch(step, par):
            hs, is_ = step // nq, step % nq

            def one(kb, _):
                pltpu.make_async_copy(ws_ref.at[hs, is_, kb], wbuf.at[par, kb], wsem.at[par, kb]).start()
                return 0

            lax.fori_loop(0, (is_ + 1) * per, one, 0)

        @pl.when(n == 0)
        def _():
            prefetch(n, p)

        @pl.when(n + 1 < 8 * nq)
        def _():
            prefetch(n + 1, 1 - p)

        q = q_ref[...]
        dob = do_ref[...].astype(BF16)
        lane = lax.broadcasted_iota(jnp.int32, (1, 128), 1)
        acc[...] = jnp.zeros_like(acc)
        gcarry[...] = jnp.zeros_like(gcarry)
        zbuf[...] = _dot(q, kt_ref[0])
        dwbuf[...] = _dot(dob, vt_ref[0])
        dzbuf[...] = jnp.zeros_like(dzbuf)
        wbuf[p, zero_slot] = jnp.zeros((qb, 256), BF16)

        q_t = q.astype(F32).T.astype(BF16)
        do_t = do_ref[...].T.astype(BF16)
        sub = lax.broadcasted_iota(jnp.int32, (128, 1), 0)

        def gradients(slot, kb):
            dz2 = dzbuf[...]
            acc[...] += _dot(dz2, k2_ref[kb])
            dk2 = _dot(q_t, dz2)
            dv2 = _dot(do_t, wbuf[p, slot])
            dk_ref[kb] += jnp.where(sub < 64, dk2[:, :128], dk2[:, 128:])
            dv_ref[kb] += jnp.where(sub < 64, dv2[:, :128], dv2[:, 128:])

        def trip(kb, lo):
            fetch(kb).wait()
            bias = bias_ref[_sb_bias_index(i, kb, per)][lo:]
            z2 = zbuf[...]
            dw2 = dwbuf[...]
            nxt = jnp.minimum(kb + 1, nkb - 1)
            zbuf[...] = _dot(q, kt_ref[nxt])
            dwbuf[...] = _dot(dob, vt_ref[nxt])
            gradients(jnp.where(kb == 0, zero_slot, kb - 1), jnp.maximum(kb - 1, 0))
            w2 = wbuf[p, kb]
            for hh in range(2):
                sl = slice(128 * hh, 128 * (hh + 1))
                z = z2[lo:, sl] + bias
                e = jnp.exp(-jnp.abs(z))
                r = 1.0 / (1.0 + e)
                sig = jnp.where(z >= 0, r, e * r)
                gw = w2[lo:, sl].astype(F32) * dw2[lo:, sl]
                cu2 = _dot(_split2(gw), etab_ref[...])
                gin = gcarry[hh, lo:, :]
                gcarry[hh, lo:, :] = gin + cu2[:, 128:]
                dzbuf[lo:, sl] = (gw - sig * (gw + cu2[:, :128] + gin)).astype(BF16)
                if lo:
                    dzbuf[:lo, sl] = jnp.zeros((lo, 128), BF16)

        def step(kb, _):
            trip(kb, 0)
            return 0

        lax.fori_loop(0, nkb - per, step, 0)
        for g in range(per):
            trip(nkb - per + g, CHUNK * g)
        gradients(nkb - 1, nkb - 1)
        dq_ref[...] = acc[...] * SB_SCALE

        @pl.when(i == nq - 1)
        def _():
            def untranspose(kb, _):
                dk_ref[kb] = dk_ref[kb].T
                dv_ref[kb] = dv_ref[kb].T
                return 0

            lax.fori_loop(0, nkb_all, untranspose, 0)

    blk = pl.BlockSpec((qb, 128), lambda h, i: (i, h))
    wide = pl.BlockSpec((None, nkb_all, 128, 256), lambda h, i: (h, 0, 0, 0))
    tall = pl.BlockSpec((None, nkb_all, 256, 128), lambda h, i: (h, 0, 0, 0))
    tab = pl.BlockSpec((256, 256), lambda h, i: (0, 0))
    kv_out = pl.BlockSpec((nkb_all, 128, 128), lambda h, i: (0, 0, h))
    ksh = jax.ShapeDtypeStruct((nkb_all, 128, 1024), F32)
    dq, dk, dv = _pcall(
        body, name="sb_bwd", grid=(8, t // qb),
        in_specs=[blk, wide, tall, wide, pl.BlockSpec(memory_space=pl.ANY), blk, tab,
                  pl.BlockSpec((per + 3, qb, 128), lambda h, i: (0, 0, 0))],
        out_specs=[blk, kv_out, kv_out], out_shape=[jax.ShapeDtypeStruct((t, 1024), F32), ksh, ksh],
        scratch_shapes=[pltpu.VMEM((qb, 128), F32), pltpu.VMEM((2, qb, 128), F32),
                        pltpu.VMEM((qb, 256), F32), pltpu.VMEM((qb, 256), F32),
                        pltpu.VMEM((2, nkb_all + 1, qb, 256), BF16), pltpu.SemaphoreType.DMA((2, nkb_all)),
                        pltpu.VMEM((qb, 256), BF16)],
        compiler_params=_params(dimension_semantics=("arbitrary", "arbitrary")))(
            qh, kt, k2, vt, wsave, do, earlier_tab, bias_tab)
    return dq, dk.reshape(t, 1024), dv.reshape(t, 1024)


def _adamw_math(w, g, m, v):
    m = ADAM_B1 * m + (1.0 - ADAM_B1) * g
    v = ADAM_B2 * v + (1.0 - ADAM_B2) * (g * g)
    m_hat = m / (1.0 - ADAM_B1 ** ADAM_STEP)
    v_hat = v / (1.0 - ADAM_B2 ** ADAM_STEP)
    delta = -ADAM_LR * (m_hat / (jnp.sqrt(v_hat) + ADAM_EPS) + ADAM_WD * w)
    return delta, m, v


def _adamw(name, w, owns, recvs, m, v, me):
    shape = w.shape
    c = shape[-1]
    nl = len(owns)
    w3, m3, v3 = (a.reshape(nl, -1, c) for a in (w, m, v))
    r = w3.shape[1]
    tr = _tile(r, (256, 128))
    owns = [o.reshape(N_DEV, r, c) for o in owns]
    recvs = [p.reshape(N_DEV - 1, r, c) for p in recvs]

    def body(me_ref, w_ref, *rest):
        own_refs, recv_refs = rest[:nl], rest[nl:2 * nl]
        m_ref, v_ref = rest[2 * nl:2 * nl + 2]
        g_out, d_out, m_out, v_out = rest[2 * nl + 2:]
        layer = pl.program_id(0)

        def grad(k):
            g = own_refs[k][...].astype(F32)
            for s in range(N_DEV - 1):
                g = g + recv_refs[k][s].astype(F32)
            return g

        g = grad(0)
        for k in range(1, nl):
            g = jnp.where(layer == k, grad(k), g)
        d, mn, vn = _adamw_math(w_ref[...], g, m_ref[...], v_ref[...])
        g_out[...] = g
        d_out[...] = d
        m_out[...] = mn
        v_out[...] = vn

    row = pl.BlockSpec((None, tr, c), lambda l, i, me_ref: (l, i, 0))
    own = lambda k: pl.BlockSpec((None, tr, c), lambda l, i, me_ref: (me_ref[0], jnp.where(l == k, i, 0), 0))
    rcv = lambda k: pl.BlockSpec((N_DEV - 1, tr, c), lambda l, i, me_ref: (0, jnp.where(l == k, i, 0), 0))
    osh = jax.ShapeDtypeStruct((nl, r, c), F32)
    grid_spec = pltpu.PrefetchScalarGridSpec(
        num_scalar_prefetch=1, grid=(nl, r // tr),
        in_specs=[row] + [own(k) for k in range(nl)] + [rcv(k) for k in range(nl)] + [row, row],
        out_specs=[row, row, row, row])
    outs = _pcall(body, name=name, grid_spec=grid_spec, out_shape=[osh, osh, osh, osh])(
        me.reshape(1), w3, *owns, *recvs, m3, v3)
    return tuple(o.reshape(shape) for o in outs)


def _place():
    x, y, c = lax.axis_index("x"), lax.axis_index("y"), lax.axis_index("c")
    return x, y, c, 4 * x + 2 * y + c


def _peer(x, y, c, rel):
    return (x ^ ((rel >> 2) & 1), y ^ ((rel >> 1) & 1), c ^ (rel & 1))


def _gather_first(now, later):
    n, k = len(now), len(later)

    def body(*refs):
        ins, outs = refs[:n + k], refs[n + k:2 * (n + k)]
        send, recv, lsem = refs[2 * (n + k):]
        x, y, c, me = _place()
        locals_ = []
        for w in range(n + k):
            local = pltpu.make_async_copy(ins[w], outs[w].at[me], lsem.at[w])
            local.start()
            locals_.append(local)
        def copy(w, src, slot, rel, to_rel):
            return pltpu.make_async_remote_copy(src_ref=src, dst_ref=outs[w].at[slot], send_sem=send.at[w, rel - 1],
                                                recv_sem=recv.at[w, rel - 1], device_id=_peer(x, y, c, to_rel),
                                                device_id_type=MESH)

        for w in range(n):
            for rel in (1, 2, 4, 6):
                copy(w, ins[w], me, rel, rel).start()
        for w in range(n):
            for rel in (2, 4, 6):
                copy(w, ins[w], me ^ rel, rel, rel).wait_recv()
                copy(w, outs[w].at[me ^ rel], me ^ rel, rel | 1, 1).start()
        for w in range(n):
            for rel in (1, 3, 5, 7):
                copy(w, ins[w], me ^ rel, rel, 1).wait_recv()
            for rel in range(1, N_DEV):
                copy(w, ins[w], me, rel, rel).wait_send()
        for local in locals_:
            local.wait()

    hbm = pl.BlockSpec(memory_space=pl.ANY)
    vmem = pl.BlockSpec(memory_space=pltpu.VMEM)
    arrays = list(now) + list(later)
    return _pcall(body, name="gather_first", in_specs=[vmem] * (n + k), out_specs=[hbm] * (n + k),
                  out_shape=[jax.ShapeDtypeStruct((N_DEV,) + a.shape, a.dtype) for a in arrays],
                  scratch_shapes=[pltpu.SemaphoreType.DMA((n, N_DEV - 1)), pltpu.SemaphoreType.DMA((n, N_DEV - 1)),
                                  pltpu.SemaphoreType.DMA((n + k,))],
                  compiler_params=_params(has_side_effects=True))(*arrays)


_HBM = pl.BlockSpec(memory_space=pltpu.HBM)
_SEM = pl.BlockSpec(memory_space=pltpu.SEMAPHORE)
_DATAFLOW = pltpu.SideEffectType.DATAFLOW_SIDE_EFFECTING


def _exchange_refs(srcs, lands, mode, me, rel, j):
    if mode == "gather":
        return srcs[j], lands[j].at[me], lands[j].at[me ^ rel]
    return srcs[j].at[me ^ rel], lands[j].at[rel - 1], lands[j].at[rel - 1]


def _exchange_start(name, srcs, lands, mode):
    n = len(srcs)

    def body(*refs):
        ins, lnd = refs[:n], refs[n:2 * n]
        send, recv = refs[2 * n], refs[2 * n + 1]
        token = refs[-1]
        x, y, c, me = _place()
        for j in range(n):
            for rel in range(1, N_DEV):
                src, dst, _ = _exchange_refs(ins, lnd, mode, me, rel, j)
                pltpu.make_async_remote_copy(src_ref=src, dst_ref=dst, send_sem=send.at[j * (N_DEV - 1) + rel - 1],
                                             recv_sem=recv.at[j * (N_DEV - 1) + rel - 1],
                                             device_id=_peer(x, y, c, rel), device_id_type=MESH).start()
        token[...] = jnp.zeros_like(token)

    sems = pltpu.SemaphoreType.DMA((n * (N_DEV - 1),))
    hbm_like = lambda a: pltpu.HBM(a.shape, a.dtype)
    outs = _pcall(body, name=name + "_start",
                  in_specs=[_HBM] * (2 * n), out_specs=[_SEM, _SEM] + [_HBM] * (2 * n) + [pl.BlockSpec(memory_space=pltpu.VMEM)],
                  out_shape=[sems, sems] + [hbm_like(a) for a in srcs] + [hbm_like(a) for a in lands]
                  + [jax.ShapeDtypeStruct((8, 128), F32)],
                  input_output_aliases={i: 2 + i for i in range(2 * n)},
                  compiler_params=pltpu.CompilerParams(has_side_effects=_DATAFLOW))(
                      *[pltpu.with_memory_space_constraint(a, pltpu.HBM) for a in list(srcs) + list(lands)])
    return dict(name=name, mode=mode, n=n, send=outs[0], recv=outs[1], srcs=outs[2:2 + n], lands=outs[2 + n:2 + 2 * n],
                token=outs[-1][0, 0])


def _exchange_wait(ex, after):
    n, mode = ex["n"], ex["mode"]

    def body(*refs):
        ins, lnd = refs[:n], refs[n:2 * n]
        send, recv = refs[2 * n], refs[2 * n + 1]
        x, y, c, me = _place()
        for j in range(n):
            for rel in range(1, N_DEV):
                src, dst, landed = _exchange_refs(ins, lnd, mode, me, rel, j)
                pltpu.make_async_remote_copy(src_ref=src, dst_ref=dst, send_sem=send.at[j * (N_DEV - 1) + rel - 1],
                                             recv_sem=recv.at[j * (N_DEV - 1) + rel - 1],
                                             device_id=_peer(x, y, c, rel), device_id_type=MESH).wait_send()
                pltpu.make_async_remote_copy(src_ref=src, dst_ref=landed, send_sem=send.at[j * (N_DEV - 1) + rel - 1],
                                             recv_sem=recv.at[j * (N_DEV - 1) + rel - 1],
                                             device_id=_peer(x, y, c, rel), device_id_type=MESH).wait_recv()

    hbm_like = lambda a: pltpu.HBM(a.shape, a.dtype)
    arrays = list(ex["srcs"]) + list(ex["lands"])
    outs = _pcall(body, name=ex["name"] + "_wait",
                  in_specs=[_HBM] * (2 * n) + [_SEM, _SEM, pl.BlockSpec(memory_space=pl.ANY)],
                  out_specs=[_HBM] * (2 * n), out_shape=[hbm_like(a) for a in arrays],
                  input_output_aliases={i: i for i in range(2 * n)},
                  compiler_params=pltpu.CompilerParams(has_side_effects=_DATAFLOW))(
                      *arrays, ex["send"], ex["recv"], after)
    return outs[:n], outs[n:]


def _scatter_start(name, grads):
    lands = [lax.empty((N_DEV - 1,) + g.shape[1:], g.dtype) for g in grads]
    return _exchange_start(name, grads, lands, "scatter")


ROW_MIX, ROW_MLP, ROW_CB, ROW_LG, ROW_LB, ROW_QN, ROW_KN, ROW_LOSS = 0, 2, 4, 5, 6, 7, 8, 9
ROW_META, ROW_CW, ROW_GN, SMALL_ROWS = 16, 32, 64, 72


def _sum_small(slots):
    def body(s_ref, o_ref):
        tot = s_ref[0]
        for s in range(1, N_DEV):
            tot = tot + s_ref[s]
        o_ref[...] = tot
        for row in (ROW_QN, ROW_KN):
            v = tot[row:row + 1, :]
            f = v[:, 0:128]
            for k in range(1, 8):
                f = f + v[:, 128 * k:128 * (k + 1)]
            o_ref[row:row + 1, 0:64] = f[:, 0:64] + f[:, 64:128]

    return _pcall(body, name="sum_small", out_shape=jax.ShapeDtypeStruct(slots.shape[1:], F32))(slots)


def _adamw_small(w, g, m, v):
    def body(w_ref, g_ref, m_ref, v_ref, d_out, m_out, v_out):
        d, mn, vn = _adamw_math(w_ref[...], g_ref[...], m_ref[...], v_ref[...])
        d_out[...] = d
        m_out[...] = mn
        v_out[...] = vn

    osh = jax.ShapeDtypeStruct(w.shape, F32)
    return _pcall(body, name="adamw_small", out_shape=[osh, osh, osh])(w, g, m, v)


def _local_step(h0, target, p, weight, emit):
    t = h0.shape[0]
    tables = _ret_tables(t)
    bd, later_tab, earlier_tab, bias_tab = _seg_tables(_sb_qb(t))
    row = lambda a, i: a[i:i + 1]

    hn_a = _rms_fwd("rms_mix0", h0, row(p["norm_mix_g"], 0))
    w_in = weight("w_in", hn_a)
    proj = _mm_cols("proj_in", hn_a, w_in, ())
    gn_flat = p["gn_g"].reshape(1, 1024)
    o_ret, states, cat = _ret_fwd(proj, gn_flat, tables)
    cat, hdn, ycv = _conv_fwd(cat, proj, p["conv_w"], p["conv_b"], p["ln_g"], p["ln_b"])
    w_out = weight("w_out", cat)
    h1, hn_b = _mm_rows_norm("mix_out", cat, w_out, h0, row(p["norm_mlp_g"], 0))
    w1_0, w2_0 = weight("w1_0", hn_b), weight("w2_0", hn_b)
    a0, s0 = _mm_cols("mlp0_up", hn_b, w1_0, (), epi="relu2")
    h2, hn_c = _mm_rows_norm("mlp0_down", s0, w2_0, h1, row(p["norm_mix_g"], 1))

    w_qkv = weight("w_qkv", hn_c)
    qkv = _mm_cols("qkv", hn_c, w_qkv, ())
    qg = jnp.tile(p["qn_g"], (1, 16))
    kg = jnp.tile(p["kn_g"], (1, 16))
    qh, kt, k2, vt, v2 = _qk_norm_fwd(qkv, qg, kg, bd)
    o_sb, w_sb = _sb_fwd(qh, kt, v2, later_tab, bias_tab)
    w_o = weight("w_o", o_sb)
    h3, hn_d = _mm_rows_norm("attn_out", o_sb, w_o, h2, row(p["norm_mlp_g"], 1))
    w1_1, w2_1 = weight("w1_1", hn_d), weight("w2_1", hn_d)
    a1, s1 = _mm_cols("mlp1_up", hn_d, w1_1, (), epi="relu2")
    dh, loss = _mm_rows_loss("mlp1_down", s1, w2_1, h3, target)

    def mlp_bwd(tag, layer, w1, w2, dh, h_in, hn, a, s):
        da = _mm_rows_t(f"{tag}_dact", dh, w2, (), out_dtype=BF16, epi="drelu2", extra=a)
        dw2 = _wgrad_rows(f"{tag}_dw2", s, dh, 512)
        dw1 = _wgrad_cols(f"{tag}_dw1", hn, da, 512)
        tok = emit(tag, [dw1, dw2])
        return _mm_cols_t_rms(f"{tag}_dhn", da, w1, h_in, row(p["norm_mlp_g"], layer) + tok, dh)

    dh, dg_mlp1 = mlp_bwd("mlp1", 1, w1_1, w2_1, dh, h3, hn_d, a1, s1)

    do_sb = _mm_rows_t("attn_dout", dh, w_o, ())
    dw_o = _wgrad_rows("attn_dwo", o_sb, dh, 128)
    dq, dk, dv = _sb_bwd(qh, kt, k2, vt, w_sb, do_sb, earlier_tab, bias_tab)
    dqkv, dqg, dkg = _qk_norm_bwd(qkv, dq, dk, dv, qg, kg, bd)
    dw_qkv = _wgrad_cols("qkv_dw", hn_c, dqkv, 384)
    tok = emit("attn", [dw_qkv, dw_o])
    dh, dg_mix1 = _mm_cols_t_rms("qkv_dhn", dqkv, w_qkv, h2, row(p["norm_mix_g"], 1) + tok, dh)

    dh, dg_mlp0 = mlp_bwd("mlp0", 0, w1_0, w2_0, dh, h1, hn_b, a0, s0)

    dw_out = _wgrad_rows("mix_dwout", cat, dh, 256)
    tok = emit("mix0_out", [dw_out])
    do_ret, dproj, dgn, dy, dlg, dlb, dcb = _mix_bwd_head(dh, w_out, o_ret, proj, gn_flat + tok, ycv,
                                                          p["ln_g"], p["ln_b"])
    dproj = _ret_bwd(dproj, proj, states, do_ret, tables)
    dproj, dug, dcw = _conv_bwd_taps(dproj, dy, hdn, proj, p["conv_w"])
    dproj = lax.dynamic_update_slice(dproj, dug, (0, 4096))
    dw_in = _wgrad_cols("proj_dw", hn_a, dproj, 640)
    tok = emit("mix0", [dw_in])
    dh, dg_mix0 = _mm_cols_t_rms("proj_dhn", dproj, w_in, h0, row(p["norm_mix_g"], 0) + tok, dh)

    rid = lax.broadcasted_iota(jnp.int32, (16, 1), 0)
    loss_row = jnp.broadcast_to(loss[0:1, 0:1], (1, D_MODEL))
    vecs = sum(jnp.where(rid == k, v, 0.0)
               for k, v in enumerate((dg_mix0, dg_mix1, dg_mlp0, dg_mlp1, dcb, dlg, dlb, dqg, dkg, loss_row)))
    small = jnp.concatenate([vecs, dh[PAD_FRONT:TOK0], dcw, jnp.where(rid[:8] == 0, dgn, 0.0)], axis=0)
    return dh[TOK0:], small


_SMALL_NAMES = ("meta", "norm_mix_g", "norm_mlp_g", "even_ret_gn_g", "even_conv_w", "even_conv_b",
                "even_conv_ln_g", "even_conv_ln_b", "odd_q_norm_g", "odd_k_norm_g")
_BIG_NAMES = ("even_w_in", "even_w_out", "odd_w_qkv", "odd_w_o", "mlp_w1", "mlp_w2")
_ORDER = ("meta", "norm_mix_g", "norm_mlp_g", "even_w_in", "even_ret_gn_g", "even_conv_w", "even_conv_b",
          "even_conv_ln_g", "even_conv_ln_b", "even_w_out", "odd_w_qkv", "odd_q_norm_g", "odd_k_norm_g",
          "odd_w_o", "mlp_w1", "mlp_w2")


def _pack128(a):
    flat = a.reshape(-1)
    n = flat.shape[0]
    rows = -(-n // 128)
    rows8 = -(-rows // 8) * 8
    return jnp.pad(flat, (0, rows8 * 128 - n)).reshape(rows8, 128)


def kernel(x, meta, norm_mix_g, norm_mlp_g, even_w_in, even_ret_gn_g, even_conv_w, even_conv_b, even_conv_ln_g, even_conv_ln_b, even_w_out, odd_w_qkv, odd_q_norm_g, odd_k_norm_g, odd_w_o, mlp_w1, mlp_w2, loss_target, m_meta, m_norm_mix_g, m_norm_mlp_g, m_even_w_in, m_even_ret_gn_g, m_even_conv_w, m_even_conv_b, m_even_conv_ln_g, m_even_conv_ln_b, m_even_w_out, m_odd_w_qkv, m_odd_q_norm_g, m_odd_k_norm_g, m_odd_w_o, m_mlp_w1, m_mlp_w2, v_meta, v_norm_mix_g, v_norm_mlp_g, v_even_w_in, v_even_ret_gn_g, v_even_conv_w, v_even_conv_b, v_even_conv_ln_g, v_even_conv_ln_b, v_even_w_out, v_odd_w_qkv, v_odd_q_norm_g, v_odd_k_norm_g, v_odd_w_o, v_mlp_w1, v_mlp_w2):
    w = dict(meta=meta, norm_mix_g=norm_mix_g, norm_mlp_g=norm_mlp_g, even_w_in=even_w_in,
             even_ret_gn_g=even_ret_gn_g, even_conv_w=even_conv_w, even_conv_b=even_conv_b,
             even_conv_ln_g=even_conv_ln_g, even_conv_ln_b=even_conv_ln_b, even_w_out=even_w_out,
             odd_w_qkv=odd_w_qkv, odd_q_norm_g=odd_q_norm_g, odd_k_norm_g=odd_k_norm_g, odd_w_o=odd_w_o,
             mlp_w1=mlp_w1, mlp_w2=mlp_w2)
    mom = dict(meta=m_meta, norm_mix_g=m_norm_mix_g, norm_mlp_g=m_norm_mlp_g, even_w_in=m_even_w_in,
               even_ret_gn_g=m_even_ret_gn_g, even_conv_w=m_even_conv_w, even_conv_b=m_even_conv_b,
               even_conv_ln_g=m_even_conv_ln_g, even_conv_ln_b=m_even_conv_ln_b, even_w_out=m_even_w_out,
               odd_w_qkv=m_odd_w_qkv, odd_q_norm_g=m_odd_q_norm_g, odd_k_norm_g=m_odd_k_norm_g, odd_w_o=m_odd_w_o,
               mlp_w1=m_mlp_w1, mlp_w2=m_mlp_w2)
    var = dict(meta=v_meta, norm_mix_g=v_norm_mix_g, norm_mlp_g=v_norm_mlp_g, even_w_in=v_even_w_in,
               even_ret_gn_g=v_even_ret_gn_g, even_conv_w=v_even_conv_w, even_conv_b=v_even_conv_b,
               even_conv_ln_g=v_even_conv_ln_g, even_conv_ln_b=v_even_conv_ln_b, even_w_out=v_even_w_out,
               odd_w_qkv=v_odd_w_qkv, odd_q_norm_g=v_odd_q_norm_g, odd_k_norm_g=v_odd_k_norm_g, odd_w_o=v_odd_w_o,
               mlp_w1=v_mlp_w1, mlp_w2=v_mlp_w2)
    me = 4 * lax.axis_index("x") + 2 * lax.axis_index("y") + lax.axis_index("c")

    small_in = jnp.concatenate([meta, jnp.pad(even_conv_w[0], ((0, 1), (0, 0))),
                                jnp.pad(even_ret_gn_g[0], ((0, 4), (0, 96)))], axis=0)
    b16 = lambda a: a.astype(BF16)
    later_src = dict(w_out=b16(even_w_out[0]), w1_0=b16(mlp_w1[0]), w2_0=b16(mlp_w2[0]),
                     w_qkv=b16(odd_w_qkv[0]), w_o=b16(odd_w_o[0]), w1_1=b16(mlp_w1[1]), w2_1=b16(mlp_w2[1]))
    landed = _gather_first([b16(even_w_in[0]), small_in], list(later_src.values()))
    g_in, g_small = landed[0], landed[1]
    own_slot = dict(zip(later_src, landed[2:]))
    groups = (("gather_l0", ("w_out", "w1_0", "w2_0")), ("gather_attn", ("w_qkv", "w_o")),
              ("gather_l1", ("w1_1", "w2_1")))
    pending = {}
    gather_tok = jnp.zeros((), F32)
    for gname, names in groups:
        ex = _exchange_start(gname, [later_src[n] for n in names], [own_slot[n] for n in names], "gather")
        gather_tok = gather_tok + ex["token"]
        for n in names:
            pending[n] = (ex, names)
    arrived = dict(w_in=g_in)

    def weight(name, after):
        if name not in arrived:
            ex, names = pending[name]
            arrived.update(zip(names, _exchange_wait(ex, after)[1]))
        return arrived[name]

    cols = lambda a: jnp.transpose(a, (1, 0, 2)).reshape(a.shape[1], -1)
    p = dict(norm_mix_g=norm_mix_g + gather_tok, norm_mlp_g=norm_mlp_g, conv_b=even_conv_b, ln_g=even_conv_ln_g,
             ln_b=even_conv_ln_b, qn_g=odd_q_norm_g, kn_g=odd_k_norm_g,
             gn_g=cols(g_small[:, 48:52, :32]),
             conv_w=jnp.pad(cols(g_small[:, 16:47]), ((0, 1), (0, 0))))
    meta_full = cols(g_small[:, 0:16])

    scatters = {}

    def emit(tag, grads):
        scatters[tag] = _scatter_start("scatter_" + tag, grads)
        return scatters[tag]["token"]

    h0 = jnp.concatenate([jnp.zeros((PAD_FRONT, D_MODEL), F32), meta_full, x[0]], axis=0)
    target = jnp.concatenate([jnp.zeros((TOK0, D_MODEL), F32), loss_target[0]], axis=0)
    grad_x, small_part = _local_step(h0, target, p, weight, emit)

    out = {}
    got = {}

    def update(names, terms, after):
        for tag in {t for name in names for t, _ in terms[name]} - set(got):
            got[tag] = _exchange_wait(scatters[tag], after)
        for name in names:
            owns, recvs = zip(*[(got[t][0][j], got[t][1][j]) for t, j in terms[name]])
            out[name] = _adamw("adamw_" + name, w[name], list(owns), list(recvs), mom[name], var[name], me)

    terms = dict(even_w_in=[("mix0", 0)], even_w_out=[("mix0_out", 0)], odd_w_qkv=[("attn", 0)], odd_w_o=[("attn", 1)],
                 mlp_w1=[("mlp0", 0), ("mlp1", 0)], mlp_w2=[("mlp0", 1), ("mlp1", 1)])
    small_ex = _exchange_start("small", [small_part], [lax.empty((N_DEV,) + small_part.shape, F32)], "gather")
    update(("mlp_w1", "mlp_w2", "odd_w_qkv", "odd_w_o", "even_w_out"), terms, grad_x)
    update(("even_w_in",), terms, out["even_w_out"][1])
    (own_part,), (slots,) = _exchange_wait(small_ex, out["even_w_in"][1])
    tot = _sum_small(lax.dynamic_update_slice(slots, own_part[None], (me, 0, 0)))
    loss = tot[ROW_LOSS, 0]

    shard_cols = lambda a, width: lax.dynamic_slice_in_dim(a, me * width, width, axis=1)
    one = lambda r: tot[r:r + 1]
    small_g = dict(
        norm_mix_g=tot[ROW_MIX:ROW_MIX + 2], norm_mlp_g=tot[ROW_MLP:ROW_MLP + 2],
        even_conv_b=one(ROW_CB), even_conv_ln_g=one(ROW_LG), even_conv_ln_b=one(ROW_LB),
        odd_q_norm_g=one(ROW_QN)[:, :64], odd_k_norm_g=one(ROW_KN)[:, :64],
        meta=shard_cols(tot[ROW_META:ROW_META + N_META], 128),
        even_conv_w=shard_cols(tot[ROW_CW:ROW_CW + CONV_WIDTH], 128)[None],
        even_ret_gn_g=shard_cols(tot[ROW_GN].reshape(4, 256), 32)[None])
    packs = {n: (_pack128(w[n]), _pack128(small_g[n]), _pack128(mom[n]), _pack128(var[n])) for n in _SMALL_NAMES}
    cat4 = [jnp.concatenate([packs[n][i] for n in _SMALL_NAMES], axis=0) for i in range(4)]
    d_s, m_s, v_s = _adamw_small(*cat4)
    r0 = 0
    for n in _SMALL_NAMES:
        rows = packs[n][0].shape[0]
        size = w[n].size
        take = lambda a: a[r0:r0 + rows].reshape(-1)[:size].reshape(w[n].shape)
        out[n] = (small_g[n].reshape(w[n].shape), take(d_s), take(m_s), take(v_s))
        r0 += rows

    res = [loss, grad_x[None]]
    for i in range(4):
        res.extend(out[n][i] for n in _ORDER)
    return tuple(res)
```

```python
import functools

import numpy as np
import jax
import jax.numpy as jnp
from jax import lax
from jax.experimental import pallas as pl
from jax.experimental.pallas import tpu as pltpu

F32 = jnp.float32
BF16 = jnp.bfloat16

D_MODEL = 1024
N_META = 16
CHUNK = 128
PAD_FRONT = 112
TOK0 = PAD_FRONT + N_META
EPS = 1e-6
N_DEV = 8
RET_HEADS = 4
RET_DECAY_OFFSET = 5.0
ROPE_BASE = 10000.0
CONV_WIDTH = 31
HALO = 32
SB_SCALE = 64 ** -0.5
RET_SCALE = 128 ** -0.5
ADAM_LR, ADAM_B1, ADAM_B2, ADAM_EPS, ADAM_WD, ADAM_STEP = 0.001, 0.9, 0.999, 1e-08, 0.01, 10
VMEM_LIMIT = 56 * 1024 * 1024
MESH = pl.DeviceIdType.MESH


def _pcall(body, **kw):
    return pl.pallas_call(body, **kw)


def _params(**kw):
    return pltpu.CompilerParams(vmem_limit_bytes=VMEM_LIMIT, **kw)


def _tile(n, cands):
    for c in cands:
        if n % c == 0:
            return c
    raise ValueError(f"no tile for {n} in {cands}")


def _sigmoid(x):
    return 1.0 / (1.0 + jnp.exp(-x))


_DIMS = {
    "nn": (((1,), (0,)), ((), ())),
    "nt": (((1,), (1,)), ((), ())),
    "tn": (((0,), (0,)), ((), ())),
}


def _matmul(name, a, b, *, grid, a_spec, b_spec, o_spec, out_shape, contract, acc_shape,
            epi="plain", extra=None, extra_spec=None):
    nk = grid[2]
    dims = _DIMS[contract]
    n_in = 3 if extra is not None else 2
    n_out = 2 if epi == "relu2" else 1

    def body(*refs):
        a_ref, b_ref = refs[0], refs[1]
        e_ref = refs[2] if extra is not None else None
        outs = refs[n_in:n_in + n_out]
        acc = refs[-1]
        k = pl.program_id(2)
        part = lax.dot_general(a_ref[...].astype(BF16), b_ref[...].astype(BF16), dims, preferred_element_type=F32)
        if nk > 1:
            @pl.when(k == 0)
            def _():
                acc[...] = jnp.zeros_like(acc)

            acc[...] += part

        @pl.when(k == nk - 1)
        def _():
            r = acc[...] if nk > 1 else part
            if epi == "plain":
                outs[0][...] = r.astype(outs[0].dtype)
            elif epi == "residual":
                outs[0][...] = (r + e_ref[...]).astype(outs[0].dtype)
            elif epi == "relu2":
                outs[0][...] = r
                rr = jnp.maximum(r, 0.0)
                outs[1][...] = (rr * rr).astype(BF16)
            elif epi == "drelu2":
                outs[0][...] = (r * (2.0 * jnp.maximum(e_ref[...], 0.0))).astype(outs[0].dtype)

    in_specs = [a_spec, b_spec] + ([extra_spec] if extra is not None else [])
    args = (a, b) + ((extra,) if extra is not None else ())
    if n_out == 2:
        out_specs = [o_spec, o_spec]
    else:
        out_specs = o_spec
    return _pcall(body, name=name, grid=grid, in_specs=in_specs, out_specs=out_specs,
                  out_shape=out_shape, scratch_shapes=[pltpu.VMEM(acc_shape, F32)],
                  compiler_params=_params(dimension_semantics=("parallel", "parallel", "arbitrary")))(*args)


def _tm(t):
    return _tile(t, (1408, 768, 384, 128))


def _mm_cols(name, a, wb, lead, out_dtype=F32, epi="plain"):
    t, kdim = a.shape
    n = wb.shape[-1]
    tm, tk = _tm(t), _tile(kdim, (1024, 512))
    nl = len(lead)
    b_spec = pl.BlockSpec((None,) * (1 + nl) + (tk, n), lambda i, j, k: (j,) + lead + (k, 0))
    o_spec = pl.BlockSpec((tm, n), lambda i, j, k: (i, j))
    if epi == "relu2":
        out_shape = [jax.ShapeDtypeStruct((t, N_DEV * n), F32), jax.ShapeDtypeStruct((t, N_DEV * n), BF16)]
    else:
        out_shape = jax.ShapeDtypeStruct((t, N_DEV * n), out_dtype)
    return _matmul(name, a, wb, grid=(t // tm, N_DEV, kdim // tk),
                   a_spec=pl.BlockSpec((tm, tk), lambda i, j, k: (i, k)), b_spec=b_spec, o_spec=o_spec,
                   out_shape=out_shape, contract="nn", acc_shape=(tm, n), epi=epi)


def _tm_deep(t, kdim):
    return _tm(t) if kdim <= 2048 else _tile(t, (704, 384, 128))


def _mm_cols_t_rms(name, a, wb, h, g, dres):
    t = a.shape[0]
    nb, kdim, n = wb.shape
    tm = _tile(t, (704, 384, 128))

    def body(a_ref, b_ref, h_ref, g_ref, r_ref, o_ref, dg_ref):
        @pl.when(pl.program_id(0) == 0)
        def _():
            dg_ref[...] = jnp.zeros_like(dg_ref)

        d = _dot(a_ref[:, 0:n].astype(BF16), b_ref[0], "nt")
        for j in range(1, nb):
            d = d + _dot(a_ref[:, j * n:(j + 1) * n].astype(BF16), b_ref[j], "nt")
        x = h_ref[...]
        rs = lax.rsqrt(jnp.mean(x * x, axis=-1, keepdims=True) + EPS)
        u = d * g_ref[...]
        m = jnp.mean(u * x, axis=-1, keepdims=True)
        o_ref[...] = r_ref[...] + rs * u - x * (rs * rs * rs * m)
        dg_ref[...] += jnp.sum(d * x * rs, axis=0, keepdims=True)

    row = pl.BlockSpec((tm, kdim), lambda i: (i, 0))
    vec = pl.BlockSpec((1, kdim), lambda i: (0, 0))
    return _pcall(body, name=name, grid=(t // tm,),
                  in_specs=[pl.BlockSpec((tm, nb * n), lambda i: (i, 0)),
                            pl.BlockSpec((nb, kdim, n), lambda i: (0, 0, 0)), row, vec, row],
                  out_specs=[row, vec],
                  out_shape=[jax.ShapeDtypeStruct((t, kdim), F32), jax.ShapeDtypeStruct((1, kdim), F32)],
                  compiler_params=_params(dimension_semantics=("arbitrary",)))(a, wb, h, g, dres)


def _mm_rows_t(name, a, wb, lead, out_dtype=F32, epi="plain", extra=None):
    t, n = a.shape
    r = wb.shape[-2]
    tm, tk = _tm(t), _tile(n, (1024,))
    nl = len(lead)
    b_spec = pl.BlockSpec((None,) * (1 + nl) + (r, tk), lambda i, j, k: (j,) + lead + (0, k))
    o_spec = pl.BlockSpec((tm, r), lambda i, j, k: (i, j))
    return _matmul(name, a, wb, grid=(t // tm, N_DEV, n // tk),
                   a_spec=pl.BlockSpec((tm, tk), lambda i, j, k: (i, k)), b_spec=b_spec, o_spec=o_spec,
                   out_shape=jax.ShapeDtypeStruct((t, N_DEV * r), out_dtype), contract="nt",
                   acc_shape=(tm, r), epi=epi, extra=extra, extra_spec=o_spec if extra is not None else None)


def _mm_rows_loss(name, a, wb, residual, target):
    t = a.shape[0]
    nb, r, n = wb.shape
    tm, tn = _tm_deep(t, nb * r), _tile(n, (512,))

    def body(a_ref, b_ref, r_ref, t_ref, d_ref, l_ref):
        i = pl.program_id(0)

        @pl.when((i == 0) & (pl.program_id(1) == 0))
        def _():
            l_ref[...] = jnp.zeros_like(l_ref)

        y = r_ref[...] + _dot(a_ref[...].astype(BF16), b_ref[...].reshape(nb * r, tn))
        diff = jnp.where(_row_ids(i, tm) >= TOK0, y - t_ref[...], 0.0)
        d_ref[...] = diff * (1.0 / D_MODEL)
        l_ref[...] += jnp.sum(diff * diff) * (0.5 / D_MODEL)

    o_spec = pl.BlockSpec((tm, tn), lambda i, j: (i, j))
    return _pcall(body, name=name, grid=(t // tm, n // tn),
                  in_specs=[pl.BlockSpec((tm, nb * r), lambda i, j: (i, 0)),
                            pl.BlockSpec((nb, r, tn), lambda i, j: (0, 0, j)), o_spec, o_spec],
                  out_specs=[o_spec, pl.BlockSpec((8, 128), lambda i, j: (0, 0))],
                  out_shape=[jax.ShapeDtypeStruct((t, n), F32), jax.ShapeDtypeStruct((8, 128), F32)],
                  compiler_params=_params(dimension_semantics=("arbitrary", "arbitrary")))(a, wb, residual, target)


def _mm_rows_norm(name, a, wb, residual, g):
    t = a.shape[0]
    nb, r, n = wb.shape
    tm = _tile(t, (704, 384, 128))

    def body(a_ref, b_ref, r_ref, g_ref, h_ref, hn_ref):
        h = r_ref[...] + _dot(a_ref[...].astype(BF16), b_ref[...].reshape(nb * r, n))
        h_ref[...] = h
        hn_ref[...] = (h * lax.rsqrt(jnp.mean(h * h, axis=-1, keepdims=True) + EPS) * g_ref[...]).astype(BF16)

    row = pl.BlockSpec((tm, n), lambda i: (i, 0))
    return _pcall(body, name=name, grid=(t // tm,),
                  in_specs=[pl.BlockSpec((tm, nb * r), lambda i: (i, 0)), pl.BlockSpec((nb, r, n), lambda i: (0, 0, 0)),
                            row, pl.BlockSpec((1, n), lambda i: (0, 0))],
                  out_specs=[row, row],
                  out_shape=[jax.ShapeDtypeStruct((t, n), F32), jax.ShapeDtypeStruct((t, n), BF16)],
                  compiler_params=_params(dimension_semantics=("parallel",)))(a, wb, residual, g)


def _wgrad_cols(name, x, dy, n):
    t, kdim = x.shape
    tk = _tm(t)
    return _matmul(name, x, dy, grid=(1, N_DEV, t // tk),
                   a_spec=pl.BlockSpec((tk, kdim), lambda i, j, k: (k, 0)),
                   b_spec=pl.BlockSpec((tk, n), lambda i, j, k: (k, j)),
                   o_spec=pl.BlockSpec((None, kdim, n), lambda i, j, k: (j, 0, 0)),
                   out_shape=jax.ShapeDtypeStruct((N_DEV, kdim, n), BF16), contract="tn", acc_shape=(kdim, n))


def _wgrad_rows(name, x, dy, r):
    t = x.shape[0]
    n = dy.shape[1]
    tk, tn = _tm(t), _tile(n, (512,))
    tm = min(N_DEV * r, 1024)
    out = _matmul(name, x, dy, grid=(N_DEV * r // tm, n // tn, t // tk),
                  a_spec=pl.BlockSpec((tk, tm), lambda i, j, k: (k, i)),
                  b_spec=pl.BlockSpec((tk, tn), lambda i, j, k: (k, j)),
                  o_spec=pl.BlockSpec((tm, tn), lambda i, j, k: (i, j)),
                  out_shape=jax.ShapeDtypeStruct((N_DEV * r, n), BF16), contract="tn", acc_shape=(tm, tn))
    return out.reshape(N_DEV, r, n)


def _rows(t):
    return _tile(t, (384, 128))


def _rms_fwd(name, h, g):
    t = h.shape[0]
    tr = _rows(t)

    def body(h_ref, g_ref, o_ref):
        x = h_ref[...]
        r = lax.rsqrt(jnp.mean(x * x, axis=-1, keepdims=True) + EPS)
        o_ref[...] = (x * r * g_ref[...]).astype(BF16)

    row = pl.BlockSpec((tr, D_MODEL), lambda i: (i, 0))
    vec = pl.BlockSpec((1, D_MODEL), lambda i: (0, 0))
    return _pcall(body, name=name, grid=(t // tr,), in_specs=[row, vec], out_specs=row,
                  out_shape=jax.ShapeDtypeStruct((t, D_MODEL), BF16))(h, g)


def _ret_tables(t):
    hh = np.arange(RET_HEADS, dtype=np.float64)
    log_g = np.log1p(-np.exp2(-RET_DECAY_OFFSET - hh))
    idx = np.arange(CHUNK, dtype=np.float64)
    diff = idx[:, None] - idx[None, :]
    dmat = np.where(diff[None] >= 0, np.exp(np.maximum(diff, 0.0)[None] * log_g[:, None, None]), 0.0)
    qdec = np.exp((idx + 1.0)[None, :, None] * log_g[:, None, None]) * np.ones((1, 1, CHUNK))
    kdec = np.exp((CHUNK - 1 - idx)[None, :, None] * log_g[:, None, None]) * np.ones((1, 1, CHUNK))
    half = CHUNK // 2
    inv_freq = (ROPE_BASE ** (-np.arange(half, dtype=np.float32) / half)).astype(np.float32)
    ang = (np.arange(t, dtype=np.float32)[:, None] * inv_freq[None, :]).astype(np.float32).astype(np.float64)
    cos2 = np.concatenate([np.cos(ang), np.cos(ang)], axis=1)
    sin2 = np.concatenate([-np.sin(ang), np.sin(ang)], axis=1)
    return tuple(jnp.asarray(v, F32) for v in (dmat, qdec, kdec, cos2, sin2))


def _rot(x, c, s):
    return x * c + pltpu.roll(x, CHUNK // 2, 1) * s


def _unrot(dx, c, s):
    return dx * c + pltpu.roll(dx * s, CHUNK // 2, 1)


def _dot(a, b, contract="nn"):
    return lax.dot_general(a, b, _DIMS[contract], preferred_element_type=F32)


def _ret_fwd(proj, gn_g, tables):
    t = proj.shape[0]
    nch = t // CHUNK
    dmat, qdec, kdec, cos2, sin2 = tables

    def body(qk_ref, v_ref, g_ref, w_ref, c_ref, s_ref, dm_ref, qd_ref, kd_ref, o_ref, st_ref, cat_ref, state):
        @pl.when(pl.program_id(0) == 0)
        def _():
            state[...] = jnp.zeros_like(state)

        c, s = c_ref[...], s_ref[...]
        for h in range(RET_HEADS):
            q = _rot(qk_ref[:, 128 * h:128 * (h + 1)], c, s)
            k = _rot(qk_ref[:, 512 + 128 * h:512 + 128 * (h + 1)], c, s) * RET_SCALE
            vb = v_ref[:, 256 * h:256 * (h + 1)].astype(BF16)
            st = state[h]
            st_ref[h] = st
            sc = _dot(q.astype(BF16), k.astype(BF16), "nt") * dm_ref[h]
            o = _dot(sc.astype(BF16), vb)
            o += _dot((q * qd_ref[h]).astype(BF16), st.astype(BF16))
            sl = slice(256 * h, 256 * (h + 1))
            o_ref[:, sl] = o
            kv = _dot((k * kd_ref[h]).astype(BF16), vb, "tn")
            state[h] = qd_ref[h, CHUNK - 1:CHUNK, 0:1] * st + kv
            mu = jnp.mean(o, axis=-1, keepdims=True)
            oc = o - mu
            rstd = lax.rsqrt(jnp.mean(oc * oc, axis=-1, keepdims=True) + EPS)
            g = g_ref[:, sl]
            cat_ref[:, sl] = (g * _sigmoid(g) * (oc * rstd * w_ref[:, sl])).astype(BF16)

    tab = pl.BlockSpec((RET_HEADS, CHUNK, CHUNK), lambda n: (0, 0, 0))
    pos = pl.BlockSpec((CHUNK, CHUNK), lambda n: (n, 0))
    row = pl.BlockSpec((CHUNK, 1024), lambda n: (n, 0))
    return _pcall(
        body, name="ret_fwd", grid=(nch,),
        in_specs=[row, pl.BlockSpec((CHUNK, 1024), lambda n: (n, 1)), pl.BlockSpec((CHUNK, 1024), lambda n: (n, 2)),
                  pl.BlockSpec((1, 1024), lambda n: (0, 0)), pos, pos, tab, tab, tab],
        out_specs=[row, pl.BlockSpec((RET_HEADS, None, 128, 256), lambda n: (0, n, 0, 0)), row],
        out_shape=[jax.ShapeDtypeStruct((t, 1024), F32), jax.ShapeDtypeStruct((RET_HEADS, nch, 128, 256), F32),
                   jax.ShapeDtypeStruct((t, 2048), BF16)],
        scratch_shapes=[pltpu.VMEM((RET_HEADS, 128, 256), F32)],
        compiler_params=_params(dimension_semantics=("arbitrary",)))(
            proj, proj, proj, gn_g, cos2, sin2, dmat, qdec, kdec)


def _ret_bwd(dproj, proj, states, do, tables):
    t = proj.shape[0]
    nch = t // CHUNK
    dmat, qdec, kdec, cos2, sin2 = tables

    def body(dp_in, qk_ref, v_ref, do_ref, st_ref, c_ref, s_ref, dm_ref, qd_ref, kd_ref, dp_ref, rst):
        del dp_in
        @pl.when(pl.program_id(0) == 0)
        def _():
            rst[...] = jnp.zeros_like(rst)

        c, s = c_ref[...], s_ref[...]
        for h in range(RET_HEADS):
            q = _rot(qk_ref[:, 128 * h:128 * (h + 1)], c, s)
            k = _rot(qk_ref[:, 512 + 128 * h:512 + 128 * (h + 1)], c, s) * RET_SCALE
            qb, kb = q.astype(BF16), k.astype(BF16)
            vb = v_ref[:, 256 * h:256 * (h + 1)].astype(BF16)
            dob = do_ref[:, 256 * h:256 * (h + 1)].astype(BF16)
            pb = st_ref[h].astype(BF16)
            r = rst[h]
            rb = r.astype(BF16)
            dm, qd, kd = dm_ref[h], qd_ref[h], kd_ref[h]
            sb = (_dot(qb, kb, "nt") * dm).astype(BF16)
            dsb = (_dot(dob, vb, "nt") * dm).astype(BF16)
            dq = _dot(dsb, kb) + _dot(dob, pb, "nt") * qd
            dk = _dot(dsb, qb, "tn") + _dot(vb, rb, "nt") * kd
            dv = _dot(sb, dob, "tn") + _dot((k * kd).astype(BF16), rb)
            rst[h] = _dot((q * qd).astype(BF16), dob, "tn") + qd[CHUNK - 1:CHUNK, 0:1] * r
            dp_ref[:, 128 * h:128 * (h + 1)] = _unrot(dq, c, s).astype(BF16)
            dp_ref[:, 512 + 128 * h:512 + 128 * (h + 1)] = (_unrot(dk, c, s) * RET_SCALE).astype(BF16)
            dp_ref[:, 1024 + 256 * h:1024 + 256 * (h + 1)] = dv.astype(BF16)

    rev = lambda n: nch - 1 - n
    tab = pl.BlockSpec((RET_HEADS, CHUNK, CHUNK), lambda n: (0, 0, 0))
    pos = pl.BlockSpec((CHUNK, CHUNK), lambda n: (rev(n), 0))
    row = pl.BlockSpec((CHUNK, 1024), lambda n: (rev(n), 0))
    return _pcall(
        body, name="ret_bwd", grid=(nch,),
        in_specs=[pl.BlockSpec(memory_space=pl.ANY), row, pl.BlockSpec((CHUNK, 1024), lambda n: (rev(n), 1)), row,
                  pl.BlockSpec((RET_HEADS, None, 128, 256), lambda n: (0, rev(n), 0, 0)),
                  pos, pos, tab, tab, tab],
        out_specs=pl.BlockSpec((CHUNK, 2048), lambda n: (rev(n), 0)),
        out_shape=jax.ShapeDtypeStruct((t, 5120), BF16),
        scratch_shapes=[pltpu.VMEM((RET_HEADS, 128, 256), F32)], input_output_aliases={0: 0},
        compiler_params=_params(dimension_semantics=("arbitrary",)))(
            dproj, proj, proj, do, states, cos2, sin2, dmat, qdec, kdec)


def _row_ids(i, tr):
    return i * tr + lax.broadcasted_iota(jnp.int32, (tr, 1), 0)


SH_ROWS = HALO - 8
CONV_VPU_TAPS = 21


def _shifted_copies(xs, sh, tr):
    for b in range(1, 8):
        sh[b - 1] = xs[pl.ds(b, tr + SH_ROWS), :]


def _shifted(xs, sh, off, tr, lanes=slice(None)):
    a, b = divmod(off, 8)
    return xs[pl.ds(8 * a, tr), lanes] if b == 0 else sh[b - 1, pl.ds(8 * a, tr), lanes]


def _taps_mxu(xs, sh, w_ref, offs, tr, first=0):
    sub = lax.broadcasted_iota(jnp.int32, (256, 128), 0)
    eye = (sub & 127) == lax.broadcasted_iota(jnp.int32, (256, 128), 1)
    outs = []
    for c in range(8):
        lanes = slice(128 * c, 128 * (c + 1))
        acc = None
        for w in range(first, len(offs), 2):
            wb = min(w + 1, len(offs) - 1)
            w_hi = w_ref[w:w + 1, lanes]
            w_lo = w_ref[wb:wb + 1, lanes] if wb > w else jnp.zeros((1, 128), F32)
            dmat = jnp.where(eye, jnp.where(sub < 128, w_hi, w_lo), 0.0).astype(BF16)
            lhs = jnp.concatenate([_shifted(xs, sh, offs[w], tr, lanes).astype(BF16),
                                   _shifted(xs, sh, offs[wb], tr, lanes).astype(BF16)], axis=1)
            d = _dot(lhs, dmat)
            acc = d if acc is None else acc + d
        outs.append(acc)
    return jnp.concatenate(outs, axis=1)


def _conv_fwd(cat, proj, conv_w, conv_b, ln_g, ln_b):
    t = proj.shape[0]
    tr = _rows(t)
    hb = tr // HALO

    def body(cat_in, ua_ref, ug_ref, pa_ref, pg_ref, w_ref, b_ref, lg_ref, lb_ref, c_ref, hd_ref, y_ref, xs, sh):
        del cat_in
        i = pl.program_id(0)
        hdn = ua_ref[...] * _sigmoid(ug_ref[...])
        hd_ref[...] = hdn
        prev = pa_ref[...] * _sigmoid(pg_ref[...])
        xs[0:HALO, :] = jnp.where(i > 0, prev, 0.0)
        xs[HALO:HALO + tr, :] = hdn
        _shifted_copies(xs, sh, tr)
        offs = [HALO - (CONV_WIDTH - 1) + w for w in range(CONV_WIDTH)]
        acc = _taps_mxu(xs, sh, w_ref, offs, tr, first=CONV_VPU_TAPS) + b_ref[...]
        for w in range(CONV_VPU_TAPS):
            acc += w_ref[w:w + 1, :] * _shifted(xs, sh, offs[w], tr)
        y_ref[...] = acc
        mu = jnp.mean(acc, axis=-1, keepdims=True)
        yc = acc - mu
        rstd = lax.rsqrt(jnp.mean(yc * yc, axis=-1, keepdims=True) + EPS)
        yn = yc * rstd * lg_ref[...] + lb_ref[...]
        c = yn * _sigmoid(yn)
        c_ref[...] = jnp.where(_row_ids(i, tr) >= PAD_FRONT, c, 0.0).astype(BF16)

    row = pl.BlockSpec((tr, 1024), lambda i: (i, 0))
    vec = pl.BlockSpec((1, 1024), lambda i: (0, 0))
    halo = lambda col: pl.BlockSpec((HALO, 1024), lambda i: (jnp.maximum(i * hb - 1, 0), col))
    return _pcall(body, name="conv_fwd", grid=(t // tr,),
                  in_specs=[pl.BlockSpec(memory_space=pl.ANY),
                            pl.BlockSpec((tr, 1024), lambda i: (i, 3)), pl.BlockSpec((tr, 1024), lambda i: (i, 4)),
                            halo(3), halo(4), pl.BlockSpec((32, 1024), lambda i: (0, 0)), vec, vec, vec],
                  out_specs=[pl.BlockSpec((tr, 1024), lambda i: (i, 1)), row, row],
                  out_shape=[jax.ShapeDtypeStruct((t, 2048), BF16), jax.ShapeDtypeStruct((t, 1024), F32),
                             jax.ShapeDtypeStruct((t, 1024), F32)],
                  scratch_shapes=[pltpu.VMEM((tr + HALO, 1024), F32), pltpu.VMEM((7, tr + SH_ROWS, 1024), F32)],
                  input_output_aliases={0: 0}, compiler_params=_params())(
                      cat, proj, proj, proj, proj, conv_w, conv_b, ln_g, ln_b)


def _mix_bwd_head(dh, w_out, o, proj, gn_g, y, ln_g, ln_b):
    t = dh.shape[0]
    tr = _rows(t)
    nb, r, n = w_out.shape

    def body(dh_ref, b_ref, o_ref, g_ref, w_ref, y_ref, lg_ref, lb_ref,
             do_ref, dp_ref, dw_ref, dy_ref, dlg_ref, dlb_ref, dcb_ref):
        i = pl.program_id(0)

        @pl.when(i == 0)
        def _():
            for ref in (dw_ref, dlg_ref, dlb_ref, dcb_ref):
                ref[...] = jnp.zeros_like(ref)

        dcat = _dot(dh_ref[...].astype(BF16), b_ref[...].reshape(nb * r, n), "nt")
        for h in range(RET_HEADS):
            sl = slice(256 * h, 256 * (h + 1))
            x = o_ref[:, sl]
            mu = jnp.mean(x, axis=-1, keepdims=True)
            xc = x - mu
            rstd = lax.rsqrt(jnp.mean(xc * xc, axis=-1, keepdims=True) + EPS)
            xh = xc * rstd
            w = w_ref[:, sl]
            g = g_ref[:, sl]
            sg = _sigmoid(g)
            d = dcat[:, sl]
            don = d * (g * sg)
            dp_ref[:, sl] = (d * (xh * w) * (sg * (1.0 + g * (1.0 - sg)))).astype(BF16)
            dw_ref[:, sl] += jnp.sum(don * xh, axis=0, keepdims=True)
            dxh = don * w
            m1 = jnp.mean(dxh, axis=-1, keepdims=True)
            m2 = jnp.mean(dxh * xh, axis=-1, keepdims=True)
            do_ref[:, sl] = rstd * (dxh - m1 - xh * m2)
        yv = y_ref[...]
        mu = jnp.mean(yv, axis=-1, keepdims=True)
        yc = yv - mu
        rstd = lax.rsqrt(jnp.mean(yc * yc, axis=-1, keepdims=True) + EPS)
        xh = yc * rstd
        lg = lg_ref[...]
        yn = xh * lg + lb_ref[...]
        sg = _sigmoid(yn)
        dyn = jnp.where(_row_ids(i, tr) >= PAD_FRONT, dcat[:, 1024:] * (sg * (1.0 + yn * (1.0 - sg))), 0.0)
        dlg_ref[...] += jnp.sum(dyn * xh, axis=0, keepdims=True)
        dlb_ref[...] += jnp.sum(dyn, axis=0, keepdims=True)
        dxh = dyn * lg
        m1 = jnp.mean(dxh, axis=-1, keepdims=True)
        m2 = jnp.mean(dxh * xh, axis=-1, keepdims=True)
        dy = rstd * (dxh - m1 - xh * m2)
        dy_ref[...] = dy
        dcb_ref[...] += jnp.sum(dy, axis=0, keepdims=True)

    row = pl.BlockSpec((tr, 1024), lambda i: (i, 0))
    vec = pl.BlockSpec((1, 1024), lambda i: (0, 0))
    gate = pl.BlockSpec((tr, 1024), lambda i: (i, 2))
    vsh = jax.ShapeDtypeStruct((1, 1024), F32)
    fsh = jax.ShapeDtypeStruct((t, 1024), F32)
    return _pcall(body, name="mix_bwd_head", grid=(t // tr,),
                  in_specs=[row, pl.BlockSpec((nb, r, n), lambda i: (0, 0, 0)), row, gate, vec, row, vec, vec],
                  out_specs=[row, gate, vec, row, vec, vec, vec],
                  out_shape=[fsh, jax.ShapeDtypeStruct((t, 5120), BF16), vsh, fsh, vsh, vsh, vsh],
                  compiler_params=_params(dimension_semantics=("arbitrary",)))(
                      dh, w_out, o, proj, gn_g, y, ln_g, ln_b)


def _conv_bwd_taps(dproj, dy, hdn, proj, conv_w):
    t = dy.shape[0]
    tr = _rows(t)
    hb = tr // HALO
    nt = t // tr

    def body(dp_in, dy_ref, nx_ref, hd_ref, ph_ref, ua_ref, ug_ref, w_ref, da_ref, dg_ref, dw_ref, xs, sh):
        del dp_in
        i = pl.program_id(0)

        @pl.when(i == 0)
        def _():
            dw_ref[...] = jnp.zeros_like(dw_ref)

        dy = dy_ref[...]
        xs[0:tr, :] = dy
        xs[tr:tr + HALO, :] = jnp.where(i < nt - 1, nx_ref[...], 0.0)
        _shifted_copies(xs, sh, tr)
        dh = _taps_mxu(xs, sh, w_ref, [CONV_WIDTH - 1 - w for w in range(CONV_WIDTH)], tr)
        xs[0:HALO, :] = jnp.where(i > 0, ph_ref[...], 0.0)
        xs[HALO:HALO + tr, :] = hd_ref[...]
        _shifted_copies(xs, sh, tr)
        for w in range(CONV_WIDTH):
            dw_ref[w:w + 1, :] += jnp.sum(dy * _shifted(xs, sh, HALO - (CONV_WIDTH - 1) + w, tr), axis=0, keepdims=True)
        dh = jnp.where(_row_ids(i, tr) >= PAD_FRONT, dh, 0.0)
        sg = _sigmoid(ug_ref[...])
        da_ref[...] = (dh * sg).astype(BF16)
        dg_ref[...] = (dh * ua_ref[...] * sg * (1.0 - sg)).astype(BF16)

    row = pl.BlockSpec((tr, 1024), lambda i: (i, 0))
    return _pcall(body, name="conv_bwd_taps", grid=(nt,),
                  in_specs=[pl.BlockSpec(memory_space=pl.ANY),
                            row, pl.BlockSpec((HALO, 1024), lambda i: (jnp.minimum((i + 1) * hb, nt * hb - 1), 0)),
                            row, pl.BlockSpec((HALO, 1024), lambda i: (jnp.maximum(i * hb - 1, 0), 0)),
                            pl.BlockSpec((tr, 1024), lambda i: (i, 3)), pl.BlockSpec((tr, 1024), lambda i: (i, 4)),
                            pl.BlockSpec((32, 1024), lambda i: (0, 0))],
                  out_specs=[pl.BlockSpec((tr, 1024), lambda i: (i, 3)), row, pl.BlockSpec((32, 1024), lambda i: (0, 0))],
                  out_shape=[jax.ShapeDtypeStruct((t, 5120), BF16), jax.ShapeDtypeStruct((t, 1024), BF16),
                             jax.ShapeDtypeStruct((32, 1024), F32)],
                  scratch_shapes=[pltpu.VMEM((tr + HALO, 1024), F32), pltpu.VMEM((7, tr + SH_ROWS, 1024), F32)],
                  input_output_aliases={0: 0}, compiler_params=_params())(
                      dproj, dy, dy, hdn, hdn, proj, proj, conv_w)


NEG_BIG = -1e30


def _seg_tables(qb):
    j = np.arange(128)
    bd = (j[:, None] // 64 == j[None, :] // 64).astype(np.float32)
    ones = np.ones((128, 128), np.float32)
    later = np.concatenate([(j[:, None] >= j[None, :]).astype(np.float32), ones], axis=1)
    earlier = np.concatenate([(j[:, None] < j[None, :]).astype(np.float32), ones], axis=1)
    per = qb // CHUNK
    row = np.arange(qb)[:, None]
    pad = np.broadcast_to(j[None, :] < PAD_FRONT, (qb, 128))
    diag = [(g * CHUNK + j[None, :]) >= row for g in range(per)]
    masks = diag + [np.zeros((qb, 128), bool), pad, diag[0] | pad]
    bias = np.stack([np.where(m, NEG_BIG, 0.0) for m in masks]).astype(np.float32)
    dup = lambda m: np.concatenate([m, m], axis=0)
    return (jnp.asarray(bd, BF16), jnp.asarray(dup(later), BF16), jnp.asarray(dup(earlier), BF16),
            jnp.asarray(bias, F32))


def _split_dot(x, m):
    hi = x.astype(BF16)
    lo = (x - hi.astype(F32)).astype(BF16)
    return _dot(hi, m) + _dot(lo, m)


def _qk_norm_fwd(qkv, qg, kg, bd):
    t = qkv.shape[0]
    tr = _rows(t)
    nb = tr // CHUNK

    def body(q_ref, k_ref, v_ref, qg_ref, kg_ref, bd_ref, qo, kt, k2, vt, v2):
        bdm = bd_ref[...]
        lane = lax.broadcasted_iota(jnp.int32, (1, 128), 1)
        sub = lax.broadcasted_iota(jnp.int32, (128, 1), 0)

        def pair_layouts(x, t_ref, s_ref, hp, b):
            xt = x.T
            t_ref[hp, b] = jnp.concatenate([jnp.where(sub < 64, xt, 0.0), jnp.where(sub >= 64, xt, 0.0)],
                                           axis=1).astype(BF16)
            s_ref[hp, b] = jnp.concatenate([jnp.where(lane < 64, x, 0.0), jnp.where(lane >= 64, x, 0.0)],
                                           axis=0).astype(BF16)

        for hp in range(8):
            sl = slice(128 * hp, 128 * (hp + 1))
            x = q_ref[:, sl]
            r = lax.rsqrt(_split_dot(x * x, bdm) * (1.0 / 64) + EPS)
            qo[:, sl] = (x * r * (qg_ref[:, sl] * SB_SCALE)).astype(BF16)
            x = k_ref[:, sl]
            r = lax.rsqrt(_split_dot(x * x, bdm) * (1.0 / 64) + EPS)
            kn = x * r * kg_ref[:, sl]
            v = v_ref[:, sl]
            for b in range(nb):
                rows = slice(CHUNK * b, CHUNK * (b + 1))
                pair_layouts(kn[rows], kt, k2, hp, b)
                pair_layouts(v[rows], vt, v2, hp, b)

    col = lambda c: pl.BlockSpec((tr, 1024), lambda i: (i, c))
    vec = pl.BlockSpec((1, 1024), lambda i: (0, 0))
    wide = pl.BlockSpec((8, nb, 128, 256), lambda i: (0, i, 0, 0))
    tall = pl.BlockSpec((8, nb, 256, 128), lambda i: (0, i, 0, 0))
    wsh = jax.ShapeDtypeStruct((8, t // CHUNK, 128, 256), BF16)
    tsh = jax.ShapeDtypeStruct((8, t // CHUNK, 256, 128), BF16)
    return _pcall(body, name="qk_norm_fwd", grid=(t // tr,),
                  in_specs=[col(0), col(1), col(2), vec, vec, pl.BlockSpec((128, 128), lambda i: (0, 0))],
                  out_specs=[col(0), wide, tall, wide, tall],
                  out_shape=[jax.ShapeDtypeStruct((t, 1024), BF16), wsh, tsh, wsh, tsh])(qkv, qkv, qkv, qg, kg, bd)


def _qk_norm_bwd(qkv, dq, dk, dv, qg, kg, bd):
    t = qkv.shape[0]
    tr = _rows(t)

    def body(q_ref, k_ref, dq_ref, dk_ref, dv_ref, qg_ref, kg_ref, bd_ref, o_ref, dqg_ref, dkg_ref):
        @pl.when(pl.program_id(0) == 0)
        def _():
            dqg_ref[...] = jnp.zeros_like(dqg_ref)
            dkg_ref[...] = jnp.zeros_like(dkg_ref)

        bdm = bd_ref[...]
        for part, (src, d_ref, g_ref, dg_ref) in enumerate(((q_ref, dq_ref, qg_ref, dqg_ref),
                                                           (k_ref, dk_ref, kg_ref, dkg_ref))):
            for cix in range(8):
                sl = slice(128 * cix, 128 * (cix + 1))
                x = src[:, sl]
                d = d_ref[:, sl]
                r = lax.rsqrt(_split_dot(x * x, bdm) * (1.0 / 64) + EPS)
                u = d * g_ref[:, sl]
                m = _split_dot(u * x, bdm) * (1.0 / 64)
                o_ref[:, 1024 * part + 128 * cix:1024 * part + 128 * (cix + 1)] = (r * u - x * (r * r * r * m)).astype(BF16)
                dg_ref[:, sl] += jnp.sum(d * x * r, axis=0, keepdims=True)
        o_ref[:, 2048:3072] = dv_ref[...].astype(BF16)

    col = lambda c: pl.BlockSpec((tr, 1024), lambda i: (i, c))
    vec = pl.BlockSpec((1, 1024), lambda i: (0, 0))
    vsh = jax.ShapeDtypeStruct((1, 1024), F32)
    return _pcall(body, name="qk_norm_bwd", grid=(t // tr,),
                  in_specs=[col(0), col(1), col(0), col(0), col(0), vec, vec, pl.BlockSpec((128, 128), lambda i: (0, 0))],
                  out_specs=[pl.BlockSpec((tr, 3072), lambda i: (i, 0)), vec, vec],
                  out_shape=[jax.ShapeDtypeStruct((t, 3072), BF16), vsh, vsh])(qkv, qkv, dq, dk, dv, qg, kg, bd)


def _split2(x):
    hi = x.astype(BF16)
    lo = (x - hi.astype(F32)).astype(BF16)
    return jnp.concatenate([hi, lo], axis=1)


def _sb_sums(z, later_tab):
    sp = jnp.maximum(z, 0.0) + jnp.log(1.0 + jnp.exp(-jnp.abs(z)))
    return _dot(_split2(sp), later_tab)


def _sb_bias_index(i, kb, per):
    g = kb - i * per
    return jnp.where(kb == 0, jnp.where(i == 0, per + 2, per + 1), jnp.where(g >= 0, g, per))


def _sb_qb(t):
    return _tile(t, (384, 128))


def _sb_fwd(qh, kt, v2, later_tab, bias_tab):
    t = qh.shape[0]
    qb = _sb_qb(t)
    per = qb // CHUNK
    nkb_all = t // CHUNK

    nq = t // qb

    def body(q_ref, kt_ref, v2_ref, tab_ref, bias_ref, o_ref, ws_ref, acc, carry, zbuf, wbuf, wsem):
        h, i = pl.program_id(0), pl.program_id(1)
        n = h * nq + i
        p = n & 1
        q = q_ref[...]
        acc[...] = jnp.zeros_like(acc)
        carry[...] = jnp.zeros_like(carry)
        nkb = (i + 1) * per
        save = lambda kb: pltpu.make_async_copy(wbuf.at[p, kb], ws_ref.at[h, i, kb], wsem.at[p, kb])

        def drain(step, par):
            hs, is_ = step // nq, step % nq

            def one(kb, _):
                pltpu.make_async_copy(wbuf.at[par, kb], ws_ref.at[hs, is_, kb], wsem.at[par, kb]).wait()
                return 0

            lax.fori_loop(0, (is_ + 1) * per, one, 0)

        @pl.when(n >= 2)
        def _():
            drain(n - 2, p)

        for u in range(per):
            zbuf[u] = _dot(q, kt_ref[nkb - 1 - u])

        def trip(s, diagonal):
            top = nkb - 1 - per * s
            if not diagonal:
                for u in range(per):
                    save(top + per - u).start()
            z2s = [zbuf[u] for u in range(per)]
            for u in range(per):
                zbuf[u] = _dot(q, kt_ref[jnp.maximum(top - per - u, 0)])
            first = [CHUNK * (per - 1 - u) if diagonal else 0 for u in range(per)]
            cins = [carry[0], carry[1]]
            zs, cus = [], []
            for u in range(per):
                zs.append([z2s[u][first[u]:, 128 * hh:128 * (hh + 1)] for hh in range(2)])
                if diagonal or u == per - 1:
                    bias = bias_ref[_sb_bias_index(i, top - u, per)][first[u]:]
                    zs[u] = [z + bias for z in zs[u]]
                cus.append([_sb_sums(z, tab_ref[...]) for z in zs[u]])
            part = None
            for u in range(per):
                kb, lo = top - u, first[u]
                for hh in range(2):
                    sl = slice(128 * hh, 128 * (hh + 1))
                    cu = cus[u][hh]
                    wbuf[p, kb, lo:, sl] = jnp.exp(zs[u][hh] - cu[:, :128] - cins[hh][lo:]).astype(BF16)
                    if lo:
                        wbuf[p, kb, :lo, sl] = jnp.zeros((lo, 128), BF16)
                        cins[hh] = jnp.concatenate([cins[hh][:lo], cins[hh][lo:] + cu[:, 128:]], axis=0)
                    else:
                        cins[hh] = cins[hh] + cu[:, 128:]
                d = _dot(wbuf[p, kb], v2_ref[kb])
                part = d if part is None else part + d
            carry[0], carry[1] = cins[0], cins[1]
            acc[...] += part

        trip(0, True)

        def step(s, _):
            trip(s, False)
            return 0

        lax.fori_loop(1, nkb // per, step, 0)
        for u in range(per):
            save(per - 1 - u).start()
        o_ref[...] = acc[...]

        @pl.when(n == 8 * nq - 1)
        def _():
            drain(n - 1, 1 - p)
            drain(n, p)

    blk = pl.BlockSpec((qb, 128), lambda h, i: (i, h))
    wide = pl.BlockSpec((None, nkb_all, 128, 256), lambda h, i: (h, 0, 0, 0))
    tall = pl.BlockSpec((None, nkb_all, 256, 128), lambda h, i: (h, 0, 0, 0))
    return _pcall(body, name="sb_fwd", grid=(8, t // qb),
                  in_specs=[blk, wide, tall, pl.BlockSpec((256, 256), lambda h, i: (0, 0)),
                            pl.BlockSpec((per + 3, qb, 128), lambda h, i: (0, 0, 0))],
                  out_specs=[blk, pl.BlockSpec(memory_space=pl.ANY)],
                  out_shape=[jax.ShapeDtypeStruct((t, 1024), F32),
                             jax.ShapeDtypeStruct((8, t // qb, nkb_all, qb, 256), BF16)],
                  scratch_shapes=[pltpu.VMEM((qb, 128), F32), pltpu.VMEM((2, qb, 128), F32),
                                  pltpu.VMEM((per, qb, 256), F32), pltpu.VMEM((2, nkb_all, qb, 256), BF16),
                                  pltpu.SemaphoreType.DMA((2, nkb_all))],
                  compiler_params=_params(dimension_semantics=("arbitrary", "arbitrary")))(
                      qh, kt, v2, later_tab, bias_tab)


def _sb_bwd(qh, kt, k2, vt, wsave, do, earlier_tab, bias_tab):
    t = qh.shape[0]
    qb = _sb_qb(t)
    per = qb // CHUNK
    nkb_all = t // CHUNK

    zero_slot = nkb_all
    nq = t // qb

    def body(q_ref, kt_ref, k2_ref, vt_ref, ws_ref, do_ref, etab_ref, bias_ref,
             dq_ref, dk_ref, dv_ref, acc, gcarry, zbuf, dwbuf, wbuf, wsem, dzbuf):
        h, i = pl.program_id(0), pl.program_id(1)
        n = h * nq + i
        p = n & 1

        @pl.when(i == 0)
        def _():
            dk_ref[...] = jnp.zeros_like(dk_ref)
            dv_ref[...] = jnp.zeros_like(dv_ref)

        nkb = (i + 1) * per
        fetch = lambda kb: pltpu.make_async_copy(ws_ref.at[h, i, kb], wbuf.at[p, kb], wsem.at[p, kb])

        def prefetch(step, par):
            hs, is_ = step // nq, step % nq

            def one(kb, _):
                pltpu.make_async_copy(ws_ref.at[hs, is_, kb], wbuf.at[par, kb], wsem.at[par, kb]).start()
                return 0

            lax.fori_loop(0, (is_ + 1) * per, one, 0)

        @pl.when(n == 0)
        def _():
            prefetch(n, p)

        @pl.when(n + 1 < 8 * nq)
        def _():
            prefetch(n + 1, 1 - p)

        q = q_ref[...]
        dob = do_ref[...].astype(BF16)
        acc[...] = jnp.zeros_like(acc)
        gcarry[...] = jnp.zeros_like(gcarry)
        zbuf[...] = _dot(q, kt_ref[0])
        dwbuf[...] = _dot(dob, vt_ref[0])
        dzbuf[...] = jnp.zeros_like(dzbuf)
        wbuf[p, zero_slot] = jnp.zeros((qb, 256), BF16)

        q_t = q.astype(F32).T.astype(BF16)
        do_t = do_ref[...].T.astype(BF16)
        sub = lax.broadcasted_iota(jnp.int32, (128, 1), 0)

        def gradients(slot, kb):
            dz2 = dzbuf[...]
            acc[...] += _dot(dz2, k2_ref[kb])
            dk2 = _dot(q_t, dz2)
            dv2 = _dot(do_t, wbuf[p, slot])
            dk_ref[kb] += jnp.where(sub < 64, dk2[:, :128], dk2[:, 128:])
            dv_ref[kb] += jnp.where(sub < 64, dv2[:, :128], dv2[:, 128:])

        def trip(kb, lo):
            fetch(kb).wait()
            bias = bias_ref[_sb_bias_index(i, kb, per)][lo:]
            z2 = zbuf[...]
            dw2 = dwbuf[...]
            nxt = jnp.minimum(kb + 1, nkb - 1)
            zbuf[...] = _dot(q, kt_ref[nxt])
            dwbuf[...] = _dot(dob, vt_ref[nxt])
            gradients(jnp.where(kb == 0, zero_slot, kb - 1), jnp.maximum(kb - 1, 0))
            w2 = wbuf[p, kb]
            for hh in range(2):
                sl = slice(128 * hh, 128 * (hh + 1))
                z = z2[lo:, sl] + bias
                e = jnp.exp(-jnp.abs(z))
                r = 1.0 / (1.0 + e)
                sig = jnp.where(z >= 0, r, e * r)
                gw = w2[lo:, sl].astype(F32) * dw2[lo:, sl]
                cu2 = _dot(_split2(gw), etab_ref[...])
                gin = gcarry[hh, lo:, :]
                gcarry[hh, lo:, :] = gin + cu2[:, 128:]
                dzbuf[lo:, sl] = (gw - sig * (gw + cu2[:, :128] + gin)).astype(BF16)
                if lo:
                    dzbuf[:lo, sl] = jnp.zeros((lo, 128), BF16)

        def step(kb, _):
            trip(kb, 0)
            return 0

        lax.fori_loop(0, nkb - per, step, 0)
        for g in range(per):
            trip(nkb - per + g, CHUNK * g)
        gradients(nkb - 1, nkb - 1)
        dq_ref[...] = acc[...] * SB_SCALE

        @pl.when(i == nq - 1)
        def _():
            def untranspose(kb, _):
                dk_ref[kb] = dk_ref[kb].T
                dv_ref[kb] = dv_ref[kb].T
                return 0

            lax.fori_loop(0, nkb_all, untranspose, 0)

    blk = pl.BlockSpec((qb, 128), lambda h, i: (i, h))
    wide = pl.BlockSpec((None, nkb_all, 128, 256), lambda h, i: (h, 0, 0, 0))
    tall = pl.BlockSpec((None, nkb_all, 256, 128), lambda h, i: (h, 0, 0, 0))
    tab = pl.BlockSpec((256, 256), lambda h, i: (0, 0))
    kv_out = pl.BlockSpec((nkb_all, 128, 128), lambda h, i: (0, 0, h))
    ksh = jax.ShapeDtypeStruct((nkb_all, 128, 1024), F32)
    dq, dk, dv = _pcall(
        body, name="sb_bwd", grid=(8, t // qb),
        in_specs=[blk, wide, tall, wide, pl.BlockSpec(memory_space=pl.ANY), blk, tab,
                  pl.BlockSpec((per + 3, qb, 128), lambda h, i: (0, 0, 0))],
        out_specs=[blk, kv_out, kv_out], out_shape=[jax.ShapeDtypeStruct((t, 1024), F32), ksh, ksh],
        scratch_shapes=[pltpu.VMEM((qb, 128), F32), pltpu.VMEM((2, qb, 128), F32),
                        pltpu.VMEM((qb, 256), F32), pltpu.VMEM((qb, 256), F32),
                        pltpu.VMEM((2, nkb_all + 1, qb, 256), BF16), pltpu.SemaphoreType.DMA((2, nkb_all)),
                        pltpu.VMEM((qb, 256), BF16)],
        compiler_params=_params(dimension_semantics=("arbitrary", "arbitrary")))(
            qh, kt, k2, vt, wsave, do, earlier_tab, bias_tab)
    return dq, dk.reshape(t, 1024), dv.reshape(t, 1024)


def _adamw_math(w, g, m, v):
    m = ADAM_B1 * m + (1.0 - ADAM_B1) * g
    v = ADAM_B2 * v + (1.0 - ADAM_B2) * (g * g)
    m_hat = m / (1.0 - ADAM_B1 ** ADAM_STEP)
    v_hat = v / (1.0 - ADAM_B2 ** ADAM_STEP)
    delta = -ADAM_LR * (m_hat / (jnp.sqrt(v_hat) + ADAM_EPS) + ADAM_WD * w)
    return delta, m, v


def _adamw(name, w, owns, recvs, m, v, me):
    shape = w.shape
    c = shape[-1]
    nl = len(owns)
    w3, m3, v3 = (a.reshape(nl, -1, c) for a in (w, m, v))
    r = w3.shape[1]
    tr = _tile(r, (256, 128))
    owns = [o.reshape(N_DEV, r, c) for o in owns]
    recvs = [p.reshape(N_DEV - 1, r, c) for p in recvs]

    def body(me_ref, w_ref, *rest):
        own_refs, recv_refs = rest[:nl], rest[nl:2 * nl]
        m_ref, v_ref = rest[2 * nl:2 * nl + 2]
        g_out, d_out, m_out, v_out = rest[2 * nl + 2:]
        layer = pl.program_id(0)

        def grad(k):
            g = own_refs[k][...].astype(F32)
            for s in range(N_DEV - 1):
                g = g + recv_refs[k][s].astype(F32)
            return g

        g = grad(0)
        for k in range(1, nl):
            g = jnp.where(layer == k, grad(k), g)
        d, mn, vn = _adamw_math(w_ref[...], g, m_ref[...], v_ref[...])
        g_out[...] = g
        d_out[...] = d
        m_out[...] = mn
        v_out[...] = vn

    row = pl.BlockSpec((None, tr, c), lambda l, i, me_ref: (l, i, 0))
    own = lambda k: pl.BlockSpec((None, tr, c), lambda l, i, me_ref: (me_ref[0], jnp.where(l == k, i, 0), 0))
    rcv = lambda k: pl.BlockSpec((N_DEV - 1, tr, c), lambda l, i, me_ref: (0, jnp.where(l == k, i, 0), 0))
    osh = jax.ShapeDtypeStruct((nl, r, c), F32)
    grid_spec = pltpu.PrefetchScalarGridSpec(
        num_scalar_prefetch=1, grid=(nl, r // tr),
        in_specs=[row] + [own(k) for k in range(nl)] + [rcv(k) for k in range(nl)] + [row, row],
        out_specs=[row, row, row, row])
    outs = _pcall(body, name=name, grid_spec=grid_spec, out_shape=[osh, osh, osh, osh])(
        me.reshape(1), w3, *owns, *recvs, m3, v3)
    return tuple(o.reshape(shape) for o in outs)


def _place():
    x, y, c = lax.axis_index("x"), lax.axis_index("y"), lax.axis_index("c")
    return x, y, c, 4 * x + 2 * y + c


def _peer(x, y, c, rel):
    return (x ^ ((rel >> 2) & 1), y ^ ((rel >> 1) & 1), c ^ (rel & 1))


def _gather_first(now, later):
    n, k = len(now), len(later)

    def body(*refs):
        ins, outs = refs[:n + k], refs[n + k:2 * (n + k)]
        send, recv, lsem = refs[2 * (n + k):]
        x, y, c, me = _place()
        locals_ = []
        for w in range(n + k):
            local = pltpu.make_async_copy(ins[w], outs[w].at[me], lsem.at[w])
            local.start()
            locals_.append(local)
        def copy(w, src, slot, rel, to_rel):
            return pltpu.make_async_remote_copy(src_ref=src, dst_ref=outs[w].at[slot], send_sem=send.at[w, rel - 1],
                                                recv_sem=recv.at[w, rel - 1], device_id=_peer(x, y, c, to_rel),
                                                device_id_type=MESH)

        for w in range(n):
            for rel in (1, 2, 4, 6):
                copy(w, ins[w], me, rel, rel).start()
        for w in range(n):
            for rel in (2, 4, 6):
                copy(w, ins[w], me ^ rel, rel, rel).wait_recv()
                copy(w, outs[w].at[me ^ rel], me ^ rel, rel | 1, 1).start()
        for w in range(n):
            for rel in (1, 3, 5, 7):
                copy(w, ins[w], me ^ rel, rel, 1).wait_recv()
            for rel in range(1, N_DEV):
                copy(w, ins[w], me, rel, rel).wait_send()
        for local in locals_:
            local.wait()

    hbm = pl.BlockSpec(memory_space=pl.ANY)
    vmem = pl.BlockSpec(memory_space=pltpu.VMEM)
    arrays = list(now) + list(later)
    return _pcall(body, name="gather_first", in_specs=[vmem] * (n + k), out_specs=[hbm] * (n + k),
                  out_shape=[jax.ShapeDtypeStruct((N_DEV,) + a.shape, a.dtype) for a in arrays],
                  scratch_shapes=[pltpu.SemaphoreType.DMA((n, N_DEV - 1)), pltpu.SemaphoreType.DMA((n, N_DEV - 1)),
                                  pltpu.SemaphoreType.DMA((n + k,))],
                  compiler_params=_params(has_side_effects=True))(*arrays)


_HBM = pl.BlockSpec(memory_space=pltpu.HBM)
_SEM = pl.BlockSpec(memory_space=pltpu.SEMAPHORE)
_DATAFLOW = pltpu.SideEffectType.DATAFLOW_SIDE_EFFECTING


def _exchange_refs(srcs, lands, mode, me, rel, j):
    if mode == "gather":
        return srcs[j], lands[j].at[me], lands[j].at[me ^ rel]
    return srcs[j].at[me ^ rel], lands[j].at[rel - 1], lands[j].at[rel - 1]


def _exchange_start(name, srcs, lands, mode):
    n = len(srcs)

    def body(*refs):
        ins, lnd = refs[:n], refs[n:2 * n]
        send, recv = refs[2 * n], refs[2 * n + 1]
        token = refs[-1]
        x, y, c, me = _place()
        for j in range(n):
            for rel in range(1, N_DEV):
                src, dst, _ = _exchange_refs(ins, lnd, mode, me, rel, j)
                pltpu.make_async_remote_copy(src_ref=src, dst_ref=dst, send_sem=send.at[j * (N_DEV - 1) + rel - 1],
                                             recv_sem=recv.at[j * (N_DEV - 1) + rel - 1],
                                             device_id=_peer(x, y, c, rel), device_id_type=MESH).start()
        token[...] = jnp.zeros_like(token)

    sems = pltpu.SemaphoreType.DMA((n * (N_DEV - 1),))
    hbm_like = lambda a: pltpu.HBM(a.shape, a.dtype)
    outs = _pcall(body, name=name + "_start",
                  in_specs=[_HBM] * (2 * n), out_specs=[_SEM, _SEM] + [_HBM] * (2 * n) + [pl.BlockSpec(memory_space=pltpu.VMEM)],
                  out_shape=[sems, sems] + [hbm_like(a) for a in srcs] + [hbm_like(a) for a in lands]
                  + [jax.ShapeDtypeStruct((8, 128), F32)],
                  input_output_aliases={i: 2 + i for i in range(2 * n)},
                  compiler_params=pltpu.CompilerParams(has_side_effects=_DATAFLOW))(
                      *[pltpu.with_memory_space_constraint(a, pltpu.HBM) for a in list(srcs) + list(lands)])
    return dict(name=name, mode=mode, n=n, send=outs[0], recv=outs[1], srcs=outs[2:2 + n], lands=outs[2 + n:2 + 2 * n],
                token=outs[-1][0, 0])


def _exchange_wait(ex, after):
    n, mode = ex["n"], ex["mode"]

    def body(*refs):
        ins, lnd = refs[:n], refs[n:2 * n]
        send, recv = refs[2 * n], refs[2 * n + 1]
        x, y, c, me = _place()
        for j in range(n):
            for rel in range(1, N_DEV):
                src, dst, landed = _exchange_refs(ins, lnd, mode, me, rel, j)
                pltpu.make_async_remote_copy(src_ref=src, dst_ref=dst, send_sem=send.at[j * (N_DEV - 1) + rel - 1],
                                             recv_sem=recv.at[j * (N_DEV - 1) + rel - 1],
                                             device_id=_peer(x, y, c, rel), device_id_type=MESH).wait_send()
                pltpu.make_async_remote_copy(src_ref=src, dst_ref=landed, send_sem=send.at[j * (N_DEV - 1) + rel - 1],
                                             recv_sem=recv.at[j * (N_DEV - 1) + rel - 1],
                                             device_id=_peer(x, y, c, rel), device_id_type=MESH).wait_recv()

    hbm_like = lambda a: pltpu.HBM(a.shape, a.dtype)
    arrays = list(ex["srcs"]) + list(ex["lands"])
    outs = _pcall(body, name=ex["name"] + "_wait",
                  in_specs=[_HBM] * (2 * n) + [_SEM, _SEM, pl.BlockSpec(memory_space=pl.ANY)],
                  out_specs=[_HBM] * (2 * n), out_shape=[hbm_like(a) for a in arrays],
                  input_output_aliases={i: i for i in range(2 * n)},
                  compiler_params=pltpu.CompilerParams(has_side_effects=_DATAFLOW))(
                      *arrays, ex["send"], ex["recv"], after)
    return outs[:n], outs[n:]


def _scatter_start(name, grads):
    lands = [lax.empty((N_DEV - 1,) + g.shape[1:], g.dtype) for g in grads]
    return _exchange_start(name, grads, lands, "scatter")


ROW_MIX, ROW_MLP, ROW_CB, ROW_LG, ROW_LB, ROW_QN, ROW_KN, ROW_LOSS = 0, 2, 4, 5, 6, 7, 8, 9
ROW_META, ROW_CW, ROW_GN, SMALL_ROWS = 16, 32, 64, 72


def _sum_small(slots):
    def body(s_ref, o_ref):
        tot = s_ref[0]
        for s in range(1, N_DEV):
            tot = tot + s_ref[s]
        o_ref[...] = tot
        for row in (ROW_QN, ROW_KN):
            v = tot[row:row + 1, :]
            f = v[:, 0:128]
            for k in range(1, 8):
                f = f + v[:, 128 * k:128 * (k + 1)]
            o_ref[row:row + 1, 0:64] = f[:, 0:64] + f[:, 64:128]

    return _pcall(body, name="sum_small", out_shape=jax.ShapeDtypeStruct(slots.shape[1:], F32))(slots)


def _adamw_small(w, g, m, v):
    def body(w_ref, g_ref, m_ref, v_ref, d_out, m_out, v_out):
        d, mn, vn = _adamw_math(w_ref[...], g_ref[...], m_ref[...], v_ref[...])
        d_out[...] = d
        m_out[...] = mn
        v_out[...] = vn

    osh = jax.ShapeDtypeStruct(w.shape, F32)
    return _pcall(body, name="adamw_small", out_shape=[osh, osh, osh])(w, g, m, v)


def _local_step(h0, target, p, weight, emit):
    t = h0.shape[0]
    tables = _ret_tables(t)
    bd, later_tab, earlier_tab, bias_tab = _seg_tables(_sb_qb(t))
    row = lambda a, i: a[i:i + 1]

    hn_a = _rms_fwd("rms_mix0", h0, row(p["norm_mix_g"], 0))
    w_in = weight("w_in", hn_a)
    proj = _mm_cols("proj_in", hn_a, w_in, ())
    gn_flat = p["gn_g"].reshape(1, 1024)
    o_ret, states, cat = _ret_fwd(proj, gn_flat, tables)
    cat, hdn, ycv = _conv_fwd(cat, proj, p["conv_w"], p["conv_b"], p["ln_g"], p["ln_b"])
    w_out = weight("w_out", cat)
    h1, hn_b = _mm_rows_norm("mix_out", cat, w_out, h0, row(p["norm_mlp_g"], 0))
    w1_0, w2_0 = weight("w1_0", hn_b), weight("w2_0", hn_b)
    a0, s0 = _mm_cols("mlp0_up", hn_b, w1_0, (), epi="relu2")
    h2, hn_c = _mm_rows_norm("mlp0_down", s0, w2_0, h1, row(p["norm_mix_g"], 1))

    w_qkv = weight("w_qkv", hn_c)
    qkv = _mm_cols("qkv", hn_c, w_qkv, ())
    qg = jnp.tile(p["qn_g"], (1, 16))
    kg = jnp.tile(p["kn_g"], (1, 16))
    qh, kt, k2, vt, v2 = _qk_norm_fwd(qkv, qg, kg, bd)
    o_sb, w_sb = _sb_fwd(qh, kt, v2, later_tab, bias_tab)
    w_o = weight("w_o", o_sb)
    h3, hn_d = _mm_rows_norm("attn_out", o_sb, w_o, h2, row(p["norm_mlp_g"], 1))
    w1_1, w2_1 = weight("w1_1", hn_d), weight("w2_1", hn_d)
    a1, s1 = _mm_cols("mlp1_up", hn_d, w1_1, (), epi="relu2")
    dh, loss = _mm_rows_loss("mlp1_down", s1, w2_1, h3, target)

    def mlp_bwd(tag, layer, w1, w2, dh, h_in, hn, a, s):
        da = _mm_rows_t(f"{tag}_dact", dh, w2, (), out_dtype=BF16, epi="drelu2", extra=a)
        dw2 = _wgrad_rows(f"{tag}_dw2", s, dh, 512)
        dw1 = _wgrad_cols(f"{tag}_dw1", hn, da, 512)
        tok = emit(tag, [dw1, dw2])
        return _mm_cols_t_rms(f"{tag}_dhn", da, w1, h_in, row(p["norm_mlp_g"], layer) + tok, dh)

    dh, dg_mlp1 = mlp_bwd("mlp1", 1, w1_1, w2_1, dh, h3, hn_d, a1, s1)

    do_sb = _mm_rows_t("attn_dout", dh, w_o, ())
    dw_o = _wgrad_rows("attn_dwo", o_sb, dh, 128)
    dq, dk, dv = _sb_bwd(qh, kt, k2, vt, w_sb, do_sb, earlier_tab, bias_tab)
    dqkv, dqg, dkg = _qk_norm_bwd(qkv, dq, dk, dv, qg, kg, bd)
    dw_qkv = _wgrad_cols("qkv_dw", hn_c, dqkv, 384)
    tok = emit("attn", [dw_qkv, dw_o])
    dh, dg_mix1 = _mm_cols_t_rms("qkv_dhn", dqkv, w_qkv, h2, row(p["norm_mix_g"], 1) + tok, dh)

    dh, dg_mlp0 = mlp_bwd("mlp0", 0, w1_0, w2_0, dh, h1, hn_b, a0, s0)

    dw_out = _wgrad_rows("mix_dwout", cat, dh, 256)
    tok = emit("mix0_out", [dw_out])
    do_ret, dproj, dgn, dy, dlg, dlb, dcb = _mix_bwd_head(dh, w_out, o_ret, proj, gn_flat + tok, ycv,
                                                          p["ln_g"], p["ln_b"])
    dproj = _ret_bwd(dproj, proj, states, do_ret, tables)
    dproj, dug, dcw = _conv_bwd_taps(dproj, dy, hdn, proj, p["conv_w"])
    dproj = lax.dynamic_update_slice(dproj, dug, (0, 4096))
    dw_in = _wgrad_cols("proj_dw", hn_a, dproj, 640)
    tok = emit("mix0", [dw_in])
    dh, dg_mix0 = _mm_cols_t_rms("proj_dhn", dproj, w_in, h0, row(p["norm_mix_g"], 0) + tok, dh)

    rid = lax.broadcasted_iota(jnp.int32, (16, 1), 0)
    loss_row = jnp.broadcast_to(loss[0:1, 0:1], (1, D_MODEL))
    vecs = sum(jnp.where(rid == k, v, 0.0)
               for k, v in enumerate((dg_mix0, dg_mix1, dg_mlp0, dg_mlp1, dcb, dlg, dlb, dqg, dkg, loss_row)))
    small = jnp.concatenate([vecs, dh[PAD_FRONT:TOK0], dcw, jnp.where(rid[:8] == 0, dgn, 0.0)], axis=0)
    return dh[TOK0:], small


_SMALL_NAMES = ("meta", "norm_mix_g", "norm_mlp_g", "even_ret_gn_g", "even_conv_w", "even_conv_b",
                "even_conv_ln_g", "even_conv_ln_b", "odd_q_norm_g", "odd_k_norm_g")
_BIG_NAMES = ("even_w_in", "even_w_out", "odd_w_qkv", "odd_w_o", "mlp_w1", "mlp_w2")
_ORDER = ("meta", "norm_mix_g", "norm_mlp_g", "even_w_in", "even_ret_gn_g", "even_conv_w", "even_conv_b",
          "even_conv_ln_g", "even_conv_ln_b", "even_w_out", "odd_w_qkv", "odd_q_norm_g", "odd_k_norm_g",
          "odd_w_o", "mlp_w1", "mlp_w2")


def _pack128(a):
    flat = a.reshape(-1)
    n = flat.shape[0]
    rows = -(-n // 128)
    rows8 = -(-rows // 8) * 8
    return jnp.pad(flat, (0, rows8 * 128 - n)).reshape(rows8, 128)


def kernel(x, meta, norm_mix_g, norm_mlp_g, even_w_in, even_ret_gn_g, even_conv_w, even_conv_b, even_conv_ln_g, even_conv_ln_b, even_w_out, odd_w_qkv, odd_q_norm_g, odd_k_norm_g, odd_w_o, mlp_w1, mlp_w2, loss_target, m_meta, m_norm_mix_g, m_norm_mlp_g, m_even_w_in, m_even_ret_gn_g, m_even_conv_w, m_even_conv_b, m_even_conv_ln_g, m_even_conv_ln_b, m_even_w_out, m_odd_w_qkv, m_odd_q_norm_g, m_odd_k_norm_g, m_odd_w_o, m_mlp_w1, m_mlp_w2, v_meta, v_norm_mix_g, v_norm_mlp_g, v_even_w_in, v_even_ret_gn_g, v_even_conv_w, v_even_conv_b, v_even_conv_ln_g, v_even_conv_ln_b, v_even_w_out, v_odd_w_qkv, v_odd_q_norm_g, v_odd_k_norm_g, v_odd_w_o, v_mlp_w1, v_mlp_w2):
    w = dict(meta=meta, norm_mix_g=norm_mix_g, norm_mlp_g=norm_mlp_g, even_w_in=even_w_in,
             even_ret_gn_g=even_ret_gn_g, even_conv_w=even_conv_w, even_conv_b=even_conv_b,
             even_conv_ln_g=even_conv_ln_g, even_conv_ln_b=even_conv_ln_b, even_w_out=even_w_out,
             odd_w_qkv=odd_w_qkv, odd_q_norm_g=odd_q_norm_g, odd_k_norm_g=odd_k_norm_g, odd_w_o=odd_w_o,
             mlp_w1=mlp_w1, mlp_w2=mlp_w2)
    mom = dict(meta=m_meta, norm_mix_g=m_norm_mix_g, norm_mlp_g=m_norm_mlp_g, even_w_in=m_even_w_in,
               even_ret_gn_g=m_even_ret_gn_g, even_conv_w=m_even_conv_w, even_conv_b=m_even_conv_b,
               even_conv_ln_g=m_even_conv_ln_g, even_conv_ln_b=m_even_conv_ln_b, even_w_out=m_even_w_out,
               odd_w_qkv=m_odd_w_qkv, odd_q_norm_g=m_odd_q_norm_g, odd_k_norm_g=m_odd_k_norm_g, odd_w_o=m_odd_w_o,
               mlp_w1=m_mlp_w1, mlp_w2=m_mlp_w2)
    var = dict(meta=v_meta, norm_mix_g=v_norm_mix_g, norm_mlp_g=v_norm_mlp_g, even_w_in=v_even_w_in,
               even_ret_gn_g=v_even_ret_gn_g, even_conv_w=v_even_conv_w, even_conv_b=v_even_conv_b,
               even_conv_ln_g=v_even_conv_ln_g, even_conv_ln_b=v_even_conv_ln_b, even_w_out=v_even_w_out,
               odd_w_qkv=v_odd_w_qkv, odd_q_norm_g=v_odd_q_norm_g, odd_k_norm_g=v_odd_k_norm_g, odd_w_o=v_odd_w_o,
               mlp_w1=v_mlp_w1, mlp_w2=v_mlp_w2)
    me = 4 * lax.axis_index("x") + 2 * lax.axis_index("y") + lax.axis_index("c")

    small_in = jnp.concatenate([meta, jnp.pad(even_conv_w[0], ((0, 1), (0, 0))),
                                jnp.pad(even_ret_gn_g[0], ((0, 4), (0, 96)))], axis=0)
    b16 = lambda a: a.astype(BF16)
    later_src = dict(w_out=b16(even_w_out[0]), w1_0=b16(mlp_w1[0]), w2_0=b16(mlp_w2[0]),
                     w_qkv=b16(odd_w_qkv[0]), w_o=b16(odd_w_o[0]), w1_1=b16(mlp_w1[1]), w2_1=b16(mlp_w2[1]))
    landed = _gather_first([b16(even_w_in[0]), small_in], list(later_src.values()))
    g_in, g_small = landed[0], landed[1]
    own_slot = dict(zip(later_src, landed[2:]))
    groups = (("gather_l0", ("w_out", "w1_0", "w2_0")), ("gather_attn", ("w_qkv", "w_o")),
              ("gather_l1", ("w1_1", "w2_1")))
    pending = {}
    gather_tok = jnp.zeros((), F32)
    for gname, names in groups:
        ex = _exchange_start(gname, [later_src[n] for n in names], [own_slot[n] for n in names], "gather")
        gather_tok = gather_tok + ex["token"]
        for n in names:
            pending[n] = (ex, names)
    arrived = dict(w_in=g_in)

    def weight(name, after):
        if name not in arrived:
            ex, names = pending[name]
            arrived.update(zip(names, _exchange_wait(ex, after)[1]))
        return arrived[name]

    cols = lambda a: jnp.transpose(a, (1, 0, 2)).reshape(a.shape[1], -1)
    p = dict(norm_mix_g=norm_mix_g + gather_tok, norm_mlp_g=norm_mlp_g, conv_b=even_conv_b, ln_g=even_conv_ln_g,
             ln_b=even_conv_ln_b, qn_g=odd_q_norm_g, kn_g=odd_k_norm_g,
             gn_g=cols(g_small[:, 48:52, :32]),
             conv_w=jnp.pad(cols(g_small[:, 16:47]), ((0, 1), (0, 0))))
    meta_full = cols(g_small[:, 0:16])

    scatters = {}

    def emit(tag, grads):
        scatters[tag] = _scatter_start("scatter_" + tag, grads)
        return scatters[tag]["token"]

    h0 = jnp.concatenate([jnp.zeros((PAD_FRONT, D_MODEL), F32), meta_full, x[0]], axis=0)
    target = jnp.concatenate([jnp.zeros((TOK0, D_MODEL), F32), loss_target[0]], axis=0)
    grad_x, small_part = _local_step(h0, target, p, weight, emit)

    out = {}
    got = {}

    def update(names, terms, after):
        for tag in {t for name in names for t, _ in terms[name]} - set(got):
            got[tag] = _exchange_wait(scatters[tag], after)
        for name in names:
            owns, recvs = zip(*[(got[t][0][j], got[t][1][j]) for t, j in terms[name]])
            out[name] = _adamw("adamw_" + name, w[name], list(owns), list(recvs), mom[name], var[name], me)

    terms = dict(even_w_in=[("mix0", 0)], even_w_out=[("mix0_out", 0)], odd_w_qkv=[("attn", 0)], odd_w_o=[("attn", 1)],
                 mlp_w1=[("mlp0", 0), ("mlp1", 0)], mlp_w2=[("mlp0", 1), ("mlp1", 1)])
    small_ex = _exchange_start("small", [small_part], [lax.empty((N_DEV,) + small_part.shape, F32)], "gather")
    update(("mlp_w1", "mlp_w2", "odd_w_qkv", "odd_w_o", "even_w_out"), terms, grad_x)
    update(("even_w_in",), terms, out["even_w_out"][1])
    (own_part,), (slots,) = _exchange_wait(small_ex, out["even_w_in"][1])
    tot = _sum_small(lax.dynamic_update_slice(slots, own_part[None], (me, 0, 0)))
    loss = tot[ROW_LOSS, 0]

    shard_cols = lambda a, width: lax.dynamic_slice_in_dim(a, me * width, width, axis=1)
    one = lambda r: tot[r:r + 1]
    small_g = dict(
        norm_mix_g=tot[ROW_MIX:ROW_MIX + 2], norm_mlp_g=tot[ROW_MLP:ROW_MLP + 2],
        even_conv_b=one(ROW_CB), even_conv_ln_g=one(ROW_LG), even_conv_ln_b=one(ROW_LB),
        odd_q_norm_g=one(ROW_QN)[:, :64], odd_k_norm_g=one(ROW_KN)[:, :64],
        meta=shard_cols(tot[ROW_META:ROW_META + N_META], 128),
        even_conv_w=shard_cols(tot[ROW_CW:ROW_CW + CONV_WIDTH], 128)[None],
        even_ret_gn_g=shard_cols(tot[ROW_GN].reshape(4, 256), 32)[None])
    packs = {n: (_pack128(w[n]), _pack128(small_g[n]), _pack128(mom[n]), _pack128(var[n])) for n in _SMALL_NAMES}
    cat4 = [jnp.concatenate([packs[n][i] for n in _SMALL_NAMES], axis=0) for i in range(4)]
    d_s, m_s, v_s = _adamw_small(*cat4)
    r0 = 0
    for n in _SMALL_NAMES:
        rows = packs[n][0].shape[0]
        size = w[n].size
        take = lambda a: a[r0:r0 + rows].reshape(-1)[:size].reshape(w[n].shape)
        out[n] = (small_g[n].reshape(w[n].shape), take(d_s), take(m_s), take(v_s))
        r0 += rows

    res = [loss, grad_x[None]]
    for i in range(4):
        res.extend(out[n][i] for n in _ORDER)
    return tuple(res)
```

```python
import functools

import numpy as np
import jax
import jax.numpy as jnp
from jax import lax
from jax.experimental import pallas as pl
from jax.experimental.pallas import tpu as pltpu

F32 = jnp.float32
BF16 = jnp.bfloat16

D_MODEL = 1024
N_META = 16
CHUNK = 128
PAD_FRONT = 112
TOK0 = PAD_FRONT + N_META
EPS = 1e-6
N_DEV = 8
RET_HEADS = 4
RET_DECAY_OFFSET = 5.0
ROPE_BASE = 10000.0
CONV_WIDTH = 31
HALO = 32
SB_SCALE = 64 ** -0.5
RET_SCALE = 128 ** -0.5
ADAM_LR, ADAM_B1, ADAM_B2, ADAM_EPS, ADAM_WD, ADAM_STEP = 0.001, 0.9, 0.999, 1e-08, 0.01, 10
VMEM_LIMIT = 56 * 1024 * 1024
MESH = pl.DeviceIdType.MESH


def _pcall(body, **kw):
    return pl.pallas_call(body, **kw)


def _params(**kw):
    return pltpu.CompilerParams(vmem_limit_bytes=VMEM_LIMIT, **kw)


def _tile(n, cands):
    for c in cands:
        if n % c == 0:
            return c
    raise ValueError(f"no tile for {n} in {cands}")


def _sigmoid(x):
    return 1.0 / (1.0 + jnp.exp(-x))


_DIMS = {
    "nn": (((1,), (0,)), ((), ())),
    "nt": (((1,), (1,)), ((), ())),
    "tn": (((0,), (0,)), ((), ())),
}


def _matmul(name, a, b, *, grid, a_spec, b_spec, o_spec, out_shape, contract, acc_shape,
            epi="plain", extra=None, extra_spec=None):
    nk = grid[2]
    dims = _DIMS[contract]
    n_in = 3 if extra is not None else 2
    n_out = 2 if epi == "relu2" else 1

    def body(*refs):
        a_ref, b_ref = refs[0], refs[1]
        e_ref = refs[2] if extra is not None else None
        outs = refs[n_in:n_in + n_out]
        acc = refs[-1]
        k = pl.program_id(2)
        part = lax.dot_general(a_ref[...].astype(BF16), b_ref[...].astype(BF16), dims, preferred_element_type=F32)
        if nk > 1:
            @pl.when(k == 0)
            def _():
                acc[...] = jnp.zeros_like(acc)

            acc[...] += part

        @pl.when(k == nk - 1)
        def _():
            r = acc[...] if nk > 1 else part
            if epi == "plain":
                outs[0][...] = r.astype(outs[0].dtype)
            elif epi == "residual":
                outs[0][...] = (r + e_ref[...]).astype(outs[0].dtype)
            elif epi == "relu2":
                outs[0][...] = r
                rr = jnp.maximum(r, 0.0)
                outs[1][...] = (rr * rr).astype(BF16)
            elif epi == "drelu2":
                outs[0][...] = (r * (2.0 * jnp.maximum(e_ref[...], 0.0))).astype(outs[0].dtype)

    in_specs = [a_spec, b_spec] + ([extra_spec] if extra is not None else [])
    args = (a, b) + ((extra,) if extra is not None else ())
    if n_out == 2:
        out_specs = [o_spec, o_spec]
    else:
        out_specs = o_spec
    return _pcall(body, name=name, grid=grid, in_specs=in_specs, out_specs=out_specs,
                  out_shape=out_shape, scratch_shapes=[pltpu.VMEM(acc_shape, F32)],
                  compiler_params=_params(dimension_semantics=("parallel", "parallel", "arbitrary")))(*args)


def _tm(t):
    return _tile(t, (1408, 768, 384, 128))


def _tm_tall(t):
    return _tile(t, (2112, 768, 384, 128))


def _mm_cols(name, a, wb, lead, out_dtype=F32, epi="plain"):
    t, kdim = a.shape
    n = wb.shape[-1]
    tm, tk = _tm_tall(t), _tile(kdim, (1024, 512))
    nl = len(lead)
    b_spec = pl.BlockSpec((None,) * (1 + nl) + (tk, n), lambda i, j, k: (j,) + lead + (k, 0))
    o_spec = pl.BlockSpec((tm, n), lambda i, j, k: (i, j))
    if epi == "relu2":
        out_shape = [jax.ShapeDtypeStruct((t, N_DEV * n), F32), jax.ShapeDtypeStruct((t, N_DEV * n), BF16)]
    else:
        out_shape = jax.ShapeDtypeStruct((t, N_DEV * n), out_dtype)
    return _matmul(name, a, wb, grid=(t // tm, N_DEV, kdim // tk),
                   a_spec=pl.BlockSpec((tm, tk), lambda i, j, k: (i, k)), b_spec=b_spec, o_spec=o_spec,
                   out_shape=out_shape, contract="nn", acc_shape=(tm, n), epi=epi)


def _tm_deep(t, kdim):
    return _tm(t) if kdim <= 2048 else _tile(t, (704, 384, 128))


def _mm_cols_t_rms(name, a, wb, h, g, dres):
    t = a.shape[0]
    nb, kdim, n = wb.shape
    tm = _tile(t, (704, 384, 128))

    def body(a_ref, b_ref, h_ref, g_ref, r_ref, o_ref, dg_ref):
        @pl.when(pl.program_id(0) == 0)
        def _():
            dg_ref[...] = jnp.zeros_like(dg_ref)

        d = _dot(a_ref[:, 0:n].astype(BF16), b_ref[0], "nt")
        for j in range(1, nb):
            d = d + _dot(a_ref[:, j * n:(j + 1) * n].astype(BF16), b_ref[j], "nt")
        x = h_ref[...]
        rs = lax.rsqrt(jnp.mean(x * x, axis=-1, keepdims=True) + EPS)
        u = d * g_ref[...]
        m = jnp.mean(u * x, axis=-1, keepdims=True)
        o_ref[...] = r_ref[...] + rs * u - x * (rs * rs * rs * m)
        dg_ref[...] += jnp.sum(d * x * rs, axis=0, keepdims=True)

    row = pl.BlockSpec((tm, kdim), lambda i: (i, 0))
    vec = pl.BlockSpec((1, kdim), lambda i: (0, 0))
    return _pcall(body, name=name, grid=(t // tm,),
                  in_specs=[pl.BlockSpec((tm, nb * n), lambda i: (i, 0)),
                            pl.BlockSpec((nb, kdim, n), lambda i: (0, 0, 0)), row, vec, row],
                  out_specs=[row, vec],
                  out_shape=[jax.ShapeDtypeStruct((t, kdim), F32), jax.ShapeDtypeStruct((1, kdim), F32)],
                  compiler_params=_params(dimension_semantics=("arbitrary",)))(a, wb, h, g, dres)


def _mm_rows_t(name, a, wb, lead, out_dtype=F32, epi="plain", extra=None):
    t, n = a.shape
    r = wb.shape[-2]
    tm, tk = _tm_tall(t), _tile(n, (1024,))
    nl = len(lead)
    b_spec = pl.BlockSpec((None,) * (1 + nl) + (r, tk), lambda i, j, k: (j,) + lead + (0, k))
    o_spec = pl.BlockSpec((tm, r), lambda i, j, k: (i, j))
    return _matmul(name, a, wb, grid=(t // tm, N_DEV, n // tk),
                   a_spec=pl.BlockSpec((tm, tk), lambda i, j, k: (i, k)), b_spec=b_spec, o_spec=o_spec,
                   out_shape=jax.ShapeDtypeStruct((t, N_DEV * r), out_dtype), contract="nt",
                   acc_shape=(tm, r), epi=epi, extra=extra, extra_spec=o_spec if extra is not None else None)


def _mm_rows_loss(name, a, wb, residual, target):
    t = a.shape[0]
    nb, r, n = wb.shape
    tm, tn = _tm_deep(t, nb * r), _tile(n, (512,))

    def body(a_ref, b_ref, r_ref, t_ref, d_ref, l_ref):
        i = pl.program_id(0)

        @pl.when((i == 0) & (pl.program_id(1) == 0))
        def _():
            l_ref[...] = jnp.zeros_like(l_ref)

        y = r_ref[...] + _dot(a_ref[...].astype(BF16), b_ref[...].reshape(nb * r, tn))
        diff = jnp.where(_row_ids(i, tm) >= TOK0, y - t_ref[...], 0.0)
        d_ref[...] = diff * (1.0 / D_MODEL)
        l_ref[...] += jnp.sum(diff * diff) * (0.5 / D_MODEL)

    o_spec = pl.BlockSpec((tm, tn), lambda i, j: (i, j))
    return _pcall(body, name=name, grid=(t // tm, n // tn),
                  in_specs=[pl.BlockSpec((tm, nb * r), lambda i, j: (i, 0)),
                            pl.BlockSpec((nb, r, tn), lambda i, j: (0, 0, j)), o_spec, o_spec],
                  out_specs=[o_spec, pl.BlockSpec((8, 128), lambda i, j: (0, 0))],
                  out_shape=[jax.ShapeDtypeStruct((t, n), F32), jax.ShapeDtypeStruct((8, 128), F32)],
                  compiler_params=_params(dimension_semantics=("arbitrary", "arbitrary")))(a, wb, residual, target)


def _mm_rows_norm(name, a, wb, residual, g):
    t = a.shape[0]
    nb, r, n = wb.shape
    tm = _tile(t, (704, 384, 128))

    def body(a_ref, b_ref, r_ref, g_ref, h_ref, hn_ref):
        h = r_ref[...] + _dot(a_ref[...].astype(BF16), b_ref[...].reshape(nb * r, n))
        h_ref[...] = h
        hn_ref[...] = (h * lax.rsqrt(jnp.mean(h * h, axis=-1, keepdims=True) + EPS) * g_ref[...]).astype(BF16)

    row = pl.BlockSpec((tm, n), lambda i: (i, 0))
    return _pcall(body, name=name, grid=(t // tm,),
                  in_specs=[pl.BlockSpec((tm, nb * r), lambda i: (i, 0)), pl.BlockSpec((nb, r, n), lambda i: (0, 0, 0)),
                            row, pl.BlockSpec((1, n), lambda i: (0, 0))],
                  out_specs=[row, row],
                  out_shape=[jax.ShapeDtypeStruct((t, n), F32), jax.ShapeDtypeStruct((t, n), BF16)],
                  compiler_params=_params(dimension_semantics=("parallel",)))(a, wb, residual, g)


def _wgrad_cols(name, x, dy, n):
    t, kdim = x.shape
    tk = _tm_tall(t)
    return _matmul(name, x, dy, grid=(1, N_DEV, t // tk),
                   a_spec=pl.BlockSpec((tk, kdim), lambda i, j, k: (k, 0)),
                   b_spec=pl.BlockSpec((tk, n), lambda i, j, k: (k, j)),
                   o_spec=pl.BlockSpec((None, kdim, n), lambda i, j, k: (j, 0, 0)),
                   out_shape=jax.ShapeDtypeStruct((N_DEV, kdim, n), BF16), contract="tn", acc_shape=(kdim, n))


def _wgrad_rows(name, x, dy, r):
    t = x.shape[0]
    n = dy.shape[1]
    tk, tn = _tm_tall(t), _tile(n, (512,))
    tm = min(N_DEV * r, 1024)
    out = _matmul(name, x, dy, grid=(N_DEV * r // tm, n // tn, t // tk),
                  a_spec=pl.BlockSpec((tk, tm), lambda i, j, k: (k, i)),
                  b_spec=pl.BlockSpec((tk, tn), lambda i, j, k: (k, j)),
                  o_spec=pl.BlockSpec((tm, tn), lambda i, j, k: (i, j)),
                  out_shape=jax.ShapeDtypeStruct((N_DEV * r, n), BF16), contract="tn", acc_shape=(tm, tn))
    return out.reshape(N_DEV, r, n)


def _rows(t):
    return _tile(t, (384, 128))


def _rms_fwd(name, h, g):
    t = h.shape[0]
    tr = _rows(t)

    def body(h_ref, g_ref, o_ref):
        x = h_ref[...]
        r = lax.rsqrt(jnp.mean(x * x, axis=-1, keepdims=True) + EPS)
        o_ref[...] = (x * r * g_ref[...]).astype(BF16)

    row = pl.BlockSpec((tr, D_MODEL), lambda i: (i, 0))
    vec = pl.BlockSpec((1, D_MODEL), lambda i: (0, 0))
    return _pcall(body, name=name, grid=(t // tr,), in_specs=[row, vec], out_specs=row,
                  out_shape=jax.ShapeDtypeStruct((t, D_MODEL), BF16))(h, g)


def _ret_tables(t):
    hh = np.arange(RET_HEADS, dtype=np.float64)
    log_g = np.log1p(-np.exp2(-RET_DECAY_OFFSET - hh))
    idx = np.arange(CHUNK, dtype=np.float64)
    diff = idx[:, None] - idx[None, :]
    dmat = np.where(diff[None] >= 0, np.exp(np.maximum(diff, 0.0)[None] * log_g[:, None, None]), 0.0)
    qdec = np.exp((idx + 1.0)[None, :, None] * log_g[:, None, None]) * np.ones((1, 1, CHUNK))
    kdec = np.exp((CHUNK - 1 - idx)[None, :, None] * log_g[:, None, None]) * np.ones((1, 1, CHUNK))
    half = CHUNK // 2
    inv_freq = (ROPE_BASE ** (-np.arange(half, dtype=np.float32) / half)).astype(np.float32)
    ang = (np.arange(t, dtype=np.float32)[:, None] * inv_freq[None, :]).astype(np.float32).astype(np.float64)
    cos2 = np.concatenate([np.cos(ang), np.cos(ang)], axis=1)
    sin2 = np.concatenate([-np.sin(ang), np.sin(ang)], axis=1)
    return tuple(jnp.asarray(v, F32) for v in (dmat, qdec, kdec, cos2, sin2))


def _rot(x, c, s):
    return x * c + pltpu.roll(x, CHUNK // 2, 1) * s


def _unrot(dx, c, s):
    return dx * c + pltpu.roll(dx * s, CHUNK // 2, 1)


def _dot(a, b, contract="nn"):
    return lax.dot_general(a, b, _DIMS[contract], preferred_element_type=F32)


def _ret_fwd(proj, gn_g, tables):
    t = proj.shape[0]
    nch = t // CHUNK
    dmat, qdec, kdec, cos2, sin2 = tables

    def body(qk_ref, v_ref, g_ref, w_ref, c_ref, s_ref, dm_ref, qd_ref, kd_ref, o_ref, st_ref, cat_ref, state):
        @pl.when(pl.program_id(0) == 0)
        def _():
            state[...] = jnp.zeros_like(state)

        c, s = c_ref[...], s_ref[...]
        for h in range(RET_HEADS):
            q = _rot(qk_ref[:, 128 * h:128 * (h + 1)], c, s)
            k = _rot(qk_ref[:, 512 + 128 * h:512 + 128 * (h + 1)], c, s) * RET_SCALE
            vb = v_ref[:, 256 * h:256 * (h + 1)].astype(BF16)
            st = state[h]
            st_ref[h] = st
            sc = _dot(q.astype(BF16), k.astype(BF16), "nt") * dm_ref[h]
            o = _dot(sc.astype(BF16), vb)
            o += _dot((q * qd_ref[h]).astype(BF16), st.astype(BF16))
            sl = slice(256 * h, 256 * (h + 1))
            o_ref[:, sl] = o
            kv = _dot((k * kd_ref[h]).astype(BF16), vb, "tn")
            state[h] = qd_ref[h, CHUNK - 1:CHUNK, 0:1] * st + kv
            mu = jnp.mean(o, axis=-1, keepdims=True)
            oc = o - mu
            rstd = lax.rsqrt(jnp.mean(oc * oc, axis=-1, keepdims=True) + EPS)
            g = g_ref[:, sl]
            cat_ref[:, sl] = (g * _sigmoid(g) * (oc * rstd * w_ref[:, sl])).astype(BF16)

    tab = pl.BlockSpec((RET_HEADS, CHUNK, CHUNK), lambda n: (0, 0, 0))
    pos = pl.BlockSpec((CHUNK, CHUNK), lambda n: (n, 0))
    row = pl.BlockSpec((CHUNK, 1024), lambda n: (n, 0))
    return _pcall(
        body, name="ret_fwd", grid=(nch,),
        in_specs=[row, pl.BlockSpec((CHUNK, 1024), lambda n: (n, 1)), pl.BlockSpec((CHUNK, 1024), lambda n: (n, 2)),
                  pl.BlockSpec((1, 1024), lambda n: (0, 0)), pos, pos, tab, tab, tab],
        out_specs=[row, pl.BlockSpec((RET_HEADS, None, 128, 256), lambda n: (0, n, 0, 0)), row],
        out_shape=[jax.ShapeDtypeStruct((t, 1024), F32), jax.ShapeDtypeStruct((RET_HEADS, nch, 128, 256), F32),
                   jax.ShapeDtypeStruct((t, 2048), BF16)],
        scratch_shapes=[pltpu.VMEM((RET_HEADS, 128, 256), F32)],
        compiler_params=_params(dimension_semantics=("arbitrary",)))(
            proj, proj, proj, gn_g, cos2, sin2, dmat, qdec, kdec)


def _ret_bwd(dproj, proj, states, do, tables):
    t = proj.shape[0]
    nch = t // CHUNK
    dmat, qdec, kdec, cos2, sin2 = tables

    def body(dp_in, qk_ref, v_ref, do_ref, st_ref, c_ref, s_ref, dm_ref, qd_ref, kd_ref, dp_ref, rst):
        del dp_in
        @pl.when(pl.program_id(0) == 0)
        def _():
            rst[...] = jnp.zeros_like(rst)

        c, s = c_ref[...], s_ref[...]
        for h in range(RET_HEADS):
            q = _rot(qk_ref[:, 128 * h:128 * (h + 1)], c, s)
            k = _rot(qk_ref[:, 512 + 128 * h:512 + 128 * (h + 1)], c, s) * RET_SCALE
            qb, kb = q.astype(BF16), k.astype(BF16)
            vb = v_ref[:, 256 * h:256 * (h + 1)].astype(BF16)
            dob = do_ref[:, 256 * h:256 * (h + 1)].astype(BF16)
            pb = st_ref[h].astype(BF16)
            r = rst[h]
            rb = r.astype(BF16)
            dm, qd, kd = dm_ref[h], qd_ref[h], kd_ref[h]
            sb = (_dot(qb, kb, "nt") * dm).astype(BF16)
            dsb = (_dot(dob, vb, "nt") * dm).astype(BF16)
            dq = _dot(dsb, kb) + _dot(dob, pb, "nt") * qd
            dk = _dot(dsb, qb, "tn") + _dot(vb, rb, "nt") * kd
            dv = _dot(sb, dob, "tn") + _dot((k * kd).astype(BF16), rb)
            rst[h] = _dot((q * qd).astype(BF16), dob, "tn") + qd[CHUNK - 1:CHUNK, 0:1] * r
            dp_ref[:, 128 * h:128 * (h + 1)] = _unrot(dq, c, s).astype(BF16)
            dp_ref[:, 512 + 128 * h:512 + 128 * (h + 1)] = (_unrot(dk, c, s) * RET_SCALE).astype(BF16)
            dp_ref[:, 1024 + 256 * h:1024 + 256 * (h + 1)] = dv.astype(BF16)

    rev = lambda n: nch - 1 - n
    tab = pl.BlockSpec((RET_HEADS, CHUNK, CHUNK), lambda n: (0, 0, 0))
    pos = pl.BlockSpec((CHUNK, CHUNK), lambda n: (rev(n), 0))
    row = pl.BlockSpec((CHUNK, 1024), lambda n: (rev(n), 0))
    return _pcall(
        body, name="ret_bwd", grid=(nch,),
        in_specs=[pl.BlockSpec(memory_space=pl.ANY), row, pl.BlockSpec((CHUNK, 1024), lambda n: (rev(n), 1)), row,
                  pl.BlockSpec((RET_HEADS, None, 128, 256), lambda n: (0, rev(n), 0, 0)),
                  pos, pos, tab, tab, tab],
        out_specs=pl.BlockSpec((CHUNK, 2048), lambda n: (rev(n), 0)),
        out_shape=jax.ShapeDtypeStruct((t, 5120), BF16),
        scratch_shapes=[pltpu.VMEM((RET_HEADS, 128, 256), F32)], input_output_aliases={0: 0},
        compiler_params=_params(dimension_semantics=("arbitrary",)))(
            dproj, proj, proj, do, states, cos2, sin2, dmat, qdec, kdec)


def _row_ids(i, tr):
    return i * tr + lax.broadcasted_iota(jnp.int32, (tr, 1), 0)


SH_ROWS = HALO - 8
CONV_VPU_TAPS = 21


def _shifted_copies(xs, sh, tr):
    for b in range(1, 8):
        sh[b - 1] = xs[pl.ds(b, tr + SH_ROWS), :]


def _shifted(xs, sh, off, tr, lanes=slice(None)):
    a, b = divmod(off, 8)
    return xs[pl.ds(8 * a, tr), lanes] if b == 0 else sh[b - 1, pl.ds(8 * a, tr), lanes]


def _taps_mxu(xs, sh, w_ref, offs, tr, first=0):
    sub = lax.broadcasted_iota(jnp.int32, (256, 128), 0)
    eye = (sub & 127) == lax.broadcasted_iota(jnp.int32, (256, 128), 1)
    outs = []
    for c in range(8):
        lanes = slice(128 * c, 128 * (c + 1))
        acc = None
        for w in range(first, len(offs), 2):
            wb = min(w + 1, len(offs) - 1)
            w_hi = w_ref[w:w + 1, lanes]
            w_lo = w_ref[wb:wb + 1, lanes] if wb > w else jnp.zeros((1, 128), F32)
            dmat = jnp.where(eye, jnp.where(sub < 128, w_hi, w_lo), 0.0).astype(BF16)
            lhs = jnp.concatenate([_shifted(xs, sh, offs[w], tr, lanes).astype(BF16),
                                   _shifted(xs, sh, offs[wb], tr, lanes).astype(BF16)], axis=1)
            d = _dot(lhs, dmat)
            acc = d if acc is None else acc + d
        outs.append(acc)
    return jnp.concatenate(outs, axis=1)


def _conv_fwd(cat, proj, conv_w, conv_b, ln_g, ln_b):
    t = proj.shape[0]
    tr = _rows(t)
    hb = tr // HALO

    def body(cat_in, ua_ref, ug_ref, pa_ref, pg_ref, w_ref, b_ref, lg_ref, lb_ref, c_ref, hd_ref, y_ref, xs, sh):
        del cat_in
        i = pl.program_id(0)
        hdn = ua_ref[...] * _sigmoid(ug_ref[...])
        hd_ref[...] = hdn
        prev = pa_ref[...] * _sigmoid(pg_ref[...])
        xs[0:HALO, :] = jnp.where(i > 0, prev, 0.0)
        xs[HALO:HALO + tr, :] = hdn
        _shifted_copies(xs, sh, tr)
        offs = [HALO - (CONV_WIDTH - 1) + w for w in range(CONV_WIDTH)]
        acc = _taps_mxu(xs, sh, w_ref, offs, tr, first=CONV_VPU_TAPS) + b_ref[...]
        for w in range(CONV_VPU_TAPS):
            acc += w_ref[w:w + 1, :] * _shifted(xs, sh, offs[w], tr)
        y_ref[...] = acc
        mu = jnp.mean(acc, axis=-1, keepdims=True)
        yc = acc - mu
        rstd = lax.rsqrt(jnp.mean(yc * yc, axis=-1, keepdims=True) + EPS)
        yn = yc * rstd * lg_ref[...] + lb_ref[...]
        c = yn * _sigmoid(yn)
        c_ref[...] = jnp.where(_row_ids(i, tr) >= PAD_FRONT, c, 0.0).astype(BF16)

    row = pl.BlockSpec((tr, 1024), lambda i: (i, 0))
    vec = pl.BlockSpec((1, 1024), lambda i: (0, 0))
    halo = lambda col: pl.BlockSpec((HALO, 1024), lambda i: (jnp.maximum(i * hb - 1, 0), col))
    return _pcall(body, name="conv_fwd", grid=(t // tr,),
                  in_specs=[pl.BlockSpec(memory_space=pl.ANY),
                            pl.BlockSpec((tr, 1024), lambda i: (i, 3)), pl.BlockSpec((tr, 1024), lambda i: (i, 4)),
                            halo(3), halo(4), pl.BlockSpec((32, 1024), lambda i: (0, 0)), vec, vec, vec],
                  out_specs=[pl.BlockSpec((tr, 1024), lambda i: (i, 1)), row, row],
                  out_shape=[jax.ShapeDtypeStruct((t, 2048), BF16), jax.ShapeDtypeStruct((t, 1024), F32),
                             jax.ShapeDtypeStruct((t, 1024), F32)],
                  scratch_shapes=[pltpu.VMEM((tr + HALO, 1024), F32), pltpu.VMEM((7, tr + SH_ROWS, 1024), F32)],
                  input_output_aliases={0: 0}, compiler_params=_params())(
                      cat, proj, proj, proj, proj, conv_w, conv_b, ln_g, ln_b)


def _mix_bwd_head(dh, w_out, o, proj, gn_g, y, ln_g, ln_b):
    t = dh.shape[0]
    tr = _rows(t)
    nb, r, n = w_out.shape

    def body(dh_ref, b_ref, o_ref, g_ref, w_ref, y_ref, lg_ref, lb_ref,
             do_ref, dp_ref, dw_ref, dy_ref, dlg_ref, dlb_ref, dcb_ref):
        i = pl.program_id(0)

        @pl.when(i == 0)
        def _():
            for ref in (dw_ref, dlg_ref, dlb_ref, dcb_ref):
                ref[...] = jnp.zeros_like(ref)

        dcat = _dot(dh_ref[...].astype(BF16), b_ref[...].reshape(nb * r, n), "nt")
        for h in range(RET_HEADS):
            sl = slice(256 * h, 256 * (h + 1))
            x = o_ref[:, sl]
            mu = jnp.mean(x, axis=-1, keepdims=True)
            xc = x - mu
            rstd = lax.rsqrt(jnp.mean(xc * xc, axis=-1, keepdims=True) + EPS)
            xh = xc * rstd
            w = w_ref[:, sl]
            g = g_ref[:, sl]
            sg = _sigmoid(g)
            d = dcat[:, sl]
            don = d * (g * sg)
            dp_ref[:, sl] = (d * (xh * w) * (sg * (1.0 + g * (1.0 - sg)))).astype(BF16)
            dw_ref[:, sl] += jnp.sum(don * xh, axis=0, keepdims=True)
            dxh = don * w
            m1 = jnp.mean(dxh, axis=-1, keepdims=True)
            m2 = jnp.mean(dxh * xh, axis=-1, keepdims=True)
            do_ref[:, sl] = rstd * (dxh - m1 - xh * m2)
        yv = y_ref[...]
        mu = jnp.mean(yv, axis=-1, keepdims=True)
        yc = yv - mu
        rstd = lax.rsqrt(jnp.mean(yc * yc, axis=-1, keepdims=True) + EPS)
        xh = yc * rstd
        lg = lg_ref[...]
        yn = xh * lg + lb_ref[...]
        sg = _sigmoid(yn)
        dyn = jnp.where(_row_ids(i, tr) >= PAD_FRONT, dcat[:, 1024:] * (sg * (1.0 + yn * (1.0 - sg))), 0.0)
        dlg_ref[...] += jnp.sum(dyn * xh, axis=0, keepdims=True)
        dlb_ref[...] += jnp.sum(dyn, axis=0, keepdims=True)
        dxh = dyn * lg
        m1 = jnp.mean(dxh, axis=-1, keepdims=True)
        m2 = jnp.mean(dxh * xh, axis=-1, keepdims=True)
        dy = rstd * (dxh - m1 - xh * m2)
        dy_ref[...] = dy
        dcb_ref[...] += jnp.sum(dy, axis=0, keepdims=True)

    row = pl.BlockSpec((tr, 1024), lambda i: (i, 0))
    vec = pl.BlockSpec((1, 1024), lambda i: (0, 0))
    gate = pl.BlockSpec((tr, 1024), lambda i: (i, 2))
    vsh = jax.ShapeDtypeStruct((1, 1024), F32)
    fsh = jax.ShapeDtypeStruct((t, 1024), F32)
    return _pcall(body, name="mix_bwd_head", grid=(t // tr,),
                  in_specs=[row, pl.BlockSpec((nb, r, n), lambda i: (0, 0, 0)), row, gate, vec, row, vec, vec],
                  out_specs=[row, gate, vec, row, vec, vec, vec],
                  out_shape=[fsh, jax.ShapeDtypeStruct((t, 5120), BF16), vsh, fsh, vsh, vsh, vsh],
                  compiler_params=_params(dimension_semantics=("arbitrary",)))(
                      dh, w_out, o, proj, gn_g, y, ln_g, ln_b)


def _conv_bwd_taps(dproj, dy, hdn, proj, conv_w):
    t = dy.shape[0]
    tr = _rows(t)
    hb = tr // HALO
    nt = t // tr

    def body(dp_in, dy_ref, nx_ref, hd_ref, ph_ref, ua_ref, ug_ref, w_ref, da_ref, dg_ref, dw_ref, xs, sh):
        del dp_in
        i = pl.program_id(0)

        @pl.when(i == 0)
        def _():
            dw_ref[...] = jnp.zeros_like(dw_ref)

        dy = dy_ref[...]
        xs[0:tr, :] = dy
        xs[tr:tr + HALO, :] = jnp.where(i < nt - 1, nx_ref[...], 0.0)
        _shifted_copies(xs, sh, tr)
        dh = _taps_mxu(xs, sh, w_ref, [CONV_WIDTH - 1 - w for w in range(CONV_WIDTH)], tr)
        xs[0:HALO, :] = jnp.where(i > 0, ph_ref[...], 0.0)
        xs[HALO:HALO + tr, :] = hd_ref[...]
        _shifted_copies(xs, sh, tr)
        for w in range(CONV_WIDTH):
            dw_ref[w:w + 1, :] += jnp.sum(dy * _shifted(xs, sh, HALO - (CONV_WIDTH - 1) + w, tr), axis=0, keepdims=True)
        dh = jnp.where(_row_ids(i, tr) >= PAD_FRONT, dh, 0.0)
        sg = _sigmoid(ug_ref[...])
        da_ref[...] = (dh * sg).astype(BF16)
        dg_ref[...] = (dh * ua_ref[...] * sg * (1.0 - sg)).astype(BF16)

    row = pl.BlockSpec((tr, 1024), lambda i: (i, 0))
    return _pcall(body, name="conv_bwd_taps", grid=(nt,),
                  in_specs=[pl.BlockSpec(memory_space=pl.ANY),
                            row, pl.BlockSpec((HALO, 1024), lambda i: (jnp.minimum((i + 1) * hb, nt * hb - 1), 0)),
                            row, pl.BlockSpec((HALO, 1024), lambda i: (jnp.maximum(i * hb - 1, 0), 0)),
                            pl.BlockSpec((tr, 1024), lambda i: (i, 3)), pl.BlockSpec((tr, 1024), lambda i: (i, 4)),
                            pl.BlockSpec((32, 1024), lambda i: (0, 0))],
                  out_specs=[pl.BlockSpec((tr, 1024), lambda i: (i, 3)), row, pl.BlockSpec((32, 1024), lambda i: (0, 0))],
                  out_shape=[jax.ShapeDtypeStruct((t, 5120), BF16), jax.ShapeDtypeStruct((t, 1024), BF16),
                             jax.ShapeDtypeStruct((32, 1024), F32)],
                  scratch_shapes=[pltpu.VMEM((tr + HALO, 1024), F32), pltpu.VMEM((7, tr + SH_ROWS, 1024), F32)],
                  input_output_aliases={0: 0}, compiler_params=_params())(
                      dproj, dy, dy, hdn, hdn, proj, proj, conv_w)


NEG_BIG = -1e30


def _seg_tables(qb):
    j = np.arange(128)
    bd = (j[:, None] // 64 == j[None, :] // 64).astype(np.float32)
    ones = np.ones((128, 128), np.float32)
    later = np.concatenate([(j[:, None] >= j[None, :]).astype(np.float32), ones], axis=1)
    earlier = np.concatenate([(j[:, None] < j[None, :]).astype(np.float32), ones], axis=1)
    per = qb // CHUNK
    row = np.arange(qb)[:, None]
    pad = np.broadcast_to(j[None, :] < PAD_FRONT, (qb, 128))
    diag = [(g * CHUNK + j[None, :]) >= row for g in range(per)]
    masks = diag + [np.zeros((qb, 128), bool), pad, diag[0] | pad]
    bias = np.stack([np.where(m, NEG_BIG, 0.0) for m in masks]).astype(np.float32)
    dup = lambda m: np.concatenate([m, m], axis=0)
    return (jnp.asarray(bd, BF16), jnp.asarray(dup(later), BF16), jnp.asarray(dup(earlier), BF16),
            jnp.asarray(bias, F32))


def _split_dot(x, m):
    hi = x.astype(BF16)
    lo = (x - hi.astype(F32)).astype(BF16)
    return _dot(hi, m) + _dot(lo, m)


def _qk_norm_fwd(qkv, qg, kg, bd):
    t = qkv.shape[0]
    tr = _rows(t)
    nb = tr // CHUNK

    def body(q_ref, k_ref, v_ref, qg_ref, kg_ref, bd_ref, qo, kt, k2, vt, v2):
        bdm = bd_ref[...]
        lane = lax.broadcasted_iota(jnp.int32, (1, 128), 1)
        sub = lax.broadcasted_iota(jnp.int32, (128, 1), 0)

        def pair_layouts(x, t_ref, s_ref, hp, b):
            xt = x.T
            t_ref[hp, b] = jnp.concatenate([jnp.where(sub < 64, xt, 0.0), jnp.where(sub >= 64, xt, 0.0)],
                                           axis=1).astype(BF16)
            s_ref[hp, b] = jnp.concatenate([jnp.where(lane < 64, x, 0.0), jnp.where(lane >= 64, x, 0.0)],
                                           axis=0).astype(BF16)

        for hp in range(8):
            sl = slice(128 * hp, 128 * (hp + 1))
            x = q_ref[:, sl]
            r = lax.rsqrt(_split_dot(x * x, bdm) * (1.0 / 64) + EPS)
            qo[:, sl] = (x * r * (qg_ref[:, sl] * SB_SCALE)).astype(BF16)
            x = k_ref[:, sl]
            r = lax.rsqrt(_split_dot(x * x, bdm) * (1.0 / 64) + EPS)
            kn = x * r * kg_ref[:, sl]
            v = v_ref[:, sl]
            for b in range(nb):
                rows = slice(CHUNK * b, CHUNK * (b + 1))
                pair_layouts(kn[rows], kt, k2, hp, b)
                pair_layouts(v[rows], vt, v2, hp, b)

    col = lambda c: pl.BlockSpec((tr, 1024), lambda i: (i, c))
    vec = pl.BlockSpec((1, 1024), lambda i: (0, 0))
    wide = pl.BlockSpec((8, nb, 128, 256), lambda i: (0, i, 0, 0))
    tall = pl.BlockSpec((8, nb, 256, 128), lambda i: (0, i, 0, 0))
    wsh = jax.ShapeDtypeStruct((8, t // CHUNK, 128, 256), BF16)
    tsh = jax.ShapeDtypeStruct((8, t // CHUNK, 256, 128), BF16)
    return _pcall(body, name="qk_norm_fwd", grid=(t // tr,),
                  in_specs=[col(0), col(1), col(2), vec, vec, pl.BlockSpec((128, 128), lambda i: (0, 0))],
                  out_specs=[col(0), wide, tall, wide, tall],
                  out_shape=[jax.ShapeDtypeStruct((t, 1024), BF16), wsh, tsh, wsh, tsh])(qkv, qkv, qkv, qg, kg, bd)


def _qk_norm_bwd(qkv, dq, dk, dv, qg, kg, bd):
    t = qkv.shape[0]
    tr = _rows(t)

    def body(q_ref, k_ref, dq_ref, dk_ref, dv_ref, qg_ref, kg_ref, bd_ref, o_ref, dqg_ref, dkg_ref):
        @pl.when(pl.program_id(0) == 0)
        def _():
            dqg_ref[...] = jnp.zeros_like(dqg_ref)
            dkg_ref[...] = jnp.zeros_like(dkg_ref)

        bdm = bd_ref[...]
        for part, (src, d_ref, g_ref, dg_ref) in enumerate(((q_ref, dq_ref, qg_ref, dqg_ref),
                                                           (k_ref, dk_ref, kg_ref, dkg_ref))):
            for cix in range(8):
                sl = slice(128 * cix, 128 * (cix + 1))
                x = src[:, sl]
                d = d_ref[:, sl]
                r = lax.rsqrt(_split_dot(x * x, bdm) * (1.0 / 64) + EPS)
                u = d * g_ref[:, sl]
                m = _split_dot(u * x, bdm) * (1.0 / 64)
                o_ref[:, 1024 * part + 128 * cix:1024 * part + 128 * (cix + 1)] = (r * u - x * (r * r * r * m)).astype(BF16)
                dg_ref[:, sl] += jnp.sum(d * x * r, axis=0, keepdims=True)
        o_ref[:, 2048:3072] = dv_ref[...].astype(BF16)

    col = lambda c: pl.BlockSpec((tr, 1024), lambda i: (i, c))
    vec = pl.BlockSpec((1, 1024), lambda i: (0, 0))
    vsh = jax.ShapeDtypeStruct((1, 1024), F32)
    return _pcall(body, name="qk_norm_bwd", grid=(t // tr,),
                  in_specs=[col(0), col(1), col(0), col(0), col(0), vec, vec, pl.BlockSpec((128, 128), lambda i: (0, 0))],
                  out_specs=[pl.BlockSpec((tr, 3072), lambda i: (i, 0)), vec, vec],
                  out_shape=[jax.ShapeDtypeStruct((t, 3072), BF16), vsh, vsh])(qkv, qkv, dq, dk, dv, qg, kg, bd)


def _split2(x):
    hi = x.astype(BF16)
    lo = (x - hi.astype(F32)).astype(BF16)
    return jnp.concatenate([hi, lo], axis=1)


def _sb_sums(z, later_tab):
    sp = jnp.maximum(z, 0.0) + jnp.log(1.0 + jnp.exp(-jnp.abs(z)))
    return _dot(_split2(sp), later_tab)


def _sb_bias_index(i, kb, per):
    g = kb - i * per
    return jnp.where(kb == 0, jnp.where(i == 0, per + 2, per + 1), jnp.where(g >= 0, g, per))


def _sb_qb(t):
    return _tile(t, (384, 128))


def _sb_fwd(qh, kt, v2, later_tab, bias_tab):
    t = qh.shape[0]
    qb = _sb_qb(t)
    per = qb // CHUNK
    nkb_all = t // CHUNK

    nq = t // qb

    def body(q_ref, kt_ref, v2_ref, tab_ref, bias_ref, o_ref, ws_ref, acc, carry, zbuf, wbuf, wsem):
        h, i = pl.program_id(0), pl.program_id(1)
        n = h * nq + i
        p = n & 1
        q = q_ref[...]
        acc[...] = jnp.zeros_like(acc)
        carry[...] = jnp.zeros_like(carry)
        nkb = (i + 1) * per
        save = lambda kb: pltpu.make_async_copy(wbuf.at[p, kb], ws_ref.at[h, i, kb], wsem.at[p, kb])

        def drain(step, par):
            hs, is_ = step // nq, step % nq

            def one(kb, _):
                pltpu.make_async_copy(wbuf.at[par, kb], ws_ref.at[hs, is_, kb], wsem.at[par, kb]).wait()
                return 0

            lax.fori_loop(0, (is_ + 1) * per, one, 0)

        @pl.when(n >= 2)
        def _():
            drain(n - 2, p)

        for u in range(per):
            zbuf[u] = _dot(q, kt_ref[nkb - 1 - u])

        def trip(s, diagonal):
            top = nkb - 1 - per * s
            if not diagonal:
                for u in range(per):
                    save(top + per - u).start()
            z2s = [zbuf[u] for u in range(per)]
            for u in range(per):
                zbuf[u] = _dot(q, kt_ref[jnp.maximum(top - per - u, 0)])
            first = [CHUNK * (per - 1 - u) if diagonal else 0 for u in range(per)]
            cins = [carry[0], carry[1]]
            zs, cus = [], []
            for u in range(per):
                zs.append([z2s[u][first[u]:, 128 * hh:128 * (hh + 1)] for hh in range(2)])
                if diagonal or u == per - 1:
                    bias = bias_ref[_sb_bias_index(i, top - u, per)][first[u]:]
                    zs[u] = [z + bias for z in zs[u]]
                cus.append([_sb_sums(z, tab_ref[...]) for z in zs[u]])
            part = None
            for u in range(per):
                kb, lo = top - u, first[u]
                for hh in range(2):
                    sl = slice(128 * hh, 128 * (hh + 1))
                    cu = cus[u][hh]
                    wbuf[p, kb, lo:, sl] = jnp.exp(zs[u][hh] - cu[:, :128] - cins[hh][lo:]).astype(BF16)
                    if lo:
                        wbuf[p, kb, :lo, sl] = jnp.zeros((lo, 128), BF16)
                        cins[hh] = jnp.concatenate([cins[hh][:lo], cins[hh][lo:] + cu[:, 128:]], axis=0)
                    else:
                        cins[hh] = cins[hh] + cu[:, 128:]
                d = _dot(wbuf[p, kb], v2_ref[kb])
                part = d if part is None else part + d
            carry[0], carry[1] = cins[0], cins[1]
            acc[...] += part

        trip(0, True)

        def step(s, _):
            trip(s, False)
            return 0

        lax.fori_loop(1, nkb // per, step, 0)
        for u in range(per):
            save(per - 1 - u).start()
        o_ref[...] = acc[...]

        @pl.when(n == 8 * nq - 1)
        def _():
            drain(n - 1, 1 - p)
            drain(n, p)

    blk = pl.BlockSpec((qb, 128), lambda h, i: (i, h))
    wide = pl.BlockSpec((None, nkb_all, 128, 256), lambda h, i: (h, 0, 0, 0))
    tall = pl.BlockSpec((None, nkb_all, 256, 128), lambda h, i: (h, 0, 0, 0))
    return _pcall(body, name="sb_fwd", grid=(8, t // qb),
                  in_specs=[blk, wide, tall, pl.BlockSpec((256, 256), lambda h, i: (0, 0)),
                            pl.BlockSpec((per + 3, qb, 128), lambda h, i: (0, 0, 0))],
                  out_specs=[blk, pl.BlockSpec(memory_space=pl.ANY)],
                  out_shape=[jax.ShapeDtypeStruct((t, 1024), F32),
                             jax.ShapeDtypeStruct((8, t // qb, nkb_all, qb, 256), BF16)],
                  scratch_shapes=[pltpu.VMEM((qb, 128), F32), pltpu.VMEM((2, qb, 128), F32),
                                  pltpu.VMEM((per, qb, 256), F32), pltpu.VMEM((2, nkb_all, qb, 256), BF16),
                                  pltpu.SemaphoreType.DMA((2, nkb_all))],
                  compiler_params=_params(dimension_semantics=("arbitrary", "arbitrary")))(
                      qh, kt, v2, later_tab, bias_tab)


def _sb_bwd(qh, kt, k2, vt, wsave, do, earlier_tab, bias_tab):
    t = qh.shape[0]
    qb = _sb_qb(t)
    per = qb // CHUNK
    nkb_all = t // CHUNK

    zero_slot = nkb_all
    nq = t // qb

    def body(q_ref, kt_ref, k2_ref, vt_ref, ws_ref, do_ref, etab_ref, bias_ref,
             dq_ref, dk_ref, dv_ref, acc, gcarry, zbuf, dwbuf, wbuf, wsem, dzbuf):
        h, i = pl.program_id(0), pl.program_id(1)
        n = h * nq + i
        p = n & 1

        @pl.when(i == 0)
        def _():
            dk_ref[...] = jnp.zeros_like(dk_ref)
            dv_ref[...] = jnp.zeros_like(dv_ref)

        nkb = (i + 1) * per
        fetch = lambda kb: pltpu.make_async_copy(ws_ref.at[h, i, kb], wbuf.at[p, kb], wsem.at[p, kb])

        def prefetch(step, par):
            hs, is_ = step // nq, step % nq

            def one(kb, _):
                pltpu.make_async_copy(ws_ref.at[hs, is_, kb], wbuf.at[par, kb], wsem.at[par, kb]).start()
                return 0

            lax.fori_loop(0, (is_ + 1) * per, one, 0)

        @pl.when(n == 0)
        def _():
            prefetch(n, p)

        @pl.when(n + 1 < 8 * nq)
        def _():
            prefetch(n + 1, 1 - p)

        q = q_ref[...]
        dob = do_ref[...].astype(BF16)
        acc[...] = jnp.zeros_like(acc)
        gcarry[...] = jnp.zeros_like(gcarry)
        zbuf[...] = _dot(q, kt_ref[0])
        dwbuf[...] = _dot(dob, vt_ref[0])
        dzbuf[...] = jnp.zeros_like(dzbuf)
        wbuf[p, zero_slot] = jnp.zeros((qb, 256), BF16)

        q_t = q.astype(F32).T.astype(BF16)
        do_t = do_ref[...].T.astype(BF16)
        sub = lax.broadcasted_iota(jnp.int32, (128, 1), 0)

        def gradients(slot, kb):
            dz2 = dzbuf[...]
            acc[...] += _dot(dz2, k2_ref[kb])
            dk2 = _dot(q_t, dz2)
            dv2 = _dot(do_t, wbuf[p, slot])
            dk_ref[kb] += jnp.where(sub < 64, dk2[:, :128], dk2[:, 128:])
            dv_ref[kb] += jnp.where(sub < 64, dv2[:, :128], dv2[:, 128:])

        def trip(kb, lo):
            fetch(kb).wait()
            bias = bias_ref[_sb_bias_index(i, kb, per)][lo:]
            z2 = zbuf[...]
            dw2 = dwbuf[...]
            nxt = jnp.minimum(kb + 1, nkb - 1)
            zbuf[...] = _dot(q, kt_ref[nxt])
            dwbuf[...] = _dot(dob, vt_ref[nxt])
            gradients(jnp.where(kb == 0, zero_slot, kb - 1), jnp.maximum(kb - 1, 0))
            w2 = wbuf[p, kb]
            for hh in range(2):
                sl = slice(128 * hh, 128 * (hh + 1))
                z = z2[lo:, sl] + bias
                e = jnp.exp(-jnp.abs(z))
                r = 1.0 / (1.0 + e)
                sig = jnp.where(z >= 0, r, e * r)
                gw = w2[lo:, sl].astype(F32) * dw2[lo:, sl]
                cu2 = _dot(_split2(gw), etab_ref[...])
                gin = gcarry[hh, lo:, :]
                gcarry[hh, lo:, :] = gin + cu2[:, 128:]
                dzbuf[lo:, sl] = (gw - sig * (gw + cu2[:, :128] + gin)).astype(BF16)
                if lo:
                    dzbuf[:lo, sl] = jnp.zeros((lo, 128), BF16)

        def step(kb, _):
            trip(kb, 0)
            return 0

        lax.fori_loop(0, nkb - per, step, 0)
        for g in range(per):
            trip(nkb - per + g, CHUNK * g)
        gradients(nkb - 1, nkb - 1)
        dq_ref[...] = acc[...] * SB_SCALE

        @pl.when(i == nq - 1)
        def _():
            def untranspose(kb, _):
                dk_ref[kb] = dk_ref[kb].T
                dv_ref[kb] = dv_ref[kb].T
                return 0

            lax.fori_loop(0, nkb_all, untranspose, 0)

    blk = pl.BlockSpec((qb, 128), lambda h, i: (i, h))
    wide = pl.BlockSpec((None, nkb_all, 128, 256), lambda h, i: (h, 0, 0, 0))
    tall = pl.BlockSpec((None, nkb_all, 256, 128), lambda h, i: (h, 0, 0, 0))
    tab = pl.BlockSpec((256, 256), lambda h, i: (0, 0))
    kv_out = pl.BlockSpec((nkb_all, 128, 128), lambda h, i: (0, 0, h))
    ksh = jax.ShapeDtypeStruct((nkb_all, 128, 1024), F32)
    dq, dk, dv = _pcall(
        body, name="sb_bwd", grid=(8, t // qb),
        in_specs=[blk, wide, tall, wide, pl.BlockSpec(memory_space=pl.ANY), blk, tab,
                  pl.BlockSpec((per + 3, qb, 128), lambda h, i: (0, 0, 0))],
        out_specs=[blk, kv_out, kv_out], out_shape=[jax.ShapeDtypeStruct((t, 1024), F32), ksh, ksh],
        scratch_shapes=[pltpu.VMEM((qb, 128), F32), pltpu.VMEM((2, qb, 128), F32),
                        pltpu.VMEM((qb, 256), F32), pltpu.VMEM((qb, 256), F32),
                        pltpu.VMEM((2, nkb_all + 1, qb, 256), BF16), pltpu.SemaphoreType.DMA((2, nkb_all)),
                        pltpu.VMEM((qb, 256), BF16)],
        compiler_params=_params(dimension_semantics=("arbitrary", "arbitrary")))(
            qh, kt, k2, vt, wsave, do, earlier_tab, bias_tab)
    return dq, dk.reshape(t, 1024), dv.reshape(t, 1024)


def _adamw_math(w, g, m, v):
    m = ADAM_B1 * m + (1.0 - ADAM_B1) * g
    v = ADAM_B2 * v + (1.0 - ADAM_B2) * (g * g)
    m_hat = m / (1.0 - ADAM_B1 ** ADAM_STEP)
    v_hat = v / (1.0 - ADAM_B2 ** ADAM_STEP)
    delta = -ADAM_LR * (m_hat / (jnp.sqrt(v_hat) + ADAM_EPS) + ADAM_WD * w)
    return delta, m, v


def _adamw(name, w, owns, recvs, m, v, me):
    shape = w.shape
    c = shape[-1]
    nl = len(owns)
    w3, m3, v3 = (a.reshape(nl, -1, c) for a in (w, m, v))
    r = w3.shape[1]
    tr = _tile(r, (256, 128))
    owns = [o.reshape(N_DEV, r, c) for o in owns]
    recvs = [p.reshape(N_DEV - 1, r, c) for p in recvs]

    def body(me_ref, w_ref, *rest):
        own_refs, recv_refs = rest[:nl], rest[nl:2 * nl]
        m_ref, v_ref = rest[2 * nl:2 * nl + 2]
        g_out, d_out, m_out, v_out = rest[2 * nl + 2:]
        layer = pl.program_id(0)

        def grad(k):
            g = own_refs[k][...].astype(F32)
            for s in range(N_DEV - 1):
                g = g + recv_refs[k][s].astype(F32)
            return g

        g = grad(0)
        for k in range(1, nl):
            g = jnp.where(layer == k, grad(k), g)
        d, mn, vn = _adamw_math(w_ref[...], g, m_ref[...], v_ref[...])
        g_out[...] = g
        d_out[...] = d
        m_out[...] = mn
        v_out[...] = vn

    row = pl.BlockSpec((None, tr, c), lambda l, i, me_ref: (l, i, 0))
    own = lambda k: pl.BlockSpec((None, tr, c), lambda l, i, me_ref: (me_ref[0], jnp.where(l == k, i, 0), 0))
    rcv = lambda k: pl.BlockSpec((N_DEV - 1, tr, c), lambda l, i, me_ref: (0, jnp.where(l == k, i, 0), 0))
    osh = jax.ShapeDtypeStruct((nl, r, c), F32)
    grid_spec = pltpu.PrefetchScalarGridSpec(
        num_scalar_prefetch=1, grid=(nl, r // tr),
        in_specs=[row] + [own(k) for k in range(nl)] + [rcv(k) for k in range(nl)] + [row, row],
        out_specs=[row, row, row, row])
    outs = _pcall(body, name=name, grid_spec=grid_spec, out_shape=[osh, osh, osh, osh])(
        me.reshape(1), w3, *owns, *recvs, m3, v3)
    return tuple(o.reshape(shape) for o in outs)


def _place():
    x, y, c = lax.axis_index("x"), lax.axis_index("y"), lax.axis_index("c")
    return x, y, c, 4 * x + 2 * y + c


def _peer(x, y, c, rel):
    return (x ^ ((rel >> 2) & 1), y ^ ((rel >> 1) & 1), c ^ (rel & 1))


def _gather_first(now, later):
    n, k = len(now), len(later)

    def body(*refs):
        ins, outs = refs[:n + k], refs[n + k:2 * (n + k)]
        send, recv, lsem = refs[2 * (n + k):]
        x, y, c, me = _place()
        locals_ = []
        for w in range(n + k):
            local = pltpu.make_async_copy(ins[w], outs[w].at[me], lsem.at[w])
            local.start()
            locals_.append(local)
        def copy(w, src, slot, rel, to_rel):
            return pltpu.make_async_remote_copy(src_ref=src, dst_ref=outs[w].at[slot], send_sem=send.at[w, rel - 1],
                                                recv_sem=recv.at[w, rel - 1], device_id=_peer(x, y, c, to_rel),
                                                device_id_type=MESH)

        for w in range(n):
            for rel in (1, 2, 4, 6):
                copy(w, ins[w], me, rel, rel).start()
        for w in range(n):
            for rel in (2, 4, 6):
                copy(w, ins[w], me ^ rel, rel, rel).wait_recv()
                copy(w, outs[w].at[me ^ rel], me ^ rel, rel | 1, 1).start()
        for w in range(n):
            for rel in (1, 3, 5, 7):
                copy(w, ins[w], me ^ rel, rel, 1).wait_recv()
            for rel in range(1, N_DEV):
                copy(w, ins[w], me, rel, rel).wait_send()
        for local in locals_:
            local.wait()

    hbm = pl.BlockSpec(memory_space=pl.ANY)
    vmem = pl.BlockSpec(memory_space=pltpu.VMEM)
    arrays = list(now) + list(later)
    return _pcall(body, name="gather_first", in_specs=[vmem] * (n + k), out_specs=[hbm] * (n + k),
                  out_shape=[jax.ShapeDtypeStruct((N_DEV,) + a.shape, a.dtype) for a in arrays],
                  scratch_shapes=[pltpu.SemaphoreType.DMA((n, N_DEV - 1)), pltpu.SemaphoreType.DMA((n, N_DEV - 1)),
                                  pltpu.SemaphoreType.DMA((n + k,))],
                  compiler_params=_params(has_side_effects=True))(*arrays)


_HBM = pl.BlockSpec(memory_space=pltpu.HBM)
_SEM = pl.BlockSpec(memory_space=pltpu.SEMAPHORE)
_DATAFLOW = pltpu.SideEffectType.DATAFLOW_SIDE_EFFECTING


def _exchange_refs(srcs, lands, mode, me, rel, j):
    if mode == "gather":
        return srcs[j], lands[j].at[me], lands[j].at[me ^ rel]
    return srcs[j].at[me ^ rel], lands[j].at[rel - 1], lands[j].at[rel - 1]


def _exchange_start(name, srcs, lands, mode):
    n = len(srcs)

    def body(*refs):
        ins, lnd = refs[:n], refs[n:2 * n]
        send, recv = refs[2 * n], refs[2 * n + 1]
        token = refs[-1]
        x, y, c, me = _place()
        for j in range(n):
            for rel in range(1, N_DEV):
                src, dst, _ = _exchange_refs(ins, lnd, mode, me, rel, j)
                pltpu.make_async_remote_copy(src_ref=src, dst_ref=dst, send_sem=send.at[j * (N_DEV - 1) + rel - 1],
                                             recv_sem=recv.at[j * (N_DEV - 1) + rel - 1],
                                             device_id=_peer(x, y, c, rel), device_id_type=MESH).start()
        token[...] = jnp.zeros_like(token)

    sems = pltpu.SemaphoreType.DMA((n * (N_DEV - 1),))
    hbm_like = lambda a: pltpu.HBM(a.shape, a.dtype)
    outs = _pcall(body, name=name + "_start",
                  in_specs=[_HBM] * (2 * n), out_specs=[_SEM, _SEM] + [_HBM] * (2 * n) + [pl.BlockSpec(memory_space=pltpu.VMEM)],
                  out_shape=[sems, sems] + [hbm_like(a) for a in srcs] + [hbm_like(a) for a in lands]
                  + [jax.ShapeDtypeStruct((8, 128), F32)],
                  input_output_aliases={i: 2 + i for i in range(2 * n)},
                  compiler_params=pltpu.CompilerParams(has_side_effects=_DATAFLOW))(
                      *[pltpu.with_memory_space_constraint(a, pltpu.HBM) for a in list(srcs) + list(lands)])
    return dict(name=name, mode=mode, n=n, send=outs[0], recv=outs[1], srcs=outs[2:2 + n], lands=outs[2 + n:2 + 2 * n],
                token=outs[-1][0, 0])


def _exchange_wait(ex, after):
    n, mode = ex["n"], ex["mode"]

    def body(*refs):
        ins, lnd = refs[:n], refs[n:2 * n]
        send, recv = refs[2 * n], refs[2 * n + 1]
        x, y, c, me = _place()
        for j in range(n):
            for rel in range(1, N_DEV):
                src, dst, landed = _exchange_refs(ins, lnd, mode, me, rel, j)
                pltpu.make_async_remote_copy(src_ref=src, dst_ref=dst, send_sem=send.at[j * (N_DEV - 1) + rel - 1],
                                             recv_sem=recv.at[j * (N_DEV - 1) + rel - 1],
                                             device_id=_peer(x, y, c, rel), device_id_type=MESH).wait_send()
                pltpu.make_async_remote_copy(src_ref=src, dst_ref=landed, send_sem=send.at[j * (N_DEV - 1) + rel - 1],
                                             recv_sem=recv.at[j * (N_DEV - 1) + rel - 1],
                                             device_id=_peer(x, y, c, rel), device_id_type=MESH).wait_recv()

    hbm_like = lambda a: pltpu.HBM(a.shape, a.dtype)
    arrays = list(ex["srcs"]) + list(ex["lands"])
    outs = _pcall(body, name=ex["name"] + "_wait",
                  in_specs=[_HBM] * (2 * n) + [_SEM, _SEM, pl.BlockSpec(memory_space=pl.ANY)],
                  out_specs=[_HBM] * (2 * n), out_shape=[hbm_like(a) for a in arrays],
                  input_output_aliases={i: i for i in range(2 * n)},
                  compiler_params=pltpu.CompilerParams(has_side_effects=_DATAFLOW))(
                      *arrays, ex["send"], ex["recv"], after)
    return outs[:n], outs[n:]


def _scatter_start(name, grads):
    lands = [lax.empty((N_DEV - 1,) + g.shape[1:], g.dtype) for g in grads]
    return _exchange_start(name, grads, lands, "scatter")


ROW_MIX, ROW_MLP, ROW_CB, ROW_LG, ROW_LB, ROW_QN, ROW_KN, ROW_LOSS = 0, 2, 4, 5, 6, 7, 8, 9
ROW_META, ROW_CW, ROW_GN, SMALL_ROWS = 16, 32, 64, 72


def _sum_small(slots):
    def body(s_ref, o_ref):
        tot = s_ref[0]
        for s in range(1, N_DEV):
            tot = tot + s_ref[s]
        o_ref[...] = tot
        for row in (ROW_QN, ROW_KN):
            v = tot[row:row + 1, :]
            f = v[:, 0:128]
            for k in range(1, 8):
                f = f + v[:, 128 * k:128 * (k + 1)]
            o_ref[row:row + 1, 0:64] = f[:, 0:64] + f[:, 64:128]

    return _pcall(body, name="sum_small", out_shape=jax.ShapeDtypeStruct(slots.shape[1:], F32))(slots)


def _adamw_small(w, g, m, v):
    def body(w_ref, g_ref, m_ref, v_ref, d_out, m_out, v_out):
        d, mn, vn = _adamw_math(w_ref[...], g_ref[...], m_ref[...], v_ref[...])
        d_out[...] = d
        m_out[...] = mn
        v_out[...] = vn

    osh = jax.ShapeDtypeStruct(w.shape, F32)
    return _pcall(body, name="adamw_small", out_shape=[osh, osh, osh])(w, g, m, v)


def _local_step(h0, target, p, weight, emit):
    t = h0.shape[0]
    tables = _ret_tables(t)
    bd, later_tab, earlier_tab, bias_tab = _seg_tables(_sb_qb(t))
    row = lambda a, i: a[i:i + 1]

    hn_a = _rms_fwd("rms_mix0", h0, row(p["norm_mix_g"], 0))
    w_in = weight("w_in", hn_a)
    proj = _mm_cols("proj_in", hn_a, w_in, ())
    gn_flat = p["gn_g"].reshape(1, 1024)
    o_ret, states, cat = _ret_fwd(proj, gn_flat, tables)
    cat, hdn, ycv = _conv_fwd(cat, proj, p["conv_w"], p["conv_b"], p["ln_g"], p["ln_b"])
    w_out = weight("w_out", cat)
    h1, hn_b = _mm_rows_norm("mix_out", cat, w_out, h0, row(p["norm_mlp_g"], 0))
    w1_0, w2_0 = weight("w1_0", hn_b), weight("w2_0", hn_b)
    a0, s0 = _mm_cols("mlp0_up", hn_b, w1_0, (), epi="relu2")
    h2, hn_c = _mm_rows_norm("mlp0_down", s0, w2_0, h1, row(p["norm_mix_g"], 1))

    w_qkv = weight("w_qkv", hn_c)
    qkv = _mm_cols("qkv", hn_c, w_qkv, ())
    qg = jnp.tile(p["qn_g"], (1, 16))
    kg = jnp.tile(p["kn_g"], (1, 16))
    qh, kt, k2, vt, v2 = _qk_norm_fwd(qkv, qg, kg, bd)
    o_sb, w_sb = _sb_fwd(qh, kt, v2, later_tab, bias_tab)
    w_o = weight("w_o", o_sb)
    h3, hn_d = _mm_rows_norm("attn_out", o_sb, w_o, h2, row(p["norm_mlp_g"], 1))
    w1_1, w2_1 = weight("w1_1", hn_d), weight("w2_1", hn_d)
    a1, s1 = _mm_cols("mlp1_up", hn_d, w1_1, (), epi="relu2")
    dh, loss = _mm_rows_loss("mlp1_down", s1, w2_1, h3, target)

    def mlp_bwd(tag, layer, w1, w2, dh, h_in, hn, a, s):
        da = _mm_rows_t(f"{tag}_dact", dh, w2, (), out_dtype=BF16, epi="drelu2", extra=a)
        dw2 = _wgrad_rows(f"{tag}_dw2", s, dh, 512)
        dw1 = _wgrad_cols(f"{tag}_dw1", hn, da, 512)
        tok = emit(tag, [dw1, dw2])
        return _mm_cols_t_rms(f"{tag}_dhn", da, w1, h_in, row(p["norm_mlp_g"], layer) + tok, dh)

    dh, dg_mlp1 = mlp_bwd("mlp1", 1, w1_1, w2_1, dh, h3, hn_d, a1, s1)

    do_sb = _mm_rows_t("attn_dout", dh, w_o, ())
    dw_o = _wgrad_rows("attn_dwo", o_sb, dh, 128)
    dq, dk, dv = _sb_bwd(qh, kt, k2, vt, w_sb, do_sb, earlier_tab, bias_tab)
    dqkv, dqg, dkg = _qk_norm_bwd(qkv, dq, dk, dv, qg, kg, bd)
    dw_qkv = _wgrad_cols("qkv_dw", hn_c, dqkv, 384)
    tok = emit("attn", [dw_qkv, dw_o])
    dh, dg_mix1 = _mm_cols_t_rms("qkv_dhn", dqkv, w_qkv, h2, row(p["norm_mix_g"], 1) + tok, dh)

    dh, dg_mlp0 = mlp_bwd("mlp0", 0, w1_0, w2_0, dh, h1, hn_b, a0, s0)

    dw_out = _wgrad_rows("mix_dwout", cat, dh, 256)
    tok = emit("mix0_out", [dw_out])
    do_ret, dproj, dgn, dy, dlg, dlb, dcb = _mix_bwd_head(dh, w_out, o_ret, proj, gn_flat + tok, ycv,
                                                          p["ln_g"], p["ln_b"])
    dproj = _ret_bwd(dproj, proj, states, do_ret, tables)
    dproj, dug, dcw = _conv_bwd_taps(dproj, dy, hdn, proj, p["conv_w"])
    dproj = lax.dynamic_update_slice(dproj, dug, (0, 4096))
    dw_in = _wgrad_cols("proj_dw", hn_a, dproj, 640)
    tok = emit("mix0", [dw_in])
    dh, dg_mix0 = _mm_cols_t_rms("proj_dhn", dproj, w_in, h0, row(p["norm_mix_g"], 0) + tok, dh)

    rid = lax.broadcasted_iota(jnp.int32, (16, 1), 0)
    loss_row = jnp.broadcast_to(loss[0:1, 0:1], (1, D_MODEL))
    vecs = sum(jnp.where(rid == k, v, 0.0)
               for k, v in enumerate((dg_mix0, dg_mix1, dg_mlp0, dg_mlp1, dcb, dlg, dlb, dqg, dkg, loss_row)))
    small = jnp.concatenate([vecs, dh[PAD_FRONT:TOK0], dcw, jnp.where(rid[:8] == 0, dgn, 0.0)], axis=0)
    return dh[TOK0:], small


_SMALL_NAMES = ("meta", "norm_mix_g", "norm_mlp_g", "even_ret_gn_g", "even_conv_w", "even_conv_b",
                "even_conv_ln_g", "even_conv_ln_b", "odd_q_norm_g", "odd_k_norm_g")
_BIG_NAMES = ("even_w_in", "even_w_out", "odd_w_qkv", "odd_w_o", "mlp_w1", "mlp_w2")
_ORDER = ("meta", "norm_mix_g", "norm_mlp_g", "even_w_in", "even_ret_gn_g", "even_conv_w", "even_conv_b",
          "even_conv_ln_g", "even_conv_ln_b", "even_w_out", "odd_w_qkv", "odd_q_norm_g", "odd_k_norm_g",
          "odd_w_o", "mlp_w1", "mlp_w2")


def _pack128(a):
    flat = a.reshape(-1)
    n = flat.shape[0]
    rows = -(-n // 128)
    rows8 = -(-rows // 8) * 8
    return jnp.pad(flat, (0, rows8 * 128 - n)).reshape(rows8, 128)


def kernel(x, meta, norm_mix_g, norm_mlp_g, even_w_in, even_ret_gn_g, even_conv_w, even_conv_b, even_conv_ln_g, even_conv_ln_b, even_w_out, odd_w_qkv, odd_q_norm_g, odd_k_norm_g, odd_w_o, mlp_w1, mlp_w2, loss_target, m_meta, m_norm_mix_g, m_norm_mlp_g, m_even_w_in, m_even_ret_gn_g, m_even_conv_w, m_even_conv_b, m_even_conv_ln_g, m_even_conv_ln_b, m_even_w_out, m_odd_w_qkv, m_odd_q_norm_g, m_odd_k_norm_g, m_odd_w_o, m_mlp_w1, m_mlp_w2, v_meta, v_norm_mix_g, v_norm_mlp_g, v_even_w_in, v_even_ret_gn_g, v_even_conv_w, v_even_conv_b, v_even_conv_ln_g, v_even_conv_ln_b, v_even_w_out, v_odd_w_qkv, v_odd_q_norm_g, v_odd_k_norm_g, v_odd_w_o, v_mlp_w1, v_mlp_w2):
    w = dict(meta=meta, norm_mix_g=norm_mix_g, norm_mlp_g=norm_mlp_g, even_w_in=even_w_in,
             even_ret_gn_g=even_ret_gn_g, even_conv_w=even_conv_w, even_conv_b=even_conv_b,
             even_conv_ln_g=even_conv_ln_g, even_conv_ln_b=even_conv_ln_b, even_w_out=even_w_out,
             odd_w_qkv=odd_w_qkv, odd_q_norm_g=odd_q_norm_g, odd_k_norm_g=odd_k_norm_g, odd_w_o=odd_w_o,
             mlp_w1=mlp_w1, mlp_w2=mlp_w2)
    mom = dict(meta=m_meta, norm_mix_g=m_norm_mix_g, norm_mlp_g=m_norm_mlp_g, even_w_in=m_even_w_in,
               even_ret_gn_g=m_even_ret_gn_g, even_conv_w=m_even_conv_w, even_conv_b=m_even_conv_b,
               even_conv_ln_g=m_even_conv_ln_g, even_conv_ln_b=m_even_conv_ln_b, even_w_out=m_even_w_out,
               odd_w_qkv=m_odd_w_qkv, odd_q_norm_g=m_odd_q_norm_g, odd_k_norm_g=m_odd_k_norm_g, odd_w_o=m_odd_w_o,
               mlp_w1=m_mlp_w1, mlp_w2=m_mlp_w2)
    var = dict(meta=v_meta, norm_mix_g=v_norm_mix_g, norm_mlp_g=v_norm_mlp_g, even_w_in=v_even_w_in,
               even_ret_gn_g=v_even_ret_gn_g, even_conv_w=v_even_conv_w, even_conv_b=v_even_conv_b,
               even_conv_ln_g=v_even_conv_ln_g, even_conv_ln_b=v_even_conv_ln_b, even_w_out=v_even_w_out,
               odd_w_qkv=v_odd_w_qkv, odd_q_norm_g=v_odd_q_norm_g, odd_k_norm_g=v_odd_k_norm_g, odd_w_o=v_odd_w_o,
               mlp_w1=v_mlp_w1, mlp_w2=v_mlp_w2)
    me = 4 * lax.axis_index("x") + 2 * lax.axis_index("y") + lax.axis_index("c")

    small_in = jnp.concatenate([meta, jnp.pad(even_conv_w[0], ((0, 1), (0, 0))),
                                jnp.pad(even_ret_gn_g[0], ((0, 4), (0, 96)))], axis=0)
    b16 = lambda a: a.astype(BF16)
    later_src = dict(w_out=b16(even_w_out[0]), w1_0=b16(mlp_w1[0]), w2_0=b16(mlp_w2[0]),
                     w_qkv=b16(odd_w_qkv[0]), w_o=b16(odd_w_o[0]), w1_1=b16(mlp_w1[1]), w2_1=b16(mlp_w2[1]))
    landed = _gather_first([b16(even_w_in[0]), small_in], list(later_src.values()))
    g_in, g_small = landed[0], landed[1]
    own_slot = dict(zip(later_src, landed[2:]))
    groups = (("gather_l0", ("w_out", "w1_0", "w2_0")), ("gather_attn", ("w_qkv", "w_o")),
              ("gather_l1", ("w1_1", "w2_1")))
    pending = {}
    gather_tok = jnp.zeros((), F32)
    for gname, names in groups:
        ex = _exchange_start(gname, [later_src[n] for n in names], [own_slot[n] for n in names], "gather")
        gather_tok = gather_tok + ex["token"]
        for n in names:
            pending[n] = (ex, names)
    arrived = dict(w_in=g_in)

    def weight(name, after):
        if name not in arrived:
            ex, names = pending[name]
            arrived.update(zip(names, _exchange_wait(ex, after)[1]))
        return arrived[name]

    cols = lambda a: jnp.transpose(a, (1, 0, 2)).reshape(a.shape[1], -1)
    p = dict(norm_mix_g=norm_mix_g + gather_tok, norm_mlp_g=norm_mlp_g, conv_b=even_conv_b, ln_g=even_conv_ln_g,
             ln_b=even_conv_ln_b, qn_g=odd_q_norm_g, kn_g=odd_k_norm_g,
             gn_g=cols(g_small[:, 48:52, :32]),
             conv_w=jnp.pad(cols(g_small[:, 16:47]), ((0, 1), (0, 0))))
    meta_full = cols(g_small[:, 0:16])

    scatters = {}

    def emit(tag, grads):
        scatters[tag] = _scatter_start("scatter_" + tag, grads)
        return scatters[tag]["token"]

    h0 = jnp.concatenate([jnp.zeros((PAD_FRONT, D_MODEL), F32), meta_full, x[0]], axis=0)
    target = jnp.concatenate([jnp.zeros((TOK0, D_MODEL), F32), loss_target[0]], axis=0)
    grad_x, small_part = _local_step(h0, target, p, weight, emit)

    out = {}
    got = {}

    def update(names, terms, after):
        for tag in {t for name in names for t, _ in terms[name]} - set(got):
            got[tag] = _exchange_wait(scatters[tag], after)
        for name in names:
            owns, recvs = zip(*[(got[t][0][j], got[t][1][j]) for t, j in terms[name]])
            out[name] = _adamw("adamw_" + name, w[name], list(owns), list(recvs), mom[name], var[name], me)

    terms = dict(even_w_in=[("mix0", 0)], even_w_out=[("mix0_out", 0)], odd_w_qkv=[("attn", 0)], odd_w_o=[("attn", 1)],
                 mlp_w1=[("mlp0", 0), ("mlp1", 0)], mlp_w2=[("mlp0", 1), ("mlp1", 1)])
    small_ex = _exchange_start("small", [small_part], [lax.empty((N_DEV,) + small_part.shape, F32)], "gather")
    update(("mlp_w1", "mlp_w2", "odd_w_qkv", "odd_w_o", "even_w_out"), terms, grad_x)
    update(("even_w_in",), terms, out["even_w_out"][1])
    (own_part,), (slots,) = _exchange_wait(small_ex, out["even_w_in"][1])
    tot = _sum_small(lax.dynamic_update_slice(slots, own_part[None], (me, 0, 0)))
    loss = tot[ROW_LOSS, 0]

    shard_cols = lambda a, width: lax.dynamic_slice_in_dim(a, me * width, width, axis=1)
    one = lambda r: tot[r:r + 1]
    small_g = dict(
        norm_mix_g=tot[ROW_MIX:ROW_MIX + 2], norm_mlp_g=tot[ROW_MLP:ROW_MLP + 2],
        even_conv_b=one(ROW_CB), even_conv_ln_g=one(ROW_LG), even_conv_ln_b=one(ROW_LB),
        odd_q_norm_g=one(ROW_QN)[:, :64], odd_k_norm_g=one(ROW_KN)[:, :64],
        meta=shard_cols(tot[ROW_META:ROW_META + N_META], 128),
        even_conv_w=shard_cols(tot[ROW_CW:ROW_CW + CONV_WIDTH], 128)[None],
        even_ret_gn_g=shard_cols(tot[ROW_GN].reshape(4, 256), 32)[None])
    packs = {n: (_pack128(w[n]), _pack128(small_g[n]), _pack128(mom[n]), _pack128(var[n])) for n in _SMALL_NAMES}
    cat4 = [jnp.concatenate([packs[n][i] for n in _SMALL_NAMES], axis=0) for i in range(4)]
    d_s, m_s, v_s = _adamw_small(*cat4)
    r0 = 0
    for n in _SMALL_NAMES:
        rows = packs[n][0].shape[0]
        size = w[n].size
        take = lambda a: a[r0:r0 + rows].reshape(-1)[:size].reshape(w[n].shape)
        out[n] = (small_g[n].reshape(w[n].shape), take(d_s), take(m_s), take(v_s))
        r0 += rows

    res = [loss, grad_x[None]]
    for i in range(4):
        res.extend(out[n][i] for n in _ORDER)
    return tuple(res)
```

```python
import functools

import numpy as np
import jax
import jax.numpy as jnp
from jax import lax
from jax.experimental import pallas as pl
from jax.experimental.pallas import tpu as pltpu

F32 = jnp.float32
BF16 = jnp.bfloat16

D_MODEL = 1024
N_META = 16
CHUNK = 128
PAD_FRONT = 112
TOK0 = PAD_FRONT + N_META
EPS = 1e-6
N_DEV = 8
RET_HEADS = 4
RET_DECAY_OFFSET = 5.0
ROPE_BASE = 10000.0
CONV_WIDTH = 31
HALO = 32
SB_SCALE = 64 ** -0.5
RET_SCALE = 128 ** -0.5
ADAM_LR, ADAM_B1, ADAM_B2, ADAM_EPS, ADAM_WD, ADAM_STEP = 0.001, 0.9, 0.999, 1e-08, 0.01, 10
VMEM_LIMIT = 56 * 1024 * 1024
MESH = pl.DeviceIdType.MESH


def _pcall(body, **kw):
    return pl.pallas_call(body, **kw)


def _params(**kw):
    return pltpu.CompilerParams(vmem_limit_bytes=VMEM_LIMIT, **kw)


def _tile(n, cands):
    for c in cands:
        if n % c == 0:
            return c
    raise ValueError(f"no tile for {n} in {cands}")


def _sigmoid(x):
    return 1.0 / (1.0 + jnp.exp(-x))


_DIMS = {
    "nn": (((1,), (0,)), ((), ())),
    "nt": (((1,), (1,)), ((), ())),
    "tn": (((0,), (0,)), ((), ())),
}


def _matmul(name, a, b, *, grid, a_spec, b_spec, o_spec, out_shape, contract, acc_shape,
            epi="plain", extra=None, extra_spec=None):
    nk = grid[2]
    dims = _DIMS[contract]
    n_in = 3 if extra is not None else 2
    n_out = 2 if epi == "relu2" else 1

    def body(*refs):
        a_ref, b_ref = refs[0], refs[1]
        e_ref = refs[2] if extra is not None else None
        outs = refs[n_in:n_in + n_out]
        acc = refs[-1]
        k = pl.program_id(2)
        part = lax.dot_general(a_ref[...].astype(BF16), b_ref[...].astype(BF16), dims, preferred_element_type=F32)
        if nk > 1:
            @pl.when(k == 0)
            def _():
                acc[...] = jnp.zeros_like(acc)

            acc[...] += part

        @pl.when(k == nk - 1)
        def _():
            r = acc[...] if nk > 1 else part
            if epi == "plain":
                outs[0][...] = r.astype(outs[0].dtype)
            elif epi == "residual":
                outs[0][...] = (r + e_ref[...]).astype(outs[0].dtype)
            elif epi == "relu2":
                outs[0][...] = r
                rr = jnp.maximum(r, 0.0)
                outs[1][...] = (rr * rr).astype(BF16)
            elif epi == "drelu2":
                outs[0][...] = (r * (2.0 * jnp.maximum(e_ref[...], 0.0))).astype(outs[0].dtype)

    in_specs = [a_spec, b_spec] + ([extra_spec] if extra is not None else [])
    args = (a, b) + ((extra,) if extra is not None else ())
    if n_out == 2:
        out_specs = [o_spec, o_spec]
    else:
        out_specs = o_spec
    return _pcall(body, name=name, grid=grid, in_specs=in_specs, out_specs=out_specs,
                  out_shape=out_shape, scratch_shapes=[pltpu.VMEM(acc_shape, F32)],
                  compiler_params=_params(dimension_semantics=("parallel", "parallel", "arbitrary")))(*args)


def _tm_tall(t):
    return _tile(t, (2112, 768, 384, 128))


def _mm_cols(name, a, wb, lead, out_dtype=F32, epi="plain"):
    t, kdim = a.shape
    n = wb.shape[-1]
    tm, tk = _tm_tall(t), _tile(kdim, (1024, 512))
    nl = len(lead)
    b_spec = pl.BlockSpec((None,) * (1 + nl) + (tk, n), lambda i, j, k: (j,) + lead + (k, 0))
    o_spec = pl.BlockSpec((tm, n), lambda i, j, k: (i, j))
    if epi == "relu2":
        out_shape = [jax.ShapeDtypeStruct((t, N_DEV * n), F32), jax.ShapeDtypeStruct((t, N_DEV * n), BF16)]
    else:
        out_shape = jax.ShapeDtypeStruct((t, N_DEV * n), out_dtype)
    return _matmul(name, a, wb, grid=(t // tm, N_DEV, kdim // tk),
                   a_spec=pl.BlockSpec((tm, tk), lambda i, j, k: (i, k)), b_spec=b_spec, o_spec=o_spec,
                   out_shape=out_shape, contract="nn", acc_shape=(tm, n), epi=epi)


def _mm_cols_t_rms(name, a, wb, h, g, dres):
    t = a.shape[0]
    nb, kdim, n = wb.shape
    tm = _tile(t, (704, 384, 128))

    def body(a_ref, b_ref, h_ref, g_ref, r_ref, o_ref, dg_ref):
        @pl.when(pl.program_id(0) == 0)
        def _():
            dg_ref[...] = jnp.zeros_like(dg_ref)

        d = _dot(a_ref[:, 0:n].astype(BF16), b_ref[0], "nt")
        for j in range(1, nb):
            d = d + _dot(a_ref[:, j * n:(j + 1) * n].astype(BF16), b_ref[j], "nt")
        x = h_ref[...]
        rs = lax.rsqrt(jnp.mean(x * x, axis=-1, keepdims=True) + EPS)
        u = d * g_ref[...]
        m = jnp.mean(u * x, axis=-1, keepdims=True)
        o_ref[...] = r_ref[...] + rs * u - x * (rs * rs * rs * m)
        dg_ref[...] += jnp.sum(d * x * rs, axis=0, keepdims=True)

    row = pl.BlockSpec((tm, kdim), lambda i: (i, 0))
    vec = pl.BlockSpec((1, kdim), lambda i: (0, 0))
    return _pcall(body, name=name, grid=(t // tm,),
                  in_specs=[pl.BlockSpec((tm, nb * n), lambda i: (i, 0)),
                            pl.BlockSpec((nb, kdim, n), lambda i: (0, 0, 0)), row, vec, row],
                  out_specs=[row, vec],
                  out_shape=[jax.ShapeDtypeStruct((t, kdim), F32), jax.ShapeDtypeStruct((1, kdim), F32)],
                  compiler_params=_params(dimension_semantics=("arbitrary",)))(a, wb, h, g, dres)


def _mm_rows_t(name, a, wb, lead, out_dtype=F32, epi="plain", extra=None):
    t, n = a.shape
    r = wb.shape[-2]
    tm, tk = _tm_tall(t), _tile(n, (1024,))
    nl = len(lead)
    b_spec = pl.BlockSpec((None,) * (1 + nl) + (r, tk), lambda i, j, k: (j,) + lead + (0, k))
    o_spec = pl.BlockSpec((tm, r), lambda i, j, k: (i, j))
    return _matmul(name, a, wb, grid=(t // tm, N_DEV, n // tk),
                   a_spec=pl.BlockSpec((tm, tk), lambda i, j, k: (i, k)), b_spec=b_spec, o_spec=o_spec,
                   out_shape=jax.ShapeDtypeStruct((t, N_DEV * r), out_dtype), contract="nt",
                   acc_shape=(tm, r), epi=epi, extra=extra, extra_spec=o_spec if extra is not None else None)


def _mm_rows_loss(name, a, wb, residual, target):
    t = a.shape[0]
    nb, r, n = wb.shape
    tm = _tile(t, (704, 384, 128))

    def body(a_ref, b_ref, r_ref, t_ref, d_ref, l_ref):
        i = pl.program_id(0)

        @pl.when(i == 0)
        def _():
            l_ref[...] = jnp.zeros_like(l_ref)

        y = r_ref[...] + _dot(a_ref[...].astype(BF16), b_ref[...].reshape(nb * r, n))
        diff = jnp.where(_row_ids(i, tm) >= TOK0, y - t_ref[...], 0.0)
        d_ref[...] = diff * (1.0 / D_MODEL)
        l_ref[...] += jnp.sum(diff * diff) * (0.5 / D_MODEL)

    row = pl.BlockSpec((tm, n), lambda i: (i, 0))
    return _pcall(body, name=name, grid=(t // tm,),
                  in_specs=[pl.BlockSpec((tm, nb * r), lambda i: (i, 0)),
                            pl.BlockSpec((nb, r, n), lambda i: (0, 0, 0)), row, row],
                  out_specs=[row, pl.BlockSpec((8, 128), lambda i: (0, 0))],
                  out_shape=[jax.ShapeDtypeStruct((t, n), F32), jax.ShapeDtypeStruct((8, 128), F32)],
                  compiler_params=_params(dimension_semantics=("arbitrary",)))(a, wb, residual, target)


def _mm_rows_norm(name, a, wb, residual, g):
    t = a.shape[0]
    nb, r, n = wb.shape
    tm = _tile(t, (704, 384, 128))

    def body(a_ref, b_ref, r_ref, g_ref, h_ref, hn_ref):
        h = r_ref[...] + _dot(a_ref[...].astype(BF16), b_ref[...].reshape(nb * r, n))
        h_ref[...] = h
        hn_ref[...] = (h * lax.rsqrt(jnp.mean(h * h, axis=-1, keepdims=True) + EPS) * g_ref[...]).astype(BF16)

    row = pl.BlockSpec((tm, n), lambda i: (i, 0))
    return _pcall(body, name=name, grid=(t // tm,),
                  in_specs=[pl.BlockSpec((tm, nb * r), lambda i: (i, 0)), pl.BlockSpec((nb, r, n), lambda i: (0, 0, 0)),
                            row, pl.BlockSpec((1, n), lambda i: (0, 0))],
                  out_specs=[row, row],
                  out_shape=[jax.ShapeDtypeStruct((t, n), F32), jax.ShapeDtypeStruct((t, n), BF16)],
                  compiler_params=_params(dimension_semantics=("parallel",)))(a, wb, residual, g)


def _wgrad_cols(name, x, dy, n):
    t, kdim = x.shape
    tk = _tm_tall(t)
    return _matmul(name, x, dy, grid=(1, N_DEV, t // tk),
                   a_spec=pl.BlockSpec((tk, kdim), lambda i, j, k: (k, 0)),
                   b_spec=pl.BlockSpec((tk, n), lambda i, j, k: (k, j)),
                   o_spec=pl.BlockSpec((None, kdim, n), lambda i, j, k: (j, 0, 0)),
                   out_shape=jax.ShapeDtypeStruct((N_DEV, kdim, n), BF16), contract="tn", acc_shape=(kdim, n))


def _wgrad_rows(name, x, dy, r):
    t = x.shape[0]
    n = dy.shape[1]
    tk, tn = _tm_tall(t), _tile(n, (512,))
    tm = min(N_DEV * r, 1024)
    out = _matmul(name, x, dy, grid=(N_DEV * r // tm, n // tn, t // tk),
                  a_spec=pl.BlockSpec((tk, tm), lambda i, j, k: (k, i)),
                  b_spec=pl.BlockSpec((tk, tn), lambda i, j, k: (k, j)),
                  o_spec=pl.BlockSpec((tm, tn), lambda i, j, k: (i, j)),
                  out_shape=jax.ShapeDtypeStruct((N_DEV * r, n), BF16), contract="tn", acc_shape=(tm, tn))
    return out.reshape(N_DEV, r, n)


def _rows(t):
    return _tile(t, (384, 128))


def _rms_fwd(name, h, g):
    t = h.shape[0]
    tr = _rows(t)

    def body(h_ref, g_ref, o_ref):
        x = h_ref[...]
        r = lax.rsqrt(jnp.mean(x * x, axis=-1, keepdims=True) + EPS)
        o_ref[...] = (x * r * g_ref[...]).astype(BF16)

    row = pl.BlockSpec((tr, D_MODEL), lambda i: (i, 0))
    vec = pl.BlockSpec((1, D_MODEL), lambda i: (0, 0))
    return _pcall(body, name=name, grid=(t // tr,), in_specs=[row, vec], out_specs=row,
                  out_shape=jax.ShapeDtypeStruct((t, D_MODEL), BF16))(h, g)


def _ret_tables(t):
    hh = np.arange(RET_HEADS, dtype=np.float64)
    log_g = np.log1p(-np.exp2(-RET_DECAY_OFFSET - hh))
    idx = np.arange(CHUNK, dtype=np.float64)
    diff = idx[:, None] - idx[None, :]
    dmat = np.where(diff[None] >= 0, np.exp(np.maximum(diff, 0.0)[None] * log_g[:, None, None]), 0.0)
    qdec = np.exp((idx + 1.0)[None, :, None] * log_g[:, None, None]) * np.ones((1, 1, CHUNK))
    kdec = np.exp((CHUNK - 1 - idx)[None, :, None] * log_g[:, None, None]) * np.ones((1, 1, CHUNK))
    half = CHUNK // 2
    inv_freq = (ROPE_BASE ** (-np.arange(half, dtype=np.float32) / half)).astype(np.float32)
    ang = (np.arange(t, dtype=np.float32)[:, None] * inv_freq[None, :]).astype(np.float32).astype(np.float64)
    cos2 = np.concatenate([np.cos(ang), np.cos(ang)], axis=1)
    sin2 = np.concatenate([-np.sin(ang), np.sin(ang)], axis=1)
    return tuple(jnp.asarray(v, F32) for v in (dmat, qdec, kdec, cos2, sin2))


def _rot(x, c, s):
    return x * c + pltpu.roll(x, CHUNK // 2, 1) * s


def _unrot(dx, c, s):
    return dx * c + pltpu.roll(dx * s, CHUNK // 2, 1)


def _dot(a, b, contract="nn"):
    return lax.dot_general(a, b, _DIMS[contract], preferred_element_type=F32)


def _ret_fwd(proj, gn_g, tables):
    t = proj.shape[0]
    nch = t // CHUNK
    dmat, qdec, kdec, cos2, sin2 = tables

    def body(qk_ref, v_ref, g_ref, w_ref, c_ref, s_ref, dm_ref, qd_ref, kd_ref, o_ref, st_ref, cat_ref, state):
        @pl.when(pl.program_id(0) == 0)
        def _():
            state[...] = jnp.zeros_like(state)

        c, s = c_ref[...], s_ref[...]
        for h in range(RET_HEADS):
            q = _rot(qk_ref[:, 128 * h:128 * (h + 1)], c, s)
            k = _rot(qk_ref[:, 512 + 128 * h:512 + 128 * (h + 1)], c, s) * RET_SCALE
            vb = v_ref[:, 256 * h:256 * (h + 1)].astype(BF16)
            st = state[h]
            st_ref[h] = st
            sc = _dot(q.astype(BF16), k.astype(BF16), "nt") * dm_ref[h]
            o = _dot(sc.astype(BF16), vb)
            o += _dot((q * qd_ref[h]).astype(BF16), st.astype(BF16))
            sl = slice(256 * h, 256 * (h + 1))
            o_ref[:, sl] = o
            kv = _dot((k * kd_ref[h]).astype(BF16), vb, "tn")
            state[h] = qd_ref[h, CHUNK - 1:CHUNK, 0:1] * st + kv
            mu = jnp.mean(o, axis=-1, keepdims=True)
            oc = o - mu
            rstd = lax.rsqrt(jnp.mean(oc * oc, axis=-1, keepdims=True) + EPS)
            g = g_ref[:, sl]
            cat_ref[:, sl] = (g * _sigmoid(g) * (oc * rstd * w_ref[:, sl])).astype(BF16)

    tab = pl.BlockSpec((RET_HEADS, CHUNK, CHUNK), lambda n: (0, 0, 0))
    pos = pl.BlockSpec((CHUNK, CHUNK), lambda n: (n, 0))
    row = pl.BlockSpec((CHUNK, 1024), lambda n: (n, 0))
    return _pcall(
        body, name="ret_fwd", grid=(nch,),
        in_specs=[row, pl.BlockSpec((CHUNK, 1024), lambda n: (n, 1)), pl.BlockSpec((CHUNK, 1024), lambda n: (n, 2)),
                  pl.BlockSpec((1, 1024), lambda n: (0, 0)), pos, pos, tab, tab, tab],
        out_specs=[row, pl.BlockSpec((RET_HEADS, None, 128, 256), lambda n: (0, n, 0, 0)), row],
        out_shape=[jax.ShapeDtypeStruct((t, 1024), F32), jax.ShapeDtypeStruct((RET_HEADS, nch, 128, 256), F32),
                   jax.ShapeDtypeStruct((t, 2048), BF16)],
        scratch_shapes=[pltpu.VMEM((RET_HEADS, 128, 256), F32)],
        compiler_params=_params(dimension_semantics=("arbitrary",)))(
            proj, proj, proj, gn_g, cos2, sin2, dmat, qdec, kdec)


def _ret_bwd(dproj, proj, states, do, tables):
    t = proj.shape[0]
    nch = t // CHUNK
    dmat, qdec, kdec, cos2, sin2 = tables

    def body(dp_in, qk_ref, v_ref, do_ref, st_ref, c_ref, s_ref, dm_ref, qd_ref, kd_ref, dp_ref, rst):
        del dp_in
        @pl.when(pl.program_id(0) == 0)
        def _():
            rst[...] = jnp.zeros_like(rst)

        c, s = c_ref[...], s_ref[...]
        for h in range(RET_HEADS):
            q = _rot(qk_ref[:, 128 * h:128 * (h + 1)], c, s)
            k = _rot(qk_ref[:, 512 + 128 * h:512 + 128 * (h + 1)], c, s) * RET_SCALE
            qb, kb = q.astype(BF16), k.astype(BF16)
            vb = v_ref[:, 256 * h:256 * (h + 1)].astype(BF16)
            dob = do_ref[:, 256 * h:256 * (h + 1)].astype(BF16)
            pb = st_ref[h].astype(BF16)
            r = rst[h]
            rb = r.astype(BF16)
            dm, qd, kd = dm_ref[h], qd_ref[h], kd_ref[h]
            sb = (_dot(qb, kb, "nt") * dm).astype(BF16)
            dsb = (_dot(dob, vb, "nt") * dm).astype(BF16)
            dq = _dot(dsb, kb) + _dot(dob, pb, "nt") * qd
            dk = _dot(dsb, qb, "tn") + _dot(vb, rb, "nt") * kd
            dv = _dot(sb, dob, "tn") + _dot((k * kd).astype(BF16), rb)
            rst[h] = _dot((q * qd).astype(BF16), dob, "tn") + qd[CHUNK - 1:CHUNK, 0:1] * r
            dp_ref[:, 128 * h:128 * (h + 1)] = _unrot(dq, c, s).astype(BF16)
            dp_ref[:, 512 + 128 * h:512 + 128 * (h + 1)] = (_unrot(dk, c, s) * RET_SCALE).astype(BF16)
            dp_ref[:, 1024 + 256 * h:1024 + 256 * (h + 1)] = dv.astype(BF16)

    rev = lambda n: nch - 1 - n
    tab = pl.BlockSpec((RET_HEADS, CHUNK, CHUNK), lambda n: (0, 0, 0))
    pos = pl.BlockSpec((CHUNK, CHUNK), lambda n: (rev(n), 0))
    row = pl.BlockSpec((CHUNK, 1024), lambda n: (rev(n), 0))
    return _pcall(
        body, name="ret_bwd", grid=(nch,),
        in_specs=[pl.BlockSpec(memory_space=pl.ANY), row, pl.BlockSpec((CHUNK, 1024), lambda n: (rev(n), 1)), row,
                  pl.BlockSpec((RET_HEADS, None, 128, 256), lambda n: (0, rev(n), 0, 0)),
                  pos, pos, tab, tab, tab],
        out_specs=pl.BlockSpec((CHUNK, 2048), lambda n: (rev(n), 0)),
        out_shape=jax.ShapeDtypeStruct((t, 5120), BF16),
        scratch_shapes=[pltpu.VMEM((RET_HEADS, 128, 256), F32)], input_output_aliases={0: 0},
        compiler_params=_params(dimension_semantics=("arbitrary",)))(
            dproj, proj, proj, do, states, cos2, sin2, dmat, qdec, kdec)


def _row_ids(i, tr):
    return i * tr + lax.broadcasted_iota(jnp.int32, (tr, 1), 0)


SH_ROWS = HALO - 8
CONV_VPU_TAPS = 21


def _shifted_copies(xs, sh, tr):
    for b in range(1, 8):
        sh[b - 1] = xs[pl.ds(b, tr + SH_ROWS), :]


def _shifted(xs, sh, off, tr, lanes=slice(None)):
    a, b = divmod(off, 8)
    return xs[pl.ds(8 * a, tr), lanes] if b == 0 else sh[b - 1, pl.ds(8 * a, tr), lanes]


def _taps_mxu(xs, sh, w_ref, offs, tr, first=0):
    sub = lax.broadcasted_iota(jnp.int32, (256, 128), 0)
    eye = (sub & 127) == lax.broadcasted_iota(jnp.int32, (256, 128), 1)
    outs = []
    for c in range(8):
        lanes = slice(128 * c, 128 * (c + 1))
        acc = None
        for w in range(first, len(offs), 2):
            wb = min(w + 1, len(offs) - 1)
            w_hi = w_ref[w:w + 1, lanes]
            w_lo = w_ref[wb:wb + 1, lanes] if wb > w else jnp.zeros((1, 128), F32)
            dmat = jnp.where(eye, jnp.where(sub < 128, w_hi, w_lo), 0.0).astype(BF16)
            lhs = jnp.concatenate([_shifted(xs, sh, offs[w], tr, lanes).astype(BF16),
                                   _shifted(xs, sh, offs[wb], tr, lanes).astype(BF16)], axis=1)
            d = _dot(lhs, dmat)
            acc = d if acc is None else acc + d
        outs.append(acc)
    return jnp.concatenate(outs, axis=1)


def _conv_fwd(cat, proj, conv_w, conv_b, ln_g, ln_b):
    t = proj.shape[0]
    tr = _rows(t)
    hb = tr // HALO

    def body(cat_in, ua_ref, ug_ref, pa_ref, pg_ref, w_ref, b_ref, lg_ref, lb_ref, c_ref, hd_ref, y_ref, xs, sh):
        del cat_in
        i = pl.program_id(0)
        hdn = ua_ref[...] * _sigmoid(ug_ref[...])
        hd_ref[...] = hdn
        prev = pa_ref[...] * _sigmoid(pg_ref[...])
        xs[0:HALO, :] = jnp.where(i > 0, prev, 0.0)
        xs[HALO:HALO + tr, :] = hdn
        _shifted_copies(xs, sh, tr)
        offs = [HALO - (CONV_WIDTH - 1) + w for w in range(CONV_WIDTH)]
        acc = _taps_mxu(xs, sh, w_ref, offs, tr, first=CONV_VPU_TAPS) + b_ref[...]
        for w in range(CONV_VPU_TAPS):
            acc += w_ref[w:w + 1, :] * _shifted(xs, sh, offs[w], tr)
        y_ref[...] = acc
        mu = jnp.mean(acc, axis=-1, keepdims=True)
        yc = acc - mu
        rstd = lax.rsqrt(jnp.mean(yc * yc, axis=-1, keepdims=True) + EPS)
        yn = yc * rstd * lg_ref[...] + lb_ref[...]
        c = yn * _sigmoid(yn)
        c_ref[...] = jnp.where(_row_ids(i, tr) >= PAD_FRONT, c, 0.0).astype(BF16)

    row = pl.BlockSpec((tr, 1024), lambda i: (i, 0))
    vec = pl.BlockSpec((1, 1024), lambda i: (0, 0))
    halo = lambda col: pl.BlockSpec((HALO, 1024), lambda i: (jnp.maximum(i * hb - 1, 0), col))
    return _pcall(body, name="conv_fwd", grid=(t // tr,),
                  in_specs=[pl.BlockSpec(memory_space=pl.ANY),
                            pl.BlockSpec((tr, 1024), lambda i: (i, 3)), pl.BlockSpec((tr, 1024), lambda i: (i, 4)),
                            halo(3), halo(4), pl.BlockSpec((32, 1024), lambda i: (0, 0)), vec, vec, vec],
                  out_specs=[pl.BlockSpec((tr, 1024), lambda i: (i, 1)), row, row],
                  out_shape=[jax.ShapeDtypeStruct((t, 2048), BF16), jax.ShapeDtypeStruct((t, 1024), F32),
                             jax.ShapeDtypeStruct((t, 1024), F32)],
                  scratch_shapes=[pltpu.VMEM((tr + HALO, 1024), F32), pltpu.VMEM((7, tr + SH_ROWS, 1024), F32)],
                  input_output_aliases={0: 0}, compiler_params=_params())(
                      cat, proj, proj, proj, proj, conv_w, conv_b, ln_g, ln_b)


def _mix_bwd_head(dh, w_out, o, proj, gn_g, y, ln_g, ln_b):
    t = dh.shape[0]
    tr = _rows(t)
    nb, r, n = w_out.shape

    def body(dh_ref, b_ref, o_ref, g_ref, w_ref, y_ref, lg_ref, lb_ref,
             do_ref, dp_ref, dw_ref, dy_ref, dlg_ref, dlb_ref, dcb_ref):
        i = pl.program_id(0)

        @pl.when(i == 0)
        def _():
            for ref in (dw_ref, dlg_ref, dlb_ref, dcb_ref):
                ref[...] = jnp.zeros_like(ref)

        dcat = _dot(dh_ref[...].astype(BF16), b_ref[...].reshape(nb * r, n), "nt")
        for h in range(RET_HEADS):
            sl = slice(256 * h, 256 * (h + 1))
            x = o_ref[:, sl]
            mu = jnp.mean(x, axis=-1, keepdims=True)
            xc = x - mu
            rstd = lax.rsqrt(jnp.mean(xc * xc, axis=-1, keepdims=True) + EPS)
            xh = xc * rstd
            w = w_ref[:, sl]
            g = g_ref[:, sl]
            sg = _sigmoid(g)
            d = dcat[:, sl]
            don = d * (g * sg)
            dp_ref[:, sl] = (d * (xh * w) * (sg * (1.0 + g * (1.0 - sg)))).astype(BF16)
            dw_ref[:, sl] += jnp.sum(don * xh, axis=0, keepdims=True)
            dxh = don * w
            m1 = jnp.mean(dxh, axis=-1, keepdims=True)
            m2 = jnp.mean(dxh * xh, axis=-1, keepdims=True)
            do_ref[:, sl] = rstd * (dxh - m1 - xh * m2)
        yv = y_ref[...]
        mu = jnp.mean(yv, axis=-1, keepdims=True)
        yc = yv - mu
        rstd = lax.rsqrt(jnp.mean(yc * yc, axis=-1, keepdims=True) + EPS)
        xh = yc * rstd
        lg = lg_ref[...]
        yn = xh * lg + lb_ref[...]
        sg = _sigmoid(yn)
        dyn = jnp.where(_row_ids(i, tr) >= PAD_FRONT, dcat[:, 1024:] * (sg * (1.0 + yn * (1.0 - sg))), 0.0)
        dlg_ref[...] += jnp.sum(dyn * xh, axis=0, keepdims=True)
        dlb_ref[...] += jnp.sum(dyn, axis=0, keepdims=True)
        dxh = dyn * lg
        m1 = jnp.mean(dxh, axis=-1, keepdims=True)
        m2 = jnp.mean(dxh * xh, axis=-1, keepdims=True)
        dy = rstd * (dxh - m1 - xh * m2)
        dy_ref[...] = dy
        dcb_ref[...] += jnp.sum(dy, axis=0, keepdims=True)

    row = pl.BlockSpec((tr, 1024), lambda i: (i, 0))
    vec = pl.BlockSpec((1, 1024), lambda i: (0, 0))
    gate = pl.BlockSpec((tr, 1024), lambda i: (i, 2))
    vsh = jax.ShapeDtypeStruct((1, 1024), F32)
    fsh = jax.ShapeDtypeStruct((t, 1024), F32)
    return _pcall(body, name="mix_bwd_head", grid=(t // tr,),
                  in_specs=[row, pl.BlockSpec((nb, r, n), lambda i: (0, 0, 0)), row, gate, vec, row, vec, vec],
                  out_specs=[row, gate, vec, row, vec, vec, vec],
                  out_shape=[fsh, jax.ShapeDtypeStruct((t, 5120), BF16), vsh, fsh, vsh, vsh, vsh],
                  compiler_params=_params(dimension_semantics=("arbitrary",)))(
                      dh, w_out, o, proj, gn_g, y, ln_g, ln_b)


def _conv_bwd_taps(dproj, dy, hdn, proj, conv_w):
    t = dy.shape[0]
    tr = _rows(t)
    hb = tr // HALO
    nt = t // tr

    def body(dp_in, dy_ref, nx_ref, hd_ref, ph_ref, ua_ref, ug_ref, w_ref, da_ref, dg_ref, dw_ref, xs, sh):
        del dp_in
        i = pl.program_id(0)

        @pl.when(i == 0)
        def _():
            dw_ref[...] = jnp.zeros_like(dw_ref)

        dy = dy_ref[...]
        xs[0:tr, :] = dy
        xs[tr:tr + HALO, :] = jnp.where(i < nt - 1, nx_ref[...], 0.0)
        _shifted_copies(xs, sh, tr)
        dh = _taps_mxu(xs, sh, w_ref, [CONV_WIDTH - 1 - w for w in range(CONV_WIDTH)], tr)
        xs[0:HALO, :] = jnp.where(i > 0, ph_ref[...], 0.0)
        xs[HALO:HALO + tr, :] = hd_ref[...]
        _shifted_copies(xs, sh, tr)
        for w in range(CONV_WIDTH):
            dw_ref[w:w + 1, :] += jnp.sum(dy * _shifted(xs, sh, HALO - (CONV_WIDTH - 1) + w, tr), axis=0, keepdims=True)
        dh = jnp.where(_row_ids(i, tr) >= PAD_FRONT, dh, 0.0)
        sg = _sigmoid(ug_ref[...])
        da_ref[...] = (dh * sg).astype(BF16)
        dg_ref[...] = (dh * ua_ref[...] * sg * (1.0 - sg)).astype(BF16)

    row = pl.BlockSpec((tr, 1024), lambda i: (i, 0))
    return _pcall(body, name="conv_bwd_taps", grid=(nt,),
                  in_specs=[pl.BlockSpec(memory_space=pl.ANY),
                            row, pl.BlockSpec((HALO, 1024), lambda i: (jnp.minimum((i + 1) * hb, nt * hb - 1), 0)),
                            row, pl.BlockSpec((HALO, 1024), lambda i: (jnp.maximum(i * hb - 1, 0), 0)),
                            pl.BlockSpec((tr, 1024), lambda i: (i, 3)), pl.BlockSpec((tr, 1024), lambda i: (i, 4)),
                            pl.BlockSpec((32, 1024), lambda i: (0, 0))],
                  out_specs=[pl.BlockSpec((tr, 1024), lambda i: (i, 3)), row, pl.BlockSpec((32, 1024), lambda i: (0, 0))],
                  out_shape=[jax.ShapeDtypeStruct((t, 5120), BF16), jax.ShapeDtypeStruct((t, 1024), BF16),
                             jax.ShapeDtypeStruct((32, 1024), F32)],
                  scratch_shapes=[pltpu.VMEM((tr + HALO, 1024), F32), pltpu.VMEM((7, tr + SH_ROWS, 1024), F32)],
                  input_output_aliases={0: 0}, compiler_params=_params())(
                      dproj, dy, dy, hdn, hdn, proj, proj, conv_w)


NEG_BIG = -1e30


def _seg_tables(qb):
    j = np.arange(128)
    bd = (j[:, None] // 64 == j[None, :] // 64).astype(np.float32)
    ones = np.ones((128, 128), np.float32)
    later = np.concatenate([(j[:, None] >= j[None, :]).astype(np.float32), ones], axis=1)
    earlier = np.concatenate([(j[:, None] < j[None, :]).astype(np.float32), ones], axis=1)
    per = qb // CHUNK
    row = np.arange(qb)[:, None]
    pad = np.broadcast_to(j[None, :] < PAD_FRONT, (qb, 128))
    diag = [(g * CHUNK + j[None, :]) >= row for g in range(per)]
    masks = diag + [np.zeros((qb, 128), bool), pad, diag[0] | pad]
    bias = np.stack([np.where(m, NEG_BIG, 0.0) for m in masks]).astype(np.float32)
    dup = lambda m: np.concatenate([m, m], axis=0)
    return (jnp.asarray(bd, BF16), jnp.asarray(dup(later), BF16), jnp.asarray(dup(earlier), BF16),
            jnp.asarray(bias, F32))


def _split_dot(x, m):
    hi = x.astype(BF16)
    lo = (x - hi.astype(F32)).astype(BF16)
    return _dot(hi, m) + _dot(lo, m)


def _qk_norm_fwd(qkv, qg, kg, bd):
    t = qkv.shape[0]
    tr = _rows(t)
    nb = tr // CHUNK

    def body(q_ref, k_ref, v_ref, qg_ref, kg_ref, bd_ref, qo, kt, k2, vt, v2):
        bdm = bd_ref[...]
        lane = lax.broadcasted_iota(jnp.int32, (1, 128), 1)
        sub = lax.broadcasted_iota(jnp.int32, (128, 1), 0)

        def pair_layouts(x, t_ref, s_ref, hp, b):
            xt = x.T
            t_ref[hp, b] = jnp.concatenate([jnp.where(sub < 64, xt, 0.0), jnp.where(sub >= 64, xt, 0.0)],
                                           axis=1).astype(BF16)
            s_ref[hp, b] = jnp.concatenate([jnp.where(lane < 64, x, 0.0), jnp.where(lane >= 64, x, 0.0)],
                                           axis=0).astype(BF16)

        for hp in range(8):
            sl = slice(128 * hp, 128 * (hp + 1))
            x = q_ref[:, sl]
            r = lax.rsqrt(_split_dot(x * x, bdm) * (1.0 / 64) + EPS)
            qo[:, sl] = (x * r * (qg_ref[:, sl] * SB_SCALE)).astype(BF16)
            x = k_ref[:, sl]
            r = lax.rsqrt(_split_dot(x * x, bdm) * (1.0 / 64) + EPS)
            kn = x * r * kg_ref[:, sl]
            v = v_ref[:, sl]
            for b in range(nb):
                rows = slice(CHUNK * b, CHUNK * (b + 1))
                pair_layouts(kn[rows], kt, k2, hp, b)
                pair_layouts(v[rows], vt, v2, hp, b)

    col = lambda c: pl.BlockSpec((tr, 1024), lambda i: (i, c))
    vec = pl.BlockSpec((1, 1024), lambda i: (0, 0))
    wide = pl.BlockSpec((8, nb, 128, 256), lambda i: (0, i, 0, 0))
    tall = pl.BlockSpec((8, nb, 256, 128), lambda i: (0, i, 0, 0))
    wsh = jax.ShapeDtypeStruct((8, t // CHUNK, 128, 256), BF16)
    tsh = jax.ShapeDtypeStruct((8, t // CHUNK, 256, 128), BF16)
    return _pcall(body, name="qk_norm_fwd", grid=(t // tr,),
                  in_specs=[col(0), col(1), col(2), vec, vec, pl.BlockSpec((128, 128), lambda i: (0, 0))],
                  out_specs=[col(0), wide, tall, wide, tall],
                  out_shape=[jax.ShapeDtypeStruct((t, 1024), BF16), wsh, tsh, wsh, tsh])(qkv, qkv, qkv, qg, kg, bd)


def _qk_norm_bwd(qkv, dq, dk, dv, qg, kg, bd):
    t = qkv.shape[0]
    tr = _rows(t)

    def body(q_ref, k_ref, dq_ref, dk_ref, dv_ref, qg_ref, kg_ref, bd_ref, o_ref, dqg_ref, dkg_ref):
        @pl.when(pl.program_id(0) == 0)
        def _():
            dqg_ref[...] = jnp.zeros_like(dqg_ref)
            dkg_ref[...] = jnp.zeros_like(dkg_ref)

        bdm = bd_ref[...]
        for part, (src, d_ref, g_ref, dg_ref) in enumerate(((q_ref, dq_ref, qg_ref, dqg_ref),
                                                           (k_ref, dk_ref, kg_ref, dkg_ref))):
            for cix in range(8):
                sl = slice(128 * cix, 128 * (cix + 1))
                x = src[:, sl]
                d = d_ref[:, sl]
                r = lax.rsqrt(_split_dot(x * x, bdm) * (1.0 / 64) + EPS)
                u = d * g_ref[:, sl]
                m = _split_dot(u * x, bdm) * (1.0 / 64)
                o_ref[:, 1024 * part + 128 * cix:1024 * part + 128 * (cix + 1)] = (r * u - x * (r * r * r * m)).astype(BF16)
                dg_ref[:, sl] += jnp.sum(d * x * r, axis=0, keepdims=True)
        o_ref[:, 2048:3072] = dv_ref[...].astype(BF16)

    col = lambda c: pl.BlockSpec((tr, 1024), lambda i: (i, c))
    vec = pl.BlockSpec((1, 1024), lambda i: (0, 0))
    vsh = jax.ShapeDtypeStruct((1, 1024), F32)
    return _pcall(body, name="qk_norm_bwd", grid=(t // tr,),
                  in_specs=[col(0), col(1), col(0), col(0), col(0), vec, vec, pl.BlockSpec((128, 128), lambda i: (0, 0))],
                  out_specs=[pl.BlockSpec((tr, 3072), lambda i: (i, 0)), vec, vec],
                  out_shape=[jax.ShapeDtypeStruct((t, 3072), BF16), vsh, vsh])(qkv, qkv, dq, dk, dv, qg, kg, bd)


def _split2(x):
    hi = x.astype(BF16)
    lo = (x - hi.astype(F32)).astype(BF16)
    return jnp.concatenate([hi, lo], axis=1)


def _sb_sums(z, later_tab):
    sp = jnp.maximum(z, 0.0) + jnp.log(1.0 + jnp.exp(-jnp.abs(z)))
    return _dot(_split2(sp), later_tab)


def _sb_bias_index(i, kb, per):
    g = kb - i * per
    return jnp.where(kb == 0, jnp.where(i == 0, per + 2, per + 1), jnp.where(g >= 0, g, per))


def _sb_qb(t):
    return _tile(t, (384, 128))


def _sb_fwd(qh, kt, v2, later_tab, bias_tab):
    t = qh.shape[0]
    qb = _sb_qb(t)
    per = qb // CHUNK
    nkb_all = t // CHUNK

    nq = t // qb

    def body(q_ref, kt_ref, v2_ref, tab_ref, bias_ref, o_ref, ws_ref, acc, carry, zbuf, wbuf, wsem):
        h, i = pl.program_id(0), pl.program_id(1)
        n = h * nq + i
        p = n & 1
        q = q_ref[...]
        acc[...] = jnp.zeros_like(acc)
        carry[...] = jnp.zeros_like(carry)
        nkb = (i + 1) * per
        save = lambda kb: pltpu.make_async_copy(wbuf.at[p, kb], ws_ref.at[h, i, kb], wsem.at[p, kb])

        def drain(step, par):
            hs, is_ = step // nq, step % nq

            def one(kb, _):
                pltpu.make_async_copy(wbuf.at[par, kb], ws_ref.at[hs, is_, kb], wsem.at[par, kb]).wait()
                return 0

            lax.fori_loop(0, (is_ + 1) * per, one, 0)

        @pl.when(n >= 2)
        def _():
            drain(n - 2, p)

        for u in range(per):
            zbuf[u] = _dot(q, kt_ref[nkb - 1 - u])

        def trip(s, diagonal):
            top = nkb - 1 - per * s
            if not diagonal:
                for u in range(per):
                    save(top + per - u).start()
            z2s = [zbuf[u] for u in range(per)]
            for u in range(per):
                zbuf[u] = _dot(q, kt_ref[jnp.maximum(top - per - u, 0)])
            first = [CHUNK * (per - 1 - u) if diagonal else 0 for u in range(per)]
            cins = [carry[0], carry[1]]
            zs, cus = [], []
            for u in range(per):
                zs.append([z2s[u][first[u]:, 128 * hh:128 * (hh + 1)] for hh in range(2)])
                if diagonal or u == per - 1:
                    bias = bias_ref[_sb_bias_index(i, top - u, per)][first[u]:]
                    zs[u] = [z + bias for z in zs[u]]
                cus.append([_sb_sums(z, tab_ref[...]) for z in zs[u]])
            part = None
            for u in range(per):
                kb, lo = top - u, first[u]
                for hh in range(2):
                    sl = slice(128 * hh, 128 * (hh + 1))
                    cu = cus[u][hh]
                    wbuf[p, kb, lo:, sl] = jnp.exp(zs[u][hh] - cu[:, :128] - cins[hh][lo:]).astype(BF16)
                    if lo:
                        wbuf[p, kb, :lo, sl] = jnp.zeros((lo, 128), BF16)
                        cins[hh] = jnp.concatenate([cins[hh][:lo], cins[hh][lo:] + cu[:, 128:]], axis=0)
                    else:
                        cins[hh] = cins[hh] + cu[:, 128:]
                d = _dot(wbuf[p, kb], v2_ref[kb])
                part = d if part is None else part + d
            carry[0], carry[1] = cins[0], cins[1]
            acc[...] += part

        trip(0, True)

        def step(s, _):
            trip(s, False)
            return 0

        lax.fori_loop(1, nkb // per, step, 0)
        for u in range(per):
            save(per - 1 - u).start()
        o_ref[...] = acc[...]

        @pl.when(n == 8 * nq - 1)
        def _():
            drain(n - 1, 1 - p)
            drain(n, p)

    blk = pl.BlockSpec((qb, 128), lambda h, i: (i, h))
    wide = pl.BlockSpec((None, nkb_all, 128, 256), lambda h, i: (h, 0, 0, 0))
    tall = pl.BlockSpec((None, nkb_all, 256, 128), lambda h, i: (h, 0, 0, 0))
    return _pcall(body, name="sb_fwd", grid=(8, t // qb),
                  in_specs=[blk, wide, tall, pl.BlockSpec((256, 256), lambda h, i: (0, 0)),
                            pl.BlockSpec((per + 3, qb, 128), lambda h, i: (0, 0, 0))],
                  out_specs=[blk, pl.BlockSpec(memory_space=pl.ANY)],
                  out_shape=[jax.ShapeDtypeStruct((t, 1024), F32),
                             jax.ShapeDtypeStruct((8, t // qb, nkb_all, qb, 256), BF16)],
                  scratch_shapes=[pltpu.VMEM((qb, 128), F32), pltpu.VMEM((2, qb, 128), F32),
                                  pltpu.VMEM((per, qb, 256), F32), pltpu.VMEM((2, nkb_all, qb, 256), BF16),
                                  pltpu.SemaphoreType.DMA((2, nkb_all))],
                  compiler_params=_params(dimension_semantics=("arbitrary", "arbitrary")))(
                      qh, kt, v2, later_tab, bias_tab)


def _sb_bwd(qh, kt, k2, vt, wsave, do, earlier_tab, bias_tab):
    t = qh.shape[0]
    qb = _sb_qb(t)
    per = qb // CHUNK
    nkb_all = t // CHUNK

    zero_slot = nkb_all
    nq = t // qb

    def body(q_ref, kt_ref, k2_ref, vt_ref, ws_ref, do_ref, etab_ref, bias_ref,
             dq_ref, dk_ref, dv_ref, acc, gcarry, zbuf, dwbuf, wbuf, wsem, dzbuf):
        h, i = pl.program_id(0), pl.program_id(1)
        n = h * nq + i
        p = n & 1

        @pl.when(i == 0)
        def _():
            dk_ref[...] = jnp.zeros_like(dk_ref)
            dv_ref[...] = jnp.zeros_like(dv_ref)

        nkb = (i + 1) * per
        fetch = lambda kb: pltpu.make_async_copy(ws_ref.at[h, i, kb], wbuf.at[p, kb], wsem.at[p, kb])

        def prefetch(step, par):
            hs, is_ = step // nq, step % nq

            def one(kb, _):
                pltpu.make_async_copy(ws_ref.at[hs, is_, kb], wbuf.at[par, kb], wsem.at[par, kb]).start()
                return 0

            lax.fori_loop(0, (is_ + 1) * per, one, 0)

        @pl.when(n == 0)
        def _():
            prefetch(n, p)

        @pl.when(n + 1 < 8 * nq)
        def _():
            prefetch(n + 1, 1 - p)

        q = q_ref[...]
        dob = do_ref[...].astype(BF16)
        acc[...] = jnp.zeros_like(acc)
        gcarry[...] = jnp.zeros_like(gcarry)
        zbuf[...] = _dot(q, kt_ref[0])
        dwbuf[...] = _dot(dob, vt_ref[0])
        dzbuf[...] = jnp.zeros_like(dzbuf)
        wbuf[p, zero_slot] = jnp.zeros((qb, 256), BF16)

        q_t = q.astype(F32).T.astype(BF16)
        do_t = do_ref[...].T.astype(BF16)
        sub = lax.broadcasted_iota(jnp.int32, (128, 1), 0)

        def gradients(slot, kb):
            dz2 = dzbuf[...]
            acc[...] += _dot(dz2, k2_ref[kb])
            dk2 = _dot(q_t, dz2)
            dv2 = _dot(do_t, wbuf[p, slot])
            dk_ref[kb] += jnp.where(sub < 64, dk2[:, :128], dk2[:, 128:])
            dv_ref[kb] += jnp.where(sub < 64, dv2[:, :128], dv2[:, 128:])

        def trip(kb, lo):
            fetch(kb).wait()
            bias = bias_ref[_sb_bias_index(i, kb, per)][lo:]
            z2 = zbuf[...]
            dw2 = dwbuf[...]
            nxt = jnp.minimum(kb + 1, nkb - 1)
            zbuf[...] = _dot(q, kt_ref[nxt])
            dwbuf[...] = _dot(dob, vt_ref[nxt])
            gradients(jnp.where(kb == 0, zero_slot, kb - 1), jnp.maximum(kb - 1, 0))
            w2 = wbuf[p, kb]
            for hh in range(2):
                sl = slice(128 * hh, 128 * (hh + 1))
                z = z2[lo:, sl] + bias
                e = jnp.exp(-jnp.abs(z))
                r = 1.0 / (1.0 + e)
                sig = jnp.where(z >= 0, r, e * r)
                gw = w2[lo:, sl].astype(F32) * dw2[lo:, sl]
                cu2 = _dot(_split2(gw), etab_ref[...])
                gin = gcarry[hh, lo:, :]
                gcarry[hh, lo:, :] = gin + cu2[:, 128:]
                dzbuf[lo:, sl] = (gw - sig * (gw + cu2[:, :128] + gin)).astype(BF16)
                if lo:
                    dzbuf[:lo, sl] = jnp.zeros((lo, 128), BF16)

        def step(kb, _):
            trip(kb, 0)
            return 0

        lax.fori_loop(0, nkb - per, step, 0)
        for g in range(per):
            trip(nkb - per + g, CHUNK * g)
        gradients(nkb - 1, nkb - 1)
        dq_ref[...] = acc[...] * SB_SCALE

        @pl.when(i == nq - 1)
        def _():
            def untranspose(kb, _):
                dk_ref[kb] = dk_ref[kb].T
                dv_ref[kb] = dv_ref[kb].T
                return 0

            lax.fori_loop(0, nkb_all, untranspose, 0)

    blk = pl.BlockSpec((qb, 128), lambda h, i: (i, h))
    wide = pl.BlockSpec((None, nkb_all, 128, 256), lambda h, i: (h, 0, 0, 0))
    tall = pl.BlockSpec((None, nkb_all, 256, 128), lambda h, i: (h, 0, 0, 0))
    tab = pl.BlockSpec((256, 256), lambda h, i: (0, 0))
    kv_out = pl.BlockSpec((nkb_all, 128, 128), lambda h, i: (0, 0, h))
    ksh = jax.ShapeDtypeStruct((nkb_all, 128, 1024), F32)
    dq, dk, dv = _pcall(
        body, name="sb_bwd", grid=(8, t // qb),
        in_specs=[blk, wide, tall, wide, pl.BlockSpec(memory_space=pl.ANY), blk, tab,
                  pl.BlockSpec((per + 3, qb, 128), lambda h, i: (0, 0, 0))],
        out_specs=[blk, kv_out, kv_out], out_shape=[jax.ShapeDtypeStruct((t, 1024), F32), ksh, ksh],
        scratch_shapes=[pltpu.VMEM((qb, 128), F32), pltpu.VMEM((2, qb, 128), F32),
                        pltpu.VMEM((qb, 256), F32), pltpu.VMEM((qb, 256), F32),
                        pltpu.VMEM((2, nkb_all + 1, qb, 256), BF16), pltpu.SemaphoreType.DMA((2, nkb_all)),
                        pltpu.VMEM((qb, 256), BF16)],
        compiler_params=_params(dimension_semantics=("arbitrary", "arbitrary")))(
            qh, kt, k2, vt, wsave, do, earlier_tab, bias_tab)
    return dq, dk.reshape(t, 1024), dv.reshape(t, 1024)


def _adamw_math(w, g, m, v):
    m = ADAM_B1 * m + (1.0 - ADAM_B1) * g
    v = ADAM_B2 * v + (1.0 - ADAM_B2) * (g * g)
    m_hat = m / (1.0 - ADAM_B1 ** ADAM_STEP)
    v_hat = v / (1.0 - ADAM_B2 ** ADAM_STEP)
    delta = -ADAM_LR * (m_hat / (jnp.sqrt(v_hat) + ADAM_EPS) + ADAM_WD * w)
    return delta, m, v


def _adamw(name, w, owns, recvs, m, v, me):
    shape = w.shape
    c = shape[-1]
    nl = len(owns)
    w3, m3, v3 = (a.reshape(nl, -1, c) for a in (w, m, v))
    r = w3.shape[1]
    tr = _tile(r, (256, 128))
    owns = [o.reshape(N_DEV, r, c) for o in owns]
    recvs = [p.reshape(N_DEV - 1, r, c) for p in recvs]

    def body(me_ref, w_ref, *rest):
        own_refs, recv_refs = rest[:nl], rest[nl:2 * nl]
        m_ref, v_ref = rest[2 * nl:2 * nl + 2]
        g_out, d_out, m_out, v_out = rest[2 * nl + 2:]
        layer = pl.program_id(0)

        def grad(k):
            g = own_refs[k][...].astype(F32)
            for s in range(N_DEV - 1):
                g = g + recv_refs[k][s].astype(F32)
            return g

        g = grad(0)
        for k in range(1, nl):
            g = jnp.where(layer == k, grad(k), g)
        d, mn, vn = _adamw_math(w_ref[...], g, m_ref[...], v_ref[...])
        g_out[...] = g
        d_out[...] = d
        m_out[...] = mn
        v_out[...] = vn

    row = pl.BlockSpec((None, tr, c), lambda l, i, me_ref: (l, i, 0))
    own = lambda k: pl.BlockSpec((None, tr, c), lambda l, i, me_ref: (me_ref[0], jnp.where(l == k, i, 0), 0))
    rcv = lambda k: pl.BlockSpec((N_DEV - 1, tr, c), lambda l, i, me_ref: (0, jnp.where(l == k, i, 0), 0))
    osh = jax.ShapeDtypeStruct((nl, r, c), F32)
    grid_spec = pltpu.PrefetchScalarGridSpec(
        num_scalar_prefetch=1, grid=(nl, r // tr),
        in_specs=[row] + [own(k) for k in range(nl)] + [rcv(k) for k in range(nl)] + [row, row],
        out_specs=[row, row, row, row])
    outs = _pcall(body, name=name, grid_spec=grid_spec, out_shape=[osh, osh, osh, osh])(
        me.reshape(1), w3, *owns, *recvs, m3, v3)
    return tuple(o.reshape(shape) for o in outs)


def _place():
    x, y, c = lax.axis_index("x"), lax.axis_index("y"), lax.axis_index("c")
    return x, y, c, 4 * x + 2 * y + c


def _peer(x, y, c, rel):
    return (x ^ ((rel >> 2) & 1), y ^ ((rel >> 1) & 1), c ^ (rel & 1))


def _gather_first(now, later):
    n, k = len(now), len(later)

    def body(*refs):
        ins, outs = refs[:n + k], refs[n + k:2 * (n + k)]
        send, recv, lsem = refs[2 * (n + k):]
        x, y, c, me = _place()
        locals_ = []
        for w in range(n + k):
            local = pltpu.make_async_copy(ins[w], outs[w].at[me], lsem.at[w])
            local.start()
            locals_.append(local)
        def copy(w, src, slot, rel, to_rel):
            return pltpu.make_async_remote_copy(src_ref=src, dst_ref=outs[w].at[slot], send_sem=send.at[w, rel - 1],
                                                recv_sem=recv.at[w, rel - 1], device_id=_peer(x, y, c, to_rel),
                                                device_id_type=MESH)

        for w in range(n):
            for rel in (1, 2, 4, 6):
                copy(w, ins[w], me, rel, rel).start()
        for w in range(n):
            for rel in (2, 4, 6):
                copy(w, ins[w], me ^ rel, rel, rel).wait_recv()
                copy(w, outs[w].at[me ^ rel], me ^ rel, rel | 1, 1).start()
        for w in range(n):
            for rel in (1, 3, 5, 7):
                copy(w, ins[w], me ^ rel, rel, 1).wait_recv()
            for rel in range(1, N_DEV):
                copy(w, ins[w], me, rel, rel).wait_send()
        for local in locals_:
            local.wait()

    hbm = pl.BlockSpec(memory_space=pl.ANY)
    vmem = pl.BlockSpec(memory_space=pltpu.VMEM)
    arrays = list(now) + list(later)
    return _pcall(body, name="gather_first", in_specs=[vmem] * (n + k), out_specs=[hbm] * (n + k),
                  out_shape=[jax.ShapeDtypeStruct((N_DEV,) + a.shape, a.dtype) for a in arrays],
                  scratch_shapes=[pltpu.SemaphoreType.DMA((n, N_DEV - 1)), pltpu.SemaphoreType.DMA((n, N_DEV - 1)),
                                  pltpu.SemaphoreType.DMA((n + k,))],
                  compiler_params=_params(has_side_effects=True))(*arrays)


_HBM = pl.BlockSpec(memory_space=pltpu.HBM)
_SEM = pl.BlockSpec(memory_space=pltpu.SEMAPHORE)
_DATAFLOW = pltpu.SideEffectType.DATAFLOW_SIDE_EFFECTING


def _exchange_refs(srcs, lands, mode, me, rel, j):
    if mode == "gather":
        return srcs[j], lands[j].at[me], lands[j].at[me ^ rel]
    return srcs[j].at[me ^ rel], lands[j].at[rel - 1], lands[j].at[rel - 1]


def _exchange_start(name, srcs, lands, mode):
    n = len(srcs)

    def body(*refs):
        ins, lnd = refs[:n], refs[n:2 * n]
        send, recv = refs[2 * n], refs[2 * n + 1]
        token = refs[-1]
        x, y, c, me = _place()
        for j in range(n):
            for rel in range(1, N_DEV):
                src, dst, _ = _exchange_refs(ins, lnd, mode, me, rel, j)
                pltpu.make_async_remote_copy(src_ref=src, dst_ref=dst, send_sem=send.at[j * (N_DEV - 1) + rel - 1],
                                             recv_sem=recv.at[j * (N_DEV - 1) + rel - 1],
                                             device_id=_peer(x, y, c, rel), device_id_type=MESH).start()
        token[...] = jnp.zeros_like(token)

    sems = pltpu.SemaphoreType.DMA((n * (N_DEV - 1),))
    hbm_like = lambda a: pltpu.HBM(a.shape, a.dtype)
    outs = _pcall(body, name=name + "_start",
                  in_specs=[_HBM] * (2 * n), out_specs=[_SEM, _SEM] + [_HBM] * (2 * n) + [pl.BlockSpec(memory_space=pltpu.VMEM)],
                  out_shape=[sems, sems] + [hbm_like(a) for a in srcs] + [hbm_like(a) for a in lands]
                  + [jax.ShapeDtypeStruct((8, 128), F32)],
                  input_output_aliases={i: 2 + i for i in range(2 * n)},
                  compiler_params=pltpu.CompilerParams(has_side_effects=_DATAFLOW))(
                      *[pltpu.with_memory_space_constraint(a, pltpu.HBM) for a in list(srcs) + list(lands)])
    return dict(name=name, mode=mode, n=n, send=outs[0], recv=outs[1], srcs=outs[2:2 + n], lands=outs[2 + n:2 + 2 * n],
                token=outs[-1][0, 0])


def _exchange_wait(ex, after):
    n, mode = ex["n"], ex["mode"]

    def body(*refs):
        ins, lnd = refs[:n], refs[n:2 * n]
        send, recv = refs[2 * n], refs[2 * n + 1]
        x, y, c, me = _place()
        for j in range(n):
            for rel in range(1, N_DEV):
                src, dst, landed = _exchange_refs(ins, lnd, mode, me, rel, j)
                pltpu.make_async_remote_copy(src_ref=src, dst_ref=dst, send_sem=send.at[j * (N_DEV - 1) + rel - 1],
                                             recv_sem=recv.at[j * (N_DEV - 1) + rel - 1],
                                             device_id=_peer(x, y, c, rel), device_id_type=MESH).wait_send()
                pltpu.make_async_remote_copy(src_ref=src, dst_ref=landed, send_sem=send.at[j * (N_DEV - 1) + rel - 1],
                                             recv_sem=recv.at[j * (N_DEV - 1) + rel - 1],
                                             device_id=_peer(x, y, c, rel), device_id_type=MESH).wait_recv()

    hbm_like = lambda a: pltpu.HBM(a.shape, a.dtype)
    arrays = list(ex["srcs"]) + list(ex["lands"])
    outs = _pcall(body, name=ex["name"] + "_wait",
                  in_specs=[_HBM] * (2 * n) + [_SEM, _SEM, pl.BlockSpec(memory_space=pl.ANY)],
                  out_specs=[_HBM] * (2 * n), out_shape=[hbm_like(a) for a in arrays],
                  input_output_aliases={i: i for i in range(2 * n)},
                  compiler_params=pltpu.CompilerParams(has_side_effects=_DATAFLOW))(
                      *arrays, ex["send"], ex["recv"], after)
    return outs[:n], outs[n:]


def _scatter_start(name, grads):
    lands = [lax.empty((N_DEV - 1,) + g.shape[1:], g.dtype) for g in grads]
    return _exchange_start(name, grads, lands, "scatter")


ROW_MIX, ROW_MLP, ROW_CB, ROW_LG, ROW_LB, ROW_QN, ROW_KN, ROW_LOSS = 0, 2, 4, 5, 6, 7, 8, 9
ROW_META, ROW_CW, ROW_GN, SMALL_ROWS = 16, 32, 64, 72


def _sum_small(slots):
    def body(s_ref, o_ref):
        tot = s_ref[0]
        for s in range(1, N_DEV):
            tot = tot + s_ref[s]
        o_ref[...] = tot
        for row in (ROW_QN, ROW_KN):
            v = tot[row:row + 1, :]
            f = v[:, 0:128]
            for k in range(1, 8):
                f = f + v[:, 128 * k:128 * (k + 1)]
            o_ref[row:row + 1, 0:64] = f[:, 0:64] + f[:, 64:128]

    return _pcall(body, name="sum_small", out_shape=jax.ShapeDtypeStruct(slots.shape[1:], F32))(slots)


def _adamw_small(w, g, m, v):
    def body(w_ref, g_ref, m_ref, v_ref, d_out, m_out, v_out):
        d, mn, vn = _adamw_math(w_ref[...], g_ref[...], m_ref[...], v_ref[...])
        d_out[...] = d
        m_out[...] = mn
        v_out[...] = vn

    osh = jax.ShapeDtypeStruct(w.shape, F32)
    return _pcall(body, name="adamw_small", out_shape=[osh, osh, osh])(w, g, m, v)


def _local_step(h0, target, p, weight, emit):
    t = h0.shape[0]
    tables = _ret_tables(t)
    bd, later_tab, earlier_tab, bias_tab = _seg_tables(_sb_qb(t))
    row = lambda a, i: a[i:i + 1]

    hn_a = _rms_fwd("rms_mix0", h0, row(p["norm_mix_g"], 0))
    w_in = weight("w_in", hn_a)
    proj = _mm_cols("proj_in", hn_a, w_in, ())
    gn_flat = p["gn_g"].reshape(1, 1024)
    o_ret, states, cat = _ret_fwd(proj, gn_flat, tables)
    cat, hdn, ycv = _conv_fwd(cat, proj, p["conv_w"], p["conv_b"], p["ln_g"], p["ln_b"])
    w_out = weight("w_out", cat)
    h1, hn_b = _mm_rows_norm("mix_out", cat, w_out, h0, row(p["norm_mlp_g"], 0))
    w1_0, w2_0 = weight("w1_0", hn_b), weight("w2_0", hn_b)
    a0, s0 = _mm_cols("mlp0_up", hn_b, w1_0, (), epi="relu2")
    h2, hn_c = _mm_rows_norm("mlp0_down", s0, w2_0, h1, row(p["norm_mix_g"], 1))

    w_qkv = weight("w_qkv", hn_c)
    qkv = _mm_cols("qkv", hn_c, w_qkv, ())
    qg = jnp.tile(p["qn_g"], (1, 16))
    kg = jnp.tile(p["kn_g"], (1, 16))
    qh, kt, k2, vt, v2 = _qk_norm_fwd(qkv, qg, kg, bd)
    o_sb, w_sb = _sb_fwd(qh, kt, v2, later_tab, bias_tab)
    w_o = weight("w_o", o_sb)
    h3, hn_d = _mm_rows_norm("attn_out", o_sb, w_o, h2, row(p["norm_mlp_g"], 1))
    w1_1, w2_1 = weight("w1_1", hn_d), weight("w2_1", hn_d)
    a1, s1 = _mm_cols("mlp1_up", hn_d, w1_1, (), epi="relu2")
    dh, loss = _mm_rows_loss("mlp1_down", s1, w2_1, h3, target)

    def mlp_bwd(tag, layer, w1, w2, dh, h_in, hn, a, s):
        da = _mm_rows_t(f"{tag}_dact", dh, w2, (), out_dtype=BF16, epi="drelu2", extra=a)
        dw2 = _wgrad_rows(f"{tag}_dw2", s, dh, 512)
        dw1 = _wgrad_cols(f"{tag}_dw1", hn, da, 512)
        tok = emit(tag, [dw1, dw2])
        return _mm_cols_t_rms(f"{tag}_dhn", da, w1, h_in, row(p["norm_mlp_g"], layer) + tok, dh)

    dh, dg_mlp1 = mlp_bwd("mlp1", 1, w1_1, w2_1, dh, h3, hn_d, a1, s1)

    do_sb = _mm_rows_t("attn_dout", dh, w_o, ())
    dw_o = _wgrad_rows("attn_dwo", o_sb, dh, 128)
    dq, dk, dv = _sb_bwd(qh, kt, k2, vt, w_sb, do_sb, earlier_tab, bias_tab)
    dqkv, dqg, dkg = _qk_norm_bwd(qkv, dq, dk, dv, qg, kg, bd)
    dw_qkv = _wgrad_cols("qkv_dw", hn_c, dqkv, 384)
    tok = emit("attn", [dw_qkv, dw_o])
    dh, dg_mix1 = _mm_cols_t_rms("qkv_dhn", dqkv, w_qkv, h2, row(p["norm_mix_g"], 1) + tok, dh)

    dh, dg_mlp0 = mlp_bwd("mlp0", 0, w1_0, w2_0, dh, h1, hn_b, a0, s0)

    dw_out = _wgrad_rows("mix_dwout", cat, dh, 256)
    tok = emit("mix0_out", [dw_out])
    do_ret, dproj, dgn, dy, dlg, dlb, dcb = _mix_bwd_head(dh, w_out, o_ret, proj, gn_flat + tok, ycv,
                                                          p["ln_g"], p["ln_b"])
    dproj = _ret_bwd(dproj, proj, states, do_ret, tables)
    dproj, dug, dcw = _conv_bwd_taps(dproj, dy, hdn, proj, p["conv_w"])
    dproj = lax.dynamic_update_slice(dproj, dug, (0, 4096))
    dw_in = _wgrad_cols("proj_dw", hn_a, dproj, 640)
    tok = emit("mix0", [dw_in])
    dh, dg_mix0 = _mm_cols_t_rms("proj_dhn", dproj, w_in, h0, row(p["norm_mix_g"], 0) + tok, dh)

    rid = lax.broadcasted_iota(jnp.int32, (16, 1), 0)
    loss_row = jnp.broadcast_to(loss[0:1, 0:1], (1, D_MODEL))
    vecs = sum(jnp.where(rid == k, v, 0.0)
               for k, v in enumerate((dg_mix0, dg_mix1, dg_mlp0, dg_mlp1, dcb, dlg, dlb, dqg, dkg, loss_row)))
    small = jnp.concatenate([vecs, dh[PAD_FRONT:TOK0], dcw, jnp.where(rid[:8] == 0, dgn, 0.0)], axis=0)
    return dh[TOK0:], small


_SMALL_NAMES = ("meta", "norm_mix_g", "norm_mlp_g", "even_ret_gn_g", "even_conv_w", "even_conv_b",
                "even_conv_ln_g", "even_conv_ln_b", "odd_q_norm_g", "odd_k_norm_g")
_BIG_NAMES = ("even_w_in", "even_w_out", "odd_w_qkv", "odd_w_o", "mlp_w1", "mlp_w2")
_ORDER = ("meta", "norm_mix_g", "norm_mlp_g", "even_w_in", "even_ret_gn_g", "even_conv_w", "even_conv_b",
          "even_conv_ln_g", "even_conv_ln_b", "even_w_out", "odd_w_qkv", "odd_q_norm_g", "odd_k_norm_g",
          "odd_w_o", "mlp_w1", "mlp_w2")


def _pack128(a):
    flat = a.reshape(-1)
    n = flat.shape[0]
    rows = -(-n // 128)
    rows8 = -(-rows // 8) * 8
    return jnp.pad(flat, (0, rows8 * 128 - n)).reshape(rows8, 128)


def kernel(x, meta, norm_mix_g, norm_mlp_g, even_w_in, even_ret_gn_g, even_conv_w, even_conv_b, even_conv_ln_g, even_conv_ln_b, even_w_out, odd_w_qkv, odd_q_norm_g, odd_k_norm_g, odd_w_o, mlp_w1, mlp_w2, loss_target, m_meta, m_norm_mix_g, m_norm_mlp_g, m_even_w_in, m_even_ret_gn_g, m_even_conv_w, m_even_conv_b, m_even_conv_ln_g, m_even_conv_ln_b, m_even_w_out, m_odd_w_qkv, m_odd_q_norm_g, m_odd_k_norm_g, m_odd_w_o, m_mlp_w1, m_mlp_w2, v_meta, v_norm_mix_g, v_norm_mlp_g, v_even_w_in, v_even_ret_gn_g, v_even_conv_w, v_even_conv_b, v_even_conv_ln_g, v_even_conv_ln_b, v_even_w_out, v_odd_w_qkv, v_odd_q_norm_g, v_odd_k_norm_g, v_odd_w_o, v_mlp_w1, v_mlp_w2):
    w = dict(meta=meta, norm_mix_g=norm_mix_g, norm_mlp_g=norm_mlp_g, even_w_in=even_w_in,
             even_ret_gn_g=even_ret_gn_g, even_conv_w=even_conv_w, even_conv_b=even_conv_b,
             even_conv_ln_g=even_conv_ln_g, even_conv_ln_b=even_conv_ln_b, even_w_out=even_w_out,
             odd_w_qkv=odd_w_qkv, odd_q_norm_g=odd_q_norm_g, odd_k_norm_g=odd_k_norm_g, odd_w_o=odd_w_o,
             mlp_w1=mlp_w1, mlp_w2=mlp_w2)
    mom = dict(meta=m_meta, norm_mix_g=m_norm_mix_g, norm_mlp_g=m_norm_mlp_g, even_w_in=m_even_w_in,
               even_ret_gn_g=m_even_ret_gn_g, even_conv_w=m_even_conv_w, even_conv_b=m_even_conv_b,
               even_conv_ln_g=m_even_conv_ln_g, even_conv_ln_b=m_even_conv_ln_b, even_w_out=m_even_w_out,
               odd_w_qkv=m_odd_w_qkv, odd_q_norm_g=m_odd_q_norm_g, odd_k_norm_g=m_odd_k_norm_g, odd_w_o=m_odd_w_o,
               mlp_w1=m_mlp_w1, mlp_w2=m_mlp_w2)
    var = dict(meta=v_meta, norm_mix_g=v_norm_mix_g, norm_mlp_g=v_norm_mlp_g, even_w_in=v_even_w_in,
               even_ret_gn_g=v_even_ret_gn_g, even_conv_w=v_even_conv_w, even_conv_b=v_even_conv_b,
               even_conv_ln_g=v_even_conv_ln_g, even_conv_ln_b=v_even_conv_ln_b, even_w_out=v_even_w_out,
               odd_w_qkv=v_odd_w_qkv, odd_q_norm_g=v_odd_q_norm_g, odd_k_norm_g=v_odd_k_norm_g, odd_w_o=v_odd_w_o,
               mlp_w1=v_mlp_w1, mlp_w2=v_mlp_w2)
    me = 4 * lax.axis_index("x") + 2 * lax.axis_index("y") + lax.axis_index("c")

    small_in = jnp.concatenate([meta, jnp.pad(even_conv_w[0], ((0, 1), (0, 0))),
                                jnp.pad(even_ret_gn_g[0], ((0, 4), (0, 96)))], axis=0)
    b16 = lambda a: a.astype(BF16)
    later_src = dict(w_out=b16(even_w_out[0]), w1_0=b16(mlp_w1[0]), w2_0=b16(mlp_w2[0]),
                     w_qkv=b16(odd_w_qkv[0]), w_o=b16(odd_w_o[0]), w1_1=b16(mlp_w1[1]), w2_1=b16(mlp_w2[1]))
    landed = _gather_first([b16(even_w_in[0]), small_in], list(later_src.values()))
    g_in, g_small = landed[0], landed[1]
    own_slot = dict(zip(later_src, landed[2:]))
    groups = (("gather_l0", ("w_out", "w1_0", "w2_0")), ("gather_attn", ("w_qkv", "w_o")),
              ("gather_l1", ("w1_1", "w2_1")))
    pending = {}
    gather_tok = jnp.zeros((), F32)
    for gname, names in groups:
        ex = _exchange_start(gname, [later_src[n] for n in names], [own_slot[n] for n in names], "gather")
        gather_tok = gather_tok + ex["token"]
        for n in names:
            pending[n] = (ex, names)
    arrived = dict(w_in=g_in)

    def weight(name, after):
        if name not in arrived:
            ex, names = pending[name]
            arrived.update(zip(names, _exchange_wait(ex, after)[1]))
        return arrived[name]

    cols = lambda a: jnp.transpose(a, (1, 0, 2)).reshape(a.shape[1], -1)
    p = dict(norm_mix_g=norm_mix_g + gather_tok, norm_mlp_g=norm_mlp_g, conv_b=even_conv_b, ln_g=even_conv_ln_g,
             ln_b=even_conv_ln_b, qn_g=odd_q_norm_g, kn_g=odd_k_norm_g,
             gn_g=cols(g_small[:, 48:52, :32]),
             conv_w=jnp.pad(cols(g_small[:, 16:47]), ((0, 1), (0, 0))))
    meta_full = cols(g_small[:, 0:16])

    scatters = {}

    def emit(tag, grads):
        scatters[tag] = _scatter_start("scatter_" + tag, grads)
        return scatters[tag]["token"]

    h0 = jnp.concatenate([jnp.zeros((PAD_FRONT, D_MODEL), F32), meta_full, x[0]], axis=0)
    target = jnp.concatenate([jnp.zeros((TOK0, D_MODEL), F32), loss_target[0]], axis=0)
    grad_x, small_part = _local_step(h0, target, p, weight, emit)

    out = {}
    got = {}

    def update(names, terms, after):
        for tag in {t for name in names for t, _ in terms[name]} - set(got):
            got[tag] = _exchange_wait(scatters[tag], after)
        for name in names:
            owns, recvs = zip(*[(got[t][0][j], got[t][1][j]) for t, j in terms[name]])
            out[name] = _adamw("adamw_" + name, w[name], list(owns), list(recvs), mom[name], var[name], me)

    terms = dict(even_w_in=[("mix0", 0)], even_w_out=[("mix0_out", 0)], odd_w_qkv=[("attn", 0)], odd_w_o=[("attn", 1)],
                 mlp_w1=[("mlp0", 0), ("mlp1", 0)], mlp_w2=[("mlp0", 1), ("mlp1", 1)])
    small_ex = _exchange_start("small", [small_part], [lax.empty((N_DEV,) + small_part.shape, F32)], "gather")
    update(("mlp_w1", "mlp_w2", "odd_w_qkv", "odd_w_o", "even_w_out"), terms, grad_x)
    update(("even_w_in",), terms, out["even_w_out"][1])
    (own_part,), (slots,) = _exchange_wait(small_ex, out["even_w_in"][1])
    tot = _sum_small(lax.dynamic_update_slice(slots, own_part[None], (me, 0, 0)))
    loss = tot[ROW_LOSS, 0]

    shard_cols = lambda a, width: lax.dynamic_slice_in_dim(a, me * width, width, axis=1)
    one = lambda r: tot[r:r + 1]
    small_g = dict(
        norm_mix_g=tot[ROW_MIX:ROW_MIX + 2], norm_mlp_g=tot[ROW_MLP:ROW_MLP + 2],
        even_conv_b=one(ROW_CB), even_conv_ln_g=one(ROW_LG), even_conv_ln_b=one(ROW_LB),
        odd_q_norm_g=one(ROW_QN)[:, :64], odd_k_norm_g=one(ROW_KN)[:, :64],
        meta=shard_cols(tot[ROW_META:ROW_META + N_META], 128),
        even_conv_w=shard_cols(tot[ROW_CW:ROW_CW + CONV_WIDTH], 128)[None],
        even_ret_gn_g=shard_cols(tot[ROW_GN].reshape(4, 256), 32)[None])
    packs = {n: (_pack128(w[n]), _pack128(small_g[n]), _pack128(mom[n]), _pack128(var[n])) for n in _SMALL_NAMES}
    cat4 = [jnp.concatenate([packs[n][i] for n in _SMALL_NAMES], axis=0) for i in range(4)]
    d_s, m_s, v_s = _adamw_small(*cat4)
    r0 = 0
    for n in _SMALL_NAMES:
        rows = packs[n][0].shape[0]
        size = w[n].size
        take = lambda a: a[r0:r0 + rows].reshape(-1)[:size].reshape(w[n].shape)
        out[n] = (small_g[n].reshape(w[n].shape), take(d_s), take(m_s), take(v_s))
        r0 += rows

    res = [loss, grad_x[None]]
    for i in range(4):
        res.extend(out[n][i] for n in _ORDER)
    return tuple(res)
```

```python
import functools

import numpy as np
import jax
import jax.numpy as jnp
from jax import lax
from jax.experimental import pallas as pl
from jax.experimental.pallas import tpu as pltpu

F32 = jnp.float32
BF16 = jnp.bfloat16

D_MODEL = 1024
N_META = 16
CHUNK = 128
PAD_FRONT = 112
TOK0 = PAD_FRONT + N_META
EPS = 1e-6
N_DEV = 8
RET_HEADS = 4
RET_DECAY_OFFSET = 5.0
ROPE_BASE = 10000.0
CONV_WIDTH = 31
HALO = 32
SB_SCALE = 64 ** -0.5
RET_SCALE = 128 ** -0.5
ADAM_LR, ADAM_B1, ADAM_B2, ADAM_EPS, ADAM_WD, ADAM_STEP = 0.001, 0.9, 0.999, 1e-08, 0.01, 10
VMEM_LIMIT = 56 * 1024 * 1024
MESH = pl.DeviceIdType.MESH


def _pcall(body, **kw):
    return pl.pallas_call(body, **kw)


def _params(**kw):
    return pltpu.CompilerParams(vmem_limit_bytes=VMEM_LIMIT, **kw)


def _tile(n, cands):
    for c in cands:
        if n % c == 0:
            return c
    raise ValueError(f"no tile for {n} in {cands}")


def _sigmoid(x):
    return 1.0 / (1.0 + jnp.exp(-x))


_DIMS = {
    "nn": (((1,), (0,)), ((), ())),
    "nt": (((1,), (1,)), ((), ())),
    "tn": (((0,), (0,)), ((), ())),
}


def _matmul(name, a, b, *, grid, a_spec, b_spec, o_spec, out_shape, contract, acc_shape,
            epi="plain", extra=None, extra_spec=None):
    nk = grid[2]
    dims = _DIMS[contract]
    n_in = 3 if extra is not None else 2
    n_out = 2 if epi == "relu2" else 1

    def body(*refs):
        a_ref, b_ref = refs[0], refs[1]
        e_ref = refs[2] if extra is not None else None
        outs = refs[n_in:n_in + n_out]
        acc = refs[-1]
        k = pl.program_id(2)
        part = lax.dot_general(a_ref[...].astype(BF16), b_ref[...].astype(BF16), dims, preferred_element_type=F32)
        if nk > 1:
            @pl.when(k == 0)
            def _():
                acc[...] = jnp.zeros_like(acc)

            acc[...] += part

        @pl.when(k == nk - 1)
        def _():
            r = acc[...] if nk > 1 else part
            if epi == "plain":
                outs[0][...] = r.astype(outs[0].dtype)
            elif epi == "residual":
                outs[0][...] = (r + e_ref[...]).astype(outs[0].dtype)
            elif epi == "relu2":
                outs[0][...] = r
                rr = jnp.maximum(r, 0.0)
                outs[1][...] = (rr * rr).astype(BF16)
            elif epi == "drelu2":
                outs[0][...] = (r * (2.0 * jnp.maximum(e_ref[...], 0.0))).astype(outs[0].dtype)

    in_specs = [a_spec, b_spec] + ([extra_spec] if extra is not None else [])
    args = (a, b) + ((extra,) if extra is not None else ())
    if n_out == 2:
        out_specs = [o_spec, o_spec]
    else:
        out_specs = o_spec
    return _pcall(body, name=name, grid=grid, in_specs=in_specs, out_specs=out_specs,
                  out_shape=out_shape, scratch_shapes=[pltpu.VMEM(acc_shape, F32)],
                  compiler_params=_params(dimension_semantics=("parallel", "parallel", "arbitrary")))(*args)


def _tm_tall(t):
    return _tile(t, (2112, 768, 384, 128))


def _mm_cols(name, a, wb, lead, out_dtype=F32, epi="plain"):
    t, kdim = a.shape
    n = wb.shape[-1]
    tm, tk = _tm_tall(t), _tile(kdim, (1024, 512))
    nl = len(lead)
    b_spec = pl.BlockSpec((None,) * (1 + nl) + (tk, n), lambda i, j, k: (j,) + lead + (k, 0))
    o_spec = pl.BlockSpec((tm, n), lambda i, j, k: (i, j))
    if epi == "relu2":
        out_shape = [jax.ShapeDtypeStruct((t, N_DEV * n), F32), jax.ShapeDtypeStruct((t, N_DEV * n), BF16)]
    else:
        out_shape = jax.ShapeDtypeStruct((t, N_DEV * n), out_dtype)
    return _matmul(name, a, wb, grid=(t // tm, N_DEV, kdim // tk),
                   a_spec=pl.BlockSpec((tm, tk), lambda i, j, k: (i, k)), b_spec=b_spec, o_spec=o_spec,
                   out_shape=out_shape, contract="nn", acc_shape=(tm, n), epi=epi)


def _mm_cols_t_rms(name, a, wb, h, g, dres):
    t = a.shape[0]
    nb, kdim, n = wb.shape
    tm = _tile(t, (704, 384, 128))

    def body(a_ref, b_ref, h_ref, g_ref, r_ref, o_ref, dg_ref):
        @pl.when(pl.program_id(0) == 0)
        def _():
            dg_ref[...] = jnp.zeros_like(dg_ref)

        d = _dot(a_ref[:, 0:n].astype(BF16), b_ref[0], "nt")
        for j in range(1, nb):
            d = d + _dot(a_ref[:, j * n:(j + 1) * n].astype(BF16), b_ref[j], "nt")
        x = h_ref[...]
        rs = lax.rsqrt(jnp.mean(x * x, axis=-1, keepdims=True) + EPS)
        u = d * g_ref[...]
        m = jnp.mean(u * x, axis=-1, keepdims=True)
        o_ref[...] = r_ref[...] + rs * u - x * (rs * rs * rs * m)
        dg_ref[...] += jnp.sum(d * x * rs, axis=0, keepdims=True)

    row = pl.BlockSpec((tm, kdim), lambda i: (i, 0))
    vec = pl.BlockSpec((1, kdim), lambda i: (0, 0))
    return _pcall(body, name=name, grid=(t // tm,),
                  in_specs=[pl.BlockSpec((tm, nb * n), lambda i: (i, 0)),
                            pl.BlockSpec((nb, kdim, n), lambda i: (0, 0, 0)), row, vec, row],
                  out_specs=[row, vec],
                  out_shape=[jax.ShapeDtypeStruct((t, kdim), F32), jax.ShapeDtypeStruct((1, kdim), F32)],
                  compiler_params=_params(dimension_semantics=("arbitrary",)))(a, wb, h, g, dres)


def _mm_rows_t(name, a, wb, lead, out_dtype=F32, epi="plain", extra=None):
    t, n = a.shape
    r = wb.shape[-2]
    tm, tk = _tm_tall(t), _tile(n, (1024,))
    nl = len(lead)
    b_spec = pl.BlockSpec((None,) * (1 + nl) + (r, tk), lambda i, j, k: (j,) + lead + (0, k))
    o_spec = pl.BlockSpec((tm, r), lambda i, j, k: (i, j))
    return _matmul(name, a, wb, grid=(t // tm, N_DEV, n // tk),
                   a_spec=pl.BlockSpec((tm, tk), lambda i, j, k: (i, k)), b_spec=b_spec, o_spec=o_spec,
                   out_shape=jax.ShapeDtypeStruct((t, N_DEV * r), out_dtype), contract="nt",
                   acc_shape=(tm, r), epi=epi, extra=extra, extra_spec=o_spec if extra is not None else None)


def _mm_rows_loss(name, a, wb, residual, target):
    t = a.shape[0]
    nb, r, n = wb.shape
    tm = _tile(t, (704, 384, 128))

    def body(a_ref, b_ref, r_ref, t_ref, d_ref, l_ref):
        i = pl.program_id(0)

        @pl.when(i == 0)
        def _():
            l_ref[...] = jnp.zeros_like(l_ref)

        y = r_ref[...] + _dot(a_ref[...].astype(BF16), b_ref[...].reshape(nb * r, n))
        diff = jnp.where(_row_ids(i, tm) >= TOK0, y - t_ref[...], 0.0)
        d_ref[...] = diff * (1.0 / D_MODEL)
        l_ref[...] += jnp.sum(diff * diff) * (0.5 / D_MODEL)

    row = pl.BlockSpec((tm, n), lambda i: (i, 0))
    return _pcall(body, name=name, grid=(t // tm,),
                  in_specs=[pl.BlockSpec((tm, nb * r), lambda i: (i, 0)),
                            pl.BlockSpec((nb, r, n), lambda i: (0, 0, 0)), row, row],
                  out_specs=[row, pl.BlockSpec((8, 128), lambda i: (0, 0))],
                  out_shape=[jax.ShapeDtypeStruct((t, n), F32), jax.ShapeDtypeStruct((8, 128), F32)],
                  compiler_params=_params(dimension_semantics=("arbitrary",)))(a, wb, residual, target)


def _mm_rows_norm(name, a, wb, residual, g):
    t = a.shape[0]
    nb, r, n = wb.shape
    tm = _tile(t, (704, 384, 128))

    def body(a_ref, b_ref, r_ref, g_ref, h_ref, hn_ref):
        h = r_ref[...] + _dot(a_ref[...].astype(BF16), b_ref[...].reshape(nb * r, n))
        h_ref[...] = h
        hn_ref[...] = (h * lax.rsqrt(jnp.mean(h * h, axis=-1, keepdims=True) + EPS) * g_ref[...]).astype(BF16)

    row = pl.BlockSpec((tm, n), lambda i: (i, 0))
    return _pcall(body, name=name, grid=(t // tm,),
                  in_specs=[pl.BlockSpec((tm, nb * r), lambda i: (i, 0)), pl.BlockSpec((nb, r, n), lambda i: (0, 0, 0)),
                            row, pl.BlockSpec((1, n), lambda i: (0, 0))],
                  out_specs=[row, row],
                  out_shape=[jax.ShapeDtypeStruct((t, n), F32), jax.ShapeDtypeStruct((t, n), BF16)],
                  compiler_params=_params(dimension_semantics=("parallel",)))(a, wb, residual, g)


def _wgrad_cols(name, x, dy, n):
    t, kdim = x.shape
    tk = _tm_tall(t)
    return _matmul(name, x, dy, grid=(1, N_DEV, t // tk),
                   a_spec=pl.BlockSpec((tk, kdim), lambda i, j, k: (k, 0)),
                   b_spec=pl.BlockSpec((tk, n), lambda i, j, k: (k, j)),
                   o_spec=pl.BlockSpec((None, kdim, n), lambda i, j, k: (j, 0, 0)),
                   out_shape=jax.ShapeDtypeStruct((N_DEV, kdim, n), BF16), contract="tn", acc_shape=(kdim, n))


def _wgrad_rows(name, x, dy, r):
    t = x.shape[0]
    n = dy.shape[1]
    tk, tn = _tm_tall(t), _tile(n, (512,))
    tm = min(N_DEV * r, 1024)
    out = _matmul(name, x, dy, grid=(N_DEV * r // tm, n // tn, t // tk),
                  a_spec=pl.BlockSpec((tk, tm), lambda i, j, k: (k, i)),
                  b_spec=pl.BlockSpec((tk, tn), lambda i, j, k: (k, j)),
                  o_spec=pl.BlockSpec((tm, tn), lambda i, j, k: (i, j)),
                  out_shape=jax.ShapeDtypeStruct((N_DEV * r, n), BF16), contract="tn", acc_shape=(tm, tn))
    return out.reshape(N_DEV, r, n)


def _rows(t):
    return _tile(t, (384, 128))


def _rms_fwd(name, h, g):
    t = h.shape[0]
    tr = _rows(t)

    def body(h_ref, g_ref, o_ref):
        x = h_ref[...]
        r = lax.rsqrt(jnp.mean(x * x, axis=-1, keepdims=True) + EPS)
        o_ref[...] = (x * r * g_ref[...]).astype(BF16)

    row = pl.BlockSpec((tr, D_MODEL), lambda i: (i, 0))
    vec = pl.BlockSpec((1, D_MODEL), lambda i: (0, 0))
    return _pcall(body, name=name, grid=(t // tr,), in_specs=[row, vec], out_specs=row,
                  out_shape=jax.ShapeDtypeStruct((t, D_MODEL), BF16))(h, g)


def _ret_tables(t):
    hh = np.arange(RET_HEADS, dtype=np.float64)
    log_g = np.log1p(-np.exp2(-RET_DECAY_OFFSET - hh))
    idx = np.arange(CHUNK, dtype=np.float64)
    diff = idx[:, None] - idx[None, :]
    dmat = np.where(diff[None] >= 0, np.exp(np.maximum(diff, 0.0)[None] * log_g[:, None, None]), 0.0)
    qdec = np.exp((idx + 1.0)[None, :, None] * log_g[:, None, None]) * np.ones((1, 1, CHUNK))
    kdec = np.exp((CHUNK - 1 - idx)[None, :, None] * log_g[:, None, None]) * np.ones((1, 1, CHUNK))
    half = CHUNK // 2
    inv_freq = (ROPE_BASE ** (-np.arange(half, dtype=np.float32) / half)).astype(np.float32)
    ang = (np.arange(t, dtype=np.float32)[:, None] * inv_freq[None, :]).astype(np.float32).astype(np.float64)
    cos2 = np.concatenate([np.cos(ang), np.cos(ang)], axis=1)
    sin2 = np.concatenate([-np.sin(ang), np.sin(ang)], axis=1)
    return tuple(jnp.asarray(v, F32) for v in (dmat, qdec, kdec, cos2, sin2))


def _rot(x, c, s):
    return x * c + pltpu.roll(x, CHUNK // 2, 1) * s


def _unrot(dx, c, s):
    return dx * c + pltpu.roll(dx * s, CHUNK // 2, 1)


def _dot(a, b, contract="nn"):
    return lax.dot_general(a, b, _DIMS[contract], preferred_element_type=F32)


def _ret_fwd(proj, gn_g, tables):
    t = proj.shape[0]
    nch = t // CHUNK
    dmat, qdec, kdec, cos2, sin2 = tables

    def body(qk_ref, v_ref, g_ref, w_ref, c_ref, s_ref, dm_ref, qd_ref, kd_ref, o_ref, st_ref, cat_ref, state):
        @pl.when(pl.program_id(0) == 0)
        def _():
            state[...] = jnp.zeros_like(state)

        c, s = c_ref[...], s_ref[...]
        for h in range(RET_HEADS):
            q = _rot(qk_ref[:, 128 * h:128 * (h + 1)], c, s)
            k = _rot(qk_ref[:, 512 + 128 * h:512 + 128 * (h + 1)], c, s) * RET_SCALE
            vb = v_ref[:, 256 * h:256 * (h + 1)].astype(BF16)
            st = state[h]
            st_ref[h] = st
            sc = _dot(q.astype(BF16), k.astype(BF16), "nt") * dm_ref[h]
            o = _dot(sc.astype(BF16), vb)
            o += _dot((q * qd_ref[h]).astype(BF16), st.astype(BF16))
            sl = slice(256 * h, 256 * (h + 1))
            o_ref[:, sl] = o
            kv = _dot((k * kd_ref[h]).astype(BF16), vb, "tn")
            state[h] = qd_ref[h, CHUNK - 1:CHUNK, 0:1] * st + kv
            mu = jnp.mean(o, axis=-1, keepdims=True)
            oc = o - mu
            rstd = lax.rsqrt(jnp.mean(oc * oc, axis=-1, keepdims=True) + EPS)
            g = g_ref[:, sl]
            cat_ref[:, sl] = (g * _sigmoid(g) * (oc * rstd * w_ref[:, sl])).astype(BF16)

    tab = pl.BlockSpec((RET_HEADS, CHUNK, CHUNK), lambda n: (0, 0, 0))
    pos = pl.BlockSpec((CHUNK, CHUNK), lambda n: (n, 0))
    row = pl.BlockSpec((CHUNK, 1024), lambda n: (n, 0))
    return _pcall(
        body, name="ret_fwd", grid=(nch,),
        in_specs=[row, pl.BlockSpec((CHUNK, 1024), lambda n: (n, 1)), pl.BlockSpec((CHUNK, 1024), lambda n: (n, 2)),
                  pl.BlockSpec((1, 1024), lambda n: (0, 0)), pos, pos, tab, tab, tab],
        out_specs=[row, pl.BlockSpec((RET_HEADS, None, 128, 256), lambda n: (0, n, 0, 0)), row],
        out_shape=[jax.ShapeDtypeStruct((t, 1024), F32), jax.ShapeDtypeStruct((RET_HEADS, nch, 128, 256), F32),
                   jax.ShapeDtypeStruct((t, 2048), BF16)],
        scratch_shapes=[pltpu.VMEM((RET_HEADS, 128, 256), F32)],
        compiler_params=_params(dimension_semantics=("arbitrary",)))(
            proj, proj, proj, gn_g, cos2, sin2, dmat, qdec, kdec)


def _ret_bwd(dproj, proj, states, do, tables):
    t = proj.shape[0]
    nch = t // CHUNK
    dmat, qdec, kdec, cos2, sin2 = tables

    def body(dp_in, qk_ref, v_ref, do_ref, st_ref, c_ref, s_ref, dm_ref, qd_ref, kd_ref, dp_ref, rst):
        del dp_in
        @pl.when(pl.program_id(0) == 0)
        def _():
            rst[...] = jnp.zeros_like(rst)

        c, s = c_ref[...], s_ref[...]
        for h in range(RET_HEADS):
            q = _rot(qk_ref[:, 128 * h:128 * (h + 1)], c, s)
            k = _rot(qk_ref[:, 512 + 128 * h:512 + 128 * (h + 1)], c, s) * RET_SCALE
            qb, kb = q.astype(BF16), k.astype(BF16)
            vb = v_ref[:, 256 * h:256 * (h + 1)].astype(BF16)
            dob = do_ref[:, 256 * h:256 * (h + 1)].astype(BF16)
            pb = st_ref[h].astype(BF16)
            r = rst[h]
            rb = r.astype(BF16)
            dm, qd, kd = dm_ref[h], qd_ref[h], kd_ref[h]
            sb = (_dot(qb, kb, "nt") * dm).astype(BF16)
            dsb = (_dot(dob, vb, "nt") * dm).astype(BF16)
            dq = _dot(dsb, kb) + _dot(dob, pb, "nt") * qd
            dk = _dot(dsb, qb, "tn") + _dot(vb, rb, "nt") * kd
            dv = _dot(sb, dob, "tn") + _dot((k * kd).astype(BF16), rb)
            rst[h] = _dot((q * qd).astype(BF16), dob, "tn") + qd[CHUNK - 1:CHUNK, 0:1] * r
            dp_ref[:, 128 * h:128 * (h + 1)] = _unrot(dq, c, s).astype(BF16)
            dp_ref[:, 512 + 128 * h:512 + 128 * (h + 1)] = (_unrot(dk, c, s) * RET_SCALE).astype(BF16)
            dp_ref[:, 1024 + 256 * h:1024 + 256 * (h + 1)] = dv.astype(BF16)

    rev = lambda n: nch - 1 - n
    tab = pl.BlockSpec((RET_HEADS, CHUNK, CHUNK), lambda n: (0, 0, 0))
    pos = pl.BlockSpec((CHUNK, CHUNK), lambda n: (rev(n), 0))
    row = pl.BlockSpec((CHUNK, 1024), lambda n: (rev(n), 0))
    return _pcall(
        body, name="ret_bwd", grid=(nch,),
        in_specs=[pl.BlockSpec(memory_space=pl.ANY), row, pl.BlockSpec((CHUNK, 1024), lambda n: (rev(n), 1)), row,
                  pl.BlockSpec((RET_HEADS, None, 128, 256), lambda n: (0, rev(n), 0, 0)),
                  pos, pos, tab, tab, tab],
        out_specs=pl.BlockSpec((CHUNK, 2048), lambda n: (rev(n), 0)),
        out_shape=jax.ShapeDtypeStruct((t, 5120), BF16),
        scratch_shapes=[pltpu.VMEM((RET_HEADS, 128, 256), F32)], input_output_aliases={0: 0},
        compiler_params=_params(dimension_semantics=("arbitrary",)))(
            dproj, proj, proj, do, states, cos2, sin2, dmat, qdec, kdec)


def _row_ids(i, tr):
    return i * tr + lax.broadcasted_iota(jnp.int32, (tr, 1), 0)


SH_ROWS = HALO - 8
CONV_VPU_TAPS = 21


def _shifted_copies(xs, sh, tr):
    for b in range(1, 8):
        sh[b - 1] = xs[pl.ds(b, tr + SH_ROWS), :]


def _shifted(xs, sh, off, tr, lanes=slice(None)):
    a, b = divmod(off, 8)
    return xs[pl.ds(8 * a, tr), lanes] if b == 0 else sh[b - 1, pl.ds(8 * a, tr), lanes]


def _taps_mxu(xs, sh, w_ref, offs, tr, first=0):
    sub = lax.broadcasted_iota(jnp.int32, (256, 128), 0)
    eye = (sub & 127) == lax.broadcasted_iota(jnp.int32, (256, 128), 1)
    outs = []
    for c in range(8):
        lanes = slice(128 * c, 128 * (c + 1))
        acc = None
        for w in range(first, len(offs), 2):
            wb = min(w + 1, len(offs) - 1)
            w_hi = w_ref[w:w + 1, lanes]
            w_lo = w_ref[wb:wb + 1, lanes] if wb > w else jnp.zeros((1, 128), F32)
            dmat = jnp.where(eye, jnp.where(sub < 128, w_hi, w_lo), 0.0).astype(BF16)
            lhs = jnp.concatenate([_shifted(xs, sh, offs[w], tr, lanes).astype(BF16),
                                   _shifted(xs, sh, offs[wb], tr, lanes).astype(BF16)], axis=1)
            d = _dot(lhs, dmat)
            acc = d if acc is None else acc + d
        outs.append(acc)
    return jnp.concatenate(outs, axis=1)


def _conv_fwd(cat, proj, conv_w, conv_b, ln_g, ln_b):
    t = proj.shape[0]
    tr = _rows(t)
    hb = tr // HALO

    def body(cat_in, ua_ref, ug_ref, pa_ref, pg_ref, w_ref, b_ref, lg_ref, lb_ref, c_ref, hd_ref, y_ref, xs, sh):
        del cat_in
        i = pl.program_id(0)
        hdn = ua_ref[...] * _sigmoid(ug_ref[...])
        hd_ref[...] = hdn
        prev = pa_ref[...] * _sigmoid(pg_ref[...])
        xs[0:HALO, :] = jnp.where(i > 0, prev, 0.0)
        xs[HALO:HALO + tr, :] = hdn
        _shifted_copies(xs, sh, tr)
        offs = [HALO - (CONV_WIDTH - 1) + w for w in range(CONV_WIDTH)]
        acc = _taps_mxu(xs, sh, w_ref, offs, tr, first=CONV_VPU_TAPS) + b_ref[...]
        for w in range(CONV_VPU_TAPS):
            acc += w_ref[w:w + 1, :] * _shifted(xs, sh, offs[w], tr)
        y_ref[...] = acc
        mu = jnp.mean(acc, axis=-1, keepdims=True)
        yc = acc - mu
        rstd = lax.rsqrt(jnp.mean(yc * yc, axis=-1, keepdims=True) + EPS)
        yn = yc * rstd * lg_ref[...] + lb_ref[...]
        c = yn * _sigmoid(yn)
        c_ref[...] = jnp.where(_row_ids(i, tr) >= PAD_FRONT, c, 0.0).astype(BF16)

    row = pl.BlockSpec((tr, 1024), lambda i: (i, 0))
    vec = pl.BlockSpec((1, 1024), lambda i: (0, 0))
    halo = lambda col: pl.BlockSpec((HALO, 1024), lambda i: (jnp.maximum(i * hb - 1, 0), col))
    return _pcall(body, name="conv_fwd", grid=(t // tr,),
                  in_specs=[pl.BlockSpec(memory_space=pl.ANY),
                            pl.BlockSpec((tr, 1024), lambda i: (i, 3)), pl.BlockSpec((tr, 1024), lambda i: (i, 4)),
                            halo(3), halo(4), pl.BlockSpec((32, 1024), lambda i: (0, 0)), vec, vec, vec],
                  out_specs=[pl.BlockSpec((tr, 1024), lambda i: (i, 1)), row, row],
                  out_shape=[jax.ShapeDtypeStruct((t, 2048), BF16), jax.ShapeDtypeStruct((t, 1024), F32),
                             jax.ShapeDtypeStruct((t, 1024), F32)],
                  scratch_shapes=[pltpu.VMEM((tr + HALO, 1024), F32), pltpu.VMEM((7, tr + SH_ROWS, 1024), F32)],
                  input_output_aliases={0: 0}, compiler_params=_params())(
                      cat, proj, proj, proj, proj, conv_w, conv_b, ln_g, ln_b)


def _mix_bwd_head(dh, w_out, o, proj, gn_g, y, ln_g, ln_b):
    t = dh.shape[0]
    tr = _rows(t)
    nb, r, n = w_out.shape

    def body(dh_ref, b_ref, o_ref, g_ref, w_ref, y_ref, lg_ref, lb_ref,
             do_ref, dp_ref, dw_ref, dy_ref, dlg_ref, dlb_ref, dcb_ref):
        i = pl.program_id(0)

        @pl.when(i == 0)
        def _():
            for ref in (dw_ref, dlg_ref, dlb_ref, dcb_ref):
                ref[...] = jnp.zeros_like(ref)

        dcat = _dot(dh_ref[...].astype(BF16), b_ref[...].reshape(nb * r, n), "nt")
        for h in range(RET_HEADS):
            sl = slice(256 * h, 256 * (h + 1))
            x = o_ref[:, sl]
            mu = jnp.mean(x, axis=-1, keepdims=True)
            xc = x - mu
            rstd = lax.rsqrt(jnp.mean(xc * xc, axis=-1, keepdims=True) + EPS)
            xh = xc * rstd
            w = w_ref[:, sl]
            g = g_ref[:, sl]
            sg = _sigmoid(g)
            d = dcat[:, sl]
            don = d * (g * sg)
            dp_ref[:, sl] = (d * (xh * w) * (sg * (1.0 + g * (1.0 - sg)))).astype(BF16)
            dw_ref[:, sl] += jnp.sum(don * xh, axis=0, keepdims=True)
            dxh = don * w
            m1 = jnp.mean(dxh, axis=-1, keepdims=True)
            m2 = jnp.mean(dxh * xh, axis=-1, keepdims=True)
            do_ref[:, sl] = rstd * (dxh - m1 - xh * m2)
        yv = y_ref[...]
        mu = jnp.mean(yv, axis=-1, keepdims=True)
        yc = yv - mu
        rstd = lax.rsqrt(jnp.mean(yc * yc, axis=-1, keepdims=True) + EPS)
        xh = yc * rstd
        lg = lg_ref[...]
        yn = xh * lg + lb_ref[...]
        sg = _sigmoid(yn)
        dyn = jnp.where(_row_ids(i, tr) >= PAD_FRONT, dcat[:, 1024:] * (sg * (1.0 + yn * (1.0 - sg))), 0.0)
        dlg_ref[...] += jnp.sum(dyn * xh, axis=0, keepdims=True)
        dlb_ref[...] += jnp.sum(dyn, axis=0, keepdims=True)
        dxh = dyn * lg
        m1 = jnp.mean(dxh, axis=-1, keepdims=True)
        m2 = jnp.mean(dxh * xh, axis=-1, keepdims=True)
        dy = rstd * (dxh - m1 - xh * m2)
        dy_ref[...] = dy
        dcb_ref[...] += jnp.sum(dy, axis=0, keepdims=True)

    row = pl.BlockSpec((tr, 1024), lambda i: (i, 0))
    vec = pl.BlockSpec((1, 1024), lambda i: (0, 0))
    gate = pl.BlockSpec((tr, 1024), lambda i: (i, 2))
    vsh = jax.ShapeDtypeStruct((1, 1024), F32)
    fsh = jax.ShapeDtypeStruct((t, 1024), F32)
    return _pcall(body, name="mix_bwd_head", grid=(t // tr,),
                  in_specs=[row, pl.BlockSpec((nb, r, n), lambda i: (0, 0, 0)), row, gate, vec, row, vec, vec],
                  out_specs=[row, gate, vec, row, vec, vec, vec],
                  out_shape=[fsh, jax.ShapeDtypeStruct((t, 5120), BF16), vsh, fsh, vsh, vsh, vsh],
                  compiler_params=_params(dimension_semantics=("arbitrary",)))(
                      dh, w_out, o, proj, gn_g, y, ln_g, ln_b)


def _conv_bwd_taps(dproj, dy, hdn, proj, conv_w):
    t = dy.shape[0]
    tr = _rows(t)
    hb = tr // HALO
    nt = t // tr

    def body(dp_in, dy_ref, nx_ref, hd_ref, ph_ref, ua_ref, ug_ref, w_ref, da_ref, dg_ref, dw_ref, xs, sh):
        del dp_in
        i = pl.program_id(0)

        @pl.when(i == 0)
        def _():
            dw_ref[...] = jnp.zeros_like(dw_ref)

        dy = dy_ref[...]
        xs[0:tr, :] = dy
        xs[tr:tr + HALO, :] = jnp.where(i < nt - 1, nx_ref[...], 0.0)
        _shifted_copies(xs, sh, tr)
        dh = _taps_mxu(xs, sh, w_ref, [CONV_WIDTH - 1 - w for w in range(CONV_WIDTH)], tr)
        xs[0:HALO, :] = jnp.where(i > 0, ph_ref[...], 0.0)
        xs[HALO:HALO + tr, :] = hd_ref[...]
        _shifted_copies(xs, sh, tr)
        for w in range(CONV_WIDTH):
            dw_ref[w:w + 1, :] += jnp.sum(dy * _shifted(xs, sh, HALO - (CONV_WIDTH - 1) + w, tr), axis=0, keepdims=True)
        dh = jnp.where(_row_ids(i, tr) >= PAD_FRONT, dh, 0.0)
        sg = _sigmoid(ug_ref[...])
        da_ref[...] = (dh * sg).astype(BF16)
        dg_ref[...] = (dh * ua_ref[...] * sg * (1.0 - sg)).astype(BF16)

    row = pl.BlockSpec((tr, 1024), lambda i: (i, 0))
    return _pcall(body, name="conv_bwd_taps", grid=(nt,),
                  in_specs=[pl.BlockSpec(memory_space=pl.ANY),
                            row, pl.BlockSpec((HALO, 1024), lambda i: (jnp.minimum((i + 1) * hb, nt * hb - 1), 0)),
                            row, pl.BlockSpec((HALO, 1024), lambda i: (jnp.maximum(i * hb - 1, 0), 0)),
                            pl.BlockSpec((tr, 1024), lambda i: (i, 3)), pl.BlockSpec((tr, 1024), lambda i: (i, 4)),
                            pl.BlockSpec((32, 1024), lambda i: (0, 0))],
                  out_specs=[pl.BlockSpec((tr, 1024), lambda i: (i, 3)), row, pl.BlockSpec((32, 1024), lambda i: (0, 0))],
                  out_shape=[jax.ShapeDtypeStruct((t, 5120), BF16), jax.ShapeDtypeStruct((t, 1024), BF16),
                             jax.ShapeDtypeStruct((32, 1024), F32)],
                  scratch_shapes=[pltpu.VMEM((tr + HALO, 1024), F32), pltpu.VMEM((7, tr + SH_ROWS, 1024), F32)],
                  input_output_aliases={0: 0}, compiler_params=_params())(
                      dproj, dy, dy, hdn, hdn, proj, proj, conv_w)


NEG_BIG = -1e30


def _seg_tables(qb):
    j = np.arange(128)
    bd = (j[:, None] // 64 == j[None, :] // 64).astype(np.float32)
    ones = np.ones((128, 128), np.float32)
    later = np.concatenate([(j[:, None] >= j[None, :]).astype(np.float32), ones], axis=1)
    earlier = np.concatenate([(j[:, None] < j[None, :]).astype(np.float32), ones], axis=1)
    per = qb // CHUNK
    row = np.arange(qb)[:, None]
    pad = np.broadcast_to(j[None, :] < PAD_FRONT, (qb, 128))
    diag = [(g * CHUNK + j[None, :]) >= row for g in range(per)]
    masks = diag + [np.zeros((qb, 128), bool), pad, diag[0] | pad]
    bias = np.stack([np.where(m, NEG_BIG, 0.0) for m in masks]).astype(np.float32)
    dup = lambda m: np.concatenate([m, m], axis=0)
    return (jnp.asarray(bd, BF16), jnp.asarray(dup(later), BF16), jnp.asarray(dup(earlier), BF16),
            jnp.asarray(bias, F32))


def _split_dot(x, m):
    hi = x.astype(BF16)
    lo = (x - hi.astype(F32)).astype(BF16)
    return _dot(hi, m) + _dot(lo, m)


def _qk_norm_fwd(qkv, qg, kg, bd):
    t = qkv.shape[0]
    tr = _rows(t)
    nb = tr // CHUNK

    def body(q_ref, k_ref, v_ref, qg_ref, kg_ref, bd_ref, qo, kt, vt):
        bdm = bd_ref[...]
        sub = lax.broadcasted_iota(jnp.int32, (128, 1), 0)

        def pair_layout(x, t_ref, hp, b):
            xt = x.T
            t_ref[hp, b] = jnp.concatenate([jnp.where(sub < 64, xt, 0.0), jnp.where(sub >= 64, xt, 0.0)],
                                           axis=1).astype(BF16)

        for hp in range(8):
            sl = slice(128 * hp, 128 * (hp + 1))
            x = q_ref[:, sl]
            r = lax.rsqrt(_split_dot(x * x, bdm) * (1.0 / 64) + EPS)
            qo[:, sl] = (x * r * (qg_ref[:, sl] * SB_SCALE)).astype(BF16)
            x = k_ref[:, sl]
            r = lax.rsqrt(_split_dot(x * x, bdm) * (1.0 / 64) + EPS)
            kn = x * r * kg_ref[:, sl]
            v = v_ref[:, sl]
            for b in range(nb):
                rows = slice(CHUNK * b, CHUNK * (b + 1))
                pair_layout(kn[rows], kt, hp, b)
                pair_layout(v[rows], vt, hp, b)

    col = lambda c: pl.BlockSpec((tr, 1024), lambda i: (i, c))
    vec = pl.BlockSpec((1, 1024), lambda i: (0, 0))
    wide = pl.BlockSpec((8, nb, 128, 256), lambda i: (0, i, 0, 0))
    wsh = jax.ShapeDtypeStruct((8, t // CHUNK, 128, 256), BF16)
    return _pcall(body, name="qk_norm_fwd", grid=(t // tr,),
                  in_specs=[col(0), col(1), col(2), vec, vec, pl.BlockSpec((128, 128), lambda i: (0, 0))],
                  out_specs=[col(0), wide, wide],
                  out_shape=[jax.ShapeDtypeStruct((t, 1024), BF16), wsh, wsh])(qkv, qkv, qkv, qg, kg, bd)


def _qk_norm_bwd(qkv, dq, dk, dv, qg, kg, bd):
    t = qkv.shape[0]
    tr = _rows(t)

    def body(q_ref, k_ref, dq_ref, dk_ref, dv_ref, qg_ref, kg_ref, bd_ref, o_ref, dqg_ref, dkg_ref):
        @pl.when(pl.program_id(0) == 0)
        def _():
            dqg_ref[...] = jnp.zeros_like(dqg_ref)
            dkg_ref[...] = jnp.zeros_like(dkg_ref)

        bdm = bd_ref[...]
        for part, (src, d_ref, g_ref, dg_ref) in enumerate(((q_ref, dq_ref, qg_ref, dqg_ref),
                                                           (k_ref, dk_ref, kg_ref, dkg_ref))):
            for cix in range(8):
                sl = slice(128 * cix, 128 * (cix + 1))
                x = src[:, sl]
                d = d_ref[:, sl]
                r = lax.rsqrt(_split_dot(x * x, bdm) * (1.0 / 64) + EPS)
                u = d * g_ref[:, sl]
                m = _split_dot(u * x, bdm) * (1.0 / 64)
                o_ref[:, 1024 * part + 128 * cix:1024 * part + 128 * (cix + 1)] = (r * u - x * (r * r * r * m)).astype(BF16)
                dg_ref[:, sl] += jnp.sum(d * x * r, axis=0, keepdims=True)
        o_ref[:, 2048:3072] = dv_ref[...].astype(BF16)

    col = lambda c: pl.BlockSpec((tr, 1024), lambda i: (i, c))
    vec = pl.BlockSpec((1, 1024), lambda i: (0, 0))
    vsh = jax.ShapeDtypeStruct((1, 1024), F32)
    return _pcall(body, name="qk_norm_bwd", grid=(t // tr,),
                  in_specs=[col(0), col(1), col(0), col(0), col(0), vec, vec, pl.BlockSpec((128, 128), lambda i: (0, 0))],
                  out_specs=[pl.BlockSpec((tr, 3072), lambda i: (i, 0)), vec, vec],
                  out_shape=[jax.ShapeDtypeStruct((t, 3072), BF16), vsh, vsh])(qkv, qkv, dq, dk, dv, qg, kg, bd)


def _split2(x):
    hi = x.astype(BF16)
    lo = (x - hi.astype(F32)).astype(BF16)
    return jnp.concatenate([hi, lo], axis=1)


def _sb_sums(z, later_tab):
    sp = jnp.maximum(z, 0.0) + jnp.log(1.0 + jnp.exp(-jnp.abs(z)))
    return _dot(_split2(sp), later_tab)


def _sb_bias_index(i, kb, per):
    g = kb - i * per
    return jnp.where(kb == 0, jnp.where(i == 0, per + 2, per + 1), jnp.where(g >= 0, g, per))


def _sb_qb(t):
    return _tile(t, (384, 128))


def _sb_fwd(qh, kt, vt, later_tab, bias_tab):
    t = qh.shape[0]
    qb = _sb_qb(t)
    per = qb // CHUNK
    nkb_all = t // CHUNK

    nq = t // qb

    def body(q_ref, kt_ref, vt_ref, tab_ref, bias_ref, o_ref, ws_ref, acc, carry, zbuf, wbuf, wsem):
        h, i = pl.program_id(0), pl.program_id(1)
        n = h * nq + i
        p = n & 1
        q = q_ref[...]
        acc[...] = jnp.zeros_like(acc)
        carry[...] = jnp.zeros_like(carry)
        nkb = (i + 1) * per
        save = lambda kb: pltpu.make_async_copy(wbuf.at[p, kb], ws_ref.at[h, i, kb], wsem.at[p, kb])

        def drain(step, par):
            hs, is_ = step // nq, step % nq

            def one(kb, _):
                pltpu.make_async_copy(wbuf.at[par, kb], ws_ref.at[hs, is_, kb], wsem.at[par, kb]).wait()
                return 0

            lax.fori_loop(0, (is_ + 1) * per, one, 0)

        @pl.when(n >= 2)
        def _():
            drain(n - 2, p)

        for u in range(per):
            zbuf[u] = _dot(q, kt_ref[nkb - 1 - u])

        def trip(s, diagonal):
            top = nkb - 1 - per * s
            if not diagonal:
                for u in range(per):
                    save(top + per - u).start()
            z2s = [zbuf[u] for u in range(per)]
            for u in range(per):
                zbuf[u] = _dot(q, kt_ref[jnp.maximum(top - per - u, 0)])
            first = [CHUNK * (per - 1 - u) if diagonal else 0 for u in range(per)]
            cins = [carry[0], carry[1]]
            zs, cus = [], []
            for u in range(per):
                zs.append([z2s[u][first[u]:, 128 * hh:128 * (hh + 1)] for hh in range(2)])
                if diagonal or u == per - 1:
                    bias = bias_ref[_sb_bias_index(i, top - u, per)][first[u]:]
                    zs[u] = [z + bias for z in zs[u]]
                cus.append([_sb_sums(z, tab_ref[...]) for z in zs[u]])
            part = None
            for u in range(per):
                kb, lo = top - u, first[u]
                for hh in range(2):
                    sl = slice(128 * hh, 128 * (hh + 1))
                    cu = cus[u][hh]
                    wbuf[p, kb, lo:, sl] = jnp.exp(zs[u][hh] - cu[:, :128] - cins[hh][lo:]).astype(BF16)
                    if lo:
                        wbuf[p, kb, :lo, sl] = jnp.zeros((lo, 128), BF16)
                        cins[hh] = jnp.concatenate([cins[hh][:lo], cins[hh][lo:] + cu[:, 128:]], axis=0)
                    else:
                        cins[hh] = cins[hh] + cu[:, 128:]
                d = _dot(wbuf[p, kb], vt_ref[kb], "nt")
                part = d if part is None else part + d
            carry[0], carry[1] = cins[0], cins[1]
            acc[...] += part

        trip(0, True)

        def step(s, _):
            trip(s, False)
            return 0

        lax.fori_loop(1, nkb // per, step, 0)
        for u in range(per):
            save(per - 1 - u).start()
        o_ref[...] = acc[...]

        @pl.when(n == 8 * nq - 1)
        def _():
            drain(n - 1, 1 - p)
            drain(n, p)

    blk = pl.BlockSpec((qb, 128), lambda h, i: (i, h))
    wide = pl.BlockSpec((None, nkb_all, 128, 256), lambda h, i: (h, 0, 0, 0))
    return _pcall(body, name="sb_fwd", grid=(8, t // qb),
                  in_specs=[blk, wide, wide, pl.BlockSpec((256, 256), lambda h, i: (0, 0)),
                            pl.BlockSpec((per + 3, qb, 128), lambda h, i: (0, 0, 0))],
                  out_specs=[blk, pl.BlockSpec(memory_space=pl.ANY)],
                  out_shape=[jax.ShapeDtypeStruct((t, 1024), F32),
                             jax.ShapeDtypeStruct((8, t // qb, nkb_all, qb, 256), BF16)],
                  scratch_shapes=[pltpu.VMEM((qb, 128), F32), pltpu.VMEM((2, qb, 128), F32),
                                  pltpu.VMEM((per, qb, 256), F32), pltpu.VMEM((2, nkb_all, qb, 256), BF16),
                                  pltpu.SemaphoreType.DMA((2, nkb_all))],
                  compiler_params=_params(dimension_semantics=("arbitrary", "arbitrary")))(
                      qh, kt, vt, later_tab, bias_tab)


def _sb_bwd(qh, kt, vt, wsave, do, earlier_tab, bias_tab):
    t = qh.shape[0]
    qb = _sb_qb(t)
    per = qb // CHUNK
    nkb_all = t // CHUNK

    zero_slot = nkb_all
    nq = t // qb

    def body(q_ref, kt_ref, vt_ref, ws_ref, do_ref, etab_ref, bias_ref,
             dq_ref, dk_ref, dv_ref, acc, gcarry, zbuf, dwbuf, wbuf, wsem, dzbuf):
        h, i = pl.program_id(0), pl.program_id(1)
        n = h * nq + i
        p = n & 1

        @pl.when(i == 0)
        def _():
            dk_ref[...] = jnp.zeros_like(dk_ref)
            dv_ref[...] = jnp.zeros_like(dv_ref)

        nkb = (i + 1) * per
        fetch = lambda kb: pltpu.make_async_copy(ws_ref.at[h, i, kb], wbuf.at[p, kb], wsem.at[p, kb])

        def prefetch(step, par):
            hs, is_ = step // nq, step % nq

            def one(kb, _):
                pltpu.make_async_copy(ws_ref.at[hs, is_, kb], wbuf.at[par, kb], wsem.at[par, kb]).start()
                return 0

            lax.fori_loop(0, (is_ + 1) * per, one, 0)

        @pl.when(n == 0)
        def _():
            prefetch(n, p)

        @pl.when(n + 1 < 8 * nq)
        def _():
            prefetch(n + 1, 1 - p)

        q = q_ref[...]
        dob = do_ref[...].astype(BF16)
        acc[...] = jnp.zeros_like(acc)
        gcarry[...] = jnp.zeros_like(gcarry)
        zbuf[...] = _dot(q, kt_ref[0])
        dwbuf[...] = _dot(dob, vt_ref[0])
        dzbuf[...] = jnp.zeros_like(dzbuf)
        wbuf[p, zero_slot] = jnp.zeros((qb, 256), BF16)

        q_t = q.astype(F32).T.astype(BF16)
        do_t = do_ref[...].T.astype(BF16)
        sub = lax.broadcasted_iota(jnp.int32, (128, 1), 0)

        def gradients(slot, kb):
            dz2 = dzbuf[...]
            acc[...] += _dot(dz2, kt_ref[kb], "nt")
            dk2 = _dot(q_t, dz2)
            dv2 = _dot(do_t, wbuf[p, slot])
            dk_ref[kb] += jnp.where(sub < 64, dk2[:, :128], dk2[:, 128:])
            dv_ref[kb] += jnp.where(sub < 64, dv2[:, :128], dv2[:, 128:])

        def trip(kb, lo):
            fetch(kb).wait()
            bias = bias_ref[_sb_bias_index(i, kb, per)][lo:]
            z2 = zbuf[...]
            dw2 = dwbuf[...]
            nxt = jnp.minimum(kb + 1, nkb - 1)
            zbuf[...] = _dot(q, kt_ref[nxt])
            dwbuf[...] = _dot(dob, vt_ref[nxt])
            gradients(jnp.where(kb == 0, zero_slot, kb - 1), jnp.maximum(kb - 1, 0))
            w2 = wbuf[p, kb]
            for hh in range(2):
                sl = slice(128 * hh, 128 * (hh + 1))
                z = z2[lo:, sl] + bias
                e = jnp.exp(-jnp.abs(z))
                r = 1.0 / (1.0 + e)
                sig = jnp.where(z >= 0, r, e * r)
                gw = w2[lo:, sl].astype(F32) * dw2[lo:, sl]
                cu2 = _dot(_split2(gw), etab_ref[...])
                gin = gcarry[hh, lo:, :]
                gcarry[hh, lo:, :] = gin + cu2[:, 128:]
                dzbuf[lo:, sl] = (gw - sig * (gw + cu2[:, :128] + gin)).astype(BF16)
                if lo:
                    dzbuf[:lo, sl] = jnp.zeros((lo, 128), BF16)

        def step(kb, _):
            trip(kb, 0)
            return 0

        lax.fori_loop(0, nkb - per, step, 0)
        for g in range(per):
            trip(nkb - per + g, CHUNK * g)
        gradients(nkb - 1, nkb - 1)
        dq_ref[...] = acc[...] * SB_SCALE

        @pl.when(i == nq - 1)
        def _():
            def untranspose(kb, _):
                dk_ref[kb] = dk_ref[kb].T
                dv_ref[kb] = dv_ref[kb].T
                return 0

            lax.fori_loop(0, nkb_all, untranspose, 0)

    blk = pl.BlockSpec((qb, 128), lambda h, i: (i, h))
    wide = pl.BlockSpec((None, nkb_all, 128, 256), lambda h, i: (h, 0, 0, 0))
    tab = pl.BlockSpec((256, 256), lambda h, i: (0, 0))
    kv_out = pl.BlockSpec((nkb_all, 128, 128), lambda h, i: (0, 0, h))
    ksh = jax.ShapeDtypeStruct((nkb_all, 128, 1024), F32)
    dq, dk, dv = _pcall(
        body, name="sb_bwd", grid=(8, t // qb),
        in_specs=[blk, wide, wide, pl.BlockSpec(memory_space=pl.ANY), blk, tab,
                  pl.BlockSpec((per + 3, qb, 128), lambda h, i: (0, 0, 0))],
        out_specs=[blk, kv_out, kv_out], out_shape=[jax.ShapeDtypeStruct((t, 1024), F32), ksh, ksh],
        scratch_shapes=[pltpu.VMEM((qb, 128), F32), pltpu.VMEM((2, qb, 128), F32),
                        pltpu.VMEM((qb, 256), F32), pltpu.VMEM((qb, 256), F32),
                        pltpu.VMEM((2, nkb_all + 1, qb, 256), BF16), pltpu.SemaphoreType.DMA((2, nkb_all)),
                        pltpu.VMEM((qb, 256), BF16)],
        compiler_params=_params(dimension_semantics=("arbitrary", "arbitrary")))(
            qh, kt, vt, wsave, do, earlier_tab, bias_tab)
    return dq, dk.reshape(t, 1024), dv.reshape(t, 1024)


def _adamw_math(w, g, m, v):
    m = ADAM_B1 * m + (1.0 - ADAM_B1) * g
    v = ADAM_B2 * v + (1.0 - ADAM_B2) * (g * g)
    m_hat = m / (1.0 - ADAM_B1 ** ADAM_STEP)
    v_hat = v / (1.0 - ADAM_B2 ** ADAM_STEP)
    delta = -ADAM_LR * (m_hat / (jnp.sqrt(v_hat) + ADAM_EPS) + ADAM_WD * w)
    return delta, m, v


def _adamw(name, w, owns, recvs, m, v, me):
    shape = w.shape
    c = shape[-1]
    nl = len(owns)
    w3, m3, v3 = (a.reshape(nl, -1, c) for a in (w, m, v))
    r = w3.shape[1]
    tr = _tile(r, (256, 128))
    owns = [o.reshape(N_DEV, r, c) for o in owns]
    recvs = [p.reshape(N_DEV - 1, r, c) for p in recvs]

    def body(me_ref, w_ref, *rest):
        own_refs, recv_refs = rest[:nl], rest[nl:2 * nl]
        m_ref, v_ref = rest[2 * nl:2 * nl + 2]
        g_out, d_out, m_out, v_out = rest[2 * nl + 2:]
        layer = pl.program_id(0)

        def grad(k):
            g = own_refs[k][...].astype(F32)
            for s in range(N_DEV - 1):
                g = g + recv_refs[k][s].astype(F32)
            return g

        g = grad(0)
        for k in range(1, nl):
            g = jnp.where(layer == k, grad(k), g)
        d, mn, vn = _adamw_math(w_ref[...], g, m_ref[...], v_ref[...])
        g_out[...] = g
        d_out[...] = d
        m_out[...] = mn
        v_out[...] = vn

    row = pl.BlockSpec((None, tr, c), lambda l, i, me_ref: (l, i, 0))
    own = lambda k: pl.BlockSpec((None, tr, c), lambda l, i, me_ref: (me_ref[0], jnp.where(l == k, i, 0), 0))
    rcv = lambda k: pl.BlockSpec((N_DEV - 1, tr, c), lambda l, i, me_ref: (0, jnp.where(l == k, i, 0), 0))
    osh = jax.ShapeDtypeStruct((nl, r, c), F32)
    grid_spec = pltpu.PrefetchScalarGridSpec(
        num_scalar_prefetch=1, grid=(nl, r // tr),
        in_specs=[row] + [own(k) for k in range(nl)] + [rcv(k) for k in range(nl)] + [row, row],
        out_specs=[row, row, row, row])
    outs = _pcall(body, name=name, grid_spec=grid_spec, out_shape=[osh, osh, osh, osh])(
        me.reshape(1), w3, *owns, *recvs, m3, v3)
    return tuple(o.reshape(shape) for o in outs)


def _place():
    x, y, c = lax.axis_index("x"), lax.axis_index("y"), lax.axis_index("c")
    return x, y, c, 4 * x + 2 * y + c


def _peer(x, y, c, rel):
    return (x ^ ((rel >> 2) & 1), y ^ ((rel >> 1) & 1), c ^ (rel & 1))


def _gather_first(now, later):
    n, k = len(now), len(later)

    def body(*refs):
        ins, outs = refs[:n + k], refs[n + k:2 * (n + k)]
        send, recv, lsem = refs[2 * (n + k):]
        x, y, c, me = _place()
        locals_ = []
        for w in range(n + k):
            local = pltpu.make_async_copy(ins[w], outs[w].at[me], lsem.at[w])
            local.start()
            locals_.append(local)
        def copy(w, src, slot, rel, to_rel):
            return pltpu.make_async_remote_copy(src_ref=src, dst_ref=outs[w].at[slot], send_sem=send.at[w, rel - 1],
                                                recv_sem=recv.at[w, rel - 1], device_id=_peer(x, y, c, to_rel),
                                                device_id_type=MESH)

        for w in range(n):
            for rel in (1, 2, 4, 6):
                copy(w, ins[w], me, rel, rel).start()
        for w in range(n):
            for rel in (2, 4, 6):
                copy(w, ins[w], me ^ rel, rel, rel).wait_recv()
                copy(w, outs[w].at[me ^ rel], me ^ rel, rel | 1, 1).start()
        for w in range(n):
            for rel in (1, 3, 5, 7):
                copy(w, ins[w], me ^ rel, rel, 1).wait_recv()
            for rel in range(1, N_DEV):
                copy(w, ins[w], me, rel, rel).wait_send()
        for local in locals_:
            local.wait()

    hbm = pl.BlockSpec(memory_space=pl.ANY)
    vmem = pl.BlockSpec(memory_space=pltpu.VMEM)
    arrays = list(now) + list(later)
    return _pcall(body, name="gather_first", in_specs=[vmem] * (n + k), out_specs=[hbm] * (n + k),
                  out_shape=[jax.ShapeDtypeStruct((N_DEV,) + a.shape, a.dtype) for a in arrays],
                  scratch_shapes=[pltpu.SemaphoreType.DMA((n, N_DEV - 1)), pltpu.SemaphoreType.DMA((n, N_DEV - 1)),
                                  pltpu.SemaphoreType.DMA((n + k,))],
                  compiler_params=_params(has_side_effects=True))(*arrays)


_HBM = pl.BlockSpec(memory_space=pltpu.HBM)
_SEM = pl.BlockSpec(memory_space=pltpu.SEMAPHORE)
_DATAFLOW = pltpu.SideEffectType.DATAFLOW_SIDE_EFFECTING


def _exchange_refs(srcs, lands, mode, me, rel, j):
    if mode == "gather":
        return srcs[j], lands[j].at[me], lands[j].at[me ^ rel]
    return srcs[j].at[me ^ rel], lands[j].at[rel - 1], lands[j].at[rel - 1]


def _exchange_start(name, srcs, lands, mode):
    n = len(srcs)

    def body(*refs):
        ins, lnd = refs[:n], refs[n:2 * n]
        send, recv = refs[2 * n], refs[2 * n + 1]
        token = refs[-1]
        x, y, c, me = _place()
        for j in range(n):
            for rel in range(1, N_DEV):
                src, dst, _ = _exchange_refs(ins, lnd, mode, me, rel, j)
                pltpu.make_async_remote_copy(src_ref=src, dst_ref=dst, send_sem=send.at[j * (N_DEV - 1) + rel - 1],
                                             recv_sem=recv.at[j * (N_DEV - 1) + rel - 1],
                                             device_id=_peer(x, y, c, rel), device_id_type=MESH).start()
        token[...] = jnp.zeros_like(token)

    sems = pltpu.SemaphoreType.DMA((n * (N_DEV - 1),))
    hbm_like = lambda a: pltpu.HBM(a.shape, a.dtype)
    outs = _pcall(body, name=name + "_start",
                  in_specs=[_HBM] * (2 * n), out_specs=[_SEM, _SEM] + [_HBM] * (2 * n) + [pl.BlockSpec(memory_space=pltpu.VMEM)],
                  out_shape=[sems, sems] + [hbm_like(a) for a in srcs] + [hbm_like(a) for a in lands]
                  + [jax.ShapeDtypeStruct((8, 128), F32)],
                  input_output_aliases={i: 2 + i for i in range(2 * n)},
                  compiler_params=pltpu.CompilerParams(has_side_effects=_DATAFLOW))(
                      *[pltpu.with_memory_space_constraint(a, pltpu.HBM) for a in list(srcs) + list(lands)])
    return dict(name=name, mode=mode, n=n, send=outs[0], recv=outs[1], srcs=outs[2:2 + n], lands=outs[2 + n:2 + 2 * n],
                token=outs[-1][0, 0])


def _exchange_wait(ex, after):
    n, mode = ex["n"], ex["mode"]

    def body(*refs):
        ins, lnd = refs[:n], refs[n:2 * n]
        send, recv = refs[2 * n], refs[2 * n + 1]
        x, y, c, me = _place()
        for j in range(n):
            for rel in range(1, N_DEV):
                src, dst, landed = _exchange_refs(ins, lnd, mode, me, rel, j)
                pltpu.make_async_remote_copy(src_ref=src, dst_ref=dst, send_sem=send.at[j * (N_DEV - 1) + rel - 1],
                                             recv_sem=recv.at[j * (N_DEV - 1) + rel - 1],
                                             device_id=_peer(x, y, c, rel), device_id_type=MESH).wait_send()
                pltpu.make_async_remote_copy(src_ref=src, dst_ref=landed, send_sem=send.at[j * (N_DEV - 1) + rel - 1],
                                             recv_sem=recv.at[j * (N_DEV - 1) + rel - 1],
                                             device_id=_peer(x, y, c, rel), device_id_type=MESH).wait_recv()

    hbm_like = lambda a: pltpu.HBM(a.shape, a.dtype)
    arrays = list(ex["srcs"]) + list(ex["lands"])
    outs = _pcall(body, name=ex["name"] + "_wait",
                  in_specs=[_HBM] * (2 * n) + [_SEM, _SEM, pl.BlockSpec(memory_space=pl.ANY)],
                  out_specs=[_HBM] * (2 * n), out_shape=[hbm_like(a) for a in arrays],
                  input_output_aliases={i: i for i in range(2 * n)},
                  compiler_params=pltpu.CompilerParams(has_side_effects=_DATAFLOW))(
                      *arrays, ex["send"], ex["recv"], after)
    return outs[:n], outs[n:]


def _scatter_start(name, grads):
    lands = [lax.empty((N_DEV - 1,) + g.shape[1:], g.dtype) for g in grads]
    return _exchange_start(name, grads, lands, "scatter")


ROW_MIX, ROW_MLP, ROW_CB, ROW_LG, ROW_LB, ROW_QN, ROW_KN, ROW_LOSS = 0, 2, 4, 5, 6, 7, 8, 9
ROW_META, ROW_CW, ROW_GN, SMALL_ROWS = 16, 32, 64, 72


def _sum_small(slots):
    def body(s_ref, o_ref):
        tot = s_ref[0]
        for s in range(1, N_DEV):
            tot = tot + s_ref[s]
        o_ref[...] = tot
        for row in (ROW_QN, ROW_KN):
            v = tot[row:row + 1, :]
            f = v[:, 0:128]
            for k in range(1, 8):
                f = f + v[:, 128 * k:128 * (k + 1)]
            o_ref[row:row + 1, 0:64] = f[:, 0:64] + f[:, 64:128]

    return _pcall(body, name="sum_small", out_shape=jax.ShapeDtypeStruct(slots.shape[1:], F32))(slots)


def _adamw_small(w, g, m, v):
    def body(w_ref, g_ref, m_ref, v_ref, d_out, m_out, v_out):
        d, mn, vn = _adamw_math(w_ref[...], g_ref[...], m_ref[...], v_ref[...])
        d_out[...] = d
        m_out[...] = mn
        v_out[...] = vn

    osh = jax.ShapeDtypeStruct(w.shape, F32)
    return _pcall(body, name="adamw_small", out_shape=[osh, osh, osh])(w, g, m, v)


def _local_step(h0, target, p, weight, emit):
    t = h0.shape[0]
    tables = _ret_tables(t)
    bd, later_tab, earlier_tab, bias_tab = _seg_tables(_sb_qb(t))
    row = lambda a, i: a[i:i + 1]

    hn_a = _rms_fwd("rms_mix0", h0, row(p["norm_mix_g"], 0))
    w_in = weight("w_in", hn_a)
    proj = _mm_cols("proj_in", hn_a, w_in, ())
    gn_flat = p["gn_g"].reshape(1, 1024)
    o_ret, states, cat = _ret_fwd(proj, gn_flat, tables)
    cat, hdn, ycv = _conv_fwd(cat, proj, p["conv_w"], p["conv_b"], p["ln_g"], p["ln_b"])
    w_out = weight("w_out", cat)
    h1, hn_b = _mm_rows_norm("mix_out", cat, w_out, h0, row(p["norm_mlp_g"], 0))
    w1_0, w2_0 = weight("w1_0", hn_b), weight("w2_0", hn_b)
    a0, s0 = _mm_cols("mlp0_up", hn_b, w1_0, (), epi="relu2")
    h2, hn_c = _mm_rows_norm("mlp0_down", s0, w2_0, h1, row(p["norm_mix_g"], 1))

    w_qkv = weight("w_qkv", hn_c)
    qkv = _mm_cols("qkv", hn_c, w_qkv, ())
    qg = jnp.tile(p["qn_g"], (1, 16))
    kg = jnp.tile(p["kn_g"], (1, 16))
    qh, kt, vt = _qk_norm_fwd(qkv, qg, kg, bd)
    o_sb, w_sb = _sb_fwd(qh, kt, vt, later_tab, bias_tab)
    w_o = weight("w_o", o_sb)
    h3, hn_d = _mm_rows_norm("attn_out", o_sb, w_o, h2, row(p["norm_mlp_g"], 1))
    w1_1, w2_1 = weight("w1_1", hn_d), weight("w2_1", hn_d)
    a1, s1 = _mm_cols("mlp1_up", hn_d, w1_1, (), epi="relu2")
    dh, loss = _mm_rows_loss("mlp1_down", s1, w2_1, h3, target)

    def mlp_bwd(tag, layer, w1, w2, dh, h_in, hn, a, s):
        da = _mm_rows_t(f"{tag}_dact", dh, w2, (), out_dtype=BF16, epi="drelu2", extra=a)
        dw2 = _wgrad_rows(f"{tag}_dw2", s, dh, 512)
        dw1 = _wgrad_cols(f"{tag}_dw1", hn, da, 512)
        tok = emit(tag, [dw1, dw2])
        return _mm_cols_t_rms(f"{tag}_dhn", da, w1, h_in, row(p["norm_mlp_g"], layer) + tok, dh)

    dh, dg_mlp1 = mlp_bwd("mlp1", 1, w1_1, w2_1, dh, h3, hn_d, a1, s1)

    do_sb = _mm_rows_t("attn_dout", dh, w_o, ())
    dw_o = _wgrad_rows("attn_dwo", o_sb, dh, 128)
    dq, dk, dv = _sb_bwd(qh, kt, vt, w_sb, do_sb, earlier_tab, bias_tab)
    dqkv, dqg, dkg = _qk_norm_bwd(qkv, dq, dk, dv, qg, kg, bd)
    dw_qkv = _wgrad_cols("qkv_dw", hn_c, dqkv, 384)
    tok = emit("attn", [dw_qkv, dw_o])
    dh, dg_mix1 = _mm_cols_t_rms("qkv_dhn", dqkv, w_qkv, h2, row(p["norm_mix_g"], 1) + tok, dh)

    dh, dg_mlp0 = mlp_bwd("mlp0", 0, w1_0, w2_0, dh, h1, hn_b, a0, s0)

    dw_out = _wgrad_rows("mix_dwout", cat, dh, 256)
    tok = emit("mix0_out", [dw_out])
    do_ret, dproj, dgn, dy, dlg, dlb, dcb = _mix_bwd_head(dh, w_out, o_ret, proj, gn_flat + tok, ycv,
                                                          p["ln_g"], p["ln_b"])
    dproj = _ret_bwd(dproj, proj, states, do_ret, tables)
    dproj, dug, dcw = _conv_bwd_taps(dproj, dy, hdn, proj, p["conv_w"])
    dproj = lax.dynamic_update_slice(dproj, dug, (0, 4096))
    dw_in = _wgrad_cols("proj_dw", hn_a, dproj, 640)
    tok = emit("mix0", [dw_in])
    dh, dg_mix0 = _mm_cols_t_rms("proj_dhn", dproj, w_in, h0, row(p["norm_mix_g"], 0) + tok, dh)

    rid = lax.broadcasted_iota(jnp.int32, (16, 1), 0)
    loss_row = jnp.broadcast_to(loss[0:1, 0:1], (1, D_MODEL))
    vecs = sum(jnp.where(rid == k, v, 0.0)
               for k, v in enumerate((dg_mix0, dg_mix1, dg_mlp0, dg_mlp1, dcb, dlg, dlb, dqg, dkg, loss_row)))
    small = jnp.concatenate([vecs, dh[PAD_FRONT:TOK0], dcw, jnp.where(rid[:8] == 0, dgn, 0.0)], axis=0)
    return dh[TOK0:], small


_SMALL_NAMES = ("meta", "norm_mix_g", "norm_mlp_g", "even_ret_gn_g", "even_conv_w", "even_conv_b",
                "even_conv_ln_g", "even_conv_ln_b", "odd_q_norm_g", "odd_k_norm_g")
_BIG_NAMES = ("even_w_in", "even_w_out", "odd_w_qkv", "odd_w_o", "mlp_w1", "mlp_w2")
_ORDER = ("meta", "norm_mix_g", "norm_mlp_g", "even_w_in", "even_ret_gn_g", "even_conv_w", "even_conv_b",
          "even_conv_ln_g", "even_conv_ln_b", "even_w_out", "odd_w_qkv", "odd_q_norm_g", "odd_k_norm_g",
          "odd_w_o", "mlp_w1", "mlp_w2")


def _pack128(a):
    flat = a.reshape(-1)
    n = flat.shape[0]
    rows = -(-n // 128)
    rows8 = -(-rows // 8) * 8
    return jnp.pad(flat, (0, rows8 * 128 - n)).reshape(rows8, 128)


def kernel(x, meta, norm_mix_g, norm_mlp_g, even_w_in, even_ret_gn_g, even_conv_w, even_conv_b, even_conv_ln_g, even_conv_ln_b, even_w_out, odd_w_qkv, odd_q_norm_g, odd_k_norm_g, odd_w_o, mlp_w1, mlp_w2, loss_target, m_meta, m_norm_mix_g, m_norm_mlp_g, m_even_w_in, m_even_ret_gn_g, m_even_conv_w, m_even_conv_b, m_even_conv_ln_g, m_even_conv_ln_b, m_even_w_out, m_odd_w_qkv, m_odd_q_norm_g, m_odd_k_norm_g, m_odd_w_o, m_mlp_w1, m_mlp_w2, v_meta, v_norm_mix_g, v_norm_mlp_g, v_even_w_in, v_even_ret_gn_g, v_even_conv_w, v_even_conv_b, v_even_conv_ln_g, v_even_conv_ln_b, v_even_w_out, v_odd_w_qkv, v_odd_q_norm_g, v_odd_k_norm_g, v_odd_w_o, v_mlp_w1, v_mlp_w2):
    w = dict(meta=meta, norm_mix_g=norm_mix_g, norm_mlp_g=norm_mlp_g, even_w_in=even_w_in,
             even_ret_gn_g=even_ret_gn_g, even_conv_w=even_conv_w, even_conv_b=even_conv_b,
             even_conv_ln_g=even_conv_ln_g, even_conv_ln_b=even_conv_ln_b, even_w_out=even_w_out,
             odd_w_qkv=odd_w_qkv, odd_q_norm_g=odd_q_norm_g, odd_k_norm_g=odd_k_norm_g, odd_w_o=odd_w_o,
             mlp_w1=mlp_w1, mlp_w2=mlp_w2)
    mom = dict(meta=m_meta, norm_mix_g=m_norm_mix_g, norm_mlp_g=m_norm_mlp_g, even_w_in=m_even_w_in,
               even_ret_gn_g=m_even_ret_gn_g, even_conv_w=m_even_conv_w, even_conv_b=m_even_conv_b,
               even_conv_ln_g=m_even_conv_ln_g, even_conv_ln_b=m_even_conv_ln_b, even_w_out=m_even_w_out,
               odd_w_qkv=m_odd_w_qkv, odd_q_norm_g=m_odd_q_norm_g, odd_k_norm_g=m_odd_k_norm_g, odd_w_o=m_odd_w_o,
               mlp_w1=m_mlp_w1, mlp_w2=m_mlp_w2)
    var = dict(meta=v_meta, norm_mix_g=v_norm_mix_g, norm_mlp_g=v_norm_mlp_g, even_w_in=v_even_w_in,
               even_ret_gn_g=v_even_ret_gn_g, even_conv_w=v_even_conv_w, even_conv_b=v_even_conv_b,
               even_conv_ln_g=v_even_conv_ln_g, even_conv_ln_b=v_even_conv_ln_b, even_w_out=v_even_w_out,
               odd_w_qkv=v_odd_w_qkv, odd_q_norm_g=v_odd_q_norm_g, odd_k_norm_g=v_odd_k_norm_g, odd_w_o=v_odd_w_o,
               mlp_w1=v_mlp_w1, mlp_w2=v_mlp_w2)
    me = 4 * lax.axis_index("x") + 2 * lax.axis_index("y") + lax.axis_index("c")

    small_in = jnp.concatenate([meta, jnp.pad(even_conv_w[0], ((0, 1), (0, 0))),
                                jnp.pad(even_ret_gn_g[0], ((0, 4), (0, 96)))], axis=0)
    b16 = lambda a: a.astype(BF16)
    later_src = dict(w_out=b16(even_w_out[0]), w1_0=b16(mlp_w1[0]), w2_0=b16(mlp_w2[0]),
                     w_qkv=b16(odd_w_qkv[0]), w_o=b16(odd_w_o[0]), w1_1=b16(mlp_w1[1]), w2_1=b16(mlp_w2[1]))
    landed = _gather_first([b16(even_w_in[0]), small_in], list(later_src.values()))
    g_in, g_small = landed[0], landed[1]
    own_slot = dict(zip(later_src, landed[2:]))
    groups = (("gather_l0", ("w_out", "w1_0", "w2_0")), ("gather_attn", ("w_qkv", "w_o")),
              ("gather_l1", ("w1_1", "w2_1")))
    pending = {}
    gather_tok = jnp.zeros((), F32)
    for gname, names in groups:
        ex = _exchange_start(gname, [later_src[n] for n in names], [own_slot[n] for n in names], "gather")
        gather_tok = gather_tok + ex["token"]
        for n in names:
            pending[n] = (ex, names)
    arrived = dict(w_in=g_in)

    def weight(name, after):
        if name not in arrived:
            ex, names = pending[name]
            arrived.update(zip(names, _exchange_wait(ex, after)[1]))
        return arrived[name]

    cols = lambda a: jnp.transpose(a, (1, 0, 2)).reshape(a.shape[1], -1)
    p = dict(norm_mix_g=norm_mix_g + gather_tok, norm_mlp_g=norm_mlp_g, conv_b=even_conv_b, ln_g=even_conv_ln_g,
             ln_b=even_conv_ln_b, qn_g=odd_q_norm_g, kn_g=odd_k_norm_g,
             gn_g=cols(g_small[:, 48:52, :32]),
             conv_w=jnp.pad(cols(g_small[:, 16:47]), ((0, 1), (0, 0))))
    meta_full = cols(g_small[:, 0:16])

    scatters = {}

    def emit(tag, grads):
        scatters[tag] = _scatter_start("scatter_" + tag, grads)
        return scatters[tag]["token"]

    h0 = jnp.concatenate([jnp.zeros((PAD_FRONT, D_MODEL), F32), meta_full, x[0]], axis=0)
    target = jnp.concatenate([jnp.zeros((TOK0, D_MODEL), F32), loss_target[0]], axis=0)
    grad_x, small_part = _local_step(h0, target, p, weight, emit)

    out = {}
    got = {}

    def update(names, terms, after):
        for tag in {t for name in names for t, _ in terms[name]} - set(got):
            got[tag] = _exchange_wait(scatters[tag], after)
        for name in names:
            owns, recvs = zip(*[(got[t][0][j], got[t][1][j]) for t, j in terms[name]])
            out[name] = _adamw("adamw_" + name, w[name], list(owns), list(recvs), mom[name], var[name], me)

    terms = dict(even_w_in=[("mix0", 0)], even_w_out=[("mix0_out", 0)], odd_w_qkv=[("attn", 0)], odd_w_o=[("attn", 1)],
                 mlp_w1=[("mlp0", 0), ("mlp1", 0)], mlp_w2=[("mlp0", 1), ("mlp1", 1)])
    small_ex = _exchange_start("small", [small_part], [lax.empty((N_DEV,) + small_part.shape, F32)], "gather")
    update(("mlp_w1", "mlp_w2", "odd_w_qkv", "odd_w_o", "even_w_out"), terms, grad_x)
    update(("even_w_in",), terms, out["even_w_out"][1])
    (own_part,), (slots,) = _exchange_wait(small_ex, out["even_w_in"][1])
    tot = _sum_small(lax.dynamic_update_slice(slots, own_part[None], (me, 0, 0)))
    loss = tot[ROW_LOSS, 0]

    shard_cols = lambda a, width: lax.dynamic_slice_in_dim(a, me * width, width, axis=1)
    one = lambda r: tot[r:r + 1]
    small_g = dict(
        norm_mix_g=tot[ROW_MIX:ROW_MIX + 2], norm_mlp_g=tot[ROW_MLP:ROW_MLP + 2],
        even_conv_b=one(ROW_CB), even_conv_ln_g=one(ROW_LG), even_conv_ln_b=one(ROW_LB),
        odd_q_norm_g=one(ROW_QN)[:, :64], odd_k_norm_g=one(ROW_KN)[:, :64],
        meta=shard_cols(tot[ROW_META:ROW_META + N_META], 128),
        even_conv_w=shard_cols(tot[ROW_CW:ROW_CW + CONV_WIDTH], 128)[None],
        even_ret_gn_g=shard_cols(tot[ROW_GN].reshape(4, 256), 32)[None])
    packs = {n: (_pack128(w[n]), _pack128(small_g[n]), _pack128(mom[n]), _pack128(var[n])) for n in _SMALL_NAMES}
    cat4 = [jnp.concatenate([packs[n][i] for n in _SMALL_NAMES], axis=0) for i in range(4)]
    d_s, m_s, v_s = _adamw_small(*cat4)
    r0 = 0
    for n in _SMALL_NAMES:
        rows = packs[n][0].shape[0]
        size = w[n].size
        take = lambda a: a[r0:r0 + rows].reshape(-1)[:size].reshape(w[n].shape)
        out[n] = (small_g[n].reshape(w[n].shape), take(d_s), take(m_s), take(v_s))
        r0 += rows

    res = [loss, grad_x[None]]
    for i in range(4):
        res.extend(out[n][i] for n in _ORDER)
    return tuple(res)
```

```python
import functools

import numpy as np
import jax
import jax.numpy as jnp
from jax import lax
from jax.experimental import pallas as pl
from jax.experimental.pallas import tpu as pltpu

F32 = jnp.float32
BF16 = jnp.bfloat16

D_MODEL = 1024
N_META = 16
CHUNK = 128
PAD_FRONT = 112
TOK0 = PAD_FRONT + N_META
EPS = 1e-6
N_DEV = 8
RET_HEADS = 4
RET_DECAY_OFFSET = 5.0
ROPE_BASE = 10000.0
CONV_WIDTH = 31
HALO = 32
SB_SCALE = 64 ** -0.5
RET_SCALE = 128 ** -0.5
ADAM_LR, ADAM_B1, ADAM_B2, ADAM_EPS, ADAM_WD, ADAM_STEP = 0.001, 0.9, 0.999, 1e-08, 0.01, 10
VMEM_LIMIT = 56 * 1024 * 1024
MESH = pl.DeviceIdType.MESH


def _pcall(body, **kw):
    return pl.pallas_call(body, **kw)


def _params(**kw):
    return pltpu.CompilerParams(vmem_limit_bytes=VMEM_LIMIT, **kw)


def _tile(n, cands):
    for c in cands:
        if n % c == 0:
            return c
    raise ValueError(f"no tile for {n} in {cands}")


def _sigmoid(x):
    return 1.0 / (1.0 + jnp.exp(-x))


_DIMS = {
    "nn": (((1,), (0,)), ((), ())),
    "nt": (((1,), (1,)), ((), ())),
    "tn": (((0,), (0,)), ((), ())),
}


def _matmul(name, a, b, *, grid, a_spec, b_spec, o_spec, out_shape, contract, acc_shape,
            epi="plain", extra=None, extra_spec=None):
    nk = grid[2]
    dims = _DIMS[contract]
    n_in = 3 if extra is not None else 2
    n_out = 2 if epi == "relu2" else 1

    def body(*refs):
        a_ref, b_ref = refs[0], refs[1]
        e_ref = refs[2] if extra is not None else None
        outs = refs[n_in:n_in + n_out]
        acc = refs[-1]
        k = pl.program_id(2)
        part = lax.dot_general(a_ref[...].astype(BF16), b_ref[...].astype(BF16), dims, preferred_element_type=F32)
        if nk > 1:
            @pl.when(k == 0)
            def _():
                acc[...] = jnp.zeros_like(acc)

            acc[...] += part

        @pl.when(k == nk - 1)
        def _():
            r = acc[...] if nk > 1 else part
            if epi == "plain":
                outs[0][...] = r.astype(outs[0].dtype)
            elif epi == "residual":
                outs[0][...] = (r + e_ref[...]).astype(outs[0].dtype)
            elif epi == "relu2":
                outs[0][...] = r
                rr = jnp.maximum(r, 0.0)
                outs[1][...] = (rr * rr).astype(BF16)
            elif epi == "drelu2":
                outs[0][...] = (r * (2.0 * jnp.maximum(e_ref[...], 0.0))).astype(outs[0].dtype)

    in_specs = [a_spec, b_spec] + ([extra_spec] if extra is not None else [])
    args = (a, b) + ((extra,) if extra is not None else ())
    if n_out == 2:
        out_specs = [o_spec, o_spec]
    else:
        out_specs = o_spec
    return _pcall(body, name=name, grid=grid, in_specs=in_specs, out_specs=out_specs,
                  out_shape=out_shape, scratch_shapes=[pltpu.VMEM(acc_shape, F32)],
                  compiler_params=_params(dimension_semantics=("parallel", "parallel", "arbitrary")))(*args)


def _tm_tall(t):
    return _tile(t, (2112, 768, 384, 128))


def _mm_cols(name, a, wb, lead, out_dtype=F32, epi="plain"):
    t, kdim = a.shape
    n = wb.shape[-1]
    tm, tk = _tm_tall(t), _tile(kdim, (1024, 512))
    nl = len(lead)
    b_spec = pl.BlockSpec((None,) * (1 + nl) + (tk, n), lambda i, j, k: (j,) + lead + (k, 0))
    o_spec = pl.BlockSpec((tm, n), lambda i, j, k: (i, j))
    if epi == "relu2":
        out_shape = [jax.ShapeDtypeStruct((t, N_DEV * n), F32), jax.ShapeDtypeStruct((t, N_DEV * n), BF16)]
    else:
        out_shape = jax.ShapeDtypeStruct((t, N_DEV * n), out_dtype)
    return _matmul(name, a, wb, grid=(t // tm, N_DEV, kdim // tk),
                   a_spec=pl.BlockSpec((tm, tk), lambda i, j, k: (i, k)), b_spec=b_spec, o_spec=o_spec,
                   out_shape=out_shape, contract="nn", acc_shape=(tm, n), epi=epi)


def _mm_cols_t_rms(name, a, wb, h, g, dres):
    t = a.shape[0]
    nb, kdim, n = wb.shape
    tm = _tile(t, (704, 384, 128))

    def body(a_ref, b_ref, h_ref, g_ref, r_ref, o_ref, dg_ref):
        @pl.when(pl.program_id(0) == 0)
        def _():
            dg_ref[...] = jnp.zeros_like(dg_ref)

        d = _dot(a_ref[:, 0:n].astype(BF16), b_ref[0], "nt")
        for j in range(1, nb):
            d = d + _dot(a_ref[:, j * n:(j + 1) * n].astype(BF16), b_ref[j], "nt")
        x = h_ref[...]
        rs = lax.rsqrt(jnp.mean(x * x, axis=-1, keepdims=True) + EPS)
        u = d * g_ref[...]
        m = jnp.mean(u * x, axis=-1, keepdims=True)
        o_ref[...] = r_ref[...] + rs * u - x * (rs * rs * rs * m)
        dg_ref[...] += jnp.sum(d * x * rs, axis=0, keepdims=True)

    row = pl.BlockSpec((tm, kdim), lambda i: (i, 0))
    vec = pl.BlockSpec((1, kdim), lambda i: (0, 0))
    return _pcall(body, name=name, grid=(t // tm,),
                  in_specs=[pl.BlockSpec((tm, nb * n), lambda i: (i, 0)),
                            pl.BlockSpec((nb, kdim, n), lambda i: (0, 0, 0)), row, vec, row],
                  out_specs=[row, vec],
                  out_shape=[jax.ShapeDtypeStruct((t, kdim), F32), jax.ShapeDtypeStruct((1, kdim), F32)],
                  compiler_params=_params(dimension_semantics=("arbitrary",)))(a, wb, h, g, dres)


def _mm_rows_t(name, a, wb, lead, out_dtype=F32, epi="plain", extra=None):
    t, n = a.shape
    r = wb.shape[-2]
    tm, tk = _tm_tall(t), _tile(n, (1024,))
    nl = len(lead)
    b_spec = pl.BlockSpec((None,) * (1 + nl) + (r, tk), lambda i, j, k: (j,) + lead + (0, k))
    o_spec = pl.BlockSpec((tm, r), lambda i, j, k: (i, j))
    return _matmul(name, a, wb, grid=(t // tm, N_DEV, n // tk),
                   a_spec=pl.BlockSpec((tm, tk), lambda i, j, k: (i, k)), b_spec=b_spec, o_spec=o_spec,
                   out_shape=jax.ShapeDtypeStruct((t, N_DEV * r), out_dtype), contract="nt",
                   acc_shape=(tm, r), epi=epi, extra=extra, extra_spec=o_spec if extra is not None else None)


def _mm_rows_loss(name, a, wb, residual, target):
    t = a.shape[0]
    nb, r, n = wb.shape
    tm = _tile(t, (704, 384, 128))

    def body(a_ref, b_ref, r_ref, t_ref, d_ref, l_ref):
        i = pl.program_id(0)

        @pl.when(i == 0)
        def _():
            l_ref[...] = jnp.zeros_like(l_ref)

        y = r_ref[...] + _dot(a_ref[...].astype(BF16), b_ref[...].reshape(nb * r, n))
        diff = jnp.where(_row_ids(i, tm) >= TOK0, y - t_ref[...], 0.0)
        d_ref[...] = diff * (1.0 / D_MODEL)
        l_ref[...] += jnp.sum(diff * diff) * (0.5 / D_MODEL)

    row = pl.BlockSpec((tm, n), lambda i: (i, 0))
    return _pcall(body, name=name, grid=(t // tm,),
                  in_specs=[pl.BlockSpec((tm, nb * r), lambda i: (i, 0)),
                            pl.BlockSpec((nb, r, n), lambda i: (0, 0, 0)), row, row],
                  out_specs=[row, pl.BlockSpec((8, 128), lambda i: (0, 0))],
                  out_shape=[jax.ShapeDtypeStruct((t, n), F32), jax.ShapeDtypeStruct((8, 128), F32)],
                  compiler_params=_params(dimension_semantics=("arbitrary",)))(a, wb, residual, target)


def _mm_rows_norm(name, a, wb, residual, g):
    t = a.shape[0]
    nb, r, n = wb.shape
    tm = _tile(t, (704, 384, 128))

    def body(a_ref, b_ref, r_ref, g_ref, h_ref, hn_ref):
        h = r_ref[...] + _dot(a_ref[...].astype(BF16), b_ref[...].reshape(nb * r, n))
        h_ref[...] = h
        hn_ref[...] = (h * lax.rsqrt(jnp.mean(h * h, axis=-1, keepdims=True) + EPS) * g_ref[...]).astype(BF16)

    row = pl.BlockSpec((tm, n), lambda i: (i, 0))
    return _pcall(body, name=name, grid=(t // tm,),
                  in_specs=[pl.BlockSpec((tm, nb * r), lambda i: (i, 0)), pl.BlockSpec((nb, r, n), lambda i: (0, 0, 0)),
                            row, pl.BlockSpec((1, n), lambda i: (0, 0))],
                  out_specs=[row, row],
                  out_shape=[jax.ShapeDtypeStruct((t, n), F32), jax.ShapeDtypeStruct((t, n), BF16)],
                  compiler_params=_params(dimension_semantics=("parallel",)))(a, wb, residual, g)


def _wgrad_cols(name, x, dy, n):
    t, kdim = x.shape
    tk = _tm_tall(t)
    return _matmul(name, x, dy, grid=(1, N_DEV, t // tk),
                   a_spec=pl.BlockSpec((tk, kdim), lambda i, j, k: (k, 0)),
                   b_spec=pl.BlockSpec((tk, n), lambda i, j, k: (k, j)),
                   o_spec=pl.BlockSpec((None, kdim, n), lambda i, j, k: (j, 0, 0)),
                   out_shape=jax.ShapeDtypeStruct((N_DEV, kdim, n), BF16), contract="tn", acc_shape=(kdim, n))


def _wgrad_rows(name, x, dy, r):
    t = x.shape[0]
    n = dy.shape[1]
    tk, tn = _tm_tall(t), _tile(n, (512,))
    tm = min(N_DEV * r, 1024)
    out = _matmul(name, x, dy, grid=(N_DEV * r // tm, n // tn, t // tk),
                  a_spec=pl.BlockSpec((tk, tm), lambda i, j, k: (k, i)),
                  b_spec=pl.BlockSpec((tk, tn), lambda i, j, k: (k, j)),
                  o_spec=pl.BlockSpec((tm, tn), lambda i, j, k: (i, j)),
                  out_shape=jax.ShapeDtypeStruct((N_DEV * r, n), BF16), contract="tn", acc_shape=(tm, tn))
    return out.reshape(N_DEV, r, n)


def _rows(t):
    return _tile(t, (384, 128))


def _rms_fwd(name, h, g):
    t = h.shape[0]
    tr = _rows(t)

    def body(h_ref, g_ref, o_ref):
        x = h_ref[...]
        r = lax.rsqrt(jnp.mean(x * x, axis=-1, keepdims=True) + EPS)
        o_ref[...] = (x * r * g_ref[...]).astype(BF16)

    row = pl.BlockSpec((tr, D_MODEL), lambda i: (i, 0))
    vec = pl.BlockSpec((1, D_MODEL), lambda i: (0, 0))
    return _pcall(body, name=name, grid=(t // tr,), in_specs=[row, vec], out_specs=row,
                  out_shape=jax.ShapeDtypeStruct((t, D_MODEL), BF16))(h, g)


def _ret_tables(t):
    hh = np.arange(RET_HEADS, dtype=np.float64)
    log_g = np.log1p(-np.exp2(-RET_DECAY_OFFSET - hh))
    idx = np.arange(CHUNK, dtype=np.float64)
    diff = idx[:, None] - idx[None, :]
    dmat = np.where(diff[None] >= 0, np.exp(np.maximum(diff, 0.0)[None] * log_g[:, None, None]), 0.0)
    qdec = np.exp((idx + 1.0)[None, :, None] * log_g[:, None, None]) * np.ones((1, 1, CHUNK))
    kdec = np.exp((CHUNK - 1 - idx)[None, :, None] * log_g[:, None, None]) * np.ones((1, 1, CHUNK))
    half = CHUNK // 2
    inv_freq = (ROPE_BASE ** (-np.arange(half, dtype=np.float32) / half)).astype(np.float32)
    ang = (np.arange(t, dtype=np.float32)[:, None] * inv_freq[None, :]).astype(np.float32).astype(np.float64)
    cos2 = np.concatenate([np.cos(ang), np.cos(ang)], axis=1)
    sin2 = np.concatenate([-np.sin(ang), np.sin(ang)], axis=1)
    return tuple(jnp.asarray(v, F32) for v in (dmat, qdec, kdec, cos2, sin2))


def _rot(x, c, s):
    return x * c + pltpu.roll(x, CHUNK // 2, 1) * s


def _unrot(dx, c, s):
    return dx * c + pltpu.roll(dx * s, CHUNK // 2, 1)


def _dot(a, b, contract="nn"):
    return lax.dot_general(a, b, _DIMS[contract], preferred_element_type=F32)


def _ret_fwd(proj, gn_g, tables):
    t = proj.shape[0]
    nch = t // CHUNK
    dmat, qdec, kdec, cos2, sin2 = tables

    def body(qk_ref, v_ref, g_ref, w_ref, c_ref, s_ref, dm_ref, qd_ref, kd_ref, o_ref, st_ref, cat_ref, state):
        @pl.when(pl.program_id(0) == 0)
        def _():
            state[...] = jnp.zeros_like(state)

        c, s = c_ref[...], s_ref[...]
        for h in range(RET_HEADS):
            q = _rot(qk_ref[:, 128 * h:128 * (h + 1)], c, s)
            k = _rot(qk_ref[:, 512 + 128 * h:512 + 128 * (h + 1)], c, s) * RET_SCALE
            vb = v_ref[:, 256 * h:256 * (h + 1)].astype(BF16)
            st = state[h]
            st_ref[h] = st
            sc = _dot(q.astype(BF16), k.astype(BF16), "nt") * dm_ref[h]
            o = _dot(sc.astype(BF16), vb)
            o += _dot((q * qd_ref[h]).astype(BF16), st.astype(BF16))
            sl = slice(256 * h, 256 * (h + 1))
            o_ref[:, sl] = o
            kv = _dot((k * kd_ref[h]).astype(BF16), vb, "tn")
            state[h] = qd_ref[h, CHUNK - 1:CHUNK, 0:1] * st + kv
            mu = jnp.mean(o, axis=-1, keepdims=True)
            oc = o - mu
            rstd = lax.rsqrt(jnp.mean(oc * oc, axis=-1, keepdims=True) + EPS)
            g = g_ref[:, sl]
            cat_ref[:, sl] = (g * _sigmoid(g) * (oc * rstd * w_ref[:, sl])).astype(BF16)

    tab = pl.BlockSpec((RET_HEADS, CHUNK, CHUNK), lambda n: (0, 0, 0))
    pos = pl.BlockSpec((CHUNK, CHUNK), lambda n: (n, 0))
    row = pl.BlockSpec((CHUNK, 1024), lambda n: (n, 0))
    return _pcall(
        body, name="ret_fwd", grid=(nch,),
        in_specs=[row, pl.BlockSpec((CHUNK, 1024), lambda n: (n, 1)), pl.BlockSpec((CHUNK, 1024), lambda n: (n, 2)),
                  pl.BlockSpec((1, 1024), lambda n: (0, 0)), pos, pos, tab, tab, tab],
        out_specs=[row, pl.BlockSpec((RET_HEADS, None, 128, 256), lambda n: (0, n, 0, 0)), row],
        out_shape=[jax.ShapeDtypeStruct((t, 1024), F32), jax.ShapeDtypeStruct((RET_HEADS, nch, 128, 256), F32),
                   jax.ShapeDtypeStruct((t, 2048), BF16)],
        scratch_shapes=[pltpu.VMEM((RET_HEADS, 128, 256), F32)],
        compiler_params=_params(dimension_semantics=("arbitrary",)))(
            proj, proj, proj, gn_g, cos2, sin2, dmat, qdec, kdec)


def _ret_bwd(dproj, proj, states, do, tables):
    t = proj.shape[0]
    nch = t // CHUNK
    dmat, qdec, kdec, cos2, sin2 = tables

    def body(dp_in, qk_ref, v_ref, do_ref, st_ref, c_ref, s_ref, dm_ref, qd_ref, kd_ref, dp_ref, rst):
        del dp_in
        @pl.when(pl.program_id(0) == 0)
        def _():
            rst[...] = jnp.zeros_like(rst)

        c, s = c_ref[...], s_ref[...]
        for h in range(RET_HEADS):
            q = _rot(qk_ref[:, 128 * h:128 * (h + 1)], c, s)
            k = _rot(qk_ref[:, 512 + 128 * h:512 + 128 * (h + 1)], c, s) * RET_SCALE
            qb, kb = q.astype(BF16), k.astype(BF16)
            vb = v_ref[:, 256 * h:256 * (h + 1)].astype(BF16)
            dob = do_ref[:, 256 * h:256 * (h + 1)].astype(BF16)
            pb = st_ref[h].astype(BF16)
            r = rst[h]
            rb = r.astype(BF16)
            dm, qd, kd = dm_ref[h], qd_ref[h], kd_ref[h]
            sb = (_dot(qb, kb, "nt") * dm).astype(BF16)
            dsb = (_dot(dob, vb, "nt") * dm).astype(BF16)
            dq = _dot(dsb, kb) + _dot(dob, pb, "nt") * qd
            dk = _dot(dsb, qb, "tn") + _dot(vb, rb, "nt") * kd
            dv = _dot(sb, dob, "tn") + _dot((k * kd).astype(BF16), rb)
            rst[h] = _dot((q * qd).astype(BF16), dob, "tn") + qd[CHUNK - 1:CHUNK, 0:1] * r
            dp_ref[:, 128 * h:128 * (h + 1)] = _unrot(dq, c, s).astype(BF16)
            dp_ref[:, 512 + 128 * h:512 + 128 * (h + 1)] = (_unrot(dk, c, s) * RET_SCALE).astype(BF16)
            dp_ref[:, 1024 + 256 * h:1024 + 256 * (h + 1)] = dv.astype(BF16)

    rev = lambda n: nch - 1 - n
    tab = pl.BlockSpec((RET_HEADS, CHUNK, CHUNK), lambda n: (0, 0, 0))
    pos = pl.BlockSpec((CHUNK, CHUNK), lambda n: (rev(n), 0))
    row = pl.BlockSpec((CHUNK, 1024), lambda n: (rev(n), 0))
    return _pcall(
        body, name="ret_bwd", grid=(nch,),
        in_specs=[pl.BlockSpec(memory_space=pl.ANY), row, pl.BlockSpec((CHUNK, 1024), lambda n: (rev(n), 1)), row,
                  pl.BlockSpec((RET_HEADS, None, 128, 256), lambda n: (0, rev(n), 0, 0)),
                  pos, pos, tab, tab, tab],
        out_specs=pl.BlockSpec((CHUNK, 2048), lambda n: (rev(n), 0)),
        out_shape=jax.ShapeDtypeStruct((t, 5120), BF16),
        scratch_shapes=[pltpu.VMEM((RET_HEADS, 128, 256), F32)], input_output_aliases={0: 0},
        compiler_params=_params(dimension_semantics=("arbitrary",)))(
            dproj, proj, proj, do, states, cos2, sin2, dmat, qdec, kdec)


def _row_ids(i, tr):
    return i * tr + lax.broadcasted_iota(jnp.int32, (tr, 1), 0)


SH_ROWS = HALO - 8
CONV_VPU_TAPS = 21


def _shifted_copies(xs, sh, tr):
    for b in range(1, 8):
        sh[b - 1] = xs[pl.ds(b, tr + SH_ROWS), :]


def _shifted(xs, sh, off, tr, lanes=slice(None)):
    a, b = divmod(off, 8)
    return xs[pl.ds(8 * a, tr), lanes] if b == 0 else sh[b - 1, pl.ds(8 * a, tr), lanes]


def _taps_mxu(xs, sh, w_ref, offs, tr, first=0):
    sub = lax.broadcasted_iota(jnp.int32, (256, 128), 0)
    eye = (sub & 127) == lax.broadcasted_iota(jnp.int32, (256, 128), 1)
    outs = []
    for c in range(8):
        lanes = slice(128 * c, 128 * (c + 1))
        acc = None
        for w in range(first, len(offs), 2):
            wb = min(w + 1, len(offs) - 1)
            w_hi = w_ref[w:w + 1, lanes]
            w_lo = w_ref[wb:wb + 1, lanes] if wb > w else jnp.zeros((1, 128), F32)
            dmat = jnp.where(eye, jnp.where(sub < 128, w_hi, w_lo), 0.0).astype(BF16)
            lhs = jnp.concatenate([_shifted(xs, sh, offs[w], tr, lanes).astype(BF16),
                                   _shifted(xs, sh, offs[wb], tr, lanes).astype(BF16)], axis=1)
            d = _dot(lhs, dmat)
            acc = d if acc is None else acc + d
        outs.append(acc)
    return jnp.concatenate(outs, axis=1)


def _conv_fwd(cat, proj, conv_w, conv_b, ln_g, ln_b):
    t = proj.shape[0]
    tr = _rows(t)
    hb = tr // HALO

    def body(cat_in, ua_ref, ug_ref, pa_ref, pg_ref, w_ref, b_ref, lg_ref, lb_ref, c_ref, hd_ref, y_ref, xs, sh):
        del cat_in
        i = pl.program_id(0)
        hdn = ua_ref[...] * _sigmoid(ug_ref[...])
        hd_ref[...] = hdn
        prev = pa_ref[...] * _sigmoid(pg_ref[...])
        xs[0:HALO, :] = jnp.where(i > 0, prev, 0.0)
        xs[HALO:HALO + tr, :] = hdn
        _shifted_copies(xs, sh, tr)
        offs = [HALO - (CONV_WIDTH - 1) + w for w in range(CONV_WIDTH)]
        acc = _taps_mxu(xs, sh, w_ref, offs, tr, first=CONV_VPU_TAPS) + b_ref[...]
        for w in range(CONV_VPU_TAPS):
            acc += w_ref[w:w + 1, :] * _shifted(xs, sh, offs[w], tr)
        y_ref[...] = acc
        mu = jnp.mean(acc, axis=-1, keepdims=True)
        yc = acc - mu
        rstd = lax.rsqrt(jnp.mean(yc * yc, axis=-1, keepdims=True) + EPS)
        yn = yc * rstd * lg_ref[...] + lb_ref[...]
        c = yn * _sigmoid(yn)
        c_ref[...] = jnp.where(_row_ids(i, tr) >= PAD_FRONT, c, 0.0).astype(BF16)

    row = pl.BlockSpec((tr, 1024), lambda i: (i, 0))
    vec = pl.BlockSpec((1, 1024), lambda i: (0, 0))
    halo = lambda col: pl.BlockSpec((HALO, 1024), lambda i: (jnp.maximum(i * hb - 1, 0), col))
    return _pcall(body, name="conv_fwd", grid=(t // tr,),
                  in_specs=[pl.BlockSpec(memory_space=pl.ANY),
                            pl.BlockSpec((tr, 1024), lambda i: (i, 3)), pl.BlockSpec((tr, 1024), lambda i: (i, 4)),
                            halo(3), halo(4), pl.BlockSpec((32, 1024), lambda i: (0, 0)), vec, vec, vec],
                  out_specs=[pl.BlockSpec((tr, 1024), lambda i: (i, 1)), row, row],
                  out_shape=[jax.ShapeDtypeStruct((t, 2048), BF16), jax.ShapeDtypeStruct((t, 1024), F32),
                             jax.ShapeDtypeStruct((t, 1024), F32)],
                  scratch_shapes=[pltpu.VMEM((tr + HALO, 1024), F32), pltpu.VMEM((7, tr + SH_ROWS, 1024), F32)],
                  input_output_aliases={0: 0}, compiler_params=_params())(
                      cat, proj, proj, proj, proj, conv_w, conv_b, ln_g, ln_b)


def _mix_bwd_head(dh, w_out, o, proj, gn_g, y, ln_g, ln_b):
    t = dh.shape[0]
    tr = _rows(t)
    nb, r, n = w_out.shape

    def body(dh_ref, b_ref, o_ref, g_ref, w_ref, y_ref, lg_ref, lb_ref,
             do_ref, dp_ref, dw_ref, dy_ref, dlg_ref, dlb_ref, dcb_ref):
        i = pl.program_id(0)

        @pl.when(i == 0)
        def _():
            for ref in (dw_ref, dlg_ref, dlb_ref, dcb_ref):
                ref[...] = jnp.zeros_like(ref)

        dcat = _dot(dh_ref[...].astype(BF16), b_ref[...].reshape(nb * r, n), "nt")
        for h in range(RET_HEADS):
            sl = slice(256 * h, 256 * (h + 1))
            x = o_ref[:, sl]
            mu = jnp.mean(x, axis=-1, keepdims=True)
            xc = x - mu
            rstd = lax.rsqrt(jnp.mean(xc * xc, axis=-1, keepdims=True) + EPS)
            xh = xc * rstd
            w = w_ref[:, sl]
            g = g_ref[:, sl]
            sg = _sigmoid(g)
            d = dcat[:, sl]
            don = d * (g * sg)
            dp_ref[:, sl] = (d * (xh * w) * (sg * (1.0 + g * (1.0 - sg)))).astype(BF16)
            dw_ref[:, sl] += jnp.sum(don * xh, axis=0, keepdims=True)
            dxh = don * w
            m1 = jnp.mean(dxh, axis=-1, keepdims=True)
            m2 = jnp.mean(dxh * xh, axis=-1, keepdims=True)
            do_ref[:, sl] = rstd * (dxh - m1 - xh * m2)
        yv = y_ref[...]
        mu = jnp.mean(yv, axis=-1, keepdims=True)
        yc = yv - mu
        rstd = lax.rsqrt(jnp.mean(yc * yc, axis=-1, keepdims=True) + EPS)
        xh = yc * rstd
        lg = lg_ref[...]
        yn = xh * lg + lb_ref[...]
        sg = _sigmoid(yn)
        dyn = jnp.where(_row_ids(i, tr) >= PAD_FRONT, dcat[:, 1024:] * (sg * (1.0 + yn * (1.0 - sg))), 0.0)
        dlg_ref[...] += jnp.sum(dyn * xh, axis=0, keepdims=True)
        dlb_ref[...] += jnp.sum(dyn, axis=0, keepdims=True)
        dxh = dyn * lg
        m1 = jnp.mean(dxh, axis=-1, keepdims=True)
        m2 = jnp.mean(dxh * xh, axis=-1, keepdims=True)
        dy = rstd * (dxh - m1 - xh * m2)
        dy_ref[...] = dy
        dcb_ref[...] += jnp.sum(dy, axis=0, keepdims=True)

    row = pl.BlockSpec((tr, 1024), lambda i: (i, 0))
    vec = pl.BlockSpec((1, 1024), lambda i: (0, 0))
    gate = pl.BlockSpec((tr, 1024), lambda i: (i, 2))
    vsh = jax.ShapeDtypeStruct((1, 1024), F32)
    fsh = jax.ShapeDtypeStruct((t, 1024), F32)
    return _pcall(body, name="mix_bwd_head", grid=(t // tr,),
                  in_specs=[row, pl.BlockSpec((nb, r, n), lambda i: (0, 0, 0)), row, gate, vec, row, vec, vec],
                  out_specs=[row, gate, vec, row, vec, vec, vec],
                  out_shape=[fsh, jax.ShapeDtypeStruct((t, 5120), BF16), vsh, fsh, vsh, vsh, vsh],
                  compiler_params=_params(dimension_semantics=("arbitrary",)))(
                      dh, w_out, o, proj, gn_g, y, ln_g, ln_b)


def _conv_bwd_taps(dproj, dy, hdn, proj, conv_w):
    t = dy.shape[0]
    tr = _rows(t)
    hb = tr // HALO
    nt = t // tr

    def body(dp_in, dy_ref, nx_ref, hd_ref, ph_ref, ua_ref, ug_ref, w_ref, da_ref, dg_ref, dw_ref, xs, sh):
        del dp_in
        i = pl.program_id(0)

        @pl.when(i == 0)
        def _():
            dw_ref[...] = jnp.zeros_like(dw_ref)

        dy = dy_ref[...]
        xs[0:tr, :] = dy
        xs[tr:tr + HALO, :] = jnp.where(i < nt - 1, nx_ref[...], 0.0)
        _shifted_copies(xs, sh, tr)
        dh = _taps_mxu(xs, sh, w_ref, [CONV_WIDTH - 1 - w for w in range(CONV_WIDTH)], tr)
        xs[0:HALO, :] = jnp.where(i > 0, ph_ref[...], 0.0)
        xs[HALO:HALO + tr, :] = hd_ref[...]
        _shifted_copies(xs, sh, tr)
        for w in range(CONV_WIDTH):
            dw_ref[w:w + 1, :] += jnp.sum(dy * _shifted(xs, sh, HALO - (CONV_WIDTH - 1) + w, tr), axis=0, keepdims=True)
        dh = jnp.where(_row_ids(i, tr) >= PAD_FRONT, dh, 0.0)
        sg = _sigmoid(ug_ref[...])
        da_ref[...] = (dh * sg).astype(BF16)
        dg_ref[...] = (dh * ua_ref[...] * sg * (1.0 - sg)).astype(BF16)

    row = pl.BlockSpec((tr, 1024), lambda i: (i, 0))
    return _pcall(body, name="conv_bwd_taps", grid=(nt,),
                  in_specs=[pl.BlockSpec(memory_space=pl.ANY),
                            row, pl.BlockSpec((HALO, 1024), lambda i: (jnp.minimum((i + 1) * hb, nt * hb - 1), 0)),
                            row, pl.BlockSpec((HALO, 1024), lambda i: (jnp.maximum(i * hb - 1, 0), 0)),
                            pl.BlockSpec((tr, 1024), lambda i: (i, 3)), pl.BlockSpec((tr, 1024), lambda i: (i, 4)),
                            pl.BlockSpec((32, 1024), lambda i: (0, 0))],
                  out_specs=[pl.BlockSpec((tr, 1024), lambda i: (i, 3)), row, pl.BlockSpec((32, 1024), lambda i: (0, 0))],
                  out_shape=[jax.ShapeDtypeStruct((t, 5120), BF16), jax.ShapeDtypeStruct((t, 1024), BF16),
                             jax.ShapeDtypeStruct((32, 1024), F32)],
                  scratch_shapes=[pltpu.VMEM((tr + HALO, 1024), F32), pltpu.VMEM((7, tr + SH_ROWS, 1024), F32)],
                  input_output_aliases={0: 0}, compiler_params=_params())(
                      dproj, dy, dy, hdn, hdn, proj, proj, conv_w)


NEG_BIG = -1e30


def _seg_tables(qb):
    j = np.arange(128)
    bd = (j[:, None] // 64 == j[None, :] // 64).astype(np.float32)
    ones = np.ones((128, 128), np.float32)
    later = np.concatenate([(j[:, None] >= j[None, :]).astype(np.float32), ones], axis=1)
    earlier = np.concatenate([(j[:, None] < j[None, :]).astype(np.float32), ones], axis=1)
    per = qb // CHUNK
    row = np.arange(qb)[:, None]
    pad = np.broadcast_to(j[None, :] < PAD_FRONT, (qb, 128))
    diag = [(g * CHUNK + j[None, :]) >= row for g in range(per)]
    masks = diag + [np.zeros((qb, 128), bool), pad, diag[0] | pad]
    bias = np.stack([np.where(m, NEG_BIG, 0.0) for m in masks]).astype(np.float32)
    dup = lambda m: np.concatenate([m, m], axis=0)
    return (jnp.asarray(bd, BF16), jnp.asarray(dup(later), BF16), jnp.asarray(dup(earlier), BF16),
            jnp.asarray(bias, F32))


def _split_dot(x, m):
    hi = x.astype(BF16)
    lo = (x - hi.astype(F32)).astype(BF16)
    return _dot(hi, m) + _dot(lo, m)


def _qk_norm_fwd(qkv, qg, kg, bd):
    t = qkv.shape[0]
    tr = _rows(t)
    nb = tr // CHUNK

    def body(q_ref, k_ref, v_ref, qg_ref, kg_ref, bd_ref, qo, kt, vt):
        bdm = bd_ref[...]
        sub = lax.broadcasted_iota(jnp.int32, (128, 1), 0)

        def pair_layout(x, t_ref, hp, b):
            xt = x.T
            t_ref[hp, b] = jnp.concatenate([jnp.where(sub < 64, xt, 0.0), jnp.where(sub >= 64, xt, 0.0)],
                                           axis=1).astype(BF16)

        for hp in range(8):
            sl = slice(128 * hp, 128 * (hp + 1))
            x = q_ref[:, sl]
            r = lax.rsqrt(_split_dot(x * x, bdm) * (1.0 / 64) + EPS)
            qo[:, sl] = (x * r * (qg_ref[:, sl] * SB_SCALE)).astype(BF16)
            x = k_ref[:, sl]
            r = lax.rsqrt(_split_dot(x * x, bdm) * (1.0 / 64) + EPS)
            kn = x * r * kg_ref[:, sl]
            v = v_ref[:, sl]
            for b in range(nb):
                rows = slice(CHUNK * b, CHUNK * (b + 1))
                pair_layout(kn[rows], kt, hp, b)
                pair_layout(v[rows], vt, hp, b)

    col = lambda c: pl.BlockSpec((tr, 1024), lambda i: (i, c))
    vec = pl.BlockSpec((1, 1024), lambda i: (0, 0))
    wide = pl.BlockSpec((8, nb, 128, 256), lambda i: (0, i, 0, 0))
    wsh = jax.ShapeDtypeStruct((8, t // CHUNK, 128, 256), BF16)
    return _pcall(body, name="qk_norm_fwd", grid=(t // tr,),
                  in_specs=[col(0), col(1), col(2), vec, vec, pl.BlockSpec((128, 128), lambda i: (0, 0))],
                  out_specs=[col(0), wide, wide],
                  out_shape=[jax.ShapeDtypeStruct((t, 1024), BF16), wsh, wsh])(qkv, qkv, qkv, qg, kg, bd)


def _qk_norm_bwd(qkv, dq, dk, dv, qg, kg, bd):
    t = qkv.shape[0]
    tr = _rows(t)

    def body(q_ref, k_ref, dq_ref, dk_ref, dv_ref, qg_ref, kg_ref, bd_ref, o_ref, dqg_ref, dkg_ref):
        @pl.when(pl.program_id(0) == 0)
        def _():
            dqg_ref[...] = jnp.zeros_like(dqg_ref)
            dkg_ref[...] = jnp.zeros_like(dkg_ref)

        bdm = bd_ref[...]
        for part, (src, d_ref, g_ref, dg_ref) in enumerate(((q_ref, dq_ref, qg_ref, dqg_ref),
                                                           (k_ref, dk_ref, kg_ref, dkg_ref))):
            for cix in range(8):
                sl = slice(128 * cix, 128 * (cix + 1))
                x = src[:, sl]
                d = d_ref[:, sl]
                r = lax.rsqrt(_split_dot(x * x, bdm) * (1.0 / 64) + EPS)
                u = d * g_ref[:, sl]
                m = _split_dot(u * x, bdm) * (1.0 / 64)
                o_ref[:, 1024 * part + 128 * cix:1024 * part + 128 * (cix + 1)] = (r * u - x * (r * r * r * m)).astype(BF16)
                dg_ref[:, sl] += jnp.sum(d * x * r, axis=0, keepdims=True)
        o_ref[:, 2048:3072] = dv_ref[...].astype(BF16)

    col = lambda c: pl.BlockSpec((tr, 1024), lambda i: (i, c))
    vec = pl.BlockSpec((1, 1024), lambda i: (0, 0))
    vsh = jax.ShapeDtypeStruct((1, 1024), F32)
    return _pcall(body, name="qk_norm_bwd", grid=(t // tr,),
                  in_specs=[col(0), col(1), col(0), col(0), col(0), vec, vec, pl.BlockSpec((128, 128), lambda i: (0, 0))],
                  out_specs=[pl.BlockSpec((tr, 3072), lambda i: (i, 0)), vec, vec],
                  out_shape=[jax.ShapeDtypeStruct((t, 3072), BF16), vsh, vsh])(qkv, qkv, dq, dk, dv, qg, kg, bd)


def _split2(x):
    hi = x.astype(BF16)
    lo = (x - hi.astype(F32)).astype(BF16)
    return jnp.concatenate([hi, lo], axis=1)


def _sb_sums(z, later_tab):
    sp = jnp.maximum(z, 0.0) + jnp.log(1.0 + jnp.exp(-jnp.abs(z)))
    return _dot(_split2(sp), later_tab)


def _sb_bias_index(i, kb, per):
    g = kb - i * per
    return jnp.where(kb == 0, jnp.where(i == 0, per + 2, per + 1), jnp.where(g >= 0, g, per))


def _sb_qb(t):
    return _tile(t, (384, 128))


def _sb_fwd(qh, kt, vt, later_tab, bias_tab):
    t = qh.shape[0]
    qb = _sb_qb(t)
    per = qb // CHUNK
    nkb_all = t // CHUNK

    nq = t // qb

    def body(q_ref, kt_ref, vt_ref, tab_ref, bias_ref, o_ref, ws_ref, acc, carry, zbuf, wbuf, wsem):
        h, i = pl.program_id(0), pl.program_id(1)
        n = h * nq + i
        p = n & 1
        q = q_ref[...]
        acc[...] = jnp.zeros_like(acc)
        carry[...] = jnp.zeros_like(carry)
        nkb = (i + 1) * per
        save = lambda kb: pltpu.make_async_copy(wbuf.at[p, kb], ws_ref.at[h, i, kb], wsem.at[p, kb])

        def drain(step, par):
            hs, is_ = step // nq, step % nq

            def one(kb, _):
                pltpu.make_async_copy(wbuf.at[par, kb], ws_ref.at[hs, is_, kb], wsem.at[par, kb]).wait()
                return 0

            lax.fori_loop(0, (is_ + 1) * per, one, 0)

        @pl.when(n >= 2)
        def _():
            drain(n - 2, p)

        for u in range(per):
            zbuf[u] = _dot(q, kt_ref[nkb - 1 - u])

        def trip(s, diagonal):
            top = nkb - 1 - per * s
            if not diagonal:
                for u in range(per):
                    save(top + per - u).start(priority=1)
            z2s = [zbuf[u] for u in range(per)]
            for u in range(per):
                zbuf[u] = _dot(q, kt_ref[jnp.maximum(top - per - u, 0)])
            first = [CHUNK * (per - 1 - u) if diagonal else 0 for u in range(per)]
            cins = [carry[0], carry[1]]
            zs, cus = [], []
            for u in range(per):
                zs.append([z2s[u][first[u]:, 128 * hh:128 * (hh + 1)] for hh in range(2)])
                if diagonal or u == per - 1:
                    bias = bias_ref[_sb_bias_index(i, top - u, per)][first[u]:]
                    zs[u] = [z + bias for z in zs[u]]
                cus.append([_sb_sums(z, tab_ref[...]) for z in zs[u]])
            part = None
            for u in range(per):
                kb, lo = top - u, first[u]
                for hh in range(2):
                    sl = slice(128 * hh, 128 * (hh + 1))
                    cu = cus[u][hh]
                    wbuf[p, kb, lo:, sl] = jnp.exp(zs[u][hh] - cu[:, :128] - cins[hh][lo:]).astype(BF16)
                    if lo:
                        wbuf[p, kb, :lo, sl] = jnp.zeros((lo, 128), BF16)
                        cins[hh] = jnp.concatenate([cins[hh][:lo], cins[hh][lo:] + cu[:, 128:]], axis=0)
                    else:
                        cins[hh] = cins[hh] + cu[:, 128:]
                d = _dot(wbuf[p, kb], vt_ref[kb], "nt")
                part = d if part is None else part + d
            carry[0], carry[1] = cins[0], cins[1]
            acc[...] += part

        trip(0, True)

        def step(s, _):
            trip(s, False)
            return 0

        lax.fori_loop(1, nkb // per, step, 0)
        for u in range(per):
            save(per - 1 - u).start(priority=1)
        o_ref[...] = acc[...]

        @pl.when(n == 8 * nq - 1)
        def _():
            drain(n - 1, 1 - p)
            drain(n, p)

    blk = pl.BlockSpec((qb, 128), lambda h, i: (i, h))
    wide = pl.BlockSpec((None, nkb_all, 128, 256), lambda h, i: (h, 0, 0, 0))
    return _pcall(body, name="sb_fwd", grid=(8, t // qb),
                  in_specs=[blk, wide, wide, pl.BlockSpec((256, 256), lambda h, i: (0, 0)),
                            pl.BlockSpec((per + 3, qb, 128), lambda h, i: (0, 0, 0))],
                  out_specs=[blk, pl.BlockSpec(memory_space=pl.ANY)],
                  out_shape=[jax.ShapeDtypeStruct((t, 1024), F32),
                             jax.ShapeDtypeStruct((8, t // qb, nkb_all, qb, 256), BF16)],
                  scratch_shapes=[pltpu.VMEM((qb, 128), F32), pltpu.VMEM((2, qb, 128), F32),
                                  pltpu.VMEM((per, qb, 256), F32), pltpu.VMEM((2, nkb_all, qb, 256), BF16),
                                  pltpu.SemaphoreType.DMA((2, nkb_all))],
                  compiler_params=_params(dimension_semantics=("arbitrary", "arbitrary")))(
                      qh, kt, vt, later_tab, bias_tab)


def _sb_bwd(qh, kt, vt, wsave, do, earlier_tab, bias_tab):
    t = qh.shape[0]
    qb = _sb_qb(t)
    per = qb // CHUNK
    nkb_all = t // CHUNK

    zero_slot = nkb_all
    nq = t // qb

    def body(q_ref, kt_ref, vt_ref, ws_ref, do_ref, etab_ref, bias_ref,
             dq_ref, dk_ref, dv_ref, acc, gcarry, zbuf, dwbuf, wbuf, wsem, dzbuf):
        h, i = pl.program_id(0), pl.program_id(1)
        n = h * nq + i
        p = n & 1

        @pl.when(i == 0)
        def _():
            dk_ref[...] = jnp.zeros_like(dk_ref)
            dv_ref[...] = jnp.zeros_like(dv_ref)

        nkb = (i + 1) * per
        fetch = lambda kb: pltpu.make_async_copy(ws_ref.at[h, i, kb], wbuf.at[p, kb], wsem.at[p, kb])

        def prefetch(step, par):
            hs, is_ = step // nq, step % nq

            def one(kb, _):
                pltpu.make_async_copy(ws_ref.at[hs, is_, kb], wbuf.at[par, kb], wsem.at[par, kb]).start(priority=1)
                return 0

            lax.fori_loop(0, (is_ + 1) * per, one, 0)

        @pl.when(n == 0)
        def _():
            prefetch(n, p)

        @pl.when(n + 1 < 8 * nq)
        def _():
            prefetch(n + 1, 1 - p)

        q = q_ref[...]
        dob = do_ref[...].astype(BF16)
        acc[...] = jnp.zeros_like(acc)
        gcarry[...] = jnp.zeros_like(gcarry)
        zbuf[...] = _dot(q, kt_ref[0])
        dwbuf[...] = _dot(dob, vt_ref[0])
        dzbuf[...] = jnp.zeros_like(dzbuf)
        wbuf[p, zero_slot] = jnp.zeros((qb, 256), BF16)

        q_t = q.astype(F32).T.astype(BF16)
        do_t = do_ref[...].T.astype(BF16)
        sub = lax.broadcasted_iota(jnp.int32, (128, 1), 0)

        def gradients(slot, kb):
            dz2 = dzbuf[...]
            acc[...] += _dot(dz2, kt_ref[kb], "nt")
            dk2 = _dot(q_t, dz2)
            dv2 = _dot(do_t, wbuf[p, slot])
            dk_ref[kb] += jnp.where(sub < 64, dk2[:, :128], dk2[:, 128:])
            dv_ref[kb] += jnp.where(sub < 64, dv2[:, :128], dv2[:, 128:])

        def trip(kb, lo):
            fetch(kb).wait()
            bias = bias_ref[_sb_bias_index(i, kb, per)][lo:]
            z2 = zbuf[...]
            dw2 = dwbuf[...]
            nxt = jnp.minimum(kb + 1, nkb - 1)
            zbuf[...] = _dot(q, kt_ref[nxt])
            dwbuf[...] = _dot(dob, vt_ref[nxt])
            gradients(jnp.where(kb == 0, zero_slot, kb - 1), jnp.maximum(kb - 1, 0))
            w2 = wbuf[p, kb]
            for hh in range(2):
                sl = slice(128 * hh, 128 * (hh + 1))
                z = z2[lo:, sl] + bias
                e = jnp.exp(-jnp.abs(z))
                r = 1.0 / (1.0 + e)
                sig = jnp.where(z >= 0, r, e * r)
                gw = w2[lo:, sl].astype(F32) * dw2[lo:, sl]
                cu2 = _dot(_split2(gw), etab_ref[...])
                gin = gcarry[hh, lo:, :]
                gcarry[hh, lo:, :] = gin + cu2[:, 128:]
                dzbuf[lo:, sl] = (gw - sig * (gw + cu2[:, :128] + gin)).astype(BF16)
                if lo:
                    dzbuf[:lo, sl] = jnp.zeros((lo, 128), BF16)

        def step(kb, _):
            trip(kb, 0)
            return 0

        lax.fori_loop(0, nkb - per, step, 0)
        for g in range(per):
            trip(nkb - per + g, CHUNK * g)
        gradients(nkb - 1, nkb - 1)
        dq_ref[...] = acc[...] * SB_SCALE

        @pl.when(i == nq - 1)
        def _():
            def untranspose(kb, _):
                dk_ref[kb] = dk_ref[kb].T
                dv_ref[kb] = dv_ref[kb].T
                return 0

            lax.fori_loop(0, nkb_all, untranspose, 0)

    blk = pl.BlockSpec((qb, 128), lambda h, i: (i, h))
    wide = pl.BlockSpec((None, nkb_all, 128, 256), lambda h, i: (h, 0, 0, 0))
    tab = pl.BlockSpec((256, 256), lambda h, i: (0, 0))
    kv_out = pl.BlockSpec((nkb_all, 128, 128), lambda h, i: (0, 0, h))
    ksh = jax.ShapeDtypeStruct((nkb_all, 128, 1024), F32)
    dq, dk, dv = _pcall(
        body, name="sb_bwd", grid=(8, t // qb),
        in_specs=[blk, wide, wide, pl.BlockSpec(memory_space=pl.ANY), blk, tab,
                  pl.BlockSpec((per + 3, qb, 128), lambda h, i: (0, 0, 0))],
        out_specs=[blk, kv_out, kv_out], out_shape=[jax.ShapeDtypeStruct((t, 1024), F32), ksh, ksh],
        scratch_shapes=[pltpu.VMEM((qb, 128), F32), pltpu.VMEM((2, qb, 128), F32),
                        pltpu.VMEM((qb, 256), F32), pltpu.VMEM((qb, 256), F32),
                        pltpu.VMEM((2, nkb_all + 1, qb, 256), BF16), pltpu.SemaphoreType.DMA((2, nkb_all)),
                        pltpu.VMEM((qb, 256), BF16)],
        compiler_params=_params(dimension_semantics=("arbitrary", "arbitrary")))(
            qh, kt, vt, wsave, do, earlier_tab, bias_tab)
    return dq, dk.reshape(t, 1024), dv.reshape(t, 1024)


def _adamw_math(w, g, m, v):
    m = ADAM_B1 * m + (1.0 - ADAM_B1) * g
    v = ADAM_B2 * v + (1.0 - ADAM_B2) * (g * g)
    m_hat = m / (1.0 - ADAM_B1 ** ADAM_STEP)
    v_hat = v / (1.0 - ADAM_B2 ** ADAM_STEP)
    delta = -ADAM_LR * (m_hat / (jnp.sqrt(v_hat) + ADAM_EPS) + ADAM_WD * w)
    return delta, m, v


def _adamw(name, w, owns, recvs, m, v, me):
    shape = w.shape
    c = shape[-1]
    nl = len(owns)
    w3, m3, v3 = (a.reshape(nl, -1, c) for a in (w, m, v))
    r = w3.shape[1]
    tr = _tile(r, (256, 128))
    owns = [o.reshape(N_DEV, r, c) for o in owns]
    recvs = [p.reshape(N_DEV - 1, r, c) for p in recvs]

    def body(me_ref, w_ref, *rest):
        own_refs, recv_refs = rest[:nl], rest[nl:2 * nl]
        m_ref, v_ref = rest[2 * nl:2 * nl + 2]
        g_out, d_out, m_out, v_out = rest[2 * nl + 2:]
        layer = pl.program_id(0)

        def grad(k):
            g = own_refs[k][...].astype(F32)
            for s in range(N_DEV - 1):
                g = g + recv_refs[k][s].astype(F32)
            return g

        g = grad(0)
        for k in range(1, nl):
            g = jnp.where(layer == k, grad(k), g)
        d, mn, vn = _adamw_math(w_ref[...], g, m_ref[...], v_ref[...])
        g_out[...] = g
        d_out[...] = d
        m_out[...] = mn
        v_out[...] = vn

    row = pl.BlockSpec((None, tr, c), lambda l, i, me_ref: (l, i, 0))
    own = lambda k: pl.BlockSpec((None, tr, c), lambda l, i, me_ref: (me_ref[0], jnp.where(l == k, i, 0), 0))
    rcv = lambda k: pl.BlockSpec((N_DEV - 1, tr, c), lambda l, i, me_ref: (0, jnp.where(l == k, i, 0), 0))
    osh = jax.ShapeDtypeStruct((nl, r, c), F32)
    grid_spec = pltpu.PrefetchScalarGridSpec(
        num_scalar_prefetch=1, grid=(nl, r // tr),
        in_specs=[row] + [own(k) for k in range(nl)] + [rcv(k) for k in range(nl)] + [row, row],
        out_specs=[row, row, row, row])
    outs = _pcall(body, name=name, grid_spec=grid_spec, out_shape=[osh, osh, osh, osh])(
        me.reshape(1), w3, *owns, *recvs, m3, v3)
    return tuple(o.reshape(shape) for o in outs)


def _place():
    x, y, c = lax.axis_index("x"), lax.axis_index("y"), lax.axis_index("c")
    return x, y, c, 4 * x + 2 * y + c


def _peer(x, y, c, rel):
    return (x ^ ((rel >> 2) & 1), y ^ ((rel >> 1) & 1), c ^ (rel & 1))


def _gather_first(now, later):
    n, k = len(now), len(later)

    def body(*refs):
        ins, outs = refs[:n + k], refs[n + k:2 * (n + k)]
        send, recv, lsem = refs[2 * (n + k):]
        x, y, c, me = _place()
        locals_ = []
        for w in range(n + k):
            local = pltpu.make_async_copy(ins[w], outs[w].at[me], lsem.at[w])
            local.start()
            locals_.append(local)
        def copy(w, src, slot, rel, to_rel):
            return pltpu.make_async_remote_copy(src_ref=src, dst_ref=outs[w].at[slot], send_sem=send.at[w, rel - 1],
                                                recv_sem=recv.at[w, rel - 1], device_id=_peer(x, y, c, to_rel),
                                                device_id_type=MESH)

        for w in range(n):
            for rel in (1, 2, 4, 6):
                copy(w, ins[w], me, rel, rel).start()
        for w in range(n):
            for rel in (2, 4, 6):
                copy(w, ins[w], me ^ rel, rel, rel).wait_recv()
                copy(w, outs[w].at[me ^ rel], me ^ rel, rel | 1, 1).start()
        for w in range(n):
            for rel in (1, 3, 5, 7):
                copy(w, ins[w], me ^ rel, rel, 1).wait_recv()
            for rel in range(1, N_DEV):
                copy(w, ins[w], me, rel, rel).wait_send()
        for local in locals_:
            local.wait()

    hbm = pl.BlockSpec(memory_space=pl.ANY)
    vmem = pl.BlockSpec(memory_space=pltpu.VMEM)
    arrays = list(now) + list(later)
    return _pcall(body, name="gather_first", in_specs=[vmem] * (n + k), out_specs=[hbm] * (n + k),
                  out_shape=[jax.ShapeDtypeStruct((N_DEV,) + a.shape, a.dtype) for a in arrays],
                  scratch_shapes=[pltpu.SemaphoreType.DMA((n, N_DEV - 1)), pltpu.SemaphoreType.DMA((n, N_DEV - 1)),
                                  pltpu.SemaphoreType.DMA((n + k,))],
                  compiler_params=_params(has_side_effects=True))(*arrays)


_HBM = pl.BlockSpec(memory_space=pltpu.HBM)
_SEM = pl.BlockSpec(memory_space=pltpu.SEMAPHORE)
_DATAFLOW = pltpu.SideEffectType.DATAFLOW_SIDE_EFFECTING


def _exchange_refs(srcs, lands, mode, me, rel, j):
    if mode == "gather":
        return srcs[j], lands[j].at[me], lands[j].at[me ^ rel]
    return srcs[j].at[me ^ rel], lands[j].at[rel - 1], lands[j].at[rel - 1]


def _exchange_start(name, srcs, lands, mode):
    n = len(srcs)

    def body(*refs):
        ins, lnd = refs[:n], refs[n:2 * n]
        send, recv = refs[2 * n], refs[2 * n + 1]
        token = refs[-1]
        x, y, c, me = _place()
        for j in range(n):
            for rel in range(1, N_DEV):
                src, dst, _ = _exchange_refs(ins, lnd, mode, me, rel, j)
                pltpu.make_async_remote_copy(src_ref=src, dst_ref=dst, send_sem=send.at[j * (N_DEV - 1) + rel - 1],
                                             recv_sem=recv.at[j * (N_DEV - 1) + rel - 1],
                                             device_id=_peer(x, y, c, rel), device_id_type=MESH).start()
        token[...] = jnp.zeros_like(token)

    sems = pltpu.SemaphoreType.DMA((n * (N_DEV - 1),))
    hbm_like = lambda a: pltpu.HBM(a.shape, a.dtype)
    outs = _pcall(body, name=name + "_start",
                  in_specs=[_HBM] * (2 * n), out_specs=[_SEM, _SEM] + [_HBM] * (2 * n) + [pl.BlockSpec(memory_space=pltpu.VMEM)],
                  out_shape=[sems, sems] + [hbm_like(a) for a in srcs] + [hbm_like(a) for a in lands]
                  + [jax.ShapeDtypeStruct((8, 128), F32)],
                  input_output_aliases={i: 2 + i for i in range(2 * n)},
                  compiler_params=pltpu.CompilerParams(has_side_effects=_DATAFLOW))(
                      *[pltpu.with_memory_space_constraint(a, pltpu.HBM) for a in list(srcs) + list(lands)])
    return dict(name=name, mode=mode, n=n, send=outs[0], recv=outs[1], srcs=outs[2:2 + n], lands=outs[2 + n:2 + 2 * n],
                token=outs[-1][0, 0])


def _exchange_wait(ex, after):
    n, mode = ex["n"], ex["mode"]

    def body(*refs):
        ins, lnd = refs[:n], refs[n:2 * n]
        send, recv = refs[2 * n], refs[2 * n + 1]
        x, y, c, me = _place()
        for j in range(n):
            for rel in range(1, N_DEV):
                src, dst, landed = _exchange_refs(ins, lnd, mode, me, rel, j)
                pltpu.make_async_remote_copy(src_ref=src, dst_ref=dst, send_sem=send.at[j * (N_DEV - 1) + rel - 1],
                                             recv_sem=recv.at[j * (N_DEV - 1) + rel - 1],
                                             device_id=_peer(x, y, c, rel), device_id_type=MESH).wait_send()
                pltpu.make_async_remote_copy(src_ref=src, dst_ref=landed, send_sem=send.at[j * (N_DEV - 1) + rel - 1],
                                             recv_sem=recv.at[j * (N_DEV - 1) + rel - 1],
                                             device_id=_peer(x, y, c, rel), device_id_type=MESH).wait_recv()

    hbm_like = lambda a: pltpu.HBM(a.shape, a.dtype)
    arrays = list(ex["srcs"]) + list(ex["lands"])
    outs = _pcall(body, name=ex["name"] + "_wait",
                  in_specs=[_HBM] * (2 * n) + [_SEM, _SEM, pl.BlockSpec(memory_space=pl.ANY)],
                  out_specs=[_HBM] * (2 * n), out_shape=[hbm_like(a) for a in arrays],
                  input_output_aliases={i: i for i in range(2 * n)},
                  compiler_params=pltpu.CompilerParams(has_side_effects=_DATAFLOW))(
                      *arrays, ex["send"], ex["recv"], after)
    return outs[:n], outs[n:]


def _scatter_start(name, grads):
    lands = [lax.empty((N_DEV - 1,) + g.shape[1:], g.dtype) for g in grads]
    return _exchange_start(name, grads, lands, "scatter")


ROW_MIX, ROW_MLP, ROW_CB, ROW_LG, ROW_LB, ROW_QN, ROW_KN, ROW_LOSS = 0, 2, 4, 5, 6, 7, 8, 9
ROW_META, ROW_CW, ROW_GN, SMALL_ROWS = 16, 32, 64, 72


def _sum_small(slots):
    def body(s_ref, o_ref):
        tot = s_ref[0]
        for s in range(1, N_DEV):
            tot = tot + s_ref[s]
        o_ref[...] = tot
        for row in (ROW_QN, ROW_KN):
            v = tot[row:row + 1, :]
            f = v[:, 0:128]
            for k in range(1, 8):
                f = f + v[:, 128 * k:128 * (k + 1)]
            o_ref[row:row + 1, 0:64] = f[:, 0:64] + f[:, 64:128]

    return _pcall(body, name="sum_small", out_shape=jax.ShapeDtypeStruct(slots.shape[1:], F32))(slots)


def _adamw_small(w, g, m, v):
    def body(w_ref, g_ref, m_ref, v_ref, d_out, m_out, v_out):
        d, mn, vn = _adamw_math(w_ref[...], g_ref[...], m_ref[...], v_ref[...])
        d_out[...] = d
        m_out[...] = mn
        v_out[...] = vn

    osh = jax.ShapeDtypeStruct(w.shape, F32)
    return _pcall(body, name="adamw_small", out_shape=[osh, osh, osh])(w, g, m, v)


def _local_step(h0, target, p, weight, emit):
    t = h0.shape[0]
    tables = _ret_tables(t)
    bd, later_tab, earlier_tab, bias_tab = _seg_tables(_sb_qb(t))
    row = lambda a, i: a[i:i + 1]

    hn_a = _rms_fwd("rms_mix0", h0, row(p["norm_mix_g"], 0))
    w_in = weight("w_in", hn_a)
    proj = _mm_cols("proj_in", hn_a, w_in, ())
    gn_flat = p["gn_g"].reshape(1, 1024)
    o_ret, states, cat = _ret_fwd(proj, gn_flat, tables)
    cat, hdn, ycv = _conv_fwd(cat, proj, p["conv_w"], p["conv_b"], p["ln_g"], p["ln_b"])
    w_out = weight("w_out", cat)
    h1, hn_b = _mm_rows_norm("mix_out", cat, w_out, h0, row(p["norm_mlp_g"], 0))
    w1_0, w2_0 = weight("w1_0", hn_b), weight("w2_0", hn_b)
    a0, s0 = _mm_cols("mlp0_up", hn_b, w1_0, (), epi="relu2")
    h2, hn_c = _mm_rows_norm("mlp0_down", s0, w2_0, h1, row(p["norm_mix_g"], 1))

    w_qkv = weight("w_qkv", hn_c)
    qkv = _mm_cols("qkv", hn_c, w_qkv, ())
    qg = jnp.tile(p["qn_g"], (1, 16))
    kg = jnp.tile(p["kn_g"], (1, 16))
    qh, kt, vt = _qk_norm_fwd(qkv, qg, kg, bd)
    o_sb, w_sb = _sb_fwd(qh, kt, vt, later_tab, bias_tab)
    w_o = weight("w_o", o_sb)
    h3, hn_d = _mm_rows_norm("attn_out", o_sb, w_o, h2, row(p["norm_mlp_g"], 1))
    w1_1, w2_1 = weight("w1_1", hn_d), weight("w2_1", hn_d)
    a1, s1 = _mm_cols("mlp1_up", hn_d, w1_1, (), epi="relu2")
    dh, loss = _mm_rows_loss("mlp1_down", s1, w2_1, h3, target)

    def mlp_bwd(tag, layer, w1, w2, dh, h_in, hn, a, s):
        da = _mm_rows_t(f"{tag}_dact", dh, w2, (), out_dtype=BF16, epi="drelu2", extra=a)
        dw2 = _wgrad_rows(f"{tag}_dw2", s, dh, 512)
        dw1 = _wgrad_cols(f"{tag}_dw1", hn, da, 512)
        tok = emit(tag, [dw1, dw2])
        return _mm_cols_t_rms(f"{tag}_dhn", da, w1, h_in, row(p["norm_mlp_g"], layer) + tok, dh)

    dh, dg_mlp1 = mlp_bwd("mlp1", 1, w1_1, w2_1, dh, h3, hn_d, a1, s1)

    do_sb = _mm_rows_t("attn_dout", dh, w_o, ())
    dw_o = _wgrad_rows("attn_dwo", o_sb, dh, 128)
    dq, dk, dv = _sb_bwd(qh, kt, vt, w_sb, do_sb, earlier_tab, bias_tab)
    dqkv, dqg, dkg = _qk_norm_bwd(qkv, dq, dk, dv, qg, kg, bd)
    dw_qkv = _wgrad_cols("qkv_dw", hn_c, dqkv, 384)
    tok = emit("attn", [dw_qkv, dw_o])
    dh, dg_mix1 = _mm_cols_t_rms("qkv_dhn", dqkv, w_qkv, h2, row(p["norm_mix_g"], 1) + tok, dh)

    dh, dg_mlp0 = mlp_bwd("mlp0", 0, w1_0, w2_0, dh, h1, hn_b, a0, s0)

    dw_out = _wgrad_rows("mix_dwout", cat, dh, 256)
    tok = emit("mix0_out", [dw_out])
    do_ret, dproj, dgn, dy, dlg, dlb, dcb = _mix_bwd_head(dh, w_out, o_ret, proj, gn_flat + tok, ycv,
                                                          p["ln_g"], p["ln_b"])
    dproj = _ret_bwd(dproj, proj, states, do_ret, tables)
    dproj, dug, dcw = _conv_bwd_taps(dproj, dy, hdn, proj, p["conv_w"])
    dproj = lax.dynamic_update_slice(dproj, dug, (0, 4096))
    dw_in = _wgrad_cols("proj_dw", hn_a, dproj, 640)
    tok = emit("mix0", [dw_in])
    dh, dg_mix0 = _mm_cols_t_rms("proj_dhn", dproj, w_in, h0, row(p["norm_mix_g"], 0) + tok, dh)

    rid = lax.broadcasted_iota(jnp.int32, (16, 1), 0)
    loss_row = jnp.broadcast_to(loss[0:1, 0:1], (1, D_MODEL))
    vecs = sum(jnp.where(rid == k, v, 0.0)
               for k, v in enumerate((dg_mix0, dg_mix1, dg_mlp0, dg_mlp1, dcb, dlg, dlb, dqg, dkg, loss_row)))
    small = jnp.concatenate([vecs, dh[PAD_FRONT:TOK0], dcw, jnp.where(rid[:8] == 0, dgn, 0.0)], axis=0)
    return dh[TOK0:], small


_SMALL_NAMES = ("meta", "norm_mix_g", "norm_mlp_g", "even_ret_gn_g", "even_conv_w", "even_conv_b",
                "even_conv_ln_g", "even_conv_ln_b", "odd_q_norm_g", "odd_k_norm_g")
_BIG_NAMES = ("even_w_in", "even_w_out", "odd_w_qkv", "odd_w_o", "mlp_w1", "mlp_w2")
_ORDER = ("meta", "norm_mix_g", "norm_mlp_g", "even_w_in", "even_ret_gn_g", "even_conv_w", "even_conv_b",
          "even_conv_ln_g", "even_conv_ln_b", "even_w_out", "odd_w_qkv", "odd_q_norm_g", "odd_k_norm_g",
          "odd_w_o", "mlp_w1", "mlp_w2")


def _pack128(a):
    flat = a.reshape(-1)
    n = flat.shape[0]
    rows = -(-n // 128)
    rows8 = -(-rows // 8) * 8
    return jnp.pad(flat, (0, rows8 * 128 - n)).reshape(rows8, 128)


def kernel(x, meta, norm_mix_g, norm_mlp_g, even_w_in, even_ret_gn_g, even_conv_w, even_conv_b, even_conv_ln_g, even_conv_ln_b, even_w_out, odd_w_qkv, odd_q_norm_g, odd_k_norm_g, odd_w_o, mlp_w1, mlp_w2, loss_target, m_meta, m_norm_mix_g, m_norm_mlp_g, m_even_w_in, m_even_ret_gn_g, m_even_conv_w, m_even_conv_b, m_even_conv_ln_g, m_even_conv_ln_b, m_even_w_out, m_odd_w_qkv, m_odd_q_norm_g, m_odd_k_norm_g, m_odd_w_o, m_mlp_w1, m_mlp_w2, v_meta, v_norm_mix_g, v_norm_mlp_g, v_even_w_in, v_even_ret_gn_g, v_even_conv_w, v_even_conv_b, v_even_conv_ln_g, v_even_conv_ln_b, v_even_w_out, v_odd_w_qkv, v_odd_q_norm_g, v_odd_k_norm_g, v_odd_w_o, v_mlp_w1, v_mlp_w2):
    w = dict(meta=meta, norm_mix_g=norm_mix_g, norm_mlp_g=norm_mlp_g, even_w_in=even_w_in,
             even_ret_gn_g=even_ret_gn_g, even_conv_w=even_conv_w, even_conv_b=even_conv_b,
             even_conv_ln_g=even_conv_ln_g, even_conv_ln_b=even_conv_ln_b, even_w_out=even_w_out,
             odd_w_qkv=odd_w_qkv, odd_q_norm_g=odd_q_norm_g, odd_k_norm_g=odd_k_norm_g, odd_w_o=odd_w_o,
             mlp_w1=mlp_w1, mlp_w2=mlp_w2)
    mom = dict(meta=m_meta, norm_mix_g=m_norm_mix_g, norm_mlp_g=m_norm_mlp_g, even_w_in=m_even_w_in,
               even_ret_gn_g=m_even_ret_gn_g, even_conv_w=m_even_conv_w, even_conv_b=m_even_conv_b,
               even_conv_ln_g=m_even_conv_ln_g, even_conv_ln_b=m_even_conv_ln_b, even_w_out=m_even_w_out,
               odd_w_qkv=m_odd_w_qkv, odd_q_norm_g=m_odd_q_norm_g, odd_k_norm_g=m_odd_k_norm_g, odd_w_o=m_odd_w_o,
               mlp_w1=m_mlp_w1, mlp_w2=m_mlp_w2)
    var = dict(meta=v_meta, norm_mix_g=v_norm_mix_g, norm_mlp_g=v_norm_mlp_g, even_w_in=v_even_w_in,
               even_ret_gn_g=v_even_ret_gn_g, even_conv_w=v_even_conv_w, even_conv_b=v_even_conv_b,
               even_conv_ln_g=v_even_conv_ln_g, even_conv_ln_b=v_even_conv_ln_b, even_w_out=v_even_w_out,
               odd_w_qkv=v_odd_w_qkv, odd_q_norm_g=v_odd_q_norm_g, odd_k_norm_g=v_odd_k_norm_g, odd_w_o=v_odd_w_o,
               mlp_w1=v_mlp_w1, mlp_w2=v_mlp_w2)
    me = 4 * lax.axis_index("x") + 2 * lax.axis_index("y") + lax.axis_index("c")

    small_in = jnp.concatenate([meta, jnp.pad(even_conv_w[0], ((0, 1), (0, 0))),
                                jnp.pad(even_ret_gn_g[0], ((0, 4), (0, 96)))], axis=0)
    b16 = lambda a: a.astype(BF16)
    later_src = dict(w_out=b16(even_w_out[0]), w1_0=b16(mlp_w1[0]), w2_0=b16(mlp_w2[0]),
                     w_qkv=b16(odd_w_qkv[0]), w_o=b16(odd_w_o[0]), w1_1=b16(mlp_w1[1]), w2_1=b16(mlp_w2[1]))
    landed = _gather_first([b16(even_w_in[0]), small_in], list(later_src.values()))
    g_in, g_small = landed[0], landed[1]
    own_slot = dict(zip(later_src, landed[2:]))
    groups = (("gather_l0", ("w_out", "w1_0", "w2_0")), ("gather_attn", ("w_qkv", "w_o")),
              ("gather_l1", ("w1_1", "w2_1")))
    pending = {}
    gather_tok = jnp.zeros((), F32)
    for gname, names in groups:
        ex = _exchange_start(gname, [later_src[n] for n in names], [own_slot[n] for n in names], "gather")
        gather_tok = gather_tok + ex["token"]
        for n in names:
            pending[n] = (ex, names)
    arrived = dict(w_in=g_in)

    def weight(name, after):
        if name not in arrived:
            ex, names = pending[name]
            arrived.update(zip(names, _exchange_wait(ex, after)[1]))
        return arrived[name]

    cols = lambda a: jnp.transpose(a, (1, 0, 2)).reshape(a.shape[1], -1)
    p = dict(norm_mix_g=norm_mix_g + gather_tok, norm_mlp_g=norm_mlp_g, conv_b=even_conv_b, ln_g=even_conv_ln_g,
             ln_b=even_conv_ln_b, qn_g=odd_q_norm_g, kn_g=odd_k_norm_g,
             gn_g=cols(g_small[:, 48:52, :32]),
             conv_w=jnp.pad(cols(g_small[:, 16:47]), ((0, 1), (0, 0))))
    meta_full = cols(g_small[:, 0:16])

    scatters = {}

    def emit(tag, grads):
        scatters[tag] = _scatter_start("scatter_" + tag, grads)
        return scatters[tag]["token"]

    h0 = jnp.concatenate([jnp.zeros((PAD_FRONT, D_MODEL), F32), meta_full, x[0]], axis=0)
    target = jnp.concatenate([jnp.zeros((TOK0, D_MODEL), F32), loss_target[0]], axis=0)
    grad_x, small_part = _local_step(h0, target, p, weight, emit)

    out = {}
    got = {}

    def update(names, terms, after):
        for tag in {t for name in names for t, _ in terms[name]} - set(got):
            got[tag] = _exchange_wait(scatters[tag], after)
        for name in names:
            owns, recvs = zip(*[(got[t][0][j], got[t][1][j]) for t, j in terms[name]])
            out[name] = _adamw("adamw_" + name, w[name], list(owns), list(recvs), mom[name], var[name], me)

    terms = dict(even_w_in=[("mix0", 0)], even_w_out=[("mix0_out", 0)], odd_w_qkv=[("attn", 0)], odd_w_o=[("attn", 1)],
                 mlp_w1=[("mlp0", 0), ("mlp1", 0)], mlp_w2=[("mlp0", 1), ("mlp1", 1)])
    small_ex = _exchange_start("small", [small_part], [lax.empty((N_DEV,) + small_part.shape, F32)], "gather")
    update(("mlp_w1", "mlp_w2", "odd_w_qkv", "odd_w_o", "even_w_out"), terms, grad_x)
    update(("even_w_in",), terms, out["even_w_out"][1])
    (own_part,), (slots,) = _exchange_wait(small_ex, out["even_w_in"][1])
    tot = _sum_small(lax.dynamic_update_slice(slots, own_part[None], (me, 0, 0)))
    loss = tot[ROW_LOSS, 0]

    shard_cols = lambda a, width: lax.dynamic_slice_in_dim(a, me * width, width, axis=1)
    one = lambda r: tot[r:r + 1]
    small_g = dict(
        norm_mix_g=tot[ROW_MIX:ROW_MIX + 2], norm_mlp_g=tot[ROW_MLP:ROW_MLP + 2],
        even_conv_b=one(ROW_CB), even_conv_ln_g=one(ROW_LG), even_conv_ln_b=one(ROW_LB),
        odd_q_norm_g=one(ROW_QN)[:, :64], odd_k_norm_g=one(ROW_KN)[:, :64],
        meta=shard_cols(tot[ROW_META:ROW_META + N_META], 128),
        even_conv_w=shard_cols(tot[ROW_CW:ROW_CW + CONV_WIDTH], 128)[None],
        even_ret_gn_g=shard_cols(tot[ROW_GN].reshape(4, 256), 32)[None])
    packs = {n: (_pack128(w[n]), _pack128(small_g[n]), _pack128(mom[n]), _pack128(var[n])) for n in _SMALL_NAMES}
    cat4 = [jnp.concatenate([packs[n][i] for n in _SMALL_NAMES], axis=0) for i in range(4)]
    d_s, m_s, v_s = _adamw_small(*cat4)
    r0 = 0
    for n in _SMALL_NAMES:
        rows = packs[n][0].shape[0]
        size = w[n].size
        take = lambda a: a[r0:r0 + rows].reshape(-1)[:size].reshape(w[n].shape)
        out[n] = (small_g[n].reshape(w[n].shape), take(d_s), take(m_s), take(v_s))
        r0 += rows

    res = [loss, grad_x[None]]
    for i in range(4):
        res.extend(out[n][i] for n in _ORDER)
    return tuple(res)
```

```python
import functools

import numpy as np
import jax
import jax.numpy as jnp
from jax import lax
from jax.experimental import pallas as pl
from jax.experimental.pallas import tpu as pltpu

F32 = jnp.float32
BF16 = jnp.bfloat16

D_MODEL = 1024
N_META = 16
CHUNK = 128
PAD_FRONT = 112
TOK0 = PAD_FRONT + N_META
EPS = 1e-6
N_DEV = 8
RET_HEADS = 4
RET_DECAY_OFFSET = 5.0
ROPE_BASE = 10000.0
CONV_WIDTH = 31
HALO = 32
SB_SCALE = 64 ** -0.5
RET_SCALE = 128 ** -0.5
ADAM_LR, ADAM_B1, ADAM_B2, ADAM_EPS, ADAM_WD, ADAM_STEP = 0.001, 0.9, 0.999, 1e-08, 0.01, 10
VMEM_LIMIT = 56 * 1024 * 1024
MESH = pl.DeviceIdType.MESH


def _pcall(body, **kw):
    return pl.pallas_call(body, **kw)


def _params(**kw):
    return pltpu.CompilerParams(vmem_limit_bytes=VMEM_LIMIT, **kw)


def _tile(n, cands):
    for c in cands:
        if n % c == 0:
            return c
    raise ValueError(f"no tile for {n} in {cands}")


def _sigmoid(x):
    return 1.0 / (1.0 + jnp.exp(-x))


_DIMS = {
    "nn": (((1,), (0,)), ((), ())),
    "nt": (((1,), (1,)), ((), ())),
    "tn": (((0,), (0,)), ((), ())),
}


def _matmul(name, a, b, *, grid, a_spec, b_spec, o_spec, out_shape, contract, acc_shape,
            epi="plain", extra=None, extra_spec=None):
    nk = grid[2]
    dims = _DIMS[contract]
    n_in = 3 if extra is not None else 2
    n_out = 2 if epi == "relu2" else 1

    def body(*refs):
        a_ref, b_ref = refs[0], refs[1]
        e_ref = refs[2] if extra is not None else None
        outs = refs[n_in:n_in + n_out]
        acc = refs[-1]
        k = pl.program_id(2)
        part = lax.dot_general(a_ref[...].astype(BF16), b_ref[...].astype(BF16), dims, preferred_element_type=F32)
        if nk > 1:
            @pl.when(k == 0)
            def _():
                acc[...] = jnp.zeros_like(acc)

            acc[...] += part

        @pl.when(k == nk - 1)
        def _():
            r = acc[...] if nk > 1 else part
            if epi == "plain":
                outs[0][...] = r.astype(outs[0].dtype)
            elif epi == "residual":
                outs[0][...] = (r + e_ref[...]).astype(outs[0].dtype)
            elif epi == "relu2":
                outs[0][...] = r
                rr = jnp.maximum(r, 0.0)
                outs[1][...] = (rr * rr).astype(BF16)
            elif epi == "drelu2":
                outs[0][...] = (r * (2.0 * jnp.maximum(e_ref[...], 0.0))).astype(outs[0].dtype)

    in_specs = [a_spec, b_spec] + ([extra_spec] if extra is not None else [])
    args = (a, b) + ((extra,) if extra is not None else ())
    if n_out == 2:
        out_specs = [o_spec, o_spec]
    else:
        out_specs = o_spec
    return _pcall(body, name=name, grid=grid, in_specs=in_specs, out_specs=out_specs,
                  out_shape=out_shape, scratch_shapes=[pltpu.VMEM(acc_shape, F32)],
                  compiler_params=_params(dimension_semantics=("parallel", "parallel", "arbitrary")))(*args)


def _tm_tall(t):
    return _tile(t, (2112, 768, 384, 128))


def _mm_cols(name, a, wb, lead, out_dtype=F32, epi="plain"):
    t, kdim = a.shape
    n = wb.shape[-1]
    tm, tk = _tm_tall(t), _tile(kdim, (1024, 512))
    nl = len(lead)
    b_spec = pl.BlockSpec((None,) * (1 + nl) + (tk, n), lambda i, j, k: (j,) + lead + (k, 0))
    o_spec = pl.BlockSpec((tm, n), lambda i, j, k: (i, j))
    if epi == "relu2":
        out_shape = [jax.ShapeDtypeStruct((t, N_DEV * n), F32), jax.ShapeDtypeStruct((t, N_DEV * n), BF16)]
    else:
        out_shape = jax.ShapeDtypeStruct((t, N_DEV * n), out_dtype)
    return _matmul(name, a, wb, grid=(t // tm, N_DEV, kdim // tk),
                   a_spec=pl.BlockSpec((tm, tk), lambda i, j, k: (i, k)), b_spec=b_spec, o_spec=o_spec,
                   out_shape=out_shape, contract="nn", acc_shape=(tm, n), epi=epi)


def _mm_cols_t_rms(name, a, wb, h, g, dres):
    t = a.shape[0]
    nb, kdim, n = wb.shape
    tm = _tile(t, (704, 384, 128))

    def body(a_ref, b_ref, h_ref, g_ref, r_ref, o_ref, dg_ref):
        @pl.when(pl.program_id(0) == 0)
        def _():
            dg_ref[...] = jnp.zeros_like(dg_ref)

        d = _dot(a_ref[:, 0:n].astype(BF16), b_ref[0], "nt")
        for j in range(1, nb):
            d = d + _dot(a_ref[:, j * n:(j + 1) * n].astype(BF16), b_ref[j], "nt")
        x = h_ref[...]
        rs = lax.rsqrt(jnp.mean(x * x, axis=-1, keepdims=True) + EPS)
        u = d * g_ref[...]
        m = jnp.mean(u * x, axis=-1, keepdims=True)
        o_ref[...] = r_ref[...] + rs * u - x * (rs * rs * rs * m)
        dg_ref[...] += jnp.sum(d * x * rs, axis=0, keepdims=True)

    row = pl.BlockSpec((tm, kdim), lambda i: (i, 0))
    vec = pl.BlockSpec((1, kdim), lambda i: (0, 0))
    return _pcall(body, name=name, grid=(t // tm,),
                  in_specs=[pl.BlockSpec((tm, nb * n), lambda i: (i, 0)),
                            pl.BlockSpec((nb, kdim, n), lambda i: (0, 0, 0)), row, vec, row],
                  out_specs=[row, vec],
                  out_shape=[jax.ShapeDtypeStruct((t, kdim), F32), jax.ShapeDtypeStruct((1, kdim), F32)],
                  compiler_params=_params(dimension_semantics=("arbitrary",)))(a, wb, h, g, dres)


def _mm_rows_t(name, a, wb, lead, out_dtype=F32, epi="plain", extra=None):
    t, n = a.shape
    r = wb.shape[-2]
    tm, tk = _tm_tall(t), _tile(n, (1024,))
    nl = len(lead)
    b_spec = pl.BlockSpec((None,) * (1 + nl) + (r, tk), lambda i, j, k: (j,) + lead + (0, k))
    o_spec = pl.BlockSpec((tm, r), lambda i, j, k: (i, j))
    return _matmul(name, a, wb, grid=(t // tm, N_DEV, n // tk),
                   a_spec=pl.BlockSpec((tm, tk), lambda i, j, k: (i, k)), b_spec=b_spec, o_spec=o_spec,
                   out_shape=jax.ShapeDtypeStruct((t, N_DEV * r), out_dtype), contract="nt",
                   acc_shape=(tm, r), epi=epi, extra=extra, extra_spec=o_spec if extra is not None else None)


def _mm_rows_loss(name, a, wb, residual, target):
    t = a.shape[0]
    nb, r, n = wb.shape
    tm = _tile(t, (704, 384, 128))

    def body(a_ref, b_ref, r_ref, t_ref, d_ref, l_ref):
        i = pl.program_id(0)

        @pl.when(i == 0)
        def _():
            l_ref[...] = jnp.zeros_like(l_ref)

        y = r_ref[...] + _dot(a_ref[...].astype(BF16), b_ref[...].reshape(nb * r, n))
        diff = jnp.where(_row_ids(i, tm) >= TOK0, y - t_ref[...], 0.0)
        d_ref[...] = diff * (1.0 / D_MODEL)
        l_ref[...] += jnp.sum(diff * diff) * (0.5 / D_MODEL)

    row = pl.BlockSpec((tm, n), lambda i: (i, 0))
    return _pcall(body, name=name, grid=(t // tm,),
                  in_specs=[pl.BlockSpec((tm, nb * r), lambda i: (i, 0)),
                            pl.BlockSpec((nb, r, n), lambda i: (0, 0, 0)), row, row],
                  out_specs=[row, pl.BlockSpec((8, 128), lambda i: (0, 0))],
                  out_shape=[jax.ShapeDtypeStruct((t, n), F32), jax.ShapeDtypeStruct((8, 128), F32)],
                  compiler_params=_params(dimension_semantics=("arbitrary",)))(a, wb, residual, target)


def _mm_rows_norm(name, a, wb, residual, g):
    t = a.shape[0]
    nb, r, n = wb.shape
    tm = _tile(t, (704, 384, 128))

    def body(a_ref, b_ref, r_ref, g_ref, h_ref, hn_ref):
        h = r_ref[...] + _dot(a_ref[...].astype(BF16), b_ref[...].reshape(nb * r, n))
        h_ref[...] = h
        hn_ref[...] = (h * lax.rsqrt(jnp.mean(h * h, axis=-1, keepdims=True) + EPS) * g_ref[...]).astype(BF16)

    row = pl.BlockSpec((tm, n), lambda i: (i, 0))
    return _pcall(body, name=name, grid=(t // tm,),
                  in_specs=[pl.BlockSpec((tm, nb * r), lambda i: (i, 0)), pl.BlockSpec((nb, r, n), lambda i: (0, 0, 0)),
                            row, pl.BlockSpec((1, n), lambda i: (0, 0))],
                  out_specs=[row, row],
                  out_shape=[jax.ShapeDtypeStruct((t, n), F32), jax.ShapeDtypeStruct((t, n), BF16)],
                  compiler_params=_params(dimension_semantics=("parallel",)))(a, wb, residual, g)


def _wgrad_cols(name, x, dy, n):
    t, kdim = x.shape
    tk = _tm_tall(t)
    return _matmul(name, x, dy, grid=(1, N_DEV, t // tk),
                   a_spec=pl.BlockSpec((tk, kdim), lambda i, j, k: (k, 0)),
                   b_spec=pl.BlockSpec((tk, n), lambda i, j, k: (k, j)),
                   o_spec=pl.BlockSpec((None, kdim, n), lambda i, j, k: (j, 0, 0)),
                   out_shape=jax.ShapeDtypeStruct((N_DEV, kdim, n), BF16), contract="tn", acc_shape=(kdim, n))


def _wgrad_rows(name, x, dy, r):
    t = x.shape[0]
    n = dy.shape[1]
    tk, tn = _tm_tall(t), _tile(n, (512,))
    tm = min(N_DEV * r, 1024)
    out = _matmul(name, x, dy, grid=(N_DEV * r // tm, n // tn, t // tk),
                  a_spec=pl.BlockSpec((tk, tm), lambda i, j, k: (k, i)),
                  b_spec=pl.BlockSpec((tk, tn), lambda i, j, k: (k, j)),
                  o_spec=pl.BlockSpec((tm, tn), lambda i, j, k: (i, j)),
                  out_shape=jax.ShapeDtypeStruct((N_DEV * r, n), BF16), contract="tn", acc_shape=(tm, tn))
    return out.reshape(N_DEV, r, n)


def _rows(t):
    return _tile(t, (384, 128))


def _rms_fwd(name, h, g):
    t = h.shape[0]
    tr = _rows(t)

    def body(h_ref, g_ref, o_ref):
        x = h_ref[...]
        r = lax.rsqrt(jnp.mean(x * x, axis=-1, keepdims=True) + EPS)
        o_ref[...] = (x * r * g_ref[...]).astype(BF16)

    row = pl.BlockSpec((tr, D_MODEL), lambda i: (i, 0))
    vec = pl.BlockSpec((1, D_MODEL), lambda i: (0, 0))
    return _pcall(body, name=name, grid=(t // tr,), in_specs=[row, vec], out_specs=row,
                  out_shape=jax.ShapeDtypeStruct((t, D_MODEL), BF16))(h, g)


def _ret_tables(t):
    hh = np.arange(RET_HEADS, dtype=np.float64)
    log_g = np.log1p(-np.exp2(-RET_DECAY_OFFSET - hh))
    idx = np.arange(CHUNK, dtype=np.float64)
    diff = idx[:, None] - idx[None, :]
    dmat = np.where(diff[None] >= 0, np.exp(np.maximum(diff, 0.0)[None] * log_g[:, None, None]), 0.0)
    qdec = np.exp((idx + 1.0)[None, :, None] * log_g[:, None, None]) * np.ones((1, 1, CHUNK))
    kdec = np.exp((CHUNK - 1 - idx)[None, :, None] * log_g[:, None, None]) * np.ones((1, 1, CHUNK))
    half = CHUNK // 2
    inv_freq = (ROPE_BASE ** (-np.arange(half, dtype=np.float32) / half)).astype(np.float32)
    ang = (np.arange(t, dtype=np.float32)[:, None] * inv_freq[None, :]).astype(np.float32).astype(np.float64)
    cos2 = np.concatenate([np.cos(ang), np.cos(ang)], axis=1)
    sin2 = np.concatenate([-np.sin(ang), np.sin(ang)], axis=1)
    return tuple(jnp.asarray(v, F32) for v in (dmat, qdec, kdec, cos2, sin2))


def _rot(x, c, s):
    return x * c + pltpu.roll(x, CHUNK // 2, 1) * s


def _unrot(dx, c, s):
    return dx * c + pltpu.roll(dx * s, CHUNK // 2, 1)


def _dot(a, b, contract="nn"):
    return lax.dot_general(a, b, _DIMS[contract], preferred_element_type=F32)


def _ret_fwd(proj, gn_g, tables):
    t = proj.shape[0]
    nch = t // CHUNK
    dmat, qdec, kdec, cos2, sin2 = tables

    def body(qk_ref, v_ref, g_ref, w_ref, c_ref, s_ref, dm_ref, qd_ref, kd_ref, o_ref, st_ref, cat_ref, state):
        @pl.when(pl.program_id(0) == 0)
        def _():
            state[...] = jnp.zeros_like(state)

        c, s = c_ref[...], s_ref[...]
        for h in range(RET_HEADS):
            q = _rot(qk_ref[:, 128 * h:128 * (h + 1)], c, s)
            k = _rot(qk_ref[:, 512 + 128 * h:512 + 128 * (h + 1)], c, s) * RET_SCALE
            vb = v_ref[:, 256 * h:256 * (h + 1)].astype(BF16)
            st = state[h]
            st_ref[h] = st
            sc = _dot(q.astype(BF16), k.astype(BF16), "nt") * dm_ref[h]
            o = _dot(sc.astype(BF16), vb)
            o += _dot((q * qd_ref[h]).astype(BF16), st.astype(BF16))
            sl = slice(256 * h, 256 * (h + 1))
            o_ref[:, sl] = o
            kv = _dot((k * kd_ref[h]).astype(BF16), vb, "tn")
            state[h] = qd_ref[h, CHUNK - 1:CHUNK, 0:1] * st + kv
            mu = jnp.mean(o, axis=-1, keepdims=True)
            oc = o - mu
            rstd = lax.rsqrt(jnp.mean(oc * oc, axis=-1, keepdims=True) + EPS)
            g = g_ref[:, sl]
            cat_ref[:, sl] = (g * _sigmoid(g) * (oc * rstd * w_ref[:, sl])).astype(BF16)

    tab = pl.BlockSpec((RET_HEADS, CHUNK, CHUNK), lambda n: (0, 0, 0))
    pos = pl.BlockSpec((CHUNK, CHUNK), lambda n: (n, 0))
    row = pl.BlockSpec((CHUNK, 1024), lambda n: (n, 0))
    return _pcall(
        body, name="ret_fwd", grid=(nch,),
        in_specs=[row, pl.BlockSpec((CHUNK, 1024), lambda n: (n, 1)), pl.BlockSpec((CHUNK, 1024), lambda n: (n, 2)),
                  pl.BlockSpec((1, 1024), lambda n: (0, 0)), pos, pos, tab, tab, tab],
        out_specs=[row, pl.BlockSpec((RET_HEADS, None, 128, 256), lambda n: (0, n, 0, 0)), row],
        out_shape=[jax.ShapeDtypeStruct((t, 1024), F32), jax.ShapeDtypeStruct((RET_HEADS, nch, 128, 256), F32),
                   jax.ShapeDtypeStruct((t, 2048), BF16)],
        scratch_shapes=[pltpu.VMEM((RET_HEADS, 128, 256), F32)],
        compiler_params=_params(dimension_semantics=("arbitrary",)))(
            proj, proj, proj, gn_g, cos2, sin2, dmat, qdec, kdec)


def _ret_bwd(dproj, proj, states, do, tables):
    t = proj.shape[0]
    nch = t // CHUNK
    dmat, qdec, kdec, cos2, sin2 = tables

    def body(dp_in, qk_ref, v_ref, do_ref, st_ref, c_ref, s_ref, dm_ref, qd_ref, kd_ref, dp_ref, rst):
        del dp_in
        @pl.when(pl.program_id(0) == 0)
        def _():
            rst[...] = jnp.zeros_like(rst)

        c, s = c_ref[...], s_ref[...]
        for h in range(RET_HEADS):
            q = _rot(qk_ref[:, 128 * h:128 * (h + 1)], c, s)
            k = _rot(qk_ref[:, 512 + 128 * h:512 + 128 * (h + 1)], c, s) * RET_SCALE
            qb, kb = q.astype(BF16), k.astype(BF16)
            vb = v_ref[:, 256 * h:256 * (h + 1)].astype(BF16)
            dob = do_ref[:, 256 * h:256 * (h + 1)].astype(BF16)
            pb = st_ref[h].astype(BF16)
            r = rst[h]
            rb = r.astype(BF16)
            dm, qd, kd = dm_ref[h], qd_ref[h], kd_ref[h]
            sb = (_dot(qb, kb, "nt") * dm).astype(BF16)
            dsb = (_dot(dob, vb, "nt") * dm).astype(BF16)
            dq = _dot(dsb, kb) + _dot(dob, pb, "nt") * qd
            dk = _dot(dsb, qb, "tn") + _dot(vb, rb, "nt") * kd
            dv = _dot(sb, dob, "tn") + _dot((k * kd).astype(BF16), rb)
            rst[h] = _dot((q * qd).astype(BF16), dob, "tn") + qd[CHUNK - 1:CHUNK, 0:1] * r
            dp_ref[:, 128 * h:128 * (h + 1)] = _unrot(dq, c, s).astype(BF16)
            dp_ref[:, 512 + 128 * h:512 + 128 * (h + 1)] = (_unrot(dk, c, s) * RET_SCALE).astype(BF16)
            dp_ref[:, 1024 + 256 * h:1024 + 256 * (h + 1)] = dv.astype(BF16)

    rev = lambda n: nch - 1 - n
    tab = pl.BlockSpec((RET_HEADS, CHUNK, CHUNK), lambda n: (0, 0, 0))
    pos = pl.BlockSpec((CHUNK, CHUNK), lambda n: (rev(n), 0))
    row = pl.BlockSpec((CHUNK, 1024), lambda n: (rev(n), 0))
    return _pcall(
        body, name="ret_bwd", grid=(nch,),
        in_specs=[pl.BlockSpec(memory_space=pl.ANY), row, pl.BlockSpec((CHUNK, 1024), lambda n: (rev(n), 1)), row,
                  pl.BlockSpec((RET_HEADS, None, 128, 256), lambda n: (0, rev(n), 0, 0)),
                  pos, pos, tab, tab, tab],
        out_specs=pl.BlockSpec((CHUNK, 2048), lambda n: (rev(n), 0)),
        out_shape=jax.ShapeDtypeStruct((t, 5120), BF16),
        scratch_shapes=[pltpu.VMEM((RET_HEADS, 128, 256), F32)], input_output_aliases={0: 0},
        compiler_params=_params(dimension_semantics=("arbitrary",)))(
            dproj, proj, proj, do, states, cos2, sin2, dmat, qdec, kdec)


def _row_ids(i, tr):
    return i * tr + lax.broadcasted_iota(jnp.int32, (tr, 1), 0)


SH_ROWS = HALO - 8
CONV_VPU_TAPS = 21


def _shifted_copies(xs, sh, tr):
    for b in range(1, 8):
        sh[b - 1] = xs[pl.ds(b, tr + SH_ROWS), :]


def _shifted(xs, sh, off, tr, lanes=slice(None)):
    a, b = divmod(off, 8)
    return xs[pl.ds(8 * a, tr), lanes] if b == 0 else sh[b - 1, pl.ds(8 * a, tr), lanes]


def _taps_mxu(xs, sh, w_ref, offs, tr, first=0):
    sub = lax.broadcasted_iota(jnp.int32, (256, 128), 0)
    eye = (sub & 127) == lax.broadcasted_iota(jnp.int32, (256, 128), 1)
    outs = []
    for c in range(8):
        lanes = slice(128 * c, 128 * (c + 1))
        acc = None
        for w in range(first, len(offs), 2):
            wb = min(w + 1, len(offs) - 1)
            w_hi = w_ref[w:w + 1, lanes]
            w_lo = w_ref[wb:wb + 1, lanes] if wb > w else jnp.zeros((1, 128), F32)
            dmat = jnp.where(eye, jnp.where(sub < 128, w_hi, w_lo), 0.0).astype(BF16)
            lhs = jnp.concatenate([_shifted(xs, sh, offs[w], tr, lanes).astype(BF16),
                                   _shifted(xs, sh, offs[wb], tr, lanes).astype(BF16)], axis=1)
            d = _dot(lhs, dmat)
            acc = d if acc is None else acc + d
        outs.append(acc)
    return jnp.concatenate(outs, axis=1)


def _conv_fwd(cat, proj, conv_w, conv_b, ln_g, ln_b):
    t = proj.shape[0]
    tr = _rows(t)
    hb = tr // HALO

    def body(cat_in, ua_ref, ug_ref, pa_ref, pg_ref, w_ref, b_ref, lg_ref, lb_ref, c_ref, hd_ref, y_ref, xs, sh):
        del cat_in
        i = pl.program_id(0)
        hdn = ua_ref[...] * _sigmoid(ug_ref[...])
        hd_ref[...] = hdn
        prev = pa_ref[...] * _sigmoid(pg_ref[...])
        xs[0:HALO, :] = jnp.where(i > 0, prev, 0.0)
        xs[HALO:HALO + tr, :] = hdn
        _shifted_copies(xs, sh, tr)
        offs = [HALO - (CONV_WIDTH - 1) + w for w in range(CONV_WIDTH)]
        acc = _taps_mxu(xs, sh, w_ref, offs, tr, first=CONV_VPU_TAPS) + b_ref[...]
        for w in range(CONV_VPU_TAPS):
            acc += w_ref[w:w + 1, :] * _shifted(xs, sh, offs[w], tr)
        y_ref[...] = acc
        mu = jnp.mean(acc, axis=-1, keepdims=True)
        yc = acc - mu
        rstd = lax.rsqrt(jnp.mean(yc * yc, axis=-1, keepdims=True) + EPS)
        yn = yc * rstd * lg_ref[...] + lb_ref[...]
        c = yn * _sigmoid(yn)
        c_ref[...] = jnp.where(_row_ids(i, tr) >= PAD_FRONT, c, 0.0).astype(BF16)

    row = pl.BlockSpec((tr, 1024), lambda i: (i, 0))
    vec = pl.BlockSpec((1, 1024), lambda i: (0, 0))
    halo = lambda col: pl.BlockSpec((HALO, 1024), lambda i: (jnp.maximum(i * hb - 1, 0), col))
    return _pcall(body, name="conv_fwd", grid=(t // tr,),
                  in_specs=[pl.BlockSpec(memory_space=pl.ANY),
                            pl.BlockSpec((tr, 1024), lambda i: (i, 3)), pl.BlockSpec((tr, 1024), lambda i: (i, 4)),
                            halo(3), halo(4), pl.BlockSpec((32, 1024), lambda i: (0, 0)), vec, vec, vec],
                  out_specs=[pl.BlockSpec((tr, 1024), lambda i: (i, 1)), row, row],
                  out_shape=[jax.ShapeDtypeStruct((t, 2048), BF16), jax.ShapeDtypeStruct((t, 1024), F32),
                             jax.ShapeDtypeStruct((t, 1024), F32)],
                  scratch_shapes=[pltpu.VMEM((tr + HALO, 1024), F32), pltpu.VMEM((7, tr + SH_ROWS, 1024), F32)],
                  input_output_aliases={0: 0}, compiler_params=_params())(
                      cat, proj, proj, proj, proj, conv_w, conv_b, ln_g, ln_b)


def _mix_bwd_head(dh, w_out, o, proj, gn_g, y, ln_g, ln_b):
    t = dh.shape[0]
    tr = _rows(t)
    nb, r, n = w_out.shape

    def body(dh_ref, b_ref, o_ref, g_ref, w_ref, y_ref, lg_ref, lb_ref,
             do_ref, dp_ref, dw_ref, dy_ref, dlg_ref, dlb_ref, dcb_ref):
        i = pl.program_id(0)

        @pl.when(i == 0)
        def _():
            for ref in (dw_ref, dlg_ref, dlb_ref, dcb_ref):
                ref[...] = jnp.zeros_like(ref)

        dcat = _dot(dh_ref[...].astype(BF16), b_ref[...].reshape(nb * r, n), "nt")
        for h in range(RET_HEADS):
            sl = slice(256 * h, 256 * (h + 1))
            x = o_ref[:, sl]
            mu = jnp.mean(x, axis=-1, keepdims=True)
            xc = x - mu
            rstd = lax.rsqrt(jnp.mean(xc * xc, axis=-1, keepdims=True) + EPS)
            xh = xc * rstd
            w = w_ref[:, sl]
            g = g_ref[:, sl]
            sg = _sigmoid(g)
            d = dcat[:, sl]
            don = d * (g * sg)
            dp_ref[:, sl] = (d * (xh * w) * (sg * (1.0 + g * (1.0 - sg)))).astype(BF16)
            dw_ref[:, sl] += jnp.sum(don * xh, axis=0, keepdims=True)
            dxh = don * w
            m1 = jnp.mean(dxh, axis=-1, keepdims=True)
            m2 = jnp.mean(dxh * xh, axis=-1, keepdims=True)
            do_ref[:, sl] = rstd * (dxh - m1 - xh * m2)
        yv = y_ref[...]
        mu = jnp.mean(yv, axis=-1, keepdims=True)
        yc = yv - mu
        rstd = lax.rsqrt(jnp.mean(yc * yc, axis=-1, keepdims=True) + EPS)
        xh = yc * rstd
        lg = lg_ref[...]
        yn = xh * lg + lb_ref[...]
        sg = _sigmoid(yn)
        dyn = jnp.where(_row_ids(i, tr) >= PAD_FRONT, dcat[:, 1024:] * (sg * (1.0 + yn * (1.0 - sg))), 0.0)
        dlg_ref[...] += jnp.sum(dyn * xh, axis=0, keepdims=True)
        dlb_ref[...] += jnp.sum(dyn, axis=0, keepdims=True)
        dxh = dyn * lg
        m1 = jnp.mean(dxh, axis=-1, keepdims=True)
        m2 = jnp.mean(dxh * xh, axis=-1, keepdims=True)
        dy = rstd * (dxh - m1 - xh * m2)
        dy_ref[...] = dy
        dcb_ref[...] += jnp.sum(dy, axis=0, keepdims=True)

    row = pl.BlockSpec((tr, 1024), lambda i: (i, 0))
    vec = pl.BlockSpec((1, 1024), lambda i: (0, 0))
    gate = pl.BlockSpec((tr, 1024), lambda i: (i, 2))
    vsh = jax.ShapeDtypeStruct((1, 1024), F32)
    fsh = jax.ShapeDtypeStruct((t, 1024), F32)
    return _pcall(body, name="mix_bwd_head", grid=(t // tr,),
                  in_specs=[row, pl.BlockSpec((nb, r, n), lambda i: (0, 0, 0)), row, gate, vec, row, vec, vec],
                  out_specs=[row, gate, vec, row, vec, vec, vec],
                  out_shape=[fsh, jax.ShapeDtypeStruct((t, 5120), BF16), vsh, fsh, vsh, vsh, vsh],
                  compiler_params=_params(dimension_semantics=("arbitrary",)))(
                      dh, w_out, o, proj, gn_g, y, ln_g, ln_b)


def _conv_bwd_taps(dproj, dy, hdn, proj, conv_w):
    t = dy.shape[0]
    tr = _rows(t)
    hb = tr // HALO
    nt = t // tr

    def body(dp_in, dy_ref, nx_ref, hd_ref, ph_ref, ua_ref, ug_ref, w_ref, da_ref, dg_ref, dw_ref, xs, sh):
        del dp_in
        i = pl.program_id(0)

        @pl.when(i == 0)
        def _():
            dw_ref[...] = jnp.zeros_like(dw_ref)

        dy = dy_ref[...]
        xs[0:tr, :] = dy
        xs[tr:tr + HALO, :] = jnp.where(i < nt - 1, nx_ref[...], 0.0)
        _shifted_copies(xs, sh, tr)
        dh = _taps_mxu(xs, sh, w_ref, [CONV_WIDTH - 1 - w for w in range(CONV_WIDTH)], tr)
        xs[0:HALO, :] = jnp.where(i > 0, ph_ref[...], 0.0)
        xs[HALO:HALO + tr, :] = hd_ref[...]
        _shifted_copies(xs, sh, tr)
        for w in range(CONV_WIDTH):
            dw_ref[w:w + 1, :] += jnp.sum(dy * _shifted(xs, sh, HALO - (CONV_WIDTH - 1) + w, tr), axis=0, keepdims=True)
        dh = jnp.where(_row_ids(i, tr) >= PAD_FRONT, dh, 0.0)
        sg = _sigmoid(ug_ref[...])
        da_ref[...] = (dh * sg).astype(BF16)
        dg_ref[...] = (dh * ua_ref[...] * sg * (1.0 - sg)).astype(BF16)

    row = pl.BlockSpec((tr, 1024), lambda i: (i, 0))
    return _pcall(body, name="conv_bwd_taps", grid=(nt,),
                  in_specs=[pl.BlockSpec(memory_space=pl.ANY),
                            row, pl.BlockSpec((HALO, 1024), lambda i: (jnp.minimum((i + 1) * hb, nt * hb - 1), 0)),
                            row, pl.BlockSpec((HALO, 1024), lambda i: (jnp.maximum(i * hb - 1, 0), 0)),
                            pl.BlockSpec((tr, 1024), lambda i: (i, 3)), pl.BlockSpec((tr, 1024), lambda i: (i, 4)),
                            pl.BlockSpec((32, 1024), lambda i: (0, 0))],
                  out_specs=[pl.BlockSpec((tr, 1024), lambda i: (i, 3)), row, pl.BlockSpec((32, 1024), lambda i: (0, 0))],
                  out_shape=[jax.ShapeDtypeStruct((t, 5120), BF16), jax.ShapeDtypeStruct((t, 1024), BF16),
                             jax.ShapeDtypeStruct((32, 1024), F32)],
                  scratch_shapes=[pltpu.VMEM((tr + HALO, 1024), F32), pltpu.VMEM((7, tr + SH_ROWS, 1024), F32)],
                  input_output_aliases={0: 0}, compiler_params=_params())(
                      dproj, dy, dy, hdn, hdn, proj, proj, conv_w)


NEG_BIG = -1e30


def _seg_tables(qb):
    j = np.arange(128)
    bd = (j[:, None] // 64 == j[None, :] // 64).astype(np.float32)
    ones = np.ones((128, 128), np.float32)
    later = np.concatenate([(j[:, None] >= j[None, :]).astype(np.float32), ones], axis=1)
    earlier = np.concatenate([(j[:, None] < j[None, :]).astype(np.float32), ones], axis=1)
    per = qb // CHUNK
    row = np.arange(qb)[:, None]
    pad = np.broadcast_to(j[None, :] < PAD_FRONT, (qb, 128))
    diag = [(g * CHUNK + j[None, :]) >= row for g in range(per)]
    masks = diag + [np.zeros((qb, 128), bool), pad, diag[0] | pad]
    bias = np.stack([np.where(m, NEG_BIG, 0.0) for m in masks]).astype(np.float32)
    dup = lambda m: np.concatenate([m, m], axis=0)
    return (jnp.asarray(bd, BF16), jnp.asarray(dup(later), BF16), jnp.asarray(dup(earlier), BF16),
            jnp.asarray(bias, F32))


def _split_dot(x, m):
    hi = x.astype(BF16)
    lo = (x - hi.astype(F32)).astype(BF16)
    return _dot(hi, m) + _dot(lo, m)


def _qk_norm_fwd(qkv, qg, kg, bd):
    t = qkv.shape[0]
    tr = _rows(t)
    nb = tr // CHUNK

    def body(q_ref, k_ref, v_ref, qg_ref, kg_ref, bd_ref, qo, kt, vt):
        bdm = bd_ref[...]
        sub = lax.broadcasted_iota(jnp.int32, (128, 1), 0)

        def pair_layout(x, t_ref, hp, b):
            xt = x.T
            t_ref[hp, b] = jnp.concatenate([jnp.where(sub < 64, xt, 0.0), jnp.where(sub >= 64, xt, 0.0)],
                                           axis=1).astype(BF16)

        for hp in range(8):
            sl = slice(128 * hp, 128 * (hp + 1))
            x = q_ref[:, sl]
            r = lax.rsqrt(_split_dot(x * x, bdm) * (1.0 / 64) + EPS)
            qo[:, sl] = (x * r * (qg_ref[:, sl] * SB_SCALE)).astype(BF16)
            x = k_ref[:, sl]
            r = lax.rsqrt(_split_dot(x * x, bdm) * (1.0 / 64) + EPS)
            kn = x * r * kg_ref[:, sl]
            v = v_ref[:, sl]
            for b in range(nb):
                rows = slice(CHUNK * b, CHUNK * (b + 1))
                pair_layout(kn[rows], kt, hp, b)
                pair_layout(v[rows], vt, hp, b)

    col = lambda c: pl.BlockSpec((tr, 1024), lambda i: (i, c))
    vec = pl.BlockSpec((1, 1024), lambda i: (0, 0))
    wide = pl.BlockSpec((8, nb, 128, 256), lambda i: (0, i, 0, 0))
    wsh = jax.ShapeDtypeStruct((8, t // CHUNK, 128, 256), BF16)
    return _pcall(body, name="qk_norm_fwd", grid=(t // tr,),
                  in_specs=[col(0), col(1), col(2), vec, vec, pl.BlockSpec((128, 128), lambda i: (0, 0))],
                  out_specs=[col(0), wide, wide],
                  out_shape=[jax.ShapeDtypeStruct((t, 1024), BF16), wsh, wsh])(qkv, qkv, qkv, qg, kg, bd)


def _qk_norm_bwd(qkv, dq, dk, dv, qg, kg, bd):
    t = qkv.shape[0]
    tr = _rows(t)

    def body(q_ref, k_ref, dq_ref, dk_ref, dv_ref, qg_ref, kg_ref, bd_ref, o_ref, dqg_ref, dkg_ref):
        @pl.when(pl.program_id(0) == 0)
        def _():
            dqg_ref[...] = jnp.zeros_like(dqg_ref)
            dkg_ref[...] = jnp.zeros_like(dkg_ref)

        bdm = bd_ref[...]
        for part, (src, d_ref, g_ref, dg_ref) in enumerate(((q_ref, dq_ref, qg_ref, dqg_ref),
                                                           (k_ref, dk_ref, kg_ref, dkg_ref))):
            for cix in range(8):
                sl = slice(128 * cix, 128 * (cix + 1))
                x = src[:, sl]
                d = d_ref[:, sl]
                r = lax.rsqrt(_split_dot(x * x, bdm) * (1.0 / 64) + EPS)
                u = d * g_ref[:, sl]
                m = _split_dot(u * x, bdm) * (1.0 / 64)
                o_ref[:, 1024 * part + 128 * cix:1024 * part + 128 * (cix + 1)] = (r * u - x * (r * r * r * m)).astype(BF16)
                dg_ref[:, sl] += jnp.sum(d * x * r, axis=0, keepdims=True)
        o_ref[:, 2048:3072] = dv_ref[...].astype(BF16)

    col = lambda c: pl.BlockSpec((tr, 1024), lambda i: (i, c))
    vec = pl.BlockSpec((1, 1024), lambda i: (0, 0))
    vsh = jax.ShapeDtypeStruct((1, 1024), F32)
    return _pcall(body, name="qk_norm_bwd", grid=(t // tr,),
                  in_specs=[col(0), col(1), col(0), col(0), col(0), vec, vec, pl.BlockSpec((128, 128), lambda i: (0, 0))],
                  out_specs=[pl.BlockSpec((tr, 3072), lambda i: (i, 0)), vec, vec],
                  out_shape=[jax.ShapeDtypeStruct((t, 3072), BF16), vsh, vsh])(qkv, qkv, dq, dk, dv, qg, kg, bd)


def _split2(x):
    hi = x.astype(BF16)
    lo = (x - hi.astype(F32)).astype(BF16)
    return jnp.concatenate([hi, lo], axis=1)


def _sb_sums(z, later_tab):
    sp = jnp.maximum(z, 0.0) + jnp.log(1.0 + jnp.exp(-jnp.abs(z)))
    return _dot(_split2(sp), later_tab)


def _sb_bias_index(i, kb, per):
    g = kb - i * per
    return jnp.where(kb == 0, jnp.where(i == 0, per + 2, per + 1), jnp.where(g >= 0, g, per))


def _sb_qb(t):
    return _tile(t, (384, 128))


def _sb_fwd(qh, kt, vt, later_tab, bias_tab):
    t = qh.shape[0]
    qb = _sb_qb(t)
    per = qb // CHUNK
    nkb_all = t // CHUNK

    nq = t // qb

    def body(q_ref, kt_ref, vt_ref, tab_ref, bias_ref, o_ref, ws_ref, acc, carry, zbuf, wbuf, wsem):
        h, i = pl.program_id(0), pl.program_id(1)
        n = h * nq + i
        p = n & 1
        q = q_ref[...]
        acc[...] = jnp.zeros_like(acc)
        carry[...] = jnp.zeros_like(carry)
        nkb = (i + 1) * per
        save = lambda kb: pltpu.make_async_copy(wbuf.at[p, kb], ws_ref.at[h, i, kb], wsem.at[p, kb])

        def drain(step, par):
            hs, is_ = step // nq, step % nq

            def one(kb, _):
                pltpu.make_async_copy(wbuf.at[par, kb], ws_ref.at[hs, is_, kb], wsem.at[par, kb]).wait()
                return 0

            lax.fori_loop(0, (is_ + 1) * per, one, 0)

        @pl.when(n >= 2)
        def _():
            drain(n - 2, p)

        for u in range(per):
            zbuf[u] = _dot(q, kt_ref[nkb - 1 - u])

        def trip(s, diagonal):
            top = nkb - 1 - per * s
            if not diagonal:
                for u in range(per):
                    save(top + per - u).start(priority=1)
            z2s = [zbuf[u] for u in range(per)]
            for u in range(per):
                zbuf[u] = _dot(q, kt_ref[jnp.maximum(top - per - u, 0)])
            first = [CHUNK * (per - 1 - u) if diagonal else 0 for u in range(per)]
            cins = [carry[0], carry[1]]
            zs, cus = [], []
            for u in range(per):
                zs.append([z2s[u][first[u]:, 128 * hh:128 * (hh + 1)] for hh in range(2)])
                if diagonal or u == per - 1:
                    bias = bias_ref[_sb_bias_index(i, top - u, per)][first[u]:]
                    zs[u] = [z + bias for z in zs[u]]
                cus.append([_sb_sums(z, tab_ref[...]) for z in zs[u]])
            part = None
            for u in range(per):
                kb, lo = top - u, first[u]
                for hh in range(2):
                    sl = slice(128 * hh, 128 * (hh + 1))
                    cu = cus[u][hh]
                    wbuf[p, kb, lo:, sl] = jnp.exp(zs[u][hh] - cu[:, :128] - cins[hh][lo:]).astype(BF16)
                    if lo:
                        wbuf[p, kb, :lo, sl] = jnp.zeros((lo, 128), BF16)
                        cins[hh] = jnp.concatenate([cins[hh][:lo], cins[hh][lo:] + cu[:, 128:]], axis=0)
                    else:
                        cins[hh] = cins[hh] + cu[:, 128:]
                d = _dot(wbuf[p, kb], vt_ref[kb], "nt")
                part = d if part is None else part + d
            carry[0], carry[1] = cins[0], cins[1]
            acc[...] += part

        trip(0, True)

        def step(s, _):
            trip(s, False)
            return 0

        lax.fori_loop(1, nkb // per, step, 0)
        for u in range(per):
            save(per - 1 - u).start(priority=1)
        o_ref[...] = acc[...]

        @pl.when(n == 8 * nq - 1)
        def _():
            drain(n - 1, 1 - p)
            drain(n, p)

    blk = pl.BlockSpec((qb, 128), lambda h, i: (i, h))
    wide = pl.BlockSpec((None, nkb_all, 128, 256), lambda h, i: (h, 0, 0, 0))
    return _pcall(body, name="sb_fwd", grid=(8, t // qb),
                  in_specs=[blk, wide, wide, pl.BlockSpec((256, 256), lambda h, i: (0, 0)),
                            pl.BlockSpec((per + 3, qb, 128), lambda h, i: (0, 0, 0))],
                  out_specs=[blk, pl.BlockSpec(memory_space=pl.ANY)],
                  out_shape=[jax.ShapeDtypeStruct((t, 1024), F32),
                             jax.ShapeDtypeStruct((8, t // qb, nkb_all, qb, 256), BF16)],
                  scratch_shapes=[pltpu.VMEM((qb, 128), F32), pltpu.VMEM((2, qb, 128), F32),
                                  pltpu.VMEM((per, qb, 256), F32), pltpu.VMEM((2, nkb_all, qb, 256), BF16),
                                  pltpu.SemaphoreType.DMA((2, nkb_all))],
                  compiler_params=_params(dimension_semantics=("arbitrary", "arbitrary")))(
                      qh, kt, vt, later_tab, bias_tab)


def _sb_bwd(qh, kt, vt, wsave, do, earlier_tab, bias_tab):
    t = qh.shape[0]
    qb = _sb_qb(t)
    per = qb // CHUNK
    nkb_all = t // CHUNK

    zero_slot = nkb_all
    nq = t // qb

    def body(q_ref, kt_ref, vt_ref, ws_ref, do_ref, etab_ref, bias_ref,
             dq_ref, dk_ref, dv_ref, acc, gcarry, zbuf, dwbuf, wbuf, wsem, dzbuf):
        h, i = pl.program_id(0), pl.program_id(1)
        n = h * nq + i
        p = n & 1

        @pl.when(i == 0)
        def _():
            dk_ref[...] = jnp.zeros_like(dk_ref)
            dv_ref[...] = jnp.zeros_like(dv_ref)

        nkb = (i + 1) * per
        fetch = lambda kb: pltpu.make_async_copy(ws_ref.at[h, i, kb], wbuf.at[p, kb], wsem.at[p, kb])

        def prefetch(step, par):
            hs, is_ = step // nq, step % nq

            def one(kb, _):
                pltpu.make_async_copy(ws_ref.at[hs, is_, kb], wbuf.at[par, kb], wsem.at[par, kb]).start(priority=1)
                return 0

            lax.fori_loop(0, (is_ + 1) * per, one, 0)

        @pl.when(n == 0)
        def _():
            prefetch(n, p)

        @pl.when(n + 1 < 8 * nq)
        def _():
            prefetch(n + 1, 1 - p)

        q = q_ref[...]
        dob = do_ref[...].astype(BF16)
        acc[...] = jnp.zeros_like(acc)
        gcarry[...] = jnp.zeros_like(gcarry)
        zbuf[...] = _dot(q, kt_ref[0])
        dwbuf[...] = _dot(dob, vt_ref[0])
        dzbuf[...] = jnp.zeros_like(dzbuf)
        wbuf[p, zero_slot] = jnp.zeros((qb, 256), BF16)

        q_t = q.astype(F32).T.astype(BF16)
        do_t = do_ref[...].T.astype(BF16)
        sub = lax.broadcasted_iota(jnp.int32, (128, 1), 0)

        def gradients(slot, kb):
            dz2 = dzbuf[...]
            acc[...] += _dot(dz2, kt_ref[kb], "nt")
            dk2 = _dot(q_t, dz2)
            dv2 = _dot(do_t, wbuf[p, slot])
            dk_ref[kb] += jnp.where(sub < 64, dk2[:, :128], dk2[:, 128:])
            dv_ref[kb] += jnp.where(sub < 64, dv2[:, :128], dv2[:, 128:])

        def trip(kb, lo):
            fetch(kb).wait()
            bias = bias_ref[_sb_bias_index(i, kb, per)][lo:]
            z2 = zbuf[...]
            dw2 = dwbuf[...]
            nxt = jnp.minimum(kb + 1, nkb - 1)
            zbuf[...] = _dot(q, kt_ref[nxt])
            dwbuf[...] = _dot(dob, vt_ref[nxt])
            gradients(jnp.where(kb == 0, zero_slot, kb - 1), jnp.maximum(kb - 1, 0))
            w2 = wbuf[p, kb]
            for hh in range(2):
                sl = slice(128 * hh, 128 * (hh + 1))
                z = z2[lo:, sl] + bias
                e = jnp.exp(-jnp.abs(z))
                r = 1.0 / (1.0 + e)
                sig = jnp.where(z >= 0, r, e * r)
                gw = w2[lo:, sl].astype(F32) * dw2[lo:, sl]
                cu2 = _dot(_split2(gw), etab_ref[...])
                gin = gcarry[hh, lo:, :]
                gcarry[hh, lo:, :] = gin + cu2[:, 128:]
                dzbuf[lo:, sl] = (gw - sig * (gw + cu2[:, :128] + gin)).astype(BF16)
                if lo:
                    dzbuf[:lo, sl] = jnp.zeros((lo, 128), BF16)

        def step(kb, _):
            trip(kb, 0)
            return 0

        lax.fori_loop(0, nkb - per, step, 0)
        for g in range(per):
            trip(nkb - per + g, CHUNK * g)
        gradients(nkb - 1, nkb - 1)
        dq_ref[...] = acc[...] * SB_SCALE

        @pl.when(i == nq - 1)
        def _():
            def untranspose(kb, _):
                dk_ref[kb] = dk_ref[kb].T
                dv_ref[kb] = dv_ref[kb].T
                return 0

            lax.fori_loop(0, nkb_all, untranspose, 0)

    blk = pl.BlockSpec((qb, 128), lambda h, i: (i, h))
    wide = pl.BlockSpec((None, nkb_all, 128, 256), lambda h, i: (h, 0, 0, 0))
    tab = pl.BlockSpec((256, 256), lambda h, i: (0, 0))
    kv_out = pl.BlockSpec((nkb_all, 128, 128), lambda h, i: (0, 0, h))
    ksh = jax.ShapeDtypeStruct((nkb_all, 128, 1024), F32)
    dq, dk, dv = _pcall(
        body, name="sb_bwd", grid=(8, t // qb),
        in_specs=[blk, wide, wide, pl.BlockSpec(memory_space=pl.ANY), blk, tab,
                  pl.BlockSpec((per + 3, qb, 128), lambda h, i: (0, 0, 0))],
        out_specs=[blk, kv_out, kv_out], out_shape=[jax.ShapeDtypeStruct((t, 1024), F32), ksh, ksh],
        scratch_shapes=[pltpu.VMEM((qb, 128), F32), pltpu.VMEM((2, qb, 128), F32),
                        pltpu.VMEM((qb, 256), F32), pltpu.VMEM((qb, 256), F32),
                        pltpu.VMEM((2, nkb_all + 1, qb, 256), BF16), pltpu.SemaphoreType.DMA((2, nkb_all)),
                        pltpu.VMEM((qb, 256), BF16)],
        compiler_params=_params(dimension_semantics=("arbitrary", "arbitrary")))(
            qh, kt, vt, wsave, do, earlier_tab, bias_tab)
    return dq, dk.reshape(t, 1024), dv.reshape(t, 1024)


def _adamw_math(w, g, m, v):
    m = ADAM_B1 * m + (1.0 - ADAM_B1) * g
    v = ADAM_B2 * v + (1.0 - ADAM_B2) * (g * g)
    m_hat = m / (1.0 - ADAM_B1 ** ADAM_STEP)
    v_hat = v / (1.0 - ADAM_B2 ** ADAM_STEP)
    delta = -ADAM_LR * (m_hat / (jnp.sqrt(v_hat) + ADAM_EPS) + ADAM_WD * w)
    return delta, m, v


def _adamw(name, w, owns, recvs, m, v, me):
    shape = w.shape
    c = shape[-1]
    nl = len(owns)
    w3, m3, v3 = (a.reshape(nl, -1, c) for a in (w, m, v))
    r = w3.shape[1]
    tr = _tile(r, (256, 128))
    owns = [o.reshape(N_DEV, r, c) for o in owns]
    recvs = [p.reshape(N_DEV - 1, r, c) for p in recvs]

    def body(me_ref, w_ref, *rest):
        own_refs, recv_refs = rest[:nl], rest[nl:2 * nl]
        m_ref, v_ref = rest[2 * nl:2 * nl + 2]
        g_out, d_out, m_out, v_out = rest[2 * nl + 2:]
        layer = pl.program_id(0)

        def grad(k):
            g = own_refs[k][...].astype(F32)
            for s in range(N_DEV - 1):
                g = g + recv_refs[k][s].astype(F32)
            return g

        g = grad(0)
        for k in range(1, nl):
            g = jnp.where(layer == k, grad(k), g)
        d, mn, vn = _adamw_math(w_ref[...], g, m_ref[...], v_ref[...])
        g_out[...] = g
        d_out[...] = d
        m_out[...] = mn
        v_out[...] = vn

    row = pl.BlockSpec((None, tr, c), lambda l, i, me_ref: (l, i, 0))
    own = lambda k: pl.BlockSpec((None, tr, c), lambda l, i, me_ref: (me_ref[0], jnp.where(l == k, i, 0), 0))
    rcv = lambda k: pl.BlockSpec((N_DEV - 1, tr, c), lambda l, i, me_ref: (0, jnp.where(l == k, i, 0), 0))
    osh = jax.ShapeDtypeStruct((nl, r, c), F32)
    grid_spec = pltpu.PrefetchScalarGridSpec(
        num_scalar_prefetch=1, grid=(nl, r // tr),
        in_specs=[row] + [own(k) for k in range(nl)] + [rcv(k) for k in range(nl)] + [row, row],
        out_specs=[row, row, row, row])
    outs = _pcall(body, name=name, grid_spec=grid_spec, out_shape=[osh, osh, osh, osh])(
        me.reshape(1), w3, *owns, *recvs, m3, v3)
    return tuple(o.reshape(shape) for o in outs)


def _place():
    x, y, c = lax.axis_index("x"), lax.axis_index("y"), lax.axis_index("c")
    return x, y, c, 4 * x + 2 * y + c


def _peer(x, y, c, rel):
    return (x ^ ((rel >> 2) & 1), y ^ ((rel >> 1) & 1), c ^ (rel & 1))


def _gather_first(now, later):
    n, k = len(now), len(later)

    def body(*refs):
        ins, outs = refs[:n + k], refs[n + k:2 * (n + k)]
        send, recv, lsem = refs[2 * (n + k):]
        x, y, c, me = _place()
        locals_ = []
        for w in range(n + k):
            local = pltpu.make_async_copy(ins[w], outs[w].at[me], lsem.at[w])
            local.start(priority=1)
            locals_.append(local)
        def copy(w, src, slot, rel, to_rel):
            return pltpu.make_async_remote_copy(src_ref=src, dst_ref=outs[w].at[slot], send_sem=send.at[w, rel - 1],
                                                recv_sem=recv.at[w, rel - 1], device_id=_peer(x, y, c, to_rel),
                                                device_id_type=MESH)

        for w in range(n):
            for rel in (1, 2, 4, 6):
                copy(w, ins[w], me, rel, rel).start()
        for w in range(n):
            for rel in (2, 4, 6):
                copy(w, ins[w], me ^ rel, rel, rel).wait_recv()
                copy(w, outs[w].at[me ^ rel], me ^ rel, rel | 1, 1).start()
        for w in range(n):
            for rel in (1, 3, 5, 7):
                copy(w, ins[w], me ^ rel, rel, 1).wait_recv()
            for rel in range(1, N_DEV):
                copy(w, ins[w], me, rel, rel).wait_send()
        for local in locals_:
            local.wait()

    hbm = pl.BlockSpec(memory_space=pl.ANY)
    vmem = pl.BlockSpec(memory_space=pltpu.VMEM)
    arrays = list(now) + list(later)
    return _pcall(body, name="gather_first", in_specs=[vmem] * (n + k), out_specs=[hbm] * (n + k),
                  out_shape=[jax.ShapeDtypeStruct((N_DEV,) + a.shape, a.dtype) for a in arrays],
                  scratch_shapes=[pltpu.SemaphoreType.DMA((n, N_DEV - 1)), pltpu.SemaphoreType.DMA((n, N_DEV - 1)),
                                  pltpu.SemaphoreType.DMA((n + k,))],
                  compiler_params=_params(has_side_effects=True))(*arrays)


_HBM = pl.BlockSpec(memory_space=pltpu.HBM)
_SEM = pl.BlockSpec(memory_space=pltpu.SEMAPHORE)
_DATAFLOW = pltpu.SideEffectType.DATAFLOW_SIDE_EFFECTING


def _exchange_refs(srcs, lands, mode, me, rel, j):
    if mode == "gather":
        return srcs[j], lands[j].at[me], lands[j].at[me ^ rel]
    return srcs[j].at[me ^ rel], lands[j].at[rel - 1], lands[j].at[rel - 1]


def _exchange_start(name, srcs, lands, mode):
    n = len(srcs)

    def body(*refs):
        ins, lnd = refs[:n], refs[n:2 * n]
        send, recv = refs[2 * n], refs[2 * n + 1]
        token = refs[-1]
        x, y, c, me = _place()
        for j in range(n):
            for rel in range(1, N_DEV):
                src, dst, _ = _exchange_refs(ins, lnd, mode, me, rel, j)
                pltpu.make_async_remote_copy(src_ref=src, dst_ref=dst, send_sem=send.at[j * (N_DEV - 1) + rel - 1],
                                             recv_sem=recv.at[j * (N_DEV - 1) + rel - 1],
                                             device_id=_peer(x, y, c, rel), device_id_type=MESH).start()
        token[...] = jnp.zeros_like(token)

    sems = pltpu.SemaphoreType.DMA((n * (N_DEV - 1),))
    hbm_like = lambda a: pltpu.HBM(a.shape, a.dtype)
    outs = _pcall(body, name=name + "_start",
                  in_specs=[_HBM] * (2 * n), out_specs=[_SEM, _SEM] + [_HBM] * (2 * n) + [pl.BlockSpec(memory_space=pltpu.VMEM)],
                  out_shape=[sems, sems] + [hbm_like(a) for a in srcs] + [hbm_like(a) for a in lands]
                  + [jax.ShapeDtypeStruct((8, 128), F32)],
                  input_output_aliases={i: 2 + i for i in range(2 * n)},
                  compiler_params=pltpu.CompilerParams(has_side_effects=_DATAFLOW))(
                      *[pltpu.with_memory_space_constraint(a, pltpu.HBM) for a in list(srcs) + list(lands)])
    return dict(name=name, mode=mode, n=n, send=outs[0], recv=outs[1], srcs=outs[2:2 + n], lands=outs[2 + n:2 + 2 * n],
                token=outs[-1][0, 0])


def _exchange_wait(ex, after):
    n, mode = ex["n"], ex["mode"]

    def body(*refs):
        ins, lnd = refs[:n], refs[n:2 * n]
        send, recv = refs[2 * n], refs[2 * n + 1]
        x, y, c, me = _place()
        for j in range(n):
            for rel in range(1, N_DEV):
                src, dst, landed = _exchange_refs(ins, lnd, mode, me, rel, j)
                pltpu.make_async_remote_copy(src_ref=src, dst_ref=dst, send_sem=send.at[j * (N_DEV - 1) + rel - 1],
                                             recv_sem=recv.at[j * (N_DEV - 1) + rel - 1],
                                             device_id=_peer(x, y, c, rel), device_id_type=MESH).wait_send()
                pltpu.make_async_remote_copy(src_ref=src, dst_ref=landed, send_sem=send.at[j * (N_DEV - 1) + rel - 1],
                                             recv_sem=recv.at[j * (N_DEV - 1) + rel - 1],
                                             device_id=_peer(x, y, c, rel), device_id_type=MESH).wait_recv()

    hbm_like = lambda a: pltpu.HBM(a.shape, a.dtype)
    arrays = list(ex["srcs"]) + list(ex["lands"])
    outs = _pcall(body, name=ex["name"] + "_wait",
                  in_specs=[_HBM] * (2 * n) + [_SEM, _SEM, pl.BlockSpec(memory_space=pl.ANY)],
                  out_specs=[_HBM] * (2 * n), out_shape=[hbm_like(a) for a in arrays],
                  input_output_aliases={i: i for i in range(2 * n)},
                  compiler_params=pltpu.CompilerParams(has_side_effects=_DATAFLOW))(
                      *arrays, ex["send"], ex["recv"], after)
    return outs[:n], outs[n:]


def _scatter_start(name, grads):
    lands = [lax.empty((N_DEV - 1,) + g.shape[1:], g.dtype) for g in grads]
    return _exchange_start(name, grads, lands, "scatter")


ROW_MIX, ROW_MLP, ROW_CB, ROW_LG, ROW_LB, ROW_QN, ROW_KN, ROW_LOSS = 0, 2, 4, 5, 6, 7, 8, 9
ROW_META, ROW_CW, ROW_GN, SMALL_ROWS = 16, 32, 64, 72


def _sum_small(slots):
    def body(s_ref, o_ref):
        tot = s_ref[0]
        for s in range(1, N_DEV):
            tot = tot + s_ref[s]
        o_ref[...] = tot
        for row in (ROW_QN, ROW_KN):
            v = tot[row:row + 1, :]
            f = v[:, 0:128]
            for k in range(1, 8):
                f = f + v[:, 128 * k:128 * (k + 1)]
            o_ref[row:row + 1, 0:64] = f[:, 0:64] + f[:, 64:128]

    return _pcall(body, name="sum_small", out_shape=jax.ShapeDtypeStruct(slots.shape[1:], F32))(slots)


def _adamw_small(w, g, m, v):
    def body(w_ref, g_ref, m_ref, v_ref, d_out, m_out, v_out):
        d, mn, vn = _adamw_math(w_ref[...], g_ref[...], m_ref[...], v_ref[...])
        d_out[...] = d
        m_out[...] = mn
        v_out[...] = vn

    osh = jax.ShapeDtypeStruct(w.shape, F32)
    return _pcall(body, name="adamw_small", out_shape=[osh, osh, osh])(w, g, m, v)


def _local_step(h0, target, p, weight, emit):
    t = h0.shape[0]
    tables = _ret_tables(t)
    bd, later_tab, earlier_tab, bias_tab = _seg_tables(_sb_qb(t))
    row = lambda a, i: a[i:i + 1]

    hn_a = _rms_fwd("rms_mix0", h0, row(p["norm_mix_g"], 0))
    w_in = weight("w_in", hn_a)
    proj = _mm_cols("proj_in", hn_a, w_in, ())
    gn_flat = p["gn_g"].reshape(1, 1024)
    o_ret, states, cat = _ret_fwd(proj, gn_flat, tables)
    cat, hdn, ycv = _conv_fwd(cat, proj, p["conv_w"], p["conv_b"], p["ln_g"], p["ln_b"])
    w_out = weight("w_out", cat)
    h1, hn_b = _mm_rows_norm("mix_out", cat, w_out, h0, row(p["norm_mlp_g"], 0))
    w1_0, w2_0 = weight("w1_0", hn_b), weight("w2_0", hn_b)
    a0, s0 = _mm_cols("mlp0_up", hn_b, w1_0, (), epi="relu2")
    h2, hn_c = _mm_rows_norm("mlp0_down", s0, w2_0, h1, row(p["norm_mix_g"], 1))

    w_qkv = weight("w_qkv", hn_c)
    qkv = _mm_cols("qkv", hn_c, w_qkv, ())
    qg = jnp.tile(p["qn_g"], (1, 16))
    kg = jnp.tile(p["kn_g"], (1, 16))
    qh, kt, vt = _qk_norm_fwd(qkv, qg, kg, bd)
    o_sb, w_sb = _sb_fwd(qh, kt, vt, later_tab, bias_tab)
    w_o = weight("w_o", o_sb)
    h3, hn_d = _mm_rows_norm("attn_out", o_sb, w_o, h2, row(p["norm_mlp_g"], 1))
    w1_1, w2_1 = weight("w1_1", hn_d), weight("w2_1", hn_d)
    a1, s1 = _mm_cols("mlp1_up", hn_d, w1_1, (), epi="relu2")
    dh, loss = _mm_rows_loss("mlp1_down", s1, w2_1, h3, target)

    def mlp_bwd(tag, layer, w1, w2, dh, h_in, hn, a, s):
        da = _mm_rows_t(f"{tag}_dact", dh, w2, (), out_dtype=BF16, epi="drelu2", extra=a)
        dw2 = _wgrad_rows(f"{tag}_dw2", s, dh, 512)
        dw1 = _wgrad_cols(f"{tag}_dw1", hn, da, 512)
        tok = emit(tag, [dw1, dw2])
        return _mm_cols_t_rms(f"{tag}_dhn", da, w1, h_in, row(p["norm_mlp_g"], layer) + tok, dh)

    dh, dg_mlp1 = mlp_bwd("mlp1", 1, w1_1, w2_1, dh, h3, hn_d, a1, s1)

    do_sb = _mm_rows_t("attn_dout", dh, w_o, ())
    dw_o = _wgrad_rows("attn_dwo", o_sb, dh, 128)
    dq, dk, dv = _sb_bwd(qh, kt, vt, w_sb, do_sb, earlier_tab, bias_tab)
    dqkv, dqg, dkg = _qk_norm_bwd(qkv, dq, dk, dv, qg, kg, bd)
    dw_qkv = _wgrad_cols("qkv_dw", hn_c, dqkv, 384)
    tok = emit("attn", [dw_qkv, dw_o])
    dh, dg_mix1 = _mm_cols_t_rms("qkv_dhn", dqkv, w_qkv, h2, row(p["norm_mix_g"], 1) + tok, dh)

    dh, dg_mlp0 = mlp_bwd("mlp0", 0, w1_0, w2_0, dh, h1, hn_b, a0, s0)

    dw_out = _wgrad_rows("mix_dwout", cat, dh, 256)
    tok = emit("mix0_out", [dw_out])
    do_ret, dproj, dgn, dy, dlg, dlb, dcb = _mix_bwd_head(dh, w_out, o_ret, proj, gn_flat + tok, ycv,
                                                          p["ln_g"], p["ln_b"])
    dproj = _ret_bwd(dproj, proj, states, do_ret, tables)
    dproj, dug, dcw = _conv_bwd_taps(dproj, dy, hdn, proj, p["conv_w"])
    dproj = lax.dynamic_update_slice(dproj, dug, (0, 4096))
    dw_in = _wgrad_cols("proj_dw", hn_a, dproj, 640)
    tok = emit("mix0", [dw_in])
    dh, dg_mix0 = _mm_cols_t_rms("proj_dhn", dproj, w_in, h0, row(p["norm_mix_g"], 0) + tok, dh)

    rid = lax.broadcasted_iota(jnp.int32, (16, 1), 0)
    loss_row = jnp.broadcast_to(loss[0:1, 0:1], (1, D_MODEL))
    vecs = sum(jnp.where(rid == k, v, 0.0)
               for k, v in enumerate((dg_mix0, dg_mix1, dg_mlp0, dg_mlp1, dcb, dlg, dlb, dqg, dkg, loss_row)))
    small = jnp.concatenate([vecs, dh[PAD_FRONT:TOK0], dcw, jnp.where(rid[:8] == 0, dgn, 0.0)], axis=0)
    return dh[TOK0:], small


_SMALL_NAMES = ("meta", "norm_mix_g", "norm_mlp_g", "even_ret_gn_g", "even_conv_w", "even_conv_b",
                "even_conv_ln_g", "even_conv_ln_b", "odd_q_norm_g", "odd_k_norm_g")
_BIG_NAMES = ("even_w_in", "even_w_out", "odd_w_qkv", "odd_w_o", "mlp_w1", "mlp_w2")
_ORDER = ("meta", "norm_mix_g", "norm_mlp_g", "even_w_in", "even_ret_gn_g", "even_conv_w", "even_conv_b",
          "even_conv_ln_g", "even_conv_ln_b", "even_w_out", "odd_w_qkv", "odd_q_norm_g", "odd_k_norm_g",
          "odd_w_o", "mlp_w1", "mlp_w2")


def _pack128(a):
    flat = a.reshape(-1)
    n = flat.shape[0]
    rows = -(-n // 128)
    rows8 = -(-rows // 8) * 8
    return jnp.pad(flat, (0, rows8 * 128 - n)).reshape(rows8, 128)


def kernel(x, meta, norm_mix_g, norm_mlp_g, even_w_in, even_ret_gn_g, even_conv_w, even_conv_b, even_conv_ln_g, even_conv_ln_b, even_w_out, odd_w_qkv, odd_q_norm_g, odd_k_norm_g, odd_w_o, mlp_w1, mlp_w2, loss_target, m_meta, m_norm_mix_g, m_norm_mlp_g, m_even_w_in, m_even_ret_gn_g, m_even_conv_w, m_even_conv_b, m_even_conv_ln_g, m_even_conv_ln_b, m_even_w_out, m_odd_w_qkv, m_odd_q_norm_g, m_odd_k_norm_g, m_odd_w_o, m_mlp_w1, m_mlp_w2, v_meta, v_norm_mix_g, v_norm_mlp_g, v_even_w_in, v_even_ret_gn_g, v_even_conv_w, v_even_conv_b, v_even_conv_ln_g, v_even_conv_ln_b, v_even_w_out, v_odd_w_qkv, v_odd_q_norm_g, v_odd_k_norm_g, v_odd_w_o, v_mlp_w1, v_mlp_w2):
    w = dict(meta=meta, norm_mix_g=norm_mix_g, norm_mlp_g=norm_mlp_g, even_w_in=even_w_in,
             even_ret_gn_g=even_ret_gn_g, even_conv_w=even_conv_w, even_conv_b=even_conv_b,
             even_conv_ln_g=even_conv_ln_g, even_conv_ln_b=even_conv_ln_b, even_w_out=even_w_out,
             odd_w_qkv=odd_w_qkv, odd_q_norm_g=odd_q_norm_g, odd_k_norm_g=odd_k_norm_g, odd_w_o=odd_w_o,
             mlp_w1=mlp_w1, mlp_w2=mlp_w2)
    mom = dict(meta=m_meta, norm_mix_g=m_norm_mix_g, norm_mlp_g=m_norm_mlp_g, even_w_in=m_even_w_in,
               even_ret_gn_g=m_even_ret_gn_g, even_conv_w=m_even_conv_w, even_conv_b=m_even_conv_b,
               even_conv_ln_g=m_even_conv_ln_g, even_conv_ln_b=m_even_conv_ln_b, even_w_out=m_even_w_out,
               odd_w_qkv=m_odd_w_qkv, odd_q_norm_g=m_odd_q_norm_g, odd_k_norm_g=m_odd_k_norm_g, odd_w_o=m_odd_w_o,
               mlp_w1=m_mlp_w1, mlp_w2=m_mlp_w2)
    var = dict(meta=v_meta, norm_mix_g=v_norm_mix_g, norm_mlp_g=v_norm_mlp_g, even_w_in=v_even_w_in,
               even_ret_gn_g=v_even_ret_gn_g, even_conv_w=v_even_conv_w, even_conv_b=v_even_conv_b,
               even_conv_ln_g=v_even_conv_ln_g, even_conv_ln_b=v_even_conv_ln_b, even_w_out=v_even_w_out,
               odd_w_qkv=v_odd_w_qkv, odd_q_norm_g=v_odd_q_norm_g, odd_k_norm_g=v_odd_k_norm_g, odd_w_o=v_odd_w_o,
               mlp_w1=v_mlp_w1, mlp_w2=v_mlp_w2)
    me = 4 * lax.axis_index("x") + 2 * lax.axis_index("y") + lax.axis_index("c")

    small_in = jnp.concatenate([meta, jnp.pad(even_conv_w[0], ((0, 1), (0, 0))),
                                jnp.pad(even_ret_gn_g[0], ((0, 4), (0, 96)))], axis=0)
    b16 = lambda a: a.astype(BF16)
    later_src = dict(w_out=b16(even_w_out[0]), w1_0=b16(mlp_w1[0]), w2_0=b16(mlp_w2[0]),
                     w_qkv=b16(odd_w_qkv[0]), w_o=b16(odd_w_o[0]), w1_1=b16(mlp_w1[1]), w2_1=b16(mlp_w2[1]))
    landed = _gather_first([b16(even_w_in[0]), small_in], list(later_src.values()))
    g_in, g_small = landed[0], landed[1]
    own_slot = dict(zip(later_src, landed[2:]))
    groups = (("gather_l0", ("w_out", "w1_0", "w2_0")), ("gather_attn", ("w_qkv", "w_o")),
              ("gather_l1", ("w1_1", "w2_1")))
    pending = {}
    gather_tok = jnp.zeros((), F32)
    for gname, names in groups:
        ex = _exchange_start(gname, [later_src[n] for n in names], [own_slot[n] for n in names], "gather")
        gather_tok = gather_tok + ex["token"]
        for n in names:
            pending[n] = (ex, names)
    arrived = dict(w_in=g_in)

    def weight(name, after):
        if name not in arrived:
            ex, names = pending[name]
            arrived.update(zip(names, _exchange_wait(ex, after)[1]))
        return arrived[name]

    cols = lambda a: jnp.transpose(a, (1, 0, 2)).reshape(a.shape[1], -1)
    p = dict(norm_mix_g=norm_mix_g + gather_tok, norm_mlp_g=norm_mlp_g, conv_b=even_conv_b, ln_g=even_conv_ln_g,
             ln_b=even_conv_ln_b, qn_g=odd_q_norm_g, kn_g=odd_k_norm_g,
             gn_g=cols(g_small[:, 48:52, :32]),
             conv_w=jnp.pad(cols(g_small[:, 16:47]), ((0, 1), (0, 0))))
    meta_full = cols(g_small[:, 0:16])

    scatters = {}

    def emit(tag, grads):
        scatters[tag] = _scatter_start("scatter_" + tag, grads)
        return scatters[tag]["token"]

    h0 = jnp.concatenate([jnp.zeros((PAD_FRONT, D_MODEL), F32), meta_full, x[0]], axis=0)
    target = jnp.concatenate([jnp.zeros((TOK0, D_MODEL), F32), loss_target[0]], axis=0)
    grad_x, small_part = _local_step(h0, target, p, weight, emit)

    out = {}
    got = {}

    def update(names, terms, after):
        for tag in {t for name in names for t, _ in terms[name]} - set(got):
            got[tag] = _exchange_wait(scatters[tag], after)
        for name in names:
            owns, recvs = zip(*[(got[t][0][j], got[t][1][j]) for t, j in terms[name]])
            out[name] = _adamw("adamw_" + name, w[name], list(owns), list(recvs), mom[name], var[name], me)

    terms = dict(even_w_in=[("mix0", 0)], even_w_out=[("mix0_out", 0)], odd_w_qkv=[("attn", 0)], odd_w_o=[("attn", 1)],
                 mlp_w1=[("mlp0", 0), ("mlp1", 0)], mlp_w2=[("mlp0", 1), ("mlp1", 1)])
    small_ex = _exchange_start("small", [small_part], [lax.empty((N_DEV,) + small_part.shape, F32)], "gather")
    update(("mlp_w1", "mlp_w2", "odd_w_qkv", "odd_w_o", "even_w_out"), terms, grad_x)
    update(("even_w_in",), terms, out["even_w_out"][1])
    (own_part,), (slots,) = _exchange_wait(small_ex, out["even_w_in"][1])
    tot = _sum_small(lax.dynamic_update_slice(slots, own_part[None], (me, 0, 0)))
    loss = tot[ROW_LOSS, 0]

    shard_cols = lambda a, width: lax.dynamic_slice_in_dim(a, me * width, width, axis=1)
    one = lambda r: tot[r:r + 1]
    small_g = dict(
        norm_mix_g=tot[ROW_MIX:ROW_MIX + 2], norm_mlp_g=tot[ROW_MLP:ROW_MLP + 2],
        even_conv_b=one(ROW_CB), even_conv_ln_g=one(ROW_LG), even_conv_ln_b=one(ROW_LB),
        odd_q_norm_g=one(ROW_QN)[:, :64], odd_k_norm_g=one(ROW_KN)[:, :64],
        meta=shard_cols(tot[ROW_META:ROW_META + N_META], 128),
        even_conv_w=shard_cols(tot[ROW_CW:ROW_CW + CONV_WIDTH], 128)[None],
        even_ret_gn_g=shard_cols(tot[ROW_GN].reshape(4, 256), 32)[None])
    packs = {n: (_pack128(w[n]), _pack128(small_g[n]), _pack128(mom[n]), _pack128(var[n])) for n in _SMALL_NAMES}
    cat4 = [jnp.concatenate([packs[n][i] for n in _SMALL_NAMES], axis=0) for i in range(4)]
    d_s, m_s, v_s = _adamw_small(*cat4)
    r0 = 0
    for n in _SMALL_NAMES:
        rows = packs[n][0].shape[0]
        size = w[n].size
        take = lambda a: a[r0:r0 + rows].reshape(-1)[:size].reshape(w[n].shape)
        out[n] = (small_g[n].reshape(w[n].shape), take(d_s), take(m_s), take(v_s))
        r0 += rows

    res = [loss, grad_x[None]]
    for i in range(4):
        res.extend(out[n][i] for n in _ORDER)
    return tuple(res)
```
